```python
import jax, jax.numpy as jnp
from jax import lax
import numpy as np

D_MODEL = 1024
BATCH = 16
SEQ = 2048
DEPTH = 2

N_MIXERS = 2
MEM_LEN = 256
EPS = 1e-6
HG_HEADS = 8
HG_DIM = D_MODEL // HG_HEADS
HG_WIDTH = HG_HEADS * HG_DIM
HG_CHUNK = 64
GM_CHUNK = 128
GM_GROUPS = 8
GM_WIDTH = 2 * D_MODEL
GM_GROUP_DIM = GM_WIDTH // GM_GROUPS
XA_HEADS = 4
XA_DIM = D_MODEL // XA_HEADS
XA_WIDTH = XA_HEADS * XA_DIM
D_FF = 2816
N_HGRN = (DEPTH + 1) // 2
N_GMLP = DEPTH // 2
HG_IN = 4 * HG_WIDTH + XA_WIDTH
GM_IN = 2 * GM_WIDTH + XA_WIDTH

kernel_name = "hybrid_hgrn2_gmlp_memory_macaron"


def rmsnorm(x, g):
    xf = x.astype(jnp.float32)
    y = xf * lax.rsqrt(jnp.mean(xf * xf, axis=-1, keepdims=True) + EPS)
    return (y * g.astype(jnp.float32)).astype(x.dtype)


def layernorm(x, g, b):
    xf = x.astype(jnp.float32)
    mu = jnp.mean(xf, axis=-1, keepdims=True)
    xc = xf - mu
    y = xc * lax.rsqrt(jnp.mean(xc * xc, axis=-1, keepdims=True) + EPS)
    return (y * g.astype(jnp.float32) + b.astype(jnp.float32)).astype(x.dtype)


def swiglu_ffn(h, w_in, w_out):
    gate, up = jnp.split(h @ w_in, 2, axis=-1)
    return (jax.nn.silu(gate) * up) @ w_out


def memory_attention(zq, mem_k, mem_v):
    B, T, _ = zq.shape
    q = zq.reshape(B, T, XA_HEADS, XA_DIM)
    s = jnp.einsum('bthd,bmhd->bhtm', q, mem_k).astype(jnp.float32) * (XA_DIM ** -0.5)
    p = jax.nn.softmax(s, axis=-1).astype(mem_v.dtype)
    o = jnp.einsum('bhtm,bmhd->bthd', p, mem_v)
    return o.reshape(B, T, XA_WIDTH)


def hgrn2_recurrence(q, k, v, log_f):
    B, T, H, Dk = q.shape
    Dv = v.shape[-1]
    n = T // HG_CHUNK

    def to_chunks(a):
        return a.reshape(B, n, HG_CHUNK, H, a.shape[-1]).transpose(1, 0, 3, 2, 4)

    qc, kc, vc, lc = to_chunks(q), to_chunks(k), to_chunks(v), to_chunks(log_f)
    causal = jnp.tril(jnp.ones((HG_CHUNK, HG_CHUNK), dtype=bool))

    def step(S, inp):
        qn, kn, vn, ln = inp
        b = jnp.cumsum(ln, axis=2)
        b_last = b[:, :, -1:, :]
        q_dec = qn * jnp.exp(b)
        k_inv = kn * jnp.exp(-b)
        a = jnp.einsum('bhtk,bhsk->bhts', q_dec, k_inv)
        a = jnp.where(causal, a, 0.0)
        o = jnp.einsum('bhts,bhsv->bhtv', a, vn) + jnp.einsum('bhtk,bhkv->bhtv', q_dec, S)
        S_new = jnp.exp(b_last[:, :, 0, :])[..., None] * S + jnp.einsum(
            'bhsk,bhsv->bhkv', kn * jnp.exp(b_last - b), vn)
        return S_new, o

    S0 = jnp.zeros((B, H, Dk, Dv), jnp.float32)
    _, o = lax.scan(step, S0, (qc, kc, vc, lc))
    return o.transpose(1, 0, 3, 2, 4).reshape(B, T, H, Dv)


def hgrn2_mixer(zq, zf, zi, zg, lb, gnorm_g):
    B, T, _ = zq.shape
    shp = (B, T, HG_HEADS, HG_DIM)
    q = jax.nn.silu(zq.astype(jnp.float32)).reshape(shp)
    lbh = lb.astype(jnp.float32).reshape(HG_HEADS, HG_DIM)
    f = lbh + (1.0 - lbh) * jax.nn.sigmoid(zf.astype(jnp.float32).reshape(shp))
    k = 1.0 - f
    v = zi.astype(jnp.float32).reshape(shp)
    o = hgrn2_recurrence(q, k, v, jnp.log(f))
    o = rmsnorm(o, gnorm_g) * jax.nn.silu(zg.astype(jnp.float32).reshape(shp))
    return o.reshape(B, T, HG_WIDTH).astype(zq.dtype)


def chunked_spatial_gating(zu, zv, ln_g, ln_b, w_s, b_s):
    B, T, _ = zu.shape
    n = T // GM_CHUNK
    u = jax.nn.gelu(zu, approximate=False)
    v = layernorm(jax.nn.gelu(zv, approximate=False), ln_g, ln_b)
    vc = v.reshape(B, n, GM_CHUNK, GM_GROUPS, GM_GROUP_DIM)
    causal = jnp.tril(jnp.ones((GM_CHUNK, GM_CHUNK), dtype=bool))
    w = jnp.where(causal[None], w_s, 0.0).astype(v.dtype)
    mixed = jnp.einsum('gts,bnsgc->bntgc', w, vc) + b_s.T.astype(v.dtype)[None, None, :, :, None]
    return u * mixed.reshape(B, T, GM_WIDTH)


def _fwd_setup_inputs(seed: int = 0) -> dict:
    key = jax.random.key(seed)
    ks = iter(jax.random.split(key, 32))

    def nrm(shape, scale):
        return jax.random.normal(next(ks), shape, jnp.float32) * scale

    def gain(shape):
        return 1.0 + nrm(shape, 0.05)

    return {
        "x": nrm((BATCH, SEQ, D_MODEL), 1.0),
        "mem": nrm((BATCH, MEM_LEN, D_MODEL), 1.0),
        "mem_norm": gain((D_MODEL,)),
        "lb_logits": nrm((DEPTH + 1, HG_WIDTH), 0.1),
        "ffn1_norm": gain((DEPTH, D_MODEL)),
        "ffn1_w_in": nrm((DEPTH, D_MODEL, 2 * D_FF), D_MODEL ** -0.5),
        "ffn1_w_out": nrm((DEPTH, D_FF, D_MODEL), D_FF ** -0.5),
        "mix_norm": gain((DEPTH, D_MODEL)),
        "mem_w_kv": nrm((DEPTH, D_MODEL, 2 * XA_WIDTH), D_MODEL ** -0.5),
        "hgrn_w_in": nrm((N_HGRN, D_MODEL, HG_IN), D_MODEL ** -0.5),
        "hgrn_gnorm": gain((N_HGRN, HG_DIM)),
        "hgrn_w_out": nrm((N_HGRN, HG_WIDTH + XA_WIDTH, D_MODEL), (HG_WIDTH + XA_WIDTH) ** -0.5),
        "gmlp_w_in": nrm((N_GMLP, D_MODEL, GM_IN), D_MODEL ** -0.5),
        "gmlp_ln_g": gain((N_GMLP, GM_WIDTH)),
        "gmlp_ln_b": nrm((N_GMLP, GM_WIDTH), 0.02),
        "gmlp_w_s": nrm((N_GMLP, GM_GROUPS, GM_CHUNK, GM_CHUNK), GM_CHUNK ** -0.5),
        "gmlp_b_s": 1.0 + nrm((N_GMLP, GM_GROUPS, GM_CHUNK), 0.1),
        "gmlp_w_out": nrm((N_GMLP, GM_WIDTH + XA_WIDTH, D_MODEL), (GM_WIDTH + XA_WIDTH) ** -0.5),
        "ffn2_norm": gain((DEPTH, D_MODEL)),
        "ffn2_w_in": nrm((DEPTH, D_MODEL, 2 * D_FF), D_MODEL ** -0.5),
        "ffn2_w_out": nrm((DEPTH, D_FF, D_MODEL), D_FF ** -0.5),
        "final_norm": gain((D_MODEL,)),
    }


def _fwd_reference(x, mem, mem_norm, lb_logits, ffn1_norm, ffn1_w_in, ffn1_w_out, mix_norm, mem_w_kv,
              hgrn_w_in, hgrn_gnorm, hgrn_w_out, gmlp_w_in, gmlp_ln_g, gmlp_ln_b, gmlp_w_s, gmlp_b_s,
              gmlp_w_out, ffn2_norm, ffn2_w_in, ffn2_w_out, final_norm):
    B, T, _ = x.shape
    M = mem.shape[1]
    mem_n = rmsnorm(mem, mem_norm)
    lower_bounds = jnp.cumsum(jax.nn.softmax(lb_logits.astype(jnp.float32), axis=0), axis=0)

    for i in range(DEPTH):
        x = x + 0.5 * swiglu_ffn(rmsnorm(x, ffn1_norm[i]), ffn1_w_in[i], ffn1_w_out[i])

        h = rmsnorm(x, mix_norm[i])
        mk, mv = jnp.split(mem_n @ mem_w_kv[i], 2, axis=-1)
        mk = mk.reshape(B, M, XA_HEADS, XA_DIM)
        mv = mv.reshape(B, M, XA_HEADS, XA_DIM)
        j = i // N_MIXERS
        if i % N_MIXERS == 0:
            z = h @ hgrn_w_in[j]
            zq, zf, zi, zg, zx = jnp.split(z, [HG_WIDTH, 2 * HG_WIDTH, 3 * HG_WIDTH, 4 * HG_WIDTH], axis=-1)
            o_mix = hgrn2_mixer(zq, zf, zi, zg, lower_bounds[i], hgrn_gnorm[j])
            w_out = hgrn_w_out[j]
        else:
            z = h @ gmlp_w_in[j]
            zu, zv, zx = jnp.split(z, [GM_WIDTH, 2 * GM_WIDTH], axis=-1)
            o_mix = chunked_spatial_gating(zu, zv, gmlp_ln_g[j], gmlp_ln_b[j], gmlp_w_s[j], gmlp_b_s[j])
            w_out = gmlp_w_out[j]
        o_mem = memory_attention(zx, mk, mv)
        x = x + jnp.concatenate([o_mix, o_mem], axis=-1) @ w_out

        x = x + 0.5 * swiglu_ffn(rmsnorm(x, ffn2_norm[i]), ffn2_w_in[i], ffn2_w_out[i])

    return rmsnorm(x, final_norm)


import jax as _jax
import jax.numpy as _jnp

TWIN_FORMAT = 'train_step'
FWD_PARAMS = ['x', 'mem', 'mem_norm', 'lb_logits', 'ffn1_norm', 'ffn1_w_in', 'ffn1_w_out', 'mix_norm', 'mem_w_kv', 'hgrn_w_in', 'hgrn_gnorm', 'hgrn_w_out', 'gmlp_w_in', 'gmlp_ln_g', 'gmlp_ln_b', 'gmlp_w_s', 'gmlp_b_s', 'gmlp_w_out', 'ffn2_norm', 'ffn2_w_in', 'ffn2_w_out', 'final_norm']
TWIN_WEIGHTS = ['mem_norm', 'lb_logits', 'ffn1_norm', 'ffn1_w_in', 'ffn1_w_out', 'mix_norm', 'mem_w_kv', 'hgrn_w_in', 'hgrn_gnorm', 'hgrn_w_out', 'gmlp_w_in', 'gmlp_ln_g', 'gmlp_ln_b', 'gmlp_w_s', 'gmlp_b_s', 'gmlp_w_out', 'ffn2_norm', 'ffn2_w_in', 'ffn2_w_out', 'final_norm']
TWIN_DIFF_INPUT = 'x'
TWIN_INPUTS = ['x', 'mem', 'mem_norm', 'lb_logits', 'ffn1_norm', 'ffn1_w_in', 'ffn1_w_out', 'mix_norm', 'mem_w_kv', 'hgrn_w_in', 'hgrn_gnorm', 'hgrn_w_out', 'gmlp_w_in', 'gmlp_ln_g', 'gmlp_ln_b', 'gmlp_w_s', 'gmlp_b_s', 'gmlp_w_out', 'ffn2_norm', 'ffn2_w_in', 'ffn2_w_out', 'final_norm', 'loss_target', 'm_mem_norm', 'm_lb_logits', 'm_ffn1_norm', 'm_ffn1_w_in', 'm_ffn1_w_out', 'm_mix_norm', 'm_mem_w_kv', 'm_hgrn_w_in', 'm_hgrn_gnorm', 'm_hgrn_w_out', 'm_gmlp_w_in', 'm_gmlp_ln_g', 'm_gmlp_ln_b', 'm_gmlp_w_s', 'm_gmlp_b_s', 'm_gmlp_w_out', 'm_ffn2_norm', 'm_ffn2_w_in', 'm_ffn2_w_out', 'm_final_norm', 'v_mem_norm', 'v_lb_logits', 'v_ffn1_norm', 'v_ffn1_w_in', 'v_ffn1_w_out', 'v_mix_norm', 'v_mem_w_kv', 'v_hgrn_w_in', 'v_hgrn_gnorm', 'v_hgrn_w_out', 'v_gmlp_w_in', 'v_gmlp_ln_g', 'v_gmlp_ln_b', 'v_gmlp_w_s', 'v_gmlp_b_s', 'v_gmlp_w_out', 'v_ffn2_norm', 'v_ffn2_w_in', 'v_ffn2_w_out', 'v_final_norm']
TWIN_OUTPUTS = ['loss', 'grad_x', 'grad_mem_norm', 'grad_lb_logits', 'grad_ffn1_norm', 'grad_ffn1_w_in', 'grad_ffn1_w_out', 'grad_mix_norm', 'grad_mem_w_kv', 'grad_hgrn_w_in', 'grad_hgrn_gnorm', 'grad_hgrn_w_out', 'grad_gmlp_w_in', 'grad_gmlp_ln_g', 'grad_gmlp_ln_b', 'grad_gmlp_w_s', 'grad_gmlp_b_s', 'grad_gmlp_w_out', 'grad_ffn2_norm', 'grad_ffn2_w_in', 'grad_ffn2_w_out', 'grad_final_norm', 'delta_mem_norm', 'delta_lb_logits', 'delta_ffn1_norm', 'delta_ffn1_w_in', 'delta_ffn1_w_out', 'delta_mix_norm', 'delta_mem_w_kv', 'delta_hgrn_w_in', 'delta_hgrn_gnorm', 'delta_hgrn_w_out', 'delta_gmlp_w_in', 'delta_gmlp_ln_g', 'delta_gmlp_ln_b', 'delta_gmlp_w_s', 'delta_gmlp_b_s', 'delta_gmlp_w_out', 'delta_ffn2_norm', 'delta_ffn2_w_in', 'delta_ffn2_w_out', 'delta_final_norm', 'new_m_mem_norm', 'new_m_lb_logits', 'new_m_ffn1_norm', 'new_m_ffn1_w_in', 'new_m_ffn1_w_out', 'new_m_mix_norm', 'new_m_mem_w_kv', 'new_m_hgrn_w_in', 'new_m_hgrn_gnorm', 'new_m_hgrn_w_out', 'new_m_gmlp_w_in', 'new_m_gmlp_ln_g', 'new_m_gmlp_ln_b', 'new_m_gmlp_w_s', 'new_m_gmlp_b_s', 'new_m_gmlp_w_out', 'new_m_ffn2_norm', 'new_m_ffn2_w_in', 'new_m_ffn2_w_out', 'new_m_final_norm', 'new_v_mem_norm', 'new_v_lb_logits', 'new_v_ffn1_norm', 'new_v_ffn1_w_in', 'new_v_ffn1_w_out', 'new_v_mix_norm', 'new_v_mem_w_kv', 'new_v_hgrn_w_in', 'new_v_hgrn_gnorm', 'new_v_hgrn_w_out', 'new_v_gmlp_w_in', 'new_v_gmlp_ln_g', 'new_v_gmlp_ln_b', 'new_v_gmlp_w_s', 'new_v_gmlp_b_s', 'new_v_gmlp_w_out', 'new_v_ffn2_norm', 'new_v_ffn2_w_in', 'new_v_ffn2_w_out', 'new_v_final_norm']
TWIN_LEAF_KINDS = {'loss': 'loss', 'grad_x': 'grad_x', 'grad_mem_norm': 'grad_w', 'grad_lb_logits': 'grad_w', 'grad_ffn1_norm': 'grad_w', 'grad_ffn1_w_in': 'grad_w', 'grad_ffn1_w_out': 'grad_w', 'grad_mix_norm': 'grad_w', 'grad_mem_w_kv': 'grad_w', 'grad_hgrn_w_in': 'grad_w', 'grad_hgrn_gnorm': 'grad_w', 'grad_hgrn_w_out': 'grad_w', 'grad_gmlp_w_in': 'grad_w', 'grad_gmlp_ln_g': 'grad_w', 'grad_gmlp_ln_b': 'grad_w', 'grad_gmlp_w_s': 'grad_w', 'grad_gmlp_b_s': 'grad_w', 'grad_gmlp_w_out': 'grad_w', 'grad_ffn2_norm': 'grad_w', 'grad_ffn2_w_in': 'grad_w', 'grad_ffn2_w_out': 'grad_w', 'grad_final_norm': 'grad_w', 'delta_mem_norm': 'delta_w', 'delta_lb_logits': 'delta_w', 'delta_ffn1_norm': 'delta_w', 'delta_ffn1_w_in': 'delta_w', 'delta_ffn1_w_out': 'delta_w', 'delta_mix_norm': 'delta_w', 'delta_mem_w_kv': 'delta_w', 'delta_hgrn_w_in': 'delta_w', 'delta_hgrn_gnorm': 'delta_w', 'delta_hgrn_w_out': 'delta_w', 'delta_gmlp_w_in': 'delta_w', 'delta_gmlp_ln_g': 'delta_w', 'delta_gmlp_ln_b': 'delta_w', 'delta_gmlp_w_s': 'delta_w', 'delta_gmlp_b_s': 'delta_w', 'delta_gmlp_w_out': 'delta_w', 'delta_ffn2_norm': 'delta_w', 'delta_ffn2_w_in': 'delta_w', 'delta_ffn2_w_out': 'delta_w', 'delta_final_norm': 'delta_w', 'new_m_mem_norm': 'new_m', 'new_m_lb_logits': 'new_m', 'new_m_ffn1_norm': 'new_m', 'new_m_ffn1_w_in': 'new_m', 'new_m_ffn1_w_out': 'new_m', 'new_m_mix_norm': 'new_m', 'new_m_mem_w_kv': 'new_m', 'new_m_hgrn_w_in': 'new_m', 'new_m_hgrn_gnorm': 'new_m', 'new_m_hgrn_w_out': 'new_m', 'new_m_gmlp_w_in': 'new_m', 'new_m_gmlp_ln_g': 'new_m', 'new_m_gmlp_ln_b': 'new_m', 'new_m_gmlp_w_s': 'new_m', 'new_m_gmlp_b_s': 'new_m', 'new_m_gmlp_w_out': 'new_m', 'new_m_ffn2_norm': 'new_m', 'new_m_ffn2_w_in': 'new_m', 'new_m_ffn2_w_out': 'new_m', 'new_m_final_norm': 'new_m', 'new_v_mem_norm': 'new_v', 'new_v_lb_logits': 'new_v', 'new_v_ffn1_norm': 'new_v', 'new_v_ffn1_w_in': 'new_v', 'new_v_ffn1_w_out': 'new_v', 'new_v_mix_norm': 'new_v', 'new_v_mem_w_kv': 'new_v', 'new_v_hgrn_w_in': 'new_v', 'new_v_hgrn_gnorm': 'new_v', 'new_v_hgrn_w_out': 'new_v', 'new_v_gmlp_w_in': 'new_v', 'new_v_gmlp_ln_g': 'new_v', 'new_v_gmlp_ln_b': 'new_v', 'new_v_gmlp_w_s': 'new_v', 'new_v_gmlp_b_s': 'new_v', 'new_v_gmlp_w_out': 'new_v', 'new_v_ffn2_norm': 'new_v', 'new_v_ffn2_w_in': 'new_v', 'new_v_ffn2_w_out': 'new_v', 'new_v_final_norm': 'new_v'}


def _forward(args):
    return _fwd_reference(*[args[k] for k in FWD_PARAMS])


def _output_shape():
    out = _jax.eval_shape(lambda: _forward(_fwd_setup_inputs(0)))
    return out.shape, out.dtype

N_MICROBATCH = 1
ADAM_LR = 0.001
ADAM_B1 = 0.9
ADAM_B2 = 0.999
ADAM_EPS = 1e-08
ADAM_WD = 0.01
ADAM_STEP = 10
PER_EXAMPLE_BATCH_AXIS = {'x': 0, 'mem': 0, 'loss_target': 0}
SHARED_INPUTS = []
_WEIGHT_DTYPES = {'mem_norm': _jnp.float32, 'lb_logits': _jnp.float32, 'ffn1_norm': _jnp.float32, 'ffn1_w_in': _jnp.float32, 'ffn1_w_out': _jnp.float32, 'mix_norm': _jnp.float32, 'mem_w_kv': _jnp.float32, 'hgrn_w_in': _jnp.float32, 'hgrn_gnorm': _jnp.float32, 'hgrn_w_out': _jnp.float32, 'gmlp_w_in': _jnp.float32, 'gmlp_ln_g': _jnp.float32, 'gmlp_ln_b': _jnp.float32, 'gmlp_w_s': _jnp.float32, 'gmlp_b_s': _jnp.float32, 'gmlp_w_out': _jnp.float32, 'ffn2_norm': _jnp.float32, 'ffn2_w_in': _jnp.float32, 'ffn2_w_out': _jnp.float32, 'final_norm': _jnp.float32}
MOMENT_SCALE = {'mem_norm': 2.439683e-02, 'lb_logits': 3.809870e-03, 'ffn1_norm': 8.442147e-02, 'ffn1_w_in': 3.444821e-02, 'ffn1_w_out': 5.643642e-02, 'mix_norm': 1.152439e-01, 'mem_w_kv': 1.149605e-02, 'hgrn_w_in': 5.031043e-02, 'hgrn_gnorm': 2.217639e-01, 'hgrn_w_out': 7.801778e-02, 'gmlp_w_in': 4.838312e-02, 'gmlp_ln_g': 3.523843e-02, 'gmlp_ln_b': 3.457385e-02, 'gmlp_w_s': 5.084247e-02, 'gmlp_b_s': 7.031734e-02, 'gmlp_w_out': 9.964145e-02, 'ffn2_norm': 6.681917e-02, 'ffn2_w_in': 2.780073e-02, 'ffn2_w_out': 4.535595e-02, 'final_norm': 3.211521e+01}


def _to_microbatches(a, axis):
    t = _jnp.moveaxis(a, axis, 0)
    t = t.reshape((N_MICROBATCH, t.shape[0] // N_MICROBATCH) + t.shape[1:])
    return _jnp.moveaxis(t, 1, axis + 1)


def setup_inputs(seed: int = 0) -> dict:
    inp = _fwd_setup_inputs(seed)
    key = _jax.random.fold_in(_jax.random.key(seed), 7919)
    shape, _ = _output_shape()
    out = dict(inp)
    out["loss_target"] = _jax.random.normal(_jax.random.fold_in(key, 0), shape, _jnp.float32)
    for i, name in enumerate(TWIN_WEIGHTS):
        w = inp[name].astype(_jnp.float32)
        if MOMENT_SCALE is None:
            s = _jnp.sqrt(_jnp.mean(_jnp.square(w)) + 1e-30)
        else:
            s = MOMENT_SCALE[name]
        km, kv = _jax.random.split(_jax.random.fold_in(key, i + 1))
        out[name] = w
        out["m_" + name] = s * _jax.random.normal(km, w.shape, _jnp.float32)
        out["v_" + name] = (s * s) * _jax.random.uniform(kv, w.shape, _jnp.float32, 0.5, 1.5)
    if N_MICROBATCH > 1:
        for name, axis in PER_EXAMPLE_BATCH_AXIS.items():
            out[name] = _to_microbatches(out[name], axis)
    return {'x': out['x'], 'mem': out['mem'], 'mem_norm': out['mem_norm'], 'lb_logits': out['lb_logits'], 'ffn1_norm': out['ffn1_norm'], 'ffn1_w_in': out['ffn1_w_in'], 'ffn1_w_out': out['ffn1_w_out'], 'mix_norm': out['mix_norm'], 'mem_w_kv': out['mem_w_kv'], 'hgrn_w_in': out['hgrn_w_in'], 'hgrn_gnorm': out['hgrn_gnorm'], 'hgrn_w_out': out['hgrn_w_out'], 'gmlp_w_in': out['gmlp_w_in'], 'gmlp_ln_g': out['gmlp_ln_g'], 'gmlp_ln_b': out['gmlp_ln_b'], 'gmlp_w_s': out['gmlp_w_s'], 'gmlp_b_s': out['gmlp_b_s'], 'gmlp_w_out': out['gmlp_w_out'], 'ffn2_norm': out['ffn2_norm'], 'ffn2_w_in': out['ffn2_w_in'], 'ffn2_w_out': out['ffn2_w_out'], 'final_norm': out['final_norm'], 'loss_target': out['loss_target'], 'm_mem_norm': out['m_mem_norm'], 'm_lb_logits': out['m_lb_logits'], 'm_ffn1_norm': out['m_ffn1_norm'], 'm_ffn1_w_in': out['m_ffn1_w_in'], 'm_ffn1_w_out': out['m_ffn1_w_out'], 'm_mix_norm': out['m_mix_norm'], 'm_mem_w_kv': out['m_mem_w_kv'], 'm_hgrn_w_in': out['m_hgrn_w_in'], 'm_hgrn_gnorm': out['m_hgrn_gnorm'], 'm_hgrn_w_out': out['m_hgrn_w_out'], 'm_gmlp_w_in': out['m_gmlp_w_in'], 'm_gmlp_ln_g': out['m_gmlp_ln_g'], 'm_gmlp_ln_b': out['m_gmlp_ln_b'], 'm_gmlp_w_s': out['m_gmlp_w_s'], 'm_gmlp_b_s': out['m_gmlp_b_s'], 'm_gmlp_w_out': out['m_gmlp_w_out'], 'm_ffn2_norm': out['m_ffn2_norm'], 'm_ffn2_w_in': out['m_ffn2_w_in'], 'm_ffn2_w_out': out['m_ffn2_w_out'], 'm_final_norm': out['m_final_norm'], 'v_mem_norm': out['v_mem_norm'], 'v_lb_logits': out['v_lb_logits'], 'v_ffn1_norm': out['v_ffn1_norm'], 'v_ffn1_w_in': out['v_ffn1_w_in'], 'v_ffn1_w_out': out['v_ffn1_w_out'], 'v_mix_norm': out['v_mix_norm'], 'v_mem_w_kv': out['v_mem_w_kv'], 'v_hgrn_w_in': out['v_hgrn_w_in'], 'v_hgrn_gnorm': out['v_hgrn_gnorm'], 'v_hgrn_w_out': out['v_hgrn_w_out'], 'v_gmlp_w_in': out['v_gmlp_w_in'], 'v_gmlp_ln_g': out['v_gmlp_ln_g'], 'v_gmlp_ln_b': out['v_gmlp_ln_b'], 'v_gmlp_w_s': out['v_gmlp_w_s'], 'v_gmlp_b_s': out['v_gmlp_b_s'], 'v_gmlp_w_out': out['v_gmlp_w_out'], 'v_ffn2_norm': out['v_ffn2_norm'], 'v_ffn2_w_in': out['v_ffn2_w_in'], 'v_ffn2_w_out': out['v_ffn2_w_out'], 'v_final_norm': out['v_final_norm']}


def _loss(weights, diff, rest, loss_target):
    with _jax.named_scope("forward"):
        args = {**rest, TWIN_DIFF_INPUT: diff, **{k: w.astype(_WEIGHT_DTYPES[k]) for k, w in weights.items()}}
        y = _forward(args)
    with _jax.named_scope("loss_head"):
        err = _jnp.square(y.astype(_jnp.float32) - loss_target)
        return 0.5 * _jnp.sum(_jnp.mean(err, axis=-1)) if err.ndim else 0.5 * err


def _adamw(w, g, m, v):
    m = ADAM_B1 * m + (1.0 - ADAM_B1) * g
    v = ADAM_B2 * v + (1.0 - ADAM_B2) * _jnp.square(g)
    m_hat = m / (1.0 - ADAM_B1 ** ADAM_STEP)
    v_hat = v / (1.0 - ADAM_B2 ** ADAM_STEP)
    delta = -ADAM_LR * (m_hat / (_jnp.sqrt(v_hat) + ADAM_EPS) + ADAM_WD * w)
    return delta, m, v


def reference(x, mem, mem_norm, lb_logits, ffn1_norm, ffn1_w_in, ffn1_w_out, mix_norm, mem_w_kv, hgrn_w_in, hgrn_gnorm, hgrn_w_out, gmlp_w_in, gmlp_ln_g, gmlp_ln_b, gmlp_w_s, gmlp_b_s, gmlp_w_out, ffn2_norm, ffn2_w_in, ffn2_w_out, final_norm, loss_target, m_mem_norm, m_lb_logits, m_ffn1_norm, m_ffn1_w_in, m_ffn1_w_out, m_mix_norm, m_mem_w_kv, m_hgrn_w_in, m_hgrn_gnorm, m_hgrn_w_out, m_gmlp_w_in, m_gmlp_ln_g, m_gmlp_ln_b, m_gmlp_w_s, m_gmlp_b_s, m_gmlp_w_out, m_ffn2_norm, m_ffn2_w_in, m_ffn2_w_out, m_final_norm, v_mem_norm, v_lb_logits, v_ffn1_norm, v_ffn1_w_in, v_ffn1_w_out, v_mix_norm, v_mem_w_kv, v_hgrn_w_in, v_hgrn_gnorm, v_hgrn_w_out, v_gmlp_w_in, v_gmlp_ln_g, v_gmlp_ln_b, v_gmlp_w_s, v_gmlp_b_s, v_gmlp_w_out, v_ffn2_norm, v_ffn2_w_in, v_ffn2_w_out, v_final_norm):
    given = dict(x=x, mem=mem, mem_norm=mem_norm, lb_logits=lb_logits, ffn1_norm=ffn1_norm, ffn1_w_in=ffn1_w_in, ffn1_w_out=ffn1_w_out, mix_norm=mix_norm, mem_w_kv=mem_w_kv, hgrn_w_in=hgrn_w_in, hgrn_gnorm=hgrn_gnorm, hgrn_w_out=hgrn_w_out, gmlp_w_in=gmlp_w_in, gmlp_ln_g=gmlp_ln_g, gmlp_ln_b=gmlp_ln_b, gmlp_w_s=gmlp_w_s, gmlp_b_s=gmlp_b_s, gmlp_w_out=gmlp_w_out, ffn2_norm=ffn2_norm, ffn2_w_in=ffn2_w_in, ffn2_w_out=ffn2_w_out, final_norm=final_norm, loss_target=loss_target, m_mem_norm=m_mem_norm, m_lb_logits=m_lb_logits, m_ffn1_norm=m_ffn1_norm, m_ffn1_w_in=m_ffn1_w_in, m_ffn1_w_out=m_ffn1_w_out, m_mix_norm=m_mix_norm, m_mem_w_kv=m_mem_w_kv, m_hgrn_w_in=m_hgrn_w_in, m_hgrn_gnorm=m_hgrn_gnorm, m_hgrn_w_out=m_hgrn_w_out, m_gmlp_w_in=m_gmlp_w_in, m_gmlp_ln_g=m_gmlp_ln_g, m_gmlp_ln_b=m_gmlp_ln_b, m_gmlp_w_s=m_gmlp_w_s, m_gmlp_b_s=m_gmlp_b_s, m_gmlp_w_out=m_gmlp_w_out, m_ffn2_norm=m_ffn2_norm, m_ffn2_w_in=m_ffn2_w_in, m_ffn2_w_out=m_ffn2_w_out, m_final_norm=m_final_norm, v_mem_norm=v_mem_norm, v_lb_logits=v_lb_logits, v_ffn1_norm=v_ffn1_norm, v_ffn1_w_in=v_ffn1_w_in, v_ffn1_w_out=v_ffn1_w_out, v_mix_norm=v_mix_norm, v_mem_w_kv=v_mem_w_kv, v_hgrn_w_in=v_hgrn_w_in, v_hgrn_gnorm=v_hgrn_gnorm, v_hgrn_w_out=v_hgrn_w_out, v_gmlp_w_in=v_gmlp_w_in, v_gmlp_ln_g=v_gmlp_ln_g, v_gmlp_ln_b=v_gmlp_ln_b, v_gmlp_w_s=v_gmlp_w_s, v_gmlp_b_s=v_gmlp_b_s, v_gmlp_w_out=v_gmlp_w_out, v_ffn2_norm=v_ffn2_norm, v_ffn2_w_in=v_ffn2_w_in, v_ffn2_w_out=v_ffn2_w_out, v_final_norm=v_final_norm)
    weights = {n: given[n] for n in TWIN_WEIGHTS}
    shared = {n: given[n] for n in SHARED_INPUTS}
    per_example = {n: given[n] for n in ['x', 'mem']}
    grad_fn = _jax.value_and_grad(_loss, argnums=(0, 1))

    def one_microbatch(ex, loss_target):
        ex = dict(ex)
        diff = ex.pop(TWIN_DIFF_INPUT)
        return grad_fn(weights, diff, {**shared, **ex}, loss_target)

    if N_MICROBATCH == 1:
        loss, (grad_w, grad_x) = one_microbatch(per_example, given["loss_target"])
    else:
        def body(carry, xs):
            loss_sum, grad_sum = carry
            l_k, (gw_k, gx_k) = one_microbatch(xs[0], xs[1])
            with _jax.named_scope("update"):
                return (loss_sum + l_k, _jax.tree.map(_jnp.add, grad_sum, gw_k)), gx_k

        init = (_jnp.zeros((), _jnp.float32), _jax.tree.map(_jnp.zeros_like, weights))
        (loss, grad_w), grad_x = _jax.lax.scan(body, init, (per_example, given["loss_target"]))
    with _jax.named_scope("update"):
        delta_w, new_m, new_v = {}, {}, {}
        for n in TWIN_WEIGHTS:
            delta_w[n], new_m[n], new_v[n] = _adamw(weights[n], grad_w[n], given["m_" + n], given["v_" + n])
    return (loss, grad_x, *[grad_w[n] for n in TWIN_WEIGHTS], *[delta_w[n] for n in TWIN_WEIGHTS],
            *[new_m[n] for n in TWIN_WEIGHTS], *[new_v[n] for n in TWIN_WEIGHTS])
```

```python
import functools
import math

import jax
import jax.numpy as jnp
from jax import lax
from jax.experimental import pallas as pl
from jax.experimental.pallas import tpu as pltpu

F32 = jnp.float32
BF16 = jnp.bfloat16

D_MODEL = 1024
SEQ = 2048
B_LOC = 2
N_TOK = B_LOC * SEQ
MEM_LEN = 256
N_DEV = 8
EPS = 1e-6
D_FF = 2816
HG_HEADS = 8
HG_DIM = 128
HG_CHUNK = 64
HG_NCHUNK = SEQ // HG_CHUNK
GM_CHUNK = 128
GM_GROUPS = 8
GM_WIDTH = 2048
GM_GDIM = GM_WIDTH // GM_GROUPS
XA_HEADS = 4
XA_DIM = 256
XA_OFF = 4096

ADAM_LR = 0.001
ADAM_B1 = 0.9
ADAM_B2 = 0.999
ADAM_EPS = 1e-08
ADAM_WD = 0.01
ADAM_STEP = 10

VMEM_LIMIT_BYTES = 56 * 1024 * 1024
MESH_AXES = ("x", "y", "c")

GROUPS = (
    ("ffn1_w_in", True, 2, 704),
    ("ffn1_w_out", False, 2, 352),
    ("mem_w_kv", True, 2, 256),
    ("hgrn_w_in", True, 1, 640),
    ("hgrn_w_out", False, 1, 256),
    ("gmlp_w_in", True, 1, 640),
    ("gmlp_w_out", False, 1, 384),
    ("ffn2_w_in", True, 2, 704),
    ("ffn2_w_out", False, 2, 352),
)
PIECES = tuple((g, l) for g, (_, _, layers, _) in enumerate(GROUPS) for l in range(layers))
SMALL_ROWS = 152


def _cp(*sem):
    return pltpu.CompilerParams(dimension_semantics=sem, vmem_limit_bytes=VMEM_LIMIT_BYTES)


def _sigmoid(x):
    return 1.0 / (1.0 + jnp.exp(-x))


def _gelu_parts(x):
    cdf = 0.5 * (1.0 + lax.erf(x * (1.0 / math.sqrt(2.0))))
    pdf = jnp.exp(-0.5 * x * x) * (1.0 / math.sqrt(2.0 * math.pi))
    return x * cdf, cdf + x * pdf


def _mm(a, b, *, ta=False, tb=False, tm, tn, tk, out_dtype, res=None, scale=1.0, name):
    m, k = (a.shape[1], a.shape[0]) if ta else a.shape
    n, kb = b.shape if tb else (b.shape[1], b.shape[0])
    assert k == kb and m % tm == 0 and n % tn == 0 and k % tk == 0, (name, a.shape, b.shape)
    nk = k // tk
    dn = (((0 if ta else 1,), (1 if tb else 0,)), ((), ()))

    def body(*refs):
        if res is None:
            a_ref, b_ref, o_ref = refs[:3]
            r_ref, scr = None, refs[3:]
        else:
            a_ref, b_ref, r_ref, o_ref = refs[:4]
            scr = refs[4:]
        p = lax.dot_general(a_ref[...].astype(BF16), b_ref[...].astype(BF16), dn, preferred_element_type=F32)

        def finish(acc):
            if scale != 1.0:
                acc = scale * acc
            if r_ref is not None:
                acc = r_ref[...] + acc
            o_ref[...] = acc.astype(out_dtype)

        if nk == 1:
            finish(p)
        else:
            acc_ref = scr[0]
            kk = pl.program_id(2)

            @pl.when(kk == 0)
            def _():
                acc_ref[...] = p

            @pl.when(kk > 0)
            def _():
                acc_ref[...] += p

            @pl.when(kk == nk - 1)
            def _():
                finish(acc_ref[...])

    a_spec = pl.BlockSpec((tk, tm), lambda i, j, kk: (kk, i)) if ta else pl.BlockSpec((tm, tk), lambda i, j, kk: (i, kk))
    b_spec = pl.BlockSpec((tn, tk), lambda i, j, kk: (j, kk)) if tb else pl.BlockSpec((tk, tn), lambda i, j, kk: (kk, j))
    o_spec = pl.BlockSpec((tm, tn), lambda i, j, kk: (i, j))
    in_specs = [a_spec, b_spec] + ([o_spec] if res is not None else [])
    args = (a, b) + ((res,) if res is not None else ())
    return pl.pallas_call(
        body,
        name=name,
        grid=(m // tm, n // tn, nk),
        in_specs=in_specs,
        out_specs=o_spec,
        out_shape=jax.ShapeDtypeStruct((m, n), out_dtype),
        scratch_shapes=[pltpu.VMEM((tm, tn), F32)] if nk > 1 else [],
        compiler_params=_cp("parallel", "parallel", "arbitrary"),
    )(*args)


def _rms_fwd(x, g, *, name, tm=512):
    rows = x.shape[0]

    def body(x_ref, g_ref, o_ref):
        xv = x_ref[...]
        r = lax.rsqrt(jnp.mean(xv * xv, axis=-1, keepdims=True) + EPS)
        o_ref[...] = (xv * r * g_ref[...]).astype(BF16)

    row = pl.BlockSpec((tm, D_MODEL), lambda i: (i, 0))
    return pl.pallas_call(
        body,
        name=name,
        grid=(rows // tm,),
        in_specs=[row, pl.BlockSpec((1, D_MODEL), lambda i: (0, 0))],
        out_specs=row,
        out_shape=jax.ShapeDtypeStruct((rows, D_MODEL), BF16),
        compiler_params=_cp("parallel"),
    )(x, g)


def _rms_bwd(x, g, dh, dres, *, name, tm=512):
    rows = x.shape[0]

    def body(x_ref, g_ref, dh_ref, dres_ref, dx_ref, dg_ref):
        xv = x_ref[...]
        r = lax.rsqrt(jnp.mean(xv * xv, axis=-1, keepdims=True) + EPS)
        xhat = xv * r
        dhv = dh_ref[...]
        part = jnp.sum(dhv * xhat, axis=0, keepdims=True)

        @pl.when(pl.program_id(0) == 0)
        def _():
            dg_ref[...] = part

        @pl.when(pl.program_id(0) > 0)
        def _():
            dg_ref[...] += part

        dxh = dhv * g_ref[...]
        dx_ref[...] = dres_ref[...] + r * (dxh - xhat * jnp.mean(dxh * xhat, axis=-1, keepdims=True))

    row = pl.BlockSpec((tm, D_MODEL), lambda i: (i, 0))
    vec = pl.BlockSpec((1, D_MODEL), lambda i: (0, 0))
    return pl.pallas_call(
        body,
        name=name,
        grid=(rows // tm,),
        in_specs=[row, vec, row, row],
        out_specs=[row, vec],
        out_shape=[jax.ShapeDtypeStruct((rows, D_MODEL), F32), jax.ShapeDtypeStruct((1, D_MODEL), F32)],
        compiler_params=_cp("arbitrary"),
    )(x, g, dh, dres)


def _swiglu_fwd(z, *, name, tm=512):
    def body(g_ref, u_ref, o_ref):
        gv = g_ref[...]
        o_ref[...] = (gv * _sigmoid(gv) * u_ref[...]).astype(BF16)

    return pl.pallas_call(
        body,
        name=name,
        grid=(N_TOK // tm,),
        in_specs=[pl.BlockSpec((tm, D_FF), lambda i: (i, 0)), pl.BlockSpec((tm, D_FF), lambda i: (i, 1))],
        out_specs=pl.BlockSpec((tm, D_FF), lambda i: (i, 0)),
        out_shape=jax.ShapeDtypeStruct((N_TOK, D_FF), BF16),
        compiler_params=_cp("parallel"),
    )(z, z)


def _swiglu_bwd(z, dact, *, scale, name, tm=512):
    def body(g_ref, u_ref, da_ref, o_ref):
        gv = g_ref[...]
        s = _sigmoid(gv)
        da = da_ref[...] * scale
        o_ref[:, :D_FF] = (da * u_ref[...] * (s * (1.0 + gv * (1.0 - s)))).astype(BF16)
        o_ref[:, D_FF:] = (da * (gv * s)).astype(BF16)

    half = lambda j: pl.BlockSpec((tm, D_FF), lambda i: (i, j))
    return pl.pallas_call(
        body,
        name=name,
        grid=(N_TOK // tm,),
        in_specs=[half(0), half(1), half(0)],
        out_specs=pl.BlockSpec((tm, 2 * D_FF), lambda i: (i, 0)),
        out_shape=jax.ShapeDtypeStruct((N_TOK, 2 * D_FF), BF16),
        compiler_params=_cp("parallel"),
    )(z, z, dact)


def _loss_head(x, g, target, *, tm=512):
    def body(x_ref, g_ref, t_ref, dx_ref, dg_ref, loss_ref):
        xv = x_ref[...]
        gv = g_ref[...]
        r = lax.rsqrt(jnp.mean(xv * xv, axis=-1, keepdims=True) + EPS)
        xhat = xv * r
        err = xhat * gv - t_ref[...]
        loss_part = jnp.zeros((1, 128), F32) + 0.5 * jnp.sum(jnp.mean(err * err, axis=-1, keepdims=True))
        dy = err * (1.0 / D_MODEL)
        dg_part = jnp.sum(dy * xhat, axis=0, keepdims=True)

        @pl.when(pl.program_id(0) == 0)
        def _():
            dg_ref[...] = dg_part
            loss_ref[...] = loss_part

        @pl.when(pl.program_id(0) > 0)
        def _():
            dg_ref[...] += dg_part
            loss_ref[...] += loss_part

        dxh = dy * gv
        dx_ref[...] = r * (dxh - xhat * jnp.mean(dxh * xhat, axis=-1, keepdims=True))

    row = pl.BlockSpec((tm, D_MODEL), lambda i: (i, 0))
    vec = pl.BlockSpec((1, D_MODEL), lambda i: (0, 0))
    return pl.pallas_call(
        body,
        name="loss_head",
        grid=(N_TOK // tm,),
        in_specs=[row, vec, row],
        out_specs=[row, vec, pl.BlockSpec((1, 128), lambda i: (0, 0))],
        out_shape=[
            jax.ShapeDtypeStruct((N_TOK, D_MODEL), F32),
            jax.ShapeDtypeStruct((1, D_MODEL), F32),
            jax.ShapeDtypeStruct((1, 128), F32),
        ],
        compiler_params=_cp("arbitrary"),
    )(x, g, target)


_NT = (((1,), (1,)), ((), ()))
_TN = (((0,), (0,)), ((), ()))
XA_TQ = 1024
XA_SCALE = XA_DIM ** -0.5


def _attn_probs(q16, k16):
    s = lax.dot_general(q16, k16, _NT, preferred_element_type=F32) * XA_SCALE
    e = jnp.exp(s - jnp.max(s, axis=-1, keepdims=True))
    return e / jnp.sum(e, axis=-1, keepdims=True)


def _attn_fwd(z, kv, *, name):
    nt = SEQ // XA_TQ

    def body(q_ref, k_ref, v_ref, o_ref):
        p = _attn_probs(q_ref[...].astype(BF16), k_ref[...].astype(BF16))
        o_ref[...] = jnp.dot(p.astype(BF16), v_ref[...].astype(BF16), preferred_element_type=F32).astype(BF16)

    return pl.pallas_call(
        body,
        name=name,
        grid=(B_LOC, XA_HEADS, nt),
        in_specs=[
            pl.BlockSpec((XA_TQ, XA_DIM), lambda b, h, t: (b * nt + t, XA_OFF // XA_DIM + h)),
            pl.BlockSpec((MEM_LEN, XA_DIM), lambda b, h, t: (b, h)),
            pl.BlockSpec((MEM_LEN, XA_DIM), lambda b, h, t: (b, XA_HEADS + h)),
        ],
        out_specs=pl.BlockSpec((XA_TQ, XA_DIM), lambda b, h, t: (b * nt + t, h)),
        out_shape=jax.ShapeDtypeStruct((N_TOK, XA_HEADS * XA_DIM), BF16),
        compiler_params=_cp("parallel", "parallel", "arbitrary"),
    )(z, kv, kv)


def _attn_bwd(z, kv, dcat, *, do_off, name):
    nt = SEQ // XA_TQ

    def body(q_ref, k_ref, v_ref, do_ref, dq_ref, dk_ref, dv_ref):
        q16 = q_ref[...].astype(BF16)
        k16 = k_ref[...].astype(BF16)
        v16 = v_ref[...].astype(BF16)
        do16 = do_ref[...].astype(BF16)
        p = _attn_probs(q16, k16)
        dv_part = lax.dot_general(p.astype(BF16), do16, _TN, preferred_element_type=F32)
        dp = lax.dot_general(do16, v16, _NT, preferred_element_type=F32)
        ds16 = (p * (dp - jnp.sum(dp * p, axis=-1, keepdims=True)) * XA_SCALE).astype(BF16)
        dq_ref[...] = jnp.dot(ds16, k16, preferred_element_type=F32).astype(BF16)
        dk_part = lax.dot_general(ds16, q16, _TN, preferred_element_type=F32)

        @pl.when(pl.program_id(2) == 0)
        def _():
            dk_ref[...] = dk_part
            dv_ref[...] = dv_part

        @pl.when(pl.program_id(2) > 0)
        def _():
            dk_ref[...] += dk_part
            dv_ref[...] += dv_part

    qspec = pl.BlockSpec((XA_TQ, XA_DIM), lambda b, h, t: (b * nt + t, XA_OFF // XA_DIM + h))
    kspec = lambda off: pl.BlockSpec((MEM_LEN, XA_DIM), lambda b, h, t: (b, off + h))
    return pl.pallas_call(
        body,
        name=name,
        grid=(B_LOC, XA_HEADS, nt),
        in_specs=[qspec, kspec(0), kspec(XA_HEADS),
                  pl.BlockSpec((XA_TQ, XA_DIM), lambda b, h, t: (b * nt + t, do_off // XA_DIM + h))],
        out_specs=[pl.BlockSpec((XA_TQ, XA_DIM), lambda b, h, t: (b * nt + t, h)), kspec(0), kspec(0)],
        out_shape=[
            jax.ShapeDtypeStruct((N_TOK, XA_HEADS * XA_DIM), BF16),
            jax.ShapeDtypeStruct((B_LOC * MEM_LEN, XA_HEADS * XA_DIM), F32),
            jax.ShapeDtypeStruct((B_LOC * MEM_LEN, XA_HEADS * XA_DIM), F32),
        ],
        compiler_params=_cp("parallel", "parallel", "arbitrary"),
    )(z, kv, kv, dcat)


def _tril(n):
    return lax.broadcasted_iota(jnp.int32, (n, n), 0) >= lax.broadcasted_iota(jnp.int32, (n, n), 1)


def _lower_bound(lbl):
    e = jnp.exp(lbl - jnp.max(lbl, axis=0, keepdims=True))
    p = e / jnp.sum(e, axis=0, keepdims=True)
    return p[0:1, :], p


def _hgrn_gates(zq, zf, lb, tril_f):
    sig = _sigmoid(zf)
    f = lb + (1.0 - lb) * sig
    kk = 1.0 - f
    sq = _sigmoid(zq)
    q = zq * sq
    b = jnp.dot(tril_f, jnp.log(f), preferred_element_type=F32, precision=lax.Precision.HIGHEST)
    bl = b[HG_CHUNK - 1:HG_CHUNK, :]
    return q, sq, sig, f, kk, b, bl


def _hgrn_zspec(section):
    return pl.BlockSpec((SEQ, HG_DIM), lambda h, b: (b, section * HG_HEADS + h))


def _hgrn_fwd(z, lb_logits, gnorm):
    def body(zq_ref, zf_ref, zi_ref, zg_ref, lbl_ref, gn_ref, o_ref, opre_ref, sall_ref, st_ref):
        lb, _ = _lower_bound(lbl_ref[...])
        gn = gn_ref[...]
        mask = _tril(HG_CHUNK)
        tril_f = mask.astype(F32)
        st_ref[...] = jnp.zeros_like(st_ref)

        def chunk(c, carry):
            rows = pl.ds(pl.multiple_of(c * HG_CHUNK, HG_CHUNK), HG_CHUNK)
            q, _, _, _, kk, b, bl = _hgrn_gates(zq_ref[rows, :], zf_ref[rows, :], lb, tril_f)
            v16 = zi_ref[rows, :].astype(BF16)
            qd16 = (q * jnp.exp(b)).astype(BF16)
            ki16 = (kk * jnp.exp(-b)).astype(BF16)
            kd16 = (kk * jnp.exp(bl - b)).astype(BF16)
            a = jnp.where(mask, lax.dot_general(qd16, ki16, _NT, preferred_element_type=F32), 0.0)
            st = st_ref[...]
            sall_ref[0, 0, c] = st
            o = jnp.dot(a.astype(BF16), v16, preferred_element_type=F32) + lax.dot_general(
                qd16, st.astype(BF16), _NT, preferred_element_type=F32)
            st_ref[...] = st * jnp.exp(bl) + lax.dot_general(v16, kd16, _TN, preferred_element_type=F32)
            opre_ref[rows, :] = o
            r = lax.rsqrt(jnp.mean(o * o, axis=-1, keepdims=True) + EPS)
            zg = zg_ref[rows, :]
            o_ref[rows, :] = ((o * r * gn) * (zg * _sigmoid(zg))).astype(BF16)
            return carry

        lax.fori_loop(0, HG_NCHUNK, chunk, 0)

    tok = pl.BlockSpec((SEQ, HG_DIM), lambda h, b: (b, h))
    return pl.pallas_call(
        body,
        name="hgrn_fwd",
        grid=(HG_HEADS, B_LOC),
        in_specs=[_hgrn_zspec(0), _hgrn_zspec(1), _hgrn_zspec(2), _hgrn_zspec(3),
                  pl.BlockSpec((3, HG_DIM), lambda h, b: (0, h)), pl.BlockSpec((1, HG_DIM), lambda h, b: (0, 0))],
        out_specs=[tok, tok, pl.BlockSpec((1, 1, HG_NCHUNK, HG_DIM, HG_DIM), lambda h, b: (b, h, 0, 0, 0))],
        out_shape=[
            jax.ShapeDtypeStruct((N_TOK, HG_HEADS * HG_DIM), BF16),
            jax.ShapeDtypeStruct((N_TOK, HG_HEADS * HG_DIM), F32),
            jax.ShapeDtypeStruct((B_LOC, HG_HEADS, HG_NCHUNK, HG_DIM, HG_DIM), F32),
        ],
        scratch_shapes=[pltpu.VMEM((HG_DIM, HG_DIM), F32)],
        compiler_params=_cp("parallel", "arbitrary"),
    )(z, z, z, z, lb_logits, gnorm)


def _hgrn_bwd(z, opre, dcat, sall, lb_logits, gnorm):
    def body(zq_ref, zf_ref, zi_ref, zg_ref, opre_ref, dout_ref, sall_ref, lbl_ref, gn_ref,
             dzq_ref, dzf_ref, dzi_ref, dzg_ref, dlbl_ref, dgn_ref, dst_ref, dlb_ref):
        h_id, b_id = pl.program_id(0), pl.program_id(1)
        lb, p = _lower_bound(lbl_ref[...])
        gn = gn_ref[...]
        mask = _tril(HG_CHUNK)
        tril_f = mask.astype(F32)
        dst_ref[...] = jnp.zeros_like(dst_ref)
        dlb_ref[...] = jnp.zeros_like(dlb_ref)

        @pl.when((h_id == 0) & (b_id == 0))
        def _():
            dgn_ref[...] = jnp.zeros_like(dgn_ref)

        def chunk(i, carry):
            c = HG_NCHUNK - 1 - i
            rows = pl.ds(pl.multiple_of(c * HG_CHUNK, HG_CHUNK), HG_CHUNK)
            zq, zg = zq_ref[rows, :], zg_ref[rows, :]
            q, sq, sig, f, kk, b, bl = _hgrn_gates(zq, zf_ref[rows, :], lb, tril_f)
            v16 = zi_ref[rows, :].astype(BF16)
            eb, enb, ebl_b, ebl = jnp.exp(b), jnp.exp(-b), jnp.exp(bl - b), jnp.exp(bl)
            qd, ki, kd = q * eb, kk * enb, kk * ebl_b
            qd16, ki16, kd16 = qd.astype(BF16), ki.astype(BF16), kd.astype(BF16)
            o = opre_ref[rows, :]
            dout = dout_ref[rows, :]
            r = lax.rsqrt(jnp.mean(o * o, axis=-1, keepdims=True) + EPS)
            ohat = o * r
            sg = _sigmoid(zg)
            d_on = dout * (zg * sg)
            dzg_ref[rows, :] = (dout * (ohat * gn) * (sg * (1.0 + zg * (1.0 - sg)))).astype(BF16)
            dgn_ref[...] += jnp.sum(d_on * ohat, axis=0, keepdims=True)
            dohat = d_on * gn
            do16 = (r * (dohat - ohat * jnp.mean(dohat * ohat, axis=-1, keepdims=True))).astype(BF16)
            st = sall_ref[0, 0, c]
            st16 = st.astype(BF16)
            dst = dst_ref[...]
            dst16 = dst.astype(BF16)
            a16 = jnp.where(mask, lax.dot_general(qd16, ki16, _NT, preferred_element_type=F32), 0.0).astype(BF16)
            da16 = jnp.where(mask, lax.dot_general(do16, v16, _NT, preferred_element_type=F32), 0.0).astype(BF16)
            dv = lax.dot_general(a16, do16, _TN, preferred_element_type=F32) + lax.dot_general(
                kd16, dst16, _NT, preferred_element_type=F32)
            dqd = jnp.dot(da16, ki16, preferred_element_type=F32) + jnp.dot(do16, st16, preferred_element_type=F32)
            dki = lax.dot_general(da16, qd16, _TN, preferred_element_type=F32)
            dkd = jnp.dot(v16, dst16, preferred_element_type=F32)
            dbl = jnp.sum(dkd * kd, axis=0, keepdims=True) + ebl * jnp.sum(st * dst, axis=0, keepdims=True)
            dst_ref[...] = dst * ebl + lax.dot_general(do16, qd16, _TN, preferred_element_type=F32)
            dzi_ref[rows, :] = dv.astype(BF16)
            dzq_ref[rows, :] = (dqd * eb * (sq * (1.0 + zq * (1.0 - sq)))).astype(BF16)
            dkk = dki * enb + dkd * ebl_b
            db = dqd * qd - dki * ki - dkd * kd
            dlogf = lax.dot_general(tril_f, db, _TN, preferred_element_type=F32, precision=lax.Precision.HIGHEST) + dbl
            df = dlogf / f - dkk
            dzf_ref[rows, :] = (df * (1.0 - lb) * sig * (1.0 - sig)).astype(BF16)
            dlb_ref[...] += jnp.sum(df * (1.0 - sig), axis=0, keepdims=True)
            return carry

        lax.fori_loop(0, HG_NCHUNK, chunk, 0)
        row0 = (lax.broadcasted_iota(jnp.int32, (3, HG_DIM), 0) == 0).astype(F32)
        dlbl_part = dlb_ref[...] * lb * (row0 - p)

        @pl.when(b_id == 0)
        def _():
            dlbl_ref[...] = dlbl_part

        @pl.when(b_id > 0)
        def _():
            dlbl_ref[...] += dlbl_part

    tok = pl.BlockSpec((SEQ, HG_DIM), lambda h, b: (b, h))
    tok_shape = jax.ShapeDtypeStruct((N_TOK, HG_HEADS * HG_DIM), BF16)
    return pl.pallas_call(
        body,
        name="hgrn_bwd",
        grid=(HG_HEADS, B_LOC),
        in_specs=[_hgrn_zspec(0), _hgrn_zspec(1), _hgrn_zspec(2), _hgrn_zspec(3), tok, tok,
                  pl.BlockSpec((1, 1, HG_NCHUNK, HG_DIM, HG_DIM), lambda h, b: (b, h, 0, 0, 0)),
                  pl.BlockSpec((3, HG_DIM), lambda h, b: (0, h)), pl.BlockSpec((1, HG_DIM), lambda h, b: (0, 0))],
        out_specs=[tok, tok, tok, tok, pl.BlockSpec((3, HG_DIM), lambda h, b: (0, h)),
                   pl.BlockSpec((1, HG_DIM), lambda h, b: (0, 0))],
        out_shape=[tok_shape, tok_shape, tok_shape, tok_shape,
                   jax.ShapeDtypeStruct((3, HG_HEADS * HG_DIM), F32), jax.ShapeDtypeStruct((1, HG_DIM), F32)],
        scratch_shapes=[pltpu.VMEM((HG_DIM, HG_DIM), F32), pltpu.VMEM((1, HG_DIM), F32)],
        compiler_params=_cp("arbitrary", "arbitrary"),
    )(z, z, z, z, opre, dcat, sall, lb_logits, gnorm)


GM_TM = 256


def _gmlp_norm(zv, ln_g, ln_b):
    gv, dgelu = _gelu_parts(zv)
    xc = gv - jnp.mean(gv, axis=-1, keepdims=True)
    rstd = lax.rsqrt(jnp.mean(xc * xc, axis=-1, keepdims=True) + EPS)
    vhat = xc * rstd
    return vhat * ln_g + ln_b, vhat, rstd, dgelu


def _gmlp_specs():
    half = lambda j: pl.BlockSpec((GM_TM, GM_WIDTH), lambda i: (i, j))
    vec = pl.BlockSpec((1, GM_WIDTH), lambda i: (0, 0))
    w = pl.BlockSpec((GM_GROUPS, GM_CHUNK, GM_CHUNK), lambda i: (0, 0, 0))
    bt = pl.BlockSpec((GM_CHUNK, GM_GROUPS), lambda i: (0, 0))
    return half, vec, w, bt


def _gmlp_fwd(z, ln_g, ln_b, w_s, b_st):
    def body(zu_ref, zv_ref, g_ref, b_ref, w_ref, bt_ref, o_ref):
        u, _ = _gelu_parts(zu_ref[...])
        v, _, _, _ = _gmlp_norm(zv_ref[...], g_ref[...], b_ref[...])
        v16 = v.astype(BF16)
        mask = _tril(GM_CHUNK)
        bt = bt_ref[...]
        for g in range(GM_GROUPS):
            wm16 = jnp.where(mask, w_ref[g], 0.0).astype(BF16)
            cols = slice(g * GM_GDIM, (g + 1) * GM_GDIM)
            for c in range(GM_TM // GM_CHUNK):
                rows = slice(c * GM_CHUNK, (c + 1) * GM_CHUNK)
                mixed = jnp.dot(wm16, v16[rows, cols], preferred_element_type=F32) + bt[:, g:g + 1]
                o_ref[rows, cols] = (u[rows, cols] * mixed).astype(BF16)

    half, vec, w, bt = _gmlp_specs()
    return pl.pallas_call(
        body,
        name="gmlp_fwd",
        grid=(N_TOK // GM_TM,),
        in_specs=[half(0), half(1), vec, vec, w, bt],
        out_specs=half(0),
        out_shape=jax.ShapeDtypeStruct((N_TOK, GM_WIDTH), BF16),
        compiler_params=_cp("parallel"),
    )(z, z, ln_g, ln_b, w_s, b_st)


def _gmlp_bwd(z, dcat, ln_g, ln_b, w_s, b_st):
    def body(zu_ref, zv_ref, dout_ref, g_ref, b_ref, w_ref, bt_ref,
             dzu_ref, dzv_ref, dw_ref, dbt_ref, dg_ref, db_ref, dv_ref):
        @pl.when(pl.program_id(0) == 0)
        def _():
            dw_ref[...] = jnp.zeros_like(dw_ref)
            dbt_ref[...] = jnp.zeros_like(dbt_ref)
            dg_ref[...] = jnp.zeros_like(dg_ref)
            db_ref[...] = jnp.zeros_like(db_ref)

        zu = zu_ref[...]
        u, du_dz = _gelu_parts(zu)
        ln_g = g_ref[...]
        v, vhat, rstd, dgv_dz = _gmlp_norm(zv_ref[...], ln_g, b_ref[...])
        v16 = v.astype(BF16)
        dout = dout_ref[...]
        dmixed = dout * u
        dm16 = dmixed.astype(BF16)
        mask = _tril(GM_CHUNK)
        bt = bt_ref[...]
        group_id = lax.broadcasted_iota(jnp.int32, (1, GM_GROUPS), 1)
        dbt = jnp.zeros((GM_CHUNK, GM_GROUPS), F32)
        for g in range(GM_GROUPS):
            wm16 = jnp.where(mask, w_ref[g], 0.0).astype(BF16)
            cols = slice(g * GM_GDIM, (g + 1) * GM_GDIM)
            dw = jnp.zeros((GM_CHUNK, GM_CHUNK), F32)
            dbt_g = jnp.zeros((GM_CHUNK, 1), F32)
            for c in range(GM_TM // GM_CHUNK):
                rows = slice(c * GM_CHUNK, (c + 1) * GM_CHUNK)
                mixed = jnp.dot(wm16, v16[rows, cols], preferred_element_type=F32) + bt[:, g:g + 1]
                dzu_ref[rows, cols] = (dout[rows, cols] * mixed * du_dz[rows, cols]).astype(BF16)
                dw += lax.dot_general(dm16[rows, cols], v16[rows, cols], _NT, preferred_element_type=F32)
                dbt_g += jnp.sum(dmixed[rows, cols], axis=-1, keepdims=True)
                dv_ref[rows, cols] = lax.dot_general(wm16, dm16[rows, cols], _TN, preferred_element_type=F32)
            dw_ref[g] += jnp.where(mask, dw, 0.0)
            dbt = dbt + dbt_g * (group_id == g).astype(F32)
        dbt_ref[...] += dbt
        dv = dv_ref[...]
        dg_ref[...] += jnp.sum(dv * vhat, axis=0, keepdims=True)
        db_ref[...] += jnp.sum(dv, axis=0, keepdims=True)
        dvh = dv * ln_g
        dgv = rstd * (dvh - jnp.mean(dvh, axis=-1, keepdims=True) - vhat * jnp.mean(dvh * vhat, axis=-1, keepdims=True))
        dzv_ref[...] = (dgv * dgv_dz).astype(BF16)

    half, vec, w, bt = _gmlp_specs()
    tok_shape = jax.ShapeDtypeStruct((N_TOK, GM_WIDTH), BF16)
    return pl.pallas_call(
        body,
        name="gmlp_bwd",
        grid=(N_TOK // GM_TM,),
        in_specs=[half(0), half(1), half(0), vec, vec, w, bt],
        out_specs=[half(0), half(0), w, bt, vec, vec],
        out_shape=[tok_shape, tok_shape, jax.ShapeDtypeStruct((GM_GROUPS, GM_CHUNK, GM_CHUNK), F32),
                   jax.ShapeDtypeStruct((GM_CHUNK, GM_GROUPS), F32),
                   jax.ShapeDtypeStruct((1, GM_WIDTH), F32), jax.ShapeDtypeStruct((1, GM_WIDTH), F32)],
        scratch_shapes=[pltpu.VMEM((GM_TM, GM_WIDTH), F32)],
        compiler_params=_cp("arbitrary"),
    )(z, z, dcat, ln_g, ln_b, w_s, b_st)


def _to_rows_bf16(w, cuts_columns, *, name):
    layers, r, c = w.shape

    def body(w_ref, o_ref):
        wv = w_ref[...]
        o_ref[...] = (wv.T if cuts_columns else wv).astype(BF16)

    out = (layers, c, r) if cuts_columns else (layers, r, c)
    return pl.pallas_call(
        body,
        name=name,
        grid=(layers,),
        in_specs=[pl.BlockSpec((None, r, c), lambda l: (l, 0, 0))],
        out_specs=pl.BlockSpec((None,) + out[1:], lambda l: (l, 0, 0)),
        out_shape=jax.ShapeDtypeStruct(out, BF16),
        compiler_params=_cp("parallel"),
    )(w)


def _mesh_pos():
    x, y, c = (lax.axis_index(a) for a in MESH_AXES)
    return x, y, c, 4 * x + 2 * y + c


def _peer(x, y, c, r):
    px = 1 - x if r & 4 else x
    py = 1 - y if r & 2 else y
    pc = 1 - c if r & 1 else c
    return (px, py, pc), 4 * px + 2 * py + pc


def _exchange(srcs, dsts, send_sems, recv_sems, local_sems, send_view, land_view):
    x, y, c, me = _mesh_pos()
    copies = []
    for k, (src, dst) in enumerate(zip(srcs, dsts)):
        own = pltpu.make_async_copy(send_view(k, src, me), land_view(k, dst, me), local_sems.at[k])
        own.start()
        copies.append(own)
    sends, recvs = [], []
    for r in range(1, N_DEV):
        peer, peer_blk = _peer(x, y, c, r)
        for k, (src, dst) in enumerate(zip(srcs, dsts)):
            idx = k * (N_DEV - 1) + r - 1
            send = pltpu.make_async_remote_copy(
                src_ref=send_view(k, src, peer_blk), dst_ref=land_view(k, dst, me),
                send_sem=send_sems.at[idx], recv_sem=recv_sems.at[idx],
                device_id=peer, device_id_type=pl.DeviceIdType.MESH)
            send.start()
            sends.append(send)
            recvs.append(pltpu.make_async_remote_copy(
                src_ref=send_view(k, src, me), dst_ref=land_view(k, dst, peer_blk),
                send_sem=send_sems.at[idx], recv_sem=recv_sems.at[idx],
                device_id=peer, device_id_type=pl.DeviceIdType.MESH))
    for cp in recvs:
        cp.wait_recv()
    for cp in sends:
        cp.wait_send()
    for cp in copies:
        cp.wait()


def _all_gather(shards, ln_slab):
    n_in = len(shards) + 1
    n_out = len(PIECES) + 1

    def body(*refs):
        ins, outs = refs[:n_in], refs[n_in:n_in + n_out]
        send_sems, recv_sems, local_sems = refs[n_in + n_out:]
        srcs = [ins[g].at[l] for g, l in PIECES] + [ins[-1]]
        _exchange(srcs, outs, send_sems, recv_sems, local_sems,
                  send_view=lambda k, src, blk: src, land_view=lambda k, dst, blk: dst.at[blk])

    out_shape = [jax.ShapeDtypeStruct((N_DEV, GROUPS[g][3], D_MODEL), BF16) for g, _ in PIECES]
    out_shape.append(jax.ShapeDtypeStruct((N_DEV,) + ln_slab.shape, F32))
    any_spec = pl.BlockSpec(memory_space=pl.ANY)
    n_cp = n_out * (N_DEV - 1)
    return pl.pallas_call(
        body,
        name="all_gather_weights",
        in_specs=[any_spec] * n_in,
        out_specs=[any_spec] * n_out,
        out_shape=out_shape,
        scratch_shapes=[pltpu.SemaphoreType.DMA((n_cp,)), pltpu.SemaphoreType.DMA((n_cp,)),
                        pltpu.SemaphoreType.DMA((n_out,))],
    )(*shards, ln_slab)


def _reduce_scatter(grads, small):
    n_in = len(PIECES) + 1
    n_out = len(GROUPS) + 1

    def body(*refs):
        ins, outs = refs[:n_in], refs[n_in:n_in + n_out]
        send_sems, recv_sems, local_sems = refs[n_in + n_out:]
        dsts = [outs[g].at[l] for g, l in PIECES] + [outs[-1]]
        last = len(PIECES)
        _exchange(ins, dsts, send_sems, recv_sems, local_sems,
                  send_view=lambda k, src, blk: src if k == last else src.at[blk],
                  land_view=lambda k, dst, blk: dst.at[blk])

    out_shape = [jax.ShapeDtypeStruct((layers, N_DEV, n, D_MODEL), BF16) for _, _, layers, n in GROUPS]
    out_shape.append(jax.ShapeDtypeStruct((N_DEV,) + small.shape, F32))
    any_spec = pl.BlockSpec(memory_space=pl.ANY)
    n_cp = n_in * (N_DEV - 1)
    return pl.pallas_call(
        body,
        name="reduce_scatter_grads",
        in_specs=[any_spec] * n_in,
        out_specs=[any_spec] * n_out,
        out_shape=out_shape,
        scratch_shapes=[pltpu.SemaphoreType.DMA((n_cp,)), pltpu.SemaphoreType.DMA((n_cp,)),
                        pltpu.SemaphoreType.DMA((n_in,))],
    )(*grads, small)


def _adamw(w, g, m, v):
    m = ADAM_B1 * m + (1.0 - ADAM_B1) * g
    v = ADAM_B2 * v + (1.0 - ADAM_B2) * (g * g)
    m_hat = m / (1.0 - ADAM_B1 ** ADAM_STEP)
    v_hat = v / (1.0 - ADAM_B2 ** ADAM_STEP)
    return -ADAM_LR * (m_hat / (jnp.sqrt(v_hat) + ADAM_EPS) + ADAM_WD * w), m, v


ADAM_TC = 256


def _adam_big(slots, w, m, v, cuts_columns, *, name):
    layers, _, n, _ = slots.shape

    def body(s_ref, w_ref, m_ref, v_ref, g_ref, d_ref, nm_ref, nv_ref):
        g = s_ref[0].astype(F32)
        for s in range(1, N_DEV):
            g = g + s_ref[s].astype(F32)
        if cuts_columns:
            g = g.T
        g_ref[...] = g
        d_ref[...], nm_ref[...], nv_ref[...] = _adamw(w_ref[...], g, m_ref[...], v_ref[...])

    s_spec = pl.BlockSpec((None, N_DEV, n, ADAM_TC), lambda l, j: (l, 0, 0, j))
    if cuts_columns:
        w_spec = pl.BlockSpec((None, ADAM_TC, n), lambda l, j: (l, j, 0))
    else:
        w_spec = pl.BlockSpec((None, n, ADAM_TC), lambda l, j: (l, 0, j))
    return pl.pallas_call(
        body,
        name=name,
        grid=(layers, D_MODEL // ADAM_TC),
        in_specs=[s_spec, w_spec, w_spec, w_spec],
        out_specs=[w_spec] * 4,
        out_shape=[jax.ShapeDtypeStruct(w.shape, F32)] * 4,
        compiler_params=_cp("parallel", "parallel"),
    )(slots, w, m, v)


def _sum_slots(slots):
    rows = slots.shape[1]

    def body(s_ref, o_ref):
        g = s_ref[0]
        for s in range(1, N_DEV):
            g = g + s_ref[s]
        o_ref[...] = g

    return pl.pallas_call(
        body,
        name="sum_small_grads",
        out_shape=jax.ShapeDtypeStruct((rows, D_MODEL), F32),
        compiler_params=pltpu.CompilerParams(vmem_limit_bytes=VMEM_LIMIT_BYTES),
    )(slots)


def _adam_small(g, w, m, v, *, name):
    def body(g_ref, w_ref, m_ref, v_ref, d_ref, nm_ref, nv_ref):
        d_ref[...], nm_ref[...], nv_ref[...] = _adamw(w_ref[...], g_ref[...], m_ref[...], v_ref[...])

    return pl.pallas_call(
        body,
        name=name,
        out_shape=[jax.ShapeDtypeStruct(w.shape, F32)] * 3,
        compiler_params=pltpu.CompilerParams(vmem_limit_bytes=VMEM_LIMIT_BYTES),
    )(g, w, m, v)


SMALL_LAYOUT = (
    ("mem_norm", 1), ("lb_logits", 3), ("ffn1_norm", 2), ("mix_norm", 2), ("hgrn_gnorm", 1),
    ("gmlp_ln_g", 2), ("gmlp_ln_b", 2), ("gmlp_w_s", 128), ("gmlp_b_s", 1), ("ffn2_norm", 2), ("final_norm", 1),
)
SMALL_SHARDED = ("gmlp_ln_g", "gmlp_ln_b")


def _pack_small(parts):
    rows = []
    for name, n_rows in SMALL_LAYOUT:
        flat = parts[name].astype(F32).reshape(-1)
        flat = jnp.pad(flat, (0, n_rows * D_MODEL - flat.shape[0]))
        rows.append(flat.reshape(n_rows, D_MODEL))
    used = sum(n for _, n in SMALL_LAYOUT)
    rows.append(jnp.zeros((SMALL_ROWS - used, D_MODEL), F32))
    return jnp.concatenate(rows, axis=0)


def _unpack_small(slab, shapes):
    out, at = {}, 0
    for name, n_rows in SMALL_LAYOUT:
        size = math.prod(shapes[name])
        out[name] = slab[at:at + n_rows].reshape(-1)[:size].reshape(shapes[name])
        at += n_rows
    return out


def _ffn_fwd(x, norm_g, w_in_t, w_out, tag):
    h = _rms_fwd(x, norm_g, name=f"{tag}_norm")
    z = _mm(h, w_in_t, tb=True, tm=1024, tn=512, tk=D_MODEL, out_dtype=F32, name=f"{tag}_in")
    act = _swiglu_fwd(z, name=f"{tag}_act")
    y = _mm(act, w_out, tm=512, tn=D_MODEL, tk=D_FF, out_dtype=F32, res=x, scale=0.5, name=f"{tag}_out")
    return y, (x, h, z, act)


def _ffn_bwd(dy, saved, norm_g, w_in_t, w_out, tag):
    x, h, z, act = saved
    dw_out = _mm(act, dy, ta=True, tm=1408, tn=D_MODEL, tk=1024, out_dtype=BF16, scale=0.5, name=f"{tag}_out_wgrad")
    dact = _mm(dy, w_out, tb=True, tm=1024, tn=1408, tk=D_MODEL, out_dtype=F32, name=f"{tag}_out_dgrad")
    dz = _swiglu_bwd(z, dact, scale=0.5, name=f"{tag}_act_bwd")
    dw_in_t = _mm(dz, h, ta=True, tm=512, tn=D_MODEL, tk=1024, out_dtype=BF16, name=f"{tag}_in_wgrad")
    dh = _mm(dz, w_in_t, tm=1024, tn=D_MODEL, tk=512, out_dtype=F32, name=f"{tag}_in_dgrad")
    dx, dg = _rms_bwd(x, norm_g, dh, dy, name=f"{tag}_norm_bwd")
    return dx, dg, dw_in_t, dw_out


def kernel(x, mem, mem_norm, lb_logits, ffn1_norm, ffn1_w_in, ffn1_w_out, mix_norm, mem_w_kv, hgrn_w_in, hgrn_gnorm, hgrn_w_out, gmlp_w_in, gmlp_ln_g, gmlp_ln_b, gmlp_w_s, gmlp_b_s, gmlp_w_out, ffn2_norm, ffn2_w_in, ffn2_w_out, final_norm, loss_target, m_mem_norm, m_lb_logits, m_ffn1_norm, m_ffn1_w_in, m_ffn1_w_out, m_mix_norm, m_mem_w_kv, m_hgrn_w_in, m_hgrn_gnorm, m_hgrn_w_out, m_gmlp_w_in, m_gmlp_ln_g, m_gmlp_ln_b, m_gmlp_w_s, m_gmlp_b_s, m_gmlp_w_out, m_ffn2_norm, m_ffn2_w_in, m_ffn2_w_out, m_final_norm, v_mem_norm, v_lb_logits, v_ffn1_norm, v_ffn1_w_in, v_ffn1_w_out, v_mix_norm, v_mem_w_kv, v_hgrn_w_in, v_hgrn_gnorm, v_hgrn_w_out, v_gmlp_w_in, v_gmlp_ln_g, v_gmlp_ln_b, v_gmlp_w_s, v_gmlp_b_s, v_gmlp_w_out, v_ffn2_norm, v_ffn2_w_in, v_ffn2_w_out, v_final_norm):
    weights = dict(mem_norm=mem_norm, lb_logits=lb_logits, ffn1_norm=ffn1_norm, ffn1_w_in=ffn1_w_in, ffn1_w_out=ffn1_w_out, mix_norm=mix_norm, mem_w_kv=mem_w_kv, hgrn_w_in=hgrn_w_in, hgrn_gnorm=hgrn_gnorm, hgrn_w_out=hgrn_w_out, gmlp_w_in=gmlp_w_in, gmlp_ln_g=gmlp_ln_g, gmlp_ln_b=gmlp_ln_b, gmlp_w_s=gmlp_w_s, gmlp_b_s=gmlp_b_s, gmlp_w_out=gmlp_w_out, ffn2_norm=ffn2_norm, ffn2_w_in=ffn2_w_in, ffn2_w_out=ffn2_w_out, final_norm=final_norm)
    mom_m = dict(mem_norm=m_mem_norm, lb_logits=m_lb_logits, ffn1_norm=m_ffn1_norm, ffn1_w_in=m_ffn1_w_in, ffn1_w_out=m_ffn1_w_out, mix_norm=m_mix_norm, mem_w_kv=m_mem_w_kv, hgrn_w_in=m_hgrn_w_in, hgrn_gnorm=m_hgrn_gnorm, hgrn_w_out=m_hgrn_w_out, gmlp_w_in=m_gmlp_w_in, gmlp_ln_g=m_gmlp_ln_g, gmlp_ln_b=m_gmlp_ln_b, gmlp_w_s=m_gmlp_w_s, gmlp_b_s=m_gmlp_b_s, gmlp_w_out=m_gmlp_w_out, ffn2_norm=m_ffn2_norm, ffn2_w_in=m_ffn2_w_in, ffn2_w_out=m_ffn2_w_out, final_norm=m_final_norm)
    mom_v = dict(mem_norm=v_mem_norm, lb_logits=v_lb_logits, ffn1_norm=v_ffn1_norm, ffn1_w_in=v_ffn1_w_in, ffn1_w_out=v_ffn1_w_out, mix_norm=v_mix_norm, mem_w_kv=v_mem_w_kv, hgrn_w_in=v_hgrn_w_in, hgrn_gnorm=v_hgrn_gnorm, hgrn_w_out=v_hgrn_w_out, gmlp_w_in=v_gmlp_w_in, gmlp_ln_g=v_gmlp_ln_g, gmlp_ln_b=v_gmlp_ln_b, gmlp_w_s=v_gmlp_w_s, gmlp_b_s=v_gmlp_b_s, gmlp_w_out=v_gmlp_w_out, ffn2_norm=v_ffn2_norm, ffn2_w_in=v_ffn2_w_in, ffn2_w_out=v_ffn2_w_out, final_norm=v_final_norm)
    order = list(weights)
    _, _, _, me = _mesh_pos()

    shards = [_to_rows_bf16(weights[name], cuts, name=f"{name}_rows") for name, cuts, _, _ in GROUPS]
    ln_slab = jnp.concatenate([gmlp_ln_g, gmlp_ln_b, jnp.zeros((6, GM_WIDTH // N_DEV), F32)], axis=0)
    *gathered, ln_all = _all_gather(shards, ln_slab)
    full = {}
    for (g, l), arr in zip(PIECES, gathered):
        full[(GROUPS[g][0], l)] = arr.reshape(N_DEV * GROUPS[g][3], D_MODEL)
    ln_g_full = ln_all[:, 0, :].reshape(1, GM_WIDTH)
    ln_b_full = ln_all[:, 1, :].reshape(1, GM_WIDTH)
    dx, grads_t, small, loss_part = _step_local(
        x, mem, loss_target, full, ln_g_full, ln_b_full, mem_norm, lb_logits, ffn1_norm, mix_norm, hgrn_gnorm,
        gmlp_w_s, gmlp_b_s, ffn2_norm, final_norm)

    send = [grads_t[(GROUPS[g][0], l)].reshape(N_DEV, GROUPS[g][3], D_MODEL) for g, l in PIECES]
    *slots, small_slots = _reduce_scatter(send, _pack_small(small))

    grad, delta, new_m, new_v = {}, {}, {}, {}
    for (name, cuts, _, _), s in zip(GROUPS, slots):
        grad[name], delta[name], new_m[name], new_v[name] = _adam_big(
            s, weights[name], mom_m[name], mom_v[name], cuts, name=f"{name}_adamw")
    small_shapes = {name: weights[name].shape for name, _ in SMALL_LAYOUT}
    for name in SMALL_SHARDED:
        small_shapes[name] = (1, GM_WIDTH)
    g_slab = _sum_slots(small_slots)
    g_small = _unpack_small(g_slab, small_shapes)
    replicated = {name: jnp.zeros(small_shapes[name], F32) for name in SMALL_SHARDED}
    w_slab = _pack_small({**{n: weights[n] for n, _ in SMALL_LAYOUT}, **replicated})
    m_slab = _pack_small({**{n: mom_m[n] for n, _ in SMALL_LAYOUT}, **replicated})
    v_slab = _pack_small({**{n: mom_v[n] for n, _ in SMALL_LAYOUT}, **replicated})
    d_slab, nm_slab, nv_slab = _adam_small(g_slab, w_slab, m_slab, v_slab, name="small_adamw")
    d_small, nm_small, nv_small = (_unpack_small(s, small_shapes) for s in (d_slab, nm_slab, nv_slab))
    for name, _ in SMALL_LAYOUT:
        if name not in SMALL_SHARDED:
            grad[name], delta[name], new_m[name], new_v[name] = g_small[name], d_small[name], nm_small[name], nv_small[name]
    blk = GM_WIDTH // N_DEV
    g_ln = jnp.concatenate([lax.dynamic_slice(g_small[n], (0, me * blk), (1, blk)) for n in SMALL_SHARDED], axis=0)
    pad = jnp.zeros((6, blk), F32)
    ln_pack = lambda src: jnp.concatenate([src[n] for n in SMALL_SHARDED] + [pad], axis=0)
    d_ln, nm_ln, nv_ln = _adam_small(jnp.concatenate([g_ln, pad], axis=0), ln_pack(weights), ln_pack(mom_m),
                                     ln_pack(mom_v), name="ln_adamw")
    for j, name in enumerate(SMALL_SHARDED):
        grad[name], delta[name], new_m[name], new_v[name] = g_ln[j:j + 1], d_ln[j:j + 1], nm_ln[j:j + 1], nv_ln[j:j + 1]

    loss = lax.psum(loss_part[0, 0], MESH_AXES)
    grad_x = dx.reshape(B_LOC, SEQ, D_MODEL)
    return (loss, grad_x, *[grad[n] for n in order], *[delta[n] for n in order],
            *[new_m[n] for n in order], *[new_v[n] for n in order])


def _step_local(x, mem, loss_target, full, ln_g_full, ln_b_full, mem_norm, lb_logits, ffn1_norm, mix_norm, hgrn_gnorm,
                gmlp_w_s, gmlp_b_s, ffn2_norm, final_norm):
    w_s = gmlp_w_s[0]
    b_st = gmlp_b_s[0].T

    xs = x.reshape(N_TOK, D_MODEL)
    mem2d = mem.reshape(B_LOC * MEM_LEN, D_MODEL)
    mem_g = mem_norm.reshape(1, D_MODEL)
    memn = _rms_fwd(mem2d, mem_g, name="mem_norm_fwd")
    saved = []
    for i in range(2):
        xs, s_ffn1 = _ffn_fwd(xs, ffn1_norm[i:i + 1], full[("ffn1_w_in", i)], full[("ffn1_w_out", i)], f"l{i}_ffn1")
        hm = _rms_fwd(xs, mix_norm[i:i + 1], name=f"l{i}_mix_norm")
        kv = _mm(memn, full[("mem_w_kv", i)], tb=True, tm=512, tn=512, tk=D_MODEL, out_dtype=F32, name=f"l{i}_mem_kv")
        w_in_t = full[("hgrn_w_in", 0)] if i == 0 else full[("gmlp_w_in", 0)]
        w_out = full[("hgrn_w_out", 0)] if i == 0 else full[("gmlp_w_out", 0)]
        zm = _mm(hm, w_in_t, tb=True, tm=1024, tn=512, tk=D_MODEL, out_dtype=F32, name=f"l{i}_mix_in")
        if i == 0:
            o_mix, o_pre, s_all = _hgrn_fwd(zm, lb_logits, hgrn_gnorm)
            mix_saved = (o_pre, s_all)
        else:
            o_mix = _gmlp_fwd(zm, ln_g_full, ln_b_full, w_s, b_st)
            mix_saved = ()
        o_mem = _attn_fwd(zm, kv, name=f"l{i}_attn")
        cat = jnp.concatenate([o_mix, o_mem], axis=1)
        x_mix = xs
        xs = _mm(cat, w_out, tm=512, tn=D_MODEL, tk=cat.shape[1], out_dtype=F32, res=xs, name=f"l{i}_mix_out")
        xs, s_ffn2 = _ffn_fwd(xs, ffn2_norm[i:i + 1], full[("ffn2_w_in", i)], full[("ffn2_w_out", i)], f"l{i}_ffn2")
        saved.append((s_ffn1, (x_mix, hm, kv, zm, cat, mix_saved), s_ffn2))

    dx, d_final, loss_part = _loss_head(xs, final_norm.reshape(1, D_MODEL), loss_target.reshape(N_TOK, D_MODEL))

    grads_t = {}
    small = {"final_norm": d_final}
    d_ffn1, d_ffn2, d_mix = [None, None], [None, None], [None, None]
    dmemn = jnp.zeros((B_LOC * MEM_LEN, D_MODEL), F32)
    for i in (1, 0):
        s_ffn1, (x_mix, hm, kv, zm, cat, mix_saved), s_ffn2 = saved[i]
        dx, d_ffn2[i], grads_t[("ffn2_w_in", i)], grads_t[("ffn2_w_out", i)] = _ffn_bwd(
            dx, s_ffn2, ffn2_norm[i:i + 1], full[("ffn2_w_in", i)], full[("ffn2_w_out", i)], f"l{i}_ffn2")
        mixer = "hgrn" if i == 0 else "gmlp"
        w_in_t, w_out = full[(f"{mixer}_w_in", 0)], full[(f"{mixer}_w_out", 0)]
        width = cat.shape[1]
        grads_t[(f"{mixer}_w_out", 0)] = _mm(cat, dx, ta=True, tm=width // 2, tn=D_MODEL, tk=1024, out_dtype=BF16,
                                             name=f"l{i}_mix_out_wgrad")
        dcat = _mm(dx, w_out, tb=True, tm=1024, tn=width // 2, tk=D_MODEL, out_dtype=F32, name=f"l{i}_mix_out_dgrad")
        dq, dk, dv = _attn_bwd(zm, kv, dcat, do_off=width - XA_HEADS * XA_DIM, name=f"l{i}_attn_bwd")
        if i == 0:
            dzq, dzf, dzi, dzg, small["lb_logits"], small["hgrn_gnorm"] = _hgrn_bwd(
                zm, mix_saved[0], dcat, mix_saved[1], lb_logits, hgrn_gnorm)
            dzm = jnp.concatenate([dzq, dzf, dzi, dzg, dq], axis=1)
        else:
            dzu, dzv, small["gmlp_w_s"], dbt, small["gmlp_ln_g"], small["gmlp_ln_b"] = _gmlp_bwd(
                zm, dcat, ln_g_full, ln_b_full, w_s, b_st)
            small["gmlp_b_s"] = dbt.T
            dzm = jnp.concatenate([dzu, dzv, dq], axis=1)
        grads_t[(f"{mixer}_w_in", 0)] = _mm(dzm, hm, ta=True, tm=512, tn=D_MODEL, tk=1024, out_dtype=BF16,
                                            name=f"l{i}_mix_in_wgrad")
        dh = _mm(dzm, w_in_t, tm=1024, tn=D_MODEL, tk=512, out_dtype=F32, name=f"l{i}_mix_in_dgrad")
        dx, d_mix[i] = _rms_bwd(x_mix, mix_norm[i:i + 1], dh, dx, name=f"l{i}_mix_norm_bwd")
        dkv = jnp.concatenate([dk, dv], axis=1)
        grads_t[("mem_w_kv", i)] = _mm(dkv, memn, ta=True, tm=512, tn=D_MODEL, tk=B_LOC * MEM_LEN, out_dtype=BF16,
                                       name=f"l{i}_mem_kv_wgrad")
        dmemn = _mm(dkv, full[("mem_w_kv", i)], tm=B_LOC * MEM_LEN, tn=D_MODEL, tk=512, out_dtype=F32, res=dmemn,
                    name=f"l{i}_mem_kv_dgrad")
        dx, d_ffn1[i], grads_t[("ffn1_w_in", i)], grads_t[("ffn1_w_out", i)] = _ffn_bwd(
            dx, s_ffn1, ffn1_norm[i:i + 1], full[("ffn1_w_in", i)], full[("ffn1_w_out", i)], f"l{i}_ffn1")
    _, small["mem_norm"] = _rms_bwd(mem2d, mem_g, dmemn, dmemn, name="mem_norm_bwd")
    small["ffn1_norm"] = jnp.concatenate(d_ffn1, axis=0)
    small["ffn2_norm"] = jnp.concatenate(d_ffn2, axis=0)
    small["mix_norm"] = jnp.concatenate(d_mix, axis=0)
    return dx, grads_t, small, loss_part
```

```python
import functools
import math

import jax
import jax.numpy as jnp
from jax import lax
from jax.experimental import pallas as pl
from jax.experimental.pallas import tpu as pltpu

F32 = jnp.float32
BF16 = jnp.bfloat16

D_MODEL = 1024
SEQ = 2048
B_LOC = 2
N_TOK = B_LOC * SEQ
MEM_LEN = 256
N_DEV = 8
EPS = 1e-6
D_FF = 2816
HG_HEADS = 8
HG_DIM = 128
HG_CHUNK = 64
HG_NCHUNK = SEQ // HG_CHUNK
GM_CHUNK = 128
GM_GROUPS = 8
GM_WIDTH = 2048
GM_GDIM = GM_WIDTH // GM_GROUPS
XA_HEADS = 4
XA_DIM = 256
XA_OFF = 4096

ADAM_LR = 0.001
ADAM_B1 = 0.9
ADAM_B2 = 0.999
ADAM_EPS = 1e-08
ADAM_WD = 0.01
ADAM_STEP = 10

VMEM_LIMIT_BYTES = 56 * 1024 * 1024
MESH_AXES = ("x", "y", "c")

GROUPS = (
    ("ffn1_w_in", True, 2, 704),
    ("ffn1_w_out", False, 2, 352),
    ("mem_w_kv", True, 2, 256),
    ("hgrn_w_in", True, 1, 640),
    ("hgrn_w_out", False, 1, 256),
    ("gmlp_w_in", True, 1, 640),
    ("gmlp_w_out", False, 1, 384),
    ("ffn2_w_in", True, 2, 704),
    ("ffn2_w_out", False, 2, 352),
)
GROUP_LAYERS = {name: layers for name, _, layers, _ in GROUPS}


def _stage_pieces(layer, block):
    if block == "mix":
        mixer = "hgrn" if layer == 0 else "gmlp"
        return (("mem_w_kv", layer), (f"{mixer}_w_in", 0), (f"{mixer}_w_out", 0))
    return ((f"{block}_w_in", layer), (f"{block}_w_out", layer))


ANY_SPEC = pl.BlockSpec(memory_space=pl.ANY)
HBM_SPEC = pl.BlockSpec(memory_space=pltpu.HBM)
SEM_SPEC = pl.BlockSpec(memory_space=pltpu.SEMAPHORE)


def _cp(*sem):
    return pltpu.CompilerParams(dimension_semantics=sem, vmem_limit_bytes=VMEM_LIMIT_BYTES)


def _sigmoid(x):
    return 1.0 / (1.0 + jnp.exp(-x))


def _gelu_parts(x):
    cdf = 0.5 * (1.0 + lax.erf(x * (1.0 / math.sqrt(2.0))))
    pdf = jnp.exp(-0.5 * x * x) * (1.0 / math.sqrt(2.0 * math.pi))
    return x * cdf, cdf + x * pdf


def _mm(a, b, *, ta=False, tb=False, tm, tn, tk, out_dtype, res=None, scale=1.0, deps=(), name):
    m, k = (a.shape[1], a.shape[0]) if ta else a.shape
    n, kb = b.shape if tb else (b.shape[1], b.shape[0])
    assert k == kb and m % tm == 0 and n % tn == 0 and k % tk == 0, (name, a.shape, b.shape)
    nk = k // tk
    dn = (((0 if ta else 1,), (1 if tb else 0,)), ((), ()))
    n_in = 2 + (res is not None) + len(deps)

    def body(*refs):
        a_ref, b_ref = refs[:2]
        r_ref = refs[2] if res is not None else None
        o_ref, scr = refs[n_in], refs[n_in + 1:]
        p = lax.dot_general(a_ref[...].astype(BF16), b_ref[...].astype(BF16), dn, preferred_element_type=F32)

        def finish(acc):
            if scale != 1.0:
                acc = scale * acc
            if r_ref is not None:
                acc = r_ref[...] + acc
            o_ref[...] = acc.astype(out_dtype)

        if nk == 1:
            finish(p)
        else:
            acc_ref = scr[0]
            kk = pl.program_id(2)

            @pl.when(kk == 0)
            def _():
                acc_ref[...] = p

            @pl.when(kk > 0)
            def _():
                acc_ref[...] += p

            @pl.when(kk == nk - 1)
            def _():
                finish(acc_ref[...])

    a_spec = pl.BlockSpec((tk, tm), lambda i, j, kk: (kk, i)) if ta else pl.BlockSpec((tm, tk), lambda i, j, kk: (i, kk))
    b_spec = pl.BlockSpec((tn, tk), lambda i, j, kk: (j, kk)) if tb else pl.BlockSpec((tk, tn), lambda i, j, kk: (kk, j))
    o_spec = pl.BlockSpec((tm, tn), lambda i, j, kk: (i, j))
    in_specs = [a_spec, b_spec] + ([o_spec] if res is not None else []) + [ANY_SPEC] * len(deps)
    args = (a, b) + ((res,) if res is not None else ()) + tuple(deps)
    return pl.pallas_call(
        body,
        name=name,
        grid=(m // tm, n // tn, nk),
        in_specs=in_specs,
        out_specs=o_spec,
        out_shape=jax.ShapeDtypeStruct((m, n), out_dtype),
        scratch_shapes=[pltpu.VMEM((tm, tn), F32)] if nk > 1 else [],
        compiler_params=_cp("parallel", "parallel", "arbitrary"),
    )(*args)


def _rms_fwd(x, g, *, name, deps=(), tm=512):
    rows = x.shape[0]

    def body(x_ref, g_ref, *rest):
        o_ref = rest[len(deps)]
        xv = x_ref[...]
        r = lax.rsqrt(jnp.mean(xv * xv, axis=-1, keepdims=True) + EPS)
        o_ref[...] = (xv * r * g_ref[...]).astype(BF16)

    row = pl.BlockSpec((tm, D_MODEL), lambda i: (i, 0))
    return pl.pallas_call(
        body,
        name=name,
        grid=(rows // tm,),
        in_specs=[row, pl.BlockSpec((1, D_MODEL), lambda i: (0, 0))] + [ANY_SPEC] * len(deps),
        out_specs=row,
        out_shape=jax.ShapeDtypeStruct((rows, D_MODEL), BF16),
        compiler_params=_cp("parallel"),
    )(x, g, *deps)


def _rms_bwd(x, g, dh, dres, *, name, deps=(), tm=512):
    rows = x.shape[0]

    def body(x_ref, g_ref, dh_ref, dres_ref, *rest):
        dx_ref, dg_ref = rest[len(deps):]
        xv = x_ref[...]
        r = lax.rsqrt(jnp.mean(xv * xv, axis=-1, keepdims=True) + EPS)
        xhat = xv * r
        dhv = dh_ref[...]
        part = jnp.sum(dhv * xhat, axis=0, keepdims=True)

        @pl.when(pl.program_id(0) == 0)
        def _():
            dg_ref[...] = part

        @pl.when(pl.program_id(0) > 0)
        def _():
            dg_ref[...] += part

        dxh = dhv * g_ref[...]
        dx_ref[...] = dres_ref[...] + r * (dxh - xhat * jnp.mean(dxh * xhat, axis=-1, keepdims=True))

    row = pl.BlockSpec((tm, D_MODEL), lambda i: (i, 0))
    vec = pl.BlockSpec((1, D_MODEL), lambda i: (0, 0))
    return pl.pallas_call(
        body,
        name=name,
        grid=(rows // tm,),
        in_specs=[row, vec, row, row] + [ANY_SPEC] * len(deps),
        out_specs=[row, vec],
        out_shape=[jax.ShapeDtypeStruct((rows, D_MODEL), F32), jax.ShapeDtypeStruct((1, D_MODEL), F32)],
        compiler_params=_cp("arbitrary"),
    )(x, g, dh, dres, *deps)


def _swiglu_fwd(z, *, name, tm=512):
    def body(g_ref, u_ref, o_ref):
        gv = g_ref[...]
        o_ref[...] = (gv * _sigmoid(gv) * u_ref[...]).astype(BF16)

    return pl.pallas_call(
        body,
        name=name,
        grid=(N_TOK // tm,),
        in_specs=[pl.BlockSpec((tm, D_FF), lambda i: (i, 0)), pl.BlockSpec((tm, D_FF), lambda i: (i, 1))],
        out_specs=pl.BlockSpec((tm, D_FF), lambda i: (i, 0)),
        out_shape=jax.ShapeDtypeStruct((N_TOK, D_FF), BF16),
        compiler_params=_cp("parallel"),
    )(z, z)


def _swiglu_bwd(z, dact, *, scale, name, tm=512):
    def body(g_ref, u_ref, da_ref, o_ref):
        gv = g_ref[...]
        s = _sigmoid(gv)
        da = da_ref[...] * scale
        o_ref[:, :D_FF] = (da * u_ref[...] * (s * (1.0 + gv * (1.0 - s)))).astype(BF16)
        o_ref[:, D_FF:] = (da * (gv * s)).astype(BF16)

    half = lambda j: pl.BlockSpec((tm, D_FF), lambda i: (i, j))
    return pl.pallas_call(
        body,
        name=name,
        grid=(N_TOK // tm,),
        in_specs=[half(0), half(1), half(0)],
        out_specs=pl.BlockSpec((tm, 2 * D_FF), lambda i: (i, 0)),
        out_shape=jax.ShapeDtypeStruct((N_TOK, 2 * D_FF), BF16),
        compiler_params=_cp("parallel"),
    )(z, z, dact)


def _loss_head(x, g, target, *, tm=512):
    def body(x_ref, g_ref, t_ref, dx_ref, dg_ref, loss_ref):
        xv = x_ref[...]
        gv = g_ref[...]
        r = lax.rsqrt(jnp.mean(xv * xv, axis=-1, keepdims=True) + EPS)
        xhat = xv * r
        err = xhat * gv - t_ref[...]
        loss_part = jnp.zeros((1, 128), F32) + 0.5 * jnp.sum(jnp.mean(err * err, axis=-1, keepdims=True))
        dy = err * (1.0 / D_MODEL)
        dg_part = jnp.sum(dy * xhat, axis=0, keepdims=True)

        @pl.when(pl.program_id(0) == 0)
        def _():
            dg_ref[...] = dg_part
            loss_ref[...] = loss_part

        @pl.when(pl.program_id(0) > 0)
        def _():
            dg_ref[...] += dg_part
            loss_ref[...] += loss_part

        dxh = dy * gv
        dx_ref[...] = r * (dxh - xhat * jnp.mean(dxh * xhat, axis=-1, keepdims=True))

    row = pl.BlockSpec((tm, D_MODEL), lambda i: (i, 0))
    vec = pl.BlockSpec((1, D_MODEL), lambda i: (0, 0))
    return pl.pallas_call(
        body,
        name="loss_head",
        grid=(N_TOK // tm,),
        in_specs=[row, vec, row],
        out_specs=[row, vec, pl.BlockSpec((1, 128), lambda i: (0, 0))],
        out_shape=[
            jax.ShapeDtypeStruct((N_TOK, D_MODEL), F32),
            jax.ShapeDtypeStruct((1, D_MODEL), F32),
            jax.ShapeDtypeStruct((1, 128), F32),
        ],
        compiler_params=_cp("arbitrary"),
    )(x, g, target)


_NT = (((1,), (1,)), ((), ()))
_TN = (((0,), (0,)), ((), ()))
XA_TQ = 1024
XA_SCALE = XA_DIM ** -0.5


def _attn_probs(q16, k16):
    s = lax.dot_general(q16, k16, _NT, preferred_element_type=F32) * XA_SCALE
    e = jnp.exp(s - jnp.max(s, axis=-1, keepdims=True))
    return e / jnp.sum(e, axis=-1, keepdims=True)


def _attn_fwd(z, kv, *, name):
    nt = SEQ // XA_TQ

    def body(q_ref, k_ref, v_ref, o_ref):
        p = _attn_probs(q_ref[...].astype(BF16), k_ref[...].astype(BF16))
        o_ref[...] = jnp.dot(p.astype(BF16), v_ref[...].astype(BF16), preferred_element_type=F32).astype(BF16)

    return pl.pallas_call(
        body,
        name=name,
        grid=(B_LOC, XA_HEADS, nt),
        in_specs=[
            pl.BlockSpec((XA_TQ, XA_DIM), lambda b, h, t: (b * nt + t, XA_OFF // XA_DIM + h)),
            pl.BlockSpec((MEM_LEN, XA_DIM), lambda b, h, t: (b, h)),
            pl.BlockSpec((MEM_LEN, XA_DIM), lambda b, h, t: (b, XA_HEADS + h)),
        ],
        out_specs=pl.BlockSpec((XA_TQ, XA_DIM), lambda b, h, t: (b * nt + t, h)),
        out_shape=jax.ShapeDtypeStruct((N_TOK, XA_HEADS * XA_DIM), BF16),
        compiler_params=_cp("parallel", "parallel", "arbitrary"),
    )(z, kv, kv)


def _attn_bwd(z, kv, dcat, *, do_off, name):
    nt = SEQ // XA_TQ

    def body(q_ref, k_ref, v_ref, do_ref, dq_ref, dk_ref, dv_ref):
        q16 = q_ref[...].astype(BF16)
        k16 = k_ref[...].astype(BF16)
        v16 = v_ref[...].astype(BF16)
        do16 = do_ref[...].astype(BF16)
        p = _attn_probs(q16, k16)
        dv_part = lax.dot_general(p.astype(BF16), do16, _TN, preferred_element_type=F32)
        dp = lax.dot_general(do16, v16, _NT, preferred_element_type=F32)
        ds16 = (p * (dp - jnp.sum(dp * p, axis=-1, keepdims=True)) * XA_SCALE).astype(BF16)
        dq_ref[...] = jnp.dot(ds16, k16, preferred_element_type=F32).astype(BF16)
        dk_part = lax.dot_general(ds16, q16, _TN, preferred_element_type=F32)

        @pl.when(pl.program_id(2) == 0)
        def _():
            dk_ref[...] = dk_part
            dv_ref[...] = dv_part

        @pl.when(pl.program_id(2) > 0)
        def _():
            dk_ref[...] += dk_part
            dv_ref[...] += dv_part

    qspec = pl.BlockSpec((XA_TQ, XA_DIM), lambda b, h, t: (b * nt + t, XA_OFF // XA_DIM + h))
    kspec = lambda off: pl.BlockSpec((MEM_LEN, XA_DIM), lambda b, h, t: (b, off + h))
    return pl.pallas_call(
        body,
        name=name,
        grid=(B_LOC, XA_HEADS, nt),
        in_specs=[qspec, kspec(0), kspec(XA_HEADS),
                  pl.BlockSpec((XA_TQ, XA_DIM), lambda b, h, t: (b * nt + t, do_off // XA_DIM + h))],
        out_specs=[pl.BlockSpec((XA_TQ, XA_DIM), lambda b, h, t: (b * nt + t, h)), kspec(0), kspec(0)],
        out_shape=[
            jax.ShapeDtypeStruct((N_TOK, XA_HEADS * XA_DIM), BF16),
            jax.ShapeDtypeStruct((B_LOC * MEM_LEN, XA_HEADS * XA_DIM), F32),
            jax.ShapeDtypeStruct((B_LOC * MEM_LEN, XA_HEADS * XA_DIM), F32),
        ],
        compiler_params=_cp("parallel", "parallel", "arbitrary"),
    )(z, kv, kv, dcat)


def _tril(n):
    return lax.broadcasted_iota(jnp.int32, (n, n), 0) >= lax.broadcasted_iota(jnp.int32, (n, n), 1)


def _lower_bound(lbl):
    e = jnp.exp(lbl - jnp.max(lbl, axis=0, keepdims=True))
    p = e / jnp.sum(e, axis=0, keepdims=True)
    return p[0:1, :], p


def _hgrn_gates(zq, zf, lb, tril_f):
    sig = _sigmoid(zf)
    f = lb + (1.0 - lb) * sig
    kk = 1.0 - f
    sq = _sigmoid(zq)
    q = zq * sq
    b = jnp.dot(tril_f, jnp.log(f), preferred_element_type=F32, precision=lax.Precision.HIGHEST)
    bl = b[HG_CHUNK - 1:HG_CHUNK, :]
    return q, sq, sig, f, kk, b, bl


def _hgrn_zspec(section):
    return pl.BlockSpec((SEQ, HG_DIM), lambda h, b: (b, section * HG_HEADS + h))


def _hgrn_fwd(z, lb_logits, gnorm):
    def body(zq_ref, zf_ref, zi_ref, zg_ref, lbl_ref, gn_ref, o_ref, opre_ref, sall_ref, st_ref):
        lb, _ = _lower_bound(lbl_ref[...])
        gn = gn_ref[...]
        mask = _tril(HG_CHUNK)
        tril_f = mask.astype(F32)
        st_ref[...] = jnp.zeros_like(st_ref)

        def chunk(c, carry):
            rows = pl.ds(pl.multiple_of(c * HG_CHUNK, HG_CHUNK), HG_CHUNK)
            q, _, _, _, kk, b, bl = _hgrn_gates(zq_ref[rows, :], zf_ref[rows, :], lb, tril_f)
            v16 = zi_ref[rows, :].astype(BF16)
            qd16 = (q * jnp.exp(b)).astype(BF16)
            ki16 = (kk * jnp.exp(-b)).astype(BF16)
            kd16 = (kk * jnp.exp(bl - b)).astype(BF16)
            a = jnp.where(mask, lax.dot_general(qd16, ki16, _NT, preferred_element_type=F32), 0.0)
            st = st_ref[...]
            sall_ref[0, 0, c] = st
            o = jnp.dot(a.astype(BF16), v16, preferred_element_type=F32) + lax.dot_general(
                qd16, st.astype(BF16), _NT, preferred_element_type=F32)
            st_ref[...] = st * jnp.exp(bl) + lax.dot_general(v16, kd16, _TN, preferred_element_type=F32)
            opre_ref[rows, :] = o
            r = lax.rsqrt(jnp.mean(o * o, axis=-1, keepdims=True) + EPS)
            zg = zg_ref[rows, :]
            o_ref[rows, :] = ((o * r * gn) * (zg * _sigmoid(zg))).astype(BF16)
            return carry

        lax.fori_loop(0, HG_NCHUNK, chunk, 0)

    tok = pl.BlockSpec((SEQ, HG_DIM), lambda h, b: (b, h))
    return pl.pallas_call(
        body,
        name="hgrn_fwd",
        grid=(HG_HEADS, B_LOC),
        in_specs=[_hgrn_zspec(0), _hgrn_zspec(1), _hgrn_zspec(2), _hgrn_zspec(3),
                  pl.BlockSpec((3, HG_DIM), lambda h, b: (0, h)), pl.BlockSpec((1, HG_DIM), lambda h, b: (0, 0))],
        out_specs=[tok, tok, pl.BlockSpec((1, 1, HG_NCHUNK, HG_DIM, HG_DIM), lambda h, b: (b, h, 0, 0, 0))],
        out_shape=[
            jax.ShapeDtypeStruct((N_TOK, HG_HEADS * HG_DIM), BF16),
            jax.ShapeDtypeStruct((N_TOK, HG_HEADS * HG_DIM), F32),
            jax.ShapeDtypeStruct((B_LOC, HG_HEADS, HG_NCHUNK, HG_DIM, HG_DIM), F32),
        ],
        scratch_shapes=[pltpu.VMEM((HG_DIM, HG_DIM), F32)],
        compiler_params=_cp("parallel", "arbitrary"),
    )(z, z, z, z, lb_logits, gnorm)


def _hgrn_bwd(z, opre, dcat, sall, lb_logits, gnorm):
    def body(zq_ref, zf_ref, zi_ref, zg_ref, opre_ref, dout_ref, sall_ref, lbl_ref, gn_ref,
             dzq_ref, dzf_ref, dzi_ref, dzg_ref, dlbl_ref, dgn_ref, dst_ref, dlb_ref):
        h_id, b_id = pl.program_id(0), pl.program_id(1)
        lb, p = _lower_bound(lbl_ref[...])
        gn = gn_ref[...]
        mask = _tril(HG_CHUNK)
        tril_f = mask.astype(F32)
        dst_ref[...] = jnp.zeros_like(dst_ref)
        dlb_ref[...] = jnp.zeros_like(dlb_ref)

        @pl.when((h_id == 0) & (b_id == 0))
        def _():
            dgn_ref[...] = jnp.zeros_like(dgn_ref)

        def chunk(i, carry):
            c = HG_NCHUNK - 1 - i
            rows = pl.ds(pl.multiple_of(c * HG_CHUNK, HG_CHUNK), HG_CHUNK)
            zq, zg = zq_ref[rows, :], zg_ref[rows, :]
            q, sq, sig, f, kk, b, bl = _hgrn_gates(zq, zf_ref[rows, :], lb, tril_f)
            v16 = zi_ref[rows, :].astype(BF16)
            eb, enb, ebl_b, ebl = jnp.exp(b), jnp.exp(-b), jnp.exp(bl - b), jnp.exp(bl)
            qd, ki, kd = q * eb, kk * enb, kk * ebl_b
            qd16, ki16, kd16 = qd.astype(BF16), ki.astype(BF16), kd.astype(BF16)
            o = opre_ref[rows, :]
            dout = dout_ref[rows, :]
            r = lax.rsqrt(jnp.mean(o * o, axis=-1, keepdims=True) + EPS)
            ohat = o * r
            sg = _sigmoid(zg)
            d_on = dout * (zg * sg)
            dzg_ref[rows, :] = (dout * (ohat * gn) * (sg * (1.0 + zg * (1.0 - sg)))).astype(BF16)
            dgn_ref[...] += jnp.sum(d_on * ohat, axis=0, keepdims=True)
            dohat = d_on * gn
            do16 = (r * (dohat - ohat * jnp.mean(dohat * ohat, axis=-1, keepdims=True))).astype(BF16)
            st = sall_ref[0, 0, c]
            st16 = st.astype(BF16)
            dst = dst_ref[...]
            dst16 = dst.astype(BF16)
            a16 = jnp.where(mask, lax.dot_general(qd16, ki16, _NT, preferred_element_type=F32), 0.0).astype(BF16)
            da16 = jnp.where(mask, lax.dot_general(do16, v16, _NT, preferred_element_type=F32), 0.0).astype(BF16)
            dv = lax.dot_general(a16, do16, _TN, preferred_element_type=F32) + lax.dot_general(
                kd16, dst16, _NT, preferred_element_type=F32)
            dqd = jnp.dot(da16, ki16, preferred_element_type=F32) + jnp.dot(do16, st16, preferred_element_type=F32)
            dki = lax.dot_general(da16, qd16, _TN, preferred_element_type=F32)
            dkd = jnp.dot(v16, dst16, preferred_element_type=F32)
            dbl = jnp.sum(dkd * kd, axis=0, keepdims=True) + ebl * jnp.sum(st * dst, axis=0, keepdims=True)
            dst_ref[...] = dst * ebl + lax.dot_general(do16, qd16, _TN, preferred_element_type=F32)
            dzi_ref[rows, :] = dv.astype(BF16)
            dzq_ref[rows, :] = (dqd * eb * (sq * (1.0 + zq * (1.0 - sq)))).astype(BF16)
            dkk = dki * enb + dkd * ebl_b
            db = dqd * qd - dki * ki - dkd * kd
            dlogf = lax.dot_general(tril_f, db, _TN, preferred_element_type=F32, precision=lax.Precision.HIGHEST) + dbl
            df = dlogf / f - dkk
            dzf_ref[rows, :] = (df * (1.0 - lb) * sig * (1.0 - sig)).astype(BF16)
            dlb_ref[...] += jnp.sum(df * (1.0 - sig), axis=0, keepdims=True)
            return carry

        lax.fori_loop(0, HG_NCHUNK, chunk, 0)
        row0 = (lax.broadcasted_iota(jnp.int32, (3, HG_DIM), 0) == 0).astype(F32)
        dlbl_part = dlb_ref[...] * lb * (row0 - p)

        @pl.when(b_id == 0)
        def _():
            dlbl_ref[...] = dlbl_part

        @pl.when(b_id > 0)
        def _():
            dlbl_ref[...] += dlbl_part

    tok = pl.BlockSpec((SEQ, HG_DIM), lambda h, b: (b, h))
    tok_shape = jax.ShapeDtypeStruct((N_TOK, HG_HEADS * HG_DIM), BF16)
    return pl.pallas_call(
        body,
        name="hgrn_bwd",
        grid=(HG_HEADS, B_LOC),
        in_specs=[_hgrn_zspec(0), _hgrn_zspec(1), _hgrn_zspec(2), _hgrn_zspec(3), tok, tok,
                  pl.BlockSpec((1, 1, HG_NCHUNK, HG_DIM, HG_DIM), lambda h, b: (b, h, 0, 0, 0)),
                  pl.BlockSpec((3, HG_DIM), lambda h, b: (0, h)), pl.BlockSpec((1, HG_DIM), lambda h, b: (0, 0))],
        out_specs=[tok, tok, tok, tok, pl.BlockSpec((3, HG_DIM), lambda h, b: (0, h)),
                   pl.BlockSpec((1, HG_DIM), lambda h, b: (0, 0))],
        out_shape=[tok_shape, tok_shape, tok_shape, tok_shape,
                   jax.ShapeDtypeStruct((3, HG_HEADS * HG_DIM), F32), jax.ShapeDtypeStruct((1, HG_DIM), F32)],
        scratch_shapes=[pltpu.VMEM((HG_DIM, HG_DIM), F32), pltpu.VMEM((1, HG_DIM), F32)],
        compiler_params=_cp("arbitrary", "arbitrary"),
    )(z, z, z, z, opre, dcat, sall, lb_logits, gnorm)


GM_TM = 256


def _gmlp_norm(zv, ln_g, ln_b):
    gv, dgelu = _gelu_parts(zv)
    xc = gv - jnp.mean(gv, axis=-1, keepdims=True)
    rstd = lax.rsqrt(jnp.mean(xc * xc, axis=-1, keepdims=True) + EPS)
    vhat = xc * rstd
    return vhat * ln_g + ln_b, vhat, rstd, dgelu


def _gmlp_specs():
    half = lambda j: pl.BlockSpec((GM_TM, GM_WIDTH), lambda i: (i, j))
    vec = pl.BlockSpec((1, GM_WIDTH), lambda i: (0, 0))
    w = pl.BlockSpec((GM_GROUPS, GM_CHUNK, GM_CHUNK), lambda i: (0, 0, 0))
    bt = pl.BlockSpec((GM_CHUNK, GM_GROUPS), lambda i: (0, 0))
    return half, vec, w, bt


def _gmlp_fwd(z, ln_g, ln_b, w_s, b_st):
    def body(zu_ref, zv_ref, g_ref, b_ref, w_ref, bt_ref, o_ref):
        u, _ = _gelu_parts(zu_ref[...])
        v, _, _, _ = _gmlp_norm(zv_ref[...], g_ref[...], b_ref[...])
        v16 = v.astype(BF16)
        mask = _tril(GM_CHUNK)
        bt = bt_ref[...]
        for g in range(GM_GROUPS):
            wm16 = jnp.where(mask, w_ref[g], 0.0).astype(BF16)
            cols = slice(g * GM_GDIM, (g + 1) * GM_GDIM)
            for c in range(GM_TM // GM_CHUNK):
                rows = slice(c * GM_CHUNK, (c + 1) * GM_CHUNK)
                mixed = jnp.dot(wm16, v16[rows, cols], preferred_element_type=F32) + bt[:, g:g + 1]
                o_ref[rows, cols] = (u[rows, cols] * mixed).astype(BF16)

    half, vec, w, bt = _gmlp_specs()
    return pl.pallas_call(
        body,
        name="gmlp_fwd",
        grid=(N_TOK // GM_TM,),
        in_specs=[half(0), half(1), vec, vec, w, bt],
        out_specs=half(0),
        out_shape=jax.ShapeDtypeStruct((N_TOK, GM_WIDTH), BF16),
        compiler_params=_cp("parallel"),
    )(z, z, ln_g, ln_b, w_s, b_st)


def _gmlp_bwd(z, dcat, ln_g, ln_b, w_s, b_st):
    def body(zu_ref, zv_ref, dout_ref, g_ref, b_ref, w_ref, bt_ref,
             dzu_ref, dzv_ref, dw_ref, dbt_ref, dg_ref, db_ref, dv_ref):
        @pl.when(pl.program_id(0) == 0)
        def _():
            dw_ref[...] = jnp.zeros_like(dw_ref)
            dbt_ref[...] = jnp.zeros_like(dbt_ref)
            dg_ref[...] = jnp.zeros_like(dg_ref)
            db_ref[...] = jnp.zeros_like(db_ref)

        zu = zu_ref[...]
        u, du_dz = _gelu_parts(zu)
        ln_g = g_ref[...]
        v, vhat, rstd, dgv_dz = _gmlp_norm(zv_ref[...], ln_g, b_ref[...])
        v16 = v.astype(BF16)
        dout = dout_ref[...]
        dmixed = dout * u
        dm16 = dmixed.astype(BF16)
        mask = _tril(GM_CHUNK)
        bt = bt_ref[...]
        group_id = lax.broadcasted_iota(jnp.int32, (1, GM_GROUPS), 1)
        dbt = jnp.zeros((GM_CHUNK, GM_GROUPS), F32)
        for g in range(GM_GROUPS):
            wm16 = jnp.where(mask, w_ref[g], 0.0).astype(BF16)
            cols = slice(g * GM_GDIM, (g + 1) * GM_GDIM)
            dw = jnp.zeros((GM_CHUNK, GM_CHUNK), F32)
            dbt_g = jnp.zeros((GM_CHUNK, 1), F32)
            for c in range(GM_TM // GM_CHUNK):
                rows = slice(c * GM_CHUNK, (c + 1) * GM_CHUNK)
                mixed = jnp.dot(wm16, v16[rows, cols], preferred_element_type=F32) + bt[:, g:g + 1]
                dzu_ref[rows, cols] = (dout[rows, cols] * mixed * du_dz[rows, cols]).astype(BF16)
                dw += lax.dot_general(dm16[rows, cols], v16[rows, cols], _NT, preferred_element_type=F32)
                dbt_g += jnp.sum(dmixed[rows, cols], axis=-1, keepdims=True)
                dv_ref[rows, cols] = lax.dot_general(wm16, dm16[rows, cols], _TN, preferred_element_type=F32)
            dw_ref[g] += jnp.where(mask, dw, 0.0)
            dbt = dbt + dbt_g * (group_id == g).astype(F32)
        dbt_ref[...] += dbt
        dv = dv_ref[...]
        dg_ref[...] += jnp.sum(dv * vhat, axis=0, keepdims=True)
        db_ref[...] += jnp.sum(dv, axis=0, keepdims=True)
        dvh = dv * ln_g
        dgv = rstd * (dvh - jnp.mean(dvh, axis=-1, keepdims=True) - vhat * jnp.mean(dvh * vhat, axis=-1, keepdims=True))
        dzv_ref[...] = (dgv * dgv_dz).astype(BF16)

    half, vec, w, bt = _gmlp_specs()
    tok_shape = jax.ShapeDtypeStruct((N_TOK, GM_WIDTH), BF16)
    return pl.pallas_call(
        body,
        name="gmlp_bwd",
        grid=(N_TOK // GM_TM,),
        in_specs=[half(0), half(1), half(0), vec, vec, w, bt],
        out_specs=[half(0), half(0), w, bt, vec, vec],
        out_shape=[tok_shape, tok_shape, jax.ShapeDtypeStruct((GM_GROUPS, GM_CHUNK, GM_CHUNK), F32),
                   jax.ShapeDtypeStruct((GM_CHUNK, GM_GROUPS), F32),
                   jax.ShapeDtypeStruct((1, GM_WIDTH), F32), jax.ShapeDtypeStruct((1, GM_WIDTH), F32)],
        scratch_shapes=[pltpu.VMEM((GM_TM, GM_WIDTH), F32)],
        compiler_params=_cp("arbitrary"),
    )(z, z, dcat, ln_g, ln_b, w_s, b_st)


def _own_slot(shape):
    return pl.BlockSpec((None,) + tuple(shape), lambda i, me_ref: (me_ref[0],) + (0,) * len(shape))


def _place_rows(w, layer, cuts_columns, me, *, name):
    _, r, c = w.shape
    n = c if cuts_columns else r

    def body(me_ref, w_ref, o_ref):
        wv = w_ref[...]
        o_ref[...] = (wv.T if cuts_columns else wv).astype(BF16)

    return pl.pallas_call(
        body,
        name=name,
        grid_spec=pltpu.PrefetchScalarGridSpec(
            num_scalar_prefetch=1, grid=(1,),
            in_specs=[pl.BlockSpec((None, r, c), lambda i, me_ref: (layer, 0, 0))],
            out_specs=_own_slot((n, D_MODEL))),
        out_shape=jax.ShapeDtypeStruct((N_DEV, n, D_MODEL), BF16),
        compiler_params=_cp("arbitrary"),
    )(me, w)


def _place_ln(ln_g, ln_b, me):
    blk = ln_g.shape[1]

    def body(me_ref, g_ref, b_ref, o_ref):
        o_ref[...] = jnp.zeros_like(o_ref)
        o_ref[0:1, :] = g_ref[...]
        o_ref[1:2, :] = b_ref[...]

    vec = pl.BlockSpec((1, blk), lambda i, me_ref: (0, 0))
    return pl.pallas_call(
        body,
        name="place_ln",
        grid_spec=pltpu.PrefetchScalarGridSpec(
            num_scalar_prefetch=1, grid=(1,), in_specs=[vec, vec], out_specs=_own_slot((8, blk))),
        out_shape=jax.ShapeDtypeStruct((N_DEV, 8, blk), F32),
        compiler_params=_cp("arbitrary"),
    )(me, ln_g, ln_b)


def _place_own(grads, me, *, name):
    k = len(grads)

    def body(me_ref, *refs):
        for src, dst in zip(refs[:k], refs[k:]):
            dst[...] = src[...]

    specs = [_own_slot(g.shape[1:]) for g in grads]
    return pl.pallas_call(
        body,
        name=name,
        grid_spec=pltpu.PrefetchScalarGridSpec(num_scalar_prefetch=1, grid=(1,), in_specs=specs, out_specs=specs),
        out_shape=[jax.ShapeDtypeStruct(g.shape, g.dtype) for g in grads],
        compiler_params=_cp("arbitrary"),
    )(me, *grads)


def _mesh_pos():
    x, y, c = (lax.axis_index(a) for a in MESH_AXES)
    return x, y, c, 4 * x + 2 * y + c


def _peer(x, y, c, r):
    px = 1 - x if r & 4 else x
    py = 1 - y if r & 2 else y
    pc = 1 - c if r & 1 else c
    return (px, py, pc), 4 * px + 2 * py + pc


def _peer_copies(srcs, lands, send_sems, recv_sems, gather):
    x, y, c, me = _mesh_pos()
    pairs = []
    for r in range(1, N_DEV):
        peer, peer_blk = _peer(x, y, c, r)
        for k, (src, land) in enumerate(zip(srcs, lands)):
            idx = k * (N_DEV - 1) + r - 1
            sems = dict(send_sem=send_sems.at[idx], recv_sem=recv_sems.at[idx], device_id=peer,
                        device_id_type=pl.DeviceIdType.MESH)
            mine = pltpu.make_async_remote_copy(
                src_ref=src.at[me if gather else peer_blk], dst_ref=land.at[me], **sems)
            theirs = pltpu.make_async_remote_copy(src_ref=src.at[me], dst_ref=land.at[peer_blk], **sems)
            pairs.append((mine, theirs))
    return pairs


DATAFLOW = pltpu.SideEffectType.DATAFLOW_SIDE_EFFECTING


def _in_hbm(a):
    return pltpu.with_memory_space_constraint(a, pltpu.HBM)


def _copies_start(srcs, lands, *, gather, name):
    arrs = list(lands) if gather else list(srcs) + list(lands)
    n, k = len(arrs), len(lands)

    def body(*refs):
        ins, send_sems, recv_sems, token = refs[:n], refs[n], refs[n + 1], refs[2 * n + 2]
        src_refs, land_refs = (ins, ins) if gather else (ins[:k], ins[k:])
        for mine, _ in _peer_copies(src_refs, land_refs, send_sems, recv_sems, gather):
            mine.start()
        token[...] = jnp.zeros_like(token)

    n_cp = k * (N_DEV - 1)
    return pl.pallas_call(
        body,
        name=name,
        in_specs=[HBM_SPEC] * n,
        out_specs=(SEM_SPEC, SEM_SPEC, *[HBM_SPEC] * n, pl.BlockSpec(memory_space=pltpu.VMEM)),
        out_shape=(pltpu.SemaphoreType.DMA((n_cp,)), pltpu.SemaphoreType.DMA((n_cp,)),
                   *[pltpu.HBM(a.shape, a.dtype) for a in arrs], jax.ShapeDtypeStruct((8, 128), F32)),
        input_output_aliases={i: 2 + i for i in range(n)},
        compiler_params=pltpu.CompilerParams(has_side_effects=DATAFLOW),
    )(*[_in_hbm(a) for a in arrs])


def _copies_wait(arrs, send_sems, recv_sems, after, *, n_lands, gather, name):
    n, k = len(arrs), n_lands

    def body(*refs):
        ins, send_sems, recv_sems = refs[:n], refs[n], refs[n + 1]
        src_refs, land_refs = (ins, ins) if gather else (ins[:k], ins[k:])
        for mine, theirs in _peer_copies(src_refs, land_refs, send_sems, recv_sems, gather):
            mine.wait_send()
            theirs.wait_recv()

    outs = pl.pallas_call(
        body,
        name=name,
        in_specs=[HBM_SPEC] * n + [SEM_SPEC, SEM_SPEC] + [ANY_SPEC] * len(after),
        out_specs=[HBM_SPEC] * n,
        out_shape=[pltpu.HBM(a.shape, a.dtype) for a in arrs],
        input_output_aliases={i: i for i in range(n)},
        compiler_params=pltpu.CompilerParams(has_side_effects=DATAFLOW),
    )(*arrs, send_sems, recv_sems, *after)
    return outs[n - k:]


def _exchange_small(slabs):
    n = len(slabs)

    def body(*refs):
        ins, outs = refs[:n], refs[n:2 * n]
        send_sems, recv_sems, local_sems = refs[2 * n:]
        x, y, c, me = _mesh_pos()
        own = [pltpu.make_async_copy(src, dst.at[me], local_sems.at[k]) for k, (src, dst) in enumerate(zip(ins, outs))]
        for cp in own:
            cp.start()
        sends, recvs = [], []
        for r in range(1, N_DEV):
            peer, peer_blk = _peer(x, y, c, r)
            for k, (src, dst) in enumerate(zip(ins, outs)):
                idx = k * (N_DEV - 1) + r - 1
                sems = dict(send_sem=send_sems.at[idx], recv_sem=recv_sems.at[idx], device_id=peer,
                            device_id_type=pl.DeviceIdType.MESH)
                send = pltpu.make_async_remote_copy(src_ref=src, dst_ref=dst.at[me], **sems)
                send.start()
                sends.append(send)
                recvs.append(pltpu.make_async_remote_copy(src_ref=src, dst_ref=dst.at[peer_blk], **sems))
        for cp in recvs:
            cp.wait_recv()
        for cp in sends:
            cp.wait_send()
        for cp in own:
            cp.wait()

    n_cp = n * (N_DEV - 1)
    return pl.pallas_call(
        body,
        name="exchange_small_grads",
        in_specs=[ANY_SPEC] * n,
        out_specs=[ANY_SPEC] * n,
        out_shape=[jax.ShapeDtypeStruct((N_DEV,) + s.shape, F32) for s in slabs],
        scratch_shapes=[pltpu.SemaphoreType.DMA((n_cp,)), pltpu.SemaphoreType.DMA((n_cp,)),
                        pltpu.SemaphoreType.DMA((n,))],
    )(*slabs)


def _adamw(w, g, m, v):
    m = ADAM_B1 * m + (1.0 - ADAM_B1) * g
    v = ADAM_B2 * v + (1.0 - ADAM_B2) * (g * g)
    m_hat = m / (1.0 - ADAM_B1 ** ADAM_STEP)
    v_hat = v / (1.0 - ADAM_B2 ** ADAM_STEP)
    return -ADAM_LR * (m_hat / (jnp.sqrt(v_hat) + ADAM_EPS) + ADAM_WD * w), m, v


ADAM_TC = 256


def _adam_big(slots, w, m, v, cuts_columns, *, name):
    layers, n, nj = len(slots), slots[0].shape[1], D_MODEL // ADAM_TC

    def body(*refs):
        s_refs = refs[:layers]
        w_ref, m_ref, v_ref, g_ref, d_ref, nm_ref, nv_ref, acc_ref = refs[layers:]
        for ll in range(layers):
            @pl.when(pl.program_id(0) == ll)
            def _(s_ref=s_refs[ll]):
                g = s_ref[0].astype(F32)
                for s in range(1, N_DEV):
                    g = g + s_ref[s].astype(F32)
                acc_ref[...] = g

        g = acc_ref[...].T if cuts_columns else acc_ref[...]
        g_ref[...] = g
        d_ref[...], nm_ref[...], nv_ref[...] = _adamw(w_ref[...], g, m_ref[...], v_ref[...])

    def slot_spec(ll):
        return pl.BlockSpec((N_DEV, n, ADAM_TC),
                            lambda l, j: (0, 0, jnp.where(l < ll, 0, jnp.where(l > ll, nj - 1, j))))

    if cuts_columns:
        w_spec = pl.BlockSpec((None, ADAM_TC, n), lambda l, j: (l, j, 0))
    else:
        w_spec = pl.BlockSpec((None, n, ADAM_TC), lambda l, j: (l, 0, j))
    return pl.pallas_call(
        body,
        name=name,
        grid=(layers, nj),
        in_specs=[slot_spec(ll) for ll in range(layers)] + [w_spec] * 3,
        out_specs=[w_spec] * 4,
        out_shape=[jax.ShapeDtypeStruct(w.shape, F32)] * 4,
        scratch_shapes=[pltpu.VMEM((n, ADAM_TC), F32)],
        compiler_params=_cp("arbitrary", "arbitrary"),
    )(*slots, w, m, v)


def _adam_slabs(slots, ws, ms, vs):
    n = len(slots)

    def body(*refs):
        ins, outs = refs[:4 * n], refs[4 * n:]
        for k in range(n):
            s_ref, w_ref, m_ref, v_ref = ins[k], ins[n + k], ins[2 * n + k], ins[3 * n + k]
            g = s_ref[0]
            for s in range(1, N_DEV):
                g = g + s_ref[s]
            outs[4 * k][...] = g
            outs[4 * k + 1][...], outs[4 * k + 2][...], outs[4 * k + 3][...] = _adamw(w_ref[...], g, m_ref[...], v_ref[...])

    res = pl.pallas_call(
        body,
        name="small_adamw",
        out_shape=[jax.ShapeDtypeStruct(w.shape, F32) for w in ws for _ in range(4)],
        compiler_params=pltpu.CompilerParams(vmem_limit_bytes=VMEM_LIMIT_BYTES),
    )(*slots, *ws, *ms, *vs)
    return [res[4 * k:4 * k + 4] for k in range(n)]


def _adam_vecs(gs, ws, ms, vs):
    n = len(gs)

    def body(*refs):
        ins, outs = refs[:4 * n], refs[4 * n:]
        for k in range(n):
            outs[3 * k][...], outs[3 * k + 1][...], outs[3 * k + 2][...] = _adamw(
                ins[n + k][...], ins[k][...], ins[2 * n + k][...], ins[3 * n + k][...])

    res = pl.pallas_call(
        body,
        name="ln_adamw",
        out_shape=[jax.ShapeDtypeStruct(w.shape, F32) for w in ws for _ in range(3)],
        compiler_params=pltpu.CompilerParams(vmem_limit_bytes=VMEM_LIMIT_BYTES),
    )(*gs, *ws, *ms, *vs)
    return [res[3 * k:3 * k + 3] for k in range(n)]


SLAB_AT = dict(mem_norm=0, lb_logits=8, ffn1_norm=16, mix_norm=24, hgrn_gnorm=32, gmlp_ln_g=40, gmlp_ln_b=48,
               gmlp_b_s=56, ffn2_norm=64, final_norm=72)
SLAB_ROWS = 80
SMALL_SHARDED = ("gmlp_ln_g", "gmlp_ln_b")


def _pack_slab(parts, *, name):
    flat, plan = [], []
    for pname, at in SLAB_AT.items():
        for a in parts.get(pname, ()):
            flat.append(a)
            plan.append((at, a.shape))
            at += a.shape[0] * max(1, a.shape[1] // D_MODEL)

    def body(*refs):
        o_ref = refs[-1]
        o_ref[...] = jnp.zeros_like(o_ref)
        for ref, (at, (r, w)) in zip(refs, plan):
            if w <= D_MODEL:
                o_ref[at:at + r, 0:w] = ref[...]
            else:
                for j in range(w // D_MODEL):
                    o_ref[at + j:at + j + 1, :] = ref[:, j * D_MODEL:(j + 1) * D_MODEL]

    return pl.pallas_call(
        body,
        name=name,
        out_shape=jax.ShapeDtypeStruct((SLAB_ROWS, D_MODEL), F32),
        compiler_params=pltpu.CompilerParams(vmem_limit_bytes=VMEM_LIMIT_BYTES),
    )(*flat)


def _unpack_slab(slab, shapes):
    out = {}
    for pname, at in SLAB_AT.items():
        if pname in SMALL_SHARDED:
            continue
        shape = shapes[pname]
        rows, width = (1, shape[0]) if len(shape) == 1 else (math.prod(shape[:-1]), shape[-1])
        out[pname] = slab[at:at + rows, :width].reshape(shape)
    return out


def _ffn_fwd(x, norm_g, w_in_t, w_out, tag):
    h = _rms_fwd(x, norm_g, name=f"{tag}_norm")
    z = _mm(h, w_in_t, tb=True, tm=1024, tn=512, tk=D_MODEL, out_dtype=F32, name=f"{tag}_in")
    act = _swiglu_fwd(z, name=f"{tag}_act")
    y = _mm(act, w_out, tm=512, tn=D_MODEL, tk=D_FF, out_dtype=F32, res=x, scale=0.5, name=f"{tag}_out")
    return y, (x, h, z, act)


def _ffn_bwd(dy, saved, norm_g, w_in_t, w_out, tag, deps=()):
    x, h, z, act = saved
    dw_out = _mm(act, dy, ta=True, tm=1408, tn=D_MODEL, tk=1024, out_dtype=BF16, scale=0.5, deps=deps,
                 name=f"{tag}_out_wgrad")
    dact = _mm(dy, w_out, tb=True, tm=1024, tn=1408, tk=D_MODEL, out_dtype=F32, name=f"{tag}_out_dgrad")
    dz = _swiglu_bwd(z, dact, scale=0.5, name=f"{tag}_act_bwd")
    dw_in_t = _mm(dz, h, ta=True, tm=512, tn=D_MODEL, tk=1024, out_dtype=BF16, name=f"{tag}_in_wgrad")
    dh = _mm(dz, w_in_t, tm=1024, tn=D_MODEL, tk=512, out_dtype=F32, name=f"{tag}_in_dgrad")
    dx, dg = _rms_bwd(x, norm_g, dh, dy, name=f"{tag}_norm_bwd")
    return dx, dg, dw_in_t, dw_out


def kernel(x, mem, mem_norm, lb_logits, ffn1_norm, ffn1_w_in, ffn1_w_out, mix_norm, mem_w_kv, hgrn_w_in, hgrn_gnorm, hgrn_w_out, gmlp_w_in, gmlp_ln_g, gmlp_ln_b, gmlp_w_s, gmlp_b_s, gmlp_w_out, ffn2_norm, ffn2_w_in, ffn2_w_out, final_norm, loss_target, m_mem_norm, m_lb_logits, m_ffn1_norm, m_ffn1_w_in, m_ffn1_w_out, m_mix_norm, m_mem_w_kv, m_hgrn_w_in, m_hgrn_gnorm, m_hgrn_w_out, m_gmlp_w_in, m_gmlp_ln_g, m_gmlp_ln_b, m_gmlp_w_s, m_gmlp_b_s, m_gmlp_w_out, m_ffn2_norm, m_ffn2_w_in, m_ffn2_w_out, m_final_norm, v_mem_norm, v_lb_logits, v_ffn1_norm, v_ffn1_w_in, v_ffn1_w_out, v_mix_norm, v_mem_w_kv, v_hgrn_w_in, v_hgrn_gnorm, v_hgrn_w_out, v_gmlp_w_in, v_gmlp_ln_g, v_gmlp_ln_b, v_gmlp_w_s, v_gmlp_b_s, v_gmlp_w_out, v_ffn2_norm, v_ffn2_w_in, v_ffn2_w_out, v_final_norm):
    weights = dict(mem_norm=mem_norm, lb_logits=lb_logits, ffn1_norm=ffn1_norm, ffn1_w_in=ffn1_w_in, ffn1_w_out=ffn1_w_out, mix_norm=mix_norm, mem_w_kv=mem_w_kv, hgrn_w_in=hgrn_w_in, hgrn_gnorm=hgrn_gnorm, hgrn_w_out=hgrn_w_out, gmlp_w_in=gmlp_w_in, gmlp_ln_g=gmlp_ln_g, gmlp_ln_b=gmlp_ln_b, gmlp_w_s=gmlp_w_s, gmlp_b_s=gmlp_b_s, gmlp_w_out=gmlp_w_out, ffn2_norm=ffn2_norm, ffn2_w_in=ffn2_w_in, ffn2_w_out=ffn2_w_out, final_norm=final_norm)
    mom_m = dict(mem_norm=m_mem_norm, lb_logits=m_lb_logits, ffn1_norm=m_ffn1_norm, ffn1_w_in=m_ffn1_w_in, ffn1_w_out=m_ffn1_w_out, mix_norm=m_mix_norm, mem_w_kv=m_mem_w_kv, hgrn_w_in=m_hgrn_w_in, hgrn_gnorm=m_hgrn_gnorm, hgrn_w_out=m_hgrn_w_out, gmlp_w_in=m_gmlp_w_in, gmlp_ln_g=m_gmlp_ln_g, gmlp_ln_b=m_gmlp_ln_b, gmlp_w_s=m_gmlp_w_s, gmlp_b_s=m_gmlp_b_s, gmlp_w_out=m_gmlp_w_out, ffn2_norm=m_ffn2_norm, ffn2_w_in=m_ffn2_w_in, ffn2_w_out=m_ffn2_w_out, final_norm=m_final_norm)
    mom_v = dict(mem_norm=v_mem_norm, lb_logits=v_lb_logits, ffn1_norm=v_ffn1_norm, ffn1_w_in=v_ffn1_w_in, ffn1_w_out=v_ffn1_w_out, mix_norm=v_mix_norm, mem_w_kv=v_mem_w_kv, hgrn_w_in=v_hgrn_w_in, hgrn_gnorm=v_hgrn_gnorm, hgrn_w_out=v_hgrn_w_out, gmlp_w_in=v_gmlp_w_in, gmlp_ln_g=v_gmlp_ln_g, gmlp_ln_b=v_gmlp_ln_b, gmlp_w_s=v_gmlp_w_s, gmlp_b_s=v_gmlp_b_s, gmlp_w_out=v_gmlp_w_out, ffn2_norm=v_ffn2_norm, ffn2_w_in=v_ffn2_w_in, ffn2_w_out=v_ffn2_w_out, final_norm=v_final_norm)
    order = list(weights)
    _, _, _, me = _mesh_pos()
    me_arr = jnp.reshape(me, (1,)).astype(jnp.int32)
    cuts = {name: c for name, c, _, _ in GROUPS}
    stages = [(i, blk) for i in (0, 1) for blk in ("ffn1", "mix", "ffn2")]

    gather, tokens = {}, []
    for st in stages:
        lands = [_place_rows(weights[name], l, cuts[name], me_arr, name=f"place_{name}_{l}")
                 for name, l in _stage_pieces(*st)]
        if st == (1, "mix"):
            lands.append(_place_ln(gmlp_ln_g, gmlp_ln_b, me_arr))
        send_sems, recv_sems, *thru, token = _copies_start(lands, lands, gather=True,
                                                           name=f"gather_start_l{st[0]}_{st[1]}")
        gather[st] = (thru, send_sems, recv_sems)
        tokens.append(token)

    def get_weights(st, after):
        thru, send_sems, recv_sems = gather[st]
        if st == stages[0]:
            after = tuple(after) + tuple(tokens)
        outs = _copies_wait(thru, send_sems, recv_sems, after, n_lands=len(thru), gather=True,
                            name=f"gather_wait_l{st[0]}_{st[1]}")
        w = {p: o.reshape(N_DEV * o.shape[1], D_MODEL) for p, o in zip(_stage_pieces(*st), outs)}
        if st == (1, "mix"):
            w["ln_g"] = outs[-1][:, 0, :].reshape(1, GM_WIDTH)
            w["ln_b"] = outs[-1][:, 1, :].reshape(1, GM_WIDTH)
        return w

    scatter = {}

    def put_grads(st, grads):
        views = [grads[p].reshape(N_DEV, -1, D_MODEL) for p in _stage_pieces(*st)]
        recv = _place_own(views, me_arr, name=f"scatter_place_l{st[0]}_{st[1]}")
        send_sems, recv_sems, *thru, token = _copies_start(views, recv, gather=False,
                                                           name=f"scatter_start_l{st[0]}_{st[1]}")
        scatter[st] = (thru, send_sems, recv_sems)
        return (token,)

    dx, small, loss_part = _step_local(
        x, mem, loss_target, get_weights, put_grads, mem_norm, lb_logits, ffn1_norm, mix_norm, hgrn_gnorm,
        gmlp_w_s, gmlp_b_s, ffn2_norm, final_norm)

    def slots_of(blk, after):
        slots = {}
        for i in (1, 0):
            thru, send_sems, recv_sems = scatter[(i, blk)]
            outs = _copies_wait(thru, send_sems, recv_sems, after, n_lands=len(thru) // 2, gather=False,
                                name=f"scatter_wait_l{i}_{blk}")
            slots.update(zip(_stage_pieces(i, blk), outs))
        return slots

    grad, delta, new_m, new_v = {}, {}, {}, {}

    def adam_groups(slots, names):
        for name in names:
            layers = GROUP_LAYERS[name]
            grad[name], delta[name], new_m[name], new_v[name] = _adam_big(
                [slots[(name, l)] for l in range(layers)], weights[name], mom_m[name], mom_v[name], cuts[name],
                name=f"{name}_adamw")

    adam_groups(slots_of("ffn2", (dx,)), ("ffn2_w_in", "ffn2_w_out"))
    adam_groups(slots_of("mix", (delta["ffn2_w_out"],)),
                ("mem_w_kv", "gmlp_w_in", "gmlp_w_out", "hgrn_w_in", "hgrn_w_out"))

    def small_parts(src):
        parts = {n: [src[n].reshape(-1, src[n].shape[-1])] for n in SLAB_AT if n not in SMALL_SHARDED}
        return parts

    w_s_rows = lambda a: a.reshape(GM_GROUPS * GM_CHUNK, GM_CHUNK)
    slab_slots, ws_slots = _exchange_small([_pack_slab(small, name="pack_small_grads"), w_s_rows(small["gmlp_w_s"][0])])
    (g_slab, d_slab, nm_slab, nv_slab), (g_ws, d_ws, nm_ws, nv_ws) = _adam_slabs(
        [slab_slots, ws_slots],
        [_pack_slab(small_parts(weights), name="pack_small_w"), w_s_rows(gmlp_w_s)],
        [_pack_slab(small_parts(mom_m), name="pack_small_m"), w_s_rows(m_gmlp_w_s)],
        [_pack_slab(small_parts(mom_v), name="pack_small_v"), w_s_rows(v_gmlp_w_s)])
    shapes = {n: weights[n].shape for n in SLAB_AT}
    for out, slab, ws in ((grad, g_slab, g_ws), (delta, d_slab, d_ws), (new_m, nm_slab, nm_ws), (new_v, nv_slab, nv_ws)):
        out.update(_unpack_slab(slab, shapes))
        out["gmlp_w_s"] = ws.reshape(gmlp_w_s.shape)
    blk = GM_WIDTH // N_DEV
    g_ln = [lax.dynamic_slice(g_slab[SLAB_AT[n]:SLAB_AT[n] + 2].reshape(1, GM_WIDTH), (0, me * blk), (1, blk))
            for n in SMALL_SHARDED]
    ln_out = _adam_vecs(g_ln, [weights[n] for n in SMALL_SHARDED], [mom_m[n] for n in SMALL_SHARDED],
                        [mom_v[n] for n in SMALL_SHARDED])
    for n, g, (d, nm, nv) in zip(SMALL_SHARDED, g_ln, ln_out):
        grad[n], delta[n], new_m[n], new_v[n] = g, d, nm, nv

    adam_groups(slots_of("ffn1", (delta["hgrn_w_out"], d_slab)), ("ffn1_w_in", "ffn1_w_out"))

    loss = lax.psum(loss_part[0, 0], MESH_AXES)
    grad_x = dx.reshape(B_LOC, SEQ, D_MODEL)
    return (loss, grad_x, *[grad[n] for n in order], *[delta[n] for n in order],
            *[new_m[n] for n in order], *[new_v[n] for n in order])


def _step_local(x, mem, loss_target, get_weights, put_grads, mem_norm, lb_logits, ffn1_norm, mix_norm, hgrn_gnorm,
                gmlp_w_s, gmlp_b_s, ffn2_norm, final_norm):
    w_s = gmlp_w_s[0]
    b_st = gmlp_b_s[0].T

    xs = x.reshape(N_TOK, D_MODEL)
    mem2d = mem.reshape(B_LOC * MEM_LEN, D_MODEL)
    mem_g = mem_norm.reshape(1, D_MODEL)
    saved, full = [], {}
    for i in range(2):
        full.update(get_weights((i, "ffn1"), (xs,)))
        if i == 0:
            memn = _rms_fwd(mem2d, mem_g, name="mem_norm_fwd")
        xs, s_ffn1 = _ffn_fwd(xs, ffn1_norm[i:i + 1], full[("ffn1_w_in", i)], full[("ffn1_w_out", i)], f"l{i}_ffn1")
        full.update(get_weights((i, "mix"), (xs,)))
        mixer = "hgrn" if i == 0 else "gmlp"
        hm = _rms_fwd(xs, mix_norm[i:i + 1], name=f"l{i}_mix_norm")
        kv = _mm(memn, full[("mem_w_kv", i)], tb=True, tm=512, tn=512, tk=D_MODEL, out_dtype=F32, name=f"l{i}_mem_kv")
        zm = _mm(hm, full[(f"{mixer}_w_in", 0)], tb=True, tm=1024, tn=512, tk=D_MODEL, out_dtype=F32, name=f"l{i}_mix_in")
        if i == 0:
            o_mix, o_pre, s_all = _hgrn_fwd(zm, lb_logits, hgrn_gnorm)
            mix_saved = (o_pre, s_all)
        else:
            o_mix = _gmlp_fwd(zm, full["ln_g"], full["ln_b"], w_s, b_st)
            mix_saved = ()
        o_mem = _attn_fwd(zm, kv, name=f"l{i}_attn")
        cat = jnp.concatenate([o_mix, o_mem], axis=1)
        x_mix = xs
        xs = _mm(cat, full[(f"{mixer}_w_out", 0)], tm=512, tn=D_MODEL, tk=cat.shape[1], out_dtype=F32, res=xs,
                 name=f"l{i}_mix_out")
        full.update(get_weights((i, "ffn2"), (xs,)))
        xs, s_ffn2 = _ffn_fwd(xs, ffn2_norm[i:i + 1], full[("ffn2_w_in", i)], full[("ffn2_w_out", i)], f"l{i}_ffn2")
        saved.append((s_ffn1, (x_mix, hm, kv, zm, cat, mix_saved), s_ffn2))

    dx, d_final, loss_part = _loss_head(xs, final_norm.reshape(1, D_MODEL), loss_target.reshape(N_TOK, D_MODEL))

    small = {"final_norm": [d_final]}
    d_ffn1, d_ffn2, d_mix = [None, None], [None, None], [None, None]
    dmemn = jnp.zeros((B_LOC * MEM_LEN, D_MODEL), F32)
    deps = ()
    for i in (1, 0):
        s_ffn1, (x_mix, hm, kv, zm, cat, mix_saved), s_ffn2 = saved[i]
        dx, d_ffn2[i], dw_in_t, dw_out = _ffn_bwd(
            dx, s_ffn2, ffn2_norm[i:i + 1], full[("ffn2_w_in", i)], full[("ffn2_w_out", i)], f"l{i}_ffn2", deps)
        deps = put_grads((i, "ffn2"), {("ffn2_w_in", i): dw_in_t, ("ffn2_w_out", i): dw_out})
        mixer = "hgrn" if i == 0 else "gmlp"
        w_in_t, w_out = full[(f"{mixer}_w_in", 0)], full[(f"{mixer}_w_out", 0)]
        width = cat.shape[1]
        g_mix = {}
        g_mix[(f"{mixer}_w_out", 0)] = _mm(cat, dx, ta=True, tm=width // 2, tn=D_MODEL, tk=1024, out_dtype=BF16,
                                           deps=deps, name=f"l{i}_mix_out_wgrad")
        dcat = _mm(dx, w_out, tb=True, tm=1024, tn=width // 2, tk=D_MODEL, out_dtype=F32, name=f"l{i}_mix_out_dgrad")
        dq, dk, dv = _attn_bwd(zm, kv, dcat, do_off=width - XA_HEADS * XA_DIM, name=f"l{i}_attn_bwd")
        if i == 0:
            dzq, dzf, dzi, dzg, dlbl, dgn = _hgrn_bwd(zm, mix_saved[0], dcat, mix_saved[1], lb_logits, hgrn_gnorm)
            small["lb_logits"], small["hgrn_gnorm"] = [dlbl], [dgn]
            dzm = jnp.concatenate([dzq, dzf, dzi, dzg, dq], axis=1)
        else:
            dzu, dzv, dws, dbt, dlng, dlnb = _gmlp_bwd(zm, dcat, full["ln_g"], full["ln_b"], w_s, b_st)
            small["gmlp_w_s"], small["gmlp_b_s"], small["gmlp_ln_g"], small["gmlp_ln_b"] = [dws], [dbt.T], [dlng], [dlnb]
            dzm = jnp.concatenate([dzu, dzv, dq], axis=1)
        g_mix[(f"{mixer}_w_in", 0)] = _mm(dzm, hm, ta=True, tm=512, tn=D_MODEL, tk=1024, out_dtype=BF16,
                                          name=f"l{i}_mix_in_wgrad")
        dkv = jnp.concatenate([dk, dv], axis=1)
        g_mix[("mem_w_kv", i)] = _mm(dkv, memn, ta=True, tm=512, tn=D_MODEL, tk=B_LOC * MEM_LEN, out_dtype=BF16,
                                     name=f"l{i}_mem_kv_wgrad")
        deps = put_grads((i, "mix"), g_mix)
        dh = _mm(dzm, w_in_t, tm=1024, tn=D_MODEL, tk=512, out_dtype=F32, deps=deps, name=f"l{i}_mix_in_dgrad")
        dx, d_mix[i] = _rms_bwd(x_mix, mix_norm[i:i + 1], dh, dx, name=f"l{i}_mix_norm_bwd")
        dmemn = _mm(dkv, full[("mem_w_kv", i)], tm=B_LOC * MEM_LEN, tn=D_MODEL, tk=512, out_dtype=F32, res=dmemn,
                    name=f"l{i}_mem_kv_dgrad")
        dx, d_ffn1[i], dw_in_t, dw_out = _ffn_bwd(
            dx, s_ffn1, ffn1_norm[i:i + 1], full[("ffn1_w_in", i)], full[("ffn1_w_out", i)], f"l{i}_ffn1")
        deps = put_grads((i, "ffn1"), {("ffn1_w_in", i): dw_in_t, ("ffn1_w_out", i): dw_out})
    _, dmem_g = _rms_bwd(mem2d, mem_g, dmemn, dmemn, deps=deps, name="mem_norm_bwd")
    small.update(mem_norm=[dmem_g], ffn1_norm=d_ffn1, ffn2_norm=d_ffn2, mix_norm=d_mix)
    return dx, small, loss_part
```

```python
import functools
import math

import jax
import jax.numpy as jnp
from jax import lax
from jax.experimental import pallas as pl
from jax.experimental.pallas import tpu as pltpu

F32 = jnp.float32
BF16 = jnp.bfloat16

D_MODEL = 1024
SEQ = 2048
B_LOC = 2
N_TOK = B_LOC * SEQ
MEM_LEN = 256
N_DEV = 8
EPS = 1e-6
D_FF = 2816
HG_HEADS = 8
HG_DIM = 128
HG_CHUNK = 64
HG_NCHUNK = SEQ // HG_CHUNK
GM_CHUNK = 128
GM_GROUPS = 8
GM_WIDTH = 2048
GM_GDIM = GM_WIDTH // GM_GROUPS
XA_HEADS = 4
XA_DIM = 256
XA_OFF = 4096

ADAM_LR = 0.001
ADAM_B1 = 0.9
ADAM_B2 = 0.999
ADAM_EPS = 1e-08
ADAM_WD = 0.01
ADAM_STEP = 10

VMEM_LIMIT_BYTES = 56 * 1024 * 1024
MESH_AXES = ("x", "y", "c")

GROUPS = (
    ("ffn1_w_in", True, 2, 704),
    ("ffn1_w_out", False, 2, 352),
    ("mem_w_kv", True, 2, 256),
    ("hgrn_w_in", True, 1, 640),
    ("hgrn_w_out", False, 1, 256),
    ("gmlp_w_in", True, 1, 640),
    ("gmlp_w_out", False, 1, 384),
    ("ffn2_w_in", True, 2, 704),
    ("ffn2_w_out", False, 2, 352),
)
GROUP_LAYERS = {name: layers for name, _, layers, _ in GROUPS}


def _stage_pieces(layer, block):
    if block == "mix":
        mixer = "hgrn" if layer == 0 else "gmlp"
        return (("mem_w_kv", layer), (f"{mixer}_w_in", 0), (f"{mixer}_w_out", 0))
    return ((f"{block}_w_in", layer), (f"{block}_w_out", layer))


ANY_SPEC = pl.BlockSpec(memory_space=pl.ANY)
HBM_SPEC = pl.BlockSpec(memory_space=pltpu.HBM)
SEM_SPEC = pl.BlockSpec(memory_space=pltpu.SEMAPHORE)


def _cp(*sem):
    return pltpu.CompilerParams(dimension_semantics=sem, vmem_limit_bytes=VMEM_LIMIT_BYTES)


def _sigmoid(x):
    return 1.0 / (1.0 + jnp.exp(-x))


def _gelu_parts(x):
    cdf = 0.5 * (1.0 + lax.erf(x * (1.0 / math.sqrt(2.0))))
    pdf = jnp.exp(-0.5 * x * x) * (1.0 / math.sqrt(2.0 * math.pi))
    return x * cdf, cdf + x * pdf


def _mm(a, b, *, ta=False, tb=False, tm, tn, tk, out_dtype, res=None, scale=1.0, deps=(), name):
    m, k = (a.shape[1], a.shape[0]) if ta else a.shape
    n, kb = b.shape if tb else (b.shape[1], b.shape[0])
    assert k == kb and m % tm == 0 and n % tn == 0 and k % tk == 0, (name, a.shape, b.shape)
    nk = k // tk
    dn = (((0 if ta else 1,), (1 if tb else 0,)), ((), ()))
    n_in = 2 + (res is not None) + len(deps)

    def body(*refs):
        a_ref, b_ref = refs[:2]
        r_ref = refs[2] if res is not None else None
        o_ref, scr = refs[n_in], refs[n_in + 1:]
        p = lax.dot_general(a_ref[...].astype(BF16), b_ref[...].astype(BF16), dn, preferred_element_type=F32)

        def finish(acc):
            if scale != 1.0:
                acc = scale * acc
            if r_ref is not None:
                acc = r_ref[...] + acc
            o_ref[...] = acc.astype(out_dtype)

        if nk == 1:
            finish(p)
        else:
            acc_ref = scr[0]
            kk = pl.program_id(2)

            @pl.when(kk == 0)
            def _():
                acc_ref[...] = p

            @pl.when(kk > 0)
            def _():
                acc_ref[...] += p

            @pl.when(kk == nk - 1)
            def _():
                finish(acc_ref[...])

    a_spec = pl.BlockSpec((tk, tm), lambda i, j, kk: (kk, i)) if ta else pl.BlockSpec((tm, tk), lambda i, j, kk: (i, kk))
    b_spec = pl.BlockSpec((tn, tk), lambda i, j, kk: (j, kk)) if tb else pl.BlockSpec((tk, tn), lambda i, j, kk: (kk, j))
    o_spec = pl.BlockSpec((tm, tn), lambda i, j, kk: (i, j))
    in_specs = [a_spec, b_spec] + ([o_spec] if res is not None else []) + [ANY_SPEC] * len(deps)
    args = (a, b) + ((res,) if res is not None else ()) + tuple(deps)
    return pl.pallas_call(
        body,
        name=name,
        grid=(m // tm, n // tn, nk),
        in_specs=in_specs,
        out_specs=o_spec,
        out_shape=jax.ShapeDtypeStruct((m, n), out_dtype),
        scratch_shapes=[pltpu.VMEM((tm, tn), F32)] if nk > 1 else [],
        compiler_params=_cp("parallel", "parallel", "arbitrary"),
    )(*args)


def _rms_fwd(x, g, *, name, deps=(), tm=512):
    rows = x.shape[0]

    def body(x_ref, g_ref, *rest):
        o_ref = rest[len(deps)]
        xv = x_ref[...]
        r = lax.rsqrt(jnp.mean(xv * xv, axis=-1, keepdims=True) + EPS)
        o_ref[...] = (xv * r * g_ref[...]).astype(BF16)

    row = pl.BlockSpec((tm, D_MODEL), lambda i: (i, 0))
    return pl.pallas_call(
        body,
        name=name,
        grid=(rows // tm,),
        in_specs=[row, pl.BlockSpec((1, D_MODEL), lambda i: (0, 0))] + [ANY_SPEC] * len(deps),
        out_specs=row,
        out_shape=jax.ShapeDtypeStruct((rows, D_MODEL), BF16),
        compiler_params=_cp("parallel"),
    )(x, g, *deps)


def _rms_bwd(x, g, dh, dres, *, name, deps=(), tm=512):
    rows = x.shape[0]

    def body(x_ref, g_ref, dh_ref, dres_ref, *rest):
        dx_ref, dg_ref = rest[len(deps):]
        xv = x_ref[...]
        r = lax.rsqrt(jnp.mean(xv * xv, axis=-1, keepdims=True) + EPS)
        xhat = xv * r
        dhv = dh_ref[...]
        part = jnp.sum(dhv * xhat, axis=0, keepdims=True)

        @pl.when(pl.program_id(0) == 0)
        def _():
            dg_ref[...] = part

        @pl.when(pl.program_id(0) > 0)
        def _():
            dg_ref[...] += part

        dxh = dhv * g_ref[...]
        dx_ref[...] = dres_ref[...] + r * (dxh - xhat * jnp.mean(dxh * xhat, axis=-1, keepdims=True))

    row = pl.BlockSpec((tm, D_MODEL), lambda i: (i, 0))
    vec = pl.BlockSpec((1, D_MODEL), lambda i: (0, 0))
    return pl.pallas_call(
        body,
        name=name,
        grid=(rows // tm,),
        in_specs=[row, vec, row, row] + [ANY_SPEC] * len(deps),
        out_specs=[row, vec],
        out_shape=[jax.ShapeDtypeStruct((rows, D_MODEL), F32), jax.ShapeDtypeStruct((1, D_MODEL), F32)],
        compiler_params=_cp("arbitrary"),
    )(x, g, dh, dres, *deps)


def _swiglu_fwd(z, *, name, tm=512):
    def body(g_ref, u_ref, o_ref):
        gv = g_ref[...]
        o_ref[...] = (gv * _sigmoid(gv) * u_ref[...]).astype(BF16)

    return pl.pallas_call(
        body,
        name=name,
        grid=(N_TOK // tm,),
        in_specs=[pl.BlockSpec((tm, D_FF), lambda i: (i, 0)), pl.BlockSpec((tm, D_FF), lambda i: (i, 1))],
        out_specs=pl.BlockSpec((tm, D_FF), lambda i: (i, 0)),
        out_shape=jax.ShapeDtypeStruct((N_TOK, D_FF), BF16),
        compiler_params=_cp("parallel"),
    )(z, z)


def _swiglu_bwd(z, dact, *, scale, name, tm=512):
    def body(g_ref, u_ref, da_ref, o_ref):
        gv = g_ref[...]
        s = _sigmoid(gv)
        da = da_ref[...] * scale
        o_ref[:, :D_FF] = (da * u_ref[...] * (s * (1.0 + gv * (1.0 - s)))).astype(BF16)
        o_ref[:, D_FF:] = (da * (gv * s)).astype(BF16)

    half = lambda j: pl.BlockSpec((tm, D_FF), lambda i: (i, j))
    return pl.pallas_call(
        body,
        name=name,
        grid=(N_TOK // tm,),
        in_specs=[half(0), half(1), half(0)],
        out_specs=pl.BlockSpec((tm, 2 * D_FF), lambda i: (i, 0)),
        out_shape=jax.ShapeDtypeStruct((N_TOK, 2 * D_FF), BF16),
        compiler_params=_cp("parallel"),
    )(z, z, dact)


def _loss_head(x, g, target, *, tm=512):
    def body(x_ref, g_ref, t_ref, dx_ref, dg_ref, loss_ref):
        xv = x_ref[...]
        gv = g_ref[...]
        r = lax.rsqrt(jnp.mean(xv * xv, axis=-1, keepdims=True) + EPS)
        xhat = xv * r
        err = xhat * gv - t_ref[...]
        loss_part = jnp.zeros((1, 128), F32) + 0.5 * jnp.sum(jnp.mean(err * err, axis=-1, keepdims=True))
        dy = err * (1.0 / D_MODEL)
        dg_part = jnp.sum(dy * xhat, axis=0, keepdims=True)

        @pl.when(pl.program_id(0) == 0)
        def _():
            dg_ref[...] = dg_part
            loss_ref[...] = loss_part

        @pl.when(pl.program_id(0) > 0)
        def _():
            dg_ref[...] += dg_part
            loss_ref[...] += loss_part

        dxh = dy * gv
        dx_ref[...] = r * (dxh - xhat * jnp.mean(dxh * xhat, axis=-1, keepdims=True))

    row = pl.BlockSpec((tm, D_MODEL), lambda i: (i, 0))
    vec = pl.BlockSpec((1, D_MODEL), lambda i: (0, 0))
    return pl.pallas_call(
        body,
        name="loss_head",
        grid=(N_TOK // tm,),
        in_specs=[row, vec, row],
        out_specs=[row, vec, pl.BlockSpec((1, 128), lambda i: (0, 0))],
        out_shape=[
            jax.ShapeDtypeStruct((N_TOK, D_MODEL), F32),
            jax.ShapeDtypeStruct((1, D_MODEL), F32),
            jax.ShapeDtypeStruct((1, 128), F32),
        ],
        compiler_params=_cp("arbitrary"),
    )(x, g, target)


_NT = (((1,), (1,)), ((), ()))
_TN = (((0,), (0,)), ((), ()))
XA_TQ = 1024
XA_SCALE = XA_DIM ** -0.5


def _attn_probs(q16, k16):
    s = lax.dot_general(q16, k16, _NT, preferred_element_type=F32) * XA_SCALE
    e = jnp.exp(s - jnp.max(s, axis=-1, keepdims=True))
    return e / jnp.sum(e, axis=-1, keepdims=True)


def _attn_fwd(z, kv, *, name):
    nt = SEQ // XA_TQ

    def body(q_ref, k_ref, v_ref, o_ref):
        p = _attn_probs(q_ref[...].astype(BF16), k_ref[...].astype(BF16))
        o_ref[...] = jnp.dot(p.astype(BF16), v_ref[...].astype(BF16), preferred_element_type=F32).astype(BF16)

    return pl.pallas_call(
        body,
        name=name,
        grid=(B_LOC, XA_HEADS, nt),
        in_specs=[
            pl.BlockSpec((XA_TQ, XA_DIM), lambda b, h, t: (b * nt + t, XA_OFF // XA_DIM + h)),
            pl.BlockSpec((MEM_LEN, XA_DIM), lambda b, h, t: (b, h)),
            pl.BlockSpec((MEM_LEN, XA_DIM), lambda b, h, t: (b, XA_HEADS + h)),
        ],
        out_specs=pl.BlockSpec((XA_TQ, XA_DIM), lambda b, h, t: (b * nt + t, h)),
        out_shape=jax.ShapeDtypeStruct((N_TOK, XA_HEADS * XA_DIM), BF16),
        compiler_params=_cp("parallel", "parallel", "arbitrary"),
    )(z, kv, kv)


def _attn_bwd(z, kv, dcat, *, do_off, name):
    nt = SEQ // XA_TQ

    def body(q_ref, k_ref, v_ref, do_ref, dq_ref, dk_ref, dv_ref):
        q16 = q_ref[...].astype(BF16)
        k16 = k_ref[...].astype(BF16)
        v16 = v_ref[...].astype(BF16)
        do16 = do_ref[...].astype(BF16)
        p = _attn_probs(q16, k16)
        dv_part = lax.dot_general(p.astype(BF16), do16, _TN, preferred_element_type=F32)
        dp = lax.dot_general(do16, v16, _NT, preferred_element_type=F32)
        ds16 = (p * (dp - jnp.sum(dp * p, axis=-1, keepdims=True)) * XA_SCALE).astype(BF16)
        dq_ref[...] = jnp.dot(ds16, k16, preferred_element_type=F32).astype(BF16)
        dk_part = lax.dot_general(ds16, q16, _TN, preferred_element_type=F32)

        @pl.when(pl.program_id(2) == 0)
        def _():
            dk_ref[...] = dk_part
            dv_ref[...] = dv_part

        @pl.when(pl.program_id(2) > 0)
        def _():
            dk_ref[...] += dk_part
            dv_ref[...] += dv_part

    qspec = pl.BlockSpec((XA_TQ, XA_DIM), lambda b, h, t: (b * nt + t, XA_OFF // XA_DIM + h))
    kspec = lambda off: pl.BlockSpec((MEM_LEN, XA_DIM), lambda b, h, t: (b, off + h))
    return pl.pallas_call(
        body,
        name=name,
        grid=(B_LOC, XA_HEADS, nt),
        in_specs=[qspec, kspec(0), kspec(XA_HEADS),
                  pl.BlockSpec((XA_TQ, XA_DIM), lambda b, h, t: (b * nt + t, do_off // XA_DIM + h))],
        out_specs=[pl.BlockSpec((XA_TQ, XA_DIM), lambda b, h, t: (b * nt + t, h)), kspec(0), kspec(0)],
        out_shape=[
            jax.ShapeDtypeStruct((N_TOK, XA_HEADS * XA_DIM), BF16),
            jax.ShapeDtypeStruct((B_LOC * MEM_LEN, XA_HEADS * XA_DIM), F32),
            jax.ShapeDtypeStruct((B_LOC * MEM_LEN, XA_HEADS * XA_DIM), F32),
        ],
        compiler_params=_cp("parallel", "parallel", "arbitrary"),
    )(z, kv, kv, dcat)


def _tril(n):
    return lax.broadcasted_iota(jnp.int32, (n, n), 0) >= lax.broadcasted_iota(jnp.int32, (n, n), 1)


def _lower_bound(lbl):
    e = jnp.exp(lbl - jnp.max(lbl, axis=0, keepdims=True))
    p = e / jnp.sum(e, axis=0, keepdims=True)
    return p[0:1, :], p


def _hgrn_gates(zq, zf, lb, tril_f):
    sig = _sigmoid(zf)
    f = lb + (1.0 - lb) * sig
    kk = 1.0 - f
    sq = _sigmoid(zq)
    q = zq * sq
    b = jnp.dot(tril_f, jnp.log(f), preferred_element_type=F32, precision=lax.Precision.HIGHEST)
    bl = b[HG_CHUNK - 1:HG_CHUNK, :]
    return q, sq, sig, f, kk, b, bl


def _hgrn_zspec(section):
    return pl.BlockSpec((SEQ, HG_DIM), lambda h, b: (b, section * HG_HEADS + h))


def _hgrn_fwd(z, lb_logits, gnorm):
    def body(zq_ref, zf_ref, zi_ref, zg_ref, lbl_ref, gn_ref, o_ref, opre_ref, sall_ref, st_ref):
        lb, _ = _lower_bound(lbl_ref[...])
        gn = gn_ref[...]
        mask = _tril(HG_CHUNK)
        tril_f = mask.astype(F32)
        st_ref[...] = jnp.zeros_like(st_ref)

        def chunk(c, carry):
            rows = pl.ds(pl.multiple_of(c * HG_CHUNK, HG_CHUNK), HG_CHUNK)
            q, _, _, _, kk, b, bl = _hgrn_gates(zq_ref[rows, :], zf_ref[rows, :], lb, tril_f)
            v16 = zi_ref[rows, :].astype(BF16)
            qd16 = (q * jnp.exp(b)).astype(BF16)
            ki16 = (kk * jnp.exp(-b)).astype(BF16)
            kd16 = (kk * jnp.exp(bl - b)).astype(BF16)
            a = jnp.where(mask, lax.dot_general(qd16, ki16, _NT, preferred_element_type=F32), 0.0)
            st = st_ref[...]
            sall_ref[0, 0, c] = st
            o = jnp.dot(a.astype(BF16), v16, preferred_element_type=F32) + lax.dot_general(
                qd16, st.astype(BF16), _NT, preferred_element_type=F32)
            st_ref[...] = st * jnp.exp(bl) + lax.dot_general(v16, kd16, _TN, preferred_element_type=F32)
            opre_ref[rows, :] = o
            r = lax.rsqrt(jnp.mean(o * o, axis=-1, keepdims=True) + EPS)
            zg = zg_ref[rows, :]
            o_ref[rows, :] = ((o * r * gn) * (zg * _sigmoid(zg))).astype(BF16)
            return carry

        lax.fori_loop(0, HG_NCHUNK, chunk, 0)

    tok = pl.BlockSpec((SEQ, HG_DIM), lambda h, b: (b, h))
    return pl.pallas_call(
        body,
        name="hgrn_fwd",
        grid=(HG_HEADS, B_LOC),
        in_specs=[_hgrn_zspec(0), _hgrn_zspec(1), _hgrn_zspec(2), _hgrn_zspec(3),
                  pl.BlockSpec((3, HG_DIM), lambda h, b: (0, h)), pl.BlockSpec((1, HG_DIM), lambda h, b: (0, 0))],
        out_specs=[tok, tok, pl.BlockSpec((1, 1, HG_NCHUNK, HG_DIM, HG_DIM), lambda h, b: (b, h, 0, 0, 0))],
        out_shape=[
            jax.ShapeDtypeStruct((N_TOK, HG_HEADS * HG_DIM), BF16),
            jax.ShapeDtypeStruct((N_TOK, HG_HEADS * HG_DIM), F32),
            jax.ShapeDtypeStruct((B_LOC, HG_HEADS, HG_NCHUNK, HG_DIM, HG_DIM), F32),
        ],
        scratch_shapes=[pltpu.VMEM((HG_DIM, HG_DIM), F32)],
        compiler_params=_cp("parallel", "arbitrary"),
    )(z, z, z, z, lb_logits, gnorm)


def _hgrn_bwd(z, opre, dcat, sall, lb_logits, gnorm):
    def body(zq_ref, zf_ref, zi_ref, zg_ref, opre_ref, dout_ref, sall_ref, lbl_ref, gn_ref,
             dzq_ref, dzf_ref, dzi_ref, dzg_ref, dlbl_ref, dgn_ref, dst_ref, dlb_ref):
        h_id, b_id = pl.program_id(0), pl.program_id(1)
        lb, p = _lower_bound(lbl_ref[...])
        gn = gn_ref[...]
        mask = _tril(HG_CHUNK)
        tril_f = mask.astype(F32)
        dst_ref[...] = jnp.zeros_like(dst_ref)
        dlb_ref[...] = jnp.zeros_like(dlb_ref)

        @pl.when((h_id == 0) & (b_id == 0))
        def _():
            dgn_ref[...] = jnp.zeros_like(dgn_ref)

        def chunk(i, carry):
            c = HG_NCHUNK - 1 - i
            rows = pl.ds(pl.multiple_of(c * HG_CHUNK, HG_CHUNK), HG_CHUNK)
            zq, zg = zq_ref[rows, :], zg_ref[rows, :]
            q, sq, sig, f, kk, b, bl = _hgrn_gates(zq, zf_ref[rows, :], lb, tril_f)
            v16 = zi_ref[rows, :].astype(BF16)
            eb, enb, ebl_b, ebl = jnp.exp(b), jnp.exp(-b), jnp.exp(bl - b), jnp.exp(bl)
            qd, ki, kd = q * eb, kk * enb, kk * ebl_b
            qd16, ki16, kd16 = qd.astype(BF16), ki.astype(BF16), kd.astype(BF16)
            o = opre_ref[rows, :]
            dout = dout_ref[rows, :]
            r = lax.rsqrt(jnp.mean(o * o, axis=-1, keepdims=True) + EPS)
            ohat = o * r
            sg = _sigmoid(zg)
            d_on = dout * (zg * sg)
            dzg_ref[rows, :] = (dout * (ohat * gn) * (sg * (1.0 + zg * (1.0 - sg)))).astype(BF16)
            dgn_ref[...] += jnp.sum(d_on * ohat, axis=0, keepdims=True)
            dohat = d_on * gn
            do16 = (r * (dohat - ohat * jnp.mean(dohat * ohat, axis=-1, keepdims=True))).astype(BF16)
            st = sall_ref[0, 0, c]
            st16 = st.astype(BF16)
            dst = dst_ref[...]
            dst16 = dst.astype(BF16)
            a16 = jnp.where(mask, lax.dot_general(qd16, ki16, _NT, preferred_element_type=F32), 0.0).astype(BF16)
            da16 = jnp.where(mask, lax.dot_general(do16, v16, _NT, preferred_element_type=F32), 0.0).astype(BF16)
            dv = lax.dot_general(a16, do16, _TN, preferred_element_type=F32) + lax.dot_general(
                kd16, dst16, _NT, preferred_element_type=F32)
            dqd = jnp.dot(da16, ki16, preferred_element_type=F32) + jnp.dot(do16, st16, preferred_element_type=F32)
            dki = lax.dot_general(da16, qd16, _TN, preferred_element_type=F32)
            dkd = jnp.dot(v16, dst16, preferred_element_type=F32)
            dbl = jnp.sum(dkd * kd, axis=0, keepdims=True) + ebl * jnp.sum(st * dst, axis=0, keepdims=True)
            dst_ref[...] = dst * ebl + lax.dot_general(do16, qd16, _TN, preferred_element_type=F32)
            dzi_ref[rows, :] = dv.astype(BF16)
            dzq_ref[rows, :] = (dqd * eb * (sq * (1.0 + zq * (1.0 - sq)))).astype(BF16)
            dkk = dki * enb + dkd * ebl_b
            db = dqd * qd - dki * ki - dkd * kd
            dlogf = lax.dot_general(tril_f, db, _TN, preferred_element_type=F32, precision=lax.Precision.HIGHEST) + dbl
            df = dlogf / f - dkk
            dzf_ref[rows, :] = (df * (1.0 - lb) * sig * (1.0 - sig)).astype(BF16)
            dlb_ref[...] += jnp.sum(df * (1.0 - sig), axis=0, keepdims=True)
            return carry

        lax.fori_loop(0, HG_NCHUNK, chunk, 0)
        row0 = (lax.broadcasted_iota(jnp.int32, (3, HG_DIM), 0) == 0).astype(F32)
        dlbl_part = dlb_ref[...] * lb * (row0 - p)

        @pl.when(b_id == 0)
        def _():
            dlbl_ref[...] = dlbl_part

        @pl.when(b_id > 0)
        def _():
            dlbl_ref[...] += dlbl_part

    tok = pl.BlockSpec((SEQ, HG_DIM), lambda h, b: (b, h))
    tok_shape = jax.ShapeDtypeStruct((N_TOK, HG_HEADS * HG_DIM), BF16)
    return pl.pallas_call(
        body,
        name="hgrn_bwd",
        grid=(HG_HEADS, B_LOC),
        in_specs=[_hgrn_zspec(0), _hgrn_zspec(1), _hgrn_zspec(2), _hgrn_zspec(3), tok, tok,
                  pl.BlockSpec((1, 1, HG_NCHUNK, HG_DIM, HG_DIM), lambda h, b: (b, h, 0, 0, 0)),
                  pl.BlockSpec((3, HG_DIM), lambda h, b: (0, h)), pl.BlockSpec((1, HG_DIM), lambda h, b: (0, 0))],
        out_specs=[tok, tok, tok, tok, pl.BlockSpec((3, HG_DIM), lambda h, b: (0, h)),
                   pl.BlockSpec((1, HG_DIM), lambda h, b: (0, 0))],
        out_shape=[tok_shape, tok_shape, tok_shape, tok_shape,
                   jax.ShapeDtypeStruct((3, HG_HEADS * HG_DIM), F32), jax.ShapeDtypeStruct((1, HG_DIM), F32)],
        scratch_shapes=[pltpu.VMEM((HG_DIM, HG_DIM), F32), pltpu.VMEM((1, HG_DIM), F32)],
        compiler_params=_cp("arbitrary", "arbitrary"),
    )(z, z, z, z, opre, dcat, sall, lb_logits, gnorm)


GM_TM = 256


def _gmlp_norm(zv, ln_g, ln_b):
    gv, dgelu = _gelu_parts(zv)
    xc = gv - jnp.mean(gv, axis=-1, keepdims=True)
    rstd = lax.rsqrt(jnp.mean(xc * xc, axis=-1, keepdims=True) + EPS)
    vhat = xc * rstd
    return vhat * ln_g + ln_b, vhat, rstd, dgelu


def _gmlp_specs():
    half = lambda j: pl.BlockSpec((GM_TM, GM_WIDTH), lambda i: (i, j))
    vec = pl.BlockSpec((1, GM_WIDTH), lambda i: (0, 0))
    w = pl.BlockSpec((GM_GROUPS, GM_CHUNK, GM_CHUNK), lambda i: (0, 0, 0))
    bt = pl.BlockSpec((GM_CHUNK, GM_GROUPS), lambda i: (0, 0))
    return half, vec, w, bt


def _gmlp_fwd(z, ln_g, ln_b, w_s, b_st):
    def body(zu_ref, zv_ref, g_ref, b_ref, w_ref, bt_ref, o_ref):
        u, _ = _gelu_parts(zu_ref[...])
        v, _, _, _ = _gmlp_norm(zv_ref[...], g_ref[...], b_ref[...])
        v16 = v.astype(BF16)
        mask = _tril(GM_CHUNK)
        bt = bt_ref[...]
        for g in range(GM_GROUPS):
            wm16 = jnp.where(mask, w_ref[g], 0.0).astype(BF16)
            cols = slice(g * GM_GDIM, (g + 1) * GM_GDIM)
            for c in range(GM_TM // GM_CHUNK):
                rows = slice(c * GM_CHUNK, (c + 1) * GM_CHUNK)
                mixed = jnp.dot(wm16, v16[rows, cols], preferred_element_type=F32) + bt[:, g:g + 1]
                o_ref[rows, cols] = (u[rows, cols] * mixed).astype(BF16)

    half, vec, w, bt = _gmlp_specs()
    return pl.pallas_call(
        body,
        name="gmlp_fwd",
        grid=(N_TOK // GM_TM,),
        in_specs=[half(0), half(1), vec, vec, w, bt],
        out_specs=half(0),
        out_shape=jax.ShapeDtypeStruct((N_TOK, GM_WIDTH), BF16),
        compiler_params=_cp("parallel"),
    )(z, z, ln_g, ln_b, w_s, b_st)


def _gmlp_bwd(z, dcat, ln_g, ln_b, w_s, b_st):
    def body(zu_ref, zv_ref, dout_ref, g_ref, b_ref, w_ref, bt_ref,
             dzu_ref, dzv_ref, dw_ref, dbt_ref, dg_ref, db_ref, dv_ref):
        @pl.when(pl.program_id(0) == 0)
        def _():
            dw_ref[...] = jnp.zeros_like(dw_ref)
            dbt_ref[...] = jnp.zeros_like(dbt_ref)
            dg_ref[...] = jnp.zeros_like(dg_ref)
            db_ref[...] = jnp.zeros_like(db_ref)

        zu = zu_ref[...]
        u, du_dz = _gelu_parts(zu)
        ln_g = g_ref[...]
        v, vhat, rstd, dgv_dz = _gmlp_norm(zv_ref[...], ln_g, b_ref[...])
        v16 = v.astype(BF16)
        dout = dout_ref[...]
        dmixed = dout * u
        dm16 = dmixed.astype(BF16)
        mask = _tril(GM_CHUNK)
        bt = bt_ref[...]
        group_id = lax.broadcasted_iota(jnp.int32, (1, GM_GROUPS), 1)
        dbt = jnp.zeros((GM_CHUNK, GM_GROUPS), F32)
        for g in range(GM_GROUPS):
            wm16 = jnp.where(mask, w_ref[g], 0.0).astype(BF16)
            cols = slice(g * GM_GDIM, (g + 1) * GM_GDIM)
            dw = jnp.zeros((GM_CHUNK, GM_CHUNK), F32)
            dbt_g = jnp.zeros((GM_CHUNK, 1), F32)
            for c in range(GM_TM // GM_CHUNK):
                rows = slice(c * GM_CHUNK, (c + 1) * GM_CHUNK)
                mixed = jnp.dot(wm16, v16[rows, cols], preferred_element_type=F32) + bt[:, g:g + 1]
                dzu_ref[rows, cols] = (dout[rows, cols] * mixed * du_dz[rows, cols]).astype(BF16)
                dw += lax.dot_general(dm16[rows, cols], v16[rows, cols], _NT, preferred_element_type=F32)
                dbt_g += jnp.sum(dmixed[rows, cols], axis=-1, keepdims=True)
                dv_ref[rows, cols] = lax.dot_general(wm16, dm16[rows, cols], _TN, preferred_element_type=F32)
            dw_ref[g] += jnp.where(mask, dw, 0.0)
            dbt = dbt + dbt_g * (group_id == g).astype(F32)
        dbt_ref[...] += dbt
        dv = dv_ref[...]
        dg_ref[...] += jnp.sum(dv * vhat, axis=0, keepdims=True)
        db_ref[...] += jnp.sum(dv, axis=0, keepdims=True)
        dvh = dv * ln_g
        dgv = rstd * (dvh - jnp.mean(dvh, axis=-1, keepdims=True) - vhat * jnp.mean(dvh * vhat, axis=-1, keepdims=True))
        dzv_ref[...] = (dgv * dgv_dz).astype(BF16)

    half, vec, w, bt = _gmlp_specs()
    tok_shape = jax.ShapeDtypeStruct((N_TOK, GM_WIDTH), BF16)
    return pl.pallas_call(
        body,
        name="gmlp_bwd",
        grid=(N_TOK // GM_TM,),
        in_specs=[half(0), half(1), half(0), vec, vec, w, bt],
        out_specs=[half(0), half(0), w, bt, vec, vec],
        out_shape=[tok_shape, tok_shape, jax.ShapeDtypeStruct((GM_GROUPS, GM_CHUNK, GM_CHUNK), F32),
                   jax.ShapeDtypeStruct((GM_CHUNK, GM_GROUPS), F32),
                   jax.ShapeDtypeStruct((1, GM_WIDTH), F32), jax.ShapeDtypeStruct((1, GM_WIDTH), F32)],
        scratch_shapes=[pltpu.VMEM((GM_TM, GM_WIDTH), F32)],
        compiler_params=_cp("arbitrary"),
    )(z, z, dcat, ln_g, ln_b, w_s, b_st)


def _own_slot(shape):
    return pl.BlockSpec((None,) + tuple(shape), lambda i, me_ref: (me_ref[0],) + (0,) * len(shape))


def _place_rows(w, layer, cuts_columns, me, *, name):
    _, r, c = w.shape
    n = c if cuts_columns else r

    def body(me_ref, w_ref, o_ref):
        wv = w_ref[...]
        o_ref[...] = (wv.T if cuts_columns else wv).astype(BF16)

    return pl.pallas_call(
        body,
        name=name,
        grid_spec=pltpu.PrefetchScalarGridSpec(
            num_scalar_prefetch=1, grid=(1,),
            in_specs=[pl.BlockSpec((None, r, c), lambda i, me_ref: (layer, 0, 0))],
            out_specs=_own_slot((n, D_MODEL))),
        out_shape=jax.ShapeDtypeStruct((N_DEV, n, D_MODEL), BF16),
        compiler_params=_cp("arbitrary"),
    )(me, w)


def _place_ln(ln_g, ln_b, me):
    blk = ln_g.shape[1]

    def body(me_ref, g_ref, b_ref, o_ref):
        o_ref[...] = jnp.zeros_like(o_ref)
        o_ref[0:1, :] = g_ref[...]
        o_ref[1:2, :] = b_ref[...]

    vec = pl.BlockSpec((1, blk), lambda i, me_ref: (0, 0))
    return pl.pallas_call(
        body,
        name="place_ln",
        grid_spec=pltpu.PrefetchScalarGridSpec(
            num_scalar_prefetch=1, grid=(1,), in_specs=[vec, vec], out_specs=_own_slot((8, blk))),
        out_shape=jax.ShapeDtypeStruct((N_DEV, 8, blk), F32),
        compiler_params=_cp("arbitrary"),
    )(me, ln_g, ln_b)


def _place_slab(a, me, *, name):
    def body(me_ref, a_ref, o_ref):
        o_ref[...] = a_ref[...]

    return pl.pallas_call(
        body,
        name=name,
        grid_spec=pltpu.PrefetchScalarGridSpec(
            num_scalar_prefetch=1, grid=(1,),
            in_specs=[pl.BlockSpec(a.shape, lambda i, me_ref: (0, 0))], out_specs=_own_slot(a.shape)),
        out_shape=jax.ShapeDtypeStruct((N_DEV,) + a.shape, a.dtype),
        compiler_params=_cp("arbitrary"),
    )(me, a)


def _place_own(grads, me, *, name):
    k = len(grads)

    def body(me_ref, *refs):
        for src, dst in zip(refs[:k], refs[k:]):
            dst[...] = src[...]

    specs = [_own_slot(g.shape[1:]) for g in grads]
    return pl.pallas_call(
        body,
        name=name,
        grid_spec=pltpu.PrefetchScalarGridSpec(num_scalar_prefetch=1, grid=(1,), in_specs=specs, out_specs=specs),
        out_shape=[jax.ShapeDtypeStruct(g.shape, g.dtype) for g in grads],
        compiler_params=_cp("arbitrary"),
    )(me, *grads)


def _mesh_pos():
    x, y, c = (lax.axis_index(a) for a in MESH_AXES)
    return x, y, c, 4 * x + 2 * y + c


def _peer(x, y, c, r):
    px = 1 - x if r & 4 else x
    py = 1 - y if r & 2 else y
    pc = 1 - c if r & 1 else c
    return (px, py, pc), 4 * px + 2 * py + pc


def _peer_copies(srcs, lands, send_sems, recv_sems, gather):
    x, y, c, me = _mesh_pos()
    pairs = []
    for r in range(1, N_DEV):
        peer, peer_blk = _peer(x, y, c, r)
        for k, (src, land) in enumerate(zip(srcs, lands)):
            idx = k * (N_DEV - 1) + r - 1
            sems = dict(send_sem=send_sems.at[idx], recv_sem=recv_sems.at[idx], device_id=peer,
                        device_id_type=pl.DeviceIdType.MESH)
            mine = pltpu.make_async_remote_copy(
                src_ref=src.at[me if gather else peer_blk], dst_ref=land.at[me], **sems)
            theirs = pltpu.make_async_remote_copy(src_ref=src.at[me], dst_ref=land.at[peer_blk], **sems)
            pairs.append((mine, theirs))
    return pairs


DATAFLOW = pltpu.SideEffectType.DATAFLOW_SIDE_EFFECTING


def _in_hbm(a):
    return pltpu.with_memory_space_constraint(a, pltpu.HBM)


def _copies_start(srcs, lands, *, gather, name, deps=()):
    arrs = list(lands) if gather else list(srcs) + list(lands)
    n, k, nd = len(arrs), len(lands), len(deps)

    def body(*refs):
        ins, send_sems, recv_sems, token = refs[:n], refs[n + nd], refs[n + nd + 1], refs[2 * n + nd + 2]
        src_refs, land_refs = (ins, ins) if gather else (ins[:k], ins[k:])
        for mine, _ in _peer_copies(src_refs, land_refs, send_sems, recv_sems, gather):
            mine.start()
        token[...] = jnp.zeros_like(token)

    n_cp = k * (N_DEV - 1)
    return pl.pallas_call(
        body,
        name=name,
        in_specs=[HBM_SPEC] * n + [ANY_SPEC] * nd,
        out_specs=(SEM_SPEC, SEM_SPEC, *[HBM_SPEC] * n, pl.BlockSpec(memory_space=pltpu.VMEM)),
        out_shape=(pltpu.SemaphoreType.DMA((n_cp,)), pltpu.SemaphoreType.DMA((n_cp,)),
                   *[pltpu.HBM(a.shape, a.dtype) for a in arrs], jax.ShapeDtypeStruct((8, 128), F32)),
        input_output_aliases={i: 2 + i for i in range(n)},
        compiler_params=pltpu.CompilerParams(has_side_effects=DATAFLOW),
    )(*[_in_hbm(a) for a in arrs], *deps)


def _copies_wait(arrs, send_sems, recv_sems, after, *, n_lands, gather, name):
    n, k = len(arrs), n_lands

    def body(*refs):
        ins, send_sems, recv_sems = refs[:n], refs[n], refs[n + 1]
        src_refs, land_refs = (ins, ins) if gather else (ins[:k], ins[k:])
        for mine, theirs in _peer_copies(src_refs, land_refs, send_sems, recv_sems, gather):
            mine.wait_send()
            theirs.wait_recv()

    outs = pl.pallas_call(
        body,
        name=name,
        in_specs=[HBM_SPEC] * n + [SEM_SPEC, SEM_SPEC] + [ANY_SPEC] * len(after),
        out_specs=[HBM_SPEC] * n,
        out_shape=[pltpu.HBM(a.shape, a.dtype) for a in arrs],
        input_output_aliases={i: i for i in range(n)},
        compiler_params=pltpu.CompilerParams(has_side_effects=DATAFLOW),
    )(*arrs, send_sems, recv_sems, *after)
    return outs[n - k:]


def _exchange_small(slabs):
    n = len(slabs)

    def body(*refs):
        ins, outs = refs[:n], refs[n:2 * n]
        send_sems, recv_sems, local_sems = refs[2 * n:]
        x, y, c, me = _mesh_pos()
        own = [pltpu.make_async_copy(src, dst.at[me], local_sems.at[k]) for k, (src, dst) in enumerate(zip(ins, outs))]
        for cp in own:
            cp.start()
        sends, recvs = [], []
        for r in range(1, N_DEV):
            peer, peer_blk = _peer(x, y, c, r)
            for k, (src, dst) in enumerate(zip(ins, outs)):
                idx = k * (N_DEV - 1) + r - 1
                sems = dict(send_sem=send_sems.at[idx], recv_sem=recv_sems.at[idx], device_id=peer,
                            device_id_type=pl.DeviceIdType.MESH)
                send = pltpu.make_async_remote_copy(src_ref=src, dst_ref=dst.at[me], **sems)
                send.start()
                sends.append(send)
                recvs.append(pltpu.make_async_remote_copy(src_ref=src, dst_ref=dst.at[peer_blk], **sems))
        for cp in recvs:
            cp.wait_recv()
        for cp in sends:
            cp.wait_send()
        for cp in own:
            cp.wait()

    n_cp = n * (N_DEV - 1)
    return pl.pallas_call(
        body,
        name="exchange_small_grads",
        in_specs=[ANY_SPEC] * n,
        out_specs=[ANY_SPEC] * n,
        out_shape=[jax.ShapeDtypeStruct((N_DEV,) + s.shape, F32) for s in slabs],
        scratch_shapes=[pltpu.SemaphoreType.DMA((n_cp,)), pltpu.SemaphoreType.DMA((n_cp,)),
                        pltpu.SemaphoreType.DMA((n,))],
    )(*slabs)


def _adamw(w, g, m, v):
    m = ADAM_B1 * m + (1.0 - ADAM_B1) * g
    v = ADAM_B2 * v + (1.0 - ADAM_B2) * (g * g)
    m_hat = m / (1.0 - ADAM_B1 ** ADAM_STEP)
    v_hat = v / (1.0 - ADAM_B2 ** ADAM_STEP)
    return -ADAM_LR * (m_hat / (jnp.sqrt(v_hat) + ADAM_EPS) + ADAM_WD * w), m, v


ADAM_TC = 256


def _adam_big(slots, w, m, v, cuts_columns, *, name):
    layers, n, nj = len(slots), slots[0].shape[1], D_MODEL // ADAM_TC

    def body(*refs):
        s_refs = refs[:layers]
        w_ref, m_ref, v_ref, g_ref, d_ref, nm_ref, nv_ref, acc_ref = refs[layers:]
        for ll in range(layers):
            @pl.when(pl.program_id(0) == ll)
            def _(s_ref=s_refs[ll]):
                g = s_ref[0].astype(F32)
                for s in range(1, N_DEV):
                    g = g + s_ref[s].astype(F32)
                acc_ref[...] = g

        g = acc_ref[...].T if cuts_columns else acc_ref[...]
        g_ref[...] = g
        d_ref[...], nm_ref[...], nv_ref[...] = _adamw(w_ref[...], g, m_ref[...], v_ref[...])

    def slot_spec(ll):
        return pl.BlockSpec((N_DEV, n, ADAM_TC),
                            lambda l, j: (0, 0, jnp.where(l < ll, 0, jnp.where(l > ll, nj - 1, j))))

    if cuts_columns:
        w_spec = pl.BlockSpec((None, ADAM_TC, n), lambda l, j: (l, j, 0))
    else:
        w_spec = pl.BlockSpec((None, n, ADAM_TC), lambda l, j: (l, 0, j))
    return pl.pallas_call(
        body,
        name=name,
        grid=(layers, nj),
        in_specs=[slot_spec(ll) for ll in range(layers)] + [w_spec] * 3,
        out_specs=[w_spec] * 4,
        out_shape=[jax.ShapeDtypeStruct(w.shape, F32)] * 4,
        scratch_shapes=[pltpu.VMEM((n, ADAM_TC), F32)],
        compiler_params=_cp("arbitrary", "arbitrary"),
    )(*slots, w, m, v)


def _adam_slabs(slots, ws, ms, vs):
    n = len(slots)

    def body(*refs):
        ins, outs = refs[:4 * n], refs[4 * n:]
        for k in range(n):
            s_ref, w_ref, m_ref, v_ref = ins[k], ins[n + k], ins[2 * n + k], ins[3 * n + k]
            g = s_ref[0]
            for s in range(1, N_DEV):
                g = g + s_ref[s]
            outs[4 * k][...] = g
            outs[4 * k + 1][...], outs[4 * k + 2][...], outs[4 * k + 3][...] = _adamw(w_ref[...], g, m_ref[...], v_ref[...])

    res = pl.pallas_call(
        body,
        name="small_adamw",
        out_shape=[jax.ShapeDtypeStruct(w.shape, F32) for w in ws for _ in range(4)],
        compiler_params=pltpu.CompilerParams(vmem_limit_bytes=VMEM_LIMIT_BYTES),
    )(*slots, *ws, *ms, *vs)
    return [res[4 * k:4 * k + 4] for k in range(n)]


def _adam_vecs(gs, ws, ms, vs):
    n = len(gs)

    def body(*refs):
        ins, outs = refs[:4 * n], refs[4 * n:]
        for k in range(n):
            outs[3 * k][...], outs[3 * k + 1][...], outs[3 * k + 2][...] = _adamw(
                ins[n + k][...], ins[k][...], ins[2 * n + k][...], ins[3 * n + k][...])

    res = pl.pallas_call(
        body,
        name="ln_adamw",
        out_shape=[jax.ShapeDtypeStruct(w.shape, F32) for w in ws for _ in range(3)],
        compiler_params=pltpu.CompilerParams(vmem_limit_bytes=VMEM_LIMIT_BYTES),
    )(*gs, *ws, *ms, *vs)
    return [res[3 * k:3 * k + 3] for k in range(n)]


SLAB_AT = dict(mem_norm=0, lb_logits=1, ffn1_norm=4, mix_norm=6, hgrn_gnorm=8, gmlp_ln_g=9, gmlp_ln_b=11,
               gmlp_b_s=13, ffn2_norm=14, final_norm=16)
SLAB_ROWS = 24
SMALL_SHARDED = ("gmlp_ln_g", "gmlp_ln_b")


def _pack_slab(parts, *, name):
    flat, plan = [], []
    for pname, at in SLAB_AT.items():
        for a in parts.get(pname, ()):
            flat.append(a)
            plan.append((at, a.shape))
            at += max(1, a.shape[0] * a.shape[1] // D_MODEL)

    def body(*refs):
        o_ref = refs[-1]
        o_ref[...] = jnp.zeros_like(o_ref)
        for ref, (at, (r, w)) in zip(refs, plan):
            if w == D_MODEL or r == 1 and w < D_MODEL:
                o_ref[at:at + r, 0:w] = ref[...]
            elif w < D_MODEL:
                for j in range(r):
                    o_ref[at:at + 1, j * w:(j + 1) * w] = ref[j:j + 1, :]
            else:
                for j in range(w // D_MODEL):
                    o_ref[at + j:at + j + 1, :] = ref[:, j * D_MODEL:(j + 1) * D_MODEL]

    return pl.pallas_call(
        body,
        name=name,
        out_shape=jax.ShapeDtypeStruct((SLAB_ROWS, D_MODEL), F32),
        compiler_params=pltpu.CompilerParams(vmem_limit_bytes=VMEM_LIMIT_BYTES),
    )(*flat)


def _unpack_slab(slab, shapes):
    out = {}
    for pname, at in SLAB_AT.items():
        if pname in SMALL_SHARDED:
            continue
        size = math.prod(shapes[pname])
        rows = max(1, size // D_MODEL)
        out[pname] = slab[at:at + rows].reshape(-1)[:size].reshape(shapes[pname])
    return out


def _ffn_fwd(x, norm_g, block, layer, full, get_weights):
    tag = f"l{layer}_{block}"
    full.update(get_weights((layer, f"{block}_in"), (x,)))
    h = _rms_fwd(x, norm_g, name=f"{tag}_norm")
    z = _mm(h, full[(f"{block}_w_in", layer)], tb=True, tm=1024, tn=512, tk=D_MODEL, out_dtype=F32, name=f"{tag}_in")
    act = _swiglu_fwd(z, name=f"{tag}_act")
    full.update(get_weights((layer, f"{block}_out"), (act,)))
    y = _mm(act, full[(f"{block}_w_out", layer)], tm=512, tn=D_MODEL, tk=D_FF, out_dtype=F32, res=x, scale=0.5,
            name=f"{tag}_out")
    return y, (x, h, z, act)


def _ffn_bwd(dy, saved, norm_g, w_in_t, w_out, tag, deps=()):
    x, h, z, act = saved
    dw_out = _mm(act, dy, ta=True, tm=1408, tn=D_MODEL, tk=1024, out_dtype=BF16, scale=0.5, deps=deps,
                 name=f"{tag}_out_wgrad")
    dact = _mm(dy, w_out, tb=True, tm=1024, tn=1408, tk=D_MODEL, out_dtype=F32, name=f"{tag}_out_dgrad")
    dz = _swiglu_bwd(z, dact, scale=0.5, name=f"{tag}_act_bwd")
    dw_in_t = _mm(dz, h, ta=True, tm=512, tn=D_MODEL, tk=1024, out_dtype=BF16, name=f"{tag}_in_wgrad")
    dh = _mm(dz, w_in_t, tm=1024, tn=D_MODEL, tk=512, out_dtype=F32, name=f"{tag}_in_dgrad")
    dx, dg = _rms_bwd(x, norm_g, dh, dy, name=f"{tag}_norm_bwd")
    return dx, dg, dw_in_t, dw_out


def kernel(x, mem, mem_norm, lb_logits, ffn1_norm, ffn1_w_in, ffn1_w_out, mix_norm, mem_w_kv, hgrn_w_in, hgrn_gnorm, hgrn_w_out, gmlp_w_in, gmlp_ln_g, gmlp_ln_b, gmlp_w_s, gmlp_b_s, gmlp_w_out, ffn2_norm, ffn2_w_in, ffn2_w_out, final_norm, loss_target, m_mem_norm, m_lb_logits, m_ffn1_norm, m_ffn1_w_in, m_ffn1_w_out, m_mix_norm, m_mem_w_kv, m_hgrn_w_in, m_hgrn_gnorm, m_hgrn_w_out, m_gmlp_w_in, m_gmlp_ln_g, m_gmlp_ln_b, m_gmlp_w_s, m_gmlp_b_s, m_gmlp_w_out, m_ffn2_norm, m_ffn2_w_in, m_ffn2_w_out, m_final_norm, v_mem_norm, v_lb_logits, v_ffn1_norm, v_ffn1_w_in, v_ffn1_w_out, v_mix_norm, v_mem_w_kv, v_hgrn_w_in, v_hgrn_gnorm, v_hgrn_w_out, v_gmlp_w_in, v_gmlp_ln_g, v_gmlp_ln_b, v_gmlp_w_s, v_gmlp_b_s, v_gmlp_w_out, v_ffn2_norm, v_ffn2_w_in, v_ffn2_w_out, v_final_norm):
    weights = dict(mem_norm=mem_norm, lb_logits=lb_logits, ffn1_norm=ffn1_norm, ffn1_w_in=ffn1_w_in, ffn1_w_out=ffn1_w_out, mix_norm=mix_norm, mem_w_kv=mem_w_kv, hgrn_w_in=hgrn_w_in, hgrn_gnorm=hgrn_gnorm, hgrn_w_out=hgrn_w_out, gmlp_w_in=gmlp_w_in, gmlp_ln_g=gmlp_ln_g, gmlp_ln_b=gmlp_ln_b, gmlp_w_s=gmlp_w_s, gmlp_b_s=gmlp_b_s, gmlp_w_out=gmlp_w_out, ffn2_norm=ffn2_norm, ffn2_w_in=ffn2_w_in, ffn2_w_out=ffn2_w_out, final_norm=final_norm)
    mom_m = dict(mem_norm=m_mem_norm, lb_logits=m_lb_logits, ffn1_norm=m_ffn1_norm, ffn1_w_in=m_ffn1_w_in, ffn1_w_out=m_ffn1_w_out, mix_norm=m_mix_norm, mem_w_kv=m_mem_w_kv, hgrn_w_in=m_hgrn_w_in, hgrn_gnorm=m_hgrn_gnorm, hgrn_w_out=m_hgrn_w_out, gmlp_w_in=m_gmlp_w_in, gmlp_ln_g=m_gmlp_ln_g, gmlp_ln_b=m_gmlp_ln_b, gmlp_w_s=m_gmlp_w_s, gmlp_b_s=m_gmlp_b_s, gmlp_w_out=m_gmlp_w_out, ffn2_norm=m_ffn2_norm, ffn2_w_in=m_ffn2_w_in, ffn2_w_out=m_ffn2_w_out, final_norm=m_final_norm)
    mom_v = dict(mem_norm=v_mem_norm, lb_logits=v_lb_logits, ffn1_norm=v_ffn1_norm, ffn1_w_in=v_ffn1_w_in, ffn1_w_out=v_ffn1_w_out, mix_norm=v_mix_norm, mem_w_kv=v_mem_w_kv, hgrn_w_in=v_hgrn_w_in, hgrn_gnorm=v_hgrn_gnorm, hgrn_w_out=v_hgrn_w_out, gmlp_w_in=v_gmlp_w_in, gmlp_ln_g=v_gmlp_ln_g, gmlp_ln_b=v_gmlp_ln_b, gmlp_w_s=v_gmlp_w_s, gmlp_b_s=v_gmlp_b_s, gmlp_w_out=v_gmlp_w_out, ffn2_norm=v_ffn2_norm, ffn2_w_in=v_ffn2_w_in, ffn2_w_out=v_ffn2_w_out, final_norm=v_final_norm)
    order = list(weights)
    _, _, _, me = _mesh_pos()
    me_arr = jnp.reshape(me, (1,)).astype(jnp.int32)
    cuts = {name: c for name, c, _, _ in GROUPS}

    mix1 = (("mem_w_kv", 1), ("gmlp_w_in", 0), ("gmlp_w_out", 0))
    gather_plan = (
        ((0, "ffn1_in"), (("ffn1_w_in", 0),), None),
        ((0, "ffn1_out"), (("ffn1_w_out", 0),), 0),
        ((0, "mix_in"), (("mem_w_kv", 0), ("hgrn_w_in", 0)), 0),
        ((0, "mix_out"), (("hgrn_w_out", 0),), 2),
        ((0, "ffn2_in"), _stage_pieces(0, "ffn2"), 2),
        ((1, "ffn1_in"), _stage_pieces(1, "ffn1"), 2),
        ((1, "mix_in"), mix1, 4),
        ((1, "ffn2_in"), _stage_pieces(1, "ffn2"), 5),
    )
    gather = {}

    def start_gather(k, deps):
        use, pieces, _ = gather_plan[k]
        lands = [_place_rows(weights[name], l, cuts[name], me_arr, name=f"place_{name}_{l}") for name, l in pieces]
        if pieces is mix1:
            lands.append(_place_ln(gmlp_ln_g, gmlp_ln_b, me_arr))
        send_sems, recv_sems, *thru, _ = _copies_start(lands, lands, gather=True, deps=deps,
                                                       name=f"gather_start_l{use[0]}_{use[1]}")
        gather[use] = (k, thru, send_sems, recv_sems)

    start_gather(0, ())

    def get_weights(use, after):
        if use not in gather:
            return {}
        k, thru, send_sems, recv_sems = gather[use]
        outs = _copies_wait(thru, send_sems, recv_sems, after, n_lands=len(thru), gather=True,
                            name=f"gather_wait_l{use[0]}_{use[1]}")
        for later, (_, _, trigger) in enumerate(gather_plan):
            if trigger == k:
                start_gather(later, (outs[0],))
        pieces = gather_plan[k][1]
        w = {p: o.reshape(N_DEV * o.shape[1], D_MODEL) for p, o in zip(pieces, outs)}
        if pieces is mix1:
            w["ln_g"] = outs[-1][:, 0, :].reshape(1, GM_WIDTH)
            w["ln_b"] = outs[-1][:, 1, :].reshape(1, GM_WIDTH)
        return w

    scatter = {}

    def put_grads(st, grads):
        if st == "w_s":
            land = _place_slab(grads.reshape(GM_GROUPS * GM_CHUNK, GM_CHUNK), me_arr, name="w_s_place")
            send_sems, recv_sems, *thru, token = _copies_start([land], [land], gather=True, name="w_s_start")
            scatter[st] = (thru, send_sems, recv_sems)
            return (token,)
        views = [grads[p].reshape(N_DEV, -1, D_MODEL) for p in _stage_pieces(*st)]
        recv = _place_own(views, me_arr, name=f"scatter_place_l{st[0]}_{st[1]}")
        send_sems, recv_sems, *thru, token = _copies_start(views, recv, gather=False,
                                                           name=f"scatter_start_l{st[0]}_{st[1]}")
        scatter[st] = (thru, send_sems, recv_sems)
        return (token,)

    dx, small, loss_part = _step_local(
        x, mem, loss_target, get_weights, put_grads, mem_norm, lb_logits, ffn1_norm, mix_norm, hgrn_gnorm,
        gmlp_w_s, gmlp_b_s, ffn2_norm, final_norm)

    def slots_of(blk, after):
        slots = {}
        for i in (1, 0):
            thru, send_sems, recv_sems = scatter[(i, blk)]
            outs = _copies_wait(thru, send_sems, recv_sems, after, n_lands=len(thru) // 2, gather=False,
                                name=f"scatter_wait_l{i}_{blk}")
            slots.update(zip(_stage_pieces(i, blk), outs))
        return slots

    grad, delta, new_m, new_v = {}, {}, {}, {}

    def adam_groups(slots, names):
        for name in names:
            layers = GROUP_LAYERS[name]
            grad[name], delta[name], new_m[name], new_v[name] = _adam_big(
                [slots[(name, l)] for l in range(layers)], weights[name], mom_m[name], mom_v[name], cuts[name],
                name=f"{name}_adamw")

    adam_groups(slots_of("ffn2", (dx,)), ("ffn2_w_in", "ffn2_w_out"))
    adam_groups(slots_of("mix", (delta["ffn2_w_out"],)),
                ("mem_w_kv", "gmlp_w_in", "gmlp_w_out", "hgrn_w_in", "hgrn_w_out"))

    def small_parts(src):
        parts = {n: [src[n].reshape(-1, src[n].shape[-1])] for n in SLAB_AT if n not in SMALL_SHARDED}
        return parts

    w_s_rows = lambda a: a.reshape(GM_GROUPS * GM_CHUNK, GM_CHUNK)
    (slab_slots,) = _exchange_small([_pack_slab(small, name="pack_small_grads")])
    thru, send_sems, recv_sems = scatter["w_s"]
    (ws_slots,) = _copies_wait(thru, send_sems, recv_sems, (slab_slots,), n_lands=1, gather=True, name="w_s_wait")
    (g_slab, d_slab, nm_slab, nv_slab), (g_ws, d_ws, nm_ws, nv_ws) = _adam_slabs(
        [slab_slots, ws_slots],
        [_pack_slab(small_parts(weights), name="pack_small_w"), w_s_rows(gmlp_w_s)],
        [_pack_slab(small_parts(mom_m), name="pack_small_m"), w_s_rows(m_gmlp_w_s)],
        [_pack_slab(small_parts(mom_v), name="pack_small_v"), w_s_rows(v_gmlp_w_s)])
    shapes = {n: weights[n].shape for n in SLAB_AT}
    for out, slab, ws in ((grad, g_slab, g_ws), (delta, d_slab, d_ws), (new_m, nm_slab, nm_ws), (new_v, nv_slab, nv_ws)):
        out.update(_unpack_slab(slab, shapes))
        out["gmlp_w_s"] = ws.reshape(gmlp_w_s.shape)
    blk = GM_WIDTH // N_DEV
    g_ln = [lax.dynamic_slice(g_slab[SLAB_AT[n]:SLAB_AT[n] + 2].reshape(1, GM_WIDTH), (0, me * blk), (1, blk))
            for n in SMALL_SHARDED]
    ln_out = _adam_vecs(g_ln, [weights[n] for n in SMALL_SHARDED], [mom_m[n] for n in SMALL_SHARDED],
                        [mom_v[n] for n in SMALL_SHARDED])
    for n, g, (d, nm, nv) in zip(SMALL_SHARDED, g_ln, ln_out):
        grad[n], delta[n], new_m[n], new_v[n] = g, d, nm, nv

    adam_groups(slots_of("ffn1", (delta["hgrn_w_out"], d_slab)), ("ffn1_w_in", "ffn1_w_out"))

    loss = lax.psum(loss_part[0, 0], MESH_AXES)
    grad_x = dx.reshape(B_LOC, SEQ, D_MODEL)
    return (loss, grad_x, *[grad[n] for n in order], *[delta[n] for n in order],
            *[new_m[n] for n in order], *[new_v[n] for n in order])


def _step_local(x, mem, loss_target, get_weights, put_grads, mem_norm, lb_logits, ffn1_norm, mix_norm, hgrn_gnorm,
                gmlp_w_s, gmlp_b_s, ffn2_norm, final_norm):
    w_s = gmlp_w_s[0]
    b_st = gmlp_b_s[0].T

    xs = x.reshape(N_TOK, D_MODEL)
    mem2d = mem.reshape(B_LOC * MEM_LEN, D_MODEL)
    mem_g = mem_norm.reshape(1, D_MODEL)
    saved, full = [], {}
    memn = _rms_fwd(mem2d, mem_g, name="mem_norm_fwd")
    for i in range(2):
        xs, s_ffn1 = _ffn_fwd(xs, ffn1_norm[i:i + 1], "ffn1", i, full, get_weights)
        full.update(get_weights((i, "mix_in"), (xs,)))
        mixer = "hgrn" if i == 0 else "gmlp"
        hm = _rms_fwd(xs, mix_norm[i:i + 1], name=f"l{i}_mix_norm")
        kv = _mm(memn, full[("mem_w_kv", i)], tb=True, tm=512, tn=512, tk=D_MODEL, out_dtype=F32, name=f"l{i}_mem_kv")
        zm = _mm(hm, full[(f"{mixer}_w_in", 0)], tb=True, tm=1024, tn=512, tk=D_MODEL, out_dtype=F32, name=f"l{i}_mix_in")
        if i == 0:
            o_mix, o_pre, s_all = _hgrn_fwd(zm, lb_logits, hgrn_gnorm)
            mix_saved = (o_pre, s_all)
        else:
            o_mix = _gmlp_fwd(zm, full["ln_g"], full["ln_b"], w_s, b_st)
            mix_saved = ()
        o_mem = _attn_fwd(zm, kv, name=f"l{i}_attn")
        cat = jnp.concatenate([o_mix, o_mem], axis=1)
        x_mix = xs
        full.update(get_weights((i, "mix_out"), (cat,)))
        xs = _mm(cat, full[(f"{mixer}_w_out", 0)], tm=512, tn=D_MODEL, tk=cat.shape[1], out_dtype=F32, res=xs,
                 name=f"l{i}_mix_out")
        xs, s_ffn2 = _ffn_fwd(xs, ffn2_norm[i:i + 1], "ffn2", i, full, get_weights)
        saved.append((s_ffn1, (x_mix, hm, kv, zm, cat, mix_saved), s_ffn2))

    dx, d_final, loss_part = _loss_head(xs, final_norm.reshape(1, D_MODEL), loss_target.reshape(N_TOK, D_MODEL))

    small = {"final_norm": [d_final]}
    d_ffn1, d_ffn2, d_mix = [None, None], [None, None], [None, None]
    dmemn = jnp.zeros((B_LOC * MEM_LEN, D_MODEL), F32)
    deps = ()
    for i in (1, 0):
        s_ffn1, (x_mix, hm, kv, zm, cat, mix_saved), s_ffn2 = saved[i]
        dx, d_ffn2[i], dw_in_t, dw_out = _ffn_bwd(
            dx, s_ffn2, ffn2_norm[i:i + 1], full[("ffn2_w_in", i)], full[("ffn2_w_out", i)], f"l{i}_ffn2", deps)
        deps = put_grads((i, "ffn2"), {("ffn2_w_in", i): dw_in_t, ("ffn2_w_out", i): dw_out})
        mixer = "hgrn" if i == 0 else "gmlp"
        w_in_t, w_out = full[(f"{mixer}_w_in", 0)], full[(f"{mixer}_w_out", 0)]
        width = cat.shape[1]
        g_mix = {}
        g_mix[(f"{mixer}_w_out", 0)] = _mm(cat, dx, ta=True, tm=width // 2, tn=D_MODEL, tk=1024, out_dtype=BF16,
                                           deps=deps, name=f"l{i}_mix_out_wgrad")
        dcat = _mm(dx, w_out, tb=True, tm=1024, tn=width // 2, tk=D_MODEL, out_dtype=F32, name=f"l{i}_mix_out_dgrad")
        dq, dk, dv = _attn_bwd(zm, kv, dcat, do_off=width - XA_HEADS * XA_DIM, name=f"l{i}_attn_bwd")
        if i == 0:
            dzq, dzf, dzi, dzg, dlbl, dgn = _hgrn_bwd(zm, mix_saved[0], dcat, mix_saved[1], lb_logits, hgrn_gnorm)
            small["lb_logits"], small["hgrn_gnorm"] = [dlbl], [dgn]
            dzm = jnp.concatenate([dzq, dzf, dzi, dzg, dq], axis=1)
            deps = ()
        else:
            dzu, dzv, dws, dbt, dlng, dlnb = _gmlp_bwd(zm, dcat, full["ln_g"], full["ln_b"], w_s, b_st)
            small["gmlp_b_s"], small["gmlp_ln_g"], small["gmlp_ln_b"] = [dbt.T], [dlng], [dlnb]
            deps = put_grads("w_s", dws)
            dzm = jnp.concatenate([dzu, dzv, dq], axis=1)
        g_mix[(f"{mixer}_w_in", 0)] = _mm(dzm, hm, ta=True, tm=512, tn=D_MODEL, tk=1024, out_dtype=BF16, deps=deps,
                                          name=f"l{i}_mix_in_wgrad")
        dkv = jnp.concatenate([dk, dv], axis=1)
        g_mix[("mem_w_kv", i)] = _mm(dkv, memn, ta=True, tm=512, tn=D_MODEL, tk=B_LOC * MEM_LEN, out_dtype=BF16,
                                     name=f"l{i}_mem_kv_wgrad")
        deps = put_grads((i, "mix"), g_mix)
        dh = _mm(dzm, w_in_t, tm=1024, tn=D_MODEL, tk=512, out_dtype=F32, deps=deps, name=f"l{i}_mix_in_dgrad")
        dx, d_mix[i] = _rms_bwd(x_mix, mix_norm[i:i + 1], dh, dx, name=f"l{i}_mix_norm_bwd")
        dmemn = _mm(dkv, full[("mem_w_kv", i)], tm=B_LOC * MEM_LEN, tn=D_MODEL, tk=512, out_dtype=F32, res=dmemn,
                    name=f"l{i}_mem_kv_dgrad")
        dx, d_ffn1[i], dw_in_t, dw_out = _ffn_bwd(
            dx, s_ffn1, ffn1_norm[i:i + 1], full[("ffn1_w_in", i)], full[("ffn1_w_out", i)], f"l{i}_ffn1")
        deps = put_grads((i, "ffn1"), {("ffn1_w_in", i): dw_in_t, ("ffn1_w_out", i): dw_out})
    _, dmem_g = _rms_bwd(mem2d, mem_g, dmemn, dmemn, deps=deps, name="mem_norm_bwd")
    small.update(mem_norm=[dmem_g], ffn1_norm=d_ffn1, ffn2_norm=d_ffn2, mix_norm=d_mix)
    return dx, small, loss_part
```

```python
import functools
import math

import jax
import jax.numpy as jnp
from jax import lax
from jax.experimental import pallas as pl
from jax.experimental.pallas import tpu as pltpu

F32 = jnp.float32
BF16 = jnp.bfloat16

D_MODEL = 1024
SEQ = 2048
B_LOC = 2
N_TOK = B_LOC * SEQ
MEM_LEN = 256
N_DEV = 8
EPS = 1e-6
D_FF = 2816
HG_HEADS = 8
HG_DIM = 128
HG_CHUNK = 64
HG_NCHUNK = SEQ // HG_CHUNK
GM_CHUNK = 128
GM_GROUPS = 8
GM_WIDTH = 2048
GM_GDIM = GM_WIDTH // GM_GROUPS
XA_HEADS = 4
XA_DIM = 256
XA_OFF = 4096

ADAM_LR = 0.001
ADAM_B1 = 0.9
ADAM_B2 = 0.999
ADAM_EPS = 1e-08
ADAM_WD = 0.01
ADAM_STEP = 10

VMEM_LIMIT_BYTES = 56 * 1024 * 1024
MESH_AXES = ("x", "y", "c")

GROUPS = (
    ("ffn1_w_in", True, 2, 704),
    ("ffn1_w_out", False, 2, 352),
    ("mem_w_kv", True, 2, 256),
    ("hgrn_w_in", True, 1, 640),
    ("hgrn_w_out", False, 1, 256),
    ("gmlp_w_in", True, 1, 640),
    ("gmlp_w_out", False, 1, 384),
    ("ffn2_w_in", True, 2, 704),
    ("ffn2_w_out", False, 2, 352),
)
GROUP_LAYERS = {name: layers for name, _, layers, _ in GROUPS}


def _stage_pieces(layer, block):
    if block == "mix":
        mixer = "hgrn" if layer == 0 else "gmlp"
        return (("mem_w_kv", layer), (f"{mixer}_w_in", 0), (f"{mixer}_w_out", 0))
    return ((f"{block}_w_in", layer), (f"{block}_w_out", layer))


ANY_SPEC = pl.BlockSpec(memory_space=pl.ANY)
HBM_SPEC = pl.BlockSpec(memory_space=pltpu.HBM)
SEM_SPEC = pl.BlockSpec(memory_space=pltpu.SEMAPHORE)


def _cp(*sem):
    return pltpu.CompilerParams(dimension_semantics=sem, vmem_limit_bytes=VMEM_LIMIT_BYTES)


def _sigmoid(x):
    return 1.0 / (1.0 + jnp.exp(-x))


def _gelu_parts(x):
    cdf = 0.5 * (1.0 + lax.erf(x * (1.0 / math.sqrt(2.0))))
    pdf = jnp.exp(-0.5 * x * x) * (1.0 / math.sqrt(2.0 * math.pi))
    return x * cdf, cdf + x * pdf


def _mm(a, b, *, ta=False, tb=False, tm, tn, tk, out_dtype, res=None, scale=1.0, deps=(), name):
    m, k = (a.shape[1], a.shape[0]) if ta else a.shape
    n, kb = b.shape if tb else (b.shape[1], b.shape[0])
    assert k == kb and m % tm == 0 and n % tn == 0 and k % tk == 0, (name, a.shape, b.shape)
    nk = k // tk
    dn = (((0 if ta else 1,), (1 if tb else 0,)), ((), ()))
    n_in = 2 + (res is not None) + len(deps)

    def body(*refs):
        a_ref, b_ref = refs[:2]
        r_ref = refs[2] if res is not None else None
        o_ref, scr = refs[n_in], refs[n_in + 1:]
        p = lax.dot_general(a_ref[...].astype(BF16), b_ref[...].astype(BF16), dn, preferred_element_type=F32)

        def finish(acc):
            if scale != 1.0:
                acc = scale * acc
            if r_ref is not None:
                acc = r_ref[...] + acc
            o_ref[...] = acc.astype(out_dtype)

        if nk == 1:
            finish(p)
        else:
            acc_ref = scr[0]
            kk = pl.program_id(2)

            @pl.when(kk == 0)
            def _():
                acc_ref[...] = p

            @pl.when(kk > 0)
            def _():
                acc_ref[...] += p

            @pl.when(kk == nk - 1)
            def _():
                finish(acc_ref[...])

    a_spec = pl.BlockSpec((tk, tm), lambda i, j, kk: (kk, i)) if ta else pl.BlockSpec((tm, tk), lambda i, j, kk: (i, kk))
    b_spec = pl.BlockSpec((tn, tk), lambda i, j, kk: (j, kk)) if tb else pl.BlockSpec((tk, tn), lambda i, j, kk: (kk, j))
    o_spec = pl.BlockSpec((tm, tn), lambda i, j, kk: (i, j))
    in_specs = [a_spec, b_spec] + ([o_spec] if res is not None else []) + [ANY_SPEC] * len(deps)
    args = (a, b) + ((res,) if res is not None else ()) + tuple(deps)
    return pl.pallas_call(
        body,
        name=name,
        grid=(m // tm, n // tn, nk),
        in_specs=in_specs,
        out_specs=o_spec,
        out_shape=jax.ShapeDtypeStruct((m, n), out_dtype),
        scratch_shapes=[pltpu.VMEM((tm, tn), F32)] if nk > 1 else [],
        compiler_params=_cp("parallel", "parallel", "arbitrary"),
    )(*args)


def _rms_fwd(x, g, *, name, deps=(), tm=512):
    rows = x.shape[0]

    def body(x_ref, g_ref, *rest):
        o_ref = rest[len(deps)]
        xv = x_ref[...]
        r = lax.rsqrt(jnp.mean(xv * xv, axis=-1, keepdims=True) + EPS)
        o_ref[...] = (xv * r * g_ref[...]).astype(BF16)

    row = pl.BlockSpec((tm, D_MODEL), lambda i: (i, 0))
    return pl.pallas_call(
        body,
        name=name,
        grid=(rows // tm,),
        in_specs=[row, pl.BlockSpec((1, D_MODEL), lambda i: (0, 0))] + [ANY_SPEC] * len(deps),
        out_specs=row,
        out_shape=jax.ShapeDtypeStruct((rows, D_MODEL), BF16),
        compiler_params=_cp("parallel"),
    )(x, g, *deps)


def _rms_bwd(x, g, dh, dres, *, name, deps=(), tm=512):
    rows = x.shape[0]

    def body(x_ref, g_ref, dh_ref, dres_ref, *rest):
        dx_ref, dg_ref = rest[len(deps):]
        xv = x_ref[...]
        r = lax.rsqrt(jnp.mean(xv * xv, axis=-1, keepdims=True) + EPS)
        xhat = xv * r
        dhv = dh_ref[...]
        part = jnp.sum(dhv * xhat, axis=0, keepdims=True)

        @pl.when(pl.program_id(0) == 0)
        def _():
            dg_ref[...] = part

        @pl.when(pl.program_id(0) > 0)
        def _():
            dg_ref[...] += part

        dxh = dhv * g_ref[...]
        dx_ref[...] = dres_ref[...] + r * (dxh - xhat * jnp.mean(dxh * xhat, axis=-1, keepdims=True))

    row = pl.BlockSpec((tm, D_MODEL), lambda i: (i, 0))
    vec = pl.BlockSpec((1, D_MODEL), lambda i: (0, 0))
    return pl.pallas_call(
        body,
        name=name,
        grid=(rows // tm,),
        in_specs=[row, vec, row, row] + [ANY_SPEC] * len(deps),
        out_specs=[row, vec],
        out_shape=[jax.ShapeDtypeStruct((rows, D_MODEL), F32), jax.ShapeDtypeStruct((1, D_MODEL), F32)],
        compiler_params=_cp("arbitrary"),
    )(x, g, dh, dres, *deps)


def _swiglu_fwd(z, *, name, tm=512):
    def body(g_ref, u_ref, o_ref):
        gv = g_ref[...]
        o_ref[...] = (gv * _sigmoid(gv) * u_ref[...]).astype(BF16)

    return pl.pallas_call(
        body,
        name=name,
        grid=(N_TOK // tm,),
        in_specs=[pl.BlockSpec((tm, D_FF), lambda i: (i, 0)), pl.BlockSpec((tm, D_FF), lambda i: (i, 1))],
        out_specs=pl.BlockSpec((tm, D_FF), lambda i: (i, 0)),
        out_shape=jax.ShapeDtypeStruct((N_TOK, D_FF), BF16),
        compiler_params=_cp("parallel"),
    )(z, z)


def _swiglu_bwd(z, dact, *, scale, name, tm=512):
    def body(g_ref, u_ref, da_ref, o_ref):
        gv = g_ref[...]
        s = _sigmoid(gv)
        da = da_ref[...] * scale
        o_ref[:, :D_FF] = (da * u_ref[...] * (s * (1.0 + gv * (1.0 - s)))).astype(BF16)
        o_ref[:, D_FF:] = (da * (gv * s)).astype(BF16)

    half = lambda j: pl.BlockSpec((tm, D_FF), lambda i: (i, j))
    return pl.pallas_call(
        body,
        name=name,
        grid=(N_TOK // tm,),
        in_specs=[half(0), half(1), half(0)],
        out_specs=pl.BlockSpec((tm, 2 * D_FF), lambda i: (i, 0)),
        out_shape=jax.ShapeDtypeStruct((N_TOK, 2 * D_FF), BF16),
        compiler_params=_cp("parallel"),
    )(z, z, dact)


def _loss_head(x, g, target, *, tm=512):
    def body(x_ref, g_ref, t_ref, dx_ref, dg_ref, loss_ref):
        xv = x_ref[...]
        gv = g_ref[...]
        r = lax.rsqrt(jnp.mean(xv * xv, axis=-1, keepdims=True) + EPS)
        xhat = xv * r
        err = xhat * gv - t_ref[...]
        loss_part = jnp.zeros((1, 128), F32) + 0.5 * jnp.sum(jnp.mean(err * err, axis=-1, keepdims=True))
        dy = err * (1.0 / D_MODEL)
        dg_part = jnp.sum(dy * xhat, axis=0, keepdims=True)

        @pl.when(pl.program_id(0) == 0)
        def _():
            dg_ref[...] = dg_part
            loss_ref[...] = loss_part

        @pl.when(pl.program_id(0) > 0)
        def _():
            dg_ref[...] += dg_part
            loss_ref[...] += loss_part

        dxh = dy * gv
        dx_ref[...] = r * (dxh - xhat * jnp.mean(dxh * xhat, axis=-1, keepdims=True))

    row = pl.BlockSpec((tm, D_MODEL), lambda i: (i, 0))
    vec = pl.BlockSpec((1, D_MODEL), lambda i: (0, 0))
    return pl.pallas_call(
        body,
        name="loss_head",
        grid=(N_TOK // tm,),
        in_specs=[row, vec, row],
        out_specs=[row, vec, pl.BlockSpec((1, 128), lambda i: (0, 0))],
        out_shape=[
            jax.ShapeDtypeStruct((N_TOK, D_MODEL), F32),
            jax.ShapeDtypeStruct((1, D_MODEL), F32),
            jax.ShapeDtypeStruct((1, 128), F32),
        ],
        compiler_params=_cp("arbitrary"),
    )(x, g, target)


_NT = (((1,), (1,)), ((), ()))
_TN = (((0,), (0,)), ((), ()))
XA_TQ = 1024
XA_SCALE = XA_DIM ** -0.5


def _attn_probs(q16, k16):
    s = lax.dot_general(q16, k16, _NT, preferred_element_type=F32) * XA_SCALE
    e = jnp.exp(s - jnp.max(s, axis=-1, keepdims=True))
    return e / jnp.sum(e, axis=-1, keepdims=True)


def _attn_fwd(z, kv, *, name):
    nt = SEQ // XA_TQ

    def body(q_ref, k_ref, v_ref, o_ref):
        p = _attn_probs(q_ref[...].astype(BF16), k_ref[...].astype(BF16))
        o_ref[...] = jnp.dot(p.astype(BF16), v_ref[...].astype(BF16), preferred_element_type=F32).astype(BF16)

    return pl.pallas_call(
        body,
        name=name,
        grid=(B_LOC, XA_HEADS, nt),
        in_specs=[
            pl.BlockSpec((XA_TQ, XA_DIM), lambda b, h, t: (b * nt + t, XA_OFF // XA_DIM + h)),
            pl.BlockSpec((MEM_LEN, XA_DIM), lambda b, h, t: (b, h)),
            pl.BlockSpec((MEM_LEN, XA_DIM), lambda b, h, t: (b, XA_HEADS + h)),
        ],
        out_specs=pl.BlockSpec((XA_TQ, XA_DIM), lambda b, h, t: (b * nt + t, h)),
        out_shape=jax.ShapeDtypeStruct((N_TOK, XA_HEADS * XA_DIM), BF16),
        compiler_params=_cp("parallel", "parallel", "arbitrary"),
    )(z, kv, kv)


def _attn_bwd(z, kv, dcat, *, do_off, name):
    nt = SEQ // XA_TQ

    def body(q_ref, k_ref, v_ref, do_ref, dq_ref, dk_ref, dv_ref):
        q16 = q_ref[...].astype(BF16)
        k16 = k_ref[...].astype(BF16)
        v16 = v_ref[...].astype(BF16)
        do16 = do_ref[...].astype(BF16)
        p = _attn_probs(q16, k16)
        dv_part = lax.dot_general(p.astype(BF16), do16, _TN, preferred_element_type=F32)
        dp = lax.dot_general(do16, v16, _NT, preferred_element_type=F32)
        ds16 = (p * (dp - jnp.sum(dp * p, axis=-1, keepdims=True)) * XA_SCALE).astype(BF16)
        dq_ref[...] = jnp.dot(ds16, k16, preferred_element_type=F32).astype(BF16)
        dk_part = lax.dot_general(ds16, q16, _TN, preferred_element_type=F32)

        @pl.when(pl.program_id(2) == 0)
        def _():
            dk_ref[...] = dk_part
            dv_ref[...] = dv_part

        @pl.when(pl.program_id(2) > 0)
        def _():
            dk_ref[...] += dk_part
            dv_ref[...] += dv_part

    qspec = pl.BlockSpec((XA_TQ, XA_DIM), lambda b, h, t: (b * nt + t, XA_OFF // XA_DIM + h))
    kspec = lambda off: pl.BlockSpec((MEM_LEN, XA_DIM), lambda b, h, t: (b, off + h))
    return pl.pallas_call(
        body,
        name=name,
        grid=(B_LOC, XA_HEADS, nt),
        in_specs=[qspec, kspec(0), kspec(XA_HEADS),
                  pl.BlockSpec((XA_TQ, XA_DIM), lambda b, h, t: (b * nt + t, do_off // XA_DIM + h))],
        out_specs=[pl.BlockSpec((XA_TQ, XA_DIM), lambda b, h, t: (b * nt + t, h)), kspec(0), kspec(0)],
        out_shape=[
            jax.ShapeDtypeStruct((N_TOK, XA_HEADS * XA_DIM), BF16),
            jax.ShapeDtypeStruct((B_LOC * MEM_LEN, XA_HEADS * XA_DIM), F32),
            jax.ShapeDtypeStruct((B_LOC * MEM_LEN, XA_HEADS * XA_DIM), F32),
        ],
        compiler_params=_cp("parallel", "parallel", "arbitrary"),
    )(z, kv, kv, dcat)


def _tril(n):
    return lax.broadcasted_iota(jnp.int32, (n, n), 0) >= lax.broadcasted_iota(jnp.int32, (n, n), 1)


def _lower_bound(lbl):
    e = jnp.exp(lbl - jnp.max(lbl, axis=0, keepdims=True))
    p = e / jnp.sum(e, axis=0, keepdims=True)
    return p[0:1, :], p


def _hgrn_gates(zq, zf, lb, tril_f):
    sig = _sigmoid(zf)
    f = lb + (1.0 - lb) * sig
    kk = 1.0 - f
    sq = _sigmoid(zq)
    q = zq * sq
    b = jnp.dot(tril_f, jnp.log(f), preferred_element_type=F32, precision=lax.Precision.HIGHEST)
    bl = b[HG_CHUNK - 1:HG_CHUNK, :]
    return q, sq, sig, f, kk, b, bl


def _hgrn_zspec(section):
    return pl.BlockSpec((SEQ, HG_DIM), lambda h, b: (b, section * HG_HEADS + h))


def _hgrn_fwd(z, lb_logits, gnorm):
    def body(zq_ref, zf_ref, zi_ref, zg_ref, lbl_ref, gn_ref, o_ref, opre_ref, sall_ref, st_ref):
        lb, _ = _lower_bound(lbl_ref[...])
        gn = gn_ref[...]
        mask = _tril(HG_CHUNK)
        tril_f = mask.astype(F32)
        st_ref[...] = jnp.zeros_like(st_ref)

        def chunk(c, carry):
            rows = pl.ds(pl.multiple_of(c * HG_CHUNK, HG_CHUNK), HG_CHUNK)
            q, _, _, _, kk, b, bl = _hgrn_gates(zq_ref[rows, :], zf_ref[rows, :], lb, tril_f)
            v16 = zi_ref[rows, :].astype(BF16)
            qd16 = (q * jnp.exp(b)).astype(BF16)
            ki16 = (kk * jnp.exp(-b)).astype(BF16)
            kd16 = (kk * jnp.exp(bl - b)).astype(BF16)
            a = jnp.where(mask, lax.dot_general(qd16, ki16, _NT, preferred_element_type=F32), 0.0)
            st = st_ref[...]
            sall_ref[0, 0, c] = st
            o = jnp.dot(a.astype(BF16), v16, preferred_element_type=F32) + lax.dot_general(
                qd16, st.astype(BF16), _NT, preferred_element_type=F32)
            st_ref[...] = st * jnp.exp(bl) + lax.dot_general(v16, kd16, _TN, preferred_element_type=F32)
            opre_ref[rows, :] = o
            r = lax.rsqrt(jnp.mean(o * o, axis=-1, keepdims=True) + EPS)
            zg = zg_ref[rows, :]
            o_ref[rows, :] = ((o * r * gn) * (zg * _sigmoid(zg))).astype(BF16)
            return carry

        lax.fori_loop(0, HG_NCHUNK, chunk, 0)

    tok = pl.BlockSpec((SEQ, HG_DIM), lambda h, b: (b, h))
    return pl.pallas_call(
        body,
        name="hgrn_fwd",
        grid=(HG_HEADS, B_LOC),
        in_specs=[_hgrn_zspec(0), _hgrn_zspec(1), _hgrn_zspec(2), _hgrn_zspec(3),
                  pl.BlockSpec((3, HG_DIM), lambda h, b: (0, h)), pl.BlockSpec((1, HG_DIM), lambda h, b: (0, 0))],
        out_specs=[tok, tok, pl.BlockSpec((1, 1, HG_NCHUNK, HG_DIM, HG_DIM), lambda h, b: (b, h, 0, 0, 0))],
        out_shape=[
            jax.ShapeDtypeStruct((N_TOK, HG_HEADS * HG_DIM), BF16),
            jax.ShapeDtypeStruct((N_TOK, HG_HEADS * HG_DIM), F32),
            jax.ShapeDtypeStruct((B_LOC, HG_HEADS, HG_NCHUNK, HG_DIM, HG_DIM), F32),
        ],
        scratch_shapes=[pltpu.VMEM((HG_DIM, HG_DIM), F32)],
        compiler_params=_cp("parallel", "arbitrary"),
    )(z, z, z, z, lb_logits, gnorm)


def _hgrn_bwd(z, opre, dcat, sall, lb_logits, gnorm):
    def body(zq_ref, zf_ref, zi_ref, zg_ref, opre_ref, dout_ref, sall_ref, lbl_ref, gn_ref,
             dzq_ref, dzf_ref, dzi_ref, dzg_ref, dlbl_ref, dgn_ref, dst_ref, dlb_ref):
        h_id, b_id = pl.program_id(0), pl.program_id(1)
        lb, p = _lower_bound(lbl_ref[...])
        gn = gn_ref[...]
        mask = _tril(HG_CHUNK)
        tril_f = mask.astype(F32)
        dst_ref[...] = jnp.zeros_like(dst_ref)
        dlb_ref[...] = jnp.zeros_like(dlb_ref)

        @pl.when((h_id == 0) & (b_id == 0))
        def _():
            dgn_ref[...] = jnp.zeros_like(dgn_ref)

        def chunk(i, carry):
            c = HG_NCHUNK - 1 - i
            rows = pl.ds(pl.multiple_of(c * HG_CHUNK, HG_CHUNK), HG_CHUNK)
            zq, zg = zq_ref[rows, :], zg_ref[rows, :]
            q, sq, sig, f, kk, b, bl = _hgrn_gates(zq, zf_ref[rows, :], lb, tril_f)
            v16 = zi_ref[rows, :].astype(BF16)
            eb, enb, ebl_b, ebl = jnp.exp(b), jnp.exp(-b), jnp.exp(bl - b), jnp.exp(bl)
            qd, ki, kd = q * eb, kk * enb, kk * ebl_b
            qd16, ki16, kd16 = qd.astype(BF16), ki.astype(BF16), kd.astype(BF16)
            o = opre_ref[rows, :]
            dout = dout_ref[rows, :]
            r = lax.rsqrt(jnp.mean(o * o, axis=-1, keepdims=True) + EPS)
            ohat = o * r
            sg = _sigmoid(zg)
            d_on = dout * (zg * sg)
            dzg_ref[rows, :] = (dout * (ohat * gn) * (sg * (1.0 + zg * (1.0 - sg)))).astype(BF16)
            dgn_ref[...] += jnp.sum(d_on * ohat, axis=0, keepdims=True)
            dohat = d_on * gn
            do16 = (r * (dohat - ohat * jnp.mean(dohat * ohat, axis=-1, keepdims=True))).astype(BF16)
            st = sall_ref[0, 0, c]
            st16 = st.astype(BF16)
            dst = dst_ref[...]
            dst16 = dst.astype(BF16)
            a16 = jnp.where(mask, lax.dot_general(qd16, ki16, _NT, preferred_element_type=F32), 0.0).astype(BF16)
            da16 = jnp.where(mask, lax.dot_general(do16, v16, _NT, preferred_element_type=F32), 0.0).astype(BF16)
            dv = lax.dot_general(a16, do16, _TN, preferred_element_type=F32) + lax.dot_general(
                kd16, dst16, _NT, preferred_element_type=F32)
            dqd = jnp.dot(da16, ki16, preferred_element_type=F32) + jnp.dot(do16, st16, preferred_element_type=F32)
            dki = lax.dot_general(da16, qd16, _TN, preferred_element_type=F32)
            dkd = jnp.dot(v16, dst16, preferred_element_type=F32)
            dbl = jnp.sum(dkd * kd, axis=0, keepdims=True) + ebl * jnp.sum(st * dst, axis=0, keepdims=True)
            dst_ref[...] = dst * ebl + lax.dot_general(do16, qd16, _TN, preferred_element_type=F32)
            dzi_ref[rows, :] = dv.astype(BF16)
            dzq_ref[rows, :] = (dqd * eb * (sq * (1.0 + zq * (1.0 - sq)))).astype(BF16)
            dkk = dki * enb + dkd * ebl_b
            db = dqd * qd - dki * ki - dkd * kd
            dlogf = lax.dot_general(tril_f, db, _TN, preferred_element_type=F32, precision=lax.Precision.HIGHEST) + dbl
            df = dlogf / f - dkk
            dzf_ref[rows, :] = (df * (1.0 - lb) * sig * (1.0 - sig)).astype(BF16)
            dlb_ref[...] += jnp.sum(df * (1.0 - sig), axis=0, keepdims=True)
            return carry

        lax.fori_loop(0, HG_NCHUNK, chunk, 0)
        row0 = (lax.broadcasted_iota(jnp.int32, (3, HG_DIM), 0) == 0).astype(F32)
        dlbl_part = dlb_ref[...] * lb * (row0 - p)

        @pl.when(b_id == 0)
        def _():
            dlbl_ref[...] = dlbl_part

        @pl.when(b_id > 0)
        def _():
            dlbl_ref[...] += dlbl_part

    tok = pl.BlockSpec((SEQ, HG_DIM), lambda h, b: (b, h))
    tok_shape = jax.ShapeDtypeStruct((N_TOK, HG_HEADS * HG_DIM), BF16)
    return pl.pallas_call(
        body,
        name="hgrn_bwd",
        grid=(HG_HEADS, B_LOC),
        in_specs=[_hgrn_zspec(0), _hgrn_zspec(1), _hgrn_zspec(2), _hgrn_zspec(3), tok, tok,
                  pl.BlockSpec((1, 1, HG_NCHUNK, HG_DIM, HG_DIM), lambda h, b: (b, h, 0, 0, 0)),
                  pl.BlockSpec((3, HG_DIM), lambda h, b: (0, h)), pl.BlockSpec((1, HG_DIM), lambda h, b: (0, 0))],
        out_specs=[tok, tok, tok, tok, pl.BlockSpec((3, HG_DIM), lambda h, b: (0, h)),
                   pl.BlockSpec((1, HG_DIM), lambda h, b: (0, 0))],
        out_shape=[tok_shape, tok_shape, tok_shape, tok_shape,
                   jax.ShapeDtypeStruct((3, HG_HEADS * HG_DIM), F32), jax.ShapeDtypeStruct((1, HG_DIM), F32)],
        scratch_shapes=[pltpu.VMEM((HG_DIM, HG_DIM), F32), pltpu.VMEM((1, HG_DIM), F32)],
        compiler_params=_cp("arbitrary", "arbitrary"),
    )(z, z, z, z, opre, dcat, sall, lb_logits, gnorm)


GM_TM = 256


def _gmlp_norm(zv, ln_g, ln_b):
    gv, dgelu = _gelu_parts(zv)
    xc = gv - jnp.mean(gv, axis=-1, keepdims=True)
    rstd = lax.rsqrt(jnp.mean(xc * xc, axis=-1, keepdims=True) + EPS)
    vhat = xc * rstd
    return vhat * ln_g + ln_b, vhat, rstd, dgelu


def _gmlp_specs():
    half = lambda j: pl.BlockSpec((GM_TM, GM_WIDTH), lambda i: (i, j))
    vec = pl.BlockSpec((1, GM_WIDTH), lambda i: (0, 0))
    w = pl.BlockSpec((GM_GROUPS, GM_CHUNK, GM_CHUNK), lambda i: (0, 0, 0))
    bt = pl.BlockSpec((GM_CHUNK, GM_GROUPS), lambda i: (0, 0))
    return half, vec, w, bt


def _gmlp_fwd(z, ln_g, ln_b, w_s, b_st):
    def body(zu_ref, zv_ref, g_ref, b_ref, w_ref, bt_ref, o_ref):
        u, _ = _gelu_parts(zu_ref[...])
        v, _, _, _ = _gmlp_norm(zv_ref[...], g_ref[...], b_ref[...])
        v16 = v.astype(BF16)
        mask = _tril(GM_CHUNK)
        bt = bt_ref[...]
        for g in range(GM_GROUPS):
            wm16 = jnp.where(mask, w_ref[g], 0.0).astype(BF16)
            cols = slice(g * GM_GDIM, (g + 1) * GM_GDIM)
            for c in range(GM_TM // GM_CHUNK):
                rows = slice(c * GM_CHUNK, (c + 1) * GM_CHUNK)
                mixed = jnp.dot(wm16, v16[rows, cols], preferred_element_type=F32) + bt[:, g:g + 1]
                o_ref[rows, cols] = (u[rows, cols] * mixed).astype(BF16)

    half, vec, w, bt = _gmlp_specs()
    return pl.pallas_call(
        body,
        name="gmlp_fwd",
        grid=(N_TOK // GM_TM,),
        in_specs=[half(0), half(1), vec, vec, w, bt],
        out_specs=half(0),
        out_shape=jax.ShapeDtypeStruct((N_TOK, GM_WIDTH), BF16),
        compiler_params=_cp("parallel"),
    )(z, z, ln_g, ln_b, w_s, b_st)


def _gmlp_bwd(z, dcat, ln_g, ln_b, w_s, b_st):
    def body(zu_ref, zv_ref, dout_ref, g_ref, b_ref, w_ref, bt_ref,
             dzu_ref, dzv_ref, dw_ref, dbt_ref, dg_ref, db_ref, dv_ref):
        @pl.when(pl.program_id(0) == 0)
        def _():
            dw_ref[...] = jnp.zeros_like(dw_ref)
            dbt_ref[...] = jnp.zeros_like(dbt_ref)
            dg_ref[...] = jnp.zeros_like(dg_ref)
            db_ref[...] = jnp.zeros_like(db_ref)

        zu = zu_ref[...]
        u, du_dz = _gelu_parts(zu)
        ln_g = g_ref[...]
        v, vhat, rstd, dgv_dz = _gmlp_norm(zv_ref[...], ln_g, b_ref[...])
        v16 = v.astype(BF16)
        dout = dout_ref[...]
        dmixed = dout * u
        dm16 = dmixed.astype(BF16)
        mask = _tril(GM_CHUNK)
        bt = bt_ref[...]
        group_id = lax.broadcasted_iota(jnp.int32, (1, GM_GROUPS), 1)
        dbt = jnp.zeros((GM_CHUNK, GM_GROUPS), F32)
        for g in range(GM_GROUPS):
            wm16 = jnp.where(mask, w_ref[g], 0.0).astype(BF16)
            cols = slice(g * GM_GDIM, (g + 1) * GM_GDIM)
            dw = jnp.zeros((GM_CHUNK, GM_CHUNK), F32)
            dbt_g = jnp.zeros((GM_CHUNK, 1), F32)
            for c in range(GM_TM // GM_CHUNK):
                rows = slice(c * GM_CHUNK, (c + 1) * GM_CHUNK)
                mixed = jnp.dot(wm16, v16[rows, cols], preferred_element_type=F32) + bt[:, g:g + 1]
                dzu_ref[rows, cols] = (dout[rows, cols] * mixed * du_dz[rows, cols]).astype(BF16)
                dw += lax.dot_general(dm16[rows, cols], v16[rows, cols], _NT, preferred_element_type=F32)
                dbt_g += jnp.sum(dmixed[rows, cols], axis=-1, keepdims=True)
                dv_ref[rows, cols] = lax.dot_general(wm16, dm16[rows, cols], _TN, preferred_element_type=F32)
            dw_ref[g] += jnp.where(mask, dw, 0.0)
            dbt = dbt + dbt_g * (group_id == g).astype(F32)
        dbt_ref[...] += dbt
        dv = dv_ref[...]
        dg_ref[...] += jnp.sum(dv * vhat, axis=0, keepdims=True)
        db_ref[...] += jnp.sum(dv, axis=0, keepdims=True)
        dvh = dv * ln_g
        dgv = rstd * (dvh - jnp.mean(dvh, axis=-1, keepdims=True) - vhat * jnp.mean(dvh * vhat, axis=-1, keepdims=True))
        dzv_ref[...] = (dgv * dgv_dz).astype(BF16)

    half, vec, w, bt = _gmlp_specs()
    tok_shape = jax.ShapeDtypeStruct((N_TOK, GM_WIDTH), BF16)
    return pl.pallas_call(
        body,
        name="gmlp_bwd",
        grid=(N_TOK // GM_TM,),
        in_specs=[half(0), half(1), half(0), vec, vec, w, bt],
        out_specs=[half(0), half(0), w, bt, vec, vec],
        out_shape=[tok_shape, tok_shape, jax.ShapeDtypeStruct((GM_GROUPS, GM_CHUNK, GM_CHUNK), F32),
                   jax.ShapeDtypeStruct((GM_CHUNK, GM_GROUPS), F32),
                   jax.ShapeDtypeStruct((1, GM_WIDTH), F32), jax.ShapeDtypeStruct((1, GM_WIDTH), F32)],
        scratch_shapes=[pltpu.VMEM((GM_TM, GM_WIDTH), F32)],
        compiler_params=_cp("arbitrary"),
    )(z, z, dcat, ln_g, ln_b, w_s, b_st)


def _own_slot(shape):
    return pl.BlockSpec((None,) + tuple(shape), lambda i, me_ref: (me_ref[0],) + (0,) * len(shape))


def _place_rows(w, layer, cuts_columns, me, *, name):
    _, r, c = w.shape
    n = c if cuts_columns else r

    def body(me_ref, w_ref, o_ref):
        wv = w_ref[...]
        o_ref[...] = (wv.T if cuts_columns else wv).astype(BF16)

    return pl.pallas_call(
        body,
        name=name,
        grid_spec=pltpu.PrefetchScalarGridSpec(
            num_scalar_prefetch=1, grid=(1,),
            in_specs=[pl.BlockSpec((None, r, c), lambda i, me_ref: (layer, 0, 0))],
            out_specs=_own_slot((n, D_MODEL))),
        out_shape=jax.ShapeDtypeStruct((N_DEV, n, D_MODEL), BF16),
        compiler_params=_cp("arbitrary"),
    )(me, w)


def _place_ln(ln_g, ln_b, me):
    blk = ln_g.shape[1]

    def body(me_ref, g_ref, b_ref, o_ref):
        o_ref[...] = jnp.zeros_like(o_ref)
        o_ref[0:1, :] = g_ref[...]
        o_ref[1:2, :] = b_ref[...]

    vec = pl.BlockSpec((1, blk), lambda i, me_ref: (0, 0))
    return pl.pallas_call(
        body,
        name="place_ln",
        grid_spec=pltpu.PrefetchScalarGridSpec(
            num_scalar_prefetch=1, grid=(1,), in_specs=[vec, vec], out_specs=_own_slot((8, blk))),
        out_shape=jax.ShapeDtypeStruct((N_DEV, 8, blk), F32),
        compiler_params=_cp("arbitrary"),
    )(me, ln_g, ln_b)


def _place_slab(a, me, *, name):
    def body(me_ref, a_ref, o_ref):
        o_ref[...] = a_ref[...]

    return pl.pallas_call(
        body,
        name=name,
        grid_spec=pltpu.PrefetchScalarGridSpec(
            num_scalar_prefetch=1, grid=(1,),
            in_specs=[pl.BlockSpec(a.shape, lambda i, me_ref: (0, 0))], out_specs=_own_slot(a.shape)),
        out_shape=jax.ShapeDtypeStruct((N_DEV,) + a.shape, a.dtype),
        compiler_params=_cp("arbitrary"),
    )(me, a)


def _place_own(grads, me, *, name):
    k = len(grads)

    def body(me_ref, *refs):
        for src, dst in zip(refs[:k], refs[k:]):
            dst[...] = src[...]

    specs = [_own_slot(g.shape[1:]) for g in grads]
    return pl.pallas_call(
        body,
        name=name,
        grid_spec=pltpu.PrefetchScalarGridSpec(num_scalar_prefetch=1, grid=(1,), in_specs=specs, out_specs=specs),
        out_shape=[jax.ShapeDtypeStruct(g.shape, g.dtype) for g in grads],
        compiler_params=_cp("arbitrary"),
    )(me, *grads)


def _mesh_pos():
    x, y, c = (lax.axis_index(a) for a in MESH_AXES)
    return x, y, c, 4 * x + 2 * y + c


def _peer(x, y, c, r):
    px = 1 - x if r & 4 else x
    py = 1 - y if r & 2 else y
    pc = 1 - c if r & 1 else c
    return (px, py, pc), 4 * px + 2 * py + pc


def _peer_copies(srcs, lands, send_sems, recv_sems, gather):
    x, y, c, me = _mesh_pos()
    pairs = []
    for r in range(1, N_DEV):
        peer, peer_blk = _peer(x, y, c, r)
        for k, (src, land) in enumerate(zip(srcs, lands)):
            idx = k * (N_DEV - 1) + r - 1
            sems = dict(send_sem=send_sems.at[idx], recv_sem=recv_sems.at[idx], device_id=peer,
                        device_id_type=pl.DeviceIdType.MESH)
            mine = pltpu.make_async_remote_copy(
                src_ref=src.at[me if gather else peer_blk], dst_ref=land.at[me], **sems)
            theirs = pltpu.make_async_remote_copy(src_ref=src.at[me], dst_ref=land.at[peer_blk], **sems)
            pairs.append((mine, theirs))
    return pairs


DATAFLOW = pltpu.SideEffectType.DATAFLOW_SIDE_EFFECTING


def _in_hbm(a):
    return pltpu.with_memory_space_constraint(a, pltpu.HBM)


def _copies_start(srcs, lands, *, gather, name, deps=()):
    arrs = list(lands) if gather else list(srcs) + list(lands)
    n, k, nd = len(arrs), len(lands), len(deps)

    def body(*refs):
        ins, send_sems, recv_sems, token = refs[:n], refs[n + nd], refs[n + nd + 1], refs[2 * n + nd + 2]
        src_refs, land_refs = (ins, ins) if gather else (ins[:k], ins[k:])
        for mine, _ in _peer_copies(src_refs, land_refs, send_sems, recv_sems, gather):
            mine.start()
        token[...] = jnp.zeros_like(token)

    n_cp = k * (N_DEV - 1)
    return pl.pallas_call(
        body,
        name=name,
        in_specs=[HBM_SPEC] * n + [ANY_SPEC] * nd,
        out_specs=(SEM_SPEC, SEM_SPEC, *[HBM_SPEC] * n, pl.BlockSpec(memory_space=pltpu.VMEM)),
        out_shape=(pltpu.SemaphoreType.DMA((n_cp,)), pltpu.SemaphoreType.DMA((n_cp,)),
                   *[pltpu.HBM(a.shape, a.dtype) for a in arrs], jax.ShapeDtypeStruct((8, 128), F32)),
        input_output_aliases={i: 2 + i for i in range(n)},
        compiler_params=pltpu.CompilerParams(has_side_effects=DATAFLOW),
    )(*[_in_hbm(a) for a in arrs], *deps)


def _copies_wait(arrs, send_sems, recv_sems, after, *, n_lands, gather, name):
    n, k = len(arrs), n_lands

    def body(*refs):
        ins, send_sems, recv_sems = refs[:n], refs[n], refs[n + 1]
        src_refs, land_refs = (ins, ins) if gather else (ins[:k], ins[k:])
        for mine, theirs in _peer_copies(src_refs, land_refs, send_sems, recv_sems, gather):
            mine.wait_send()
            theirs.wait_recv()

    outs = pl.pallas_call(
        body,
        name=name,
        in_specs=[HBM_SPEC] * n + [SEM_SPEC, SEM_SPEC] + [ANY_SPEC] * len(after),
        out_specs=[HBM_SPEC] * n,
        out_shape=[pltpu.HBM(a.shape, a.dtype) for a in arrs],
        input_output_aliases={i: i for i in range(n)},
        compiler_params=pltpu.CompilerParams(has_side_effects=DATAFLOW),
    )(*arrs, send_sems, recv_sems, *after)
    return outs[n - k:]


def _exchange_small(slabs):
    n = len(slabs)

    def body(*refs):
        ins, outs = refs[:n], refs[n:2 * n]
        send_sems, recv_sems, local_sems = refs[2 * n:]
        x, y, c, me = _mesh_pos()
        own = [pltpu.make_async_copy(src, dst.at[me], local_sems.at[k]) for k, (src, dst) in enumerate(zip(ins, outs))]
        for cp in own:
            cp.start()
        sends, recvs = [], []
        for r in range(1, N_DEV):
            peer, peer_blk = _peer(x, y, c, r)
            for k, (src, dst) in enumerate(zip(ins, outs)):
                idx = k * (N_DEV - 1) + r - 1
                sems = dict(send_sem=send_sems.at[idx], recv_sem=recv_sems.at[idx], device_id=peer,
                            device_id_type=pl.DeviceIdType.MESH)
                send = pltpu.make_async_remote_copy(src_ref=src, dst_ref=dst.at[me], **sems)
                send.start()
                sends.append(send)
                recvs.append(pltpu.make_async_remote_copy(src_ref=src, dst_ref=dst.at[peer_blk], **sems))
        for cp in recvs:
            cp.wait_recv()
        for cp in sends:
            cp.wait_send()
        for cp in own:
            cp.wait()

    n_cp = n * (N_DEV - 1)
    return pl.pallas_call(
        body,
        name="exchange_small_grads",
        in_specs=[ANY_SPEC] * n,
        out_specs=[ANY_SPEC] * n,
        out_shape=[jax.ShapeDtypeStruct((N_DEV,) + s.shape, F32) for s in slabs],
        scratch_shapes=[pltpu.SemaphoreType.DMA((n_cp,)), pltpu.SemaphoreType.DMA((n_cp,)),
                        pltpu.SemaphoreType.DMA((n,))],
    )(*slabs)


def _adamw(w, g, m, v):
    m = ADAM_B1 * m + (1.0 - ADAM_B1) * g
    v = ADAM_B2 * v + (1.0 - ADAM_B2) * (g * g)
    m_hat = m / (1.0 - ADAM_B1 ** ADAM_STEP)
    v_hat = v / (1.0 - ADAM_B2 ** ADAM_STEP)
    return -ADAM_LR * (m_hat / (jnp.sqrt(v_hat) + ADAM_EPS) + ADAM_WD * w), m, v


ADAM_TC = 256


def _adam_big(slots, w, m, v, cuts_columns, *, name):
    layers, n, nj = len(slots), slots[0].shape[1], D_MODEL // ADAM_TC

    def body(*refs):
        s_refs = refs[:layers]
        w_ref, m_ref, v_ref, g_ref, d_ref, nm_ref, nv_ref, acc_ref = refs[layers:]
        for ll in range(layers):
            @pl.when(pl.program_id(0) == ll)
            def _(s_ref=s_refs[ll]):
                g = s_ref[0].astype(F32)
                for s in range(1, N_DEV):
                    g = g + s_ref[s].astype(F32)
                acc_ref[...] = g

        g = acc_ref[...].T if cuts_columns else acc_ref[...]
        g_ref[...] = g
        d_ref[...], nm_ref[...], nv_ref[...] = _adamw(w_ref[...], g, m_ref[...], v_ref[...])

    def slot_spec(ll):
        return pl.BlockSpec((N_DEV, n, ADAM_TC),
                            lambda l, j: (0, 0, jnp.where(l < ll, 0, jnp.where(l > ll, nj - 1, j))))

    if cuts_columns:
        w_spec = pl.BlockSpec((None, ADAM_TC, n), lambda l, j: (l, j, 0))
    else:
        w_spec = pl.BlockSpec((None, n, ADAM_TC), lambda l, j: (l, 0, j))
    return pl.pallas_call(
        body,
        name=name,
        grid=(layers, nj),
        in_specs=[slot_spec(ll) for ll in range(layers)] + [w_spec] * 3,
        out_specs=[w_spec] * 4,
        out_shape=[jax.ShapeDtypeStruct(w.shape, F32)] * 4,
        scratch_shapes=[pltpu.VMEM((n, ADAM_TC), F32)],
        compiler_params=_cp("arbitrary", "arbitrary"),
    )(*slots, w, m, v)


def _adam_slabs(slots, ws, ms, vs):
    n = len(slots)

    def body(*refs):
        ins, outs = refs[:4 * n], refs[4 * n:]
        for k in range(n):
            s_ref, w_ref, m_ref, v_ref = ins[k], ins[n + k], ins[2 * n + k], ins[3 * n + k]
            g = s_ref[0]
            for s in range(1, N_DEV):
                g = g + s_ref[s]
            outs[4 * k][...] = g
            outs[4 * k + 1][...], outs[4 * k + 2][...], outs[4 * k + 3][...] = _adamw(w_ref[...], g, m_ref[...], v_ref[...])

    res = pl.pallas_call(
        body,
        name="small_adamw",
        out_shape=[jax.ShapeDtypeStruct(w.shape, F32) for w in ws for _ in range(4)],
        compiler_params=pltpu.CompilerParams(vmem_limit_bytes=VMEM_LIMIT_BYTES),
    )(*slots, *ws, *ms, *vs)
    return [res[4 * k:4 * k + 4] for k in range(n)]


def _adam_vecs(gs, ws, ms, vs):
    n = len(gs)

    def body(*refs):
        ins, outs = refs[:4 * n], refs[4 * n:]
        for k in range(n):
            outs[3 * k][...], outs[3 * k + 1][...], outs[3 * k + 2][...] = _adamw(
                ins[n + k][...], ins[k][...], ins[2 * n + k][...], ins[3 * n + k][...])

    res = pl.pallas_call(
        body,
        name="ln_adamw",
        out_shape=[jax.ShapeDtypeStruct(w.shape, F32) for w in ws for _ in range(3)],
        compiler_params=pltpu.CompilerParams(vmem_limit_bytes=VMEM_LIMIT_BYTES),
    )(*gs, *ws, *ms, *vs)
    return [res[3 * k:3 * k + 3] for k in range(n)]


SLAB_AT = dict(mem_norm=0, lb_logits=1, ffn1_norm=4, mix_norm=6, hgrn_gnorm=8, gmlp_ln_g=9, gmlp_ln_b=11,
               gmlp_b_s=13, ffn2_norm=14, final_norm=16)
SLAB_ROWS = 24
SMALL_SHARDED = ("gmlp_ln_g", "gmlp_ln_b")


def _pack_slab(parts, *, name):
    flat, plan = [], []
    for pname, at in SLAB_AT.items():
        for a in parts.get(pname, ()):
            flat.append(a)
            plan.append((at, a.shape))
            at += max(1, a.shape[0] * a.shape[1] // D_MODEL)

    def body(*refs):
        o_ref = refs[-1]
        o_ref[...] = jnp.zeros_like(o_ref)
        for ref, (at, (r, w)) in zip(refs, plan):
            if w == D_MODEL or r == 1 and w < D_MODEL:
                o_ref[at:at + r, 0:w] = ref[...]
            elif w < D_MODEL:
                for j in range(r):
                    o_ref[at:at + 1, j * w:(j + 1) * w] = ref[j:j + 1, :]
            else:
                for j in range(w // D_MODEL):
                    o_ref[at + j:at + j + 1, :] = ref[:, j * D_MODEL:(j + 1) * D_MODEL]

    return pl.pallas_call(
        body,
        name=name,
        out_shape=jax.ShapeDtypeStruct((SLAB_ROWS, D_MODEL), F32),
        compiler_params=pltpu.CompilerParams(vmem_limit_bytes=VMEM_LIMIT_BYTES),
    )(*flat)


def _unpack_slab(slab, shapes):
    out = {}
    for pname, at in SLAB_AT.items():
        if pname in SMALL_SHARDED:
            continue
        size = math.prod(shapes[pname])
        rows = max(1, size // D_MODEL)
        out[pname] = slab[at:at + rows].reshape(-1)[:size].reshape(shapes[pname])
    return out


def _ffn_fwd(x, norm_g, block, layer, full, get_weights):
    tag = f"l{layer}_{block}"
    full.update(get_weights((layer, f"{block}_in"), (x,)))
    h = _rms_fwd(x, norm_g, deps=full.pop("deps", ()), name=f"{tag}_norm")
    z = _mm(h, full[(f"{block}_w_in", layer)], tb=True, tm=1024, tn=512, tk=D_MODEL, out_dtype=F32, name=f"{tag}_in")
    act = _swiglu_fwd(z, name=f"{tag}_act")
    full.update(get_weights((layer, f"{block}_out"), (act,)))
    y = _mm(act, full[(f"{block}_w_out", layer)], tm=512, tn=D_MODEL, tk=D_FF, out_dtype=F32, res=x, scale=0.5,
            deps=full.pop("deps", ()), name=f"{tag}_out")
    return y, (x, h, z, act)


def _ffn_bwd(dy, saved, norm_g, w_in_t, w_out, tag, deps=()):
    x, h, z, act = saved
    dw_out = _mm(act, dy, ta=True, tm=1408, tn=D_MODEL, tk=1024, out_dtype=BF16, scale=0.5, deps=deps,
                 name=f"{tag}_out_wgrad")
    dact = _mm(dy, w_out, tb=True, tm=1024, tn=1408, tk=D_MODEL, out_dtype=F32, name=f"{tag}_out_dgrad")
    dz = _swiglu_bwd(z, dact, scale=0.5, name=f"{tag}_act_bwd")
    dw_in_t = _mm(dz, h, ta=True, tm=512, tn=D_MODEL, tk=1024, out_dtype=BF16, name=f"{tag}_in_wgrad")
    dh = _mm(dz, w_in_t, tm=1024, tn=D_MODEL, tk=512, out_dtype=F32, name=f"{tag}_in_dgrad")
    dx, dg = _rms_bwd(x, norm_g, dh, dy, name=f"{tag}_norm_bwd")
    return dx, dg, dw_in_t, dw_out


def kernel(x, mem, mem_norm, lb_logits, ffn1_norm, ffn1_w_in, ffn1_w_out, mix_norm, mem_w_kv, hgrn_w_in, hgrn_gnorm, hgrn_w_out, gmlp_w_in, gmlp_ln_g, gmlp_ln_b, gmlp_w_s, gmlp_b_s, gmlp_w_out, ffn2_norm, ffn2_w_in, ffn2_w_out, final_norm, loss_target, m_mem_norm, m_lb_logits, m_ffn1_norm, m_ffn1_w_in, m_ffn1_w_out, m_mix_norm, m_mem_w_kv, m_hgrn_w_in, m_hgrn_gnorm, m_hgrn_w_out, m_gmlp_w_in, m_gmlp_ln_g, m_gmlp_ln_b, m_gmlp_w_s, m_gmlp_b_s, m_gmlp_w_out, m_ffn2_norm, m_ffn2_w_in, m_ffn2_w_out, m_final_norm, v_mem_norm, v_lb_logits, v_ffn1_norm, v_ffn1_w_in, v_ffn1_w_out, v_mix_norm, v_mem_w_kv, v_hgrn_w_in, v_hgrn_gnorm, v_hgrn_w_out, v_gmlp_w_in, v_gmlp_ln_g, v_gmlp_ln_b, v_gmlp_w_s, v_gmlp_b_s, v_gmlp_w_out, v_ffn2_norm, v_ffn2_w_in, v_ffn2_w_out, v_final_norm):
    weights = dict(mem_norm=mem_norm, lb_logits=lb_logits, ffn1_norm=ffn1_norm, ffn1_w_in=ffn1_w_in, ffn1_w_out=ffn1_w_out, mix_norm=mix_norm, mem_w_kv=mem_w_kv, hgrn_w_in=hgrn_w_in, hgrn_gnorm=hgrn_gnorm, hgrn_w_out=hgrn_w_out, gmlp_w_in=gmlp_w_in, gmlp_ln_g=gmlp_ln_g, gmlp_ln_b=gmlp_ln_b, gmlp_w_s=gmlp_w_s, gmlp_b_s=gmlp_b_s, gmlp_w_out=gmlp_w_out, ffn2_norm=ffn2_norm, ffn2_w_in=ffn2_w_in, ffn2_w_out=ffn2_w_out, final_norm=final_norm)
    mom_m = dict(mem_norm=m_mem_norm, lb_logits=m_lb_logits, ffn1_norm=m_ffn1_norm, ffn1_w_in=m_ffn1_w_in, ffn1_w_out=m_ffn1_w_out, mix_norm=m_mix_norm, mem_w_kv=m_mem_w_kv, hgrn_w_in=m_hgrn_w_in, hgrn_gnorm=m_hgrn_gnorm, hgrn_w_out=m_hgrn_w_out, gmlp_w_in=m_gmlp_w_in, gmlp_ln_g=m_gmlp_ln_g, gmlp_ln_b=m_gmlp_ln_b, gmlp_w_s=m_gmlp_w_s, gmlp_b_s=m_gmlp_b_s, gmlp_w_out=m_gmlp_w_out, ffn2_norm=m_ffn2_norm, ffn2_w_in=m_ffn2_w_in, ffn2_w_out=m_ffn2_w_out, final_norm=m_final_norm)
    mom_v = dict(mem_norm=v_mem_norm, lb_logits=v_lb_logits, ffn1_norm=v_ffn1_norm, ffn1_w_in=v_ffn1_w_in, ffn1_w_out=v_ffn1_w_out, mix_norm=v_mix_norm, mem_w_kv=v_mem_w_kv, hgrn_w_in=v_hgrn_w_in, hgrn_gnorm=v_hgrn_gnorm, hgrn_w_out=v_hgrn_w_out, gmlp_w_in=v_gmlp_w_in, gmlp_ln_g=v_gmlp_ln_g, gmlp_ln_b=v_gmlp_ln_b, gmlp_w_s=v_gmlp_w_s, gmlp_b_s=v_gmlp_b_s, gmlp_w_out=v_gmlp_w_out, ffn2_norm=v_ffn2_norm, ffn2_w_in=v_ffn2_w_in, ffn2_w_out=v_ffn2_w_out, final_norm=v_final_norm)
    order = list(weights)
    _, _, _, me = _mesh_pos()
    me_arr = jnp.reshape(me, (1,)).astype(jnp.int32)
    cuts = {name: c for name, c, _, _ in GROUPS}

    mix1 = (("mem_w_kv", 1), ("gmlp_w_in", 0), ("gmlp_w_out", 0))
    gather_plan = (
        ((0, "ffn1_in"), (("ffn1_w_in", 0),), None),
        ((0, "ffn1_out"), (("ffn1_w_out", 0),), 0),
        ((0, "mix_in"), (("mem_w_kv", 0), ("hgrn_w_in", 0)), 0),
        ((0, "mix_out"), (("hgrn_w_out", 0),), 2),
        ((0, "ffn2_in"), _stage_pieces(0, "ffn2"), 2),
        ((1, "ffn1_in"), _stage_pieces(1, "ffn1"), 2),
        ((1, "mix_in"), mix1, 4),
        ((1, "ffn2_in"), _stage_pieces(1, "ffn2"), 5),
    )
    gather = {}

    def start_gather(k, deps):
        use, pieces, _ = gather_plan[k]
        lands = [_place_rows(weights[name], l, cuts[name], me_arr, name=f"place_{name}_{l}") for name, l in pieces]
        if pieces is mix1:
            lands.append(_place_ln(gmlp_ln_g, gmlp_ln_b, me_arr))
        send_sems, recv_sems, *thru, token = _copies_start(lands, lands, gather=True, deps=deps,
                                                           name=f"gather_start_l{use[0]}_{use[1]}")
        gather[use] = (k, thru, send_sems, recv_sems)
        return token

    start_gather(0, ())

    def get_weights(use, after):
        if use not in gather:
            return {}
        k, thru, send_sems, recv_sems = gather[use]
        outs = _copies_wait(thru, send_sems, recv_sems, after, n_lands=len(thru), gather=True,
                            name=f"gather_wait_l{use[0]}_{use[1]}")
        pieces = gather_plan[k][1]
        w = {p: o.reshape(N_DEV * o.shape[1], D_MODEL) for p, o in zip(pieces, outs)}
        w["deps"] = tuple(start_gather(later, (outs[0],))
                          for later, (_, _, trigger) in enumerate(gather_plan) if trigger == k)
        if pieces is mix1:
            w["ln_g"] = outs[-1][:, 0, :].reshape(1, GM_WIDTH)
            w["ln_b"] = outs[-1][:, 1, :].reshape(1, GM_WIDTH)
        return w

    scatter = {}

    def put_grads(st, grads):
        if st == "w_s":
            land = _place_slab(grads.reshape(GM_GROUPS * GM_CHUNK, GM_CHUNK), me_arr, name="w_s_place")
            send_sems, recv_sems, *thru, token = _copies_start([land], [land], gather=True, name="w_s_start")
            scatter[st] = (thru, send_sems, recv_sems)
            return (token,)
        views = [grads[p].reshape(N_DEV, -1, D_MODEL) for p in _stage_pieces(*st)]
        recv = _place_own(views, me_arr, name=f"scatter_place_l{st[0]}_{st[1]}")
        send_sems, recv_sems, *thru, token = _copies_start(views, recv, gather=False,
                                                           name=f"scatter_start_l{st[0]}_{st[1]}")
        scatter[st] = (thru, send_sems, recv_sems)
        return (token,)

    dx, small, loss_part = _step_local(
        x, mem, loss_target, get_weights, put_grads, mem_norm, lb_logits, ffn1_norm, mix_norm, hgrn_gnorm,
        gmlp_w_s, gmlp_b_s, ffn2_norm, final_norm)

    def slots_of(blk, after):
        slots = {}
        for i in (1, 0):
            thru, send_sems, recv_sems = scatter[(i, blk)]
            outs = _copies_wait(thru, send_sems, recv_sems, after, n_lands=len(thru) // 2, gather=False,
                                name=f"scatter_wait_l{i}_{blk}")
            slots.update(zip(_stage_pieces(i, blk), outs))
        return slots

    grad, delta, new_m, new_v = {}, {}, {}, {}

    def adam_groups(slots, names):
        for name in names:
            layers = GROUP_LAYERS[name]
            grad[name], delta[name], new_m[name], new_v[name] = _adam_big(
                [slots[(name, l)] for l in range(layers)], weights[name], mom_m[name], mom_v[name], cuts[name],
                name=f"{name}_adamw")

    adam_groups(slots_of("ffn2", (dx,)), ("ffn2_w_in", "ffn2_w_out"))
    adam_groups(slots_of("mix", (delta["ffn2_w_out"],)),
                ("mem_w_kv", "gmlp_w_in", "gmlp_w_out", "hgrn_w_in", "hgrn_w_out"))

    def small_parts(src):
        parts = {n: [src[n].reshape(-1, src[n].shape[-1])] for n in SLAB_AT if n not in SMALL_SHARDED}
        return parts

    w_s_rows = lambda a: a.reshape(GM_GROUPS * GM_CHUNK, GM_CHUNK)
    (slab_slots,) = _exchange_small([_pack_slab(small, name="pack_small_grads")])
    thru, send_sems, recv_sems = scatter["w_s"]
    (ws_slots,) = _copies_wait(thru, send_sems, recv_sems, (slab_slots,), n_lands=1, gather=True, name="w_s_wait")
    (g_slab, d_slab, nm_slab, nv_slab), (g_ws, d_ws, nm_ws, nv_ws) = _adam_slabs(
        [slab_slots, ws_slots],
        [_pack_slab(small_parts(weights), name="pack_small_w"), w_s_rows(gmlp_w_s)],
        [_pack_slab(small_parts(mom_m), name="pack_small_m"), w_s_rows(m_gmlp_w_s)],
        [_pack_slab(small_parts(mom_v), name="pack_small_v"), w_s_rows(v_gmlp_w_s)])
    shapes = {n: weights[n].shape for n in SLAB_AT}
    for out, slab, ws in ((grad, g_slab, g_ws), (delta, d_slab, d_ws), (new_m, nm_slab, nm_ws), (new_v, nv_slab, nv_ws)):
        out.update(_unpack_slab(slab, shapes))
        out["gmlp_w_s"] = ws.reshape(gmlp_w_s.shape)
    blk = GM_WIDTH // N_DEV
    g_ln = [lax.dynamic_slice(g_slab[SLAB_AT[n]:SLAB_AT[n] + 2].reshape(1, GM_WIDTH), (0, me * blk), (1, blk))
            for n in SMALL_SHARDED]
    ln_out = _adam_vecs(g_ln, [weights[n] for n in SMALL_SHARDED], [mom_m[n] for n in SMALL_SHARDED],
                        [mom_v[n] for n in SMALL_SHARDED])
    for n, g, (d, nm, nv) in zip(SMALL_SHARDED, g_ln, ln_out):
        grad[n], delta[n], new_m[n], new_v[n] = g, d, nm, nv

    adam_groups(slots_of("ffn1", (delta["hgrn_w_out"], d_slab)), ("ffn1_w_in", "ffn1_w_out"))

    loss = lax.psum(loss_part[0, 0], MESH_AXES)
    grad_x = dx.reshape(B_LOC, SEQ, D_MODEL)
    return (loss, grad_x, *[grad[n] for n in order], *[delta[n] for n in order],
            *[new_m[n] for n in order], *[new_v[n] for n in order])


def _step_local(x, mem, loss_target, get_weights, put_grads, mem_norm, lb_logits, ffn1_norm, mix_norm, hgrn_gnorm,
                gmlp_w_s, gmlp_b_s, ffn2_norm, final_norm):
    w_s = gmlp_w_s[0]
    b_st = gmlp_b_s[0].T

    xs = x.reshape(N_TOK, D_MODEL)
    mem2d = mem.reshape(B_LOC * MEM_LEN, D_MODEL)
    mem_g = mem_norm.reshape(1, D_MODEL)
    saved, full = [], {}
    memn = _rms_fwd(mem2d, mem_g, name="mem_norm_fwd")
    for i in range(2):
        xs, s_ffn1 = _ffn_fwd(xs, ffn1_norm[i:i + 1], "ffn1", i, full, get_weights)
        full.update(get_weights((i, "mix_in"), (xs,)))
        mixer = "hgrn" if i == 0 else "gmlp"
        hm = _rms_fwd(xs, mix_norm[i:i + 1], deps=full.pop("deps", ()), name=f"l{i}_mix_norm")
        kv = _mm(memn, full[("mem_w_kv", i)], tb=True, tm=512, tn=512, tk=D_MODEL, out_dtype=F32, name=f"l{i}_mem_kv")
        zm = _mm(hm, full[(f"{mixer}_w_in", 0)], tb=True, tm=1024, tn=512, tk=D_MODEL, out_dtype=F32, name=f"l{i}_mix_in")
        if i == 0:
            o_mix, o_pre, s_all = _hgrn_fwd(zm, lb_logits, hgrn_gnorm)
            mix_saved = (o_pre, s_all)
        else:
            o_mix = _gmlp_fwd(zm, full["ln_g"], full["ln_b"], w_s, b_st)
            mix_saved = ()
        o_mem = _attn_fwd(zm, kv, name=f"l{i}_attn")
        cat = jnp.concatenate([o_mix, o_mem], axis=1)
        x_mix = xs
        full.update(get_weights((i, "mix_out"), (cat,)))
        xs = _mm(cat, full[(f"{mixer}_w_out", 0)], tm=512, tn=D_MODEL, tk=cat.shape[1], out_dtype=F32, res=xs,
                 deps=full.pop("deps", ()), name=f"l{i}_mix_out")
        xs, s_ffn2 = _ffn_fwd(xs, ffn2_norm[i:i + 1], "ffn2", i, full, get_weights)
        saved.append((s_ffn1, (x_mix, hm, kv, zm, cat, mix_saved), s_ffn2))

    dx, d_final, loss_part = _loss_head(xs, final_norm.reshape(1, D_MODEL), loss_target.reshape(N_TOK, D_MODEL))

    small = {"final_norm": [d_final]}
    d_ffn1, d_ffn2, d_mix = [None, None], [None, None], [None, None]
    dmemn = jnp.zeros((B_LOC * MEM_LEN, D_MODEL), F32)
    deps = ()
    for i in (1, 0):
        s_ffn1, (x_mix, hm, kv, zm, cat, mix_saved), s_ffn2 = saved[i]
        dx, d_ffn2[i], dw_in_t, dw_out = _ffn_bwd(
            dx, s_ffn2, ffn2_norm[i:i + 1], full[("ffn2_w_in", i)], full[("ffn2_w_out", i)], f"l{i}_ffn2", deps)
        deps = put_grads((i, "ffn2"), {("ffn2_w_in", i): dw_in_t, ("ffn2_w_out", i): dw_out})
        mixer = "hgrn" if i == 0 else "gmlp"
        w_in_t, w_out = full[(f"{mixer}_w_in", 0)], full[(f"{mixer}_w_out", 0)]
        width = cat.shape[1]
        g_mix = {}
        g_mix[(f"{mixer}_w_out", 0)] = _mm(cat, dx, ta=True, tm=width // 2, tn=D_MODEL, tk=1024, out_dtype=BF16,
                                           deps=deps, name=f"l{i}_mix_out_wgrad")
        dcat = _mm(dx, w_out, tb=True, tm=1024, tn=width // 2, tk=D_MODEL, out_dtype=F32, name=f"l{i}_mix_out_dgrad")
        dq, dk, dv = _attn_bwd(zm, kv, dcat, do_off=width - XA_HEADS * XA_DIM, name=f"l{i}_attn_bwd")
        if i == 0:
            dzq, dzf, dzi, dzg, dlbl, dgn = _hgrn_bwd(zm, mix_saved[0], dcat, mix_saved[1], lb_logits, hgrn_gnorm)
            small["lb_logits"], small["hgrn_gnorm"] = [dlbl], [dgn]
            dzm = jnp.concatenate([dzq, dzf, dzi, dzg, dq], axis=1)
            deps = ()
        else:
            dzu, dzv, dws, dbt, dlng, dlnb = _gmlp_bwd(zm, dcat, full["ln_g"], full["ln_b"], w_s, b_st)
            small["gmlp_b_s"], small["gmlp_ln_g"], small["gmlp_ln_b"] = [dbt.T], [dlng], [dlnb]
            deps = put_grads("w_s", dws)
            dzm = jnp.concatenate([dzu, dzv, dq], axis=1)
        g_mix[(f"{mixer}_w_in", 0)] = _mm(dzm, hm, ta=True, tm=512, tn=D_MODEL, tk=1024, out_dtype=BF16, deps=deps,
                                          name=f"l{i}_mix_in_wgrad")
        dkv = jnp.concatenate([dk, dv], axis=1)
        g_mix[("mem_w_kv", i)] = _mm(dkv, memn, ta=True, tm=512, tn=D_MODEL, tk=B_LOC * MEM_LEN, out_dtype=BF16,
                                     name=f"l{i}_mem_kv_wgrad")
        deps = put_grads((i, "mix"), g_mix)
        dh = _mm(dzm, w_in_t, tm=1024, tn=D_MODEL, tk=512, out_dtype=F32, deps=deps, name=f"l{i}_mix_in_dgrad")
        dx, d_mix[i] = _rms_bwd(x_mix, mix_norm[i:i + 1], dh, dx, name=f"l{i}_mix_norm_bwd")
        dmemn = _mm(dkv, full[("mem_w_kv", i)], tm=B_LOC * MEM_LEN, tn=D_MODEL, tk=512, out_dtype=F32, res=dmemn,
                    name=f"l{i}_mem_kv_dgrad")
        dx, d_ffn1[i], dw_in_t, dw_out = _ffn_bwd(
            dx, s_ffn1, ffn1_norm[i:i + 1], full[("ffn1_w_in", i)], full[("ffn1_w_out", i)], f"l{i}_ffn1")
        deps = put_grads((i, "ffn1"), {("ffn1_w_in", i): dw_in_t, ("ffn1_w_out", i): dw_out})
    _, dmem_g = _rms_bwd(mem2d, mem_g, dmemn, dmemn, deps=deps, name="mem_norm_bwd")
    small.update(mem_norm=[dmem_g], ffn1_norm=d_ffn1, ffn2_norm=d_ffn2, mix_norm=d_mix)
    return dx, small, loss_part
```

```python
import functools
import math

import jax
import jax.numpy as jnp
from jax import lax
from jax.experimental import pallas as pl
from jax.experimental.pallas import tpu as pltpu

F32 = jnp.float32
BF16 = jnp.bfloat16

D_MODEL = 1024
SEQ = 2048
B_LOC = 2
N_TOK = B_LOC * SEQ
MEM_LEN = 256
N_DEV = 8
EPS = 1e-6
D_FF = 2816
HG_HEADS = 8
HG_DIM = 128
HG_CHUNK = 64
HG_NCHUNK = SEQ // HG_CHUNK
GM_CHUNK = 128
GM_GROUPS = 8
GM_WIDTH = 2048
GM_GDIM = GM_WIDTH // GM_GROUPS
XA_HEADS = 4
XA_DIM = 256
XA_OFF = 4096

ADAM_LR = 0.001
ADAM_B1 = 0.9
ADAM_B2 = 0.999
ADAM_EPS = 1e-08
ADAM_WD = 0.01
ADAM_STEP = 10

VMEM_LIMIT_BYTES = 56 * 1024 * 1024
MESH_AXES = ("x", "y", "c")

GROUPS = (
    ("ffn1_w_in", True, 2, 704),
    ("ffn1_w_out", False, 2, 352),
    ("mem_w_kv", True, 2, 256),
    ("hgrn_w_in", True, 1, 640),
    ("hgrn_w_out", False, 1, 256),
    ("gmlp_w_in", True, 1, 640),
    ("gmlp_w_out", False, 1, 384),
    ("ffn2_w_in", True, 2, 704),
    ("ffn2_w_out", False, 2, 352),
)
GROUP_LAYERS = {name: layers for name, _, layers, _ in GROUPS}


def _stage_pieces(layer, block):
    if block == "mix":
        mixer = "hgrn" if layer == 0 else "gmlp"
        return (("mem_w_kv", layer), (f"{mixer}_w_in", 0), (f"{mixer}_w_out", 0))
    return ((f"{block}_w_in", layer), (f"{block}_w_out", layer))


ANY_SPEC = pl.BlockSpec(memory_space=pl.ANY)
HBM_SPEC = pl.BlockSpec(memory_space=pltpu.HBM)
SEM_SPEC = pl.BlockSpec(memory_space=pltpu.SEMAPHORE)


def _cp(*sem):
    return pltpu.CompilerParams(dimension_semantics=sem, vmem_limit_bytes=VMEM_LIMIT_BYTES)


def _sigmoid(x):
    return 1.0 / (1.0 + jnp.exp(-x))


def _gelu_parts(x):
    cdf = 0.5 * (1.0 + lax.erf(x * (1.0 / math.sqrt(2.0))))
    pdf = jnp.exp(-0.5 * x * x) * (1.0 / math.sqrt(2.0 * math.pi))
    return x * cdf, cdf + x * pdf


def _mm(a, b, *, ta=False, tb=False, tm, tn, tk, out_dtype, res=None, scale=1.0, deps=(), name):
    m, k = (a.shape[1], a.shape[0]) if ta else a.shape
    n, kb = b.shape if tb else (b.shape[1], b.shape[0])
    assert k == kb and m % tm == 0 and n % tn == 0 and k % tk == 0, (name, a.shape, b.shape)
    nk = k // tk
    dn = (((0 if ta else 1,), (1 if tb else 0,)), ((), ()))
    n_in = 2 + (res is not None) + len(deps)

    def body(*refs):
        a_ref, b_ref = refs[:2]
        r_ref = refs[2] if res is not None else None
        o_ref, scr = refs[n_in], refs[n_in + 1:]
        p = lax.dot_general(a_ref[...].astype(BF16), b_ref[...].astype(BF16), dn, preferred_element_type=F32)

        def finish(acc):
            if scale != 1.0:
                acc = scale * acc
            if r_ref is not None:
                acc = r_ref[...] + acc
            o_ref[...] = acc.astype(out_dtype)

        if nk == 1:
            finish(p)
        else:
            acc_ref = scr[0]
            kk = pl.program_id(2)

            @pl.when(kk == 0)
            def _():
                acc_ref[...] = p

            @pl.when(kk > 0)
            def _():
                acc_ref[...] += p

            @pl.when(kk == nk - 1)
            def _():
                finish(acc_ref[...])

    a_spec = pl.BlockSpec((tk, tm), lambda i, j, kk: (kk, i)) if ta else pl.BlockSpec((tm, tk), lambda i, j, kk: (i, kk))
    b_spec = pl.BlockSpec((tn, tk), lambda i, j, kk: (j, kk)) if tb else pl.BlockSpec((tk, tn), lambda i, j, kk: (kk, j))
    o_spec = pl.BlockSpec((tm, tn), lambda i, j, kk: (i, j))
    in_specs = [a_spec, b_spec] + ([o_spec] if res is not None else []) + [ANY_SPEC] * len(deps)
    args = (a, b) + ((res,) if res is not None else ()) + tuple(deps)
    return pl.pallas_call(
        body,
        name=name,
        grid=(m // tm, n // tn, nk),
        in_specs=in_specs,
        out_specs=o_spec,
        out_shape=jax.ShapeDtypeStruct((m, n), out_dtype),
        scratch_shapes=[pltpu.VMEM((tm, tn), F32)] if nk > 1 else [],
        compiler_params=_cp("parallel", "parallel", "arbitrary"),
    )(*args)


def _rms_fwd(x, g, *, name, deps=(), tm=512):
    rows = x.shape[0]

    def body(x_ref, g_ref, *rest):
        o_ref = rest[len(deps)]
        xv = x_ref[...]
        r = lax.rsqrt(jnp.mean(xv * xv, axis=-1, keepdims=True) + EPS)
        o_ref[...] = (xv * r * g_ref[...]).astype(BF16)

    row = pl.BlockSpec((tm, D_MODEL), lambda i: (i, 0))
    return pl.pallas_call(
        body,
        name=name,
        grid=(rows // tm,),
        in_specs=[row, pl.BlockSpec((1, D_MODEL), lambda i: (0, 0))] + [ANY_SPEC] * len(deps),
        out_specs=row,
        out_shape=jax.ShapeDtypeStruct((rows, D_MODEL), BF16),
        compiler_params=_cp("parallel"),
    )(x, g, *deps)


def _rms_bwd(x, g, dh, dres, *, name, deps=(), tm=512):
    rows = x.shape[0]

    def body(x_ref, g_ref, dh_ref, dres_ref, *rest):
        dx_ref, dg_ref = rest[len(deps):]
        xv = x_ref[...]
        r = lax.rsqrt(jnp.mean(xv * xv, axis=-1, keepdims=True) + EPS)
        xhat = xv * r
        dhv = dh_ref[...]
        part = jnp.sum(dhv * xhat, axis=0, keepdims=True)

        @pl.when(pl.program_id(0) == 0)
        def _():
            dg_ref[...] = part

        @pl.when(pl.program_id(0) > 0)
        def _():
            dg_ref[...] += part

        dxh = dhv * g_ref[...]
        dx_ref[...] = dres_ref[...] + r * (dxh - xhat * jnp.mean(dxh * xhat, axis=-1, keepdims=True))

    row = pl.BlockSpec((tm, D_MODEL), lambda i: (i, 0))
    vec = pl.BlockSpec((1, D_MODEL), lambda i: (0, 0))
    return pl.pallas_call(
        body,
        name=name,
        grid=(rows // tm,),
        in_specs=[row, vec, row, row] + [ANY_SPEC] * len(deps),
        out_specs=[row, vec],
        out_shape=[jax.ShapeDtypeStruct((rows, D_MODEL), F32), jax.ShapeDtypeStruct((1, D_MODEL), F32)],
        compiler_params=_cp("arbitrary"),
    )(x, g, dh, dres, *deps)


def _swiglu_fwd(z, *, name, tm=512):
    def body(g_ref, u_ref, o_ref):
        gv = g_ref[...]
        o_ref[...] = (gv * _sigmoid(gv) * u_ref[...]).astype(BF16)

    return pl.pallas_call(
        body,
        name=name,
        grid=(N_TOK // tm,),
        in_specs=[pl.BlockSpec((tm, D_FF), lambda i: (i, 0)), pl.BlockSpec((tm, D_FF), lambda i: (i, 1))],
        out_specs=pl.BlockSpec((tm, D_FF), lambda i: (i, 0)),
        out_shape=jax.ShapeDtypeStruct((N_TOK, D_FF), BF16),
        compiler_params=_cp("parallel"),
    )(z, z)


def _swiglu_bwd(z, dact, *, scale, name, tm=512):
    def body(g_ref, u_ref, da_ref, o_ref):
        gv = g_ref[...]
        s = _sigmoid(gv)
        da = da_ref[...] * scale
        o_ref[:, :D_FF] = (da * u_ref[...] * (s * (1.0 + gv * (1.0 - s)))).astype(BF16)
        o_ref[:, D_FF:] = (da * (gv * s)).astype(BF16)

    half = lambda j: pl.BlockSpec((tm, D_FF), lambda i: (i, j))
    return pl.pallas_call(
        body,
        name=name,
        grid=(N_TOK // tm,),
        in_specs=[half(0), half(1), half(0)],
        out_specs=pl.BlockSpec((tm, 2 * D_FF), lambda i: (i, 0)),
        out_shape=jax.ShapeDtypeStruct((N_TOK, 2 * D_FF), BF16),
        compiler_params=_cp("parallel"),
    )(z, z, dact)


def _loss_head(x, g, target, *, tm=512):
    def body(x_ref, g_ref, t_ref, dx_ref, dg_ref, loss_ref):
        xv = x_ref[...]
        gv = g_ref[...]
        r = lax.rsqrt(jnp.mean(xv * xv, axis=-1, keepdims=True) + EPS)
        xhat = xv * r
        err = xhat * gv - t_ref[...]
        loss_part = jnp.zeros((1, 128), F32) + 0.5 * jnp.sum(jnp.mean(err * err, axis=-1, keepdims=True))
        dy = err * (1.0 / D_MODEL)
        dg_part = jnp.sum(dy * xhat, axis=0, keepdims=True)

        @pl.when(pl.program_id(0) == 0)
        def _():
            dg_ref[...] = dg_part
            loss_ref[...] = loss_part

        @pl.when(pl.program_id(0) > 0)
        def _():
            dg_ref[...] += dg_part
            loss_ref[...] += loss_part

        dxh = dy * gv
        dx_ref[...] = r * (dxh - xhat * jnp.mean(dxh * xhat, axis=-1, keepdims=True))

    row = pl.BlockSpec((tm, D_MODEL), lambda i: (i, 0))
    vec = pl.BlockSpec((1, D_MODEL), lambda i: (0, 0))
    return pl.pallas_call(
        body,
        name="loss_head",
        grid=(N_TOK // tm,),
        in_specs=[row, vec, row],
        out_specs=[row, vec, pl.BlockSpec((1, 128), lambda i: (0, 0))],
        out_shape=[
            jax.ShapeDtypeStruct((N_TOK, D_MODEL), F32),
            jax.ShapeDtypeStruct((1, D_MODEL), F32),
            jax.ShapeDtypeStruct((1, 128), F32),
        ],
        compiler_params=_cp("arbitrary"),
    )(x, g, target)


_NT = (((1,), (1,)), ((), ()))
_TN = (((0,), (0,)), ((), ()))
XA_TQ = 1024
XA_SCALE = XA_DIM ** -0.5


def _attn_probs(q16, k16):
    s = lax.dot_general(q16, k16, _NT, preferred_element_type=F32) * XA_SCALE
    e = jnp.exp(s - jnp.max(s, axis=-1, keepdims=True))
    return e / jnp.sum(e, axis=-1, keepdims=True)


def _attn_fwd(z, kv, *, name):
    nt = SEQ // XA_TQ

    def body(q_ref, k_ref, v_ref, o_ref):
        p = _attn_probs(q_ref[...].astype(BF16), k_ref[...].astype(BF16))
        o_ref[...] = jnp.dot(p.astype(BF16), v_ref[...].astype(BF16), preferred_element_type=F32).astype(BF16)

    return pl.pallas_call(
        body,
        name=name,
        grid=(B_LOC, XA_HEADS, nt),
        in_specs=[
            pl.BlockSpec((XA_TQ, XA_DIM), lambda b, h, t: (b * nt + t, XA_OFF // XA_DIM + h)),
            pl.BlockSpec((MEM_LEN, XA_DIM), lambda b, h, t: (b, h)),
            pl.BlockSpec((MEM_LEN, XA_DIM), lambda b, h, t: (b, XA_HEADS + h)),
        ],
        out_specs=pl.BlockSpec((XA_TQ, XA_DIM), lambda b, h, t: (b * nt + t, h)),
        out_shape=jax.ShapeDtypeStruct((N_TOK, XA_HEADS * XA_DIM), BF16),
        compiler_params=_cp("parallel", "parallel", "arbitrary"),
    )(z, kv, kv)


def _attn_bwd(z, kv, dcat, *, do_off, name):
    nt = SEQ // XA_TQ

    def body(q_ref, k_ref, v_ref, do_ref, dq_ref, dk_ref, dv_ref):
        q16 = q_ref[...].astype(BF16)
        k16 = k_ref[...].astype(BF16)
        v16 = v_ref[...].astype(BF16)
        do16 = do_ref[...].astype(BF16)
        p = _attn_probs(q16, k16)
        dv_part = lax.dot_general(p.astype(BF16), do16, _TN, preferred_element_type=F32)
        dp = lax.dot_general(do16, v16, _NT, preferred_element_type=F32)
        ds16 = (p * (dp - jnp.sum(dp * p, axis=-1, keepdims=True)) * XA_SCALE).astype(BF16)
        dq_ref[...] = jnp.dot(ds16, k16, preferred_element_type=F32).astype(BF16)
        dk_part = lax.dot_general(ds16, q16, _TN, preferred_element_type=F32)

        @pl.when(pl.program_id(2) == 0)
        def _():
            dk_ref[...] = dk_part
            dv_ref[...] = dv_part

        @pl.when(pl.program_id(2) > 0)
        def _():
            dk_ref[...] += dk_part
            dv_ref[...] += dv_part

    qspec = pl.BlockSpec((XA_TQ, XA_DIM), lambda b, h, t: (b * nt + t, XA_OFF // XA_DIM + h))
    kspec = lambda off: pl.BlockSpec((MEM_LEN, XA_DIM), lambda b, h, t: (b, off + h))
    return pl.pallas_call(
        body,
        name=name,
        grid=(B_LOC, XA_HEADS, nt),
        in_specs=[qspec, kspec(0), kspec(XA_HEADS),
                  pl.BlockSpec((XA_TQ, XA_DIM), lambda b, h, t: (b * nt + t, do_off // XA_DIM + h))],
        out_specs=[pl.BlockSpec((XA_TQ, XA_DIM), lambda b, h, t: (b * nt + t, h)), kspec(0), kspec(0)],
        out_shape=[
            jax.ShapeDtypeStruct((N_TOK, XA_HEADS * XA_DIM), BF16),
            jax.ShapeDtypeStruct((B_LOC * MEM_LEN, XA_HEADS * XA_DIM), F32),
            jax.ShapeDtypeStruct((B_LOC * MEM_LEN, XA_HEADS * XA_DIM), F32),
        ],
        compiler_params=_cp("parallel", "parallel", "arbitrary"),
    )(z, kv, kv, dcat)


def _tril(n):
    return lax.broadcasted_iota(jnp.int32, (n, n), 0) >= lax.broadcasted_iota(jnp.int32, (n, n), 1)


def _lower_bound(lbl):
    e = jnp.exp(lbl - jnp.max(lbl, axis=0, keepdims=True))
    p = e / jnp.sum(e, axis=0, keepdims=True)
    return p[0:1, :], p


def _hgrn_gates(zq, zf, lb, tril_f):
    sig = _sigmoid(zf)
    f = lb + (1.0 - lb) * sig
    kk = 1.0 - f
    sq = _sigmoid(zq)
    q = zq * sq
    b = jnp.dot(tril_f, jnp.log(f), preferred_element_type=F32, precision=lax.Precision.HIGHEST)
    bl = b[HG_CHUNK - 1:HG_CHUNK, :]
    return q, sq, sig, f, kk, b, bl


HG_TB = 512
HG_CPB = HG_TB // HG_CHUNK
HG_NT = SEQ // HG_TB
HG_WIDTH = HG_HEADS * HG_DIM


def _head(h, section=0):
    return slice(section * HG_WIDTH + h * HG_DIM, section * HG_WIDTH + (h + 1) * HG_DIM)


def _hgrn_fwd(z, o_mem, lb_logits, gnorm):
    def body(zq_ref, zf_ref, zi_ref, zg_ref, omem_ref, lbl_ref, gn_ref, o_ref, opre_ref, sall_ref, st_ref):
        lb, _ = _lower_bound(lbl_ref[...])
        gn = gn_ref[...]
        mask = _tril(HG_CHUNK)
        tril_f = mask.astype(F32)
        o_ref[:, HG_WIDTH:] = omem_ref[...]

        @pl.when(pl.program_id(1) == 0)
        def _():
            st_ref[...] = jnp.zeros_like(st_ref)

        def chunk(c, carry):
            rows = pl.ds(pl.multiple_of(c * HG_CHUNK, HG_CHUNK), HG_CHUNK)
            q, _, _, _, kk, b, bl = _hgrn_gates(zq_ref[rows, :], zf_ref[rows, :], lb, tril_f)
            v16 = zi_ref[rows, :].astype(BF16)
            qd16 = (q * jnp.exp(b)).astype(BF16)
            ki16 = (kk * jnp.exp(-b)).astype(BF16)
            kd16 = (kk * jnp.exp(bl - b)).astype(BF16)
            ebl = jnp.exp(bl)
            zg = zg_ref[rows, :]
            gate = zg * _sigmoid(zg)
            for h in range(HG_HEADS):
                sl = _head(h)
                a = jnp.where(mask, lax.dot_general(qd16[:, sl], ki16[:, sl], _NT, preferred_element_type=F32), 0.0)
                st = st_ref[h]
                sall_ref[0, h, c] = st
                o = jnp.dot(a.astype(BF16), v16[:, sl], preferred_element_type=F32) + lax.dot_general(
                    qd16[:, sl], st.astype(BF16), _NT, preferred_element_type=F32)
                st_ref[h] = st * ebl[:, sl] + lax.dot_general(v16[:, sl], kd16[:, sl], _TN, preferred_element_type=F32)
                opre_ref[rows, sl] = o
                r = lax.rsqrt(jnp.mean(o * o, axis=-1, keepdims=True) + EPS)
                o_ref[rows, sl] = ((o * r * gn) * gate[:, sl]).astype(BF16)
            return carry

        lax.fori_loop(0, HG_CPB, chunk, 0)

    zspec = lambda s: pl.BlockSpec((HG_TB, HG_WIDTH), lambda b, t: (b * HG_NT + t, s))
    return pl.pallas_call(
        body,
        name="hgrn_fwd",
        grid=(B_LOC, HG_NT),
        in_specs=[zspec(0), zspec(1), zspec(2), zspec(3), zspec(0),
                  pl.BlockSpec((3, HG_WIDTH), lambda b, t: (0, 0)), pl.BlockSpec((1, HG_DIM), lambda b, t: (0, 0))],
        out_specs=[pl.BlockSpec((HG_TB, 2 * HG_WIDTH), lambda b, t: (b * HG_NT + t, 0)), zspec(0),
                   pl.BlockSpec((1, HG_HEADS, HG_CPB, HG_DIM, HG_DIM), lambda b, t: (b, 0, t, 0, 0))],
        out_shape=[
            jax.ShapeDtypeStruct((N_TOK, 2 * HG_WIDTH), BF16),
            jax.ShapeDtypeStruct((N_TOK, HG_WIDTH), F32),
            jax.ShapeDtypeStruct((B_LOC, HG_HEADS, HG_NCHUNK, HG_DIM, HG_DIM), F32),
        ],
        scratch_shapes=[pltpu.VMEM((HG_HEADS, HG_DIM, HG_DIM), F32)],
        compiler_params=_cp("parallel", "arbitrary"),
    )(z, z, z, z, o_mem, lb_logits, gnorm)


def _hgrn_bwd(z, opre, dcat, dq_mem, sall, lb_logits, gnorm):
    def body(zq_ref, zf_ref, zi_ref, zg_ref, opre_ref, dout_ref, dqm_ref, sall_ref, lbl_ref, gn_ref,
             dz_ref, dlbl_ref, dgn_ref, dst_ref, dlb_ref, dgn_acc, db_ref, dkk_ref, dbl_ref):
        b_id, t_id = pl.program_id(0), pl.program_id(1)
        lb, p = _lower_bound(lbl_ref[...])
        gn = gn_ref[...]
        mask = _tril(HG_CHUNK)
        tril_f = mask.astype(F32)
        dz_ref[:, 4 * HG_WIDTH:] = dqm_ref[...]

        @pl.when(t_id == 0)
        def _():
            dst_ref[...] = jnp.zeros_like(dst_ref)
            dlb_ref[...] = jnp.zeros_like(dlb_ref)

        @pl.when((b_id == 0) & (t_id == 0))
        def _():
            dgn_acc[...] = jnp.zeros_like(dgn_acc)

        def chunk(i, carry):
            c = HG_CPB - 1 - i
            rows = pl.ds(pl.multiple_of(c * HG_CHUNK, HG_CHUNK), HG_CHUNK)
            zq, zg = zq_ref[rows, :], zg_ref[rows, :]
            q, sq, sig, f, kk, b, bl = _hgrn_gates(zq, zf_ref[rows, :], lb, tril_f)
            v16 = zi_ref[rows, :].astype(BF16)
            eb, enb, ebl_b, ebl = jnp.exp(b), jnp.exp(-b), jnp.exp(bl - b), jnp.exp(bl)
            qd, ki, kd = q * eb, kk * enb, kk * ebl_b
            qd16, ki16, kd16 = qd.astype(BF16), ki.astype(BF16), kd.astype(BF16)
            o_all = opre_ref[rows, :]
            dout = dout_ref[rows, :]
            sg = _sigmoid(zg)
            d_on_all = dout * (zg * sg)
            dgate = dout * (sg * (1.0 + zg * (1.0 - sg)))
            dq_scale = eb * (sq * (1.0 + zq * (1.0 - sq)))
            for h in range(HG_HEADS):
                sl = _head(h)
                o = o_all[:, sl]
                r = lax.rsqrt(jnp.mean(o * o, axis=-1, keepdims=True) + EPS)
                ohat = o * r
                d_on = d_on_all[:, sl]
                dz_ref[rows, _head(h, 3)] = (dgate[:, sl] * (ohat * gn)).astype(BF16)
                dgn_acc[...] += jnp.sum(d_on * ohat, axis=0, keepdims=True)
                dohat = d_on * gn
                do16 = (r * (dohat - ohat * jnp.mean(dohat * ohat, axis=-1, keepdims=True))).astype(BF16)
                st = sall_ref[0, h, c]
                dst = dst_ref[h]
                st16, dst16 = st.astype(BF16), dst.astype(BF16)
                qd_h, ki_h, kd_h, v_h = qd16[:, sl], ki16[:, sl], kd16[:, sl], v16[:, sl]
                a16 = jnp.where(mask, lax.dot_general(qd_h, ki_h, _NT, preferred_element_type=F32), 0.0).astype(BF16)
                da16 = jnp.where(mask, lax.dot_general(do16, v_h, _NT, preferred_element_type=F32), 0.0).astype(BF16)
                dv = lax.dot_general(a16, do16, _TN, preferred_element_type=F32) + lax.dot_general(
                    kd_h, dst16, _NT, preferred_element_type=F32)
                dqd = jnp.dot(da16, ki_h, preferred_element_type=F32) + jnp.dot(do16, st16, preferred_element_type=F32)
                dki = lax.dot_general(da16, qd_h, _TN, preferred_element_type=F32)
                dkd = jnp.dot(v_h, dst16, preferred_element_type=F32)
                dbl_ref[:, sl] = jnp.sum(dkd * kd[:, sl], axis=0, keepdims=True) + ebl[:, sl] * jnp.sum(
                    st * dst, axis=0, keepdims=True)
                dst_ref[h] = dst * ebl[:, sl] + lax.dot_general(do16, qd_h, _TN, preferred_element_type=F32)
                dz_ref[rows, _head(h, 2)] = dv.astype(BF16)
                dz_ref[rows, sl] = (dqd * dq_scale[:, sl]).astype(BF16)
                dkk_ref[:, sl] = dki * enb[:, sl] + dkd * ebl_b[:, sl]
                db_ref[:, sl] = dqd * qd[:, sl] - dki * ki[:, sl] - dkd * kd[:, sl]
            dlogf = lax.dot_general(tril_f, db_ref[...], _TN, preferred_element_type=F32,
                                    precision=lax.Precision.HIGHEST) + dbl_ref[...]
            df = dlogf / f - dkk_ref[...]
            dz_ref[rows, HG_WIDTH:2 * HG_WIDTH] = (df * (1.0 - lb) * sig * (1.0 - sig)).astype(BF16)
            dlb_ref[...] += jnp.sum(df * (1.0 - sig), axis=0, keepdims=True)
            return carry

        lax.fori_loop(0, HG_CPB, chunk, 0)

        @pl.when(t_id == HG_NT - 1)
        def _():
            row0 = (lax.broadcasted_iota(jnp.int32, (3, HG_WIDTH), 0) == 0).astype(F32)
            dlbl_part = dlb_ref[...] * lb * (row0 - p)

            @pl.when(b_id == 0)
            def _():
                dlbl_ref[...] = dlbl_part

            @pl.when(b_id > 0)
            def _():
                dlbl_ref[...] += dlbl_part

            dgn_ref[...] = dgn_acc[...]

    rev = lambda b, t: b * HG_NT + HG_NT - 1 - t
    zspec = lambda s: pl.BlockSpec((HG_TB, HG_WIDTH), lambda b, t: (rev(b, t), s))
    return pl.pallas_call(
        body,
        name="hgrn_bwd",
        grid=(B_LOC, HG_NT),
        in_specs=[zspec(0), zspec(1), zspec(2), zspec(3), zspec(0), zspec(0), zspec(0),
                  pl.BlockSpec((1, HG_HEADS, HG_CPB, HG_DIM, HG_DIM), lambda b, t: (b, 0, HG_NT - 1 - t, 0, 0)),
                  pl.BlockSpec((3, HG_WIDTH), lambda b, t: (0, 0)), pl.BlockSpec((1, HG_DIM), lambda b, t: (0, 0))],
        out_specs=[pl.BlockSpec((HG_TB, 5 * HG_WIDTH), lambda b, t: (rev(b, t), 0)),
                   pl.BlockSpec((3, HG_WIDTH), lambda b, t: (0, 0)), pl.BlockSpec((1, HG_DIM), lambda b, t: (0, 0))],
        out_shape=[jax.ShapeDtypeStruct((N_TOK, 5 * HG_WIDTH), BF16),
                   jax.ShapeDtypeStruct((3, HG_WIDTH), F32), jax.ShapeDtypeStruct((1, HG_DIM), F32)],
        scratch_shapes=[pltpu.VMEM((HG_HEADS, HG_DIM, HG_DIM), F32), pltpu.VMEM((1, HG_WIDTH), F32),
                        pltpu.VMEM((1, HG_DIM), F32), pltpu.VMEM((HG_CHUNK, HG_WIDTH), F32),
                        pltpu.VMEM((HG_CHUNK, HG_WIDTH), F32), pltpu.VMEM((1, HG_WIDTH), F32)],
        compiler_params=_cp("arbitrary", "arbitrary"),
    )(z, z, z, z, opre, dcat, dq_mem, sall, lb_logits, gnorm)


GM_TM = 256


def _gmlp_norm(zv, ln_g, ln_b):
    gv, dgelu = _gelu_parts(zv)
    xc = gv - jnp.mean(gv, axis=-1, keepdims=True)
    rstd = lax.rsqrt(jnp.mean(xc * xc, axis=-1, keepdims=True) + EPS)
    vhat = xc * rstd
    return vhat * ln_g + ln_b, vhat, rstd, dgelu


def _gmlp_specs():
    half = lambda j: pl.BlockSpec((GM_TM, GM_WIDTH), lambda i: (i, j))
    vec = pl.BlockSpec((1, GM_WIDTH), lambda i: (0, 0))
    w = pl.BlockSpec((GM_GROUPS, GM_CHUNK, GM_CHUNK), lambda i: (0, 0, 0))
    bt = pl.BlockSpec((GM_CHUNK, GM_GROUPS), lambda i: (0, 0))
    return half, vec, w, bt


def _gmlp_fwd(z, o_mem, ln_g, ln_b, w_s, b_st):
    def body(zu_ref, zv_ref, omem_ref, g_ref, b_ref, w_ref, bt_ref, o_ref):
        o_ref[:, GM_WIDTH:] = omem_ref[...]
        u, _ = _gelu_parts(zu_ref[...])
        v, _, _, _ = _gmlp_norm(zv_ref[...], g_ref[...], b_ref[...])
        v16 = v.astype(BF16)
        mask = _tril(GM_CHUNK)
        bt = bt_ref[...]
        for g in range(GM_GROUPS):
            wm16 = jnp.where(mask, w_ref[g], 0.0).astype(BF16)
            cols = slice(g * GM_GDIM, (g + 1) * GM_GDIM)
            for c in range(GM_TM // GM_CHUNK):
                rows = slice(c * GM_CHUNK, (c + 1) * GM_CHUNK)
                mixed = jnp.dot(wm16, v16[rows, cols], preferred_element_type=F32) + bt[:, g:g + 1]
                o_ref[rows, cols] = (u[rows, cols] * mixed).astype(BF16)

    half, vec, w, bt = _gmlp_specs()
    return pl.pallas_call(
        body,
        name="gmlp_fwd",
        grid=(N_TOK // GM_TM,),
        in_specs=[half(0), half(1), pl.BlockSpec((GM_TM, XA_HEADS * XA_DIM), lambda i: (i, 0)), vec, vec, w, bt],
        out_specs=pl.BlockSpec((GM_TM, GM_WIDTH + XA_HEADS * XA_DIM), lambda i: (i, 0)),
        out_shape=jax.ShapeDtypeStruct((N_TOK, GM_WIDTH + XA_HEADS * XA_DIM), BF16),
        compiler_params=_cp("parallel"),
    )(z, z, o_mem, ln_g, ln_b, w_s, b_st)


def _gmlp_bwd(z, dcat, dq_mem, ln_g, ln_b, w_s, b_st):
    def body(zu_ref, zv_ref, dout_ref, dqm_ref, g_ref, b_ref, w_ref, bt_ref,
             dz_ref, dw_ref, dbt_ref, dg_ref, db_ref, dv_ref):
        dz_ref[:, 2 * GM_WIDTH:] = dqm_ref[...]
        @pl.when(pl.program_id(0) == 0)
        def _():
            dw_ref[...] = jnp.zeros_like(dw_ref)
            dbt_ref[...] = jnp.zeros_like(dbt_ref)
            dg_ref[...] = jnp.zeros_like(dg_ref)
            db_ref[...] = jnp.zeros_like(db_ref)

        zu = zu_ref[...]
        u, du_dz = _gelu_parts(zu)
        ln_g = g_ref[...]
        v, vhat, rstd, dgv_dz = _gmlp_norm(zv_ref[...], ln_g, b_ref[...])
        v16 = v.astype(BF16)
        dout = dout_ref[...]
        dmixed = dout * u
        dm16 = dmixed.astype(BF16)
        mask = _tril(GM_CHUNK)
        bt = bt_ref[...]
        group_id = lax.broadcasted_iota(jnp.int32, (1, GM_GROUPS), 1)
        dbt = jnp.zeros((GM_CHUNK, GM_GROUPS), F32)
        for g in range(GM_GROUPS):
            wm16 = jnp.where(mask, w_ref[g], 0.0).astype(BF16)
            cols = slice(g * GM_GDIM, (g + 1) * GM_GDIM)
            dw = jnp.zeros((GM_CHUNK, GM_CHUNK), F32)
            dbt_g = jnp.zeros((GM_CHUNK, 1), F32)
            for c in range(GM_TM // GM_CHUNK):
                rows = slice(c * GM_CHUNK, (c + 1) * GM_CHUNK)
                mixed = jnp.dot(wm16, v16[rows, cols], preferred_element_type=F32) + bt[:, g:g + 1]
                dz_ref[rows, cols] = (dout[rows, cols] * mixed * du_dz[rows, cols]).astype(BF16)
                dw += lax.dot_general(dm16[rows, cols], v16[rows, cols], _NT, preferred_element_type=F32)
                dbt_g += jnp.sum(dmixed[rows, cols], axis=-1, keepdims=True)
                dv_ref[rows, cols] = lax.dot_general(wm16, dm16[rows, cols], _TN, preferred_element_type=F32)
            dw_ref[g] += jnp.where(mask, dw, 0.0)
            dbt = dbt + dbt_g * (group_id == g).astype(F32)
        dbt_ref[...] += dbt
        dv = dv_ref[...]
        dg_ref[...] += jnp.sum(dv * vhat, axis=0, keepdims=True)
        db_ref[...] += jnp.sum(dv, axis=0, keepdims=True)
        dvh = dv * ln_g
        dgv = rstd * (dvh - jnp.mean(dvh, axis=-1, keepdims=True) - vhat * jnp.mean(dvh * vhat, axis=-1, keepdims=True))
        dz_ref[:, GM_WIDTH:2 * GM_WIDTH] = (dgv * dgv_dz).astype(BF16)

    half, vec, w, bt = _gmlp_specs()
    dz_width = 2 * GM_WIDTH + XA_HEADS * XA_DIM
    return pl.pallas_call(
        body,
        name="gmlp_bwd",
        grid=(N_TOK // GM_TM,),
        in_specs=[half(0), half(1), half(0), pl.BlockSpec((GM_TM, XA_HEADS * XA_DIM), lambda i: (i, 0)), vec, vec, w, bt],
        out_specs=[pl.BlockSpec((GM_TM, dz_width), lambda i: (i, 0)), w, bt, vec, vec],
        out_shape=[jax.ShapeDtypeStruct((N_TOK, dz_width), BF16),
                   jax.ShapeDtypeStruct((GM_GROUPS, GM_CHUNK, GM_CHUNK), F32),
                   jax.ShapeDtypeStruct((GM_CHUNK, GM_GROUPS), F32),
                   jax.ShapeDtypeStruct((1, GM_WIDTH), F32), jax.ShapeDtypeStruct((1, GM_WIDTH), F32)],
        scratch_shapes=[pltpu.VMEM((GM_TM, GM_WIDTH), F32)],
        compiler_params=_cp("arbitrary"),
    )(z, z, dcat, dq_mem, ln_g, ln_b, w_s, b_st)


def _own_slot(shape):
    return pl.BlockSpec((None,) + tuple(shape), lambda i, me_ref: (me_ref[0],) + (0,) * len(shape))


def _place_rows(w, layer, cuts_columns, me, *, name):
    _, r, c = w.shape
    n = c if cuts_columns else r

    def body(me_ref, w_ref, o_ref):
        wv = w_ref[...]
        o_ref[...] = (wv.T if cuts_columns else wv).astype(BF16)

    return pl.pallas_call(
        body,
        name=name,
        grid_spec=pltpu.PrefetchScalarGridSpec(
            num_scalar_prefetch=1, grid=(1,),
            in_specs=[pl.BlockSpec((None, r, c), lambda i, me_ref: (layer, 0, 0))],
            out_specs=_own_slot((n, D_MODEL))),
        out_shape=jax.ShapeDtypeStruct((N_DEV, n, D_MODEL), BF16),
        compiler_params=_cp("arbitrary"),
    )(me, w)


def _place_ln(ln_g, ln_b, me):
    blk = ln_g.shape[1]

    def body(me_ref, g_ref, b_ref, o_ref):
        o_ref[...] = jnp.zeros_like(o_ref)
        o_ref[0:1, :] = g_ref[...]
        o_ref[1:2, :] = b_ref[...]

    vec = pl.BlockSpec((1, blk), lambda i, me_ref: (0, 0))
    return pl.pallas_call(
        body,
        name="place_ln",
        grid_spec=pltpu.PrefetchScalarGridSpec(
            num_scalar_prefetch=1, grid=(1,), in_specs=[vec, vec], out_specs=_own_slot((8, blk))),
        out_shape=jax.ShapeDtypeStruct((N_DEV, 8, blk), F32),
        compiler_params=_cp("arbitrary"),
    )(me, ln_g, ln_b)


def _place_slab(a, me, *, name):
    def body(me_ref, a_ref, o_ref):
        o_ref[...] = a_ref[...]

    return pl.pallas_call(
        body,
        name=name,
        grid_spec=pltpu.PrefetchScalarGridSpec(
            num_scalar_prefetch=1, grid=(1,),
            in_specs=[pl.BlockSpec(a.shape, lambda i, me_ref: (0, 0))], out_specs=_own_slot(a.shape)),
        out_shape=jax.ShapeDtypeStruct((N_DEV,) + a.shape, a.dtype),
        compiler_params=_cp("arbitrary"),
    )(me, a)


def _place_own(grads, me, *, name):
    k = len(grads)

    def body(me_ref, *refs):
        for src, dst in zip(refs[:k], refs[k:]):
            dst[...] = src[...]

    specs = [_own_slot(g.shape[1:]) for g in grads]
    return pl.pallas_call(
        body,
        name=name,
        grid_spec=pltpu.PrefetchScalarGridSpec(num_scalar_prefetch=1, grid=(1,), in_specs=specs, out_specs=specs),
        out_shape=[jax.ShapeDtypeStruct(g.shape, g.dtype) for g in grads],
        compiler_params=_cp("arbitrary"),
    )(me, *grads)


def _mesh_pos():
    x, y, c = (lax.axis_index(a) for a in MESH_AXES)
    return x, y, c, 4 * x + 2 * y + c


def _peer(x, y, c, r):
    px = 1 - x if r & 4 else x
    py = 1 - y if r & 2 else y
    pc = 1 - c if r & 1 else c
    return (px, py, pc), 4 * px + 2 * py + pc


def _peer_copies(srcs, lands, send_sems, recv_sems, gather, waits):
    x, y, c, me = _mesh_pos()
    pairs = []
    for r in range(1, N_DEV):
        peer, peer_blk = _peer(x, y, c, r)
        for k, (src, land) in enumerate(zip(srcs, lands)):
            idx = k * (N_DEV - 1) + r - 1
            sems = dict(send_sem=send_sems.at[idx], recv_sem=recv_sems.at[idx], device_id=peer,
                        device_id_type=pl.DeviceIdType.MESH)
            mine = pltpu.make_async_remote_copy(
                src_ref=src.at[me if gather else peer_blk], dst_ref=land.at[me], **sems)
            theirs = pltpu.make_async_remote_copy(src_ref=src.at[me], dst_ref=land.at[peer_blk], **sems) if waits else None
            pairs.append((mine, theirs))
    return pairs


DATAFLOW = pltpu.SideEffectType.DATAFLOW_SIDE_EFFECTING


def _in_hbm(a):
    return pltpu.with_memory_space_constraint(a, pltpu.HBM)


def _copies_start(srcs, lands, *, gather, name, deps=()):
    arrs = list(lands) if gather else list(srcs) + list(lands)
    n, k, nd = len(arrs), len(lands), len(deps)

    def body(*refs):
        ins, send_sems, recv_sems, token = refs[:n], refs[n + nd], refs[n + nd + 1], refs[2 * n + nd + 2]
        src_refs, land_refs = (ins, ins) if gather else (ins[:k], ins[k:])
        for mine, _ in _peer_copies(src_refs, land_refs, send_sems, recv_sems, gather, waits=False):
            mine.start()
        token[...] = jnp.zeros_like(token)

    n_cp = k * (N_DEV - 1)
    return pl.pallas_call(
        body,
        name=name,
        in_specs=[HBM_SPEC] * n + [ANY_SPEC] * nd,
        out_specs=(SEM_SPEC, SEM_SPEC, *[HBM_SPEC] * n, pl.BlockSpec(memory_space=pltpu.VMEM)),
        out_shape=(pltpu.SemaphoreType.DMA((n_cp,)), pltpu.SemaphoreType.DMA((n_cp,)),
                   *[pltpu.HBM(a.shape, a.dtype) for a in arrs], jax.ShapeDtypeStruct((8, 128), F32)),
        input_output_aliases={i: 2 + i for i in range(n)},
        compiler_params=pltpu.CompilerParams(has_side_effects=DATAFLOW),
    )(*[_in_hbm(a) for a in arrs], *deps)


def _copies_wait(arrs, send_sems, recv_sems, after, *, n_lands, gather, name):
    n, k = len(arrs), n_lands

    def body(*refs):
        ins, send_sems, recv_sems = refs[:n], refs[n], refs[n + 1]
        src_refs, land_refs = (ins, ins) if gather else (ins[:k], ins[k:])
        for mine, theirs in _peer_copies(src_refs, land_refs, send_sems, recv_sems, gather, waits=True):
            mine.wait_send()
            theirs.wait_recv()

    outs = pl.pallas_call(
        body,
        name=name,
        in_specs=[HBM_SPEC] * n + [SEM_SPEC, SEM_SPEC] + [ANY_SPEC] * len(after),
        out_specs=[HBM_SPEC] * n,
        out_shape=[pltpu.HBM(a.shape, a.dtype) for a in arrs],
        input_output_aliases={i: i for i in range(n)},
        compiler_params=pltpu.CompilerParams(has_side_effects=DATAFLOW),
    )(*arrs, send_sems, recv_sems, *after)
    return outs[n - k:]


def _exchange_small(slabs):
    n = len(slabs)

    def body(*refs):
        ins, outs = refs[:n], refs[n:2 * n]
        send_sems, recv_sems, local_sems = refs[2 * n:]
        x, y, c, me = _mesh_pos()
        own = [pltpu.make_async_copy(src, dst.at[me], local_sems.at[k]) for k, (src, dst) in enumerate(zip(ins, outs))]
        for cp in own:
            cp.start()
        sends, recvs = [], []
        for r in range(1, N_DEV):
            peer, peer_blk = _peer(x, y, c, r)
            for k, (src, dst) in enumerate(zip(ins, outs)):
                idx = k * (N_DEV - 1) + r - 1
                sems = dict(send_sem=send_sems.at[idx], recv_sem=recv_sems.at[idx], device_id=peer,
                            device_id_type=pl.DeviceIdType.MESH)
                send = pltpu.make_async_remote_copy(src_ref=src, dst_ref=dst.at[me], **sems)
                send.start()
                sends.append(send)
                recvs.append(pltpu.make_async_remote_copy(src_ref=src, dst_ref=dst.at[peer_blk], **sems))
        for cp in recvs:
            cp.wait_recv()
        for cp in sends:
            cp.wait_send()
        for cp in own:
            cp.wait()

    n_cp = n * (N_DEV - 1)
    return pl.pallas_call(
        body,
        name="exchange_small_grads",
        in_specs=[ANY_SPEC] * n,
        out_specs=[ANY_SPEC] * n,
        out_shape=[jax.ShapeDtypeStruct((N_DEV,) + s.shape, F32) for s in slabs],
        scratch_shapes=[pltpu.SemaphoreType.DMA((n_cp,)), pltpu.SemaphoreType.DMA((n_cp,)),
                        pltpu.SemaphoreType.DMA((n,))],
    )(*slabs)


def _adamw(w, g, m, v):
    m = ADAM_B1 * m + (1.0 - ADAM_B1) * g
    v = ADAM_B2 * v + (1.0 - ADAM_B2) * (g * g)
    m_hat = m / (1.0 - ADAM_B1 ** ADAM_STEP)
    v_hat = v / (1.0 - ADAM_B2 ** ADAM_STEP)
    return -ADAM_LR * (m_hat / (jnp.sqrt(v_hat) + ADAM_EPS) + ADAM_WD * w), m, v


ADAM_TC = 256


def _adam_big(slots, w, m, v, cuts_columns, *, name):
    layers, n, nj = len(slots), slots[0].shape[1], D_MODEL // ADAM_TC

    def body(*refs):
        s_refs = refs[:layers]
        w_ref, m_ref, v_ref, g_ref, d_ref, nm_ref, nv_ref, acc_ref = refs[layers:]
        for ll in range(layers):
            @pl.when(pl.program_id(0) == ll)
            def _(s_ref=s_refs[ll]):
                g = s_ref[0].astype(F32)
                for s in range(1, N_DEV):
                    g = g + s_ref[s].astype(F32)
                acc_ref[...] = g

        g = acc_ref[...].T if cuts_columns else acc_ref[...]
        g_ref[...] = g
        d_ref[...], nm_ref[...], nv_ref[...] = _adamw(w_ref[...], g, m_ref[...], v_ref[...])

    def slot_spec(ll):
        return pl.BlockSpec((N_DEV, n, ADAM_TC),
                            lambda l, j: (0, 0, jnp.where(l < ll, 0, jnp.where(l > ll, nj - 1, j))))

    if cuts_columns:
        w_spec = pl.BlockSpec((None, ADAM_TC, n), lambda l, j: (l, j, 0))
    else:
        w_spec = pl.BlockSpec((None, n, ADAM_TC), lambda l, j: (l, 0, j))
    return pl.pallas_call(
        body,
        name=name,
        grid=(layers, nj),
        in_specs=[slot_spec(ll) for ll in range(layers)] + [w_spec] * 3,
        out_specs=[w_spec] * 4,
        out_shape=[jax.ShapeDtypeStruct(w.shape, F32)] * 4,
        scratch_shapes=[pltpu.VMEM((n, ADAM_TC), F32)],
        compiler_params=_cp("arbitrary", "arbitrary"),
    )(*slots, w, m, v)


def _adam_slabs(slots, ws, ms, vs):
    n = len(slots)

    def body(*refs):
        ins, outs = refs[:4 * n], refs[4 * n:]
        for k in range(n):
            s_ref, w_ref, m_ref, v_ref = ins[k], ins[n + k], ins[2 * n + k], ins[3 * n + k]
            g = s_ref[0]
            for s in range(1, N_DEV):
                g = g + s_ref[s]
            outs[4 * k][...] = g
            outs[4 * k + 1][...], outs[4 * k + 2][...], outs[4 * k + 3][...] = _adamw(w_ref[...], g, m_ref[...], v_ref[...])

    res = pl.pallas_call(
        body,
        name="small_adamw",
        out_shape=[jax.ShapeDtypeStruct(w.shape, F32) for w in ws for _ in range(4)],
        compiler_params=pltpu.CompilerParams(vmem_limit_bytes=VMEM_LIMIT_BYTES),
    )(*slots, *ws, *ms, *vs)
    return [res[4 * k:4 * k + 4] for k in range(n)]


def _adam_vecs(gs, ws, ms, vs):
    n = len(gs)

    def body(*refs):
        ins, outs = refs[:4 * n], refs[4 * n:]
        for k in range(n):
            outs[3 * k][...], outs[3 * k + 1][...], outs[3 * k + 2][...] = _adamw(
                ins[n + k][...], ins[k][...], ins[2 * n + k][...], ins[3 * n + k][...])

    res = pl.pallas_call(
        body,
        name="ln_adamw",
        out_shape=[jax.ShapeDtypeStruct(w.shape, F32) for w in ws for _ in range(3)],
        compiler_params=pltpu.CompilerParams(vmem_limit_bytes=VMEM_LIMIT_BYTES),
    )(*gs, *ws, *ms, *vs)
    return [res[3 * k:3 * k + 3] for k in range(n)]


SLAB_AT = dict(mem_norm=0, lb_logits=1, ffn1_norm=4, mix_norm=6, hgrn_gnorm=8, gmlp_ln_g=9, gmlp_ln_b=11,
               gmlp_b_s=13, ffn2_norm=14, final_norm=16)
SLAB_ROWS = 24
SMALL_SHARDED = ("gmlp_ln_g", "gmlp_ln_b")


def _pack_slab(parts, *, name):
    flat, plan = [], []
    for pname, at in SLAB_AT.items():
        for a in parts.get(pname, ()):
            flat.append(a)
            plan.append((at, a.shape))
            at += max(1, a.shape[0] * a.shape[1] // D_MODEL)

    def body(*refs):
        o_ref = refs[-1]
        o_ref[...] = jnp.zeros_like(o_ref)
        for ref, (at, (r, w)) in zip(refs, plan):
            if w == D_MODEL or r == 1 and w < D_MODEL:
                o_ref[at:at + r, 0:w] = ref[...]
            elif w < D_MODEL:
                for j in range(r):
                    o_ref[at:at + 1, j * w:(j + 1) * w] = ref[j:j + 1, :]
            else:
                for j in range(w // D_MODEL):
                    o_ref[at + j:at + j + 1, :] = ref[:, j * D_MODEL:(j + 1) * D_MODEL]

    return pl.pallas_call(
        body,
        name=name,
        out_shape=jax.ShapeDtypeStruct((SLAB_ROWS, D_MODEL), F32),
        compiler_params=pltpu.CompilerParams(vmem_limit_bytes=VMEM_LIMIT_BYTES),
    )(*flat)


def _unpack_slab(slab, shapes):
    out = {}
    for pname, at in SLAB_AT.items():
        if pname in SMALL_SHARDED:
            continue
        size = math.prod(shapes[pname])
        rows = max(1, size // D_MODEL)
        out[pname] = slab[at:at + rows].reshape(-1)[:size].reshape(shapes[pname])
    return out


def _ffn_fwd(x, norm_g, block, layer, full, get_weights):
    tag = f"l{layer}_{block}"
    full.update(get_weights((layer, f"{block}_in"), (x,)))
    h = _rms_fwd(x, norm_g, deps=full.pop("deps", ()), name=f"{tag}_norm")
    z = _mm(h, full[(f"{block}_w_in", layer)], tb=True, tm=1024, tn=512, tk=D_MODEL, out_dtype=F32, name=f"{tag}_in")
    act = _swiglu_fwd(z, name=f"{tag}_act")
    full.update(get_weights((layer, f"{block}_out"), (act,)))
    y = _mm(act, full[(f"{block}_w_out", layer)], tm=512, tn=D_MODEL, tk=D_FF, out_dtype=F32, res=x, scale=0.5,
            deps=full.pop("deps", ()), name=f"{tag}_out")
    return y, (x, h, z, act)


def _ffn_bwd(dy, saved, norm_g, w_in_t, w_out, tag, deps=()):
    x, h, z, act = saved
    dw_out = _mm(act, dy, ta=True, tm=1408, tn=D_MODEL, tk=1024, out_dtype=BF16, scale=0.5, deps=deps,
                 name=f"{tag}_out_wgrad")
    dact = _mm(dy, w_out, tb=True, tm=1024, tn=1408, tk=D_MODEL, out_dtype=F32, name=f"{tag}_out_dgrad")
    dz = _swiglu_bwd(z, dact, scale=0.5, name=f"{tag}_act_bwd")
    dw_in_t = _mm(dz, h, ta=True, tm=512, tn=D_MODEL, tk=1024, out_dtype=BF16, name=f"{tag}_in_wgrad")
    dh = _mm(dz, w_in_t, tm=1024, tn=D_MODEL, tk=512, out_dtype=F32, name=f"{tag}_in_dgrad")
    dx, dg = _rms_bwd(x, norm_g, dh, dy, name=f"{tag}_norm_bwd")
    return dx, dg, dw_in_t, dw_out


def kernel(x, mem, mem_norm, lb_logits, ffn1_norm, ffn1_w_in, ffn1_w_out, mix_norm, mem_w_kv, hgrn_w_in, hgrn_gnorm, hgrn_w_out, gmlp_w_in, gmlp_ln_g, gmlp_ln_b, gmlp_w_s, gmlp_b_s, gmlp_w_out, ffn2_norm, ffn2_w_in, ffn2_w_out, final_norm, loss_target, m_mem_norm, m_lb_logits, m_ffn1_norm, m_ffn1_w_in, m_ffn1_w_out, m_mix_norm, m_mem_w_kv, m_hgrn_w_in, m_hgrn_gnorm, m_hgrn_w_out, m_gmlp_w_in, m_gmlp_ln_g, m_gmlp_ln_b, m_gmlp_w_s, m_gmlp_b_s, m_gmlp_w_out, m_ffn2_norm, m_ffn2_w_in, m_ffn2_w_out, m_final_norm, v_mem_norm, v_lb_logits, v_ffn1_norm, v_ffn1_w_in, v_ffn1_w_out, v_mix_norm, v_mem_w_kv, v_hgrn_w_in, v_hgrn_gnorm, v_hgrn_w_out, v_gmlp_w_in, v_gmlp_ln_g, v_gmlp_ln_b, v_gmlp_w_s, v_gmlp_b_s, v_gmlp_w_out, v_ffn2_norm, v_ffn2_w_in, v_ffn2_w_out, v_final_norm):
    weights = dict(mem_norm=mem_norm, lb_logits=lb_logits, ffn1_norm=ffn1_norm, ffn1_w_in=ffn1_w_in, ffn1_w_out=ffn1_w_out, mix_norm=mix_norm, mem_w_kv=mem_w_kv, hgrn_w_in=hgrn_w_in, hgrn_gnorm=hgrn_gnorm, hgrn_w_out=hgrn_w_out, gmlp_w_in=gmlp_w_in, gmlp_ln_g=gmlp_ln_g, gmlp_ln_b=gmlp_ln_b, gmlp_w_s=gmlp_w_s, gmlp_b_s=gmlp_b_s, gmlp_w_out=gmlp_w_out, ffn2_norm=ffn2_norm, ffn2_w_in=ffn2_w_in, ffn2_w_out=ffn2_w_out, final_norm=final_norm)
    mom_m = dict(mem_norm=m_mem_norm, lb_logits=m_lb_logits, ffn1_norm=m_ffn1_norm, ffn1_w_in=m_ffn1_w_in, ffn1_w_out=m_ffn1_w_out, mix_norm=m_mix_norm, mem_w_kv=m_mem_w_kv, hgrn_w_in=m_hgrn_w_in, hgrn_gnorm=m_hgrn_gnorm, hgrn_w_out=m_hgrn_w_out, gmlp_w_in=m_gmlp_w_in, gmlp_ln_g=m_gmlp_ln_g, gmlp_ln_b=m_gmlp_ln_b, gmlp_w_s=m_gmlp_w_s, gmlp_b_s=m_gmlp_b_s, gmlp_w_out=m_gmlp_w_out, ffn2_norm=m_ffn2_norm, ffn2_w_in=m_ffn2_w_in, ffn2_w_out=m_ffn2_w_out, final_norm=m_final_norm)
    mom_v = dict(mem_norm=v_mem_norm, lb_logits=v_lb_logits, ffn1_norm=v_ffn1_norm, ffn1_w_in=v_ffn1_w_in, ffn1_w_out=v_ffn1_w_out, mix_norm=v_mix_norm, mem_w_kv=v_mem_w_kv, hgrn_w_in=v_hgrn_w_in, hgrn_gnorm=v_hgrn_gnorm, hgrn_w_out=v_hgrn_w_out, gmlp_w_in=v_gmlp_w_in, gmlp_ln_g=v_gmlp_ln_g, gmlp_ln_b=v_gmlp_ln_b, gmlp_w_s=v_gmlp_w_s, gmlp_b_s=v_gmlp_b_s, gmlp_w_out=v_gmlp_w_out, ffn2_norm=v_ffn2_norm, ffn2_w_in=v_ffn2_w_in, ffn2_w_out=v_ffn2_w_out, final_norm=v_final_norm)
    order = list(weights)
    _, _, _, me = _mesh_pos()
    me_arr = jnp.reshape(me, (1,)).astype(jnp.int32)
    cuts = {name: c for name, c, _, _ in GROUPS}

    mix1 = (("mem_w_kv", 1), ("gmlp_w_in", 0), ("gmlp_w_out", 0))
    gather_plan = (
        ((0, "ffn1_in"), (("ffn1_w_in", 0),), None),
        ((0, "ffn1_out"), (("ffn1_w_out", 0),), 0),
        ((0, "mix_in"), (("mem_w_kv", 0), ("hgrn_w_in", 0)), 0),
        ((0, "mix_out"), (("hgrn_w_out", 0),), 2),
        ((0, "ffn2_in"), _stage_pieces(0, "ffn2"), 2),
        ((1, "ffn1_in"), _stage_pieces(1, "ffn1"), 2),
        ((1, "mix_in"), mix1, 4),
        ((1, "ffn2_in"), _stage_pieces(1, "ffn2"), 5),
    )
    gather = {}

    def start_gather(k, deps):
        use, pieces, _ = gather_plan[k]
        lands = [_place_rows(weights[name], l, cuts[name], me_arr, name=f"place_{name}_{l}") for name, l in pieces]
        if pieces is mix1:
            lands.append(_place_ln(gmlp_ln_g, gmlp_ln_b, me_arr))
        send_sems, recv_sems, *thru, token = _copies_start(lands, lands, gather=True, deps=deps,
                                                           name=f"gather_start_l{use[0]}_{use[1]}")
        gather[use] = (k, thru, send_sems, recv_sems)
        return token

    start_gather(0, ())

    def get_weights(use, after):
        if use not in gather:
            return {}
        k, thru, send_sems, recv_sems = gather[use]
        outs = _copies_wait(thru, send_sems, recv_sems, after, n_lands=len(thru), gather=True,
                            name=f"gather_wait_l{use[0]}_{use[1]}")
        pieces = gather_plan[k][1]
        w = {p: o.reshape(N_DEV * o.shape[1], D_MODEL) for p, o in zip(pieces, outs)}
        w["deps"] = tuple(start_gather(later, (outs[0],))
                          for later, (_, _, trigger) in enumerate(gather_plan) if trigger == k)
        if pieces is mix1:
            w["ln_g"] = outs[-1][:, 0, :].reshape(1, GM_WIDTH)
            w["ln_b"] = outs[-1][:, 1, :].reshape(1, GM_WIDTH)
        return w

    scatter = {}

    def put_grads(st, grads):
        if st == "w_s":
            land = _place_slab(grads.reshape(GM_GROUPS * GM_CHUNK, GM_CHUNK), me_arr, name="w_s_place")
            send_sems, recv_sems, *thru, token = _copies_start([land], [land], gather=True, name="w_s_start")
            scatter[st] = (thru, send_sems, recv_sems)
            return (token,)
        views = [grads[p].reshape(N_DEV, -1, D_MODEL) for p in _stage_pieces(*st)]
        recv = _place_own(views, me_arr, name=f"scatter_place_l{st[0]}_{st[1]}")
        send_sems, recv_sems, *thru, token = _copies_start(views, recv, gather=False,
                                                           name=f"scatter_start_l{st[0]}_{st[1]}")
        scatter[st] = (thru, send_sems, recv_sems)
        return (token,)

    dx, small, loss_part = _step_local(
        x, mem, loss_target, get_weights, put_grads, mem_norm, lb_logits, ffn1_norm, mix_norm, hgrn_gnorm,
        gmlp_w_s, gmlp_b_s, ffn2_norm, final_norm)

    def slots_of(blk, after):
        slots = {}
        for i in (1, 0):
            thru, send_sems, recv_sems = scatter[(i, blk)]
            outs = _copies_wait(thru, send_sems, recv_sems, after, n_lands=len(thru) // 2, gather=False,
                                name=f"scatter_wait_l{i}_{blk}")
            slots.update(zip(_stage_pieces(i, blk), outs))
        return slots

    grad, delta, new_m, new_v = {}, {}, {}, {}

    def adam_groups(slots, names):
        for name in names:
            layers = GROUP_LAYERS[name]
            grad[name], delta[name], new_m[name], new_v[name] = _adam_big(
                [slots[(name, l)] for l in range(layers)], weights[name], mom_m[name], mom_v[name], cuts[name],
                name=f"{name}_adamw")

    adam_groups(slots_of("ffn2", (dx,)), ("ffn2_w_in", "ffn2_w_out"))
    adam_groups(slots_of("mix", (delta["ffn2_w_out"],)),
                ("mem_w_kv", "gmlp_w_in", "gmlp_w_out", "hgrn_w_in", "hgrn_w_out"))

    def small_parts(src):
        parts = {n: [src[n].reshape(-1, src[n].shape[-1])] for n in SLAB_AT if n not in SMALL_SHARDED}
        return parts

    w_s_rows = lambda a: a.reshape(GM_GROUPS * GM_CHUNK, GM_CHUNK)
    (slab_slots,) = _exchange_small([_pack_slab(small, name="pack_small_grads")])
    thru, send_sems, recv_sems = scatter["w_s"]
    (ws_slots,) = _copies_wait(thru, send_sems, recv_sems, (slab_slots,), n_lands=1, gather=True, name="w_s_wait")
    (g_slab, d_slab, nm_slab, nv_slab), (g_ws, d_ws, nm_ws, nv_ws) = _adam_slabs(
        [slab_slots, ws_slots],
        [_pack_slab(small_parts(weights), name="pack_small_w"), w_s_rows(gmlp_w_s)],
        [_pack_slab(small_parts(mom_m), name="pack_small_m"), w_s_rows(m_gmlp_w_s)],
        [_pack_slab(small_parts(mom_v), name="pack_small_v"), w_s_rows(v_gmlp_w_s)])
    shapes = {n: weights[n].shape for n in SLAB_AT}
    for out, slab, ws in ((grad, g_slab, g_ws), (delta, d_slab, d_ws), (new_m, nm_slab, nm_ws), (new_v, nv_slab, nv_ws)):
        out.update(_unpack_slab(slab, shapes))
        out["gmlp_w_s"] = ws.reshape(gmlp_w_s.shape)
    blk = GM_WIDTH // N_DEV
    g_ln = [lax.dynamic_slice(g_slab[SLAB_AT[n]:SLAB_AT[n] + 2].reshape(1, GM_WIDTH), (0, me * blk), (1, blk))
            for n in SMALL_SHARDED]
    ln_out = _adam_vecs(g_ln, [weights[n] for n in SMALL_SHARDED], [mom_m[n] for n in SMALL_SHARDED],
                        [mom_v[n] for n in SMALL_SHARDED])
    for n, g, (d, nm, nv) in zip(SMALL_SHARDED, g_ln, ln_out):
        grad[n], delta[n], new_m[n], new_v[n] = g, d, nm, nv

    adam_groups(slots_of("ffn1", (delta["hgrn_w_out"], d_slab)), ("ffn1_w_in", "ffn1_w_out"))

    loss = lax.psum(loss_part[0, 0], MESH_AXES)
    grad_x = dx.reshape(B_LOC, SEQ, D_MODEL)
    return (loss, grad_x, *[grad[n] for n in order], *[delta[n] for n in order],
            *[new_m[n] for n in order], *[new_v[n] for n in order])


def _step_local(x, mem, loss_target, get_weights, put_grads, mem_norm, lb_logits, ffn1_norm, mix_norm, hgrn_gnorm,
                gmlp_w_s, gmlp_b_s, ffn2_norm, final_norm):
    w_s = gmlp_w_s[0]
    b_st = gmlp_b_s[0].T

    xs = x.reshape(N_TOK, D_MODEL)
    mem2d = mem.reshape(B_LOC * MEM_LEN, D_MODEL)
    mem_g = mem_norm.reshape(1, D_MODEL)
    saved, full = [], {}
    memn = _rms_fwd(mem2d, mem_g, name="mem_norm_fwd")
    for i in range(2):
        xs, s_ffn1 = _ffn_fwd(xs, ffn1_norm[i:i + 1], "ffn1", i, full, get_weights)
        full.update(get_weights((i, "mix_in"), (xs,)))
        mixer = "hgrn" if i == 0 else "gmlp"
        hm = _rms_fwd(xs, mix_norm[i:i + 1], deps=full.pop("deps", ()), name=f"l{i}_mix_norm")
        kv = _mm(memn, full[("mem_w_kv", i)], tb=True, tm=512, tn=512, tk=D_MODEL, out_dtype=F32, name=f"l{i}_mem_kv")
        zm = _mm(hm, full[(f"{mixer}_w_in", 0)], tb=True, tm=1024, tn=512, tk=D_MODEL, out_dtype=F32, name=f"l{i}_mix_in")
        o_mem = _attn_fwd(zm, kv, name=f"l{i}_attn")
        if i == 0:
            cat, o_pre, s_all = _hgrn_fwd(zm, o_mem, lb_logits, hgrn_gnorm)
            mix_saved = (o_pre, s_all)
        else:
            cat = _gmlp_fwd(zm, o_mem, full["ln_g"], full["ln_b"], w_s, b_st)
            mix_saved = ()
        x_mix = xs
        full.update(get_weights((i, "mix_out"), (cat,)))
        xs = _mm(cat, full[(f"{mixer}_w_out", 0)], tm=512, tn=D_MODEL, tk=cat.shape[1], out_dtype=F32, res=xs,
                 deps=full.pop("deps", ()), name=f"l{i}_mix_out")
        xs, s_ffn2 = _ffn_fwd(xs, ffn2_norm[i:i + 1], "ffn2", i, full, get_weights)
        saved.append((s_ffn1, (x_mix, hm, kv, zm, cat, mix_saved), s_ffn2))

    dx, d_final, loss_part = _loss_head(xs, final_norm.reshape(1, D_MODEL), loss_target.reshape(N_TOK, D_MODEL))

    small = {"final_norm": [d_final]}
    d_ffn1, d_ffn2, d_mix = [None, None], [None, None], [None, None]
    dmemn = jnp.zeros((B_LOC * MEM_LEN, D_MODEL), F32)
    deps = ()
    for i in (1, 0):
        s_ffn1, (x_mix, hm, kv, zm, cat, mix_saved), s_ffn2 = saved[i]
        dx, d_ffn2[i], dw_in_t, dw_out = _ffn_bwd(
            dx, s_ffn2, ffn2_norm[i:i + 1], full[("ffn2_w_in", i)], full[("ffn2_w_out", i)], f"l{i}_ffn2", deps)
        deps = put_grads((i, "ffn2"), {("ffn2_w_in", i): dw_in_t, ("ffn2_w_out", i): dw_out})
        mixer = "hgrn" if i == 0 else "gmlp"
        w_in_t, w_out = full[(f"{mixer}_w_in", 0)], full[(f"{mixer}_w_out", 0)]
        width = cat.shape[1]
        g_mix = {}
        g_mix[(f"{mixer}_w_out", 0)] = _mm(cat, dx, ta=True, tm=width // 2, tn=D_MODEL, tk=1024, out_dtype=BF16,
                                           deps=deps, name=f"l{i}_mix_out_wgrad")
        dcat = _mm(dx, w_out, tb=True, tm=1024, tn=width // 2, tk=D_MODEL, out_dtype=F32, name=f"l{i}_mix_out_dgrad")
        dq, dk, dv = _attn_bwd(zm, kv, dcat, do_off=width - XA_HEADS * XA_DIM, name=f"l{i}_attn_bwd")
        if i == 0:
            dzm, dlbl, dgn = _hgrn_bwd(zm, mix_saved[0], dcat, dq, mix_saved[1], lb_logits, hgrn_gnorm)
            small["lb_logits"], small["hgrn_gnorm"] = [dlbl], [dgn]
            deps = ()
        else:
            dzm, dws, dbt, dlng, dlnb = _gmlp_bwd(zm, dcat, dq, full["ln_g"], full["ln_b"], w_s, b_st)
            small["gmlp_b_s"], small["gmlp_ln_g"], small["gmlp_ln_b"] = [dbt.T], [dlng], [dlnb]
            deps = put_grads("w_s", dws)
        g_mix[(f"{mixer}_w_in", 0)] = _mm(dzm, hm, ta=True, tm=512, tn=D_MODEL, tk=1024, out_dtype=BF16, deps=deps,
                                          name=f"l{i}_mix_in_wgrad")
        dkv = jnp.concatenate([dk, dv], axis=1)
        g_mix[("mem_w_kv", i)] = _mm(dkv, memn, ta=True, tm=512, tn=D_MODEL, tk=B_LOC * MEM_LEN, out_dtype=BF16,
                                     name=f"l{i}_mem_kv_wgrad")
        deps = put_grads((i, "mix"), g_mix)
        dh = _mm(dzm, w_in_t, tm=1024, tn=D_MODEL, tk=512, out_dtype=F32, deps=deps, name=f"l{i}_mix_in_dgrad")
        dx, d_mix[i] = _rms_bwd(x_mix, mix_norm[i:i + 1], dh, dx, name=f"l{i}_mix_norm_bwd")
        dmemn = _mm(dkv, full[("mem_w_kv", i)], tm=B_LOC * MEM_LEN, tn=D_MODEL, tk=512, out_dtype=F32, res=dmemn,
                    name=f"l{i}_mem_kv_dgrad")
        dx, d_ffn1[i], dw_in_t, dw_out = _ffn_bwd(
            dx, s_ffn1, ffn1_norm[i:i + 1], full[("ffn1_w_in", i)], full[("ffn1_w_out", i)], f"l{i}_ffn1")
        deps = put_grads((i, "ffn1"), {("ffn1_w_in", i): dw_in_t, ("ffn1_w_out", i): dw_out})
    _, dmem_g = _rms_bwd(mem2d, mem_g, dmemn, dmemn, deps=deps, name="mem_norm_bwd")
    small.update(mem_norm=[dmem_g], ffn1_norm=d_ffn1, ffn2_norm=d_ffn2, mix_norm=d_mix)
    return dx, small, loss_part
```

```python
import functools
import math

import jax
import jax.numpy as jnp
from jax import lax
from jax.experimental import pallas as pl
from jax.experimental.pallas import tpu as pltpu

F32 = jnp.float32
BF16 = jnp.bfloat16

D_MODEL = 1024
SEQ = 2048
B_LOC = 2
N_TOK = B_LOC * SEQ
MEM_LEN = 256
N_DEV = 8
EPS = 1e-6
D_FF = 2816
HG_HEADS = 8
HG_DIM = 128
HG_CHUNK = 64
HG_NCHUNK = SEQ // HG_CHUNK
GM_CHUNK = 128
GM_GROUPS = 8
GM_WIDTH = 2048
GM_GDIM = GM_WIDTH // GM_GROUPS
XA_HEADS = 4
XA_DIM = 256
XA_OFF = 4096

ADAM_LR = 0.001
ADAM_B1 = 0.9
ADAM_B2 = 0.999
ADAM_EPS = 1e-08
ADAM_WD = 0.01
ADAM_STEP = 10

VMEM_LIMIT_BYTES = 56 * 1024 * 1024
MESH_AXES = ("x", "y", "c")

GROUPS = (
    ("ffn1_w_in", True, 2, 704),
    ("ffn1_w_out", False, 2, 352),
    ("mem_w_kv", True, 2, 256),
    ("hgrn_w_in", True, 1, 640),
    ("hgrn_w_out", False, 1, 256),
    ("gmlp_w_in", True, 1, 640),
    ("gmlp_w_out", False, 1, 384),
    ("ffn2_w_in", True, 2, 704),
    ("ffn2_w_out", False, 2, 352),
)
GROUP_LAYERS = {name: layers for name, _, layers, _ in GROUPS}


def _stage_pieces(layer, block):
    if block == "mix":
        mixer = "hgrn" if layer == 0 else "gmlp"
        return (("mem_w_kv", layer), (f"{mixer}_w_in", 0), (f"{mixer}_w_out", 0))
    return ((f"{block}_w_in", layer), (f"{block}_w_out", layer))


ANY_SPEC = pl.BlockSpec(memory_space=pl.ANY)
HBM_SPEC = pl.BlockSpec(memory_space=pltpu.HBM)
SEM_SPEC = pl.BlockSpec(memory_space=pltpu.SEMAPHORE)


def _cp(*sem):
    return pltpu.CompilerParams(dimension_semantics=sem, vmem_limit_bytes=VMEM_LIMIT_BYTES)


def _sigmoid(x):
    return 1.0 / (1.0 + jnp.exp(-x))


def _gelu_parts(x):
    cdf = 0.5 * (1.0 + lax.erf(x * (1.0 / math.sqrt(2.0))))
    pdf = jnp.exp(-0.5 * x * x) * (1.0 / math.sqrt(2.0 * math.pi))
    return x * cdf, cdf + x * pdf


def _mm(a, b, *, ta=False, tb=False, tm, tn, tk, out_dtype, res=None, scale=1.0, deps=(), name):
    m, k = (a.shape[1], a.shape[0]) if ta else a.shape
    n, kb = b.shape if tb else (b.shape[1], b.shape[0])
    assert k == kb and m % tm == 0 and n % tn == 0 and k % tk == 0, (name, a.shape, b.shape)
    nk = k // tk
    dn = (((0 if ta else 1,), (1 if tb else 0,)), ((), ()))
    n_in = 2 + (res is not None) + len(deps)

    def body(*refs):
        a_ref, b_ref = refs[:2]
        r_ref = refs[2] if res is not None else None
        o_ref, scr = refs[n_in], refs[n_in + 1:]
        p = lax.dot_general(a_ref[...].astype(BF16), b_ref[...].astype(BF16), dn, preferred_element_type=F32)

        def finish(acc):
            if scale != 1.0:
                acc = scale * acc
            if r_ref is not None:
                acc = r_ref[...] + acc
            o_ref[...] = acc.astype(out_dtype)

        if nk == 1:
            finish(p)
        else:
            acc_ref = scr[0]
            kk = pl.program_id(2)

            @pl.when(kk == 0)
            def _():
                acc_ref[...] = p

            @pl.when(kk > 0)
            def _():
                acc_ref[...] += p

            @pl.when(kk == nk - 1)
            def _():
                finish(acc_ref[...])

    a_spec = pl.BlockSpec((tk, tm), lambda i, j, kk: (kk, i)) if ta else pl.BlockSpec((tm, tk), lambda i, j, kk: (i, kk))
    b_mode = dict(pipeline_mode=pl.Buffered(1)) if n == tn and nk == 1 else {}
    if tb:
        b_spec = pl.BlockSpec((tn, tk), lambda i, j, kk: (j, kk), **b_mode)
    else:
        b_spec = pl.BlockSpec((tk, tn), lambda i, j, kk: (kk, j), **b_mode)
    o_spec = pl.BlockSpec((tm, tn), lambda i, j, kk: (i, j))
    in_specs = [a_spec, b_spec] + ([o_spec] if res is not None else []) + [ANY_SPEC] * len(deps)
    args = (a, b) + ((res,) if res is not None else ()) + tuple(deps)
    return pl.pallas_call(
        body,
        name=name,
        grid=(m // tm, n // tn, nk),
        in_specs=in_specs,
        out_specs=o_spec,
        out_shape=jax.ShapeDtypeStruct((m, n), out_dtype),
        scratch_shapes=[pltpu.VMEM((tm, tn), F32)] if nk > 1 else [],
        compiler_params=_cp("parallel", "parallel", "arbitrary"),
    )(*args)


def _rms_fwd(x, g, *, name, deps=(), tm=512):
    rows = x.shape[0]

    def body(x_ref, g_ref, *rest):
        o_ref = rest[len(deps)]
        xv = x_ref[...]
        r = lax.rsqrt(jnp.mean(xv * xv, axis=-1, keepdims=True) + EPS)
        o_ref[...] = (xv * r * g_ref[...]).astype(BF16)

    row = pl.BlockSpec((tm, D_MODEL), lambda i: (i, 0))
    return pl.pallas_call(
        body,
        name=name,
        grid=(rows // tm,),
        in_specs=[row, pl.BlockSpec((1, D_MODEL), lambda i: (0, 0))] + [ANY_SPEC] * len(deps),
        out_specs=row,
        out_shape=jax.ShapeDtypeStruct((rows, D_MODEL), BF16),
        compiler_params=_cp("parallel"),
    )(x, g, *deps)


def _rms_bwd(x, g, dh, dres, *, name, deps=(), tm=512):
    rows = x.shape[0]

    def body(x_ref, g_ref, dh_ref, dres_ref, *rest):
        dx_ref, dx16_ref, dg_ref = rest[len(deps):]
        xv = x_ref[...]
        r = lax.rsqrt(jnp.mean(xv * xv, axis=-1, keepdims=True) + EPS)
        xhat = xv * r
        dhv = dh_ref[...]
        part = jnp.sum(dhv * xhat, axis=0, keepdims=True)

        @pl.when(pl.program_id(0) == 0)
        def _():
            dg_ref[...] = part

        @pl.when(pl.program_id(0) > 0)
        def _():
            dg_ref[...] += part

        dxh = dhv * g_ref[...]
        dx = dres_ref[...] + r * (dxh - xhat * jnp.mean(dxh * xhat, axis=-1, keepdims=True))
        dx_ref[...] = dx
        dx16_ref[...] = dx.astype(BF16)

    row = pl.BlockSpec((tm, D_MODEL), lambda i: (i, 0))
    vec = pl.BlockSpec((1, D_MODEL), lambda i: (0, 0))
    return pl.pallas_call(
        body,
        name=name,
        grid=(rows // tm,),
        in_specs=[row, vec, row, row] + [ANY_SPEC] * len(deps),
        out_specs=[row, row, vec],
        out_shape=[jax.ShapeDtypeStruct((rows, D_MODEL), F32), jax.ShapeDtypeStruct((rows, D_MODEL), BF16),
                   jax.ShapeDtypeStruct((1, D_MODEL), F32)],
        compiler_params=_cp("arbitrary"),
    )(x, g, dh, dres, *deps)


def _swiglu_fwd(z, *, name, tm=512):
    def body(g_ref, u_ref, o_ref):
        gv = g_ref[...]
        o_ref[...] = (gv * _sigmoid(gv) * u_ref[...]).astype(BF16)

    return pl.pallas_call(
        body,
        name=name,
        grid=(N_TOK // tm,),
        in_specs=[pl.BlockSpec((tm, D_FF), lambda i: (i, 0)), pl.BlockSpec((tm, D_FF), lambda i: (i, 1))],
        out_specs=pl.BlockSpec((tm, D_FF), lambda i: (i, 0)),
        out_shape=jax.ShapeDtypeStruct((N_TOK, D_FF), BF16),
        compiler_params=_cp("parallel"),
    )(z, z)


def _swiglu_bwd(z, dact, *, scale, name, tm=512):
    def body(g_ref, u_ref, da_ref, o_ref):
        gv = g_ref[...]
        s = _sigmoid(gv)
        da = da_ref[...] * scale
        o_ref[:, :D_FF] = (da * u_ref[...] * (s * (1.0 + gv * (1.0 - s)))).astype(BF16)
        o_ref[:, D_FF:] = (da * (gv * s)).astype(BF16)

    half = lambda j: pl.BlockSpec((tm, D_FF), lambda i: (i, j))
    return pl.pallas_call(
        body,
        name=name,
        grid=(N_TOK // tm,),
        in_specs=[half(0), half(1), half(0)],
        out_specs=pl.BlockSpec((tm, 2 * D_FF), lambda i: (i, 0)),
        out_shape=jax.ShapeDtypeStruct((N_TOK, 2 * D_FF), BF16),
        compiler_params=_cp("parallel"),
    )(z, z, dact)


def _loss_head(x, g, target, *, tm=512):
    def body(x_ref, g_ref, t_ref, dx_ref, dx16_ref, dg_ref, loss_ref):
        xv = x_ref[...]
        gv = g_ref[...]
        r = lax.rsqrt(jnp.mean(xv * xv, axis=-1, keepdims=True) + EPS)
        xhat = xv * r
        err = xhat * gv - t_ref[...]
        loss_part = jnp.zeros((1, 128), F32) + 0.5 * jnp.sum(jnp.mean(err * err, axis=-1, keepdims=True))
        dy = err * (1.0 / D_MODEL)
        dg_part = jnp.sum(dy * xhat, axis=0, keepdims=True)

        @pl.when(pl.program_id(0) == 0)
        def _():
            dg_ref[...] = dg_part
            loss_ref[...] = loss_part

        @pl.when(pl.program_id(0) > 0)
        def _():
            dg_ref[...] += dg_part
            loss_ref[...] += loss_part

        dxh = dy * gv
        dx = r * (dxh - xhat * jnp.mean(dxh * xhat, axis=-1, keepdims=True))
        dx_ref[...] = dx
        dx16_ref[...] = dx.astype(BF16)

    row = pl.BlockSpec((tm, D_MODEL), lambda i: (i, 0))
    vec = pl.BlockSpec((1, D_MODEL), lambda i: (0, 0))
    return pl.pallas_call(
        body,
        name="loss_head",
        grid=(N_TOK // tm,),
        in_specs=[row, vec, row],
        out_specs=[row, row, vec, pl.BlockSpec((1, 128), lambda i: (0, 0))],
        out_shape=[
            jax.ShapeDtypeStruct((N_TOK, D_MODEL), F32),
            jax.ShapeDtypeStruct((N_TOK, D_MODEL), BF16),
            jax.ShapeDtypeStruct((1, D_MODEL), F32),
            jax.ShapeDtypeStruct((1, 128), F32),
        ],
        compiler_params=_cp("arbitrary"),
    )(x, g, target)


_NT = (((1,), (1,)), ((), ()))
_TN = (((0,), (0,)), ((), ()))
XA_TQ = 1024
XA_SCALE = XA_DIM ** -0.5


def _attn_probs(q16, k16):
    s = lax.dot_general(q16, k16, _NT, preferred_element_type=F32) * XA_SCALE
    e = jnp.exp(s - jnp.max(s, axis=-1, keepdims=True))
    return e / jnp.sum(e, axis=-1, keepdims=True)


def _attn_fwd(z, kv, *, name):
    nt = SEQ // XA_TQ

    def body(q_ref, k_ref, v_ref, o_ref):
        p = _attn_probs(q_ref[...].astype(BF16), k_ref[...].astype(BF16))
        o_ref[...] = jnp.dot(p.astype(BF16), v_ref[...].astype(BF16), preferred_element_type=F32).astype(BF16)

    return pl.pallas_call(
        body,
        name=name,
        grid=(B_LOC, XA_HEADS, nt),
        in_specs=[
            pl.BlockSpec((XA_TQ, XA_DIM), lambda b, h, t: (b * nt + t, XA_OFF // XA_DIM + h)),
            pl.BlockSpec((MEM_LEN, XA_DIM), lambda b, h, t: (b, h)),
            pl.BlockSpec((MEM_LEN, XA_DIM), lambda b, h, t: (b, XA_HEADS + h)),
        ],
        out_specs=pl.BlockSpec((XA_TQ, XA_DIM), lambda b, h, t: (b * nt + t, h)),
        out_shape=jax.ShapeDtypeStruct((N_TOK, XA_HEADS * XA_DIM), BF16),
        compiler_params=_cp("parallel", "parallel", "arbitrary"),
    )(z, kv, kv)


def _attn_bwd(z, kv, dcat, *, do_off, name):
    nt = SEQ // XA_TQ

    def body(q_ref, k_ref, v_ref, do_ref, dq_ref, dk_ref, dv_ref):
        q16 = q_ref[...].astype(BF16)
        k16 = k_ref[...].astype(BF16)
        v16 = v_ref[...].astype(BF16)
        do16 = do_ref[...].astype(BF16)
        p = _attn_probs(q16, k16)
        dv_part = lax.dot_general(p.astype(BF16), do16, _TN, preferred_element_type=F32)
        dp = lax.dot_general(do16, v16, _NT, preferred_element_type=F32)
        ds16 = (p * (dp - jnp.sum(dp * p, axis=-1, keepdims=True)) * XA_SCALE).astype(BF16)
        dq_ref[...] = jnp.dot(ds16, k16, preferred_element_type=F32).astype(BF16)
        dk_part = lax.dot_general(ds16, q16, _TN, preferred_element_type=F32)

        @pl.when(pl.program_id(2) == 0)
        def _():
            dk_ref[...] = dk_part
            dv_ref[...] = dv_part

        @pl.when(pl.program_id(2) > 0)
        def _():
            dk_ref[...] += dk_part
            dv_ref[...] += dv_part

    qspec = pl.BlockSpec((XA_TQ, XA_DIM), lambda b, h, t: (b * nt + t, XA_OFF // XA_DIM + h))
    kspec = lambda off: pl.BlockSpec((MEM_LEN, XA_DIM), lambda b, h, t: (b, off + h))
    return pl.pallas_call(
        body,
        name=name,
        grid=(B_LOC, XA_HEADS, nt),
        in_specs=[qspec, kspec(0), kspec(XA_HEADS),
                  pl.BlockSpec((XA_TQ, XA_DIM), lambda b, h, t: (b * nt + t, do_off // XA_DIM + h))],
        out_specs=[pl.BlockSpec((XA_TQ, XA_DIM), lambda b, h, t: (b * nt + t, h)), kspec(0), kspec(0)],
        out_shape=[
            jax.ShapeDtypeStruct((N_TOK, XA_HEADS * XA_DIM), BF16),
            jax.ShapeDtypeStruct((B_LOC * MEM_LEN, XA_HEADS * XA_DIM), F32),
            jax.ShapeDtypeStruct((B_LOC * MEM_LEN, XA_HEADS * XA_DIM), F32),
        ],
        compiler_params=_cp("parallel", "parallel", "arbitrary"),
    )(z, kv, kv, dcat)


def _tril(n):
    return lax.broadcasted_iota(jnp.int32, (n, n), 0) >= lax.broadcasted_iota(jnp.int32, (n, n), 1)


def _lower_bound(lbl):
    e = jnp.exp(lbl - jnp.max(lbl, axis=0, keepdims=True))
    p = e / jnp.sum(e, axis=0, keepdims=True)
    return p[0:1, :], p


def _hgrn_gates(zq, zf, lb, tril_f):
    sig = _sigmoid(zf)
    f = lb + (1.0 - lb) * sig
    kk = 1.0 - f
    sq = _sigmoid(zq)
    q = zq * sq
    b = jnp.dot(tril_f, jnp.log(f), preferred_element_type=F32, precision=lax.Precision.HIGHEST)
    bl = b[HG_CHUNK - 1:HG_CHUNK, :]
    return q, sq, sig, f, kk, b, bl


HG_TB = 512
HG_CPB = HG_TB // HG_CHUNK
HG_NT = SEQ // HG_TB
HG_WIDTH = HG_HEADS * HG_DIM


def _head(h, section=0):
    return slice(section * HG_WIDTH + h * HG_DIM, section * HG_WIDTH + (h + 1) * HG_DIM)


def _hgrn_fwd(z, o_mem, lb_logits, gnorm):
    def body(zq_ref, zf_ref, zi_ref, zg_ref, omem_ref, lbl_ref, gn_ref, o_ref, opre_ref, sall_ref, st_ref):
        lb, _ = _lower_bound(lbl_ref[...])
        gn = gn_ref[...]
        mask = _tril(HG_CHUNK)
        tril_f = mask.astype(F32)
        o_ref[:, HG_WIDTH:] = omem_ref[...]

        @pl.when(pl.program_id(1) == 0)
        def _():
            st_ref[...] = jnp.zeros_like(st_ref)

        def chunk(c, carry):
            rows = pl.ds(pl.multiple_of(c * HG_CHUNK, HG_CHUNK), HG_CHUNK)
            q, _, _, _, kk, b, bl = _hgrn_gates(zq_ref[rows, :], zf_ref[rows, :], lb, tril_f)
            v16 = zi_ref[rows, :].astype(BF16)
            qd16 = (q * jnp.exp(b)).astype(BF16)
            ki16 = (kk * jnp.exp(-b)).astype(BF16)
            kd16 = (kk * jnp.exp(bl - b)).astype(BF16)
            ebl = jnp.exp(bl)
            zg = zg_ref[rows, :]
            gate = zg * _sigmoid(zg)
            for h in range(HG_HEADS):
                sl = _head(h)
                a = jnp.where(mask, lax.dot_general(qd16[:, sl], ki16[:, sl], _NT, preferred_element_type=F32), 0.0)
                st = st_ref[h]
                sall_ref[0, h, c] = st
                o = jnp.dot(a.astype(BF16), v16[:, sl], preferred_element_type=F32) + lax.dot_general(
                    qd16[:, sl], st.astype(BF16), _NT, preferred_element_type=F32)
                st_ref[h] = st * ebl[:, sl] + lax.dot_general(v16[:, sl], kd16[:, sl], _TN, preferred_element_type=F32)
                opre_ref[rows, sl] = o
                r = lax.rsqrt(jnp.mean(o * o, axis=-1, keepdims=True) + EPS)
                o_ref[rows, sl] = ((o * r * gn) * gate[:, sl]).astype(BF16)
            return carry

        lax.fori_loop(0, HG_CPB, chunk, 0)

    zspec = lambda s: pl.BlockSpec((HG_TB, HG_WIDTH), lambda b, t: (b * HG_NT + t, s))
    return pl.pallas_call(
        body,
        name="hgrn_fwd",
        grid=(B_LOC, HG_NT),
        in_specs=[zspec(0), zspec(1), zspec(2), zspec(3), zspec(0),
                  pl.BlockSpec((3, HG_WIDTH), lambda b, t: (0, 0)), pl.BlockSpec((1, HG_DIM), lambda b, t: (0, 0))],
        out_specs=[pl.BlockSpec((HG_TB, 2 * HG_WIDTH), lambda b, t: (b * HG_NT + t, 0)), zspec(0),
                   pl.BlockSpec((1, HG_HEADS, HG_CPB, HG_DIM, HG_DIM), lambda b, t: (b, 0, t, 0, 0))],
        out_shape=[
            jax.ShapeDtypeStruct((N_TOK, 2 * HG_WIDTH), BF16),
            jax.ShapeDtypeStruct((N_TOK, HG_WIDTH), F32),
            jax.ShapeDtypeStruct((B_LOC, HG_HEADS, HG_NCHUNK, HG_DIM, HG_DIM), F32),
        ],
        scratch_shapes=[pltpu.VMEM((HG_HEADS, HG_DIM, HG_DIM), F32)],
        compiler_params=_cp("parallel", "arbitrary"),
    )(z, z, z, z, o_mem, lb_logits, gnorm)


def _hgrn_bwd(z, opre, dcat, dq_mem, sall, lb_logits, gnorm):
    def body(zq_ref, zf_ref, zi_ref, zg_ref, opre_ref, dout_ref, dqm_ref, sall_ref, lbl_ref, gn_ref,
             dz_ref, dlbl_ref, dgn_ref, dst_ref, dlb_ref, dgn_acc, db_ref, dkk_ref, dbl_ref):
        b_id, t_id = pl.program_id(0), pl.program_id(1)
        lb, p = _lower_bound(lbl_ref[...])
        gn = gn_ref[...]
        mask = _tril(HG_CHUNK)
        tril_f = mask.astype(F32)
        dz_ref[:, 4 * HG_WIDTH:] = dqm_ref[...]

        @pl.when(t_id == 0)
        def _():
            dst_ref[...] = jnp.zeros_like(dst_ref)
            dlb_ref[...] = jnp.zeros_like(dlb_ref)

        @pl.when((b_id == 0) & (t_id == 0))
        def _():
            dgn_acc[...] = jnp.zeros_like(dgn_acc)

        def chunk(i, carry):
            c = HG_CPB - 1 - i
            rows = pl.ds(pl.multiple_of(c * HG_CHUNK, HG_CHUNK), HG_CHUNK)
            zq, zg = zq_ref[rows, :], zg_ref[rows, :]
            q, sq, sig, f, kk, b, bl = _hgrn_gates(zq, zf_ref[rows, :], lb, tril_f)
            v16 = zi_ref[rows, :].astype(BF16)
            eb, enb, ebl_b, ebl = jnp.exp(b), jnp.exp(-b), jnp.exp(bl - b), jnp.exp(bl)
            qd, ki, kd = q * eb, kk * enb, kk * ebl_b
            qd16, ki16, kd16 = qd.astype(BF16), ki.astype(BF16), kd.astype(BF16)
            o_all = opre_ref[rows, :]
            dout = dout_ref[rows, :]
            sg = _sigmoid(zg)
            d_on_all = dout * (zg * sg)
            dgate = dout * (sg * (1.0 + zg * (1.0 - sg)))
            dq_scale = eb * (sq * (1.0 + zq * (1.0 - sq)))
            for h in range(HG_HEADS):
                sl = _head(h)
                o = o_all[:, sl]
                r = lax.rsqrt(jnp.mean(o * o, axis=-1, keepdims=True) + EPS)
                ohat = o * r
                d_on = d_on_all[:, sl]
                dz_ref[rows, _head(h, 3)] = (dgate[:, sl] * (ohat * gn)).astype(BF16)
                dgn_acc[...] += jnp.sum(d_on * ohat, axis=0, keepdims=True)
                dohat = d_on * gn
                do16 = (r * (dohat - ohat * jnp.mean(dohat * ohat, axis=-1, keepdims=True))).astype(BF16)
                st = sall_ref[0, h, c]
                dst = dst_ref[h]
                st16, dst16 = st.astype(BF16), dst.astype(BF16)
                qd_h, ki_h, kd_h, v_h = qd16[:, sl], ki16[:, sl], kd16[:, sl], v16[:, sl]
                a16 = jnp.where(mask, lax.dot_general(qd_h, ki_h, _NT, preferred_element_type=F32), 0.0).astype(BF16)
                da16 = jnp.where(mask, lax.dot_general(do16, v_h, _NT, preferred_element_type=F32), 0.0).astype(BF16)
                dv = lax.dot_general(a16, do16, _TN, preferred_element_type=F32) + lax.dot_general(
                    kd_h, dst16, _NT, preferred_element_type=F32)
                dqd = jnp.dot(da16, ki_h, preferred_element_type=F32) + jnp.dot(do16, st16, preferred_element_type=F32)
                dki = lax.dot_general(da16, qd_h, _TN, preferred_element_type=F32)
                dkd = jnp.dot(v_h, dst16, preferred_element_type=F32)
                dbl_ref[:, sl] = jnp.sum(dkd * kd[:, sl], axis=0, keepdims=True) + ebl[:, sl] * jnp.sum(
                    st * dst, axis=0, keepdims=True)
                dst_ref[h] = dst * ebl[:, sl] + lax.dot_general(do16, qd_h, _TN, preferred_element_type=F32)
                dz_ref[rows, _head(h, 2)] = dv.astype(BF16)
                dz_ref[rows, sl] = (dqd * dq_scale[:, sl]).astype(BF16)
                dkk_ref[:, sl] = dki * enb[:, sl] + dkd * ebl_b[:, sl]
                db_ref[:, sl] = dqd * qd[:, sl] - dki * ki[:, sl] - dkd * kd[:, sl]
            dlogf = lax.dot_general(tril_f, db_ref[...], _TN, preferred_element_type=F32,
                                    precision=lax.Precision.HIGHEST) + dbl_ref[...]
            df = dlogf / f - dkk_ref[...]
            dz_ref[rows, HG_WIDTH:2 * HG_WIDTH] = (df * (1.0 - lb) * sig * (1.0 - sig)).astype(BF16)
            dlb_ref[...] += jnp.sum(df * (1.0 - sig), axis=0, keepdims=True)
            return carry

        lax.fori_loop(0, HG_CPB, chunk, 0)

        @pl.when(t_id == HG_NT - 1)
        def _():
            row0 = (lax.broadcasted_iota(jnp.int32, (3, HG_WIDTH), 0) == 0).astype(F32)
            dlbl_part = dlb_ref[...] * lb * (row0 - p)

            @pl.when(b_id == 0)
            def _():
                dlbl_ref[...] = dlbl_part

            @pl.when(b_id > 0)
            def _():
                dlbl_ref[...] += dlbl_part

            dgn_ref[...] = dgn_acc[...]

    rev = lambda b, t: b * HG_NT + HG_NT - 1 - t
    zspec = lambda s: pl.BlockSpec((HG_TB, HG_WIDTH), lambda b, t: (rev(b, t), s))
    return pl.pallas_call(
        body,
        name="hgrn_bwd",
        grid=(B_LOC, HG_NT),
        in_specs=[zspec(0), zspec(1), zspec(2), zspec(3), zspec(0), zspec(0), zspec(0),
                  pl.BlockSpec((1, HG_HEADS, HG_CPB, HG_DIM, HG_DIM), lambda b, t: (b, 0, HG_NT - 1 - t, 0, 0)),
                  pl.BlockSpec((3, HG_WIDTH), lambda b, t: (0, 0)), pl.BlockSpec((1, HG_DIM), lambda b, t: (0, 0))],
        out_specs=[pl.BlockSpec((HG_TB, 5 * HG_WIDTH), lambda b, t: (rev(b, t), 0)),
                   pl.BlockSpec((3, HG_WIDTH), lambda b, t: (0, 0)), pl.BlockSpec((1, HG_DIM), lambda b, t: (0, 0))],
        out_shape=[jax.ShapeDtypeStruct((N_TOK, 5 * HG_WIDTH), BF16),
                   jax.ShapeDtypeStruct((3, HG_WIDTH), F32), jax.ShapeDtypeStruct((1, HG_DIM), F32)],
        scratch_shapes=[pltpu.VMEM((HG_HEADS, HG_DIM, HG_DIM), F32), pltpu.VMEM((1, HG_WIDTH), F32),
                        pltpu.VMEM((1, HG_DIM), F32), pltpu.VMEM((HG_CHUNK, HG_WIDTH), F32),
                        pltpu.VMEM((HG_CHUNK, HG_WIDTH), F32), pltpu.VMEM((1, HG_WIDTH), F32)],
        compiler_params=_cp("arbitrary", "arbitrary"),
    )(z, z, z, z, opre, dcat, dq_mem, sall, lb_logits, gnorm)


GM_TM = 256


def _gmlp_norm(zv, ln_g, ln_b):
    gv, dgelu = _gelu_parts(zv)
    xc = gv - jnp.mean(gv, axis=-1, keepdims=True)
    rstd = lax.rsqrt(jnp.mean(xc * xc, axis=-1, keepdims=True) + EPS)
    vhat = xc * rstd
    return vhat * ln_g + ln_b, vhat, rstd, dgelu


def _gmlp_specs():
    half = lambda j: pl.BlockSpec((GM_TM, GM_WIDTH), lambda i: (i, j))
    vec = pl.BlockSpec((1, GM_WIDTH), lambda i: (0, 0))
    w = pl.BlockSpec((GM_GROUPS, GM_CHUNK, GM_CHUNK), lambda i: (0, 0, 0))
    bt = pl.BlockSpec((GM_CHUNK, GM_GROUPS), lambda i: (0, 0))
    return half, vec, w, bt


def _gmlp_fwd(z, o_mem, ln_g, ln_b, w_s, b_st):
    def body(zu_ref, zv_ref, omem_ref, g_ref, b_ref, w_ref, bt_ref, o_ref):
        o_ref[:, GM_WIDTH:] = omem_ref[...]
        u, _ = _gelu_parts(zu_ref[...])
        v, _, _, _ = _gmlp_norm(zv_ref[...], g_ref[...], b_ref[...])
        v16 = v.astype(BF16)
        mask = _tril(GM_CHUNK)
        bt = bt_ref[...]
        for g in range(GM_GROUPS):
            wm16 = jnp.where(mask, w_ref[g], 0.0).astype(BF16)
            cols = slice(g * GM_GDIM, (g + 1) * GM_GDIM)
            for c in range(GM_TM // GM_CHUNK):
                rows = slice(c * GM_CHUNK, (c + 1) * GM_CHUNK)
                mixed = jnp.dot(wm16, v16[rows, cols], preferred_element_type=F32) + bt[:, g:g + 1]
                o_ref[rows, cols] = (u[rows, cols] * mixed).astype(BF16)

    half, vec, w, bt = _gmlp_specs()
    return pl.pallas_call(
        body,
        name="gmlp_fwd",
        grid=(N_TOK // GM_TM,),
        in_specs=[half(0), half(1), pl.BlockSpec((GM_TM, XA_HEADS * XA_DIM), lambda i: (i, 0)), vec, vec, w, bt],
        out_specs=pl.BlockSpec((GM_TM, GM_WIDTH + XA_HEADS * XA_DIM), lambda i: (i, 0)),
        out_shape=jax.ShapeDtypeStruct((N_TOK, GM_WIDTH + XA_HEADS * XA_DIM), BF16),
        compiler_params=_cp("parallel"),
    )(z, z, o_mem, ln_g, ln_b, w_s, b_st)


def _gmlp_bwd(z, dcat, dq_mem, ln_g, ln_b, w_s, b_st):
    def body(zu_ref, zv_ref, dout_ref, dqm_ref, g_ref, b_ref, w_ref, bt_ref,
             dz_ref, dw_ref, dbt_ref, dg_ref, db_ref, dv_ref):
        dz_ref[:, 2 * GM_WIDTH:] = dqm_ref[...]
        @pl.when(pl.program_id(0) == 0)
        def _():
            dw_ref[...] = jnp.zeros_like(dw_ref)
            dbt_ref[...] = jnp.zeros_like(dbt_ref)
            dg_ref[...] = jnp.zeros_like(dg_ref)
            db_ref[...] = jnp.zeros_like(db_ref)

        zu = zu_ref[...]
        u, du_dz = _gelu_parts(zu)
        ln_g = g_ref[...]
        v, vhat, rstd, dgv_dz = _gmlp_norm(zv_ref[...], ln_g, b_ref[...])
        v16 = v.astype(BF16)
        dout = dout_ref[...]
        dmixed = dout * u
        dm16 = dmixed.astype(BF16)
        mask = _tril(GM_CHUNK)
        bt = bt_ref[...]
        group_id = lax.broadcasted_iota(jnp.int32, (1, GM_GROUPS), 1)
        dbt = jnp.zeros((GM_CHUNK, GM_GROUPS), F32)
        for g in range(GM_GROUPS):
            wm16 = jnp.where(mask, w_ref[g], 0.0).astype(BF16)
            cols = slice(g * GM_GDIM, (g + 1) * GM_GDIM)
            dw = jnp.zeros((GM_CHUNK, GM_CHUNK), F32)
            dbt_g = jnp.zeros((GM_CHUNK, 1), F32)
            for c in range(GM_TM // GM_CHUNK):
                rows = slice(c * GM_CHUNK, (c + 1) * GM_CHUNK)
                mixed = jnp.dot(wm16, v16[rows, cols], preferred_element_type=F32) + bt[:, g:g + 1]
                dz_ref[rows, cols] = (dout[rows, cols] * mixed * du_dz[rows, cols]).astype(BF16)
                dw += lax.dot_general(dm16[rows, cols], v16[rows, cols], _NT, preferred_element_type=F32)
                dbt_g += jnp.sum(dmixed[rows, cols], axis=-1, keepdims=True)
                dv_ref[rows, cols] = lax.dot_general(wm16, dm16[rows, cols], _TN, preferred_element_type=F32)
            dw_ref[g] += jnp.where(mask, dw, 0.0)
            dbt = dbt + dbt_g * (group_id == g).astype(F32)
        dbt_ref[...] += dbt
        dv = dv_ref[...]
        dg_ref[...] += jnp.sum(dv * vhat, axis=0, keepdims=True)
        db_ref[...] += jnp.sum(dv, axis=0, keepdims=True)
        dvh = dv * ln_g
        dgv = rstd * (dvh - jnp.mean(dvh, axis=-1, keepdims=True) - vhat * jnp.mean(dvh * vhat, axis=-1, keepdims=True))
        dz_ref[:, GM_WIDTH:2 * GM_WIDTH] = (dgv * dgv_dz).astype(BF16)

    half, vec, w, bt = _gmlp_specs()
    dz_width = 2 * GM_WIDTH + XA_HEADS * XA_DIM
    return pl.pallas_call(
        body,
        name="gmlp_bwd",
        grid=(N_TOK // GM_TM,),
        in_specs=[half(0), half(1), half(0), pl.BlockSpec((GM_TM, XA_HEADS * XA_DIM), lambda i: (i, 0)), vec, vec, w, bt],
        out_specs=[pl.BlockSpec((GM_TM, dz_width), lambda i: (i, 0)), w, bt, vec, vec],
        out_shape=[jax.ShapeDtypeStruct((N_TOK, dz_width), BF16),
                   jax.ShapeDtypeStruct((GM_GROUPS, GM_CHUNK, GM_CHUNK), F32),
                   jax.ShapeDtypeStruct((GM_CHUNK, GM_GROUPS), F32),
                   jax.ShapeDtypeStruct((1, GM_WIDTH), F32), jax.ShapeDtypeStruct((1, GM_WIDTH), F32)],
        scratch_shapes=[pltpu.VMEM((GM_TM, GM_WIDTH), F32)],
        compiler_params=_cp("arbitrary"),
    )(z, z, dcat, dq_mem, ln_g, ln_b, w_s, b_st)


def _own_slot(shape):
    return pl.BlockSpec((None,) + tuple(shape), lambda i, me_ref: (me_ref[0],) + (0,) * len(shape))


def _place_rows(w, layer, cuts_columns, me, *, name):
    _, r, c = w.shape
    n = c if cuts_columns else r

    def body(me_ref, w_ref, o_ref):
        wv = w_ref[...]
        o_ref[...] = (wv.T if cuts_columns else wv).astype(BF16)

    return pl.pallas_call(
        body,
        name=name,
        grid_spec=pltpu.PrefetchScalarGridSpec(
            num_scalar_prefetch=1, grid=(1,),
            in_specs=[pl.BlockSpec((None, r, c), lambda i, me_ref: (layer, 0, 0))],
            out_specs=_own_slot((n, D_MODEL))),
        out_shape=jax.ShapeDtypeStruct((N_DEV, n, D_MODEL), BF16),
        compiler_params=_cp("arbitrary"),
    )(me, w)


def _place_ln(ln_g, ln_b, me):
    blk = ln_g.shape[1]

    def body(me_ref, g_ref, b_ref, o_ref):
        o_ref[...] = jnp.zeros_like(o_ref)
        o_ref[0:1, :] = g_ref[...]
        o_ref[1:2, :] = b_ref[...]

    vec = pl.BlockSpec((1, blk), lambda i, me_ref: (0, 0))
    return pl.pallas_call(
        body,
        name="place_ln",
        grid_spec=pltpu.PrefetchScalarGridSpec(
            num_scalar_prefetch=1, grid=(1,), in_specs=[vec, vec], out_specs=_own_slot((8, blk))),
        out_shape=jax.ShapeDtypeStruct((N_DEV, 8, blk), F32),
        compiler_params=_cp("arbitrary"),
    )(me, ln_g, ln_b)


def _place_slab(a, me, *, name):
    def body(me_ref, a_ref, o_ref):
        o_ref[...] = a_ref[...]

    return pl.pallas_call(
        body,
        name=name,
        grid_spec=pltpu.PrefetchScalarGridSpec(
            num_scalar_prefetch=1, grid=(1,),
            in_specs=[pl.BlockSpec(a.shape, lambda i, me_ref: (0, 0))], out_specs=_own_slot(a.shape)),
        out_shape=jax.ShapeDtypeStruct((N_DEV,) + a.shape, a.dtype),
        compiler_params=_cp("arbitrary"),
    )(me, a)


def _place_own(grads, me, *, name):
    k = len(grads)

    def body(me_ref, *refs):
        for src, dst in zip(refs[:k], refs[k:]):
            dst[...] = src[...]

    specs = [_own_slot(g.shape[1:]) for g in grads]
    return pl.pallas_call(
        body,
        name=name,
        grid_spec=pltpu.PrefetchScalarGridSpec(num_scalar_prefetch=1, grid=(1,), in_specs=specs, out_specs=specs),
        out_shape=[jax.ShapeDtypeStruct(g.shape, g.dtype) for g in grads],
        compiler_params=_cp("arbitrary"),
    )(me, *grads)


def _mesh_pos():
    x, y, c = (lax.axis_index(a) for a in MESH_AXES)
    return x, y, c, 4 * x + 2 * y + c


def _peer(x, y, c, r):
    px = 1 - x if r & 4 else x
    py = 1 - y if r & 2 else y
    pc = 1 - c if r & 1 else c
    return (px, py, pc), 4 * px + 2 * py + pc


def _peer_copies(srcs, lands, send_sems, recv_sems, gather, waits):
    x, y, c, me = _mesh_pos()
    pairs = []
    for r in range(1, N_DEV):
        peer, peer_blk = _peer(x, y, c, r)
        for k, (src, land) in enumerate(zip(srcs, lands)):
            idx = k * (N_DEV - 1) + r - 1
            sems = dict(send_sem=send_sems.at[idx], recv_sem=recv_sems.at[idx], device_id=peer,
                        device_id_type=pl.DeviceIdType.MESH)
            mine = pltpu.make_async_remote_copy(
                src_ref=src.at[me if gather else peer_blk], dst_ref=land.at[me], **sems)
            theirs = pltpu.make_async_remote_copy(src_ref=src.at[me], dst_ref=land.at[peer_blk], **sems) if waits else None
            pairs.append((mine, theirs))
    return pairs


DATAFLOW = pltpu.SideEffectType.DATAFLOW_SIDE_EFFECTING


def _in_hbm(a):
    return pltpu.with_memory_space_constraint(a, pltpu.HBM)


def _copies_start(srcs, lands, *, gather, name, deps=()):
    arrs = list(lands) if gather else list(srcs) + list(lands)
    n, k, nd = len(arrs), len(lands), len(deps)

    def body(*refs):
        ins, send_sems, recv_sems, token = refs[:n], refs[n + nd], refs[n + nd + 1], refs[2 * n + nd + 2]
        src_refs, land_refs = (ins, ins) if gather else (ins[:k], ins[k:])
        for mine, _ in _peer_copies(src_refs, land_refs, send_sems, recv_sems, gather, waits=False):
            mine.start()
        token[...] = jnp.zeros_like(token)

    n_cp = k * (N_DEV - 1)
    return pl.pallas_call(
        body,
        name=name,
        in_specs=[HBM_SPEC] * n + [ANY_SPEC] * nd,
        out_specs=(SEM_SPEC, SEM_SPEC, *[HBM_SPEC] * n, pl.BlockSpec(memory_space=pltpu.VMEM)),
        out_shape=(pltpu.SemaphoreType.DMA((n_cp,)), pltpu.SemaphoreType.DMA((n_cp,)),
                   *[pltpu.HBM(a.shape, a.dtype) for a in arrs], jax.ShapeDtypeStruct((8, 128), F32)),
        input_output_aliases={i: 2 + i for i in range(n)},
        compiler_params=pltpu.CompilerParams(has_side_effects=DATAFLOW),
    )(*[_in_hbm(a) for a in arrs], *deps)


def _copies_wait(arrs, send_sems, recv_sems, after, *, n_lands, gather, name):
    n, k = len(arrs), n_lands

    def body(*refs):
        ins, send_sems, recv_sems = refs[:n], refs[n], refs[n + 1]
        src_refs, land_refs = (ins, ins) if gather else (ins[:k], ins[k:])
        for mine, theirs in _peer_copies(src_refs, land_refs, send_sems, recv_sems, gather, waits=True):
            mine.wait_send()
            theirs.wait_recv()

    outs = pl.pallas_call(
        body,
        name=name,
        in_specs=[HBM_SPEC] * n + [SEM_SPEC, SEM_SPEC] + [ANY_SPEC] * len(after),
        out_specs=[HBM_SPEC] * n,
        out_shape=[pltpu.HBM(a.shape, a.dtype) for a in arrs],
        input_output_aliases={i: i for i in range(n)},
        compiler_params=pltpu.CompilerParams(has_side_effects=DATAFLOW),
    )(*arrs, send_sems, recv_sems, *after)
    return outs[n - k:]


def _exchange_small(slabs):
    n = len(slabs)

    def body(*refs):
        ins, outs = refs[:n], refs[n:2 * n]
        send_sems, recv_sems, local_sems = refs[2 * n:]
        x, y, c, me = _mesh_pos()
        own = [pltpu.make_async_copy(src, dst.at[me], local_sems.at[k]) for k, (src, dst) in enumerate(zip(ins, outs))]
        for cp in own:
            cp.start()
        sends, recvs = [], []
        for r in range(1, N_DEV):
            peer, peer_blk = _peer(x, y, c, r)
            for k, (src, dst) in enumerate(zip(ins, outs)):
                idx = k * (N_DEV - 1) + r - 1
                sems = dict(send_sem=send_sems.at[idx], recv_sem=recv_sems.at[idx], device_id=peer,
                            device_id_type=pl.DeviceIdType.MESH)
                send = pltpu.make_async_remote_copy(src_ref=src, dst_ref=dst.at[me], **sems)
                send.start()
                sends.append(send)
                recvs.append(pltpu.make_async_remote_copy(src_ref=src, dst_ref=dst.at[peer_blk], **sems))
        for cp in recvs:
            cp.wait_recv()
        for cp in sends:
            cp.wait_send()
        for cp in own:
            cp.wait()

    n_cp = n * (N_DEV - 1)
    return pl.pallas_call(
        body,
        name="exchange_small_grads",
        in_specs=[ANY_SPEC] * n,
        out_specs=[ANY_SPEC] * n,
        out_shape=[jax.ShapeDtypeStruct((N_DEV,) + s.shape, F32) for s in slabs],
        scratch_shapes=[pltpu.SemaphoreType.DMA((n_cp,)), pltpu.SemaphoreType.DMA((n_cp,)),
                        pltpu.SemaphoreType.DMA((n,))],
    )(*slabs)


def _adamw(w, g, m, v):
    m = ADAM_B1 * m + (1.0 - ADAM_B1) * g
    v = ADAM_B2 * v + (1.0 - ADAM_B2) * (g * g)
    m_hat = m / (1.0 - ADAM_B1 ** ADAM_STEP)
    v_hat = v / (1.0 - ADAM_B2 ** ADAM_STEP)
    return -ADAM_LR * (m_hat / (jnp.sqrt(v_hat) + ADAM_EPS) + ADAM_WD * w), m, v


ADAM_TC = 256


def _adam_big(slots, w, m, v, cuts_columns, *, name):
    layers, n, nj = len(slots), slots[0].shape[1], D_MODEL // ADAM_TC

    def body(*refs):
        s_refs = refs[:layers]
        w_ref, m_ref, v_ref, g_ref, d_ref, nm_ref, nv_ref, acc_ref = refs[layers:]
        for ll in range(layers):
            @pl.when(pl.program_id(0) == ll)
            def _(s_ref=s_refs[ll]):
                g = s_ref[0].astype(F32)
                for s in range(1, N_DEV):
                    g = g + s_ref[s].astype(F32)
                acc_ref[...] = g

        g = acc_ref[...].T if cuts_columns else acc_ref[...]
        g_ref[...] = g
        d_ref[...], nm_ref[...], nv_ref[...] = _adamw(w_ref[...], g, m_ref[...], v_ref[...])

    def slot_spec(ll):
        return pl.BlockSpec((N_DEV, n, ADAM_TC),
                            lambda l, j: (0, 0, jnp.where(l < ll, 0, jnp.where(l > ll, nj - 1, j))))

    if cuts_columns:
        w_spec = pl.BlockSpec((None, ADAM_TC, n), lambda l, j: (l, j, 0))
    else:
        w_spec = pl.BlockSpec((None, n, ADAM_TC), lambda l, j: (l, 0, j))
    return pl.pallas_call(
        body,
        name=name,
        grid=(layers, nj),
        in_specs=[slot_spec(ll) for ll in range(layers)] + [w_spec] * 3,
        out_specs=[w_spec] * 4,
        out_shape=[jax.ShapeDtypeStruct(w.shape, F32)] * 4,
        scratch_shapes=[pltpu.VMEM((n, ADAM_TC), F32)],
        compiler_params=_cp("arbitrary", "arbitrary"),
    )(*slots, w, m, v)


def _adam_slabs(slots, ws, ms, vs):
    n = len(slots)

    def body(*refs):
        ins, outs = refs[:4 * n], refs[4 * n:]
        for k in range(n):
            s_ref, w_ref, m_ref, v_ref = ins[k], ins[n + k], ins[2 * n + k], ins[3 * n + k]
            g = s_ref[0]
            for s in range(1, N_DEV):
                g = g + s_ref[s]
            outs[4 * k][...] = g
            outs[4 * k + 1][...], outs[4 * k + 2][...], outs[4 * k + 3][...] = _adamw(w_ref[...], g, m_ref[...], v_ref[...])

    res = pl.pallas_call(
        body,
        name="small_adamw",
        out_shape=[jax.ShapeDtypeStruct(w.shape, F32) for w in ws for _ in range(4)],
        compiler_params=pltpu.CompilerParams(vmem_limit_bytes=VMEM_LIMIT_BYTES),
    )(*slots, *ws, *ms, *vs)
    return [res[4 * k:4 * k + 4] for k in range(n)]


def _adam_vecs(gs, ws, ms, vs):
    n = len(gs)

    def body(*refs):
        ins, outs = refs[:4 * n], refs[4 * n:]
        for k in range(n):
            outs[3 * k][...], outs[3 * k + 1][...], outs[3 * k + 2][...] = _adamw(
                ins[n + k][...], ins[k][...], ins[2 * n + k][...], ins[3 * n + k][...])

    res = pl.pallas_call(
        body,
        name="ln_adamw",
        out_shape=[jax.ShapeDtypeStruct(w.shape, F32) for w in ws for _ in range(3)],
        compiler_params=pltpu.CompilerParams(vmem_limit_bytes=VMEM_LIMIT_BYTES),
    )(*gs, *ws, *ms, *vs)
    return [res[3 * k:3 * k + 3] for k in range(n)]


SLAB_AT = dict(mem_norm=0, lb_logits=1, ffn1_norm=4, mix_norm=6, hgrn_gnorm=8, gmlp_ln_g=9, gmlp_ln_b=11,
               gmlp_b_s=13, ffn2_norm=14, final_norm=16)
SLAB_ROWS = 24
SMALL_SHARDED = ("gmlp_ln_g", "gmlp_ln_b")


def _pack_slab(parts, *, name):
    flat, plan = [], []
    for pname, at in SLAB_AT.items():
        for a in parts.get(pname, ()):
            flat.append(a)
            plan.append((at, a.shape))
            at += max(1, a.shape[0] * a.shape[1] // D_MODEL)

    def body(*refs):
        o_ref = refs[-1]
        o_ref[...] = jnp.zeros_like(o_ref)
        for ref, (at, (r, w)) in zip(refs, plan):
            if w == D_MODEL or r == 1 and w < D_MODEL:
                o_ref[at:at + r, 0:w] = ref[...]
            elif w < D_MODEL:
                for j in range(r):
                    o_ref[at:at + 1, j * w:(j + 1) * w] = ref[j:j + 1, :]
            else:
                for j in range(w // D_MODEL):
                    o_ref[at + j:at + j + 1, :] = ref[:, j * D_MODEL:(j + 1) * D_MODEL]

    return pl.pallas_call(
        body,
        name=name,
        out_shape=jax.ShapeDtypeStruct((SLAB_ROWS, D_MODEL), F32),
        compiler_params=pltpu.CompilerParams(vmem_limit_bytes=VMEM_LIMIT_BYTES),
    )(*flat)


def _unpack_slab(slab, shapes):
    out = {}
    for pname, at in SLAB_AT.items():
        if pname in SMALL_SHARDED:
            continue
        size = math.prod(shapes[pname])
        rows = max(1, size // D_MODEL)
        out[pname] = slab[at:at + rows].reshape(-1)[:size].reshape(shapes[pname])
    return out


def _ffn_fwd(x, norm_g, block, layer, full, get_weights):
    tag = f"l{layer}_{block}"
    full.update(get_weights((layer, f"{block}_in"), (x,)))
    h = _rms_fwd(x, norm_g, deps=full.pop("deps", ()), name=f"{tag}_norm")
    z = _mm(h, full[(f"{block}_w_in", layer)], tb=True, tm=1024, tn=512, tk=D_MODEL, out_dtype=F32, name=f"{tag}_in")
    act = _swiglu_fwd(z, name=f"{tag}_act")
    full.update(get_weights((layer, f"{block}_out"), (act,)))
    y = _mm(act, full[(f"{block}_w_out", layer)], tm=512, tn=D_MODEL, tk=D_FF, out_dtype=F32, res=x, scale=0.5,
            deps=full.pop("deps", ()), name=f"{tag}_out")
    return y, (x, h, z, act)


def _ffn_bwd(dy, dy16, saved, norm_g, w_in_t, w_out, tag, deps=()):
    x, h, z, act = saved
    dw_out = _mm(act, dy16, ta=True, tm=1408, tn=D_MODEL, tk=N_TOK, out_dtype=BF16, scale=0.5, deps=deps,
                 name=f"{tag}_out_wgrad")
    dact = _mm(dy16, w_out, tb=True, tm=1024, tn=1408, tk=D_MODEL, out_dtype=F32, name=f"{tag}_out_dgrad")
    dz = _swiglu_bwd(z, dact, scale=0.5, name=f"{tag}_act_bwd")
    dw_in_t = _mm(dz, h, ta=True, tm=1408, tn=D_MODEL, tk=N_TOK, out_dtype=BF16, name=f"{tag}_in_wgrad")
    dh = _mm(dz, w_in_t, tm=512, tn=D_MODEL, tk=2 * D_FF, out_dtype=F32, name=f"{tag}_in_dgrad")
    dx, dx16, dg = _rms_bwd(x, norm_g, dh, dy, name=f"{tag}_norm_bwd")
    return dx, dx16, dg, dw_in_t, dw_out


def kernel(x, mem, mem_norm, lb_logits, ffn1_norm, ffn1_w_in, ffn1_w_out, mix_norm, mem_w_kv, hgrn_w_in, hgrn_gnorm, hgrn_w_out, gmlp_w_in, gmlp_ln_g, gmlp_ln_b, gmlp_w_s, gmlp_b_s, gmlp_w_out, ffn2_norm, ffn2_w_in, ffn2_w_out, final_norm, loss_target, m_mem_norm, m_lb_logits, m_ffn1_norm, m_ffn1_w_in, m_ffn1_w_out, m_mix_norm, m_mem_w_kv, m_hgrn_w_in, m_hgrn_gnorm, m_hgrn_w_out, m_gmlp_w_in, m_gmlp_ln_g, m_gmlp_ln_b, m_gmlp_w_s, m_gmlp_b_s, m_gmlp_w_out, m_ffn2_norm, m_ffn2_w_in, m_ffn2_w_out, m_final_norm, v_mem_norm, v_lb_logits, v_ffn1_norm, v_ffn1_w_in, v_ffn1_w_out, v_mix_norm, v_mem_w_kv, v_hgrn_w_in, v_hgrn_gnorm, v_hgrn_w_out, v_gmlp_w_in, v_gmlp_ln_g, v_gmlp_ln_b, v_gmlp_w_s, v_gmlp_b_s, v_gmlp_w_out, v_ffn2_norm, v_ffn2_w_in, v_ffn2_w_out, v_final_norm):
    weights = dict(mem_norm=mem_norm, lb_logits=lb_logits, ffn1_norm=ffn1_norm, ffn1_w_in=ffn1_w_in, ffn1_w_out=ffn1_w_out, mix_norm=mix_norm, mem_w_kv=mem_w_kv, hgrn_w_in=hgrn_w_in, hgrn_gnorm=hgrn_gnorm, hgrn_w_out=hgrn_w_out, gmlp_w_in=gmlp_w_in, gmlp_ln_g=gmlp_ln_g, gmlp_ln_b=gmlp_ln_b, gmlp_w_s=gmlp_w_s, gmlp_b_s=gmlp_b_s, gmlp_w_out=gmlp_w_out, ffn2_norm=ffn2_norm, ffn2_w_in=ffn2_w_in, ffn2_w_out=ffn2_w_out, final_norm=final_norm)
    mom_m = dict(mem_norm=m_mem_norm, lb_logits=m_lb_logits, ffn1_norm=m_ffn1_norm, ffn1_w_in=m_ffn1_w_in, ffn1_w_out=m_ffn1_w_out, mix_norm=m_mix_norm, mem_w_kv=m_mem_w_kv, hgrn_w_in=m_hgrn_w_in, hgrn_gnorm=m_hgrn_gnorm, hgrn_w_out=m_hgrn_w_out, gmlp_w_in=m_gmlp_w_in, gmlp_ln_g=m_gmlp_ln_g, gmlp_ln_b=m_gmlp_ln_b, gmlp_w_s=m_gmlp_w_s, gmlp_b_s=m_gmlp_b_s, gmlp_w_out=m_gmlp_w_out, ffn2_norm=m_ffn2_norm, ffn2_w_in=m_ffn2_w_in, ffn2_w_out=m_ffn2_w_out, final_norm=m_final_norm)
    mom_v = dict(mem_norm=v_mem_norm, lb_logits=v_lb_logits, ffn1_norm=v_ffn1_norm, ffn1_w_in=v_ffn1_w_in, ffn1_w_out=v_ffn1_w_out, mix_norm=v_mix_norm, mem_w_kv=v_mem_w_kv, hgrn_w_in=v_hgrn_w_in, hgrn_gnorm=v_hgrn_gnorm, hgrn_w_out=v_hgrn_w_out, gmlp_w_in=v_gmlp_w_in, gmlp_ln_g=v_gmlp_ln_g, gmlp_ln_b=v_gmlp_ln_b, gmlp_w_s=v_gmlp_w_s, gmlp_b_s=v_gmlp_b_s, gmlp_w_out=v_gmlp_w_out, ffn2_norm=v_ffn2_norm, ffn2_w_in=v_ffn2_w_in, ffn2_w_out=v_ffn2_w_out, final_norm=v_final_norm)
    order = list(weights)
    _, _, _, me = _mesh_pos()
    me_arr = jnp.reshape(me, (1,)).astype(jnp.int32)
    cuts = {name: c for name, c, _, _ in GROUPS}

    mix1 = (("mem_w_kv", 1), ("gmlp_w_in", 0), ("gmlp_w_out", 0))
    gather_plan = (
        ((0, "ffn1_in"), (("ffn1_w_in", 0),), None),
        ((0, "ffn1_out"), (("ffn1_w_out", 0),), 0),
        ((0, "mix_in"), (("mem_w_kv", 0), ("hgrn_w_in", 0)), 0),
        ((0, "mix_out"), (("hgrn_w_out", 0),), 2),
        ((0, "ffn2_in"), _stage_pieces(0, "ffn2"), 2),
        ((1, "ffn1_in"), _stage_pieces(1, "ffn1"), 3),
        ((1, "mix_in"), mix1, 4),
        ((1, "ffn2_in"), _stage_pieces(1, "ffn2"), 5),
    )
    gather = {}

    def start_gather(k, deps):
        use, pieces, _ = gather_plan[k]
        lands = [_place_rows(weights[name], l, cuts[name], me_arr, name=f"place_{name}_{l}") for name, l in pieces]
        if pieces is mix1:
            lands.append(_place_ln(gmlp_ln_g, gmlp_ln_b, me_arr))
        send_sems, recv_sems, *thru, token = _copies_start(lands, lands, gather=True, deps=deps,
                                                           name=f"gather_start_l{use[0]}_{use[1]}")
        gather[use] = (k, thru, send_sems, recv_sems)
        return token

    start_gather(0, ())

    def get_weights(use, after):
        if use not in gather:
            return {}
        k, thru, send_sems, recv_sems = gather[use]
        outs = _copies_wait(thru, send_sems, recv_sems, after, n_lands=len(thru), gather=True,
                            name=f"gather_wait_l{use[0]}_{use[1]}")
        pieces = gather_plan[k][1]
        w = {p: o.reshape(N_DEV * o.shape[1], D_MODEL) for p, o in zip(pieces, outs)}
        w["deps"] = tuple(start_gather(later, (outs[0],))
                          for later, (_, _, trigger) in enumerate(gather_plan) if trigger == k)
        if pieces is mix1:
            w["ln_g"] = outs[-1][:, 0, :].reshape(1, GM_WIDTH)
            w["ln_b"] = outs[-1][:, 1, :].reshape(1, GM_WIDTH)
        return w

    scatter = {}

    def put_grads(st, grads):
        if st == "w_s":
            land = _place_slab(grads.reshape(GM_GROUPS * GM_CHUNK, GM_CHUNK), me_arr, name="w_s_place")
            send_sems, recv_sems, *thru, token = _copies_start([land], [land], gather=True, name="w_s_start")
            scatter[st] = (thru, send_sems, recv_sems)
            return (token,)
        views = [grads[p].reshape(N_DEV, -1, D_MODEL) for p in _stage_pieces(*st)]
        recv = _place_own(views, me_arr, name=f"scatter_place_l{st[0]}_{st[1]}")
        send_sems, recv_sems, *thru, token = _copies_start(views, recv, gather=False,
                                                           name=f"scatter_start_l{st[0]}_{st[1]}")
        scatter[st] = (thru, send_sems, recv_sems)
        return (token,)

    dx, small, loss_part = _step_local(
        x, mem, loss_target, get_weights, put_grads, mem_norm, lb_logits, ffn1_norm, mix_norm, hgrn_gnorm,
        gmlp_w_s, gmlp_b_s, ffn2_norm, final_norm)

    def slots_of(blk, after):
        slots = {}
        for i in (1, 0):
            thru, send_sems, recv_sems = scatter[(i, blk)]
            outs = _copies_wait(thru, send_sems, recv_sems, after, n_lands=len(thru) // 2, gather=False,
                                name=f"scatter_wait_l{i}_{blk}")
            slots.update(zip(_stage_pieces(i, blk), outs))
        return slots

    grad, delta, new_m, new_v = {}, {}, {}, {}

    def adam_groups(slots, names):
        for name in names:
            layers = GROUP_LAYERS[name]
            grad[name], delta[name], new_m[name], new_v[name] = _adam_big(
                [slots[(name, l)] for l in range(layers)], weights[name], mom_m[name], mom_v[name], cuts[name],
                name=f"{name}_adamw")

    adam_groups(slots_of("ffn2", (dx,)), ("ffn2_w_in", "ffn2_w_out"))
    adam_groups(slots_of("mix", (delta["ffn2_w_out"],)),
                ("mem_w_kv", "gmlp_w_in", "gmlp_w_out", "hgrn_w_in", "hgrn_w_out"))

    def small_parts(src):
        parts = {n: [src[n].reshape(-1, src[n].shape[-1])] for n in SLAB_AT if n not in SMALL_SHARDED}
        return parts

    w_s_rows = lambda a: a.reshape(GM_GROUPS * GM_CHUNK, GM_CHUNK)
    (slab_slots,) = _exchange_small([_pack_slab(small, name="pack_small_grads")])
    thru, send_sems, recv_sems = scatter["w_s"]
    (ws_slots,) = _copies_wait(thru, send_sems, recv_sems, (slab_slots,), n_lands=1, gather=True, name="w_s_wait")
    (g_slab, d_slab, nm_slab, nv_slab), (g_ws, d_ws, nm_ws, nv_ws) = _adam_slabs(
        [slab_slots, ws_slots],
        [_pack_slab(small_parts(weights), name="pack_small_w"), w_s_rows(gmlp_w_s)],
        [_pack_slab(small_parts(mom_m), name="pack_small_m"), w_s_rows(m_gmlp_w_s)],
        [_pack_slab(small_parts(mom_v), name="pack_small_v"), w_s_rows(v_gmlp_w_s)])
    shapes = {n: weights[n].shape for n in SLAB_AT}
    for out, slab, ws in ((grad, g_slab, g_ws), (delta, d_slab, d_ws), (new_m, nm_slab, nm_ws), (new_v, nv_slab, nv_ws)):
        out.update(_unpack_slab(slab, shapes))
        out["gmlp_w_s"] = ws.reshape(gmlp_w_s.shape)
    blk = GM_WIDTH // N_DEV
    g_ln = [lax.dynamic_slice(g_slab[SLAB_AT[n]:SLAB_AT[n] + 2].reshape(1, GM_WIDTH), (0, me * blk), (1, blk))
            for n in SMALL_SHARDED]
    ln_out = _adam_vecs(g_ln, [weights[n] for n in SMALL_SHARDED], [mom_m[n] for n in SMALL_SHARDED],
                        [mom_v[n] for n in SMALL_SHARDED])
    for n, g, (d, nm, nv) in zip(SMALL_SHARDED, g_ln, ln_out):
        grad[n], delta[n], new_m[n], new_v[n] = g, d, nm, nv

    adam_groups(slots_of("ffn1", (delta["hgrn_w_out"], d_slab)), ("ffn1_w_in", "ffn1_w_out"))

    loss = lax.psum(loss_part[0, 0], MESH_AXES)
    grad_x = dx.reshape(B_LOC, SEQ, D_MODEL)
    return (loss, grad_x, *[grad[n] for n in order], *[delta[n] for n in order],
            *[new_m[n] for n in order], *[new_v[n] for n in order])


def _step_local(x, mem, loss_target, get_weights, put_grads, mem_norm, lb_logits, ffn1_norm, mix_norm, hgrn_gnorm,
                gmlp_w_s, gmlp_b_s, ffn2_norm, final_norm):
    w_s = gmlp_w_s[0]
    b_st = gmlp_b_s[0].T

    xs = x.reshape(N_TOK, D_MODEL)
    mem2d = mem.reshape(B_LOC * MEM_LEN, D_MODEL)
    mem_g = mem_norm.reshape(1, D_MODEL)
    saved, full = [], {}
    memn = _rms_fwd(mem2d, mem_g, name="mem_norm_fwd")
    for i in range(2):
        xs, s_ffn1 = _ffn_fwd(xs, ffn1_norm[i:i + 1], "ffn1", i, full, get_weights)
        full.update(get_weights((i, "mix_in"), (xs,)))
        mixer = "hgrn" if i == 0 else "gmlp"
        hm = _rms_fwd(xs, mix_norm[i:i + 1], deps=full.pop("deps", ()), name=f"l{i}_mix_norm")
        kv = _mm(memn, full[("mem_w_kv", i)], tb=True, tm=512, tn=512, tk=D_MODEL, out_dtype=F32, name=f"l{i}_mem_kv")
        zm = _mm(hm, full[(f"{mixer}_w_in", 0)], tb=True, tm=1024, tn=512, tk=D_MODEL, out_dtype=F32, name=f"l{i}_mix_in")
        o_mem = _attn_fwd(zm, kv, name=f"l{i}_attn")
        if i == 0:
            cat, o_pre, s_all = _hgrn_fwd(zm, o_mem, lb_logits, hgrn_gnorm)
            mix_saved = (o_pre, s_all)
        else:
            cat = _gmlp_fwd(zm, o_mem, full["ln_g"], full["ln_b"], w_s, b_st)
            mix_saved = ()
        x_mix = xs
        full.update(get_weights((i, "mix_out"), (cat,)))
        xs = _mm(cat, full[(f"{mixer}_w_out", 0)], tm=512, tn=D_MODEL, tk=cat.shape[1], out_dtype=F32, res=xs,
                 deps=full.pop("deps", ()), name=f"l{i}_mix_out")
        xs, s_ffn2 = _ffn_fwd(xs, ffn2_norm[i:i + 1], "ffn2", i, full, get_weights)
        saved.append((s_ffn1, (x_mix, hm, kv, zm, cat, mix_saved), s_ffn2))

    dx, dx16, d_final, loss_part = _loss_head(xs, final_norm.reshape(1, D_MODEL), loss_target.reshape(N_TOK, D_MODEL))

    small = {"final_norm": [d_final]}
    d_ffn1, d_ffn2, d_mix = [None, None], [None, None], [None, None]
    dmemn = jnp.zeros((B_LOC * MEM_LEN, D_MODEL), F32)
    deps = ()
    for i in (1, 0):
        s_ffn1, (x_mix, hm, kv, zm, cat, mix_saved), s_ffn2 = saved[i]
        dx, dx16, d_ffn2[i], dw_in_t, dw_out = _ffn_bwd(
            dx, dx16, s_ffn2, ffn2_norm[i:i + 1], full[("ffn2_w_in", i)], full[("ffn2_w_out", i)], f"l{i}_ffn2", deps)
        deps = put_grads((i, "ffn2"), {("ffn2_w_in", i): dw_in_t, ("ffn2_w_out", i): dw_out})
        mixer = "hgrn" if i == 0 else "gmlp"
        w_in_t, w_out = full[(f"{mixer}_w_in", 0)], full[(f"{mixer}_w_out", 0)]
        width = cat.shape[1]
        g_mix = {}
        g_mix[(f"{mixer}_w_out", 0)] = _mm(cat, dx16, ta=True, tm=1024, tn=D_MODEL, tk=N_TOK, out_dtype=BF16,
                                           deps=deps, name=f"l{i}_mix_out_wgrad")
        dcat = _mm(dx16, w_out, tb=True, tm=1024, tn=width // 2, tk=D_MODEL, out_dtype=F32, name=f"l{i}_mix_out_dgrad")
        dq, dk, dv = _attn_bwd(zm, kv, dcat, do_off=width - XA_HEADS * XA_DIM, name=f"l{i}_attn_bwd")
        if i == 0:
            dzm, dlbl, dgn = _hgrn_bwd(zm, mix_saved[0], dcat, dq, mix_saved[1], lb_logits, hgrn_gnorm)
            small["lb_logits"], small["hgrn_gnorm"] = [dlbl], [dgn]
            deps = ()
        else:
            dzm, dws, dbt, dlng, dlnb = _gmlp_bwd(zm, dcat, dq, full["ln_g"], full["ln_b"], w_s, b_st)
            small["gmlp_b_s"], small["gmlp_ln_g"], small["gmlp_ln_b"] = [dbt.T], [dlng], [dlnb]
            deps = put_grads("w_s", dws)
        g_mix[(f"{mixer}_w_in", 0)] = _mm(dzm, hm, ta=True, tm=1024, tn=D_MODEL, tk=N_TOK, out_dtype=BF16, deps=deps,
                                          name=f"l{i}_mix_in_wgrad")
        dkv = jnp.concatenate([dk, dv], axis=1)
        g_mix[("mem_w_kv", i)] = _mm(dkv, memn, ta=True, tm=512, tn=D_MODEL, tk=B_LOC * MEM_LEN, out_dtype=BF16,
                                     name=f"l{i}_mem_kv_wgrad")
        deps = put_grads((i, "mix"), g_mix)
        dh = _mm(dzm, w_in_t, tm=512, tn=D_MODEL, tk=dzm.shape[1], out_dtype=F32, deps=deps, name=f"l{i}_mix_in_dgrad")
        dx, dx16, d_mix[i] = _rms_bwd(x_mix, mix_norm[i:i + 1], dh, dx, name=f"l{i}_mix_norm_bwd")
        dmemn = _mm(dkv, full[("mem_w_kv", i)], tm=B_LOC * MEM_LEN, tn=D_MODEL, tk=512, out_dtype=F32, res=dmemn,
                    name=f"l{i}_mem_kv_dgrad")
        dx, dx16, d_ffn1[i], dw_in_t, dw_out = _ffn_bwd(
            dx, dx16, s_ffn1, ffn1_norm[i:i + 1], full[("ffn1_w_in", i)], full[("ffn1_w_out", i)], f"l{i}_ffn1")
        deps = put_grads((i, "ffn1"), {("ffn1_w_in", i): dw_in_t, ("ffn1_w_out", i): dw_out})
    _, _, dmem_g = _rms_bwd(mem2d, mem_g, dmemn, dmemn, deps=deps, name="mem_norm_bwd")
    small.update(mem_norm=[dmem_g], ffn1_norm=d_ffn1, ffn2_norm=d_ffn2, mix_norm=d_mix)
    return dx, small, loss_part
```

```python
import functools
import math

import jax
import jax.numpy as jnp
from jax import lax
from jax.experimental import pallas as pl
from jax.experimental.pallas import tpu as pltpu

F32 = jnp.float32
BF16 = jnp.bfloat16

D_MODEL = 1024
SEQ = 2048
B_LOC = 2
N_TOK = B_LOC * SEQ
MEM_LEN = 256
N_DEV = 8
EPS = 1e-6
D_FF = 2816
HG_HEADS = 8
HG_DIM = 128
HG_CHUNK = 64
HG_NCHUNK = SEQ // HG_CHUNK
GM_CHUNK = 128
GM_GROUPS = 8
GM_WIDTH = 2048
GM_GDIM = GM_WIDTH // GM_GROUPS
XA_HEADS = 4
XA_DIM = 256
XA_OFF = 4096

ADAM_LR = 0.001
ADAM_B1 = 0.9
ADAM_B2 = 0.999
ADAM_EPS = 1e-08
ADAM_WD = 0.01
ADAM_STEP = 10

VMEM_LIMIT_BYTES = 56 * 1024 * 1024
MESH_AXES = ("x", "y", "c")

GROUPS = (
    ("ffn1_w_in", True, 2, 704),
    ("ffn1_w_out", False, 2, 352),
    ("mem_w_kv", True, 2, 256),
    ("hgrn_w_in", True, 1, 640),
    ("hgrn_w_out", False, 1, 256),
    ("gmlp_w_in", True, 1, 640),
    ("gmlp_w_out", False, 1, 384),
    ("ffn2_w_in", True, 2, 704),
    ("ffn2_w_out", False, 2, 352),
)
GROUP_LAYERS = {name: layers for name, _, layers, _ in GROUPS}


def _stage_pieces(layer, block):
    if block == "mix":
        mixer = "hgrn" if layer == 0 else "gmlp"
        return (("mem_w_kv", layer), (f"{mixer}_w_in", 0), (f"{mixer}_w_out", 0))
    return ((f"{block}_w_in", layer), (f"{block}_w_out", layer))


ANY_SPEC = pl.BlockSpec(memory_space=pl.ANY)
HBM_SPEC = pl.BlockSpec(memory_space=pltpu.HBM)
SEM_SPEC = pl.BlockSpec(memory_space=pltpu.SEMAPHORE)


def _cp(*sem):
    return pltpu.CompilerParams(dimension_semantics=sem, vmem_limit_bytes=VMEM_LIMIT_BYTES)


def _sigmoid(x):
    return 1.0 / (1.0 + jnp.exp(-x))


def _gelu_parts(x):
    cdf = 0.5 * (1.0 + lax.erf(x * (1.0 / math.sqrt(2.0))))
    pdf = jnp.exp(-0.5 * x * x) * (1.0 / math.sqrt(2.0 * math.pi))
    return x * cdf, cdf + x * pdf


def _mm(a, b, *, ta=False, tb=False, tm, tn, tk, out_dtype, res=None, scale=1.0, deps=(), name):
    m, k = (a.shape[1], a.shape[0]) if ta else a.shape
    n, kb = b.shape if tb else (b.shape[1], b.shape[0])
    assert k == kb and m % tm == 0 and n % tn == 0 and k % tk == 0, (name, a.shape, b.shape)
    nk = k // tk
    dn = (((0 if ta else 1,), (1 if tb else 0,)), ((), ()))
    n_in = 2 + (res is not None) + len(deps)

    def body(*refs):
        a_ref, b_ref = refs[:2]
        r_ref = refs[2] if res is not None else None
        o_ref, scr = refs[n_in], refs[n_in + 1:]
        p = lax.dot_general(a_ref[...].astype(BF16), b_ref[...].astype(BF16), dn, preferred_element_type=F32)

        def finish(acc):
            if scale != 1.0:
                acc = scale * acc
            if r_ref is not None:
                acc = r_ref[...] + acc
            o_ref[...] = acc.astype(out_dtype)

        if nk == 1:
            finish(p)
        else:
            acc_ref = scr[0]
            kk = pl.program_id(2)

            @pl.when(kk == 0)
            def _():
                acc_ref[...] = p

            @pl.when(kk > 0)
            def _():
                acc_ref[...] += p

            @pl.when(kk == nk - 1)
            def _():
                finish(acc_ref[...])

    a_spec = pl.BlockSpec((tk, tm), lambda i, j, kk: (kk, i)) if ta else pl.BlockSpec((tm, tk), lambda i, j, kk: (i, kk))
    b_mode = dict(pipeline_mode=pl.Buffered(1)) if n == tn and nk == 1 else {}
    if tb:
        b_spec = pl.BlockSpec((tn, tk), lambda i, j, kk: (j, kk), **b_mode)
    else:
        b_spec = pl.BlockSpec((tk, tn), lambda i, j, kk: (kk, j), **b_mode)
    o_spec = pl.BlockSpec((tm, tn), lambda i, j, kk: (i, j))
    in_specs = [a_spec, b_spec] + ([o_spec] if res is not None else []) + [ANY_SPEC] * len(deps)
    args = (a, b) + ((res,) if res is not None else ()) + tuple(deps)
    return pl.pallas_call(
        body,
        name=name,
        grid=(m // tm, n // tn, nk),
        in_specs=in_specs,
        out_specs=o_spec,
        out_shape=jax.ShapeDtypeStruct((m, n), out_dtype),
        scratch_shapes=[pltpu.VMEM((tm, tn), F32)] if nk > 1 else [],
        compiler_params=_cp("parallel", "parallel", "arbitrary"),
    )(*args)


def _rms_fwd(x, g, *, name, deps=(), tm=512):
    rows = x.shape[0]

    def body(x_ref, g_ref, *rest):
        o_ref = rest[len(deps)]
        xv = x_ref[...]
        r = lax.rsqrt(jnp.mean(xv * xv, axis=-1, keepdims=True) + EPS)
        o_ref[...] = (xv * r * g_ref[...]).astype(BF16)

    row = pl.BlockSpec((tm, D_MODEL), lambda i: (i, 0))
    return pl.pallas_call(
        body,
        name=name,
        grid=(rows // tm,),
        in_specs=[row, pl.BlockSpec((1, D_MODEL), lambda i: (0, 0))] + [ANY_SPEC] * len(deps),
        out_specs=row,
        out_shape=jax.ShapeDtypeStruct((rows, D_MODEL), BF16),
        compiler_params=_cp("parallel"),
    )(x, g, *deps)


def _rms_bwd(x, g, dh, dres, *, name, deps=(), tm=512):
    rows = x.shape[0]

    def body(x_ref, g_ref, dh_ref, dres_ref, *rest):
        dx_ref, dx16_ref, dg_ref = rest[len(deps):]
        xv = x_ref[...]
        r = lax.rsqrt(jnp.mean(xv * xv, axis=-1, keepdims=True) + EPS)
        xhat = xv * r
        dhv = dh_ref[...]
        part = jnp.sum(dhv * xhat, axis=0, keepdims=True)

        @pl.when(pl.program_id(0) == 0)
        def _():
            dg_ref[...] = part

        @pl.when(pl.program_id(0) > 0)
        def _():
            dg_ref[...] += part

        dxh = dhv * g_ref[...]
        dx = dres_ref[...] + r * (dxh - xhat * jnp.mean(dxh * xhat, axis=-1, keepdims=True))
        dx_ref[...] = dx
        dx16_ref[...] = dx.astype(BF16)

    row = pl.BlockSpec((tm, D_MODEL), lambda i: (i, 0))
    vec = pl.BlockSpec((1, D_MODEL), lambda i: (0, 0))
    return pl.pallas_call(
        body,
        name=name,
        grid=(rows // tm,),
        in_specs=[row, vec, row, row] + [ANY_SPEC] * len(deps),
        out_specs=[row, row, vec],
        out_shape=[jax.ShapeDtypeStruct((rows, D_MODEL), F32), jax.ShapeDtypeStruct((rows, D_MODEL), BF16),
                   jax.ShapeDtypeStruct((1, D_MODEL), F32)],
        compiler_params=_cp("arbitrary"),
    )(x, g, dh, dres, *deps)


_NT = (((1,), (1,)), ((), ()))
_TN = (((0,), (0,)), ((), ()))


def _norm_mm(x, g, w_t, *, swiglu, name, deps=(), tm=1024):
    rows = w_t.shape[0]
    half = rows // 2
    tn = 256 if swiglu else 512
    nj = (half if swiglu else rows) // tn
    nw = 2 if swiglu else 1
    nd = len(deps)

    def body(x_ref, g_ref, *rest):
        w_refs, outs = rest[:nw], rest[nw + nd:]
        h_ref, z_ref = outs[:2]

        @pl.when(pl.program_id(1) == 0)
        def _():
            xv = x_ref[...]
            r = lax.rsqrt(jnp.mean(xv * xv, axis=-1, keepdims=True) + EPS)
            h_ref[...] = (xv * r * g_ref[...]).astype(BF16)

        h = h_ref[...]
        if swiglu:
            gate = lax.dot_general(h, w_refs[0][...], _NT, preferred_element_type=F32)
            up = lax.dot_general(h, w_refs[1][...], _NT, preferred_element_type=F32)
            z_ref[0] = gate.astype(BF16)
            z_ref[1] = up.astype(BF16)
            outs[2][...] = (gate * _sigmoid(gate) * up).astype(BF16)
        else:
            z_ref[...] = lax.dot_general(h, w_refs[0][...], _NT, preferred_element_type=F32)

    row = pl.BlockSpec((tm, D_MODEL), lambda i, j: (i, 0))
    w_specs = [pl.BlockSpec((tn, D_MODEL), lambda i, j: (j, 0))]
    out_specs = [row]
    out_shape = [jax.ShapeDtypeStruct((N_TOK, D_MODEL), BF16)]
    if swiglu:
        w_specs.append(pl.BlockSpec((tn, D_MODEL), lambda i, j: (j + nj, 0)))
        out_specs += [pl.BlockSpec((2, tm, tn), lambda i, j: (0, i, j)), pl.BlockSpec((tm, tn), lambda i, j: (i, j))]
        out_shape += [jax.ShapeDtypeStruct((2, N_TOK, half), BF16), jax.ShapeDtypeStruct((N_TOK, half), BF16)]
    else:
        out_specs.append(pl.BlockSpec((tm, tn), lambda i, j: (i, j)))
        out_shape.append(jax.ShapeDtypeStruct((N_TOK, rows), F32))
    return pl.pallas_call(
        body,
        name=name,
        grid=(N_TOK // tm, nj),
        in_specs=[row, pl.BlockSpec((1, D_MODEL), lambda i, j: (0, 0))] + w_specs + [ANY_SPEC] * nd,
        out_specs=out_specs,
        out_shape=out_shape,
        compiler_params=_cp("parallel", "arbitrary"),
    )(x, g, *([w_t] * nw), *deps)


def _swiglu_dgrad(dy16, w_out, z, *, scale, name, tm=1024, tn=256):
    def body(dy_ref, w_ref, z_ref, dz_ref):
        da = lax.dot_general(dy_ref[...], w_ref[...], _NT, preferred_element_type=F32) * scale
        gate, up = z_ref[0].astype(F32), z_ref[1].astype(F32)
        s = _sigmoid(gate)
        dz_ref[0] = (da * up * (s * (1.0 + gate * (1.0 - s)))).astype(BF16)
        dz_ref[1] = (da * (gate * s)).astype(BF16)

    planes = pl.BlockSpec((2, tm, tn), lambda i, j: (0, i, j))
    return pl.pallas_call(
        body,
        name=name,
        grid=(N_TOK // tm, D_FF // tn),
        in_specs=[pl.BlockSpec((tm, D_MODEL), lambda i, j: (i, 0)), pl.BlockSpec((tn, D_MODEL), lambda i, j: (j, 0)), planes],
        out_specs=planes,
        out_shape=jax.ShapeDtypeStruct((2, N_TOK, D_FF), BF16),
        compiler_params=_cp("parallel", "parallel"),
    )(dy16, w_out, z)


def _planes_wgrad(dz, h, *, name, tm=1408):
    per_plane = D_FF // tm

    def body(a_ref, b_ref, o_ref):
        o_ref[...] = lax.dot_general(a_ref[...], b_ref[...], _TN, preferred_element_type=F32).astype(BF16)

    return pl.pallas_call(
        body,
        name=name,
        grid=(2 * per_plane,),
        in_specs=[pl.BlockSpec((None, N_TOK, tm),
                               lambda i: (jnp.where(i < per_plane, 0, 1), 0, jnp.where(i < per_plane, i, i - per_plane))),
                  pl.BlockSpec((N_TOK, D_MODEL), lambda i: (0, 0), pipeline_mode=pl.Buffered(1))],
        out_specs=pl.BlockSpec((tm, D_MODEL), lambda i: (i, 0)),
        out_shape=jax.ShapeDtypeStruct((2 * D_FF, D_MODEL), BF16),
        compiler_params=_cp("parallel"),
    )(dz, h)


def _dgrad_norm_bwd(dz, w_t, x, g, dres, *, name, deps=(), tm=512):
    planes = dz.ndim == 3
    rows = w_t.shape[0]
    half = rows // 2
    nd = len(deps)

    def body(a_ref, b_ref, x_ref, g_ref, dres_ref, *rest):
        dx_ref, dx16_ref, dg_ref = rest[nd:]
        if planes:
            dh = jnp.dot(a_ref[0], b_ref[:half, :], preferred_element_type=F32) + jnp.dot(
                a_ref[1], b_ref[half:, :], preferred_element_type=F32)
        else:
            dh = jnp.dot(a_ref[...], b_ref[...], preferred_element_type=F32)
        xv = x_ref[...]
        r = lax.rsqrt(jnp.mean(xv * xv, axis=-1, keepdims=True) + EPS)
        xhat = xv * r
        part = jnp.sum(dh * xhat, axis=0, keepdims=True)

        @pl.when(pl.program_id(0) == 0)
        def _():
            dg_ref[...] = part

        @pl.when(pl.program_id(0) > 0)
        def _():
            dg_ref[...] += part

        dxh = dh * g_ref[...]
        dx = dres_ref[...] + r * (dxh - xhat * jnp.mean(dxh * xhat, axis=-1, keepdims=True))
        dx_ref[...] = dx
        dx16_ref[...] = dx.astype(BF16)

    a_spec = pl.BlockSpec((2, tm, half), lambda i: (0, i, 0)) if planes else pl.BlockSpec((tm, rows), lambda i: (i, 0))
    row = pl.BlockSpec((tm, D_MODEL), lambda i: (i, 0))
    vec = pl.BlockSpec((1, D_MODEL), lambda i: (0, 0))
    return pl.pallas_call(
        body,
        name=name,
        grid=(N_TOK // tm,),
        in_specs=[a_spec, pl.BlockSpec((rows, D_MODEL), lambda i: (0, 0), pipeline_mode=pl.Buffered(1)), row, vec, row]
        + [ANY_SPEC] * nd,
        out_specs=[row, row, vec],
        out_shape=[jax.ShapeDtypeStruct((N_TOK, D_MODEL), F32), jax.ShapeDtypeStruct((N_TOK, D_MODEL), BF16),
                   jax.ShapeDtypeStruct((1, D_MODEL), F32)],
        compiler_params=_cp("arbitrary"),
    )(dz, w_t, x, g, dres, *deps)


def _loss_head(x, g, target, *, tm=512):
    def body(x_ref, g_ref, t_ref, dx_ref, dx16_ref, dg_ref, loss_ref):
        xv = x_ref[...]
        gv = g_ref[...]
        r = lax.rsqrt(jnp.mean(xv * xv, axis=-1, keepdims=True) + EPS)
        xhat = xv * r
        err = xhat * gv - t_ref[...]
        loss_part = jnp.zeros((1, 128), F32) + 0.5 * jnp.sum(jnp.mean(err * err, axis=-1, keepdims=True))
        dy = err * (1.0 / D_MODEL)
        dg_part = jnp.sum(dy * xhat, axis=0, keepdims=True)

        @pl.when(pl.program_id(0) == 0)
        def _():
            dg_ref[...] = dg_part
            loss_ref[...] = loss_part

        @pl.when(pl.program_id(0) > 0)
        def _():
            dg_ref[...] += dg_part
            loss_ref[...] += loss_part

        dxh = dy * gv
        dx = r * (dxh - xhat * jnp.mean(dxh * xhat, axis=-1, keepdims=True))
        dx_ref[...] = dx
        dx16_ref[...] = dx.astype(BF16)

    row = pl.BlockSpec((tm, D_MODEL), lambda i: (i, 0))
    vec = pl.BlockSpec((1, D_MODEL), lambda i: (0, 0))
    return pl.pallas_call(
        body,
        name="loss_head",
        grid=(N_TOK // tm,),
        in_specs=[row, vec, row],
        out_specs=[row, row, vec, pl.BlockSpec((1, 128), lambda i: (0, 0))],
        out_shape=[
            jax.ShapeDtypeStruct((N_TOK, D_MODEL), F32),
            jax.ShapeDtypeStruct((N_TOK, D_MODEL), BF16),
            jax.ShapeDtypeStruct((1, D_MODEL), F32),
            jax.ShapeDtypeStruct((1, 128), F32),
        ],
        compiler_params=_cp("arbitrary"),
    )(x, g, target)


XA_TQ = 1024
XA_SCALE = XA_DIM ** -0.5


def _attn_probs(q16, k16):
    s = lax.dot_general(q16, k16, _NT, preferred_element_type=F32) * XA_SCALE
    e = jnp.exp(s - jnp.max(s, axis=-1, keepdims=True))
    return e / jnp.sum(e, axis=-1, keepdims=True)


def _attn_fwd(z, kv, *, name):
    nt = SEQ // XA_TQ

    def body(q_ref, k_ref, v_ref, o_ref):
        p = _attn_probs(q_ref[...].astype(BF16), k_ref[...].astype(BF16))
        o_ref[...] = jnp.dot(p.astype(BF16), v_ref[...].astype(BF16), preferred_element_type=F32).astype(BF16)

    return pl.pallas_call(
        body,
        name=name,
        grid=(B_LOC, XA_HEADS, nt),
        in_specs=[
            pl.BlockSpec((XA_TQ, XA_DIM), lambda b, h, t: (b * nt + t, XA_OFF // XA_DIM + h)),
            pl.BlockSpec((MEM_LEN, XA_DIM), lambda b, h, t: (b, h)),
            pl.BlockSpec((MEM_LEN, XA_DIM), lambda b, h, t: (b, XA_HEADS + h)),
        ],
        out_specs=pl.BlockSpec((XA_TQ, XA_DIM), lambda b, h, t: (b * nt + t, h)),
        out_shape=jax.ShapeDtypeStruct((N_TOK, XA_HEADS * XA_DIM), BF16),
        compiler_params=_cp("parallel", "parallel", "arbitrary"),
    )(z, kv, kv)


def _attn_bwd(z, kv, dcat, *, do_off, name):
    nt = SEQ // XA_TQ

    def body(q_ref, k_ref, v_ref, do_ref, dq_ref, dk_ref, dv_ref):
        q16 = q_ref[...].astype(BF16)
        k16 = k_ref[...].astype(BF16)
        v16 = v_ref[...].astype(BF16)
        do16 = do_ref[...].astype(BF16)
        p = _attn_probs(q16, k16)
        dv_part = lax.dot_general(p.astype(BF16), do16, _TN, preferred_element_type=F32)
        dp = lax.dot_general(do16, v16, _NT, preferred_element_type=F32)
        ds16 = (p * (dp - jnp.sum(dp * p, axis=-1, keepdims=True)) * XA_SCALE).astype(BF16)
        dq_ref[...] = jnp.dot(ds16, k16, preferred_element_type=F32).astype(BF16)
        dk_part = lax.dot_general(ds16, q16, _TN, preferred_element_type=F32)

        @pl.when(pl.program_id(2) == 0)
        def _():
            dk_ref[...] = dk_part
            dv_ref[...] = dv_part

        @pl.when(pl.program_id(2) > 0)
        def _():
            dk_ref[...] += dk_part
            dv_ref[...] += dv_part

    qspec = pl.BlockSpec((XA_TQ, XA_DIM), lambda b, h, t: (b * nt + t, XA_OFF // XA_DIM + h))
    kspec = lambda off: pl.BlockSpec((MEM_LEN, XA_DIM), lambda b, h, t: (b, off + h))
    return pl.pallas_call(
        body,
        name=name,
        grid=(B_LOC, XA_HEADS, nt),
        in_specs=[qspec, kspec(0), kspec(XA_HEADS),
                  pl.BlockSpec((XA_TQ, XA_DIM), lambda b, h, t: (b * nt + t, do_off // XA_DIM + h))],
        out_specs=[pl.BlockSpec((XA_TQ, XA_DIM), lambda b, h, t: (b * nt + t, h)), kspec(0), kspec(0)],
        out_shape=[
            jax.ShapeDtypeStruct((N_TOK, XA_HEADS * XA_DIM), BF16),
            jax.ShapeDtypeStruct((B_LOC * MEM_LEN, XA_HEADS * XA_DIM), F32),
            jax.ShapeDtypeStruct((B_LOC * MEM_LEN, XA_HEADS * XA_DIM), F32),
        ],
        compiler_params=_cp("parallel", "parallel", "arbitrary"),
    )(z, kv, kv, dcat)


def _tril(n):
    return lax.broadcasted_iota(jnp.int32, (n, n), 0) >= lax.broadcasted_iota(jnp.int32, (n, n), 1)


def _lower_bound(lbl):
    e = jnp.exp(lbl - jnp.max(lbl, axis=0, keepdims=True))
    p = e / jnp.sum(e, axis=0, keepdims=True)
    return p[0:1, :], p


def _hgrn_gates(zq, zf, lb, tril_f):
    sig = _sigmoid(zf)
    f = lb + (1.0 - lb) * sig
    kk = 1.0 - f
    sq = _sigmoid(zq)
    q = zq * sq
    b = jnp.dot(tril_f, jnp.log(f), preferred_element_type=F32, precision=lax.Precision.HIGHEST)
    bl = b[HG_CHUNK - 1:HG_CHUNK, :]
    return q, sq, sig, f, kk, b, bl


HG_TB = 512
HG_CPB = HG_TB // HG_CHUNK
HG_NT = SEQ // HG_TB
HG_WIDTH = HG_HEADS * HG_DIM


def _head(h, section=0):
    return slice(section * HG_WIDTH + h * HG_DIM, section * HG_WIDTH + (h + 1) * HG_DIM)


def _hgrn_fwd(z, o_mem, lb_logits, gnorm):
    def body(zq_ref, zf_ref, zi_ref, zg_ref, omem_ref, lbl_ref, gn_ref, o_ref, opre_ref, sall_ref, st_ref):
        lb, _ = _lower_bound(lbl_ref[...])
        gn = gn_ref[...]
        mask = _tril(HG_CHUNK)
        tril_f = mask.astype(F32)
        o_ref[:, HG_WIDTH:] = omem_ref[...]

        @pl.when(pl.program_id(1) == 0)
        def _():
            st_ref[...] = jnp.zeros_like(st_ref)

        def chunk(c, carry):
            rows = pl.ds(pl.multiple_of(c * HG_CHUNK, HG_CHUNK), HG_CHUNK)
            q, _, _, _, kk, b, bl = _hgrn_gates(zq_ref[rows, :], zf_ref[rows, :], lb, tril_f)
            v16 = zi_ref[rows, :].astype(BF16)
            qd16 = (q * jnp.exp(b)).astype(BF16)
            ki16 = (kk * jnp.exp(-b)).astype(BF16)
            kd16 = (kk * jnp.exp(bl - b)).astype(BF16)
            ebl = jnp.exp(bl)
            zg = zg_ref[rows, :]
            gate = zg * _sigmoid(zg)
            for h in range(HG_HEADS):
                sl = _head(h)
                a = jnp.where(mask, lax.dot_general(qd16[:, sl], ki16[:, sl], _NT, preferred_element_type=F32), 0.0)
                st = st_ref[h]
                sall_ref[0, h, c] = st
                o = jnp.dot(a.astype(BF16), v16[:, sl], preferred_element_type=F32) + lax.dot_general(
                    qd16[:, sl], st.astype(BF16), _NT, preferred_element_type=F32)
                st_ref[h] = st * ebl[:, sl] + lax.dot_general(v16[:, sl], kd16[:, sl], _TN, preferred_element_type=F32)
                opre_ref[rows, sl] = o
                r = lax.rsqrt(jnp.mean(o * o, axis=-1, keepdims=True) + EPS)
                o_ref[rows, sl] = ((o * r * gn) * gate[:, sl]).astype(BF16)
            return carry

        lax.fori_loop(0, HG_CPB, chunk, 0)

    zspec = lambda s: pl.BlockSpec((HG_TB, HG_WIDTH), lambda b, t: (b * HG_NT + t, s))
    return pl.pallas_call(
        body,
        name="hgrn_fwd",
        grid=(B_LOC, HG_NT),
        in_specs=[zspec(0), zspec(1), zspec(2), zspec(3), zspec(0),
                  pl.BlockSpec((3, HG_WIDTH), lambda b, t: (0, 0)), pl.BlockSpec((1, HG_DIM), lambda b, t: (0, 0))],
        out_specs=[pl.BlockSpec((HG_TB, 2 * HG_WIDTH), lambda b, t: (b * HG_NT + t, 0)), zspec(0),
                   pl.BlockSpec((1, HG_HEADS, HG_CPB, HG_DIM, HG_DIM), lambda b, t: (b, 0, t, 0, 0))],
        out_shape=[
            jax.ShapeDtypeStruct((N_TOK, 2 * HG_WIDTH), BF16),
            jax.ShapeDtypeStruct((N_TOK, HG_WIDTH), F32),
            jax.ShapeDtypeStruct((B_LOC, HG_HEADS, HG_NCHUNK, HG_DIM, HG_DIM), F32),
        ],
        scratch_shapes=[pltpu.VMEM((HG_HEADS, HG_DIM, HG_DIM), F32)],
        compiler_params=_cp("parallel", "arbitrary"),
    )(z, z, z, z, o_mem, lb_logits, gnorm)


def _hgrn_bwd(z, opre, dcat, dq_mem, sall, lb_logits, gnorm):
    def body(zq_ref, zf_ref, zi_ref, zg_ref, opre_ref, dout_ref, dqm_ref, sall_ref, lbl_ref, gn_ref,
             dz_ref, dlbl_ref, dgn_ref, dst_ref, dlb_ref, dgn_acc, db_ref, dkk_ref, dbl_ref):
        b_id, t_id = pl.program_id(0), pl.program_id(1)
        lb, p = _lower_bound(lbl_ref[...])
        gn = gn_ref[...]
        mask = _tril(HG_CHUNK)
        tril_f = mask.astype(F32)
        dz_ref[:, 4 * HG_WIDTH:] = dqm_ref[...]

        @pl.when(t_id == 0)
        def _():
            dst_ref[...] = jnp.zeros_like(dst_ref)
            dlb_ref[...] = jnp.zeros_like(dlb_ref)

        @pl.when((b_id == 0) & (t_id == 0))
        def _():
            dgn_acc[...] = jnp.zeros_like(dgn_acc)

        def chunk(i, carry):
            c = HG_CPB - 1 - i
            rows = pl.ds(pl.multiple_of(c * HG_CHUNK, HG_CHUNK), HG_CHUNK)
            zq, zg = zq_ref[rows, :], zg_ref[rows, :]
            q, sq, sig, f, kk, b, bl = _hgrn_gates(zq, zf_ref[rows, :], lb, tril_f)
            v16 = zi_ref[rows, :].astype(BF16)
            eb, enb, ebl_b, ebl = jnp.exp(b), jnp.exp(-b), jnp.exp(bl - b), jnp.exp(bl)
            qd, ki, kd = q * eb, kk * enb, kk * ebl_b
            qd16, ki16, kd16 = qd.astype(BF16), ki.astype(BF16), kd.astype(BF16)
            o_all = opre_ref[rows, :]
            dout = dout_ref[rows, :]
            sg = _sigmoid(zg)
            d_on_all = dout * (zg * sg)
            dgate = dout * (sg * (1.0 + zg * (1.0 - sg)))
            dq_scale = eb * (sq * (1.0 + zq * (1.0 - sq)))
            for h in range(HG_HEADS):
                sl = _head(h)
                o = o_all[:, sl]
                r = lax.rsqrt(jnp.mean(o * o, axis=-1, keepdims=True) + EPS)
                ohat = o * r
                d_on = d_on_all[:, sl]
                dz_ref[rows, _head(h, 3)] = (dgate[:, sl] * (ohat * gn)).astype(BF16)
                dgn_acc[...] += jnp.sum(d_on * ohat, axis=0, keepdims=True)
                dohat = d_on * gn
                do16 = (r * (dohat - ohat * jnp.mean(dohat * ohat, axis=-1, keepdims=True))).astype(BF16)
                st = sall_ref[0, h, c]
                dst = dst_ref[h]
                st16, dst16 = st.astype(BF16), dst.astype(BF16)
                qd_h, ki_h, kd_h, v_h = qd16[:, sl], ki16[:, sl], kd16[:, sl], v16[:, sl]
                a16 = jnp.where(mask, lax.dot_general(qd_h, ki_h, _NT, preferred_element_type=F32), 0.0).astype(BF16)
                da16 = jnp.where(mask, lax.dot_general(do16, v_h, _NT, preferred_element_type=F32), 0.0).astype(BF16)
                dv = lax.dot_general(a16, do16, _TN, preferred_element_type=F32) + lax.dot_general(
                    kd_h, dst16, _NT, preferred_element_type=F32)
                dqd = jnp.dot(da16, ki_h, preferred_element_type=F32) + jnp.dot(do16, st16, preferred_element_type=F32)
                dki = lax.dot_general(da16, qd_h, _TN, preferred_element_type=F32)
                dkd = jnp.dot(v_h, dst16, preferred_element_type=F32)
                dbl_ref[:, sl] = jnp.sum(dkd * kd[:, sl], axis=0, keepdims=True) + ebl[:, sl] * jnp.sum(
                    st * dst, axis=0, keepdims=True)
                dst_ref[h] = dst * ebl[:, sl] + lax.dot_general(do16, qd_h, _TN, preferred_element_type=F32)
                dz_ref[rows, _head(h, 2)] = dv.astype(BF16)
                dz_ref[rows, sl] = (dqd * dq_scale[:, sl]).astype(BF16)
                dkk_ref[:, sl] = dki * enb[:, sl] + dkd * ebl_b[:, sl]
                db_ref[:, sl] = dqd * qd[:, sl] - dki * ki[:, sl] - dkd * kd[:, sl]
            dlogf = lax.dot_general(tril_f, db_ref[...], _TN, preferred_element_type=F32,
                                    precision=lax.Precision.HIGHEST) + dbl_ref[...]
            df = dlogf / f - dkk_ref[...]
            dz_ref[rows, HG_WIDTH:2 * HG_WIDTH] = (df * (1.0 - lb) * sig * (1.0 - sig)).astype(BF16)
            dlb_ref[...] += jnp.sum(df * (1.0 - sig), axis=0, keepdims=True)
            return carry

        lax.fori_loop(0, HG_CPB, chunk, 0)

        @pl.when(t_id == HG_NT - 1)
        def _():
            row0 = (lax.broadcasted_iota(jnp.int32, (3, HG_WIDTH), 0) == 0).astype(F32)
            dlbl_part = dlb_ref[...] * lb * (row0 - p)

            @pl.when(b_id == 0)
            def _():
                dlbl_ref[...] = dlbl_part

            @pl.when(b_id > 0)
            def _():
                dlbl_ref[...] += dlbl_part

            dgn_ref[...] = dgn_acc[...]

    rev = lambda b, t: b * HG_NT + HG_NT - 1 - t
    zspec = lambda s: pl.BlockSpec((HG_TB, HG_WIDTH), lambda b, t: (rev(b, t), s))
    return pl.pallas_call(
        body,
        name="hgrn_bwd",
        grid=(B_LOC, HG_NT),
        in_specs=[zspec(0), zspec(1), zspec(2), zspec(3), zspec(0), zspec(0), zspec(0),
                  pl.BlockSpec((1, HG_HEADS, HG_CPB, HG_DIM, HG_DIM), lambda b, t: (b, 0, HG_NT - 1 - t, 0, 0)),
                  pl.BlockSpec((3, HG_WIDTH), lambda b, t: (0, 0)), pl.BlockSpec((1, HG_DIM), lambda b, t: (0, 0))],
        out_specs=[pl.BlockSpec((HG_TB, 5 * HG_WIDTH), lambda b, t: (rev(b, t), 0)),
                   pl.BlockSpec((3, HG_WIDTH), lambda b, t: (0, 0)), pl.BlockSpec((1, HG_DIM), lambda b, t: (0, 0))],
        out_shape=[jax.ShapeDtypeStruct((N_TOK, 5 * HG_WIDTH), BF16),
                   jax.ShapeDtypeStruct((3, HG_WIDTH), F32), jax.ShapeDtypeStruct((1, HG_DIM), F32)],
        scratch_shapes=[pltpu.VMEM((HG_HEADS, HG_DIM, HG_DIM), F32), pltpu.VMEM((1, HG_WIDTH), F32),
                        pltpu.VMEM((1, HG_DIM), F32), pltpu.VMEM((HG_CHUNK, HG_WIDTH), F32),
                        pltpu.VMEM((HG_CHUNK, HG_WIDTH), F32), pltpu.VMEM((1, HG_WIDTH), F32)],
        compiler_params=_cp("arbitrary", "arbitrary"),
    )(z, z, z, z, opre, dcat, dq_mem, sall, lb_logits, gnorm)


GM_TM = 256


def _gmlp_norm(zv, ln_g, ln_b):
    gv, dgelu = _gelu_parts(zv)
    xc = gv - jnp.mean(gv, axis=-1, keepdims=True)
    rstd = lax.rsqrt(jnp.mean(xc * xc, axis=-1, keepdims=True) + EPS)
    vhat = xc * rstd
    return vhat * ln_g + ln_b, vhat, rstd, dgelu


def _gmlp_specs():
    half = lambda j: pl.BlockSpec((GM_TM, GM_WIDTH), lambda i: (i, j))
    vec = pl.BlockSpec((1, GM_WIDTH), lambda i: (0, 0))
    w = pl.BlockSpec((GM_GROUPS, GM_CHUNK, GM_CHUNK), lambda i: (0, 0, 0))
    bt = pl.BlockSpec((GM_CHUNK, GM_GROUPS), lambda i: (0, 0))
    return half, vec, w, bt


def _gmlp_fwd(z, o_mem, ln_g, ln_b, w_s, b_st):
    def body(zu_ref, zv_ref, omem_ref, g_ref, b_ref, w_ref, bt_ref, o_ref):
        o_ref[:, GM_WIDTH:] = omem_ref[...]
        u, _ = _gelu_parts(zu_ref[...])
        v, _, _, _ = _gmlp_norm(zv_ref[...], g_ref[...], b_ref[...])
        v16 = v.astype(BF16)
        mask = _tril(GM_CHUNK)
        bt = bt_ref[...]
        for g in range(GM_GROUPS):
            wm16 = jnp.where(mask, w_ref[g], 0.0).astype(BF16)
            cols = slice(g * GM_GDIM, (g + 1) * GM_GDIM)
            for c in range(GM_TM // GM_CHUNK):
                rows = slice(c * GM_CHUNK, (c + 1) * GM_CHUNK)
                mixed = jnp.dot(wm16, v16[rows, cols], preferred_element_type=F32) + bt[:, g:g + 1]
                o_ref[rows, cols] = (u[rows, cols] * mixed).astype(BF16)

    half, vec, w, bt = _gmlp_specs()
    return pl.pallas_call(
        body,
        name="gmlp_fwd",
        grid=(N_TOK // GM_TM,),
        in_specs=[half(0), half(1), pl.BlockSpec((GM_TM, XA_HEADS * XA_DIM), lambda i: (i, 0)), vec, vec, w, bt],
        out_specs=pl.BlockSpec((GM_TM, GM_WIDTH + XA_HEADS * XA_DIM), lambda i: (i, 0)),
        out_shape=jax.ShapeDtypeStruct((N_TOK, GM_WIDTH + XA_HEADS * XA_DIM), BF16),
        compiler_params=_cp("parallel"),
    )(z, z, o_mem, ln_g, ln_b, w_s, b_st)


def _gmlp_bwd(z, dcat, dq_mem, ln_g, ln_b, w_s, b_st):
    def body(zu_ref, zv_ref, dout_ref, dqm_ref, g_ref, b_ref, w_ref, bt_ref,
             dz_ref, dw_ref, dbt_ref, dg_ref, db_ref, dv_ref):
        dz_ref[:, 2 * GM_WIDTH:] = dqm_ref[...]
        @pl.when(pl.program_id(0) == 0)
        def _():
            dw_ref[...] = jnp.zeros_like(dw_ref)
            dbt_ref[...] = jnp.zeros_like(dbt_ref)
            dg_ref[...] = jnp.zeros_like(dg_ref)
            db_ref[...] = jnp.zeros_like(db_ref)

        zu = zu_ref[...]
        u, du_dz = _gelu_parts(zu)
        ln_g = g_ref[...]
        v, vhat, rstd, dgv_dz = _gmlp_norm(zv_ref[...], ln_g, b_ref[...])
        v16 = v.astype(BF16)
        dout = dout_ref[...]
        dmixed = dout * u
        dm16 = dmixed.astype(BF16)
        mask = _tril(GM_CHUNK)
        bt = bt_ref[...]
        group_id = lax.broadcasted_iota(jnp.int32, (1, GM_GROUPS), 1)
        dbt = jnp.zeros((GM_CHUNK, GM_GROUPS), F32)
        for g in range(GM_GROUPS):
            wm16 = jnp.where(mask, w_ref[g], 0.0).astype(BF16)
            cols = slice(g * GM_GDIM, (g + 1) * GM_GDIM)
            dw = jnp.zeros((GM_CHUNK, GM_CHUNK), F32)
            dbt_g = jnp.zeros((GM_CHUNK, 1), F32)
            for c in range(GM_TM // GM_CHUNK):
                rows = slice(c * GM_CHUNK, (c + 1) * GM_CHUNK)
                mixed = jnp.dot(wm16, v16[rows, cols], preferred_element_type=F32) + bt[:, g:g + 1]
                dz_ref[rows, cols] = (dout[rows, cols] * mixed * du_dz[rows, cols]).astype(BF16)
                dw += lax.dot_general(dm16[rows, cols], v16[rows, cols], _NT, preferred_element_type=F32)
                dbt_g += jnp.sum(dmixed[rows, cols], axis=-1, keepdims=True)
                dv_ref[rows, cols] = lax.dot_general(wm16, dm16[rows, cols], _TN, preferred_element_type=F32)
            dw_ref[g] += jnp.where(mask, dw, 0.0)
            dbt = dbt + dbt_g * (group_id == g).astype(F32)
        dbt_ref[...] += dbt
        dv = dv_ref[...]
        dg_ref[...] += jnp.sum(dv * vhat, axis=0, keepdims=True)
        db_ref[...] += jnp.sum(dv, axis=0, keepdims=True)
        dvh = dv * ln_g
        dgv = rstd * (dvh - jnp.mean(dvh, axis=-1, keepdims=True) - vhat * jnp.mean(dvh * vhat, axis=-1, keepdims=True))
        dz_ref[:, GM_WIDTH:2 * GM_WIDTH] = (dgv * dgv_dz).astype(BF16)

    half, vec, w, bt = _gmlp_specs()
    dz_width = 2 * GM_WIDTH + XA_HEADS * XA_DIM
    return pl.pallas_call(
        body,
        name="gmlp_bwd",
        grid=(N_TOK // GM_TM,),
        in_specs=[half(0), half(1), half(0), pl.BlockSpec((GM_TM, XA_HEADS * XA_DIM), lambda i: (i, 0)), vec, vec, w, bt],
        out_specs=[pl.BlockSpec((GM_TM, dz_width), lambda i: (i, 0)), w, bt, vec, vec],
        out_shape=[jax.ShapeDtypeStruct((N_TOK, dz_width), BF16),
                   jax.ShapeDtypeStruct((GM_GROUPS, GM_CHUNK, GM_CHUNK), F32),
                   jax.ShapeDtypeStruct((GM_CHUNK, GM_GROUPS), F32),
                   jax.ShapeDtypeStruct((1, GM_WIDTH), F32), jax.ShapeDtypeStruct((1, GM_WIDTH), F32)],
        scratch_shapes=[pltpu.VMEM((GM_TM, GM_WIDTH), F32)],
        compiler_params=_cp("arbitrary"),
    )(z, z, dcat, dq_mem, ln_g, ln_b, w_s, b_st)


def _own_slot(shape):
    return pl.BlockSpec((None,) + tuple(shape), lambda i, me_ref: (me_ref[0],) + (0,) * len(shape))


def _place_rows(w, layer, cuts_columns, me, *, name):
    _, r, c = w.shape
    n = c if cuts_columns else r

    def body(me_ref, w_ref, o_ref):
        wv = w_ref[...]
        o_ref[...] = (wv.T if cuts_columns else wv).astype(BF16)

    return pl.pallas_call(
        body,
        name=name,
        grid_spec=pltpu.PrefetchScalarGridSpec(
            num_scalar_prefetch=1, grid=(1,),
            in_specs=[pl.BlockSpec((None, r, c), lambda i, me_ref: (layer, 0, 0))],
            out_specs=_own_slot((n, D_MODEL))),
        out_shape=jax.ShapeDtypeStruct((N_DEV, n, D_MODEL), BF16),
        compiler_params=_cp("arbitrary"),
    )(me, w)


def _place_ln(ln_g, ln_b, me):
    blk = ln_g.shape[1]

    def body(me_ref, g_ref, b_ref, o_ref):
        o_ref[...] = jnp.zeros_like(o_ref)
        o_ref[0:1, :] = g_ref[...]
        o_ref[1:2, :] = b_ref[...]

    vec = pl.BlockSpec((1, blk), lambda i, me_ref: (0, 0))
    return pl.pallas_call(
        body,
        name="place_ln",
        grid_spec=pltpu.PrefetchScalarGridSpec(
            num_scalar_prefetch=1, grid=(1,), in_specs=[vec, vec], out_specs=_own_slot((8, blk))),
        out_shape=jax.ShapeDtypeStruct((N_DEV, 8, blk), F32),
        compiler_params=_cp("arbitrary"),
    )(me, ln_g, ln_b)


def _place_slab(a, me, *, name):
    def body(me_ref, a_ref, o_ref):
        o_ref[...] = a_ref[...]

    return pl.pallas_call(
        body,
        name=name,
        grid_spec=pltpu.PrefetchScalarGridSpec(
            num_scalar_prefetch=1, grid=(1,),
            in_specs=[pl.BlockSpec(a.shape, lambda i, me_ref: (0, 0))], out_specs=_own_slot(a.shape)),
        out_shape=jax.ShapeDtypeStruct((N_DEV,) + a.shape, a.dtype),
        compiler_params=_cp("arbitrary"),
    )(me, a)


def _place_own(grads, me, *, name):
    k = len(grads)

    def body(me_ref, *refs):
        for src, dst in zip(refs[:k], refs[k:]):
            dst[...] = src[...]

    specs = [_own_slot(g.shape[1:]) for g in grads]
    return pl.pallas_call(
        body,
        name=name,
        grid_spec=pltpu.PrefetchScalarGridSpec(num_scalar_prefetch=1, grid=(1,), in_specs=specs, out_specs=specs),
        out_shape=[jax.ShapeDtypeStruct(g.shape, g.dtype) for g in grads],
        compiler_params=_cp("arbitrary"),
    )(me, *grads)


def _mesh_pos():
    x, y, c = (lax.axis_index(a) for a in MESH_AXES)
    return x, y, c, 4 * x + 2 * y + c


def _peer(x, y, c, r):
    px = 1 - x if r & 4 else x
    py = 1 - y if r & 2 else y
    pc = 1 - c if r & 1 else c
    return (px, py, pc), 4 * px + 2 * py + pc


def _peer_copies(srcs, lands, send_sems, recv_sems, gather, waits):
    x, y, c, me = _mesh_pos()
    pairs = []
    for r in range(1, N_DEV):
        peer, peer_blk = _peer(x, y, c, r)
        for k, (src, land) in enumerate(zip(srcs, lands)):
            idx = k * (N_DEV - 1) + r - 1
            sems = dict(send_sem=send_sems.at[idx], recv_sem=recv_sems.at[idx], device_id=peer,
                        device_id_type=pl.DeviceIdType.MESH)
            mine = pltpu.make_async_remote_copy(
                src_ref=src.at[me if gather else peer_blk], dst_ref=land.at[me], **sems)
            theirs = pltpu.make_async_remote_copy(src_ref=src.at[me], dst_ref=land.at[peer_blk], **sems) if waits else None
            pairs.append((mine, theirs))
    return pairs


DATAFLOW = pltpu.SideEffectType.DATAFLOW_SIDE_EFFECTING


def _in_hbm(a):
    return pltpu.with_memory_space_constraint(a, pltpu.HBM)


def _copies_start(srcs, lands, *, gather, name, deps=()):
    arrs = list(lands) if gather else list(srcs) + list(lands)
    n, k, nd = len(arrs), len(lands), len(deps)

    def body(*refs):
        ins, send_sems, recv_sems, token = refs[:n], refs[n + nd], refs[n + nd + 1], refs[2 * n + nd + 2]
        src_refs, land_refs = (ins, ins) if gather else (ins[:k], ins[k:])
        for mine, _ in _peer_copies(src_refs, land_refs, send_sems, recv_sems, gather, waits=False):
            mine.start()
        token[...] = jnp.zeros_like(token)

    n_cp = k * (N_DEV - 1)
    return pl.pallas_call(
        body,
        name=name,
        in_specs=[HBM_SPEC] * n + [ANY_SPEC] * nd,
        out_specs=(SEM_SPEC, SEM_SPEC, *[HBM_SPEC] * n, pl.BlockSpec(memory_space=pltpu.VMEM)),
        out_shape=(pltpu.SemaphoreType.DMA((n_cp,)), pltpu.SemaphoreType.DMA((n_cp,)),
                   *[pltpu.HBM(a.shape, a.dtype) for a in arrs], jax.ShapeDtypeStruct((8, 128), F32)),
        input_output_aliases={i: 2 + i for i in range(n)},
        compiler_params=pltpu.CompilerParams(has_side_effects=DATAFLOW),
    )(*[_in_hbm(a) for a in arrs], *deps)


def _copies_wait(arrs, send_sems, recv_sems, after, *, n_lands, gather, name):
    n, k = len(arrs), n_lands

    def body(*refs):
        ins, send_sems, recv_sems = refs[:n], refs[n], refs[n + 1]
        src_refs, land_refs = (ins, ins) if gather else (ins[:k], ins[k:])
        for mine, theirs in _peer_copies(src_refs, land_refs, send_sems, recv_sems, gather, waits=True):
            mine.wait_send()
            theirs.wait_recv()

    outs = pl.pallas_call(
        body,
        name=name,
        in_specs=[HBM_SPEC] * n + [SEM_SPEC, SEM_SPEC] + [ANY_SPEC] * len(after),
        out_specs=[HBM_SPEC] * n,
        out_shape=[pltpu.HBM(a.shape, a.dtype) for a in arrs],
        input_output_aliases={i: i for i in range(n)},
        compiler_params=pltpu.CompilerParams(has_side_effects=DATAFLOW),
    )(*arrs, send_sems, recv_sems, *after)
    return outs[n - k:]


def _exchange_small(slabs):
    n = len(slabs)

    def body(*refs):
        ins, outs = refs[:n], refs[n:2 * n]
        send_sems, recv_sems, local_sems = refs[2 * n:]
        x, y, c, me = _mesh_pos()
        own = [pltpu.make_async_copy(src, dst.at[me], local_sems.at[k]) for k, (src, dst) in enumerate(zip(ins, outs))]
        for cp in own:
            cp.start()
        sends, recvs = [], []
        for r in range(1, N_DEV):
            peer, peer_blk = _peer(x, y, c, r)
            for k, (src, dst) in enumerate(zip(ins, outs)):
                idx = k * (N_DEV - 1) + r - 1
                sems = dict(send_sem=send_sems.at[idx], recv_sem=recv_sems.at[idx], device_id=peer,
                            device_id_type=pl.DeviceIdType.MESH)
                send = pltpu.make_async_remote_copy(src_ref=src, dst_ref=dst.at[me], **sems)
                send.start()
                sends.append(send)
                recvs.append(pltpu.make_async_remote_copy(src_ref=src, dst_ref=dst.at[peer_blk], **sems))
        for cp in recvs:
            cp.wait_recv()
        for cp in sends:
            cp.wait_send()
        for cp in own:
            cp.wait()

    n_cp = n * (N_DEV - 1)
    return pl.pallas_call(
        body,
        name="exchange_small_grads",
        in_specs=[ANY_SPEC] * n,
        out_specs=[ANY_SPEC] * n,
        out_shape=[jax.ShapeDtypeStruct((N_DEV,) + s.shape, F32) for s in slabs],
        scratch_shapes=[pltpu.SemaphoreType.DMA((n_cp,)), pltpu.SemaphoreType.DMA((n_cp,)),
                        pltpu.SemaphoreType.DMA((n,))],
    )(*slabs)


def _adamw(w, g, m, v):
    m = ADAM_B1 * m + (1.0 - ADAM_B1) * g
    v = ADAM_B2 * v + (1.0 - ADAM_B2) * (g * g)
    m_hat = m / (1.0 - ADAM_B1 ** ADAM_STEP)
    v_hat = v / (1.0 - ADAM_B2 ** ADAM_STEP)
    return -ADAM_LR * (m_hat / (jnp.sqrt(v_hat) + ADAM_EPS) + ADAM_WD * w), m, v


ADAM_TC = 256


def _adam_big(slots, w, m, v, cuts_columns, *, name):
    layers, n, nj = len(slots), slots[0].shape[1], D_MODEL // ADAM_TC

    def body(*refs):
        s_refs = refs[:layers]
        w_ref, m_ref, v_ref, g_ref, d_ref, nm_ref, nv_ref, acc_ref = refs[layers:]
        for ll in range(layers):
            @pl.when(pl.program_id(0) == ll)
            def _(s_ref=s_refs[ll]):
                g = s_ref[0].astype(F32)
                for s in range(1, N_DEV):
                    g = g + s_ref[s].astype(F32)
                acc_ref[...] = g

        g = acc_ref[...].T if cuts_columns else acc_ref[...]
        g_ref[...] = g
        d_ref[...], nm_ref[...], nv_ref[...] = _adamw(w_ref[...], g, m_ref[...], v_ref[...])

    def slot_spec(ll):
        return pl.BlockSpec((N_DEV, n, ADAM_TC),
                            lambda l, j: (0, 0, jnp.where(l < ll, 0, jnp.where(l > ll, nj - 1, j))))

    if cuts_columns:
        w_spec = pl.BlockSpec((None, ADAM_TC, n), lambda l, j: (l, j, 0))
    else:
        w_spec = pl.BlockSpec((None, n, ADAM_TC), lambda l, j: (l, 0, j))
    return pl.pallas_call(
        body,
        name=name,
        grid=(layers, nj),
        in_specs=[slot_spec(ll) for ll in range(layers)] + [w_spec] * 3,
        out_specs=[w_spec] * 4,
        out_shape=[jax.ShapeDtypeStruct(w.shape, F32)] * 4,
        scratch_shapes=[pltpu.VMEM((n, ADAM_TC), F32)],
        compiler_params=_cp("arbitrary", "arbitrary"),
    )(*slots, w, m, v)


def _adam_slabs(slots, ws, ms, vs):
    n = len(slots)

    def body(*refs):
        ins, outs = refs[:4 * n], refs[4 * n:]
        for k in range(n):
            s_ref, w_ref, m_ref, v_ref = ins[k], ins[n + k], ins[2 * n + k], ins[3 * n + k]
            g = s_ref[0]
            for s in range(1, N_DEV):
                g = g + s_ref[s]
            outs[4 * k][...] = g
            outs[4 * k + 1][...], outs[4 * k + 2][...], outs[4 * k + 3][...] = _adamw(w_ref[...], g, m_ref[...], v_ref[...])

    res = pl.pallas_call(
        body,
        name="small_adamw",
        out_shape=[jax.ShapeDtypeStruct(w.shape, F32) for w in ws for _ in range(4)],
        compiler_params=pltpu.CompilerParams(vmem_limit_bytes=VMEM_LIMIT_BYTES),
    )(*slots, *ws, *ms, *vs)
    return [res[4 * k:4 * k + 4] for k in range(n)]


def _adam_vecs(gs, ws, ms, vs):
    n = len(gs)

    def body(*refs):
        ins, outs = refs[:4 * n], refs[4 * n:]
        for k in range(n):
            outs[3 * k][...], outs[3 * k + 1][...], outs[3 * k + 2][...] = _adamw(
                ins[n + k][...], ins[k][...], ins[2 * n + k][...], ins[3 * n + k][...])

    res = pl.pallas_call(
        body,
        name="ln_adamw",
        out_shape=[jax.ShapeDtypeStruct(w.shape, F32) for w in ws for _ in range(3)],
        compiler_params=pltpu.CompilerParams(vmem_limit_bytes=VMEM_LIMIT_BYTES),
    )(*gs, *ws, *ms, *vs)
    return [res[3 * k:3 * k + 3] for k in range(n)]


SLAB_AT = dict(mem_norm=0, lb_logits=1, ffn1_norm=4, mix_norm=6, hgrn_gnorm=8, gmlp_ln_g=9, gmlp_ln_b=11,
               gmlp_b_s=13, ffn2_norm=14, final_norm=16)
SLAB_ROWS = 24
SMALL_SHARDED = ("gmlp_ln_g", "gmlp_ln_b")


def _pack_slab(parts, *, name):
    flat, plan = [], []
    for pname, at in SLAB_AT.items():
        for a in parts.get(pname, ()):
            flat.append(a)
            plan.append((at, a.shape))
            at += max(1, a.shape[0] * a.shape[1] // D_MODEL)

    def body(*refs):
        o_ref = refs[-1]
        o_ref[...] = jnp.zeros_like(o_ref)
        for ref, (at, (r, w)) in zip(refs, plan):
            if w == D_MODEL or r == 1 and w < D_MODEL:
                o_ref[at:at + r, 0:w] = ref[...]
            elif w < D_MODEL:
                for j in range(r):
                    o_ref[at:at + 1, j * w:(j + 1) * w] = ref[j:j + 1, :]
            else:
                for j in range(w // D_MODEL):
                    o_ref[at + j:at + j + 1, :] = ref[:, j * D_MODEL:(j + 1) * D_MODEL]

    return pl.pallas_call(
        body,
        name=name,
        out_shape=jax.ShapeDtypeStruct((SLAB_ROWS, D_MODEL), F32),
        compiler_params=pltpu.CompilerParams(vmem_limit_bytes=VMEM_LIMIT_BYTES),
    )(*flat)


def _unpack_slab(slab, shapes):
    out = {}
    for pname, at in SLAB_AT.items():
        if pname in SMALL_SHARDED:
            continue
        size = math.prod(shapes[pname])
        rows = max(1, size // D_MODEL)
        out[pname] = slab[at:at + rows].reshape(-1)[:size].reshape(shapes[pname])
    return out


def _ffn_fwd(x, norm_g, block, layer, full, get_weights):
    tag = f"l{layer}_{block}"
    full.update(get_weights((layer, f"{block}_in"), (x,)))
    h, z, act = _norm_mm(x, norm_g, full[(f"{block}_w_in", layer)], swiglu=True, deps=full.pop("deps", ()),
                         name=f"{tag}_in")
    full.update(get_weights((layer, f"{block}_out"), (act,)))
    y = _mm(act, full[(f"{block}_w_out", layer)], tm=512, tn=D_MODEL, tk=D_FF, out_dtype=F32, res=x, scale=0.5,
            deps=full.pop("deps", ()), name=f"{tag}_out")
    return y, (x, h, z, act)


def _ffn_bwd(dy, dy16, saved, norm_g, w_in_t, w_out, tag, deps=()):
    x, h, z, act = saved
    dw_out = _mm(act, dy16, ta=True, tm=1408, tn=D_MODEL, tk=N_TOK, out_dtype=BF16, scale=0.5, deps=deps,
                 name=f"{tag}_out_wgrad")
    dz = _swiglu_dgrad(dy16, w_out, z, scale=0.5, name=f"{tag}_out_dgrad")
    dw_in_t = _planes_wgrad(dz, h, name=f"{tag}_in_wgrad")
    dx, dx16, dg = _dgrad_norm_bwd(dz, w_in_t, x, norm_g, dy, name=f"{tag}_in_dgrad")
    return dx, dx16, dg, dw_in_t, dw_out


def kernel(x, mem, mem_norm, lb_logits, ffn1_norm, ffn1_w_in, ffn1_w_out, mix_norm, mem_w_kv, hgrn_w_in, hgrn_gnorm, hgrn_w_out, gmlp_w_in, gmlp_ln_g, gmlp_ln_b, gmlp_w_s, gmlp_b_s, gmlp_w_out, ffn2_norm, ffn2_w_in, ffn2_w_out, final_norm, loss_target, m_mem_norm, m_lb_logits, m_ffn1_norm, m_ffn1_w_in, m_ffn1_w_out, m_mix_norm, m_mem_w_kv, m_hgrn_w_in, m_hgrn_gnorm, m_hgrn_w_out, m_gmlp_w_in, m_gmlp_ln_g, m_gmlp_ln_b, m_gmlp_w_s, m_gmlp_b_s, m_gmlp_w_out, m_ffn2_norm, m_ffn2_w_in, m_ffn2_w_out, m_final_norm, v_mem_norm, v_lb_logits, v_ffn1_norm, v_ffn1_w_in, v_ffn1_w_out, v_mix_norm, v_mem_w_kv, v_hgrn_w_in, v_hgrn_gnorm, v_hgrn_w_out, v_gmlp_w_in, v_gmlp_ln_g, v_gmlp_ln_b, v_gmlp_w_s, v_gmlp_b_s, v_gmlp_w_out, v_ffn2_norm, v_ffn2_w_in, v_ffn2_w_out, v_final_norm):
    weights = dict(mem_norm=mem_norm, lb_logits=lb_logits, ffn1_norm=ffn1_norm, ffn1_w_in=ffn1_w_in, ffn1_w_out=ffn1_w_out, mix_norm=mix_norm, mem_w_kv=mem_w_kv, hgrn_w_in=hgrn_w_in, hgrn_gnorm=hgrn_gnorm, hgrn_w_out=hgrn_w_out, gmlp_w_in=gmlp_w_in, gmlp_ln_g=gmlp_ln_g, gmlp_ln_b=gmlp_ln_b, gmlp_w_s=gmlp_w_s, gmlp_b_s=gmlp_b_s, gmlp_w_out=gmlp_w_out, ffn2_norm=ffn2_norm, ffn2_w_in=ffn2_w_in, ffn2_w_out=ffn2_w_out, final_norm=final_norm)
    mom_m = dict(mem_norm=m_mem_norm, lb_logits=m_lb_logits, ffn1_norm=m_ffn1_norm, ffn1_w_in=m_ffn1_w_in, ffn1_w_out=m_ffn1_w_out, mix_norm=m_mix_norm, mem_w_kv=m_mem_w_kv, hgrn_w_in=m_hgrn_w_in, hgrn_gnorm=m_hgrn_gnorm, hgrn_w_out=m_hgrn_w_out, gmlp_w_in=m_gmlp_w_in, gmlp_ln_g=m_gmlp_ln_g, gmlp_ln_b=m_gmlp_ln_b, gmlp_w_s=m_gmlp_w_s, gmlp_b_s=m_gmlp_b_s, gmlp_w_out=m_gmlp_w_out, ffn2_norm=m_ffn2_norm, ffn2_w_in=m_ffn2_w_in, ffn2_w_out=m_ffn2_w_out, final_norm=m_final_norm)
    mom_v = dict(mem_norm=v_mem_norm, lb_logits=v_lb_logits, ffn1_norm=v_ffn1_norm, ffn1_w_in=v_ffn1_w_in, ffn1_w_out=v_ffn1_w_out, mix_norm=v_mix_norm, mem_w_kv=v_mem_w_kv, hgrn_w_in=v_hgrn_w_in, hgrn_gnorm=v_hgrn_gnorm, hgrn_w_out=v_hgrn_w_out, gmlp_w_in=v_gmlp_w_in, gmlp_ln_g=v_gmlp_ln_g, gmlp_ln_b=v_gmlp_ln_b, gmlp_w_s=v_gmlp_w_s, gmlp_b_s=v_gmlp_b_s, gmlp_w_out=v_gmlp_w_out, ffn2_norm=v_ffn2_norm, ffn2_w_in=v_ffn2_w_in, ffn2_w_out=v_ffn2_w_out, final_norm=v_final_norm)
    order = list(weights)
    _, _, _, me = _mesh_pos()
    me_arr = jnp.reshape(me, (1,)).astype(jnp.int32)
    cuts = {name: c for name, c, _, _ in GROUPS}

    mix1 = (("mem_w_kv", 1), ("gmlp_w_in", 0), ("gmlp_w_out", 0))
    gather_plan = (
        ((0, "ffn1_in"), (("ffn1_w_in", 0),), None),
        ((0, "ffn1_out"), (("ffn1_w_out", 0),), 0),
        ((0, "mix_in"), (("mem_w_kv", 0), ("hgrn_w_in", 0)), 0),
        ((0, "mix_out"), (("hgrn_w_out", 0),), 2),
        ((0, "ffn2_in"), _stage_pieces(0, "ffn2"), 2),
        ((1, "ffn1_in"), _stage_pieces(1, "ffn1"), 3),
        ((1, "mix_in"), mix1, 4),
        ((1, "ffn2_in"), _stage_pieces(1, "ffn2"), 5),
    )
    gather = {}

    def start_gather(k, deps):
        use, pieces, _ = gather_plan[k]
        lands = [_place_rows(weights[name], l, cuts[name], me_arr, name=f"place_{name}_{l}") for name, l in pieces]
        if pieces is mix1:
            lands.append(_place_ln(gmlp_ln_g, gmlp_ln_b, me_arr))
        send_sems, recv_sems, *thru, token = _copies_start(lands, lands, gather=True, deps=deps,
                                                           name=f"gather_start_l{use[0]}_{use[1]}")
        gather[use] = (k, thru, send_sems, recv_sems)
        return token

    start_gather(0, ())

    def get_weights(use, after):
        if use not in gather:
            return {}
        k, thru, send_sems, recv_sems = gather[use]
        outs = _copies_wait(thru, send_sems, recv_sems, after, n_lands=len(thru), gather=True,
                            name=f"gather_wait_l{use[0]}_{use[1]}")
        pieces = gather_plan[k][1]
        w = {p: o.reshape(N_DEV * o.shape[1], D_MODEL) for p, o in zip(pieces, outs)}
        w["deps"] = tuple(start_gather(later, (outs[0],))
                          for later, (_, _, trigger) in enumerate(gather_plan) if trigger == k)
        if pieces is mix1:
            w["ln_g"] = outs[-1][:, 0, :].reshape(1, GM_WIDTH)
            w["ln_b"] = outs[-1][:, 1, :].reshape(1, GM_WIDTH)
        return w

    scatter = {}

    def put_grads(st, grads):
        if st == "w_s":
            land = _place_slab(grads.reshape(GM_GROUPS * GM_CHUNK, GM_CHUNK), me_arr, name="w_s_place")
            send_sems, recv_sems, *thru, token = _copies_start([land], [land], gather=True, name="w_s_start")
            scatter[st] = (thru, send_sems, recv_sems)
            return (token,)
        views = [grads[p].reshape(N_DEV, -1, D_MODEL) for p in _stage_pieces(*st)]
        recv = _place_own(views, me_arr, name=f"scatter_place_l{st[0]}_{st[1]}")
        send_sems, recv_sems, *thru, token = _copies_start(views, recv, gather=False,
                                                           name=f"scatter_start_l{st[0]}_{st[1]}")
        scatter[st] = (thru, send_sems, recv_sems)
        return (token,)

    dx, small, loss_part = _step_local(
        x, mem, loss_target, get_weights, put_grads, mem_norm, lb_logits, ffn1_norm, mix_norm, hgrn_gnorm,
        gmlp_w_s, gmlp_b_s, ffn2_norm, final_norm)

    def slots_of(blk, after):
        slots = {}
        for i in (1, 0):
            thru, send_sems, recv_sems = scatter[(i, blk)]
            outs = _copies_wait(thru, send_sems, recv_sems, after, n_lands=len(thru) // 2, gather=False,
                                name=f"scatter_wait_l{i}_{blk}")
            slots.update(zip(_stage_pieces(i, blk), outs))
        return slots

    grad, delta, new_m, new_v = {}, {}, {}, {}

    def adam_groups(slots, names):
        for name in names:
            layers = GROUP_LAYERS[name]
            grad[name], delta[name], new_m[name], new_v[name] = _adam_big(
                [slots[(name, l)] for l in range(layers)], weights[name], mom_m[name], mom_v[name], cuts[name],
                name=f"{name}_adamw")

    adam_groups(slots_of("ffn2", (dx,)), ("ffn2_w_in", "ffn2_w_out"))
    adam_groups(slots_of("mix", (delta["ffn2_w_out"],)),
                ("mem_w_kv", "gmlp_w_in", "gmlp_w_out", "hgrn_w_in", "hgrn_w_out"))

    def small_parts(src):
        parts = {n: [src[n].reshape(-1, src[n].shape[-1])] for n in SLAB_AT if n not in SMALL_SHARDED}
        return parts

    w_s_rows = lambda a: a.reshape(GM_GROUPS * GM_CHUNK, GM_CHUNK)
    (slab_slots,) = _exchange_small([_pack_slab(small, name="pack_small_grads")])
    thru, send_sems, recv_sems = scatter["w_s"]
    (ws_slots,) = _copies_wait(thru, send_sems, recv_sems, (slab_slots,), n_lands=1, gather=True, name="w_s_wait")
    (g_slab, d_slab, nm_slab, nv_slab), (g_ws, d_ws, nm_ws, nv_ws) = _adam_slabs(
        [slab_slots, ws_slots],
        [_pack_slab(small_parts(weights), name="pack_small_w"), w_s_rows(gmlp_w_s)],
        [_pack_slab(small_parts(mom_m), name="pack_small_m"), w_s_rows(m_gmlp_w_s)],
        [_pack_slab(small_parts(mom_v), name="pack_small_v"), w_s_rows(v_gmlp_w_s)])
    shapes = {n: weights[n].shape for n in SLAB_AT}
    for out, slab, ws in ((grad, g_slab, g_ws), (delta, d_slab, d_ws), (new_m, nm_slab, nm_ws), (new_v, nv_slab, nv_ws)):
        out.update(_unpack_slab(slab, shapes))
        out["gmlp_w_s"] = ws.reshape(gmlp_w_s.shape)
    blk = GM_WIDTH // N_DEV
    g_ln = [lax.dynamic_slice(g_slab[SLAB_AT[n]:SLAB_AT[n] + 2].reshape(1, GM_WIDTH), (0, me * blk), (1, blk))
            for n in SMALL_SHARDED]
    ln_out = _adam_vecs(g_ln, [weights[n] for n in SMALL_SHARDED], [mom_m[n] for n in SMALL_SHARDED],
                        [mom_v[n] for n in SMALL_SHARDED])
    for n, g, (d, nm, nv) in zip(SMALL_SHARDED, g_ln, ln_out):
        grad[n], delta[n], new_m[n], new_v[n] = g, d, nm, nv

    adam_groups(slots_of("ffn1", (delta["hgrn_w_out"], d_slab)), ("ffn1_w_in", "ffn1_w_out"))

    loss = lax.psum(loss_part[0, 0], MESH_AXES)
    grad_x = dx.reshape(B_LOC, SEQ, D_MODEL)
    return (loss, grad_x, *[grad[n] for n in order], *[delta[n] for n in order],
            *[new_m[n] for n in order], *[new_v[n] for n in order])


def _step_local(x, mem, loss_target, get_weights, put_grads, mem_norm, lb_logits, ffn1_norm, mix_norm, hgrn_gnorm,
                gmlp_w_s, gmlp_b_s, ffn2_norm, final_norm):
    w_s = gmlp_w_s[0]
    b_st = gmlp_b_s[0].T

    xs = x.reshape(N_TOK, D_MODEL)
    mem2d = mem.reshape(B_LOC * MEM_LEN, D_MODEL)
    mem_g = mem_norm.reshape(1, D_MODEL)
    saved, full = [], {}
    memn = _rms_fwd(mem2d, mem_g, name="mem_norm_fwd")
    for i in range(2):
        xs, s_ffn1 = _ffn_fwd(xs, ffn1_norm[i:i + 1], "ffn1", i, full, get_weights)
        full.update(get_weights((i, "mix_in"), (xs,)))
        mixer = "hgrn" if i == 0 else "gmlp"
        hm, zm = _norm_mm(xs, mix_norm[i:i + 1], full[(f"{mixer}_w_in", 0)], swiglu=False, deps=full.pop("deps", ()),
                          name=f"l{i}_mix_in")
        kv = _mm(memn, full[("mem_w_kv", i)], tb=True, tm=512, tn=512, tk=D_MODEL, out_dtype=F32, name=f"l{i}_mem_kv")
        o_mem = _attn_fwd(zm, kv, name=f"l{i}_attn")
        if i == 0:
            cat, o_pre, s_all = _hgrn_fwd(zm, o_mem, lb_logits, hgrn_gnorm)
            mix_saved = (o_pre, s_all)
        else:
            cat = _gmlp_fwd(zm, o_mem, full["ln_g"], full["ln_b"], w_s, b_st)
            mix_saved = ()
        x_mix = xs
        full.update(get_weights((i, "mix_out"), (cat,)))
        xs = _mm(cat, full[(f"{mixer}_w_out", 0)], tm=512, tn=D_MODEL, tk=cat.shape[1], out_dtype=F32, res=xs,
                 deps=full.pop("deps", ()), name=f"l{i}_mix_out")
        xs, s_ffn2 = _ffn_fwd(xs, ffn2_norm[i:i + 1], "ffn2", i, full, get_weights)
        saved.append((s_ffn1, (x_mix, hm, kv, zm, cat, mix_saved), s_ffn2))

    dx, dx16, d_final, loss_part = _loss_head(xs, final_norm.reshape(1, D_MODEL), loss_target.reshape(N_TOK, D_MODEL))

    small = {"final_norm": [d_final]}
    d_ffn1, d_ffn2, d_mix = [None, None], [None, None], [None, None]
    dmemn = jnp.zeros((B_LOC * MEM_LEN, D_MODEL), F32)
    deps = ()
    for i in (1, 0):
        s_ffn1, (x_mix, hm, kv, zm, cat, mix_saved), s_ffn2 = saved[i]
        dx, dx16, d_ffn2[i], dw_in_t, dw_out = _ffn_bwd(
            dx, dx16, s_ffn2, ffn2_norm[i:i + 1], full[("ffn2_w_in", i)], full[("ffn2_w_out", i)], f"l{i}_ffn2", deps)
        deps = put_grads((i, "ffn2"), {("ffn2_w_in", i): dw_in_t, ("ffn2_w_out", i): dw_out})
        mixer = "hgrn" if i == 0 else "gmlp"
        w_in_t, w_out = full[(f"{mixer}_w_in", 0)], full[(f"{mixer}_w_out", 0)]
        width = cat.shape[1]
        g_mix = {}
        g_mix[(f"{mixer}_w_out", 0)] = _mm(cat, dx16, ta=True, tm=1024, tn=D_MODEL, tk=N_TOK, out_dtype=BF16,
                                           deps=deps, name=f"l{i}_mix_out_wgrad")
        dcat = _mm(dx16, w_out, tb=True, tm=1024, tn=width // 2, tk=D_MODEL, out_dtype=F32, name=f"l{i}_mix_out_dgrad")
        dq, dk, dv = _attn_bwd(zm, kv, dcat, do_off=width - XA_HEADS * XA_DIM, name=f"l{i}_attn_bwd")
        if i == 0:
            dzm, dlbl, dgn = _hgrn_bwd(zm, mix_saved[0], dcat, dq, mix_saved[1], lb_logits, hgrn_gnorm)
            small["lb_logits"], small["hgrn_gnorm"] = [dlbl], [dgn]
            deps = ()
        else:
            dzm, dws, dbt, dlng, dlnb = _gmlp_bwd(zm, dcat, dq, full["ln_g"], full["ln_b"], w_s, b_st)
            small["gmlp_b_s"], small["gmlp_ln_g"], small["gmlp_ln_b"] = [dbt.T], [dlng], [dlnb]
            deps = put_grads("w_s", dws)
        g_mix[(f"{mixer}_w_in", 0)] = _mm(dzm, hm, ta=True, tm=1024, tn=D_MODEL, tk=N_TOK, out_dtype=BF16, deps=deps,
                                          name=f"l{i}_mix_in_wgrad")
        dkv = jnp.concatenate([dk, dv], axis=1)
        g_mix[("mem_w_kv", i)] = _mm(dkv, memn, ta=True, tm=512, tn=D_MODEL, tk=B_LOC * MEM_LEN, out_dtype=BF16,
                                     name=f"l{i}_mem_kv_wgrad")
        deps = put_grads((i, "mix"), g_mix)
        dx, dx16, d_mix[i] = _dgrad_norm_bwd(dzm, w_in_t, x_mix, mix_norm[i:i + 1], dx, deps=deps,
                                             name=f"l{i}_mix_in_dgrad")
        dmemn = _mm(dkv, full[("mem_w_kv", i)], tm=B_LOC * MEM_LEN, tn=D_MODEL, tk=512, out_dtype=F32, res=dmemn,
                    name=f"l{i}_mem_kv_dgrad")
        dx, dx16, d_ffn1[i], dw_in_t, dw_out = _ffn_bwd(
            dx, dx16, s_ffn1, ffn1_norm[i:i + 1], full[("ffn1_w_in", i)], full[("ffn1_w_out", i)], f"l{i}_ffn1")
        deps = put_grads((i, "ffn1"), {("ffn1_w_in", i): dw_in_t, ("ffn1_w_out", i): dw_out})
    _, _, dmem_g = _rms_bwd(mem2d, mem_g, dmemn, dmemn, deps=deps, name="mem_norm_bwd")
    small.update(mem_norm=[dmem_g], ffn1_norm=d_ffn1, ffn2_norm=d_ffn2, mix_norm=d_mix)
    return dx, small, loss_part
```

```python
import functools
import math

import jax
import jax.numpy as jnp
from jax import lax
from jax.experimental import pallas as pl
from jax.experimental.pallas import tpu as pltpu

F32 = jnp.float32
BF16 = jnp.bfloat16

D_MODEL = 1024
SEQ = 2048
B_LOC = 2
N_TOK = B_LOC * SEQ
MEM_LEN = 256
N_DEV = 8
EPS = 1e-6
D_FF = 2816
HG_HEADS = 8
HG_DIM = 128
HG_CHUNK = 64
HG_NCHUNK = SEQ // HG_CHUNK
GM_CHUNK = 128
GM_GROUPS = 8
GM_WIDTH = 2048
GM_GDIM = GM_WIDTH // GM_GROUPS
XA_HEADS = 4
XA_DIM = 256
XA_OFF = 4096

ADAM_LR = 0.001
ADAM_B1 = 0.9
ADAM_B2 = 0.999
ADAM_EPS = 1e-08
ADAM_WD = 0.01
ADAM_STEP = 10

VMEM_LIMIT_BYTES = 56 * 1024 * 1024
MESH_AXES = ("x", "y", "c")

GROUPS = (
    ("ffn1_w_in", True, 2, 704),
    ("ffn1_w_out", False, 2, 352),
    ("mem_w_kv", True, 2, 256),
    ("hgrn_w_in", True, 1, 640),
    ("hgrn_w_out", False, 1, 256),
    ("gmlp_w_in", True, 1, 640),
    ("gmlp_w_out", False, 1, 384),
    ("ffn2_w_in", True, 2, 704),
    ("ffn2_w_out", False, 2, 352),
)
GROUP_LAYERS = {name: layers for name, _, layers, _ in GROUPS}


def _stage_pieces(layer, block):
    if block == "mix":
        mixer = "hgrn" if layer == 0 else "gmlp"
        return (("mem_w_kv", layer), (f"{mixer}_w_in", 0), (f"{mixer}_w_out", 0))
    return ((f"{block}_w_in", layer), (f"{block}_w_out", layer))


ANY_SPEC = pl.BlockSpec(memory_space=pl.ANY)
HBM_SPEC = pl.BlockSpec(memory_space=pltpu.HBM)
SEM_SPEC = pl.BlockSpec(memory_space=pltpu.SEMAPHORE)


def _cp(*sem):
    return pltpu.CompilerParams(dimension_semantics=sem, vmem_limit_bytes=VMEM_LIMIT_BYTES)


def _sigmoid(x):
    return 0.5 * jnp.tanh(0.5 * x) + 0.5


def _gelu_parts(x):
    cdf = 0.5 * (1.0 + lax.erf(x * (1.0 / math.sqrt(2.0))))
    pdf = jnp.exp(-0.5 * x * x) * (1.0 / math.sqrt(2.0 * math.pi))
    return x * cdf, cdf + x * pdf


def _mm(a, b, *, ta=False, tb=False, tm, tn, tk, out_dtype, res=None, scale=1.0, deps=(), name):
    m, k = (a.shape[1], a.shape[0]) if ta else a.shape
    n, kb = b.shape if tb else (b.shape[1], b.shape[0])
    assert k == kb and m % tm == 0 and n % tn == 0 and k % tk == 0, (name, a.shape, b.shape)
    nk = k // tk
    dn = (((0 if ta else 1,), (1 if tb else 0,)), ((), ()))
    n_in = 2 + (res is not None) + len(deps)

    def body(*refs):
        a_ref, b_ref = refs[:2]
        r_ref = refs[2] if res is not None else None
        o_ref, scr = refs[n_in], refs[n_in + 1:]
        p = lax.dot_general(a_ref[...].astype(BF16), b_ref[...].astype(BF16), dn, preferred_element_type=F32)

        def finish(acc):
            if scale != 1.0:
                acc = scale * acc
            if r_ref is not None:
                acc = r_ref[...] + acc
            o_ref[...] = acc.astype(out_dtype)

        if nk == 1:
            finish(p)
        else:
            acc_ref = scr[0]
            kk = pl.program_id(2)

            @pl.when(kk == 0)
            def _():
                acc_ref[...] = p

            @pl.when(kk > 0)
            def _():
                acc_ref[...] += p

            @pl.when(kk == nk - 1)
            def _():
                finish(acc_ref[...])

    a_spec = pl.BlockSpec((tk, tm), lambda i, j, kk: (kk, i)) if ta else pl.BlockSpec((tm, tk), lambda i, j, kk: (i, kk))
    b_mode = dict(pipeline_mode=pl.Buffered(1)) if n == tn and nk == 1 else {}
    if tb:
        b_spec = pl.BlockSpec((tn, tk), lambda i, j, kk: (j, kk), **b_mode)
    else:
        b_spec = pl.BlockSpec((tk, tn), lambda i, j, kk: (kk, j), **b_mode)
    o_spec = pl.BlockSpec((tm, tn), lambda i, j, kk: (i, j))
    in_specs = [a_spec, b_spec] + ([o_spec] if res is not None else []) + [ANY_SPEC] * len(deps)
    args = (a, b) + ((res,) if res is not None else ()) + tuple(deps)
    return pl.pallas_call(
        body,
        name=name,
        grid=(m // tm, n // tn, nk),
        in_specs=in_specs,
        out_specs=o_spec,
        out_shape=jax.ShapeDtypeStruct((m, n), out_dtype),
        scratch_shapes=[pltpu.VMEM((tm, tn), F32)] if nk > 1 else [],
        compiler_params=_cp("parallel", "parallel", "arbitrary"),
    )(*args)


def _rms_fwd(x, g, *, name, deps=(), tm=512):
    rows = x.shape[0]

    def body(x_ref, g_ref, *rest):
        o_ref = rest[len(deps)]
        xv = x_ref[...]
        r = lax.rsqrt(jnp.mean(xv * xv, axis=-1, keepdims=True) + EPS)
        o_ref[...] = (xv * r * g_ref[...]).astype(BF16)

    row = pl.BlockSpec((tm, D_MODEL), lambda i: (i, 0))
    return pl.pallas_call(
        body,
        name=name,
        grid=(rows // tm,),
        in_specs=[row, pl.BlockSpec((1, D_MODEL), lambda i: (0, 0))] + [ANY_SPEC] * len(deps),
        out_specs=row,
        out_shape=jax.ShapeDtypeStruct((rows, D_MODEL), BF16),
        compiler_params=_cp("parallel"),
    )(x, g, *deps)


def _rms_bwd(x, g, dh, dres, *, name, deps=(), tm=512):
    rows = x.shape[0]

    def body(x_ref, g_ref, dh_ref, dres_ref, *rest):
        dx_ref, dx16_ref, dg_ref = rest[len(deps):]
        xv = x_ref[...]
        r = lax.rsqrt(jnp.mean(xv * xv, axis=-1, keepdims=True) + EPS)
        xhat = xv * r
        dhv = dh_ref[...]
        part = jnp.sum(dhv * xhat, axis=0, keepdims=True)

        @pl.when(pl.program_id(0) == 0)
        def _():
            dg_ref[...] = part

        @pl.when(pl.program_id(0) > 0)
        def _():
            dg_ref[...] += part

        dxh = dhv * g_ref[...]
        dx = dres_ref[...] + r * (dxh - xhat * jnp.mean(dxh * xhat, axis=-1, keepdims=True))
        dx_ref[...] = dx
        dx16_ref[...] = dx.astype(BF16)

    row = pl.BlockSpec((tm, D_MODEL), lambda i: (i, 0))
    vec = pl.BlockSpec((1, D_MODEL), lambda i: (0, 0))
    return pl.pallas_call(
        body,
        name=name,
        grid=(rows // tm,),
        in_specs=[row, vec, row, row] + [ANY_SPEC] * len(deps),
        out_specs=[row, row, vec],
        out_shape=[jax.ShapeDtypeStruct((rows, D_MODEL), F32), jax.ShapeDtypeStruct((rows, D_MODEL), BF16),
                   jax.ShapeDtypeStruct((1, D_MODEL), F32)],
        compiler_params=_cp("arbitrary"),
    )(x, g, dh, dres, *deps)


_NT = (((1,), (1,)), ((), ()))
_TN = (((0,), (0,)), ((), ()))


def _norm_mm(x, g, w_t, *, swiglu, name, deps=(), tm=1024):
    rows = w_t.shape[0]
    half = rows // 2
    tn = 256 if swiglu else 512
    nj = (half if swiglu else rows) // tn
    nw = 2 if swiglu else 1
    nd = len(deps)

    def body(x_ref, g_ref, *rest):
        w_refs, outs = rest[:nw], rest[nw + nd:]
        h_ref, z_ref = outs[:2]

        @pl.when(pl.program_id(1) == 0)
        def _():
            xv = x_ref[...]
            r = lax.rsqrt(jnp.mean(xv * xv, axis=-1, keepdims=True) + EPS)
            h_ref[...] = (xv * r * g_ref[...]).astype(BF16)

        h = h_ref[...]
        if swiglu:
            gate = lax.dot_general(h, w_refs[0][...], _NT, preferred_element_type=F32)
            up = lax.dot_general(h, w_refs[1][...], _NT, preferred_element_type=F32)
            z_ref[0] = gate.astype(BF16)
            z_ref[1] = up.astype(BF16)
            outs[2][...] = (gate * _sigmoid(gate) * up).astype(BF16)
        else:
            z_ref[...] = lax.dot_general(h, w_refs[0][...], _NT, preferred_element_type=F32)

    row = pl.BlockSpec((tm, D_MODEL), lambda i, j: (i, 0))
    w_specs = [pl.BlockSpec((tn, D_MODEL), lambda i, j: (j, 0))]
    out_specs = [row]
    out_shape = [jax.ShapeDtypeStruct((N_TOK, D_MODEL), BF16)]
    if swiglu:
        w_specs.append(pl.BlockSpec((tn, D_MODEL), lambda i, j: (j + nj, 0)))
        out_specs += [pl.BlockSpec((2, tm, tn), lambda i, j: (0, i, j)), pl.BlockSpec((tm, tn), lambda i, j: (i, j))]
        out_shape += [jax.ShapeDtypeStruct((2, N_TOK, half), BF16), jax.ShapeDtypeStruct((N_TOK, half), BF16)]
    else:
        out_specs.append(pl.BlockSpec((tm, tn), lambda i, j: (i, j)))
        out_shape.append(jax.ShapeDtypeStruct((N_TOK, rows), F32))
    return pl.pallas_call(
        body,
        name=name,
        grid=(N_TOK // tm, nj),
        in_specs=[row, pl.BlockSpec((1, D_MODEL), lambda i, j: (0, 0))] + w_specs + [ANY_SPEC] * nd,
        out_specs=out_specs,
        out_shape=out_shape,
        compiler_params=_cp("parallel", "arbitrary"),
    )(x, g, *([w_t] * nw), *deps)


def _swiglu_dgrad(dy16, w_out, z, *, scale, name, tm=1024, tn=256):
    def body(dy_ref, w_ref, z_ref, dz_ref):
        da = lax.dot_general(dy_ref[...], w_ref[...], _NT, preferred_element_type=F32) * scale
        gate, up = z_ref[0].astype(F32), z_ref[1].astype(F32)
        s = _sigmoid(gate)
        dz_ref[0] = (da * up * (s * (1.0 + gate * (1.0 - s)))).astype(BF16)
        dz_ref[1] = (da * (gate * s)).astype(BF16)

    planes = pl.BlockSpec((2, tm, tn), lambda i, j: (0, i, j))
    return pl.pallas_call(
        body,
        name=name,
        grid=(N_TOK // tm, D_FF // tn),
        in_specs=[pl.BlockSpec((tm, D_MODEL), lambda i, j: (i, 0)), pl.BlockSpec((tn, D_MODEL), lambda i, j: (j, 0)), planes],
        out_specs=planes,
        out_shape=jax.ShapeDtypeStruct((2, N_TOK, D_FF), BF16),
        compiler_params=_cp("parallel", "parallel"),
    )(dy16, w_out, z)


def _planes_wgrad(dz, h, *, name, tm=1408):
    per_plane = D_FF // tm

    def body(a_ref, b_ref, o_ref):
        o_ref[...] = lax.dot_general(a_ref[...], b_ref[...], _TN, preferred_element_type=F32).astype(BF16)

    return pl.pallas_call(
        body,
        name=name,
        grid=(2 * per_plane,),
        in_specs=[pl.BlockSpec((None, N_TOK, tm),
                               lambda i: (jnp.where(i < per_plane, 0, 1), 0, jnp.where(i < per_plane, i, i - per_plane))),
                  pl.BlockSpec((N_TOK, D_MODEL), lambda i: (0, 0), pipeline_mode=pl.Buffered(1))],
        out_specs=pl.BlockSpec((tm, D_MODEL), lambda i: (i, 0)),
        out_shape=jax.ShapeDtypeStruct((2 * D_FF, D_MODEL), BF16),
        compiler_params=_cp("parallel"),
    )(dz, h)


def _dgrad_norm_bwd(dz, w_t, x, g, dres, *, name, deps=(), tm=512):
    planes = dz.ndim == 3
    rows = w_t.shape[0]
    half = rows // 2
    nd = len(deps)

    def body(a_ref, b_ref, x_ref, g_ref, dres_ref, *rest):
        dx_ref, dx16_ref, dg_ref = rest[nd:]
        if planes:
            dh = jnp.dot(a_ref[0], b_ref[:half, :], preferred_element_type=F32) + jnp.dot(
                a_ref[1], b_ref[half:, :], preferred_element_type=F32)
        else:
            dh = jnp.dot(a_ref[...], b_ref[...], preferred_element_type=F32)
        xv = x_ref[...]
        r = lax.rsqrt(jnp.mean(xv * xv, axis=-1, keepdims=True) + EPS)
        xhat = xv * r
        part = jnp.sum(dh * xhat, axis=0, keepdims=True)

        @pl.when(pl.program_id(0) == 0)
        def _():
            dg_ref[...] = part

        @pl.when(pl.program_id(0) > 0)
        def _():
            dg_ref[...] += part

        dxh = dh * g_ref[...]
        dx = dres_ref[...] + r * (dxh - xhat * jnp.mean(dxh * xhat, axis=-1, keepdims=True))
        dx_ref[...] = dx
        dx16_ref[...] = dx.astype(BF16)

    a_spec = pl.BlockSpec((2, tm, half), lambda i: (0, i, 0)) if planes else pl.BlockSpec((tm, rows), lambda i: (i, 0))
    row = pl.BlockSpec((tm, D_MODEL), lambda i: (i, 0))
    vec = pl.BlockSpec((1, D_MODEL), lambda i: (0, 0))
    return pl.pallas_call(
        body,
        name=name,
        grid=(N_TOK // tm,),
        in_specs=[a_spec, pl.BlockSpec((rows, D_MODEL), lambda i: (0, 0), pipeline_mode=pl.Buffered(1)), row, vec, row]
        + [ANY_SPEC] * nd,
        out_specs=[row, row, vec],
        out_shape=[jax.ShapeDtypeStruct((N_TOK, D_MODEL), F32), jax.ShapeDtypeStruct((N_TOK, D_MODEL), BF16),
                   jax.ShapeDtypeStruct((1, D_MODEL), F32)],
        compiler_params=_cp("arbitrary"),
    )(dz, w_t, x, g, dres, *deps)


def _loss_head(x, g, target, *, tm=512):
    def body(x_ref, g_ref, t_ref, dx_ref, dx16_ref, dg_ref, loss_ref):
        xv = x_ref[...]
        gv = g_ref[...]
        r = lax.rsqrt(jnp.mean(xv * xv, axis=-1, keepdims=True) + EPS)
        xhat = xv * r
        err = xhat * gv - t_ref[...]
        loss_part = jnp.zeros((1, 128), F32) + 0.5 * jnp.sum(jnp.mean(err * err, axis=-1, keepdims=True))
        dy = err * (1.0 / D_MODEL)
        dg_part = jnp.sum(dy * xhat, axis=0, keepdims=True)

        @pl.when(pl.program_id(0) == 0)
        def _():
            dg_ref[...] = dg_part
            loss_ref[...] = loss_part

        @pl.when(pl.program_id(0) > 0)
        def _():
            dg_ref[...] += dg_part
            loss_ref[...] += loss_part

        dxh = dy * gv
        dx = r * (dxh - xhat * jnp.mean(dxh * xhat, axis=-1, keepdims=True))
        dx_ref[...] = dx
        dx16_ref[...] = dx.astype(BF16)

    row = pl.BlockSpec((tm, D_MODEL), lambda i: (i, 0))
    vec = pl.BlockSpec((1, D_MODEL), lambda i: (0, 0))
    return pl.pallas_call(
        body,
        name="loss_head",
        grid=(N_TOK // tm,),
        in_specs=[row, vec, row],
        out_specs=[row, row, vec, pl.BlockSpec((1, 128), lambda i: (0, 0))],
        out_shape=[
            jax.ShapeDtypeStruct((N_TOK, D_MODEL), F32),
            jax.ShapeDtypeStruct((N_TOK, D_MODEL), BF16),
            jax.ShapeDtypeStruct((1, D_MODEL), F32),
            jax.ShapeDtypeStruct((1, 128), F32),
        ],
        compiler_params=_cp("arbitrary"),
    )(x, g, target)


XA_TQ = 1024
XA_SCALE = XA_DIM ** -0.5


def _attn_probs(q16, k16):
    s = lax.dot_general(q16, k16, _NT, preferred_element_type=F32) * XA_SCALE
    e = jnp.exp(s - jnp.max(s, axis=-1, keepdims=True))
    return e / jnp.sum(e, axis=-1, keepdims=True)


def _attn_fwd(z, kv, *, name):
    nt = SEQ // XA_TQ

    def body(q_ref, k_ref, v_ref, o_ref):
        p = _attn_probs(q_ref[...].astype(BF16), k_ref[...].astype(BF16))
        o_ref[...] = jnp.dot(p.astype(BF16), v_ref[...].astype(BF16), preferred_element_type=F32).astype(BF16)

    return pl.pallas_call(
        body,
        name=name,
        grid=(B_LOC, XA_HEADS, nt),
        in_specs=[
            pl.BlockSpec((XA_TQ, XA_DIM), lambda b, h, t: (b * nt + t, XA_OFF // XA_DIM + h)),
            pl.BlockSpec((MEM_LEN, XA_DIM), lambda b, h, t: (b, h)),
            pl.BlockSpec((MEM_LEN, XA_DIM), lambda b, h, t: (b, XA_HEADS + h)),
        ],
        out_specs=pl.BlockSpec((XA_TQ, XA_DIM), lambda b, h, t: (b * nt + t, h)),
        out_shape=jax.ShapeDtypeStruct((N_TOK, XA_HEADS * XA_DIM), BF16),
        compiler_params=_cp("parallel", "parallel", "arbitrary"),
    )(z, kv, kv)


def _attn_bwd(z, kv, dcat, *, do_off, name):
    nt = SEQ // XA_TQ

    def body(q_ref, k_ref, v_ref, do_ref, dq_ref, dk_ref, dv_ref):
        q16 = q_ref[...].astype(BF16)
        k16 = k_ref[...].astype(BF16)
        v16 = v_ref[...].astype(BF16)
        do16 = do_ref[...].astype(BF16)
        p = _attn_probs(q16, k16)
        dv_part = lax.dot_general(p.astype(BF16), do16, _TN, preferred_element_type=F32)
        dp = lax.dot_general(do16, v16, _NT, preferred_element_type=F32)
        ds16 = (p * (dp - jnp.sum(dp * p, axis=-1, keepdims=True)) * XA_SCALE).astype(BF16)
        dq_ref[...] = jnp.dot(ds16, k16, preferred_element_type=F32).astype(BF16)
        dk_part = lax.dot_general(ds16, q16, _TN, preferred_element_type=F32)

        @pl.when(pl.program_id(2) == 0)
        def _():
            dk_ref[...] = dk_part
            dv_ref[...] = dv_part

        @pl.when(pl.program_id(2) > 0)
        def _():
            dk_ref[...] += dk_part
            dv_ref[...] += dv_part

    qspec = pl.BlockSpec((XA_TQ, XA_DIM), lambda b, h, t: (b * nt + t, XA_OFF // XA_DIM + h))
    kspec = lambda off: pl.BlockSpec((MEM_LEN, XA_DIM), lambda b, h, t: (b, off + h))
    return pl.pallas_call(
        body,
        name=name,
        grid=(B_LOC, XA_HEADS, nt),
        in_specs=[qspec, kspec(0), kspec(XA_HEADS),
                  pl.BlockSpec((XA_TQ, XA_DIM), lambda b, h, t: (b * nt + t, do_off // XA_DIM + h))],
        out_specs=[pl.BlockSpec((XA_TQ, XA_DIM), lambda b, h, t: (b * nt + t, h)), kspec(0), kspec(0)],
        out_shape=[
            jax.ShapeDtypeStruct((N_TOK, XA_HEADS * XA_DIM), BF16),
            jax.ShapeDtypeStruct((B_LOC * MEM_LEN, XA_HEADS * XA_DIM), F32),
            jax.ShapeDtypeStruct((B_LOC * MEM_LEN, XA_HEADS * XA_DIM), F32),
        ],
        compiler_params=_cp("parallel", "parallel", "arbitrary"),
    )(z, kv, kv, dcat)


def _tril(n):
    return lax.broadcasted_iota(jnp.int32, (n, n), 0) >= lax.broadcasted_iota(jnp.int32, (n, n), 1)


def _lower_bound(lbl):
    e = jnp.exp(lbl - jnp.max(lbl, axis=0, keepdims=True))
    p = e / jnp.sum(e, axis=0, keepdims=True)
    return p[0:1, :], p


def _hgrn_gates(zq, zf, lb, tril_f):
    sig = _sigmoid(zf)
    f = lb + (1.0 - lb) * sig
    kk = 1.0 - f
    sq = _sigmoid(zq)
    q = zq * sq
    b = jnp.dot(tril_f, jnp.log(f), preferred_element_type=F32, precision=lax.Precision.HIGHEST)
    bl = b[HG_CHUNK - 1:HG_CHUNK, :]
    return q, sq, sig, f, kk, b, bl


HG_TB = 512
HG_CPB = HG_TB // HG_CHUNK
HG_NT = SEQ // HG_TB
HG_WIDTH = HG_HEADS * HG_DIM


def _head(h, section=0):
    return slice(section * HG_WIDTH + h * HG_DIM, section * HG_WIDTH + (h + 1) * HG_DIM)


def _hgrn_fwd(z, o_mem, lb_logits, gnorm):
    def body(zq_ref, zf_ref, zi_ref, zg_ref, omem_ref, lbl_ref, gn_ref, o_ref, opre_ref, sall_ref, st_ref):
        lb, _ = _lower_bound(lbl_ref[...])
        gn = gn_ref[...]
        mask = _tril(HG_CHUNK)
        tril_f = mask.astype(F32)
        o_ref[:, HG_WIDTH:] = omem_ref[...]

        @pl.when(pl.program_id(1) == 0)
        def _():
            st_ref[...] = jnp.zeros_like(st_ref)

        def chunk(c, carry):
            rows = pl.ds(pl.multiple_of(c * HG_CHUNK, HG_CHUNK), HG_CHUNK)
            q, _, _, _, kk, b, bl = _hgrn_gates(zq_ref[rows, :], zf_ref[rows, :], lb, tril_f)
            v16 = zi_ref[rows, :].astype(BF16)
            qd16 = (q * jnp.exp(b)).astype(BF16)
            ki16 = (kk * jnp.exp(-b)).astype(BF16)
            kd16 = (kk * jnp.exp(bl - b)).astype(BF16)
            ebl = jnp.exp(bl)
            zg = zg_ref[rows, :]
            gate = zg * _sigmoid(zg)
            for h in range(HG_HEADS):
                sl = _head(h)
                a = jnp.where(mask, lax.dot_general(qd16[:, sl], ki16[:, sl], _NT, preferred_element_type=F32), 0.0)
                st = st_ref[h]
                sall_ref[0, h, c] = st
                o = jnp.dot(a.astype(BF16), v16[:, sl], preferred_element_type=F32) + lax.dot_general(
                    qd16[:, sl], st.astype(BF16), _NT, preferred_element_type=F32)
                st_ref[h] = st * ebl[:, sl] + lax.dot_general(v16[:, sl], kd16[:, sl], _TN, preferred_element_type=F32)
                opre_ref[rows, sl] = o
                r = lax.rsqrt(jnp.mean(o * o, axis=-1, keepdims=True) + EPS)
                o_ref[rows, sl] = ((o * r * gn) * gate[:, sl]).astype(BF16)
            return carry

        lax.fori_loop(0, HG_CPB, chunk, 0)

    zspec = lambda s: pl.BlockSpec((HG_TB, HG_WIDTH), lambda b, t: (b * HG_NT + t, s))
    return pl.pallas_call(
        body,
        name="hgrn_fwd",
        grid=(B_LOC, HG_NT),
        in_specs=[zspec(0), zspec(1), zspec(2), zspec(3), zspec(0),
                  pl.BlockSpec((3, HG_WIDTH), lambda b, t: (0, 0)), pl.BlockSpec((1, HG_DIM), lambda b, t: (0, 0))],
        out_specs=[pl.BlockSpec((HG_TB, 2 * HG_WIDTH), lambda b, t: (b * HG_NT + t, 0)), zspec(0),
                   pl.BlockSpec((1, HG_HEADS, HG_CPB, HG_DIM, HG_DIM), lambda b, t: (b, 0, t, 0, 0))],
        out_shape=[
            jax.ShapeDtypeStruct((N_TOK, 2 * HG_WIDTH), BF16),
            jax.ShapeDtypeStruct((N_TOK, HG_WIDTH), F32),
            jax.ShapeDtypeStruct((B_LOC, HG_HEADS, HG_NCHUNK, HG_DIM, HG_DIM), F32),
        ],
        scratch_shapes=[pltpu.VMEM((HG_HEADS, HG_DIM, HG_DIM), F32)],
        compiler_params=_cp("parallel", "arbitrary"),
    )(z, z, z, z, o_mem, lb_logits, gnorm)


def _hgrn_bwd(z, opre, dcat, dq_mem, sall, lb_logits, gnorm):
    def body(zq_ref, zf_ref, zi_ref, zg_ref, opre_ref, dout_ref, dqm_ref, sall_ref, lbl_ref, gn_ref,
             dz_ref, dlbl_ref, dgn_ref, dst_ref, dlb_ref, dgn_acc, db_ref, dkk_ref, dbl_ref):
        b_id, t_id = pl.program_id(0), pl.program_id(1)
        lb, p = _lower_bound(lbl_ref[...])
        gn = gn_ref[...]
        mask = _tril(HG_CHUNK)
        tril_f = mask.astype(F32)
        dz_ref[:, 4 * HG_WIDTH:] = dqm_ref[...]

        @pl.when(t_id == 0)
        def _():
            dst_ref[...] = jnp.zeros_like(dst_ref)
            dlb_ref[...] = jnp.zeros_like(dlb_ref)

        @pl.when((b_id == 0) & (t_id == 0))
        def _():
            dgn_acc[...] = jnp.zeros_like(dgn_acc)

        def chunk(i, carry):
            c = HG_CPB - 1 - i
            rows = pl.ds(pl.multiple_of(c * HG_CHUNK, HG_CHUNK), HG_CHUNK)
            zq, zg = zq_ref[rows, :], zg_ref[rows, :]
            q, sq, sig, f, kk, b, bl = _hgrn_gates(zq, zf_ref[rows, :], lb, tril_f)
            v16 = zi_ref[rows, :].astype(BF16)
            eb, enb, ebl_b, ebl = jnp.exp(b), jnp.exp(-b), jnp.exp(bl - b), jnp.exp(bl)
            qd, ki, kd = q * eb, kk * enb, kk * ebl_b
            qd16, ki16, kd16 = qd.astype(BF16), ki.astype(BF16), kd.astype(BF16)
            o_all = opre_ref[rows, :]
            dout = dout_ref[rows, :]
            sg = _sigmoid(zg)
            d_on_all = dout * (zg * sg)
            dgate = dout * (sg * (1.0 + zg * (1.0 - sg)))
            dq_scale = eb * (sq * (1.0 + zq * (1.0 - sq)))
            for h in range(HG_HEADS):
                sl = _head(h)
                o = o_all[:, sl]
                r = lax.rsqrt(jnp.mean(o * o, axis=-1, keepdims=True) + EPS)
                ohat = o * r
                d_on = d_on_all[:, sl]
                dz_ref[rows, _head(h, 3)] = (dgate[:, sl] * (ohat * gn)).astype(BF16)
                dgn_acc[...] += jnp.sum(d_on * ohat, axis=0, keepdims=True)
                dohat = d_on * gn
                do16 = (r * (dohat - ohat * jnp.mean(dohat * ohat, axis=-1, keepdims=True))).astype(BF16)
                st = sall_ref[0, h, c]
                dst = dst_ref[h]
                st16, dst16 = st.astype(BF16), dst.astype(BF16)
                qd_h, ki_h, kd_h, v_h = qd16[:, sl], ki16[:, sl], kd16[:, sl], v16[:, sl]
                a16 = jnp.where(mask, lax.dot_general(qd_h, ki_h, _NT, preferred_element_type=F32), 0.0).astype(BF16)
                da16 = jnp.where(mask, lax.dot_general(do16, v_h, _NT, preferred_element_type=F32), 0.0).astype(BF16)
                dv = lax.dot_general(a16, do16, _TN, preferred_element_type=F32) + lax.dot_general(
                    kd_h, dst16, _NT, preferred_element_type=F32)
                dqd = jnp.dot(da16, ki_h, preferred_element_type=F32) + jnp.dot(do16, st16, preferred_element_type=F32)
                dki = lax.dot_general(da16, qd_h, _TN, preferred_element_type=F32)
                dkd = jnp.dot(v_h, dst16, preferred_element_type=F32)
                dbl_ref[:, sl] = jnp.sum(dkd * kd[:, sl], axis=0, keepdims=True) + ebl[:, sl] * jnp.sum(
                    st * dst, axis=0, keepdims=True)
                dst_ref[h] = dst * ebl[:, sl] + lax.dot_general(do16, qd_h, _TN, preferred_element_type=F32)
                dz_ref[rows, _head(h, 2)] = dv.astype(BF16)
                dz_ref[rows, sl] = (dqd * dq_scale[:, sl]).astype(BF16)
                dkk_ref[:, sl] = dki * enb[:, sl] + dkd * ebl_b[:, sl]
                db_ref[:, sl] = dqd * qd[:, sl] - dki * ki[:, sl] - dkd * kd[:, sl]
            dlogf = lax.dot_general(tril_f, db_ref[...], _TN, preferred_element_type=F32,
                                    precision=lax.Precision.HIGHEST) + dbl_ref[...]
            df = dlogf / f - dkk_ref[...]
            dz_ref[rows, HG_WIDTH:2 * HG_WIDTH] = (df * (1.0 - lb) * sig * (1.0 - sig)).astype(BF16)
            dlb_ref[...] += jnp.sum(df * (1.0 - sig), axis=0, keepdims=True)
            return carry

        lax.fori_loop(0, HG_CPB, chunk, 0)

        @pl.when(t_id == HG_NT - 1)
        def _():
            row0 = (lax.broadcasted_iota(jnp.int32, (3, HG_WIDTH), 0) == 0).astype(F32)
            dlbl_part = dlb_ref[...] * lb * (row0 - p)

            @pl.when(b_id == 0)
            def _():
                dlbl_ref[...] = dlbl_part

            @pl.when(b_id > 0)
            def _():
                dlbl_ref[...] += dlbl_part

            dgn_ref[...] = dgn_acc[...]

    rev = lambda b, t: b * HG_NT + HG_NT - 1 - t
    zspec = lambda s: pl.BlockSpec((HG_TB, HG_WIDTH), lambda b, t: (rev(b, t), s))
    return pl.pallas_call(
        body,
        name="hgrn_bwd",
        grid=(B_LOC, HG_NT),
        in_specs=[zspec(0), zspec(1), zspec(2), zspec(3), zspec(0), zspec(0), zspec(0),
                  pl.BlockSpec((1, HG_HEADS, HG_CPB, HG_DIM, HG_DIM), lambda b, t: (b, 0, HG_NT - 1 - t, 0, 0)),
                  pl.BlockSpec((3, HG_WIDTH), lambda b, t: (0, 0)), pl.BlockSpec((1, HG_DIM), lambda b, t: (0, 0))],
        out_specs=[pl.BlockSpec((HG_TB, 5 * HG_WIDTH), lambda b, t: (rev(b, t), 0)),
                   pl.BlockSpec((3, HG_WIDTH), lambda b, t: (0, 0)), pl.BlockSpec((1, HG_DIM), lambda b, t: (0, 0))],
        out_shape=[jax.ShapeDtypeStruct((N_TOK, 5 * HG_WIDTH), BF16),
                   jax.ShapeDtypeStruct((3, HG_WIDTH), F32), jax.ShapeDtypeStruct((1, HG_DIM), F32)],
        scratch_shapes=[pltpu.VMEM((HG_HEADS, HG_DIM, HG_DIM), F32), pltpu.VMEM((1, HG_WIDTH), F32),
                        pltpu.VMEM((1, HG_DIM), F32), pltpu.VMEM((HG_CHUNK, HG_WIDTH), F32),
                        pltpu.VMEM((HG_CHUNK, HG_WIDTH), F32), pltpu.VMEM((1, HG_WIDTH), F32)],
        compiler_params=_cp("arbitrary", "arbitrary"),
    )(z, z, z, z, opre, dcat, dq_mem, sall, lb_logits, gnorm)


GM_TM = 256


def _gmlp_norm(zv, ln_g, ln_b):
    gv, dgelu = _gelu_parts(zv)
    xc = gv - jnp.mean(gv, axis=-1, keepdims=True)
    rstd = lax.rsqrt(jnp.mean(xc * xc, axis=-1, keepdims=True) + EPS)
    vhat = xc * rstd
    return vhat * ln_g + ln_b, vhat, rstd, dgelu


def _gmlp_specs():
    half = lambda j: pl.BlockSpec((GM_TM, GM_WIDTH), lambda i: (i, j))
    vec = pl.BlockSpec((1, GM_WIDTH), lambda i: (0, 0))
    w = pl.BlockSpec((GM_GROUPS, GM_CHUNK, GM_CHUNK), lambda i: (0, 0, 0))
    bt = pl.BlockSpec((GM_CHUNK, GM_GROUPS), lambda i: (0, 0))
    return half, vec, w, bt


def _gmlp_fwd(z, o_mem, ln_g, ln_b, w_s, b_st):
    def body(zu_ref, zv_ref, omem_ref, g_ref, b_ref, w_ref, bt_ref, o_ref):
        o_ref[:, GM_WIDTH:] = omem_ref[...]
        u, _ = _gelu_parts(zu_ref[...])
        v, _, _, _ = _gmlp_norm(zv_ref[...], g_ref[...], b_ref[...])
        v16 = v.astype(BF16)
        mask = _tril(GM_CHUNK)
        bt = bt_ref[...]
        for g in range(GM_GROUPS):
            wm16 = jnp.where(mask, w_ref[g], 0.0).astype(BF16)
            cols = slice(g * GM_GDIM, (g + 1) * GM_GDIM)
            for c in range(GM_TM // GM_CHUNK):
                rows = slice(c * GM_CHUNK, (c + 1) * GM_CHUNK)
                mixed = jnp.dot(wm16, v16[rows, cols], preferred_element_type=F32) + bt[:, g:g + 1]
                o_ref[rows, cols] = (u[rows, cols] * mixed).astype(BF16)

    half, vec, w, bt = _gmlp_specs()
    return pl.pallas_call(
        body,
        name="gmlp_fwd",
        grid=(N_TOK // GM_TM,),
        in_specs=[half(0), half(1), pl.BlockSpec((GM_TM, XA_HEADS * XA_DIM), lambda i: (i, 0)), vec, vec, w, bt],
        out_specs=pl.BlockSpec((GM_TM, GM_WIDTH + XA_HEADS * XA_DIM), lambda i: (i, 0)),
        out_shape=jax.ShapeDtypeStruct((N_TOK, GM_WIDTH + XA_HEADS * XA_DIM), BF16),
        compiler_params=_cp("parallel"),
    )(z, z, o_mem, ln_g, ln_b, w_s, b_st)


def _gmlp_bwd(z, dcat, dq_mem, ln_g, ln_b, w_s, b_st):
    def body(zu_ref, zv_ref, dout_ref, dqm_ref, g_ref, b_ref, w_ref, bt_ref,
             dz_ref, dw_ref, dbt_ref, dg_ref, db_ref, dv_ref):
        dz_ref[:, 2 * GM_WIDTH:] = dqm_ref[...]
        @pl.when(pl.program_id(0) == 0)
        def _():
            dw_ref[...] = jnp.zeros_like(dw_ref)
            dbt_ref[...] = jnp.zeros_like(dbt_ref)
            dg_ref[...] = jnp.zeros_like(dg_ref)
            db_ref[...] = jnp.zeros_like(db_ref)

        zu = zu_ref[...]
        u, du_dz = _gelu_parts(zu)
        ln_g = g_ref[...]
        v, vhat, rstd, dgv_dz = _gmlp_norm(zv_ref[...], ln_g, b_ref[...])
        v16 = v.astype(BF16)
        dout = dout_ref[...]
        dmixed = dout * u
        dm16 = dmixed.astype(BF16)
        mask = _tril(GM_CHUNK)
        bt = bt_ref[...]
        group_id = lax.broadcasted_iota(jnp.int32, (1, GM_GROUPS), 1)
        dbt = jnp.zeros((GM_CHUNK, GM_GROUPS), F32)
        for g in range(GM_GROUPS):
            wm16 = jnp.where(mask, w_ref[g], 0.0).astype(BF16)
            cols = slice(g * GM_GDIM, (g + 1) * GM_GDIM)
            dw = jnp.zeros((GM_CHUNK, GM_CHUNK), F32)
            dbt_g = jnp.zeros((GM_CHUNK, 1), F32)
            for c in range(GM_TM // GM_CHUNK):
                rows = slice(c * GM_CHUNK, (c + 1) * GM_CHUNK)
                mixed = jnp.dot(wm16, v16[rows, cols], preferred_element_type=F32) + bt[:, g:g + 1]
                dz_ref[rows, cols] = (dout[rows, cols] * mixed * du_dz[rows, cols]).astype(BF16)
                dw += lax.dot_general(dm16[rows, cols], v16[rows, cols], _NT, preferred_element_type=F32)
                dbt_g += jnp.sum(dmixed[rows, cols], axis=-1, keepdims=True)
                dv_ref[rows, cols] = lax.dot_general(wm16, dm16[rows, cols], _TN, preferred_element_type=F32)
            dw_ref[g] += jnp.where(mask, dw, 0.0)
            dbt = dbt + dbt_g * (group_id == g).astype(F32)
        dbt_ref[...] += dbt
        dv = dv_ref[...]
        dg_ref[...] += jnp.sum(dv * vhat, axis=0, keepdims=True)
        db_ref[...] += jnp.sum(dv, axis=0, keepdims=True)
        dvh = dv * ln_g
        dgv = rstd * (dvh - jnp.mean(dvh, axis=-1, keepdims=True) - vhat * jnp.mean(dvh * vhat, axis=-1, keepdims=True))
        dz_ref[:, GM_WIDTH:2 * GM_WIDTH] = (dgv * dgv_dz).astype(BF16)

    half, vec, w, bt = _gmlp_specs()
    dz_width = 2 * GM_WIDTH + XA_HEADS * XA_DIM
    return pl.pallas_call(
        body,
        name="gmlp_bwd",
        grid=(N_TOK // GM_TM,),
        in_specs=[half(0), half(1), half(0), pl.BlockSpec((GM_TM, XA_HEADS * XA_DIM), lambda i: (i, 0)), vec, vec, w, bt],
        out_specs=[pl.BlockSpec((GM_TM, dz_width), lambda i: (i, 0)), w, bt, vec, vec],
        out_shape=[jax.ShapeDtypeStruct((N_TOK, dz_width), BF16),
                   jax.ShapeDtypeStruct((GM_GROUPS, GM_CHUNK, GM_CHUNK), F32),
                   jax.ShapeDtypeStruct((GM_CHUNK, GM_GROUPS), F32),
                   jax.ShapeDtypeStruct((1, GM_WIDTH), F32), jax.ShapeDtypeStruct((1, GM_WIDTH), F32)],
        scratch_shapes=[pltpu.VMEM((GM_TM, GM_WIDTH), F32)],
        compiler_params=_cp("arbitrary"),
    )(z, z, dcat, dq_mem, ln_g, ln_b, w_s, b_st)


def _own_slot(shape):
    return pl.BlockSpec((None,) + tuple(shape), lambda i, me_ref: (me_ref[0],) + (0,) * len(shape))


def _place_rows(w, layer, cuts_columns, me, *, name):
    _, r, c = w.shape
    n = c if cuts_columns else r

    def body(me_ref, w_ref, o_ref):
        wv = w_ref[...]
        o_ref[...] = (wv.T if cuts_columns else wv).astype(BF16)

    return pl.pallas_call(
        body,
        name=name,
        grid_spec=pltpu.PrefetchScalarGridSpec(
            num_scalar_prefetch=1, grid=(1,),
            in_specs=[pl.BlockSpec((None, r, c), lambda i, me_ref: (layer, 0, 0))],
            out_specs=_own_slot((n, D_MODEL))),
        out_shape=jax.ShapeDtypeStruct((N_DEV, n, D_MODEL), BF16),
        compiler_params=_cp("arbitrary"),
    )(me, w)


def _place_ln(ln_g, ln_b, me):
    blk = ln_g.shape[1]

    def body(me_ref, g_ref, b_ref, o_ref):
        o_ref[...] = jnp.zeros_like(o_ref)
        o_ref[0:1, :] = g_ref[...]
        o_ref[1:2, :] = b_ref[...]

    vec = pl.BlockSpec((1, blk), lambda i, me_ref: (0, 0))
    return pl.pallas_call(
        body,
        name="place_ln",
        grid_spec=pltpu.PrefetchScalarGridSpec(
            num_scalar_prefetch=1, grid=(1,), in_specs=[vec, vec], out_specs=_own_slot((8, blk))),
        out_shape=jax.ShapeDtypeStruct((N_DEV, 8, blk), F32),
        compiler_params=_cp("arbitrary"),
    )(me, ln_g, ln_b)


def _place_slab(a, me, *, name):
    def body(me_ref, a_ref, o_ref):
        o_ref[...] = a_ref[...]

    return pl.pallas_call(
        body,
        name=name,
        grid_spec=pltpu.PrefetchScalarGridSpec(
            num_scalar_prefetch=1, grid=(1,),
            in_specs=[pl.BlockSpec(a.shape, lambda i, me_ref: (0, 0))], out_specs=_own_slot(a.shape)),
        out_shape=jax.ShapeDtypeStruct((N_DEV,) + a.shape, a.dtype),
        compiler_params=_cp("arbitrary"),
    )(me, a)


def _place_own(grads, me, *, name):
    k = len(grads)

    def body(me_ref, *refs):
        for src, dst in zip(refs[:k], refs[k:]):
            dst[...] = src[...]

    specs = [_own_slot(g.shape[1:]) for g in grads]
    return pl.pallas_call(
        body,
        name=name,
        grid_spec=pltpu.PrefetchScalarGridSpec(num_scalar_prefetch=1, grid=(1,), in_specs=specs, out_specs=specs),
        out_shape=[jax.ShapeDtypeStruct(g.shape, g.dtype) for g in grads],
        compiler_params=_cp("arbitrary"),
    )(me, *grads)


def _mesh_pos():
    x, y, c = (lax.axis_index(a) for a in MESH_AXES)
    return x, y, c, 4 * x + 2 * y + c


def _peer(x, y, c, r):
    px = 1 - x if r & 4 else x
    py = 1 - y if r & 2 else y
    pc = 1 - c if r & 1 else c
    return (px, py, pc), 4 * px + 2 * py + pc


RELATIONS = {"scatter": (1, 2, 3, 4, 5, 6, 7), "gather_all": (1, 2, 3, 4, 5, 6, 7), "gather_chips": (1, 2, 4, 6),
             "gather_sibling": (2, 4, 6)}


def _peer_copies(srcs, lands, send_sems, recv_sems, mode, waits):
    x, y, c, me = _mesh_pos()
    rel = RELATIONS[mode]
    pairs = []
    for ri, r in enumerate(rel):
        if mode == "gather_sibling":
            peer, _ = _peer(x, y, c, 1)
            _, sent_blk = _peer(x, y, c, r)
            _, got_blk = _peer(x, y, c, r ^ 1)
        else:
            peer, peer_blk = _peer(x, y, c, r)
            sent_blk, got_blk = (peer_blk if mode == "scatter" else me), peer_blk
        for k, (src, land) in enumerate(zip(srcs, lands)):
            idx = k * len(rel) + ri
            sems = dict(send_sem=send_sems.at[idx], recv_sem=recv_sems.at[idx], device_id=peer,
                        device_id_type=pl.DeviceIdType.MESH)
            dst_blk = sent_blk if mode == "gather_sibling" else me
            mine = pltpu.make_async_remote_copy(src_ref=src.at[sent_blk], dst_ref=land.at[dst_blk], **sems)
            theirs = pltpu.make_async_remote_copy(src_ref=src.at[sent_blk], dst_ref=land.at[got_blk], **sems) if waits else None
            pairs.append((mine, theirs))
    return pairs


DATAFLOW = pltpu.SideEffectType.DATAFLOW_SIDE_EFFECTING


def _in_hbm(a):
    return pltpu.with_memory_space_constraint(a, pltpu.HBM)


def _copies_start(srcs, lands, *, mode, name, deps=()):
    gather = mode != "scatter"
    arrs = list(lands) if gather else list(srcs) + list(lands)
    n, k, nd = len(arrs), len(lands), len(deps)

    def body(*refs):
        ins, send_sems, recv_sems, token = refs[:n], refs[n + nd], refs[n + nd + 1], refs[2 * n + nd + 2]
        src_refs, land_refs = (ins, ins) if gather else (ins[:k], ins[k:])
        for mine, _ in _peer_copies(src_refs, land_refs, send_sems, recv_sems, mode, waits=False):
            mine.start()
        token[...] = jnp.zeros_like(token)

    n_cp = k * len(RELATIONS[mode])
    return pl.pallas_call(
        body,
        name=name,
        in_specs=[HBM_SPEC] * n + [ANY_SPEC] * nd,
        out_specs=(SEM_SPEC, SEM_SPEC, *[HBM_SPEC] * n, pl.BlockSpec(memory_space=pltpu.VMEM)),
        out_shape=(pltpu.SemaphoreType.DMA((n_cp,)), pltpu.SemaphoreType.DMA((n_cp,)),
                   *[pltpu.HBM(a.shape, a.dtype) for a in arrs], jax.ShapeDtypeStruct((8, 128), F32)),
        input_output_aliases={i: 2 + i for i in range(n)},
        compiler_params=pltpu.CompilerParams(has_side_effects=DATAFLOW),
    )(*[_in_hbm(a) for a in arrs], *deps)


def _copies_wait(arrs, send_sems, recv_sems, after, *, n_lands, mode, name):
    n, k = len(arrs), n_lands
    gather = mode != "scatter"

    def body(*refs):
        ins, send_sems, recv_sems = refs[:n], refs[n], refs[n + 1]
        src_refs, land_refs = (ins, ins) if gather else (ins[:k], ins[k:])
        for mine, theirs in _peer_copies(src_refs, land_refs, send_sems, recv_sems, mode, waits=True):
            mine.wait_send()
            theirs.wait_recv()

    outs = pl.pallas_call(
        body,
        name=name,
        in_specs=[HBM_SPEC] * n + [SEM_SPEC, SEM_SPEC] + [ANY_SPEC] * len(after),
        out_specs=[HBM_SPEC] * n,
        out_shape=[pltpu.HBM(a.shape, a.dtype) for a in arrs],
        input_output_aliases={i: i for i in range(n)},
        compiler_params=pltpu.CompilerParams(has_side_effects=DATAFLOW),
    )(*arrs, send_sems, recv_sems, *after)
    return outs[n - k:]


def _exchange_small(slabs):
    n = len(slabs)

    def body(*refs):
        ins, outs = refs[:n], refs[n:2 * n]
        send_sems, recv_sems, local_sems = refs[2 * n:]
        x, y, c, me = _mesh_pos()
        own = [pltpu.make_async_copy(src, dst.at[me], local_sems.at[k]) for k, (src, dst) in enumerate(zip(ins, outs))]
        for cp in own:
            cp.start()
        sends, recvs = [], []
        for r in range(1, N_DEV):
            peer, peer_blk = _peer(x, y, c, r)
            for k, (src, dst) in enumerate(zip(ins, outs)):
                idx = k * (N_DEV - 1) + r - 1
                sems = dict(send_sem=send_sems.at[idx], recv_sem=recv_sems.at[idx], device_id=peer,
                            device_id_type=pl.DeviceIdType.MESH)
                send = pltpu.make_async_remote_copy(src_ref=src, dst_ref=dst.at[me], **sems)
                send.start()
                sends.append(send)
                recvs.append(pltpu.make_async_remote_copy(src_ref=src, dst_ref=dst.at[peer_blk], **sems))
        for cp in recvs:
            cp.wait_recv()
        for cp in sends:
            cp.wait_send()
        for cp in own:
            cp.wait()

    n_cp = n * (N_DEV - 1)
    return pl.pallas_call(
        body,
        name="exchange_small_grads",
        in_specs=[ANY_SPEC] * n,
        out_specs=[ANY_SPEC] * n,
        out_shape=[jax.ShapeDtypeStruct((N_DEV,) + s.shape, F32) for s in slabs],
        scratch_shapes=[pltpu.SemaphoreType.DMA((n_cp,)), pltpu.SemaphoreType.DMA((n_cp,)),
                        pltpu.SemaphoreType.DMA((n,))],
    )(*slabs)


def _adamw(w, g, m, v):
    m = ADAM_B1 * m + (1.0 - ADAM_B1) * g
    v = ADAM_B2 * v + (1.0 - ADAM_B2) * (g * g)
    m_hat = m / (1.0 - ADAM_B1 ** ADAM_STEP)
    v_hat = v / (1.0 - ADAM_B2 ** ADAM_STEP)
    return -ADAM_LR * (m_hat / (jnp.sqrt(v_hat) + ADAM_EPS) + ADAM_WD * w), m, v


ADAM_TC = 256


def _adam_big(slots, w, m, v, cuts_columns, *, name):
    layers, n, nj = len(slots), slots[0].shape[1], D_MODEL // ADAM_TC

    def body(*refs):
        s_refs = refs[:layers]
        w_ref, m_ref, v_ref, g_ref, d_ref, nm_ref, nv_ref, acc_ref = refs[layers:]
        for ll in range(layers):
            @pl.when(pl.program_id(0) == ll)
            def _(s_ref=s_refs[ll]):
                g = s_ref[0].astype(F32)
                for s in range(1, N_DEV):
                    g = g + s_ref[s].astype(F32)
                acc_ref[...] = g

        g = acc_ref[...].T if cuts_columns else acc_ref[...]
        g_ref[...] = g
        d_ref[...], nm_ref[...], nv_ref[...] = _adamw(w_ref[...], g, m_ref[...], v_ref[...])

    def slot_spec(ll):
        return pl.BlockSpec((N_DEV, n, ADAM_TC),
                            lambda l, j: (0, 0, jnp.where(l < ll, 0, jnp.where(l > ll, nj - 1, j))))

    if cuts_columns:
        w_spec = pl.BlockSpec((None, ADAM_TC, n), lambda l, j: (l, j, 0))
    else:
        w_spec = pl.BlockSpec((None, n, ADAM_TC), lambda l, j: (l, 0, j))
    return pl.pallas_call(
        body,
        name=name,
        grid=(layers, nj),
        in_specs=[slot_spec(ll) for ll in range(layers)] + [w_spec] * 3,
        out_specs=[w_spec] * 4,
        out_shape=[jax.ShapeDtypeStruct(w.shape, F32)] * 4,
        scratch_shapes=[pltpu.VMEM((n, ADAM_TC), F32)],
        compiler_params=_cp("arbitrary", "arbitrary"),
    )(*slots, w, m, v)


def _adam_slabs(slots, ws, ms, vs):
    n = len(slots)

    def body(*refs):
        ins, outs = refs[:4 * n], refs[4 * n:]
        for k in range(n):
            s_ref, w_ref, m_ref, v_ref = ins[k], ins[n + k], ins[2 * n + k], ins[3 * n + k]
            g = s_ref[0]
            for s in range(1, N_DEV):
                g = g + s_ref[s]
            outs[4 * k][...] = g
            outs[4 * k + 1][...], outs[4 * k + 2][...], outs[4 * k + 3][...] = _adamw(w_ref[...], g, m_ref[...], v_ref[...])

    res = pl.pallas_call(
        body,
        name="small_adamw",
        out_shape=[jax.ShapeDtypeStruct(w.shape, F32) for w in ws for _ in range(4)],
        compiler_params=pltpu.CompilerParams(vmem_limit_bytes=VMEM_LIMIT_BYTES),
    )(*slots, *ws, *ms, *vs)
    return [res[4 * k:4 * k + 4] for k in range(n)]


def _adam_vecs(gs, ws, ms, vs):
    n = len(gs)

    def body(*refs):
        ins, outs = refs[:4 * n], refs[4 * n:]
        for k in range(n):
            outs[3 * k][...], outs[3 * k + 1][...], outs[3 * k + 2][...] = _adamw(
                ins[n + k][...], ins[k][...], ins[2 * n + k][...], ins[3 * n + k][...])

    res = pl.pallas_call(
        body,
        name="ln_adamw",
        out_shape=[jax.ShapeDtypeStruct(w.shape, F32) for w in ws for _ in range(3)],
        compiler_params=pltpu.CompilerParams(vmem_limit_bytes=VMEM_LIMIT_BYTES),
    )(*gs, *ws, *ms, *vs)
    return [res[3 * k:3 * k + 3] for k in range(n)]


SLAB_AT = dict(mem_norm=0, lb_logits=1, ffn1_norm=4, mix_norm=6, hgrn_gnorm=8, gmlp_ln_g=9, gmlp_ln_b=11,
               gmlp_b_s=13, ffn2_norm=14, final_norm=16)
SLAB_ROWS = 24
SMALL_SHARDED = ("gmlp_ln_g", "gmlp_ln_b")


def _pack_slab(parts, *, name):
    flat, plan = [], []
    for pname, at in SLAB_AT.items():
        for a in parts.get(pname, ()):
            flat.append(a)
            plan.append((at, a.shape))
            at += max(1, a.shape[0] * a.shape[1] // D_MODEL)

    def body(*refs):
        o_ref = refs[-1]
        o_ref[...] = jnp.zeros_like(o_ref)
        for ref, (at, (r, w)) in zip(refs, plan):
            if w == D_MODEL or r == 1 and w < D_MODEL:
                o_ref[at:at + r, 0:w] = ref[...]
            elif w < D_MODEL:
                for j in range(r):
                    o_ref[at:at + 1, j * w:(j + 1) * w] = ref[j:j + 1, :]
            else:
                for j in range(w // D_MODEL):
                    o_ref[at + j:at + j + 1, :] = ref[:, j * D_MODEL:(j + 1) * D_MODEL]

    return pl.pallas_call(
        body,
        name=name,
        out_shape=jax.ShapeDtypeStruct((SLAB_ROWS, D_MODEL), F32),
        compiler_params=pltpu.CompilerParams(vmem_limit_bytes=VMEM_LIMIT_BYTES),
    )(*flat)


def _unpack_slab(slab, shapes):
    out = {}
    for pname, at in SLAB_AT.items():
        if pname in SMALL_SHARDED:
            continue
        size = math.prod(shapes[pname])
        rows = max(1, size // D_MODEL)
        out[pname] = slab[at:at + rows].reshape(-1)[:size].reshape(shapes[pname])
    return out


def _ffn_fwd(x, norm_g, block, layer, full, get_weights):
    tag = f"l{layer}_{block}"
    full.update(get_weights((layer, f"{block}_in"), (x,)))
    h, z, act = _norm_mm(x, norm_g, full[(f"{block}_w_in", layer)], swiglu=True, deps=full.pop("deps", ()),
                         name=f"{tag}_in")
    full.update(get_weights((layer, f"{block}_out"), (act,)))
    y = _mm(act, full[(f"{block}_w_out", layer)], tm=512, tn=D_MODEL, tk=D_FF, out_dtype=F32, res=x, scale=0.5,
            deps=full.pop("deps", ()), name=f"{tag}_out")
    return y, (x, h, z, act)


def _ffn_bwd(dy, dy16, saved, norm_g, w_in_t, w_out, tag, deps=()):
    x, h, z, act = saved
    dw_out = _mm(act, dy16, ta=True, tm=1408, tn=D_MODEL, tk=N_TOK, out_dtype=BF16, scale=0.5, deps=deps,
                 name=f"{tag}_out_wgrad")
    dz = _swiglu_dgrad(dy16, w_out, z, scale=0.5, name=f"{tag}_out_dgrad")
    dw_in_t = _planes_wgrad(dz, h, name=f"{tag}_in_wgrad")
    dx, dx16, dg = _dgrad_norm_bwd(dz, w_in_t, x, norm_g, dy, name=f"{tag}_in_dgrad")
    return dx, dx16, dg, dw_in_t, dw_out


def kernel(x, mem, mem_norm, lb_logits, ffn1_norm, ffn1_w_in, ffn1_w_out, mix_norm, mem_w_kv, hgrn_w_in, hgrn_gnorm, hgrn_w_out, gmlp_w_in, gmlp_ln_g, gmlp_ln_b, gmlp_w_s, gmlp_b_s, gmlp_w_out, ffn2_norm, ffn2_w_in, ffn2_w_out, final_norm, loss_target, m_mem_norm, m_lb_logits, m_ffn1_norm, m_ffn1_w_in, m_ffn1_w_out, m_mix_norm, m_mem_w_kv, m_hgrn_w_in, m_hgrn_gnorm, m_hgrn_w_out, m_gmlp_w_in, m_gmlp_ln_g, m_gmlp_ln_b, m_gmlp_w_s, m_gmlp_b_s, m_gmlp_w_out, m_ffn2_norm, m_ffn2_w_in, m_ffn2_w_out, m_final_norm, v_mem_norm, v_lb_logits, v_ffn1_norm, v_ffn1_w_in, v_ffn1_w_out, v_mix_norm, v_mem_w_kv, v_hgrn_w_in, v_hgrn_gnorm, v_hgrn_w_out, v_gmlp_w_in, v_gmlp_ln_g, v_gmlp_ln_b, v_gmlp_w_s, v_gmlp_b_s, v_gmlp_w_out, v_ffn2_norm, v_ffn2_w_in, v_ffn2_w_out, v_final_norm):
    weights = dict(mem_norm=mem_norm, lb_logits=lb_logits, ffn1_norm=ffn1_norm, ffn1_w_in=ffn1_w_in, ffn1_w_out=ffn1_w_out, mix_norm=mix_norm, mem_w_kv=mem_w_kv, hgrn_w_in=hgrn_w_in, hgrn_gnorm=hgrn_gnorm, hgrn_w_out=hgrn_w_out, gmlp_w_in=gmlp_w_in, gmlp_ln_g=gmlp_ln_g, gmlp_ln_b=gmlp_ln_b, gmlp_w_s=gmlp_w_s, gmlp_b_s=gmlp_b_s, gmlp_w_out=gmlp_w_out, ffn2_norm=ffn2_norm, ffn2_w_in=ffn2_w_in, ffn2_w_out=ffn2_w_out, final_norm=final_norm)
    mom_m = dict(mem_norm=m_mem_norm, lb_logits=m_lb_logits, ffn1_norm=m_ffn1_norm, ffn1_w_in=m_ffn1_w_in, ffn1_w_out=m_ffn1_w_out, mix_norm=m_mix_norm, mem_w_kv=m_mem_w_kv, hgrn_w_in=m_hgrn_w_in, hgrn_gnorm=m_hgrn_gnorm, hgrn_w_out=m_hgrn_w_out, gmlp_w_in=m_gmlp_w_in, gmlp_ln_g=m_gmlp_ln_g, gmlp_ln_b=m_gmlp_ln_b, gmlp_w_s=m_gmlp_w_s, gmlp_b_s=m_gmlp_b_s, gmlp_w_out=m_gmlp_w_out, ffn2_norm=m_ffn2_norm, ffn2_w_in=m_ffn2_w_in, ffn2_w_out=m_ffn2_w_out, final_norm=m_final_norm)
    mom_v = dict(mem_norm=v_mem_norm, lb_logits=v_lb_logits, ffn1_norm=v_ffn1_norm, ffn1_w_in=v_ffn1_w_in, ffn1_w_out=v_ffn1_w_out, mix_norm=v_mix_norm, mem_w_kv=v_mem_w_kv, hgrn_w_in=v_hgrn_w_in, hgrn_gnorm=v_hgrn_gnorm, hgrn_w_out=v_hgrn_w_out, gmlp_w_in=v_gmlp_w_in, gmlp_ln_g=v_gmlp_ln_g, gmlp_ln_b=v_gmlp_ln_b, gmlp_w_s=v_gmlp_w_s, gmlp_b_s=v_gmlp_b_s, gmlp_w_out=v_gmlp_w_out, ffn2_norm=v_ffn2_norm, ffn2_w_in=v_ffn2_w_in, ffn2_w_out=v_ffn2_w_out, final_norm=v_final_norm)
    order = list(weights)
    _, _, _, me = _mesh_pos()
    me_arr = jnp.reshape(me, (1,)).astype(jnp.int32)
    cuts = {name: c for name, c, _, _ in GROUPS}

    mix1 = (("mem_w_kv", 1), ("gmlp_w_in", 0), ("gmlp_w_out", 0))
    gather_plan = (
        ((0, "ffn1_in"), (("ffn1_w_in", 0),)),
        ((0, "ffn1_out"), (("ffn1_w_out", 0),)),
        ((0, "mix_in"), _stage_pieces(0, "mix")),
        ((0, "ffn2_in"), _stage_pieces(0, "ffn2")),
        ((1, "ffn1_in"), _stage_pieces(1, "ffn1")),
        ((1, "mix_in"), mix1),
        ((1, "ffn2_in"), _stage_pieces(1, "ffn2")),
    )
    stage_of = {use: k for k, (use, _) in enumerate(gather_plan)}
    in_flight = {}

    def start_chips(k, deps):
        pieces = gather_plan[k][1]
        lands = [_place_rows(weights[name], l, cuts[name], me_arr, name=f"place_{name}_{l}") for name, l in pieces]
        if pieces is mix1:
            lands.append(_place_ln(gmlp_ln_g, gmlp_ln_b, me_arr))
        send_sems, recv_sems, *thru, token = _copies_start(lands, lands, mode="gather_chips", deps=deps,
                                                           name=f"gather{k}_chips_start")
        in_flight[k] = (thru, send_sems, recv_sems)
        return token

    def pass_to_sibling(k, after):
        thru, send_sems, recv_sems = in_flight[k]
        outs = _copies_wait(thru, send_sems, recv_sems, after, n_lands=len(thru), mode="gather_chips",
                            name=f"gather{k}_chips_wait")
        send_sems, recv_sems, *thru, token = _copies_start(outs, outs, mode="gather_sibling",
                                                           name=f"gather{k}_sibling_start")
        in_flight[k] = (thru, send_sems, recv_sems)
        return token, token

    start_chips(0, ())

    def get_weights(use, after):
        if use not in stage_of:
            return {}
        k = stage_of[use]
        tokens = []
        if k == 0:
            token, landed = pass_to_sibling(0, after)
            tokens += [token, start_chips(1, (landed,))]
        thru, send_sems, recv_sems = in_flight[k]
        outs = _copies_wait(thru, send_sems, recv_sems, after, n_lands=len(thru), mode="gather_sibling",
                            name=f"gather{k}_sibling_wait")
        if k + 1 < len(gather_plan):
            token, landed = pass_to_sibling(k + 1, (outs[0],))
            tokens.append(token)
            if k + 2 < len(gather_plan):
                tokens.append(start_chips(k + 2, (landed,)))
        pieces = gather_plan[k][1]
        w = {p: o.reshape(N_DEV * o.shape[1], D_MODEL) for p, o in zip(pieces, outs)}
        w["deps"] = tuple(tokens)
        if pieces is mix1:
            w["ln_g"] = outs[-1][:, 0, :].reshape(1, GM_WIDTH)
            w["ln_b"] = outs[-1][:, 1, :].reshape(1, GM_WIDTH)
        return w

    scatter = {}

    def put_grads(st, grads):
        if st == "w_s":
            land = _place_slab(grads.reshape(GM_GROUPS * GM_CHUNK, GM_CHUNK), me_arr, name="w_s_place")
            send_sems, recv_sems, *thru, token = _copies_start([land], [land], mode="gather_all", name="w_s_start")
            scatter[st] = (thru, send_sems, recv_sems)
            return (token,)
        views = [grads[p].reshape(N_DEV, -1, D_MODEL) for p in _stage_pieces(*st)]
        recv = _place_own(views, me_arr, name=f"scatter_place_l{st[0]}_{st[1]}")
        send_sems, recv_sems, *thru, token = _copies_start(views, recv, mode="scatter",
                                                           name=f"scatter_start_l{st[0]}_{st[1]}")
        scatter[st] = (thru, send_sems, recv_sems)
        return (token,)

    dx, small, loss_part = _step_local(
        x, mem, loss_target, get_weights, put_grads, mem_norm, lb_logits, ffn1_norm, mix_norm, hgrn_gnorm,
        gmlp_w_s, gmlp_b_s, ffn2_norm, final_norm)

    def slots_of(blk, after):
        slots = {}
        for i in (1, 0):
            thru, send_sems, recv_sems = scatter[(i, blk)]
            outs = _copies_wait(thru, send_sems, recv_sems, after, n_lands=len(thru) // 2, mode="scatter",
                                name=f"scatter_wait_l{i}_{blk}")
            slots.update(zip(_stage_pieces(i, blk), outs))
        return slots

    grad, delta, new_m, new_v = {}, {}, {}, {}

    def adam_groups(slots, names):
        for name in names:
            layers = GROUP_LAYERS[name]
            grad[name], delta[name], new_m[name], new_v[name] = _adam_big(
                [slots[(name, l)] for l in range(layers)], weights[name], mom_m[name], mom_v[name], cuts[name],
                name=f"{name}_adamw")

    adam_groups(slots_of("ffn2", (dx,)), ("ffn2_w_in", "ffn2_w_out"))
    adam_groups(slots_of("mix", (delta["ffn2_w_out"],)),
                ("mem_w_kv", "gmlp_w_in", "gmlp_w_out", "hgrn_w_in", "hgrn_w_out"))

    def small_parts(src):
        parts = {n: [src[n].reshape(-1, src[n].shape[-1])] for n in SLAB_AT if n not in SMALL_SHARDED}
        return parts

    w_s_rows = lambda a: a.reshape(GM_GROUPS * GM_CHUNK, GM_CHUNK)
    (slab_slots,) = _exchange_small([_pack_slab(small, name="pack_small_grads")])
    thru, send_sems, recv_sems = scatter["w_s"]
    (ws_slots,) = _copies_wait(thru, send_sems, recv_sems, (slab_slots,), n_lands=1, mode="gather_all", name="w_s_wait")
    (g_slab, d_slab, nm_slab, nv_slab), (g_ws, d_ws, nm_ws, nv_ws) = _adam_slabs(
        [slab_slots, ws_slots],
        [_pack_slab(small_parts(weights), name="pack_small_w"), w_s_rows(gmlp_w_s)],
        [_pack_slab(small_parts(mom_m), name="pack_small_m"), w_s_rows(m_gmlp_w_s)],
        [_pack_slab(small_parts(mom_v), name="pack_small_v"), w_s_rows(v_gmlp_w_s)])
    shapes = {n: weights[n].shape for n in SLAB_AT}
    for out, slab, ws in ((grad, g_slab, g_ws), (delta, d_slab, d_ws), (new_m, nm_slab, nm_ws), (new_v, nv_slab, nv_ws)):
        out.update(_unpack_slab(slab, shapes))
        out["gmlp_w_s"] = ws.reshape(gmlp_w_s.shape)
    blk = GM_WIDTH // N_DEV
    g_ln = [lax.dynamic_slice(g_slab[SLAB_AT[n]:SLAB_AT[n] + 2].reshape(1, GM_WIDTH), (0, me * blk), (1, blk))
            for n in SMALL_SHARDED]
    ln_out = _adam_vecs(g_ln, [weights[n] for n in SMALL_SHARDED], [mom_m[n] for n in SMALL_SHARDED],
                        [mom_v[n] for n in SMALL_SHARDED])
    for n, g, (d, nm, nv) in zip(SMALL_SHARDED, g_ln, ln_out):
        grad[n], delta[n], new_m[n], new_v[n] = g, d, nm, nv

    adam_groups(slots_of("ffn1", (delta["hgrn_w_out"], d_slab)), ("ffn1_w_in", "ffn1_w_out"))

    loss = lax.psum(loss_part[0, 0], MESH_AXES)
    grad_x = dx.reshape(B_LOC, SEQ, D_MODEL)
    return (loss, grad_x, *[grad[n] for n in order], *[delta[n] for n in order],
            *[new_m[n] for n in order], *[new_v[n] for n in order])


def _step_local(x, mem, loss_target, get_weights, put_grads, mem_norm, lb_logits, ffn1_norm, mix_norm, hgrn_gnorm,
                gmlp_w_s, gmlp_b_s, ffn2_norm, final_norm):
    w_s = gmlp_w_s[0]
    b_st = gmlp_b_s[0].T

    xs = x.reshape(N_TOK, D_MODEL)
    mem2d = mem.reshape(B_LOC * MEM_LEN, D_MODEL)
    mem_g = mem_norm.reshape(1, D_MODEL)
    saved, full = [], {}
    memn = _rms_fwd(mem2d, mem_g, name="mem_norm_fwd")
    for i in range(2):
        xs, s_ffn1 = _ffn_fwd(xs, ffn1_norm[i:i + 1], "ffn1", i, full, get_weights)
        full.update(get_weights((i, "mix_in"), (xs,)))
        mixer = "hgrn" if i == 0 else "gmlp"
        hm, zm = _norm_mm(xs, mix_norm[i:i + 1], full[(f"{mixer}_w_in", 0)], swiglu=False, deps=full.pop("deps", ()),
                          name=f"l{i}_mix_in")
        kv = _mm(memn, full[("mem_w_kv", i)], tb=True, tm=512, tn=512, tk=D_MODEL, out_dtype=F32, name=f"l{i}_mem_kv")
        o_mem = _attn_fwd(zm, kv, name=f"l{i}_attn")
        if i == 0:
            cat, o_pre, s_all = _hgrn_fwd(zm, o_mem, lb_logits, hgrn_gnorm)
            mix_saved = (o_pre, s_all)
        else:
            cat = _gmlp_fwd(zm, o_mem, full["ln_g"], full["ln_b"], w_s, b_st)
            mix_saved = ()
        x_mix = xs
        full.update(get_weights((i, "mix_out"), (cat,)))
        xs = _mm(cat, full[(f"{mixer}_w_out", 0)], tm=512, tn=D_MODEL, tk=cat.shape[1], out_dtype=F32, res=xs,
                 deps=full.pop("deps", ()), name=f"l{i}_mix_out")
        xs, s_ffn2 = _ffn_fwd(xs, ffn2_norm[i:i + 1], "ffn2", i, full, get_weights)
        saved.append((s_ffn1, (x_mix, hm, kv, zm, cat, mix_saved), s_ffn2))

    dx, dx16, d_final, loss_part = _loss_head(xs, final_norm.reshape(1, D_MODEL), loss_target.reshape(N_TOK, D_MODEL))

    small = {"final_norm": [d_final]}
    d_ffn1, d_ffn2, d_mix = [None, None], [None, None], [None, None]
    dmemn = jnp.zeros((B_LOC * MEM_LEN, D_MODEL), F32)
    deps = ()
    for i in (1, 0):
        s_ffn1, (x_mix, hm, kv, zm, cat, mix_saved), s_ffn2 = saved[i]
        dx, dx16, d_ffn2[i], dw_in_t, dw_out = _ffn_bwd(
            dx, dx16, s_ffn2, ffn2_norm[i:i + 1], full[("ffn2_w_in", i)], full[("ffn2_w_out", i)], f"l{i}_ffn2", deps)
        deps = put_grads((i, "ffn2"), {("ffn2_w_in", i): dw_in_t, ("ffn2_w_out", i): dw_out})
        mixer = "hgrn" if i == 0 else "gmlp"
        w_in_t, w_out = full[(f"{mixer}_w_in", 0)], full[(f"{mixer}_w_out", 0)]
        width = cat.shape[1]
        g_mix = {}
        g_mix[(f"{mixer}_w_out", 0)] = _mm(cat, dx16, ta=True, tm=1024, tn=D_MODEL, tk=N_TOK, out_dtype=BF16,
                                           deps=deps, name=f"l{i}_mix_out_wgrad")
        dcat = _mm(dx16, w_out, tb=True, tm=1024, tn=width // 2, tk=D_MODEL, out_dtype=F32, name=f"l{i}_mix_out_dgrad")
        dq, dk, dv = _attn_bwd(zm, kv, dcat, do_off=width - XA_HEADS * XA_DIM, name=f"l{i}_attn_bwd")
        if i == 0:
            dzm, dlbl, dgn = _hgrn_bwd(zm, mix_saved[0], dcat, dq, mix_saved[1], lb_logits, hgrn_gnorm)
            small["lb_logits"], small["hgrn_gnorm"] = [dlbl], [dgn]
            deps = ()
        else:
            dzm, dws, dbt, dlng, dlnb = _gmlp_bwd(zm, dcat, dq, full["ln_g"], full["ln_b"], w_s, b_st)
            small["gmlp_b_s"], small["gmlp_ln_g"], small["gmlp_ln_b"] = [dbt.T], [dlng], [dlnb]
            deps = put_grads("w_s", dws)
        g_mix[(f"{mixer}_w_in", 0)] = _mm(dzm, hm, ta=True, tm=1024, tn=D_MODEL, tk=N_TOK, out_dtype=BF16, deps=deps,
                                          name=f"l{i}_mix_in_wgrad")
        dkv = jnp.concatenate([dk, dv], axis=1)
        g_mix[("mem_w_kv", i)] = _mm(dkv, memn, ta=True, tm=512, tn=D_MODEL, tk=B_LOC * MEM_LEN, out_dtype=BF16,
                                     name=f"l{i}_mem_kv_wgrad")
        deps = put_grads((i, "mix"), g_mix)
        dx, dx16, d_mix[i] = _dgrad_norm_bwd(dzm, w_in_t, x_mix, mix_norm[i:i + 1], dx, deps=deps,
                                             name=f"l{i}_mix_in_dgrad")
        dmemn = _mm(dkv, full[("mem_w_kv", i)], tm=B_LOC * MEM_LEN, tn=D_MODEL, tk=512, out_dtype=F32, res=dmemn,
                    name=f"l{i}_mem_kv_dgrad")
        dx, dx16, d_ffn1[i], dw_in_t, dw_out = _ffn_bwd(
            dx, dx16, s_ffn1, ffn1_norm[i:i + 1], full[("ffn1_w_in", i)], full[("ffn1_w_out", i)], f"l{i}_ffn1")
        deps = put_grads((i, "ffn1"), {("ffn1_w_in", i): dw_in_t, ("ffn1_w_out", i): dw_out})
    _, _, dmem_g = _rms_bwd(mem2d, mem_g, dmemn, dmemn, deps=deps, name="mem_norm_bwd")
    small.update(mem_norm=[dmem_g], ffn1_norm=d_ffn1, ffn2_norm=d_ffn2, mix_norm=d_mix)
    return dx, small, loss_part
```

```python
import functools
import math

import jax
import jax.numpy as jnp
from jax import lax
from jax.experimental import pallas as pl
from jax.experimental.pallas import tpu as pltpu

F32 = jnp.float32
BF16 = jnp.bfloat16

D_MODEL = 1024
SEQ = 2048
B_LOC = 2
N_TOK = B_LOC * SEQ
MEM_LEN = 256
N_DEV = 8
EPS = 1e-6
D_FF = 2816
HG_HEADS = 8
HG_DIM = 128
HG_CHUNK = 64
HG_NCHUNK = SEQ // HG_CHUNK
GM_CHUNK = 128
GM_GROUPS = 8
GM_WIDTH = 2048
GM_GDIM = GM_WIDTH // GM_GROUPS
XA_HEADS = 4
XA_DIM = 256
XA_OFF = 4096

ADAM_LR = 0.001
ADAM_B1 = 0.9
ADAM_B2 = 0.999
ADAM_EPS = 1e-08
ADAM_WD = 0.01
ADAM_STEP = 10

VMEM_LIMIT_BYTES = 56 * 1024 * 1024
MESH_AXES = ("x", "y", "c")

GROUPS = (
    ("ffn1_w_in", True, 2, 704),
    ("ffn1_w_out", False, 2, 352),
    ("mem_w_kv", True, 2, 256),
    ("hgrn_w_in", True, 1, 640),
    ("hgrn_w_out", False, 1, 256),
    ("gmlp_w_in", True, 1, 640),
    ("gmlp_w_out", False, 1, 384),
    ("ffn2_w_in", True, 2, 704),
    ("ffn2_w_out", False, 2, 352),
)
GROUP_LAYERS = {name: layers for name, _, layers, _ in GROUPS}


def _stage_pieces(layer, block):
    if block == "mix":
        mixer = "hgrn" if layer == 0 else "gmlp"
        return (("mem_w_kv", layer), (f"{mixer}_w_in", 0), (f"{mixer}_w_out", 0))
    return ((f"{block}_w_in", layer), (f"{block}_w_out", layer))


ANY_SPEC = pl.BlockSpec(memory_space=pl.ANY)
HBM_SPEC = pl.BlockSpec(memory_space=pltpu.HBM)
SEM_SPEC = pl.BlockSpec(memory_space=pltpu.SEMAPHORE)


def _cp(*sem):
    return pltpu.CompilerParams(dimension_semantics=sem, vmem_limit_bytes=VMEM_LIMIT_BYTES)


def _sigmoid(x):
    return 0.5 * jnp.tanh(0.5 * x) + 0.5


def _gelu_parts(x):
    cdf = 0.5 * (1.0 + lax.erf(x * (1.0 / math.sqrt(2.0))))
    pdf = jnp.exp(-0.5 * x * x) * (1.0 / math.sqrt(2.0 * math.pi))
    return x * cdf, cdf + x * pdf


def _mm(a, b, *, ta=False, tb=False, tm, tn, tk, out_dtype, res=None, scale=1.0, deps=(), name):
    m, k = (a.shape[1], a.shape[0]) if ta else a.shape
    n, kb = b.shape if tb else (b.shape[1], b.shape[0])
    assert k == kb and m % tm == 0 and n % tn == 0 and k % tk == 0, (name, a.shape, b.shape)
    nk = k // tk
    dn = (((0 if ta else 1,), (1 if tb else 0,)), ((), ()))
    n_in = 2 + (res is not None) + len(deps)

    def body(*refs):
        a_ref, b_ref = refs[:2]
        r_ref = refs[2] if res is not None else None
        o_ref, scr = refs[n_in], refs[n_in + 1:]
        p = lax.dot_general(a_ref[...].astype(BF16), b_ref[...].astype(BF16), dn, preferred_element_type=F32)

        def finish(acc):
            if scale != 1.0:
                acc = scale * acc
            if r_ref is not None:
                acc = r_ref[...] + acc
            o_ref[...] = acc.astype(out_dtype)

        if nk == 1:
            finish(p)
        else:
            acc_ref = scr[0]
            kk = pl.program_id(2)

            @pl.when(kk == 0)
            def _():
                acc_ref[...] = p

            @pl.when(kk > 0)
            def _():
                acc_ref[...] += p

            @pl.when(kk == nk - 1)
            def _():
                finish(acc_ref[...])

    a_spec = pl.BlockSpec((tk, tm), lambda i, j, kk: (kk, i)) if ta else pl.BlockSpec((tm, tk), lambda i, j, kk: (i, kk))
    b_mode = dict(pipeline_mode=pl.Buffered(1)) if n == tn and nk == 1 else {}
    if tb:
        b_spec = pl.BlockSpec((tn, tk), lambda i, j, kk: (j, kk), **b_mode)
    else:
        b_spec = pl.BlockSpec((tk, tn), lambda i, j, kk: (kk, j), **b_mode)
    o_spec = pl.BlockSpec((tm, tn), lambda i, j, kk: (i, j))
    in_specs = [a_spec, b_spec] + ([o_spec] if res is not None else []) + [ANY_SPEC] * len(deps)
    args = (a, b) + ((res,) if res is not None else ()) + tuple(deps)
    return pl.pallas_call(
        body,
        name=name,
        grid=(m // tm, n // tn, nk),
        in_specs=in_specs,
        out_specs=o_spec,
        out_shape=jax.ShapeDtypeStruct((m, n), out_dtype),
        scratch_shapes=[pltpu.VMEM((tm, tn), F32)] if nk > 1 else [],
        compiler_params=_cp("parallel", "parallel", "arbitrary"),
    )(*args)


def _rms_fwd(x, g, *, name, deps=(), tm=512):
    rows = x.shape[0]

    def body(x_ref, g_ref, *rest):
        o_ref = rest[len(deps)]
        xv = x_ref[...]
        r = lax.rsqrt(jnp.mean(xv * xv, axis=-1, keepdims=True) + EPS)
        o_ref[...] = (xv * r * g_ref[...]).astype(BF16)

    row = pl.BlockSpec((tm, D_MODEL), lambda i: (i, 0))
    return pl.pallas_call(
        body,
        name=name,
        grid=(rows // tm,),
        in_specs=[row, pl.BlockSpec((1, D_MODEL), lambda i: (0, 0))] + [ANY_SPEC] * len(deps),
        out_specs=row,
        out_shape=jax.ShapeDtypeStruct((rows, D_MODEL), BF16),
        compiler_params=_cp("parallel"),
    )(x, g, *deps)


def _rms_bwd(x, g, dh, dres, *, name, deps=(), tm=512):
    rows = x.shape[0]

    def body(x_ref, g_ref, dh_ref, dres_ref, *rest):
        dx_ref, dx16_ref, dg_ref = rest[len(deps):]
        xv = x_ref[...]
        r = lax.rsqrt(jnp.mean(xv * xv, axis=-1, keepdims=True) + EPS)
        xhat = xv * r
        dhv = dh_ref[...]
        part = jnp.sum(dhv * xhat, axis=0, keepdims=True)

        @pl.when(pl.program_id(0) == 0)
        def _():
            dg_ref[...] = part

        @pl.when(pl.program_id(0) > 0)
        def _():
            dg_ref[...] += part

        dxh = dhv * g_ref[...]
        dx = dres_ref[...] + r * (dxh - xhat * jnp.mean(dxh * xhat, axis=-1, keepdims=True))
        dx_ref[...] = dx
        dx16_ref[...] = dx.astype(BF16)

    row = pl.BlockSpec((tm, D_MODEL), lambda i: (i, 0))
    vec = pl.BlockSpec((1, D_MODEL), lambda i: (0, 0))
    return pl.pallas_call(
        body,
        name=name,
        grid=(rows // tm,),
        in_specs=[row, vec, row, row] + [ANY_SPEC] * len(deps),
        out_specs=[row, row, vec],
        out_shape=[jax.ShapeDtypeStruct((rows, D_MODEL), F32), jax.ShapeDtypeStruct((rows, D_MODEL), BF16),
                   jax.ShapeDtypeStruct((1, D_MODEL), F32)],
        compiler_params=_cp("arbitrary"),
    )(x, g, dh, dres, *deps)


_NT = (((1,), (1,)), ((), ()))
_TN = (((0,), (0,)), ((), ()))


def _norm_mm(x, g, w_t, *, swiglu, name, tm, tn, deps=()):
    rows = w_t.shape[0]
    half = rows // 2
    nj = (half if swiglu else rows) // tn
    nw = 2 if swiglu else 1
    nd = len(deps)

    def body(x_ref, g_ref, *rest):
        w_refs, outs = rest[:nw], rest[nw + nd:]
        h_ref, z_ref = outs[:2]

        @pl.when(pl.program_id(1) == 0)
        def _():
            xv = x_ref[...]
            r = lax.rsqrt(jnp.mean(xv * xv, axis=-1, keepdims=True) + EPS)
            h_ref[...] = (xv * r * g_ref[...]).astype(BF16)

        h = h_ref[...]
        if swiglu:
            gate = lax.dot_general(h, w_refs[0][...], _NT, preferred_element_type=F32)
            up = lax.dot_general(h, w_refs[1][...], _NT, preferred_element_type=F32)
            z_ref[0] = gate.astype(BF16)
            z_ref[1] = up.astype(BF16)
            outs[2][...] = (gate * _sigmoid(gate) * up).astype(BF16)
        else:
            z_ref[...] = lax.dot_general(h, w_refs[0][...], _NT, preferred_element_type=F32)

    row = pl.BlockSpec((tm, D_MODEL), lambda i, j: (i, 0))
    w_specs = [pl.BlockSpec((tn, D_MODEL), lambda i, j: (j, 0))]
    out_specs = [row]
    out_shape = [jax.ShapeDtypeStruct((N_TOK, D_MODEL), BF16)]
    if swiglu:
        w_specs.append(pl.BlockSpec((tn, D_MODEL), lambda i, j: (j + nj, 0)))
        out_specs += [pl.BlockSpec((2, tm, tn), lambda i, j: (0, i, j)), pl.BlockSpec((tm, tn), lambda i, j: (i, j))]
        out_shape += [jax.ShapeDtypeStruct((2, N_TOK, half), BF16), jax.ShapeDtypeStruct((N_TOK, half), BF16)]
    else:
        out_specs.append(pl.BlockSpec((tm, tn), lambda i, j: (i, j)))
        out_shape.append(jax.ShapeDtypeStruct((N_TOK, rows), F32))
    return pl.pallas_call(
        body,
        name=name,
        grid=(N_TOK // tm, nj),
        in_specs=[row, pl.BlockSpec((1, D_MODEL), lambda i, j: (0, 0))] + w_specs + [ANY_SPEC] * nd,
        out_specs=out_specs,
        out_shape=out_shape,
        compiler_params=_cp("parallel", "arbitrary"),
    )(x, g, *([w_t] * nw), *deps)


def _swiglu_dgrad(dy16, w_out, z, *, scale, name, tm=512, tn=1408):
    def body(dy_ref, w_ref, z_ref, dz_ref):
        da = lax.dot_general(dy_ref[...], w_ref[...], _NT, preferred_element_type=F32) * scale
        gate, up = z_ref[0].astype(F32), z_ref[1].astype(F32)
        s = _sigmoid(gate)
        dz_ref[0] = (da * up * (s * (1.0 + gate * (1.0 - s)))).astype(BF16)
        dz_ref[1] = (da * (gate * s)).astype(BF16)

    planes = pl.BlockSpec((2, tm, tn), lambda i, j: (0, i, j))
    return pl.pallas_call(
        body,
        name=name,
        grid=(N_TOK // tm, D_FF // tn),
        in_specs=[pl.BlockSpec((tm, D_MODEL), lambda i, j: (i, 0)), pl.BlockSpec((tn, D_MODEL), lambda i, j: (j, 0)), planes],
        out_specs=planes,
        out_shape=jax.ShapeDtypeStruct((2, N_TOK, D_FF), BF16),
        compiler_params=_cp("parallel", "parallel"),
    )(dy16, w_out, z)


def _planes_wgrad(dz, h, *, name, tm=1408):
    per_plane = D_FF // tm

    def body(a_ref, b_ref, o_ref):
        o_ref[...] = lax.dot_general(a_ref[...], b_ref[...], _TN, preferred_element_type=F32).astype(BF16)

    return pl.pallas_call(
        body,
        name=name,
        grid=(2 * per_plane,),
        in_specs=[pl.BlockSpec((None, N_TOK, tm),
                               lambda i: (jnp.where(i < per_plane, 0, 1), 0, jnp.where(i < per_plane, i, i - per_plane))),
                  pl.BlockSpec((N_TOK, D_MODEL), lambda i: (0, 0), pipeline_mode=pl.Buffered(1))],
        out_specs=pl.BlockSpec((tm, D_MODEL), lambda i: (i, 0)),
        out_shape=jax.ShapeDtypeStruct((2 * D_FF, D_MODEL), BF16),
        compiler_params=_cp("parallel"),
    )(dz, h)


def _dgrad_norm_bwd(dz, w_t, x, g, dres, *, name, deps=(), tm=512):
    planes = dz.ndim == 3
    rows = w_t.shape[0]
    half = rows // 2
    nd = len(deps)

    def body(a_ref, b_ref, x_ref, g_ref, dres_ref, *rest):
        dx_ref, dx16_ref, dg_ref = rest[nd:]
        if planes:
            dh = jnp.dot(a_ref[0], b_ref[:half, :], preferred_element_type=F32) + jnp.dot(
                a_ref[1], b_ref[half:, :], preferred_element_type=F32)
        else:
            dh = jnp.dot(a_ref[...], b_ref[...], preferred_element_type=F32)
        xv = x_ref[...]
        r = lax.rsqrt(jnp.mean(xv * xv, axis=-1, keepdims=True) + EPS)
        xhat = xv * r
        part = jnp.sum(dh * xhat, axis=0, keepdims=True)

        @pl.when(pl.program_id(0) == 0)
        def _():
            dg_ref[...] = part

        @pl.when(pl.program_id(0) > 0)
        def _():
            dg_ref[...] += part

        dxh = dh * g_ref[...]
        dx = dres_ref[...] + r * (dxh - xhat * jnp.mean(dxh * xhat, axis=-1, keepdims=True))
        dx_ref[...] = dx
        dx16_ref[...] = dx.astype(BF16)

    a_spec = pl.BlockSpec((2, tm, half), lambda i: (0, i, 0)) if planes else pl.BlockSpec((tm, rows), lambda i: (i, 0))
    row = pl.BlockSpec((tm, D_MODEL), lambda i: (i, 0))
    vec = pl.BlockSpec((1, D_MODEL), lambda i: (0, 0))
    return pl.pallas_call(
        body,
        name=name,
        grid=(N_TOK // tm,),
        in_specs=[a_spec, pl.BlockSpec((rows, D_MODEL), lambda i: (0, 0), pipeline_mode=pl.Buffered(1)), row, vec, row]
        + [ANY_SPEC] * nd,
        out_specs=[row, row, vec],
        out_shape=[jax.ShapeDtypeStruct((N_TOK, D_MODEL), F32), jax.ShapeDtypeStruct((N_TOK, D_MODEL), BF16),
                   jax.ShapeDtypeStruct((1, D_MODEL), F32)],
        compiler_params=_cp("arbitrary"),
    )(dz, w_t, x, g, dres, *deps)


def _loss_head(x, g, target, *, tm=512):
    def body(x_ref, g_ref, t_ref, dx_ref, dx16_ref, dg_ref, loss_ref):
        xv = x_ref[...]
        gv = g_ref[...]
        r = lax.rsqrt(jnp.mean(xv * xv, axis=-1, keepdims=True) + EPS)
        xhat = xv * r
        err = xhat * gv - t_ref[...]
        loss_part = jnp.zeros((1, 128), F32) + 0.5 * jnp.sum(jnp.mean(err * err, axis=-1, keepdims=True))
        dy = err * (1.0 / D_MODEL)
        dg_part = jnp.sum(dy * xhat, axis=0, keepdims=True)

        @pl.when(pl.program_id(0) == 0)
        def _():
            dg_ref[...] = dg_part
            loss_ref[...] = loss_part

        @pl.when(pl.program_id(0) > 0)
        def _():
            dg_ref[...] += dg_part
            loss_ref[...] += loss_part

        dxh = dy * gv
        dx = r * (dxh - xhat * jnp.mean(dxh * xhat, axis=-1, keepdims=True))
        dx_ref[...] = dx
        dx16_ref[...] = dx.astype(BF16)

    row = pl.BlockSpec((tm, D_MODEL), lambda i: (i, 0))
    vec = pl.BlockSpec((1, D_MODEL), lambda i: (0, 0))
    return pl.pallas_call(
        body,
        name="loss_head",
        grid=(N_TOK // tm,),
        in_specs=[row, vec, row],
        out_specs=[row, row, vec, pl.BlockSpec((1, 128), lambda i: (0, 0))],
        out_shape=[
            jax.ShapeDtypeStruct((N_TOK, D_MODEL), F32),
            jax.ShapeDtypeStruct((N_TOK, D_MODEL), BF16),
            jax.ShapeDtypeStruct((1, D_MODEL), F32),
            jax.ShapeDtypeStruct((1, 128), F32),
        ],
        compiler_params=_cp("arbitrary"),
    )(x, g, target)


XA_TQ = 1024
XA_SCALE = XA_DIM ** -0.5


def _attn_probs(q16, k16):
    s = lax.dot_general(q16, k16, _NT, preferred_element_type=F32) * XA_SCALE
    e = jnp.exp(s - jnp.max(s, axis=-1, keepdims=True))
    return e / jnp.sum(e, axis=-1, keepdims=True)


def _attn_fwd(z, kv, *, name):
    nt = SEQ // XA_TQ

    def body(q_ref, k_ref, v_ref, o_ref):
        p = _attn_probs(q_ref[...].astype(BF16), k_ref[...].astype(BF16))
        o_ref[...] = jnp.dot(p.astype(BF16), v_ref[...].astype(BF16), preferred_element_type=F32).astype(BF16)

    return pl.pallas_call(
        body,
        name=name,
        grid=(B_LOC, XA_HEADS, nt),
        in_specs=[
            pl.BlockSpec((XA_TQ, XA_DIM), lambda b, h, t: (b * nt + t, XA_OFF // XA_DIM + h)),
            pl.BlockSpec((MEM_LEN, XA_DIM), lambda b, h, t: (b, h)),
            pl.BlockSpec((MEM_LEN, XA_DIM), lambda b, h, t: (b, XA_HEADS + h)),
        ],
        out_specs=pl.BlockSpec((XA_TQ, XA_DIM), lambda b, h, t: (b * nt + t, h)),
        out_shape=jax.ShapeDtypeStruct((N_TOK, XA_HEADS * XA_DIM), BF16),
        compiler_params=_cp("parallel", "parallel", "arbitrary"),
    )(z, kv, kv)


def _attn_bwd(z, kv, dcat, *, do_off, name):
    nt = SEQ // XA_TQ

    def body(q_ref, k_ref, v_ref, do_ref, dq_ref, dk_ref, dv_ref):
        q16 = q_ref[...].astype(BF16)
        k16 = k_ref[...].astype(BF16)
        v16 = v_ref[...].astype(BF16)
        do16 = do_ref[...].astype(BF16)
        p = _attn_probs(q16, k16)
        dv_part = lax.dot_general(p.astype(BF16), do16, _TN, preferred_element_type=F32)
        dp = lax.dot_general(do16, v16, _NT, preferred_element_type=F32)
        ds16 = (p * (dp - jnp.sum(dp * p, axis=-1, keepdims=True)) * XA_SCALE).astype(BF16)
        dq_ref[...] = jnp.dot(ds16, k16, preferred_element_type=F32).astype(BF16)
        dk_part = lax.dot_general(ds16, q16, _TN, preferred_element_type=F32)

        @pl.when(pl.program_id(2) == 0)
        def _():
            dk_ref[...] = dk_part
            dv_ref[...] = dv_part

        @pl.when(pl.program_id(2) > 0)
        def _():
            dk_ref[...] += dk_part
            dv_ref[...] += dv_part

    qspec = pl.BlockSpec((XA_TQ, XA_DIM), lambda b, h, t: (b * nt + t, XA_OFF // XA_DIM + h))
    kspec = lambda off: pl.BlockSpec((MEM_LEN, XA_DIM), lambda b, h, t: (b, off + h))
    return pl.pallas_call(
        body,
        name=name,
        grid=(B_LOC, XA_HEADS, nt),
        in_specs=[qspec, kspec(0), kspec(XA_HEADS),
                  pl.BlockSpec((XA_TQ, XA_DIM), lambda b, h, t: (b * nt + t, do_off // XA_DIM + h))],
        out_specs=[pl.BlockSpec((XA_TQ, XA_DIM), lambda b, h, t: (b * nt + t, h)), kspec(0), kspec(0)],
        out_shape=[
            jax.ShapeDtypeStruct((N_TOK, XA_HEADS * XA_DIM), BF16),
            jax.ShapeDtypeStruct((B_LOC * MEM_LEN, XA_HEADS * XA_DIM), F32),
            jax.ShapeDtypeStruct((B_LOC * MEM_LEN, XA_HEADS * XA_DIM), F32),
        ],
        compiler_params=_cp("parallel", "parallel", "arbitrary"),
    )(z, kv, kv, dcat)


def _tril(n):
    return lax.broadcasted_iota(jnp.int32, (n, n), 0) >= lax.broadcasted_iota(jnp.int32, (n, n), 1)


def _lower_bound(lbl):
    e = jnp.exp(lbl - jnp.max(lbl, axis=0, keepdims=True))
    p = e / jnp.sum(e, axis=0, keepdims=True)
    return p[0:1, :], p


def _hgrn_gates(zq, zf, lb, tril_f):
    sig = _sigmoid(zf)
    f = lb + (1.0 - lb) * sig
    kk = 1.0 - f
    sq = _sigmoid(zq)
    q = zq * sq
    b = jnp.dot(tril_f, jnp.log(f), preferred_element_type=F32, precision=lax.Precision.HIGHEST)
    bl = b[HG_CHUNK - 1:HG_CHUNK, :]
    return q, sq, sig, f, kk, b, bl


HG_TB = 512
HG_CPB = HG_TB // HG_CHUNK
HG_NT = SEQ // HG_TB
HG_WIDTH = HG_HEADS * HG_DIM


def _head(h, section=0):
    return slice(section * HG_WIDTH + h * HG_DIM, section * HG_WIDTH + (h + 1) * HG_DIM)


def _hgrn_fwd(z, o_mem, lb_logits, gnorm):
    def body(zq_ref, zf_ref, zi_ref, zg_ref, omem_ref, lbl_ref, gn_ref, o_ref, opre_ref, sall_ref, st_ref):
        lb, _ = _lower_bound(lbl_ref[...])
        gn = gn_ref[...]
        mask = _tril(HG_CHUNK)
        tril_f = mask.astype(F32)
        o_ref[:, HG_WIDTH:] = omem_ref[...]

        @pl.when(pl.program_id(1) == 0)
        def _():
            st_ref[...] = jnp.zeros_like(st_ref)

        def chunk(c, carry):
            rows = pl.ds(pl.multiple_of(c * HG_CHUNK, HG_CHUNK), HG_CHUNK)
            q, _, _, _, kk, b, bl = _hgrn_gates(zq_ref[rows, :], zf_ref[rows, :], lb, tril_f)
            v16 = zi_ref[rows, :].astype(BF16)
            qd16 = (q * jnp.exp(b)).astype(BF16)
            ki16 = (kk * jnp.exp(-b)).astype(BF16)
            kd16 = (kk * jnp.exp(bl - b)).astype(BF16)
            ebl = jnp.exp(bl)
            zg = zg_ref[rows, :]
            gate = zg * _sigmoid(zg)
            for h in range(HG_HEADS):
                sl = _head(h)
                a = jnp.where(mask, lax.dot_general(qd16[:, sl], ki16[:, sl], _NT, preferred_element_type=F32), 0.0)
                st = st_ref[h]
                sall_ref[0, h, c] = st
                o = jnp.dot(a.astype(BF16), v16[:, sl], preferred_element_type=F32) + lax.dot_general(
                    qd16[:, sl], st.astype(BF16), _NT, preferred_element_type=F32)
                st_ref[h] = st * ebl[:, sl] + lax.dot_general(v16[:, sl], kd16[:, sl], _TN, preferred_element_type=F32)
                opre_ref[rows, sl] = o
                r = lax.rsqrt(jnp.mean(o * o, axis=-1, keepdims=True) + EPS)
                o_ref[rows, sl] = ((o * r * gn) * gate[:, sl]).astype(BF16)
            return carry

        lax.fori_loop(0, HG_CPB, chunk, 0)

    zspec = lambda s: pl.BlockSpec((HG_TB, HG_WIDTH), lambda b, t: (b * HG_NT + t, s))
    return pl.pallas_call(
        body,
        name="hgrn_fwd",
        grid=(B_LOC, HG_NT),
        in_specs=[zspec(0), zspec(1), zspec(2), zspec(3), zspec(0),
                  pl.BlockSpec((3, HG_WIDTH), lambda b, t: (0, 0)), pl.BlockSpec((1, HG_DIM), lambda b, t: (0, 0))],
        out_specs=[pl.BlockSpec((HG_TB, 2 * HG_WIDTH), lambda b, t: (b * HG_NT + t, 0)), zspec(0),
                   pl.BlockSpec((1, HG_HEADS, HG_CPB, HG_DIM, HG_DIM), lambda b, t: (b, 0, t, 0, 0))],
        out_shape=[
            jax.ShapeDtypeStruct((N_TOK, 2 * HG_WIDTH), BF16),
            jax.ShapeDtypeStruct((N_TOK, HG_WIDTH), F32),
            jax.ShapeDtypeStruct((B_LOC, HG_HEADS, HG_NCHUNK, HG_DIM, HG_DIM), F32),
        ],
        scratch_shapes=[pltpu.VMEM((HG_HEADS, HG_DIM, HG_DIM), F32)],
        compiler_params=_cp("parallel", "arbitrary"),
    )(z, z, z, z, o_mem, lb_logits, gnorm)


def _hgrn_bwd(z, opre, dcat, dq_mem, sall, lb_logits, gnorm):
    def body(zq_ref, zf_ref, zi_ref, zg_ref, opre_ref, dout_ref, dqm_ref, sall_ref, lbl_ref, gn_ref,
             dz_ref, dlbl_ref, dgn_ref, dst_ref, dlb_ref, dgn_acc, db_ref, dkk_ref, dbl_ref):
        b_id, t_id = pl.program_id(0), pl.program_id(1)
        lb, p = _lower_bound(lbl_ref[...])
        gn = gn_ref[...]
        mask = _tril(HG_CHUNK)
        tril_f = mask.astype(F32)
        dz_ref[:, 4 * HG_WIDTH:] = dqm_ref[...]

        @pl.when(t_id == 0)
        def _():
            dst_ref[...] = jnp.zeros_like(dst_ref)
            dlb_ref[...] = jnp.zeros_like(dlb_ref)

        @pl.when((b_id == 0) & (t_id == 0))
        def _():
            dgn_acc[...] = jnp.zeros_like(dgn_acc)

        def chunk(i, carry):
            c = HG_CPB - 1 - i
            rows = pl.ds(pl.multiple_of(c * HG_CHUNK, HG_CHUNK), HG_CHUNK)
            zq, zg = zq_ref[rows, :], zg_ref[rows, :]
            q, sq, sig, f, kk, b, bl = _hgrn_gates(zq, zf_ref[rows, :], lb, tril_f)
            v16 = zi_ref[rows, :].astype(BF16)
            eb, enb, ebl_b, ebl = jnp.exp(b), jnp.exp(-b), jnp.exp(bl - b), jnp.exp(bl)
            qd, ki, kd = q * eb, kk * enb, kk * ebl_b
            qd16, ki16, kd16 = qd.astype(BF16), ki.astype(BF16), kd.astype(BF16)
            o_all = opre_ref[rows, :]
            dout = dout_ref[rows, :]
            sg = _sigmoid(zg)
            d_on_all = dout * (zg * sg)
            dgate = dout * (sg * (1.0 + zg * (1.0 - sg)))
            dq_scale = eb * (sq * (1.0 + zq * (1.0 - sq)))
            for h in range(HG_HEADS):
                sl = _head(h)
                o = o_all[:, sl]
                r = lax.rsqrt(jnp.mean(o * o, axis=-1, keepdims=True) + EPS)
                ohat = o * r
                d_on = d_on_all[:, sl]
                dz_ref[rows, _head(h, 3)] = (dgate[:, sl] * (ohat * gn)).astype(BF16)
                dgn_acc[...] += jnp.sum(d_on * ohat, axis=0, keepdims=True)
                dohat = d_on * gn
                do16 = (r * (dohat - ohat * jnp.mean(dohat * ohat, axis=-1, keepdims=True))).astype(BF16)
                st = sall_ref[0, h, c]
                dst = dst_ref[h]
                st16, dst16 = st.astype(BF16), dst.astype(BF16)
                qd_h, ki_h, kd_h, v_h = qd16[:, sl], ki16[:, sl], kd16[:, sl], v16[:, sl]
                a16 = jnp.where(mask, lax.dot_general(qd_h, ki_h, _NT, preferred_element_type=F32), 0.0).astype(BF16)
                da16 = jnp.where(mask, lax.dot_general(do16, v_h, _NT, preferred_element_type=F32), 0.0).astype(BF16)
                dv = lax.dot_general(a16, do16, _TN, preferred_element_type=F32) + lax.dot_general(
                    kd_h, dst16, _NT, preferred_element_type=F32)
                dqd = jnp.dot(da16, ki_h, preferred_element_type=F32) + jnp.dot(do16, st16, preferred_element_type=F32)
                dki = lax.dot_general(da16, qd_h, _TN, preferred_element_type=F32)
                dkd = jnp.dot(v_h, dst16, preferred_element_type=F32)
                dbl_ref[:, sl] = jnp.sum(dkd * kd[:, sl], axis=0, keepdims=True) + ebl[:, sl] * jnp.sum(
                    st * dst, axis=0, keepdims=True)
                dst_ref[h] = dst * ebl[:, sl] + lax.dot_general(do16, qd_h, _TN, preferred_element_type=F32)
                dz_ref[rows, _head(h, 2)] = dv.astype(BF16)
                dz_ref[rows, sl] = (dqd * dq_scale[:, sl]).astype(BF16)
                dkk_ref[:, sl] = dki * enb[:, sl] + dkd * ebl_b[:, sl]
                db_ref[:, sl] = dqd * qd[:, sl] - dki * ki[:, sl] - dkd * kd[:, sl]
            dlogf = lax.dot_general(tril_f, db_ref[...], _TN, preferred_element_type=F32,
                                    precision=lax.Precision.HIGHEST) + dbl_ref[...]
            df = dlogf / f - dkk_ref[...]
            dz_ref[rows, HG_WIDTH:2 * HG_WIDTH] = (df * (1.0 - lb) * sig * (1.0 - sig)).astype(BF16)
            dlb_ref[...] += jnp.sum(df * (1.0 - sig), axis=0, keepdims=True)
            return carry

        lax.fori_loop(0, HG_CPB, chunk, 0)

        @pl.when(t_id == HG_NT - 1)
        def _():
            row0 = (lax.broadcasted_iota(jnp.int32, (3, HG_WIDTH), 0) == 0).astype(F32)
            dlbl_part = dlb_ref[...] * lb * (row0 - p)

            @pl.when(b_id == 0)
            def _():
                dlbl_ref[...] = dlbl_part

            @pl.when(b_id > 0)
            def _():
                dlbl_ref[...] += dlbl_part

            dgn_ref[...] = dgn_acc[...]

    rev = lambda b, t: b * HG_NT + HG_NT - 1 - t
    zspec = lambda s: pl.BlockSpec((HG_TB, HG_WIDTH), lambda b, t: (rev(b, t), s))
    return pl.pallas_call(
        body,
        name="hgrn_bwd",
        grid=(B_LOC, HG_NT),
        in_specs=[zspec(0), zspec(1), zspec(2), zspec(3), zspec(0), zspec(0), zspec(0),
                  pl.BlockSpec((1, HG_HEADS, HG_CPB, HG_DIM, HG_DIM), lambda b, t: (b, 0, HG_NT - 1 - t, 0, 0)),
                  pl.BlockSpec((3, HG_WIDTH), lambda b, t: (0, 0)), pl.BlockSpec((1, HG_DIM), lambda b, t: (0, 0))],
        out_specs=[pl.BlockSpec((HG_TB, 5 * HG_WIDTH), lambda b, t: (rev(b, t), 0)),
                   pl.BlockSpec((3, HG_WIDTH), lambda b, t: (0, 0)), pl.BlockSpec((1, HG_DIM), lambda b, t: (0, 0))],
        out_shape=[jax.ShapeDtypeStruct((N_TOK, 5 * HG_WIDTH), BF16),
                   jax.ShapeDtypeStruct((3, HG_WIDTH), F32), jax.ShapeDtypeStruct((1, HG_DIM), F32)],
        scratch_shapes=[pltpu.VMEM((HG_HEADS, HG_DIM, HG_DIM), F32), pltpu.VMEM((1, HG_WIDTH), F32),
                        pltpu.VMEM((1, HG_DIM), F32), pltpu.VMEM((HG_CHUNK, HG_WIDTH), F32),
                        pltpu.VMEM((HG_CHUNK, HG_WIDTH), F32), pltpu.VMEM((1, HG_WIDTH), F32)],
        compiler_params=_cp("arbitrary", "arbitrary"),
    )(z, z, z, z, opre, dcat, dq_mem, sall, lb_logits, gnorm)


GM_TM = 256


def _gmlp_norm(zv, ln_g, ln_b):
    gv, dgelu = _gelu_parts(zv)
    xc = gv - jnp.mean(gv, axis=-1, keepdims=True)
    rstd = lax.rsqrt(jnp.mean(xc * xc, axis=-1, keepdims=True) + EPS)
    vhat = xc * rstd
    return vhat * ln_g + ln_b, vhat, rstd, dgelu


def _gmlp_specs():
    half = lambda j: pl.BlockSpec((GM_TM, GM_WIDTH), lambda i: (i, j))
    vec = pl.BlockSpec((1, GM_WIDTH), lambda i: (0, 0))
    w = pl.BlockSpec((GM_GROUPS, GM_CHUNK, GM_CHUNK), lambda i: (0, 0, 0))
    bt = pl.BlockSpec((GM_CHUNK, GM_GROUPS), lambda i: (0, 0))
    return half, vec, w, bt


def _gmlp_fwd(z, o_mem, ln_g, ln_b, w_s, b_st):
    def body(zu_ref, zv_ref, omem_ref, g_ref, b_ref, w_ref, bt_ref, o_ref):
        o_ref[:, GM_WIDTH:] = omem_ref[...]
        u, _ = _gelu_parts(zu_ref[...])
        v, _, _, _ = _gmlp_norm(zv_ref[...], g_ref[...], b_ref[...])
        v16 = v.astype(BF16)
        mask = _tril(GM_CHUNK)
        bt = bt_ref[...]
        for g in range(GM_GROUPS):
            wm16 = jnp.where(mask, w_ref[g], 0.0).astype(BF16)
            cols = slice(g * GM_GDIM, (g + 1) * GM_GDIM)
            for c in range(GM_TM // GM_CHUNK):
                rows = slice(c * GM_CHUNK, (c + 1) * GM_CHUNK)
                mixed = jnp.dot(wm16, v16[rows, cols], preferred_element_type=F32) + bt[:, g:g + 1]
                o_ref[rows, cols] = (u[rows, cols] * mixed).astype(BF16)

    half, vec, w, bt = _gmlp_specs()
    return pl.pallas_call(
        body,
        name="gmlp_fwd",
        grid=(N_TOK // GM_TM,),
        in_specs=[half(0), half(1), pl.BlockSpec((GM_TM, XA_HEADS * XA_DIM), lambda i: (i, 0)), vec, vec, w, bt],
        out_specs=pl.BlockSpec((GM_TM, GM_WIDTH + XA_HEADS * XA_DIM), lambda i: (i, 0)),
        out_shape=jax.ShapeDtypeStruct((N_TOK, GM_WIDTH + XA_HEADS * XA_DIM), BF16),
        compiler_params=_cp("parallel"),
    )(z, z, o_mem, ln_g, ln_b, w_s, b_st)


def _gmlp_bwd(z, dcat, dq_mem, ln_g, ln_b, w_s, b_st):
    def body(zu_ref, zv_ref, dout_ref, dqm_ref, g_ref, b_ref, w_ref, bt_ref,
             dz_ref, dw_ref, dbt_ref, dg_ref, db_ref, dv_ref):
        dz_ref[:, 2 * GM_WIDTH:] = dqm_ref[...]
        @pl.when(pl.program_id(0) == 0)
        def _():
            dw_ref[...] = jnp.zeros_like(dw_ref)
            dbt_ref[...] = jnp.zeros_like(dbt_ref)
            dg_ref[...] = jnp.zeros_like(dg_ref)
            db_ref[...] = jnp.zeros_like(db_ref)

        zu = zu_ref[...]
        u, du_dz = _gelu_parts(zu)
        ln_g = g_ref[...]
        v, vhat, rstd, dgv_dz = _gmlp_norm(zv_ref[...], ln_g, b_ref[...])
        v16 = v.astype(BF16)
        dout = dout_ref[...]
        dmixed = dout * u
        dm16 = dmixed.astype(BF16)
        mask = _tril(GM_CHUNK)
        bt = bt_ref[...]
        group_id = lax.broadcasted_iota(jnp.int32, (1, GM_GROUPS), 1)
        dbt = jnp.zeros((GM_CHUNK, GM_GROUPS), F32)
        for g in range(GM_GROUPS):
            wm16 = jnp.where(mask, w_ref[g], 0.0).astype(BF16)
            cols = slice(g * GM_GDIM, (g + 1) * GM_GDIM)
            dw = jnp.zeros((GM_CHUNK, GM_CHUNK), F32)
            dbt_g = jnp.zeros((GM_CHUNK, 1), F32)
            for c in range(GM_TM // GM_CHUNK):
                rows = slice(c * GM_CHUNK, (c + 1) * GM_CHUNK)
                mixed = jnp.dot(wm16, v16[rows, cols], preferred_element_type=F32) + bt[:, g:g + 1]
                dz_ref[rows, cols] = (dout[rows, cols] * mixed * du_dz[rows, cols]).astype(BF16)
                dw += lax.dot_general(dm16[rows, cols], v16[rows, cols], _NT, preferred_element_type=F32)
                dbt_g += jnp.sum(dmixed[rows, cols], axis=-1, keepdims=True)
                dv_ref[rows, cols] = lax.dot_general(wm16, dm16[rows, cols], _TN, preferred_element_type=F32)
            dw_ref[g] += jnp.where(mask, dw, 0.0)
            dbt = dbt + dbt_g * (group_id == g).astype(F32)
        dbt_ref[...] += dbt
        dv = dv_ref[...]
        dg_ref[...] += jnp.sum(dv * vhat, axis=0, keepdims=True)
        db_ref[...] += jnp.sum(dv, axis=0, keepdims=True)
        dvh = dv * ln_g
        dgv = rstd * (dvh - jnp.mean(dvh, axis=-1, keepdims=True) - vhat * jnp.mean(dvh * vhat, axis=-1, keepdims=True))
        dz_ref[:, GM_WIDTH:2 * GM_WIDTH] = (dgv * dgv_dz).astype(BF16)

    half, vec, w, bt = _gmlp_specs()
    dz_width = 2 * GM_WIDTH + XA_HEADS * XA_DIM
    return pl.pallas_call(
        body,
        name="gmlp_bwd",
        grid=(N_TOK // GM_TM,),
        in_specs=[half(0), half(1), half(0), pl.BlockSpec((GM_TM, XA_HEADS * XA_DIM), lambda i: (i, 0)), vec, vec, w, bt],
        out_specs=[pl.BlockSpec((GM_TM, dz_width), lambda i: (i, 0)), w, bt, vec, vec],
        out_shape=[jax.ShapeDtypeStruct((N_TOK, dz_width), BF16),
                   jax.ShapeDtypeStruct((GM_GROUPS, GM_CHUNK, GM_CHUNK), F32),
                   jax.ShapeDtypeStruct((GM_CHUNK, GM_GROUPS), F32),
                   jax.ShapeDtypeStruct((1, GM_WIDTH), F32), jax.ShapeDtypeStruct((1, GM_WIDTH), F32)],
        scratch_shapes=[pltpu.VMEM((GM_TM, GM_WIDTH), F32)],
        compiler_params=_cp("arbitrary"),
    )(z, z, dcat, dq_mem, ln_g, ln_b, w_s, b_st)


def _own_slot(shape):
    return pl.BlockSpec((None,) + tuple(shape), lambda i, me_ref: (me_ref[0],) + (0,) * len(shape))


def _place_rows(w, layer, cuts_columns, me, *, name):
    _, r, c = w.shape
    n = c if cuts_columns else r

    def body(me_ref, w_ref, o_ref):
        wv = w_ref[...]
        o_ref[...] = (wv.T if cuts_columns else wv).astype(BF16)

    return pl.pallas_call(
        body,
        name=name,
        grid_spec=pltpu.PrefetchScalarGridSpec(
            num_scalar_prefetch=1, grid=(1,),
            in_specs=[pl.BlockSpec((None, r, c), lambda i, me_ref: (layer, 0, 0))],
            out_specs=_own_slot((n, D_MODEL))),
        out_shape=jax.ShapeDtypeStruct((N_DEV, n, D_MODEL), BF16),
        compiler_params=_cp("arbitrary"),
    )(me, w)


def _place_ln(ln_g, ln_b, me):
    blk = ln_g.shape[1]

    def body(me_ref, g_ref, b_ref, o_ref):
        o_ref[...] = jnp.zeros_like(o_ref)
        o_ref[0:1, :] = g_ref[...]
        o_ref[1:2, :] = b_ref[...]

    vec = pl.BlockSpec((1, blk), lambda i, me_ref: (0, 0))
    return pl.pallas_call(
        body,
        name="place_ln",
        grid_spec=pltpu.PrefetchScalarGridSpec(
            num_scalar_prefetch=1, grid=(1,), in_specs=[vec, vec], out_specs=_own_slot((8, blk))),
        out_shape=jax.ShapeDtypeStruct((N_DEV, 8, blk), F32),
        compiler_params=_cp("arbitrary"),
    )(me, ln_g, ln_b)


def _place_slab(a, me, *, name):
    def body(me_ref, a_ref, o_ref):
        o_ref[...] = a_ref[...]

    return pl.pallas_call(
        body,
        name=name,
        grid_spec=pltpu.PrefetchScalarGridSpec(
            num_scalar_prefetch=1, grid=(1,),
            in_specs=[pl.BlockSpec(a.shape, lambda i, me_ref: (0, 0))], out_specs=_own_slot(a.shape)),
        out_shape=jax.ShapeDtypeStruct((N_DEV,) + a.shape, a.dtype),
        compiler_params=_cp("arbitrary"),
    )(me, a)


def _place_own(grads, me, *, name):
    k = len(grads)

    def body(me_ref, *refs):
        for src, dst in zip(refs[:k], refs[k:]):
            dst[...] = src[...]

    specs = [_own_slot(g.shape[1:]) for g in grads]
    return pl.pallas_call(
        body,
        name=name,
        grid_spec=pltpu.PrefetchScalarGridSpec(num_scalar_prefetch=1, grid=(1,), in_specs=specs, out_specs=specs),
        out_shape=[jax.ShapeDtypeStruct(g.shape, g.dtype) for g in grads],
        compiler_params=_cp("arbitrary"),
    )(me, *grads)


def _mesh_pos():
    x, y, c = (lax.axis_index(a) for a in MESH_AXES)
    return x, y, c, 4 * x + 2 * y + c


def _peer(x, y, c, r):
    px = 1 - x if r & 4 else x
    py = 1 - y if r & 2 else y
    pc = 1 - c if r & 1 else c
    return (px, py, pc), 4 * px + 2 * py + pc


RELATIONS = {"scatter": (1, 2, 3, 4, 5, 6, 7), "gather_all": (1, 2, 3, 4, 5, 6, 7), "gather_chips": (1, 2, 4, 6),
             "gather_sibling": (2, 4, 6)}


def _peer_copies(srcs, lands, send_sems, recv_sems, mode, waits):
    x, y, c, me = _mesh_pos()
    rel = RELATIONS[mode]
    pairs = []
    for ri, r in enumerate(rel):
        if mode == "gather_sibling":
            peer, _ = _peer(x, y, c, 1)
            _, sent_blk = _peer(x, y, c, r)
            _, got_blk = _peer(x, y, c, r ^ 1)
        else:
            peer, peer_blk = _peer(x, y, c, r)
            sent_blk, got_blk = (peer_blk if mode == "scatter" else me), peer_blk
        for k, (src, land) in enumerate(zip(srcs, lands)):
            idx = k * len(rel) + ri
            sems = dict(send_sem=send_sems.at[idx], recv_sem=recv_sems.at[idx], device_id=peer,
                        device_id_type=pl.DeviceIdType.MESH)
            dst_blk = sent_blk if mode == "gather_sibling" else me
            mine = pltpu.make_async_remote_copy(src_ref=src.at[sent_blk], dst_ref=land.at[dst_blk], **sems)
            theirs = pltpu.make_async_remote_copy(src_ref=src.at[sent_blk], dst_ref=land.at[got_blk], **sems) if waits else None
            pairs.append((mine, theirs))
    return pairs


DATAFLOW = pltpu.SideEffectType.DATAFLOW_SIDE_EFFECTING


def _in_hbm(a):
    return pltpu.with_memory_space_constraint(a, pltpu.HBM)


def _copies_start(srcs, lands, *, mode, name, deps=()):
    gather = mode != "scatter"
    arrs = list(lands) if gather else list(srcs) + list(lands)
    n, k, nd = len(arrs), len(lands), len(deps)

    def body(*refs):
        ins, send_sems, recv_sems, token = refs[:n], refs[n + nd], refs[n + nd + 1], refs[2 * n + nd + 2]
        src_refs, land_refs = (ins, ins) if gather else (ins[:k], ins[k:])
        for mine, _ in _peer_copies(src_refs, land_refs, send_sems, recv_sems, mode, waits=False):
            mine.start()
        token[...] = jnp.zeros_like(token)

    n_cp = k * len(RELATIONS[mode])
    return pl.pallas_call(
        body,
        name=name,
        in_specs=[HBM_SPEC] * n + [ANY_SPEC] * nd,
        out_specs=(SEM_SPEC, SEM_SPEC, *[HBM_SPEC] * n, pl.BlockSpec(memory_space=pltpu.VMEM)),
        out_shape=(pltpu.SemaphoreType.DMA((n_cp,)), pltpu.SemaphoreType.DMA((n_cp,)),
                   *[pltpu.HBM(a.shape, a.dtype) for a in arrs], jax.ShapeDtypeStruct((8, 128), F32)),
        input_output_aliases={i: 2 + i for i in range(n)},
        compiler_params=pltpu.CompilerParams(has_side_effects=DATAFLOW),
    )(*[_in_hbm(a) for a in arrs], *deps)


def _copies_wait(arrs, send_sems, recv_sems, after, *, n_lands, mode, name):
    n, k = len(arrs), n_lands
    gather = mode != "scatter"

    def body(*refs):
        ins, send_sems, recv_sems = refs[:n], refs[n], refs[n + 1]
        src_refs, land_refs = (ins, ins) if gather else (ins[:k], ins[k:])
        for mine, theirs in _peer_copies(src_refs, land_refs, send_sems, recv_sems, mode, waits=True):
            mine.wait_send()
            theirs.wait_recv()

    outs = pl.pallas_call(
        body,
        name=name,
        in_specs=[HBM_SPEC] * n + [SEM_SPEC, SEM_SPEC] + [ANY_SPEC] * len(after),
        out_specs=[HBM_SPEC] * n,
        out_shape=[pltpu.HBM(a.shape, a.dtype) for a in arrs],
        input_output_aliases={i: i for i in range(n)},
        compiler_params=pltpu.CompilerParams(has_side_effects=DATAFLOW),
    )(*arrs, send_sems, recv_sems, *after)
    return outs[n - k:]


def _exchange_small(slabs):
    n = len(slabs)

    def body(*refs):
        ins, outs = refs[:n], refs[n:2 * n]
        send_sems, recv_sems, local_sems = refs[2 * n:]
        x, y, c, me = _mesh_pos()
        own = [pltpu.make_async_copy(src, dst.at[me], local_sems.at[k]) for k, (src, dst) in enumerate(zip(ins, outs))]
        for cp in own:
            cp.start()
        sends, recvs = [], []
        for r in range(1, N_DEV):
            peer, peer_blk = _peer(x, y, c, r)
            for k, (src, dst) in enumerate(zip(ins, outs)):
                idx = k * (N_DEV - 1) + r - 1
                sems = dict(send_sem=send_sems.at[idx], recv_sem=recv_sems.at[idx], device_id=peer,
                            device_id_type=pl.DeviceIdType.MESH)
                send = pltpu.make_async_remote_copy(src_ref=src, dst_ref=dst.at[me], **sems)
                send.start()
                sends.append(send)
                recvs.append(pltpu.make_async_remote_copy(src_ref=src, dst_ref=dst.at[peer_blk], **sems))
        for cp in recvs:
            cp.wait_recv()
        for cp in sends:
            cp.wait_send()
        for cp in own:
            cp.wait()

    n_cp = n * (N_DEV - 1)
    return pl.pallas_call(
        body,
        name="exchange_small_grads",
        in_specs=[ANY_SPEC] * n,
        out_specs=[ANY_SPEC] * n,
        out_shape=[jax.ShapeDtypeStruct((N_DEV,) + s.shape, F32) for s in slabs],
        scratch_shapes=[pltpu.SemaphoreType.DMA((n_cp,)), pltpu.SemaphoreType.DMA((n_cp,)),
                        pltpu.SemaphoreType.DMA((n,))],
    )(*slabs)


def _adamw(w, g, m, v):
    m = ADAM_B1 * m + (1.0 - ADAM_B1) * g
    v = ADAM_B2 * v + (1.0 - ADAM_B2) * (g * g)
    m_hat = m / (1.0 - ADAM_B1 ** ADAM_STEP)
    v_hat = v / (1.0 - ADAM_B2 ** ADAM_STEP)
    return -ADAM_LR * (m_hat / (jnp.sqrt(v_hat) + ADAM_EPS) + ADAM_WD * w), m, v


ADAM_TC = 256


def _adam_big(slots, w, m, v, cuts_columns, *, name):
    layers, n, nj = len(slots), slots[0].shape[1], D_MODEL // ADAM_TC

    def body(*refs):
        s_refs = refs[:layers]
        w_ref, m_ref, v_ref, g_ref, d_ref, nm_ref, nv_ref, acc_ref = refs[layers:]
        for ll in range(layers):
            @pl.when(pl.program_id(0) == ll)
            def _(s_ref=s_refs[ll]):
                g = s_ref[0].astype(F32)
                for s in range(1, N_DEV):
                    g = g + s_ref[s].astype(F32)
                acc_ref[...] = g

        g = acc_ref[...].T if cuts_columns else acc_ref[...]
        g_ref[...] = g
        d_ref[...], nm_ref[...], nv_ref[...] = _adamw(w_ref[...], g, m_ref[...], v_ref[...])

    def slot_spec(ll):
        return pl.BlockSpec((N_DEV, n, ADAM_TC),
                            lambda l, j: (0, 0, jnp.where(l < ll, 0, jnp.where(l > ll, nj - 1, j))))

    if cuts_columns:
        w_spec = pl.BlockSpec((None, ADAM_TC, n), lambda l, j: (l, j, 0))
    else:
        w_spec = pl.BlockSpec((None, n, ADAM_TC), lambda l, j: (l, 0, j))
    return pl.pallas_call(
        body,
        name=name,
        grid=(layers, nj),
        in_specs=[slot_spec(ll) for ll in range(layers)] + [w_spec] * 3,
        out_specs=[w_spec] * 4,
        out_shape=[jax.ShapeDtypeStruct(w.shape, F32)] * 4,
        scratch_shapes=[pltpu.VMEM((n, ADAM_TC), F32)],
        compiler_params=_cp("arbitrary", "arbitrary"),
    )(*slots, w, m, v)


def _adam_slabs(slots, ws, ms, vs):
    n = len(slots)

    def body(*refs):
        ins, outs = refs[:4 * n], refs[4 * n:]
        for k in range(n):
            s_ref, w_ref, m_ref, v_ref = ins[k], ins[n + k], ins[2 * n + k], ins[3 * n + k]
            g = s_ref[0]
            for s in range(1, N_DEV):
                g = g + s_ref[s]
            outs[4 * k][...] = g
            outs[4 * k + 1][...], outs[4 * k + 2][...], outs[4 * k + 3][...] = _adamw(w_ref[...], g, m_ref[...], v_ref[...])

    res = pl.pallas_call(
        body,
        name="small_adamw",
        out_shape=[jax.ShapeDtypeStruct(w.shape, F32) for w in ws for _ in range(4)],
        compiler_params=pltpu.CompilerParams(vmem_limit_bytes=VMEM_LIMIT_BYTES),
    )(*slots, *ws, *ms, *vs)
    return [res[4 * k:4 * k + 4] for k in range(n)]


def _adam_vecs(gs, ws, ms, vs):
    n = len(gs)

    def body(*refs):
        ins, outs = refs[:4 * n], refs[4 * n:]
        for k in range(n):
            outs[3 * k][...], outs[3 * k + 1][...], outs[3 * k + 2][...] = _adamw(
                ins[n + k][...], ins[k][...], ins[2 * n + k][...], ins[3 * n + k][...])

    res = pl.pallas_call(
        body,
        name="ln_adamw",
        out_shape=[jax.ShapeDtypeStruct(w.shape, F32) for w in ws for _ in range(3)],
        compiler_params=pltpu.CompilerParams(vmem_limit_bytes=VMEM_LIMIT_BYTES),
    )(*gs, *ws, *ms, *vs)
    return [res[3 * k:3 * k + 3] for k in range(n)]


SLAB_AT = dict(mem_norm=0, lb_logits=1, ffn1_norm=4, mix_norm=6, hgrn_gnorm=8, gmlp_ln_g=9, gmlp_ln_b=11,
               gmlp_b_s=13, ffn2_norm=14, final_norm=16)
SLAB_ROWS = 24
SMALL_SHARDED = ("gmlp_ln_g", "gmlp_ln_b")


def _pack_slab(parts, *, name):
    flat, plan = [], []
    for pname, at in SLAB_AT.items():
        for a in parts.get(pname, ()):
            flat.append(a)
            plan.append((at, a.shape))
            at += max(1, a.shape[0] * a.shape[1] // D_MODEL)

    def body(*refs):
        o_ref = refs[-1]
        o_ref[...] = jnp.zeros_like(o_ref)
        for ref, (at, (r, w)) in zip(refs, plan):
            if w == D_MODEL or r == 1 and w < D_MODEL:
                o_ref[at:at + r, 0:w] = ref[...]
            elif w < D_MODEL:
                for j in range(r):
                    o_ref[at:at + 1, j * w:(j + 1) * w] = ref[j:j + 1, :]
            else:
                for j in range(w // D_MODEL):
                    o_ref[at + j:at + j + 1, :] = ref[:, j * D_MODEL:(j + 1) * D_MODEL]

    return pl.pallas_call(
        body,
        name=name,
        out_shape=jax.ShapeDtypeStruct((SLAB_ROWS, D_MODEL), F32),
        compiler_params=pltpu.CompilerParams(vmem_limit_bytes=VMEM_LIMIT_BYTES),
    )(*flat)


def _unpack_slab(slab, shapes):
    out = {}
    for pname, at in SLAB_AT.items():
        if pname in SMALL_SHARDED:
            continue
        size = math.prod(shapes[pname])
        rows = max(1, size // D_MODEL)
        out[pname] = slab[at:at + rows].reshape(-1)[:size].reshape(shapes[pname])
    return out


def _ffn_fwd(x, norm_g, block, layer, full, get_weights):
    tag = f"l{layer}_{block}"
    full.update(get_weights((layer, f"{block}_in"), (x,)))
    h, z, act = _norm_mm(x, norm_g, full[(f"{block}_w_in", layer)], swiglu=True, tm=512, tn=1408, deps=full.pop("deps", ()),
                         name=f"{tag}_in")
    full.update(get_weights((layer, f"{block}_out"), (act,)))
    y = _mm(act, full[(f"{block}_w_out", layer)], tm=512, tn=D_MODEL, tk=D_FF, out_dtype=F32, res=x, scale=0.5,
            deps=full.pop("deps", ()), name=f"{tag}_out")
    return y, (x, h, z, act)


def _ffn_bwd(dy, dy16, saved, norm_g, w_in_t, w_out, tag, deps=()):
    x, h, z, act = saved
    dw_out = _mm(act, dy16, ta=True, tm=1408, tn=D_MODEL, tk=N_TOK, out_dtype=BF16, scale=0.5, deps=deps,
                 name=f"{tag}_out_wgrad")
    dz = _swiglu_dgrad(dy16, w_out, z, scale=0.5, name=f"{tag}_out_dgrad")
    dw_in_t = _planes_wgrad(dz, h, name=f"{tag}_in_wgrad")
    dx, dx16, dg = _dgrad_norm_bwd(dz, w_in_t, x, norm_g, dy, name=f"{tag}_in_dgrad")
    return dx, dx16, dg, dw_in_t, dw_out


def kernel(x, mem, mem_norm, lb_logits, ffn1_norm, ffn1_w_in, ffn1_w_out, mix_norm, mem_w_kv, hgrn_w_in, hgrn_gnorm, hgrn_w_out, gmlp_w_in, gmlp_ln_g, gmlp_ln_b, gmlp_w_s, gmlp_b_s, gmlp_w_out, ffn2_norm, ffn2_w_in, ffn2_w_out, final_norm, loss_target, m_mem_norm, m_lb_logits, m_ffn1_norm, m_ffn1_w_in, m_ffn1_w_out, m_mix_norm, m_mem_w_kv, m_hgrn_w_in, m_hgrn_gnorm, m_hgrn_w_out, m_gmlp_w_in, m_gmlp_ln_g, m_gmlp_ln_b, m_gmlp_w_s, m_gmlp_b_s, m_gmlp_w_out, m_ffn2_norm, m_ffn2_w_in, m_ffn2_w_out, m_final_norm, v_mem_norm, v_lb_logits, v_ffn1_norm, v_ffn1_w_in, v_ffn1_w_out, v_mix_norm, v_mem_w_kv, v_hgrn_w_in, v_hgrn_gnorm, v_hgrn_w_out, v_gmlp_w_in, v_gmlp_ln_g, v_gmlp_ln_b, v_gmlp_w_s, v_gmlp_b_s, v_gmlp_w_out, v_ffn2_norm, v_ffn2_w_in, v_ffn2_w_out, v_final_norm):
    weights = dict(mem_norm=mem_norm, lb_logits=lb_logits, ffn1_norm=ffn1_norm, ffn1_w_in=ffn1_w_in, ffn1_w_out=ffn1_w_out, mix_norm=mix_norm, mem_w_kv=mem_w_kv, hgrn_w_in=hgrn_w_in, hgrn_gnorm=hgrn_gnorm, hgrn_w_out=hgrn_w_out, gmlp_w_in=gmlp_w_in, gmlp_ln_g=gmlp_ln_g, gmlp_ln_b=gmlp_ln_b, gmlp_w_s=gmlp_w_s, gmlp_b_s=gmlp_b_s, gmlp_w_out=gmlp_w_out, ffn2_norm=ffn2_norm, ffn2_w_in=ffn2_w_in, ffn2_w_out=ffn2_w_out, final_norm=final_norm)
    mom_m = dict(mem_norm=m_mem_norm, lb_logits=m_lb_logits, ffn1_norm=m_ffn1_norm, ffn1_w_in=m_ffn1_w_in, ffn1_w_out=m_ffn1_w_out, mix_norm=m_mix_norm, mem_w_kv=m_mem_w_kv, hgrn_w_in=m_hgrn_w_in, hgrn_gnorm=m_hgrn_gnorm, hgrn_w_out=m_hgrn_w_out, gmlp_w_in=m_gmlp_w_in, gmlp_ln_g=m_gmlp_ln_g, gmlp_ln_b=m_gmlp_ln_b, gmlp_w_s=m_gmlp_w_s, gmlp_b_s=m_gmlp_b_s, gmlp_w_out=m_gmlp_w_out, ffn2_norm=m_ffn2_norm, ffn2_w_in=m_ffn2_w_in, ffn2_w_out=m_ffn2_w_out, final_norm=m_final_norm)
    mom_v = dict(mem_norm=v_mem_norm, lb_logits=v_lb_logits, ffn1_norm=v_ffn1_norm, ffn1_w_in=v_ffn1_w_in, ffn1_w_out=v_ffn1_w_out, mix_norm=v_mix_norm, mem_w_kv=v_mem_w_kv, hgrn_w_in=v_hgrn_w_in, hgrn_gnorm=v_hgrn_gnorm, hgrn_w_out=v_hgrn_w_out, gmlp_w_in=v_gmlp_w_in, gmlp_ln_g=v_gmlp_ln_g, gmlp_ln_b=v_gmlp_ln_b, gmlp_w_s=v_gmlp_w_s, gmlp_b_s=v_gmlp_b_s, gmlp_w_out=v_gmlp_w_out, ffn2_norm=v_ffn2_norm, ffn2_w_in=v_ffn2_w_in, ffn2_w_out=v_ffn2_w_out, final_norm=v_final_norm)
    order = list(weights)
    _, _, _, me = _mesh_pos()
    me_arr = jnp.reshape(me, (1,)).astype(jnp.int32)
    cuts = {name: c for name, c, _, _ in GROUPS}
    rows_already = tuple(name for name, c, _, n in GROUPS if c and n % 128)
    as_rows = lambda a: jnp.transpose(a, (0, 2, 1))
    for name in rows_already:
        weights[name], mom_m[name], mom_v[name] = as_rows(weights[name]), as_rows(mom_m[name]), as_rows(mom_v[name])
        cuts[name] = False

    mix1 = (("mem_w_kv", 1), ("gmlp_w_in", 0), ("gmlp_w_out", 0))
    gather_plan = (
        ((0, "ffn1_in"), (("ffn1_w_in", 0),)),
        ((0, "ffn1_out"), (("ffn1_w_out", 0),)),
        ((0, "mix_in"), _stage_pieces(0, "mix")),
        ((0, "ffn2_in"), _stage_pieces(0, "ffn2")),
        ((1, "ffn1_in"), _stage_pieces(1, "ffn1")),
        ((1, "mix_in"), mix1),
        ((1, "ffn2_in"), _stage_pieces(1, "ffn2")),
    )
    stage_of = {use: k for k, (use, _) in enumerate(gather_plan)}
    in_flight = {}

    def start_chips(k, deps):
        pieces = gather_plan[k][1]
        lands = [_place_rows(weights[name], l, cuts[name], me_arr, name=f"place_{name}_{l}") for name, l in pieces]
        if pieces is mix1:
            lands.append(_place_ln(gmlp_ln_g, gmlp_ln_b, me_arr))
        send_sems, recv_sems, *thru, token = _copies_start(lands, lands, mode="gather_chips", deps=deps,
                                                           name=f"gather{k}_chips_start")
        in_flight[k] = (thru, send_sems, recv_sems)
        return token

    def pass_to_sibling(k, after):
        thru, send_sems, recv_sems = in_flight[k]
        outs = _copies_wait(thru, send_sems, recv_sems, after, n_lands=len(thru), mode="gather_chips",
                            name=f"gather{k}_chips_wait")
        send_sems, recv_sems, *thru, token = _copies_start(outs, outs, mode="gather_sibling",
                                                           name=f"gather{k}_sibling_start")
        in_flight[k] = (thru, send_sems, recv_sems)
        return token, token

    start_chips(0, ())

    def get_weights(use, after):
        if use not in stage_of:
            return {}
        k = stage_of[use]
        tokens = []
        if k == 0:
            token, landed = pass_to_sibling(0, after)
            tokens += [token, start_chips(1, (landed,))]
        thru, send_sems, recv_sems = in_flight[k]
        outs = _copies_wait(thru, send_sems, recv_sems, after, n_lands=len(thru), mode="gather_sibling",
                            name=f"gather{k}_sibling_wait")
        if k + 1 < len(gather_plan):
            token, landed = pass_to_sibling(k + 1, (outs[0],))
            tokens.append(token)
            if k + 2 < len(gather_plan):
                tokens.append(start_chips(k + 2, (landed,)))
        pieces = gather_plan[k][1]
        w = {p: o.reshape(N_DEV * o.shape[1], D_MODEL) for p, o in zip(pieces, outs)}
        w["deps"] = tuple(tokens)
        if pieces is mix1:
            w["ln_g"] = outs[-1][:, 0, :].reshape(1, GM_WIDTH)
            w["ln_b"] = outs[-1][:, 1, :].reshape(1, GM_WIDTH)
        return w

    scatter = {}

    def put_grads(st, grads):
        if st == "w_s":
            land = _place_slab(grads.reshape(GM_GROUPS * GM_CHUNK, GM_CHUNK), me_arr, name="w_s_place")
            send_sems, recv_sems, *thru, token = _copies_start([land], [land], mode="gather_all", name="w_s_start")
            scatter[st] = (thru, send_sems, recv_sems)
            return (token,)
        views = [grads[p].reshape(N_DEV, -1, D_MODEL) for p in _stage_pieces(*st)]
        recv = _place_own(views, me_arr, name=f"scatter_place_l{st[0]}_{st[1]}")
        send_sems, recv_sems, *thru, token = _copies_start(views, recv, mode="scatter",
                                                           name=f"scatter_start_l{st[0]}_{st[1]}")
        scatter[st] = (thru, send_sems, recv_sems)
        return (token,)

    dx, small, loss_part = _step_local(
        x, mem, loss_target, get_weights, put_grads, mem_norm, lb_logits, ffn1_norm, mix_norm, hgrn_gnorm,
        gmlp_w_s, gmlp_b_s, ffn2_norm, final_norm)

    def slots_of(blk, after):
        slots = {}
        for i in (1, 0):
            thru, send_sems, recv_sems = scatter[(i, blk)]
            outs = _copies_wait(thru, send_sems, recv_sems, after, n_lands=len(thru) // 2, mode="scatter",
                                name=f"scatter_wait_l{i}_{blk}")
            slots.update(zip(_stage_pieces(i, blk), outs))
        return slots

    grad, delta, new_m, new_v = {}, {}, {}, {}

    def adam_groups(slots, names):
        for name in names:
            layers = GROUP_LAYERS[name]
            grad[name], delta[name], new_m[name], new_v[name] = _adam_big(
                [slots[(name, l)] for l in range(layers)], weights[name], mom_m[name], mom_v[name], cuts[name],
                name=f"{name}_adamw")

    adam_groups(slots_of("ffn2", (dx,)), ("ffn2_w_in", "ffn2_w_out"))
    adam_groups(slots_of("mix", (delta["ffn2_w_out"],)),
                ("mem_w_kv", "gmlp_w_in", "gmlp_w_out", "hgrn_w_in", "hgrn_w_out"))

    def small_parts(src):
        parts = {n: [src[n].reshape(-1, src[n].shape[-1])] for n in SLAB_AT if n not in SMALL_SHARDED}
        return parts

    w_s_rows = lambda a: a.reshape(GM_GROUPS * GM_CHUNK, GM_CHUNK)
    (slab_slots,) = _exchange_small([_pack_slab(small, name="pack_small_grads")])
    thru, send_sems, recv_sems = scatter["w_s"]
    (ws_slots,) = _copies_wait(thru, send_sems, recv_sems, (slab_slots,), n_lands=1, mode="gather_all", name="w_s_wait")
    (g_slab, d_slab, nm_slab, nv_slab), (g_ws, d_ws, nm_ws, nv_ws) = _adam_slabs(
        [slab_slots, ws_slots],
        [_pack_slab(small_parts(weights), name="pack_small_w"), w_s_rows(gmlp_w_s)],
        [_pack_slab(small_parts(mom_m), name="pack_small_m"), w_s_rows(m_gmlp_w_s)],
        [_pack_slab(small_parts(mom_v), name="pack_small_v"), w_s_rows(v_gmlp_w_s)])
    shapes = {n: weights[n].shape for n in SLAB_AT}
    for out, slab, ws in ((grad, g_slab, g_ws), (delta, d_slab, d_ws), (new_m, nm_slab, nm_ws), (new_v, nv_slab, nv_ws)):
        out.update(_unpack_slab(slab, shapes))
        out["gmlp_w_s"] = ws.reshape(gmlp_w_s.shape)
    blk = GM_WIDTH // N_DEV
    g_ln = [lax.dynamic_slice(g_slab[SLAB_AT[n]:SLAB_AT[n] + 2].reshape(1, GM_WIDTH), (0, me * blk), (1, blk))
            for n in SMALL_SHARDED]
    ln_out = _adam_vecs(g_ln, [weights[n] for n in SMALL_SHARDED], [mom_m[n] for n in SMALL_SHARDED],
                        [mom_v[n] for n in SMALL_SHARDED])
    for n, g, (d, nm, nv) in zip(SMALL_SHARDED, g_ln, ln_out):
        grad[n], delta[n], new_m[n], new_v[n] = g, d, nm, nv

    adam_groups(slots_of("ffn1", (delta["hgrn_w_out"], d_slab)), ("ffn1_w_in", "ffn1_w_out"))

    for name in rows_already:
        for out in (grad, delta, new_m, new_v):
            out[name] = as_rows(out[name])
    loss = lax.psum(loss_part[0, 0], MESH_AXES)
    grad_x = dx.reshape(B_LOC, SEQ, D_MODEL)
    return (loss, grad_x, *[grad[n] for n in order], *[delta[n] for n in order],
            *[new_m[n] for n in order], *[new_v[n] for n in order])


def _step_local(x, mem, loss_target, get_weights, put_grads, mem_norm, lb_logits, ffn1_norm, mix_norm, hgrn_gnorm,
                gmlp_w_s, gmlp_b_s, ffn2_norm, final_norm):
    w_s = gmlp_w_s[0]
    b_st = gmlp_b_s[0].T

    xs = x.reshape(N_TOK, D_MODEL)
    mem2d = mem.reshape(B_LOC * MEM_LEN, D_MODEL)
    mem_g = mem_norm.reshape(1, D_MODEL)
    saved, full = [], {}
    memn = _rms_fwd(mem2d, mem_g, name="mem_norm_fwd")
    for i in range(2):
        xs, s_ffn1 = _ffn_fwd(xs, ffn1_norm[i:i + 1], "ffn1", i, full, get_weights)
        full.update(get_weights((i, "mix_in"), (xs,)))
        mixer = "hgrn" if i == 0 else "gmlp"
        hm, zm = _norm_mm(xs, mix_norm[i:i + 1], full[(f"{mixer}_w_in", 0)], swiglu=False, tm=1024, tn=1280, deps=full.pop("deps", ()),
                          name=f"l{i}_mix_in")
        kv = _mm(memn, full[("mem_w_kv", i)], tb=True, tm=512, tn=512, tk=D_MODEL, out_dtype=F32, name=f"l{i}_mem_kv")
        o_mem = _attn_fwd(zm, kv, name=f"l{i}_attn")
        if i == 0:
            cat, o_pre, s_all = _hgrn_fwd(zm, o_mem, lb_logits, hgrn_gnorm)
            mix_saved = (o_pre, s_all)
        else:
            cat = _gmlp_fwd(zm, o_mem, full["ln_g"], full["ln_b"], w_s, b_st)
            mix_saved = ()
        x_mix = xs
        full.update(get_weights((i, "mix_out"), (cat,)))
        xs = _mm(cat, full[(f"{mixer}_w_out", 0)], tm=512, tn=D_MODEL, tk=cat.shape[1], out_dtype=F32, res=xs,
                 deps=full.pop("deps", ()), name=f"l{i}_mix_out")
        xs, s_ffn2 = _ffn_fwd(xs, ffn2_norm[i:i + 1], "ffn2", i, full, get_weights)
        saved.append((s_ffn1, (x_mix, hm, kv, zm, cat, mix_saved), s_ffn2))

    dx, dx16, d_final, loss_part = _loss_head(xs, final_norm.reshape(1, D_MODEL), loss_target.reshape(N_TOK, D_MODEL))

    small = {"final_norm": [d_final]}
    d_ffn1, d_ffn2, d_mix = [None, None], [None, None], [None, None]
    dmemn = jnp.zeros((B_LOC * MEM_LEN, D_MODEL), F32)
    deps = ()
    for i in (1, 0):
        s_ffn1, (x_mix, hm, kv, zm, cat, mix_saved), s_ffn2 = saved[i]
        dx, dx16, d_ffn2[i], dw_in_t, dw_out = _ffn_bwd(
            dx, dx16, s_ffn2, ffn2_norm[i:i + 1], full[("ffn2_w_in", i)], full[("ffn2_w_out", i)], f"l{i}_ffn2", deps)
        deps = put_grads((i, "ffn2"), {("ffn2_w_in", i): dw_in_t, ("ffn2_w_out", i): dw_out})
        mixer = "hgrn" if i == 0 else "gmlp"
        w_in_t, w_out = full[(f"{mixer}_w_in", 0)], full[(f"{mixer}_w_out", 0)]
        width = cat.shape[1]
        g_mix = {}
        g_mix[(f"{mixer}_w_out", 0)] = _mm(cat, dx16, ta=True, tm=1024, tn=D_MODEL, tk=N_TOK, out_dtype=BF16,
                                           deps=deps, name=f"l{i}_mix_out_wgrad")
        dcat = _mm(dx16, w_out, tb=True, tm=1024, tn=width // 2, tk=D_MODEL, out_dtype=F32, name=f"l{i}_mix_out_dgrad")
        dq, dk, dv = _attn_bwd(zm, kv, dcat, do_off=width - XA_HEADS * XA_DIM, name=f"l{i}_attn_bwd")
        if i == 0:
            dzm, dlbl, dgn = _hgrn_bwd(zm, mix_saved[0], dcat, dq, mix_saved[1], lb_logits, hgrn_gnorm)
            small["lb_logits"], small["hgrn_gnorm"] = [dlbl], [dgn]
            deps = ()
        else:
            dzm, dws, dbt, dlng, dlnb = _gmlp_bwd(zm, dcat, dq, full["ln_g"], full["ln_b"], w_s, b_st)
            small["gmlp_b_s"], small["gmlp_ln_g"], small["gmlp_ln_b"] = [dbt.T], [dlng], [dlnb]
            deps = put_grads("w_s", dws)
        g_mix[(f"{mixer}_w_in", 0)] = _mm(dzm, hm, ta=True, tm=1024, tn=D_MODEL, tk=N_TOK, out_dtype=BF16, deps=deps,
                                          name=f"l{i}_mix_in_wgrad")
        dkv = jnp.concatenate([dk, dv], axis=1)
        g_mix[("mem_w_kv", i)] = _mm(dkv, memn, ta=True, tm=512, tn=D_MODEL, tk=B_LOC * MEM_LEN, out_dtype=BF16,
                                     name=f"l{i}_mem_kv_wgrad")
        deps = put_grads((i, "mix"), g_mix)
        dx, dx16, d_mix[i] = _dgrad_norm_bwd(dzm, w_in_t, x_mix, mix_norm[i:i + 1], dx, deps=deps,
                                             name=f"l{i}_mix_in_dgrad")
        dmemn = _mm(dkv, full[("mem_w_kv", i)], tm=B_LOC * MEM_LEN, tn=D_MODEL, tk=512, out_dtype=F32, res=dmemn,
                    name=f"l{i}_mem_kv_dgrad")
        dx, dx16, d_ffn1[i], dw_in_t, dw_out = _ffn_bwd(
            dx, dx16, s_ffn1, ffn1_norm[i:i + 1], full[("ffn1_w_in", i)], full[("ffn1_w_out", i)], f"l{i}_ffn1")
        deps = put_grads((i, "ffn1"), {("ffn1_w_in", i): dw_in_t, ("ffn1_w_out", i): dw_out})
    _, _, dmem_g = _rms_bwd(mem2d, mem_g, dmemn, dmemn, deps=deps, name="mem_norm_bwd")
    small.update(mem_norm=[dmem_g], ffn1_norm=d_ffn1, ffn2_norm=d_ffn2, mix_norm=d_mix)
    return dx, small, loss_part
```

```python
import functools
import math

import jax
import jax.numpy as jnp
from jax import lax
from jax.experimental import pallas as pl
from jax.experimental.pallas import tpu as pltpu

F32 = jnp.float32
BF16 = jnp.bfloat16

D_MODEL = 1024
SEQ = 2048
B_LOC = 2
N_TOK = B_LOC * SEQ
MEM_LEN = 256
N_DEV = 8
EPS = 1e-6
D_FF = 2816
HG_HEADS = 8
HG_DIM = 128
HG_CHUNK = 64
HG_NCHUNK = SEQ // HG_CHUNK
GM_CHUNK = 128
GM_GROUPS = 8
GM_WIDTH = 2048
GM_GDIM = GM_WIDTH // GM_GROUPS
XA_HEADS = 4
XA_DIM = 256
XA_OFF = 4096

ADAM_LR = 0.001
ADAM_B1 = 0.9
ADAM_B2 = 0.999
ADAM_EPS = 1e-08
ADAM_WD = 0.01
ADAM_STEP = 10

VMEM_LIMIT_BYTES = 56 * 1024 * 1024
MESH_AXES = ("x", "y", "c")

GROUPS = (
    ("ffn1_w_in", True, 2, 704),
    ("ffn1_w_out", False, 2, 352),
    ("mem_w_kv", True, 2, 256),
    ("hgrn_w_in", True, 1, 640),
    ("hgrn_w_out", False, 1, 256),
    ("gmlp_w_in", True, 1, 640),
    ("gmlp_w_out", False, 1, 384),
    ("ffn2_w_in", True, 2, 704),
    ("ffn2_w_out", False, 2, 352),
)
GROUP_LAYERS = {name: layers for name, _, layers, _ in GROUPS}


def _stage_pieces(layer, block):
    if block == "mix":
        mixer = "hgrn" if layer == 0 else "gmlp"
        return (("mem_w_kv", layer), (f"{mixer}_w_in", 0), (f"{mixer}_w_out", 0))
    return ((f"{block}_w_in", layer), (f"{block}_w_out", layer))


ANY_SPEC = pl.BlockSpec(memory_space=pl.ANY)
HBM_SPEC = pl.BlockSpec(memory_space=pltpu.HBM)
SEM_SPEC = pl.BlockSpec(memory_space=pltpu.SEMAPHORE)


def _cp(*sem):
    return pltpu.CompilerParams(dimension_semantics=sem, vmem_limit_bytes=VMEM_LIMIT_BYTES)


def _sigmoid(x):
    return 0.5 * jnp.tanh(0.5 * x) + 0.5


def _gelu_parts(x):
    cdf = 0.5 * (1.0 + lax.erf(x * (1.0 / math.sqrt(2.0))))
    pdf = jnp.exp(-0.5 * x * x) * (1.0 / math.sqrt(2.0 * math.pi))
    return x * cdf, cdf + x * pdf


def _mm(a, b, *, ta=False, tb=False, tm, tn, tk, out_dtype, res=None, scale=1.0, deps=(), name):
    m, k = (a.shape[1], a.shape[0]) if ta else a.shape
    n, kb = b.shape if tb else (b.shape[1], b.shape[0])
    assert k == kb and m % tm == 0 and n % tn == 0 and k % tk == 0, (name, a.shape, b.shape)
    nk = k // tk
    dn = (((0 if ta else 1,), (1 if tb else 0,)), ((), ()))
    n_in = 2 + (res is not None) + len(deps)

    def body(*refs):
        a_ref, b_ref = refs[:2]
        r_ref = refs[2] if res is not None else None
        o_ref, scr = refs[n_in], refs[n_in + 1:]
        p = lax.dot_general(a_ref[...].astype(BF16), b_ref[...].astype(BF16), dn, preferred_element_type=F32)

        def finish(acc):
            if scale != 1.0:
                acc = scale * acc
            if r_ref is not None:
                acc = r_ref[...] + acc
            o_ref[...] = acc.astype(out_dtype)

        if nk == 1:
            finish(p)
        else:
            acc_ref = scr[0]
            kk = pl.program_id(2)

            @pl.when(kk == 0)
            def _():
                acc_ref[...] = p

            @pl.when(kk > 0)
            def _():
                acc_ref[...] += p

            @pl.when(kk == nk - 1)
            def _():
                finish(acc_ref[...])

    a_spec = pl.BlockSpec((tk, tm), lambda i, j, kk: (kk, i)) if ta else pl.BlockSpec((tm, tk), lambda i, j, kk: (i, kk))
    b_mode = dict(pipeline_mode=pl.Buffered(1)) if n == tn and nk == 1 else {}
    if tb:
        b_spec = pl.BlockSpec((tn, tk), lambda i, j, kk: (j, kk), **b_mode)
    else:
        b_spec = pl.BlockSpec((tk, tn), lambda i, j, kk: (kk, j), **b_mode)
    o_spec = pl.BlockSpec((tm, tn), lambda i, j, kk: (i, j))
    in_specs = [a_spec, b_spec] + ([o_spec] if res is not None else []) + [ANY_SPEC] * len(deps)
    args = (a, b) + ((res,) if res is not None else ()) + tuple(deps)
    return pl.pallas_call(
        body,
        name=name,
        grid=(m // tm, n // tn, nk),
        in_specs=in_specs,
        out_specs=o_spec,
        out_shape=jax.ShapeDtypeStruct((m, n), out_dtype),
        scratch_shapes=[pltpu.VMEM((tm, tn), F32)] if nk > 1 else [],
        compiler_params=_cp("parallel", "parallel", "arbitrary"),
    )(*args)


def _rms_fwd(x, g, *, name, deps=(), tm=512):
    rows = x.shape[0]

    def body(x_ref, g_ref, *rest):
        o_ref = rest[len(deps)]
        xv = x_ref[...]
        r = lax.rsqrt(jnp.mean(xv * xv, axis=-1, keepdims=True) + EPS)
        o_ref[...] = (xv * r * g_ref[...]).astype(BF16)

    row = pl.BlockSpec((tm, D_MODEL), lambda i: (i, 0))
    return pl.pallas_call(
        body,
        name=name,
        grid=(rows // tm,),
        in_specs=[row, pl.BlockSpec((1, D_MODEL), lambda i: (0, 0))] + [ANY_SPEC] * len(deps),
        out_specs=row,
        out_shape=jax.ShapeDtypeStruct((rows, D_MODEL), BF16),
        compiler_params=_cp("parallel"),
    )(x, g, *deps)


def _rms_bwd(x, g, dh, dres, *, name, deps=(), tm=512):
    rows = x.shape[0]

    def body(x_ref, g_ref, dh_ref, dres_ref, *rest):
        dx_ref, dx16_ref, dg_ref = rest[len(deps):]
        xv = x_ref[...]
        r = lax.rsqrt(jnp.mean(xv * xv, axis=-1, keepdims=True) + EPS)
        xhat = xv * r
        dhv = dh_ref[...]
        part = jnp.sum(dhv * xhat, axis=0, keepdims=True)

        @pl.when(pl.program_id(0) == 0)
        def _():
            dg_ref[...] = part

        @pl.when(pl.program_id(0) > 0)
        def _():
            dg_ref[...] += part

        dxh = dhv * g_ref[...]
        dx = dres_ref[...] + r * (dxh - xhat * jnp.mean(dxh * xhat, axis=-1, keepdims=True))
        dx_ref[...] = dx
        dx16_ref[...] = dx.astype(BF16)

    row = pl.BlockSpec((tm, D_MODEL), lambda i: (i, 0))
    vec = pl.BlockSpec((1, D_MODEL), lambda i: (0, 0))
    return pl.pallas_call(
        body,
        name=name,
        grid=(rows // tm,),
        in_specs=[row, vec, row, row] + [ANY_SPEC] * len(deps),
        out_specs=[row, row, vec],
        out_shape=[jax.ShapeDtypeStruct((rows, D_MODEL), F32), jax.ShapeDtypeStruct((rows, D_MODEL), BF16),
                   jax.ShapeDtypeStruct((1, D_MODEL), F32)],
        compiler_params=_cp("arbitrary"),
    )(x, g, dh, dres, *deps)


_NT = (((1,), (1,)), ((), ()))
_TN = (((0,), (0,)), ((), ()))


def _norm_mm(x, g, w_t, *, swiglu, name, tm, tn, deps=()):
    rows = w_t.shape[0]
    half = rows // 2
    nj = (half if swiglu else rows) // tn
    nw = 2 if swiglu else 1
    nd = len(deps)

    def body(x_ref, g_ref, *rest):
        w_refs, outs = rest[:nw], rest[nw + nd:]
        h_ref, z_ref = outs[:2]

        @pl.when(pl.program_id(1) == 0)
        def _():
            xv = x_ref[...]
            r = lax.rsqrt(jnp.mean(xv * xv, axis=-1, keepdims=True) + EPS)
            h_ref[...] = (xv * r * g_ref[...]).astype(BF16)

        h = h_ref[...]
        if swiglu:
            gate = lax.dot_general(h, w_refs[0][...], _NT, preferred_element_type=F32)
            up = lax.dot_general(h, w_refs[1][...], _NT, preferred_element_type=F32)
            z_ref[0] = gate.astype(BF16)
            z_ref[1] = up.astype(BF16)
            outs[2][...] = (gate * _sigmoid(gate) * up).astype(BF16)
        else:
            z_ref[...] = lax.dot_general(h, w_refs[0][...], _NT, preferred_element_type=F32)

    row = pl.BlockSpec((tm, D_MODEL), lambda i, j: (i, 0))
    w_specs = [pl.BlockSpec((tn, D_MODEL), lambda i, j: (j, 0))]
    out_specs = [row]
    out_shape = [jax.ShapeDtypeStruct((N_TOK, D_MODEL), BF16)]
    if swiglu:
        w_specs.append(pl.BlockSpec((tn, D_MODEL), lambda i, j: (j + nj, 0)))
        out_specs += [pl.BlockSpec((2, tm, tn), lambda i, j: (0, i, j)), pl.BlockSpec((tm, tn), lambda i, j: (i, j))]
        out_shape += [jax.ShapeDtypeStruct((2, N_TOK, half), BF16), jax.ShapeDtypeStruct((N_TOK, half), BF16)]
    else:
        out_specs.append(pl.BlockSpec((tm, tn), lambda i, j: (i, j)))
        out_shape.append(jax.ShapeDtypeStruct((N_TOK, rows), F32))
    return pl.pallas_call(
        body,
        name=name,
        grid=(N_TOK // tm, nj),
        in_specs=[row, pl.BlockSpec((1, D_MODEL), lambda i, j: (0, 0))] + w_specs + [ANY_SPEC] * nd,
        out_specs=out_specs,
        out_shape=out_shape,
        compiler_params=_cp("parallel", "arbitrary"),
    )(x, g, *([w_t] * nw), *deps)


def _swiglu_dgrad(dy16, w_out, z, *, scale, name, tm=512, tn=1408):
    def body(dy_ref, w_ref, z_ref, dz_ref):
        da = lax.dot_general(dy_ref[...], w_ref[...], _NT, preferred_element_type=F32) * scale
        gate, up = z_ref[0].astype(F32), z_ref[1].astype(F32)
        s = _sigmoid(gate)
        dz_ref[0] = (da * up * (s * (1.0 + gate * (1.0 - s)))).astype(BF16)
        dz_ref[1] = (da * (gate * s)).astype(BF16)

    planes = pl.BlockSpec((2, tm, tn), lambda i, j: (0, i, j))
    return pl.pallas_call(
        body,
        name=name,
        grid=(N_TOK // tm, D_FF // tn),
        in_specs=[pl.BlockSpec((tm, D_MODEL), lambda i, j: (i, 0)), pl.BlockSpec((tn, D_MODEL), lambda i, j: (j, 0)), planes],
        out_specs=planes,
        out_shape=jax.ShapeDtypeStruct((2, N_TOK, D_FF), BF16),
        compiler_params=_cp("parallel", "parallel"),
    )(dy16, w_out, z)


def _planes_wgrad(dz, h, *, name, deps=(), tm=1408):
    per_plane = D_FF // tm

    def body(a_ref, b_ref, *rest):
        o_ref = rest[len(deps)]
        o_ref[...] = lax.dot_general(a_ref[...], b_ref[...], _TN, preferred_element_type=F32).astype(BF16)

    return pl.pallas_call(
        body,
        name=name,
        grid=(2 * per_plane,),
        in_specs=[pl.BlockSpec((None, N_TOK, tm),
                               lambda i: (jnp.where(i < per_plane, 0, 1), 0, jnp.where(i < per_plane, i, i - per_plane))),
                  pl.BlockSpec((N_TOK, D_MODEL), lambda i: (0, 0), pipeline_mode=pl.Buffered(1))] + [ANY_SPEC] * len(deps),
        out_specs=pl.BlockSpec((tm, D_MODEL), lambda i: (i, 0)),
        out_shape=jax.ShapeDtypeStruct((2 * D_FF, D_MODEL), BF16),
        compiler_params=_cp("parallel"),
    )(dz, h, *deps)


def _dgrad_norm_bwd(dz, w_t, x, g, dres, *, name, deps=(), tm=512):
    planes = dz.ndim == 3
    rows = w_t.shape[0]
    half = rows // 2
    nd = len(deps)

    def body(a_ref, b_ref, x_ref, g_ref, dres_ref, *rest):
        dx_ref, dx16_ref, dg_ref = rest[nd:]
        if planes:
            dh = jnp.dot(a_ref[0], b_ref[:half, :], preferred_element_type=F32) + jnp.dot(
                a_ref[1], b_ref[half:, :], preferred_element_type=F32)
        else:
            dh = jnp.dot(a_ref[...], b_ref[...], preferred_element_type=F32)
        xv = x_ref[...]
        r = lax.rsqrt(jnp.mean(xv * xv, axis=-1, keepdims=True) + EPS)
        xhat = xv * r
        part = jnp.sum(dh * xhat, axis=0, keepdims=True)

        @pl.when(pl.program_id(0) == 0)
        def _():
            dg_ref[...] = part

        @pl.when(pl.program_id(0) > 0)
        def _():
            dg_ref[...] += part

        dxh = dh * g_ref[...]
        dx = dres_ref[...] + r * (dxh - xhat * jnp.mean(dxh * xhat, axis=-1, keepdims=True))
        dx_ref[...] = dx
        dx16_ref[...] = dx.astype(BF16)

    a_spec = pl.BlockSpec((2, tm, half), lambda i: (0, i, 0)) if planes else pl.BlockSpec((tm, rows), lambda i: (i, 0))
    row = pl.BlockSpec((tm, D_MODEL), lambda i: (i, 0))
    vec = pl.BlockSpec((1, D_MODEL), lambda i: (0, 0))
    return pl.pallas_call(
        body,
        name=name,
        grid=(N_TOK // tm,),
        in_specs=[a_spec, pl.BlockSpec((rows, D_MODEL), lambda i: (0, 0), pipeline_mode=pl.Buffered(1)), row, vec, row]
        + [ANY_SPEC] * nd,
        out_specs=[row, row, vec],
        out_shape=[jax.ShapeDtypeStruct((N_TOK, D_MODEL), F32), jax.ShapeDtypeStruct((N_TOK, D_MODEL), BF16),
                   jax.ShapeDtypeStruct((1, D_MODEL), F32)],
        compiler_params=_cp("arbitrary"),
    )(dz, w_t, x, g, dres, *deps)


def _loss_head(x, g, target, *, tm=512):
    def body(x_ref, g_ref, t_ref, dx_ref, dx16_ref, dg_ref, loss_ref):
        xv = x_ref[...]
        gv = g_ref[...]
        r = lax.rsqrt(jnp.mean(xv * xv, axis=-1, keepdims=True) + EPS)
        xhat = xv * r
        err = xhat * gv - t_ref[...]
        loss_part = jnp.zeros((1, 128), F32) + 0.5 * jnp.sum(jnp.mean(err * err, axis=-1, keepdims=True))
        dy = err * (1.0 / D_MODEL)
        dg_part = jnp.sum(dy * xhat, axis=0, keepdims=True)

        @pl.when(pl.program_id(0) == 0)
        def _():
            dg_ref[...] = dg_part
            loss_ref[...] = loss_part

        @pl.when(pl.program_id(0) > 0)
        def _():
            dg_ref[...] += dg_part
            loss_ref[...] += loss_part

        dxh = dy * gv
        dx = r * (dxh - xhat * jnp.mean(dxh * xhat, axis=-1, keepdims=True))
        dx_ref[...] = dx
        dx16_ref[...] = dx.astype(BF16)

    row = pl.BlockSpec((tm, D_MODEL), lambda i: (i, 0))
    vec = pl.BlockSpec((1, D_MODEL), lambda i: (0, 0))
    return pl.pallas_call(
        body,
        name="loss_head",
        grid=(N_TOK // tm,),
        in_specs=[row, vec, row],
        out_specs=[row, row, vec, pl.BlockSpec((1, 128), lambda i: (0, 0))],
        out_shape=[
            jax.ShapeDtypeStruct((N_TOK, D_MODEL), F32),
            jax.ShapeDtypeStruct((N_TOK, D_MODEL), BF16),
            jax.ShapeDtypeStruct((1, D_MODEL), F32),
            jax.ShapeDtypeStruct((1, 128), F32),
        ],
        compiler_params=_cp("arbitrary"),
    )(x, g, target)


XA_TQ = 1024
XA_SCALE = XA_DIM ** -0.5


def _attn_probs(q16, k16):
    s = lax.dot_general(q16, k16, _NT, preferred_element_type=F32) * XA_SCALE
    e = jnp.exp(s - jnp.max(s, axis=-1, keepdims=True))
    return e / jnp.sum(e, axis=-1, keepdims=True)


def _attn_fwd(z, kv, *, name):
    nt = SEQ // XA_TQ

    def body(q_ref, k_ref, v_ref, o_ref):
        p = _attn_probs(q_ref[...].astype(BF16), k_ref[...].astype(BF16))
        o_ref[...] = jnp.dot(p.astype(BF16), v_ref[...].astype(BF16), preferred_element_type=F32).astype(BF16)

    return pl.pallas_call(
        body,
        name=name,
        grid=(B_LOC, XA_HEADS, nt),
        in_specs=[
            pl.BlockSpec((XA_TQ, XA_DIM), lambda b, h, t: (b * nt + t, XA_OFF // XA_DIM + h)),
            pl.BlockSpec((MEM_LEN, XA_DIM), lambda b, h, t: (b, h)),
            pl.BlockSpec((MEM_LEN, XA_DIM), lambda b, h, t: (b, XA_HEADS + h)),
        ],
        out_specs=pl.BlockSpec((XA_TQ, XA_DIM), lambda b, h, t: (b * nt + t, h)),
        out_shape=jax.ShapeDtypeStruct((N_TOK, XA_HEADS * XA_DIM), BF16),
        compiler_params=_cp("parallel", "parallel", "arbitrary"),
    )(z, kv, kv)


def _attn_bwd(z, kv, dcat, *, do_off, name):
    nt = SEQ // XA_TQ

    def body(q_ref, k_ref, v_ref, do_ref, dq_ref, dk_ref, dv_ref):
        q16 = q_ref[...].astype(BF16)
        k16 = k_ref[...].astype(BF16)
        v16 = v_ref[...].astype(BF16)
        do16 = do_ref[...].astype(BF16)
        p = _attn_probs(q16, k16)
        dv_part = lax.dot_general(p.astype(BF16), do16, _TN, preferred_element_type=F32)
        dp = lax.dot_general(do16, v16, _NT, preferred_element_type=F32)
        ds16 = (p * (dp - jnp.sum(dp * p, axis=-1, keepdims=True)) * XA_SCALE).astype(BF16)
        dq_ref[...] = jnp.dot(ds16, k16, preferred_element_type=F32).astype(BF16)
        dk_part = lax.dot_general(ds16, q16, _TN, preferred_element_type=F32)

        @pl.when(pl.program_id(2) == 0)
        def _():
            dk_ref[...] = dk_part
            dv_ref[...] = dv_part

        @pl.when(pl.program_id(2) > 0)
        def _():
            dk_ref[...] += dk_part
            dv_ref[...] += dv_part

    qspec = pl.BlockSpec((XA_TQ, XA_DIM), lambda b, h, t: (b * nt + t, XA_OFF // XA_DIM + h))
    kspec = lambda off: pl.BlockSpec((MEM_LEN, XA_DIM), lambda b, h, t: (b, off + h))
    return pl.pallas_call(
        body,
        name=name,
        grid=(B_LOC, XA_HEADS, nt),
        in_specs=[qspec, kspec(0), kspec(XA_HEADS),
                  pl.BlockSpec((XA_TQ, XA_DIM), lambda b, h, t: (b * nt + t, do_off // XA_DIM + h))],
        out_specs=[pl.BlockSpec((XA_TQ, XA_DIM), lambda b, h, t: (b * nt + t, h)), kspec(0), kspec(0)],
        out_shape=[
            jax.ShapeDtypeStruct((N_TOK, XA_HEADS * XA_DIM), BF16),
            jax.ShapeDtypeStruct((B_LOC * MEM_LEN, XA_HEADS * XA_DIM), F32),
            jax.ShapeDtypeStruct((B_LOC * MEM_LEN, XA_HEADS * XA_DIM), F32),
        ],
        compiler_params=_cp("parallel", "parallel", "arbitrary"),
    )(z, kv, kv, dcat)


def _tril(n):
    return lax.broadcasted_iota(jnp.int32, (n, n), 0) >= lax.broadcasted_iota(jnp.int32, (n, n), 1)


def _lower_bound(lbl):
    e = jnp.exp(lbl - jnp.max(lbl, axis=0, keepdims=True))
    p = e / jnp.sum(e, axis=0, keepdims=True)
    return p[0:1, :], p


def _hgrn_gates(zq, zf, lb, tril_f):
    sig = _sigmoid(zf)
    f = lb + (1.0 - lb) * sig
    kk = 1.0 - f
    sq = _sigmoid(zq)
    q = zq * sq
    b = jnp.dot(tril_f, jnp.log(f), preferred_element_type=F32, precision=lax.Precision.HIGHEST)
    bl = b[HG_CHUNK - 1:HG_CHUNK, :]
    return q, sq, sig, f, kk, b, bl


HG_TB = 512
HG_CPB = HG_TB // HG_CHUNK
HG_NT = SEQ // HG_TB
HG_WIDTH = HG_HEADS * HG_DIM


def _head(h, section=0):
    return slice(section * HG_WIDTH + h * HG_DIM, section * HG_WIDTH + (h + 1) * HG_DIM)


def _hgrn_fwd(z, o_mem, lb_logits, gnorm):
    def body(zq_ref, zf_ref, zi_ref, zg_ref, omem_ref, lbl_ref, gn_ref, o_ref, opre_ref, sall_ref, st_ref):
        lb, _ = _lower_bound(lbl_ref[...])
        gn = gn_ref[...]
        mask = _tril(HG_CHUNK)
        tril_f = mask.astype(F32)
        o_ref[:, HG_WIDTH:] = omem_ref[...]

        @pl.when(pl.program_id(1) == 0)
        def _():
            st_ref[...] = jnp.zeros_like(st_ref)

        def chunk(c, carry):
            rows = pl.ds(pl.multiple_of(c * HG_CHUNK, HG_CHUNK), HG_CHUNK)
            q, _, _, _, kk, b, bl = _hgrn_gates(zq_ref[rows, :], zf_ref[rows, :], lb, tril_f)
            v16 = zi_ref[rows, :].astype(BF16)
            qd16 = (q * jnp.exp(b)).astype(BF16)
            ki16 = (kk * jnp.exp(-b)).astype(BF16)
            kd16 = (kk * jnp.exp(bl - b)).astype(BF16)
            ebl = jnp.exp(bl)
            zg = zg_ref[rows, :]
            gate = zg * _sigmoid(zg)
            for h in range(HG_HEADS):
                sl = _head(h)
                a = jnp.where(mask, lax.dot_general(qd16[:, sl], ki16[:, sl], _NT, preferred_element_type=F32), 0.0)
                st = st_ref[h]
                sall_ref[0, h, c] = st
                o = jnp.dot(a.astype(BF16), v16[:, sl], preferred_element_type=F32) + lax.dot_general(
                    qd16[:, sl], st.astype(BF16), _NT, preferred_element_type=F32)
                st_ref[h] = st * ebl[:, sl] + lax.dot_general(v16[:, sl], kd16[:, sl], _TN, preferred_element_type=F32)
                opre_ref[rows, sl] = o
                r = lax.rsqrt(jnp.mean(o * o, axis=-1, keepdims=True) + EPS)
                o_ref[rows, sl] = ((o * r * gn) * gate[:, sl]).astype(BF16)
            return carry

        lax.fori_loop(0, HG_CPB, chunk, 0)

    zspec = lambda s: pl.BlockSpec((HG_TB, HG_WIDTH), lambda b, t: (b * HG_NT + t, s))
    return pl.pallas_call(
        body,
        name="hgrn_fwd",
        grid=(B_LOC, HG_NT),
        in_specs=[zspec(0), zspec(1), zspec(2), zspec(3), zspec(0),
                  pl.BlockSpec((3, HG_WIDTH), lambda b, t: (0, 0)), pl.BlockSpec((1, HG_DIM), lambda b, t: (0, 0))],
        out_specs=[pl.BlockSpec((HG_TB, 2 * HG_WIDTH), lambda b, t: (b * HG_NT + t, 0)), zspec(0),
                   pl.BlockSpec((1, HG_HEADS, HG_CPB, HG_DIM, HG_DIM), lambda b, t: (b, 0, t, 0, 0))],
        out_shape=[
            jax.ShapeDtypeStruct((N_TOK, 2 * HG_WIDTH), BF16),
            jax.ShapeDtypeStruct((N_TOK, HG_WIDTH), F32),
            jax.ShapeDtypeStruct((B_LOC, HG_HEADS, HG_NCHUNK, HG_DIM, HG_DIM), F32),
        ],
        scratch_shapes=[pltpu.VMEM((HG_HEADS, HG_DIM, HG_DIM), F32)],
        compiler_params=_cp("parallel", "arbitrary"),
    )(z, z, z, z, o_mem, lb_logits, gnorm)


def _hgrn_bwd(z, opre, dcat, dq_mem, sall, lb_logits, gnorm):
    def body(zq_ref, zf_ref, zi_ref, zg_ref, opre_ref, dout_ref, dqm_ref, sall_ref, lbl_ref, gn_ref,
             dz_ref, dlbl_ref, dgn_ref, dst_ref, dlb_ref, dgn_acc, db_ref, dkk_ref, dbl_ref):
        b_id, t_id = pl.program_id(0), pl.program_id(1)
        lb, p = _lower_bound(lbl_ref[...])
        gn = gn_ref[...]
        mask = _tril(HG_CHUNK)
        tril_f = mask.astype(F32)
        dz_ref[:, 4 * HG_WIDTH:] = dqm_ref[...]

        @pl.when(t_id == 0)
        def _():
            dst_ref[...] = jnp.zeros_like(dst_ref)
            dlb_ref[...] = jnp.zeros_like(dlb_ref)

        @pl.when((b_id == 0) & (t_id == 0))
        def _():
            dgn_acc[...] = jnp.zeros_like(dgn_acc)

        def chunk(i, carry):
            c = HG_CPB - 1 - i
            rows = pl.ds(pl.multiple_of(c * HG_CHUNK, HG_CHUNK), HG_CHUNK)
            zq, zg = zq_ref[rows, :], zg_ref[rows, :]
            q, sq, sig, f, kk, b, bl = _hgrn_gates(zq, zf_ref[rows, :], lb, tril_f)
            v16 = zi_ref[rows, :].astype(BF16)
            eb, enb, ebl_b, ebl = jnp.exp(b), jnp.exp(-b), jnp.exp(bl - b), jnp.exp(bl)
            qd, ki, kd = q * eb, kk * enb, kk * ebl_b
            qd16, ki16, kd16 = qd.astype(BF16), ki.astype(BF16), kd.astype(BF16)
            o_all = opre_ref[rows, :]
            dout = dout_ref[rows, :]
            sg = _sigmoid(zg)
            d_on_all = dout * (zg * sg)
            dgate = dout * (sg * (1.0 + zg * (1.0 - sg)))
            dq_scale = eb * (sq * (1.0 + zq * (1.0 - sq)))
            for h in range(HG_HEADS):
                sl = _head(h)
                o = o_all[:, sl]
                r = lax.rsqrt(jnp.mean(o * o, axis=-1, keepdims=True) + EPS)
                ohat = o * r
                d_on = d_on_all[:, sl]
                dz_ref[rows, _head(h, 3)] = (dgate[:, sl] * (ohat * gn)).astype(BF16)
                dgn_acc[...] += jnp.sum(d_on * ohat, axis=0, keepdims=True)
                dohat = d_on * gn
                do16 = (r * (dohat - ohat * jnp.mean(dohat * ohat, axis=-1, keepdims=True))).astype(BF16)
                st = sall_ref[0, h, c]
                dst = dst_ref[h]
                st16, dst16 = st.astype(BF16), dst.astype(BF16)
                qd_h, ki_h, kd_h, v_h = qd16[:, sl], ki16[:, sl], kd16[:, sl], v16[:, sl]
                a16 = jnp.where(mask, lax.dot_general(qd_h, ki_h, _NT, preferred_element_type=F32), 0.0).astype(BF16)
                da16 = jnp.where(mask, lax.dot_general(do16, v_h, _NT, preferred_element_type=F32), 0.0).astype(BF16)
                dv = lax.dot_general(a16, do16, _TN, preferred_element_type=F32) + lax.dot_general(
                    kd_h, dst16, _NT, preferred_element_type=F32)
                dqd = jnp.dot(da16, ki_h, preferred_element_type=F32) + jnp.dot(do16, st16, preferred_element_type=F32)
                dki = lax.dot_general(da16, qd_h, _TN, preferred_element_type=F32)
                dkd = jnp.dot(v_h, dst16, preferred_element_type=F32)
                dbl_ref[:, sl] = jnp.sum(dkd * kd[:, sl], axis=0, keepdims=True) + ebl[:, sl] * jnp.sum(
                    st * dst, axis=0, keepdims=True)
                dst_ref[h] = dst * ebl[:, sl] + lax.dot_general(do16, qd_h, _TN, preferred_element_type=F32)
                dz_ref[rows, _head(h, 2)] = dv.astype(BF16)
                dz_ref[rows, sl] = (dqd * dq_scale[:, sl]).astype(BF16)
                dkk_ref[:, sl] = dki * enb[:, sl] + dkd * ebl_b[:, sl]
                db_ref[:, sl] = dqd * qd[:, sl] - dki * ki[:, sl] - dkd * kd[:, sl]
            dlogf = lax.dot_general(tril_f, db_ref[...], _TN, preferred_element_type=F32,
                                    precision=lax.Precision.HIGHEST) + dbl_ref[...]
            df = dlogf / f - dkk_ref[...]
            dz_ref[rows, HG_WIDTH:2 * HG_WIDTH] = (df * (1.0 - lb) * sig * (1.0 - sig)).astype(BF16)
            dlb_ref[...] += jnp.sum(df * (1.0 - sig), axis=0, keepdims=True)
            return carry

        lax.fori_loop(0, HG_CPB, chunk, 0)

        @pl.when(t_id == HG_NT - 1)
        def _():
            row0 = (lax.broadcasted_iota(jnp.int32, (3, HG_WIDTH), 0) == 0).astype(F32)
            dlbl_part = dlb_ref[...] * lb * (row0 - p)

            @pl.when(b_id == 0)
            def _():
                dlbl_ref[...] = dlbl_part

            @pl.when(b_id > 0)
            def _():
                dlbl_ref[...] += dlbl_part

            dgn_ref[...] = dgn_acc[...]

    rev = lambda b, t: b * HG_NT + HG_NT - 1 - t
    zspec = lambda s: pl.BlockSpec((HG_TB, HG_WIDTH), lambda b, t: (rev(b, t), s))
    return pl.pallas_call(
        body,
        name="hgrn_bwd",
        grid=(B_LOC, HG_NT),
        in_specs=[zspec(0), zspec(1), zspec(2), zspec(3), zspec(0), zspec(0), zspec(0),
                  pl.BlockSpec((1, HG_HEADS, HG_CPB, HG_DIM, HG_DIM), lambda b, t: (b, 0, HG_NT - 1 - t, 0, 0)),
                  pl.BlockSpec((3, HG_WIDTH), lambda b, t: (0, 0)), pl.BlockSpec((1, HG_DIM), lambda b, t: (0, 0))],
        out_specs=[pl.BlockSpec((HG_TB, 5 * HG_WIDTH), lambda b, t: (rev(b, t), 0)),
                   pl.BlockSpec((3, HG_WIDTH), lambda b, t: (0, 0)), pl.BlockSpec((1, HG_DIM), lambda b, t: (0, 0))],
        out_shape=[jax.ShapeDtypeStruct((N_TOK, 5 * HG_WIDTH), BF16),
                   jax.ShapeDtypeStruct((3, HG_WIDTH), F32), jax.ShapeDtypeStruct((1, HG_DIM), F32)],
        scratch_shapes=[pltpu.VMEM((HG_HEADS, HG_DIM, HG_DIM), F32), pltpu.VMEM((1, HG_WIDTH), F32),
                        pltpu.VMEM((1, HG_DIM), F32), pltpu.VMEM((HG_CHUNK, HG_WIDTH), F32),
                        pltpu.VMEM((HG_CHUNK, HG_WIDTH), F32), pltpu.VMEM((1, HG_WIDTH), F32)],
        compiler_params=_cp("arbitrary", "arbitrary"),
    )(z, z, z, z, opre, dcat, dq_mem, sall, lb_logits, gnorm)


GM_TM = 256


def _gmlp_norm(zv, ln_g, ln_b):
    gv, dgelu = _gelu_parts(zv)
    xc = gv - jnp.mean(gv, axis=-1, keepdims=True)
    rstd = lax.rsqrt(jnp.mean(xc * xc, axis=-1, keepdims=True) + EPS)
    vhat = xc * rstd
    return vhat * ln_g + ln_b, vhat, rstd, dgelu


def _gmlp_specs():
    half = lambda j: pl.BlockSpec((GM_TM, GM_WIDTH), lambda i: (i, j))
    vec = pl.BlockSpec((1, GM_WIDTH), lambda i: (0, 0))
    w = pl.BlockSpec((GM_GROUPS, GM_CHUNK, GM_CHUNK), lambda i: (0, 0, 0))
    bt = pl.BlockSpec((GM_CHUNK, GM_GROUPS), lambda i: (0, 0))
    return half, vec, w, bt


def _gmlp_fwd(z, o_mem, ln_g, ln_b, w_s, b_st):
    def body(zu_ref, zv_ref, omem_ref, g_ref, b_ref, w_ref, bt_ref, o_ref):
        o_ref[:, GM_WIDTH:] = omem_ref[...]
        u, _ = _gelu_parts(zu_ref[...])
        v, _, _, _ = _gmlp_norm(zv_ref[...], g_ref[...], b_ref[...])
        v16 = v.astype(BF16)
        mask = _tril(GM_CHUNK)
        bt = bt_ref[...]
        for g in range(GM_GROUPS):
            wm16 = jnp.where(mask, w_ref[g], 0.0).astype(BF16)
            cols = slice(g * GM_GDIM, (g + 1) * GM_GDIM)
            for c in range(GM_TM // GM_CHUNK):
                rows = slice(c * GM_CHUNK, (c + 1) * GM_CHUNK)
                mixed = jnp.dot(wm16, v16[rows, cols], preferred_element_type=F32) + bt[:, g:g + 1]
                o_ref[rows, cols] = (u[rows, cols] * mixed).astype(BF16)

    half, vec, w, bt = _gmlp_specs()
    return pl.pallas_call(
        body,
        name="gmlp_fwd",
        grid=(N_TOK // GM_TM,),
        in_specs=[half(0), half(1), pl.BlockSpec((GM_TM, XA_HEADS * XA_DIM), lambda i: (i, 0)), vec, vec, w, bt],
        out_specs=pl.BlockSpec((GM_TM, GM_WIDTH + XA_HEADS * XA_DIM), lambda i: (i, 0)),
        out_shape=jax.ShapeDtypeStruct((N_TOK, GM_WIDTH + XA_HEADS * XA_DIM), BF16),
        compiler_params=_cp("parallel"),
    )(z, z, o_mem, ln_g, ln_b, w_s, b_st)


def _gmlp_bwd(z, dcat, dq_mem, ln_g, ln_b, w_s, b_st):
    def body(zu_ref, zv_ref, dout_ref, dqm_ref, g_ref, b_ref, w_ref, bt_ref,
             dz_ref, dw_ref, dbt_ref, dg_ref, db_ref, dv_ref):
        dz_ref[:, 2 * GM_WIDTH:] = dqm_ref[...]
        @pl.when(pl.program_id(0) == 0)
        def _():
            dw_ref[...] = jnp.zeros_like(dw_ref)
            dbt_ref[...] = jnp.zeros_like(dbt_ref)
            dg_ref[...] = jnp.zeros_like(dg_ref)
            db_ref[...] = jnp.zeros_like(db_ref)

        zu = zu_ref[...]
        u, du_dz = _gelu_parts(zu)
        ln_g = g_ref[...]
        v, vhat, rstd, dgv_dz = _gmlp_norm(zv_ref[...], ln_g, b_ref[...])
        v16 = v.astype(BF16)
        dout = dout_ref[...]
        dmixed = dout * u
        dm16 = dmixed.astype(BF16)
        mask = _tril(GM_CHUNK)
        bt = bt_ref[...]
        group_id = lax.broadcasted_iota(jnp.int32, (1, GM_GROUPS), 1)
        dbt = jnp.zeros((GM_CHUNK, GM_GROUPS), F32)
        for g in range(GM_GROUPS):
            wm16 = jnp.where(mask, w_ref[g], 0.0).astype(BF16)
            cols = slice(g * GM_GDIM, (g + 1) * GM_GDIM)
            dw = jnp.zeros((GM_CHUNK, GM_CHUNK), F32)
            dbt_g = jnp.zeros((GM_CHUNK, 1), F32)
            for c in range(GM_TM // GM_CHUNK):
                rows = slice(c * GM_CHUNK, (c + 1) * GM_CHUNK)
                mixed = jnp.dot(wm16, v16[rows, cols], preferred_element_type=F32) + bt[:, g:g + 1]
                dz_ref[rows, cols] = (dout[rows, cols] * mixed * du_dz[rows, cols]).astype(BF16)
                dw += lax.dot_general(dm16[rows, cols], v16[rows, cols], _NT, preferred_element_type=F32)
                dbt_g += jnp.sum(dmixed[rows, cols], axis=-1, keepdims=True)
                dv_ref[rows, cols] = lax.dot_general(wm16, dm16[rows, cols], _TN, preferred_element_type=F32)
            dw_ref[g] += jnp.where(mask, dw, 0.0)
            dbt = dbt + dbt_g * (group_id == g).astype(F32)
        dbt_ref[...] += dbt
        dv = dv_ref[...]
        dg_ref[...] += jnp.sum(dv * vhat, axis=0, keepdims=True)
        db_ref[...] += jnp.sum(dv, axis=0, keepdims=True)
        dvh = dv * ln_g
        dgv = rstd * (dvh - jnp.mean(dvh, axis=-1, keepdims=True) - vhat * jnp.mean(dvh * vhat, axis=-1, keepdims=True))
        dz_ref[:, GM_WIDTH:2 * GM_WIDTH] = (dgv * dgv_dz).astype(BF16)

    half, vec, w, bt = _gmlp_specs()
    dz_width = 2 * GM_WIDTH + XA_HEADS * XA_DIM
    return pl.pallas_call(
        body,
        name="gmlp_bwd",
        grid=(N_TOK // GM_TM,),
        in_specs=[half(0), half(1), half(0), pl.BlockSpec((GM_TM, XA_HEADS * XA_DIM), lambda i: (i, 0)), vec, vec, w, bt],
        out_specs=[pl.BlockSpec((GM_TM, dz_width), lambda i: (i, 0)), w, bt, vec, vec],
        out_shape=[jax.ShapeDtypeStruct((N_TOK, dz_width), BF16),
                   jax.ShapeDtypeStruct((GM_GROUPS, GM_CHUNK, GM_CHUNK), F32),
                   jax.ShapeDtypeStruct((GM_CHUNK, GM_GROUPS), F32),
                   jax.ShapeDtypeStruct((1, GM_WIDTH), F32), jax.ShapeDtypeStruct((1, GM_WIDTH), F32)],
        scratch_shapes=[pltpu.VMEM((GM_TM, GM_WIDTH), F32)],
        compiler_params=_cp("arbitrary"),
    )(z, z, dcat, dq_mem, ln_g, ln_b, w_s, b_st)


def _own_slot(shape):
    return pl.BlockSpec((None,) + tuple(shape), lambda i, me_ref: (me_ref[0],) + (0,) * len(shape))


def _place_rows(w, layer, cuts_columns, me, *, name):
    _, r, c = w.shape
    n = c if cuts_columns else r

    def body(me_ref, w_ref, o_ref):
        wv = w_ref[...]
        o_ref[...] = (wv.T if cuts_columns else wv).astype(BF16)

    return pl.pallas_call(
        body,
        name=name,
        grid_spec=pltpu.PrefetchScalarGridSpec(
            num_scalar_prefetch=1, grid=(1,),
            in_specs=[pl.BlockSpec((None, r, c), lambda i, me_ref: (layer, 0, 0))],
            out_specs=_own_slot((n, D_MODEL))),
        out_shape=jax.ShapeDtypeStruct((N_DEV, n, D_MODEL), BF16),
        compiler_params=_cp("arbitrary"),
    )(me, w)


def _place_ln(ln_g, ln_b, me):
    blk = ln_g.shape[1]

    def body(me_ref, g_ref, b_ref, o_ref):
        o_ref[...] = jnp.zeros_like(o_ref)
        o_ref[0:1, :] = g_ref[...]
        o_ref[1:2, :] = b_ref[...]

    vec = pl.BlockSpec((1, blk), lambda i, me_ref: (0, 0))
    return pl.pallas_call(
        body,
        name="place_ln",
        grid_spec=pltpu.PrefetchScalarGridSpec(
            num_scalar_prefetch=1, grid=(1,), in_specs=[vec, vec], out_specs=_own_slot((8, blk))),
        out_shape=jax.ShapeDtypeStruct((N_DEV, 8, blk), F32),
        compiler_params=_cp("arbitrary"),
    )(me, ln_g, ln_b)


def _place_slab(a, me, *, name):
    def body(me_ref, a_ref, o_ref):
        o_ref[...] = a_ref[...]

    return pl.pallas_call(
        body,
        name=name,
        grid_spec=pltpu.PrefetchScalarGridSpec(
            num_scalar_prefetch=1, grid=(1,),
            in_specs=[pl.BlockSpec(a.shape, lambda i, me_ref: (0, 0))], out_specs=_own_slot(a.shape)),
        out_shape=jax.ShapeDtypeStruct((N_DEV,) + a.shape, a.dtype),
        compiler_params=_cp("arbitrary"),
    )(me, a)


def _place_own(grads, me, *, name):
    k = len(grads)

    def body(me_ref, *refs):
        for src, dst in zip(refs[:k], refs[k:]):
            dst[...] = src[...]

    specs = [_own_slot(g.shape[1:]) for g in grads]
    return pl.pallas_call(
        body,
        name=name,
        grid_spec=pltpu.PrefetchScalarGridSpec(num_scalar_prefetch=1, grid=(1,), in_specs=specs, out_specs=specs),
        out_shape=[jax.ShapeDtypeStruct(g.shape, g.dtype) for g in grads],
        compiler_params=_cp("arbitrary"),
    )(me, *grads)


def _mesh_pos():
    x, y, c = (lax.axis_index(a) for a in MESH_AXES)
    return x, y, c, 4 * x + 2 * y + c


def _peer(x, y, c, r):
    px = 1 - x if r & 4 else x
    py = 1 - y if r & 2 else y
    pc = 1 - c if r & 1 else c
    return (px, py, pc), 4 * px + 2 * py + pc


RELATIONS = {"scatter": (1, 2, 3, 4, 5, 6, 7), "gather_all": (1, 2, 3, 4, 5, 6, 7), "gather_chips": (1, 2, 4, 6),
             "gather_sibling": (2, 4, 6)}


def _peer_copies(srcs, lands, send_sems, recv_sems, mode, waits):
    x, y, c, me = _mesh_pos()
    rel = RELATIONS[mode]
    pairs = []
    for ri, r in enumerate(rel):
        if mode == "gather_sibling":
            peer, _ = _peer(x, y, c, 1)
            _, sent_blk = _peer(x, y, c, r)
            _, got_blk = _peer(x, y, c, r ^ 1)
        else:
            peer, peer_blk = _peer(x, y, c, r)
            sent_blk, got_blk = (peer_blk if mode == "scatter" else me), peer_blk
        for k, (src, land) in enumerate(zip(srcs, lands)):
            idx = k * len(rel) + ri
            sems = dict(send_sem=send_sems.at[idx], recv_sem=recv_sems.at[idx], device_id=peer,
                        device_id_type=pl.DeviceIdType.MESH)
            dst_blk = sent_blk if mode == "gather_sibling" else me
            mine = pltpu.make_async_remote_copy(src_ref=src.at[sent_blk], dst_ref=land.at[dst_blk], **sems)
            theirs = pltpu.make_async_remote_copy(src_ref=src.at[sent_blk], dst_ref=land.at[got_blk], **sems) if waits else None
            pairs.append((mine, theirs))
    return pairs


DATAFLOW = pltpu.SideEffectType.DATAFLOW_SIDE_EFFECTING


def _in_hbm(a):
    return pltpu.with_memory_space_constraint(a, pltpu.HBM)


def _copies_start(srcs, lands, *, mode, name, deps=()):
    gather = mode != "scatter"
    arrs = list(lands) if gather else list(srcs) + list(lands)
    n, k, nd = len(arrs), len(lands), len(deps)

    def body(*refs):
        ins, send_sems, recv_sems, token = refs[:n], refs[n + nd], refs[n + nd + 1], refs[2 * n + nd + 2]
        src_refs, land_refs = (ins, ins) if gather else (ins[:k], ins[k:])
        for mine, _ in _peer_copies(src_refs, land_refs, send_sems, recv_sems, mode, waits=False):
            mine.start()
        token[...] = jnp.zeros_like(token)

    n_cp = k * len(RELATIONS[mode])
    return pl.pallas_call(
        body,
        name=name,
        in_specs=[HBM_SPEC] * n + [ANY_SPEC] * nd,
        out_specs=(SEM_SPEC, SEM_SPEC, *[HBM_SPEC] * n, pl.BlockSpec(memory_space=pltpu.VMEM)),
        out_shape=(pltpu.SemaphoreType.DMA((n_cp,)), pltpu.SemaphoreType.DMA((n_cp,)),
                   *[pltpu.HBM(a.shape, a.dtype) for a in arrs], jax.ShapeDtypeStruct((8, 128), F32)),
        input_output_aliases={i: 2 + i for i in range(n)},
        compiler_params=pltpu.CompilerParams(has_side_effects=DATAFLOW),
    )(*[_in_hbm(a) for a in arrs], *deps)


def _copies_wait(arrs, send_sems, recv_sems, after, *, n_lands, mode, name):
    n, k = len(arrs), n_lands
    gather = mode != "scatter"

    def body(*refs):
        ins, send_sems, recv_sems = refs[:n], refs[n], refs[n + 1]
        src_refs, land_refs = (ins, ins) if gather else (ins[:k], ins[k:])
        for mine, theirs in _peer_copies(src_refs, land_refs, send_sems, recv_sems, mode, waits=True):
            mine.wait_send()
            theirs.wait_recv()

    outs = pl.pallas_call(
        body,
        name=name,
        in_specs=[HBM_SPEC] * n + [SEM_SPEC, SEM_SPEC] + [ANY_SPEC] * len(after),
        out_specs=[HBM_SPEC] * n,
        out_shape=[pltpu.HBM(a.shape, a.dtype) for a in arrs],
        input_output_aliases={i: i for i in range(n)},
        compiler_params=pltpu.CompilerParams(has_side_effects=DATAFLOW),
    )(*arrs, send_sems, recv_sems, *after)
    return outs[n - k:]


def _adamw(w, g, m, v):
    m = ADAM_B1 * m + (1.0 - ADAM_B1) * g
    v = ADAM_B2 * v + (1.0 - ADAM_B2) * (g * g)
    m_hat = m / (1.0 - ADAM_B1 ** ADAM_STEP)
    v_hat = v / (1.0 - ADAM_B2 ** ADAM_STEP)
    return -ADAM_LR * (m_hat / (jnp.sqrt(v_hat) + ADAM_EPS) + ADAM_WD * w), m, v


ADAM_TC = 256


def _adam_big(slots, w, m, v, cuts_columns, *, name):
    layers, n, nj = len(slots), slots[0].shape[1], D_MODEL // ADAM_TC

    def body(*refs):
        s_refs = refs[:layers]
        w_ref, m_ref, v_ref, g_ref, d_ref, nm_ref, nv_ref, acc_ref = refs[layers:]
        for ll in range(layers):
            @pl.when(pl.program_id(0) == ll)
            def _(s_ref=s_refs[ll]):
                g = s_ref[0].astype(F32)
                for s in range(1, N_DEV):
                    g = g + s_ref[s].astype(F32)
                acc_ref[...] = g

        g = acc_ref[...].T if cuts_columns else acc_ref[...]
        g_ref[...] = g
        d_ref[...], nm_ref[...], nv_ref[...] = _adamw(w_ref[...], g, m_ref[...], v_ref[...])

    def slot_spec(ll):
        return pl.BlockSpec((N_DEV, n, ADAM_TC),
                            lambda l, j: (0, 0, jnp.where(l < ll, 0, jnp.where(l > ll, nj - 1, j))))

    if cuts_columns:
        w_spec = pl.BlockSpec((None, ADAM_TC, n), lambda l, j: (l, j, 0))
    else:
        w_spec = pl.BlockSpec((None, n, ADAM_TC), lambda l, j: (l, 0, j))
    return pl.pallas_call(
        body,
        name=name,
        grid=(layers, nj),
        in_specs=[slot_spec(ll) for ll in range(layers)] + [w_spec] * 3,
        out_specs=[w_spec] * 4,
        out_shape=[jax.ShapeDtypeStruct(w.shape, F32)] * 4,
        scratch_shapes=[pltpu.VMEM((n, ADAM_TC), F32)],
        compiler_params=_cp("arbitrary", "arbitrary"),
    )(*slots, w, m, v)


def _adam_slabs(slots, ws, ms, vs):
    n = len(slots)

    def body(*refs):
        ins, outs = refs[:4 * n], refs[4 * n:]
        for k in range(n):
            s_ref, w_ref, m_ref, v_ref = ins[k], ins[n + k], ins[2 * n + k], ins[3 * n + k]
            g = s_ref[0]
            for s in range(1, N_DEV):
                g = g + s_ref[s]
            outs[4 * k][...] = g
            outs[4 * k + 1][...], outs[4 * k + 2][...], outs[4 * k + 3][...] = _adamw(w_ref[...], g, m_ref[...], v_ref[...])

    res = pl.pallas_call(
        body,
        name="small_adamw",
        out_shape=[jax.ShapeDtypeStruct(w.shape, F32) for w in ws for _ in range(4)],
        compiler_params=pltpu.CompilerParams(vmem_limit_bytes=VMEM_LIMIT_BYTES),
    )(*slots, *ws, *ms, *vs)
    return [res[4 * k:4 * k + 4] for k in range(n)]


def _adam_vecs(gs, ws, ms, vs):
    n = len(gs)

    def body(*refs):
        ins, outs = refs[:4 * n], refs[4 * n:]
        for k in range(n):
            outs[3 * k][...], outs[3 * k + 1][...], outs[3 * k + 2][...] = _adamw(
                ins[n + k][...], ins[k][...], ins[2 * n + k][...], ins[3 * n + k][...])

    res = pl.pallas_call(
        body,
        name="ln_adamw",
        out_shape=[jax.ShapeDtypeStruct(w.shape, F32) for w in ws for _ in range(3)],
        compiler_params=pltpu.CompilerParams(vmem_limit_bytes=VMEM_LIMIT_BYTES),
    )(*gs, *ws, *ms, *vs)
    return [res[3 * k:3 * k + 3] for k in range(n)]


SLAB_AT = dict(mem_norm=0, lb_logits=1, ffn1_norm=4, mix_norm=6, hgrn_gnorm=8, gmlp_ln_g=9, gmlp_ln_b=11,
               gmlp_b_s=13, ffn2_norm=14, final_norm=16)
SLAB_ROWS = 24
SMALL_SHARDED = ("gmlp_ln_g", "gmlp_ln_b")


def _pack_slab(parts, *, name):
    flat, plan = [], []
    for pname, at in SLAB_AT.items():
        for a in parts.get(pname, ()):
            flat.append(a)
            plan.append((at, a.shape))
            at += max(1, a.shape[0] * a.shape[1] // D_MODEL)

    def body(*refs):
        o_ref = refs[-1]
        o_ref[...] = jnp.zeros_like(o_ref)
        for ref, (at, (r, w)) in zip(refs, plan):
            if w == D_MODEL or r == 1 and w < D_MODEL:
                o_ref[at:at + r, 0:w] = ref[...]
            elif w < D_MODEL:
                for j in range(r):
                    o_ref[at:at + 1, j * w:(j + 1) * w] = ref[j:j + 1, :]
            else:
                for j in range(w // D_MODEL):
                    o_ref[at + j:at + j + 1, :] = ref[:, j * D_MODEL:(j + 1) * D_MODEL]

    return pl.pallas_call(
        body,
        name=name,
        out_shape=jax.ShapeDtypeStruct((SLAB_ROWS, D_MODEL), F32),
        compiler_params=pltpu.CompilerParams(vmem_limit_bytes=VMEM_LIMIT_BYTES),
    )(*flat)


def _unpack_slab(slab, shapes):
    out = {}
    for pname, at in SLAB_AT.items():
        if pname in SMALL_SHARDED:
            continue
        size = math.prod(shapes[pname])
        rows = max(1, size // D_MODEL)
        out[pname] = slab[at:at + rows].reshape(-1)[:size].reshape(shapes[pname])
    return out


def _ffn_fwd(x, norm_g, block, layer, full, get_weights):
    tag = f"l{layer}_{block}"
    full.update(get_weights((layer, f"{block}_in"), (x,)))
    h, z, act = _norm_mm(x, norm_g, full[(f"{block}_w_in", layer)], swiglu=True, tm=512, tn=1408, deps=full.pop("deps", ()),
                         name=f"{tag}_in")
    full.update(get_weights((layer, f"{block}_out"), (act,)))
    y = _mm(act, full[(f"{block}_w_out", layer)], tm=512, tn=D_MODEL, tk=D_FF, out_dtype=F32, res=x, scale=0.5,
            deps=full.pop("deps", ()), name=f"{tag}_out")
    return y, (x, h, z, act)


def _ffn_bwd(dy, dy16, saved, norm_g, w_in_t, w_out, tag, deps=(), before_in_wgrad=None):
    x, h, z, act = saved
    dw_out = _mm(act, dy16, ta=True, tm=1408, tn=D_MODEL, tk=N_TOK, out_dtype=BF16, scale=0.5, deps=deps,
                 name=f"{tag}_out_wgrad")
    dz = _swiglu_dgrad(dy16, w_out, z, scale=0.5, name=f"{tag}_out_dgrad")
    if before_in_wgrad is None:
        dw_in_t = _planes_wgrad(dz, h, name=f"{tag}_in_wgrad")
        dx, dx16, dg = _dgrad_norm_bwd(dz, w_in_t, x, norm_g, dy, name=f"{tag}_in_dgrad")
    else:
        dx, dx16, dg = _dgrad_norm_bwd(dz, w_in_t, x, norm_g, dy, name=f"{tag}_in_dgrad")
        dw_in_t = _planes_wgrad(dz, h, deps=before_in_wgrad(dg), name=f"{tag}_in_wgrad")
    return dx, dx16, dg, dw_in_t, dw_out


def kernel(x, mem, mem_norm, lb_logits, ffn1_norm, ffn1_w_in, ffn1_w_out, mix_norm, mem_w_kv, hgrn_w_in, hgrn_gnorm, hgrn_w_out, gmlp_w_in, gmlp_ln_g, gmlp_ln_b, gmlp_w_s, gmlp_b_s, gmlp_w_out, ffn2_norm, ffn2_w_in, ffn2_w_out, final_norm, loss_target, m_mem_norm, m_lb_logits, m_ffn1_norm, m_ffn1_w_in, m_ffn1_w_out, m_mix_norm, m_mem_w_kv, m_hgrn_w_in, m_hgrn_gnorm, m_hgrn_w_out, m_gmlp_w_in, m_gmlp_ln_g, m_gmlp_ln_b, m_gmlp_w_s, m_gmlp_b_s, m_gmlp_w_out, m_ffn2_norm, m_ffn2_w_in, m_ffn2_w_out, m_final_norm, v_mem_norm, v_lb_logits, v_ffn1_norm, v_ffn1_w_in, v_ffn1_w_out, v_mix_norm, v_mem_w_kv, v_hgrn_w_in, v_hgrn_gnorm, v_hgrn_w_out, v_gmlp_w_in, v_gmlp_ln_g, v_gmlp_ln_b, v_gmlp_w_s, v_gmlp_b_s, v_gmlp_w_out, v_ffn2_norm, v_ffn2_w_in, v_ffn2_w_out, v_final_norm):
    weights = dict(mem_norm=mem_norm, lb_logits=lb_logits, ffn1_norm=ffn1_norm, ffn1_w_in=ffn1_w_in, ffn1_w_out=ffn1_w_out, mix_norm=mix_norm, mem_w_kv=mem_w_kv, hgrn_w_in=hgrn_w_in, hgrn_gnorm=hgrn_gnorm, hgrn_w_out=hgrn_w_out, gmlp_w_in=gmlp_w_in, gmlp_ln_g=gmlp_ln_g, gmlp_ln_b=gmlp_ln_b, gmlp_w_s=gmlp_w_s, gmlp_b_s=gmlp_b_s, gmlp_w_out=gmlp_w_out, ffn2_norm=ffn2_norm, ffn2_w_in=ffn2_w_in, ffn2_w_out=ffn2_w_out, final_norm=final_norm)
    mom_m = dict(mem_norm=m_mem_norm, lb_logits=m_lb_logits, ffn1_norm=m_ffn1_norm, ffn1_w_in=m_ffn1_w_in, ffn1_w_out=m_ffn1_w_out, mix_norm=m_mix_norm, mem_w_kv=m_mem_w_kv, hgrn_w_in=m_hgrn_w_in, hgrn_gnorm=m_hgrn_gnorm, hgrn_w_out=m_hgrn_w_out, gmlp_w_in=m_gmlp_w_in, gmlp_ln_g=m_gmlp_ln_g, gmlp_ln_b=m_gmlp_ln_b, gmlp_w_s=m_gmlp_w_s, gmlp_b_s=m_gmlp_b_s, gmlp_w_out=m_gmlp_w_out, ffn2_norm=m_ffn2_norm, ffn2_w_in=m_ffn2_w_in, ffn2_w_out=m_ffn2_w_out, final_norm=m_final_norm)
    mom_v = dict(mem_norm=v_mem_norm, lb_logits=v_lb_logits, ffn1_norm=v_ffn1_norm, ffn1_w_in=v_ffn1_w_in, ffn1_w_out=v_ffn1_w_out, mix_norm=v_mix_norm, mem_w_kv=v_mem_w_kv, hgrn_w_in=v_hgrn_w_in, hgrn_gnorm=v_hgrn_gnorm, hgrn_w_out=v_hgrn_w_out, gmlp_w_in=v_gmlp_w_in, gmlp_ln_g=v_gmlp_ln_g, gmlp_ln_b=v_gmlp_ln_b, gmlp_w_s=v_gmlp_w_s, gmlp_b_s=v_gmlp_b_s, gmlp_w_out=v_gmlp_w_out, ffn2_norm=v_ffn2_norm, ffn2_w_in=v_ffn2_w_in, ffn2_w_out=v_ffn2_w_out, final_norm=v_final_norm)
    order = list(weights)
    _, _, _, me = _mesh_pos()
    me_arr = jnp.reshape(me, (1,)).astype(jnp.int32)
    cuts = {name: c for name, c, _, _ in GROUPS}
    rows_already = tuple(name for name, c, _, n in GROUPS if c and n % 128)
    as_rows = lambda a: jnp.transpose(a, (0, 2, 1))
    for name in rows_already:
        weights[name], mom_m[name], mom_v[name] = as_rows(weights[name]), as_rows(mom_m[name]), as_rows(mom_v[name])
        cuts[name] = False

    mix1 = (("mem_w_kv", 1), ("gmlp_w_in", 0), ("gmlp_w_out", 0))
    gather_plan = (
        ((0, "ffn1_in"), (("ffn1_w_in", 0),)),
        ((0, "ffn1_out"), (("ffn1_w_out", 0),)),
        ((0, "mix_in"), _stage_pieces(0, "mix")),
        ((0, "ffn2_in"), _stage_pieces(0, "ffn2")),
        ((1, "ffn1_in"), _stage_pieces(1, "ffn1")),
        ((1, "mix_in"), mix1),
        ((1, "ffn2_in"), _stage_pieces(1, "ffn2")),
    )
    stage_of = {use: k for k, (use, _) in enumerate(gather_plan)}
    in_flight = {}

    def start_chips(k, deps):
        pieces = gather_plan[k][1]
        lands = [_place_rows(weights[name], l, cuts[name], me_arr, name=f"place_{name}_{l}") for name, l in pieces]
        if pieces is mix1:
            lands.append(_place_ln(gmlp_ln_g, gmlp_ln_b, me_arr))
        send_sems, recv_sems, *thru, token = _copies_start(lands, lands, mode="gather_chips", deps=deps,
                                                           name=f"gather{k}_chips_start")
        in_flight[k] = (thru, send_sems, recv_sems)
        return token

    def pass_to_sibling(k, after):
        thru, send_sems, recv_sems = in_flight[k]
        outs = _copies_wait(thru, send_sems, recv_sems, after, n_lands=len(thru), mode="gather_chips",
                            name=f"gather{k}_chips_wait")
        send_sems, recv_sems, *thru, token = _copies_start(outs, outs, mode="gather_sibling",
                                                           name=f"gather{k}_sibling_start")
        in_flight[k] = (thru, send_sems, recv_sems)
        return token, token

    start_chips(0, ())
    points = [(i, p) for i in (0, 1) for p in ("ffn1_in", "ffn1_out", "mix_in", "mix_out", "ffn2_in", "ffn2_out")]
    pass_at = {j: points[points.index(use) - 1] for j, (use, _) in enumerate(gather_plan) if j}

    def get_weights(use, after):
        tokens, w = [], {}
        k = stage_of.get(use)
        if k == 0:
            token, landed = pass_to_sibling(0, after)
            tokens += [token, start_chips(1, (landed,))]
        if k is not None:
            thru, send_sems, recv_sems = in_flight[k]
            outs = _copies_wait(thru, send_sems, recv_sems, after, n_lands=len(thru), mode="gather_sibling",
                                name=f"gather{k}_sibling_wait")
            after = (outs[0],)
            pieces = gather_plan[k][1]
            w = {p: o.reshape(N_DEV * o.shape[1], D_MODEL) for p, o in zip(pieces, outs)}
            if pieces is mix1:
                w["ln_g"] = outs[-1][:, 0, :].reshape(1, GM_WIDTH)
                w["ln_b"] = outs[-1][:, 1, :].reshape(1, GM_WIDTH)
        for j, at in pass_at.items():
            if at == use:
                token, landed = pass_to_sibling(j, after)
                tokens.append(token)
                if j + 1 < len(gather_plan):
                    tokens.append(start_chips(j + 1, (landed,)))
        w["deps"] = tuple(tokens)
        return w

    scatter = {}

    def put_grads(st, grads):
        if st in ("w_s", "small"):
            slab = grads.reshape(GM_GROUPS * GM_CHUNK, GM_CHUNK) if st == "w_s" else _pack_slab(grads, name="pack_small_grads")
            land = _place_slab(slab, me_arr, name=f"{st}_place")
            send_sems, recv_sems, *thru, token = _copies_start([land], [land], mode="gather_all", name=f"{st}_start")
            scatter[st] = (thru, send_sems, recv_sems)
            return (token,)
        views = [grads[p].reshape(N_DEV, -1, D_MODEL) for p in _stage_pieces(*st)]
        recv = _place_own(views, me_arr, name=f"scatter_place_l{st[0]}_{st[1]}")
        send_sems, recv_sems, *thru, token = _copies_start(views, recv, mode="scatter",
                                                           name=f"scatter_start_l{st[0]}_{st[1]}")
        scatter[st] = (thru, send_sems, recv_sems)
        return (token,)

    dx, loss_part = _step_local(
        x, mem, loss_target, get_weights, put_grads, mem_norm, lb_logits, ffn1_norm, mix_norm, hgrn_gnorm,
        gmlp_w_s, gmlp_b_s, ffn2_norm, final_norm)

    def slots_of(blk, after):
        slots = {}
        for i in (1, 0):
            thru, send_sems, recv_sems = scatter[(i, blk)]
            outs = _copies_wait(thru, send_sems, recv_sems, after, n_lands=len(thru) // 2, mode="scatter",
                                name=f"scatter_wait_l{i}_{blk}")
            slots.update(zip(_stage_pieces(i, blk), outs))
        return slots

    grad, delta, new_m, new_v = {}, {}, {}, {}

    def adam_groups(slots, names):
        for name in names:
            layers = GROUP_LAYERS[name]
            grad[name], delta[name], new_m[name], new_v[name] = _adam_big(
                [slots[(name, l)] for l in range(layers)], weights[name], mom_m[name], mom_v[name], cuts[name],
                name=f"{name}_adamw")

    adam_groups(slots_of("ffn2", (dx,)), ("ffn2_w_in", "ffn2_w_out"))
    adam_groups(slots_of("mix", (delta["ffn2_w_out"],)),
                ("mem_w_kv", "gmlp_w_in", "gmlp_w_out", "hgrn_w_in", "hgrn_w_out"))

    def small_parts(src):
        parts = {n: [src[n].reshape(-1, src[n].shape[-1])] for n in SLAB_AT if n not in SMALL_SHARDED}
        return parts

    w_s_rows = lambda a: a.reshape(GM_GROUPS * GM_CHUNK, GM_CHUNK)
    small_done = (delta["hgrn_w_out"],)
    (slab_slots,) = _copies_wait(*scatter["small"], small_done, n_lands=1, mode="gather_all", name="small_wait")
    (ws_slots,) = _copies_wait(*scatter["w_s"], small_done, n_lands=1, mode="gather_all", name="w_s_wait")
    (g_slab, d_slab, nm_slab, nv_slab), (g_ws, d_ws, nm_ws, nv_ws) = _adam_slabs(
        [slab_slots, ws_slots],
        [_pack_slab(small_parts(weights), name="pack_small_w"), w_s_rows(gmlp_w_s)],
        [_pack_slab(small_parts(mom_m), name="pack_small_m"), w_s_rows(m_gmlp_w_s)],
        [_pack_slab(small_parts(mom_v), name="pack_small_v"), w_s_rows(v_gmlp_w_s)])
    shapes = {n: weights[n].shape for n in SLAB_AT}
    for out, slab, ws in ((grad, g_slab, g_ws), (delta, d_slab, d_ws), (new_m, nm_slab, nm_ws), (new_v, nv_slab, nv_ws)):
        out.update(_unpack_slab(slab, shapes))
        out["gmlp_w_s"] = ws.reshape(gmlp_w_s.shape)
    blk = GM_WIDTH // N_DEV
    g_ln = [lax.dynamic_slice(g_slab[SLAB_AT[n]:SLAB_AT[n] + 2].reshape(1, GM_WIDTH), (0, me * blk), (1, blk))
            for n in SMALL_SHARDED]
    ln_out = _adam_vecs(g_ln, [weights[n] for n in SMALL_SHARDED], [mom_m[n] for n in SMALL_SHARDED],
                        [mom_v[n] for n in SMALL_SHARDED])
    for n, g, (d, nm, nv) in zip(SMALL_SHARDED, g_ln, ln_out):
        grad[n], delta[n], new_m[n], new_v[n] = g, d, nm, nv

    adam_groups(slots_of("ffn1", (delta["hgrn_w_out"], d_slab)), ("ffn1_w_in", "ffn1_w_out"))

    for name in rows_already:
        for out in (grad, delta, new_m, new_v):
            out[name] = as_rows(out[name])
    loss = lax.psum(loss_part[0, 0], MESH_AXES)
    grad_x = dx.reshape(B_LOC, SEQ, D_MODEL)
    return (loss, grad_x, *[grad[n] for n in order], *[delta[n] for n in order],
            *[new_m[n] for n in order], *[new_v[n] for n in order])


def _step_local(x, mem, loss_target, get_weights, put_grads, mem_norm, lb_logits, ffn1_norm, mix_norm, hgrn_gnorm,
                gmlp_w_s, gmlp_b_s, ffn2_norm, final_norm):
    w_s = gmlp_w_s[0]
    b_st = gmlp_b_s[0].T

    xs = x.reshape(N_TOK, D_MODEL)
    mem2d = mem.reshape(B_LOC * MEM_LEN, D_MODEL)
    mem_g = mem_norm.reshape(1, D_MODEL)
    saved, full = [], {}
    memn = _rms_fwd(mem2d, mem_g, name="mem_norm_fwd")
    for i in range(2):
        xs, s_ffn1 = _ffn_fwd(xs, ffn1_norm[i:i + 1], "ffn1", i, full, get_weights)
        full.update(get_weights((i, "mix_in"), (xs,)))
        mixer = "hgrn" if i == 0 else "gmlp"
        hm, zm = _norm_mm(xs, mix_norm[i:i + 1], full[(f"{mixer}_w_in", 0)], swiglu=False, tm=1024, tn=1280, deps=full.pop("deps", ()),
                          name=f"l{i}_mix_in")
        kv = _mm(memn, full[("mem_w_kv", i)], tb=True, tm=512, tn=512, tk=D_MODEL, out_dtype=F32, name=f"l{i}_mem_kv")
        o_mem = _attn_fwd(zm, kv, name=f"l{i}_attn")
        if i == 0:
            cat, o_pre, s_all = _hgrn_fwd(zm, o_mem, lb_logits, hgrn_gnorm)
            mix_saved = (o_pre, s_all)
        else:
            cat = _gmlp_fwd(zm, o_mem, full["ln_g"], full["ln_b"], w_s, b_st)
            mix_saved = ()
        x_mix = xs
        full.update(get_weights((i, "mix_out"), (cat,)))
        xs = _mm(cat, full[(f"{mixer}_w_out", 0)], tm=512, tn=D_MODEL, tk=cat.shape[1], out_dtype=F32, res=xs,
                 deps=full.pop("deps", ()), name=f"l{i}_mix_out")
        xs, s_ffn2 = _ffn_fwd(xs, ffn2_norm[i:i + 1], "ffn2", i, full, get_weights)
        saved.append((s_ffn1, (x_mix, hm, kv, zm, cat, mix_saved), s_ffn2))

    dx, dx16, d_final, loss_part = _loss_head(xs, final_norm.reshape(1, D_MODEL), loss_target.reshape(N_TOK, D_MODEL))

    small = {"final_norm": [d_final]}
    d_ffn1, d_ffn2, d_mix = [None, None], [None, None], [None, None]
    dmemn = jnp.zeros((B_LOC * MEM_LEN, D_MODEL), F32)
    deps = ()
    for i in (1, 0):
        s_ffn1, (x_mix, hm, kv, zm, cat, mix_saved), s_ffn2 = saved[i]
        dx, dx16, d_ffn2[i], dw_in_t, dw_out = _ffn_bwd(
            dx, dx16, s_ffn2, ffn2_norm[i:i + 1], full[("ffn2_w_in", i)], full[("ffn2_w_out", i)], f"l{i}_ffn2", deps)
        deps = put_grads((i, "ffn2"), {("ffn2_w_in", i): dw_in_t, ("ffn2_w_out", i): dw_out})
        mixer = "hgrn" if i == 0 else "gmlp"
        w_in_t, w_out = full[(f"{mixer}_w_in", 0)], full[(f"{mixer}_w_out", 0)]
        width = cat.shape[1]
        g_mix = {}
        g_mix[(f"{mixer}_w_out", 0)] = _mm(cat, dx16, ta=True, tm=1024, tn=D_MODEL, tk=N_TOK, out_dtype=BF16,
                                           deps=deps, name=f"l{i}_mix_out_wgrad")
        dcat = _mm(dx16, w_out, tb=True, tm=1024, tn=width // 2, tk=D_MODEL, out_dtype=F32, name=f"l{i}_mix_out_dgrad")
        dq, dk, dv = _attn_bwd(zm, kv, dcat, do_off=width - XA_HEADS * XA_DIM, name=f"l{i}_attn_bwd")
        if i == 0:
            dzm, dlbl, dgn = _hgrn_bwd(zm, mix_saved[0], dcat, dq, mix_saved[1], lb_logits, hgrn_gnorm)
            small["lb_logits"], small["hgrn_gnorm"] = [dlbl], [dgn]
            deps = ()
        else:
            dzm, dws, dbt, dlng, dlnb = _gmlp_bwd(zm, dcat, dq, full["ln_g"], full["ln_b"], w_s, b_st)
            small["gmlp_b_s"], small["gmlp_ln_g"], small["gmlp_ln_b"] = [dbt.T], [dlng], [dlnb]
            deps = put_grads("w_s", dws)
        g_mix[(f"{mixer}_w_in", 0)] = _mm(dzm, hm, ta=True, tm=1024, tn=D_MODEL, tk=N_TOK, out_dtype=BF16, deps=deps,
                                          name=f"l{i}_mix_in_wgrad")
        dkv = jnp.concatenate([dk, dv], axis=1)
        g_mix[("mem_w_kv", i)] = _mm(dkv, memn, ta=True, tm=512, tn=D_MODEL, tk=B_LOC * MEM_LEN, out_dtype=BF16,
                                     name=f"l{i}_mem_kv_wgrad")
        deps = put_grads((i, "mix"), g_mix)
        dx, dx16, d_mix[i] = _dgrad_norm_bwd(dzm, w_in_t, x_mix, mix_norm[i:i + 1], dx, deps=deps,
                                             name=f"l{i}_mix_in_dgrad")
        dmemn = _mm(dkv, full[("mem_w_kv", i)], tm=B_LOC * MEM_LEN, tn=D_MODEL, tk=512, out_dtype=F32, res=dmemn,
                    name=f"l{i}_mem_kv_dgrad")
        def send_small(dg, i=i, dmemn=dmemn):
            d_ffn1[i] = dg
            _, _, dmem_g = _rms_bwd(mem2d, mem_g, dmemn, dmemn, name="mem_norm_bwd")
            small.update(mem_norm=[dmem_g], ffn1_norm=d_ffn1, ffn2_norm=d_ffn2, mix_norm=d_mix)
            return put_grads("small", small)

        dx, dx16, d_ffn1[i], dw_in_t, dw_out = _ffn_bwd(
            dx, dx16, s_ffn1, ffn1_norm[i:i + 1], full[("ffn1_w_in", i)], full[("ffn1_w_out", i)], f"l{i}_ffn1",
            before_in_wgrad=send_small if i == 0 else None)
        deps = put_grads((i, "ffn1"), {("ffn1_w_in", i): dw_in_t, ("ffn1_w_out", i): dw_out})
    return dx, loss_part
```

```python
import functools
import math

import jax
import jax.numpy as jnp
from jax import lax
from jax.experimental import pallas as pl
from jax.experimental.pallas import tpu as pltpu

F32 = jnp.float32
BF16 = jnp.bfloat16

D_MODEL = 1024
SEQ = 2048
B_LOC = 2
N_TOK = B_LOC * SEQ
MEM_LEN = 256
N_DEV = 8
EPS = 1e-6
D_FF = 2816
HG_HEADS = 8
HG_DIM = 128
HG_CHUNK = 64
HG_NCHUNK = SEQ // HG_CHUNK
GM_CHUNK = 128
GM_GROUPS = 8
GM_WIDTH = 2048
GM_GDIM = GM_WIDTH // GM_GROUPS
XA_HEADS = 4
XA_DIM = 256
XA_OFF = 4096

ADAM_LR = 0.001
ADAM_B1 = 0.9
ADAM_B2 = 0.999
ADAM_EPS = 1e-08
ADAM_WD = 0.01
ADAM_STEP = 10

VMEM_LIMIT_BYTES = 56 * 1024 * 1024
MESH_AXES = ("x", "y", "c")

GROUPS = (
    ("ffn1_w_in", True, 2, 704),
    ("ffn1_w_out", False, 2, 352),
    ("mem_w_kv", True, 2, 256),
    ("hgrn_w_in", True, 1, 640),
    ("hgrn_w_out", False, 1, 256),
    ("gmlp_w_in", True, 1, 640),
    ("gmlp_w_out", False, 1, 384),
    ("ffn2_w_in", True, 2, 704),
    ("ffn2_w_out", False, 2, 352),
)
GROUP_LAYERS = {name: layers for name, _, layers, _ in GROUPS}


def _stage_pieces(layer, block):
    if block == "mix":
        mixer = "hgrn" if layer == 0 else "gmlp"
        return (("mem_w_kv", layer), (f"{mixer}_w_in", 0), (f"{mixer}_w_out", 0))
    return ((f"{block}_w_in", layer), (f"{block}_w_out", layer))


ANY_SPEC = pl.BlockSpec(memory_space=pl.ANY)
HBM_SPEC = pl.BlockSpec(memory_space=pltpu.HBM)
SEM_SPEC = pl.BlockSpec(memory_space=pltpu.SEMAPHORE)


def _cp(*sem):
    return pltpu.CompilerParams(dimension_semantics=sem, vmem_limit_bytes=VMEM_LIMIT_BYTES)


def _sigmoid(x):
    return 0.5 * jnp.tanh(0.5 * x) + 0.5


def _gelu_parts(x):
    cdf = 0.5 * (1.0 + lax.erf(x * (1.0 / math.sqrt(2.0))))
    pdf = jnp.exp(-0.5 * x * x) * (1.0 / math.sqrt(2.0 * math.pi))
    return x * cdf, cdf + x * pdf


def _mm(a, b, *, ta=False, tb=False, tm, tn, tk, out_dtype, res=None, scale=1.0, deps=(), name):
    m, k = (a.shape[1], a.shape[0]) if ta else a.shape
    n, kb = b.shape if tb else (b.shape[1], b.shape[0])
    assert k == kb and m % tm == 0 and n % tn == 0 and k % tk == 0, (name, a.shape, b.shape)
    nk = k // tk
    dn = (((0 if ta else 1,), (1 if tb else 0,)), ((), ()))
    n_in = 2 + (res is not None) + len(deps)

    def body(*refs):
        a_ref, b_ref = refs[:2]
        r_ref = refs[2] if res is not None else None
        o_ref, scr = refs[n_in], refs[n_in + 1:]
        p = lax.dot_general(a_ref[...].astype(BF16), b_ref[...].astype(BF16), dn, preferred_element_type=F32)

        def finish(acc):
            if scale != 1.0:
                acc = scale * acc
            if r_ref is not None:
                acc = r_ref[...] + acc
            o_ref[...] = acc.astype(out_dtype)

        if nk == 1:
            finish(p)
        else:
            acc_ref = scr[0]
            kk = pl.program_id(2)

            @pl.when(kk == 0)
            def _():
                acc_ref[...] = p

            @pl.when(kk > 0)
            def _():
                acc_ref[...] += p

            @pl.when(kk == nk - 1)
            def _():
                finish(acc_ref[...])

    a_spec = pl.BlockSpec((tk, tm), lambda i, j, kk: (kk, i)) if ta else pl.BlockSpec((tm, tk), lambda i, j, kk: (i, kk))
    b_mode = dict(pipeline_mode=pl.Buffered(1)) if n == tn and nk == 1 else {}
    if tb:
        b_spec = pl.BlockSpec((tn, tk), lambda i, j, kk: (j, kk), **b_mode)
    else:
        b_spec = pl.BlockSpec((tk, tn), lambda i, j, kk: (kk, j), **b_mode)
    o_spec = pl.BlockSpec((tm, tn), lambda i, j, kk: (i, j))
    in_specs = [a_spec, b_spec] + ([o_spec] if res is not None else []) + [ANY_SPEC] * len(deps)
    args = (a, b) + ((res,) if res is not None else ()) + tuple(deps)
    return pl.pallas_call(
        body,
        name=name,
        grid=(m // tm, n // tn, nk),
        in_specs=in_specs,
        out_specs=o_spec,
        out_shape=jax.ShapeDtypeStruct((m, n), out_dtype),
        scratch_shapes=[pltpu.VMEM((tm, tn), F32)] if nk > 1 else [],
        compiler_params=_cp("parallel", "parallel", "arbitrary"),
    )(*args)


def _rms_fwd(x, g, *, name, deps=(), tm=512):
    rows = x.shape[0]

    def body(x_ref, g_ref, *rest):
        o_ref = rest[len(deps)]
        xv = x_ref[...]
        r = lax.rsqrt(jnp.mean(xv * xv, axis=-1, keepdims=True) + EPS)
        o_ref[...] = (xv * r * g_ref[...]).astype(BF16)

    row = pl.BlockSpec((tm, D_MODEL), lambda i: (i, 0))
    return pl.pallas_call(
        body,
        name=name,
        grid=(rows // tm,),
        in_specs=[row, pl.BlockSpec((1, D_MODEL), lambda i: (0, 0))] + [ANY_SPEC] * len(deps),
        out_specs=row,
        out_shape=jax.ShapeDtypeStruct((rows, D_MODEL), BF16),
        compiler_params=_cp("parallel"),
    )(x, g, *deps)


def _rms_bwd(x, g, dh, dres, *, name, deps=(), tm=512):
    rows = x.shape[0]

    def body(x_ref, g_ref, dh_ref, dres_ref, *rest):
        dx_ref, dx16_ref, dg_ref = rest[len(deps):]
        xv = x_ref[...]
        r = lax.rsqrt(jnp.mean(xv * xv, axis=-1, keepdims=True) + EPS)
        xhat = xv * r
        dhv = dh_ref[...]
        part = jnp.sum(dhv * xhat, axis=0, keepdims=True)

        @pl.when(pl.program_id(0) == 0)
        def _():
            dg_ref[...] = part

        @pl.when(pl.program_id(0) > 0)
        def _():
            dg_ref[...] += part

        dxh = dhv * g_ref[...]
        dx = dres_ref[...] + r * (dxh - xhat * jnp.mean(dxh * xhat, axis=-1, keepdims=True))
        dx_ref[...] = dx
        dx16_ref[...] = dx.astype(BF16)

    row = pl.BlockSpec((tm, D_MODEL), lambda i: (i, 0))
    vec = pl.BlockSpec((1, D_MODEL), lambda i: (0, 0))
    return pl.pallas_call(
        body,
        name=name,
        grid=(rows // tm,),
        in_specs=[row, vec, row, row] + [ANY_SPEC] * len(deps),
        out_specs=[row, row, vec],
        out_shape=[jax.ShapeDtypeStruct((rows, D_MODEL), F32), jax.ShapeDtypeStruct((rows, D_MODEL), BF16),
                   jax.ShapeDtypeStruct((1, D_MODEL), F32)],
        compiler_params=_cp("arbitrary"),
    )(x, g, dh, dres, *deps)


_NT = (((1,), (1,)), ((), ()))
_TN = (((0,), (0,)), ((), ()))


def _norm_mm(x, g, w_t, *, swiglu, name, tm, tn, deps=()):
    rows = w_t.shape[0]
    half = rows // 2
    nj = (half if swiglu else rows) // tn
    nw = 2 if swiglu else 1
    nd = len(deps)

    def body(x_ref, g_ref, *rest):
        w_refs, outs = rest[:nw], rest[nw + nd:]
        h_ref, z_ref = outs[:2]

        @pl.when(pl.program_id(1) == 0)
        def _():
            xv = x_ref[...]
            r = lax.rsqrt(jnp.mean(xv * xv, axis=-1, keepdims=True) + EPS)
            h_ref[...] = (xv * r * g_ref[...]).astype(BF16)

        h = h_ref[...]
        if swiglu:
            gate = lax.dot_general(h, w_refs[0][...], _NT, preferred_element_type=F32)
            up = lax.dot_general(h, w_refs[1][...], _NT, preferred_element_type=F32)
            z_ref[0] = gate.astype(BF16)
            z_ref[1] = up.astype(BF16)
            outs[2][...] = (gate * _sigmoid(gate) * up).astype(BF16)
        else:
            z_ref[...] = lax.dot_general(h, w_refs[0][...], _NT, preferred_element_type=F32)

    row = pl.BlockSpec((tm, D_MODEL), lambda i, j: (i, 0))
    w_specs = [pl.BlockSpec((tn, D_MODEL), lambda i, j: (j, 0))]
    out_specs = [row]
    out_shape = [jax.ShapeDtypeStruct((N_TOK, D_MODEL), BF16)]
    if swiglu:
        w_specs.append(pl.BlockSpec((tn, D_MODEL), lambda i, j: (j + nj, 0)))
        out_specs += [pl.BlockSpec((2, tm, tn), lambda i, j: (0, i, j)), pl.BlockSpec((tm, tn), lambda i, j: (i, j))]
        out_shape += [jax.ShapeDtypeStruct((2, N_TOK, half), BF16), jax.ShapeDtypeStruct((N_TOK, half), BF16)]
    else:
        out_specs.append(pl.BlockSpec((tm, tn), lambda i, j: (i, j)))
        out_shape.append(jax.ShapeDtypeStruct((N_TOK, rows), F32))
    return pl.pallas_call(
        body,
        name=name,
        grid=(N_TOK // tm, nj),
        in_specs=[row, pl.BlockSpec((1, D_MODEL), lambda i, j: (0, 0))] + w_specs + [ANY_SPEC] * nd,
        out_specs=out_specs,
        out_shape=out_shape,
        compiler_params=_cp("parallel", "arbitrary"),
    )(x, g, *([w_t] * nw), *deps)


def _swiglu_dgrad(dy16, w_out, z, *, scale, name, tm=512, tn=1408):
    def body(dy_ref, w_ref, z_ref, dz_ref):
        da = lax.dot_general(dy_ref[...], w_ref[...], _NT, preferred_element_type=F32) * scale
        gate, up = z_ref[0].astype(F32), z_ref[1].astype(F32)
        s = _sigmoid(gate)
        dz_ref[0] = (da * up * (s * (1.0 + gate * (1.0 - s)))).astype(BF16)
        dz_ref[1] = (da * (gate * s)).astype(BF16)

    planes = pl.BlockSpec((2, tm, tn), lambda i, j: (0, i, j))
    return pl.pallas_call(
        body,
        name=name,
        grid=(N_TOK // tm, D_FF // tn),
        in_specs=[pl.BlockSpec((tm, D_MODEL), lambda i, j: (i, 0)), pl.BlockSpec((tn, D_MODEL), lambda i, j: (j, 0)), planes],
        out_specs=planes,
        out_shape=jax.ShapeDtypeStruct((2, N_TOK, D_FF), BF16),
        compiler_params=_cp("parallel", "parallel"),
    )(dy16, w_out, z)


def _planes_wgrad(dz, h, *, name, deps=(), tm=1408):
    per_plane = D_FF // tm

    def body(a_ref, b_ref, *rest):
        o_ref = rest[len(deps)]
        o_ref[...] = lax.dot_general(a_ref[...], b_ref[...], _TN, preferred_element_type=F32).astype(BF16)

    return pl.pallas_call(
        body,
        name=name,
        grid=(2 * per_plane,),
        in_specs=[pl.BlockSpec((None, N_TOK, tm),
                               lambda i: (jnp.where(i < per_plane, 0, 1), 0, jnp.where(i < per_plane, i, i - per_plane))),
                  pl.BlockSpec((N_TOK, D_MODEL), lambda i: (0, 0), pipeline_mode=pl.Buffered(1))] + [ANY_SPEC] * len(deps),
        out_specs=pl.BlockSpec((tm, D_MODEL), lambda i: (i, 0)),
        out_shape=jax.ShapeDtypeStruct((2 * D_FF, D_MODEL), BF16),
        compiler_params=_cp("parallel"),
    )(dz, h, *deps)


def _dgrad_norm_bwd(dz, w_t, x, g, dres, *, name, deps=(), tm=512):
    planes = dz.ndim == 3
    rows = w_t.shape[0]
    half = rows // 2
    nd = len(deps)

    def body(a_ref, b_ref, x_ref, g_ref, dres_ref, *rest):
        dx_ref, dx16_ref, dg_ref = rest[nd:]
        if planes:
            dh = jnp.dot(a_ref[0], b_ref[:half, :], preferred_element_type=F32) + jnp.dot(
                a_ref[1], b_ref[half:, :], preferred_element_type=F32)
        else:
            dh = jnp.dot(a_ref[...], b_ref[...], preferred_element_type=F32)
        xv = x_ref[...]
        r = lax.rsqrt(jnp.mean(xv * xv, axis=-1, keepdims=True) + EPS)
        xhat = xv * r
        part = jnp.sum(dh * xhat, axis=0, keepdims=True)

        @pl.when(pl.program_id(0) == 0)
        def _():
            dg_ref[...] = part

        @pl.when(pl.program_id(0) > 0)
        def _():
            dg_ref[...] += part

        dxh = dh * g_ref[...]
        dx = dres_ref[...] + r * (dxh - xhat * jnp.mean(dxh * xhat, axis=-1, keepdims=True))
        dx_ref[...] = dx
        dx16_ref[...] = dx.astype(BF16)

    a_spec = pl.BlockSpec((2, tm, half), lambda i: (0, i, 0)) if planes else pl.BlockSpec((tm, rows), lambda i: (i, 0))
    row = pl.BlockSpec((tm, D_MODEL), lambda i: (i, 0))
    vec = pl.BlockSpec((1, D_MODEL), lambda i: (0, 0))
    return pl.pallas_call(
        body,
        name=name,
        grid=(N_TOK // tm,),
        in_specs=[a_spec, pl.BlockSpec((rows, D_MODEL), lambda i: (0, 0), pipeline_mode=pl.Buffered(1)), row, vec, row]
        + [ANY_SPEC] * nd,
        out_specs=[row, row, vec],
        out_shape=[jax.ShapeDtypeStruct((N_TOK, D_MODEL), F32), jax.ShapeDtypeStruct((N_TOK, D_MODEL), BF16),
                   jax.ShapeDtypeStruct((1, D_MODEL), F32)],
        compiler_params=_cp("arbitrary"),
    )(dz, w_t, x, g, dres, *deps)


def _loss_head(x, g, target, *, tm=512):
    def body(x_ref, g_ref, t_ref, dx_ref, dx16_ref, dg_ref, loss_ref):
        xv = x_ref[...]
        gv = g_ref[...]
        r = lax.rsqrt(jnp.mean(xv * xv, axis=-1, keepdims=True) + EPS)
        xhat = xv * r
        err = xhat * gv - t_ref[...]
        loss_part = jnp.zeros((1, 128), F32) + 0.5 * jnp.sum(jnp.mean(err * err, axis=-1, keepdims=True))
        dy = err * (1.0 / D_MODEL)
        dg_part = jnp.sum(dy * xhat, axis=0, keepdims=True)

        @pl.when(pl.program_id(0) == 0)
        def _():
            dg_ref[...] = dg_part
            loss_ref[...] = loss_part

        @pl.when(pl.program_id(0) > 0)
        def _():
            dg_ref[...] += dg_part
            loss_ref[...] += loss_part

        dxh = dy * gv
        dx = r * (dxh - xhat * jnp.mean(dxh * xhat, axis=-1, keepdims=True))
        dx_ref[...] = dx
        dx16_ref[...] = dx.astype(BF16)

    row = pl.BlockSpec((tm, D_MODEL), lambda i: (i, 0))
    vec = pl.BlockSpec((1, D_MODEL), lambda i: (0, 0))
    return pl.pallas_call(
        body,
        name="loss_head",
        grid=(N_TOK // tm,),
        in_specs=[row, vec, row],
        out_specs=[row, row, vec, pl.BlockSpec((1, 128), lambda i: (0, 0))],
        out_shape=[
            jax.ShapeDtypeStruct((N_TOK, D_MODEL), F32),
            jax.ShapeDtypeStruct((N_TOK, D_MODEL), BF16),
            jax.ShapeDtypeStruct((1, D_MODEL), F32),
            jax.ShapeDtypeStruct((1, 128), F32),
        ],
        compiler_params=_cp("arbitrary"),
    )(x, g, target)


XA_TQ = 1024
XA_SCALE = XA_DIM ** -0.5


def _attn_probs(q16, k16):
    s = lax.dot_general(q16, k16, _NT, preferred_element_type=F32) * XA_SCALE
    e = jnp.exp(s - jnp.max(s, axis=-1, keepdims=True))
    return e / jnp.sum(e, axis=-1, keepdims=True)


def _attn_fwd(z, kv, *, name):
    nt = SEQ // XA_TQ

    def body(q_ref, k_ref, v_ref, o_ref):
        p = _attn_probs(q_ref[...].astype(BF16), k_ref[...].astype(BF16))
        o_ref[...] = jnp.dot(p.astype(BF16), v_ref[...].astype(BF16), preferred_element_type=F32).astype(BF16)

    return pl.pallas_call(
        body,
        name=name,
        grid=(B_LOC, XA_HEADS, nt),
        in_specs=[
            pl.BlockSpec((XA_TQ, XA_DIM), lambda b, h, t: (b * nt + t, XA_OFF // XA_DIM + h)),
            pl.BlockSpec((MEM_LEN, XA_DIM), lambda b, h, t: (b, h)),
            pl.BlockSpec((MEM_LEN, XA_DIM), lambda b, h, t: (b, XA_HEADS + h)),
        ],
        out_specs=pl.BlockSpec((XA_TQ, XA_DIM), lambda b, h, t: (b * nt + t, h)),
        out_shape=jax.ShapeDtypeStruct((N_TOK, XA_HEADS * XA_DIM), BF16),
        compiler_params=_cp("parallel", "parallel", "arbitrary"),
    )(z, kv, kv)


def _attn_bwd(z, kv, dcat, *, do_off, name):
    nt = SEQ // XA_TQ

    def body(q_ref, k_ref, v_ref, do_ref, dq_ref, dk_ref, dv_ref):
        q16 = q_ref[...].astype(BF16)
        k16 = k_ref[...].astype(BF16)
        v16 = v_ref[...].astype(BF16)
        do16 = do_ref[...].astype(BF16)
        p = _attn_probs(q16, k16)
        dv_part = lax.dot_general(p.astype(BF16), do16, _TN, preferred_element_type=F32)
        dp = lax.dot_general(do16, v16, _NT, preferred_element_type=F32)
        ds16 = (p * (dp - jnp.sum(dp * p, axis=-1, keepdims=True)) * XA_SCALE).astype(BF16)
        dq_ref[...] = jnp.dot(ds16, k16, preferred_element_type=F32).astype(BF16)
        dk_part = lax.dot_general(ds16, q16, _TN, preferred_element_type=F32)

        @pl.when(pl.program_id(2) == 0)
        def _():
            dk_ref[...] = dk_part
            dv_ref[...] = dv_part

        @pl.when(pl.program_id(2) > 0)
        def _():
            dk_ref[...] += dk_part
            dv_ref[...] += dv_part

    qspec = pl.BlockSpec((XA_TQ, XA_DIM), lambda b, h, t: (b * nt + t, XA_OFF // XA_DIM + h))
    kspec = lambda off: pl.BlockSpec((MEM_LEN, XA_DIM), lambda b, h, t: (b, off + h))
    return pl.pallas_call(
        body,
        name=name,
        grid=(B_LOC, XA_HEADS, nt),
        in_specs=[qspec, kspec(0), kspec(XA_HEADS),
                  pl.BlockSpec((XA_TQ, XA_DIM), lambda b, h, t: (b * nt + t, do_off // XA_DIM + h))],
        out_specs=[pl.BlockSpec((XA_TQ, XA_DIM), lambda b, h, t: (b * nt + t, h)), kspec(0), kspec(0)],
        out_shape=[
            jax.ShapeDtypeStruct((N_TOK, XA_HEADS * XA_DIM), BF16),
            jax.ShapeDtypeStruct((B_LOC * MEM_LEN, XA_HEADS * XA_DIM), F32),
            jax.ShapeDtypeStruct((B_LOC * MEM_LEN, XA_HEADS * XA_DIM), F32),
        ],
        compiler_params=_cp("parallel", "parallel", "arbitrary"),
    )(z, kv, kv, dcat)


def _tril(n):
    return lax.broadcasted_iota(jnp.int32, (n, n), 0) >= lax.broadcasted_iota(jnp.int32, (n, n), 1)


def _lower_bound(lbl):
    e = jnp.exp(lbl - jnp.max(lbl, axis=0, keepdims=True))
    p = e / jnp.sum(e, axis=0, keepdims=True)
    return p[0:1, :], p


def _hgrn_gates(zq, zf, lb, tril_f):
    sig = _sigmoid(zf)
    f = lb + (1.0 - lb) * sig
    kk = 1.0 - f
    sq = _sigmoid(zq)
    q = zq * sq
    b = jnp.dot(tril_f, jnp.log(f), preferred_element_type=F32, precision=lax.Precision.HIGHEST)
    bl = b[HG_CHUNK - 1:HG_CHUNK, :]
    return q, sq, sig, f, kk, b, bl


HG_TB = 512
HG_CPB = HG_TB // HG_CHUNK
HG_NT = SEQ // HG_TB
HG_WIDTH = HG_HEADS * HG_DIM


def _head(h, section=0):
    return slice(section * HG_WIDTH + h * HG_DIM, section * HG_WIDTH + (h + 1) * HG_DIM)


def _hgrn_fwd(z, o_mem, lb_logits, gnorm):
    def body(zq_ref, zf_ref, zi_ref, zg_ref, omem_ref, lbl_ref, gn_ref, o_ref, opre_ref, sall_ref, st_ref):
        lb, _ = _lower_bound(lbl_ref[...])
        gn = gn_ref[...]
        mask = _tril(HG_CHUNK)
        tril_f = mask.astype(F32)
        o_ref[:, HG_WIDTH:] = omem_ref[...]

        @pl.when(pl.program_id(1) == 0)
        def _():
            st_ref[...] = jnp.zeros_like(st_ref)

        def chunk(c, carry):
            rows = pl.ds(pl.multiple_of(c * HG_CHUNK, HG_CHUNK), HG_CHUNK)
            q, _, _, _, kk, b, bl = _hgrn_gates(zq_ref[rows, :], zf_ref[rows, :], lb, tril_f)
            v16 = zi_ref[rows, :].astype(BF16)
            qd16 = (q * jnp.exp(b)).astype(BF16)
            ki16 = (kk * jnp.exp(-b)).astype(BF16)
            kd16 = (kk * jnp.exp(bl - b)).astype(BF16)
            ebl = jnp.exp(bl)
            zg = zg_ref[rows, :]
            gate = zg * _sigmoid(zg)
            for h in range(HG_HEADS):
                sl = _head(h)
                a = jnp.where(mask, lax.dot_general(qd16[:, sl], ki16[:, sl], _NT, preferred_element_type=F32), 0.0)
                st = st_ref[h]
                sall_ref[0, h, c] = st
                o = jnp.dot(a.astype(BF16), v16[:, sl], preferred_element_type=F32) + lax.dot_general(
                    qd16[:, sl], st.astype(BF16), _NT, preferred_element_type=F32)
                st_ref[h] = st * ebl[:, sl] + lax.dot_general(v16[:, sl], kd16[:, sl], _TN, preferred_element_type=F32)
                opre_ref[rows, sl] = o
                r = lax.rsqrt(jnp.mean(o * o, axis=-1, keepdims=True) + EPS)
                o_ref[rows, sl] = ((o * r * gn) * gate[:, sl]).astype(BF16)
            return carry

        lax.fori_loop(0, HG_CPB, chunk, 0, unroll=2)

    zspec = lambda s: pl.BlockSpec((HG_TB, HG_WIDTH), lambda b, t: (b * HG_NT + t, s))
    return pl.pallas_call(
        body,
        name="hgrn_fwd",
        grid=(B_LOC, HG_NT),
        in_specs=[zspec(0), zspec(1), zspec(2), zspec(3), zspec(0),
                  pl.BlockSpec((3, HG_WIDTH), lambda b, t: (0, 0)), pl.BlockSpec((1, HG_DIM), lambda b, t: (0, 0))],
        out_specs=[pl.BlockSpec((HG_TB, 2 * HG_WIDTH), lambda b, t: (b * HG_NT + t, 0)), zspec(0),
                   pl.BlockSpec((1, HG_HEADS, HG_CPB, HG_DIM, HG_DIM), lambda b, t: (b, 0, t, 0, 0))],
        out_shape=[
            jax.ShapeDtypeStruct((N_TOK, 2 * HG_WIDTH), BF16),
            jax.ShapeDtypeStruct((N_TOK, HG_WIDTH), F32),
            jax.ShapeDtypeStruct((B_LOC, HG_HEADS, HG_NCHUNK, HG_DIM, HG_DIM), F32),
        ],
        scratch_shapes=[pltpu.VMEM((HG_HEADS, HG_DIM, HG_DIM), F32)],
        compiler_params=_cp("parallel", "arbitrary"),
    )(z, z, z, z, o_mem, lb_logits, gnorm)


def _hgrn_bwd(z, opre, dcat, dq_mem, sall, lb_logits, gnorm):
    def body(zq_ref, zf_ref, zi_ref, zg_ref, opre_ref, dout_ref, dqm_ref, sall_ref, lbl_ref, gn_ref,
             dz_ref, dlbl_ref, dgn_ref, dst_ref, dlb_ref, dgn_acc, db_ref, dkk_ref, dbl_ref):
        b_id, t_id = pl.program_id(0), pl.program_id(1)
        lb, p = _lower_bound(lbl_ref[...])
        gn = gn_ref[...]
        mask = _tril(HG_CHUNK)
        tril_f = mask.astype(F32)
        dz_ref[:, 4 * HG_WIDTH:] = dqm_ref[...]

        @pl.when(t_id == 0)
        def _():
            dst_ref[...] = jnp.zeros_like(dst_ref)
            dlb_ref[...] = jnp.zeros_like(dlb_ref)

        @pl.when((b_id == 0) & (t_id == 0))
        def _():
            dgn_acc[...] = jnp.zeros_like(dgn_acc)

        def chunk(i, carry):
            c = HG_CPB - 1 - i
            rows = pl.ds(pl.multiple_of(c * HG_CHUNK, HG_CHUNK), HG_CHUNK)
            zq, zg = zq_ref[rows, :], zg_ref[rows, :]
            q, sq, sig, f, kk, b, bl = _hgrn_gates(zq, zf_ref[rows, :], lb, tril_f)
            v16 = zi_ref[rows, :].astype(BF16)
            eb, enb, ebl_b, ebl = jnp.exp(b), jnp.exp(-b), jnp.exp(bl - b), jnp.exp(bl)
            qd, ki, kd = q * eb, kk * enb, kk * ebl_b
            qd16, ki16, kd16 = qd.astype(BF16), ki.astype(BF16), kd.astype(BF16)
            o_all = opre_ref[rows, :]
            dout = dout_ref[rows, :]
            sg = _sigmoid(zg)
            d_on_all = dout * (zg * sg)
            dgate = dout * (sg * (1.0 + zg * (1.0 - sg)))
            dq_scale = eb * (sq * (1.0 + zq * (1.0 - sq)))
            for h in range(HG_HEADS):
                sl = _head(h)
                o = o_all[:, sl]
                r = lax.rsqrt(jnp.mean(o * o, axis=-1, keepdims=True) + EPS)
                ohat = o * r
                d_on = d_on_all[:, sl]
                dz_ref[rows, _head(h, 3)] = (dgate[:, sl] * (ohat * gn)).astype(BF16)
                dgn_acc[...] += jnp.sum(d_on * ohat, axis=0, keepdims=True)
                dohat = d_on * gn
                do16 = (r * (dohat - ohat * jnp.mean(dohat * ohat, axis=-1, keepdims=True))).astype(BF16)
                st = sall_ref[0, h, c]
                dst = dst_ref[h]
                st16, dst16 = st.astype(BF16), dst.astype(BF16)
                qd_h, ki_h, kd_h, v_h = qd16[:, sl], ki16[:, sl], kd16[:, sl], v16[:, sl]
                a16 = jnp.where(mask, lax.dot_general(qd_h, ki_h, _NT, preferred_element_type=F32), 0.0).astype(BF16)
                da16 = jnp.where(mask, lax.dot_general(do16, v_h, _NT, preferred_element_type=F32), 0.0).astype(BF16)
                dv = lax.dot_general(a16, do16, _TN, preferred_element_type=F32) + lax.dot_general(
                    kd_h, dst16, _NT, preferred_element_type=F32)
                dqd = jnp.dot(da16, ki_h, preferred_element_type=F32) + jnp.dot(do16, st16, preferred_element_type=F32)
                dki = lax.dot_general(da16, qd_h, _TN, preferred_element_type=F32)
                dkd = jnp.dot(v_h, dst16, preferred_element_type=F32)
                dbl_ref[:, sl] = jnp.sum(dkd * kd[:, sl], axis=0, keepdims=True) + ebl[:, sl] * jnp.sum(
                    st * dst, axis=0, keepdims=True)
                dst_ref[h] = dst * ebl[:, sl] + lax.dot_general(do16, qd_h, _TN, preferred_element_type=F32)
                dz_ref[rows, _head(h, 2)] = dv.astype(BF16)
                dz_ref[rows, sl] = (dqd * dq_scale[:, sl]).astype(BF16)
                dkk_ref[:, sl] = dki * enb[:, sl] + dkd * ebl_b[:, sl]
                db_ref[:, sl] = dqd * qd[:, sl] - dki * ki[:, sl] - dkd * kd[:, sl]
            dlogf = lax.dot_general(tril_f, db_ref[...], _TN, preferred_element_type=F32,
                                    precision=lax.Precision.HIGHEST) + dbl_ref[...]
            df = dlogf / f - dkk_ref[...]
            dz_ref[rows, HG_WIDTH:2 * HG_WIDTH] = (df * (1.0 - lb) * sig * (1.0 - sig)).astype(BF16)
            dlb_ref[...] += jnp.sum(df * (1.0 - sig), axis=0, keepdims=True)
            return carry

        lax.fori_loop(0, HG_CPB, chunk, 0, unroll=2)

        @pl.when(t_id == HG_NT - 1)
        def _():
            row0 = (lax.broadcasted_iota(jnp.int32, (3, HG_WIDTH), 0) == 0).astype(F32)
            dlbl_part = dlb_ref[...] * lb * (row0 - p)

            @pl.when(b_id == 0)
            def _():
                dlbl_ref[...] = dlbl_part

            @pl.when(b_id > 0)
            def _():
                dlbl_ref[...] += dlbl_part

            dgn_ref[...] = dgn_acc[...]

    rev = lambda b, t: b * HG_NT + HG_NT - 1 - t
    zspec = lambda s: pl.BlockSpec((HG_TB, HG_WIDTH), lambda b, t: (rev(b, t), s))
    return pl.pallas_call(
        body,
        name="hgrn_bwd",
        grid=(B_LOC, HG_NT),
        in_specs=[zspec(0), zspec(1), zspec(2), zspec(3), zspec(0), zspec(0), zspec(0),
                  pl.BlockSpec((1, HG_HEADS, HG_CPB, HG_DIM, HG_DIM), lambda b, t: (b, 0, HG_NT - 1 - t, 0, 0)),
                  pl.BlockSpec((3, HG_WIDTH), lambda b, t: (0, 0)), pl.BlockSpec((1, HG_DIM), lambda b, t: (0, 0))],
        out_specs=[pl.BlockSpec((HG_TB, 5 * HG_WIDTH), lambda b, t: (rev(b, t), 0)),
                   pl.BlockSpec((3, HG_WIDTH), lambda b, t: (0, 0)), pl.BlockSpec((1, HG_DIM), lambda b, t: (0, 0))],
        out_shape=[jax.ShapeDtypeStruct((N_TOK, 5 * HG_WIDTH), BF16),
                   jax.ShapeDtypeStruct((3, HG_WIDTH), F32), jax.ShapeDtypeStruct((1, HG_DIM), F32)],
        scratch_shapes=[pltpu.VMEM((HG_HEADS, HG_DIM, HG_DIM), F32), pltpu.VMEM((1, HG_WIDTH), F32),
                        pltpu.VMEM((1, HG_DIM), F32), pltpu.VMEM((HG_CHUNK, HG_WIDTH), F32),
                        pltpu.VMEM((HG_CHUNK, HG_WIDTH), F32), pltpu.VMEM((1, HG_WIDTH), F32)],
        compiler_params=_cp("arbitrary", "arbitrary"),
    )(z, z, z, z, opre, dcat, dq_mem, sall, lb_logits, gnorm)


GM_TM = 256


def _gmlp_norm(zv, ln_g, ln_b):
    gv, dgelu = _gelu_parts(zv)
    xc = gv - jnp.mean(gv, axis=-1, keepdims=True)
    rstd = lax.rsqrt(jnp.mean(xc * xc, axis=-1, keepdims=True) + EPS)
    vhat = xc * rstd
    return vhat * ln_g + ln_b, vhat, rstd, dgelu


def _gmlp_specs():
    half = lambda j: pl.BlockSpec((GM_TM, GM_WIDTH), lambda i: (i, j))
    vec = pl.BlockSpec((1, GM_WIDTH), lambda i: (0, 0))
    w = pl.BlockSpec((GM_GROUPS, GM_CHUNK, GM_CHUNK), lambda i: (0, 0, 0))
    bt = pl.BlockSpec((GM_CHUNK, GM_GROUPS), lambda i: (0, 0))
    return half, vec, w, bt


def _gmlp_fwd(z, o_mem, ln_g, ln_b, w_s, b_st):
    def body(zu_ref, zv_ref, omem_ref, g_ref, b_ref, w_ref, bt_ref, o_ref):
        o_ref[:, GM_WIDTH:] = omem_ref[...]
        u, _ = _gelu_parts(zu_ref[...])
        v, _, _, _ = _gmlp_norm(zv_ref[...], g_ref[...], b_ref[...])
        v16 = v.astype(BF16)
        mask = _tril(GM_CHUNK)
        bt = bt_ref[...]
        for g in range(GM_GROUPS):
            wm16 = jnp.where(mask, w_ref[g], 0.0).astype(BF16)
            cols = slice(g * GM_GDIM, (g + 1) * GM_GDIM)
            for c in range(GM_TM // GM_CHUNK):
                rows = slice(c * GM_CHUNK, (c + 1) * GM_CHUNK)
                mixed = jnp.dot(wm16, v16[rows, cols], preferred_element_type=F32) + bt[:, g:g + 1]
                o_ref[rows, cols] = (u[rows, cols] * mixed).astype(BF16)

    half, vec, w, bt = _gmlp_specs()
    return pl.pallas_call(
        body,
        name="gmlp_fwd",
        grid=(N_TOK // GM_TM,),
        in_specs=[half(0), half(1), pl.BlockSpec((GM_TM, XA_HEADS * XA_DIM), lambda i: (i, 0)), vec, vec, w, bt],
        out_specs=pl.BlockSpec((GM_TM, GM_WIDTH + XA_HEADS * XA_DIM), lambda i: (i, 0)),
        out_shape=jax.ShapeDtypeStruct((N_TOK, GM_WIDTH + XA_HEADS * XA_DIM), BF16),
        compiler_params=_cp("parallel"),
    )(z, z, o_mem, ln_g, ln_b, w_s, b_st)


def _gmlp_bwd(z, dcat, dq_mem, ln_g, ln_b, w_s, b_st):
    def body(zu_ref, zv_ref, dout_ref, dqm_ref, g_ref, b_ref, w_ref, bt_ref,
             dz_ref, dw_ref, dbt_ref, dg_ref, db_ref, dv_ref):
        dz_ref[:, 2 * GM_WIDTH:] = dqm_ref[...]
        @pl.when(pl.program_id(0) == 0)
        def _():
            dw_ref[...] = jnp.zeros_like(dw_ref)
            dbt_ref[...] = jnp.zeros_like(dbt_ref)
            dg_ref[...] = jnp.zeros_like(dg_ref)
            db_ref[...] = jnp.zeros_like(db_ref)

        zu = zu_ref[...]
        u, du_dz = _gelu_parts(zu)
        ln_g = g_ref[...]
        v, vhat, rstd, dgv_dz = _gmlp_norm(zv_ref[...], ln_g, b_ref[...])
        v16 = v.astype(BF16)
        dout = dout_ref[...]
        dmixed = dout * u
        dm16 = dmixed.astype(BF16)
        mask = _tril(GM_CHUNK)
        bt = bt_ref[...]
        group_id = lax.broadcasted_iota(jnp.int32, (1, GM_GROUPS), 1)
        dbt = jnp.zeros((GM_CHUNK, GM_GROUPS), F32)
        for g in range(GM_GROUPS):
            wm16 = jnp.where(mask, w_ref[g], 0.0).astype(BF16)
            cols = slice(g * GM_GDIM, (g + 1) * GM_GDIM)
            dw = jnp.zeros((GM_CHUNK, GM_CHUNK), F32)
            dbt_g = jnp.zeros((GM_CHUNK, 1), F32)
            for c in range(GM_TM // GM_CHUNK):
                rows = slice(c * GM_CHUNK, (c + 1) * GM_CHUNK)
                mixed = jnp.dot(wm16, v16[rows, cols], preferred_element_type=F32) + bt[:, g:g + 1]
                dz_ref[rows, cols] = (dout[rows, cols] * mixed * du_dz[rows, cols]).astype(BF16)
                dw += lax.dot_general(dm16[rows, cols], v16[rows, cols], _NT, preferred_element_type=F32)
                dbt_g += jnp.sum(dmixed[rows, cols], axis=-1, keepdims=True)
                dv_ref[rows, cols] = lax.dot_general(wm16, dm16[rows, cols], _TN, preferred_element_type=F32)
            dw_ref[g] += jnp.where(mask, dw, 0.0)
            dbt = dbt + dbt_g * (group_id == g).astype(F32)
        dbt_ref[...] += dbt
        dv = dv_ref[...]
        dg_ref[...] += jnp.sum(dv * vhat, axis=0, keepdims=True)
        db_ref[...] += jnp.sum(dv, axis=0, keepdims=True)
        dvh = dv * ln_g
        dgv = rstd * (dvh - jnp.mean(dvh, axis=-1, keepdims=True) - vhat * jnp.mean(dvh * vhat, axis=-1, keepdims=True))
        dz_ref[:, GM_WIDTH:2 * GM_WIDTH] = (dgv * dgv_dz).astype(BF16)

    half, vec, w, bt = _gmlp_specs()
    dz_width = 2 * GM_WIDTH + XA_HEADS * XA_DIM
    return pl.pallas_call(
        body,
        name="gmlp_bwd",
        grid=(N_TOK // GM_TM,),
        in_specs=[half(0), half(1), half(0), pl.BlockSpec((GM_TM, XA_HEADS * XA_DIM), lambda i: (i, 0)), vec, vec, w, bt],
        out_specs=[pl.BlockSpec((GM_TM, dz_width), lambda i: (i, 0)), w, bt, vec, vec],
        out_shape=[jax.ShapeDtypeStruct((N_TOK, dz_width), BF16),
                   jax.ShapeDtypeStruct((GM_GROUPS, GM_CHUNK, GM_CHUNK), F32),
                   jax.ShapeDtypeStruct((GM_CHUNK, GM_GROUPS), F32),
                   jax.ShapeDtypeStruct((1, GM_WIDTH), F32), jax.ShapeDtypeStruct((1, GM_WIDTH), F32)],
        scratch_shapes=[pltpu.VMEM((GM_TM, GM_WIDTH), F32)],
        compiler_params=_cp("arbitrary"),
    )(z, z, dcat, dq_mem, ln_g, ln_b, w_s, b_st)


def _own_slot(shape):
    return pl.BlockSpec((None,) + tuple(shape), lambda i, me_ref: (me_ref[0],) + (0,) * len(shape))


def _place_rows(w, layer, cuts_columns, me, *, name):
    _, r, c = w.shape
    n = c if cuts_columns else r

    def body(me_ref, w_ref, o_ref):
        wv = w_ref[...]
        o_ref[...] = (wv.T if cuts_columns else wv).astype(BF16)

    return pl.pallas_call(
        body,
        name=name,
        grid_spec=pltpu.PrefetchScalarGridSpec(
            num_scalar_prefetch=1, grid=(1,),
            in_specs=[pl.BlockSpec((None, r, c), lambda i, me_ref: (layer, 0, 0))],
            out_specs=_own_slot((n, D_MODEL))),
        out_shape=jax.ShapeDtypeStruct((N_DEV, n, D_MODEL), BF16),
        compiler_params=_cp("arbitrary"),
    )(me, w)


def _place_ln(ln_g, ln_b, me):
    blk = ln_g.shape[1]

    def body(me_ref, g_ref, b_ref, o_ref):
        o_ref[...] = jnp.zeros_like(o_ref)
        o_ref[0:1, :] = g_ref[...]
        o_ref[1:2, :] = b_ref[...]

    vec = pl.BlockSpec((1, blk), lambda i, me_ref: (0, 0))
    return pl.pallas_call(
        body,
        name="place_ln",
        grid_spec=pltpu.PrefetchScalarGridSpec(
            num_scalar_prefetch=1, grid=(1,), in_specs=[vec, vec], out_specs=_own_slot((8, blk))),
        out_shape=jax.ShapeDtypeStruct((N_DEV, 8, blk), F32),
        compiler_params=_cp("arbitrary"),
    )(me, ln_g, ln_b)


def _place_slab(a, me, *, name):
    def body(me_ref, a_ref, o_ref):
        o_ref[...] = a_ref[...]

    return pl.pallas_call(
        body,
        name=name,
        grid_spec=pltpu.PrefetchScalarGridSpec(
            num_scalar_prefetch=1, grid=(1,),
            in_specs=[pl.BlockSpec(a.shape, lambda i, me_ref: (0, 0))], out_specs=_own_slot(a.shape)),
        out_shape=jax.ShapeDtypeStruct((N_DEV,) + a.shape, a.dtype),
        compiler_params=_cp("arbitrary"),
    )(me, a)


def _place_own(grads, me, *, name):
    k = len(grads)

    def body(me_ref, *refs):
        for src, dst in zip(refs[:k], refs[k:]):
            dst[...] = src[...]

    specs = [_own_slot(g.shape[1:]) for g in grads]
    return pl.pallas_call(
        body,
        name=name,
        grid_spec=pltpu.PrefetchScalarGridSpec(num_scalar_prefetch=1, grid=(1,), in_specs=specs, out_specs=specs),
        out_shape=[jax.ShapeDtypeStruct(g.shape, g.dtype) for g in grads],
        compiler_params=_cp("arbitrary"),
    )(me, *grads)


def _mesh_pos():
    x, y, c = (lax.axis_index(a) for a in MESH_AXES)
    return x, y, c, 4 * x + 2 * y + c


def _peer(x, y, c, r):
    px = 1 - x if r & 4 else x
    py = 1 - y if r & 2 else y
    pc = 1 - c if r & 1 else c
    return (px, py, pc), 4 * px + 2 * py + pc


RELATIONS = {"scatter": (1, 2, 3, 4, 5, 6, 7), "gather_all": (1, 2, 3, 4, 5, 6, 7), "gather_chips": (1, 2, 4, 6),
             "gather_sibling": (2, 4, 6)}


def _peer_copies(srcs, lands, send_sems, recv_sems, mode, waits):
    x, y, c, me = _mesh_pos()
    rel = RELATIONS[mode]
    pairs = []
    for ri, r in enumerate(rel):
        if mode == "gather_sibling":
            peer, _ = _peer(x, y, c, 1)
            _, sent_blk = _peer(x, y, c, r)
            _, got_blk = _peer(x, y, c, r ^ 1)
        else:
            peer, peer_blk = _peer(x, y, c, r)
            sent_blk, got_blk = (peer_blk if mode == "scatter" else me), peer_blk
        for k, (src, land) in enumerate(zip(srcs, lands)):
            idx = k * len(rel) + ri
            sems = dict(send_sem=send_sems.at[idx], recv_sem=recv_sems.at[idx], device_id=peer,
                        device_id_type=pl.DeviceIdType.MESH)
            dst_blk = sent_blk if mode == "gather_sibling" else me
            mine = pltpu.make_async_remote_copy(src_ref=src.at[sent_blk], dst_ref=land.at[dst_blk], **sems)
            theirs = pltpu.make_async_remote_copy(src_ref=src.at[sent_blk], dst_ref=land.at[got_blk], **sems) if waits else None
            pairs.append((mine, theirs))
    return pairs


DATAFLOW = pltpu.SideEffectType.DATAFLOW_SIDE_EFFECTING


def _in_hbm(a):
    return pltpu.with_memory_space_constraint(a, pltpu.HBM)


def _copies_start(srcs, lands, *, mode, name, deps=()):
    gather = mode != "scatter"
    arrs = list(lands) if gather else list(srcs) + list(lands)
    n, k, nd = len(arrs), len(lands), len(deps)

    def body(*refs):
        ins, send_sems, recv_sems, token = refs[:n], refs[n + nd], refs[n + nd + 1], refs[2 * n + nd + 2]
        src_refs, land_refs = (ins, ins) if gather else (ins[:k], ins[k:])
        for mine, _ in _peer_copies(src_refs, land_refs, send_sems, recv_sems, mode, waits=False):
            mine.start()
        token[...] = jnp.zeros_like(token)

    n_cp = k * len(RELATIONS[mode])
    return pl.pallas_call(
        body,
        name=name,
        in_specs=[HBM_SPEC] * n + [ANY_SPEC] * nd,
        out_specs=(SEM_SPEC, SEM_SPEC, *[HBM_SPEC] * n, pl.BlockSpec(memory_space=pltpu.VMEM)),
        out_shape=(pltpu.SemaphoreType.DMA((n_cp,)), pltpu.SemaphoreType.DMA((n_cp,)),
                   *[pltpu.HBM(a.shape, a.dtype) for a in arrs], jax.ShapeDtypeStruct((8, 128), F32)),
        input_output_aliases={i: 2 + i for i in range(n)},
        compiler_params=pltpu.CompilerParams(has_side_effects=DATAFLOW),
    )(*[_in_hbm(a) for a in arrs], *deps)


def _copies_wait(arrs, send_sems, recv_sems, after, *, n_lands, mode, name):
    n, k = len(arrs), n_lands
    gather = mode != "scatter"

    def body(*refs):
        ins, send_sems, recv_sems = refs[:n], refs[n], refs[n + 1]
        src_refs, land_refs = (ins, ins) if gather else (ins[:k], ins[k:])
        for mine, theirs in _peer_copies(src_refs, land_refs, send_sems, recv_sems, mode, waits=True):
            mine.wait_send()
            theirs.wait_recv()

    outs = pl.pallas_call(
        body,
        name=name,
        in_specs=[HBM_SPEC] * n + [SEM_SPEC, SEM_SPEC] + [ANY_SPEC] * len(after),
        out_specs=[HBM_SPEC] * n,
        out_shape=[pltpu.HBM(a.shape, a.dtype) for a in arrs],
        input_output_aliases={i: i for i in range(n)},
        compiler_params=pltpu.CompilerParams(has_side_effects=DATAFLOW),
    )(*arrs, send_sems, recv_sems, *after)
    return outs[n - k:]


def _adamw(w, g, m, v):
    m = ADAM_B1 * m + (1.0 - ADAM_B1) * g
    v = ADAM_B2 * v + (1.0 - ADAM_B2) * (g * g)
    m_hat = m / (1.0 - ADAM_B1 ** ADAM_STEP)
    v_hat = v / (1.0 - ADAM_B2 ** ADAM_STEP)
    return -ADAM_LR * (m_hat / (jnp.sqrt(v_hat) + ADAM_EPS) + ADAM_WD * w), m, v


ADAM_TC = 256


def _adam_big(slots, w, m, v, cuts_columns, *, name):
    layers, n, nj = len(slots), slots[0].shape[1], D_MODEL // ADAM_TC

    def body(*refs):
        s_refs = refs[:layers]
        w_ref, m_ref, v_ref, g_ref, d_ref, nm_ref, nv_ref, acc_ref = refs[layers:]
        for ll in range(layers):
            @pl.when(pl.program_id(0) == ll)
            def _(s_ref=s_refs[ll]):
                g = s_ref[0].astype(F32)
                for s in range(1, N_DEV):
                    g = g + s_ref[s].astype(F32)
                acc_ref[...] = g

        g = acc_ref[...].T if cuts_columns else acc_ref[...]
        g_ref[...] = g
        d_ref[...], nm_ref[...], nv_ref[...] = _adamw(w_ref[...], g, m_ref[...], v_ref[...])

    def slot_spec(ll):
        return pl.BlockSpec((N_DEV, n, ADAM_TC),
                            lambda l, j: (0, 0, jnp.where(l < ll, 0, jnp.where(l > ll, nj - 1, j))))

    if cuts_columns:
        w_spec = pl.BlockSpec((None, ADAM_TC, n), lambda l, j: (l, j, 0))
    else:
        w_spec = pl.BlockSpec((None, n, ADAM_TC), lambda l, j: (l, 0, j))
    return pl.pallas_call(
        body,
        name=name,
        grid=(layers, nj),
        in_specs=[slot_spec(ll) for ll in range(layers)] + [w_spec] * 3,
        out_specs=[w_spec] * 4,
        out_shape=[jax.ShapeDtypeStruct(w.shape, F32)] * 4,
        scratch_shapes=[pltpu.VMEM((n, ADAM_TC), F32)],
        compiler_params=_cp("arbitrary", "arbitrary"),
    )(*slots, w, m, v)


def _adam_slabs(slots, ws, ms, vs):
    n = len(slots)

    def body(*refs):
        ins, outs = refs[:4 * n], refs[4 * n:]
        for k in range(n):
            s_ref, w_ref, m_ref, v_ref = ins[k], ins[n + k], ins[2 * n + k], ins[3 * n + k]
            g = s_ref[0]
            for s in range(1, N_DEV):
                g = g + s_ref[s]
            outs[4 * k][...] = g
            outs[4 * k + 1][...], outs[4 * k + 2][...], outs[4 * k + 3][...] = _adamw(w_ref[...], g, m_ref[...], v_ref[...])

    res = pl.pallas_call(
        body,
        name="small_adamw",
        out_shape=[jax.ShapeDtypeStruct(w.shape, F32) for w in ws for _ in range(4)],
        compiler_params=pltpu.CompilerParams(vmem_limit_bytes=VMEM_LIMIT_BYTES),
    )(*slots, *ws, *ms, *vs)
    return [res[4 * k:4 * k + 4] for k in range(n)]


def _adam_vecs(gs, ws, ms, vs):
    n = len(gs)

    def body(*refs):
        ins, outs = refs[:4 * n], refs[4 * n:]
        for k in range(n):
            outs[3 * k][...], outs[3 * k + 1][...], outs[3 * k + 2][...] = _adamw(
                ins[n + k][...], ins[k][...], ins[2 * n + k][...], ins[3 * n + k][...])

    res = pl.pallas_call(
        body,
        name="ln_adamw",
        out_shape=[jax.ShapeDtypeStruct(w.shape, F32) for w in ws for _ in range(3)],
        compiler_params=pltpu.CompilerParams(vmem_limit_bytes=VMEM_LIMIT_BYTES),
    )(*gs, *ws, *ms, *vs)
    return [res[3 * k:3 * k + 3] for k in range(n)]


SLAB_AT = dict(mem_norm=0, lb_logits=1, ffn1_norm=4, mix_norm=6, hgrn_gnorm=8, gmlp_ln_g=9, gmlp_ln_b=11,
               gmlp_b_s=13, ffn2_norm=14, final_norm=16)
SLAB_ROWS = 24
SMALL_SHARDED = ("gmlp_ln_g", "gmlp_ln_b")


def _pack_slab(parts, *, name):
    flat, plan = [], []
    for pname, at in SLAB_AT.items():
        for a in parts.get(pname, ()):
            flat.append(a)
            plan.append((at, a.shape))
            at += max(1, a.shape[0] * a.shape[1] // D_MODEL)

    def body(*refs):
        o_ref = refs[-1]
        o_ref[...] = jnp.zeros_like(o_ref)
        for ref, (at, (r, w)) in zip(refs, plan):
            if w == D_MODEL or r == 1 and w < D_MODEL:
                o_ref[at:at + r, 0:w] = ref[...]
            elif w < D_MODEL:
                for j in range(r):
                    o_ref[at:at + 1, j * w:(j + 1) * w] = ref[j:j + 1, :]
            else:
                for j in range(w // D_MODEL):
                    o_ref[at + j:at + j + 1, :] = ref[:, j * D_MODEL:(j + 1) * D_MODEL]

    return pl.pallas_call(
        body,
        name=name,
        out_shape=jax.ShapeDtypeStruct((SLAB_ROWS, D_MODEL), F32),
        compiler_params=pltpu.CompilerParams(vmem_limit_bytes=VMEM_LIMIT_BYTES),
    )(*flat)


def _unpack_slab(slab, shapes):
    out = {}
    for pname, at in SLAB_AT.items():
        if pname in SMALL_SHARDED:
            continue
        size = math.prod(shapes[pname])
        rows = max(1, size // D_MODEL)
        out[pname] = slab[at:at + rows].reshape(-1)[:size].reshape(shapes[pname])
    return out


def _ffn_fwd(x, norm_g, block, layer, full, get_weights):
    tag = f"l{layer}_{block}"
    full.update(get_weights((layer, f"{block}_in"), (x,)))
    h, z, act = _norm_mm(x, norm_g, full[(f"{block}_w_in", layer)], swiglu=True, tm=512, tn=1408, deps=full.pop("deps", ()),
                         name=f"{tag}_in")
    full.update(get_weights((layer, f"{block}_out"), (act,)))
    y = _mm(act, full[(f"{block}_w_out", layer)], tm=512, tn=D_MODEL, tk=D_FF, out_dtype=F32, res=x, scale=0.5,
            deps=full.pop("deps", ()), name=f"{tag}_out")
    return y, (x, h, z, act)


def _ffn_bwd(dy, dy16, saved, norm_g, w_in_t, w_out, tag, deps=(), before_in_wgrad=None):
    x, h, z, act = saved
    dw_out = _mm(act, dy16, ta=True, tm=1408, tn=D_MODEL, tk=N_TOK, out_dtype=BF16, scale=0.5, deps=deps,
                 name=f"{tag}_out_wgrad")
    dz = _swiglu_dgrad(dy16, w_out, z, scale=0.5, name=f"{tag}_out_dgrad")
    if before_in_wgrad is None:
        dw_in_t = _planes_wgrad(dz, h, name=f"{tag}_in_wgrad")
        dx, dx16, dg = _dgrad_norm_bwd(dz, w_in_t, x, norm_g, dy, name=f"{tag}_in_dgrad")
    else:
        dx, dx16, dg = _dgrad_norm_bwd(dz, w_in_t, x, norm_g, dy, name=f"{tag}_in_dgrad")
        dw_in_t = _planes_wgrad(dz, h, deps=before_in_wgrad(dg), name=f"{tag}_in_wgrad")
    return dx, dx16, dg, dw_in_t, dw_out


def kernel(x, mem, mem_norm, lb_logits, ffn1_norm, ffn1_w_in, ffn1_w_out, mix_norm, mem_w_kv, hgrn_w_in, hgrn_gnorm, hgrn_w_out, gmlp_w_in, gmlp_ln_g, gmlp_ln_b, gmlp_w_s, gmlp_b_s, gmlp_w_out, ffn2_norm, ffn2_w_in, ffn2_w_out, final_norm, loss_target, m_mem_norm, m_lb_logits, m_ffn1_norm, m_ffn1_w_in, m_ffn1_w_out, m_mix_norm, m_mem_w_kv, m_hgrn_w_in, m_hgrn_gnorm, m_hgrn_w_out, m_gmlp_w_in, m_gmlp_ln_g, m_gmlp_ln_b, m_gmlp_w_s, m_gmlp_b_s, m_gmlp_w_out, m_ffn2_norm, m_ffn2_w_in, m_ffn2_w_out, m_final_norm, v_mem_norm, v_lb_logits, v_ffn1_norm, v_ffn1_w_in, v_ffn1_w_out, v_mix_norm, v_mem_w_kv, v_hgrn_w_in, v_hgrn_gnorm, v_hgrn_w_out, v_gmlp_w_in, v_gmlp_ln_g, v_gmlp_ln_b, v_gmlp_w_s, v_gmlp_b_s, v_gmlp_w_out, v_ffn2_norm, v_ffn2_w_in, v_ffn2_w_out, v_final_norm):
    weights = dict(mem_norm=mem_norm, lb_logits=lb_logits, ffn1_norm=ffn1_norm, ffn1_w_in=ffn1_w_in, ffn1_w_out=ffn1_w_out, mix_norm=mix_norm, mem_w_kv=mem_w_kv, hgrn_w_in=hgrn_w_in, hgrn_gnorm=hgrn_gnorm, hgrn_w_out=hgrn_w_out, gmlp_w_in=gmlp_w_in, gmlp_ln_g=gmlp_ln_g, gmlp_ln_b=gmlp_ln_b, gmlp_w_s=gmlp_w_s, gmlp_b_s=gmlp_b_s, gmlp_w_out=gmlp_w_out, ffn2_norm=ffn2_norm, ffn2_w_in=ffn2_w_in, ffn2_w_out=ffn2_w_out, final_norm=final_norm)
    mom_m = dict(mem_norm=m_mem_norm, lb_logits=m_lb_logits, ffn1_norm=m_ffn1_norm, ffn1_w_in=m_ffn1_w_in, ffn1_w_out=m_ffn1_w_out, mix_norm=m_mix_norm, mem_w_kv=m_mem_w_kv, hgrn_w_in=m_hgrn_w_in, hgrn_gnorm=m_hgrn_gnorm, hgrn_w_out=m_hgrn_w_out, gmlp_w_in=m_gmlp_w_in, gmlp_ln_g=m_gmlp_ln_g, gmlp_ln_b=m_gmlp_ln_b, gmlp_w_s=m_gmlp_w_s, gmlp_b_s=m_gmlp_b_s, gmlp_w_out=m_gmlp_w_out, ffn2_norm=m_ffn2_norm, ffn2_w_in=m_ffn2_w_in, ffn2_w_out=m_ffn2_w_out, final_norm=m_final_norm)
    mom_v = dict(mem_norm=v_mem_norm, lb_logits=v_lb_logits, ffn1_norm=v_ffn1_norm, ffn1_w_in=v_ffn1_w_in, ffn1_w_out=v_ffn1_w_out, mix_norm=v_mix_norm, mem_w_kv=v_mem_w_kv, hgrn_w_in=v_hgrn_w_in, hgrn_gnorm=v_hgrn_gnorm, hgrn_w_out=v_hgrn_w_out, gmlp_w_in=v_gmlp_w_in, gmlp_ln_g=v_gmlp_ln_g, gmlp_ln_b=v_gmlp_ln_b, gmlp_w_s=v_gmlp_w_s, gmlp_b_s=v_gmlp_b_s, gmlp_w_out=v_gmlp_w_out, ffn2_norm=v_ffn2_norm, ffn2_w_in=v_ffn2_w_in, ffn2_w_out=v_ffn2_w_out, final_norm=v_final_norm)
    order = list(weights)
    _, _, _, me = _mesh_pos()
    me_arr = jnp.reshape(me, (1,)).astype(jnp.int32)
    cuts = {name: c for name, c, _, _ in GROUPS}
    rows_already = tuple(name for name, c, _, n in GROUPS if c and n % 128)
    as_rows = lambda a: jnp.transpose(a, (0, 2, 1))
    for name in rows_already:
        weights[name], mom_m[name], mom_v[name] = as_rows(weights[name]), as_rows(mom_m[name]), as_rows(mom_v[name])
        cuts[name] = False

    mix1 = (("mem_w_kv", 1), ("gmlp_w_in", 0), ("gmlp_w_out", 0))
    gather_plan = (
        ((0, "ffn1_in"), (("ffn1_w_in", 0),)),
        ((0, "ffn1_out"), (("ffn1_w_out", 0),)),
        ((0, "mix_in"), _stage_pieces(0, "mix")),
        ((0, "ffn2_in"), _stage_pieces(0, "ffn2")),
        ((1, "ffn1_in"), _stage_pieces(1, "ffn1")),
        ((1, "mix_in"), mix1),
        ((1, "ffn2_in"), _stage_pieces(1, "ffn2")),
    )
    stage_of = {use: k for k, (use, _) in enumerate(gather_plan)}
    in_flight = {}

    def start_chips(k, deps):
        pieces = gather_plan[k][1]
        lands = [_place_rows(weights[name], l, cuts[name], me_arr, name=f"place_{name}_{l}") for name, l in pieces]
        if pieces is mix1:
            lands.append(_place_ln(gmlp_ln_g, gmlp_ln_b, me_arr))
        send_sems, recv_sems, *thru, token = _copies_start(lands, lands, mode="gather_chips", deps=deps,
                                                           name=f"gather{k}_chips_start")
        in_flight[k] = (thru, send_sems, recv_sems)
        return token

    def pass_to_sibling(k, after):
        thru, send_sems, recv_sems = in_flight[k]
        outs = _copies_wait(thru, send_sems, recv_sems, after, n_lands=len(thru), mode="gather_chips",
                            name=f"gather{k}_chips_wait")
        send_sems, recv_sems, *thru, token = _copies_start(outs, outs, mode="gather_sibling",
                                                           name=f"gather{k}_sibling_start")
        in_flight[k] = (thru, send_sems, recv_sems)
        return token, token

    start_chips(0, ())
    points = [(i, p) for i in (0, 1) for p in ("ffn1_in", "ffn1_out", "mix_in", "mix_out", "ffn2_in", "ffn2_out")]
    pass_at = {j: points[points.index(use) - 1] for j, (use, _) in enumerate(gather_plan) if j}

    def get_weights(use, after):
        tokens, w = [], {}
        k = stage_of.get(use)
        if k == 0:
            token, landed = pass_to_sibling(0, after)
            tokens += [token, start_chips(1, (landed,))]
        if k is not None:
            thru, send_sems, recv_sems = in_flight[k]
            outs = _copies_wait(thru, send_sems, recv_sems, after, n_lands=len(thru), mode="gather_sibling",
                                name=f"gather{k}_sibling_wait")
            after = (outs[0],)
            pieces = gather_plan[k][1]
            w = {p: o.reshape(N_DEV * o.shape[1], D_MODEL) for p, o in zip(pieces, outs)}
            if pieces is mix1:
                w["ln_g"] = outs[-1][:, 0, :].reshape(1, GM_WIDTH)
                w["ln_b"] = outs[-1][:, 1, :].reshape(1, GM_WIDTH)
        for j, at in pass_at.items():
            if at == use:
                token, landed = pass_to_sibling(j, after)
                tokens.append(token)
                if j + 1 < len(gather_plan):
                    tokens.append(start_chips(j + 1, (landed,)))
        w["deps"] = tuple(tokens)
        return w

    scatter = {}

    def put_grads(st, grads):
        if st in ("w_s", "small"):
            slab = grads.reshape(GM_GROUPS * GM_CHUNK, GM_CHUNK) if st == "w_s" else _pack_slab(grads, name="pack_small_grads")
            land = _place_slab(slab, me_arr, name=f"{st}_place")
            send_sems, recv_sems, *thru, token = _copies_start([land], [land], mode="gather_all", name=f"{st}_start")
            scatter[st] = (thru, send_sems, recv_sems)
            return (token,)
        views = [grads[p].reshape(N_DEV, -1, D_MODEL) for p in _stage_pieces(*st)]
        recv = _place_own(views, me_arr, name=f"scatter_place_l{st[0]}_{st[1]}")
        send_sems, recv_sems, *thru, token = _copies_start(views, recv, mode="scatter",
                                                           name=f"scatter_start_l{st[0]}_{st[1]}")
        scatter[st] = (thru, send_sems, recv_sems)
        return (token,)

    dx, loss_part, last_sent = _step_local(
        x, mem, loss_target, get_weights, put_grads, mem_norm, lb_logits, ffn1_norm, mix_norm, hgrn_gnorm,
        gmlp_w_s, gmlp_b_s, ffn2_norm, final_norm)

    def slots_of(blk, after):
        slots = {}
        for i in (1, 0):
            thru, send_sems, recv_sems = scatter[(i, blk)]
            outs = _copies_wait(thru, send_sems, recv_sems, after, n_lands=len(thru) // 2, mode="scatter",
                                name=f"scatter_wait_l{i}_{blk}")
            slots.update(zip(_stage_pieces(i, blk), outs))
        return slots

    grad, delta, new_m, new_v = {}, {}, {}, {}

    def adam_groups(slots, names):
        for name in names:
            layers = GROUP_LAYERS[name]
            grad[name], delta[name], new_m[name], new_v[name] = _adam_big(
                [slots[(name, l)] for l in range(layers)], weights[name], mom_m[name], mom_v[name], cuts[name],
                name=f"{name}_adamw")

    adam_groups(slots_of("ffn2", (dx, *last_sent)), ("ffn2_w_in", "ffn2_w_out"))
    adam_groups(slots_of("mix", (delta["ffn2_w_out"],)),
                ("mem_w_kv", "gmlp_w_in", "gmlp_w_out", "hgrn_w_in", "hgrn_w_out"))

    def small_parts(src):
        parts = {n: [src[n].reshape(-1, src[n].shape[-1])] for n in SLAB_AT if n not in SMALL_SHARDED}
        return parts

    w_s_rows = lambda a: a.reshape(GM_GROUPS * GM_CHUNK, GM_CHUNK)
    small_done = (delta["hgrn_w_out"],)
    (slab_slots,) = _copies_wait(*scatter["small"], small_done, n_lands=1, mode="gather_all", name="small_wait")
    (ws_slots,) = _copies_wait(*scatter["w_s"], small_done, n_lands=1, mode="gather_all", name="w_s_wait")
    (g_slab, d_slab, nm_slab, nv_slab), (g_ws, d_ws, nm_ws, nv_ws) = _adam_slabs(
        [slab_slots, ws_slots],
        [_pack_slab(small_parts(weights), name="pack_small_w"), w_s_rows(gmlp_w_s)],
        [_pack_slab(small_parts(mom_m), name="pack_small_m"), w_s_rows(m_gmlp_w_s)],
        [_pack_slab(small_parts(mom_v), name="pack_small_v"), w_s_rows(v_gmlp_w_s)])
    shapes = {n: weights[n].shape for n in SLAB_AT}
    for out, slab, ws in ((grad, g_slab, g_ws), (delta, d_slab, d_ws), (new_m, nm_slab, nm_ws), (new_v, nv_slab, nv_ws)):
        out.update(_unpack_slab(slab, shapes))
        out["gmlp_w_s"] = ws.reshape(gmlp_w_s.shape)
    blk = GM_WIDTH // N_DEV
    g_ln = [lax.dynamic_slice(g_slab[SLAB_AT[n]:SLAB_AT[n] + 2].reshape(1, GM_WIDTH), (0, me * blk), (1, blk))
            for n in SMALL_SHARDED]
    ln_out = _adam_vecs(g_ln, [weights[n] for n in SMALL_SHARDED], [mom_m[n] for n in SMALL_SHARDED],
                        [mom_v[n] for n in SMALL_SHARDED])
    for n, g, (d, nm, nv) in zip(SMALL_SHARDED, g_ln, ln_out):
        grad[n], delta[n], new_m[n], new_v[n] = g, d, nm, nv

    adam_groups(slots_of("ffn1", (delta["hgrn_w_out"], d_slab)), ("ffn1_w_in", "ffn1_w_out"))

    for name in rows_already:
        for out in (grad, delta, new_m, new_v):
            out[name] = as_rows(out[name])
    loss = lax.psum(loss_part[0, 0], MESH_AXES)
    grad_x = dx.reshape(B_LOC, SEQ, D_MODEL)
    return (loss, grad_x, *[grad[n] for n in order], *[delta[n] for n in order],
            *[new_m[n] for n in order], *[new_v[n] for n in order])


def _step_local(x, mem, loss_target, get_weights, put_grads, mem_norm, lb_logits, ffn1_norm, mix_norm, hgrn_gnorm,
                gmlp_w_s, gmlp_b_s, ffn2_norm, final_norm):
    w_s = gmlp_w_s[0]
    b_st = gmlp_b_s[0].T

    xs = x.reshape(N_TOK, D_MODEL)
    mem2d = mem.reshape(B_LOC * MEM_LEN, D_MODEL)
    mem_g = mem_norm.reshape(1, D_MODEL)
    saved, full = [], {}
    memn = _rms_fwd(mem2d, mem_g, name="mem_norm_fwd")
    for i in range(2):
        xs, s_ffn1 = _ffn_fwd(xs, ffn1_norm[i:i + 1], "ffn1", i, full, get_weights)
        full.update(get_weights((i, "mix_in"), (xs,)))
        mixer = "hgrn" if i == 0 else "gmlp"
        hm, zm = _norm_mm(xs, mix_norm[i:i + 1], full[(f"{mixer}_w_in", 0)], swiglu=False, tm=1024, tn=1280, deps=full.pop("deps", ()),
                          name=f"l{i}_mix_in")
        kv = _mm(memn, full[("mem_w_kv", i)], tb=True, tm=512, tn=512, tk=D_MODEL, out_dtype=F32, name=f"l{i}_mem_kv")
        o_mem = _attn_fwd(zm, kv, name=f"l{i}_attn")
        if i == 0:
            cat, o_pre, s_all = _hgrn_fwd(zm, o_mem, lb_logits, hgrn_gnorm)
            mix_saved = (o_pre, s_all)
        else:
            cat = _gmlp_fwd(zm, o_mem, full["ln_g"], full["ln_b"], w_s, b_st)
            mix_saved = ()
        x_mix = xs
        full.update(get_weights((i, "mix_out"), (cat,)))
        xs = _mm(cat, full[(f"{mixer}_w_out", 0)], tm=512, tn=D_MODEL, tk=cat.shape[1], out_dtype=F32, res=xs,
                 deps=full.pop("deps", ()), name=f"l{i}_mix_out")
        xs, s_ffn2 = _ffn_fwd(xs, ffn2_norm[i:i + 1], "ffn2", i, full, get_weights)
        saved.append((s_ffn1, (x_mix, hm, kv, zm, cat, mix_saved), s_ffn2))

    dx, dx16, d_final, loss_part = _loss_head(xs, final_norm.reshape(1, D_MODEL), loss_target.reshape(N_TOK, D_MODEL))

    small = {"final_norm": [d_final]}
    d_ffn1, d_ffn2, d_mix = [None, None], [None, None], [None, None]
    dmemn = jnp.zeros((B_LOC * MEM_LEN, D_MODEL), F32)
    deps = ()
    for i in (1, 0):
        s_ffn1, (x_mix, hm, kv, zm, cat, mix_saved), s_ffn2 = saved[i]
        dx, dx16, d_ffn2[i], dw_in_t, dw_out = _ffn_bwd(
            dx, dx16, s_ffn2, ffn2_norm[i:i + 1], full[("ffn2_w_in", i)], full[("ffn2_w_out", i)], f"l{i}_ffn2", deps)
        deps = put_grads((i, "ffn2"), {("ffn2_w_in", i): dw_in_t, ("ffn2_w_out", i): dw_out})
        mixer = "hgrn" if i == 0 else "gmlp"
        w_in_t, w_out = full[(f"{mixer}_w_in", 0)], full[(f"{mixer}_w_out", 0)]
        width = cat.shape[1]
        g_mix = {}
        g_mix[(f"{mixer}_w_out", 0)] = _mm(cat, dx16, ta=True, tm=1024, tn=D_MODEL, tk=N_TOK, out_dtype=BF16,
                                           deps=deps, name=f"l{i}_mix_out_wgrad")
        dcat = _mm(dx16, w_out, tb=True, tm=1024, tn=width // 2, tk=D_MODEL, out_dtype=F32, name=f"l{i}_mix_out_dgrad")
        dq, dk, dv = _attn_bwd(zm, kv, dcat, do_off=width - XA_HEADS * XA_DIM, name=f"l{i}_attn_bwd")
        if i == 0:
            dzm, dlbl, dgn = _hgrn_bwd(zm, mix_saved[0], dcat, dq, mix_saved[1], lb_logits, hgrn_gnorm)
            small["lb_logits"], small["hgrn_gnorm"] = [dlbl], [dgn]
            deps = ()
        else:
            dzm, dws, dbt, dlng, dlnb = _gmlp_bwd(zm, dcat, dq, full["ln_g"], full["ln_b"], w_s, b_st)
            small["gmlp_b_s"], small["gmlp_ln_g"], small["gmlp_ln_b"] = [dbt.T], [dlng], [dlnb]
            deps = put_grads("w_s", dws)
        g_mix[(f"{mixer}_w_in", 0)] = _mm(dzm, hm, ta=True, tm=1024, tn=D_MODEL, tk=N_TOK, out_dtype=BF16, deps=deps,
                                          name=f"l{i}_mix_in_wgrad")
        dkv = jnp.concatenate([dk, dv], axis=1)
        g_mix[("mem_w_kv", i)] = _mm(dkv, memn, ta=True, tm=512, tn=D_MODEL, tk=B_LOC * MEM_LEN, out_dtype=BF16,
                                     name=f"l{i}_mem_kv_wgrad")
        deps = put_grads((i, "mix"), g_mix)
        dx, dx16, d_mix[i] = _dgrad_norm_bwd(dzm, w_in_t, x_mix, mix_norm[i:i + 1], dx, deps=deps,
                                             name=f"l{i}_mix_in_dgrad")
        dmemn = _mm(dkv, full[("mem_w_kv", i)], tm=B_LOC * MEM_LEN, tn=D_MODEL, tk=512, out_dtype=F32, res=dmemn,
                    name=f"l{i}_mem_kv_dgrad")
        def send_small(dg, i=i, dmemn=dmemn):
            d_ffn1[i] = dg
            _, _, dmem_g = _rms_bwd(mem2d, mem_g, dmemn, dmemn, name="mem_norm_bwd")
            small.update(mem_norm=[dmem_g], ffn1_norm=d_ffn1, ffn2_norm=d_ffn2, mix_norm=d_mix)
            return put_grads("small", small)

        dx, dx16, d_ffn1[i], dw_in_t, dw_out = _ffn_bwd(
            dx, dx16, s_ffn1, ffn1_norm[i:i + 1], full[("ffn1_w_in", i)], full[("ffn1_w_out", i)], f"l{i}_ffn1",
            before_in_wgrad=send_small if i == 0 else None)
        deps = put_grads((i, "ffn1"), {("ffn1_w_in", i): dw_in_t, ("ffn1_w_out", i): dw_out})
    return dx, loss_part, deps
```

```python
import functools
import math

import jax
import jax.numpy as jnp
from jax import lax
from jax.experimental import pallas as pl
from jax.experimental.pallas import tpu as pltpu

F32 = jnp.float32
BF16 = jnp.bfloat16

D_MODEL = 1024
SEQ = 2048
B_LOC = 2
N_TOK = B_LOC * SEQ
MEM_LEN = 256
N_DEV = 8
EPS = 1e-6
D_FF = 2816
HG_HEADS = 8
HG_DIM = 128
HG_CHUNK = 64
HG_NCHUNK = SEQ // HG_CHUNK
GM_CHUNK = 128
GM_GROUPS = 8
GM_WIDTH = 2048
GM_GDIM = GM_WIDTH // GM_GROUPS
XA_HEADS = 4
XA_DIM = 256
XA_OFF = 4096

ADAM_LR = 0.001
ADAM_B1 = 0.9
ADAM_B2 = 0.999
ADAM_EPS = 1e-08
ADAM_WD = 0.01
ADAM_STEP = 10

VMEM_LIMIT_BYTES = 56 * 1024 * 1024
MESH_AXES = ("x", "y", "c")

GROUPS = (
    ("ffn1_w_in", True, 2, 704),
    ("ffn1_w_out", False, 2, 352),
    ("mem_w_kv", True, 2, 256),
    ("hgrn_w_in", True, 1, 640),
    ("hgrn_w_out", False, 1, 256),
    ("gmlp_w_in", True, 1, 640),
    ("gmlp_w_out", False, 1, 384),
    ("ffn2_w_in", True, 2, 704),
    ("ffn2_w_out", False, 2, 352),
)
GROUP_LAYERS = {name: layers for name, _, layers, _ in GROUPS}


def _stage_pieces(layer, block):
    if block == "mix":
        mixer = "hgrn" if layer == 0 else "gmlp"
        return (("mem_w_kv", layer), (f"{mixer}_w_in", 0), (f"{mixer}_w_out", 0))
    return ((f"{block}_w_in", layer), (f"{block}_w_out", layer))


ANY_SPEC = pl.BlockSpec(memory_space=pl.ANY)
HBM_SPEC = pl.BlockSpec(memory_space=pltpu.HBM)
SEM_SPEC = pl.BlockSpec(memory_space=pltpu.SEMAPHORE)


def _cp(*sem):
    return pltpu.CompilerParams(dimension_semantics=sem, vmem_limit_bytes=VMEM_LIMIT_BYTES)


def _sigmoid(x):
    return 0.5 * jnp.tanh(0.5 * x) + 0.5


def _gelu_parts(x):
    cdf = 0.5 * (1.0 + lax.erf(x * (1.0 / math.sqrt(2.0))))
    pdf = jnp.exp(-0.5 * x * x) * (1.0 / math.sqrt(2.0 * math.pi))
    return x * cdf, cdf + x * pdf


def _mm(a, b, *, ta=False, tb=False, tm, tn, tk, out_dtype, res=None, scale=1.0, deps=(), name):
    m, k = (a.shape[1], a.shape[0]) if ta else a.shape
    n, kb = b.shape if tb else (b.shape[1], b.shape[0])
    assert k == kb and m % tm == 0 and n % tn == 0 and k % tk == 0, (name, a.shape, b.shape)
    nk = k // tk
    dn = (((0 if ta else 1,), (1 if tb else 0,)), ((), ()))
    n_in = 2 + (res is not None) + len(deps)

    def body(*refs):
        a_ref, b_ref = refs[:2]
        r_ref = refs[2] if res is not None else None
        o_ref, scr = refs[n_in], refs[n_in + 1:]
        p = lax.dot_general(a_ref[...].astype(BF16), b_ref[...].astype(BF16), dn, preferred_element_type=F32)

        def finish(acc):
            if scale != 1.0:
                acc = scale * acc
            if r_ref is not None:
                acc = r_ref[...] + acc
            o_ref[...] = acc.astype(out_dtype)

        if nk == 1:
            finish(p)
        else:
            acc_ref = scr[0]
            kk = pl.program_id(2)

            @pl.when(kk == 0)
            def _():
                acc_ref[...] = p

            @pl.when(kk > 0)
            def _():
                acc_ref[...] += p

            @pl.when(kk == nk - 1)
            def _():
                finish(acc_ref[...])

    a_spec = pl.BlockSpec((tk, tm), lambda i, j, kk: (kk, i)) if ta else pl.BlockSpec((tm, tk), lambda i, j, kk: (i, kk))
    b_mode = dict(pipeline_mode=pl.Buffered(1)) if n == tn and nk == 1 else {}
    if tb:
        b_spec = pl.BlockSpec((tn, tk), lambda i, j, kk: (j, kk), **b_mode)
    else:
        b_spec = pl.BlockSpec((tk, tn), lambda i, j, kk: (kk, j), **b_mode)
    o_spec = pl.BlockSpec((tm, tn), lambda i, j, kk: (i, j))
    in_specs = [a_spec, b_spec] + ([o_spec] if res is not None else []) + [ANY_SPEC] * len(deps)
    args = (a, b) + ((res,) if res is not None else ()) + tuple(deps)
    return pl.pallas_call(
        body,
        name=name,
        grid=(m // tm, n // tn, nk),
        in_specs=in_specs,
        out_specs=o_spec,
        out_shape=jax.ShapeDtypeStruct((m, n), out_dtype),
        scratch_shapes=[pltpu.VMEM((tm, tn), F32)] if nk > 1 else [],
        compiler_params=_cp("parallel", "parallel", "arbitrary"),
    )(*args)


def _rms_fwd(x, g, *, name, deps=(), tm=512):
    rows = x.shape[0]

    def body(x_ref, g_ref, *rest):
        o_ref = rest[len(deps)]
        xv = x_ref[...]
        r = lax.rsqrt(jnp.mean(xv * xv, axis=-1, keepdims=True) + EPS)
        o_ref[...] = (xv * r * g_ref[...]).astype(BF16)

    row = pl.BlockSpec((tm, D_MODEL), lambda i: (i, 0))
    return pl.pallas_call(
        body,
        name=name,
        grid=(rows // tm,),
        in_specs=[row, pl.BlockSpec((1, D_MODEL), lambda i: (0, 0))] + [ANY_SPEC] * len(deps),
        out_specs=row,
        out_shape=jax.ShapeDtypeStruct((rows, D_MODEL), BF16),
        compiler_params=_cp("parallel"),
    )(x, g, *deps)


def _rms_bwd(x, g, dh, dres, *, name, deps=(), tm=512):
    rows = x.shape[0]

    def body(x_ref, g_ref, dh_ref, dres_ref, *rest):
        dx_ref, dx16_ref, dg_ref = rest[len(deps):]
        xv = x_ref[...]
        r = lax.rsqrt(jnp.mean(xv * xv, axis=-1, keepdims=True) + EPS)
        xhat = xv * r
        dhv = dh_ref[...]
        part = jnp.sum(dhv * xhat, axis=0, keepdims=True)

        @pl.when(pl.program_id(0) == 0)
        def _():
            dg_ref[...] = part

        @pl.when(pl.program_id(0) > 0)
        def _():
            dg_ref[...] += part

        dxh = dhv * g_ref[...]
        dx = dres_ref[...] + r * (dxh - xhat * jnp.mean(dxh * xhat, axis=-1, keepdims=True))
        dx_ref[...] = dx
        dx16_ref[...] = dx.astype(BF16)

    row = pl.BlockSpec((tm, D_MODEL), lambda i: (i, 0))
    vec = pl.BlockSpec((1, D_MODEL), lambda i: (0, 0))
    return pl.pallas_call(
        body,
        name=name,
        grid=(rows // tm,),
        in_specs=[row, vec, row, row] + [ANY_SPEC] * len(deps),
        out_specs=[row, row, vec],
        out_shape=[jax.ShapeDtypeStruct((rows, D_MODEL), F32), jax.ShapeDtypeStruct((rows, D_MODEL), BF16),
                   jax.ShapeDtypeStruct((1, D_MODEL), F32)],
        compiler_params=_cp("arbitrary"),
    )(x, g, dh, dres, *deps)


_NT = (((1,), (1,)), ((), ()))
_TN = (((0,), (0,)), ((), ()))


def _norm_mm(x, g, w_t, *, swiglu, name, tm, tn, deps=()):
    rows = w_t.shape[0]
    half = rows // 2
    nj = (half if swiglu else rows) // tn
    nw = 2 if swiglu else 1
    nd = len(deps)

    def body(x_ref, g_ref, *rest):
        w_refs, outs = rest[:nw], rest[nw + nd:]
        h_ref, z_ref = outs[:2]

        @pl.when(pl.program_id(1) == 0)
        def _():
            xv = x_ref[...]
            r = lax.rsqrt(jnp.mean(xv * xv, axis=-1, keepdims=True) + EPS)
            h_ref[...] = (xv * r * g_ref[...]).astype(BF16)

        h = h_ref[...]
        if swiglu:
            gate = lax.dot_general(h, w_refs[0][...], _NT, preferred_element_type=F32)
            up = lax.dot_general(h, w_refs[1][...], _NT, preferred_element_type=F32)
            z_ref[0] = gate.astype(BF16)
            z_ref[1] = up.astype(BF16)
            outs[2][...] = (gate * _sigmoid(gate) * up).astype(BF16)
        else:
            z_ref[...] = lax.dot_general(h, w_refs[0][...], _NT, preferred_element_type=F32)

    row = pl.BlockSpec((tm, D_MODEL), lambda i, j: (i, 0))
    w_specs = [pl.BlockSpec((tn, D_MODEL), lambda i, j: (j, 0))]
    out_specs = [row]
    out_shape = [jax.ShapeDtypeStruct((N_TOK, D_MODEL), BF16)]
    if swiglu:
        w_specs.append(pl.BlockSpec((tn, D_MODEL), lambda i, j: (j + nj, 0)))
        out_specs += [pl.BlockSpec((2, tm, tn), lambda i, j: (0, i, j)), pl.BlockSpec((tm, tn), lambda i, j: (i, j))]
        out_shape += [jax.ShapeDtypeStruct((2, N_TOK, half), BF16), jax.ShapeDtypeStruct((N_TOK, half), BF16)]
    else:
        out_specs.append(pl.BlockSpec((tm, tn), lambda i, j: (i, j)))
        out_shape.append(jax.ShapeDtypeStruct((N_TOK, rows), F32))
    return pl.pallas_call(
        body,
        name=name,
        grid=(N_TOK // tm, nj),
        in_specs=[row, pl.BlockSpec((1, D_MODEL), lambda i, j: (0, 0))] + w_specs + [ANY_SPEC] * nd,
        out_specs=out_specs,
        out_shape=out_shape,
        compiler_params=_cp("parallel", "arbitrary"),
    )(x, g, *([w_t] * nw), *deps)


def _swiglu_dgrad(dy16, w_out, z, *, scale, name, deps=(), tm=512, tn=1408):
    def body(dy_ref, w_ref, z_ref, *rest):
        dz_ref = rest[len(deps)]
        da = lax.dot_general(dy_ref[...], w_ref[...], _NT, preferred_element_type=F32) * scale
        gate, up = z_ref[0].astype(F32), z_ref[1].astype(F32)
        s = _sigmoid(gate)
        dz_ref[0] = (da * up * (s * (1.0 + gate * (1.0 - s)))).astype(BF16)
        dz_ref[1] = (da * (gate * s)).astype(BF16)

    planes = pl.BlockSpec((2, tm, tn), lambda i, j: (0, i, j))
    return pl.pallas_call(
        body,
        name=name,
        grid=(N_TOK // tm, D_FF // tn),
        in_specs=[pl.BlockSpec((tm, D_MODEL), lambda i, j: (i, 0)), pl.BlockSpec((tn, D_MODEL), lambda i, j: (j, 0)), planes]
        + [ANY_SPEC] * len(deps),
        out_specs=planes,
        out_shape=jax.ShapeDtypeStruct((2, N_TOK, D_FF), BF16),
        compiler_params=_cp("parallel", "parallel"),
    )(dy16, w_out, z, *deps)


def _planes_wgrad(dz, h, *, name, deps=(), tm=1408):
    per_plane = D_FF // tm

    def body(a_ref, b_ref, *rest):
        o_ref = rest[len(deps)]
        o_ref[...] = lax.dot_general(a_ref[...], b_ref[...], _TN, preferred_element_type=F32).astype(BF16)

    return pl.pallas_call(
        body,
        name=name,
        grid=(2 * per_plane,),
        in_specs=[pl.BlockSpec((None, N_TOK, tm),
                               lambda i: (jnp.where(i < per_plane, 0, 1), 0, jnp.where(i < per_plane, i, i - per_plane))),
                  pl.BlockSpec((N_TOK, D_MODEL), lambda i: (0, 0), pipeline_mode=pl.Buffered(1))] + [ANY_SPEC] * len(deps),
        out_specs=pl.BlockSpec((tm, D_MODEL), lambda i: (i, 0)),
        out_shape=jax.ShapeDtypeStruct((2 * D_FF, D_MODEL), BF16),
        compiler_params=_cp("parallel"),
    )(dz, h, *deps)


def _dgrad_norm_bwd(dz, w_t, x, g, dres, *, name, deps=(), tm=512):
    planes = dz.ndim == 3
    rows = w_t.shape[0]
    half = rows // 2
    nd = len(deps)

    def body(a_ref, b_ref, x_ref, g_ref, dres_ref, *rest):
        dx_ref, dx16_ref, dg_ref = rest[nd:]
        if planes:
            dh = jnp.dot(a_ref[0], b_ref[:half, :], preferred_element_type=F32) + jnp.dot(
                a_ref[1], b_ref[half:, :], preferred_element_type=F32)
        else:
            dh = jnp.dot(a_ref[...], b_ref[...], preferred_element_type=F32)
        xv = x_ref[...]
        r = lax.rsqrt(jnp.mean(xv * xv, axis=-1, keepdims=True) + EPS)
        xhat = xv * r
        part = jnp.sum(dh * xhat, axis=0, keepdims=True)

        @pl.when(pl.program_id(0) == 0)
        def _():
            dg_ref[...] = part

        @pl.when(pl.program_id(0) > 0)
        def _():
            dg_ref[...] += part

        dxh = dh * g_ref[...]
        dx = dres_ref[...] + r * (dxh - xhat * jnp.mean(dxh * xhat, axis=-1, keepdims=True))
        dx_ref[...] = dx
        dx16_ref[...] = dx.astype(BF16)

    a_spec = pl.BlockSpec((2, tm, half), lambda i: (0, i, 0)) if planes else pl.BlockSpec((tm, rows), lambda i: (i, 0))
    row = pl.BlockSpec((tm, D_MODEL), lambda i: (i, 0))
    vec = pl.BlockSpec((1, D_MODEL), lambda i: (0, 0))
    return pl.pallas_call(
        body,
        name=name,
        grid=(N_TOK // tm,),
        in_specs=[a_spec, pl.BlockSpec((rows, D_MODEL), lambda i: (0, 0), pipeline_mode=pl.Buffered(1)), row, vec, row]
        + [ANY_SPEC] * nd,
        out_specs=[row, row, vec],
        out_shape=[jax.ShapeDtypeStruct((N_TOK, D_MODEL), F32), jax.ShapeDtypeStruct((N_TOK, D_MODEL), BF16),
                   jax.ShapeDtypeStruct((1, D_MODEL), F32)],
        compiler_params=_cp("arbitrary"),
    )(dz, w_t, x, g, dres, *deps)


def _loss_head(x, g, target, *, tm=512):
    def body(x_ref, g_ref, t_ref, dx_ref, dx16_ref, dg_ref, loss_ref):
        xv = x_ref[...]
        gv = g_ref[...]
        r = lax.rsqrt(jnp.mean(xv * xv, axis=-1, keepdims=True) + EPS)
        xhat = xv * r
        err = xhat * gv - t_ref[...]
        loss_part = jnp.zeros((1, 128), F32) + 0.5 * jnp.sum(jnp.mean(err * err, axis=-1, keepdims=True))
        dy = err * (1.0 / D_MODEL)
        dg_part = jnp.sum(dy * xhat, axis=0, keepdims=True)

        @pl.when(pl.program_id(0) == 0)
        def _():
            dg_ref[...] = dg_part
            loss_ref[...] = loss_part

        @pl.when(pl.program_id(0) > 0)
        def _():
            dg_ref[...] += dg_part
            loss_ref[...] += loss_part

        dxh = dy * gv
        dx = r * (dxh - xhat * jnp.mean(dxh * xhat, axis=-1, keepdims=True))
        dx_ref[...] = dx
        dx16_ref[...] = dx.astype(BF16)

    row = pl.BlockSpec((tm, D_MODEL), lambda i: (i, 0))
    vec = pl.BlockSpec((1, D_MODEL), lambda i: (0, 0))
    return pl.pallas_call(
        body,
        name="loss_head",
        grid=(N_TOK // tm,),
        in_specs=[row, vec, row],
        out_specs=[row, row, vec, pl.BlockSpec((1, 128), lambda i: (0, 0))],
        out_shape=[
            jax.ShapeDtypeStruct((N_TOK, D_MODEL), F32),
            jax.ShapeDtypeStruct((N_TOK, D_MODEL), BF16),
            jax.ShapeDtypeStruct((1, D_MODEL), F32),
            jax.ShapeDtypeStruct((1, 128), F32),
        ],
        compiler_params=_cp("arbitrary"),
    )(x, g, target)


XA_TQ = 1024
XA_SCALE = XA_DIM ** -0.5


def _attn_probs(q16, k16):
    s = lax.dot_general(q16, k16, _NT, preferred_element_type=F32) * XA_SCALE
    e = jnp.exp(s - jnp.max(s, axis=-1, keepdims=True))
    return e / jnp.sum(e, axis=-1, keepdims=True)


def _attn_fwd(z, kv, *, name):
    nt = SEQ // XA_TQ

    def body(q_ref, k_ref, v_ref, o_ref):
        p = _attn_probs(q_ref[...].astype(BF16), k_ref[...].astype(BF16))
        o_ref[...] = jnp.dot(p.astype(BF16), v_ref[...].astype(BF16), preferred_element_type=F32).astype(BF16)

    return pl.pallas_call(
        body,
        name=name,
        grid=(B_LOC, XA_HEADS, nt),
        in_specs=[
            pl.BlockSpec((XA_TQ, XA_DIM), lambda b, h, t: (b * nt + t, XA_OFF // XA_DIM + h)),
            pl.BlockSpec((MEM_LEN, XA_DIM), lambda b, h, t: (b, h)),
            pl.BlockSpec((MEM_LEN, XA_DIM), lambda b, h, t: (b, XA_HEADS + h)),
        ],
        out_specs=pl.BlockSpec((XA_TQ, XA_DIM), lambda b, h, t: (b * nt + t, h)),
        out_shape=jax.ShapeDtypeStruct((N_TOK, XA_HEADS * XA_DIM), BF16),
        compiler_params=_cp("parallel", "parallel", "arbitrary"),
    )(z, kv, kv)


def _attn_bwd(z, kv, dcat, *, do_off, name):
    nt = SEQ // XA_TQ

    def body(q_ref, k_ref, v_ref, do_ref, dq_ref, dk_ref, dv_ref):
        q16 = q_ref[...].astype(BF16)
        k16 = k_ref[...].astype(BF16)
        v16 = v_ref[...].astype(BF16)
        do16 = do_ref[...].astype(BF16)
        p = _attn_probs(q16, k16)
        dv_part = lax.dot_general(p.astype(BF16), do16, _TN, preferred_element_type=F32)
        dp = lax.dot_general(do16, v16, _NT, preferred_element_type=F32)
        ds16 = (p * (dp - jnp.sum(dp * p, axis=-1, keepdims=True)) * XA_SCALE).astype(BF16)
        dq_ref[...] = jnp.dot(ds16, k16, preferred_element_type=F32).astype(BF16)
        dk_part = lax.dot_general(ds16, q16, _TN, preferred_element_type=F32)

        @pl.when(pl.program_id(2) == 0)
        def _():
            dk_ref[...] = dk_part
            dv_ref[...] = dv_part

        @pl.when(pl.program_id(2) > 0)
        def _():
            dk_ref[...] += dk_part
            dv_ref[...] += dv_part

    qspec = pl.BlockSpec((XA_TQ, XA_DIM), lambda b, h, t: (b * nt + t, XA_OFF // XA_DIM + h))
    kspec = lambda off: pl.BlockSpec((MEM_LEN, XA_DIM), lambda b, h, t: (b, off + h))
    return pl.pallas_call(
        body,
        name=name,
        grid=(B_LOC, XA_HEADS, nt),
        in_specs=[qspec, kspec(0), kspec(XA_HEADS),
                  pl.BlockSpec((XA_TQ, XA_DIM), lambda b, h, t: (b * nt + t, do_off // XA_DIM + h))],
        out_specs=[pl.BlockSpec((XA_TQ, XA_DIM), lambda b, h, t: (b * nt + t, h)), kspec(0), kspec(0)],
        out_shape=[
            jax.ShapeDtypeStruct((N_TOK, XA_HEADS * XA_DIM), BF16),
            jax.ShapeDtypeStruct((B_LOC * MEM_LEN, XA_HEADS * XA_DIM), F32),
            jax.ShapeDtypeStruct((B_LOC * MEM_LEN, XA_HEADS * XA_DIM), F32),
        ],
        compiler_params=_cp("parallel", "parallel", "arbitrary"),
    )(z, kv, kv, dcat)


def _tril(n):
    return lax.broadcasted_iota(jnp.int32, (n, n), 0) >= lax.broadcasted_iota(jnp.int32, (n, n), 1)


def _lower_bound(lbl):
    e = jnp.exp(lbl - jnp.max(lbl, axis=0, keepdims=True))
    p = e / jnp.sum(e, axis=0, keepdims=True)
    return p[0:1, :], p


def _hgrn_gates(zq, zf, lb, tril_f):
    sig = _sigmoid(zf)
    f = lb + (1.0 - lb) * sig
    kk = 1.0 - f
    sq = _sigmoid(zq)
    q = zq * sq
    b = jnp.dot(tril_f, jnp.log(f), preferred_element_type=F32, precision=lax.Precision.HIGHEST)
    bl = b[HG_CHUNK - 1:HG_CHUNK, :]
    return q, sq, sig, f, kk, b, bl


HG_TB = 512
HG_CPB = HG_TB // HG_CHUNK
HG_NT = SEQ // HG_TB
HG_WIDTH = HG_HEADS * HG_DIM


def _head(h, section=0):
    return slice(section * HG_WIDTH + h * HG_DIM, section * HG_WIDTH + (h + 1) * HG_DIM)


def _hgrn_fwd(z, o_mem, lb_logits, gnorm):
    def body(zq_ref, zf_ref, zi_ref, zg_ref, omem_ref, lbl_ref, gn_ref, o_ref, opre_ref, sall_ref, st_ref):
        lb, _ = _lower_bound(lbl_ref[...])
        gn = gn_ref[...]
        mask = _tril(HG_CHUNK)
        tril_f = mask.astype(F32)
        o_ref[:, HG_WIDTH:] = omem_ref[...]

        @pl.when(pl.program_id(1) == 0)
        def _():
            st_ref[...] = jnp.zeros_like(st_ref)

        def chunk(c, carry):
            rows = pl.ds(pl.multiple_of(c * HG_CHUNK, HG_CHUNK), HG_CHUNK)
            q, _, _, _, kk, b, bl = _hgrn_gates(zq_ref[rows, :], zf_ref[rows, :], lb, tril_f)
            v16 = zi_ref[rows, :].astype(BF16)
            qd16 = (q * jnp.exp(b)).astype(BF16)
            ki16 = (kk * jnp.exp(-b)).astype(BF16)
            kd16 = (kk * jnp.exp(bl - b)).astype(BF16)
            ebl = jnp.exp(bl)
            zg = zg_ref[rows, :]
            gate = zg * _sigmoid(zg)
            for h in range(HG_HEADS):
                sl = _head(h)
                a = jnp.where(mask, lax.dot_general(qd16[:, sl], ki16[:, sl], _NT, preferred_element_type=F32), 0.0)
                st = st_ref[h]
                sall_ref[0, h, c] = st
                o = jnp.dot(a.astype(BF16), v16[:, sl], preferred_element_type=F32) + lax.dot_general(
                    qd16[:, sl], st.astype(BF16), _NT, preferred_element_type=F32)
                st_ref[h] = st * ebl[:, sl] + lax.dot_general(v16[:, sl], kd16[:, sl], _TN, preferred_element_type=F32)
                opre_ref[rows, sl] = o
                r = lax.rsqrt(jnp.mean(o * o, axis=-1, keepdims=True) + EPS)
                o_ref[rows, sl] = ((o * r * gn) * gate[:, sl]).astype(BF16)
            return carry

        lax.fori_loop(0, HG_CPB, chunk, 0, unroll=2)

    zspec = lambda s: pl.BlockSpec((HG_TB, HG_WIDTH), lambda b, t: (b * HG_NT + t, s))
    return pl.pallas_call(
        body,
        name="hgrn_fwd",
        grid=(B_LOC, HG_NT),
        in_specs=[zspec(0), zspec(1), zspec(2), zspec(3), zspec(0),
                  pl.BlockSpec((3, HG_WIDTH), lambda b, t: (0, 0)), pl.BlockSpec((1, HG_DIM), lambda b, t: (0, 0))],
        out_specs=[pl.BlockSpec((HG_TB, 2 * HG_WIDTH), lambda b, t: (b * HG_NT + t, 0)), zspec(0),
                   pl.BlockSpec((1, HG_HEADS, HG_CPB, HG_DIM, HG_DIM), lambda b, t: (b, 0, t, 0, 0))],
        out_shape=[
            jax.ShapeDtypeStruct((N_TOK, 2 * HG_WIDTH), BF16),
            jax.ShapeDtypeStruct((N_TOK, HG_WIDTH), F32),
            jax.ShapeDtypeStruct((B_LOC, HG_HEADS, HG_NCHUNK, HG_DIM, HG_DIM), F32),
        ],
        scratch_shapes=[pltpu.VMEM((HG_HEADS, HG_DIM, HG_DIM), F32)],
        compiler_params=_cp("parallel", "arbitrary"),
    )(z, z, z, z, o_mem, lb_logits, gnorm)


def _hgrn_bwd(z, opre, dcat, dq_mem, sall, lb_logits, gnorm):
    def body(zq_ref, zf_ref, zi_ref, zg_ref, opre_ref, dout_ref, dqm_ref, sall_ref, lbl_ref, gn_ref,
             dz_ref, dlbl_ref, dgn_ref, dst_ref, dlb_ref, dgn_acc, db_ref, dkk_ref, dbl_ref):
        b_id, t_id = pl.program_id(0), pl.program_id(1)
        lb, p = _lower_bound(lbl_ref[...])
        gn = gn_ref[...]
        mask = _tril(HG_CHUNK)
        tril_f = mask.astype(F32)
        dz_ref[:, 4 * HG_WIDTH:] = dqm_ref[...]

        @pl.when(t_id == 0)
        def _():
            dst_ref[...] = jnp.zeros_like(dst_ref)
            dlb_ref[...] = jnp.zeros_like(dlb_ref)

        @pl.when((b_id == 0) & (t_id == 0))
        def _():
            dgn_acc[...] = jnp.zeros_like(dgn_acc)

        def chunk(i, carry):
            c = HG_CPB - 1 - i
            rows = pl.ds(pl.multiple_of(c * HG_CHUNK, HG_CHUNK), HG_CHUNK)
            zq, zg = zq_ref[rows, :], zg_ref[rows, :]
            q, sq, sig, f, kk, b, bl = _hgrn_gates(zq, zf_ref[rows, :], lb, tril_f)
            v16 = zi_ref[rows, :].astype(BF16)
            eb, enb, ebl_b, ebl = jnp.exp(b), jnp.exp(-b), jnp.exp(bl - b), jnp.exp(bl)
            qd, ki, kd = q * eb, kk * enb, kk * ebl_b
            qd16, ki16, kd16 = qd.astype(BF16), ki.astype(BF16), kd.astype(BF16)
            o_all = opre_ref[rows, :]
            dout = dout_ref[rows, :]
            sg = _sigmoid(zg)
            d_on_all = dout * (zg * sg)
            dgate = dout * (sg * (1.0 + zg * (1.0 - sg)))
            dq_scale = eb * (sq * (1.0 + zq * (1.0 - sq)))
            for h in range(HG_HEADS):
                sl = _head(h)
                o = o_all[:, sl]
                r = lax.rsqrt(jnp.mean(o * o, axis=-1, keepdims=True) + EPS)
                ohat = o * r
                d_on = d_on_all[:, sl]
                dz_ref[rows, _head(h, 3)] = (dgate[:, sl] * (ohat * gn)).astype(BF16)
                dgn_acc[...] += jnp.sum(d_on * ohat, axis=0, keepdims=True)
                dohat = d_on * gn
                do16 = (r * (dohat - ohat * jnp.mean(dohat * ohat, axis=-1, keepdims=True))).astype(BF16)
                st = sall_ref[0, h, c]
                dst = dst_ref[h]
                st16, dst16 = st.astype(BF16), dst.astype(BF16)
                qd_h, ki_h, kd_h, v_h = qd16[:, sl], ki16[:, sl], kd16[:, sl], v16[:, sl]
                a16 = jnp.where(mask, lax.dot_general(qd_h, ki_h, _NT, preferred_element_type=F32), 0.0).astype(BF16)
                da16 = jnp.where(mask, lax.dot_general(do16, v_h, _NT, preferred_element_type=F32), 0.0).astype(BF16)
                dv = lax.dot_general(a16, do16, _TN, preferred_element_type=F32) + lax.dot_general(
                    kd_h, dst16, _NT, preferred_element_type=F32)
                dqd = jnp.dot(da16, ki_h, preferred_element_type=F32) + jnp.dot(do16, st16, preferred_element_type=F32)
                dki = lax.dot_general(da16, qd_h, _TN, preferred_element_type=F32)
                dkd = jnp.dot(v_h, dst16, preferred_element_type=F32)
                dbl_ref[:, sl] = jnp.sum(dkd * kd[:, sl], axis=0, keepdims=True) + ebl[:, sl] * jnp.sum(
                    st * dst, axis=0, keepdims=True)
                dst_ref[h] = dst * ebl[:, sl] + lax.dot_general(do16, qd_h, _TN, preferred_element_type=F32)
                dz_ref[rows, _head(h, 2)] = dv.astype(BF16)
                dz_ref[rows, sl] = (dqd * dq_scale[:, sl]).astype(BF16)
                dkk_ref[:, sl] = dki * enb[:, sl] + dkd * ebl_b[:, sl]
                db_ref[:, sl] = dqd * qd[:, sl] - dki * ki[:, sl] - dkd * kd[:, sl]
            dlogf = lax.dot_general(tril_f, db_ref[...], _TN, preferred_element_type=F32,
                                    precision=lax.Precision.HIGHEST) + dbl_ref[...]
            df = dlogf / f - dkk_ref[...]
            dz_ref[rows, HG_WIDTH:2 * HG_WIDTH] = (df * (1.0 - lb) * sig * (1.0 - sig)).astype(BF16)
            dlb_ref[...] += jnp.sum(df * (1.0 - sig), axis=0, keepdims=True)
            return carry

        lax.fori_loop(0, HG_CPB, chunk, 0, unroll=2)

        @pl.when(t_id == HG_NT - 1)
        def _():
            row0 = (lax.broadcasted_iota(jnp.int32, (3, HG_WIDTH), 0) == 0).astype(F32)
            dlbl_part = dlb_ref[...] * lb * (row0 - p)

            @pl.when(b_id == 0)
            def _():
                dlbl_ref[...] = dlbl_part

            @pl.when(b_id > 0)
            def _():
                dlbl_ref[...] += dlbl_part

            dgn_ref[...] = dgn_acc[...]

    rev = lambda b, t: b * HG_NT + HG_NT - 1 - t
    zspec = lambda s: pl.BlockSpec((HG_TB, HG_WIDTH), lambda b, t: (rev(b, t), s))
    return pl.pallas_call(
        body,
        name="hgrn_bwd",
        grid=(B_LOC, HG_NT),
        in_specs=[zspec(0), zspec(1), zspec(2), zspec(3), zspec(0), zspec(0), zspec(0),
                  pl.BlockSpec((1, HG_HEADS, HG_CPB, HG_DIM, HG_DIM), lambda b, t: (b, 0, HG_NT - 1 - t, 0, 0)),
                  pl.BlockSpec((3, HG_WIDTH), lambda b, t: (0, 0)), pl.BlockSpec((1, HG_DIM), lambda b, t: (0, 0))],
        out_specs=[pl.BlockSpec((HG_TB, 5 * HG_WIDTH), lambda b, t: (rev(b, t), 0)),
                   pl.BlockSpec((3, HG_WIDTH), lambda b, t: (0, 0)), pl.BlockSpec((1, HG_DIM), lambda b, t: (0, 0))],
        out_shape=[jax.ShapeDtypeStruct((N_TOK, 5 * HG_WIDTH), BF16),
                   jax.ShapeDtypeStruct((3, HG_WIDTH), F32), jax.ShapeDtypeStruct((1, HG_DIM), F32)],
        scratch_shapes=[pltpu.VMEM((HG_HEADS, HG_DIM, HG_DIM), F32), pltpu.VMEM((1, HG_WIDTH), F32),
                        pltpu.VMEM((1, HG_DIM), F32), pltpu.VMEM((HG_CHUNK, HG_WIDTH), F32),
                        pltpu.VMEM((HG_CHUNK, HG_WIDTH), F32), pltpu.VMEM((1, HG_WIDTH), F32)],
        compiler_params=_cp("arbitrary", "arbitrary"),
    )(z, z, z, z, opre, dcat, dq_mem, sall, lb_logits, gnorm)


GM_TM = 256


def _gmlp_norm(zv, ln_g, ln_b):
    gv, dgelu = _gelu_parts(zv)
    xc = gv - jnp.mean(gv, axis=-1, keepdims=True)
    rstd = lax.rsqrt(jnp.mean(xc * xc, axis=-1, keepdims=True) + EPS)
    vhat = xc * rstd
    return vhat * ln_g + ln_b, vhat, rstd, dgelu


def _gmlp_specs():
    half = lambda j: pl.BlockSpec((GM_TM, GM_WIDTH), lambda i: (i, j))
    vec = pl.BlockSpec((1, GM_WIDTH), lambda i: (0, 0))
    w = pl.BlockSpec((GM_GROUPS, GM_CHUNK, GM_CHUNK), lambda i: (0, 0, 0))
    bt = pl.BlockSpec((GM_CHUNK, GM_GROUPS), lambda i: (0, 0))
    return half, vec, w, bt


def _gmlp_fwd(z, o_mem, ln_g, ln_b, w_s, b_st):
    def body(zu_ref, zv_ref, omem_ref, g_ref, b_ref, w_ref, bt_ref, o_ref):
        o_ref[:, GM_WIDTH:] = omem_ref[...]
        u, _ = _gelu_parts(zu_ref[...])
        v, _, _, _ = _gmlp_norm(zv_ref[...], g_ref[...], b_ref[...])
        v16 = v.astype(BF16)
        mask = _tril(GM_CHUNK)
        bt = bt_ref[...]
        for g in range(GM_GROUPS):
            wm16 = jnp.where(mask, w_ref[g], 0.0).astype(BF16)
            cols = slice(g * GM_GDIM, (g + 1) * GM_GDIM)
            for c in range(GM_TM // GM_CHUNK):
                rows = slice(c * GM_CHUNK, (c + 1) * GM_CHUNK)
                mixed = jnp.dot(wm16, v16[rows, cols], preferred_element_type=F32) + bt[:, g:g + 1]
                o_ref[rows, cols] = (u[rows, cols] * mixed).astype(BF16)

    half, vec, w, bt = _gmlp_specs()
    return pl.pallas_call(
        body,
        name="gmlp_fwd",
        grid=(N_TOK // GM_TM,),
        in_specs=[half(0), half(1), pl.BlockSpec((GM_TM, XA_HEADS * XA_DIM), lambda i: (i, 0)), vec, vec, w, bt],
        out_specs=pl.BlockSpec((GM_TM, GM_WIDTH + XA_HEADS * XA_DIM), lambda i: (i, 0)),
        out_shape=jax.ShapeDtypeStruct((N_TOK, GM_WIDTH + XA_HEADS * XA_DIM), BF16),
        compiler_params=_cp("parallel"),
    )(z, z, o_mem, ln_g, ln_b, w_s, b_st)


def _gmlp_bwd(z, dcat, dq_mem, ln_g, ln_b, w_s, b_st):
    def body(zu_ref, zv_ref, dout_ref, dqm_ref, g_ref, b_ref, w_ref, bt_ref,
             dz_ref, dw_ref, dbt_ref, dg_ref, db_ref, dv_ref):
        dz_ref[:, 2 * GM_WIDTH:] = dqm_ref[...]
        @pl.when(pl.program_id(0) == 0)
        def _():
            dw_ref[...] = jnp.zeros_like(dw_ref)
            dbt_ref[...] = jnp.zeros_like(dbt_ref)
            dg_ref[...] = jnp.zeros_like(dg_ref)
            db_ref[...] = jnp.zeros_like(db_ref)

        zu = zu_ref[...]
        u, du_dz = _gelu_parts(zu)
        ln_g = g_ref[...]
        v, vhat, rstd, dgv_dz = _gmlp_norm(zv_ref[...], ln_g, b_ref[...])
        v16 = v.astype(BF16)
        dout = dout_ref[...]
        dmixed = dout * u
        dm16 = dmixed.astype(BF16)
        mask = _tril(GM_CHUNK)
        bt = bt_ref[...]
        group_id = lax.broadcasted_iota(jnp.int32, (1, GM_GROUPS), 1)
        dbt = jnp.zeros((GM_CHUNK, GM_GROUPS), F32)
        for g in range(GM_GROUPS):
            wm16 = jnp.where(mask, w_ref[g], 0.0).astype(BF16)
            cols = slice(g * GM_GDIM, (g + 1) * GM_GDIM)
            dw = jnp.zeros((GM_CHUNK, GM_CHUNK), F32)
            dbt_g = jnp.zeros((GM_CHUNK, 1), F32)
            for c in range(GM_TM // GM_CHUNK):
                rows = slice(c * GM_CHUNK, (c + 1) * GM_CHUNK)
                mixed = jnp.dot(wm16, v16[rows, cols], preferred_element_type=F32) + bt[:, g:g + 1]
                dz_ref[rows, cols] = (dout[rows, cols] * mixed * du_dz[rows, cols]).astype(BF16)
                dw += lax.dot_general(dm16[rows, cols], v16[rows, cols], _NT, preferred_element_type=F32)
                dbt_g += jnp.sum(dmixed[rows, cols], axis=-1, keepdims=True)
                dv_ref[rows, cols] = lax.dot_general(wm16, dm16[rows, cols], _TN, preferred_element_type=F32)
            dw_ref[g] += jnp.where(mask, dw, 0.0)
            dbt = dbt + dbt_g * (group_id == g).astype(F32)
        dbt_ref[...] += dbt
        dv = dv_ref[...]
        dg_ref[...] += jnp.sum(dv * vhat, axis=0, keepdims=True)
        db_ref[...] += jnp.sum(dv, axis=0, keepdims=True)
        dvh = dv * ln_g
        dgv = rstd * (dvh - jnp.mean(dvh, axis=-1, keepdims=True) - vhat * jnp.mean(dvh * vhat, axis=-1, keepdims=True))
        dz_ref[:, GM_WIDTH:2 * GM_WIDTH] = (dgv * dgv_dz).astype(BF16)

    half, vec, w, bt = _gmlp_specs()
    dz_width = 2 * GM_WIDTH + XA_HEADS * XA_DIM
    return pl.pallas_call(
        body,
        name="gmlp_bwd",
        grid=(N_TOK // GM_TM,),
        in_specs=[half(0), half(1), half(0), pl.BlockSpec((GM_TM, XA_HEADS * XA_DIM), lambda i: (i, 0)), vec, vec, w, bt],
        out_specs=[pl.BlockSpec((GM_TM, dz_width), lambda i: (i, 0)), w, bt, vec, vec],
        out_shape=[jax.ShapeDtypeStruct((N_TOK, dz_width), BF16),
                   jax.ShapeDtypeStruct((GM_GROUPS, GM_CHUNK, GM_CHUNK), F32),
                   jax.ShapeDtypeStruct((GM_CHUNK, GM_GROUPS), F32),
                   jax.ShapeDtypeStruct((1, GM_WIDTH), F32), jax.ShapeDtypeStruct((1, GM_WIDTH), F32)],
        scratch_shapes=[pltpu.VMEM((GM_TM, GM_WIDTH), F32)],
        compiler_params=_cp("arbitrary"),
    )(z, z, dcat, dq_mem, ln_g, ln_b, w_s, b_st)


def _own_slot(shape):
    return pl.BlockSpec((None,) + tuple(shape), lambda i, me_ref: (me_ref[0],) + (0,) * len(shape))


def _place_rows(w, layer, cuts_columns, me, *, name, deps=()):
    _, r, c = w.shape
    n = c if cuts_columns else r

    def body(me_ref, w_ref, *rest):
        o_ref = rest[len(deps)]
        wv = w_ref[...]
        o_ref[...] = (wv.T if cuts_columns else wv).astype(BF16)

    return pl.pallas_call(
        body,
        name=name,
        grid_spec=pltpu.PrefetchScalarGridSpec(
            num_scalar_prefetch=1, grid=(1,),
            in_specs=[pl.BlockSpec((None, r, c), lambda i, me_ref: (layer, 0, 0))] + [ANY_SPEC] * len(deps),
            out_specs=_own_slot((n, D_MODEL))),
        out_shape=jax.ShapeDtypeStruct((N_DEV, n, D_MODEL), BF16),
        compiler_params=_cp("arbitrary"),
    )(me, w, *deps)


def _place_ln(ln_g, ln_b, me):
    blk = ln_g.shape[1]

    def body(me_ref, g_ref, b_ref, o_ref):
        o_ref[...] = jnp.zeros_like(o_ref)
        o_ref[0:1, :] = g_ref[...]
        o_ref[1:2, :] = b_ref[...]

    vec = pl.BlockSpec((1, blk), lambda i, me_ref: (0, 0))
    return pl.pallas_call(
        body,
        name="place_ln",
        grid_spec=pltpu.PrefetchScalarGridSpec(
            num_scalar_prefetch=1, grid=(1,), in_specs=[vec, vec], out_specs=_own_slot((8, blk))),
        out_shape=jax.ShapeDtypeStruct((N_DEV, 8, blk), F32),
        compiler_params=_cp("arbitrary"),
    )(me, ln_g, ln_b)


def _place_slab(a, me, *, name):
    def body(me_ref, a_ref, o_ref):
        o_ref[...] = a_ref[...]

    return pl.pallas_call(
        body,
        name=name,
        grid_spec=pltpu.PrefetchScalarGridSpec(
            num_scalar_prefetch=1, grid=(1,),
            in_specs=[pl.BlockSpec(a.shape, lambda i, me_ref: (0, 0))], out_specs=_own_slot(a.shape)),
        out_shape=jax.ShapeDtypeStruct((N_DEV,) + a.shape, a.dtype),
        compiler_params=_cp("arbitrary"),
    )(me, a)


def _place_own(grads, me, *, name):
    k = len(grads)

    def body(me_ref, *refs):
        for src, dst in zip(refs[:k], refs[k:]):
            dst[...] = src[...]

    specs = [_own_slot(g.shape[1:]) for g in grads]
    return pl.pallas_call(
        body,
        name=name,
        grid_spec=pltpu.PrefetchScalarGridSpec(num_scalar_prefetch=1, grid=(1,), in_specs=specs, out_specs=specs),
        out_shape=[jax.ShapeDtypeStruct(g.shape, g.dtype) for g in grads],
        compiler_params=_cp("arbitrary"),
    )(me, *grads)


def _mesh_pos():
    x, y, c = (lax.axis_index(a) for a in MESH_AXES)
    return x, y, c, 4 * x + 2 * y + c


def _peer(x, y, c, r):
    px = 1 - x if r & 4 else x
    py = 1 - y if r & 2 else y
    pc = 1 - c if r & 1 else c
    return (px, py, pc), 4 * px + 2 * py + pc


RELATIONS = {"scatter": (1, 2, 3, 4, 5, 6, 7), "gather_all": (1, 2, 3, 4, 5, 6, 7), "gather_chips": (1, 2, 4, 6),
             "gather_sibling": (2, 4, 6)}


def _peer_copies(srcs, lands, send_sems, recv_sems, mode, waits):
    x, y, c, me = _mesh_pos()
    rel = RELATIONS[mode]
    pairs = []
    for ri, r in enumerate(rel):
        if mode == "gather_sibling":
            peer, _ = _peer(x, y, c, 1)
            _, sent_blk = _peer(x, y, c, r)
            _, got_blk = _peer(x, y, c, r ^ 1)
        else:
            peer, peer_blk = _peer(x, y, c, r)
            sent_blk, got_blk = (peer_blk if mode == "scatter" else me), peer_blk
        for k, (src, land) in enumerate(zip(srcs, lands)):
            idx = k * len(rel) + ri
            sems = dict(send_sem=send_sems.at[idx], recv_sem=recv_sems.at[idx], device_id=peer,
                        device_id_type=pl.DeviceIdType.MESH)
            dst_blk = sent_blk if mode == "gather_sibling" else me
            mine = pltpu.make_async_remote_copy(src_ref=src.at[sent_blk], dst_ref=land.at[dst_blk], **sems)
            theirs = pltpu.make_async_remote_copy(src_ref=src.at[sent_blk], dst_ref=land.at[got_blk], **sems) if waits else None
            pairs.append((mine, theirs))
    return pairs


DATAFLOW = pltpu.SideEffectType.DATAFLOW_SIDE_EFFECTING


def _in_hbm(a):
    return pltpu.with_memory_space_constraint(a, pltpu.HBM)


def _copies_start(srcs, lands, *, mode, name, deps=()):
    gather = mode != "scatter"
    arrs = list(lands) if gather else list(srcs) + list(lands)
    n, k, nd = len(arrs), len(lands), len(deps)

    def body(*refs):
        ins, send_sems, recv_sems, token = refs[:n], refs[n + nd], refs[n + nd + 1], refs[2 * n + nd + 2]
        src_refs, land_refs = (ins, ins) if gather else (ins[:k], ins[k:])
        for mine, _ in _peer_copies(src_refs, land_refs, send_sems, recv_sems, mode, waits=False):
            mine.start()
        token[...] = jnp.zeros_like(token)

    n_cp = k * len(RELATIONS[mode])
    return pl.pallas_call(
        body,
        name=name,
        in_specs=[HBM_SPEC] * n + [ANY_SPEC] * nd,
        out_specs=(SEM_SPEC, SEM_SPEC, *[HBM_SPEC] * n, pl.BlockSpec(memory_space=pltpu.VMEM)),
        out_shape=(pltpu.SemaphoreType.DMA((n_cp,)), pltpu.SemaphoreType.DMA((n_cp,)),
                   *[pltpu.HBM(a.shape, a.dtype) for a in arrs], jax.ShapeDtypeStruct((8, 128), F32)),
        input_output_aliases={i: 2 + i for i in range(n)},
        compiler_params=pltpu.CompilerParams(has_side_effects=DATAFLOW),
    )(*[_in_hbm(a) for a in arrs], *deps)


def _copies_wait(arrs, send_sems, recv_sems, after, *, n_lands, mode, name):
    n, k = len(arrs), n_lands
    gather = mode != "scatter"

    def body(*refs):
        ins, send_sems, recv_sems = refs[:n], refs[n], refs[n + 1]
        src_refs, land_refs = (ins, ins) if gather else (ins[:k], ins[k:])
        for mine, theirs in _peer_copies(src_refs, land_refs, send_sems, recv_sems, mode, waits=True):
            mine.wait_send()
            theirs.wait_recv()

    outs = pl.pallas_call(
        body,
        name=name,
        in_specs=[HBM_SPEC] * n + [SEM_SPEC, SEM_SPEC] + [ANY_SPEC] * len(after),
        out_specs=[HBM_SPEC] * n,
        out_shape=[pltpu.HBM(a.shape, a.dtype) for a in arrs],
        input_output_aliases={i: i for i in range(n)},
        compiler_params=pltpu.CompilerParams(has_side_effects=DATAFLOW),
    )(*arrs, send_sems, recv_sems, *after)
    return outs[n - k:]


def _adamw(w, g, m, v):
    m = ADAM_B1 * m + (1.0 - ADAM_B1) * g
    v = ADAM_B2 * v + (1.0 - ADAM_B2) * (g * g)
    m_hat = m / (1.0 - ADAM_B1 ** ADAM_STEP)
    v_hat = v / (1.0 - ADAM_B2 ** ADAM_STEP)
    return -ADAM_LR * (m_hat / (jnp.sqrt(v_hat) + ADAM_EPS) + ADAM_WD * w), m, v


ADAM_TC = 256


def _adam_big(slots, w, m, v, cuts_columns, *, name):
    layers, n, nj = len(slots), slots[0].shape[1], D_MODEL // ADAM_TC

    def body(*refs):
        s_refs = refs[:layers]
        w_ref, m_ref, v_ref, g_ref, d_ref, nm_ref, nv_ref, acc_ref = refs[layers:]
        for ll in range(layers):
            @pl.when(pl.program_id(0) == ll)
            def _(s_ref=s_refs[ll]):
                g = s_ref[0].astype(F32)
                for s in range(1, N_DEV):
                    g = g + s_ref[s].astype(F32)
                acc_ref[...] = g

        g = acc_ref[...].T if cuts_columns else acc_ref[...]
        g_ref[...] = g
        d_ref[...], nm_ref[...], nv_ref[...] = _adamw(w_ref[...], g, m_ref[...], v_ref[...])

    def slot_spec(ll):
        return pl.BlockSpec((N_DEV, n, ADAM_TC),
                            lambda l, j: (0, 0, jnp.where(l < ll, 0, jnp.where(l > ll, nj - 1, j))))

    if cuts_columns:
        w_spec = pl.BlockSpec((None, ADAM_TC, n), lambda l, j: (l, j, 0))
    else:
        w_spec = pl.BlockSpec((None, n, ADAM_TC), lambda l, j: (l, 0, j))
    return pl.pallas_call(
        body,
        name=name,
        grid=(layers, nj),
        in_specs=[slot_spec(ll) for ll in range(layers)] + [w_spec] * 3,
        out_specs=[w_spec] * 4,
        out_shape=[jax.ShapeDtypeStruct(w.shape, F32)] * 4,
        scratch_shapes=[pltpu.VMEM((n, ADAM_TC), F32)],
        compiler_params=_cp("arbitrary", "arbitrary"),
    )(*slots, w, m, v)


def _adam_slabs(slots, ws, ms, vs):
    n = len(slots)

    def body(*refs):
        ins, outs = refs[:4 * n], refs[4 * n:]
        for k in range(n):
            s_ref, w_ref, m_ref, v_ref = ins[k], ins[n + k], ins[2 * n + k], ins[3 * n + k]
            g = s_ref[0]
            for s in range(1, N_DEV):
                g = g + s_ref[s]
            outs[4 * k][...] = g
            outs[4 * k + 1][...], outs[4 * k + 2][...], outs[4 * k + 3][...] = _adamw(w_ref[...], g, m_ref[...], v_ref[...])

    res = pl.pallas_call(
        body,
        name="small_adamw",
        out_shape=[jax.ShapeDtypeStruct(w.shape, F32) for w in ws for _ in range(4)],
        compiler_params=pltpu.CompilerParams(vmem_limit_bytes=VMEM_LIMIT_BYTES),
    )(*slots, *ws, *ms, *vs)
    return [res[4 * k:4 * k + 4] for k in range(n)]


def _adam_vecs(gs, ws, ms, vs):
    n = len(gs)

    def body(*refs):
        ins, outs = refs[:4 * n], refs[4 * n:]
        for k in range(n):
            outs[3 * k][...], outs[3 * k + 1][...], outs[3 * k + 2][...] = _adamw(
                ins[n + k][...], ins[k][...], ins[2 * n + k][...], ins[3 * n + k][...])

    res = pl.pallas_call(
        body,
        name="ln_adamw",
        out_shape=[jax.ShapeDtypeStruct(w.shape, F32) for w in ws for _ in range(3)],
        compiler_params=pltpu.CompilerParams(vmem_limit_bytes=VMEM_LIMIT_BYTES),
    )(*gs, *ws, *ms, *vs)
    return [res[3 * k:3 * k + 3] for k in range(n)]


SLAB_AT = dict(mem_norm=0, lb_logits=1, ffn1_norm=4, mix_norm=6, hgrn_gnorm=8, gmlp_ln_g=9, gmlp_ln_b=11,
               gmlp_b_s=13, ffn2_norm=14, final_norm=16)
SLAB_ROWS = 24
SMALL_SHARDED = ("gmlp_ln_g", "gmlp_ln_b")


def _pack_slab(parts, *, name):
    flat, plan = [], []
    for pname, at in SLAB_AT.items():
        for a in parts.get(pname, ()):
            flat.append(a)
            plan.append((at, a.shape))
            at += max(1, a.shape[0] * a.shape[1] // D_MODEL)

    def body(*refs):
        o_ref = refs[-1]
        o_ref[...] = jnp.zeros_like(o_ref)
        for ref, (at, (r, w)) in zip(refs, plan):
            if w == D_MODEL or r == 1 and w < D_MODEL:
                o_ref[at:at + r, 0:w] = ref[...]
            elif w < D_MODEL:
                for j in range(r):
                    o_ref[at:at + 1, j * w:(j + 1) * w] = ref[j:j + 1, :]
            else:
                for j in range(w // D_MODEL):
                    o_ref[at + j:at + j + 1, :] = ref[:, j * D_MODEL:(j + 1) * D_MODEL]

    return pl.pallas_call(
        body,
        name=name,
        out_shape=jax.ShapeDtypeStruct((SLAB_ROWS, D_MODEL), F32),
        compiler_params=pltpu.CompilerParams(vmem_limit_bytes=VMEM_LIMIT_BYTES),
    )(*flat)


def _unpack_slab(slab, shapes):
    out = {}
    for pname, at in SLAB_AT.items():
        if pname in SMALL_SHARDED:
            continue
        size = math.prod(shapes[pname])
        rows = max(1, size // D_MODEL)
        out[pname] = slab[at:at + rows].reshape(-1)[:size].reshape(shapes[pname])
    return out


def _ffn_fwd(x, norm_g, block, layer, full, get_weights):
    tag = f"l{layer}_{block}"
    full.update(get_weights((layer, f"{block}_in"), (x,)))
    h, z, act = _norm_mm(x, norm_g, full[(f"{block}_w_in", layer)], swiglu=True, tm=512, tn=1408, deps=full.pop("deps", ()),
                         name=f"{tag}_in")
    full.update(get_weights((layer, f"{block}_out"), (act,)))
    y = _mm(act, full[(f"{block}_w_out", layer)], tm=512, tn=D_MODEL, tk=D_FF, out_dtype=F32, res=x, scale=0.5,
            deps=full.pop("deps", ()), name=f"{tag}_out")
    return y, (x, h, z, act)


def _ffn_bwd(dy, dy16, saved, norm_g, w_in_t, w_out, tag, deps=(), after_out_wgrad=None, before_in_wgrad=None):
    x, h, z, act = saved
    dw_out = _mm(act, dy16, ta=True, tm=1408, tn=D_MODEL, tk=N_TOK, out_dtype=BF16, scale=0.5, deps=deps,
                 name=f"{tag}_out_wgrad")
    sent = after_out_wgrad(dw_out) if after_out_wgrad is not None else ()
    dz = _swiglu_dgrad(dy16, w_out, z, scale=0.5, deps=sent, name=f"{tag}_out_dgrad")
    if before_in_wgrad is None:
        dw_in_t = _planes_wgrad(dz, h, name=f"{tag}_in_wgrad")
        dx, dx16, dg = _dgrad_norm_bwd(dz, w_in_t, x, norm_g, dy, name=f"{tag}_in_dgrad")
    else:
        dx, dx16, dg = _dgrad_norm_bwd(dz, w_in_t, x, norm_g, dy, name=f"{tag}_in_dgrad")
        dw_in_t = _planes_wgrad(dz, h, deps=before_in_wgrad(dg), name=f"{tag}_in_wgrad")
    return dx, dx16, dg, dw_in_t, dw_out


def kernel(x, mem, mem_norm, lb_logits, ffn1_norm, ffn1_w_in, ffn1_w_out, mix_norm, mem_w_kv, hgrn_w_in, hgrn_gnorm, hgrn_w_out, gmlp_w_in, gmlp_ln_g, gmlp_ln_b, gmlp_w_s, gmlp_b_s, gmlp_w_out, ffn2_norm, ffn2_w_in, ffn2_w_out, final_norm, loss_target, m_mem_norm, m_lb_logits, m_ffn1_norm, m_ffn1_w_in, m_ffn1_w_out, m_mix_norm, m_mem_w_kv, m_hgrn_w_in, m_hgrn_gnorm, m_hgrn_w_out, m_gmlp_w_in, m_gmlp_ln_g, m_gmlp_ln_b, m_gmlp_w_s, m_gmlp_b_s, m_gmlp_w_out, m_ffn2_norm, m_ffn2_w_in, m_ffn2_w_out, m_final_norm, v_mem_norm, v_lb_logits, v_ffn1_norm, v_ffn1_w_in, v_ffn1_w_out, v_mix_norm, v_mem_w_kv, v_hgrn_w_in, v_hgrn_gnorm, v_hgrn_w_out, v_gmlp_w_in, v_gmlp_ln_g, v_gmlp_ln_b, v_gmlp_w_s, v_gmlp_b_s, v_gmlp_w_out, v_ffn2_norm, v_ffn2_w_in, v_ffn2_w_out, v_final_norm):
    weights = dict(mem_norm=mem_norm, lb_logits=lb_logits, ffn1_norm=ffn1_norm, ffn1_w_in=ffn1_w_in, ffn1_w_out=ffn1_w_out, mix_norm=mix_norm, mem_w_kv=mem_w_kv, hgrn_w_in=hgrn_w_in, hgrn_gnorm=hgrn_gnorm, hgrn_w_out=hgrn_w_out, gmlp_w_in=gmlp_w_in, gmlp_ln_g=gmlp_ln_g, gmlp_ln_b=gmlp_ln_b, gmlp_w_s=gmlp_w_s, gmlp_b_s=gmlp_b_s, gmlp_w_out=gmlp_w_out, ffn2_norm=ffn2_norm, ffn2_w_in=ffn2_w_in, ffn2_w_out=ffn2_w_out, final_norm=final_norm)
    mom_m = dict(mem_norm=m_mem_norm, lb_logits=m_lb_logits, ffn1_norm=m_ffn1_norm, ffn1_w_in=m_ffn1_w_in, ffn1_w_out=m_ffn1_w_out, mix_norm=m_mix_norm, mem_w_kv=m_mem_w_kv, hgrn_w_in=m_hgrn_w_in, hgrn_gnorm=m_hgrn_gnorm, hgrn_w_out=m_hgrn_w_out, gmlp_w_in=m_gmlp_w_in, gmlp_ln_g=m_gmlp_ln_g, gmlp_ln_b=m_gmlp_ln_b, gmlp_w_s=m_gmlp_w_s, gmlp_b_s=m_gmlp_b_s, gmlp_w_out=m_gmlp_w_out, ffn2_norm=m_ffn2_norm, ffn2_w_in=m_ffn2_w_in, ffn2_w_out=m_ffn2_w_out, final_norm=m_final_norm)
    mom_v = dict(mem_norm=v_mem_norm, lb_logits=v_lb_logits, ffn1_norm=v_ffn1_norm, ffn1_w_in=v_ffn1_w_in, ffn1_w_out=v_ffn1_w_out, mix_norm=v_mix_norm, mem_w_kv=v_mem_w_kv, hgrn_w_in=v_hgrn_w_in, hgrn_gnorm=v_hgrn_gnorm, hgrn_w_out=v_hgrn_w_out, gmlp_w_in=v_gmlp_w_in, gmlp_ln_g=v_gmlp_ln_g, gmlp_ln_b=v_gmlp_ln_b, gmlp_w_s=v_gmlp_w_s, gmlp_b_s=v_gmlp_b_s, gmlp_w_out=v_gmlp_w_out, ffn2_norm=v_ffn2_norm, ffn2_w_in=v_ffn2_w_in, ffn2_w_out=v_ffn2_w_out, final_norm=v_final_norm)
    order = list(weights)
    _, _, _, me = _mesh_pos()
    me_arr = jnp.reshape(me, (1,)).astype(jnp.int32)
    cuts = {name: c for name, c, _, _ in GROUPS}
    rows_already = tuple(name for name, c, _, n in GROUPS if c and n % 128)
    as_rows = lambda a: jnp.transpose(a, (0, 2, 1))
    for name in rows_already:
        weights[name], mom_m[name], mom_v[name] = as_rows(weights[name]), as_rows(mom_m[name]), as_rows(mom_v[name])
        cuts[name] = False

    mix1 = (("mem_w_kv", 1), ("gmlp_w_in", 0), ("gmlp_w_out", 0))
    gather_plan = (
        ((0, "ffn1_in"), (("ffn1_w_in", 0),)),
        ((0, "ffn1_out"), (("ffn1_w_out", 0),)),
        ((0, "mix_in"), _stage_pieces(0, "mix")),
        ((0, "ffn2_in"), _stage_pieces(0, "ffn2")),
        ((1, "ffn1_in"), _stage_pieces(1, "ffn1")),
        ((1, "mix_in"), mix1),
        ((1, "ffn2_in"), _stage_pieces(1, "ffn2")),
    )
    stage_of = {use: k for k, (use, _) in enumerate(gather_plan)}
    in_flight = {}

    def place(k, deps=()):
        pieces = gather_plan[k][1]
        lands = [_place_rows(weights[name], l, cuts[name], me_arr, deps=deps, name=f"place_{name}_{l}")
                 for name, l in pieces]
        if pieces is mix1:
            lands.append(_place_ln(gmlp_ln_g, gmlp_ln_b, me_arr))
        return lands

    placed = {0: place(0)}

    def start_chips(k, deps):
        lands = placed[k]
        send_sems, recv_sems, *thru, token = _copies_start(lands, lands, mode="gather_chips", deps=deps,
                                                           name=f"gather{k}_chips_start")
        in_flight[k] = (thru, send_sems, recv_sems)
        return token

    def pass_to_sibling(k, after):
        thru, send_sems, recv_sems = in_flight[k]
        outs = _copies_wait(thru, send_sems, recv_sems, after, n_lands=len(thru), mode="gather_chips",
                            name=f"gather{k}_chips_wait")
        send_sems, recv_sems, *thru, token = _copies_start(outs, outs, mode="gather_sibling",
                                                           name=f"gather{k}_sibling_start")
        in_flight[k] = (thru, send_sems, recv_sems)
        return token, token

    first_sent = start_chips(0, ())
    placed.update({k: place(k, (first_sent,)) for k in range(1, len(gather_plan))})
    placed_later = tuple(a for k in range(1, len(gather_plan)) for a in placed[k])
    points = [(i, p) for i in (0, 1) for p in ("ffn1_in", "ffn1_out", "mix_in", "mix_out", "ffn2_in", "ffn2_out")]
    pass_at = {j: points[points.index(use) - 1] for j, (use, _) in enumerate(gather_plan) if j}

    def get_weights(use, after):
        tokens, w = [], {}
        k = stage_of.get(use)
        if k == 0:
            token, landed = pass_to_sibling(0, tuple(after) + placed_later)
            tokens += [token, start_chips(1, (landed,))]
        if k is not None:
            thru, send_sems, recv_sems = in_flight[k]
            outs = _copies_wait(thru, send_sems, recv_sems, after, n_lands=len(thru), mode="gather_sibling",
                                name=f"gather{k}_sibling_wait")
            after = (outs[0],)
            pieces = gather_plan[k][1]
            w = {p: o.reshape(N_DEV * o.shape[1], D_MODEL) for p, o in zip(pieces, outs)}
            if pieces is mix1:
                w["ln_g"] = outs[-1][:, 0, :].reshape(1, GM_WIDTH)
                w["ln_b"] = outs[-1][:, 1, :].reshape(1, GM_WIDTH)
        for j, at in pass_at.items():
            if at == use:
                token, landed = pass_to_sibling(j, after)
                tokens.append(token)
                if j + 1 < len(gather_plan):
                    tokens.append(start_chips(j + 1, (landed,)))
        w["deps"] = tuple(tokens)
        return w

    scatter = {}

    def put_grads(st, grads):
        if st in ("w_s", "small"):
            slab = grads.reshape(GM_GROUPS * GM_CHUNK, GM_CHUNK) if st == "w_s" else _pack_slab(grads, name="pack_small_grads")
            land = _place_slab(slab, me_arr, name=f"{st}_place")
            send_sems, recv_sems, *thru, token = _copies_start([land], [land], mode="gather_all", name=f"{st}_start")
            scatter[st] = (thru, send_sems, recv_sems)
            return (token,)
        views = [g.reshape(N_DEV, -1, D_MODEL) for g in grads.values()]
        recv = _place_own(views, me_arr, name=f"scatter_place_l{st[0]}_{st[1]}")
        send_sems, recv_sems, *thru, token = _copies_start(views, recv, mode="scatter",
                                                           name=f"scatter_start_l{st[0]}_{st[1]}")
        scatter[st] = (tuple(grads), thru, send_sems, recv_sems)
        return (token,)

    dx, loss_part, last_sent = _step_local(
        x, mem, loss_target, get_weights, put_grads, mem_norm, lb_logits, ffn1_norm, mix_norm, hgrn_gnorm,
        gmlp_w_s, gmlp_b_s, ffn2_norm, final_norm)

    def slots_of(blk, after):
        slots = {}
        for st, entry in scatter.items():
            if isinstance(st, tuple) and st[1].startswith(blk):
                pieces, thru, send_sems, recv_sems = entry
                outs = _copies_wait(thru, send_sems, recv_sems, after, n_lands=len(thru) // 2, mode="scatter",
                                    name=f"scatter_wait_l{st[0]}_{st[1]}")
                slots.update(zip(pieces, outs))
        return slots

    grad, delta, new_m, new_v = {}, {}, {}, {}

    def adam_groups(slots, names):
        for name in names:
            layers = GROUP_LAYERS[name]
            grad[name], delta[name], new_m[name], new_v[name] = _adam_big(
                [slots[(name, l)] for l in range(layers)], weights[name], mom_m[name], mom_v[name], cuts[name],
                name=f"{name}_adamw")

    adam_groups(slots_of("ffn2", (dx, *last_sent)), ("ffn2_w_in", "ffn2_w_out"))
    adam_groups(slots_of("mix", (delta["ffn2_w_out"],)),
                ("mem_w_kv", "gmlp_w_in", "gmlp_w_out", "hgrn_w_in", "hgrn_w_out"))

    def small_parts(src):
        parts = {n: [src[n].reshape(-1, src[n].shape[-1])] for n in SLAB_AT if n not in SMALL_SHARDED}
        return parts

    w_s_rows = lambda a: a.reshape(GM_GROUPS * GM_CHUNK, GM_CHUNK)
    small_done = (delta["hgrn_w_out"],)
    (slab_slots,) = _copies_wait(*scatter["small"], small_done, n_lands=1, mode="gather_all", name="small_wait")
    (ws_slots,) = _copies_wait(*scatter["w_s"], small_done, n_lands=1, mode="gather_all", name="w_s_wait")
    (g_slab, d_slab, nm_slab, nv_slab), (g_ws, d_ws, nm_ws, nv_ws) = _adam_slabs(
        [slab_slots, ws_slots],
        [_pack_slab(small_parts(weights), name="pack_small_w"), w_s_rows(gmlp_w_s)],
        [_pack_slab(small_parts(mom_m), name="pack_small_m"), w_s_rows(m_gmlp_w_s)],
        [_pack_slab(small_parts(mom_v), name="pack_small_v"), w_s_rows(v_gmlp_w_s)])
    shapes = {n: weights[n].shape for n in SLAB_AT}
    for out, slab, ws in ((grad, g_slab, g_ws), (delta, d_slab, d_ws), (new_m, nm_slab, nm_ws), (new_v, nv_slab, nv_ws)):
        out.update(_unpack_slab(slab, shapes))
        out["gmlp_w_s"] = ws.reshape(gmlp_w_s.shape)
    blk = GM_WIDTH // N_DEV
    g_ln = [lax.dynamic_slice(g_slab[SLAB_AT[n]:SLAB_AT[n] + 2].reshape(1, GM_WIDTH), (0, me * blk), (1, blk))
            for n in SMALL_SHARDED]
    ln_out = _adam_vecs(g_ln, [weights[n] for n in SMALL_SHARDED], [mom_m[n] for n in SMALL_SHARDED],
                        [mom_v[n] for n in SMALL_SHARDED])
    for n, g, (d, nm, nv) in zip(SMALL_SHARDED, g_ln, ln_out):
        grad[n], delta[n], new_m[n], new_v[n] = g, d, nm, nv

    adam_groups(slots_of("ffn1", (delta["hgrn_w_out"], d_slab)), ("ffn1_w_in", "ffn1_w_out"))

    for name in rows_already:
        for out in (grad, delta, new_m, new_v):
            out[name] = as_rows(out[name])
    loss = lax.psum(loss_part[0, 0], MESH_AXES)
    grad_x = dx.reshape(B_LOC, SEQ, D_MODEL)
    return (loss, grad_x, *[grad[n] for n in order], *[delta[n] for n in order],
            *[new_m[n] for n in order], *[new_v[n] for n in order])


def _step_local(x, mem, loss_target, get_weights, put_grads, mem_norm, lb_logits, ffn1_norm, mix_norm, hgrn_gnorm,
                gmlp_w_s, gmlp_b_s, ffn2_norm, final_norm):
    w_s = gmlp_w_s[0]
    b_st = gmlp_b_s[0].T

    xs = x.reshape(N_TOK, D_MODEL)
    mem2d = mem.reshape(B_LOC * MEM_LEN, D_MODEL)
    mem_g = mem_norm.reshape(1, D_MODEL)
    saved, full = [], {}
    memn = _rms_fwd(mem2d, mem_g, name="mem_norm_fwd")
    for i in range(2):
        xs, s_ffn1 = _ffn_fwd(xs, ffn1_norm[i:i + 1], "ffn1", i, full, get_weights)
        full.update(get_weights((i, "mix_in"), (xs,)))
        mixer = "hgrn" if i == 0 else "gmlp"
        hm, zm = _norm_mm(xs, mix_norm[i:i + 1], full[(f"{mixer}_w_in", 0)], swiglu=False, tm=1024, tn=1280, deps=full.pop("deps", ()),
                          name=f"l{i}_mix_in")
        kv = _mm(memn, full[("mem_w_kv", i)], tb=True, tm=512, tn=512, tk=D_MODEL, out_dtype=F32, name=f"l{i}_mem_kv")
        o_mem = _attn_fwd(zm, kv, name=f"l{i}_attn")
        if i == 0:
            cat, o_pre, s_all = _hgrn_fwd(zm, o_mem, lb_logits, hgrn_gnorm)
            mix_saved = (o_pre, s_all)
        else:
            cat = _gmlp_fwd(zm, o_mem, full["ln_g"], full["ln_b"], w_s, b_st)
            mix_saved = ()
        x_mix = xs
        full.update(get_weights((i, "mix_out"), (cat,)))
        xs = _mm(cat, full[(f"{mixer}_w_out", 0)], tm=512, tn=D_MODEL, tk=cat.shape[1], out_dtype=F32, res=xs,
                 deps=full.pop("deps", ()), name=f"l{i}_mix_out")
        xs, s_ffn2 = _ffn_fwd(xs, ffn2_norm[i:i + 1], "ffn2", i, full, get_weights)
        saved.append((s_ffn1, (x_mix, hm, kv, zm, cat, mix_saved), s_ffn2))

    dx, dx16, d_final, loss_part = _loss_head(xs, final_norm.reshape(1, D_MODEL), loss_target.reshape(N_TOK, D_MODEL))

    small = {"final_norm": [d_final]}
    d_ffn1, d_ffn2, d_mix = [None, None], [None, None], [None, None]
    dmemn = jnp.zeros((B_LOC * MEM_LEN, D_MODEL), F32)
    deps = ()
    for i in (1, 0):
        s_ffn1, (x_mix, hm, kv, zm, cat, mix_saved), s_ffn2 = saved[i]
        dx, dx16, d_ffn2[i], dw_in_t, dw_out = _ffn_bwd(
            dx, dx16, s_ffn2, ffn2_norm[i:i + 1], full[("ffn2_w_in", i)], full[("ffn2_w_out", i)], f"l{i}_ffn2", deps)
        deps = put_grads((i, "ffn2"), {("ffn2_w_in", i): dw_in_t, ("ffn2_w_out", i): dw_out})
        mixer = "hgrn" if i == 0 else "gmlp"
        w_in_t, w_out = full[(f"{mixer}_w_in", 0)], full[(f"{mixer}_w_out", 0)]
        width = cat.shape[1]
        g_mix = {}
        g_mix[(f"{mixer}_w_out", 0)] = _mm(cat, dx16, ta=True, tm=1024, tn=D_MODEL, tk=N_TOK, out_dtype=BF16,
                                           deps=deps, name=f"l{i}_mix_out_wgrad")
        dcat = _mm(dx16, w_out, tb=True, tm=1024, tn=width // 2, tk=D_MODEL, out_dtype=F32, name=f"l{i}_mix_out_dgrad")
        dq, dk, dv = _attn_bwd(zm, kv, dcat, do_off=width - XA_HEADS * XA_DIM, name=f"l{i}_attn_bwd")
        if i == 0:
            dzm, dlbl, dgn = _hgrn_bwd(zm, mix_saved[0], dcat, dq, mix_saved[1], lb_logits, hgrn_gnorm)
            small["lb_logits"], small["hgrn_gnorm"] = [dlbl], [dgn]
            deps = ()
        else:
            dzm, dws, dbt, dlng, dlnb = _gmlp_bwd(zm, dcat, dq, full["ln_g"], full["ln_b"], w_s, b_st)
            small["gmlp_b_s"], small["gmlp_ln_g"], small["gmlp_ln_b"] = [dbt.T], [dlng], [dlnb]
            deps = put_grads("w_s", dws)
        g_mix[(f"{mixer}_w_in", 0)] = _mm(dzm, hm, ta=True, tm=1024, tn=D_MODEL, tk=N_TOK, out_dtype=BF16, deps=deps,
                                          name=f"l{i}_mix_in_wgrad")
        dkv = jnp.concatenate([dk, dv], axis=1)
        g_mix[("mem_w_kv", i)] = _mm(dkv, memn, ta=True, tm=512, tn=D_MODEL, tk=B_LOC * MEM_LEN, out_dtype=BF16,
                                     name=f"l{i}_mem_kv_wgrad")
        deps = put_grads((i, "mix"), g_mix)
        dx, dx16, d_mix[i] = _dgrad_norm_bwd(dzm, w_in_t, x_mix, mix_norm[i:i + 1], dx, deps=deps,
                                             name=f"l{i}_mix_in_dgrad")
        dmemn = _mm(dkv, full[("mem_w_kv", i)], tm=B_LOC * MEM_LEN, tn=D_MODEL, tk=512, out_dtype=F32, res=dmemn,
                    name=f"l{i}_mem_kv_dgrad")
        def send_small(dg, i=i, dmemn=dmemn):
            d_ffn1[i] = dg
            _, _, dmem_g = _rms_bwd(mem2d, mem_g, dmemn, dmemn, name="mem_norm_bwd")
            small.update(mem_norm=[dmem_g], ffn1_norm=d_ffn1, ffn2_norm=d_ffn2, mix_norm=d_mix)
            return put_grads("small", small)

        if i == 0:
            send_out = lambda dw_out: put_grads((0, "ffn1_out"), {("ffn1_w_out", 0): dw_out})
            dx, dx16, d_ffn1[i], dw_in_t, _ = _ffn_bwd(
                dx, dx16, s_ffn1, ffn1_norm[i:i + 1], full[("ffn1_w_in", i)], full[("ffn1_w_out", i)], f"l{i}_ffn1",
                after_out_wgrad=send_out, before_in_wgrad=send_small)
            deps = put_grads((0, "ffn1_in"), {("ffn1_w_in", 0): dw_in_t})
        else:
            dx, dx16, d_ffn1[i], dw_in_t, dw_out = _ffn_bwd(
                dx, dx16, s_ffn1, ffn1_norm[i:i + 1], full[("ffn1_w_in", i)], full[("ffn1_w_out", i)], f"l{i}_ffn1")
            deps = put_grads((i, "ffn1"), {("ffn1_w_in", i): dw_in_t, ("ffn1_w_out", i): dw_out})
    return dx, loss_part, deps
```

```python
import functools
import math

import jax
import jax.numpy as jnp
from jax import lax
from jax.experimental import pallas as pl
from jax.experimental.pallas import tpu as pltpu

F32 = jnp.float32
BF16 = jnp.bfloat16

D_MODEL = 1024
SEQ = 2048
B_LOC = 2
N_TOK = B_LOC * SEQ
MEM_LEN = 256
N_DEV = 8
EPS = 1e-6
D_FF = 2816
HG_HEADS = 8
HG_DIM = 128
HG_CHUNK = 64
HG_NCHUNK = SEQ // HG_CHUNK
GM_CHUNK = 128
GM_GROUPS = 8
GM_WIDTH = 2048
GM_GDIM = GM_WIDTH // GM_GROUPS
XA_HEADS = 4
XA_DIM = 256
XA_OFF = 4096

ADAM_LR = 0.001
ADAM_B1 = 0.9
ADAM_B2 = 0.999
ADAM_EPS = 1e-08
ADAM_WD = 0.01
ADAM_STEP = 10

VMEM_LIMIT_BYTES = 56 * 1024 * 1024
MESH_AXES = ("x", "y", "c")

GROUPS = (
    ("ffn1_w_in", True, 2, 704),
    ("ffn1_w_out", False, 2, 352),
    ("mem_w_kv", True, 2, 256),
    ("hgrn_w_in", True, 1, 640),
    ("hgrn_w_out", False, 1, 256),
    ("gmlp_w_in", True, 1, 640),
    ("gmlp_w_out", False, 1, 384),
    ("ffn2_w_in", True, 2, 704),
    ("ffn2_w_out", False, 2, 352),
)
GROUP_LAYERS = {name: layers for name, _, layers, _ in GROUPS}


def _stage_pieces(layer, block):
    if block == "mix":
        mixer = "hgrn" if layer == 0 else "gmlp"
        return (("mem_w_kv", layer), (f"{mixer}_w_in", 0), (f"{mixer}_w_out", 0))
    return ((f"{block}_w_in", layer), (f"{block}_w_out", layer))


ANY_SPEC = pl.BlockSpec(memory_space=pl.ANY)
HBM_SPEC = pl.BlockSpec(memory_space=pltpu.HBM)
SEM_SPEC = pl.BlockSpec(memory_space=pltpu.SEMAPHORE)


def _cp(*sem):
    return pltpu.CompilerParams(dimension_semantics=sem, vmem_limit_bytes=VMEM_LIMIT_BYTES)


def _sigmoid(x):
    return 0.5 * jnp.tanh(0.5 * x) + 0.5


def _gelu_parts(x):
    cdf = 0.5 * (1.0 + lax.erf(x * (1.0 / math.sqrt(2.0))))
    pdf = jnp.exp(-0.5 * x * x) * (1.0 / math.sqrt(2.0 * math.pi))
    return x * cdf, cdf + x * pdf


def _mm(a, b, *, ta=False, tb=False, tm, tn, tk, out_dtype, res=None, scale=1.0, deps=(), name):
    m, k = (a.shape[1], a.shape[0]) if ta else a.shape
    n, kb = b.shape if tb else (b.shape[1], b.shape[0])
    assert k == kb and m % tm == 0 and n % tn == 0 and k % tk == 0, (name, a.shape, b.shape)
    nk = k // tk
    dn = (((0 if ta else 1,), (1 if tb else 0,)), ((), ()))
    n_in = 2 + (res is not None) + len(deps)

    def body(*refs):
        a_ref, b_ref = refs[:2]
        r_ref = refs[2] if res is not None else None
        o_ref, scr = refs[n_in], refs[n_in + 1:]
        p = lax.dot_general(a_ref[...].astype(BF16), b_ref[...].astype(BF16), dn, preferred_element_type=F32)

        def finish(acc):
            if scale != 1.0:
                acc = scale * acc
            if r_ref is not None:
                acc = r_ref[...] + acc
            o_ref[...] = acc.astype(out_dtype)

        if nk == 1:
            finish(p)
        else:
            acc_ref = scr[0]
            kk = pl.program_id(2)

            @pl.when(kk == 0)
            def _():
                acc_ref[...] = p

            @pl.when(kk > 0)
            def _():
                acc_ref[...] += p

            @pl.when(kk == nk - 1)
            def _():
                finish(acc_ref[...])

    a_spec = pl.BlockSpec((tk, tm), lambda i, j, kk: (kk, i)) if ta else pl.BlockSpec((tm, tk), lambda i, j, kk: (i, kk))
    b_mode = dict(pipeline_mode=pl.Buffered(1)) if n == tn and nk == 1 else {}
    if tb:
        b_spec = pl.BlockSpec((tn, tk), lambda i, j, kk: (j, kk), **b_mode)
    else:
        b_spec = pl.BlockSpec((tk, tn), lambda i, j, kk: (kk, j), **b_mode)
    o_spec = pl.BlockSpec((tm, tn), lambda i, j, kk: (i, j))
    in_specs = [a_spec, b_spec] + ([o_spec] if res is not None else []) + [ANY_SPEC] * len(deps)
    args = (a, b) + ((res,) if res is not None else ()) + tuple(deps)
    return pl.pallas_call(
        body,
        name=name,
        grid=(m // tm, n // tn, nk),
        in_specs=in_specs,
        out_specs=o_spec,
        out_shape=jax.ShapeDtypeStruct((m, n), out_dtype),
        scratch_shapes=[pltpu.VMEM((tm, tn), F32)] if nk > 1 else [],
        compiler_params=_cp("parallel", "parallel", "arbitrary"),
    )(*args)


def _rms_fwd(x, g, *, name, deps=(), tm=512):
    rows = x.shape[0]

    def body(x_ref, g_ref, *rest):
        o_ref = rest[len(deps)]
        xv = x_ref[...]
        r = lax.rsqrt(jnp.mean(xv * xv, axis=-1, keepdims=True) + EPS)
        o_ref[...] = (xv * r * g_ref[...]).astype(BF16)

    row = pl.BlockSpec((tm, D_MODEL), lambda i: (i, 0))
    return pl.pallas_call(
        body,
        name=name,
        grid=(rows // tm,),
        in_specs=[row, pl.BlockSpec((1, D_MODEL), lambda i: (0, 0))] + [ANY_SPEC] * len(deps),
        out_specs=row,
        out_shape=jax.ShapeDtypeStruct((rows, D_MODEL), BF16),
        compiler_params=_cp("parallel"),
    )(x, g, *deps)


def _rms_bwd(x, g, dh, dres, *, name, deps=(), tm=512):
    rows = x.shape[0]

    def body(x_ref, g_ref, dh_ref, dres_ref, *rest):
        dx_ref, dx16_ref, dg_ref = rest[len(deps):]
        xv = x_ref[...]
        r = lax.rsqrt(jnp.mean(xv * xv, axis=-1, keepdims=True) + EPS)
        xhat = xv * r
        dhv = dh_ref[...]
        part = jnp.sum(dhv * xhat, axis=0, keepdims=True)

        @pl.when(pl.program_id(0) == 0)
        def _():
            dg_ref[...] = part

        @pl.when(pl.program_id(0) > 0)
        def _():
            dg_ref[...] += part

        dxh = dhv * g_ref[...]
        dx = dres_ref[...] + r * (dxh - xhat * jnp.mean(dxh * xhat, axis=-1, keepdims=True))
        dx_ref[...] = dx
        dx16_ref[...] = dx.astype(BF16)

    row = pl.BlockSpec((tm, D_MODEL), lambda i: (i, 0))
    vec = pl.BlockSpec((1, D_MODEL), lambda i: (0, 0))
    return pl.pallas_call(
        body,
        name=name,
        grid=(rows // tm,),
        in_specs=[row, vec, row, row] + [ANY_SPEC] * len(deps),
        out_specs=[row, row, vec],
        out_shape=[jax.ShapeDtypeStruct((rows, D_MODEL), F32), jax.ShapeDtypeStruct((rows, D_MODEL), BF16),
                   jax.ShapeDtypeStruct((1, D_MODEL), F32)],
        compiler_params=_cp("arbitrary"),
    )(x, g, dh, dres, *deps)


_NT = (((1,), (1,)), ((), ()))
_TN = (((0,), (0,)), ((), ()))


def _norm_mm(x, g, w_t, *, swiglu, name, tm, tn, deps=()):
    rows = w_t.shape[0]
    half = rows // 2
    nj = (half if swiglu else rows) // tn
    nw = 2 if swiglu else 1
    nd = len(deps)

    def body(x_ref, g_ref, *rest):
        w_refs, outs = rest[:nw], rest[nw + nd:]
        h_ref, z_ref = outs[:2]

        @pl.when(pl.program_id(1) == 0)
        def _():
            xv = x_ref[...]
            r = lax.rsqrt(jnp.mean(xv * xv, axis=-1, keepdims=True) + EPS)
            h_ref[...] = (xv * r * g_ref[...]).astype(BF16)

        h = h_ref[...]
        if swiglu:
            gate = lax.dot_general(h, w_refs[0][...], _NT, preferred_element_type=F32)
            up = lax.dot_general(h, w_refs[1][...], _NT, preferred_element_type=F32)
            s = _sigmoid(gate)
            silu = gate * s
            z_ref[0] = (up * (s + silu * (1.0 - s))).astype(BF16)
            z_ref[1] = silu.astype(BF16)
            outs[2][...] = (silu * up).astype(BF16)
        else:
            z_ref[...] = lax.dot_general(h, w_refs[0][...], _NT, preferred_element_type=F32)

    row = pl.BlockSpec((tm, D_MODEL), lambda i, j: (i, 0))
    w_specs = [pl.BlockSpec((tn, D_MODEL), lambda i, j: (j, 0))]
    out_specs = [row]
    out_shape = [jax.ShapeDtypeStruct((N_TOK, D_MODEL), BF16)]
    if swiglu:
        w_specs.append(pl.BlockSpec((tn, D_MODEL), lambda i, j: (j + nj, 0)))
        out_specs += [pl.BlockSpec((2, tm, tn), lambda i, j: (0, i, j)), pl.BlockSpec((tm, tn), lambda i, j: (i, j))]
        out_shape += [jax.ShapeDtypeStruct((2, N_TOK, half), BF16), jax.ShapeDtypeStruct((N_TOK, half), BF16)]
    else:
        out_specs.append(pl.BlockSpec((tm, tn), lambda i, j: (i, j)))
        out_shape.append(jax.ShapeDtypeStruct((N_TOK, rows), F32))
    return pl.pallas_call(
        body,
        name=name,
        grid=(N_TOK // tm, nj),
        in_specs=[row, pl.BlockSpec((1, D_MODEL), lambda i, j: (0, 0))] + w_specs + [ANY_SPEC] * nd,
        out_specs=out_specs,
        out_shape=out_shape,
        compiler_params=_cp("parallel", "arbitrary"),
    )(x, g, *([w_t] * nw), *deps)


def _swiglu_dgrad(dy16, w_out, z, *, scale, name, deps=(), tm=512, tn=1408):
    def body(dy_ref, w_ref, z_ref, *rest):
        dz_ref = rest[len(deps)]
        da = lax.dot_general(dy_ref[...], w_ref[...], _NT, preferred_element_type=F32) * scale
        dz_ref[0] = (da * z_ref[0].astype(F32)).astype(BF16)
        dz_ref[1] = (da * z_ref[1].astype(F32)).astype(BF16)

    planes = pl.BlockSpec((2, tm, tn), lambda i, j: (0, i, j))
    return pl.pallas_call(
        body,
        name=name,
        grid=(N_TOK // tm, D_FF // tn),
        in_specs=[pl.BlockSpec((tm, D_MODEL), lambda i, j: (i, 0)), pl.BlockSpec((tn, D_MODEL), lambda i, j: (j, 0)), planes]
        + [ANY_SPEC] * len(deps),
        out_specs=planes,
        out_shape=jax.ShapeDtypeStruct((2, N_TOK, D_FF), BF16),
        compiler_params=_cp("parallel", "parallel"),
    )(dy16, w_out, z, *deps)


def _planes_wgrad(dz, h, *, name, deps=(), tm=1408):
    per_plane = D_FF // tm

    def body(a_ref, b_ref, *rest):
        o_ref = rest[len(deps)]
        o_ref[...] = lax.dot_general(a_ref[...], b_ref[...], _TN, preferred_element_type=F32).astype(BF16)

    return pl.pallas_call(
        body,
        name=name,
        grid=(2 * per_plane,),
        in_specs=[pl.BlockSpec((None, N_TOK, tm),
                               lambda i: (jnp.where(i < per_plane, 0, 1), 0, jnp.where(i < per_plane, i, i - per_plane))),
                  pl.BlockSpec((N_TOK, D_MODEL), lambda i: (0, 0), pipeline_mode=pl.Buffered(1))] + [ANY_SPEC] * len(deps),
        out_specs=pl.BlockSpec((tm, D_MODEL), lambda i: (i, 0)),
        out_shape=jax.ShapeDtypeStruct((2 * D_FF, D_MODEL), BF16),
        compiler_params=_cp("parallel"),
    )(dz, h, *deps)


def _dgrad_norm_bwd(dz, w_t, x, g, dres, *, name, deps=(), tm=512):
    planes = dz.ndim == 3
    rows = w_t.shape[0]
    half = rows // 2
    nd = len(deps)

    def body(a_ref, b_ref, x_ref, g_ref, dres_ref, *rest):
        dx_ref, dx16_ref, dg_ref = rest[nd:]
        if planes:
            dh = jnp.dot(a_ref[0], b_ref[:half, :], preferred_element_type=F32) + jnp.dot(
                a_ref[1], b_ref[half:, :], preferred_element_type=F32)
        else:
            dh = jnp.dot(a_ref[...], b_ref[...], preferred_element_type=F32)
        xv = x_ref[...]
        r = lax.rsqrt(jnp.mean(xv * xv, axis=-1, keepdims=True) + EPS)
        xhat = xv * r
        part = jnp.sum(dh * xhat, axis=0, keepdims=True)

        @pl.when(pl.program_id(0) == 0)
        def _():
            dg_ref[...] = part

        @pl.when(pl.program_id(0) > 0)
        def _():
            dg_ref[...] += part

        dxh = dh * g_ref[...]
        dx = dres_ref[...] + r * (dxh - xhat * jnp.mean(dxh * xhat, axis=-1, keepdims=True))
        dx_ref[...] = dx
        dx16_ref[...] = dx.astype(BF16)

    a_spec = pl.BlockSpec((2, tm, half), lambda i: (0, i, 0)) if planes else pl.BlockSpec((tm, rows), lambda i: (i, 0))
    row = pl.BlockSpec((tm, D_MODEL), lambda i: (i, 0))
    vec = pl.BlockSpec((1, D_MODEL), lambda i: (0, 0))
    return pl.pallas_call(
        body,
        name=name,
        grid=(N_TOK // tm,),
        in_specs=[a_spec, pl.BlockSpec((rows, D_MODEL), lambda i: (0, 0), pipeline_mode=pl.Buffered(1)), row, vec, row]
        + [ANY_SPEC] * nd,
        out_specs=[row, row, vec],
        out_shape=[jax.ShapeDtypeStruct((N_TOK, D_MODEL), F32), jax.ShapeDtypeStruct((N_TOK, D_MODEL), BF16),
                   jax.ShapeDtypeStruct((1, D_MODEL), F32)],
        compiler_params=_cp("arbitrary"),
    )(dz, w_t, x, g, dres, *deps)


def _loss_head(x, g, target, *, tm=512):
    def body(x_ref, g_ref, t_ref, dx_ref, dx16_ref, dg_ref, loss_ref):
        xv = x_ref[...]
        gv = g_ref[...]
        r = lax.rsqrt(jnp.mean(xv * xv, axis=-1, keepdims=True) + EPS)
        xhat = xv * r
        err = xhat * gv - t_ref[...]
        loss_part = jnp.zeros((1, 128), F32) + 0.5 * jnp.sum(jnp.mean(err * err, axis=-1, keepdims=True))
        dy = err * (1.0 / D_MODEL)
        dg_part = jnp.sum(dy * xhat, axis=0, keepdims=True)

        @pl.when(pl.program_id(0) == 0)
        def _():
            dg_ref[...] = dg_part
            loss_ref[...] = loss_part

        @pl.when(pl.program_id(0) > 0)
        def _():
            dg_ref[...] += dg_part
            loss_ref[...] += loss_part

        dxh = dy * gv
        dx = r * (dxh - xhat * jnp.mean(dxh * xhat, axis=-1, keepdims=True))
        dx_ref[...] = dx
        dx16_ref[...] = dx.astype(BF16)

    row = pl.BlockSpec((tm, D_MODEL), lambda i: (i, 0))
    vec = pl.BlockSpec((1, D_MODEL), lambda i: (0, 0))
    return pl.pallas_call(
        body,
        name="loss_head",
        grid=(N_TOK // tm,),
        in_specs=[row, vec, row],
        out_specs=[row, row, vec, pl.BlockSpec((1, 128), lambda i: (0, 0))],
        out_shape=[
            jax.ShapeDtypeStruct((N_TOK, D_MODEL), F32),
            jax.ShapeDtypeStruct((N_TOK, D_MODEL), BF16),
            jax.ShapeDtypeStruct((1, D_MODEL), F32),
            jax.ShapeDtypeStruct((1, 128), F32),
        ],
        compiler_params=_cp("arbitrary"),
    )(x, g, target)


XA_TQ = 1024
XA_SCALE = XA_DIM ** -0.5


def _attn_probs(q16, k16):
    s = lax.dot_general(q16, k16, _NT, preferred_element_type=F32) * XA_SCALE
    e = jnp.exp(s - jnp.max(s, axis=-1, keepdims=True))
    return e / jnp.sum(e, axis=-1, keepdims=True)


def _attn_fwd(z, kv, *, name):
    nt = SEQ // XA_TQ

    def body(q_ref, k_ref, v_ref, o_ref):
        p = _attn_probs(q_ref[...].astype(BF16), k_ref[...].astype(BF16))
        o_ref[...] = jnp.dot(p.astype(BF16), v_ref[...].astype(BF16), preferred_element_type=F32).astype(BF16)

    return pl.pallas_call(
        body,
        name=name,
        grid=(B_LOC, XA_HEADS, nt),
        in_specs=[
            pl.BlockSpec((XA_TQ, XA_DIM), lambda b, h, t: (b * nt + t, XA_OFF // XA_DIM + h)),
            pl.BlockSpec((MEM_LEN, XA_DIM), lambda b, h, t: (b, h)),
            pl.BlockSpec((MEM_LEN, XA_DIM), lambda b, h, t: (b, XA_HEADS + h)),
        ],
        out_specs=pl.BlockSpec((XA_TQ, XA_DIM), lambda b, h, t: (b * nt + t, h)),
        out_shape=jax.ShapeDtypeStruct((N_TOK, XA_HEADS * XA_DIM), BF16),
        compiler_params=_cp("parallel", "parallel", "arbitrary"),
    )(z, kv, kv)


def _attn_bwd(z, kv, dcat, *, do_off, name):
    nt = SEQ // XA_TQ

    def body(q_ref, k_ref, v_ref, do_ref, dq_ref, dk_ref, dv_ref):
        q16 = q_ref[...].astype(BF16)
        k16 = k_ref[...].astype(BF16)
        v16 = v_ref[...].astype(BF16)
        do16 = do_ref[...].astype(BF16)
        p = _attn_probs(q16, k16)
        dv_part = lax.dot_general(p.astype(BF16), do16, _TN, preferred_element_type=F32)
        dp = lax.dot_general(do16, v16, _NT, preferred_element_type=F32)
        ds16 = (p * (dp - jnp.sum(dp * p, axis=-1, keepdims=True)) * XA_SCALE).astype(BF16)
        dq_ref[...] = jnp.dot(ds16, k16, preferred_element_type=F32).astype(BF16)
        dk_part = lax.dot_general(ds16, q16, _TN, preferred_element_type=F32)

        @pl.when(pl.program_id(2) == 0)
        def _():
            dk_ref[...] = dk_part
            dv_ref[...] = dv_part

        @pl.when(pl.program_id(2) > 0)
        def _():
            dk_ref[...] += dk_part
            dv_ref[...] += dv_part

    qspec = pl.BlockSpec((XA_TQ, XA_DIM), lambda b, h, t: (b * nt + t, XA_OFF // XA_DIM + h))
    kspec = lambda off: pl.BlockSpec((MEM_LEN, XA_DIM), lambda b, h, t: (b, off + h))
    return pl.pallas_call(
        body,
        name=name,
        grid=(B_LOC, XA_HEADS, nt),
        in_specs=[qspec, kspec(0), kspec(XA_HEADS),
                  pl.BlockSpec((XA_TQ, XA_DIM), lambda b, h, t: (b * nt + t, do_off // XA_DIM + h))],
        out_specs=[pl.BlockSpec((XA_TQ, XA_DIM), lambda b, h, t: (b * nt + t, h)), kspec(0), kspec(0)],
        out_shape=[
            jax.ShapeDtypeStruct((N_TOK, XA_HEADS * XA_DIM), BF16),
            jax.ShapeDtypeStruct((B_LOC * MEM_LEN, XA_HEADS * XA_DIM), F32),
            jax.ShapeDtypeStruct((B_LOC * MEM_LEN, XA_HEADS * XA_DIM), F32),
        ],
        compiler_params=_cp("parallel", "parallel", "arbitrary"),
    )(z, kv, kv, dcat)


def _tril(n):
    return lax.broadcasted_iota(jnp.int32, (n, n), 0) >= lax.broadcasted_iota(jnp.int32, (n, n), 1)


def _lower_bound(lbl):
    e = jnp.exp(lbl - jnp.max(lbl, axis=0, keepdims=True))
    p = e / jnp.sum(e, axis=0, keepdims=True)
    return p[0:1, :], p


def _hgrn_gates(zq, zf, lb, tril_f):
    sig = _sigmoid(zf)
    f = lb + (1.0 - lb) * sig
    kk = 1.0 - f
    sq = _sigmoid(zq)
    q = zq * sq
    b = jnp.dot(tril_f, jnp.log(f), preferred_element_type=F32, precision=lax.Precision.HIGHEST)
    bl = b[HG_CHUNK - 1:HG_CHUNK, :]
    return q, sq, sig, f, kk, b, bl


HG_TB = 512
HG_CPB = HG_TB // HG_CHUNK
HG_NT = SEQ // HG_TB
HG_WIDTH = HG_HEADS * HG_DIM


def _head(h, section=0):
    return slice(section * HG_WIDTH + h * HG_DIM, section * HG_WIDTH + (h + 1) * HG_DIM)


def _hgrn_fwd(z, o_mem, lb_logits, gnorm):
    def body(zq_ref, zf_ref, zi_ref, zg_ref, omem_ref, lbl_ref, gn_ref, o_ref, opre_ref, sall_ref, st_ref):
        lb, _ = _lower_bound(lbl_ref[...])
        gn = gn_ref[...]
        mask = _tril(HG_CHUNK)
        tril_f = mask.astype(F32)
        o_ref[:, HG_WIDTH:] = omem_ref[...]

        @pl.when(pl.program_id(1) == 0)
        def _():
            st_ref[...] = jnp.zeros_like(st_ref)

        def chunk(c, carry):
            rows = pl.ds(pl.multiple_of(c * HG_CHUNK, HG_CHUNK), HG_CHUNK)
            q, _, _, _, kk, b, bl = _hgrn_gates(zq_ref[rows, :], zf_ref[rows, :], lb, tril_f)
            v16 = zi_ref[rows, :].astype(BF16)
            qd16 = (q * jnp.exp(b)).astype(BF16)
            ki16 = (kk * jnp.exp(-b)).astype(BF16)
            kd16 = (kk * jnp.exp(bl - b)).astype(BF16)
            ebl = jnp.exp(bl)
            zg = zg_ref[rows, :]
            gate = zg * _sigmoid(zg)
            for h in range(HG_HEADS):
                sl = _head(h)
                a = jnp.where(mask, lax.dot_general(qd16[:, sl], ki16[:, sl], _NT, preferred_element_type=F32), 0.0)
                st = st_ref[h]
                sall_ref[0, h, c] = st
                o = jnp.dot(a.astype(BF16), v16[:, sl], preferred_element_type=F32) + lax.dot_general(
                    qd16[:, sl], st.astype(BF16), _NT, preferred_element_type=F32)
                st_ref[h] = st * ebl[:, sl] + lax.dot_general(v16[:, sl], kd16[:, sl], _TN, preferred_element_type=F32)
                opre_ref[rows, sl] = o
                r = lax.rsqrt(jnp.mean(o * o, axis=-1, keepdims=True) + EPS)
                o_ref[rows, sl] = ((o * r * gn) * gate[:, sl]).astype(BF16)
            return carry

        lax.fori_loop(0, HG_CPB, chunk, 0, unroll=2)

    zspec = lambda s: pl.BlockSpec((HG_TB, HG_WIDTH), lambda b, t: (b * HG_NT + t, s))
    return pl.pallas_call(
        body,
        name="hgrn_fwd",
        grid=(B_LOC, HG_NT),
        in_specs=[zspec(0), zspec(1), zspec(2), zspec(3), zspec(0),
                  pl.BlockSpec((3, HG_WIDTH), lambda b, t: (0, 0)), pl.BlockSpec((1, HG_DIM), lambda b, t: (0, 0))],
        out_specs=[pl.BlockSpec((HG_TB, 2 * HG_WIDTH), lambda b, t: (b * HG_NT + t, 0)), zspec(0),
                   pl.BlockSpec((1, HG_HEADS, HG_CPB, HG_DIM, HG_DIM), lambda b, t: (b, 0, t, 0, 0))],
        out_shape=[
            jax.ShapeDtypeStruct((N_TOK, 2 * HG_WIDTH), BF16),
            jax.ShapeDtypeStruct((N_TOK, HG_WIDTH), F32),
            jax.ShapeDtypeStruct((B_LOC, HG_HEADS, HG_NCHUNK, HG_DIM, HG_DIM), F32),
        ],
        scratch_shapes=[pltpu.VMEM((HG_HEADS, HG_DIM, HG_DIM), F32)],
        compiler_params=_cp("parallel", "arbitrary"),
    )(z, z, z, z, o_mem, lb_logits, gnorm)


def _hgrn_bwd(z, opre, dcat, dq_mem, sall, lb_logits, gnorm):
    def body(zq_ref, zf_ref, zi_ref, zg_ref, opre_ref, dout_ref, dqm_ref, sall_ref, lbl_ref, gn_ref,
             dz_ref, dlbl_ref, dgn_ref, dst_ref, dlb_ref, dgn_acc, db_ref, dkk_ref, dbl_ref):
        b_id, t_id = pl.program_id(0), pl.program_id(1)
        lb, p = _lower_bound(lbl_ref[...])
        gn = gn_ref[...]
        mask = _tril(HG_CHUNK)
        tril_f = mask.astype(F32)
        dz_ref[:, 4 * HG_WIDTH:] = dqm_ref[...]

        @pl.when(t_id == 0)
        def _():
            dst_ref[...] = jnp.zeros_like(dst_ref)
            dlb_ref[...] = jnp.zeros_like(dlb_ref)

        @pl.when((b_id == 0) & (t_id == 0))
        def _():
            dgn_acc[...] = jnp.zeros_like(dgn_acc)

        def chunk(i, carry):
            c = HG_CPB - 1 - i
            rows = pl.ds(pl.multiple_of(c * HG_CHUNK, HG_CHUNK), HG_CHUNK)
            zq, zg = zq_ref[rows, :], zg_ref[rows, :]
            q, sq, sig, f, kk, b, bl = _hgrn_gates(zq, zf_ref[rows, :], lb, tril_f)
            v16 = zi_ref[rows, :].astype(BF16)
            eb, enb, ebl_b, ebl = jnp.exp(b), jnp.exp(-b), jnp.exp(bl - b), jnp.exp(bl)
            qd, ki, kd = q * eb, kk * enb, kk * ebl_b
            qd16, ki16, kd16 = qd.astype(BF16), ki.astype(BF16), kd.astype(BF16)
            o_all = opre_ref[rows, :]
            dout = dout_ref[rows, :]
            sg = _sigmoid(zg)
            d_on_all = dout * (zg * sg)
            dgate = dout * (sg * (1.0 + zg * (1.0 - sg)))
            dq_scale = eb * (sq * (1.0 + zq * (1.0 - sq)))
            for h in range(HG_HEADS):
                sl = _head(h)
                o = o_all[:, sl]
                r = lax.rsqrt(jnp.mean(o * o, axis=-1, keepdims=True) + EPS)
                ohat = o * r
                d_on = d_on_all[:, sl]
                dz_ref[rows, _head(h, 3)] = (dgate[:, sl] * (ohat * gn)).astype(BF16)
                dgn_acc[...] += jnp.sum(d_on * ohat, axis=0, keepdims=True)
                dohat = d_on * gn
                do16 = (r * (dohat - ohat * jnp.mean(dohat * ohat, axis=-1, keepdims=True))).astype(BF16)
                st = sall_ref[0, h, c]
                dst = dst_ref[h]
                st16, dst16 = st.astype(BF16), dst.astype(BF16)
                qd_h, ki_h, kd_h, v_h = qd16[:, sl], ki16[:, sl], kd16[:, sl], v16[:, sl]
                a16 = jnp.where(mask, lax.dot_general(qd_h, ki_h, _NT, preferred_element_type=F32), 0.0).astype(BF16)
                da16 = jnp.where(mask, lax.dot_general(do16, v_h, _NT, preferred_element_type=F32), 0.0).astype(BF16)
                dv = lax.dot_general(a16, do16, _TN, preferred_element_type=F32) + lax.dot_general(
                    kd_h, dst16, _NT, preferred_element_type=F32)
                dqd = jnp.dot(da16, ki_h, preferred_element_type=F32) + jnp.dot(do16, st16, preferred_element_type=F32)
                dki = lax.dot_general(da16, qd_h, _TN, preferred_element_type=F32)
                dkd = jnp.dot(v_h, dst16, preferred_element_type=F32)
                dbl_ref[:, sl] = jnp.sum(dkd * kd[:, sl], axis=0, keepdims=True) + ebl[:, sl] * jnp.sum(
                    st * dst, axis=0, keepdims=True)
                dst_ref[h] = dst * ebl[:, sl] + lax.dot_general(do16, qd_h, _TN, preferred_element_type=F32)
                dz_ref[rows, _head(h, 2)] = dv.astype(BF16)
                dz_ref[rows, sl] = (dqd * dq_scale[:, sl]).astype(BF16)
                dkk_ref[:, sl] = dki * enb[:, sl] + dkd * ebl_b[:, sl]
                db_ref[:, sl] = dqd * qd[:, sl] - dki * ki[:, sl] - dkd * kd[:, sl]
            dlogf = lax.dot_general(tril_f, db_ref[...], _TN, preferred_element_type=F32,
                                    precision=lax.Precision.HIGHEST) + dbl_ref[...]
            df = dlogf / f - dkk_ref[...]
            dz_ref[rows, HG_WIDTH:2 * HG_WIDTH] = (df * (1.0 - lb) * sig * (1.0 - sig)).astype(BF16)
            dlb_ref[...] += jnp.sum(df * (1.0 - sig), axis=0, keepdims=True)
            return carry

        lax.fori_loop(0, HG_CPB, chunk, 0, unroll=2)

        @pl.when(t_id == HG_NT - 1)
        def _():
            row0 = (lax.broadcasted_iota(jnp.int32, (3, HG_WIDTH), 0) == 0).astype(F32)
            dlbl_part = dlb_ref[...] * lb * (row0 - p)

            @pl.when(b_id == 0)
            def _():
                dlbl_ref[...] = dlbl_part

            @pl.when(b_id > 0)
            def _():
                dlbl_ref[...] += dlbl_part

            dgn_ref[...] = dgn_acc[...]

    rev = lambda b, t: b * HG_NT + HG_NT - 1 - t
    zspec = lambda s: pl.BlockSpec((HG_TB, HG_WIDTH), lambda b, t: (rev(b, t), s))
    return pl.pallas_call(
        body,
        name="hgrn_bwd",
        grid=(B_LOC, HG_NT),
        in_specs=[zspec(0), zspec(1), zspec(2), zspec(3), zspec(0), zspec(0), zspec(0),
                  pl.BlockSpec((1, HG_HEADS, HG_CPB, HG_DIM, HG_DIM), lambda b, t: (b, 0, HG_NT - 1 - t, 0, 0)),
                  pl.BlockSpec((3, HG_WIDTH), lambda b, t: (0, 0)), pl.BlockSpec((1, HG_DIM), lambda b, t: (0, 0))],
        out_specs=[pl.BlockSpec((HG_TB, 5 * HG_WIDTH), lambda b, t: (rev(b, t), 0)),
                   pl.BlockSpec((3, HG_WIDTH), lambda b, t: (0, 0)), pl.BlockSpec((1, HG_DIM), lambda b, t: (0, 0))],
        out_shape=[jax.ShapeDtypeStruct((N_TOK, 5 * HG_WIDTH), BF16),
                   jax.ShapeDtypeStruct((3, HG_WIDTH), F32), jax.ShapeDtypeStruct((1, HG_DIM), F32)],
        scratch_shapes=[pltpu.VMEM((HG_HEADS, HG_DIM, HG_DIM), F32), pltpu.VMEM((1, HG_WIDTH), F32),
                        pltpu.VMEM((1, HG_DIM), F32), pltpu.VMEM((HG_CHUNK, HG_WIDTH), F32),
                        pltpu.VMEM((HG_CHUNK, HG_WIDTH), F32), pltpu.VMEM((1, HG_WIDTH), F32)],
        compiler_params=_cp("arbitrary", "arbitrary"),
    )(z, z, z, z, opre, dcat, dq_mem, sall, lb_logits, gnorm)


GM_TM = 256


def _gmlp_norm(zv, ln_g, ln_b):
    gv, dgelu = _gelu_parts(zv)
    xc = gv - jnp.mean(gv, axis=-1, keepdims=True)
    rstd = lax.rsqrt(jnp.mean(xc * xc, axis=-1, keepdims=True) + EPS)
    vhat = xc * rstd
    return vhat * ln_g + ln_b, vhat, rstd, dgelu


def _gmlp_specs():
    half = lambda j: pl.BlockSpec((GM_TM, GM_WIDTH), lambda i: (i, j))
    vec = pl.BlockSpec((1, GM_WIDTH), lambda i: (0, 0))
    w = pl.BlockSpec((GM_GROUPS, GM_CHUNK, GM_CHUNK), lambda i: (0, 0, 0))
    bt = pl.BlockSpec((GM_CHUNK, GM_GROUPS), lambda i: (0, 0))
    return half, vec, w, bt


def _gmlp_fwd(z, o_mem, ln_g, ln_b, w_s, b_st):
    def body(zu_ref, zv_ref, omem_ref, g_ref, b_ref, w_ref, bt_ref, o_ref):
        o_ref[:, GM_WIDTH:] = omem_ref[...]
        u, _ = _gelu_parts(zu_ref[...])
        v, _, _, _ = _gmlp_norm(zv_ref[...], g_ref[...], b_ref[...])
        v16 = v.astype(BF16)
        mask = _tril(GM_CHUNK)
        bt = bt_ref[...]
        for g in range(GM_GROUPS):
            wm16 = jnp.where(mask, w_ref[g], 0.0).astype(BF16)
            cols = slice(g * GM_GDIM, (g + 1) * GM_GDIM)
            for c in range(GM_TM // GM_CHUNK):
                rows = slice(c * GM_CHUNK, (c + 1) * GM_CHUNK)
                mixed = jnp.dot(wm16, v16[rows, cols], preferred_element_type=F32) + bt[:, g:g + 1]
                o_ref[rows, cols] = (u[rows, cols] * mixed).astype(BF16)

    half, vec, w, bt = _gmlp_specs()
    return pl.pallas_call(
        body,
        name="gmlp_fwd",
        grid=(N_TOK // GM_TM,),
        in_specs=[half(0), half(1), pl.BlockSpec((GM_TM, XA_HEADS * XA_DIM), lambda i: (i, 0)), vec, vec, w, bt],
        out_specs=pl.BlockSpec((GM_TM, GM_WIDTH + XA_HEADS * XA_DIM), lambda i: (i, 0)),
        out_shape=jax.ShapeDtypeStruct((N_TOK, GM_WIDTH + XA_HEADS * XA_DIM), BF16),
        compiler_params=_cp("parallel"),
    )(z, z, o_mem, ln_g, ln_b, w_s, b_st)


def _gmlp_bwd(z, dcat, dq_mem, ln_g, ln_b, w_s, b_st):
    def body(zu_ref, zv_ref, dout_ref, dqm_ref, g_ref, b_ref, w_ref, bt_ref,
             dz_ref, dw_ref, dbt_ref, dg_ref, db_ref, dv_ref):
        dz_ref[:, 2 * GM_WIDTH:] = dqm_ref[...]
        @pl.when(pl.program_id(0) == 0)
        def _():
            dw_ref[...] = jnp.zeros_like(dw_ref)
            dbt_ref[...] = jnp.zeros_like(dbt_ref)
            dg_ref[...] = jnp.zeros_like(dg_ref)
            db_ref[...] = jnp.zeros_like(db_ref)

        zu = zu_ref[...]
        u, du_dz = _gelu_parts(zu)
        ln_g = g_ref[...]
        v, vhat, rstd, dgv_dz = _gmlp_norm(zv_ref[...], ln_g, b_ref[...])
        v16 = v.astype(BF16)
        dout = dout_ref[...]
        dmixed = dout * u
        dm16 = dmixed.astype(BF16)
        mask = _tril(GM_CHUNK)
        bt = bt_ref[...]
        group_id = lax.broadcasted_iota(jnp.int32, (1, GM_GROUPS), 1)
        dbt = jnp.zeros((GM_CHUNK, GM_GROUPS), F32)
        for g in range(GM_GROUPS):
            wm16 = jnp.where(mask, w_ref[g], 0.0).astype(BF16)
            cols = slice(g * GM_GDIM, (g + 1) * GM_GDIM)
            dw = jnp.zeros((GM_CHUNK, GM_CHUNK), F32)
            dbt_g = jnp.zeros((GM_CHUNK, 1), F32)
            for c in range(GM_TM // GM_CHUNK):
                rows = slice(c * GM_CHUNK, (c + 1) * GM_CHUNK)
                mixed = jnp.dot(wm16, v16[rows, cols], preferred_element_type=F32) + bt[:, g:g + 1]
                dz_ref[rows, cols] = (dout[rows, cols] * mixed * du_dz[rows, cols]).astype(BF16)
                dw += lax.dot_general(dm16[rows, cols], v16[rows, cols], _NT, preferred_element_type=F32)
                dbt_g += jnp.sum(dmixed[rows, cols], axis=-1, keepdims=True)
                dv_ref[rows, cols] = lax.dot_general(wm16, dm16[rows, cols], _TN, preferred_element_type=F32)
            dw_ref[g] += jnp.where(mask, dw, 0.0)
            dbt = dbt + dbt_g * (group_id == g).astype(F32)
        dbt_ref[...] += dbt
        dv = dv_ref[...]
        dg_ref[...] += jnp.sum(dv * vhat, axis=0, keepdims=True)
        db_ref[...] += jnp.sum(dv, axis=0, keepdims=True)
        dvh = dv * ln_g
        dgv = rstd * (dvh - jnp.mean(dvh, axis=-1, keepdims=True) - vhat * jnp.mean(dvh * vhat, axis=-1, keepdims=True))
        dz_ref[:, GM_WIDTH:2 * GM_WIDTH] = (dgv * dgv_dz).astype(BF16)

    half, vec, w, bt = _gmlp_specs()
    dz_width = 2 * GM_WIDTH + XA_HEADS * XA_DIM
    return pl.pallas_call(
        body,
        name="gmlp_bwd",
        grid=(N_TOK // GM_TM,),
        in_specs=[half(0), half(1), half(0), pl.BlockSpec((GM_TM, XA_HEADS * XA_DIM), lambda i: (i, 0)), vec, vec, w, bt],
        out_specs=[pl.BlockSpec((GM_TM, dz_width), lambda i: (i, 0)), w, bt, vec, vec],
        out_shape=[jax.ShapeDtypeStruct((N_TOK, dz_width), BF16),
                   jax.ShapeDtypeStruct((GM_GROUPS, GM_CHUNK, GM_CHUNK), F32),
                   jax.ShapeDtypeStruct((GM_CHUNK, GM_GROUPS), F32),
                   jax.ShapeDtypeStruct((1, GM_WIDTH), F32), jax.ShapeDtypeStruct((1, GM_WIDTH), F32)],
        scratch_shapes=[pltpu.VMEM((GM_TM, GM_WIDTH), F32)],
        compiler_params=_cp("arbitrary"),
    )(z, z, dcat, dq_mem, ln_g, ln_b, w_s, b_st)


def _own_slot(shape):
    return pl.BlockSpec((None,) + tuple(shape), lambda i, me_ref: (me_ref[0],) + (0,) * len(shape))


def _place_rows(w, layer, cuts_columns, me, *, name, deps=()):
    _, r, c = w.shape
    n = c if cuts_columns else r

    def body(me_ref, w_ref, *rest):
        o_ref = rest[len(deps)]
        wv = w_ref[...]
        o_ref[...] = (wv.T if cuts_columns else wv).astype(BF16)

    return pl.pallas_call(
        body,
        name=name,
        grid_spec=pltpu.PrefetchScalarGridSpec(
            num_scalar_prefetch=1, grid=(1,),
            in_specs=[pl.BlockSpec((None, r, c), lambda i, me_ref: (layer, 0, 0))] + [ANY_SPEC] * len(deps),
            out_specs=_own_slot((n, D_MODEL))),
        out_shape=jax.ShapeDtypeStruct((N_DEV, n, D_MODEL), BF16),
        compiler_params=_cp("arbitrary"),
    )(me, w, *deps)


def _place_ln(ln_g, ln_b, me):
    blk = ln_g.shape[1]

    def body(me_ref, g_ref, b_ref, o_ref):
        o_ref[...] = jnp.zeros_like(o_ref)
        o_ref[0:1, :] = g_ref[...]
        o_ref[1:2, :] = b_ref[...]

    vec = pl.BlockSpec((1, blk), lambda i, me_ref: (0, 0))
    return pl.pallas_call(
        body,
        name="place_ln",
        grid_spec=pltpu.PrefetchScalarGridSpec(
            num_scalar_prefetch=1, grid=(1,), in_specs=[vec, vec], out_specs=_own_slot((8, blk))),
        out_shape=jax.ShapeDtypeStruct((N_DEV, 8, blk), F32),
        compiler_params=_cp("arbitrary"),
    )(me, ln_g, ln_b)


def _place_slab(a, me, *, name):
    def body(me_ref, a_ref, o_ref):
        o_ref[...] = a_ref[...]

    return pl.pallas_call(
        body,
        name=name,
        grid_spec=pltpu.PrefetchScalarGridSpec(
            num_scalar_prefetch=1, grid=(1,),
            in_specs=[pl.BlockSpec(a.shape, lambda i, me_ref: (0, 0))], out_specs=_own_slot(a.shape)),
        out_shape=jax.ShapeDtypeStruct((N_DEV,) + a.shape, a.dtype),
        compiler_params=_cp("arbitrary"),
    )(me, a)


def _place_own(grads, me, *, name):
    k = len(grads)

    def body(me_ref, *refs):
        for src, dst in zip(refs[:k], refs[k:]):
            dst[...] = src[...]

    specs = [_own_slot(g.shape[1:]) for g in grads]
    return pl.pallas_call(
        body,
        name=name,
        grid_spec=pltpu.PrefetchScalarGridSpec(num_scalar_prefetch=1, grid=(1,), in_specs=specs, out_specs=specs),
        out_shape=[jax.ShapeDtypeStruct(g.shape, g.dtype) for g in grads],
        compiler_params=_cp("arbitrary"),
    )(me, *grads)


def _mesh_pos():
    x, y, c = (lax.axis_index(a) for a in MESH_AXES)
    return x, y, c, 4 * x + 2 * y + c


def _peer(x, y, c, r):
    px = 1 - x if r & 4 else x
    py = 1 - y if r & 2 else y
    pc = 1 - c if r & 1 else c
    return (px, py, pc), 4 * px + 2 * py + pc


RELATIONS = {"scatter": (1, 2, 3, 4, 5, 6, 7), "gather_all": (1, 2, 3, 4, 5, 6, 7), "gather_chips": (1, 2, 4, 6),
             "gather_sibling": (2, 4, 6)}


def _peer_copies(srcs, lands, send_sems, recv_sems, mode, waits):
    x, y, c, me = _mesh_pos()
    rel = RELATIONS[mode]
    pairs = []
    for ri, r in enumerate(rel):
        if mode == "gather_sibling":
            peer, _ = _peer(x, y, c, 1)
            _, sent_blk = _peer(x, y, c, r)
            _, got_blk = _peer(x, y, c, r ^ 1)
        else:
            peer, peer_blk = _peer(x, y, c, r)
            sent_blk, got_blk = (peer_blk if mode == "scatter" else me), peer_blk
        for k, (src, land) in enumerate(zip(srcs, lands)):
            idx = k * len(rel) + ri
            sems = dict(send_sem=send_sems.at[idx], recv_sem=recv_sems.at[idx], device_id=peer,
                        device_id_type=pl.DeviceIdType.MESH)
            dst_blk = sent_blk if mode == "gather_sibling" else me
            mine = pltpu.make_async_remote_copy(src_ref=src.at[sent_blk], dst_ref=land.at[dst_blk], **sems)
            theirs = pltpu.make_async_remote_copy(src_ref=src.at[sent_blk], dst_ref=land.at[got_blk], **sems) if waits else None
            pairs.append((mine, theirs))
    return pairs


DATAFLOW = pltpu.SideEffectType.DATAFLOW_SIDE_EFFECTING


def _in_hbm(a):
    return pltpu.with_memory_space_constraint(a, pltpu.HBM)


def _copies_start(srcs, lands, *, mode, name, deps=()):
    gather = mode != "scatter"
    arrs = list(lands) if gather else list(srcs) + list(lands)
    n, k, nd = len(arrs), len(lands), len(deps)

    def body(*refs):
        ins, send_sems, recv_sems, token = refs[:n], refs[n + nd], refs[n + nd + 1], refs[2 * n + nd + 2]
        src_refs, land_refs = (ins, ins) if gather else (ins[:k], ins[k:])
        for mine, _ in _peer_copies(src_refs, land_refs, send_sems, recv_sems, mode, waits=False):
            mine.start()
        token[...] = jnp.zeros_like(token)

    n_cp = k * len(RELATIONS[mode])
    return pl.pallas_call(
        body,
        name=name,
        in_specs=[HBM_SPEC] * n + [ANY_SPEC] * nd,
        out_specs=(SEM_SPEC, SEM_SPEC, *[HBM_SPEC] * n, pl.BlockSpec(memory_space=pltpu.VMEM)),
        out_shape=(pltpu.SemaphoreType.DMA((n_cp,)), pltpu.SemaphoreType.DMA((n_cp,)),
                   *[pltpu.HBM(a.shape, a.dtype) for a in arrs], jax.ShapeDtypeStruct((8, 128), F32)),
        input_output_aliases={i: 2 + i for i in range(n)},
        compiler_params=pltpu.CompilerParams(has_side_effects=DATAFLOW),
    )(*[_in_hbm(a) for a in arrs], *deps)


def _copies_wait(arrs, send_sems, recv_sems, after, *, n_lands, mode, name):
    n, k = len(arrs), n_lands
    gather = mode != "scatter"

    def body(*refs):
        ins, send_sems, recv_sems = refs[:n], refs[n], refs[n + 1]
        src_refs, land_refs = (ins, ins) if gather else (ins[:k], ins[k:])
        for mine, theirs in _peer_copies(src_refs, land_refs, send_sems, recv_sems, mode, waits=True):
            mine.wait_send()
            theirs.wait_recv()

    outs = pl.pallas_call(
        body,
        name=name,
        in_specs=[HBM_SPEC] * n + [SEM_SPEC, SEM_SPEC] + [ANY_SPEC] * len(after),
        out_specs=[HBM_SPEC] * n,
        out_shape=[pltpu.HBM(a.shape, a.dtype) for a in arrs],
        input_output_aliases={i: i for i in range(n)},
        compiler_params=pltpu.CompilerParams(has_side_effects=DATAFLOW),
    )(*arrs, send_sems, recv_sems, *after)
    return outs[n - k:]


def _adamw(w, g, m, v):
    m = ADAM_B1 * m + (1.0 - ADAM_B1) * g
    v = ADAM_B2 * v + (1.0 - ADAM_B2) * (g * g)
    m_hat = m / (1.0 - ADAM_B1 ** ADAM_STEP)
    v_hat = v / (1.0 - ADAM_B2 ** ADAM_STEP)
    return -ADAM_LR * (m_hat / (jnp.sqrt(v_hat) + ADAM_EPS) + ADAM_WD * w), m, v


ADAM_TC = 256


def _adam_big(slots, w, m, v, cuts_columns, *, name):
    layers, n, nj = len(slots), slots[0].shape[1], D_MODEL // ADAM_TC

    def body(*refs):
        s_refs = refs[:layers]
        w_ref, m_ref, v_ref, g_ref, d_ref, nm_ref, nv_ref, acc_ref = refs[layers:]
        for ll in range(layers):
            @pl.when(pl.program_id(0) == ll)
            def _(s_ref=s_refs[ll]):
                g = s_ref[0].astype(F32)
                for s in range(1, N_DEV):
                    g = g + s_ref[s].astype(F32)
                acc_ref[...] = g

        g = acc_ref[...].T if cuts_columns else acc_ref[...]
        g_ref[...] = g
        d_ref[...], nm_ref[...], nv_ref[...] = _adamw(w_ref[...], g, m_ref[...], v_ref[...])

    def slot_spec(ll):
        return pl.BlockSpec((N_DEV, n, ADAM_TC),
                            lambda l, j: (0, 0, jnp.where(l < ll, 0, jnp.where(l > ll, nj - 1, j))))

    if cuts_columns:
        w_spec = pl.BlockSpec((None, ADAM_TC, n), lambda l, j: (l, j, 0))
    else:
        w_spec = pl.BlockSpec((None, n, ADAM_TC), lambda l, j: (l, 0, j))
    return pl.pallas_call(
        body,
        name=name,
        grid=(layers, nj),
        in_specs=[slot_spec(ll) for ll in range(layers)] + [w_spec] * 3,
        out_specs=[w_spec] * 4,
        out_shape=[jax.ShapeDtypeStruct(w.shape, F32)] * 4,
        scratch_shapes=[pltpu.VMEM((n, ADAM_TC), F32)],
        compiler_params=_cp("arbitrary", "arbitrary"),
    )(*slots, w, m, v)


def _adam_slabs(slots, ws, ms, vs):
    n = len(slots)

    def body(*refs):
        ins, outs = refs[:4 * n], refs[4 * n:]
        for k in range(n):
            s_ref, w_ref, m_ref, v_ref = ins[k], ins[n + k], ins[2 * n + k], ins[3 * n + k]
            g = s_ref[0]
            for s in range(1, N_DEV):
                g = g + s_ref[s]
            outs[4 * k][...] = g
            outs[4 * k + 1][...], outs[4 * k + 2][...], outs[4 * k + 3][...] = _adamw(w_ref[...], g, m_ref[...], v_ref[...])

    res = pl.pallas_call(
        body,
        name="small_adamw",
        out_shape=[jax.ShapeDtypeStruct(w.shape, F32) for w in ws for _ in range(4)],
        compiler_params=pltpu.CompilerParams(vmem_limit_bytes=VMEM_LIMIT_BYTES),
    )(*slots, *ws, *ms, *vs)
    return [res[4 * k:4 * k + 4] for k in range(n)]


def _adam_vecs(gs, ws, ms, vs):
    n = len(gs)

    def body(*refs):
        ins, outs = refs[:4 * n], refs[4 * n:]
        for k in range(n):
            outs[3 * k][...], outs[3 * k + 1][...], outs[3 * k + 2][...] = _adamw(
                ins[n + k][...], ins[k][...], ins[2 * n + k][...], ins[3 * n + k][...])

    res = pl.pallas_call(
        body,
        name="ln_adamw",
        out_shape=[jax.ShapeDtypeStruct(w.shape, F32) for w in ws for _ in range(3)],
        compiler_params=pltpu.CompilerParams(vmem_limit_bytes=VMEM_LIMIT_BYTES),
    )(*gs, *ws, *ms, *vs)
    return [res[3 * k:3 * k + 3] for k in range(n)]


SLAB_AT = dict(mem_norm=0, lb_logits=1, ffn1_norm=4, mix_norm=6, hgrn_gnorm=8, gmlp_ln_g=9, gmlp_ln_b=11,
               gmlp_b_s=13, ffn2_norm=14, final_norm=16)
SLAB_ROWS = 24
SMALL_SHARDED = ("gmlp_ln_g", "gmlp_ln_b")


def _pack_slab(parts, *, name):
    flat, plan = [], []
    for pname, at in SLAB_AT.items():
        for a in parts.get(pname, ()):
            flat.append(a)
            plan.append((at, a.shape))
            at += max(1, a.shape[0] * a.shape[1] // D_MODEL)

    def body(*refs):
        o_ref = refs[-1]
        o_ref[...] = jnp.zeros_like(o_ref)
        for ref, (at, (r, w)) in zip(refs, plan):
            if w == D_MODEL or r == 1 and w < D_MODEL:
                o_ref[at:at + r, 0:w] = ref[...]
            elif w < D_MODEL:
                for j in range(r):
                    o_ref[at:at + 1, j * w:(j + 1) * w] = ref[j:j + 1, :]
            else:
                for j in range(w // D_MODEL):
                    o_ref[at + j:at + j + 1, :] = ref[:, j * D_MODEL:(j + 1) * D_MODEL]

    return pl.pallas_call(
        body,
        name=name,
        out_shape=jax.ShapeDtypeStruct((SLAB_ROWS, D_MODEL), F32),
        compiler_params=pltpu.CompilerParams(vmem_limit_bytes=VMEM_LIMIT_BYTES),
    )(*flat)


def _unpack_slab(slab, shapes):
    out = {}
    for pname, at in SLAB_AT.items():
        if pname in SMALL_SHARDED:
            continue
        size = math.prod(shapes[pname])
        rows = max(1, size // D_MODEL)
        out[pname] = slab[at:at + rows].reshape(-1)[:size].reshape(shapes[pname])
    return out


def _ffn_fwd(x, norm_g, block, layer, full, get_weights):
    tag = f"l{layer}_{block}"
    full.update(get_weights((layer, f"{block}_in"), (x,)))
    h, z, act = _norm_mm(x, norm_g, full[(f"{block}_w_in", layer)], swiglu=True, tm=512, tn=1408, deps=full.pop("deps", ()),
                         name=f"{tag}_in")
    full.update(get_weights((layer, f"{block}_out"), (act,)))
    y = _mm(act, full[(f"{block}_w_out", layer)], tm=512, tn=D_MODEL, tk=D_FF, out_dtype=F32, res=x, scale=0.5,
            deps=full.pop("deps", ()), name=f"{tag}_out")
    return y, (x, h, z, act)


def _ffn_bwd(dy, dy16, saved, norm_g, w_in_t, w_out, tag, deps=(), after_out_wgrad=None, before_in_wgrad=None):
    x, h, z, act = saved
    dw_out = _mm(act, dy16, ta=True, tm=1408, tn=D_MODEL, tk=N_TOK, out_dtype=BF16, scale=0.5, deps=deps,
                 name=f"{tag}_out_wgrad")
    sent = after_out_wgrad(dw_out) if after_out_wgrad is not None else ()
    dz = _swiglu_dgrad(dy16, w_out, z, scale=0.5, deps=sent, name=f"{tag}_out_dgrad")
    if before_in_wgrad is None:
        dw_in_t = _planes_wgrad(dz, h, name=f"{tag}_in_wgrad")
        dx, dx16, dg = _dgrad_norm_bwd(dz, w_in_t, x, norm_g, dy, name=f"{tag}_in_dgrad")
    else:
        dx, dx16, dg = _dgrad_norm_bwd(dz, w_in_t, x, norm_g, dy, name=f"{tag}_in_dgrad")
        dw_in_t = _planes_wgrad(dz, h, deps=before_in_wgrad(dg), name=f"{tag}_in_wgrad")
    return dx, dx16, dg, dw_in_t, dw_out


def kernel(x, mem, mem_norm, lb_logits, ffn1_norm, ffn1_w_in, ffn1_w_out, mix_norm, mem_w_kv, hgrn_w_in, hgrn_gnorm, hgrn_w_out, gmlp_w_in, gmlp_ln_g, gmlp_ln_b, gmlp_w_s, gmlp_b_s, gmlp_w_out, ffn2_norm, ffn2_w_in, ffn2_w_out, final_norm, loss_target, m_mem_norm, m_lb_logits, m_ffn1_norm, m_ffn1_w_in, m_ffn1_w_out, m_mix_norm, m_mem_w_kv, m_hgrn_w_in, m_hgrn_gnorm, m_hgrn_w_out, m_gmlp_w_in, m_gmlp_ln_g, m_gmlp_ln_b, m_gmlp_w_s, m_gmlp_b_s, m_gmlp_w_out, m_ffn2_norm, m_ffn2_w_in, m_ffn2_w_out, m_final_norm, v_mem_norm, v_lb_logits, v_ffn1_norm, v_ffn1_w_in, v_ffn1_w_out, v_mix_norm, v_mem_w_kv, v_hgrn_w_in, v_hgrn_gnorm, v_hgrn_w_out, v_gmlp_w_in, v_gmlp_ln_g, v_gmlp_ln_b, v_gmlp_w_s, v_gmlp_b_s, v_gmlp_w_out, v_ffn2_norm, v_ffn2_w_in, v_ffn2_w_out, v_final_norm):
    weights = dict(mem_norm=mem_norm, lb_logits=lb_logits, ffn1_norm=ffn1_norm, ffn1_w_in=ffn1_w_in, ffn1_w_out=ffn1_w_out, mix_norm=mix_norm, mem_w_kv=mem_w_kv, hgrn_w_in=hgrn_w_in, hgrn_gnorm=hgrn_gnorm, hgrn_w_out=hgrn_w_out, gmlp_w_in=gmlp_w_in, gmlp_ln_g=gmlp_ln_g, gmlp_ln_b=gmlp_ln_b, gmlp_w_s=gmlp_w_s, gmlp_b_s=gmlp_b_s, gmlp_w_out=gmlp_w_out, ffn2_norm=ffn2_norm, ffn2_w_in=ffn2_w_in, ffn2_w_out=ffn2_w_out, final_norm=final_norm)
    mom_m = dict(mem_norm=m_mem_norm, lb_logits=m_lb_logits, ffn1_norm=m_ffn1_norm, ffn1_w_in=m_ffn1_w_in, ffn1_w_out=m_ffn1_w_out, mix_norm=m_mix_norm, mem_w_kv=m_mem_w_kv, hgrn_w_in=m_hgrn_w_in, hgrn_gnorm=m_hgrn_gnorm, hgrn_w_out=m_hgrn_w_out, gmlp_w_in=m_gmlp_w_in, gmlp_ln_g=m_gmlp_ln_g, gmlp_ln_b=m_gmlp_ln_b, gmlp_w_s=m_gmlp_w_s, gmlp_b_s=m_gmlp_b_s, gmlp_w_out=m_gmlp_w_out, ffn2_norm=m_ffn2_norm, ffn2_w_in=m_ffn2_w_in, ffn2_w_out=m_ffn2_w_out, final_norm=m_final_norm)
    mom_v = dict(mem_norm=v_mem_norm, lb_logits=v_lb_logits, ffn1_norm=v_ffn1_norm, ffn1_w_in=v_ffn1_w_in, ffn1_w_out=v_ffn1_w_out, mix_norm=v_mix_norm, mem_w_kv=v_mem_w_kv, hgrn_w_in=v_hgrn_w_in, hgrn_gnorm=v_hgrn_gnorm, hgrn_w_out=v_hgrn_w_out, gmlp_w_in=v_gmlp_w_in, gmlp_ln_g=v_gmlp_ln_g, gmlp_ln_b=v_gmlp_ln_b, gmlp_w_s=v_gmlp_w_s, gmlp_b_s=v_gmlp_b_s, gmlp_w_out=v_gmlp_w_out, ffn2_norm=v_ffn2_norm, ffn2_w_in=v_ffn2_w_in, ffn2_w_out=v_ffn2_w_out, final_norm=v_final_norm)
    order = list(weights)
    _, _, _, me = _mesh_pos()
    me_arr = jnp.reshape(me, (1,)).astype(jnp.int32)
    cuts = {name: c for name, c, _, _ in GROUPS}
    rows_already = tuple(name for name, c, _, n in GROUPS if c and n % 128)
    as_rows = lambda a: jnp.transpose(a, (0, 2, 1))
    for name in rows_already:
        weights[name], mom_m[name], mom_v[name] = as_rows(weights[name]), as_rows(mom_m[name]), as_rows(mom_v[name])
        cuts[name] = False

    mix1 = (("mem_w_kv", 1), ("gmlp_w_in", 0), ("gmlp_w_out", 0))
    gather_plan = (
        ((0, "ffn1_in"), (("ffn1_w_in", 0),)),
        ((0, "ffn1_out"), (("ffn1_w_out", 0),)),
        ((0, "mix_in"), _stage_pieces(0, "mix")),
        ((0, "ffn2_in"), _stage_pieces(0, "ffn2")),
        ((1, "ffn1_in"), _stage_pieces(1, "ffn1")),
        ((1, "mix_in"), mix1),
        ((1, "ffn2_in"), _stage_pieces(1, "ffn2")),
    )
    stage_of = {use: k for k, (use, _) in enumerate(gather_plan)}
    in_flight = {}

    def place(k, deps=()):
        pieces = gather_plan[k][1]
        lands = [_place_rows(weights[name], l, cuts[name], me_arr, deps=deps, name=f"place_{name}_{l}")
                 for name, l in pieces]
        if pieces is mix1:
            lands.append(_place_ln(gmlp_ln_g, gmlp_ln_b, me_arr))
        return lands

    placed = {0: place(0)}

    def start_chips(k, deps):
        lands = placed[k]
        send_sems, recv_sems, *thru, token = _copies_start(lands, lands, mode="gather_chips", deps=deps,
                                                           name=f"gather{k}_chips_start")
        in_flight[k] = (thru, send_sems, recv_sems)
        return token

    def pass_to_sibling(k, after):
        thru, send_sems, recv_sems = in_flight[k]
        outs = _copies_wait(thru, send_sems, recv_sems, after, n_lands=len(thru), mode="gather_chips",
                            name=f"gather{k}_chips_wait")
        send_sems, recv_sems, *thru, token = _copies_start(outs, outs, mode="gather_sibling",
                                                           name=f"gather{k}_sibling_start")
        in_flight[k] = (thru, send_sems, recv_sems)
        return token, token

    first_sent = start_chips(0, ())
    placed.update({k: place(k, (first_sent,)) for k in range(1, len(gather_plan))})
    placed_later = tuple(a for k in range(1, len(gather_plan)) for a in placed[k])
    points = [(i, p) for i in (0, 1) for p in ("ffn1_in", "ffn1_out", "mix_in", "mix_out", "ffn2_in", "ffn2_out")]
    pass_at = {j: points[points.index(use) - 1] for j, (use, _) in enumerate(gather_plan) if j}

    def get_weights(use, after):
        tokens, w = [], {}
        k = stage_of.get(use)
        if k == 0:
            token, landed = pass_to_sibling(0, tuple(after) + placed_later)
            tokens += [token, start_chips(1, (landed,))]
        if k is not None:
            thru, send_sems, recv_sems = in_flight[k]
            outs = _copies_wait(thru, send_sems, recv_sems, after, n_lands=len(thru), mode="gather_sibling",
                                name=f"gather{k}_sibling_wait")
            after = (outs[0],)
            pieces = gather_plan[k][1]
            w = {p: o.reshape(N_DEV * o.shape[1], D_MODEL) for p, o in zip(pieces, outs)}
            if pieces is mix1:
                w["ln_g"] = outs[-1][:, 0, :].reshape(1, GM_WIDTH)
                w["ln_b"] = outs[-1][:, 1, :].reshape(1, GM_WIDTH)
        for j, at in pass_at.items():
            if at == use:
                token, landed = pass_to_sibling(j, after)
                tokens.append(token)
                if j + 1 < len(gather_plan):
                    tokens.append(start_chips(j + 1, (landed,)))
        w["deps"] = tuple(tokens)
        return w

    scatter = {}

    def put_grads(st, grads):
        if st in ("w_s", "small"):
            slab = grads.reshape(GM_GROUPS * GM_CHUNK, GM_CHUNK) if st == "w_s" else _pack_slab(grads, name="pack_small_grads")
            land = _place_slab(slab, me_arr, name=f"{st}_place")
            send_sems, recv_sems, *thru, token = _copies_start([land], [land], mode="gather_all", name=f"{st}_start")
            scatter[st] = (thru, send_sems, recv_sems)
            return (token,)
        views = [g.reshape(N_DEV, -1, D_MODEL) for g in grads.values()]
        recv = _place_own(views, me_arr, name=f"scatter_place_l{st[0]}_{st[1]}")
        send_sems, recv_sems, *thru, token = _copies_start(views, recv, mode="scatter",
                                                           name=f"scatter_start_l{st[0]}_{st[1]}")
        scatter[st] = (tuple(grads), thru, send_sems, recv_sems)
        return (token,)

    dx, loss_part, last_sent = _step_local(
        x, mem, loss_target, get_weights, put_grads, mem_norm, lb_logits, ffn1_norm, mix_norm, hgrn_gnorm,
        gmlp_w_s, gmlp_b_s, ffn2_norm, final_norm)

    slots = {}

    def wait_grads(blk, after, last=False):
        for st, entry in scatter.items():
            if isinstance(st, tuple) and st[1].startswith(blk) and (st == (0, "ffn1_in")) == last:
                pieces, thru, send_sems, recv_sems = entry
                outs = _copies_wait(thru, send_sems, recv_sems, after, n_lands=len(thru) // 2, mode="scatter",
                                    name=f"scatter_wait_l{st[0]}_{st[1]}")
                slots.update(zip(pieces, outs))

    grad, delta, new_m, new_v = {}, {}, {}, {}

    def adam_groups(names):
        for name in names:
            layers = GROUP_LAYERS[name]
            grad[name], delta[name], new_m[name], new_v[name] = _adam_big(
                [slots[(name, l)] for l in range(layers)], weights[name], mom_m[name], mom_v[name], cuts[name],
                name=f"{name}_adamw")

    wait_grads("ffn2", (dx, *last_sent))
    adam_groups(("ffn2_w_in", "ffn2_w_out"))
    wait_grads("mix", (delta["ffn2_w_out"],))
    adam_groups(("mem_w_kv", "gmlp_w_in", "gmlp_w_out", "hgrn_w_in", "hgrn_w_out"))
    wait_grads("ffn1", (delta["hgrn_w_out"],))
    adam_groups(("ffn1_w_out",))

    def small_parts(src):
        parts = {n: [src[n].reshape(-1, src[n].shape[-1])] for n in SLAB_AT if n not in SMALL_SHARDED}
        return parts

    w_s_rows = lambda a: a.reshape(GM_GROUPS * GM_CHUNK, GM_CHUNK)
    small_done = (delta["hgrn_w_out"],)
    (slab_slots,) = _copies_wait(*scatter["small"], small_done, n_lands=1, mode="gather_all", name="small_wait")
    (ws_slots,) = _copies_wait(*scatter["w_s"], small_done, n_lands=1, mode="gather_all", name="w_s_wait")
    (g_slab, d_slab, nm_slab, nv_slab), (g_ws, d_ws, nm_ws, nv_ws) = _adam_slabs(
        [slab_slots, ws_slots],
        [_pack_slab(small_parts(weights), name="pack_small_w"), w_s_rows(gmlp_w_s)],
        [_pack_slab(small_parts(mom_m), name="pack_small_m"), w_s_rows(m_gmlp_w_s)],
        [_pack_slab(small_parts(mom_v), name="pack_small_v"), w_s_rows(v_gmlp_w_s)])
    shapes = {n: weights[n].shape for n in SLAB_AT}
    for out, slab, ws in ((grad, g_slab, g_ws), (delta, d_slab, d_ws), (new_m, nm_slab, nm_ws), (new_v, nv_slab, nv_ws)):
        out.update(_unpack_slab(slab, shapes))
        out["gmlp_w_s"] = ws.reshape(gmlp_w_s.shape)
    blk = GM_WIDTH // N_DEV
    g_ln = [lax.dynamic_slice(g_slab[SLAB_AT[n]:SLAB_AT[n] + 2].reshape(1, GM_WIDTH), (0, me * blk), (1, blk))
            for n in SMALL_SHARDED]
    ln_out = _adam_vecs(g_ln, [weights[n] for n in SMALL_SHARDED], [mom_m[n] for n in SMALL_SHARDED],
                        [mom_v[n] for n in SMALL_SHARDED])
    for n, g, (d, nm, nv) in zip(SMALL_SHARDED, g_ln, ln_out):
        grad[n], delta[n], new_m[n], new_v[n] = g, d, nm, nv

    wait_grads("ffn1", tuple(delta[n] for n in delta if n in GROUP_LAYERS) + (d_slab,), last=True)
    adam_groups(("ffn1_w_in",))

    for name in rows_already:
        for out in (grad, delta, new_m, new_v):
            out[name] = as_rows(out[name])
    loss = lax.psum(loss_part[0, 0], MESH_AXES)
    grad_x = dx.reshape(B_LOC, SEQ, D_MODEL)
    return (loss, grad_x, *[grad[n] for n in order], *[delta[n] for n in order],
            *[new_m[n] for n in order], *[new_v[n] for n in order])


def _step_local(x, mem, loss_target, get_weights, put_grads, mem_norm, lb_logits, ffn1_norm, mix_norm, hgrn_gnorm,
                gmlp_w_s, gmlp_b_s, ffn2_norm, final_norm):
    w_s = gmlp_w_s[0]
    b_st = gmlp_b_s[0].T

    xs = x.reshape(N_TOK, D_MODEL)
    mem2d = mem.reshape(B_LOC * MEM_LEN, D_MODEL)
    mem_g = mem_norm.reshape(1, D_MODEL)
    saved, full = [], {}
    memn = _rms_fwd(mem2d, mem_g, name="mem_norm_fwd")
    for i in range(2):
        xs, s_ffn1 = _ffn_fwd(xs, ffn1_norm[i:i + 1], "ffn1", i, full, get_weights)
        full.update(get_weights((i, "mix_in"), (xs,)))
        mixer = "hgrn" if i == 0 else "gmlp"
        hm, zm = _norm_mm(xs, mix_norm[i:i + 1], full[(f"{mixer}_w_in", 0)], swiglu=False, tm=1024, tn=1280, deps=full.pop("deps", ()),
                          name=f"l{i}_mix_in")
        kv = _mm(memn, full[("mem_w_kv", i)], tb=True, tm=512, tn=512, tk=D_MODEL, out_dtype=F32, name=f"l{i}_mem_kv")
        o_mem = _attn_fwd(zm, kv, name=f"l{i}_attn")
        if i == 0:
            cat, o_pre, s_all = _hgrn_fwd(zm, o_mem, lb_logits, hgrn_gnorm)
            mix_saved = (o_pre, s_all)
        else:
            cat = _gmlp_fwd(zm, o_mem, full["ln_g"], full["ln_b"], w_s, b_st)
            mix_saved = ()
        x_mix = xs
        full.update(get_weights((i, "mix_out"), (cat,)))
        xs = _mm(cat, full[(f"{mixer}_w_out", 0)], tm=512, tn=D_MODEL, tk=cat.shape[1], out_dtype=F32, res=xs,
                 deps=full.pop("deps", ()), name=f"l{i}_mix_out")
        xs, s_ffn2 = _ffn_fwd(xs, ffn2_norm[i:i + 1], "ffn2", i, full, get_weights)
        saved.append((s_ffn1, (x_mix, hm, kv, zm, cat, mix_saved), s_ffn2))

    dx, dx16, d_final, loss_part = _loss_head(xs, final_norm.reshape(1, D_MODEL), loss_target.reshape(N_TOK, D_MODEL))

    small = {"final_norm": [d_final]}
    d_ffn1, d_ffn2, d_mix = [None, None], [None, None], [None, None]
    dmemn = jnp.zeros((B_LOC * MEM_LEN, D_MODEL), F32)
    deps = ()
    for i in (1, 0):
        s_ffn1, (x_mix, hm, kv, zm, cat, mix_saved), s_ffn2 = saved[i]
        dx, dx16, d_ffn2[i], dw_in_t, dw_out = _ffn_bwd(
            dx, dx16, s_ffn2, ffn2_norm[i:i + 1], full[("ffn2_w_in", i)], full[("ffn2_w_out", i)], f"l{i}_ffn2", deps)
        deps = put_grads((i, "ffn2"), {("ffn2_w_in", i): dw_in_t, ("ffn2_w_out", i): dw_out})
        mixer = "hgrn" if i == 0 else "gmlp"
        w_in_t, w_out = full[(f"{mixer}_w_in", 0)], full[(f"{mixer}_w_out", 0)]
        width = cat.shape[1]
        g_mix = {}
        g_mix[(f"{mixer}_w_out", 0)] = _mm(cat, dx16, ta=True, tm=1024, tn=D_MODEL, tk=N_TOK, out_dtype=BF16,
                                           deps=deps, name=f"l{i}_mix_out_wgrad")
        dcat = _mm(dx16, w_out, tb=True, tm=1024, tn=width // 2, tk=D_MODEL, out_dtype=F32, name=f"l{i}_mix_out_dgrad")
        dq, dk, dv = _attn_bwd(zm, kv, dcat, do_off=width - XA_HEADS * XA_DIM, name=f"l{i}_attn_bwd")
        if i == 0:
            dzm, dlbl, dgn = _hgrn_bwd(zm, mix_saved[0], dcat, dq, mix_saved[1], lb_logits, hgrn_gnorm)
            small["lb_logits"], small["hgrn_gnorm"] = [dlbl], [dgn]
            deps = ()
        else:
            dzm, dws, dbt, dlng, dlnb = _gmlp_bwd(zm, dcat, dq, full["ln_g"], full["ln_b"], w_s, b_st)
            small["gmlp_b_s"], small["gmlp_ln_g"], small["gmlp_ln_b"] = [dbt.T], [dlng], [dlnb]
            deps = put_grads("w_s", dws)
        g_mix[(f"{mixer}_w_in", 0)] = _mm(dzm, hm, ta=True, tm=1024, tn=D_MODEL, tk=N_TOK, out_dtype=BF16, deps=deps,
                                          name=f"l{i}_mix_in_wgrad")
        dkv = jnp.concatenate([dk, dv], axis=1)
        g_mix[("mem_w_kv", i)] = _mm(dkv, memn, ta=True, tm=512, tn=D_MODEL, tk=B_LOC * MEM_LEN, out_dtype=BF16,
                                     name=f"l{i}_mem_kv_wgrad")
        deps = put_grads((i, "mix"), g_mix)
        dx, dx16, d_mix[i] = _dgrad_norm_bwd(dzm, w_in_t, x_mix, mix_norm[i:i + 1], dx, deps=deps,
                                             name=f"l{i}_mix_in_dgrad")
        dmemn = _mm(dkv, full[("mem_w_kv", i)], tm=B_LOC * MEM_LEN, tn=D_MODEL, tk=512, out_dtype=F32, res=dmemn,
                    name=f"l{i}_mem_kv_dgrad")
        def send_small(dg, i=i, dmemn=dmemn):
            d_ffn1[i] = dg
            _, _, dmem_g = _rms_bwd(mem2d, mem_g, dmemn, dmemn, name="mem_norm_bwd")
            small.update(mem_norm=[dmem_g], ffn1_norm=d_ffn1, ffn2_norm=d_ffn2, mix_norm=d_mix)
            return put_grads("small", small)

        if i == 0:
            send_out = lambda dw_out: put_grads((0, "ffn1_out"), {("ffn1_w_out", 0): dw_out})
            dx, dx16, d_ffn1[i], dw_in_t, _ = _ffn_bwd(
                dx, dx16, s_ffn1, ffn1_norm[i:i + 1], full[("ffn1_w_in", i)], full[("ffn1_w_out", i)], f"l{i}_ffn1",
                after_out_wgrad=send_out, before_in_wgrad=send_small)
            deps = put_grads((0, "ffn1_in"), {("ffn1_w_in", 0): dw_in_t})
        else:
            dx, dx16, d_ffn1[i], dw_in_t, dw_out = _ffn_bwd(
                dx, dx16, s_ffn1, ffn1_norm[i:i + 1], full[("ffn1_w_in", i)], full[("ffn1_w_out", i)], f"l{i}_ffn1")
            deps = put_grads((i, "ffn1"), {("ffn1_w_in", i): dw_in_t, ("ffn1_w_out", i): dw_out})
    return dx, loss_part, deps
```

```python
import functools
import math

import jax
import jax.numpy as jnp
from jax import lax
from jax.experimental import pallas as pl
from jax.experimental.pallas import tpu as pltpu

F32 = jnp.float32
BF16 = jnp.bfloat16

D_MODEL = 1024
SEQ = 2048
B_LOC = 2
N_TOK = B_LOC * SEQ
MEM_LEN = 256
N_DEV = 8
EPS = 1e-6
D_FF = 2816
HG_HEADS = 8
HG_DIM = 128
HG_CHUNK = 64
HG_NCHUNK = SEQ // HG_CHUNK
GM_CHUNK = 128
GM_GROUPS = 8
GM_WIDTH = 2048
GM_GDIM = GM_WIDTH // GM_GROUPS
XA_HEADS = 4
XA_DIM = 256
XA_OFF = 4096

ADAM_LR = 0.001
ADAM_B1 = 0.9
ADAM_B2 = 0.999
ADAM_EPS = 1e-08
ADAM_WD = 0.01
ADAM_STEP = 10

VMEM_LIMIT_BYTES = 56 * 1024 * 1024
MESH_AXES = ("x", "y", "c")

GROUPS = (
    ("ffn1_w_in", True, 2, 704),
    ("ffn1_w_out", False, 2, 352),
    ("mem_w_kv", True, 2, 256),
    ("hgrn_w_in", True, 1, 640),
    ("hgrn_w_out", False, 1, 256),
    ("gmlp_w_in", True, 1, 640),
    ("gmlp_w_out", False, 1, 384),
    ("ffn2_w_in", True, 2, 704),
    ("ffn2_w_out", False, 2, 352),
)
GROUP_LAYERS = {name: layers for name, _, layers, _ in GROUPS}


def _stage_pieces(layer, block):
    if block == "mix":
        mixer = "hgrn" if layer == 0 else "gmlp"
        return (("mem_w_kv", layer), (f"{mixer}_w_in", 0), (f"{mixer}_w_out", 0))
    return ((f"{block}_w_in", layer), (f"{block}_w_out", layer))


ANY_SPEC = pl.BlockSpec(memory_space=pl.ANY)
HBM_SPEC = pl.BlockSpec(memory_space=pltpu.HBM)
SEM_SPEC = pl.BlockSpec(memory_space=pltpu.SEMAPHORE)


def _cp(*sem):
    return pltpu.CompilerParams(dimension_semantics=sem, vmem_limit_bytes=VMEM_LIMIT_BYTES)


def _sigmoid(x):
    return 0.5 * jnp.tanh(0.5 * x) + 0.5


def _gelu_parts(x):
    cdf = 0.5 * (1.0 + lax.erf(x * (1.0 / math.sqrt(2.0))))
    pdf = jnp.exp(-0.5 * x * x) * (1.0 / math.sqrt(2.0 * math.pi))
    return x * cdf, cdf + x * pdf


def _mm(a, b, *, ta=False, tb=False, tm, tn, tk, out_dtype, res=None, scale=1.0, deps=(), name):
    m, k = (a.shape[1], a.shape[0]) if ta else a.shape
    n, kb = b.shape if tb else (b.shape[1], b.shape[0])
    assert k == kb and m % tm == 0 and n % tn == 0 and k % tk == 0, (name, a.shape, b.shape)
    nk = k // tk
    dn = (((0 if ta else 1,), (1 if tb else 0,)), ((), ()))
    n_in = 2 + (res is not None) + len(deps)

    def body(*refs):
        a_ref, b_ref = refs[:2]
        r_ref = refs[2] if res is not None else None
        o_ref, scr = refs[n_in], refs[n_in + 1:]
        p = lax.dot_general(a_ref[...].astype(BF16), b_ref[...].astype(BF16), dn, preferred_element_type=F32)

        def finish(acc):
            if scale != 1.0:
                acc = scale * acc
            if r_ref is not None:
                acc = r_ref[...] + acc
            o_ref[...] = acc.astype(out_dtype)

        if nk == 1:
            finish(p)
        else:
            acc_ref = scr[0]
            kk = pl.program_id(2)

            @pl.when(kk == 0)
            def _():
                acc_ref[...] = p

            @pl.when(kk > 0)
            def _():
                acc_ref[...] += p

            @pl.when(kk == nk - 1)
            def _():
                finish(acc_ref[...])

    a_spec = pl.BlockSpec((tk, tm), lambda i, j, kk: (kk, i)) if ta else pl.BlockSpec((tm, tk), lambda i, j, kk: (i, kk))
    b_mode = dict(pipeline_mode=pl.Buffered(1)) if n == tn and nk == 1 else {}
    if tb:
        b_spec = pl.BlockSpec((tn, tk), lambda i, j, kk: (j, kk), **b_mode)
    else:
        b_spec = pl.BlockSpec((tk, tn), lambda i, j, kk: (kk, j), **b_mode)
    o_spec = pl.BlockSpec((tm, tn), lambda i, j, kk: (i, j))
    in_specs = [a_spec, b_spec] + ([o_spec] if res is not None else []) + [ANY_SPEC] * len(deps)
    args = (a, b) + ((res,) if res is not None else ()) + tuple(deps)
    return pl.pallas_call(
        body,
        name=name,
        grid=(m // tm, n // tn, nk),
        in_specs=in_specs,
        out_specs=o_spec,
        out_shape=jax.ShapeDtypeStruct((m, n), out_dtype),
        scratch_shapes=[pltpu.VMEM((tm, tn), F32)] if nk > 1 else [],
        compiler_params=_cp("parallel", "parallel", "arbitrary"),
    )(*args)


def _rms_fwd(x, g, *, name, deps=(), tm=512):
    rows = x.shape[0]

    def body(x_ref, g_ref, *rest):
        o_ref = rest[len(deps)]
        xv = x_ref[...]
        r = lax.rsqrt(jnp.mean(xv * xv, axis=-1, keepdims=True) + EPS)
        o_ref[...] = (xv * r * g_ref[...]).astype(BF16)

    row = pl.BlockSpec((tm, D_MODEL), lambda i: (i, 0))
    return pl.pallas_call(
        body,
        name=name,
        grid=(rows // tm,),
        in_specs=[row, pl.BlockSpec((1, D_MODEL), lambda i: (0, 0))] + [ANY_SPEC] * len(deps),
        out_specs=row,
        out_shape=jax.ShapeDtypeStruct((rows, D_MODEL), BF16),
        compiler_params=_cp("parallel"),
    )(x, g, *deps)


def _rms_bwd(x, g, dh, dres, *, name, deps=(), tm=512):
    rows = x.shape[0]

    def body(x_ref, g_ref, dh_ref, dres_ref, *rest):
        dx_ref, dx16_ref, dg_ref = rest[len(deps):]
        xv = x_ref[...]
        r = lax.rsqrt(jnp.mean(xv * xv, axis=-1, keepdims=True) + EPS)
        xhat = xv * r
        dhv = dh_ref[...]
        part = jnp.sum(dhv * xhat, axis=0, keepdims=True)

        @pl.when(pl.program_id(0) == 0)
        def _():
            dg_ref[...] = part

        @pl.when(pl.program_id(0) > 0)
        def _():
            dg_ref[...] += part

        dxh = dhv * g_ref[...]
        dx = dres_ref[...] + r * (dxh - xhat * jnp.mean(dxh * xhat, axis=-1, keepdims=True))
        dx_ref[...] = dx
        dx16_ref[...] = dx.astype(BF16)

    row = pl.BlockSpec((tm, D_MODEL), lambda i: (i, 0))
    vec = pl.BlockSpec((1, D_MODEL), lambda i: (0, 0))
    return pl.pallas_call(
        body,
        name=name,
        grid=(rows // tm,),
        in_specs=[row, vec, row, row] + [ANY_SPEC] * len(deps),
        out_specs=[row, row, vec],
        out_shape=[jax.ShapeDtypeStruct((rows, D_MODEL), F32), jax.ShapeDtypeStruct((rows, D_MODEL), BF16),
                   jax.ShapeDtypeStruct((1, D_MODEL), F32)],
        compiler_params=_cp("arbitrary"),
    )(x, g, dh, dres, *deps)


_NT = (((1,), (1,)), ((), ()))
_TN = (((0,), (0,)), ((), ()))


def _norm_mm(x, g, w_t, *, swiglu, name, tm, tn, deps=()):
    rows = w_t.shape[0]
    half = rows // 2
    nj = (half if swiglu else rows) // tn
    nd = len(deps)

    def body(x_ref, g_ref, w_ref, *rest):
        outs = rest[nd:]
        h_ref, z_ref = outs[:2]

        def norm():
            xv = x_ref[...]
            r = lax.rsqrt(jnp.mean(xv * xv, axis=-1, keepdims=True) + EPS)
            h_ref[...] = (xv * r * g_ref[...]).astype(BF16)

        if swiglu:
            norm()
            h = h_ref[...]
            for j in range(nj):
                cols = slice(j * tn, (j + 1) * tn)
                gate = lax.dot_general(h, w_ref[j * tn:(j + 1) * tn, :], _NT, preferred_element_type=F32)
                up = lax.dot_general(h, w_ref[half + j * tn:half + (j + 1) * tn, :], _NT, preferred_element_type=F32)
                s = _sigmoid(gate)
                silu = gate * s
                z_ref[0, :, cols] = (up * (s + silu * (1.0 - s))).astype(BF16)
                z_ref[1, :, cols] = silu.astype(BF16)
                outs[2][:, cols] = (silu * up).astype(BF16)
        else:
            j = pl.program_id(1)
            pl.when(j == 0)(norm)
            w = w_ref[pl.ds(pl.multiple_of(j * tn, tn), tn), :]
            z_ref[...] = lax.dot_general(h_ref[...], w, _NT, preferred_element_type=F32)

    grid = (N_TOK // tm,) if swiglu else (N_TOK // tm, nj)
    row = pl.BlockSpec((tm, D_MODEL), lambda i, *_: (i, 0))
    out_specs = [row]
    out_shape = [jax.ShapeDtypeStruct((N_TOK, D_MODEL), BF16)]
    if swiglu:
        out_specs += [pl.BlockSpec((2, tm, half), lambda i: (0, i, 0)), pl.BlockSpec((tm, half), lambda i: (i, 0))]
        out_shape += [jax.ShapeDtypeStruct((2, N_TOK, half), BF16), jax.ShapeDtypeStruct((N_TOK, half), BF16)]
    else:
        out_specs.append(pl.BlockSpec((tm, tn), lambda i, j: (i, j)))
        out_shape.append(jax.ShapeDtypeStruct((N_TOK, rows), F32))
    return pl.pallas_call(
        body,
        name=name,
        grid=grid,
        in_specs=[row, pl.BlockSpec((1, D_MODEL), lambda *_: (0, 0)),
                  pl.BlockSpec((rows, D_MODEL), lambda *_: (0, 0), pipeline_mode=pl.Buffered(1))] + [ANY_SPEC] * nd,
        out_specs=out_specs,
        out_shape=out_shape,
        compiler_params=_cp(*(("parallel",) if swiglu else ("parallel", "arbitrary"))),
    )(x, g, w_t, *deps)


def _swiglu_dgrad(dy16, w_out, z, *, scale, name, deps=(), tm=512, tn=1408):
    def body(dy_ref, w_ref, z_ref, *rest):
        dz_ref = rest[len(deps)]
        dy = dy_ref[...]
        for j in range(D_FF // tn):
            cols = slice(j * tn, (j + 1) * tn)
            da = lax.dot_general(dy, w_ref[cols, :], _NT, preferred_element_type=F32) * scale
            dz_ref[0, :, cols] = (da * z_ref[0, :, cols].astype(F32)).astype(BF16)
            dz_ref[1, :, cols] = (da * z_ref[1, :, cols].astype(F32)).astype(BF16)

    planes = pl.BlockSpec((2, tm, D_FF), lambda i: (0, i, 0))
    return pl.pallas_call(
        body,
        name=name,
        grid=(N_TOK // tm,),
        in_specs=[pl.BlockSpec((tm, D_MODEL), lambda i: (i, 0)),
                  pl.BlockSpec((D_FF, D_MODEL), lambda i: (0, 0), pipeline_mode=pl.Buffered(1)), planes]
        + [ANY_SPEC] * len(deps),
        out_specs=planes,
        out_shape=jax.ShapeDtypeStruct((2, N_TOK, D_FF), BF16),
        compiler_params=_cp("parallel"),
    )(dy16, w_out, z, *deps)


def _planes_wgrad(dz, h, *, name, deps=(), tm=1408):
    per_plane = D_FF // tm

    def body(a_ref, b_ref, *rest):
        o_ref = rest[len(deps)]
        o_ref[...] = lax.dot_general(a_ref[...], b_ref[...], _TN, preferred_element_type=F32).astype(BF16)

    return pl.pallas_call(
        body,
        name=name,
        grid=(2 * per_plane,),
        in_specs=[pl.BlockSpec((None, N_TOK, tm),
                               lambda i: (jnp.where(i < per_plane, 0, 1), 0, jnp.where(i < per_plane, i, i - per_plane))),
                  pl.BlockSpec((N_TOK, D_MODEL), lambda i: (0, 0), pipeline_mode=pl.Buffered(1))] + [ANY_SPEC] * len(deps),
        out_specs=pl.BlockSpec((tm, D_MODEL), lambda i: (i, 0)),
        out_shape=jax.ShapeDtypeStruct((2 * D_FF, D_MODEL), BF16),
        compiler_params=_cp("parallel"),
    )(dz, h, *deps)


def _dgrad_norm_bwd(dz, w_t, x, g, dres, *, name, deps=(), tm=512):
    planes = dz.ndim == 3
    rows = w_t.shape[0]
    half = rows // 2
    nd = len(deps)

    def body(a_ref, b_ref, x_ref, g_ref, dres_ref, *rest):
        dx_ref, dx16_ref, dg_ref = rest[nd:]
        if planes:
            dh = jnp.dot(a_ref[0], b_ref[:half, :], preferred_element_type=F32) + jnp.dot(
                a_ref[1], b_ref[half:, :], preferred_element_type=F32)
        else:
            dh = jnp.dot(a_ref[...], b_ref[...], preferred_element_type=F32)
        xv = x_ref[...]
        r = lax.rsqrt(jnp.mean(xv * xv, axis=-1, keepdims=True) + EPS)
        xhat = xv * r
        part = jnp.sum(dh * xhat, axis=0, keepdims=True)

        @pl.when(pl.program_id(0) == 0)
        def _():
            dg_ref[...] = part

        @pl.when(pl.program_id(0) > 0)
        def _():
            dg_ref[...] += part

        dxh = dh * g_ref[...]
        dx = dres_ref[...] + r * (dxh - xhat * jnp.mean(dxh * xhat, axis=-1, keepdims=True))
        dx_ref[...] = dx
        dx16_ref[...] = dx.astype(BF16)

    a_spec = pl.BlockSpec((2, tm, half), lambda i: (0, i, 0)) if planes else pl.BlockSpec((tm, rows), lambda i: (i, 0))
    row = pl.BlockSpec((tm, D_MODEL), lambda i: (i, 0))
    vec = pl.BlockSpec((1, D_MODEL), lambda i: (0, 0))
    return pl.pallas_call(
        body,
        name=name,
        grid=(N_TOK // tm,),
        in_specs=[a_spec, pl.BlockSpec((rows, D_MODEL), lambda i: (0, 0), pipeline_mode=pl.Buffered(1)), row, vec, row]
        + [ANY_SPEC] * nd,
        out_specs=[row, row, vec],
        out_shape=[jax.ShapeDtypeStruct((N_TOK, D_MODEL), F32), jax.ShapeDtypeStruct((N_TOK, D_MODEL), BF16),
                   jax.ShapeDtypeStruct((1, D_MODEL), F32)],
        compiler_params=_cp("arbitrary"),
    )(dz, w_t, x, g, dres, *deps)


def _loss_head(x, g, target, *, tm=512):
    def body(x_ref, g_ref, t_ref, dx_ref, dx16_ref, dg_ref, loss_ref):
        xv = x_ref[...]
        gv = g_ref[...]
        r = lax.rsqrt(jnp.mean(xv * xv, axis=-1, keepdims=True) + EPS)
        xhat = xv * r
        err = xhat * gv - t_ref[...]
        loss_part = jnp.zeros((1, 128), F32) + 0.5 * jnp.sum(jnp.mean(err * err, axis=-1, keepdims=True))
        dy = err * (1.0 / D_MODEL)
        dg_part = jnp.sum(dy * xhat, axis=0, keepdims=True)

        @pl.when(pl.program_id(0) == 0)
        def _():
            dg_ref[...] = dg_part
            loss_ref[...] = loss_part

        @pl.when(pl.program_id(0) > 0)
        def _():
            dg_ref[...] += dg_part
            loss_ref[...] += loss_part

        dxh = dy * gv
        dx = r * (dxh - xhat * jnp.mean(dxh * xhat, axis=-1, keepdims=True))
        dx_ref[...] = dx
        dx16_ref[...] = dx.astype(BF16)

    row = pl.BlockSpec((tm, D_MODEL), lambda i: (i, 0))
    vec = pl.BlockSpec((1, D_MODEL), lambda i: (0, 0))
    return pl.pallas_call(
        body,
        name="loss_head",
        grid=(N_TOK // tm,),
        in_specs=[row, vec, row],
        out_specs=[row, row, vec, pl.BlockSpec((1, 128), lambda i: (0, 0))],
        out_shape=[
            jax.ShapeDtypeStruct((N_TOK, D_MODEL), F32),
            jax.ShapeDtypeStruct((N_TOK, D_MODEL), BF16),
            jax.ShapeDtypeStruct((1, D_MODEL), F32),
            jax.ShapeDtypeStruct((1, 128), F32),
        ],
        compiler_params=_cp("arbitrary"),
    )(x, g, target)


XA_TQ = 1024
XA_SCALE = XA_DIM ** -0.5


def _attn_probs(q16, k16):
    s = lax.dot_general(q16, k16, _NT, preferred_element_type=F32) * XA_SCALE
    e = jnp.exp(s - jnp.max(s, axis=-1, keepdims=True))
    return e / jnp.sum(e, axis=-1, keepdims=True)


def _attn_fwd(z, kv, *, name):
    nt = SEQ // XA_TQ

    def body(q_ref, k_ref, v_ref, o_ref):
        p = _attn_probs(q_ref[...].astype(BF16), k_ref[...].astype(BF16))
        o_ref[...] = jnp.dot(p.astype(BF16), v_ref[...].astype(BF16), preferred_element_type=F32).astype(BF16)

    return pl.pallas_call(
        body,
        name=name,
        grid=(B_LOC, XA_HEADS, nt),
        in_specs=[
            pl.BlockSpec((XA_TQ, XA_DIM), lambda b, h, t: (b * nt + t, XA_OFF // XA_DIM + h)),
            pl.BlockSpec((MEM_LEN, XA_DIM), lambda b, h, t: (b, h)),
            pl.BlockSpec((MEM_LEN, XA_DIM), lambda b, h, t: (b, XA_HEADS + h)),
        ],
        out_specs=pl.BlockSpec((XA_TQ, XA_DIM), lambda b, h, t: (b * nt + t, h)),
        out_shape=jax.ShapeDtypeStruct((N_TOK, XA_HEADS * XA_DIM), BF16),
        compiler_params=_cp("parallel", "parallel", "arbitrary"),
    )(z, kv, kv)


def _attn_bwd(z, kv, dcat, *, do_off, name):
    nt = SEQ // XA_TQ

    def body(q_ref, k_ref, v_ref, do_ref, dq_ref, dk_ref, dv_ref):
        q16 = q_ref[...].astype(BF16)
        k16 = k_ref[...].astype(BF16)
        v16 = v_ref[...].astype(BF16)
        do16 = do_ref[...].astype(BF16)
        p = _attn_probs(q16, k16)
        dv_part = lax.dot_general(p.astype(BF16), do16, _TN, preferred_element_type=F32)
        dp = lax.dot_general(do16, v16, _NT, preferred_element_type=F32)
        ds16 = (p * (dp - jnp.sum(dp * p, axis=-1, keepdims=True)) * XA_SCALE).astype(BF16)
        dq_ref[...] = jnp.dot(ds16, k16, preferred_element_type=F32).astype(BF16)
        dk_part = lax.dot_general(ds16, q16, _TN, preferred_element_type=F32)

        @pl.when(pl.program_id(2) == 0)
        def _():
            dk_ref[...] = dk_part
            dv_ref[...] = dv_part

        @pl.when(pl.program_id(2) > 0)
        def _():
            dk_ref[...] += dk_part
            dv_ref[...] += dv_part

    qspec = pl.BlockSpec((XA_TQ, XA_DIM), lambda b, h, t: (b * nt + t, XA_OFF // XA_DIM + h))
    kspec = lambda off: pl.BlockSpec((MEM_LEN, XA_DIM), lambda b, h, t: (b, off + h))
    return pl.pallas_call(
        body,
        name=name,
        grid=(B_LOC, XA_HEADS, nt),
        in_specs=[qspec, kspec(0), kspec(XA_HEADS),
                  pl.BlockSpec((XA_TQ, XA_DIM), lambda b, h, t: (b * nt + t, do_off // XA_DIM + h))],
        out_specs=[pl.BlockSpec((XA_TQ, XA_DIM), lambda b, h, t: (b * nt + t, h)), kspec(0), kspec(0)],
        out_shape=[
            jax.ShapeDtypeStruct((N_TOK, XA_HEADS * XA_DIM), BF16),
            jax.ShapeDtypeStruct((B_LOC * MEM_LEN, XA_HEADS * XA_DIM), F32),
            jax.ShapeDtypeStruct((B_LOC * MEM_LEN, XA_HEADS * XA_DIM), F32),
        ],
        compiler_params=_cp("parallel", "parallel", "arbitrary"),
    )(z, kv, kv, dcat)


def _tril(n):
    return lax.broadcasted_iota(jnp.int32, (n, n), 0) >= lax.broadcasted_iota(jnp.int32, (n, n), 1)


def _lower_bound(lbl):
    e = jnp.exp(lbl - jnp.max(lbl, axis=0, keepdims=True))
    p = e / jnp.sum(e, axis=0, keepdims=True)
    return p[0:1, :], p


def _hgrn_gates(zq, zf, lb, tril_f):
    sig = _sigmoid(zf)
    f = lb + (1.0 - lb) * sig
    kk = 1.0 - f
    sq = _sigmoid(zq)
    q = zq * sq
    b = jnp.dot(tril_f, jnp.log(f), preferred_element_type=F32, precision=lax.Precision.HIGHEST)
    bl = b[HG_CHUNK - 1:HG_CHUNK, :]
    return q, sq, sig, f, kk, b, bl


HG_TB = 512
HG_CPB = HG_TB // HG_CHUNK
HG_NT = SEQ // HG_TB
HG_WIDTH = HG_HEADS * HG_DIM


def _head(h, section=0):
    return slice(section * HG_WIDTH + h * HG_DIM, section * HG_WIDTH + (h + 1) * HG_DIM)


def _hgrn_fwd(z, o_mem, lb_logits, gnorm):
    def body(zq_ref, zf_ref, zi_ref, zg_ref, omem_ref, lbl_ref, gn_ref, o_ref, opre_ref, sall_ref, st_ref):
        lb, _ = _lower_bound(lbl_ref[...])
        gn = gn_ref[...]
        mask = _tril(HG_CHUNK)
        tril_f = mask.astype(F32)
        o_ref[:, HG_WIDTH:] = omem_ref[...]

        @pl.when(pl.program_id(1) == 0)
        def _():
            st_ref[...] = jnp.zeros_like(st_ref)

        def chunk(c, carry):
            rows = pl.ds(pl.multiple_of(c * HG_CHUNK, HG_CHUNK), HG_CHUNK)
            q, _, _, _, kk, b, bl = _hgrn_gates(zq_ref[rows, :], zf_ref[rows, :], lb, tril_f)
            v16 = zi_ref[rows, :].astype(BF16)
            qd16 = (q * jnp.exp(b)).astype(BF16)
            ki16 = (kk * jnp.exp(-b)).astype(BF16)
            kd16 = (kk * jnp.exp(bl - b)).astype(BF16)
            ebl = jnp.exp(bl)
            zg = zg_ref[rows, :]
            gate = zg * _sigmoid(zg)
            for h in range(HG_HEADS):
                sl = _head(h)
                a = jnp.where(mask, lax.dot_general(qd16[:, sl], ki16[:, sl], _NT, preferred_element_type=F32), 0.0)
                st = st_ref[h]
                sall_ref[0, h, c] = st
                o = jnp.dot(a.astype(BF16), v16[:, sl], preferred_element_type=F32) + lax.dot_general(
                    qd16[:, sl], st.astype(BF16), _NT, preferred_element_type=F32)
                st_ref[h] = st * ebl[:, sl] + lax.dot_general(v16[:, sl], kd16[:, sl], _TN, preferred_element_type=F32)
                opre_ref[rows, sl] = o
                r = lax.rsqrt(jnp.mean(o * o, axis=-1, keepdims=True) + EPS)
                o_ref[rows, sl] = ((o * r * gn) * gate[:, sl]).astype(BF16)
            return carry

        lax.fori_loop(0, HG_CPB, chunk, 0, unroll=2)

    zspec = lambda s: pl.BlockSpec((HG_TB, HG_WIDTH), lambda b, t: (b * HG_NT + t, s))
    return pl.pallas_call(
        body,
        name="hgrn_fwd",
        grid=(B_LOC, HG_NT),
        in_specs=[zspec(0), zspec(1), zspec(2), zspec(3), zspec(0),
                  pl.BlockSpec((3, HG_WIDTH), lambda b, t: (0, 0)), pl.BlockSpec((1, HG_DIM), lambda b, t: (0, 0))],
        out_specs=[pl.BlockSpec((HG_TB, 2 * HG_WIDTH), lambda b, t: (b * HG_NT + t, 0)), zspec(0),
                   pl.BlockSpec((1, HG_HEADS, HG_CPB, HG_DIM, HG_DIM), lambda b, t: (b, 0, t, 0, 0))],
        out_shape=[
            jax.ShapeDtypeStruct((N_TOK, 2 * HG_WIDTH), BF16),
            jax.ShapeDtypeStruct((N_TOK, HG_WIDTH), F32),
            jax.ShapeDtypeStruct((B_LOC, HG_HEADS, HG_NCHUNK, HG_DIM, HG_DIM), F32),
        ],
        scratch_shapes=[pltpu.VMEM((HG_HEADS, HG_DIM, HG_DIM), F32)],
        compiler_params=_cp("parallel", "arbitrary"),
    )(z, z, z, z, o_mem, lb_logits, gnorm)


def _hgrn_bwd(z, opre, dcat, dq_mem, sall, lb_logits, gnorm):
    def body(zq_ref, zf_ref, zi_ref, zg_ref, opre_ref, dout_ref, dqm_ref, sall_ref, lbl_ref, gn_ref,
             dz_ref, dlbl_ref, dgn_ref, dst_ref, dlb_ref, dgn_acc, db_ref, dkk_ref, dbl_ref):
        b_id, t_id = pl.program_id(0), pl.program_id(1)
        lb, p = _lower_bound(lbl_ref[...])
        gn = gn_ref[...]
        mask = _tril(HG_CHUNK)
        tril_f = mask.astype(F32)
        dz_ref[:, 4 * HG_WIDTH:] = dqm_ref[...]

        @pl.when(t_id == 0)
        def _():
            dst_ref[...] = jnp.zeros_like(dst_ref)
            dlb_ref[...] = jnp.zeros_like(dlb_ref)

        @pl.when((b_id == 0) & (t_id == 0))
        def _():
            dgn_acc[...] = jnp.zeros_like(dgn_acc)

        def chunk(i, carry):
            c = HG_CPB - 1 - i
            rows = pl.ds(pl.multiple_of(c * HG_CHUNK, HG_CHUNK), HG_CHUNK)
            zq, zg = zq_ref[rows, :], zg_ref[rows, :]
            q, sq, sig, f, kk, b, bl = _hgrn_gates(zq, zf_ref[rows, :], lb, tril_f)
            v16 = zi_ref[rows, :].astype(BF16)
            eb, enb, ebl_b, ebl = jnp.exp(b), jnp.exp(-b), jnp.exp(bl - b), jnp.exp(bl)
            qd, ki, kd = q * eb, kk * enb, kk * ebl_b
            qd16, ki16, kd16 = qd.astype(BF16), ki.astype(BF16), kd.astype(BF16)
            o_all = opre_ref[rows, :]
            dout = dout_ref[rows, :]
            sg = _sigmoid(zg)
            d_on_all = dout * (zg * sg)
            dgate = dout * (sg * (1.0 + zg * (1.0 - sg)))
            dq_scale = eb * (sq * (1.0 + zq * (1.0 - sq)))
            for h in range(HG_HEADS):
                sl = _head(h)
                o = o_all[:, sl]
                r = lax.rsqrt(jnp.mean(o * o, axis=-1, keepdims=True) + EPS)
                ohat = o * r
                d_on = d_on_all[:, sl]
                dz_ref[rows, _head(h, 3)] = (dgate[:, sl] * (ohat * gn)).astype(BF16)
                dgn_acc[...] += jnp.sum(d_on * ohat, axis=0, keepdims=True)
                dohat = d_on * gn
                do16 = (r * (dohat - ohat * jnp.mean(dohat * ohat, axis=-1, keepdims=True))).astype(BF16)
                st = sall_ref[0, h, c]
                dst = dst_ref[h]
                st16, dst16 = st.astype(BF16), dst.astype(BF16)
                qd_h, ki_h, kd_h, v_h = qd16[:, sl], ki16[:, sl], kd16[:, sl], v16[:, sl]
                a16 = jnp.where(mask, lax.dot_general(qd_h, ki_h, _NT, preferred_element_type=F32), 0.0).astype(BF16)
                da16 = jnp.where(mask, lax.dot_general(do16, v_h, _NT, preferred_element_type=F32), 0.0).astype(BF16)
                dv = lax.dot_general(a16, do16, _TN, preferred_element_type=F32) + lax.dot_general(
                    kd_h, dst16, _NT, preferred_element_type=F32)
                dqd = jnp.dot(da16, ki_h, preferred_element_type=F32) + jnp.dot(do16, st16, preferred_element_type=F32)
                dki = lax.dot_general(da16, qd_h, _TN, preferred_element_type=F32)
                dkd = jnp.dot(v_h, dst16, preferred_element_type=F32)
                dbl_ref[:, sl] = jnp.sum(dkd * kd[:, sl], axis=0, keepdims=True) + ebl[:, sl] * jnp.sum(
                    st * dst, axis=0, keepdims=True)
                dst_ref[h] = dst * ebl[:, sl] + lax.dot_general(do16, qd_h, _TN, preferred_element_type=F32)
                dz_ref[rows, _head(h, 2)] = dv.astype(BF16)
                dz_ref[rows, sl] = (dqd * dq_scale[:, sl]).astype(BF16)
                dkk_ref[:, sl] = dki * enb[:, sl] + dkd * ebl_b[:, sl]
                db_ref[:, sl] = dqd * qd[:, sl] - dki * ki[:, sl] - dkd * kd[:, sl]
            dlogf = lax.dot_general(tril_f, db_ref[...], _TN, preferred_element_type=F32,
                                    precision=lax.Precision.HIGHEST) + dbl_ref[...]
            df = dlogf / f - dkk_ref[...]
            dz_ref[rows, HG_WIDTH:2 * HG_WIDTH] = (df * (1.0 - lb) * sig * (1.0 - sig)).astype(BF16)
            dlb_ref[...] += jnp.sum(df * (1.0 - sig), axis=0, keepdims=True)
            return carry

        lax.fori_loop(0, HG_CPB, chunk, 0, unroll=2)

        @pl.when(t_id == HG_NT - 1)
        def _():
            row0 = (lax.broadcasted_iota(jnp.int32, (3, HG_WIDTH), 0) == 0).astype(F32)
            dlbl_part = dlb_ref[...] * lb * (row0 - p)

            @pl.when(b_id == 0)
            def _():
                dlbl_ref[...] = dlbl_part

            @pl.when(b_id > 0)
            def _():
                dlbl_ref[...] += dlbl_part

            dgn_ref[...] = dgn_acc[...]

    rev = lambda b, t: b * HG_NT + HG_NT - 1 - t
    zspec = lambda s: pl.BlockSpec((HG_TB, HG_WIDTH), lambda b, t: (rev(b, t), s))
    return pl.pallas_call(
        body,
        name="hgrn_bwd",
        grid=(B_LOC, HG_NT),
        in_specs=[zspec(0), zspec(1), zspec(2), zspec(3), zspec(0), zspec(0), zspec(0),
                  pl.BlockSpec((1, HG_HEADS, HG_CPB, HG_DIM, HG_DIM), lambda b, t: (b, 0, HG_NT - 1 - t, 0, 0)),
                  pl.BlockSpec((3, HG_WIDTH), lambda b, t: (0, 0)), pl.BlockSpec((1, HG_DIM), lambda b, t: (0, 0))],
        out_specs=[pl.BlockSpec((HG_TB, 5 * HG_WIDTH), lambda b, t: (rev(b, t), 0)),
                   pl.BlockSpec((3, HG_WIDTH), lambda b, t: (0, 0)), pl.BlockSpec((1, HG_DIM), lambda b, t: (0, 0))],
        out_shape=[jax.ShapeDtypeStruct((N_TOK, 5 * HG_WIDTH), BF16),
                   jax.ShapeDtypeStruct((3, HG_WIDTH), F32), jax.ShapeDtypeStruct((1, HG_DIM), F32)],
        scratch_shapes=[pltpu.VMEM((HG_HEADS, HG_DIM, HG_DIM), F32), pltpu.VMEM((1, HG_WIDTH), F32),
                        pltpu.VMEM((1, HG_DIM), F32), pltpu.VMEM((HG_CHUNK, HG_WIDTH), F32),
                        pltpu.VMEM((HG_CHUNK, HG_WIDTH), F32), pltpu.VMEM((1, HG_WIDTH), F32)],
        compiler_params=_cp("arbitrary", "arbitrary"),
    )(z, z, z, z, opre, dcat, dq_mem, sall, lb_logits, gnorm)


GM_TM = 256


def _gmlp_norm(zv, ln_g, ln_b):
    gv, dgelu = _gelu_parts(zv)
    xc = gv - jnp.mean(gv, axis=-1, keepdims=True)
    rstd = lax.rsqrt(jnp.mean(xc * xc, axis=-1, keepdims=True) + EPS)
    vhat = xc * rstd
    return vhat * ln_g + ln_b, vhat, rstd, dgelu


def _gmlp_specs():
    half = lambda j: pl.BlockSpec((GM_TM, GM_WIDTH), lambda i: (i, j))
    vec = pl.BlockSpec((1, GM_WIDTH), lambda i: (0, 0))
    w = pl.BlockSpec((GM_GROUPS, GM_CHUNK, GM_CHUNK), lambda i: (0, 0, 0))
    bt = pl.BlockSpec((GM_CHUNK, GM_GROUPS), lambda i: (0, 0))
    return half, vec, w, bt


def _gmlp_fwd(z, o_mem, ln_g, ln_b, w_s, b_st):
    def body(zu_ref, zv_ref, omem_ref, g_ref, b_ref, w_ref, bt_ref, o_ref):
        o_ref[:, GM_WIDTH:] = omem_ref[...]
        u, _ = _gelu_parts(zu_ref[...])
        v, _, _, _ = _gmlp_norm(zv_ref[...], g_ref[...], b_ref[...])
        v16 = v.astype(BF16)
        mask = _tril(GM_CHUNK)
        bt = bt_ref[...]
        for g in range(GM_GROUPS):
            wm16 = jnp.where(mask, w_ref[g], 0.0).astype(BF16)
            cols = slice(g * GM_GDIM, (g + 1) * GM_GDIM)
            for c in range(GM_TM // GM_CHUNK):
                rows = slice(c * GM_CHUNK, (c + 1) * GM_CHUNK)
                mixed = jnp.dot(wm16, v16[rows, cols], preferred_element_type=F32) + bt[:, g:g + 1]
                o_ref[rows, cols] = (u[rows, cols] * mixed).astype(BF16)

    half, vec, w, bt = _gmlp_specs()
    return pl.pallas_call(
        body,
        name="gmlp_fwd",
        grid=(N_TOK // GM_TM,),
        in_specs=[half(0), half(1), pl.BlockSpec((GM_TM, XA_HEADS * XA_DIM), lambda i: (i, 0)), vec, vec, w, bt],
        out_specs=pl.BlockSpec((GM_TM, GM_WIDTH + XA_HEADS * XA_DIM), lambda i: (i, 0)),
        out_shape=jax.ShapeDtypeStruct((N_TOK, GM_WIDTH + XA_HEADS * XA_DIM), BF16),
        compiler_params=_cp("parallel"),
    )(z, z, o_mem, ln_g, ln_b, w_s, b_st)


def _gmlp_bwd(z, dcat, dq_mem, ln_g, ln_b, w_s, b_st):
    def body(zu_ref, zv_ref, dout_ref, dqm_ref, g_ref, b_ref, w_ref, bt_ref,
             dz_ref, dw_ref, dbt_ref, dg_ref, db_ref, dv_ref):
        dz_ref[:, 2 * GM_WIDTH:] = dqm_ref[...]
        @pl.when(pl.program_id(0) == 0)
        def _():
            dw_ref[...] = jnp.zeros_like(dw_ref)
            dbt_ref[...] = jnp.zeros_like(dbt_ref)
            dg_ref[...] = jnp.zeros_like(dg_ref)
            db_ref[...] = jnp.zeros_like(db_ref)

        zu = zu_ref[...]
        u, du_dz = _gelu_parts(zu)
        ln_g = g_ref[...]
        v, vhat, rstd, dgv_dz = _gmlp_norm(zv_ref[...], ln_g, b_ref[...])
        v16 = v.astype(BF16)
        dout = dout_ref[...]
        dmixed = dout * u
        dm16 = dmixed.astype(BF16)
        mask = _tril(GM_CHUNK)
        bt = bt_ref[...]
        group_id = lax.broadcasted_iota(jnp.int32, (1, GM_GROUPS), 1)
        dbt = jnp.zeros((GM_CHUNK, GM_GROUPS), F32)
        for g in range(GM_GROUPS):
            wm16 = jnp.where(mask, w_ref[g], 0.0).astype(BF16)
            cols = slice(g * GM_GDIM, (g + 1) * GM_GDIM)
            dw = jnp.zeros((GM_CHUNK, GM_CHUNK), F32)
            dbt_g = jnp.zeros((GM_CHUNK, 1), F32)
            for c in range(GM_TM // GM_CHUNK):
                rows = slice(c * GM_CHUNK, (c + 1) * GM_CHUNK)
                mixed = jnp.dot(wm16, v16[rows, cols], preferred_element_type=F32) + bt[:, g:g + 1]
                dz_ref[rows, cols] = (dout[rows, cols] * mixed * du_dz[rows, cols]).astype(BF16)
                dw += lax.dot_general(dm16[rows, cols], v16[rows, cols], _NT, preferred_element_type=F32)
                dbt_g += jnp.sum(dmixed[rows, cols], axis=-1, keepdims=True)
                dv_ref[rows, cols] = lax.dot_general(wm16, dm16[rows, cols], _TN, preferred_element_type=F32)
            dw_ref[g] += jnp.where(mask, dw, 0.0)
            dbt = dbt + dbt_g * (group_id == g).astype(F32)
        dbt_ref[...] += dbt
        dv = dv_ref[...]
        dg_ref[...] += jnp.sum(dv * vhat, axis=0, keepdims=True)
        db_ref[...] += jnp.sum(dv, axis=0, keepdims=True)
        dvh = dv * ln_g
        dgv = rstd * (dvh - jnp.mean(dvh, axis=-1, keepdims=True) - vhat * jnp.mean(dvh * vhat, axis=-1, keepdims=True))
        dz_ref[:, GM_WIDTH:2 * GM_WIDTH] = (dgv * dgv_dz).astype(BF16)

    half, vec, w, bt = _gmlp_specs()
    dz_width = 2 * GM_WIDTH + XA_HEADS * XA_DIM
    return pl.pallas_call(
        body,
        name="gmlp_bwd",
        grid=(N_TOK // GM_TM,),
        in_specs=[half(0), half(1), half(0), pl.BlockSpec((GM_TM, XA_HEADS * XA_DIM), lambda i: (i, 0)), vec, vec, w, bt],
        out_specs=[pl.BlockSpec((GM_TM, dz_width), lambda i: (i, 0)), w, bt, vec, vec],
        out_shape=[jax.ShapeDtypeStruct((N_TOK, dz_width), BF16),
                   jax.ShapeDtypeStruct((GM_GROUPS, GM_CHUNK, GM_CHUNK), F32),
                   jax.ShapeDtypeStruct((GM_CHUNK, GM_GROUPS), F32),
                   jax.ShapeDtypeStruct((1, GM_WIDTH), F32), jax.ShapeDtypeStruct((1, GM_WIDTH), F32)],
        scratch_shapes=[pltpu.VMEM((GM_TM, GM_WIDTH), F32)],
        compiler_params=_cp("arbitrary"),
    )(z, z, dcat, dq_mem, ln_g, ln_b, w_s, b_st)


def _own_slot(shape):
    return pl.BlockSpec((None,) + tuple(shape), lambda i, me_ref: (me_ref[0],) + (0,) * len(shape))


def _place_rows(w, layer, cuts_columns, me, *, name, deps=()):
    _, r, c = w.shape
    n = c if cuts_columns else r

    def body(me_ref, w_ref, *rest):
        o_ref = rest[len(deps)]
        wv = w_ref[...]
        o_ref[...] = (wv.T if cuts_columns else wv).astype(BF16)

    return pl.pallas_call(
        body,
        name=name,
        grid_spec=pltpu.PrefetchScalarGridSpec(
            num_scalar_prefetch=1, grid=(1,),
            in_specs=[pl.BlockSpec((None, r, c), lambda i, me_ref: (layer, 0, 0))] + [ANY_SPEC] * len(deps),
            out_specs=_own_slot((n, D_MODEL))),
        out_shape=jax.ShapeDtypeStruct((N_DEV, n, D_MODEL), BF16),
        compiler_params=_cp("arbitrary"),
    )(me, w, *deps)


def _place_ln(ln_g, ln_b, me):
    blk = ln_g.shape[1]

    def body(me_ref, g_ref, b_ref, o_ref):
        o_ref[...] = jnp.zeros_like(o_ref)
        o_ref[0:1, :] = g_ref[...]
        o_ref[1:2, :] = b_ref[...]

    vec = pl.BlockSpec((1, blk), lambda i, me_ref: (0, 0))
    return pl.pallas_call(
        body,
        name="place_ln",
        grid_spec=pltpu.PrefetchScalarGridSpec(
            num_scalar_prefetch=1, grid=(1,), in_specs=[vec, vec], out_specs=_own_slot((8, blk))),
        out_shape=jax.ShapeDtypeStruct((N_DEV, 8, blk), F32),
        compiler_params=_cp("arbitrary"),
    )(me, ln_g, ln_b)


def _place_slab(a, me, *, name):
    def body(me_ref, a_ref, o_ref):
        o_ref[...] = a_ref[...]

    return pl.pallas_call(
        body,
        name=name,
        grid_spec=pltpu.PrefetchScalarGridSpec(
            num_scalar_prefetch=1, grid=(1,),
            in_specs=[pl.BlockSpec(a.shape, lambda i, me_ref: (0, 0))], out_specs=_own_slot(a.shape)),
        out_shape=jax.ShapeDtypeStruct((N_DEV,) + a.shape, a.dtype),
        compiler_params=_cp("arbitrary"),
    )(me, a)


def _place_own(grads, me, *, name):
    k = len(grads)

    def body(me_ref, *refs):
        for src, dst in zip(refs[:k], refs[k:]):
            dst[...] = src[...]

    specs = [_own_slot(g.shape[1:]) for g in grads]
    return pl.pallas_call(
        body,
        name=name,
        grid_spec=pltpu.PrefetchScalarGridSpec(num_scalar_prefetch=1, grid=(1,), in_specs=specs, out_specs=specs),
        out_shape=[jax.ShapeDtypeStruct(g.shape, g.dtype) for g in grads],
        compiler_params=_cp("arbitrary"),
    )(me, *grads)


def _mesh_pos():
    x, y, c = (lax.axis_index(a) for a in MESH_AXES)
    return x, y, c, 4 * x + 2 * y + c


def _peer(x, y, c, r):
    px = 1 - x if r & 4 else x
    py = 1 - y if r & 2 else y
    pc = 1 - c if r & 1 else c
    return (px, py, pc), 4 * px + 2 * py + pc


RELATIONS = {"scatter": (1, 2, 3, 4, 5, 6, 7), "gather_all": (1, 2, 3, 4, 5, 6, 7), "gather_chips": (1, 2, 4, 6),
             "gather_sibling": (2, 4, 6)}


def _peer_copies(srcs, lands, send_sems, recv_sems, mode, waits):
    x, y, c, me = _mesh_pos()
    rel = RELATIONS[mode]
    pairs = []
    for ri, r in enumerate(rel):
        if mode == "gather_sibling":
            peer, _ = _peer(x, y, c, 1)
            _, sent_blk = _peer(x, y, c, r)
            _, got_blk = _peer(x, y, c, r ^ 1)
        else:
            peer, peer_blk = _peer(x, y, c, r)
            sent_blk, got_blk = (peer_blk if mode == "scatter" else me), peer_blk
        for k, (src, land) in enumerate(zip(srcs, lands)):
            idx = k * len(rel) + ri
            sems = dict(send_sem=send_sems.at[idx], recv_sem=recv_sems.at[idx], device_id=peer,
                        device_id_type=pl.DeviceIdType.MESH)
            dst_blk = sent_blk if mode == "gather_sibling" else me
            mine = pltpu.make_async_remote_copy(src_ref=src.at[sent_blk], dst_ref=land.at[dst_blk], **sems)
            theirs = pltpu.make_async_remote_copy(src_ref=src.at[sent_blk], dst_ref=land.at[got_blk], **sems) if waits else None
            pairs.append((mine, theirs))
    return pairs


DATAFLOW = pltpu.SideEffectType.DATAFLOW_SIDE_EFFECTING


def _in_hbm(a):
    return pltpu.with_memory_space_constraint(a, pltpu.HBM)


def _copies_start(srcs, lands, *, mode, name, deps=()):
    gather = mode != "scatter"
    arrs = list(lands) if gather else list(srcs) + list(lands)
    n, k, nd = len(arrs), len(lands), len(deps)

    def body(*refs):
        ins, send_sems, recv_sems, token = refs[:n], refs[n + nd], refs[n + nd + 1], refs[2 * n + nd + 2]
        src_refs, land_refs = (ins, ins) if gather else (ins[:k], ins[k:])
        for mine, _ in _peer_copies(src_refs, land_refs, send_sems, recv_sems, mode, waits=False):
            mine.start()
        token[...] = jnp.zeros_like(token)

    n_cp = k * len(RELATIONS[mode])
    return pl.pallas_call(
        body,
        name=name,
        in_specs=[HBM_SPEC] * n + [ANY_SPEC] * nd,
        out_specs=(SEM_SPEC, SEM_SPEC, *[HBM_SPEC] * n, pl.BlockSpec(memory_space=pltpu.VMEM)),
        out_shape=(pltpu.SemaphoreType.DMA((n_cp,)), pltpu.SemaphoreType.DMA((n_cp,)),
                   *[pltpu.HBM(a.shape, a.dtype) for a in arrs], jax.ShapeDtypeStruct((8, 128), F32)),
        input_output_aliases={i: 2 + i for i in range(n)},
        compiler_params=pltpu.CompilerParams(has_side_effects=DATAFLOW),
    )(*[_in_hbm(a) for a in arrs], *deps)


def _copies_wait(arrs, send_sems, recv_sems, after, *, n_lands, mode, name):
    n, k = len(arrs), n_lands
    gather = mode != "scatter"

    def body(*refs):
        ins, send_sems, recv_sems = refs[:n], refs[n], refs[n + 1]
        src_refs, land_refs = (ins, ins) if gather else (ins[:k], ins[k:])
        for mine, theirs in _peer_copies(src_refs, land_refs, send_sems, recv_sems, mode, waits=True):
            mine.wait_send()
            theirs.wait_recv()

    outs = pl.pallas_call(
        body,
        name=name,
        in_specs=[HBM_SPEC] * n + [SEM_SPEC, SEM_SPEC] + [ANY_SPEC] * len(after),
        out_specs=[HBM_SPEC] * n,
        out_shape=[pltpu.HBM(a.shape, a.dtype) for a in arrs],
        input_output_aliases={i: i for i in range(n)},
        compiler_params=pltpu.CompilerParams(has_side_effects=DATAFLOW),
    )(*arrs, send_sems, recv_sems, *after)
    return outs[n - k:]


def _adamw(w, g, m, v):
    m = ADAM_B1 * m + (1.0 - ADAM_B1) * g
    v = ADAM_B2 * v + (1.0 - ADAM_B2) * (g * g)
    m_hat = m / (1.0 - ADAM_B1 ** ADAM_STEP)
    v_hat = v / (1.0 - ADAM_B2 ** ADAM_STEP)
    return -ADAM_LR * (m_hat / (jnp.sqrt(v_hat) + ADAM_EPS) + ADAM_WD * w), m, v


ADAM_TC = 256


def _adam_big(slots, w, m, v, cuts_columns, *, name):
    layers, n, nj = len(slots), slots[0].shape[1], D_MODEL // ADAM_TC

    def body(*refs):
        s_refs = refs[:layers]
        w_ref, m_ref, v_ref, g_ref, d_ref, nm_ref, nv_ref, acc_ref = refs[layers:]
        for ll in range(layers):
            @pl.when(pl.program_id(0) == ll)
            def _(s_ref=s_refs[ll]):
                g = s_ref[0].astype(F32)
                for s in range(1, N_DEV):
                    g = g + s_ref[s].astype(F32)
                acc_ref[...] = g

        g = acc_ref[...].T if cuts_columns else acc_ref[...]
        g_ref[...] = g
        d_ref[...], nm_ref[...], nv_ref[...] = _adamw(w_ref[...], g, m_ref[...], v_ref[...])

    def slot_spec(ll):
        return pl.BlockSpec((N_DEV, n, ADAM_TC),
                            lambda l, j: (0, 0, jnp.where(l < ll, 0, jnp.where(l > ll, nj - 1, j))))

    if cuts_columns:
        w_spec = pl.BlockSpec((None, ADAM_TC, n), lambda l, j: (l, j, 0))
    else:
        w_spec = pl.BlockSpec((None, n, ADAM_TC), lambda l, j: (l, 0, j))
    return pl.pallas_call(
        body,
        name=name,
        grid=(layers, nj),
        in_specs=[slot_spec(ll) for ll in range(layers)] + [w_spec] * 3,
        out_specs=[w_spec] * 4,
        out_shape=[jax.ShapeDtypeStruct(w.shape, F32)] * 4,
        scratch_shapes=[pltpu.VMEM((n, ADAM_TC), F32)],
        compiler_params=_cp("arbitrary", "arbitrary"),
    )(*slots, w, m, v)


def _adam_slabs(slots, ws, ms, vs):
    n = len(slots)

    def body(*refs):
        ins, outs = refs[:4 * n], refs[4 * n:]
        for k in range(n):
            s_ref, w_ref, m_ref, v_ref = ins[k], ins[n + k], ins[2 * n + k], ins[3 * n + k]
            g = s_ref[0]
            for s in range(1, N_DEV):
                g = g + s_ref[s]
            outs[4 * k][...] = g
            outs[4 * k + 1][...], outs[4 * k + 2][...], outs[4 * k + 3][...] = _adamw(w_ref[...], g, m_ref[...], v_ref[...])

    res = pl.pallas_call(
        body,
        name="small_adamw",
        out_shape=[jax.ShapeDtypeStruct(w.shape, F32) for w in ws for _ in range(4)],
        compiler_params=pltpu.CompilerParams(vmem_limit_bytes=VMEM_LIMIT_BYTES),
    )(*slots, *ws, *ms, *vs)
    return [res[4 * k:4 * k + 4] for k in range(n)]


def _adam_vecs(gs, ws, ms, vs):
    n = len(gs)

    def body(*refs):
        ins, outs = refs[:4 * n], refs[4 * n:]
        for k in range(n):
            outs[3 * k][...], outs[3 * k + 1][...], outs[3 * k + 2][...] = _adamw(
                ins[n + k][...], ins[k][...], ins[2 * n + k][...], ins[3 * n + k][...])

    res = pl.pallas_call(
        body,
        name="ln_adamw",
        out_shape=[jax.ShapeDtypeStruct(w.shape, F32) for w in ws for _ in range(3)],
        compiler_params=pltpu.CompilerParams(vmem_limit_bytes=VMEM_LIMIT_BYTES),
    )(*gs, *ws, *ms, *vs)
    return [res[3 * k:3 * k + 3] for k in range(n)]


SLAB_AT = dict(mem_norm=0, lb_logits=1, ffn1_norm=4, mix_norm=6, hgrn_gnorm=8, gmlp_ln_g=9, gmlp_ln_b=11,
               gmlp_b_s=13, ffn2_norm=14, final_norm=16)
SLAB_ROWS = 24
SMALL_SHARDED = ("gmlp_ln_g", "gmlp_ln_b")


def _pack_slab(parts, *, name):
    flat, plan = [], []
    for pname, at in SLAB_AT.items():
        for a in parts.get(pname, ()):
            flat.append(a)
            plan.append((at, a.shape))
            at += max(1, a.shape[0] * a.shape[1] // D_MODEL)

    def body(*refs):
        o_ref = refs[-1]
        o_ref[...] = jnp.zeros_like(o_ref)
        for ref, (at, (r, w)) in zip(refs, plan):
            if w == D_MODEL or r == 1 and w < D_MODEL:
                o_ref[at:at + r, 0:w] = ref[...]
            elif w < D_MODEL:
                for j in range(r):
                    o_ref[at:at + 1, j * w:(j + 1) * w] = ref[j:j + 1, :]
            else:
                for j in range(w // D_MODEL):
                    o_ref[at + j:at + j + 1, :] = ref[:, j * D_MODEL:(j + 1) * D_MODEL]

    return pl.pallas_call(
        body,
        name=name,
        out_shape=jax.ShapeDtypeStruct((SLAB_ROWS, D_MODEL), F32),
        compiler_params=pltpu.CompilerParams(vmem_limit_bytes=VMEM_LIMIT_BYTES),
    )(*flat)


def _unpack_slab(slab, shapes):
    out = {}
    for pname, at in SLAB_AT.items():
        if pname in SMALL_SHARDED:
            continue
        size = math.prod(shapes[pname])
        rows = max(1, size // D_MODEL)
        out[pname] = slab[at:at + rows].reshape(-1)[:size].reshape(shapes[pname])
    return out


def _ffn_fwd(x, norm_g, block, layer, full, get_weights):
    tag = f"l{layer}_{block}"
    full.update(get_weights((layer, f"{block}_in"), (x,)))
    h, z, act = _norm_mm(x, norm_g, full[(f"{block}_w_in", layer)], swiglu=True, tm=512, tn=1408, deps=full.pop("deps", ()),
                         name=f"{tag}_in")
    full.update(get_weights((layer, f"{block}_out"), (act,)))
    y = _mm(act, full[(f"{block}_w_out", layer)], tm=512, tn=D_MODEL, tk=D_FF, out_dtype=F32, res=x, scale=0.5,
            deps=full.pop("deps", ()), name=f"{tag}_out")
    return y, (x, h, z, act)


def _ffn_bwd(dy, dy16, saved, norm_g, w_in_t, w_out, tag, deps=(), after_out_wgrad=None, before_in_wgrad=None):
    x, h, z, act = saved
    dw_out = _mm(act, dy16, ta=True, tm=1408, tn=D_MODEL, tk=N_TOK, out_dtype=BF16, scale=0.5, deps=deps,
                 name=f"{tag}_out_wgrad")
    sent = after_out_wgrad(dw_out) if after_out_wgrad is not None else ()
    dz = _swiglu_dgrad(dy16, w_out, z, scale=0.5, deps=sent, name=f"{tag}_out_dgrad")
    if before_in_wgrad is None:
        dw_in_t = _planes_wgrad(dz, h, name=f"{tag}_in_wgrad")
        dx, dx16, dg = _dgrad_norm_bwd(dz, w_in_t, x, norm_g, dy, name=f"{tag}_in_dgrad")
    else:
        dx, dx16, dg = _dgrad_norm_bwd(dz, w_in_t, x, norm_g, dy, name=f"{tag}_in_dgrad")
        dw_in_t = _planes_wgrad(dz, h, deps=before_in_wgrad(dg), name=f"{tag}_in_wgrad")
    return dx, dx16, dg, dw_in_t, dw_out


def kernel(x, mem, mem_norm, lb_logits, ffn1_norm, ffn1_w_in, ffn1_w_out, mix_norm, mem_w_kv, hgrn_w_in, hgrn_gnorm, hgrn_w_out, gmlp_w_in, gmlp_ln_g, gmlp_ln_b, gmlp_w_s, gmlp_b_s, gmlp_w_out, ffn2_norm, ffn2_w_in, ffn2_w_out, final_norm, loss_target, m_mem_norm, m_lb_logits, m_ffn1_norm, m_ffn1_w_in, m_ffn1_w_out, m_mix_norm, m_mem_w_kv, m_hgrn_w_in, m_hgrn_gnorm, m_hgrn_w_out, m_gmlp_w_in, m_gmlp_ln_g, m_gmlp_ln_b, m_gmlp_w_s, m_gmlp_b_s, m_gmlp_w_out, m_ffn2_norm, m_ffn2_w_in, m_ffn2_w_out, m_final_norm, v_mem_norm, v_lb_logits, v_ffn1_norm, v_ffn1_w_in, v_ffn1_w_out, v_mix_norm, v_mem_w_kv, v_hgrn_w_in, v_hgrn_gnorm, v_hgrn_w_out, v_gmlp_w_in, v_gmlp_ln_g, v_gmlp_ln_b, v_gmlp_w_s, v_gmlp_b_s, v_gmlp_w_out, v_ffn2_norm, v_ffn2_w_in, v_ffn2_w_out, v_final_norm):
    weights = dict(mem_norm=mem_norm, lb_logits=lb_logits, ffn1_norm=ffn1_norm, ffn1_w_in=ffn1_w_in, ffn1_w_out=ffn1_w_out, mix_norm=mix_norm, mem_w_kv=mem_w_kv, hgrn_w_in=hgrn_w_in, hgrn_gnorm=hgrn_gnorm, hgrn_w_out=hgrn_w_out, gmlp_w_in=gmlp_w_in, gmlp_ln_g=gmlp_ln_g, gmlp_ln_b=gmlp_ln_b, gmlp_w_s=gmlp_w_s, gmlp_b_s=gmlp_b_s, gmlp_w_out=gmlp_w_out, ffn2_norm=ffn2_norm, ffn2_w_in=ffn2_w_in, ffn2_w_out=ffn2_w_out, final_norm=final_norm)
    mom_m = dict(mem_norm=m_mem_norm, lb_logits=m_lb_logits, ffn1_norm=m_ffn1_norm, ffn1_w_in=m_ffn1_w_in, ffn1_w_out=m_ffn1_w_out, mix_norm=m_mix_norm, mem_w_kv=m_mem_w_kv, hgrn_w_in=m_hgrn_w_in, hgrn_gnorm=m_hgrn_gnorm, hgrn_w_out=m_hgrn_w_out, gmlp_w_in=m_gmlp_w_in, gmlp_ln_g=m_gmlp_ln_g, gmlp_ln_b=m_gmlp_ln_b, gmlp_w_s=m_gmlp_w_s, gmlp_b_s=m_gmlp_b_s, gmlp_w_out=m_gmlp_w_out, ffn2_norm=m_ffn2_norm, ffn2_w_in=m_ffn2_w_in, ffn2_w_out=m_ffn2_w_out, final_norm=m_final_norm)
    mom_v = dict(mem_norm=v_mem_norm, lb_logits=v_lb_logits, ffn1_norm=v_ffn1_norm, ffn1_w_in=v_ffn1_w_in, ffn1_w_out=v_ffn1_w_out, mix_norm=v_mix_norm, mem_w_kv=v_mem_w_kv, hgrn_w_in=v_hgrn_w_in, hgrn_gnorm=v_hgrn_gnorm, hgrn_w_out=v_hgrn_w_out, gmlp_w_in=v_gmlp_w_in, gmlp_ln_g=v_gmlp_ln_g, gmlp_ln_b=v_gmlp_ln_b, gmlp_w_s=v_gmlp_w_s, gmlp_b_s=v_gmlp_b_s, gmlp_w_out=v_gmlp_w_out, ffn2_norm=v_ffn2_norm, ffn2_w_in=v_ffn2_w_in, ffn2_w_out=v_ffn2_w_out, final_norm=v_final_norm)
    order = list(weights)
    _, _, _, me = _mesh_pos()
    me_arr = jnp.reshape(me, (1,)).astype(jnp.int32)
    cuts = {name: c for name, c, _, _ in GROUPS}
    rows_already = tuple(name for name, c, _, n in GROUPS if c and n % 128)
    as_rows = lambda a: jnp.transpose(a, (0, 2, 1))
    for name in rows_already:
        weights[name], mom_m[name], mom_v[name] = as_rows(weights[name]), as_rows(mom_m[name]), as_rows(mom_v[name])
        cuts[name] = False

    mix1 = (("mem_w_kv", 1), ("gmlp_w_in", 0), ("gmlp_w_out", 0))
    gather_plan = (
        ((0, "ffn1_in"), (("ffn1_w_in", 0),)),
        ((0, "ffn1_out"), (("ffn1_w_out", 0),)),
        ((0, "mix_in"), _stage_pieces(0, "mix")),
        ((0, "ffn2_in"), _stage_pieces(0, "ffn2")),
        ((1, "ffn1_in"), _stage_pieces(1, "ffn1")),
        ((1, "mix_in"), mix1),
        ((1, "ffn2_in"), _stage_pieces(1, "ffn2")),
    )
    stage_of = {use: k for k, (use, _) in enumerate(gather_plan)}
    in_flight = {}

    def place(k, deps=()):
        pieces = gather_plan[k][1]
        lands = [_place_rows(weights[name], l, cuts[name], me_arr, deps=deps, name=f"place_{name}_{l}")
                 for name, l in pieces]
        if pieces is mix1:
            lands.append(_place_ln(gmlp_ln_g, gmlp_ln_b, me_arr))
        return lands

    placed = {0: place(0)}

    def start_chips(k, deps):
        lands = placed[k]
        send_sems, recv_sems, *thru, token = _copies_start(lands, lands, mode="gather_chips", deps=deps,
                                                           name=f"gather{k}_chips_start")
        in_flight[k] = (thru, send_sems, recv_sems)
        return token

    def pass_to_sibling(k, after):
        thru, send_sems, recv_sems = in_flight[k]
        outs = _copies_wait(thru, send_sems, recv_sems, after, n_lands=len(thru), mode="gather_chips",
                            name=f"gather{k}_chips_wait")
        send_sems, recv_sems, *thru, token = _copies_start(outs, outs, mode="gather_sibling",
                                                           name=f"gather{k}_sibling_start")
        in_flight[k] = (thru, send_sems, recv_sems)
        return token, token

    first_sent = start_chips(0, ())
    placed.update({k: place(k, (first_sent,)) for k in range(1, len(gather_plan))})
    placed_later = tuple(a for k in range(1, len(gather_plan)) for a in placed[k])
    points = [(i, p) for i in (0, 1) for p in ("ffn1_in", "ffn1_out", "mix_in", "mix_out", "ffn2_in", "ffn2_out")]
    pass_at = {j: points[points.index(use) - 1] for j, (use, _) in enumerate(gather_plan) if j}

    def get_weights(use, after):
        tokens, w = [], {}
        k = stage_of.get(use)
        if k == 0:
            token, landed = pass_to_sibling(0, tuple(after) + placed_later)
            tokens += [token, start_chips(1, (landed,))]
        if k is not None:
            thru, send_sems, recv_sems = in_flight[k]
            outs = _copies_wait(thru, send_sems, recv_sems, after, n_lands=len(thru), mode="gather_sibling",
                                name=f"gather{k}_sibling_wait")
            after = (outs[0],)
            pieces = gather_plan[k][1]
            w = {p: o.reshape(N_DEV * o.shape[1], D_MODEL) for p, o in zip(pieces, outs)}
            if pieces is mix1:
                w["ln_g"] = outs[-1][:, 0, :].reshape(1, GM_WIDTH)
                w["ln_b"] = outs[-1][:, 1, :].reshape(1, GM_WIDTH)
        for j, at in pass_at.items():
            if at == use:
                token, landed = pass_to_sibling(j, after)
                tokens.append(token)
                if j + 1 < len(gather_plan):
                    tokens.append(start_chips(j + 1, (landed,)))
        w["deps"] = tuple(tokens)
        return w

    scatter = {}

    def put_grads(st, grads):
        if st in ("w_s", "small"):
            slab = grads.reshape(GM_GROUPS * GM_CHUNK, GM_CHUNK) if st == "w_s" else _pack_slab(grads, name="pack_small_grads")
            land = _place_slab(slab, me_arr, name=f"{st}_place")
            send_sems, recv_sems, *thru, token = _copies_start([land], [land], mode="gather_all", name=f"{st}_start")
            scatter[st] = (thru, send_sems, recv_sems)
            return (token,)
        views = [g.reshape(N_DEV, -1, D_MODEL) for g in grads.values()]
        recv = _place_own(views, me_arr, name=f"scatter_place_l{st[0]}_{st[1]}")
        send_sems, recv_sems, *thru, token = _copies_start(views, recv, mode="scatter",
                                                           name=f"scatter_start_l{st[0]}_{st[1]}")
        scatter[st] = (tuple(grads), thru, send_sems, recv_sems)
        return (token,)

    dx, loss_part, last_sent = _step_local(
        x, mem, loss_target, get_weights, put_grads, mem_norm, lb_logits, ffn1_norm, mix_norm, hgrn_gnorm,
        gmlp_w_s, gmlp_b_s, ffn2_norm, final_norm)

    slots = {}

    def wait_grads(blk, after, last=False):
        for st, entry in scatter.items():
            if isinstance(st, tuple) and st[1].startswith(blk) and (st == (0, "ffn1_in")) == last:
                pieces, thru, send_sems, recv_sems = entry
                outs = _copies_wait(thru, send_sems, recv_sems, after, n_lands=len(thru) // 2, mode="scatter",
                                    name=f"scatter_wait_l{st[0]}_{st[1]}")
                slots.update(zip(pieces, outs))

    grad, delta, new_m, new_v = {}, {}, {}, {}

    def adam_groups(names):
        for name in names:
            layers = GROUP_LAYERS[name]
            grad[name], delta[name], new_m[name], new_v[name] = _adam_big(
                [slots[(name, l)] for l in range(layers)], weights[name], mom_m[name], mom_v[name], cuts[name],
                name=f"{name}_adamw")

    wait_grads("ffn2", (dx, *last_sent))
    adam_groups(("ffn2_w_in", "ffn2_w_out"))
    wait_grads("mix", (delta["ffn2_w_out"],))
    adam_groups(("mem_w_kv", "gmlp_w_in", "gmlp_w_out", "hgrn_w_in", "hgrn_w_out"))
    wait_grads("ffn1", (delta["hgrn_w_out"],))
    adam_groups(("ffn1_w_out",))

    def small_parts(src):
        parts = {n: [src[n].reshape(-1, src[n].shape[-1])] for n in SLAB_AT if n not in SMALL_SHARDED}
        return parts

    w_s_rows = lambda a: a.reshape(GM_GROUPS * GM_CHUNK, GM_CHUNK)
    small_done = (delta["hgrn_w_out"],)
    (slab_slots,) = _copies_wait(*scatter["small"], small_done, n_lands=1, mode="gather_all", name="small_wait")
    (ws_slots,) = _copies_wait(*scatter["w_s"], small_done, n_lands=1, mode="gather_all", name="w_s_wait")
    (g_slab, d_slab, nm_slab, nv_slab), (g_ws, d_ws, nm_ws, nv_ws) = _adam_slabs(
        [slab_slots, ws_slots],
        [_pack_slab(small_parts(weights), name="pack_small_w"), w_s_rows(gmlp_w_s)],
        [_pack_slab(small_parts(mom_m), name="pack_small_m"), w_s_rows(m_gmlp_w_s)],
        [_pack_slab(small_parts(mom_v), name="pack_small_v"), w_s_rows(v_gmlp_w_s)])
    shapes = {n: weights[n].shape for n in SLAB_AT}
    for out, slab, ws in ((grad, g_slab, g_ws), (delta, d_slab, d_ws), (new_m, nm_slab, nm_ws), (new_v, nv_slab, nv_ws)):
        out.update(_unpack_slab(slab, shapes))
        out["gmlp_w_s"] = ws.reshape(gmlp_w_s.shape)
    blk = GM_WIDTH // N_DEV
    g_ln = [lax.dynamic_slice(g_slab[SLAB_AT[n]:SLAB_AT[n] + 2].reshape(1, GM_WIDTH), (0, me * blk), (1, blk))
            for n in SMALL_SHARDED]
    ln_out = _adam_vecs(g_ln, [weights[n] for n in SMALL_SHARDED], [mom_m[n] for n in SMALL_SHARDED],
                        [mom_v[n] for n in SMALL_SHARDED])
    for n, g, (d, nm, nv) in zip(SMALL_SHARDED, g_ln, ln_out):
        grad[n], delta[n], new_m[n], new_v[n] = g, d, nm, nv

    wait_grads("ffn1", tuple(delta[n] for n in delta if n in GROUP_LAYERS) + (d_slab,), last=True)
    adam_groups(("ffn1_w_in",))

    for name in rows_already:
        for out in (grad, delta, new_m, new_v):
            out[name] = as_rows(out[name])
    loss = lax.psum(loss_part[0, 0], MESH_AXES)
    grad_x = dx.reshape(B_LOC, SEQ, D_MODEL)
    return (loss, grad_x, *[grad[n] for n in order], *[delta[n] for n in order],
            *[new_m[n] for n in order], *[new_v[n] for n in order])


def _step_local(x, mem, loss_target, get_weights, put_grads, mem_norm, lb_logits, ffn1_norm, mix_norm, hgrn_gnorm,
                gmlp_w_s, gmlp_b_s, ffn2_norm, final_norm):
    w_s = gmlp_w_s[0]
    b_st = gmlp_b_s[0].T

    xs = x.reshape(N_TOK, D_MODEL)
    mem2d = mem.reshape(B_LOC * MEM_LEN, D_MODEL)
    mem_g = mem_norm.reshape(1, D_MODEL)
    saved, full = [], {}
    memn = _rms_fwd(mem2d, mem_g, name="mem_norm_fwd")
    for i in range(2):
        xs, s_ffn1 = _ffn_fwd(xs, ffn1_norm[i:i + 1], "ffn1", i, full, get_weights)
        full.update(get_weights((i, "mix_in"), (xs,)))
        mixer = "hgrn" if i == 0 else "gmlp"
        hm, zm = _norm_mm(xs, mix_norm[i:i + 1], full[(f"{mixer}_w_in", 0)], swiglu=False, tm=1024, tn=1280, deps=full.pop("deps", ()),
                          name=f"l{i}_mix_in")
        kv = _mm(memn, full[("mem_w_kv", i)], tb=True, tm=512, tn=512, tk=D_MODEL, out_dtype=F32, name=f"l{i}_mem_kv")
        o_mem = _attn_fwd(zm, kv, name=f"l{i}_attn")
        if i == 0:
            cat, o_pre, s_all = _hgrn_fwd(zm, o_mem, lb_logits, hgrn_gnorm)
            mix_saved = (o_pre, s_all)
        else:
            cat = _gmlp_fwd(zm, o_mem, full["ln_g"], full["ln_b"], w_s, b_st)
            mix_saved = ()
        x_mix = xs
        full.update(get_weights((i, "mix_out"), (cat,)))
        xs = _mm(cat, full[(f"{mixer}_w_out", 0)], tm=512, tn=D_MODEL, tk=cat.shape[1], out_dtype=F32, res=xs,
                 deps=full.pop("deps", ()), name=f"l{i}_mix_out")
        xs, s_ffn2 = _ffn_fwd(xs, ffn2_norm[i:i + 1], "ffn2", i, full, get_weights)
        saved.append((s_ffn1, (x_mix, hm, kv, zm, cat, mix_saved), s_ffn2))

    dx, dx16, d_final, loss_part = _loss_head(xs, final_norm.reshape(1, D_MODEL), loss_target.reshape(N_TOK, D_MODEL))

    small = {"final_norm": [d_final]}
    d_ffn1, d_ffn2, d_mix = [None, None], [None, None], [None, None]
    dmemn = jnp.zeros((B_LOC * MEM_LEN, D_MODEL), F32)
    deps = ()
    for i in (1, 0):
        s_ffn1, (x_mix, hm, kv, zm, cat, mix_saved), s_ffn2 = saved[i]
        dx, dx16, d_ffn2[i], dw_in_t, dw_out = _ffn_bwd(
            dx, dx16, s_ffn2, ffn2_norm[i:i + 1], full[("ffn2_w_in", i)], full[("ffn2_w_out", i)], f"l{i}_ffn2", deps)
        deps = put_grads((i, "ffn2"), {("ffn2_w_in", i): dw_in_t, ("ffn2_w_out", i): dw_out})
        mixer = "hgrn" if i == 0 else "gmlp"
        w_in_t, w_out = full[(f"{mixer}_w_in", 0)], full[(f"{mixer}_w_out", 0)]
        width = cat.shape[1]
        g_mix = {}
        g_mix[(f"{mixer}_w_out", 0)] = _mm(cat, dx16, ta=True, tm=1024, tn=D_MODEL, tk=N_TOK, out_dtype=BF16,
                                           deps=deps, name=f"l{i}_mix_out_wgrad")
        dcat = _mm(dx16, w_out, tb=True, tm=1024, tn=width // 2, tk=D_MODEL, out_dtype=F32, name=f"l{i}_mix_out_dgrad")
        dq, dk, dv = _attn_bwd(zm, kv, dcat, do_off=width - XA_HEADS * XA_DIM, name=f"l{i}_attn_bwd")
        if i == 0:
            dzm, dlbl, dgn = _hgrn_bwd(zm, mix_saved[0], dcat, dq, mix_saved[1], lb_logits, hgrn_gnorm)
            small["lb_logits"], small["hgrn_gnorm"] = [dlbl], [dgn]
            deps = ()
        else:
            dzm, dws, dbt, dlng, dlnb = _gmlp_bwd(zm, dcat, dq, full["ln_g"], full["ln_b"], w_s, b_st)
            small["gmlp_b_s"], small["gmlp_ln_g"], small["gmlp_ln_b"] = [dbt.T], [dlng], [dlnb]
            deps = put_grads("w_s", dws)
        g_mix[(f"{mixer}_w_in", 0)] = _mm(dzm, hm, ta=True, tm=1024, tn=D_MODEL, tk=N_TOK, out_dtype=BF16, deps=deps,
                                          name=f"l{i}_mix_in_wgrad")
        dkv = jnp.concatenate([dk, dv], axis=1)
        g_mix[("mem_w_kv", i)] = _mm(dkv, memn, ta=True, tm=512, tn=D_MODEL, tk=B_LOC * MEM_LEN, out_dtype=BF16,
                                     name=f"l{i}_mem_kv_wgrad")
        deps = put_grads((i, "mix"), g_mix)
        dx, dx16, d_mix[i] = _dgrad_norm_bwd(dzm, w_in_t, x_mix, mix_norm[i:i + 1], dx, deps=deps,
                                             name=f"l{i}_mix_in_dgrad")
        dmemn = _mm(dkv, full[("mem_w_kv", i)], tm=B_LOC * MEM_LEN, tn=D_MODEL, tk=512, out_dtype=F32, res=dmemn,
                    name=f"l{i}_mem_kv_dgrad")
        def send_small(dg, i=i, dmemn=dmemn):
            d_ffn1[i] = dg
            _, _, dmem_g = _rms_bwd(mem2d, mem_g, dmemn, dmemn, name="mem_norm_bwd")
            small.update(mem_norm=[dmem_g], ffn1_norm=d_ffn1, ffn2_norm=d_ffn2, mix_norm=d_mix)
            return put_grads("small", small)

        if i == 0:
            send_out = lambda dw_out: put_grads((0, "ffn1_out"), {("ffn1_w_out", 0): dw_out})
            dx, dx16, d_ffn1[i], dw_in_t, _ = _ffn_bwd(
                dx, dx16, s_ffn1, ffn1_norm[i:i + 1], full[("ffn1_w_in", i)], full[("ffn1_w_out", i)], f"l{i}_ffn1",
                after_out_wgrad=send_out, before_in_wgrad=send_small)
            deps = put_grads((0, "ffn1_in"), {("ffn1_w_in", 0): dw_in_t})
        else:
            dx, dx16, d_ffn1[i], dw_in_t, dw_out = _ffn_bwd(
                dx, dx16, s_ffn1, ffn1_norm[i:i + 1], full[("ffn1_w_in", i)], full[("ffn1_w_out", i)], f"l{i}_ffn1")
            deps = put_grads((i, "ffn1"), {("ffn1_w_in", i): dw_in_t, ("ffn1_w_out", i): dw_out})
    return dx, loss_part, deps
```

```python
import functools
import math

import jax
import jax.numpy as jnp
from jax import lax
from jax.experimental import pallas as pl
from jax.experimental.pallas import tpu as pltpu

F32 = jnp.float32
BF16 = jnp.bfloat16

D_MODEL = 1024
SEQ = 2048
B_LOC = 2
N_TOK = B_LOC * SEQ
MEM_LEN = 256
N_DEV = 8
EPS = 1e-6
D_FF = 2816
HG_HEADS = 8
HG_DIM = 128
HG_CHUNK = 64
HG_NCHUNK = SEQ // HG_CHUNK
GM_CHUNK = 128
GM_GROUPS = 8
GM_WIDTH = 2048
GM_GDIM = GM_WIDTH // GM_GROUPS
XA_HEADS = 4
XA_DIM = 256
XA_OFF = 4096

ADAM_LR = 0.001
ADAM_B1 = 0.9
ADAM_B2 = 0.999
ADAM_EPS = 1e-08
ADAM_WD = 0.01
ADAM_STEP = 10

VMEM_LIMIT_BYTES = 56 * 1024 * 1024
MESH_AXES = ("x", "y", "c")

GROUPS = (
    ("ffn1_w_in", True, 2, 704),
    ("ffn1_w_out", False, 2, 352),
    ("mem_w_kv", True, 2, 256),
    ("hgrn_w_in", True, 1, 640),
    ("hgrn_w_out", False, 1, 256),
    ("gmlp_w_in", True, 1, 640),
    ("gmlp_w_out", False, 1, 384),
    ("ffn2_w_in", True, 2, 704),
    ("ffn2_w_out", False, 2, 352),
)
GROUP_LAYERS = {name: layers for name, _, layers, _ in GROUPS}


def _stage_pieces(layer, block):
    if block == "mix":
        mixer = "hgrn" if layer == 0 else "gmlp"
        return (("mem_w_kv", layer), (f"{mixer}_w_in", 0), (f"{mixer}_w_out", 0))
    return ((f"{block}_w_in", layer), (f"{block}_w_out", layer))


ANY_SPEC = pl.BlockSpec(memory_space=pl.ANY)
HBM_SPEC = pl.BlockSpec(memory_space=pltpu.HBM)
SEM_SPEC = pl.BlockSpec(memory_space=pltpu.SEMAPHORE)


def _cp(*sem):
    return pltpu.CompilerParams(dimension_semantics=sem, vmem_limit_bytes=VMEM_LIMIT_BYTES)


def _sigmoid(x):
    return 0.5 * jnp.tanh(0.5 * x) + 0.5


def _gelu_parts(x):
    cdf = 0.5 * (1.0 + lax.erf(x * (1.0 / math.sqrt(2.0))))
    pdf = jnp.exp(-0.5 * x * x) * (1.0 / math.sqrt(2.0 * math.pi))
    return x * cdf, cdf + x * pdf


def _mm(a, b, *, ta=False, tb=False, tm, tn, tk, out_dtype, res=None, scale=1.0, deps=(), name):
    m, k = (a.shape[1], a.shape[0]) if ta else a.shape
    n, kb = b.shape if tb else (b.shape[1], b.shape[0])
    assert k == kb and m % tm == 0 and n % tn == 0 and k % tk == 0, (name, a.shape, b.shape)
    nk = k // tk
    dn = (((0 if ta else 1,), (1 if tb else 0,)), ((), ()))
    n_in = 2 + (res is not None) + len(deps)

    def body(*refs):
        a_ref, b_ref = refs[:2]
        r_ref = refs[2] if res is not None else None
        o_ref, scr = refs[n_in], refs[n_in + 1:]
        p = lax.dot_general(a_ref[...].astype(BF16), b_ref[...].astype(BF16), dn, preferred_element_type=F32)

        def finish(acc):
            if scale != 1.0:
                acc = scale * acc
            if r_ref is not None:
                acc = r_ref[...] + acc
            o_ref[...] = acc.astype(out_dtype)

        if nk == 1:
            finish(p)
        else:
            acc_ref = scr[0]
            kk = pl.program_id(2)

            @pl.when(kk == 0)
            def _():
                acc_ref[...] = p

            @pl.when(kk > 0)
            def _():
                acc_ref[...] += p

            @pl.when(kk == nk - 1)
            def _():
                finish(acc_ref[...])

    a_spec = pl.BlockSpec((tk, tm), lambda i, j, kk: (kk, i)) if ta else pl.BlockSpec((tm, tk), lambda i, j, kk: (i, kk))
    b_mode = dict(pipeline_mode=pl.Buffered(1)) if n == tn and nk == 1 else {}
    if tb:
        b_spec = pl.BlockSpec((tn, tk), lambda i, j, kk: (j, kk), **b_mode)
    else:
        b_spec = pl.BlockSpec((tk, tn), lambda i, j, kk: (kk, j), **b_mode)
    o_spec = pl.BlockSpec((tm, tn), lambda i, j, kk: (i, j))
    in_specs = [a_spec, b_spec] + ([o_spec] if res is not None else []) + [ANY_SPEC] * len(deps)
    args = (a, b) + ((res,) if res is not None else ()) + tuple(deps)
    return pl.pallas_call(
        body,
        name=name,
        grid=(m // tm, n // tn, nk),
        in_specs=in_specs,
        out_specs=o_spec,
        out_shape=jax.ShapeDtypeStruct((m, n), out_dtype),
        scratch_shapes=[pltpu.VMEM((tm, tn), F32)] if nk > 1 else [],
        compiler_params=_cp("parallel", "parallel", "arbitrary"),
    )(*args)


def _rms_fwd(x, g, *, name, deps=(), tm=512):
    rows = x.shape[0]

    def body(x_ref, g_ref, *rest):
        o_ref = rest[len(deps)]
        xv = x_ref[...]
        r = lax.rsqrt(jnp.mean(xv * xv, axis=-1, keepdims=True) + EPS)
        o_ref[...] = (xv * r * g_ref[...]).astype(BF16)

    row = pl.BlockSpec((tm, D_MODEL), lambda i: (i, 0))
    return pl.pallas_call(
        body,
        name=name,
        grid=(rows // tm,),
        in_specs=[row, pl.BlockSpec((1, D_MODEL), lambda i: (0, 0))] + [ANY_SPEC] * len(deps),
        out_specs=row,
        out_shape=jax.ShapeDtypeStruct((rows, D_MODEL), BF16),
        compiler_params=_cp("parallel"),
    )(x, g, *deps)


def _rms_bwd(x, g, dh, dres, *, name, deps=(), tm=512):
    rows = x.shape[0]

    def body(x_ref, g_ref, dh_ref, dres_ref, *rest):
        dx_ref, dx16_ref, dg_ref = rest[len(deps):]
        xv = x_ref[...]
        r = lax.rsqrt(jnp.mean(xv * xv, axis=-1, keepdims=True) + EPS)
        xhat = xv * r
        dhv = dh_ref[...]
        part = jnp.sum(dhv * xhat, axis=0, keepdims=True)

        @pl.when(pl.program_id(0) == 0)
        def _():
            dg_ref[...] = part

        @pl.when(pl.program_id(0) > 0)
        def _():
            dg_ref[...] += part

        dxh = dhv * g_ref[...]
        dx = dres_ref[...] + r * (dxh - xhat * jnp.mean(dxh * xhat, axis=-1, keepdims=True))
        dx_ref[...] = dx
        dx16_ref[...] = dx.astype(BF16)

    row = pl.BlockSpec((tm, D_MODEL), lambda i: (i, 0))
    vec = pl.BlockSpec((1, D_MODEL), lambda i: (0, 0))
    return pl.pallas_call(
        body,
        name=name,
        grid=(rows // tm,),
        in_specs=[row, vec, row, row] + [ANY_SPEC] * len(deps),
        out_specs=[row, row, vec],
        out_shape=[jax.ShapeDtypeStruct((rows, D_MODEL), F32), jax.ShapeDtypeStruct((rows, D_MODEL), BF16),
                   jax.ShapeDtypeStruct((1, D_MODEL), F32)],
        compiler_params=_cp("arbitrary"),
    )(x, g, dh, dres, *deps)


_NT = (((1,), (1,)), ((), ()))
_TN = (((0,), (0,)), ((), ()))


def _norm_mm(x, g, w_t, *, swiglu, name, tm, tn, deps=()):
    rows = w_t.shape[0]
    half = rows // 2
    nj = (half if swiglu else rows) // tn
    nd = len(deps)

    def body(x_ref, g_ref, w_ref, *rest):
        outs = rest[nd:]
        h_ref, z_ref = outs[:2]

        def norm():
            xv = x_ref[...]
            r = lax.rsqrt(jnp.mean(xv * xv, axis=-1, keepdims=True) + EPS)
            h_ref[...] = (xv * r * g_ref[...]).astype(BF16)

        if swiglu:
            norm()
            h = h_ref[...]
            for j in range(nj):
                cols = slice(j * tn, (j + 1) * tn)
                gate = lax.dot_general(h, w_ref[j * tn:(j + 1) * tn, :], _NT, preferred_element_type=F32)
                up = lax.dot_general(h, w_ref[half + j * tn:half + (j + 1) * tn, :], _NT, preferred_element_type=F32)
                s = _sigmoid(gate)
                silu = gate * s
                z_ref[0, :, cols] = (up * (s + silu * (1.0 - s))).astype(BF16)
                z_ref[1, :, cols] = silu.astype(BF16)
                outs[2][:, cols] = (silu * up).astype(BF16)
        else:
            j = pl.program_id(1)
            pl.when(j == 0)(norm)
            w = w_ref[pl.ds(pl.multiple_of(j * tn, tn), tn), :]
            z_ref[...] = lax.dot_general(h_ref[...], w, _NT, preferred_element_type=F32)

    grid = (N_TOK // tm,) if swiglu else (N_TOK // tm, nj)
    row = pl.BlockSpec((tm, D_MODEL), lambda i, *_: (i, 0))
    out_specs = [row]
    out_shape = [jax.ShapeDtypeStruct((N_TOK, D_MODEL), BF16)]
    if swiglu:
        out_specs += [pl.BlockSpec((2, tm, half), lambda i: (0, i, 0)), pl.BlockSpec((tm, half), lambda i: (i, 0))]
        out_shape += [jax.ShapeDtypeStruct((2, N_TOK, half), BF16), jax.ShapeDtypeStruct((N_TOK, half), BF16)]
    else:
        out_specs.append(pl.BlockSpec((tm, tn), lambda i, j: (i, j)))
        out_shape.append(jax.ShapeDtypeStruct((N_TOK, rows), F32))
    return pl.pallas_call(
        body,
        name=name,
        grid=grid,
        in_specs=[row, pl.BlockSpec((1, D_MODEL), lambda *_: (0, 0)),
                  pl.BlockSpec((rows, D_MODEL), lambda *_: (0, 0), pipeline_mode=pl.Buffered(1))] + [ANY_SPEC] * nd,
        out_specs=out_specs,
        out_shape=out_shape,
        compiler_params=_cp(*(("parallel",) if swiglu else ("parallel", "arbitrary"))),
    )(x, g, w_t, *deps)


def _swiglu_dgrad(dy16, w_out, z, *, scale, name, deps=(), tm=512, tn=1408):
    def body(dy_ref, w_ref, z_ref, *rest):
        dz_ref = rest[len(deps)]
        dy = dy_ref[...]
        for j in range(D_FF // tn):
            cols = slice(j * tn, (j + 1) * tn)
            da = lax.dot_general(dy, w_ref[cols, :], _NT, preferred_element_type=F32) * scale
            dz_ref[0, :, cols] = (da * z_ref[0, :, cols].astype(F32)).astype(BF16)
            dz_ref[1, :, cols] = (da * z_ref[1, :, cols].astype(F32)).astype(BF16)

    planes = pl.BlockSpec((2, tm, D_FF), lambda i: (0, i, 0))
    return pl.pallas_call(
        body,
        name=name,
        grid=(N_TOK // tm,),
        in_specs=[pl.BlockSpec((tm, D_MODEL), lambda i: (i, 0)),
                  pl.BlockSpec((D_FF, D_MODEL), lambda i: (0, 0), pipeline_mode=pl.Buffered(1)), planes]
        + [ANY_SPEC] * len(deps),
        out_specs=planes,
        out_shape=jax.ShapeDtypeStruct((2, N_TOK, D_FF), BF16),
        compiler_params=_cp("parallel"),
    )(dy16, w_out, z, *deps)


def _planes_wgrad(dz, h, *, name, deps=(), tm=1408):
    per_plane = D_FF // tm

    def body(a_ref, b_ref, *rest):
        o_ref = rest[len(deps)]
        o_ref[...] = lax.dot_general(a_ref[...], b_ref[...], _TN, preferred_element_type=F32).astype(BF16)

    return pl.pallas_call(
        body,
        name=name,
        grid=(2 * per_plane,),
        in_specs=[pl.BlockSpec((None, N_TOK, tm),
                               lambda i: (jnp.where(i < per_plane, 0, 1), 0, jnp.where(i < per_plane, i, i - per_plane))),
                  pl.BlockSpec((N_TOK, D_MODEL), lambda i: (0, 0), pipeline_mode=pl.Buffered(1))] + [ANY_SPEC] * len(deps),
        out_specs=pl.BlockSpec((tm, D_MODEL), lambda i: (i, 0)),
        out_shape=jax.ShapeDtypeStruct((2 * D_FF, D_MODEL), BF16),
        compiler_params=_cp("parallel"),
    )(dz, h, *deps)


def _dgrad_norm_bwd(dz, w_t, x, g, dres, *, name, deps=(), tm=512):
    planes = dz.ndim == 3
    rows = w_t.shape[0]
    half = rows // 2
    nd = len(deps)

    def body(a_ref, b_ref, x_ref, g_ref, dres_ref, *rest):
        dx_ref, dx16_ref, dg_ref = rest[nd:]
        if planes:
            dh = jnp.dot(a_ref[0], b_ref[:half, :], preferred_element_type=F32) + jnp.dot(
                a_ref[1], b_ref[half:, :], preferred_element_type=F32)
        else:
            dh = jnp.dot(a_ref[...], b_ref[...], preferred_element_type=F32)
        xv = x_ref[...]
        r = lax.rsqrt(jnp.mean(xv * xv, axis=-1, keepdims=True) + EPS)
        xhat = xv * r
        part = jnp.sum(dh * xhat, axis=0, keepdims=True)

        @pl.when(pl.program_id(0) == 0)
        def _():
            dg_ref[...] = part

        @pl.when(pl.program_id(0) > 0)
        def _():
            dg_ref[...] += part

        dxh = dh * g_ref[...]
        dx = dres_ref[...] + r * (dxh - xhat * jnp.mean(dxh * xhat, axis=-1, keepdims=True))
        dx_ref[...] = dx
        dx16_ref[...] = dx.astype(BF16)

    a_spec = pl.BlockSpec((2, tm, half), lambda i: (0, i, 0)) if planes else pl.BlockSpec((tm, rows), lambda i: (i, 0))
    row = pl.BlockSpec((tm, D_MODEL), lambda i: (i, 0))
    vec = pl.BlockSpec((1, D_MODEL), lambda i: (0, 0))
    return pl.pallas_call(
        body,
        name=name,
        grid=(N_TOK // tm,),
        in_specs=[a_spec, pl.BlockSpec((rows, D_MODEL), lambda i: (0, 0), pipeline_mode=pl.Buffered(1)), row, vec, row]
        + [ANY_SPEC] * nd,
        out_specs=[row, row, vec],
        out_shape=[jax.ShapeDtypeStruct((N_TOK, D_MODEL), F32), jax.ShapeDtypeStruct((N_TOK, D_MODEL), BF16),
                   jax.ShapeDtypeStruct((1, D_MODEL), F32)],
        compiler_params=_cp("arbitrary"),
    )(dz, w_t, x, g, dres, *deps)


def _loss_head(x, g, target, *, tm=512):
    def body(x_ref, g_ref, t_ref, dx_ref, dx16_ref, dg_ref, loss_ref):
        xv = x_ref[...]
        gv = g_ref[...]
        r = lax.rsqrt(jnp.mean(xv * xv, axis=-1, keepdims=True) + EPS)
        xhat = xv * r
        err = xhat * gv - t_ref[...]
        loss_part = jnp.zeros((1, 128), F32) + 0.5 * jnp.sum(jnp.mean(err * err, axis=-1, keepdims=True))
        dy = err * (1.0 / D_MODEL)
        dg_part = jnp.sum(dy * xhat, axis=0, keepdims=True)

        @pl.when(pl.program_id(0) == 0)
        def _():
            dg_ref[...] = dg_part
            loss_ref[...] = loss_part

        @pl.when(pl.program_id(0) > 0)
        def _():
            dg_ref[...] += dg_part
            loss_ref[...] += loss_part

        dxh = dy * gv
        dx = r * (dxh - xhat * jnp.mean(dxh * xhat, axis=-1, keepdims=True))
        dx_ref[...] = dx
        dx16_ref[...] = dx.astype(BF16)

    row = pl.BlockSpec((tm, D_MODEL), lambda i: (i, 0))
    vec = pl.BlockSpec((1, D_MODEL), lambda i: (0, 0))
    return pl.pallas_call(
        body,
        name="loss_head",
        grid=(N_TOK // tm,),
        in_specs=[row, vec, row],
        out_specs=[row, row, vec, pl.BlockSpec((1, 128), lambda i: (0, 0))],
        out_shape=[
            jax.ShapeDtypeStruct((N_TOK, D_MODEL), F32),
            jax.ShapeDtypeStruct((N_TOK, D_MODEL), BF16),
            jax.ShapeDtypeStruct((1, D_MODEL), F32),
            jax.ShapeDtypeStruct((1, 128), F32),
        ],
        compiler_params=_cp("arbitrary"),
    )(x, g, target)


XA_TQ = 1024
XA_SCALE = XA_DIM ** -0.5


def _attn_probs(q16, k16):
    s = lax.dot_general(q16, k16, _NT, preferred_element_type=F32) * XA_SCALE
    e = jnp.exp(s - jnp.max(s, axis=-1, keepdims=True))
    return e / jnp.sum(e, axis=-1, keepdims=True)


def _attn_fwd(z, kv, *, name):
    nt = SEQ // XA_TQ

    def body(q_ref, k_ref, v_ref, o_ref):
        p = _attn_probs(q_ref[...].astype(BF16), k_ref[...].astype(BF16))
        o_ref[...] = jnp.dot(p.astype(BF16), v_ref[...].astype(BF16), preferred_element_type=F32).astype(BF16)

    return pl.pallas_call(
        body,
        name=name,
        grid=(B_LOC, XA_HEADS, nt),
        in_specs=[
            pl.BlockSpec((XA_TQ, XA_DIM), lambda b, h, t: (b * nt + t, XA_OFF // XA_DIM + h)),
            pl.BlockSpec((MEM_LEN, XA_DIM), lambda b, h, t: (b, h)),
            pl.BlockSpec((MEM_LEN, XA_DIM), lambda b, h, t: (b, XA_HEADS + h)),
        ],
        out_specs=pl.BlockSpec((XA_TQ, XA_DIM), lambda b, h, t: (b * nt + t, h)),
        out_shape=jax.ShapeDtypeStruct((N_TOK, XA_HEADS * XA_DIM), BF16),
        compiler_params=_cp("parallel", "parallel", "arbitrary"),
    )(z, kv, kv)


def _attn_bwd(z, kv, dcat, *, do_off, name):
    nt = SEQ // XA_TQ

    def body(q_ref, k_ref, v_ref, do_ref, dq_ref, dk_ref, dv_ref):
        q16 = q_ref[...].astype(BF16)
        k16 = k_ref[...].astype(BF16)
        v16 = v_ref[...].astype(BF16)
        do16 = do_ref[...].astype(BF16)
        p = _attn_probs(q16, k16)
        dv_part = lax.dot_general(p.astype(BF16), do16, _TN, preferred_element_type=F32)
        dp = lax.dot_general(do16, v16, _NT, preferred_element_type=F32)
        ds16 = (p * (dp - jnp.sum(dp * p, axis=-1, keepdims=True)) * XA_SCALE).astype(BF16)
        dq_ref[...] = jnp.dot(ds16, k16, preferred_element_type=F32).astype(BF16)
        dk_part = lax.dot_general(ds16, q16, _TN, preferred_element_type=F32)

        @pl.when(pl.program_id(2) == 0)
        def _():
            dk_ref[...] = dk_part
            dv_ref[...] = dv_part

        @pl.when(pl.program_id(2) > 0)
        def _():
            dk_ref[...] += dk_part
            dv_ref[...] += dv_part

    qspec = pl.BlockSpec((XA_TQ, XA_DIM), lambda b, h, t: (b * nt + t, XA_OFF // XA_DIM + h))
    kspec = lambda off: pl.BlockSpec((MEM_LEN, XA_DIM), lambda b, h, t: (b, off + h))
    return pl.pallas_call(
        body,
        name=name,
        grid=(B_LOC, XA_HEADS, nt),
        in_specs=[qspec, kspec(0), kspec(XA_HEADS),
                  pl.BlockSpec((XA_TQ, XA_DIM), lambda b, h, t: (b * nt + t, do_off // XA_DIM + h))],
        out_specs=[pl.BlockSpec((XA_TQ, XA_DIM), lambda b, h, t: (b * nt + t, h)), kspec(0), kspec(0)],
        out_shape=[
            jax.ShapeDtypeStruct((N_TOK, XA_HEADS * XA_DIM), BF16),
            jax.ShapeDtypeStruct((B_LOC * MEM_LEN, XA_HEADS * XA_DIM), F32),
            jax.ShapeDtypeStruct((B_LOC * MEM_LEN, XA_HEADS * XA_DIM), F32),
        ],
        compiler_params=_cp("parallel", "parallel", "arbitrary"),
    )(z, kv, kv, dcat)


def _tril(n):
    return lax.broadcasted_iota(jnp.int32, (n, n), 0) >= lax.broadcasted_iota(jnp.int32, (n, n), 1)


def _lower_bound(lbl):
    e = jnp.exp(lbl - jnp.max(lbl, axis=0, keepdims=True))
    p = e / jnp.sum(e, axis=0, keepdims=True)
    return p[0:1, :], p


def _hgrn_gates(zq, zf, lb, tril_f):
    sig = _sigmoid(zf)
    f = lb + (1.0 - lb) * sig
    kk = 1.0 - f
    sq = _sigmoid(zq)
    q = zq * sq
    b = jnp.dot(tril_f, jnp.log(f), preferred_element_type=F32, precision=lax.Precision.HIGHEST)
    bl = b[HG_CHUNK - 1:HG_CHUNK, :]
    return q, sq, sig, f, kk, b, bl


HG_TB = 512
HG_CPB = HG_TB // HG_CHUNK
HG_NT = SEQ // HG_TB
HG_WIDTH = HG_HEADS * HG_DIM


def _head(h, section=0):
    return slice(section * HG_WIDTH + h * HG_DIM, section * HG_WIDTH + (h + 1) * HG_DIM)


def _hgrn_fwd(z, o_mem, lb_logits, gnorm):
    def body(zq_ref, zf_ref, zi_ref, zg_ref, omem_ref, lbl_ref, gn_ref, o_ref, opre_ref, sall_ref, st_ref):
        lb, _ = _lower_bound(lbl_ref[...])
        gn = gn_ref[...]
        mask = _tril(HG_CHUNK)
        tril_f = mask.astype(F32)
        o_ref[:, HG_WIDTH:] = omem_ref[...]

        @pl.when(pl.program_id(1) == 0)
        def _():
            st_ref[...] = jnp.zeros_like(st_ref)

        def chunk(c, carry):
            rows = pl.ds(pl.multiple_of(c * HG_CHUNK, HG_CHUNK), HG_CHUNK)
            q, _, _, _, kk, b, bl = _hgrn_gates(zq_ref[rows, :], zf_ref[rows, :], lb, tril_f)
            v16 = zi_ref[rows, :].astype(BF16)
            qd16 = (q * jnp.exp(b)).astype(BF16)
            ki16 = (kk * jnp.exp(-b)).astype(BF16)
            kd16 = (kk * jnp.exp(bl - b)).astype(BF16)
            ebl = jnp.exp(bl)
            zg = zg_ref[rows, :]
            gate = zg * _sigmoid(zg)
            for h in range(HG_HEADS):
                sl = _head(h)
                a = jnp.where(mask, lax.dot_general(qd16[:, sl], ki16[:, sl], _NT, preferred_element_type=F32), 0.0)
                st = st_ref[h]
                sall_ref[0, h, c] = st
                o = jnp.dot(a.astype(BF16), v16[:, sl], preferred_element_type=F32) + lax.dot_general(
                    qd16[:, sl], st.astype(BF16), _NT, preferred_element_type=F32)
                st_ref[h] = st * ebl[:, sl] + lax.dot_general(v16[:, sl], kd16[:, sl], _TN, preferred_element_type=F32)
                opre_ref[rows, sl] = o
                r = lax.rsqrt(jnp.mean(o * o, axis=-1, keepdims=True) + EPS)
                o_ref[rows, sl] = ((o * r * gn) * gate[:, sl]).astype(BF16)
            return carry

        lax.fori_loop(0, HG_CPB, chunk, 0, unroll=2)

    zspec = lambda s: pl.BlockSpec((HG_TB, HG_WIDTH), lambda b, t: (b * HG_NT + t, s))
    return pl.pallas_call(
        body,
        name="hgrn_fwd",
        grid=(B_LOC, HG_NT),
        in_specs=[zspec(0), zspec(1), zspec(2), zspec(3), zspec(0),
                  pl.BlockSpec((3, HG_WIDTH), lambda b, t: (0, 0)), pl.BlockSpec((1, HG_DIM), lambda b, t: (0, 0))],
        out_specs=[pl.BlockSpec((HG_TB, 2 * HG_WIDTH), lambda b, t: (b * HG_NT + t, 0)), zspec(0),
                   pl.BlockSpec((1, HG_HEADS, HG_CPB, HG_DIM, HG_DIM), lambda b, t: (b, 0, t, 0, 0))],
        out_shape=[
            jax.ShapeDtypeStruct((N_TOK, 2 * HG_WIDTH), BF16),
            jax.ShapeDtypeStruct((N_TOK, HG_WIDTH), F32),
            jax.ShapeDtypeStruct((B_LOC, HG_HEADS, HG_NCHUNK, HG_DIM, HG_DIM), F32),
        ],
        scratch_shapes=[pltpu.VMEM((HG_HEADS, HG_DIM, HG_DIM), F32)],
        compiler_params=_cp("parallel", "arbitrary"),
    )(z, z, z, z, o_mem, lb_logits, gnorm)


def _hgrn_bwd(z, opre, dcat, dq_mem, sall, lb_logits, gnorm):
    def body(zq_ref, zf_ref, zi_ref, zg_ref, opre_ref, dout_ref, dqm_ref, sall_ref, lbl_ref, gn_ref,
             dz_ref, dlbl_ref, dgn_ref, dst_ref, dlb_ref, dgn_acc, db_ref, dkk_ref, dbl_ref):
        b_id, t_id = pl.program_id(0), pl.program_id(1)
        lb, p = _lower_bound(lbl_ref[...])
        gn = gn_ref[...]
        mask = _tril(HG_CHUNK)
        tril_f = mask.astype(F32)
        dz_ref[:, 4 * HG_WIDTH:] = dqm_ref[...]

        @pl.when(t_id == 0)
        def _():
            dst_ref[...] = jnp.zeros_like(dst_ref)
            dlb_ref[...] = jnp.zeros_like(dlb_ref)

        @pl.when((b_id == 0) & (t_id == 0))
        def _():
            dgn_acc[...] = jnp.zeros_like(dgn_acc)

        def chunk(i, carry):
            c = HG_CPB - 1 - i
            rows = pl.ds(pl.multiple_of(c * HG_CHUNK, HG_CHUNK), HG_CHUNK)
            zq, zg = zq_ref[rows, :], zg_ref[rows, :]
            q, sq, sig, f, kk, b, bl = _hgrn_gates(zq, zf_ref[rows, :], lb, tril_f)
            v16 = zi_ref[rows, :].astype(BF16)
            eb, enb, ebl_b, ebl = jnp.exp(b), jnp.exp(-b), jnp.exp(bl - b), jnp.exp(bl)
            qd, ki, kd = q * eb, kk * enb, kk * ebl_b
            qd16, ki16, kd16 = qd.astype(BF16), ki.astype(BF16), kd.astype(BF16)
            o_all = opre_ref[rows, :]
            dout = dout_ref[rows, :]
            sg = _sigmoid(zg)
            d_on_all = dout * (zg * sg)
            dgate = dout * (sg * (1.0 + zg * (1.0 - sg)))
            dq_scale = eb * (sq * (1.0 + zq * (1.0 - sq)))
            for h in range(HG_HEADS):
                sl = _head(h)
                o = o_all[:, sl]
                r = lax.rsqrt(jnp.mean(o * o, axis=-1, keepdims=True) + EPS)
                ohat = o * r
                d_on = d_on_all[:, sl]
                dz_ref[rows, _head(h, 3)] = (dgate[:, sl] * (ohat * gn)).astype(BF16)
                dgn_acc[...] += jnp.sum(d_on * ohat, axis=0, keepdims=True)
                dohat = d_on * gn
                do16 = (r * (dohat - ohat * jnp.mean(dohat * ohat, axis=-1, keepdims=True))).astype(BF16)
                st = sall_ref[0, h, c]
                dst = dst_ref[h]
                st16, dst16 = st.astype(BF16), dst.astype(BF16)
                qd_h, ki_h, kd_h, v_h = qd16[:, sl], ki16[:, sl], kd16[:, sl], v16[:, sl]
                a16 = jnp.where(mask, lax.dot_general(qd_h, ki_h, _NT, preferred_element_type=F32), 0.0).astype(BF16)
                da16 = jnp.where(mask, lax.dot_general(do16, v_h, _NT, preferred_element_type=F32), 0.0).astype(BF16)
                dv = lax.dot_general(a16, do16, _TN, preferred_element_type=F32) + lax.dot_general(
                    kd_h, dst16, _NT, preferred_element_type=F32)
                dqd = jnp.dot(da16, ki_h, preferred_element_type=F32) + jnp.dot(do16, st16, preferred_element_type=F32)
                dki = lax.dot_general(da16, qd_h, _TN, preferred_element_type=F32)
                dkd = jnp.dot(v_h, dst16, preferred_element_type=F32)
                dbl_ref[:, sl] = jnp.sum(dkd * kd[:, sl], axis=0, keepdims=True) + ebl[:, sl] * jnp.sum(
                    st * dst, axis=0, keepdims=True)
                dst_ref[h] = dst * ebl[:, sl] + lax.dot_general(do16, qd_h, _TN, preferred_element_type=F32)
                dz_ref[rows, _head(h, 2)] = dv.astype(BF16)
                dz_ref[rows, sl] = (dqd * dq_scale[:, sl]).astype(BF16)
                dkk_ref[:, sl] = dki * enb[:, sl] + dkd * ebl_b[:, sl]
                db_ref[:, sl] = dqd * qd[:, sl] - dki * ki[:, sl] - dkd * kd[:, sl]
            dlogf = lax.dot_general(tril_f, db_ref[...], _TN, preferred_element_type=F32,
                                    precision=lax.Precision.HIGHEST) + dbl_ref[...]
            df = dlogf / f - dkk_ref[...]
            dz_ref[rows, HG_WIDTH:2 * HG_WIDTH] = (df * (1.0 - lb) * sig * (1.0 - sig)).astype(BF16)
            dlb_ref[...] += jnp.sum(df * (1.0 - sig), axis=0, keepdims=True)
            return carry

        lax.fori_loop(0, HG_CPB, chunk, 0, unroll=2)

        @pl.when(t_id == HG_NT - 1)
        def _():
            row0 = (lax.broadcasted_iota(jnp.int32, (3, HG_WIDTH), 0) == 0).astype(F32)
            dlbl_part = dlb_ref[...] * lb * (row0 - p)

            @pl.when(b_id == 0)
            def _():
                dlbl_ref[...] = dlbl_part

            @pl.when(b_id > 0)
            def _():
                dlbl_ref[...] += dlbl_part

            dgn_ref[...] = dgn_acc[...]

    rev = lambda b, t: b * HG_NT + HG_NT - 1 - t
    zspec = lambda s: pl.BlockSpec((HG_TB, HG_WIDTH), lambda b, t: (rev(b, t), s))
    return pl.pallas_call(
        body,
        name="hgrn_bwd",
        grid=(B_LOC, HG_NT),
        in_specs=[zspec(0), zspec(1), zspec(2), zspec(3), zspec(0), zspec(0), zspec(0),
                  pl.BlockSpec((1, HG_HEADS, HG_CPB, HG_DIM, HG_DIM), lambda b, t: (b, 0, HG_NT - 1 - t, 0, 0)),
                  pl.BlockSpec((3, HG_WIDTH), lambda b, t: (0, 0)), pl.BlockSpec((1, HG_DIM), lambda b, t: (0, 0))],
        out_specs=[pl.BlockSpec((HG_TB, 5 * HG_WIDTH), lambda b, t: (rev(b, t), 0)),
                   pl.BlockSpec((3, HG_WIDTH), lambda b, t: (0, 0)), pl.BlockSpec((1, HG_DIM), lambda b, t: (0, 0))],
        out_shape=[jax.ShapeDtypeStruct((N_TOK, 5 * HG_WIDTH), BF16),
                   jax.ShapeDtypeStruct((3, HG_WIDTH), F32), jax.ShapeDtypeStruct((1, HG_DIM), F32)],
        scratch_shapes=[pltpu.VMEM((HG_HEADS, HG_DIM, HG_DIM), F32), pltpu.VMEM((1, HG_WIDTH), F32),
                        pltpu.VMEM((1, HG_DIM), F32), pltpu.VMEM((HG_CHUNK, HG_WIDTH), F32),
                        pltpu.VMEM((HG_CHUNK, HG_WIDTH), F32), pltpu.VMEM((1, HG_WIDTH), F32)],
        compiler_params=_cp("arbitrary", "arbitrary"),
    )(z, z, z, z, opre, dcat, dq_mem, sall, lb_logits, gnorm)


GM_TM = 256


def _gmlp_norm(zv, ln_g, ln_b):
    gv, dgelu = _gelu_parts(zv)
    xc = gv - jnp.mean(gv, axis=-1, keepdims=True)
    rstd = lax.rsqrt(jnp.mean(xc * xc, axis=-1, keepdims=True) + EPS)
    vhat = xc * rstd
    return vhat * ln_g + ln_b, vhat, rstd, dgelu


def _gmlp_specs():
    half = lambda j: pl.BlockSpec((GM_TM, GM_WIDTH), lambda i: (i, j))
    vec = pl.BlockSpec((1, GM_WIDTH), lambda i: (0, 0))
    w = pl.BlockSpec((GM_GROUPS, GM_CHUNK, GM_CHUNK), lambda i: (0, 0, 0))
    bt = pl.BlockSpec((GM_CHUNK, GM_GROUPS), lambda i: (0, 0))
    return half, vec, w, bt


def _gmlp_fwd(z, o_mem, ln_g, ln_b, w_s, b_st):
    def body(zu_ref, zv_ref, omem_ref, g_ref, b_ref, w_ref, bt_ref, o_ref):
        o_ref[:, GM_WIDTH:] = omem_ref[...]
        u, _ = _gelu_parts(zu_ref[...])
        v, _, _, _ = _gmlp_norm(zv_ref[...], g_ref[...], b_ref[...])
        v16 = v.astype(BF16)
        mask = _tril(GM_CHUNK)
        bt = bt_ref[...]
        for g in range(GM_GROUPS):
            wm16 = jnp.where(mask, w_ref[g], 0.0).astype(BF16)
            cols = slice(g * GM_GDIM, (g + 1) * GM_GDIM)
            for c in range(GM_TM // GM_CHUNK):
                rows = slice(c * GM_CHUNK, (c + 1) * GM_CHUNK)
                mixed = jnp.dot(wm16, v16[rows, cols], preferred_element_type=F32) + bt[:, g:g + 1]
                o_ref[rows, cols] = (u[rows, cols] * mixed).astype(BF16)

    half, vec, w, bt = _gmlp_specs()
    return pl.pallas_call(
        body,
        name="gmlp_fwd",
        grid=(N_TOK // GM_TM,),
        in_specs=[half(0), half(1), pl.BlockSpec((GM_TM, XA_HEADS * XA_DIM), lambda i: (i, 0)), vec, vec, w, bt],
        out_specs=pl.BlockSpec((GM_TM, GM_WIDTH + XA_HEADS * XA_DIM), lambda i: (i, 0)),
        out_shape=jax.ShapeDtypeStruct((N_TOK, GM_WIDTH + XA_HEADS * XA_DIM), BF16),
        compiler_params=_cp("parallel"),
    )(z, z, o_mem, ln_g, ln_b, w_s, b_st)


def _gmlp_bwd(z, dcat, dq_mem, ln_g, ln_b, w_s, b_st):
    def body(zu_ref, zv_ref, dout_ref, dqm_ref, g_ref, b_ref, w_ref, bt_ref,
             dz_ref, dw_ref, dbt_ref, dg_ref, db_ref, dv_ref):
        dz_ref[:, 2 * GM_WIDTH:] = dqm_ref[...]
        @pl.when(pl.program_id(0) == 0)
        def _():
            dw_ref[...] = jnp.zeros_like(dw_ref)
            dbt_ref[...] = jnp.zeros_like(dbt_ref)
            dg_ref[...] = jnp.zeros_like(dg_ref)
            db_ref[...] = jnp.zeros_like(db_ref)

        zu = zu_ref[...]
        u, du_dz = _gelu_parts(zu)
        ln_g = g_ref[...]
        v, vhat, rstd, dgv_dz = _gmlp_norm(zv_ref[...], ln_g, b_ref[...])
        v16 = v.astype(BF16)
        dout = dout_ref[...]
        dmixed = dout * u
        dm16 = dmixed.astype(BF16)
        mask = _tril(GM_CHUNK)
        bt = bt_ref[...]
        group_id = lax.broadcasted_iota(jnp.int32, (1, GM_GROUPS), 1)
        dbt = jnp.zeros((GM_CHUNK, GM_GROUPS), F32)
        for g in range(GM_GROUPS):
            wm16 = jnp.where(mask, w_ref[g], 0.0).astype(BF16)
            cols = slice(g * GM_GDIM, (g + 1) * GM_GDIM)
            dw = jnp.zeros((GM_CHUNK, GM_CHUNK), F32)
            dbt_g = jnp.zeros((GM_CHUNK, 1), F32)
            for c in range(GM_TM // GM_CHUNK):
                rows = slice(c * GM_CHUNK, (c + 1) * GM_CHUNK)
                mixed = jnp.dot(wm16, v16[rows, cols], preferred_element_type=F32) + bt[:, g:g + 1]
                dz_ref[rows, cols] = (dout[rows, cols] * mixed * du_dz[rows, cols]).astype(BF16)
                dw += lax.dot_general(dm16[rows, cols], v16[rows, cols], _NT, preferred_element_type=F32)
                dbt_g += jnp.sum(dmixed[rows, cols], axis=-1, keepdims=True)
                dv_ref[rows, cols] = lax.dot_general(wm16, dm16[rows, cols], _TN, preferred_element_type=F32)
            dw_ref[g] += jnp.where(mask, dw, 0.0)
            dbt = dbt + dbt_g * (group_id == g).astype(F32)
        dbt_ref[...] += dbt
        dv = dv_ref[...]
        dg_ref[...] += jnp.sum(dv * vhat, axis=0, keepdims=True)
        db_ref[...] += jnp.sum(dv, axis=0, keepdims=True)
        dvh = dv * ln_g
        dgv = rstd * (dvh - jnp.mean(dvh, axis=-1, keepdims=True) - vhat * jnp.mean(dvh * vhat, axis=-1, keepdims=True))
        dz_ref[:, GM_WIDTH:2 * GM_WIDTH] = (dgv * dgv_dz).astype(BF16)

    half, vec, w, bt = _gmlp_specs()
    dz_width = 2 * GM_WIDTH + XA_HEADS * XA_DIM
    return pl.pallas_call(
        body,
        name="gmlp_bwd",
        grid=(N_TOK // GM_TM,),
        in_specs=[half(0), half(1), half(0), pl.BlockSpec((GM_TM, XA_HEADS * XA_DIM), lambda i: (i, 0)), vec, vec, w, bt],
        out_specs=[pl.BlockSpec((GM_TM, dz_width), lambda i: (i, 0)), w, bt, vec, vec],
        out_shape=[jax.ShapeDtypeStruct((N_TOK, dz_width), BF16),
                   jax.ShapeDtypeStruct((GM_GROUPS, GM_CHUNK, GM_CHUNK), F32),
                   jax.ShapeDtypeStruct((GM_CHUNK, GM_GROUPS), F32),
                   jax.ShapeDtypeStruct((1, GM_WIDTH), F32), jax.ShapeDtypeStruct((1, GM_WIDTH), F32)],
        scratch_shapes=[pltpu.VMEM((GM_TM, GM_WIDTH), F32)],
        compiler_params=_cp("arbitrary"),
    )(z, z, dcat, dq_mem, ln_g, ln_b, w_s, b_st)


def _own_slot(shape):
    return pl.BlockSpec((None,) + tuple(shape), lambda i, me_ref: (me_ref[0],) + (0,) * len(shape))


def _place_rows(w, layer, cuts_columns, me, *, name, deps=()):
    _, r, c = w.shape
    n = c if cuts_columns else r

    def body(me_ref, w_ref, *rest):
        o_ref = rest[len(deps)]
        wv = w_ref[...]
        o_ref[...] = (wv.T if cuts_columns else wv).astype(BF16)

    return pl.pallas_call(
        body,
        name=name,
        grid_spec=pltpu.PrefetchScalarGridSpec(
            num_scalar_prefetch=1, grid=(1,),
            in_specs=[pl.BlockSpec((None, r, c), lambda i, me_ref: (layer, 0, 0))] + [ANY_SPEC] * len(deps),
            out_specs=_own_slot((n, D_MODEL))),
        out_shape=jax.ShapeDtypeStruct((N_DEV, n, D_MODEL), BF16),
        compiler_params=_cp("arbitrary"),
    )(me, w, *deps)


def _place_ln(ln_g, ln_b, me):
    blk = ln_g.shape[1]

    def body(me_ref, g_ref, b_ref, o_ref):
        o_ref[...] = jnp.zeros_like(o_ref)
        o_ref[0:1, :] = g_ref[...]
        o_ref[1:2, :] = b_ref[...]

    vec = pl.BlockSpec((1, blk), lambda i, me_ref: (0, 0))
    return pl.pallas_call(
        body,
        name="place_ln",
        grid_spec=pltpu.PrefetchScalarGridSpec(
            num_scalar_prefetch=1, grid=(1,), in_specs=[vec, vec], out_specs=_own_slot((8, blk))),
        out_shape=jax.ShapeDtypeStruct((N_DEV, 8, blk), F32),
        compiler_params=_cp("arbitrary"),
    )(me, ln_g, ln_b)


def _place_slab(a, me, *, name):
    def body(me_ref, a_ref, o_ref):
        o_ref[...] = a_ref[...]

    return pl.pallas_call(
        body,
        name=name,
        grid_spec=pltpu.PrefetchScalarGridSpec(
            num_scalar_prefetch=1, grid=(1,),
            in_specs=[pl.BlockSpec(a.shape, lambda i, me_ref: (0, 0))], out_specs=_own_slot(a.shape)),
        out_shape=jax.ShapeDtypeStruct((N_DEV,) + a.shape, a.dtype),
        compiler_params=_cp("arbitrary"),
    )(me, a)


def _place_own(grads, me, *, name):
    k = len(grads)

    def body(me_ref, *refs):
        for src, dst in zip(refs[:k], refs[k:]):
            dst[...] = src[...]

    specs = [_own_slot(g.shape[1:]) for g in grads]
    return pl.pallas_call(
        body,
        name=name,
        grid_spec=pltpu.PrefetchScalarGridSpec(num_scalar_prefetch=1, grid=(1,), in_specs=specs, out_specs=specs),
        out_shape=[jax.ShapeDtypeStruct(g.shape, g.dtype) for g in grads],
        compiler_params=_cp("arbitrary"),
    )(me, *grads)


def _mesh_pos():
    x, y, c = (lax.axis_index(a) for a in MESH_AXES)
    return x, y, c, 4 * x + 2 * y + c


def _peer(x, y, c, r):
    px = 1 - x if r & 4 else x
    py = 1 - y if r & 2 else y
    pc = 1 - c if r & 1 else c
    return (px, py, pc), 4 * px + 2 * py + pc


RELATIONS = {"scatter": (1, 2, 3, 4, 5, 6, 7), "gather_all": (1, 2, 3, 4, 5, 6, 7), "gather_chips": (1, 2, 4, 6),
             "gather_sibling": (2, 4, 6)}


def _peer_copies(srcs, lands, send_sems, recv_sems, mode, waits):
    x, y, c, me = _mesh_pos()
    rel = RELATIONS[mode]
    pairs = []
    for ri, r in enumerate(rel):
        if mode == "gather_sibling":
            peer, _ = _peer(x, y, c, 1)
            _, sent_blk = _peer(x, y, c, r)
            _, got_blk = _peer(x, y, c, r ^ 1)
        else:
            peer, peer_blk = _peer(x, y, c, r)
            sent_blk, got_blk = (peer_blk if mode == "scatter" else me), peer_blk
        for k, (src, land) in enumerate(zip(srcs, lands)):
            idx = k * len(rel) + ri
            sems = dict(send_sem=send_sems.at[idx], recv_sem=recv_sems.at[idx], device_id=peer,
                        device_id_type=pl.DeviceIdType.MESH)
            dst_blk = sent_blk if mode == "gather_sibling" else me
            mine = pltpu.make_async_remote_copy(src_ref=src.at[sent_blk], dst_ref=land.at[dst_blk], **sems)
            theirs = pltpu.make_async_remote_copy(src_ref=src.at[sent_blk], dst_ref=land.at[got_blk], **sems) if waits else None
            pairs.append((mine, theirs))
    return pairs


DATAFLOW = pltpu.SideEffectType.DATAFLOW_SIDE_EFFECTING


def _in_hbm(a):
    return pltpu.with_memory_space_constraint(a, pltpu.HBM)


def _copies_start(srcs, lands, *, mode, name, deps=()):
    gather = mode != "scatter"
    arrs = list(lands) if gather else list(srcs) + list(lands)
    n, k, nd = len(arrs), len(lands), len(deps)

    def body(*refs):
        ins, send_sems, recv_sems, token = refs[:n], refs[n + nd], refs[n + nd + 1], refs[2 * n + nd + 2]
        src_refs, land_refs = (ins, ins) if gather else (ins[:k], ins[k:])
        for mine, _ in _peer_copies(src_refs, land_refs, send_sems, recv_sems, mode, waits=False):
            mine.start()
        token[...] = jnp.zeros_like(token)

    n_cp = k * len(RELATIONS[mode])
    return pl.pallas_call(
        body,
        name=name,
        in_specs=[HBM_SPEC] * n + [ANY_SPEC] * nd,
        out_specs=(SEM_SPEC, SEM_SPEC, *[HBM_SPEC] * n, pl.BlockSpec(memory_space=pltpu.VMEM)),
        out_shape=(pltpu.SemaphoreType.DMA((n_cp,)), pltpu.SemaphoreType.DMA((n_cp,)),
                   *[pltpu.HBM(a.shape, a.dtype) for a in arrs], jax.ShapeDtypeStruct((8, 128), F32)),
        input_output_aliases={i: 2 + i for i in range(n)},
        compiler_params=pltpu.CompilerParams(has_side_effects=DATAFLOW),
    )(*[_in_hbm(a) for a in arrs], *deps)


def _copies_wait(arrs, send_sems, recv_sems, after, *, n_lands, mode, name):
    n, k = len(arrs), n_lands
    gather = mode != "scatter"

    def body(*refs):
        ins, send_sems, recv_sems = refs[:n], refs[n], refs[n + 1]
        src_refs, land_refs = (ins, ins) if gather else (ins[:k], ins[k:])
        for mine, theirs in _peer_copies(src_refs, land_refs, send_sems, recv_sems, mode, waits=True):
            mine.wait_send()
            theirs.wait_recv()

    outs = pl.pallas_call(
        body,
        name=name,
        in_specs=[HBM_SPEC] * n + [SEM_SPEC, SEM_SPEC] + [ANY_SPEC] * len(after),
        out_specs=[HBM_SPEC] * n,
        out_shape=[pltpu.HBM(a.shape, a.dtype) for a in arrs],
        input_output_aliases={i: i for i in range(n)},
        compiler_params=pltpu.CompilerParams(has_side_effects=DATAFLOW),
    )(*arrs, send_sems, recv_sems, *after)
    return outs[n - k:]


def _adamw(w, g, m, v):
    m = ADAM_B1 * m + (1.0 - ADAM_B1) * g
    v = ADAM_B2 * v + (1.0 - ADAM_B2) * (g * g)
    m_hat = m / (1.0 - ADAM_B1 ** ADAM_STEP)
    v_hat = v / (1.0 - ADAM_B2 ** ADAM_STEP)
    return -ADAM_LR * (m_hat / (jnp.sqrt(v_hat) + ADAM_EPS) + ADAM_WD * w), m, v


ADAM_TC = 256


def _adam_big(slots, w, m, v, cuts_columns, *, name):
    layers, n, nj = len(slots), slots[0].shape[1], D_MODEL // ADAM_TC

    def body(*refs):
        s_refs = refs[:layers]
        w_ref, m_ref, v_ref, g_ref, d_ref, nm_ref, nv_ref, acc_ref = refs[layers:]
        for ll in range(layers):
            @pl.when(pl.program_id(0) == ll)
            def _(s_ref=s_refs[ll]):
                g = s_ref[0].astype(F32)
                for s in range(1, N_DEV):
                    g = g + s_ref[s].astype(F32)
                acc_ref[...] = g

        g = acc_ref[...].T if cuts_columns else acc_ref[...]
        g_ref[...] = g
        d_ref[...], nm_ref[...], nv_ref[...] = _adamw(w_ref[...], g, m_ref[...], v_ref[...])

    def slot_spec(ll):
        return pl.BlockSpec((N_DEV, n, ADAM_TC),
                            lambda l, j: (0, 0, jnp.where(l < ll, 0, jnp.where(l > ll, nj - 1, j))))

    if cuts_columns:
        w_spec = pl.BlockSpec((None, ADAM_TC, n), lambda l, j: (l, j, 0))
    else:
        w_spec = pl.BlockSpec((None, n, ADAM_TC), lambda l, j: (l, 0, j))
    return pl.pallas_call(
        body,
        name=name,
        grid=(layers, nj),
        in_specs=[slot_spec(ll) for ll in range(layers)] + [w_spec] * 3,
        out_specs=[w_spec] * 4,
        out_shape=[jax.ShapeDtypeStruct(w.shape, F32)] * 4,
        scratch_shapes=[pltpu.VMEM((n, ADAM_TC), F32)],
        compiler_params=_cp("arbitrary", "arbitrary"),
    )(*slots, w, m, v)


def _adam_slabs(slots, ws, ms, vs):
    n = len(slots)

    def body(*refs):
        ins, outs = refs[:4 * n], refs[4 * n:]
        for k in range(n):
            s_ref, w_ref, m_ref, v_ref = ins[k], ins[n + k], ins[2 * n + k], ins[3 * n + k]
            g = s_ref[0]
            for s in range(1, N_DEV):
                g = g + s_ref[s]
            outs[4 * k][...] = g
            outs[4 * k + 1][...], outs[4 * k + 2][...], outs[4 * k + 3][...] = _adamw(w_ref[...], g, m_ref[...], v_ref[...])

    res = pl.pallas_call(
        body,
        name="small_adamw",
        out_shape=[jax.ShapeDtypeStruct(w.shape, F32) for w in ws for _ in range(4)],
        compiler_params=pltpu.CompilerParams(vmem_limit_bytes=VMEM_LIMIT_BYTES),
    )(*slots, *ws, *ms, *vs)
    return [res[4 * k:4 * k + 4] for k in range(n)]


def _adam_vecs(gs, ws, ms, vs):
    n = len(gs)

    def body(*refs):
        ins, outs = refs[:4 * n], refs[4 * n:]
        for k in range(n):
            outs[3 * k][...], outs[3 * k + 1][...], outs[3 * k + 2][...] = _adamw(
                ins[n + k][...], ins[k][...], ins[2 * n + k][...], ins[3 * n + k][...])

    res = pl.pallas_call(
        body,
        name="ln_adamw",
        out_shape=[jax.ShapeDtypeStruct(w.shape, F32) for w in ws for _ in range(3)],
        compiler_params=pltpu.CompilerParams(vmem_limit_bytes=VMEM_LIMIT_BYTES),
    )(*gs, *ws, *ms, *vs)
    return [res[3 * k:3 * k + 3] for k in range(n)]


SLAB_AT = dict(mem_norm=0, lb_logits=1, ffn1_norm=4, mix_norm=6, hgrn_gnorm=8, gmlp_ln_g=9, gmlp_ln_b=11,
               gmlp_b_s=13, ffn2_norm=14, final_norm=16)
SLAB_ROWS = 24
SMALL_SHARDED = ("gmlp_ln_g", "gmlp_ln_b")


def _pack_slab(parts, *, name, deps=()):
    flat, plan = [], []
    for pname, at in SLAB_AT.items():
        for a in parts.get(pname, ()):
            flat.append(a)
            plan.append((at, a.shape))
            at += max(1, a.shape[0] * a.shape[1] // D_MODEL)

    def body(*refs):
        o_ref = refs[-1]
        o_ref[...] = jnp.zeros_like(o_ref)
        for ref, (at, (r, w)) in zip(refs, plan):
            if w == D_MODEL or r == 1 and w < D_MODEL:
                o_ref[at:at + r, 0:w] = ref[...]
            elif w < D_MODEL:
                for j in range(r):
                    o_ref[at:at + 1, j * w:(j + 1) * w] = ref[j:j + 1, :]
            else:
                for j in range(w // D_MODEL):
                    o_ref[at + j:at + j + 1, :] = ref[:, j * D_MODEL:(j + 1) * D_MODEL]

    return pl.pallas_call(
        body,
        name=name,
        in_specs=[pl.BlockSpec(memory_space=pltpu.VMEM)] * len(flat) + [ANY_SPEC] * len(deps),
        out_shape=jax.ShapeDtypeStruct((SLAB_ROWS, D_MODEL), F32),
        compiler_params=pltpu.CompilerParams(vmem_limit_bytes=VMEM_LIMIT_BYTES),
    )(*flat, *deps)


def _unpack_slab(slab, shapes):
    out = {}
    for pname, at in SLAB_AT.items():
        if pname in SMALL_SHARDED:
            continue
        size = math.prod(shapes[pname])
        rows = max(1, size // D_MODEL)
        out[pname] = slab[at:at + rows].reshape(-1)[:size].reshape(shapes[pname])
    return out


def _ffn_fwd(x, norm_g, block, layer, full, get_weights):
    tag = f"l{layer}_{block}"
    full.update(get_weights((layer, f"{block}_in"), (x,)))
    h, z, act = _norm_mm(x, norm_g, full[(f"{block}_w_in", layer)], swiglu=True, tm=512, tn=1408, deps=full.pop("deps", ()),
                         name=f"{tag}_in")
    full.update(get_weights((layer, f"{block}_out"), (act,)))
    y = _mm(act, full[(f"{block}_w_out", layer)], tm=512, tn=D_MODEL, tk=D_FF, out_dtype=F32, res=x, scale=0.5,
            deps=full.pop("deps", ()), name=f"{tag}_out")
    return y, (x, h, z, act)


def _ffn_bwd(dy, dy16, saved, norm_g, w_in_t, w_out, tag, deps=(), after_out_wgrad=None, before_in_wgrad=None):
    x, h, z, act = saved
    dw_out = _mm(act, dy16, ta=True, tm=1408, tn=D_MODEL, tk=N_TOK, out_dtype=BF16, scale=0.5, deps=deps,
                 name=f"{tag}_out_wgrad")
    sent = after_out_wgrad(dw_out) if after_out_wgrad is not None else ()
    dz = _swiglu_dgrad(dy16, w_out, z, scale=0.5, deps=sent, name=f"{tag}_out_dgrad")
    if before_in_wgrad is None:
        dw_in_t = _planes_wgrad(dz, h, name=f"{tag}_in_wgrad")
        dx, dx16, dg = _dgrad_norm_bwd(dz, w_in_t, x, norm_g, dy, name=f"{tag}_in_dgrad")
    else:
        dx, dx16, dg = _dgrad_norm_bwd(dz, w_in_t, x, norm_g, dy, name=f"{tag}_in_dgrad")
        dw_in_t = _planes_wgrad(dz, h, deps=before_in_wgrad(dg), name=f"{tag}_in_wgrad")
    return dx, dx16, dg, dw_in_t, dw_out


def kernel(x, mem, mem_norm, lb_logits, ffn1_norm, ffn1_w_in, ffn1_w_out, mix_norm, mem_w_kv, hgrn_w_in, hgrn_gnorm, hgrn_w_out, gmlp_w_in, gmlp_ln_g, gmlp_ln_b, gmlp_w_s, gmlp_b_s, gmlp_w_out, ffn2_norm, ffn2_w_in, ffn2_w_out, final_norm, loss_target, m_mem_norm, m_lb_logits, m_ffn1_norm, m_ffn1_w_in, m_ffn1_w_out, m_mix_norm, m_mem_w_kv, m_hgrn_w_in, m_hgrn_gnorm, m_hgrn_w_out, m_gmlp_w_in, m_gmlp_ln_g, m_gmlp_ln_b, m_gmlp_w_s, m_gmlp_b_s, m_gmlp_w_out, m_ffn2_norm, m_ffn2_w_in, m_ffn2_w_out, m_final_norm, v_mem_norm, v_lb_logits, v_ffn1_norm, v_ffn1_w_in, v_ffn1_w_out, v_mix_norm, v_mem_w_kv, v_hgrn_w_in, v_hgrn_gnorm, v_hgrn_w_out, v_gmlp_w_in, v_gmlp_ln_g, v_gmlp_ln_b, v_gmlp_w_s, v_gmlp_b_s, v_gmlp_w_out, v_ffn2_norm, v_ffn2_w_in, v_ffn2_w_out, v_final_norm):
    weights = dict(mem_norm=mem_norm, lb_logits=lb_logits, ffn1_norm=ffn1_norm, ffn1_w_in=ffn1_w_in, ffn1_w_out=ffn1_w_out, mix_norm=mix_norm, mem_w_kv=mem_w_kv, hgrn_w_in=hgrn_w_in, hgrn_gnorm=hgrn_gnorm, hgrn_w_out=hgrn_w_out, gmlp_w_in=gmlp_w_in, gmlp_ln_g=gmlp_ln_g, gmlp_ln_b=gmlp_ln_b, gmlp_w_s=gmlp_w_s, gmlp_b_s=gmlp_b_s, gmlp_w_out=gmlp_w_out, ffn2_norm=ffn2_norm, ffn2_w_in=ffn2_w_in, ffn2_w_out=ffn2_w_out, final_norm=final_norm)
    mom_m = dict(mem_norm=m_mem_norm, lb_logits=m_lb_logits, ffn1_norm=m_ffn1_norm, ffn1_w_in=m_ffn1_w_in, ffn1_w_out=m_ffn1_w_out, mix_norm=m_mix_norm, mem_w_kv=m_mem_w_kv, hgrn_w_in=m_hgrn_w_in, hgrn_gnorm=m_hgrn_gnorm, hgrn_w_out=m_hgrn_w_out, gmlp_w_in=m_gmlp_w_in, gmlp_ln_g=m_gmlp_ln_g, gmlp_ln_b=m_gmlp_ln_b, gmlp_w_s=m_gmlp_w_s, gmlp_b_s=m_gmlp_b_s, gmlp_w_out=m_gmlp_w_out, ffn2_norm=m_ffn2_norm, ffn2_w_in=m_ffn2_w_in, ffn2_w_out=m_ffn2_w_out, final_norm=m_final_norm)
    mom_v = dict(mem_norm=v_mem_norm, lb_logits=v_lb_logits, ffn1_norm=v_ffn1_norm, ffn1_w_in=v_ffn1_w_in, ffn1_w_out=v_ffn1_w_out, mix_norm=v_mix_norm, mem_w_kv=v_mem_w_kv, hgrn_w_in=v_hgrn_w_in, hgrn_gnorm=v_hgrn_gnorm, hgrn_w_out=v_hgrn_w_out, gmlp_w_in=v_gmlp_w_in, gmlp_ln_g=v_gmlp_ln_g, gmlp_ln_b=v_gmlp_ln_b, gmlp_w_s=v_gmlp_w_s, gmlp_b_s=v_gmlp_b_s, gmlp_w_out=v_gmlp_w_out, ffn2_norm=v_ffn2_norm, ffn2_w_in=v_ffn2_w_in, ffn2_w_out=v_ffn2_w_out, final_norm=v_final_norm)
    order = list(weights)
    _, _, _, me = _mesh_pos()
    me_arr = jnp.reshape(me, (1,)).astype(jnp.int32)
    cuts = {name: c for name, c, _, _ in GROUPS}
    rows_already = tuple(name for name, c, _, n in GROUPS if c and n % 128)
    as_rows = lambda a: jnp.transpose(a, (0, 2, 1))
    for name in rows_already:
        weights[name], mom_m[name], mom_v[name] = as_rows(weights[name]), as_rows(mom_m[name]), as_rows(mom_v[name])
        cuts[name] = False

    mix1 = (("mem_w_kv", 1), ("gmlp_w_in", 0), ("gmlp_w_out", 0))
    gather_plan = (
        ((0, "ffn1_in"), (("ffn1_w_in", 0),)),
        ((0, "ffn1_out"), (("ffn1_w_out", 0),)),
        ((0, "mix_in"), _stage_pieces(0, "mix")),
        ((0, "ffn2_in"), _stage_pieces(0, "ffn2")),
        ((1, "ffn1_in"), _stage_pieces(1, "ffn1")),
        ((1, "mix_in"), mix1),
        ((1, "ffn2_in"), _stage_pieces(1, "ffn2")),
    )
    stage_of = {use: k for k, (use, _) in enumerate(gather_plan)}
    in_flight = {}

    def place(k, deps=()):
        pieces = gather_plan[k][1]
        lands = [_place_rows(weights[name], l, cuts[name], me_arr, deps=deps, name=f"place_{name}_{l}")
                 for name, l in pieces]
        if pieces is mix1:
            lands.append(_place_ln(gmlp_ln_g, gmlp_ln_b, me_arr))
        return lands

    placed = {0: place(0)}

    def start_chips(k, deps):
        lands = placed[k]
        send_sems, recv_sems, *thru, token = _copies_start(lands, lands, mode="gather_chips", deps=deps,
                                                           name=f"gather{k}_chips_start")
        in_flight[k] = (thru, send_sems, recv_sems)
        return token

    def pass_to_sibling(k, after):
        thru, send_sems, recv_sems = in_flight[k]
        outs = _copies_wait(thru, send_sems, recv_sems, after, n_lands=len(thru), mode="gather_chips",
                            name=f"gather{k}_chips_wait")
        send_sems, recv_sems, *thru, token = _copies_start(outs, outs, mode="gather_sibling",
                                                           name=f"gather{k}_sibling_start")
        in_flight[k] = (thru, send_sems, recv_sems)
        return token, token

    first_sent = start_chips(0, ())
    placed.update({k: place(k, (first_sent,)) for k in range(1, len(gather_plan))})
    placed_later = tuple(a for k in range(1, len(gather_plan)) for a in placed[k])
    points = [(i, p) for i in (0, 1) for p in ("ffn1_in", "ffn1_out", "mix_in", "mix_out", "ffn2_in", "ffn2_out")]
    pass_at = {j: points[points.index(use) - 1] for j, (use, _) in enumerate(gather_plan) if j}
    pass_at[1], pass_at[2] = gather_plan[1][0], gather_plan[2][0]

    started = {0}

    def get_weights(use, after):
        tokens, w = [], {}
        k = stage_of.get(use)

        def pass_on(j, after):
            token, landed = pass_to_sibling(j, after)
            tokens.append(token)
            for nxt in (j + 1, j + 2) if j == 0 else (j + 1,):
                if nxt < len(gather_plan) and nxt not in started:
                    started.add(nxt)
                    tokens.append(start_chips(nxt, (landed,)))

        if k == 0:
            pass_on(0, tuple(after) + placed_later)
        elif k is not None and pass_at[k] == use:
            pass_on(k, after)
        if k is not None:
            thru, send_sems, recv_sems = in_flight[k]
            outs = _copies_wait(thru, send_sems, recv_sems, after, n_lands=len(thru), mode="gather_sibling",
                                name=f"gather{k}_sibling_wait")
            after = (outs[0],)
            pieces = gather_plan[k][1]
            w = {p: o.reshape(N_DEV * o.shape[1], D_MODEL) for p, o in zip(pieces, outs)}
            if pieces is mix1:
                w["ln_g"] = outs[-1][:, 0, :].reshape(1, GM_WIDTH)
                w["ln_b"] = outs[-1][:, 1, :].reshape(1, GM_WIDTH)
        for j, at in pass_at.items():
            if at == use and j != k:
                pass_on(j, after)
        w["deps"] = tuple(tokens)
        return w

    scatter = {}

    def put_grads(st, grads):
        if st in ("w_s", "small"):
            slab = grads.reshape(GM_GROUPS * GM_CHUNK, GM_CHUNK) if st == "w_s" else _pack_slab(grads, name="pack_small_grads")
            land = _place_slab(slab, me_arr, name=f"{st}_place")
            send_sems, recv_sems, *thru, token = _copies_start([land], [land], mode="gather_all", name=f"{st}_start")
            scatter[st] = (thru, send_sems, recv_sems)
            return (token,)
        views = [g.reshape(N_DEV, -1, D_MODEL) for g in grads.values()]
        recv = _place_own(views, me_arr, name=f"scatter_place_l{st[0]}_{st[1]}")
        send_sems, recv_sems, *thru, token = _copies_start(views, recv, mode="scatter",
                                                           name=f"scatter_start_l{st[0]}_{st[1]}")
        scatter[st] = (tuple(grads), thru, send_sems, recv_sems)
        return (token,)

    dx, loss_part, last_sent = _step_local(
        x, mem, loss_target, get_weights, put_grads, mem_norm, lb_logits, ffn1_norm, mix_norm, hgrn_gnorm,
        gmlp_w_s, gmlp_b_s, ffn2_norm, final_norm)

    slots = {}

    def wait_grads(blk, after, last=False):
        for st, entry in scatter.items():
            if isinstance(st, tuple) and st[1].startswith(blk) and (st == (0, "ffn1_in")) == last:
                pieces, thru, send_sems, recv_sems = entry
                outs = _copies_wait(thru, send_sems, recv_sems, after, n_lands=len(thru) // 2, mode="scatter",
                                    name=f"scatter_wait_l{st[0]}_{st[1]}")
                slots.update(zip(pieces, outs))

    grad, delta, new_m, new_v = {}, {}, {}, {}

    def adam_groups(names):
        for name in names:
            layers = GROUP_LAYERS[name]
            grad[name], delta[name], new_m[name], new_v[name] = _adam_big(
                [slots[(name, l)] for l in range(layers)], weights[name], mom_m[name], mom_v[name], cuts[name],
                name=f"{name}_adamw")

    wait_grads("ffn2", (dx, *last_sent))
    adam_groups(("ffn2_w_in", "ffn2_w_out"))
    wait_grads("mix", (delta["ffn2_w_out"],))
    adam_groups(("mem_w_kv", "gmlp_w_in", "gmlp_w_out", "hgrn_w_in", "hgrn_w_out"))
    wait_grads("ffn1", (delta["hgrn_w_out"],))
    adam_groups(("ffn1_w_out",))

    def small_parts(src):
        parts = {n: [src[n].reshape(-1, src[n].shape[-1])] for n in SLAB_AT if n not in SMALL_SHARDED}
        return parts

    w_s_rows = lambda a: a.reshape(GM_GROUPS * GM_CHUNK, GM_CHUNK)
    small_done = (delta["hgrn_w_out"],)
    (slab_slots,) = _copies_wait(*scatter["small"], small_done, n_lands=1, mode="gather_all", name="small_wait")
    (ws_slots,) = _copies_wait(*scatter["w_s"], small_done, n_lands=1, mode="gather_all", name="w_s_wait")
    (g_slab, d_slab, nm_slab, nv_slab), (g_ws, d_ws, nm_ws, nv_ws) = _adam_slabs(
        [slab_slots, ws_slots],
        [_pack_slab(small_parts(weights), deps=(dx,), name="pack_small_w"), w_s_rows(gmlp_w_s)],
        [_pack_slab(small_parts(mom_m), deps=(dx,), name="pack_small_m"), w_s_rows(m_gmlp_w_s)],
        [_pack_slab(small_parts(mom_v), deps=(dx,), name="pack_small_v"), w_s_rows(v_gmlp_w_s)])
    shapes = {n: weights[n].shape for n in SLAB_AT}
    for out, slab, ws in ((grad, g_slab, g_ws), (delta, d_slab, d_ws), (new_m, nm_slab, nm_ws), (new_v, nv_slab, nv_ws)):
        out.update(_unpack_slab(slab, shapes))
        out["gmlp_w_s"] = ws.reshape(gmlp_w_s.shape)
    blk = GM_WIDTH // N_DEV
    g_ln = [lax.dynamic_slice(g_slab[SLAB_AT[n]:SLAB_AT[n] + 2].reshape(1, GM_WIDTH), (0, me * blk), (1, blk))
            for n in SMALL_SHARDED]
    ln_out = _adam_vecs(g_ln, [weights[n] for n in SMALL_SHARDED], [mom_m[n] for n in SMALL_SHARDED],
                        [mom_v[n] for n in SMALL_SHARDED])
    for n, g, (d, nm, nv) in zip(SMALL_SHARDED, g_ln, ln_out):
        grad[n], delta[n], new_m[n], new_v[n] = g, d, nm, nv

    wait_grads("ffn1", tuple(delta[n] for n in delta if n in GROUP_LAYERS) + (d_slab,), last=True)
    adam_groups(("ffn1_w_in",))

    for name in rows_already:
        for out in (grad, delta, new_m, new_v):
            out[name] = as_rows(out[name])
    loss = lax.psum(loss_part[0, 0], MESH_AXES)
    grad_x = dx.reshape(B_LOC, SEQ, D_MODEL)
    return (loss, grad_x, *[grad[n] for n in order], *[delta[n] for n in order],
            *[new_m[n] for n in order], *[new_v[n] for n in order])


def _step_local(x, mem, loss_target, get_weights, put_grads, mem_norm, lb_logits, ffn1_norm, mix_norm, hgrn_gnorm,
                gmlp_w_s, gmlp_b_s, ffn2_norm, final_norm):
    w_s = gmlp_w_s[0]
    b_st = gmlp_b_s[0].T

    xs = x.reshape(N_TOK, D_MODEL)
    mem2d = mem.reshape(B_LOC * MEM_LEN, D_MODEL)
    mem_g = mem_norm.reshape(1, D_MODEL)
    saved, full = [], {}
    for i in range(2):
        xs, s_ffn1 = _ffn_fwd(xs, ffn1_norm[i:i + 1], "ffn1", i, full, get_weights)
        if i == 0:
            memn = _rms_fwd(mem2d, mem_g, deps=(xs,), name="mem_norm_fwd")
        full.update(get_weights((i, "mix_in"), (xs,)))
        mixer = "hgrn" if i == 0 else "gmlp"
        hm, zm = _norm_mm(xs, mix_norm[i:i + 1], full[(f"{mixer}_w_in", 0)], swiglu=False, tm=1024, tn=1280, deps=full.pop("deps", ()),
                          name=f"l{i}_mix_in")
        kv = _mm(memn, full[("mem_w_kv", i)], tb=True, tm=512, tn=512, tk=D_MODEL, out_dtype=F32, name=f"l{i}_mem_kv")
        o_mem = _attn_fwd(zm, kv, name=f"l{i}_attn")
        if i == 0:
            cat, o_pre, s_all = _hgrn_fwd(zm, o_mem, lb_logits, hgrn_gnorm)
            mix_saved = (o_pre, s_all)
        else:
            cat = _gmlp_fwd(zm, o_mem, full["ln_g"], full["ln_b"], w_s, b_st)
            mix_saved = ()
        x_mix = xs
        full.update(get_weights((i, "mix_out"), (cat,)))
        xs = _mm(cat, full[(f"{mixer}_w_out", 0)], tm=512, tn=D_MODEL, tk=cat.shape[1], out_dtype=F32, res=xs,
                 deps=full.pop("deps", ()), name=f"l{i}_mix_out")
        xs, s_ffn2 = _ffn_fwd(xs, ffn2_norm[i:i + 1], "ffn2", i, full, get_weights)
        saved.append((s_ffn1, (x_mix, hm, kv, zm, cat, mix_saved), s_ffn2))

    dx, dx16, d_final, loss_part = _loss_head(xs, final_norm.reshape(1, D_MODEL), loss_target.reshape(N_TOK, D_MODEL))

    small = {"final_norm": [d_final]}
    d_ffn1, d_ffn2, d_mix = [None, None], [None, None], [None, None]
    dmemn = jnp.zeros((B_LOC * MEM_LEN, D_MODEL), F32)
    deps = ()
    for i in (1, 0):
        s_ffn1, (x_mix, hm, kv, zm, cat, mix_saved), s_ffn2 = saved[i]
        dx, dx16, d_ffn2[i], dw_in_t, dw_out = _ffn_bwd(
            dx, dx16, s_ffn2, ffn2_norm[i:i + 1], full[("ffn2_w_in", i)], full[("ffn2_w_out", i)], f"l{i}_ffn2", deps)
        deps = put_grads((i, "ffn2"), {("ffn2_w_in", i): dw_in_t, ("ffn2_w_out", i): dw_out})
        mixer = "hgrn" if i == 0 else "gmlp"
        w_in_t, w_out = full[(f"{mixer}_w_in", 0)], full[(f"{mixer}_w_out", 0)]
        width = cat.shape[1]
        g_mix = {}
        g_mix[(f"{mixer}_w_out", 0)] = _mm(cat, dx16, ta=True, tm=1024, tn=D_MODEL, tk=N_TOK, out_dtype=BF16,
                                           deps=deps, name=f"l{i}_mix_out_wgrad")
        dcat = _mm(dx16, w_out, tb=True, tm=1024, tn=width // 2, tk=D_MODEL, out_dtype=F32, name=f"l{i}_mix_out_dgrad")
        dq, dk, dv = _attn_bwd(zm, kv, dcat, do_off=width - XA_HEADS * XA_DIM, name=f"l{i}_attn_bwd")
        if i == 0:
            dzm, dlbl, dgn = _hgrn_bwd(zm, mix_saved[0], dcat, dq, mix_saved[1], lb_logits, hgrn_gnorm)
            small["lb_logits"], small["hgrn_gnorm"] = [dlbl], [dgn]
            deps = ()
        else:
            dzm, dws, dbt, dlng, dlnb = _gmlp_bwd(zm, dcat, dq, full["ln_g"], full["ln_b"], w_s, b_st)
            small["gmlp_b_s"], small["gmlp_ln_g"], small["gmlp_ln_b"] = [dbt.T], [dlng], [dlnb]
            deps = put_grads("w_s", dws)
        g_mix[(f"{mixer}_w_in", 0)] = _mm(dzm, hm, ta=True, tm=1024, tn=D_MODEL, tk=N_TOK, out_dtype=BF16, deps=deps,
                                          name=f"l{i}_mix_in_wgrad")
        dkv = jnp.concatenate([dk, dv], axis=1)
        g_mix[("mem_w_kv", i)] = _mm(dkv, memn, ta=True, tm=512, tn=D_MODEL, tk=B_LOC * MEM_LEN, out_dtype=BF16,
                                     name=f"l{i}_mem_kv_wgrad")
        deps = put_grads((i, "mix"), g_mix)
        dx, dx16, d_mix[i] = _dgrad_norm_bwd(dzm, w_in_t, x_mix, mix_norm[i:i + 1], dx, deps=deps,
                                             name=f"l{i}_mix_in_dgrad")
        dmemn = _mm(dkv, full[("mem_w_kv", i)], tm=B_LOC * MEM_LEN, tn=D_MODEL, tk=512, out_dtype=F32, res=dmemn,
                    name=f"l{i}_mem_kv_dgrad")
        def send_small(dg, i=i, dmemn=dmemn):
            d_ffn1[i] = dg
            _, _, dmem_g = _rms_bwd(mem2d, mem_g, dmemn, dmemn, name="mem_norm_bwd")
            small.update(mem_norm=[dmem_g], ffn1_norm=d_ffn1, ffn2_norm=d_ffn2, mix_norm=d_mix)
            return put_grads("small", small)

        if i == 0:
            send_out = lambda dw_out: put_grads((0, "ffn1_out"), {("ffn1_w_out", 0): dw_out})
            dx, dx16, d_ffn1[i], dw_in_t, _ = _ffn_bwd(
                dx, dx16, s_ffn1, ffn1_norm[i:i + 1], full[("ffn1_w_in", i)], full[("ffn1_w_out", i)], f"l{i}_ffn1",
                after_out_wgrad=send_out, before_in_wgrad=send_small)
            deps = put_grads((0, "ffn1_in"), {("ffn1_w_in", 0): dw_in_t})
        else:
            dx, dx16, d_ffn1[i], dw_in_t, dw_out = _ffn_bwd(
                dx, dx16, s_ffn1, ffn1_norm[i:i + 1], full[("ffn1_w_in", i)], full[("ffn1_w_out", i)], f"l{i}_ffn1")
            deps = put_grads((i, "ffn1"), {("ffn1_w_in", i): dw_in_t, ("ffn1_w_out", i): dw_out})
    return dx, loss_part, deps
```

```python
import functools
import math

import jax
import jax.numpy as jnp
from jax import lax
from jax.experimental import pallas as pl
from jax.experimental.pallas import tpu as pltpu

F32 = jnp.float32
BF16 = jnp.bfloat16

D_MODEL = 1024
SEQ = 2048
B_LOC = 2
N_TOK = B_LOC * SEQ
MEM_LEN = 256
N_DEV = 8
EPS = 1e-6
D_FF = 2816
HG_HEADS = 8
HG_DIM = 128
HG_CHUNK = 64
HG_NCHUNK = SEQ // HG_CHUNK
GM_CHUNK = 128
GM_GROUPS = 8
GM_WIDTH = 2048
GM_GDIM = GM_WIDTH // GM_GROUPS
XA_HEADS = 4
XA_DIM = 256
XA_OFF = 4096

ADAM_LR = 0.001
ADAM_B1 = 0.9
ADAM_B2 = 0.999
ADAM_EPS = 1e-08
ADAM_WD = 0.01
ADAM_STEP = 10

VMEM_LIMIT_BYTES = 56 * 1024 * 1024
MESH_AXES = ("x", "y", "c")

GROUPS = (
    ("ffn1_w_in", True, 2, 704),
    ("ffn1_w_out", False, 2, 352),
    ("mem_w_kv", True, 2, 256),
    ("hgrn_w_in", True, 1, 640),
    ("hgrn_w_out", False, 1, 256),
    ("gmlp_w_in", True, 1, 640),
    ("gmlp_w_out", False, 1, 384),
    ("ffn2_w_in", True, 2, 704),
    ("ffn2_w_out", False, 2, 352),
)
GROUP_LAYERS = {name: layers for name, _, layers, _ in GROUPS}


def _stage_pieces(layer, block):
    if block == "mix":
        mixer = "hgrn" if layer == 0 else "gmlp"
        return (("mem_w_kv", layer), (f"{mixer}_w_in", 0), (f"{mixer}_w_out", 0))
    return ((f"{block}_w_in", layer), (f"{block}_w_out", layer))


ANY_SPEC = pl.BlockSpec(memory_space=pl.ANY)
HBM_SPEC = pl.BlockSpec(memory_space=pltpu.HBM)
SEM_SPEC = pl.BlockSpec(memory_space=pltpu.SEMAPHORE)


def _cp(*sem):
    return pltpu.CompilerParams(dimension_semantics=sem, vmem_limit_bytes=VMEM_LIMIT_BYTES)


def _sigmoid(x):
    return 0.5 * jnp.tanh(0.5 * x) + 0.5


def _gelu_parts(x):
    cdf = 0.5 * (1.0 + lax.erf(x * (1.0 / math.sqrt(2.0))))
    pdf = jnp.exp(-0.5 * x * x) * (1.0 / math.sqrt(2.0 * math.pi))
    return x * cdf, cdf + x * pdf


def _mm(a, b, *, ta=False, tb=False, tm, tn, tk, out_dtype, res=None, scale=1.0, deps=(), name):
    m, k = (a.shape[1], a.shape[0]) if ta else a.shape
    n, kb = b.shape if tb else (b.shape[1], b.shape[0])
    assert k == kb and m % tm == 0 and n % tn == 0 and k % tk == 0, (name, a.shape, b.shape)
    nk = k // tk
    dn = (((0 if ta else 1,), (1 if tb else 0,)), ((), ()))
    n_in = 2 + (res is not None) + len(deps)

    def body(*refs):
        a_ref, b_ref = refs[:2]
        r_ref = refs[2] if res is not None else None
        o_ref, scr = refs[n_in], refs[n_in + 1:]
        p = lax.dot_general(a_ref[...].astype(BF16), b_ref[...].astype(BF16), dn, preferred_element_type=F32)

        def finish(acc):
            if scale != 1.0:
                acc = scale * acc
            if r_ref is not None:
                acc = r_ref[...] + acc
            o_ref[...] = acc.astype(out_dtype)

        if nk == 1:
            finish(p)
        else:
            acc_ref = scr[0]
            kk = pl.program_id(2)

            @pl.when(kk == 0)
            def _():
                acc_ref[...] = p

            @pl.when(kk > 0)
            def _():
                acc_ref[...] += p

            @pl.when(kk == nk - 1)
            def _():
                finish(acc_ref[...])

    a_spec = pl.BlockSpec((tk, tm), lambda i, j, kk: (kk, i)) if ta else pl.BlockSpec((tm, tk), lambda i, j, kk: (i, kk))
    b_mode = dict(pipeline_mode=pl.Buffered(1)) if n == tn and nk == 1 else {}
    if tb:
        b_spec = pl.BlockSpec((tn, tk), lambda i, j, kk: (j, kk), **b_mode)
    else:
        b_spec = pl.BlockSpec((tk, tn), lambda i, j, kk: (kk, j), **b_mode)
    o_spec = pl.BlockSpec((tm, tn), lambda i, j, kk: (i, j))
    in_specs = [a_spec, b_spec] + ([o_spec] if res is not None else []) + [ANY_SPEC] * len(deps)
    args = (a, b) + ((res,) if res is not None else ()) + tuple(deps)
    return pl.pallas_call(
        body,
        name=name,
        grid=(m // tm, n // tn, nk),
        in_specs=in_specs,
        out_specs=o_spec,
        out_shape=jax.ShapeDtypeStruct((m, n), out_dtype),
        scratch_shapes=[pltpu.VMEM((tm, tn), F32)] if nk > 1 else [],
        compiler_params=_cp("parallel", "parallel", "arbitrary"),
    )(*args)


def _rms_fwd(x, g, *, name, deps=(), tm=512):
    rows = x.shape[0]

    def body(x_ref, g_ref, *rest):
        o_ref = rest[len(deps)]
        xv = x_ref[...]
        r = lax.rsqrt(jnp.mean(xv * xv, axis=-1, keepdims=True) + EPS)
        o_ref[...] = (xv * r * g_ref[...]).astype(BF16)

    row = pl.BlockSpec((tm, D_MODEL), lambda i: (i, 0))
    return pl.pallas_call(
        body,
        name=name,
        grid=(rows // tm,),
        in_specs=[row, pl.BlockSpec((1, D_MODEL), lambda i: (0, 0))] + [ANY_SPEC] * len(deps),
        out_specs=row,
        out_shape=jax.ShapeDtypeStruct((rows, D_MODEL), BF16),
        compiler_params=_cp("parallel"),
    )(x, g, *deps)


def _rms_bwd(x, g, dh, dres, *, name, deps=(), tm=512):
    rows = x.shape[0]

    def body(x_ref, g_ref, dh_ref, dres_ref, *rest):
        dx_ref, dx16_ref, dg_ref = rest[len(deps):]
        xv = x_ref[...]
        r = lax.rsqrt(jnp.mean(xv * xv, axis=-1, keepdims=True) + EPS)
        xhat = xv * r
        dhv = dh_ref[...]
        part = jnp.sum(dhv * xhat, axis=0, keepdims=True)

        @pl.when(pl.program_id(0) == 0)
        def _():
            dg_ref[...] = part

        @pl.when(pl.program_id(0) > 0)
        def _():
            dg_ref[...] += part

        dxh = dhv * g_ref[...]
        dx = dres_ref[...] + r * (dxh - xhat * jnp.mean(dxh * xhat, axis=-1, keepdims=True))
        dx_ref[...] = dx
        dx16_ref[...] = dx.astype(BF16)

    row = pl.BlockSpec((tm, D_MODEL), lambda i: (i, 0))
    vec = pl.BlockSpec((1, D_MODEL), lambda i: (0, 0))
    return pl.pallas_call(
        body,
        name=name,
        grid=(rows // tm,),
        in_specs=[row, vec, row, row] + [ANY_SPEC] * len(deps),
        out_specs=[row, row, vec],
        out_shape=[jax.ShapeDtypeStruct((rows, D_MODEL), F32), jax.ShapeDtypeStruct((rows, D_MODEL), BF16),
                   jax.ShapeDtypeStruct((1, D_MODEL), F32)],
        compiler_params=_cp("arbitrary"),
    )(x, g, dh, dres, *deps)


_NT = (((1,), (1,)), ((), ()))
_TN = (((0,), (0,)), ((), ()))


def _norm_mm(x, g, w_t, *, swiglu, name, tm, tn, deps=()):
    rows = w_t.shape[0]
    half = rows // 2
    nj = (half if swiglu else rows) // tn
    nd = len(deps)

    def body(x_ref, g_ref, w_ref, *rest):
        outs = rest[nd:]
        h_ref, z_ref = outs[:2]

        def norm():
            xv = x_ref[...]
            r = lax.rsqrt(jnp.mean(xv * xv, axis=-1, keepdims=True) + EPS)
            h_ref[...] = (xv * r * g_ref[...]).astype(BF16)

        if swiglu:
            norm()
            h = h_ref[...]
            for j in range(nj):
                cols = slice(j * tn, (j + 1) * tn)
                gate = lax.dot_general(h, w_ref[j * tn:(j + 1) * tn, :], _NT, preferred_element_type=F32)
                up = lax.dot_general(h, w_ref[half + j * tn:half + (j + 1) * tn, :], _NT, preferred_element_type=F32)
                s = _sigmoid(gate)
                silu = gate * s
                z_ref[0, :, cols] = (up * (s + silu * (1.0 - s))).astype(BF16)
                z_ref[1, :, cols] = silu.astype(BF16)
                outs[2][:, cols] = (silu * up).astype(BF16)
        else:
            j = pl.program_id(1)
            pl.when(j == 0)(norm)
            w = w_ref[pl.ds(pl.multiple_of(j * tn, tn), tn), :]
            z_ref[...] = lax.dot_general(h_ref[...], w, _NT, preferred_element_type=F32)

    grid = (N_TOK // tm,) if swiglu else (N_TOK // tm, nj)
    row = pl.BlockSpec((tm, D_MODEL), lambda i, *_: (i, 0))
    out_specs = [row]
    out_shape = [jax.ShapeDtypeStruct((N_TOK, D_MODEL), BF16)]
    if swiglu:
        out_specs += [pl.BlockSpec((2, tm, half), lambda i: (0, i, 0)), pl.BlockSpec((tm, half), lambda i: (i, 0))]
        out_shape += [jax.ShapeDtypeStruct((2, N_TOK, half), BF16), jax.ShapeDtypeStruct((N_TOK, half), BF16)]
    else:
        out_specs.append(pl.BlockSpec((tm, tn), lambda i, j: (i, j)))
        out_shape.append(jax.ShapeDtypeStruct((N_TOK, rows), F32))
    return pl.pallas_call(
        body,
        name=name,
        grid=grid,
        in_specs=[row, pl.BlockSpec((1, D_MODEL), lambda *_: (0, 0)),
                  pl.BlockSpec((rows, D_MODEL), lambda *_: (0, 0), pipeline_mode=pl.Buffered(1))] + [ANY_SPEC] * nd,
        out_specs=out_specs,
        out_shape=out_shape,
        compiler_params=_cp(*(("parallel",) if swiglu else ("parallel", "arbitrary"))),
    )(x, g, w_t, *deps)


def _swiglu_dgrad(dy16, w_out, z, *, scale, name, deps=(), tm=512, tn=1408):
    def body(dy_ref, w_ref, z_ref, *rest):
        dz_ref = rest[len(deps)]
        dy = dy_ref[...]
        for j in range(D_FF // tn):
            cols = slice(j * tn, (j + 1) * tn)
            da = lax.dot_general(dy, w_ref[cols, :], _NT, preferred_element_type=F32) * scale
            dz_ref[0, :, cols] = (da * z_ref[0, :, cols].astype(F32)).astype(BF16)
            dz_ref[1, :, cols] = (da * z_ref[1, :, cols].astype(F32)).astype(BF16)

    planes = pl.BlockSpec((2, tm, D_FF), lambda i: (0, i, 0))
    return pl.pallas_call(
        body,
        name=name,
        grid=(N_TOK // tm,),
        in_specs=[pl.BlockSpec((tm, D_MODEL), lambda i: (i, 0)),
                  pl.BlockSpec((D_FF, D_MODEL), lambda i: (0, 0), pipeline_mode=pl.Buffered(1)), planes]
        + [ANY_SPEC] * len(deps),
        out_specs=planes,
        out_shape=jax.ShapeDtypeStruct((2, N_TOK, D_FF), BF16),
        compiler_params=_cp("parallel"),
    )(dy16, w_out, z, *deps)


def _planes_wgrad(dz, h, *, name, deps=(), tm=1408):
    per_plane = D_FF // tm

    def body(a_ref, b_ref, *rest):
        o_ref = rest[len(deps)]
        o_ref[...] = lax.dot_general(a_ref[...], b_ref[...], _TN, preferred_element_type=F32).astype(BF16)

    return pl.pallas_call(
        body,
        name=name,
        grid=(2 * per_plane,),
        in_specs=[pl.BlockSpec((None, N_TOK, tm),
                               lambda i: (jnp.where(i < per_plane, 0, 1), 0, jnp.where(i < per_plane, i, i - per_plane))),
                  pl.BlockSpec((N_TOK, D_MODEL), lambda i: (0, 0), pipeline_mode=pl.Buffered(1))] + [ANY_SPEC] * len(deps),
        out_specs=pl.BlockSpec((tm, D_MODEL), lambda i: (i, 0)),
        out_shape=jax.ShapeDtypeStruct((2 * D_FF, D_MODEL), BF16),
        compiler_params=_cp("parallel"),
    )(dz, h, *deps)


def _dgrad_norm_bwd(dz, w_t, x, g, dres, *, name, deps=(), tm=512):
    planes = dz.ndim == 3
    rows = w_t.shape[0]
    half = rows // 2
    nd = len(deps)

    def body(a_ref, b_ref, x_ref, g_ref, dres_ref, *rest):
        dx_ref, dx16_ref, dg_ref = rest[nd:]
        if planes:
            dh = jnp.dot(a_ref[0], b_ref[:half, :], preferred_element_type=F32) + jnp.dot(
                a_ref[1], b_ref[half:, :], preferred_element_type=F32)
        else:
            dh = jnp.dot(a_ref[...], b_ref[...], preferred_element_type=F32)
        xv = x_ref[...]
        r = lax.rsqrt(jnp.mean(xv * xv, axis=-1, keepdims=True) + EPS)
        xhat = xv * r
        part = jnp.sum(dh * xhat, axis=0, keepdims=True)

        @pl.when(pl.program_id(0) == 0)
        def _():
            dg_ref[...] = part

        @pl.when(pl.program_id(0) > 0)
        def _():
            dg_ref[...] += part

        dxh = dh * g_ref[...]
        dx = dres_ref[...] + r * (dxh - xhat * jnp.mean(dxh * xhat, axis=-1, keepdims=True))
        dx_ref[...] = dx
        dx16_ref[...] = dx.astype(BF16)

    a_spec = pl.BlockSpec((2, tm, half), lambda i: (0, i, 0)) if planes else pl.BlockSpec((tm, rows), lambda i: (i, 0))
    row = pl.BlockSpec((tm, D_MODEL), lambda i: (i, 0))
    vec = pl.BlockSpec((1, D_MODEL), lambda i: (0, 0))
    return pl.pallas_call(
        body,
        name=name,
        grid=(N_TOK // tm,),
        in_specs=[a_spec, pl.BlockSpec((rows, D_MODEL), lambda i: (0, 0), pipeline_mode=pl.Buffered(1)), row, vec, row]
        + [ANY_SPEC] * nd,
        out_specs=[row, row, vec],
        out_shape=[jax.ShapeDtypeStruct((N_TOK, D_MODEL), F32), jax.ShapeDtypeStruct((N_TOK, D_MODEL), BF16),
                   jax.ShapeDtypeStruct((1, D_MODEL), F32)],
        compiler_params=_cp("arbitrary"),
    )(dz, w_t, x, g, dres, *deps)


def _loss_head(x, g, target, *, tm=512):
    def body(x_ref, g_ref, t_ref, dx_ref, dx16_ref, dg_ref, loss_ref):
        xv = x_ref[...]
        gv = g_ref[...]
        r = lax.rsqrt(jnp.mean(xv * xv, axis=-1, keepdims=True) + EPS)
        xhat = xv * r
        err = xhat * gv - t_ref[...]
        loss_part = jnp.zeros((1, 128), F32) + 0.5 * jnp.sum(jnp.mean(err * err, axis=-1, keepdims=True))
        dy = err * (1.0 / D_MODEL)
        dg_part = jnp.sum(dy * xhat, axis=0, keepdims=True)

        @pl.when(pl.program_id(0) == 0)
        def _():
            dg_ref[...] = dg_part
            loss_ref[...] = loss_part

        @pl.when(pl.program_id(0) > 0)
        def _():
            dg_ref[...] += dg_part
            loss_ref[...] += loss_part

        dxh = dy * gv
        dx = r * (dxh - xhat * jnp.mean(dxh * xhat, axis=-1, keepdims=True))
        dx_ref[...] = dx
        dx16_ref[...] = dx.astype(BF16)

    row = pl.BlockSpec((tm, D_MODEL), lambda i: (i, 0))
    vec = pl.BlockSpec((1, D_MODEL), lambda i: (0, 0))
    return pl.pallas_call(
        body,
        name="loss_head",
        grid=(N_TOK // tm,),
        in_specs=[row, vec, row],
        out_specs=[row, row, vec, pl.BlockSpec((1, 128), lambda i: (0, 0))],
        out_shape=[
            jax.ShapeDtypeStruct((N_TOK, D_MODEL), F32),
            jax.ShapeDtypeStruct((N_TOK, D_MODEL), BF16),
            jax.ShapeDtypeStruct((1, D_MODEL), F32),
            jax.ShapeDtypeStruct((1, 128), F32),
        ],
        compiler_params=_cp("arbitrary"),
    )(x, g, target)


XA_TQ = 1024
XA_SCALE = XA_DIM ** -0.5


def _attn_probs(q16, k16):
    s = lax.dot_general(q16, k16, _NT, preferred_element_type=F32) * XA_SCALE
    e = jnp.exp(s - jnp.max(s, axis=-1, keepdims=True))
    return e / jnp.sum(e, axis=-1, keepdims=True)


def _attn_fwd(z, kv, *, name):
    nt = SEQ // XA_TQ

    def body(q_ref, k_ref, v_ref, o_ref):
        p = _attn_probs(q_ref[...].astype(BF16), k_ref[...].astype(BF16))
        o_ref[...] = jnp.dot(p.astype(BF16), v_ref[...].astype(BF16), preferred_element_type=F32).astype(BF16)

    return pl.pallas_call(
        body,
        name=name,
        grid=(B_LOC, XA_HEADS, nt),
        in_specs=[
            pl.BlockSpec((XA_TQ, XA_DIM), lambda b, h, t: (b * nt + t, XA_OFF // XA_DIM + h)),
            pl.BlockSpec((MEM_LEN, XA_DIM), lambda b, h, t: (b, h)),
            pl.BlockSpec((MEM_LEN, XA_DIM), lambda b, h, t: (b, XA_HEADS + h)),
        ],
        out_specs=pl.BlockSpec((XA_TQ, XA_DIM), lambda b, h, t: (b * nt + t, h)),
        out_shape=jax.ShapeDtypeStruct((N_TOK, XA_HEADS * XA_DIM), BF16),
        compiler_params=_cp("parallel", "parallel", "arbitrary"),
    )(z, kv, kv)


def _attn_bwd(z, kv, dcat, *, do_off, name):
    nt = SEQ // XA_TQ

    def body(q_ref, k_ref, v_ref, do_ref, dq_ref, dk_ref, dv_ref):
        q16 = q_ref[...].astype(BF16)
        k16 = k_ref[...].astype(BF16)
        v16 = v_ref[...].astype(BF16)
        do16 = do_ref[...].astype(BF16)
        p = _attn_probs(q16, k16)
        dv_part = lax.dot_general(p.astype(BF16), do16, _TN, preferred_element_type=F32)
        dp = lax.dot_general(do16, v16, _NT, preferred_element_type=F32)
        ds16 = (p * (dp - jnp.sum(dp * p, axis=-1, keepdims=True)) * XA_SCALE).astype(BF16)
        dq_ref[...] = jnp.dot(ds16, k16, preferred_element_type=F32).astype(BF16)
        dk_part = lax.dot_general(ds16, q16, _TN, preferred_element_type=F32)

        @pl.when(pl.program_id(2) == 0)
        def _():
            dk_ref[...] = dk_part
            dv_ref[...] = dv_part

        @pl.when(pl.program_id(2) > 0)
        def _():
            dk_ref[...] += dk_part
            dv_ref[...] += dv_part

    qspec = pl.BlockSpec((XA_TQ, XA_DIM), lambda b, h, t: (b * nt + t, XA_OFF // XA_DIM + h))
    kspec = lambda off: pl.BlockSpec((MEM_LEN, XA_DIM), lambda b, h, t: (b, off + h))
    return pl.pallas_call(
        body,
        name=name,
        grid=(B_LOC, XA_HEADS, nt),
        in_specs=[qspec, kspec(0), kspec(XA_HEADS),
                  pl.BlockSpec((XA_TQ, XA_DIM), lambda b, h, t: (b * nt + t, do_off // XA_DIM + h))],
        out_specs=[pl.BlockSpec((XA_TQ, XA_DIM), lambda b, h, t: (b * nt + t, h)), kspec(0), kspec(0)],
        out_shape=[
            jax.ShapeDtypeStruct((N_TOK, XA_HEADS * XA_DIM), BF16),
            jax.ShapeDtypeStruct((B_LOC * MEM_LEN, XA_HEADS * XA_DIM), F32),
            jax.ShapeDtypeStruct((B_LOC * MEM_LEN, XA_HEADS * XA_DIM), F32),
        ],
        compiler_params=_cp("parallel", "parallel", "arbitrary"),
    )(z, kv, kv, dcat)


def _tril(n):
    return lax.broadcasted_iota(jnp.int32, (n, n), 0) >= lax.broadcasted_iota(jnp.int32, (n, n), 1)


def _lower_bound(lbl):
    e = jnp.exp(lbl - jnp.max(lbl, axis=0, keepdims=True))
    p = e / jnp.sum(e, axis=0, keepdims=True)
    return p[0:1, :], p


def _hgrn_gates(zq, zf, lb, tril_f):
    sig = _sigmoid(zf)
    f = lb + (1.0 - lb) * sig
    kk = 1.0 - f
    sq = _sigmoid(zq)
    q = zq * sq
    b = jnp.dot(tril_f, jnp.log(f), preferred_element_type=F32, precision=lax.Precision.HIGHEST)
    bl = b[HG_CHUNK - 1:HG_CHUNK, :]
    return q, sq, sig, f, kk, b, bl


HG_TB = 512
HG_CPB = HG_TB // HG_CHUNK
HG_NT = SEQ // HG_TB
HG_WIDTH = HG_HEADS * HG_DIM


def _head(h, section=0):
    return slice(section * HG_WIDTH + h * HG_DIM, section * HG_WIDTH + (h + 1) * HG_DIM)


def _hgrn_fwd(z, o_mem, lb_logits, gnorm):
    def body(zq_ref, zf_ref, zi_ref, zg_ref, omem_ref, lbl_ref, gn_ref, o_ref, opre_ref, sall_ref, st_ref):
        lb, _ = _lower_bound(lbl_ref[...])
        gn = gn_ref[...]
        mask = _tril(HG_CHUNK)
        tril_f = mask.astype(F32)
        o_ref[:, HG_WIDTH:] = omem_ref[...]

        @pl.when(pl.program_id(1) == 0)
        def _():
            st_ref[...] = jnp.zeros_like(st_ref)

        def chunk(c, carry):
            rows = pl.ds(pl.multiple_of(c * HG_CHUNK, HG_CHUNK), HG_CHUNK)
            q, _, _, _, kk, b, bl = _hgrn_gates(zq_ref[rows, :], zf_ref[rows, :], lb, tril_f)
            v16 = zi_ref[rows, :].astype(BF16)
            qd16 = (q * jnp.exp(b)).astype(BF16)
            ki16 = (kk * jnp.exp(-b)).astype(BF16)
            kd16 = (kk * jnp.exp(bl - b)).astype(BF16)
            ebl = jnp.exp(bl)
            zg = zg_ref[rows, :]
            gate = zg * _sigmoid(zg)
            for h in range(HG_HEADS):
                sl = _head(h)
                a = jnp.where(mask, lax.dot_general(qd16[:, sl], ki16[:, sl], _NT, preferred_element_type=F32), 0.0)
                st = st_ref[h]
                sall_ref[0, h, c] = st
                o = jnp.dot(a.astype(BF16), v16[:, sl], preferred_element_type=F32) + lax.dot_general(
                    qd16[:, sl], st.astype(BF16), _NT, preferred_element_type=F32)
                st_ref[h] = st * ebl[:, sl] + lax.dot_general(v16[:, sl], kd16[:, sl], _TN, preferred_element_type=F32)
                opre_ref[rows, sl] = o
                r = lax.rsqrt(jnp.mean(o * o, axis=-1, keepdims=True) + EPS)
                o_ref[rows, sl] = ((o * r * gn) * gate[:, sl]).astype(BF16)
            return carry

        lax.fori_loop(0, HG_CPB, chunk, 0, unroll=2)

    zspec = lambda s: pl.BlockSpec((HG_TB, HG_WIDTH), lambda b, t: (b * HG_NT + t, s))
    return pl.pallas_call(
        body,
        name="hgrn_fwd",
        grid=(B_LOC, HG_NT),
        in_specs=[zspec(0), zspec(1), zspec(2), zspec(3), zspec(0),
                  pl.BlockSpec((3, HG_WIDTH), lambda b, t: (0, 0)), pl.BlockSpec((1, HG_DIM), lambda b, t: (0, 0))],
        out_specs=[pl.BlockSpec((HG_TB, 2 * HG_WIDTH), lambda b, t: (b * HG_NT + t, 0)), zspec(0),
                   pl.BlockSpec((1, HG_HEADS, HG_CPB, HG_DIM, HG_DIM), lambda b, t: (b, 0, t, 0, 0))],
        out_shape=[
            jax.ShapeDtypeStruct((N_TOK, 2 * HG_WIDTH), BF16),
            jax.ShapeDtypeStruct((N_TOK, HG_WIDTH), F32),
            jax.ShapeDtypeStruct((B_LOC, HG_HEADS, HG_NCHUNK, HG_DIM, HG_DIM), F32),
        ],
        scratch_shapes=[pltpu.VMEM((HG_HEADS, HG_DIM, HG_DIM), F32)],
        compiler_params=_cp("parallel", "arbitrary"),
    )(z, z, z, z, o_mem, lb_logits, gnorm)


def _hgrn_bwd(z, opre, dcat, dq_mem, sall, lb_logits, gnorm):
    def body(zq_ref, zf_ref, zi_ref, zg_ref, opre_ref, dout_ref, dqm_ref, sall_ref, lbl_ref, gn_ref,
             dz_ref, dlbl_ref, dgn_ref, dst_ref, dlb_ref, dgn_acc, db_ref, dkk_ref, dbl_ref):
        b_id, t_id = pl.program_id(0), pl.program_id(1)
        lb, p = _lower_bound(lbl_ref[...])
        gn = gn_ref[...]
        mask = _tril(HG_CHUNK)
        tril_f = mask.astype(F32)
        dz_ref[:, 4 * HG_WIDTH:] = dqm_ref[...]

        @pl.when(t_id == 0)
        def _():
            dst_ref[...] = jnp.zeros_like(dst_ref)
            dlb_ref[...] = jnp.zeros_like(dlb_ref)

        @pl.when((b_id == 0) & (t_id == 0))
        def _():
            dgn_acc[...] = jnp.zeros_like(dgn_acc)

        def chunk(i, carry):
            c = HG_CPB - 1 - i
            rows = pl.ds(pl.multiple_of(c * HG_CHUNK, HG_CHUNK), HG_CHUNK)
            zq, zg = zq_ref[rows, :], zg_ref[rows, :]
            q, sq, sig, f, kk, b, bl = _hgrn_gates(zq, zf_ref[rows, :], lb, tril_f)
            v16 = zi_ref[rows, :].astype(BF16)
            eb, enb, ebl_b, ebl = jnp.exp(b), jnp.exp(-b), jnp.exp(bl - b), jnp.exp(bl)
            qd, ki, kd = q * eb, kk * enb, kk * ebl_b
            qd16, ki16, kd16 = qd.astype(BF16), ki.astype(BF16), kd.astype(BF16)
            o_all = opre_ref[rows, :]
            dout = dout_ref[rows, :]
            sg = _sigmoid(zg)
            d_on_all = dout * (zg * sg)
            dgate = dout * (sg * (1.0 + zg * (1.0 - sg)))
            dq_scale = eb * (sq * (1.0 + zq * (1.0 - sq)))
            for h in range(HG_HEADS):
                sl = _head(h)
                o = o_all[:, sl]
                r = lax.rsqrt(jnp.mean(o * o, axis=-1, keepdims=True) + EPS)
                ohat = o * r
                d_on = d_on_all[:, sl]
                dz_ref[rows, _head(h, 3)] = (dgate[:, sl] * (ohat * gn)).astype(BF16)
                dgn_acc[...] += jnp.sum(d_on * ohat, axis=0, keepdims=True)
                dohat = d_on * gn
                do16 = (r * (dohat - ohat * jnp.mean(dohat * ohat, axis=-1, keepdims=True))).astype(BF16)
                st = sall_ref[0, h, c]
                dst = dst_ref[h]
                st16, dst16 = st.astype(BF16), dst.astype(BF16)
                qd_h, ki_h, kd_h, v_h = qd16[:, sl], ki16[:, sl], kd16[:, sl], v16[:, sl]
                a16 = jnp.where(mask, lax.dot_general(qd_h, ki_h, _NT, preferred_element_type=F32), 0.0).astype(BF16)
                da16 = jnp.where(mask, lax.dot_general(do16, v_h, _NT, preferred_element_type=F32), 0.0).astype(BF16)
                dv = lax.dot_general(a16, do16, _TN, preferred_element_type=F32) + lax.dot_general(
                    kd_h, dst16, _NT, preferred_element_type=F32)
                dqd = jnp.dot(da16, ki_h, preferred_element_type=F32) + jnp.dot(do16, st16, preferred_element_type=F32)
                dki = lax.dot_general(da16, qd_h, _TN, preferred_element_type=F32)
                dkd = jnp.dot(v_h, dst16, preferred_element_type=F32)
                dbl_ref[:, sl] = jnp.sum(dkd * kd[:, sl], axis=0, keepdims=True) + ebl[:, sl] * jnp.sum(
                    st * dst, axis=0, keepdims=True)
                dst_ref[h] = dst * ebl[:, sl] + lax.dot_general(do16, qd_h, _TN, preferred_element_type=F32)
                dz_ref[rows, _head(h, 2)] = dv.astype(BF16)
                dz_ref[rows, sl] = (dqd * dq_scale[:, sl]).astype(BF16)
                dkk_ref[:, sl] = dki * enb[:, sl] + dkd * ebl_b[:, sl]
                db_ref[:, sl] = dqd * qd[:, sl] - dki * ki[:, sl] - dkd * kd[:, sl]
            dlogf = lax.dot_general(tril_f, db_ref[...], _TN, preferred_element_type=F32,
                                    precision=lax.Precision.HIGHEST) + dbl_ref[...]
            df = dlogf / f - dkk_ref[...]
            dz_ref[rows, HG_WIDTH:2 * HG_WIDTH] = (df * (1.0 - lb) * sig * (1.0 - sig)).astype(BF16)
            dlb_ref[...] += jnp.sum(df * (1.0 - sig), axis=0, keepdims=True)
            return carry

        lax.fori_loop(0, HG_CPB, chunk, 0, unroll=2)

        @pl.when(t_id == HG_NT - 1)
        def _():
            row0 = (lax.broadcasted_iota(jnp.int32, (3, HG_WIDTH), 0) == 0).astype(F32)
            dlbl_part = dlb_ref[...] * lb * (row0 - p)

            @pl.when(b_id == 0)
            def _():
                dlbl_ref[...] = dlbl_part

            @pl.when(b_id > 0)
            def _():
                dlbl_ref[...] += dlbl_part

            dgn_ref[...] = dgn_acc[...]

    rev = lambda b, t: b * HG_NT + HG_NT - 1 - t
    zspec = lambda s: pl.BlockSpec((HG_TB, HG_WIDTH), lambda b, t: (rev(b, t), s))
    return pl.pallas_call(
        body,
        name="hgrn_bwd",
        grid=(B_LOC, HG_NT),
        in_specs=[zspec(0), zspec(1), zspec(2), zspec(3), zspec(0), zspec(0), zspec(0),
                  pl.BlockSpec((1, HG_HEADS, HG_CPB, HG_DIM, HG_DIM), lambda b, t: (b, 0, HG_NT - 1 - t, 0, 0)),
                  pl.BlockSpec((3, HG_WIDTH), lambda b, t: (0, 0)), pl.BlockSpec((1, HG_DIM), lambda b, t: (0, 0))],
        out_specs=[pl.BlockSpec((HG_TB, 5 * HG_WIDTH), lambda b, t: (rev(b, t), 0)),
                   pl.BlockSpec((3, HG_WIDTH), lambda b, t: (0, 0)), pl.BlockSpec((1, HG_DIM), lambda b, t: (0, 0))],
        out_shape=[jax.ShapeDtypeStruct((N_TOK, 5 * HG_WIDTH), BF16),
                   jax.ShapeDtypeStruct((3, HG_WIDTH), F32), jax.ShapeDtypeStruct((1, HG_DIM), F32)],
        scratch_shapes=[pltpu.VMEM((HG_HEADS, HG_DIM, HG_DIM), F32), pltpu.VMEM((1, HG_WIDTH), F32),
                        pltpu.VMEM((1, HG_DIM), F32), pltpu.VMEM((HG_CHUNK, HG_WIDTH), F32),
                        pltpu.VMEM((HG_CHUNK, HG_WIDTH), F32), pltpu.VMEM((1, HG_WIDTH), F32)],
        compiler_params=_cp("arbitrary", "arbitrary"),
    )(z, z, z, z, opre, dcat, dq_mem, sall, lb_logits, gnorm)


GM_TM = 256


def _gmlp_norm(zv, ln_g, ln_b):
    gv, dgelu = _gelu_parts(zv)
    xc = gv - jnp.mean(gv, axis=-1, keepdims=True)
    rstd = lax.rsqrt(jnp.mean(xc * xc, axis=-1, keepdims=True) + EPS)
    vhat = xc * rstd
    return vhat * ln_g + ln_b, vhat, rstd, dgelu


def _gmlp_specs():
    half = lambda j: pl.BlockSpec((GM_TM, GM_WIDTH), lambda i: (i, j))
    vec = pl.BlockSpec((1, GM_WIDTH), lambda i: (0, 0))
    w = pl.BlockSpec((GM_GROUPS, GM_CHUNK, GM_CHUNK), lambda i: (0, 0, 0))
    bt = pl.BlockSpec((GM_CHUNK, GM_GROUPS), lambda i: (0, 0))
    return half, vec, w, bt


def _gmlp_fwd(z, o_mem, ln_g, ln_b, w_s, b_st):
    def body(zu_ref, zv_ref, omem_ref, g_ref, b_ref, w_ref, bt_ref, o_ref):
        o_ref[:, GM_WIDTH:] = omem_ref[...]
        u, _ = _gelu_parts(zu_ref[...])
        v, _, _, _ = _gmlp_norm(zv_ref[...], g_ref[...], b_ref[...])
        v16 = v.astype(BF16)
        mask = _tril(GM_CHUNK)
        bt = bt_ref[...]
        for g in range(GM_GROUPS):
            wm16 = jnp.where(mask, w_ref[g], 0.0).astype(BF16)
            cols = slice(g * GM_GDIM, (g + 1) * GM_GDIM)
            for c in range(GM_TM // GM_CHUNK):
                rows = slice(c * GM_CHUNK, (c + 1) * GM_CHUNK)
                mixed = jnp.dot(wm16, v16[rows, cols], preferred_element_type=F32) + bt[:, g:g + 1]
                o_ref[rows, cols] = (u[rows, cols] * mixed).astype(BF16)

    half, vec, w, bt = _gmlp_specs()
    return pl.pallas_call(
        body,
        name="gmlp_fwd",
        grid=(N_TOK // GM_TM,),
        in_specs=[half(0), half(1), pl.BlockSpec((GM_TM, XA_HEADS * XA_DIM), lambda i: (i, 0)), vec, vec, w, bt],
        out_specs=pl.BlockSpec((GM_TM, GM_WIDTH + XA_HEADS * XA_DIM), lambda i: (i, 0)),
        out_shape=jax.ShapeDtypeStruct((N_TOK, GM_WIDTH + XA_HEADS * XA_DIM), BF16),
        compiler_params=_cp("parallel"),
    )(z, z, o_mem, ln_g, ln_b, w_s, b_st)


def _gmlp_bwd(z, dcat, dq_mem, ln_g, ln_b, w_s, b_st):
    def body(zu_ref, zv_ref, dout_ref, dqm_ref, g_ref, b_ref, w_ref, bt_ref,
             dz_ref, dw_ref, dbt_ref, dg_ref, db_ref, dv_ref):
        dz_ref[:, 2 * GM_WIDTH:] = dqm_ref[...]
        @pl.when(pl.program_id(0) == 0)
        def _():
            dw_ref[...] = jnp.zeros_like(dw_ref)
            dbt_ref[...] = jnp.zeros_like(dbt_ref)
            dg_ref[...] = jnp.zeros_like(dg_ref)
            db_ref[...] = jnp.zeros_like(db_ref)

        zu = zu_ref[...]
        u, du_dz = _gelu_parts(zu)
        ln_g = g_ref[...]
        v, vhat, rstd, dgv_dz = _gmlp_norm(zv_ref[...], ln_g, b_ref[...])
        v16 = v.astype(BF16)
        dout = dout_ref[...]
        dmixed = dout * u
        dm16 = dmixed.astype(BF16)
        mask = _tril(GM_CHUNK)
        bt = bt_ref[...]
        group_id = lax.broadcasted_iota(jnp.int32, (1, GM_GROUPS), 1)
        dbt = jnp.zeros((GM_CHUNK, GM_GROUPS), F32)
        for g in range(GM_GROUPS):
            wm16 = jnp.where(mask, w_ref[g], 0.0).astype(BF16)
            cols = slice(g * GM_GDIM, (g + 1) * GM_GDIM)
            dw = jnp.zeros((GM_CHUNK, GM_CHUNK), F32)
            dbt_g = jnp.zeros((GM_CHUNK, 1), F32)
            for c in range(GM_TM // GM_CHUNK):
                rows = slice(c * GM_CHUNK, (c + 1) * GM_CHUNK)
                mixed = jnp.dot(wm16, v16[rows, cols], preferred_element_type=F32) + bt[:, g:g + 1]
                dz_ref[rows, cols] = (dout[rows, cols] * mixed * du_dz[rows, cols]).astype(BF16)
                dw += lax.dot_general(dm16[rows, cols], v16[rows, cols], _NT, preferred_element_type=F32)
                dbt_g += jnp.sum(dmixed[rows, cols], axis=-1, keepdims=True)
                dv_ref[rows, cols] = lax.dot_general(wm16, dm16[rows, cols], _TN, preferred_element_type=F32)
            dw_ref[g] += jnp.where(mask, dw, 0.0)
            dbt = dbt + dbt_g * (group_id == g).astype(F32)
        dbt_ref[...] += dbt
        dv = dv_ref[...]
        dg_ref[...] += jnp.sum(dv * vhat, axis=0, keepdims=True)
        db_ref[...] += jnp.sum(dv, axis=0, keepdims=True)
        dvh = dv * ln_g
        dgv = rstd * (dvh - jnp.mean(dvh, axis=-1, keepdims=True) - vhat * jnp.mean(dvh * vhat, axis=-1, keepdims=True))
        dz_ref[:, GM_WIDTH:2 * GM_WIDTH] = (dgv * dgv_dz).astype(BF16)

    half, vec, w, bt = _gmlp_specs()
    dz_width = 2 * GM_WIDTH + XA_HEADS * XA_DIM
    return pl.pallas_call(
        body,
        name="gmlp_bwd",
        grid=(N_TOK // GM_TM,),
        in_specs=[half(0), half(1), half(0), pl.BlockSpec((GM_TM, XA_HEADS * XA_DIM), lambda i: (i, 0)), vec, vec, w, bt],
        out_specs=[pl.BlockSpec((GM_TM, dz_width), lambda i: (i, 0)), w, bt, vec, vec],
        out_shape=[jax.ShapeDtypeStruct((N_TOK, dz_width), BF16),
                   jax.ShapeDtypeStruct((GM_GROUPS, GM_CHUNK, GM_CHUNK), F32),
                   jax.ShapeDtypeStruct((GM_CHUNK, GM_GROUPS), F32),
                   jax.ShapeDtypeStruct((1, GM_WIDTH), F32), jax.ShapeDtypeStruct((1, GM_WIDTH), F32)],
        scratch_shapes=[pltpu.VMEM((GM_TM, GM_WIDTH), F32)],
        compiler_params=_cp("arbitrary"),
    )(z, z, dcat, dq_mem, ln_g, ln_b, w_s, b_st)


def _own_slot(shape):
    return pl.BlockSpec((None,) + tuple(shape), lambda i, me_ref: (me_ref[0],) + (0,) * len(shape))


def _place_rows(w, layer, cuts_columns, me, *, name, deps=()):
    _, r, c = w.shape
    n = c if cuts_columns else r

    def body(me_ref, w_ref, *rest):
        o_ref = rest[len(deps)]
        wv = w_ref[...]
        o_ref[...] = (wv.T if cuts_columns else wv).astype(BF16)

    return pl.pallas_call(
        body,
        name=name,
        grid_spec=pltpu.PrefetchScalarGridSpec(
            num_scalar_prefetch=1, grid=(1,),
            in_specs=[pl.BlockSpec((None, r, c), lambda i, me_ref: (layer, 0, 0))] + [ANY_SPEC] * len(deps),
            out_specs=_own_slot((n, D_MODEL))),
        out_shape=jax.ShapeDtypeStruct((N_DEV, n, D_MODEL), BF16),
        compiler_params=_cp("arbitrary"),
    )(me, w, *deps)


def _place_ln(ln_g, ln_b, me):
    blk = ln_g.shape[1]

    def body(me_ref, g_ref, b_ref, o_ref):
        o_ref[...] = jnp.zeros_like(o_ref)
        o_ref[0:1, :] = g_ref[...]
        o_ref[1:2, :] = b_ref[...]

    vec = pl.BlockSpec((1, blk), lambda i, me_ref: (0, 0))
    return pl.pallas_call(
        body,
        name="place_ln",
        grid_spec=pltpu.PrefetchScalarGridSpec(
            num_scalar_prefetch=1, grid=(1,), in_specs=[vec, vec], out_specs=_own_slot((8, blk))),
        out_shape=jax.ShapeDtypeStruct((N_DEV, 8, blk), F32),
        compiler_params=_cp("arbitrary"),
    )(me, ln_g, ln_b)


def _place_slab(a, me, *, name):
    def body(me_ref, a_ref, o_ref):
        o_ref[...] = a_ref[...]

    return pl.pallas_call(
        body,
        name=name,
        grid_spec=pltpu.PrefetchScalarGridSpec(
            num_scalar_prefetch=1, grid=(1,),
            in_specs=[pl.BlockSpec(a.shape, lambda i, me_ref: (0, 0))], out_specs=_own_slot(a.shape)),
        out_shape=jax.ShapeDtypeStruct((N_DEV,) + a.shape, a.dtype),
        compiler_params=_cp("arbitrary"),
    )(me, a)


def _place_own(grads, me, *, name):
    k = len(grads)

    def body(me_ref, *refs):
        for src, dst in zip(refs[:k], refs[k:]):
            dst[...] = src[...]

    specs = [_own_slot(g.shape[1:]) for g in grads]
    return pl.pallas_call(
        body,
        name=name,
        grid_spec=pltpu.PrefetchScalarGridSpec(num_scalar_prefetch=1, grid=(1,), in_specs=specs, out_specs=specs),
        out_shape=[jax.ShapeDtypeStruct(g.shape, g.dtype) for g in grads],
        compiler_params=_cp("arbitrary"),
    )(me, *grads)


def _mesh_pos():
    x, y, c = (lax.axis_index(a) for a in MESH_AXES)
    return x, y, c, 4 * x + 2 * y + c


def _peer(x, y, c, r):
    px = 1 - x if r & 4 else x
    py = 1 - y if r & 2 else y
    pc = 1 - c if r & 1 else c
    return (px, py, pc), 4 * px + 2 * py + pc


RELATIONS = {"scatter": (1, 2, 3, 4, 5, 6, 7), "gather_all": (1, 2, 3, 4, 5, 6, 7), "gather_chips": (1, 2, 4, 6),
             "gather_sibling": (2, 4, 6)}


def _peer_copies(srcs, lands, send_sems, recv_sems, mode, waits):
    x, y, c, me = _mesh_pos()
    rel = RELATIONS[mode]
    pairs = []
    for ri, r in enumerate(rel):
        if mode == "gather_sibling":
            peer, _ = _peer(x, y, c, 1)
            _, sent_blk = _peer(x, y, c, r)
            _, got_blk = _peer(x, y, c, r ^ 1)
        else:
            peer, peer_blk = _peer(x, y, c, r)
            sent_blk, got_blk = (peer_blk if mode == "scatter" else me), peer_blk
        for k, (src, land) in enumerate(zip(srcs, lands)):
            idx = k * len(rel) + ri
            sems = dict(send_sem=send_sems.at[idx], recv_sem=recv_sems.at[idx], device_id=peer,
                        device_id_type=pl.DeviceIdType.MESH)
            dst_blk = sent_blk if mode == "gather_sibling" else me
            mine = pltpu.make_async_remote_copy(src_ref=src.at[sent_blk], dst_ref=land.at[dst_blk], **sems)
            theirs = pltpu.make_async_remote_copy(src_ref=src.at[sent_blk], dst_ref=land.at[got_blk], **sems) if waits else None
            pairs.append((mine, theirs))
    return pairs


DATAFLOW = pltpu.SideEffectType.DATAFLOW_SIDE_EFFECTING


def _in_hbm(a):
    return pltpu.with_memory_space_constraint(a, pltpu.HBM)


def _copies_start(srcs, lands, *, mode, name, deps=()):
    gather = mode != "scatter"
    arrs = list(lands) if gather else list(srcs) + list(lands)
    n, k, nd = len(arrs), len(lands), len(deps)

    def body(*refs):
        ins, send_sems, recv_sems, token = refs[:n], refs[n + nd], refs[n + nd + 1], refs[2 * n + nd + 2]
        src_refs, land_refs = (ins, ins) if gather else (ins[:k], ins[k:])
        for mine, _ in _peer_copies(src_refs, land_refs, send_sems, recv_sems, mode, waits=False):
            mine.start()
        token[...] = jnp.zeros_like(token)

    n_cp = k * len(RELATIONS[mode])
    return pl.pallas_call(
        body,
        name=name,
        in_specs=[HBM_SPEC] * n + [ANY_SPEC] * nd,
        out_specs=(SEM_SPEC, SEM_SPEC, *[HBM_SPEC] * n, pl.BlockSpec(memory_space=pltpu.VMEM)),
        out_shape=(pltpu.SemaphoreType.DMA((n_cp,)), pltpu.SemaphoreType.DMA((n_cp,)),
                   *[pltpu.HBM(a.shape, a.dtype) for a in arrs], jax.ShapeDtypeStruct((8, 128), F32)),
        input_output_aliases={i: 2 + i for i in range(n)},
        compiler_params=pltpu.CompilerParams(has_side_effects=DATAFLOW),
    )(*[_in_hbm(a) for a in arrs], *deps)


def _copies_wait(arrs, send_sems, recv_sems, after, *, n_lands, mode, name):
    n, k = len(arrs), n_lands
    gather = mode != "scatter"

    def body(*refs):
        ins, send_sems, recv_sems = refs[:n], refs[n], refs[n + 1]
        src_refs, land_refs = (ins, ins) if gather else (ins[:k], ins[k:])
        for mine, theirs in _peer_copies(src_refs, land_refs, send_sems, recv_sems, mode, waits=True):
            mine.wait_send()
            theirs.wait_recv()

    outs = pl.pallas_call(
        body,
        name=name,
        in_specs=[HBM_SPEC] * n + [SEM_SPEC, SEM_SPEC] + [ANY_SPEC] * len(after),
        out_specs=[HBM_SPEC] * n,
        out_shape=[pltpu.HBM(a.shape, a.dtype) for a in arrs],
        input_output_aliases={i: i for i in range(n)},
        compiler_params=pltpu.CompilerParams(has_side_effects=DATAFLOW),
    )(*arrs, send_sems, recv_sems, *after)
    return outs[n - k:]


def _adamw(w, g, m, v):
    m = ADAM_B1 * m + (1.0 - ADAM_B1) * g
    v = ADAM_B2 * v + (1.0 - ADAM_B2) * (g * g)
    m_hat = m / (1.0 - ADAM_B1 ** ADAM_STEP)
    v_hat = v / (1.0 - ADAM_B2 ** ADAM_STEP)
    return -ADAM_LR * (m_hat / (jnp.sqrt(v_hat) + ADAM_EPS) + ADAM_WD * w), m, v


ADAM_TC = 256


def _adam_big(slots, w, m, v, cuts_columns, *, name):
    layers, n, nj = len(slots), slots[0].shape[1], D_MODEL // ADAM_TC

    def body(*refs):
        s_refs = refs[:layers]
        w_ref, m_ref, v_ref, g_ref, d_ref, nm_ref, nv_ref, acc_ref = refs[layers:]
        for ll in range(layers):
            @pl.when(pl.program_id(0) == ll)
            def _(s_ref=s_refs[ll]):
                g = s_ref[0].astype(F32)
                for s in range(1, N_DEV):
                    g = g + s_ref[s].astype(F32)
                acc_ref[...] = g

        g = acc_ref[...].T if cuts_columns else acc_ref[...]
        g_ref[...] = g
        d_ref[...], nm_ref[...], nv_ref[...] = _adamw(w_ref[...], g, m_ref[...], v_ref[...])

    def slot_spec(ll):
        return pl.BlockSpec((N_DEV, n, ADAM_TC),
                            lambda l, j: (0, 0, jnp.where(l < ll, 0, jnp.where(l > ll, nj - 1, j))))

    if cuts_columns:
        w_spec = pl.BlockSpec((None, ADAM_TC, n), lambda l, j: (l, j, 0))
    else:
        w_spec = pl.BlockSpec((None, n, ADAM_TC), lambda l, j: (l, 0, j))
    return pl.pallas_call(
        body,
        name=name,
        grid=(layers, nj),
        in_specs=[slot_spec(ll) for ll in range(layers)] + [w_spec] * 3,
        out_specs=[w_spec] * 4,
        out_shape=[jax.ShapeDtypeStruct(w.shape, F32)] * 4,
        scratch_shapes=[pltpu.VMEM((n, ADAM_TC), F32)],
        compiler_params=_cp("arbitrary", "arbitrary"),
    )(*slots, w, m, v)


def _adam_slabs(slots, ws, ms, vs):
    n = len(slots)

    def body(*refs):
        ins, outs = refs[:4 * n], refs[4 * n:]
        for k in range(n):
            s_ref, w_ref, m_ref, v_ref = ins[k], ins[n + k], ins[2 * n + k], ins[3 * n + k]
            g = s_ref[0]
            for s in range(1, N_DEV):
                g = g + s_ref[s]
            outs[4 * k][...] = g
            outs[4 * k + 1][...], outs[4 * k + 2][...], outs[4 * k + 3][...] = _adamw(w_ref[...], g, m_ref[...], v_ref[...])

    res = pl.pallas_call(
        body,
        name="small_adamw",
        out_shape=[jax.ShapeDtypeStruct(w.shape, F32) for w in ws for _ in range(4)],
        compiler_params=pltpu.CompilerParams(vmem_limit_bytes=VMEM_LIMIT_BYTES),
    )(*slots, *ws, *ms, *vs)
    return [res[4 * k:4 * k + 4] for k in range(n)]


def _adam_vecs(gs, ws, ms, vs):
    n = len(gs)

    def body(*refs):
        ins, outs = refs[:4 * n], refs[4 * n:]
        for k in range(n):
            outs[3 * k][...], outs[3 * k + 1][...], outs[3 * k + 2][...] = _adamw(
                ins[n + k][...], ins[k][...], ins[2 * n + k][...], ins[3 * n + k][...])

    res = pl.pallas_call(
        body,
        name="ln_adamw",
        out_shape=[jax.ShapeDtypeStruct(w.shape, F32) for w in ws for _ in range(3)],
        compiler_params=pltpu.CompilerParams(vmem_limit_bytes=VMEM_LIMIT_BYTES),
    )(*gs, *ws, *ms, *vs)
    return [res[3 * k:3 * k + 3] for k in range(n)]


SLAB_AT = dict(mem_norm=0, lb_logits=1, ffn1_norm=4, mix_norm=6, hgrn_gnorm=8, gmlp_ln_g=9, gmlp_ln_b=11,
               gmlp_b_s=13, ffn2_norm=14, final_norm=16)
SLAB_ROWS = 24
SMALL_SHARDED = ("gmlp_ln_g", "gmlp_ln_b")


def _pack_slab(parts, *, name, deps=()):
    flat, plan = [], []
    for pname, at in SLAB_AT.items():
        for a in parts.get(pname, ()):
            flat.append(a)
            plan.append((at, a.shape))
            at += max(1, a.shape[0] * a.shape[1] // D_MODEL)

    def body(*refs):
        o_ref = refs[-1]
        o_ref[...] = jnp.zeros_like(o_ref)
        for ref, (at, (r, w)) in zip(refs, plan):
            if w == D_MODEL or r == 1 and w < D_MODEL:
                o_ref[at:at + r, 0:w] = ref[...]
            elif w < D_MODEL:
                for j in range(r):
                    o_ref[at:at + 1, j * w:(j + 1) * w] = ref[j:j + 1, :]
            else:
                for j in range(w // D_MODEL):
                    o_ref[at + j:at + j + 1, :] = ref[:, j * D_MODEL:(j + 1) * D_MODEL]

    return pl.pallas_call(
        body,
        name=name,
        in_specs=[pl.BlockSpec(memory_space=pltpu.VMEM)] * len(flat) + [ANY_SPEC] * len(deps),
        out_shape=jax.ShapeDtypeStruct((SLAB_ROWS, D_MODEL), F32),
        compiler_params=pltpu.CompilerParams(vmem_limit_bytes=VMEM_LIMIT_BYTES),
    )(*flat, *deps)


def _unpack_slab(slab, shapes):
    out = {}
    for pname, at in SLAB_AT.items():
        if pname in SMALL_SHARDED:
            continue
        size = math.prod(shapes[pname])
        rows = max(1, size // D_MODEL)
        out[pname] = slab[at:at + rows].reshape(-1)[:size].reshape(shapes[pname])
    return out


def _ffn_fwd(x, norm_g, block, layer, full, get_weights):
    tag = f"l{layer}_{block}"
    full.update(get_weights((layer, f"{block}_in"), (x,)))
    h, z, act = _norm_mm(x, norm_g, full[(f"{block}_w_in", layer)], swiglu=True, tm=512, tn=1408, deps=full.pop("deps", ()),
                         name=f"{tag}_in")
    full.update(get_weights((layer, f"{block}_out"), (act,)))
    y = _mm(act, full[(f"{block}_w_out", layer)], tm=512, tn=D_MODEL, tk=D_FF, out_dtype=F32, res=x, scale=0.5,
            deps=full.pop("deps", ()), name=f"{tag}_out")
    return y, (x, h, z, act)


def _ffn_bwd(dy, dy16, saved, norm_g, w_in_t, w_out, tag, deps=(), after_out_wgrad=None, before_in_wgrad=None):
    x, h, z, act = saved
    dw_out = _mm(act, dy16, ta=True, tm=1408, tn=D_MODEL, tk=N_TOK, out_dtype=BF16, scale=0.5, deps=deps,
                 name=f"{tag}_out_wgrad")
    sent = after_out_wgrad(dw_out) if after_out_wgrad is not None else ()
    dz = _swiglu_dgrad(dy16, w_out, z, scale=0.5, deps=sent, name=f"{tag}_out_dgrad")
    if before_in_wgrad is None:
        dw_in_t = _planes_wgrad(dz, h, name=f"{tag}_in_wgrad")
        dx, dx16, dg = _dgrad_norm_bwd(dz, w_in_t, x, norm_g, dy, name=f"{tag}_in_dgrad")
    else:
        dx, dx16, dg = _dgrad_norm_bwd(dz, w_in_t, x, norm_g, dy, name=f"{tag}_in_dgrad")
        dw_in_t = _planes_wgrad(dz, h, deps=before_in_wgrad(dg), name=f"{tag}_in_wgrad")
    return dx, dx16, dg, dw_in_t, dw_out


def kernel(x, mem, mem_norm, lb_logits, ffn1_norm, ffn1_w_in, ffn1_w_out, mix_norm, mem_w_kv, hgrn_w_in, hgrn_gnorm, hgrn_w_out, gmlp_w_in, gmlp_ln_g, gmlp_ln_b, gmlp_w_s, gmlp_b_s, gmlp_w_out, ffn2_norm, ffn2_w_in, ffn2_w_out, final_norm, loss_target, m_mem_norm, m_lb_logits, m_ffn1_norm, m_ffn1_w_in, m_ffn1_w_out, m_mix_norm, m_mem_w_kv, m_hgrn_w_in, m_hgrn_gnorm, m_hgrn_w_out, m_gmlp_w_in, m_gmlp_ln_g, m_gmlp_ln_b, m_gmlp_w_s, m_gmlp_b_s, m_gmlp_w_out, m_ffn2_norm, m_ffn2_w_in, m_ffn2_w_out, m_final_norm, v_mem_norm, v_lb_logits, v_ffn1_norm, v_ffn1_w_in, v_ffn1_w_out, v_mix_norm, v_mem_w_kv, v_hgrn_w_in, v_hgrn_gnorm, v_hgrn_w_out, v_gmlp_w_in, v_gmlp_ln_g, v_gmlp_ln_b, v_gmlp_w_s, v_gmlp_b_s, v_gmlp_w_out, v_ffn2_norm, v_ffn2_w_in, v_ffn2_w_out, v_final_norm):
    weights = dict(mem_norm=mem_norm, lb_logits=lb_logits, ffn1_norm=ffn1_norm, ffn1_w_in=ffn1_w_in, ffn1_w_out=ffn1_w_out, mix_norm=mix_norm, mem_w_kv=mem_w_kv, hgrn_w_in=hgrn_w_in, hgrn_gnorm=hgrn_gnorm, hgrn_w_out=hgrn_w_out, gmlp_w_in=gmlp_w_in, gmlp_ln_g=gmlp_ln_g, gmlp_ln_b=gmlp_ln_b, gmlp_w_s=gmlp_w_s, gmlp_b_s=gmlp_b_s, gmlp_w_out=gmlp_w_out, ffn2_norm=ffn2_norm, ffn2_w_in=ffn2_w_in, ffn2_w_out=ffn2_w_out, final_norm=final_norm)
    mom_m = dict(mem_norm=m_mem_norm, lb_logits=m_lb_logits, ffn1_norm=m_ffn1_norm, ffn1_w_in=m_ffn1_w_in, ffn1_w_out=m_ffn1_w_out, mix_norm=m_mix_norm, mem_w_kv=m_mem_w_kv, hgrn_w_in=m_hgrn_w_in, hgrn_gnorm=m_hgrn_gnorm, hgrn_w_out=m_hgrn_w_out, gmlp_w_in=m_gmlp_w_in, gmlp_ln_g=m_gmlp_ln_g, gmlp_ln_b=m_gmlp_ln_b, gmlp_w_s=m_gmlp_w_s, gmlp_b_s=m_gmlp_b_s, gmlp_w_out=m_gmlp_w_out, ffn2_norm=m_ffn2_norm, ffn2_w_in=m_ffn2_w_in, ffn2_w_out=m_ffn2_w_out, final_norm=m_final_norm)
    mom_v = dict(mem_norm=v_mem_norm, lb_logits=v_lb_logits, ffn1_norm=v_ffn1_norm, ffn1_w_in=v_ffn1_w_in, ffn1_w_out=v_ffn1_w_out, mix_norm=v_mix_norm, mem_w_kv=v_mem_w_kv, hgrn_w_in=v_hgrn_w_in, hgrn_gnorm=v_hgrn_gnorm, hgrn_w_out=v_hgrn_w_out, gmlp_w_in=v_gmlp_w_in, gmlp_ln_g=v_gmlp_ln_g, gmlp_ln_b=v_gmlp_ln_b, gmlp_w_s=v_gmlp_w_s, gmlp_b_s=v_gmlp_b_s, gmlp_w_out=v_gmlp_w_out, ffn2_norm=v_ffn2_norm, ffn2_w_in=v_ffn2_w_in, ffn2_w_out=v_ffn2_w_out, final_norm=v_final_norm)
    order = list(weights)
    _, _, _, me = _mesh_pos()
    me_arr = jnp.reshape(me, (1,)).astype(jnp.int32)
    cuts = {name: c for name, c, _, _ in GROUPS}
    rows_already = tuple(name for name, c, _, n in GROUPS if c and n % 128)
    as_rows = lambda a: jnp.transpose(a, (0, 2, 1))
    for name in rows_already:
        weights[name], mom_m[name], mom_v[name] = as_rows(weights[name]), as_rows(mom_m[name]), as_rows(mom_v[name])
        cuts[name] = False

    mix1 = (("mem_w_kv", 1), ("gmlp_w_in", 0), ("gmlp_w_out", 0))
    gather_plan = (
        ((0, "ffn1_in"), _stage_pieces(0, "ffn1")),
        ((0, "mix_in"), _stage_pieces(0, "mix")),
        ((0, "ffn2_in"), _stage_pieces(0, "ffn2")),
        ((1, "ffn1_in"), _stage_pieces(1, "ffn1")),
        ((1, "mix_in"), mix1),
        ((1, "ffn2_in"), _stage_pieces(1, "ffn2")),
    )
    stage_of = {use: k for k, (use, _) in enumerate(gather_plan)}
    in_flight = {}

    def place(k, deps=()):
        pieces = gather_plan[k][1]
        lands = [_place_rows(weights[name], l, cuts[name], me_arr, deps=deps, name=f"place_{name}_{l}")
                 for name, l in pieces]
        if pieces is mix1:
            lands.append(_place_ln(gmlp_ln_g, gmlp_ln_b, me_arr))
        return lands

    placed = {0: place(0)}

    def start_chips(k, deps):
        lands = placed[k]
        send_sems, recv_sems, *thru, token = _copies_start(lands, lands, mode="gather_chips", deps=deps,
                                                           name=f"gather{k}_chips_start")
        in_flight[k] = (thru, send_sems, recv_sems)
        return token

    def pass_to_sibling(k, after):
        thru, send_sems, recv_sems = in_flight[k]
        outs = _copies_wait(thru, send_sems, recv_sems, after, n_lands=len(thru), mode="gather_chips",
                            name=f"gather{k}_chips_wait")
        send_sems, recv_sems, *thru, token = _copies_start(outs, outs, mode="gather_sibling",
                                                           name=f"gather{k}_sibling_start")
        in_flight[k] = (thru, send_sems, recv_sems)
        return token, token

    first_sent = start_chips(0, ())
    placed.update({k: place(k, (first_sent,)) for k in range(1, len(gather_plan))})
    placed_later = tuple(a for k in range(1, len(gather_plan)) for a in placed[k])
    points = [(i, p) for i in (0, 1) for p in ("ffn1_in", "ffn1_out", "mix_in", "mix_out", "ffn2_in", "ffn2_out")]
    pass_at = {j: points[points.index(use) - 1] for j, (use, _) in enumerate(gather_plan) if j}
    pass_at[1] = gather_plan[1][0]

    started = {0}

    def get_weights(use, after):
        tokens, w = [], {}
        k = stage_of.get(use)

        def pass_on(j, after):
            token, landed = pass_to_sibling(j, after)
            tokens.append(token)
            if j + 1 < len(gather_plan) and j + 1 not in started:
                started.add(j + 1)
                tokens.append(start_chips(j + 1, (landed,)))

        if k == 0:
            pass_on(0, tuple(after) + placed_later)
        elif k is not None and pass_at[k] == use:
            pass_on(k, after)
        if k is not None:
            thru, send_sems, recv_sems = in_flight[k]
            outs = _copies_wait(thru, send_sems, recv_sems, after, n_lands=len(thru), mode="gather_sibling",
                                name=f"gather{k}_sibling_wait")
            after = (outs[0],)
            pieces = gather_plan[k][1]
            w = {p: o.reshape(N_DEV * o.shape[1], D_MODEL) for p, o in zip(pieces, outs)}
            if pieces is mix1:
                w["ln_g"] = outs[-1][:, 0, :].reshape(1, GM_WIDTH)
                w["ln_b"] = outs[-1][:, 1, :].reshape(1, GM_WIDTH)
        for j, at in pass_at.items():
            if at == use and j != k:
                pass_on(j, after)
        w["deps"] = tuple(tokens)
        return w

    scatter = {}

    def put_grads(st, grads):
        if st in ("w_s", "small"):
            slab = grads.reshape(GM_GROUPS * GM_CHUNK, GM_CHUNK) if st == "w_s" else _pack_slab(grads, name="pack_small_grads")
            land = _place_slab(slab, me_arr, name=f"{st}_place")
            send_sems, recv_sems, *thru, token = _copies_start([land], [land], mode="gather_all", name=f"{st}_start")
            scatter[st] = (thru, send_sems, recv_sems)
            return (token,)
        views = [g.reshape(N_DEV, -1, D_MODEL) for g in grads.values()]
        recv = _place_own(views, me_arr, name=f"scatter_place_l{st[0]}_{st[1]}")
        send_sems, recv_sems, *thru, token = _copies_start(views, recv, mode="scatter",
                                                           name=f"scatter_start_l{st[0]}_{st[1]}")
        scatter[st] = (tuple(grads), thru, send_sems, recv_sems)
        return (token,)

    dx, loss_part, last_sent = _step_local(
        x, mem, loss_target, get_weights, put_grads, mem_norm, lb_logits, ffn1_norm, mix_norm, hgrn_gnorm,
        gmlp_w_s, gmlp_b_s, ffn2_norm, final_norm)

    slots = {}

    def wait_grads(blk, after, last=False):
        for st, entry in scatter.items():
            if isinstance(st, tuple) and st[1].startswith(blk) and (st == (0, "ffn1_in")) == last:
                pieces, thru, send_sems, recv_sems = entry
                outs = _copies_wait(thru, send_sems, recv_sems, after, n_lands=len(thru) // 2, mode="scatter",
                                    name=f"scatter_wait_l{st[0]}_{st[1]}")
                slots.update(zip(pieces, outs))

    grad, delta, new_m, new_v = {}, {}, {}, {}

    def adam_groups(names):
        for name in names:
            layers = GROUP_LAYERS[name]
            grad[name], delta[name], new_m[name], new_v[name] = _adam_big(
                [slots[(name, l)] for l in range(layers)], weights[name], mom_m[name], mom_v[name], cuts[name],
                name=f"{name}_adamw")

    wait_grads("ffn2", (dx, *last_sent))
    adam_groups(("ffn2_w_in", "ffn2_w_out"))
    wait_grads("mix", (delta["ffn2_w_out"],))
    adam_groups(("mem_w_kv", "gmlp_w_in", "gmlp_w_out", "hgrn_w_in", "hgrn_w_out"))
    wait_grads("ffn1", (delta["hgrn_w_out"],))
    adam_groups(("ffn1_w_out",))

    def small_parts(src):
        parts = {n: [src[n].reshape(-1, src[n].shape[-1])] for n in SLAB_AT if n not in SMALL_SHARDED}
        return parts

    w_s_rows = lambda a: a.reshape(GM_GROUPS * GM_CHUNK, GM_CHUNK)
    small_done = (delta["hgrn_w_out"],)
    (slab_slots,) = _copies_wait(*scatter["small"], small_done, n_lands=1, mode="gather_all", name="small_wait")
    (ws_slots,) = _copies_wait(*scatter["w_s"], small_done, n_lands=1, mode="gather_all", name="w_s_wait")
    (g_slab, d_slab, nm_slab, nv_slab), (g_ws, d_ws, nm_ws, nv_ws) = _adam_slabs(
        [slab_slots, ws_slots],
        [_pack_slab(small_parts(weights), deps=(dx,), name="pack_small_w"), w_s_rows(gmlp_w_s)],
        [_pack_slab(small_parts(mom_m), deps=(dx,), name="pack_small_m"), w_s_rows(m_gmlp_w_s)],
        [_pack_slab(small_parts(mom_v), deps=(dx,), name="pack_small_v"), w_s_rows(v_gmlp_w_s)])
    shapes = {n: weights[n].shape for n in SLAB_AT}
    for out, slab, ws in ((grad, g_slab, g_ws), (delta, d_slab, d_ws), (new_m, nm_slab, nm_ws), (new_v, nv_slab, nv_ws)):
        out.update(_unpack_slab(slab, shapes))
        out["gmlp_w_s"] = ws.reshape(gmlp_w_s.shape)
    blk = GM_WIDTH // N_DEV
    g_ln = [lax.dynamic_slice(g_slab[SLAB_AT[n]:SLAB_AT[n] + 2].reshape(1, GM_WIDTH), (0, me * blk), (1, blk))
            for n in SMALL_SHARDED]
    ln_out = _adam_vecs(g_ln, [weights[n] for n in SMALL_SHARDED], [mom_m[n] for n in SMALL_SHARDED],
                        [mom_v[n] for n in SMALL_SHARDED])
    for n, g, (d, nm, nv) in zip(SMALL_SHARDED, g_ln, ln_out):
        grad[n], delta[n], new_m[n], new_v[n] = g, d, nm, nv

    wait_grads("ffn1", tuple(delta[n] for n in delta if n in GROUP_LAYERS) + (d_slab,), last=True)
    adam_groups(("ffn1_w_in",))

    for name in rows_already:
        for out in (grad, delta, new_m, new_v):
            out[name] = as_rows(out[name])
    loss = lax.psum(loss_part[0, 0], MESH_AXES)
    grad_x = dx.reshape(B_LOC, SEQ, D_MODEL)
    return (loss, grad_x, *[grad[n] for n in order], *[delta[n] for n in order],
            *[new_m[n] for n in order], *[new_v[n] for n in order])


def _step_local(x, mem, loss_target, get_weights, put_grads, mem_norm, lb_logits, ffn1_norm, mix_norm, hgrn_gnorm,
                gmlp_w_s, gmlp_b_s, ffn2_norm, final_norm):
    w_s = gmlp_w_s[0]
    b_st = gmlp_b_s[0].T

    xs = x.reshape(N_TOK, D_MODEL)
    mem2d = mem.reshape(B_LOC * MEM_LEN, D_MODEL)
    mem_g = mem_norm.reshape(1, D_MODEL)
    saved, full = [], {}
    for i in range(2):
        xs, s_ffn1 = _ffn_fwd(xs, ffn1_norm[i:i + 1], "ffn1", i, full, get_weights)
        if i == 0:
            memn = _rms_fwd(mem2d, mem_g, deps=(xs,), name="mem_norm_fwd")
        full.update(get_weights((i, "mix_in"), (xs,)))
        mixer = "hgrn" if i == 0 else "gmlp"
        hm, zm = _norm_mm(xs, mix_norm[i:i + 1], full[(f"{mixer}_w_in", 0)], swiglu=False, tm=1024, tn=1280, deps=full.pop("deps", ()),
                          name=f"l{i}_mix_in")
        kv = _mm(memn, full[("mem_w_kv", i)], tb=True, tm=512, tn=512, tk=D_MODEL, out_dtype=F32, name=f"l{i}_mem_kv")
        o_mem = _attn_fwd(zm, kv, name=f"l{i}_attn")
        if i == 0:
            cat, o_pre, s_all = _hgrn_fwd(zm, o_mem, lb_logits, hgrn_gnorm)
            mix_saved = (o_pre, s_all)
        else:
            cat = _gmlp_fwd(zm, o_mem, full["ln_g"], full["ln_b"], w_s, b_st)
            mix_saved = ()
        x_mix = xs
        full.update(get_weights((i, "mix_out"), (cat,)))
        xs = _mm(cat, full[(f"{mixer}_w_out", 0)], tm=512, tn=D_MODEL, tk=cat.shape[1], out_dtype=F32, res=xs,
                 deps=full.pop("deps", ()), name=f"l{i}_mix_out")
        xs, s_ffn2 = _ffn_fwd(xs, ffn2_norm[i:i + 1], "ffn2", i, full, get_weights)
        saved.append((s_ffn1, (x_mix, hm, kv, zm, cat, mix_saved), s_ffn2))

    dx, dx16, d_final, loss_part = _loss_head(xs, final_norm.reshape(1, D_MODEL), loss_target.reshape(N_TOK, D_MODEL))

    small = {"final_norm": [d_final]}
    d_ffn1, d_ffn2, d_mix = [None, None], [None, None], [None, None]
    dmemn = jnp.zeros((B_LOC * MEM_LEN, D_MODEL), F32)
    deps = ()
    for i in (1, 0):
        s_ffn1, (x_mix, hm, kv, zm, cat, mix_saved), s_ffn2 = saved[i]
        dx, dx16, d_ffn2[i], dw_in_t, dw_out = _ffn_bwd(
            dx, dx16, s_ffn2, ffn2_norm[i:i + 1], full[("ffn2_w_in", i)], full[("ffn2_w_out", i)], f"l{i}_ffn2", deps)
        deps = put_grads((i, "ffn2"), {("ffn2_w_in", i): dw_in_t, ("ffn2_w_out", i): dw_out})
        mixer = "hgrn" if i == 0 else "gmlp"
        w_in_t, w_out = full[(f"{mixer}_w_in", 0)], full[(f"{mixer}_w_out", 0)]
        width = cat.shape[1]
        g_mix = {}
        g_mix[(f"{mixer}_w_out", 0)] = _mm(cat, dx16, ta=True, tm=1024, tn=D_MODEL, tk=N_TOK, out_dtype=BF16,
                                           deps=deps, name=f"l{i}_mix_out_wgrad")
        dcat = _mm(dx16, w_out, tb=True, tm=1024, tn=width // 2, tk=D_MODEL, out_dtype=F32, name=f"l{i}_mix_out_dgrad")
        dq, dk, dv = _attn_bwd(zm, kv, dcat, do_off=width - XA_HEADS * XA_DIM, name=f"l{i}_attn_bwd")
        if i == 0:
            dzm, dlbl, dgn = _hgrn_bwd(zm, mix_saved[0], dcat, dq, mix_saved[1], lb_logits, hgrn_gnorm)
            small["lb_logits"], small["hgrn_gnorm"] = [dlbl], [dgn]
            deps = ()
        else:
            dzm, dws, dbt, dlng, dlnb = _gmlp_bwd(zm, dcat, dq, full["ln_g"], full["ln_b"], w_s, b_st)
            small["gmlp_b_s"], small["gmlp_ln_g"], small["gmlp_ln_b"] = [dbt.T], [dlng], [dlnb]
            deps = put_grads("w_s", dws)
        g_mix[(f"{mixer}_w_in", 0)] = _mm(dzm, hm, ta=True, tm=1024, tn=D_MODEL, tk=N_TOK, out_dtype=BF16, deps=deps,
                                          name=f"l{i}_mix_in_wgrad")
        dkv = jnp.concatenate([dk, dv], axis=1)
        g_mix[("mem_w_kv", i)] = _mm(dkv, memn, ta=True, tm=512, tn=D_MODEL, tk=B_LOC * MEM_LEN, out_dtype=BF16,
                                     name=f"l{i}_mem_kv_wgrad")
        deps = put_grads((i, "mix"), g_mix)
        dx, dx16, d_mix[i] = _dgrad_norm_bwd(dzm, w_in_t, x_mix, mix_norm[i:i + 1], dx, deps=deps,
                                             name=f"l{i}_mix_in_dgrad")
        dmemn = _mm(dkv, full[("mem_w_kv", i)], tm=B_LOC * MEM_LEN, tn=D_MODEL, tk=512, out_dtype=F32, res=dmemn,
                    name=f"l{i}_mem_kv_dgrad")
        def send_small(dg, i=i, dmemn=dmemn):
            d_ffn1[i] = dg
            _, _, dmem_g = _rms_bwd(mem2d, mem_g, dmemn, dmemn, name="mem_norm_bwd")
            small.update(mem_norm=[dmem_g], ffn1_norm=d_ffn1, ffn2_norm=d_ffn2, mix_norm=d_mix)
            return put_grads("small", small)

        if i == 0:
            send_out = lambda dw_out: put_grads((0, "ffn1_out"), {("ffn1_w_out", 0): dw_out})
            dx, dx16, d_ffn1[i], dw_in_t, _ = _ffn_bwd(
                dx, dx16, s_ffn1, ffn1_norm[i:i + 1], full[("ffn1_w_in", i)], full[("ffn1_w_out", i)], f"l{i}_ffn1",
                after_out_wgrad=send_out, before_in_wgrad=send_small)
            deps = put_grads((0, "ffn1_in"), {("ffn1_w_in", 0): dw_in_t})
        else:
            dx, dx16, d_ffn1[i], dw_in_t, dw_out = _ffn_bwd(
                dx, dx16, s_ffn1, ffn1_norm[i:i + 1], full[("ffn1_w_in", i)], full[("ffn1_w_out", i)], f"l{i}_ffn1")
            deps = put_grads((i, "ffn1"), {("ffn1_w_in", i): dw_in_t, ("ffn1_w_out", i): dw_out})
    return dx, loss_part, deps
```

```python
import functools
import math

import jax
import jax.numpy as jnp
from jax import lax
from jax.experimental import pallas as pl
from jax.experimental.pallas import tpu as pltpu

F32 = jnp.float32
BF16 = jnp.bfloat16

D_MODEL = 1024
SEQ = 2048
B_LOC = 2
N_TOK = B_LOC * SEQ
MEM_LEN = 256
N_DEV = 8
EPS = 1e-6
D_FF = 2816
HG_HEADS = 8
HG_DIM = 128
HG_CHUNK = 64
HG_NCHUNK = SEQ // HG_CHUNK
GM_CHUNK = 128
GM_GROUPS = 8
GM_WIDTH = 2048
GM_GDIM = GM_WIDTH // GM_GROUPS
XA_HEADS = 4
XA_DIM = 256
XA_OFF = 4096

ADAM_LR = 0.001
ADAM_B1 = 0.9
ADAM_B2 = 0.999
ADAM_EPS = 1e-08
ADAM_WD = 0.01
ADAM_STEP = 10

VMEM_LIMIT_BYTES = 56 * 1024 * 1024
MESH_AXES = ("x", "y", "c")

GROUPS = (
    ("ffn1_w_in", True, 2, 704),
    ("ffn1_w_out", False, 2, 352),
    ("mem_w_kv", True, 2, 256),
    ("hgrn_w_in", True, 1, 640),
    ("hgrn_w_out", False, 1, 256),
    ("gmlp_w_in", True, 1, 640),
    ("gmlp_w_out", False, 1, 384),
    ("ffn2_w_in", True, 2, 704),
    ("ffn2_w_out", False, 2, 352),
)
GROUP_LAYERS = {name: layers for name, _, layers, _ in GROUPS}


def _stage_pieces(layer, block):
    if block == "mix":
        mixer = "hgrn" if layer == 0 else "gmlp"
        return (("mem_w_kv", layer), (f"{mixer}_w_in", 0), (f"{mixer}_w_out", 0))
    return ((f"{block}_w_in", layer), (f"{block}_w_out", layer))


ANY_SPEC = pl.BlockSpec(memory_space=pl.ANY)
HBM_SPEC = pl.BlockSpec(memory_space=pltpu.HBM)
SEM_SPEC = pl.BlockSpec(memory_space=pltpu.SEMAPHORE)


def _cp(*sem):
    return pltpu.CompilerParams(dimension_semantics=sem, vmem_limit_bytes=VMEM_LIMIT_BYTES)


def _sigmoid(x):
    return 0.5 * jnp.tanh(0.5 * x) + 0.5


def _gelu_parts(x):
    cdf = 0.5 * (1.0 + lax.erf(x * (1.0 / math.sqrt(2.0))))
    pdf = jnp.exp(-0.5 * x * x) * (1.0 / math.sqrt(2.0 * math.pi))
    return x * cdf, cdf + x * pdf


def _mm(a, b, *, ta=False, tb=False, tm, tn, tk, out_dtype, res=None, scale=1.0, deps=(), name):
    m, k = (a.shape[1], a.shape[0]) if ta else a.shape
    n, kb = b.shape if tb else (b.shape[1], b.shape[0])
    assert k == kb and m % tm == 0 and n % tn == 0 and k % tk == 0, (name, a.shape, b.shape)
    nk = k // tk
    dn = (((0 if ta else 1,), (1 if tb else 0,)), ((), ()))
    n_in = 2 + (res is not None) + len(deps)

    def body(*refs):
        a_ref, b_ref = refs[:2]
        r_ref = refs[2] if res is not None else None
        o_ref, scr = refs[n_in], refs[n_in + 1:]
        p = lax.dot_general(a_ref[...].astype(BF16), b_ref[...].astype(BF16), dn, preferred_element_type=F32)

        def finish(acc):
            if scale != 1.0:
                acc = scale * acc
            if r_ref is not None:
                acc = r_ref[...] + acc
            o_ref[...] = acc.astype(out_dtype)

        if nk == 1:
            finish(p)
        else:
            acc_ref = scr[0]
            kk = pl.program_id(2)

            @pl.when(kk == 0)
            def _():
                acc_ref[...] = p

            @pl.when(kk > 0)
            def _():
                acc_ref[...] += p

            @pl.when(kk == nk - 1)
            def _():
                finish(acc_ref[...])

    a_spec = pl.BlockSpec((tk, tm), lambda i, j, kk: (kk, i)) if ta else pl.BlockSpec((tm, tk), lambda i, j, kk: (i, kk))
    b_mode = dict(pipeline_mode=pl.Buffered(1)) if n == tn and nk == 1 else {}
    if tb:
        b_spec = pl.BlockSpec((tn, tk), lambda i, j, kk: (j, kk), **b_mode)
    else:
        b_spec = pl.BlockSpec((tk, tn), lambda i, j, kk: (kk, j), **b_mode)
    o_spec = pl.BlockSpec((tm, tn), lambda i, j, kk: (i, j))
    in_specs = [a_spec, b_spec] + ([o_spec] if res is not None else []) + [ANY_SPEC] * len(deps)
    args = (a, b) + ((res,) if res is not None else ()) + tuple(deps)
    return pl.pallas_call(
        body,
        name=name,
        grid=(m // tm, n // tn, nk),
        in_specs=in_specs,
        out_specs=o_spec,
        out_shape=jax.ShapeDtypeStruct((m, n), out_dtype),
        scratch_shapes=[pltpu.VMEM((tm, tn), F32)] if nk > 1 else [],
        compiler_params=_cp("parallel", "parallel", "arbitrary"),
    )(*args)


def _rms_fwd(x, g, *, name, deps=(), tm=512):
    rows = x.shape[0]

    def body(x_ref, g_ref, *rest):
        o_ref = rest[len(deps)]
        xv = x_ref[...]
        r = lax.rsqrt(jnp.mean(xv * xv, axis=-1, keepdims=True) + EPS)
        o_ref[...] = (xv * r * g_ref[...]).astype(BF16)

    row = pl.BlockSpec((tm, D_MODEL), lambda i: (i, 0))
    return pl.pallas_call(
        body,
        name=name,
        grid=(rows // tm,),
        in_specs=[row, pl.BlockSpec((1, D_MODEL), lambda i: (0, 0))] + [ANY_SPEC] * len(deps),
        out_specs=row,
        out_shape=jax.ShapeDtypeStruct((rows, D_MODEL), BF16),
        compiler_params=_cp("parallel"),
    )(x, g, *deps)


def _rms_bwd(x, g, dh, dres, *, name, deps=(), tm=512):
    rows = x.shape[0]

    def body(x_ref, g_ref, dh_ref, dres_ref, *rest):
        dx_ref, dx16_ref, dg_ref = rest[len(deps):]
        xv = x_ref[...]
        r = lax.rsqrt(jnp.mean(xv * xv, axis=-1, keepdims=True) + EPS)
        xhat = xv * r
        dhv = dh_ref[...]
        part = jnp.sum(dhv * xhat, axis=0, keepdims=True)

        @pl.when(pl.program_id(0) == 0)
        def _():
            dg_ref[...] = part

        @pl.when(pl.program_id(0) > 0)
        def _():
            dg_ref[...] += part

        dxh = dhv * g_ref[...]
        dx = dres_ref[...] + r * (dxh - xhat * jnp.mean(dxh * xhat, axis=-1, keepdims=True))
        dx_ref[...] = dx
        dx16_ref[...] = dx.astype(BF16)

    row = pl.BlockSpec((tm, D_MODEL), lambda i: (i, 0))
    vec = pl.BlockSpec((1, D_MODEL), lambda i: (0, 0))
    return pl.pallas_call(
        body,
        name=name,
        grid=(rows // tm,),
        in_specs=[row, vec, row, row] + [ANY_SPEC] * len(deps),
        out_specs=[row, row, vec],
        out_shape=[jax.ShapeDtypeStruct((rows, D_MODEL), F32), jax.ShapeDtypeStruct((rows, D_MODEL), BF16),
                   jax.ShapeDtypeStruct((1, D_MODEL), F32)],
        compiler_params=_cp("arbitrary"),
    )(x, g, dh, dres, *deps)


_NT = (((1,), (1,)), ((), ()))
_TN = (((0,), (0,)), ((), ()))


def _norm_mm(x, g, w_t, *, swiglu, name, tm, tn, deps=()):
    rows = w_t.shape[0]
    half = rows // 2
    nj = (half if swiglu else rows) // tn
    nd = len(deps)

    def body(x_ref, g_ref, w_ref, *rest):
        outs = rest[nd:]
        h_ref, z_ref = outs[:2]

        def norm():
            xv = x_ref[...]
            r = lax.rsqrt(jnp.mean(xv * xv, axis=-1, keepdims=True) + EPS)
            h_ref[...] = (xv * r * g_ref[...]).astype(BF16)

        if swiglu:
            norm()
            h = h_ref[...]
            for j in range(nj):
                cols = slice(j * tn, (j + 1) * tn)
                gate = lax.dot_general(h, w_ref[j * tn:(j + 1) * tn, :], _NT, preferred_element_type=F32)
                up = lax.dot_general(h, w_ref[half + j * tn:half + (j + 1) * tn, :], _NT, preferred_element_type=F32)
                s = _sigmoid(gate)
                silu = gate * s
                z_ref[0, :, cols] = (up * (s + silu * (1.0 - s))).astype(BF16)
                z_ref[1, :, cols] = silu.astype(BF16)
                outs[2][:, cols] = (silu * up).astype(BF16)
        else:
            j = pl.program_id(1)
            pl.when(j == 0)(norm)
            w = w_ref[pl.ds(pl.multiple_of(j * tn, tn), tn), :]
            z_ref[...] = lax.dot_general(h_ref[...], w, _NT, preferred_element_type=F32)

    grid = (N_TOK // tm,) if swiglu else (N_TOK // tm, nj)
    row = pl.BlockSpec((tm, D_MODEL), lambda i, *_: (i, 0))
    out_specs = [row]
    out_shape = [jax.ShapeDtypeStruct((N_TOK, D_MODEL), BF16)]
    if swiglu:
        out_specs += [pl.BlockSpec((2, tm, half), lambda i: (0, i, 0)), pl.BlockSpec((tm, half), lambda i: (i, 0))]
        out_shape += [jax.ShapeDtypeStruct((2, N_TOK, half), BF16), jax.ShapeDtypeStruct((N_TOK, half), BF16)]
    else:
        out_specs.append(pl.BlockSpec((tm, tn), lambda i, j: (i, j)))
        out_shape.append(jax.ShapeDtypeStruct((N_TOK, rows), F32))
    return pl.pallas_call(
        body,
        name=name,
        grid=grid,
        in_specs=[row, pl.BlockSpec((1, D_MODEL), lambda *_: (0, 0)),
                  pl.BlockSpec((rows, D_MODEL), lambda *_: (0, 0), pipeline_mode=pl.Buffered(1))] + [ANY_SPEC] * nd,
        out_specs=out_specs,
        out_shape=out_shape,
        compiler_params=_cp(*(("parallel",) if swiglu else ("parallel", "arbitrary"))),
    )(x, g, w_t, *deps)


def _swiglu_dgrad(dy16, w_out, z, *, scale, name, deps=(), tm=512, tn=1408):
    def body(dy_ref, w_ref, z_ref, *rest):
        dz_ref = rest[len(deps)]
        dy = dy_ref[...]
        for j in range(D_FF // tn):
            cols = slice(j * tn, (j + 1) * tn)
            da = lax.dot_general(dy, w_ref[cols, :], _NT, preferred_element_type=F32) * scale
            dz_ref[0, :, cols] = (da * z_ref[0, :, cols].astype(F32)).astype(BF16)
            dz_ref[1, :, cols] = (da * z_ref[1, :, cols].astype(F32)).astype(BF16)

    planes = pl.BlockSpec((2, tm, D_FF), lambda i: (0, i, 0))
    return pl.pallas_call(
        body,
        name=name,
        grid=(N_TOK // tm,),
        in_specs=[pl.BlockSpec((tm, D_MODEL), lambda i: (i, 0)),
                  pl.BlockSpec((D_FF, D_MODEL), lambda i: (0, 0), pipeline_mode=pl.Buffered(1)), planes]
        + [ANY_SPEC] * len(deps),
        out_specs=planes,
        out_shape=jax.ShapeDtypeStruct((2, N_TOK, D_FF), BF16),
        compiler_params=_cp("parallel"),
    )(dy16, w_out, z, *deps)


def _planes_wgrad(dz, h, *, name, deps=(), tm=1408):
    per_plane = D_FF // tm

    def body(a_ref, b_ref, *rest):
        o_ref = rest[len(deps)]
        o_ref[...] = lax.dot_general(a_ref[...], b_ref[...], _TN, preferred_element_type=F32).astype(BF16)

    return pl.pallas_call(
        body,
        name=name,
        grid=(2 * per_plane,),
        in_specs=[pl.BlockSpec((None, N_TOK, tm),
                               lambda i: (jnp.where(i < per_plane, 0, 1), 0, jnp.where(i < per_plane, i, i - per_plane))),
                  pl.BlockSpec((N_TOK, D_MODEL), lambda i: (0, 0), pipeline_mode=pl.Buffered(1))] + [ANY_SPEC] * len(deps),
        out_specs=pl.BlockSpec((tm, D_MODEL), lambda i: (i, 0)),
        out_shape=jax.ShapeDtypeStruct((2 * D_FF, D_MODEL), BF16),
        compiler_params=_cp("parallel"),
    )(dz, h, *deps)


def _dgrad_norm_bwd(dz, w_t, x, g, dres, *, name, deps=(), tm=512):
    planes = dz.ndim == 3
    rows = w_t.shape[0]
    half = rows // 2
    nd = len(deps)

    def body(a_ref, b_ref, x_ref, g_ref, dres_ref, *rest):
        dx_ref, dx16_ref, dg_ref = rest[nd:]
        if planes:
            dh = jnp.dot(a_ref[0], b_ref[:half, :], preferred_element_type=F32) + jnp.dot(
                a_ref[1], b_ref[half:, :], preferred_element_type=F32)
        else:
            dh = jnp.dot(a_ref[...], b_ref[...], preferred_element_type=F32)
        xv = x_ref[...]
        r = lax.rsqrt(jnp.mean(xv * xv, axis=-1, keepdims=True) + EPS)
        xhat = xv * r
        part = jnp.sum(dh * xhat, axis=0, keepdims=True)

        @pl.when(pl.program_id(0) == 0)
        def _():
            dg_ref[...] = part

        @pl.when(pl.program_id(0) > 0)
        def _():
            dg_ref[...] += part

        dxh = dh * g_ref[...]
        dx = dres_ref[...] + r * (dxh - xhat * jnp.mean(dxh * xhat, axis=-1, keepdims=True))
        dx_ref[...] = dx
        dx16_ref[...] = dx.astype(BF16)

    a_spec = pl.BlockSpec((2, tm, half), lambda i: (0, i, 0)) if planes else pl.BlockSpec((tm, rows), lambda i: (i, 0))
    row = pl.BlockSpec((tm, D_MODEL), lambda i: (i, 0))
    vec = pl.BlockSpec((1, D_MODEL), lambda i: (0, 0))
    return pl.pallas_call(
        body,
        name=name,
        grid=(N_TOK // tm,),
        in_specs=[a_spec, pl.BlockSpec((rows, D_MODEL), lambda i: (0, 0), pipeline_mode=pl.Buffered(1)), row, vec, row]
        + [ANY_SPEC] * nd,
        out_specs=[row, row, vec],
        out_shape=[jax.ShapeDtypeStruct((N_TOK, D_MODEL), F32), jax.ShapeDtypeStruct((N_TOK, D_MODEL), BF16),
                   jax.ShapeDtypeStruct((1, D_MODEL), F32)],
        compiler_params=_cp("arbitrary"),
    )(dz, w_t, x, g, dres, *deps)


def _loss_head(x, g, target, *, tm=512):
    def body(x_ref, g_ref, t_ref, dx_ref, dx16_ref, dg_ref, loss_ref):
        xv = x_ref[...]
        gv = g_ref[...]
        r = lax.rsqrt(jnp.mean(xv * xv, axis=-1, keepdims=True) + EPS)
        xhat = xv * r
        err = xhat * gv - t_ref[...]
        loss_part = jnp.zeros((1, 128), F32) + 0.5 * jnp.sum(jnp.mean(err * err, axis=-1, keepdims=True))
        dy = err * (1.0 / D_MODEL)
        dg_part = jnp.sum(dy * xhat, axis=0, keepdims=True)

        @pl.when(pl.program_id(0) == 0)
        def _():
            dg_ref[...] = dg_part
            loss_ref[...] = loss_part

        @pl.when(pl.program_id(0) > 0)
        def _():
            dg_ref[...] += dg_part
            loss_ref[...] += loss_part

        dxh = dy * gv
        dx = r * (dxh - xhat * jnp.mean(dxh * xhat, axis=-1, keepdims=True))
        dx_ref[...] = dx
        dx16_ref[...] = dx.astype(BF16)

    row = pl.BlockSpec((tm, D_MODEL), lambda i: (i, 0))
    vec = pl.BlockSpec((1, D_MODEL), lambda i: (0, 0))
    return pl.pallas_call(
        body,
        name="loss_head",
        grid=(N_TOK // tm,),
        in_specs=[row, vec, row],
        out_specs=[row, row, vec, pl.BlockSpec((1, 128), lambda i: (0, 0))],
        out_shape=[
            jax.ShapeDtypeStruct((N_TOK, D_MODEL), F32),
            jax.ShapeDtypeStruct((N_TOK, D_MODEL), BF16),
            jax.ShapeDtypeStruct((1, D_MODEL), F32),
            jax.ShapeDtypeStruct((1, 128), F32),
        ],
        compiler_params=_cp("arbitrary"),
    )(x, g, target)


XA_TQ = 2048
XA_SCALE = XA_DIM ** -0.5


def _attn_probs(q16, k16):
    s = lax.dot_general(q16, k16, _NT, preferred_element_type=F32) * XA_SCALE
    e = jnp.exp(s - jnp.max(s, axis=-1, keepdims=True))
    return e / jnp.sum(e, axis=-1, keepdims=True)


def _attn_fwd(z, kv, *, name):
    nt = SEQ // XA_TQ

    def body(q_ref, k_ref, v_ref, o_ref):
        p = _attn_probs(q_ref[...].astype(BF16), k_ref[...].astype(BF16))
        o_ref[...] = jnp.dot(p.astype(BF16), v_ref[...].astype(BF16), preferred_element_type=F32).astype(BF16)

    return pl.pallas_call(
        body,
        name=name,
        grid=(B_LOC, XA_HEADS, nt),
        in_specs=[
            pl.BlockSpec((XA_TQ, XA_DIM), lambda b, h, t: (b * nt + t, XA_OFF // XA_DIM + h)),
            pl.BlockSpec((MEM_LEN, XA_DIM), lambda b, h, t: (b, h)),
            pl.BlockSpec((MEM_LEN, XA_DIM), lambda b, h, t: (b, XA_HEADS + h)),
        ],
        out_specs=pl.BlockSpec((XA_TQ, XA_DIM), lambda b, h, t: (b * nt + t, h)),
        out_shape=jax.ShapeDtypeStruct((N_TOK, XA_HEADS * XA_DIM), BF16),
        compiler_params=_cp("parallel", "parallel", "arbitrary"),
    )(z, kv, kv)


def _attn_bwd(z, kv, dcat, *, do_off, name):
    nt = SEQ // XA_TQ

    def body(q_ref, k_ref, v_ref, do_ref, dq_ref, dk_ref, dv_ref):
        q16 = q_ref[...].astype(BF16)
        k16 = k_ref[...].astype(BF16)
        v16 = v_ref[...].astype(BF16)
        do16 = do_ref[...].astype(BF16)
        p = _attn_probs(q16, k16)
        dv_part = lax.dot_general(p.astype(BF16), do16, _TN, preferred_element_type=F32)
        dp = lax.dot_general(do16, v16, _NT, preferred_element_type=F32)
        ds16 = (p * (dp - jnp.sum(dp * p, axis=-1, keepdims=True)) * XA_SCALE).astype(BF16)
        dq_ref[...] = jnp.dot(ds16, k16, preferred_element_type=F32).astype(BF16)
        dk_part = lax.dot_general(ds16, q16, _TN, preferred_element_type=F32)

        @pl.when(pl.program_id(2) == 0)
        def _():
            dk_ref[...] = dk_part
            dv_ref[...] = dv_part

        @pl.when(pl.program_id(2) > 0)
        def _():
            dk_ref[...] += dk_part
            dv_ref[...] += dv_part

    qspec = pl.BlockSpec((XA_TQ, XA_DIM), lambda b, h, t: (b * nt + t, XA_OFF // XA_DIM + h))
    kspec = lambda off: pl.BlockSpec((MEM_LEN, XA_DIM), lambda b, h, t: (b, off + h))
    return pl.pallas_call(
        body,
        name=name,
        grid=(B_LOC, XA_HEADS, nt),
        in_specs=[qspec, kspec(0), kspec(XA_HEADS),
                  pl.BlockSpec((XA_TQ, XA_DIM), lambda b, h, t: (b * nt + t, do_off // XA_DIM + h))],
        out_specs=[pl.BlockSpec((XA_TQ, XA_DIM), lambda b, h, t: (b * nt + t, h)), kspec(0), kspec(0)],
        out_shape=[
            jax.ShapeDtypeStruct((N_TOK, XA_HEADS * XA_DIM), BF16),
            jax.ShapeDtypeStruct((B_LOC * MEM_LEN, XA_HEADS * XA_DIM), F32),
            jax.ShapeDtypeStruct((B_LOC * MEM_LEN, XA_HEADS * XA_DIM), F32),
        ],
        compiler_params=_cp("parallel", "parallel", "arbitrary"),
    )(z, kv, kv, dcat)


def _tril(n):
    return lax.broadcasted_iota(jnp.int32, (n, n), 0) >= lax.broadcasted_iota(jnp.int32, (n, n), 1)


def _lower_bound(lbl):
    e = jnp.exp(lbl - jnp.max(lbl, axis=0, keepdims=True))
    p = e / jnp.sum(e, axis=0, keepdims=True)
    return p[0:1, :], p


def _hgrn_gates(zq, zf, lb, tril_f):
    sig = _sigmoid(zf)
    f = lb + (1.0 - lb) * sig
    kk = 1.0 - f
    sq = _sigmoid(zq)
    q = zq * sq
    b = jnp.dot(tril_f, jnp.log(f), preferred_element_type=F32, precision=lax.Precision.HIGHEST)
    bl = b[HG_CHUNK - 1:HG_CHUNK, :]
    return q, sq, sig, f, kk, b, bl


HG_TB = 512
HG_CPB = HG_TB // HG_CHUNK
HG_NT = SEQ // HG_TB
HG_WIDTH = HG_HEADS * HG_DIM


def _head(h, section=0):
    return slice(section * HG_WIDTH + h * HG_DIM, section * HG_WIDTH + (h + 1) * HG_DIM)


def _hgrn_fwd(z, o_mem, lb_logits, gnorm):
    def body(zq_ref, zf_ref, zi_ref, zg_ref, omem_ref, lbl_ref, gn_ref, o_ref, opre_ref, sall_ref, st_ref):
        lb, _ = _lower_bound(lbl_ref[...])
        gn = gn_ref[...]
        mask = _tril(HG_CHUNK)
        tril_f = mask.astype(F32)
        o_ref[:, HG_WIDTH:] = omem_ref[...]

        @pl.when(pl.program_id(1) == 0)
        def _():
            st_ref[...] = jnp.zeros_like(st_ref)

        def chunk(c, carry):
            rows = pl.ds(pl.multiple_of(c * HG_CHUNK, HG_CHUNK), HG_CHUNK)
            q, _, _, _, kk, b, bl = _hgrn_gates(zq_ref[rows, :], zf_ref[rows, :], lb, tril_f)
            v16 = zi_ref[rows, :].astype(BF16)
            qd16 = (q * jnp.exp(b)).astype(BF16)
            ki16 = (kk * jnp.exp(-b)).astype(BF16)
            kd16 = (kk * jnp.exp(bl - b)).astype(BF16)
            ebl = jnp.exp(bl)
            zg = zg_ref[rows, :]
            gate = zg * _sigmoid(zg)
            for h in range(HG_HEADS):
                sl = _head(h)
                a = jnp.where(mask, lax.dot_general(qd16[:, sl], ki16[:, sl], _NT, preferred_element_type=F32), 0.0)
                st = st_ref[h]
                sall_ref[0, h, c] = st
                o = jnp.dot(a.astype(BF16), v16[:, sl], preferred_element_type=F32) + lax.dot_general(
                    qd16[:, sl], st.astype(BF16), _NT, preferred_element_type=F32)
                st_ref[h] = st * ebl[:, sl] + lax.dot_general(v16[:, sl], kd16[:, sl], _TN, preferred_element_type=F32)
                opre_ref[rows, sl] = o
                r = lax.rsqrt(jnp.mean(o * o, axis=-1, keepdims=True) + EPS)
                o_ref[rows, sl] = ((o * r * gn) * gate[:, sl]).astype(BF16)
            return carry

        lax.fori_loop(0, HG_CPB, chunk, 0, unroll=2)

    zspec = lambda s: pl.BlockSpec((HG_TB, HG_WIDTH), lambda b, t: (b * HG_NT + t, s))
    return pl.pallas_call(
        body,
        name="hgrn_fwd",
        grid=(B_LOC, HG_NT),
        in_specs=[zspec(0), zspec(1), zspec(2), zspec(3), zspec(0),
                  pl.BlockSpec((3, HG_WIDTH), lambda b, t: (0, 0)), pl.BlockSpec((1, HG_DIM), lambda b, t: (0, 0))],
        out_specs=[pl.BlockSpec((HG_TB, 2 * HG_WIDTH), lambda b, t: (b * HG_NT + t, 0)), zspec(0),
                   pl.BlockSpec((1, HG_HEADS, HG_CPB, HG_DIM, HG_DIM), lambda b, t: (b, 0, t, 0, 0))],
        out_shape=[
            jax.ShapeDtypeStruct((N_TOK, 2 * HG_WIDTH), BF16),
            jax.ShapeDtypeStruct((N_TOK, HG_WIDTH), F32),
            jax.ShapeDtypeStruct((B_LOC, HG_HEADS, HG_NCHUNK, HG_DIM, HG_DIM), F32),
        ],
        scratch_shapes=[pltpu.VMEM((HG_HEADS, HG_DIM, HG_DIM), F32)],
        compiler_params=_cp("parallel", "arbitrary"),
    )(z, z, z, z, o_mem, lb_logits, gnorm)


def _hgrn_bwd(z, opre, dcat, dq_mem, sall, lb_logits, gnorm):
    def body(zq_ref, zf_ref, zi_ref, zg_ref, opre_ref, dout_ref, dqm_ref, sall_ref, lbl_ref, gn_ref,
             dz_ref, dlbl_ref, dgn_ref, dst_ref, dlb_ref, dgn_acc, db_ref, dkk_ref, dbl_ref):
        b_id, t_id = pl.program_id(0), pl.program_id(1)
        lb, p = _lower_bound(lbl_ref[...])
        gn = gn_ref[...]
        mask = _tril(HG_CHUNK)
        tril_f = mask.astype(F32)
        dz_ref[:, 4 * HG_WIDTH:] = dqm_ref[...]

        @pl.when(t_id == 0)
        def _():
            dst_ref[...] = jnp.zeros_like(dst_ref)
            dlb_ref[...] = jnp.zeros_like(dlb_ref)

        @pl.when((b_id == 0) & (t_id == 0))
        def _():
            dgn_acc[...] = jnp.zeros_like(dgn_acc)

        def chunk(i, carry):
            c = HG_CPB - 1 - i
            rows = pl.ds(pl.multiple_of(c * HG_CHUNK, HG_CHUNK), HG_CHUNK)
            zq, zg = zq_ref[rows, :], zg_ref[rows, :]
            q, sq, sig, f, kk, b, bl = _hgrn_gates(zq, zf_ref[rows, :], lb, tril_f)
            v16 = zi_ref[rows, :].astype(BF16)
            eb, enb, ebl_b, ebl = jnp.exp(b), jnp.exp(-b), jnp.exp(bl - b), jnp.exp(bl)
            qd, ki, kd = q * eb, kk * enb, kk * ebl_b
            qd16, ki16, kd16 = qd.astype(BF16), ki.astype(BF16), kd.astype(BF16)
            o_all = opre_ref[rows, :]
            dout = dout_ref[rows, :]
            sg = _sigmoid(zg)
            d_on_all = dout * (zg * sg)
            dgate = dout * (sg * (1.0 + zg * (1.0 - sg)))
            dq_scale = eb * (sq * (1.0 + zq * (1.0 - sq)))
            for h in range(HG_HEADS):
                sl = _head(h)
                o = o_all[:, sl]
                r = lax.rsqrt(jnp.mean(o * o, axis=-1, keepdims=True) + EPS)
                ohat = o * r
                d_on = d_on_all[:, sl]
                dz_ref[rows, _head(h, 3)] = (dgate[:, sl] * (ohat * gn)).astype(BF16)
                dgn_acc[...] += jnp.sum(d_on * ohat, axis=0, keepdims=True)
                dohat = d_on * gn
                do16 = (r * (dohat - ohat * jnp.mean(dohat * ohat, axis=-1, keepdims=True))).astype(BF16)
                st = sall_ref[0, h, c]
                dst = dst_ref[h]
                st16, dst16 = st.astype(BF16), dst.astype(BF16)
                qd_h, ki_h, kd_h, v_h = qd16[:, sl], ki16[:, sl], kd16[:, sl], v16[:, sl]
                a16 = jnp.where(mask, lax.dot_general(qd_h, ki_h, _NT, preferred_element_type=F32), 0.0).astype(BF16)
                da16 = jnp.where(mask, lax.dot_general(do16, v_h, _NT, preferred_element_type=F32), 0.0).astype(BF16)
                dv = lax.dot_general(a16, do16, _TN, preferred_element_type=F32) + lax.dot_general(
                    kd_h, dst16, _NT, preferred_element_type=F32)
                dqd = jnp.dot(da16, ki_h, preferred_element_type=F32) + jnp.dot(do16, st16, preferred_element_type=F32)
                dki = lax.dot_general(da16, qd_h, _TN, preferred_element_type=F32)
                dkd = jnp.dot(v_h, dst16, preferred_element_type=F32)
                dbl_ref[:, sl] = jnp.sum(dkd * kd[:, sl], axis=0, keepdims=True) + ebl[:, sl] * jnp.sum(
                    st * dst, axis=0, keepdims=True)
                dst_ref[h] = dst * ebl[:, sl] + lax.dot_general(do16, qd_h, _TN, preferred_element_type=F32)
                dz_ref[rows, _head(h, 2)] = dv.astype(BF16)
                dz_ref[rows, sl] = (dqd * dq_scale[:, sl]).astype(BF16)
                dkk_ref[:, sl] = dki * enb[:, sl] + dkd * ebl_b[:, sl]
                db_ref[:, sl] = dqd * qd[:, sl] - dki * ki[:, sl] - dkd * kd[:, sl]
            dlogf = lax.dot_general(tril_f, db_ref[...], _TN, preferred_element_type=F32,
                                    precision=lax.Precision.HIGHEST) + dbl_ref[...]
            df = dlogf / f - dkk_ref[...]
            dz_ref[rows, HG_WIDTH:2 * HG_WIDTH] = (df * (1.0 - lb) * sig * (1.0 - sig)).astype(BF16)
            dlb_ref[...] += jnp.sum(df * (1.0 - sig), axis=0, keepdims=True)
            return carry

        lax.fori_loop(0, HG_CPB, chunk, 0, unroll=2)

        @pl.when(t_id == HG_NT - 1)
        def _():
            row0 = (lax.broadcasted_iota(jnp.int32, (3, HG_WIDTH), 0) == 0).astype(F32)
            dlbl_part = dlb_ref[...] * lb * (row0 - p)

            @pl.when(b_id == 0)
            def _():
                dlbl_ref[...] = dlbl_part

            @pl.when(b_id > 0)
            def _():
                dlbl_ref[...] += dlbl_part

            dgn_ref[...] = dgn_acc[...]

    rev = lambda b, t: b * HG_NT + HG_NT - 1 - t
    zspec = lambda s: pl.BlockSpec((HG_TB, HG_WIDTH), lambda b, t: (rev(b, t), s))
    return pl.pallas_call(
        body,
        name="hgrn_bwd",
        grid=(B_LOC, HG_NT),
        in_specs=[zspec(0), zspec(1), zspec(2), zspec(3), zspec(0), zspec(0), zspec(0),
                  pl.BlockSpec((1, HG_HEADS, HG_CPB, HG_DIM, HG_DIM), lambda b, t: (b, 0, HG_NT - 1 - t, 0, 0)),
                  pl.BlockSpec((3, HG_WIDTH), lambda b, t: (0, 0)), pl.BlockSpec((1, HG_DIM), lambda b, t: (0, 0))],
        out_specs=[pl.BlockSpec((HG_TB, 5 * HG_WIDTH), lambda b, t: (rev(b, t), 0)),
                   pl.BlockSpec((3, HG_WIDTH), lambda b, t: (0, 0)), pl.BlockSpec((1, HG_DIM), lambda b, t: (0, 0))],
        out_shape=[jax.ShapeDtypeStruct((N_TOK, 5 * HG_WIDTH), BF16),
                   jax.ShapeDtypeStruct((3, HG_WIDTH), F32), jax.ShapeDtypeStruct((1, HG_DIM), F32)],
        scratch_shapes=[pltpu.VMEM((HG_HEADS, HG_DIM, HG_DIM), F32), pltpu.VMEM((1, HG_WIDTH), F32),
                        pltpu.VMEM((1, HG_DIM), F32), pltpu.VMEM((HG_CHUNK, HG_WIDTH), F32),
                        pltpu.VMEM((HG_CHUNK, HG_WIDTH), F32), pltpu.VMEM((1, HG_WIDTH), F32)],
        compiler_params=_cp("arbitrary", "arbitrary"),
    )(z, z, z, z, opre, dcat, dq_mem, sall, lb_logits, gnorm)


GM_TM = 256


def _gmlp_norm(zv, ln_g, ln_b):
    gv, dgelu = _gelu_parts(zv)
    xc = gv - jnp.mean(gv, axis=-1, keepdims=True)
    rstd = lax.rsqrt(jnp.mean(xc * xc, axis=-1, keepdims=True) + EPS)
    vhat = xc * rstd
    return vhat * ln_g + ln_b, vhat, rstd, dgelu


def _gmlp_specs():
    half = lambda j: pl.BlockSpec((GM_TM, GM_WIDTH), lambda i: (i, j))
    vec = pl.BlockSpec((1, GM_WIDTH), lambda i: (0, 0))
    w = pl.BlockSpec((GM_GROUPS, GM_CHUNK, GM_CHUNK), lambda i: (0, 0, 0))
    bt = pl.BlockSpec((GM_CHUNK, GM_GROUPS), lambda i: (0, 0))
    return half, vec, w, bt


def _gmlp_fwd(z, o_mem, ln_g, ln_b, w_s, b_st):
    def body(zu_ref, zv_ref, omem_ref, g_ref, b_ref, w_ref, bt_ref, o_ref):
        o_ref[:, GM_WIDTH:] = omem_ref[...]
        u, _ = _gelu_parts(zu_ref[...])
        v, _, _, _ = _gmlp_norm(zv_ref[...], g_ref[...], b_ref[...])
        v16 = v.astype(BF16)
        mask = _tril(GM_CHUNK)
        bt = bt_ref[...]
        for g in range(GM_GROUPS):
            wm16 = jnp.where(mask, w_ref[g], 0.0).astype(BF16)
            cols = slice(g * GM_GDIM, (g + 1) * GM_GDIM)
            for c in range(GM_TM // GM_CHUNK):
                rows = slice(c * GM_CHUNK, (c + 1) * GM_CHUNK)
                mixed = jnp.dot(wm16, v16[rows, cols], preferred_element_type=F32) + bt[:, g:g + 1]
                o_ref[rows, cols] = (u[rows, cols] * mixed).astype(BF16)

    half, vec, w, bt = _gmlp_specs()
    return pl.pallas_call(
        body,
        name="gmlp_fwd",
        grid=(N_TOK // GM_TM,),
        in_specs=[half(0), half(1), pl.BlockSpec((GM_TM, XA_HEADS * XA_DIM), lambda i: (i, 0)), vec, vec, w, bt],
        out_specs=pl.BlockSpec((GM_TM, GM_WIDTH + XA_HEADS * XA_DIM), lambda i: (i, 0)),
        out_shape=jax.ShapeDtypeStruct((N_TOK, GM_WIDTH + XA_HEADS * XA_DIM), BF16),
        compiler_params=_cp("parallel"),
    )(z, z, o_mem, ln_g, ln_b, w_s, b_st)


def _gmlp_bwd(z, dcat, dq_mem, ln_g, ln_b, w_s, b_st):
    def body(zu_ref, zv_ref, dout_ref, dqm_ref, g_ref, b_ref, w_ref, bt_ref,
             dz_ref, dw_ref, dbt_ref, dg_ref, db_ref, dv_ref):
        dz_ref[:, 2 * GM_WIDTH:] = dqm_ref[...]
        @pl.when(pl.program_id(0) == 0)
        def _():
            dw_ref[...] = jnp.zeros_like(dw_ref)
            dbt_ref[...] = jnp.zeros_like(dbt_ref)
            dg_ref[...] = jnp.zeros_like(dg_ref)
            db_ref[...] = jnp.zeros_like(db_ref)

        zu = zu_ref[...]
        u, du_dz = _gelu_parts(zu)
        ln_g = g_ref[...]
        v, vhat, rstd, dgv_dz = _gmlp_norm(zv_ref[...], ln_g, b_ref[...])
        v16 = v.astype(BF16)
        dout = dout_ref[...]
        dmixed = dout * u
        dm16 = dmixed.astype(BF16)
        mask = _tril(GM_CHUNK)
        bt = bt_ref[...]
        group_id = lax.broadcasted_iota(jnp.int32, (1, GM_GROUPS), 1)
        dbt = jnp.zeros((GM_CHUNK, GM_GROUPS), F32)
        for g in range(GM_GROUPS):
            wm16 = jnp.where(mask, w_ref[g], 0.0).astype(BF16)
            cols = slice(g * GM_GDIM, (g + 1) * GM_GDIM)
            dw = jnp.zeros((GM_CHUNK, GM_CHUNK), F32)
            dbt_g = jnp.zeros((GM_CHUNK, 1), F32)
            for c in range(GM_TM // GM_CHUNK):
                rows = slice(c * GM_CHUNK, (c + 1) * GM_CHUNK)
                mixed = jnp.dot(wm16, v16[rows, cols], preferred_element_type=F32) + bt[:, g:g + 1]
                dz_ref[rows, cols] = (dout[rows, cols] * mixed * du_dz[rows, cols]).astype(BF16)
                dw += lax.dot_general(dm16[rows, cols], v16[rows, cols], _NT, preferred_element_type=F32)
                dbt_g += jnp.sum(dmixed[rows, cols], axis=-1, keepdims=True)
                dv_ref[rows, cols] = lax.dot_general(wm16, dm16[rows, cols], _TN, preferred_element_type=F32)
            dw_ref[g] += jnp.where(mask, dw, 0.0)
            dbt = dbt + dbt_g * (group_id == g).astype(F32)
        dbt_ref[...] += dbt
        dv = dv_ref[...]
        dg_ref[...] += jnp.sum(dv * vhat, axis=0, keepdims=True)
        db_ref[...] += jnp.sum(dv, axis=0, keepdims=True)
        dvh = dv * ln_g
        dgv = rstd * (dvh - jnp.mean(dvh, axis=-1, keepdims=True) - vhat * jnp.mean(dvh * vhat, axis=-1, keepdims=True))
        dz_ref[:, GM_WIDTH:2 * GM_WIDTH] = (dgv * dgv_dz).astype(BF16)

    half, vec, w, bt = _gmlp_specs()
    dz_width = 2 * GM_WIDTH + XA_HEADS * XA_DIM
    return pl.pallas_call(
        body,
        name="gmlp_bwd",
        grid=(N_TOK // GM_TM,),
        in_specs=[half(0), half(1), half(0), pl.BlockSpec((GM_TM, XA_HEADS * XA_DIM), lambda i: (i, 0)), vec, vec, w, bt],
        out_specs=[pl.BlockSpec((GM_TM, dz_width), lambda i: (i, 0)), w, bt, vec, vec],
        out_shape=[jax.ShapeDtypeStruct((N_TOK, dz_width), BF16),
                   jax.ShapeDtypeStruct((GM_GROUPS, GM_CHUNK, GM_CHUNK), F32),
                   jax.ShapeDtypeStruct((GM_CHUNK, GM_GROUPS), F32),
                   jax.ShapeDtypeStruct((1, GM_WIDTH), F32), jax.ShapeDtypeStruct((1, GM_WIDTH), F32)],
        scratch_shapes=[pltpu.VMEM((GM_TM, GM_WIDTH), F32)],
        compiler_params=_cp("arbitrary"),
    )(z, z, dcat, dq_mem, ln_g, ln_b, w_s, b_st)


def _own_slot(shape):
    return pl.BlockSpec((None,) + tuple(shape), lambda i, me_ref: (me_ref[0],) + (0,) * len(shape))


def _place_rows(w, layer, cuts_columns, me, *, name, deps=()):
    _, r, c = w.shape
    n = c if cuts_columns else r

    def body(me_ref, w_ref, *rest):
        o_ref = rest[len(deps)]
        wv = w_ref[...]
        o_ref[...] = (wv.T if cuts_columns else wv).astype(BF16)

    return pl.pallas_call(
        body,
        name=name,
        grid_spec=pltpu.PrefetchScalarGridSpec(
            num_scalar_prefetch=1, grid=(1,),
            in_specs=[pl.BlockSpec((None, r, c), lambda i, me_ref: (layer, 0, 0))] + [ANY_SPEC] * len(deps),
            out_specs=_own_slot((n, D_MODEL))),
        out_shape=jax.ShapeDtypeStruct((N_DEV, n, D_MODEL), BF16),
        compiler_params=_cp("arbitrary"),
    )(me, w, *deps)


def _place_ln(ln_g, ln_b, me):
    blk = ln_g.shape[1]

    def body(me_ref, g_ref, b_ref, o_ref):
        o_ref[...] = jnp.zeros_like(o_ref)
        o_ref[0:1, :] = g_ref[...]
        o_ref[1:2, :] = b_ref[...]

    vec = pl.BlockSpec((1, blk), lambda i, me_ref: (0, 0))
    return pl.pallas_call(
        body,
        name="place_ln",
        grid_spec=pltpu.PrefetchScalarGridSpec(
            num_scalar_prefetch=1, grid=(1,), in_specs=[vec, vec], out_specs=_own_slot((8, blk))),
        out_shape=jax.ShapeDtypeStruct((N_DEV, 8, blk), F32),
        compiler_params=_cp("arbitrary"),
    )(me, ln_g, ln_b)


def _place_slab(a, me, *, name):
    def body(me_ref, a_ref, o_ref):
        o_ref[...] = a_ref[...]

    return pl.pallas_call(
        body,
        name=name,
        grid_spec=pltpu.PrefetchScalarGridSpec(
            num_scalar_prefetch=1, grid=(1,),
            in_specs=[pl.BlockSpec(a.shape, lambda i, me_ref: (0, 0))], out_specs=_own_slot(a.shape)),
        out_shape=jax.ShapeDtypeStruct((N_DEV,) + a.shape, a.dtype),
        compiler_params=_cp("arbitrary"),
    )(me, a)


def _place_own(grads, me, *, name):
    k = len(grads)

    def body(me_ref, *refs):
        for src, dst in zip(refs[:k], refs[k:]):
            dst[...] = src[...]

    specs = [_own_slot(g.shape[1:]) for g in grads]
    return pl.pallas_call(
        body,
        name=name,
        grid_spec=pltpu.PrefetchScalarGridSpec(num_scalar_prefetch=1, grid=(1,), in_specs=specs, out_specs=specs),
        out_shape=[jax.ShapeDtypeStruct(g.shape, g.dtype) for g in grads],
        compiler_params=_cp("arbitrary"),
    )(me, *grads)


def _mesh_pos():
    x, y, c = (lax.axis_index(a) for a in MESH_AXES)
    return x, y, c, 4 * x + 2 * y + c


def _peer(x, y, c, r):
    px = 1 - x if r & 4 else x
    py = 1 - y if r & 2 else y
    pc = 1 - c if r & 1 else c
    return (px, py, pc), 4 * px + 2 * py + pc


RELATIONS = {"scatter": (1, 2, 3, 4, 5, 6, 7), "gather_all": (1, 2, 3, 4, 5, 6, 7), "gather_chips": (1, 2, 4, 6),
             "gather_sibling": (2, 4, 6)}


def _peer_copies(srcs, lands, send_sems, recv_sems, mode, waits):
    x, y, c, me = _mesh_pos()
    rel = RELATIONS[mode]
    pairs = []
    for ri, r in enumerate(rel):
        if mode == "gather_sibling":
            peer, _ = _peer(x, y, c, 1)
            _, sent_blk = _peer(x, y, c, r)
            _, got_blk = _peer(x, y, c, r ^ 1)
        else:
            peer, peer_blk = _peer(x, y, c, r)
            sent_blk, got_blk = (peer_blk if mode == "scatter" else me), peer_blk
        for k, (src, land) in enumerate(zip(srcs, lands)):
            idx = k * len(rel) + ri
            sems = dict(send_sem=send_sems.at[idx], recv_sem=recv_sems.at[idx], device_id=peer,
                        device_id_type=pl.DeviceIdType.MESH)
            dst_blk = sent_blk if mode == "gather_sibling" else me
            mine = pltpu.make_async_remote_copy(src_ref=src.at[sent_blk], dst_ref=land.at[dst_blk], **sems)
            theirs = pltpu.make_async_remote_copy(src_ref=src.at[sent_blk], dst_ref=land.at[got_blk], **sems) if waits else None
            pairs.append((mine, theirs))
    return pairs


DATAFLOW = pltpu.SideEffectType.DATAFLOW_SIDE_EFFECTING


def _in_hbm(a):
    return pltpu.with_memory_space_constraint(a, pltpu.HBM)


def _copies_start(srcs, lands, *, mode, name, deps=()):
    gather = mode != "scatter"
    arrs = list(lands) if gather else list(srcs) + list(lands)
    n, k, nd = len(arrs), len(lands), len(deps)

    def body(*refs):
        ins, send_sems, recv_sems, token = refs[:n], refs[n + nd], refs[n + nd + 1], refs[2 * n + nd + 2]
        src_refs, land_refs = (ins, ins) if gather else (ins[:k], ins[k:])
        for mine, _ in _peer_copies(src_refs, land_refs, send_sems, recv_sems, mode, waits=False):
            mine.start()
        token[...] = jnp.zeros_like(token)

    n_cp = k * len(RELATIONS[mode])
    return pl.pallas_call(
        body,
        name=name,
        in_specs=[HBM_SPEC] * n + [ANY_SPEC] * nd,
        out_specs=(SEM_SPEC, SEM_SPEC, *[HBM_SPEC] * n, pl.BlockSpec(memory_space=pltpu.VMEM)),
        out_shape=(pltpu.SemaphoreType.DMA((n_cp,)), pltpu.SemaphoreType.DMA((n_cp,)),
                   *[pltpu.HBM(a.shape, a.dtype) for a in arrs], jax.ShapeDtypeStruct((8, 128), F32)),
        input_output_aliases={i: 2 + i for i in range(n)},
        compiler_params=pltpu.CompilerParams(has_side_effects=DATAFLOW),
    )(*[_in_hbm(a) for a in arrs], *deps)


def _copies_wait(arrs, send_sems, recv_sems, after, *, n_lands, mode, name):
    n, k = len(arrs), n_lands
    gather = mode != "scatter"

    def body(*refs):
        ins, send_sems, recv_sems = refs[:n], refs[n], refs[n + 1]
        src_refs, land_refs = (ins, ins) if gather else (ins[:k], ins[k:])
        for mine, theirs in _peer_copies(src_refs, land_refs, send_sems, recv_sems, mode, waits=True):
            mine.wait_send()
            theirs.wait_recv()

    outs = pl.pallas_call(
        body,
        name=name,
        in_specs=[HBM_SPEC] * n + [SEM_SPEC, SEM_SPEC] + [ANY_SPEC] * len(after),
        out_specs=[HBM_SPEC] * n,
        out_shape=[pltpu.HBM(a.shape, a.dtype) for a in arrs],
        input_output_aliases={i: i for i in range(n)},
        compiler_params=pltpu.CompilerParams(has_side_effects=DATAFLOW),
    )(*arrs, send_sems, recv_sems, *after)
    return outs[n - k:]


def _adamw(w, g, m, v):
    m = ADAM_B1 * m + (1.0 - ADAM_B1) * g
    v = ADAM_B2 * v + (1.0 - ADAM_B2) * (g * g)
    m_hat = m / (1.0 - ADAM_B1 ** ADAM_STEP)
    v_hat = v / (1.0 - ADAM_B2 ** ADAM_STEP)
    return -ADAM_LR * (m_hat / (jnp.sqrt(v_hat) + ADAM_EPS) + ADAM_WD * w), m, v


ADAM_TC = 256


def _adam_big(slots, w, m, v, cuts_columns, *, name):
    layers, n, nj = len(slots), slots[0].shape[1], D_MODEL // ADAM_TC

    def body(*refs):
        s_refs = refs[:layers]
        w_ref, m_ref, v_ref, g_ref, d_ref, nm_ref, nv_ref, acc_ref = refs[layers:]
        for ll in range(layers):
            @pl.when(pl.program_id(0) == ll)
            def _(s_ref=s_refs[ll]):
                g = s_ref[0].astype(F32)
                for s in range(1, N_DEV):
                    g = g + s_ref[s].astype(F32)
                acc_ref[...] = g

        g = acc_ref[...].T if cuts_columns else acc_ref[...]
        g_ref[...] = g
        d_ref[...], nm_ref[...], nv_ref[...] = _adamw(w_ref[...], g, m_ref[...], v_ref[...])

    def slot_spec(ll):
        return pl.BlockSpec((N_DEV, n, ADAM_TC),
                            lambda l, j: (0, 0, jnp.where(l < ll, 0, jnp.where(l > ll, nj - 1, j))))

    if cuts_columns:
        w_spec = pl.BlockSpec((None, ADAM_TC, n), lambda l, j: (l, j, 0))
    else:
        w_spec = pl.BlockSpec((None, n, ADAM_TC), lambda l, j: (l, 0, j))
    return pl.pallas_call(
        body,
        name=name,
        grid=(layers, nj),
        in_specs=[slot_spec(ll) for ll in range(layers)] + [w_spec] * 3,
        out_specs=[w_spec] * 4,
        out_shape=[jax.ShapeDtypeStruct(w.shape, F32)] * 4,
        scratch_shapes=[pltpu.VMEM((n, ADAM_TC), F32)],
        compiler_params=_cp("arbitrary", "arbitrary"),
    )(*slots, w, m, v)


def _adam_slabs(slots, ws, ms, vs):
    n = len(slots)

    def body(*refs):
        ins, outs = refs[:4 * n], refs[4 * n:]
        for k in range(n):
            s_ref, w_ref, m_ref, v_ref = ins[k], ins[n + k], ins[2 * n + k], ins[3 * n + k]
            g = s_ref[0]
            for s in range(1, N_DEV):
                g = g + s_ref[s]
            outs[4 * k][...] = g
            outs[4 * k + 1][...], outs[4 * k + 2][...], outs[4 * k + 3][...] = _adamw(w_ref[...], g, m_ref[...], v_ref[...])

    res = pl.pallas_call(
        body,
        name="small_adamw",
        out_shape=[jax.ShapeDtypeStruct(w.shape, F32) for w in ws for _ in range(4)],
        compiler_params=pltpu.CompilerParams(vmem_limit_bytes=VMEM_LIMIT_BYTES),
    )(*slots, *ws, *ms, *vs)
    return [res[4 * k:4 * k + 4] for k in range(n)]


def _adam_vecs(gs, ws, ms, vs):
    n = len(gs)

    def body(*refs):
        ins, outs = refs[:4 * n], refs[4 * n:]
        for k in range(n):
            outs[3 * k][...], outs[3 * k + 1][...], outs[3 * k + 2][...] = _adamw(
                ins[n + k][...], ins[k][...], ins[2 * n + k][...], ins[3 * n + k][...])

    res = pl.pallas_call(
        body,
        name="ln_adamw",
        out_shape=[jax.ShapeDtypeStruct(w.shape, F32) for w in ws for _ in range(3)],
        compiler_params=pltpu.CompilerParams(vmem_limit_bytes=VMEM_LIMIT_BYTES),
    )(*gs, *ws, *ms, *vs)
    return [res[3 * k:3 * k + 3] for k in range(n)]


SLAB_AT = dict(mem_norm=0, lb_logits=1, ffn1_norm=4, mix_norm=6, hgrn_gnorm=8, gmlp_ln_g=9, gmlp_ln_b=11,
               gmlp_b_s=13, ffn2_norm=14, final_norm=16)
SLAB_ROWS = 24
LOSS_ROW = 17
SMALL_SHARDED = ("gmlp_ln_g", "gmlp_ln_b")


def _pack_slab(parts, *, name, deps=()):
    flat, plan = [], []
    for pname, at in SLAB_AT.items():
        for a in parts.get(pname, ()):
            flat.append(a)
            plan.append((at, a.shape))
            at += max(1, a.shape[0] * a.shape[1] // D_MODEL)
    for a in parts.get("loss", ()):
        flat.append(a)
        plan.append((LOSS_ROW, a.shape))

    def body(*refs):
        o_ref = refs[-1]
        o_ref[...] = jnp.zeros_like(o_ref)
        for ref, (at, (r, w)) in zip(refs, plan):
            if w == D_MODEL or r == 1 and w < D_MODEL:
                o_ref[at:at + r, 0:w] = ref[...]
            elif w < D_MODEL:
                for j in range(r):
                    o_ref[at:at + 1, j * w:(j + 1) * w] = ref[j:j + 1, :]
            else:
                for j in range(w // D_MODEL):
                    o_ref[at + j:at + j + 1, :] = ref[:, j * D_MODEL:(j + 1) * D_MODEL]

    return pl.pallas_call(
        body,
        name=name,
        in_specs=[pl.BlockSpec(memory_space=pltpu.VMEM)] * len(flat) + [ANY_SPEC] * len(deps),
        out_shape=jax.ShapeDtypeStruct((SLAB_ROWS, D_MODEL), F32),
        compiler_params=pltpu.CompilerParams(vmem_limit_bytes=VMEM_LIMIT_BYTES),
    )(*flat, *deps)


def _unpack_slab(slab, shapes):
    out = {}
    for pname, at in SLAB_AT.items():
        if pname in SMALL_SHARDED:
            continue
        size = math.prod(shapes[pname])
        rows = max(1, size // D_MODEL)
        out[pname] = slab[at:at + rows].reshape(-1)[:size].reshape(shapes[pname])
    return out


def _ffn_fwd(x, norm_g, block, layer, full, get_weights):
    tag = f"l{layer}_{block}"
    full.update(get_weights((layer, f"{block}_in"), (x,)))
    h, z, act = _norm_mm(x, norm_g, full[(f"{block}_w_in", layer)], swiglu=True, tm=512, tn=1408, deps=full.pop("deps", ()),
                         name=f"{tag}_in")
    full.update(get_weights((layer, f"{block}_out"), (act,)))
    y = _mm(act, full[(f"{block}_w_out", layer)], tm=512, tn=D_MODEL, tk=D_FF, out_dtype=F32, res=x, scale=0.5,
            deps=full.pop("deps", ()), name=f"{tag}_out")
    return y, (x, h, z, act)


def _ffn_bwd(dy, dy16, saved, norm_g, w_in_t, w_out, tag, deps=(), after_out_wgrad=None, before_in_wgrad=None):
    x, h, z, act = saved
    dw_out = _mm(act, dy16, ta=True, tm=1408, tn=D_MODEL, tk=N_TOK, out_dtype=BF16, scale=0.5, deps=deps,
                 name=f"{tag}_out_wgrad")
    sent = after_out_wgrad(dw_out) if after_out_wgrad is not None else ()
    dz = _swiglu_dgrad(dy16, w_out, z, scale=0.5, deps=sent, name=f"{tag}_out_dgrad")
    if before_in_wgrad is None:
        dw_in_t = _planes_wgrad(dz, h, name=f"{tag}_in_wgrad")
        dx, dx16, dg = _dgrad_norm_bwd(dz, w_in_t, x, norm_g, dy, name=f"{tag}_in_dgrad")
    else:
        dx, dx16, dg = _dgrad_norm_bwd(dz, w_in_t, x, norm_g, dy, name=f"{tag}_in_dgrad")
        dw_in_t = _planes_wgrad(dz, h, deps=before_in_wgrad(dg), name=f"{tag}_in_wgrad")
    return dx, dx16, dg, dw_in_t, dw_out


def kernel(x, mem, mem_norm, lb_logits, ffn1_norm, ffn1_w_in, ffn1_w_out, mix_norm, mem_w_kv, hgrn_w_in, hgrn_gnorm, hgrn_w_out, gmlp_w_in, gmlp_ln_g, gmlp_ln_b, gmlp_w_s, gmlp_b_s, gmlp_w_out, ffn2_norm, ffn2_w_in, ffn2_w_out, final_norm, loss_target, m_mem_norm, m_lb_logits, m_ffn1_norm, m_ffn1_w_in, m_ffn1_w_out, m_mix_norm, m_mem_w_kv, m_hgrn_w_in, m_hgrn_gnorm, m_hgrn_w_out, m_gmlp_w_in, m_gmlp_ln_g, m_gmlp_ln_b, m_gmlp_w_s, m_gmlp_b_s, m_gmlp_w_out, m_ffn2_norm, m_ffn2_w_in, m_ffn2_w_out, m_final_norm, v_mem_norm, v_lb_logits, v_ffn1_norm, v_ffn1_w_in, v_ffn1_w_out, v_mix_norm, v_mem_w_kv, v_hgrn_w_in, v_hgrn_gnorm, v_hgrn_w_out, v_gmlp_w_in, v_gmlp_ln_g, v_gmlp_ln_b, v_gmlp_w_s, v_gmlp_b_s, v_gmlp_w_out, v_ffn2_norm, v_ffn2_w_in, v_ffn2_w_out, v_final_norm):
    weights = dict(mem_norm=mem_norm, lb_logits=lb_logits, ffn1_norm=ffn1_norm, ffn1_w_in=ffn1_w_in, ffn1_w_out=ffn1_w_out, mix_norm=mix_norm, mem_w_kv=mem_w_kv, hgrn_w_in=hgrn_w_in, hgrn_gnorm=hgrn_gnorm, hgrn_w_out=hgrn_w_out, gmlp_w_in=gmlp_w_in, gmlp_ln_g=gmlp_ln_g, gmlp_ln_b=gmlp_ln_b, gmlp_w_s=gmlp_w_s, gmlp_b_s=gmlp_b_s, gmlp_w_out=gmlp_w_out, ffn2_norm=ffn2_norm, ffn2_w_in=ffn2_w_in, ffn2_w_out=ffn2_w_out, final_norm=final_norm)
    mom_m = dict(mem_norm=m_mem_norm, lb_logits=m_lb_logits, ffn1_norm=m_ffn1_norm, ffn1_w_in=m_ffn1_w_in, ffn1_w_out=m_ffn1_w_out, mix_norm=m_mix_norm, mem_w_kv=m_mem_w_kv, hgrn_w_in=m_hgrn_w_in, hgrn_gnorm=m_hgrn_gnorm, hgrn_w_out=m_hgrn_w_out, gmlp_w_in=m_gmlp_w_in, gmlp_ln_g=m_gmlp_ln_g, gmlp_ln_b=m_gmlp_ln_b, gmlp_w_s=m_gmlp_w_s, gmlp_b_s=m_gmlp_b_s, gmlp_w_out=m_gmlp_w_out, ffn2_norm=m_ffn2_norm, ffn2_w_in=m_ffn2_w_in, ffn2_w_out=m_ffn2_w_out, final_norm=m_final_norm)
    mom_v = dict(mem_norm=v_mem_norm, lb_logits=v_lb_logits, ffn1_norm=v_ffn1_norm, ffn1_w_in=v_ffn1_w_in, ffn1_w_out=v_ffn1_w_out, mix_norm=v_mix_norm, mem_w_kv=v_mem_w_kv, hgrn_w_in=v_hgrn_w_in, hgrn_gnorm=v_hgrn_gnorm, hgrn_w_out=v_hgrn_w_out, gmlp_w_in=v_gmlp_w_in, gmlp_ln_g=v_gmlp_ln_g, gmlp_ln_b=v_gmlp_ln_b, gmlp_w_s=v_gmlp_w_s, gmlp_b_s=v_gmlp_b_s, gmlp_w_out=v_gmlp_w_out, ffn2_norm=v_ffn2_norm, ffn2_w_in=v_ffn2_w_in, ffn2_w_out=v_ffn2_w_out, final_norm=v_final_norm)
    order = list(weights)
    _, _, _, me = _mesh_pos()
    me_arr = jnp.reshape(me, (1,)).astype(jnp.int32)
    cuts = {name: c for name, c, _, _ in GROUPS}
    rows_already = tuple(name for name, c, _, n in GROUPS if c and n % 128)
    as_rows = lambda a: jnp.transpose(a, (0, 2, 1))
    for name in rows_already:
        weights[name], mom_m[name], mom_v[name] = as_rows(weights[name]), as_rows(mom_m[name]), as_rows(mom_v[name])
        cuts[name] = False

    mix1 = (("mem_w_kv", 1), ("gmlp_w_in", 0), ("gmlp_w_out", 0))
    gather_plan = (
        ((0, "ffn1_in"), _stage_pieces(0, "ffn1")),
        ((0, "mix_in"), _stage_pieces(0, "mix")),
        ((0, "ffn2_in"), _stage_pieces(0, "ffn2")),
        ((1, "ffn1_in"), _stage_pieces(1, "ffn1")),
        ((1, "mix_in"), mix1),
        ((1, "ffn2_in"), _stage_pieces(1, "ffn2")),
    )
    stage_of = {use: k for k, (use, _) in enumerate(gather_plan)}
    in_flight = {}

    def place(k, deps=()):
        pieces = gather_plan[k][1]
        lands = [_place_rows(weights[name], l, cuts[name], me_arr, deps=deps, name=f"place_{name}_{l}")
                 for name, l in pieces]
        if pieces is mix1:
            lands.append(_place_ln(gmlp_ln_g, gmlp_ln_b, me_arr))
        return lands

    placed = {0: place(0)}

    def start_chips(k, deps):
        lands = placed[k]
        send_sems, recv_sems, *thru, token = _copies_start(lands, lands, mode="gather_chips", deps=deps,
                                                           name=f"gather{k}_chips_start")
        in_flight[k] = (thru, send_sems, recv_sems)
        return token

    def pass_to_sibling(k, after):
        thru, send_sems, recv_sems = in_flight[k]
        outs = _copies_wait(thru, send_sems, recv_sems, after, n_lands=len(thru), mode="gather_chips",
                            name=f"gather{k}_chips_wait")
        send_sems, recv_sems, *thru, token = _copies_start(outs, outs, mode="gather_sibling",
                                                           name=f"gather{k}_sibling_start")
        in_flight[k] = (thru, send_sems, recv_sems)
        return token, token

    first_sent = start_chips(0, ())
    placed.update({k: place(k, (first_sent,)) for k in range(1, len(gather_plan))})
    placed_later = tuple(a for k in range(1, len(gather_plan)) for a in placed[k])
    points = [(i, p) for i in (0, 1) for p in ("ffn1_in", "ffn1_out", "mix_in", "mix_out", "ffn2_in", "ffn2_out")]
    pass_at = {j: points[points.index(use) - 1] for j, (use, _) in enumerate(gather_plan) if j}
    pass_at[1] = gather_plan[1][0]

    started = {0}

    def get_weights(use, after):
        tokens, w = [], {}
        k = stage_of.get(use)

        def pass_on(j, after):
            token, landed = pass_to_sibling(j, after)
            tokens.append(token)
            if j + 1 < len(gather_plan) and j + 1 not in started:
                started.add(j + 1)
                tokens.append(start_chips(j + 1, (landed,)))

        if k == 0:
            pass_on(0, tuple(after) + placed_later)
        elif k is not None and pass_at[k] == use:
            pass_on(k, after)
        if k is not None:
            thru, send_sems, recv_sems = in_flight[k]
            outs = _copies_wait(thru, send_sems, recv_sems, after, n_lands=len(thru), mode="gather_sibling",
                                name=f"gather{k}_sibling_wait")
            after = (outs[0],)
            pieces = gather_plan[k][1]
            w = {p: o.reshape(N_DEV * o.shape[1], D_MODEL) for p, o in zip(pieces, outs)}
            if pieces is mix1:
                w["ln_g"] = outs[-1][:, 0, :].reshape(1, GM_WIDTH)
                w["ln_b"] = outs[-1][:, 1, :].reshape(1, GM_WIDTH)
        for j, at in pass_at.items():
            if at == use and j != k:
                pass_on(j, after)
        w["deps"] = tuple(tokens)
        return w

    scatter = {}

    def put_grads(st, grads):
        if st in ("w_s", "small"):
            slab = grads.reshape(GM_GROUPS * GM_CHUNK, GM_CHUNK) if st == "w_s" else _pack_slab(grads, name="pack_small_grads")
            land = _place_slab(slab, me_arr, name=f"{st}_place")
            send_sems, recv_sems, *thru, token = _copies_start([land], [land], mode="gather_all", name=f"{st}_start")
            scatter[st] = (thru, send_sems, recv_sems)
            return (token,)
        views = [g.reshape(N_DEV, -1, D_MODEL) for g in grads.values()]
        recv = _place_own(views, me_arr, name=f"scatter_place_l{st[0]}_{st[1]}")
        send_sems, recv_sems, *thru, token = _copies_start(views, recv, mode="scatter",
                                                           name=f"scatter_start_l{st[0]}_{st[1]}")
        scatter[st] = (tuple(grads), thru, send_sems, recv_sems)
        return (token,)

    dx, last_sent = _step_local(
        x, mem, loss_target, get_weights, put_grads, mem_norm, lb_logits, ffn1_norm, mix_norm, hgrn_gnorm,
        gmlp_w_s, gmlp_b_s, ffn2_norm, final_norm)

    slots = {}

    def wait_grads(blk, after, last=False):
        for st, entry in scatter.items():
            if isinstance(st, tuple) and st[1].startswith(blk) and (st == (0, "ffn1_in")) == last:
                pieces, thru, send_sems, recv_sems = entry
                outs = _copies_wait(thru, send_sems, recv_sems, after, n_lands=len(thru) // 2, mode="scatter",
                                    name=f"scatter_wait_l{st[0]}_{st[1]}")
                slots.update(zip(pieces, outs))

    grad, delta, new_m, new_v = {}, {}, {}, {}

    def adam_groups(names):
        for name in names:
            layers = GROUP_LAYERS[name]
            grad[name], delta[name], new_m[name], new_v[name] = _adam_big(
                [slots[(name, l)] for l in range(layers)], weights[name], mom_m[name], mom_v[name], cuts[name],
                name=f"{name}_adamw")

    wait_grads("ffn2", (dx, *last_sent))
    adam_groups(("ffn2_w_in", "ffn2_w_out"))
    wait_grads("mix", (delta["ffn2_w_out"],))
    adam_groups(("mem_w_kv", "gmlp_w_in", "gmlp_w_out", "hgrn_w_in", "hgrn_w_out"))
    wait_grads("ffn1", (delta["hgrn_w_out"],))
    adam_groups(("ffn1_w_out",))

    def small_parts(src):
        parts = {n: [src[n].reshape(-1, src[n].shape[-1])] for n in SLAB_AT if n not in SMALL_SHARDED}
        return parts

    w_s_rows = lambda a: a.reshape(GM_GROUPS * GM_CHUNK, GM_CHUNK)
    small_done = (delta["hgrn_w_out"],)
    (slab_slots,) = _copies_wait(*scatter["small"], small_done, n_lands=1, mode="gather_all", name="small_wait")
    (ws_slots,) = _copies_wait(*scatter["w_s"], small_done, n_lands=1, mode="gather_all", name="w_s_wait")
    (g_slab, d_slab, nm_slab, nv_slab), (g_ws, d_ws, nm_ws, nv_ws) = _adam_slabs(
        [slab_slots, ws_slots],
        [_pack_slab(small_parts(weights), deps=(dx,), name="pack_small_w"), w_s_rows(gmlp_w_s)],
        [_pack_slab(small_parts(mom_m), deps=(dx,), name="pack_small_m"), w_s_rows(m_gmlp_w_s)],
        [_pack_slab(small_parts(mom_v), deps=(dx,), name="pack_small_v"), w_s_rows(v_gmlp_w_s)])
    shapes = {n: weights[n].shape for n in SLAB_AT}
    for out, slab, ws in ((grad, g_slab, g_ws), (delta, d_slab, d_ws), (new_m, nm_slab, nm_ws), (new_v, nv_slab, nv_ws)):
        out.update(_unpack_slab(slab, shapes))
        out["gmlp_w_s"] = ws.reshape(gmlp_w_s.shape)
    blk = GM_WIDTH // N_DEV
    g_ln = [lax.dynamic_slice(g_slab[SLAB_AT[n]:SLAB_AT[n] + 2].reshape(1, GM_WIDTH), (0, me * blk), (1, blk))
            for n in SMALL_SHARDED]
    ln_out = _adam_vecs(g_ln, [weights[n] for n in SMALL_SHARDED], [mom_m[n] for n in SMALL_SHARDED],
                        [mom_v[n] for n in SMALL_SHARDED])
    for n, g, (d, nm, nv) in zip(SMALL_SHARDED, g_ln, ln_out):
        grad[n], delta[n], new_m[n], new_v[n] = g, d, nm, nv

    wait_grads("ffn1", tuple(delta[n] for n in delta if n in GROUP_LAYERS) + (d_slab,), last=True)
    adam_groups(("ffn1_w_in",))

    for name in rows_already:
        for out in (grad, delta, new_m, new_v):
            out[name] = as_rows(out[name])
    loss = g_slab[LOSS_ROW, 0]
    grad_x = dx.reshape(B_LOC, SEQ, D_MODEL)
    return (loss, grad_x, *[grad[n] for n in order], *[delta[n] for n in order],
            *[new_m[n] for n in order], *[new_v[n] for n in order])


def _step_local(x, mem, loss_target, get_weights, put_grads, mem_norm, lb_logits, ffn1_norm, mix_norm, hgrn_gnorm,
                gmlp_w_s, gmlp_b_s, ffn2_norm, final_norm):
    w_s = gmlp_w_s[0]
    b_st = gmlp_b_s[0].T

    xs = x.reshape(N_TOK, D_MODEL)
    mem2d = mem.reshape(B_LOC * MEM_LEN, D_MODEL)
    mem_g = mem_norm.reshape(1, D_MODEL)
    saved, full = [], {}
    for i in range(2):
        xs, s_ffn1 = _ffn_fwd(xs, ffn1_norm[i:i + 1], "ffn1", i, full, get_weights)
        if i == 0:
            memn = _rms_fwd(mem2d, mem_g, deps=(xs,), name="mem_norm_fwd")
        full.update(get_weights((i, "mix_in"), (xs,)))
        mixer = "hgrn" if i == 0 else "gmlp"
        hm, zm = _norm_mm(xs, mix_norm[i:i + 1], full[(f"{mixer}_w_in", 0)], swiglu=False, tm=1024, tn=1280, deps=full.pop("deps", ()),
                          name=f"l{i}_mix_in")
        kv = _mm(memn, full[("mem_w_kv", i)], tb=True, tm=512, tn=512, tk=D_MODEL, out_dtype=F32, name=f"l{i}_mem_kv")
        o_mem = _attn_fwd(zm, kv, name=f"l{i}_attn")
        if i == 0:
            cat, o_pre, s_all = _hgrn_fwd(zm, o_mem, lb_logits, hgrn_gnorm)
            mix_saved = (o_pre, s_all)
        else:
            cat = _gmlp_fwd(zm, o_mem, full["ln_g"], full["ln_b"], w_s, b_st)
            mix_saved = ()
        x_mix = xs
        full.update(get_weights((i, "mix_out"), (cat,)))
        xs = _mm(cat, full[(f"{mixer}_w_out", 0)], tm=512, tn=D_MODEL, tk=cat.shape[1], out_dtype=F32, res=xs,
                 deps=full.pop("deps", ()), name=f"l{i}_mix_out")
        xs, s_ffn2 = _ffn_fwd(xs, ffn2_norm[i:i + 1], "ffn2", i, full, get_weights)
        saved.append((s_ffn1, (x_mix, hm, kv, zm, cat, mix_saved), s_ffn2))

    dx, dx16, d_final, loss_part = _loss_head(xs, final_norm.reshape(1, D_MODEL), loss_target.reshape(N_TOK, D_MODEL))

    small = {"final_norm": [d_final], "loss": [loss_part]}
    d_ffn1, d_ffn2, d_mix = [None, None], [None, None], [None, None]
    dmemn = jnp.zeros((B_LOC * MEM_LEN, D_MODEL), F32)
    deps = ()
    for i in (1, 0):
        s_ffn1, (x_mix, hm, kv, zm, cat, mix_saved), s_ffn2 = saved[i]
        dx, dx16, d_ffn2[i], dw_in_t, dw_out = _ffn_bwd(
            dx, dx16, s_ffn2, ffn2_norm[i:i + 1], full[("ffn2_w_in", i)], full[("ffn2_w_out", i)], f"l{i}_ffn2", deps)
        deps = put_grads((i, "ffn2"), {("ffn2_w_in", i): dw_in_t, ("ffn2_w_out", i): dw_out})
        mixer = "hgrn" if i == 0 else "gmlp"
        w_in_t, w_out = full[(f"{mixer}_w_in", 0)], full[(f"{mixer}_w_out", 0)]
        width = cat.shape[1]
        g_mix = {}
        g_mix[(f"{mixer}_w_out", 0)] = _mm(cat, dx16, ta=True, tm=1024, tn=D_MODEL, tk=N_TOK, out_dtype=BF16,
                                           deps=deps, name=f"l{i}_mix_out_wgrad")
        dcat = _mm(dx16, w_out, tb=True, tm=1024, tn=width // 2, tk=D_MODEL, out_dtype=F32, name=f"l{i}_mix_out_dgrad")
        dq, dk, dv = _attn_bwd(zm, kv, dcat, do_off=width - XA_HEADS * XA_DIM, name=f"l{i}_attn_bwd")
        if i == 0:
            dzm, dlbl, dgn = _hgrn_bwd(zm, mix_saved[0], dcat, dq, mix_saved[1], lb_logits, hgrn_gnorm)
            small["lb_logits"], small["hgrn_gnorm"] = [dlbl], [dgn]
            deps = ()
        else:
            dzm, dws, dbt, dlng, dlnb = _gmlp_bwd(zm, dcat, dq, full["ln_g"], full["ln_b"], w_s, b_st)
            small["gmlp_b_s"], small["gmlp_ln_g"], small["gmlp_ln_b"] = [dbt.T], [dlng], [dlnb]
            deps = put_grads("w_s", dws)
        g_mix[(f"{mixer}_w_in", 0)] = _mm(dzm, hm, ta=True, tm=1024, tn=D_MODEL, tk=N_TOK, out_dtype=BF16, deps=deps,
                                          name=f"l{i}_mix_in_wgrad")
        dkv = jnp.concatenate([dk, dv], axis=1)
        g_mix[("mem_w_kv", i)] = _mm(dkv, memn, ta=True, tm=512, tn=D_MODEL, tk=B_LOC * MEM_LEN, out_dtype=BF16,
                                     name=f"l{i}_mem_kv_wgrad")
        deps = put_grads((i, "mix"), g_mix)
        dx, dx16, d_mix[i] = _dgrad_norm_bwd(dzm, w_in_t, x_mix, mix_norm[i:i + 1], dx, deps=deps,
                                             name=f"l{i}_mix_in_dgrad")
        dmemn = _mm(dkv, full[("mem_w_kv", i)], tm=B_LOC * MEM_LEN, tn=D_MODEL, tk=512, out_dtype=F32, res=dmemn,
                    name=f"l{i}_mem_kv_dgrad")
        def send_small(dg, i=i, dmemn=dmemn):
            d_ffn1[i] = dg
            _, _, dmem_g = _rms_bwd(mem2d, mem_g, dmemn, dmemn, name="mem_norm_bwd")
            small.update(mem_norm=[dmem_g], ffn1_norm=d_ffn1, ffn2_norm=d_ffn2, mix_norm=d_mix)
            return put_grads("small", small)

        if i == 0:
            send_out = lambda dw_out: put_grads((0, "ffn1_out"), {("ffn1_w_out", 0): dw_out})
            dx, dx16, d_ffn1[i], dw_in_t, _ = _ffn_bwd(
                dx, dx16, s_ffn1, ffn1_norm[i:i + 1], full[("ffn1_w_in", i)], full[("ffn1_w_out", i)], f"l{i}_ffn1",
                after_out_wgrad=send_out, before_in_wgrad=send_small)
            deps = put_grads((0, "ffn1_in"), {("ffn1_w_in", 0): dw_in_t})
        else:
            dx, dx16, d_ffn1[i], dw_in_t, dw_out = _ffn_bwd(
                dx, dx16, s_ffn1, ffn1_norm[i:i + 1], full[("ffn1_w_in", i)], full[("ffn1_w_out", i)], f"l{i}_ffn1")
            deps = put_grads((i, "ffn1"), {("ffn1_w_in", i): dw_in_t, ("ffn1_w_out", i): dw_out})
    return dx, deps
```

```python
import functools
import math

import jax
import jax.numpy as jnp
from jax import lax
from jax.experimental import pallas as pl
from jax.experimental.pallas import tpu as pltpu

F32 = jnp.float32
BF16 = jnp.bfloat16

D_MODEL = 1024
SEQ = 2048
B_LOC = 2
N_TOK = B_LOC * SEQ
MEM_LEN = 256
N_DEV = 8
EPS = 1e-6
D_FF = 2816
HG_HEADS = 8
HG_DIM = 128
HG_CHUNK = 64
HG_NCHUNK = SEQ // HG_CHUNK
GM_CHUNK = 128
GM_GROUPS = 8
GM_WIDTH = 2048
GM_GDIM = GM_WIDTH // GM_GROUPS
XA_HEADS = 4
XA_DIM = 256
XA_OFF = 4096

ADAM_LR = 0.001
ADAM_B1 = 0.9
ADAM_B2 = 0.999
ADAM_EPS = 1e-08
ADAM_WD = 0.01
ADAM_STEP = 10

VMEM_LIMIT_BYTES = 56 * 1024 * 1024
MESH_AXES = ("x", "y", "c")

GROUPS = (
    ("ffn1_w_in", True, 2, 704),
    ("ffn1_w_out", False, 2, 352),
    ("mem_w_kv", True, 2, 256),
    ("hgrn_w_in", True, 1, 640),
    ("hgrn_w_out", False, 1, 256),
    ("gmlp_w_in", True, 1, 640),
    ("gmlp_w_out", False, 1, 384),
    ("ffn2_w_in", True, 2, 704),
    ("ffn2_w_out", False, 2, 352),
)
GROUP_LAYERS = {name: layers for name, _, layers, _ in GROUPS}


def _stage_pieces(layer, block):
    if block == "mix":
        mixer = "hgrn" if layer == 0 else "gmlp"
        return (("mem_w_kv", layer), (f"{mixer}_w_in", 0), (f"{mixer}_w_out", 0))
    return ((f"{block}_w_in", layer), (f"{block}_w_out", layer))


ANY_SPEC = pl.BlockSpec(memory_space=pl.ANY)
HBM_SPEC = pl.BlockSpec(memory_space=pltpu.HBM)
SEM_SPEC = pl.BlockSpec(memory_space=pltpu.SEMAPHORE)


def _cp(*sem):
    return pltpu.CompilerParams(dimension_semantics=sem, vmem_limit_bytes=VMEM_LIMIT_BYTES)


def _sigmoid(x):
    return 0.5 * jnp.tanh(0.5 * x) + 0.5


def _gelu_parts(x):
    cdf = 0.5 * (1.0 + lax.erf(x * (1.0 / math.sqrt(2.0))))
    pdf = jnp.exp(-0.5 * x * x) * (1.0 / math.sqrt(2.0 * math.pi))
    return x * cdf, cdf + x * pdf


def _mm(a, b, *, ta=False, tb=False, tm, tn, tk, out_dtype, res=None, scale=1.0, deps=(), name):
    m, k = (a.shape[1], a.shape[0]) if ta else a.shape
    n, kb = b.shape if tb else (b.shape[1], b.shape[0])
    assert k == kb and m % tm == 0 and n % tn == 0 and k % tk == 0, (name, a.shape, b.shape)
    nk = k // tk
    dn = (((0 if ta else 1,), (1 if tb else 0,)), ((), ()))
    n_in = 2 + (res is not None) + len(deps)

    def body(*refs):
        a_ref, b_ref = refs[:2]
        r_ref = refs[2] if res is not None else None
        o_ref, scr = refs[n_in], refs[n_in + 1:]
        p = lax.dot_general(a_ref[...].astype(BF16), b_ref[...].astype(BF16), dn, preferred_element_type=F32)

        def finish(acc):
            if scale != 1.0:
                acc = scale * acc
            if r_ref is not None:
                acc = r_ref[...] + acc
            o_ref[...] = acc.astype(out_dtype)

        if nk == 1:
            finish(p)
        else:
            acc_ref = scr[0]
            kk = pl.program_id(2)

            @pl.when(kk == 0)
            def _():
                acc_ref[...] = p

            @pl.when(kk > 0)
            def _():
                acc_ref[...] += p

            @pl.when(kk == nk - 1)
            def _():
                finish(acc_ref[...])

    a_spec = pl.BlockSpec((tk, tm), lambda i, j, kk: (kk, i)) if ta else pl.BlockSpec((tm, tk), lambda i, j, kk: (i, kk))
    b_mode = dict(pipeline_mode=pl.Buffered(1)) if n == tn and nk == 1 else {}
    if tb:
        b_spec = pl.BlockSpec((tn, tk), lambda i, j, kk: (j, kk), **b_mode)
    else:
        b_spec = pl.BlockSpec((tk, tn), lambda i, j, kk: (kk, j), **b_mode)
    o_spec = pl.BlockSpec((tm, tn), lambda i, j, kk: (i, j))
    in_specs = [a_spec, b_spec] + ([o_spec] if res is not None else []) + [ANY_SPEC] * len(deps)
    args = (a, b) + ((res,) if res is not None else ()) + tuple(deps)
    return pl.pallas_call(
        body,
        name=name,
        grid=(m // tm, n // tn, nk),
        in_specs=in_specs,
        out_specs=o_spec,
        out_shape=jax.ShapeDtypeStruct((m, n), out_dtype),
        scratch_shapes=[pltpu.VMEM((tm, tn), F32)] if nk > 1 else [],
        compiler_params=_cp("parallel", "parallel", "arbitrary"),
    )(*args)


def _rms_fwd(x, g, *, name, deps=(), tm=512):
    rows = x.shape[0]

    def body(x_ref, g_ref, *rest):
        o_ref = rest[len(deps)]
        xv = x_ref[...]
        r = lax.rsqrt(jnp.mean(xv * xv, axis=-1, keepdims=True) + EPS)
        o_ref[...] = (xv * r * g_ref[...]).astype(BF16)

    row = pl.BlockSpec((tm, D_MODEL), lambda i: (i, 0))
    return pl.pallas_call(
        body,
        name=name,
        grid=(rows // tm,),
        in_specs=[row, pl.BlockSpec((1, D_MODEL), lambda i: (0, 0))] + [ANY_SPEC] * len(deps),
        out_specs=row,
        out_shape=jax.ShapeDtypeStruct((rows, D_MODEL), BF16),
        compiler_params=_cp("parallel"),
    )(x, g, *deps)


def _rms_bwd(x, g, dh, dres, *, name, deps=(), tm=512):
    rows = x.shape[0]

    def body(x_ref, g_ref, dh_ref, dres_ref, *rest):
        dx_ref, dx16_ref, dg_ref = rest[len(deps):]
        xv = x_ref[...]
        r = lax.rsqrt(jnp.mean(xv * xv, axis=-1, keepdims=True) + EPS)
        xhat = xv * r
        dhv = dh_ref[...]
        part = jnp.sum(dhv * xhat, axis=0, keepdims=True)

        @pl.when(pl.program_id(0) == 0)
        def _():
            dg_ref[...] = part

        @pl.when(pl.program_id(0) > 0)
        def _():
            dg_ref[...] += part

        dxh = dhv * g_ref[...]
        dx = dres_ref[...] + r * (dxh - xhat * jnp.mean(dxh * xhat, axis=-1, keepdims=True))
        dx_ref[...] = dx
        dx16_ref[...] = dx.astype(BF16)

    row = pl.BlockSpec((tm, D_MODEL), lambda i: (i, 0))
    vec = pl.BlockSpec((1, D_MODEL), lambda i: (0, 0))
    return pl.pallas_call(
        body,
        name=name,
        grid=(rows // tm,),
        in_specs=[row, vec, row, row] + [ANY_SPEC] * len(deps),
        out_specs=[row, row, vec],
        out_shape=[jax.ShapeDtypeStruct((rows, D_MODEL), F32), jax.ShapeDtypeStruct((rows, D_MODEL), BF16),
                   jax.ShapeDtypeStruct((1, D_MODEL), F32)],
        compiler_params=_cp("arbitrary"),
    )(x, g, dh, dres, *deps)


_NT = (((1,), (1,)), ((), ()))
_TN = (((0,), (0,)), ((), ()))


def _norm_mm(x, g, w_t, *, swiglu, name, tm, tn, deps=()):
    rows = w_t.shape[0]
    half = rows // 2
    nj = (half if swiglu else rows) // tn
    nd = len(deps)

    def body(x_ref, g_ref, w_ref, *rest):
        outs = rest[nd:]
        h_ref, z_ref = outs[:2]

        def norm():
            xv = x_ref[...]
            r = lax.rsqrt(jnp.mean(xv * xv, axis=-1, keepdims=True) + EPS)
            h_ref[...] = (xv * r * g_ref[...]).astype(BF16)

        if swiglu:
            norm()
            h = h_ref[...]
            for j in range(nj):
                cols = slice(j * tn, (j + 1) * tn)
                gate = lax.dot_general(h, w_ref[j * tn:(j + 1) * tn, :], _NT, preferred_element_type=F32)
                up = lax.dot_general(h, w_ref[half + j * tn:half + (j + 1) * tn, :], _NT, preferred_element_type=F32)
                s = _sigmoid(gate)
                silu = gate * s
                z_ref[0, :, cols] = (up * (s + silu * (1.0 - s))).astype(BF16)
                z_ref[1, :, cols] = silu.astype(BF16)
                outs[2][:, cols] = (silu * up).astype(BF16)
        else:
            j = pl.program_id(1)
            pl.when(j == 0)(norm)
            w = w_ref[pl.ds(pl.multiple_of(j * tn, tn), tn), :]
            z_ref[...] = lax.dot_general(h_ref[...], w, _NT, preferred_element_type=F32)

    grid = (N_TOK // tm,) if swiglu else (N_TOK // tm, nj)
    row = pl.BlockSpec((tm, D_MODEL), lambda i, *_: (i, 0))
    out_specs = [row]
    out_shape = [jax.ShapeDtypeStruct((N_TOK, D_MODEL), BF16)]
    if swiglu:
        out_specs += [pl.BlockSpec((2, tm, half), lambda i: (0, i, 0)), pl.BlockSpec((tm, half), lambda i: (i, 0))]
        out_shape += [jax.ShapeDtypeStruct((2, N_TOK, half), BF16), jax.ShapeDtypeStruct((N_TOK, half), BF16)]
    else:
        out_specs.append(pl.BlockSpec((tm, tn), lambda i, j: (i, j)))
        out_shape.append(jax.ShapeDtypeStruct((N_TOK, rows), F32))
    return pl.pallas_call(
        body,
        name=name,
        grid=grid,
        in_specs=[row, pl.BlockSpec((1, D_MODEL), lambda *_: (0, 0)),
                  pl.BlockSpec((rows, D_MODEL), lambda *_: (0, 0), pipeline_mode=pl.Buffered(1))] + [ANY_SPEC] * nd,
        out_specs=out_specs,
        out_shape=out_shape,
        compiler_params=_cp(*(("parallel",) if swiglu else ("parallel", "arbitrary"))),
    )(x, g, w_t, *deps)


def _swiglu_dgrad(dy16, w_out, z, *, scale, name, deps=(), tm=512, tn=1408):
    def body(dy_ref, w_ref, z_ref, *rest):
        dz_ref = rest[len(deps)]
        dy = dy_ref[...]
        for j in range(D_FF // tn):
            cols = slice(j * tn, (j + 1) * tn)
            da = lax.dot_general(dy, w_ref[cols, :], _NT, preferred_element_type=F32) * scale
            dz_ref[0, :, cols] = (da * z_ref[0, :, cols].astype(F32)).astype(BF16)
            dz_ref[1, :, cols] = (da * z_ref[1, :, cols].astype(F32)).astype(BF16)

    planes = pl.BlockSpec((2, tm, D_FF), lambda i: (0, i, 0))
    return pl.pallas_call(
        body,
        name=name,
        grid=(N_TOK // tm,),
        in_specs=[pl.BlockSpec((tm, D_MODEL), lambda i: (i, 0)),
                  pl.BlockSpec((D_FF, D_MODEL), lambda i: (0, 0), pipeline_mode=pl.Buffered(1)), planes]
        + [ANY_SPEC] * len(deps),
        out_specs=planes,
        out_shape=jax.ShapeDtypeStruct((2, N_TOK, D_FF), BF16),
        compiler_params=_cp("parallel"),
    )(dy16, w_out, z, *deps)


def _planes_wgrad(dz, h, *, name, deps=(), tm=1408):
    per_plane = D_FF // tm

    def body(a_ref, b_ref, *rest):
        o_ref = rest[len(deps)]
        o_ref[...] = lax.dot_general(a_ref[...], b_ref[...], _TN, preferred_element_type=F32).astype(BF16)

    return pl.pallas_call(
        body,
        name=name,
        grid=(2 * per_plane,),
        in_specs=[pl.BlockSpec((None, N_TOK, tm),
                               lambda i: (jnp.where(i < per_plane, 0, 1), 0, jnp.where(i < per_plane, i, i - per_plane))),
                  pl.BlockSpec((N_TOK, D_MODEL), lambda i: (0, 0), pipeline_mode=pl.Buffered(1))] + [ANY_SPEC] * len(deps),
        out_specs=pl.BlockSpec((tm, D_MODEL), lambda i: (i, 0)),
        out_shape=jax.ShapeDtypeStruct((2 * D_FF, D_MODEL), BF16),
        compiler_params=_cp("parallel"),
    )(dz, h, *deps)


def _dgrad_norm_bwd(dz, w_t, x, g, dres, *, name, deps=(), tm=512):
    planes = dz.ndim == 3
    rows = w_t.shape[0]
    half = rows // 2
    nd = len(deps)

    def body(a_ref, b_ref, x_ref, g_ref, dres_ref, *rest):
        dx_ref, dx16_ref, dg_ref = rest[nd:]
        if planes:
            dh = jnp.dot(a_ref[0], b_ref[:half, :], preferred_element_type=F32) + jnp.dot(
                a_ref[1], b_ref[half:, :], preferred_element_type=F32)
        else:
            dh = jnp.dot(a_ref[...], b_ref[...], preferred_element_type=F32)
        xv = x_ref[...]
        r = lax.rsqrt(jnp.mean(xv * xv, axis=-1, keepdims=True) + EPS)
        xhat = xv * r
        part = jnp.sum(dh * xhat, axis=0, keepdims=True)

        @pl.when(pl.program_id(0) == 0)
        def _():
            dg_ref[...] = part

        @pl.when(pl.program_id(0) > 0)
        def _():
            dg_ref[...] += part

        dxh = dh * g_ref[...]
        dx = dres_ref[...] + r * (dxh - xhat * jnp.mean(dxh * xhat, axis=-1, keepdims=True))
        dx_ref[...] = dx
        dx16_ref[...] = dx.astype(BF16)

    a_spec = pl.BlockSpec((2, tm, half), lambda i: (0, i, 0)) if planes else pl.BlockSpec((tm, rows), lambda i: (i, 0))
    row = pl.BlockSpec((tm, D_MODEL), lambda i: (i, 0))
    vec = pl.BlockSpec((1, D_MODEL), lambda i: (0, 0))
    return pl.pallas_call(
        body,
        name=name,
        grid=(N_TOK // tm,),
        in_specs=[a_spec, pl.BlockSpec((rows, D_MODEL), lambda i: (0, 0), pipeline_mode=pl.Buffered(1)), row, vec, row]
        + [ANY_SPEC] * nd,
        out_specs=[row, row, vec],
        out_shape=[jax.ShapeDtypeStruct((N_TOK, D_MODEL), F32), jax.ShapeDtypeStruct((N_TOK, D_MODEL), BF16),
                   jax.ShapeDtypeStruct((1, D_MODEL), F32)],
        compiler_params=_cp("arbitrary"),
    )(dz, w_t, x, g, dres, *deps)


def _loss_head(x, g, target, *, tm=512):
    def body(x_ref, g_ref, t_ref, dx_ref, dx16_ref, dg_ref, loss_ref):
        xv = x_ref[...]
        gv = g_ref[...]
        r = lax.rsqrt(jnp.mean(xv * xv, axis=-1, keepdims=True) + EPS)
        xhat = xv * r
        err = xhat * gv - t_ref[...]
        loss_part = jnp.zeros((1, 128), F32) + 0.5 * jnp.sum(jnp.mean(err * err, axis=-1, keepdims=True))
        dy = err * (1.0 / D_MODEL)
        dg_part = jnp.sum(dy * xhat, axis=0, keepdims=True)

        @pl.when(pl.program_id(0) == 0)
        def _():
            dg_ref[...] = dg_part
            loss_ref[...] = loss_part

        @pl.when(pl.program_id(0) > 0)
        def _():
            dg_ref[...] += dg_part
            loss_ref[...] += loss_part

        dxh = dy * gv
        dx = r * (dxh - xhat * jnp.mean(dxh * xhat, axis=-1, keepdims=True))
        dx_ref[...] = dx
        dx16_ref[...] = dx.astype(BF16)

    row = pl.BlockSpec((tm, D_MODEL), lambda i: (i, 0))
    vec = pl.BlockSpec((1, D_MODEL), lambda i: (0, 0))
    return pl.pallas_call(
        body,
        name="loss_head",
        grid=(N_TOK // tm,),
        in_specs=[row, vec, row],
        out_specs=[row, row, vec, pl.BlockSpec((1, 128), lambda i: (0, 0))],
        out_shape=[
            jax.ShapeDtypeStruct((N_TOK, D_MODEL), F32),
            jax.ShapeDtypeStruct((N_TOK, D_MODEL), BF16),
            jax.ShapeDtypeStruct((1, D_MODEL), F32),
            jax.ShapeDtypeStruct((1, 128), F32),
        ],
        compiler_params=_cp("arbitrary"),
    )(x, g, target)


XA_TQ = 2048
XA_SCALE = XA_DIM ** -0.5


def _attn_probs(q16, k16):
    s = lax.dot_general(q16, k16, _NT, preferred_element_type=F32) * XA_SCALE
    e = jnp.exp(s - jnp.max(s, axis=-1, keepdims=True))
    return e / jnp.sum(e, axis=-1, keepdims=True)


def _attn_fwd(z, kv, *, name):
    nt = SEQ // XA_TQ

    def body(q_ref, k_ref, v_ref, o_ref):
        p = _attn_probs(q_ref[...].astype(BF16), k_ref[...].astype(BF16))
        o_ref[...] = jnp.dot(p.astype(BF16), v_ref[...].astype(BF16), preferred_element_type=F32).astype(BF16)

    return pl.pallas_call(
        body,
        name=name,
        grid=(B_LOC, XA_HEADS, nt),
        in_specs=[
            pl.BlockSpec((XA_TQ, XA_DIM), lambda b, h, t: (b * nt + t, XA_OFF // XA_DIM + h)),
            pl.BlockSpec((MEM_LEN, XA_DIM), lambda b, h, t: (b, h)),
            pl.BlockSpec((MEM_LEN, XA_DIM), lambda b, h, t: (b, XA_HEADS + h)),
        ],
        out_specs=pl.BlockSpec((XA_TQ, XA_DIM), lambda b, h, t: (b * nt + t, h)),
        out_shape=jax.ShapeDtypeStruct((N_TOK, XA_HEADS * XA_DIM), BF16),
        compiler_params=_cp("parallel", "parallel", "arbitrary"),
    )(z, kv, kv)


def _attn_bwd(z, kv, dcat, *, do_off, name):
    nt = SEQ // XA_TQ

    def body(q_ref, k_ref, v_ref, do_ref, dq_ref, dk_ref, dv_ref):
        q16 = q_ref[...].astype(BF16)
        k16 = k_ref[...].astype(BF16)
        v16 = v_ref[...].astype(BF16)
        do16 = do_ref[...].astype(BF16)
        p = _attn_probs(q16, k16)
        dv_part = lax.dot_general(p.astype(BF16), do16, _TN, preferred_element_type=F32)
        dp = lax.dot_general(do16, v16, _NT, preferred_element_type=F32)
        ds16 = (p * (dp - jnp.sum(dp * p, axis=-1, keepdims=True)) * XA_SCALE).astype(BF16)
        dq_ref[...] = jnp.dot(ds16, k16, preferred_element_type=F32).astype(BF16)
        dk_part = lax.dot_general(ds16, q16, _TN, preferred_element_type=F32)

        @pl.when(pl.program_id(2) == 0)
        def _():
            dk_ref[...] = dk_part
            dv_ref[...] = dv_part

        @pl.when(pl.program_id(2) > 0)
        def _():
            dk_ref[...] += dk_part
            dv_ref[...] += dv_part

    qspec = pl.BlockSpec((XA_TQ, XA_DIM), lambda b, h, t: (b * nt + t, XA_OFF // XA_DIM + h))
    kspec = lambda off: pl.BlockSpec((MEM_LEN, XA_DIM), lambda b, h, t: (b, off + h))
    return pl.pallas_call(
        body,
        name=name,
        grid=(B_LOC, XA_HEADS, nt),
        in_specs=[qspec, kspec(0), kspec(XA_HEADS),
                  pl.BlockSpec((XA_TQ, XA_DIM), lambda b, h, t: (b * nt + t, do_off // XA_DIM + h))],
        out_specs=[pl.BlockSpec((XA_TQ, XA_DIM), lambda b, h, t: (b * nt + t, h)), kspec(0), kspec(0)],
        out_shape=[
            jax.ShapeDtypeStruct((N_TOK, XA_HEADS * XA_DIM), BF16),
            jax.ShapeDtypeStruct((B_LOC * MEM_LEN, XA_HEADS * XA_DIM), F32),
            jax.ShapeDtypeStruct((B_LOC * MEM_LEN, XA_HEADS * XA_DIM), F32),
        ],
        compiler_params=_cp("parallel", "parallel", "arbitrary"),
    )(z, kv, kv, dcat)


def _tril(n):
    return lax.broadcasted_iota(jnp.int32, (n, n), 0) >= lax.broadcasted_iota(jnp.int32, (n, n), 1)


def _lower_bound(lbl):
    e = jnp.exp(lbl - jnp.max(lbl, axis=0, keepdims=True))
    p = e / jnp.sum(e, axis=0, keepdims=True)
    return p[0:1, :], p


def _hgrn_gates(zq, zf, lb, tril_f):
    sig = _sigmoid(zf)
    f = lb + (1.0 - lb) * sig
    kk = 1.0 - f
    sq = _sigmoid(zq)
    q = zq * sq
    b = jnp.dot(tril_f, jnp.log(f), preferred_element_type=F32, precision=lax.Precision.HIGHEST)
    bl = b[HG_CHUNK - 1:HG_CHUNK, :]
    return q, sq, sig, f, kk, b, bl


HG_TB = 512
HG_CPB = HG_TB // HG_CHUNK
HG_NT = SEQ // HG_TB
HG_WIDTH = HG_HEADS * HG_DIM


def _head(h, section=0):
    return slice(section * HG_WIDTH + h * HG_DIM, section * HG_WIDTH + (h + 1) * HG_DIM)


def _hgrn_fwd(z, o_mem, lb_logits, gnorm):
    def body(zq_ref, zf_ref, zi_ref, zg_ref, omem_ref, lbl_ref, gn_ref, o_ref, opre_ref, sall_ref, st_ref):
        lb, _ = _lower_bound(lbl_ref[...])
        gn = gn_ref[...]
        mask = _tril(HG_CHUNK)
        tril_f = mask.astype(F32)
        o_ref[:, HG_WIDTH:] = omem_ref[...]

        @pl.when(pl.program_id(1) == 0)
        def _():
            st_ref[...] = jnp.zeros_like(st_ref)

        def chunk(c, carry):
            rows = pl.ds(pl.multiple_of(c * HG_CHUNK, HG_CHUNK), HG_CHUNK)
            q, _, _, _, kk, b, bl = _hgrn_gates(zq_ref[rows, :], zf_ref[rows, :], lb, tril_f)
            v16 = zi_ref[rows, :].astype(BF16)
            qd16 = (q * jnp.exp(b)).astype(BF16)
            ki16 = (kk * jnp.exp(-b)).astype(BF16)
            kd16 = (kk * jnp.exp(bl - b)).astype(BF16)
            ebl = jnp.exp(bl)
            zg = zg_ref[rows, :]
            gate = zg * _sigmoid(zg)
            for h in range(HG_HEADS):
                sl = _head(h)
                a = jnp.where(mask, lax.dot_general(qd16[:, sl], ki16[:, sl], _NT, preferred_element_type=F32), 0.0)
                st = st_ref[h]
                sall_ref[0, h, c] = st
                o = jnp.dot(a.astype(BF16), v16[:, sl], preferred_element_type=F32) + lax.dot_general(
                    qd16[:, sl], st.astype(BF16), _NT, preferred_element_type=F32)
                st_ref[h] = st * ebl[:, sl] + lax.dot_general(v16[:, sl], kd16[:, sl], _TN, preferred_element_type=F32)
                opre_ref[rows, sl] = o
                r = lax.rsqrt(jnp.mean(o * o, axis=-1, keepdims=True) + EPS)
                o_ref[rows, sl] = ((o * r * gn) * gate[:, sl]).astype(BF16)
            return carry

        lax.fori_loop(0, HG_CPB, chunk, 0, unroll=4)

    zspec = lambda s: pl.BlockSpec((HG_TB, HG_WIDTH), lambda b, t: (b * HG_NT + t, s))
    return pl.pallas_call(
        body,
        name="hgrn_fwd",
        grid=(B_LOC, HG_NT),
        in_specs=[zspec(0), zspec(1), zspec(2), zspec(3), zspec(0),
                  pl.BlockSpec((3, HG_WIDTH), lambda b, t: (0, 0)), pl.BlockSpec((1, HG_DIM), lambda b, t: (0, 0))],
        out_specs=[pl.BlockSpec((HG_TB, 2 * HG_WIDTH), lambda b, t: (b * HG_NT + t, 0)), zspec(0),
                   pl.BlockSpec((1, HG_HEADS, HG_CPB, HG_DIM, HG_DIM), lambda b, t: (b, 0, t, 0, 0))],
        out_shape=[
            jax.ShapeDtypeStruct((N_TOK, 2 * HG_WIDTH), BF16),
            jax.ShapeDtypeStruct((N_TOK, HG_WIDTH), F32),
            jax.ShapeDtypeStruct((B_LOC, HG_HEADS, HG_NCHUNK, HG_DIM, HG_DIM), F32),
        ],
        scratch_shapes=[pltpu.VMEM((HG_HEADS, HG_DIM, HG_DIM), F32)],
        compiler_params=_cp("parallel", "arbitrary"),
    )(z, z, z, z, o_mem, lb_logits, gnorm)


def _hgrn_bwd(z, opre, dcat, dq_mem, sall, lb_logits, gnorm):
    def body(zq_ref, zf_ref, zi_ref, zg_ref, opre_ref, dout_ref, dqm_ref, sall_ref, lbl_ref, gn_ref,
             dz_ref, dlbl_ref, dgn_ref, dst_ref, dlb_ref, dgn_acc, db_ref, dkk_ref, dbl_ref):
        b_id, t_id = pl.program_id(0), pl.program_id(1)
        lb, p = _lower_bound(lbl_ref[...])
        gn = gn_ref[...]
        mask = _tril(HG_CHUNK)
        tril_f = mask.astype(F32)
        dz_ref[:, 4 * HG_WIDTH:] = dqm_ref[...]

        @pl.when(t_id == 0)
        def _():
            dst_ref[...] = jnp.zeros_like(dst_ref)
            dlb_ref[...] = jnp.zeros_like(dlb_ref)

        @pl.when((b_id == 0) & (t_id == 0))
        def _():
            dgn_acc[...] = jnp.zeros_like(dgn_acc)

        def chunk(i, carry):
            c = HG_CPB - 1 - i
            rows = pl.ds(pl.multiple_of(c * HG_CHUNK, HG_CHUNK), HG_CHUNK)
            zq, zg = zq_ref[rows, :], zg_ref[rows, :]
            q, sq, sig, f, kk, b, bl = _hgrn_gates(zq, zf_ref[rows, :], lb, tril_f)
            v16 = zi_ref[rows, :].astype(BF16)
            eb, enb, ebl_b, ebl = jnp.exp(b), jnp.exp(-b), jnp.exp(bl - b), jnp.exp(bl)
            qd, ki, kd = q * eb, kk * enb, kk * ebl_b
            qd16, ki16, kd16 = qd.astype(BF16), ki.astype(BF16), kd.astype(BF16)
            o_all = opre_ref[rows, :]
            dout = dout_ref[rows, :]
            sg = _sigmoid(zg)
            d_on_all = dout * (zg * sg)
            dgate = dout * (sg * (1.0 + zg * (1.0 - sg)))
            dq_scale = eb * (sq * (1.0 + zq * (1.0 - sq)))
            for h in range(HG_HEADS):
                sl = _head(h)
                o = o_all[:, sl]
                r = lax.rsqrt(jnp.mean(o * o, axis=-1, keepdims=True) + EPS)
                ohat = o * r
                d_on = d_on_all[:, sl]
                dz_ref[rows, _head(h, 3)] = (dgate[:, sl] * (ohat * gn)).astype(BF16)
                dgn_acc[...] += jnp.sum(d_on * ohat, axis=0, keepdims=True)
                dohat = d_on * gn
                do16 = (r * (dohat - ohat * jnp.mean(dohat * ohat, axis=-1, keepdims=True))).astype(BF16)
                st = sall_ref[0, h, c]
                dst = dst_ref[h]
                st16, dst16 = st.astype(BF16), dst.astype(BF16)
                qd_h, ki_h, kd_h, v_h = qd16[:, sl], ki16[:, sl], kd16[:, sl], v16[:, sl]
                a16 = jnp.where(mask, lax.dot_general(qd_h, ki_h, _NT, preferred_element_type=F32), 0.0).astype(BF16)
                da16 = jnp.where(mask, lax.dot_general(do16, v_h, _NT, preferred_element_type=F32), 0.0).astype(BF16)
                dv = lax.dot_general(a16, do16, _TN, preferred_element_type=F32) + lax.dot_general(
                    kd_h, dst16, _NT, preferred_element_type=F32)
                dqd = jnp.dot(da16, ki_h, preferred_element_type=F32) + jnp.dot(do16, st16, preferred_element_type=F32)
                dki = lax.dot_general(da16, qd_h, _TN, preferred_element_type=F32)
                dkd = jnp.dot(v_h, dst16, preferred_element_type=F32)
                dbl_ref[:, sl] = jnp.sum(dkd * kd[:, sl], axis=0, keepdims=True) + ebl[:, sl] * jnp.sum(
                    st * dst, axis=0, keepdims=True)
                dst_ref[h] = dst * ebl[:, sl] + lax.dot_general(do16, qd_h, _TN, preferred_element_type=F32)
                dz_ref[rows, _head(h, 2)] = dv.astype(BF16)
                dz_ref[rows, sl] = (dqd * dq_scale[:, sl]).astype(BF16)
                dkk_ref[:, sl] = dki * enb[:, sl] + dkd * ebl_b[:, sl]
                db_ref[:, sl] = dqd * qd[:, sl] - dki * ki[:, sl] - dkd * kd[:, sl]
            dlogf = lax.dot_general(tril_f, db_ref[...], _TN, preferred_element_type=F32,
                                    precision=lax.Precision.HIGHEST) + dbl_ref[...]
            df = dlogf / f - dkk_ref[...]
            dz_ref[rows, HG_WIDTH:2 * HG_WIDTH] = (df * (1.0 - lb) * sig * (1.0 - sig)).astype(BF16)
            dlb_ref[...] += jnp.sum(df * (1.0 - sig), axis=0, keepdims=True)
            return carry

        lax.fori_loop(0, HG_CPB, chunk, 0, unroll=4)

        @pl.when(t_id == HG_NT - 1)
        def _():
            row0 = (lax.broadcasted_iota(jnp.int32, (3, HG_WIDTH), 0) == 0).astype(F32)
            dlbl_part = dlb_ref[...] * lb * (row0 - p)

            @pl.when(b_id == 0)
            def _():
                dlbl_ref[...] = dlbl_part

            @pl.when(b_id > 0)
            def _():
                dlbl_ref[...] += dlbl_part

            dgn_ref[...] = dgn_acc[...]

    rev = lambda b, t: b * HG_NT + HG_NT - 1 - t
    zspec = lambda s: pl.BlockSpec((HG_TB, HG_WIDTH), lambda b, t: (rev(b, t), s))
    return pl.pallas_call(
        body,
        name="hgrn_bwd",
        grid=(B_LOC, HG_NT),
        in_specs=[zspec(0), zspec(1), zspec(2), zspec(3), zspec(0), zspec(0), zspec(0),
                  pl.BlockSpec((1, HG_HEADS, HG_CPB, HG_DIM, HG_DIM), lambda b, t: (b, 0, HG_NT - 1 - t, 0, 0)),
                  pl.BlockSpec((3, HG_WIDTH), lambda b, t: (0, 0)), pl.BlockSpec((1, HG_DIM), lambda b, t: (0, 0))],
        out_specs=[pl.BlockSpec((HG_TB, 5 * HG_WIDTH), lambda b, t: (rev(b, t), 0)),
                   pl.BlockSpec((3, HG_WIDTH), lambda b, t: (0, 0)), pl.BlockSpec((1, HG_DIM), lambda b, t: (0, 0))],
        out_shape=[jax.ShapeDtypeStruct((N_TOK, 5 * HG_WIDTH), BF16),
                   jax.ShapeDtypeStruct((3, HG_WIDTH), F32), jax.ShapeDtypeStruct((1, HG_DIM), F32)],
        scratch_shapes=[pltpu.VMEM((HG_HEADS, HG_DIM, HG_DIM), F32), pltpu.VMEM((1, HG_WIDTH), F32),
                        pltpu.VMEM((1, HG_DIM), F32), pltpu.VMEM((HG_CHUNK, HG_WIDTH), F32),
                        pltpu.VMEM((HG_CHUNK, HG_WIDTH), F32), pltpu.VMEM((1, HG_WIDTH), F32)],
        compiler_params=_cp("arbitrary", "arbitrary"),
    )(z, z, z, z, opre, dcat, dq_mem, sall, lb_logits, gnorm)


GM_TM = 256


def _gmlp_norm(zv, ln_g, ln_b):
    gv, dgelu = _gelu_parts(zv)
    xc = gv - jnp.mean(gv, axis=-1, keepdims=True)
    rstd = lax.rsqrt(jnp.mean(xc * xc, axis=-1, keepdims=True) + EPS)
    vhat = xc * rstd
    return vhat * ln_g + ln_b, vhat, rstd, dgelu


def _gmlp_specs():
    half = lambda j: pl.BlockSpec((GM_TM, GM_WIDTH), lambda i: (i, j))
    vec = pl.BlockSpec((1, GM_WIDTH), lambda i: (0, 0))
    w = pl.BlockSpec((GM_GROUPS, GM_CHUNK, GM_CHUNK), lambda i: (0, 0, 0))
    bt = pl.BlockSpec((GM_CHUNK, GM_GROUPS), lambda i: (0, 0))
    return half, vec, w, bt


def _gmlp_fwd(z, o_mem, ln_g, ln_b, w_s, b_st):
    def body(zu_ref, zv_ref, omem_ref, g_ref, b_ref, w_ref, bt_ref, o_ref):
        o_ref[:, GM_WIDTH:] = omem_ref[...]
        u, _ = _gelu_parts(zu_ref[...])
        v, _, _, _ = _gmlp_norm(zv_ref[...], g_ref[...], b_ref[...])
        v16 = v.astype(BF16)
        mask = _tril(GM_CHUNK)
        bt = bt_ref[...]
        for g in range(GM_GROUPS):
            wm16 = jnp.where(mask, w_ref[g], 0.0).astype(BF16)
            cols = slice(g * GM_GDIM, (g + 1) * GM_GDIM)
            for c in range(GM_TM // GM_CHUNK):
                rows = slice(c * GM_CHUNK, (c + 1) * GM_CHUNK)
                mixed = jnp.dot(wm16, v16[rows, cols], preferred_element_type=F32) + bt[:, g:g + 1]
                o_ref[rows, cols] = (u[rows, cols] * mixed).astype(BF16)

    half, vec, w, bt = _gmlp_specs()
    return pl.pallas_call(
        body,
        name="gmlp_fwd",
        grid=(N_TOK // GM_TM,),
        in_specs=[half(0), half(1), pl.BlockSpec((GM_TM, XA_HEADS * XA_DIM), lambda i: (i, 0)), vec, vec, w, bt],
        out_specs=pl.BlockSpec((GM_TM, GM_WIDTH + XA_HEADS * XA_DIM), lambda i: (i, 0)),
        out_shape=jax.ShapeDtypeStruct((N_TOK, GM_WIDTH + XA_HEADS * XA_DIM), BF16),
        compiler_params=_cp("parallel"),
    )(z, z, o_mem, ln_g, ln_b, w_s, b_st)


def _gmlp_bwd(z, dcat, dq_mem, ln_g, ln_b, w_s, b_st):
    def body(zu_ref, zv_ref, dout_ref, dqm_ref, g_ref, b_ref, w_ref, bt_ref,
             dz_ref, dw_ref, dbt_ref, dg_ref, db_ref, dv_ref):
        dz_ref[:, 2 * GM_WIDTH:] = dqm_ref[...]
        @pl.when(pl.program_id(0) == 0)
        def _():
            dw_ref[...] = jnp.zeros_like(dw_ref)
            dbt_ref[...] = jnp.zeros_like(dbt_ref)
            dg_ref[...] = jnp.zeros_like(dg_ref)
            db_ref[...] = jnp.zeros_like(db_ref)

        zu = zu_ref[...]
        u, du_dz = _gelu_parts(zu)
        ln_g = g_ref[...]
        v, vhat, rstd, dgv_dz = _gmlp_norm(zv_ref[...], ln_g, b_ref[...])
        v16 = v.astype(BF16)
        dout = dout_ref[...]
        dmixed = dout * u
        dm16 = dmixed.astype(BF16)
        mask = _tril(GM_CHUNK)
        bt = bt_ref[...]
        group_id = lax.broadcasted_iota(jnp.int32, (1, GM_GROUPS), 1)
        dbt = jnp.zeros((GM_CHUNK, GM_GROUPS), F32)
        for g in range(GM_GROUPS):
            wm16 = jnp.where(mask, w_ref[g], 0.0).astype(BF16)
            cols = slice(g * GM_GDIM, (g + 1) * GM_GDIM)
            dw = jnp.zeros((GM_CHUNK, GM_CHUNK), F32)
            dbt_g = jnp.zeros((GM_CHUNK, 1), F32)
            for c in range(GM_TM // GM_CHUNK):
                rows = slice(c * GM_CHUNK, (c + 1) * GM_CHUNK)
                mixed = jnp.dot(wm16, v16[rows, cols], preferred_element_type=F32) + bt[:, g:g + 1]
                dz_ref[rows, cols] = (dout[rows, cols] * mixed * du_dz[rows, cols]).astype(BF16)
                dw += lax.dot_general(dm16[rows, cols], v16[rows, cols], _NT, preferred_element_type=F32)
                dbt_g += jnp.sum(dmixed[rows, cols], axis=-1, keepdims=True)
                dv_ref[rows, cols] = lax.dot_general(wm16, dm16[rows, cols], _TN, preferred_element_type=F32)
            dw_ref[g] += jnp.where(mask, dw, 0.0)
            dbt = dbt + dbt_g * (group_id == g).astype(F32)
        dbt_ref[...] += dbt
        dv = dv_ref[...]
        dg_ref[...] += jnp.sum(dv * vhat, axis=0, keepdims=True)
        db_ref[...] += jnp.sum(dv, axis=0, keepdims=True)
        dvh = dv * ln_g
        dgv = rstd * (dvh - jnp.mean(dvh, axis=-1, keepdims=True) - vhat * jnp.mean(dvh * vhat, axis=-1, keepdims=True))
        dz_ref[:, GM_WIDTH:2 * GM_WIDTH] = (dgv * dgv_dz).astype(BF16)

    half, vec, w, bt = _gmlp_specs()
    dz_width = 2 * GM_WIDTH + XA_HEADS * XA_DIM
    return pl.pallas_call(
        body,
        name="gmlp_bwd",
        grid=(N_TOK // GM_TM,),
        in_specs=[half(0), half(1), half(0), pl.BlockSpec((GM_TM, XA_HEADS * XA_DIM), lambda i: (i, 0)), vec, vec, w, bt],
        out_specs=[pl.BlockSpec((GM_TM, dz_width), lambda i: (i, 0)), w, bt, vec, vec],
        out_shape=[jax.ShapeDtypeStruct((N_TOK, dz_width), BF16),
                   jax.ShapeDtypeStruct((GM_GROUPS, GM_CHUNK, GM_CHUNK), F32),
                   jax.ShapeDtypeStruct((GM_CHUNK, GM_GROUPS), F32),
                   jax.ShapeDtypeStruct((1, GM_WIDTH), F32), jax.ShapeDtypeStruct((1, GM_WIDTH), F32)],
        scratch_shapes=[pltpu.VMEM((GM_TM, GM_WIDTH), F32)],
        compiler_params=_cp("arbitrary"),
    )(z, z, dcat, dq_mem, ln_g, ln_b, w_s, b_st)


def _own_slot(shape):
    return pl.BlockSpec((None,) + tuple(shape), lambda i, me_ref: (me_ref[0],) + (0,) * len(shape))


def _place_rows(w, layer, cuts_columns, me, *, name, deps=()):
    _, r, c = w.shape
    n = c if cuts_columns else r

    def body(me_ref, w_ref, *rest):
        o_ref = rest[len(deps)]
        wv = w_ref[...]
        o_ref[...] = (wv.T if cuts_columns else wv).astype(BF16)

    return pl.pallas_call(
        body,
        name=name,
        grid_spec=pltpu.PrefetchScalarGridSpec(
            num_scalar_prefetch=1, grid=(1,),
            in_specs=[pl.BlockSpec((None, r, c), lambda i, me_ref: (layer, 0, 0))] + [ANY_SPEC] * len(deps),
            out_specs=_own_slot((n, D_MODEL))),
        out_shape=jax.ShapeDtypeStruct((N_DEV, n, D_MODEL), BF16),
        compiler_params=_cp("arbitrary"),
    )(me, w, *deps)


def _place_ln(ln_g, ln_b, me):
    blk = ln_g.shape[1]

    def body(me_ref, g_ref, b_ref, o_ref):
        o_ref[...] = jnp.zeros_like(o_ref)
        o_ref[0:1, :] = g_ref[...]
        o_ref[1:2, :] = b_ref[...]

    vec = pl.BlockSpec((1, blk), lambda i, me_ref: (0, 0))
    return pl.pallas_call(
        body,
        name="place_ln",
        grid_spec=pltpu.PrefetchScalarGridSpec(
            num_scalar_prefetch=1, grid=(1,), in_specs=[vec, vec], out_specs=_own_slot((8, blk))),
        out_shape=jax.ShapeDtypeStruct((N_DEV, 8, blk), F32),
        compiler_params=_cp("arbitrary"),
    )(me, ln_g, ln_b)


def _place_slab(a, me, *, name):
    def body(me_ref, a_ref, o_ref):
        o_ref[...] = a_ref[...]

    return pl.pallas_call(
        body,
        name=name,
        grid_spec=pltpu.PrefetchScalarGridSpec(
            num_scalar_prefetch=1, grid=(1,),
            in_specs=[pl.BlockSpec(a.shape, lambda i, me_ref: (0, 0))], out_specs=_own_slot(a.shape)),
        out_shape=jax.ShapeDtypeStruct((N_DEV,) + a.shape, a.dtype),
        compiler_params=_cp("arbitrary"),
    )(me, a)


def _place_own(grads, me, *, name):
    k = len(grads)

    def body(me_ref, *refs):
        for src, dst in zip(refs[:k], refs[k:]):
            dst[...] = src[...]

    specs = [_own_slot(g.shape[1:]) for g in grads]
    return pl.pallas_call(
        body,
        name=name,
        grid_spec=pltpu.PrefetchScalarGridSpec(num_scalar_prefetch=1, grid=(1,), in_specs=specs, out_specs=specs),
        out_shape=[jax.ShapeDtypeStruct(g.shape, g.dtype) for g in grads],
        compiler_params=_cp("arbitrary"),
    )(me, *grads)


def _mesh_pos():
    x, y, c = (lax.axis_index(a) for a in MESH_AXES)
    return x, y, c, 4 * x + 2 * y + c


def _peer(x, y, c, r):
    px = 1 - x if r & 4 else x
    py = 1 - y if r & 2 else y
    pc = 1 - c if r & 1 else c
    return (px, py, pc), 4 * px + 2 * py + pc


RELATIONS = {"scatter": (1, 2, 3, 4, 5, 6, 7), "gather_all": (1, 2, 3, 4, 5, 6, 7), "gather_chips": (1, 2, 4, 6),
             "gather_sibling": (2, 4, 6)}


def _peer_copies(srcs, lands, send_sems, recv_sems, mode, waits):
    x, y, c, me = _mesh_pos()
    rel = RELATIONS[mode]
    pairs = []
    for ri, r in enumerate(rel):
        if mode == "gather_sibling":
            peer, _ = _peer(x, y, c, 1)
            _, sent_blk = _peer(x, y, c, r)
            _, got_blk = _peer(x, y, c, r ^ 1)
        else:
            peer, peer_blk = _peer(x, y, c, r)
            sent_blk, got_blk = (peer_blk if mode == "scatter" else me), peer_blk
        for k, (src, land) in enumerate(zip(srcs, lands)):
            idx = k * len(rel) + ri
            sems = dict(send_sem=send_sems.at[idx], recv_sem=recv_sems.at[idx], device_id=peer,
                        device_id_type=pl.DeviceIdType.MESH)
            dst_blk = sent_blk if mode == "gather_sibling" else me
            mine = pltpu.make_async_remote_copy(src_ref=src.at[sent_blk], dst_ref=land.at[dst_blk], **sems)
            theirs = pltpu.make_async_remote_copy(src_ref=src.at[sent_blk], dst_ref=land.at[got_blk], **sems) if waits else None
            pairs.append((mine, theirs))
    return pairs


DATAFLOW = pltpu.SideEffectType.DATAFLOW_SIDE_EFFECTING


def _in_hbm(a):
    return pltpu.with_memory_space_constraint(a, pltpu.HBM)


def _copies_start(srcs, lands, *, mode, name, deps=()):
    gather = mode != "scatter"
    arrs = list(lands) if gather else list(srcs) + list(lands)
    n, k, nd = len(arrs), len(lands), len(deps)

    def body(*refs):
        ins, send_sems, recv_sems, token = refs[:n], refs[n + nd], refs[n + nd + 1], refs[2 * n + nd + 2]
        src_refs, land_refs = (ins, ins) if gather else (ins[:k], ins[k:])
        for mine, _ in _peer_copies(src_refs, land_refs, send_sems, recv_sems, mode, waits=False):
            mine.start()
        token[...] = jnp.zeros_like(token)

    n_cp = k * len(RELATIONS[mode])
    return pl.pallas_call(
        body,
        name=name,
        in_specs=[HBM_SPEC] * n + [ANY_SPEC] * nd,
        out_specs=(SEM_SPEC, SEM_SPEC, *[HBM_SPEC] * n, pl.BlockSpec(memory_space=pltpu.VMEM)),
        out_shape=(pltpu.SemaphoreType.DMA((n_cp,)), pltpu.SemaphoreType.DMA((n_cp,)),
                   *[pltpu.HBM(a.shape, a.dtype) for a in arrs], jax.ShapeDtypeStruct((8, 128), F32)),
        input_output_aliases={i: 2 + i for i in range(n)},
        compiler_params=pltpu.CompilerParams(has_side_effects=DATAFLOW),
    )(*[_in_hbm(a) for a in arrs], *deps)


def _copies_wait(arrs, send_sems, recv_sems, after, *, n_lands, mode, name):
    n, k = len(arrs), n_lands
    gather = mode != "scatter"

    def body(*refs):
        ins, send_sems, recv_sems = refs[:n], refs[n], refs[n + 1]
        src_refs, land_refs = (ins, ins) if gather else (ins[:k], ins[k:])
        for mine, theirs in _peer_copies(src_refs, land_refs, send_sems, recv_sems, mode, waits=True):
            mine.wait_send()
            theirs.wait_recv()

    outs = pl.pallas_call(
        body,
        name=name,
        in_specs=[HBM_SPEC] * n + [SEM_SPEC, SEM_SPEC] + [ANY_SPEC] * len(after),
        out_specs=[HBM_SPEC] * n,
        out_shape=[pltpu.HBM(a.shape, a.dtype) for a in arrs],
        input_output_aliases={i: i for i in range(n)},
        compiler_params=pltpu.CompilerParams(has_side_effects=DATAFLOW),
    )(*arrs, send_sems, recv_sems, *after)
    return outs[n - k:]


def _adamw(w, g, m, v):
    m = ADAM_B1 * m + (1.0 - ADAM_B1) * g
    v = ADAM_B2 * v + (1.0 - ADAM_B2) * (g * g)
    m_hat = m / (1.0 - ADAM_B1 ** ADAM_STEP)
    v_hat = v / (1.0 - ADAM_B2 ** ADAM_STEP)
    return -ADAM_LR * (m_hat / (jnp.sqrt(v_hat) + ADAM_EPS) + ADAM_WD * w), m, v


ADAM_TC = 256


def _adam_big(slots, w, m, v, cuts_columns, *, name):
    layers, n, nj = len(slots), slots[0].shape[1], D_MODEL // ADAM_TC

    def body(*refs):
        s_refs = refs[:layers]
        w_ref, m_ref, v_ref, g_ref, d_ref, nm_ref, nv_ref, acc_ref = refs[layers:]
        for ll in range(layers):
            @pl.when(pl.program_id(0) == ll)
            def _(s_ref=s_refs[ll]):
                g = s_ref[0].astype(F32)
                for s in range(1, N_DEV):
                    g = g + s_ref[s].astype(F32)
                acc_ref[...] = g

        g = acc_ref[...].T if cuts_columns else acc_ref[...]
        g_ref[...] = g
        d_ref[...], nm_ref[...], nv_ref[...] = _adamw(w_ref[...], g, m_ref[...], v_ref[...])

    def slot_spec(ll):
        return pl.BlockSpec((N_DEV, n, ADAM_TC),
                            lambda l, j: (0, 0, jnp.where(l < ll, 0, jnp.where(l > ll, nj - 1, j))))

    if cuts_columns:
        w_spec = pl.BlockSpec((None, ADAM_TC, n), lambda l, j: (l, j, 0))
    else:
        w_spec = pl.BlockSpec((None, n, ADAM_TC), lambda l, j: (l, 0, j))
    return pl.pallas_call(
        body,
        name=name,
        grid=(layers, nj),
        in_specs=[slot_spec(ll) for ll in range(layers)] + [w_spec] * 3,
        out_specs=[w_spec] * 4,
        out_shape=[jax.ShapeDtypeStruct(w.shape, F32)] * 4,
        scratch_shapes=[pltpu.VMEM((n, ADAM_TC), F32)],
        compiler_params=_cp("arbitrary", "arbitrary"),
    )(*slots, w, m, v)


def _adam_slabs(slots, ws, ms, vs):
    n = len(slots)

    def body(*refs):
        ins, outs = refs[:4 * n], refs[4 * n:]
        for k in range(n):
            s_ref, w_ref, m_ref, v_ref = ins[k], ins[n + k], ins[2 * n + k], ins[3 * n + k]
            g = s_ref[0]
            for s in range(1, N_DEV):
                g = g + s_ref[s]
            outs[4 * k][...] = g
            outs[4 * k + 1][...], outs[4 * k + 2][...], outs[4 * k + 3][...] = _adamw(w_ref[...], g, m_ref[...], v_ref[...])

    res = pl.pallas_call(
        body,
        name="small_adamw",
        out_shape=[jax.ShapeDtypeStruct(w.shape, F32) for w in ws for _ in range(4)],
        compiler_params=pltpu.CompilerParams(vmem_limit_bytes=VMEM_LIMIT_BYTES),
    )(*slots, *ws, *ms, *vs)
    return [res[4 * k:4 * k + 4] for k in range(n)]


def _adam_vecs(gs, ws, ms, vs):
    n = len(gs)

    def body(*refs):
        ins, outs = refs[:4 * n], refs[4 * n:]
        for k in range(n):
            outs[3 * k][...], outs[3 * k + 1][...], outs[3 * k + 2][...] = _adamw(
                ins[n + k][...], ins[k][...], ins[2 * n + k][...], ins[3 * n + k][...])

    res = pl.pallas_call(
        body,
        name="ln_adamw",
        out_shape=[jax.ShapeDtypeStruct(w.shape, F32) for w in ws for _ in range(3)],
        compiler_params=pltpu.CompilerParams(vmem_limit_bytes=VMEM_LIMIT_BYTES),
    )(*gs, *ws, *ms, *vs)
    return [res[3 * k:3 * k + 3] for k in range(n)]


SLAB_AT = dict(mem_norm=0, lb_logits=1, ffn1_norm=4, mix_norm=6, hgrn_gnorm=8, gmlp_ln_g=9, gmlp_ln_b=11,
               gmlp_b_s=13, ffn2_norm=14, final_norm=16)
SLAB_ROWS = 24
LOSS_ROW = 17
SMALL_SHARDED = ("gmlp_ln_g", "gmlp_ln_b")


def _pack_slab(parts, *, name, deps=()):
    flat, plan = [], []
    for pname, at in SLAB_AT.items():
        for a in parts.get(pname, ()):
            flat.append(a)
            plan.append((at, a.shape))
            at += max(1, a.shape[0] * a.shape[1] // D_MODEL)
    for a in parts.get("loss", ()):
        flat.append(a)
        plan.append((LOSS_ROW, a.shape))

    def body(*refs):
        o_ref = refs[-1]
        o_ref[...] = jnp.zeros_like(o_ref)
        for ref, (at, (r, w)) in zip(refs, plan):
            if w == D_MODEL or r == 1 and w < D_MODEL:
                o_ref[at:at + r, 0:w] = ref[...]
            elif w < D_MODEL:
                for j in range(r):
                    o_ref[at:at + 1, j * w:(j + 1) * w] = ref[j:j + 1, :]
            else:
                for j in range(w // D_MODEL):
                    o_ref[at + j:at + j + 1, :] = ref[:, j * D_MODEL:(j + 1) * D_MODEL]

    return pl.pallas_call(
        body,
        name=name,
        in_specs=[pl.BlockSpec(memory_space=pltpu.VMEM)] * len(flat) + [ANY_SPEC] * len(deps),
        out_shape=jax.ShapeDtypeStruct((SLAB_ROWS, D_MODEL), F32),
        compiler_params=pltpu.CompilerParams(vmem_limit_bytes=VMEM_LIMIT_BYTES),
    )(*flat, *deps)


def _unpack_slab(slab, shapes):
    out = {}
    for pname, at in SLAB_AT.items():
        if pname in SMALL_SHARDED:
            continue
        size = math.prod(shapes[pname])
        rows = max(1, size // D_MODEL)
        out[pname] = slab[at:at + rows].reshape(-1)[:size].reshape(shapes[pname])
    return out


def _ffn_fwd(x, norm_g, block, layer, full, get_weights):
    tag = f"l{layer}_{block}"
    full.update(get_weights((layer, f"{block}_in"), (x,)))
    h, z, act = _norm_mm(x, norm_g, full[(f"{block}_w_in", layer)], swiglu=True, tm=512, tn=1408, deps=full.pop("deps", ()),
                         name=f"{tag}_in")
    full.update(get_weights((layer, f"{block}_out"), (act,)))
    y = _mm(act, full[(f"{block}_w_out", layer)], tm=1024, tn=D_MODEL, tk=D_FF, out_dtype=F32, res=x, scale=0.5,
            deps=full.pop("deps", ()), name=f"{tag}_out")
    return y, (x, h, z, act)


def _ffn_bwd(dy, dy16, saved, norm_g, w_in_t, w_out, tag, deps=(), after_out_wgrad=None, before_in_wgrad=None):
    x, h, z, act = saved
    dw_out = _mm(act, dy16, ta=True, tm=1408, tn=D_MODEL, tk=N_TOK, out_dtype=BF16, scale=0.5, deps=deps,
                 name=f"{tag}_out_wgrad")
    sent = after_out_wgrad(dw_out) if after_out_wgrad is not None else ()
    dz = _swiglu_dgrad(dy16, w_out, z, scale=0.5, deps=sent, name=f"{tag}_out_dgrad")
    if before_in_wgrad is None:
        dw_in_t = _planes_wgrad(dz, h, name=f"{tag}_in_wgrad")
        dx, dx16, dg = _dgrad_norm_bwd(dz, w_in_t, x, norm_g, dy, name=f"{tag}_in_dgrad")
    else:
        dx, dx16, dg = _dgrad_norm_bwd(dz, w_in_t, x, norm_g, dy, name=f"{tag}_in_dgrad")
        dw_in_t = _planes_wgrad(dz, h, deps=before_in_wgrad(dg), name=f"{tag}_in_wgrad")
    return dx, dx16, dg, dw_in_t, dw_out


def kernel(x, mem, mem_norm, lb_logits, ffn1_norm, ffn1_w_in, ffn1_w_out, mix_norm, mem_w_kv, hgrn_w_in, hgrn_gnorm, hgrn_w_out, gmlp_w_in, gmlp_ln_g, gmlp_ln_b, gmlp_w_s, gmlp_b_s, gmlp_w_out, ffn2_norm, ffn2_w_in, ffn2_w_out, final_norm, loss_target, m_mem_norm, m_lb_logits, m_ffn1_norm, m_ffn1_w_in, m_ffn1_w_out, m_mix_norm, m_mem_w_kv, m_hgrn_w_in, m_hgrn_gnorm, m_hgrn_w_out, m_gmlp_w_in, m_gmlp_ln_g, m_gmlp_ln_b, m_gmlp_w_s, m_gmlp_b_s, m_gmlp_w_out, m_ffn2_norm, m_ffn2_w_in, m_ffn2_w_out, m_final_norm, v_mem_norm, v_lb_logits, v_ffn1_norm, v_ffn1_w_in, v_ffn1_w_out, v_mix_norm, v_mem_w_kv, v_hgrn_w_in, v_hgrn_gnorm, v_hgrn_w_out, v_gmlp_w_in, v_gmlp_ln_g, v_gmlp_ln_b, v_gmlp_w_s, v_gmlp_b_s, v_gmlp_w_out, v_ffn2_norm, v_ffn2_w_in, v_ffn2_w_out, v_final_norm):
    weights = dict(mem_norm=mem_norm, lb_logits=lb_logits, ffn1_norm=ffn1_norm, ffn1_w_in=ffn1_w_in, ffn1_w_out=ffn1_w_out, mix_norm=mix_norm, mem_w_kv=mem_w_kv, hgrn_w_in=hgrn_w_in, hgrn_gnorm=hgrn_gnorm, hgrn_w_out=hgrn_w_out, gmlp_w_in=gmlp_w_in, gmlp_ln_g=gmlp_ln_g, gmlp_ln_b=gmlp_ln_b, gmlp_w_s=gmlp_w_s, gmlp_b_s=gmlp_b_s, gmlp_w_out=gmlp_w_out, ffn2_norm=ffn2_norm, ffn2_w_in=ffn2_w_in, ffn2_w_out=ffn2_w_out, final_norm=final_norm)
    mom_m = dict(mem_norm=m_mem_norm, lb_logits=m_lb_logits, ffn1_norm=m_ffn1_norm, ffn1_w_in=m_ffn1_w_in, ffn1_w_out=m_ffn1_w_out, mix_norm=m_mix_norm, mem_w_kv=m_mem_w_kv, hgrn_w_in=m_hgrn_w_in, hgrn_gnorm=m_hgrn_gnorm, hgrn_w_out=m_hgrn_w_out, gmlp_w_in=m_gmlp_w_in, gmlp_ln_g=m_gmlp_ln_g, gmlp_ln_b=m_gmlp_ln_b, gmlp_w_s=m_gmlp_w_s, gmlp_b_s=m_gmlp_b_s, gmlp_w_out=m_gmlp_w_out, ffn2_norm=m_ffn2_norm, ffn2_w_in=m_ffn2_w_in, ffn2_w_out=m_ffn2_w_out, final_norm=m_final_norm)
    mom_v = dict(mem_norm=v_mem_norm, lb_logits=v_lb_logits, ffn1_norm=v_ffn1_norm, ffn1_w_in=v_ffn1_w_in, ffn1_w_out=v_ffn1_w_out, mix_norm=v_mix_norm, mem_w_kv=v_mem_w_kv, hgrn_w_in=v_hgrn_w_in, hgrn_gnorm=v_hgrn_gnorm, hgrn_w_out=v_hgrn_w_out, gmlp_w_in=v_gmlp_w_in, gmlp_ln_g=v_gmlp_ln_g, gmlp_ln_b=v_gmlp_ln_b, gmlp_w_s=v_gmlp_w_s, gmlp_b_s=v_gmlp_b_s, gmlp_w_out=v_gmlp_w_out, ffn2_norm=v_ffn2_norm, ffn2_w_in=v_ffn2_w_in, ffn2_w_out=v_ffn2_w_out, final_norm=v_final_norm)
    order = list(weights)
    _, _, _, me = _mesh_pos()
    me_arr = jnp.reshape(me, (1,)).astype(jnp.int32)
    cuts = {name: c for name, c, _, _ in GROUPS}
    rows_already = tuple(name for name, c, _, n in GROUPS if c and n % 128)
    as_rows = lambda a: jnp.transpose(a, (0, 2, 1))
    for name in rows_already:
        weights[name], mom_m[name], mom_v[name] = as_rows(weights[name]), as_rows(mom_m[name]), as_rows(mom_v[name])
        cuts[name] = False

    mix1 = (("mem_w_kv", 1), ("gmlp_w_in", 0), ("gmlp_w_out", 0))
    gather_plan = (
        ((0, "ffn1_in"), _stage_pieces(0, "ffn1")),
        ((0, "mix_in"), _stage_pieces(0, "mix")),
        ((0, "ffn2_in"), _stage_pieces(0, "ffn2")),
        ((1, "ffn1_in"), _stage_pieces(1, "ffn1")),
        ((1, "mix_in"), mix1),
        ((1, "ffn2_in"), _stage_pieces(1, "ffn2")),
    )
    stage_of = {use: k for k, (use, _) in enumerate(gather_plan)}
    in_flight = {}

    def place(k, deps=()):
        pieces = gather_plan[k][1]
        lands = [_place_rows(weights[name], l, cuts[name], me_arr, deps=deps, name=f"place_{name}_{l}")
                 for name, l in pieces]
        if pieces is mix1:
            lands.append(_place_ln(gmlp_ln_g, gmlp_ln_b, me_arr))
        return lands

    placed = {0: place(0)}

    def start_chips(k, deps):
        lands = placed[k]
        send_sems, recv_sems, *thru, token = _copies_start(lands, lands, mode="gather_chips", deps=deps,
                                                           name=f"gather{k}_chips_start")
        in_flight[k] = (thru, send_sems, recv_sems)
        return token

    def pass_to_sibling(k, after):
        thru, send_sems, recv_sems = in_flight[k]
        outs = _copies_wait(thru, send_sems, recv_sems, after, n_lands=len(thru), mode="gather_chips",
                            name=f"gather{k}_chips_wait")
        send_sems, recv_sems, *thru, token = _copies_start(outs, outs, mode="gather_sibling",
                                                           name=f"gather{k}_sibling_start")
        in_flight[k] = (thru, send_sems, recv_sems)
        return token, token

    first_sent = start_chips(0, ())
    placed.update({k: place(k, (first_sent,)) for k in range(1, len(gather_plan))})
    placed_later = tuple(a for k in range(1, len(gather_plan)) for a in placed[k])
    points = [(i, p) for i in (0, 1) for p in ("ffn1_in", "ffn1_out", "mix_in", "mix_out", "ffn2_in", "ffn2_out")]
    pass_at = {j: points[points.index(use) - 1] for j, (use, _) in enumerate(gather_plan) if j}
    pass_at[1] = gather_plan[1][0]

    started = {0}

    def get_weights(use, after):
        tokens, w = [], {}
        k = stage_of.get(use)

        def pass_on(j, after):
            token, landed = pass_to_sibling(j, after)
            tokens.append(token)
            if j + 1 < len(gather_plan) and j + 1 not in started:
                started.add(j + 1)
                tokens.append(start_chips(j + 1, (landed,)))

        if k == 0:
            pass_on(0, tuple(after) + placed_later)
        elif k is not None and pass_at[k] == use:
            pass_on(k, after)
        if k is not None:
            thru, send_sems, recv_sems = in_flight[k]
            outs = _copies_wait(thru, send_sems, recv_sems, after, n_lands=len(thru), mode="gather_sibling",
                                name=f"gather{k}_sibling_wait")
            after = (outs[0],)
            pieces = gather_plan[k][1]
            w = {p: o.reshape(N_DEV * o.shape[1], D_MODEL) for p, o in zip(pieces, outs)}
            if pieces is mix1:
                w["ln_g"] = outs[-1][:, 0, :].reshape(1, GM_WIDTH)
                w["ln_b"] = outs[-1][:, 1, :].reshape(1, GM_WIDTH)
        for j, at in pass_at.items():
            if at == use and j != k:
                pass_on(j, after)
        w["deps"] = tuple(tokens)
        return w

    scatter = {}

    def put_grads(st, grads):
        if st in ("w_s", "small"):
            slab = grads.reshape(GM_GROUPS * GM_CHUNK, GM_CHUNK) if st == "w_s" else _pack_slab(grads, name="pack_small_grads")
            land = _place_slab(slab, me_arr, name=f"{st}_place")
            send_sems, recv_sems, *thru, token = _copies_start([land], [land], mode="gather_all", name=f"{st}_start")
            scatter[st] = (thru, send_sems, recv_sems)
            return (token,)
        views = [g.reshape(N_DEV, -1, D_MODEL) for g in grads.values()]
        recv = _place_own(views, me_arr, name=f"scatter_place_l{st[0]}_{st[1]}")
        send_sems, recv_sems, *thru, token = _copies_start(views, recv, mode="scatter",
                                                           name=f"scatter_start_l{st[0]}_{st[1]}")
        scatter[st] = (tuple(grads), thru, send_sems, recv_sems)
        return (token,)

    dx, last_sent = _step_local(
        x, mem, loss_target, get_weights, put_grads, mem_norm, lb_logits, ffn1_norm, mix_norm, hgrn_gnorm,
        gmlp_w_s, gmlp_b_s, ffn2_norm, final_norm)

    slots = {}

    def wait_grads(blk, after, last=False):
        for st, entry in scatter.items():
            if isinstance(st, tuple) and st[1].startswith(blk) and (st == (0, "ffn1_in")) == last:
                pieces, thru, send_sems, recv_sems = entry
                outs = _copies_wait(thru, send_sems, recv_sems, after, n_lands=len(thru) // 2, mode="scatter",
                                    name=f"scatter_wait_l{st[0]}_{st[1]}")
                slots.update(zip(pieces, outs))

    grad, delta, new_m, new_v = {}, {}, {}, {}

    def adam_groups(names):
        for name in names:
            layers = GROUP_LAYERS[name]
            grad[name], delta[name], new_m[name], new_v[name] = _adam_big(
                [slots[(name, l)] for l in range(layers)], weights[name], mom_m[name], mom_v[name], cuts[name],
                name=f"{name}_adamw")

    wait_grads("ffn2", (dx, *last_sent))
    adam_groups(("ffn2_w_in", "ffn2_w_out"))
    wait_grads("mix", (delta["ffn2_w_out"],))
    adam_groups(("mem_w_kv", "gmlp_w_in", "gmlp_w_out", "hgrn_w_in", "hgrn_w_out"))
    wait_grads("ffn1", (delta["hgrn_w_out"],))
    adam_groups(("ffn1_w_out",))

    def small_parts(src):
        parts = {n: [src[n].reshape(-1, src[n].shape[-1])] for n in SLAB_AT if n not in SMALL_SHARDED}
        return parts

    w_s_rows = lambda a: a.reshape(GM_GROUPS * GM_CHUNK, GM_CHUNK)
    small_done = (delta["hgrn_w_out"],)
    (slab_slots,) = _copies_wait(*scatter["small"], small_done, n_lands=1, mode="gather_all", name="small_wait")
    (ws_slots,) = _copies_wait(*scatter["w_s"], small_done, n_lands=1, mode="gather_all", name="w_s_wait")
    (g_slab, d_slab, nm_slab, nv_slab), (g_ws, d_ws, nm_ws, nv_ws) = _adam_slabs(
        [slab_slots, ws_slots],
        [_pack_slab(small_parts(weights), deps=(dx,), name="pack_small_w"), w_s_rows(gmlp_w_s)],
        [_pack_slab(small_parts(mom_m), deps=(dx,), name="pack_small_m"), w_s_rows(m_gmlp_w_s)],
        [_pack_slab(small_parts(mom_v), deps=(dx,), name="pack_small_v"), w_s_rows(v_gmlp_w_s)])
    shapes = {n: weights[n].shape for n in SLAB_AT}
    for out, slab, ws in ((grad, g_slab, g_ws), (delta, d_slab, d_ws), (new_m, nm_slab, nm_ws), (new_v, nv_slab, nv_ws)):
        out.update(_unpack_slab(slab, shapes))
        out["gmlp_w_s"] = ws.reshape(gmlp_w_s.shape)
    blk = GM_WIDTH // N_DEV
    g_ln = [lax.dynamic_slice(g_slab[SLAB_AT[n]:SLAB_AT[n] + 2].reshape(1, GM_WIDTH), (0, me * blk), (1, blk))
            for n in SMALL_SHARDED]
    ln_out = _adam_vecs(g_ln, [weights[n] for n in SMALL_SHARDED], [mom_m[n] for n in SMALL_SHARDED],
                        [mom_v[n] for n in SMALL_SHARDED])
    for n, g, (d, nm, nv) in zip(SMALL_SHARDED, g_ln, ln_out):
        grad[n], delta[n], new_m[n], new_v[n] = g, d, nm, nv

    wait_grads("ffn1", tuple(delta[n] for n in delta if n in GROUP_LAYERS) + (d_slab,), last=True)
    adam_groups(("ffn1_w_in",))

    for name in rows_already:
        for out in (grad, delta, new_m, new_v):
            out[name] = as_rows(out[name])
    loss = g_slab[LOSS_ROW, 0]
    grad_x = dx.reshape(B_LOC, SEQ, D_MODEL)
    return (loss, grad_x, *[grad[n] for n in order], *[delta[n] for n in order],
            *[new_m[n] for n in order], *[new_v[n] for n in order])


def _step_local(x, mem, loss_target, get_weights, put_grads, mem_norm, lb_logits, ffn1_norm, mix_norm, hgrn_gnorm,
                gmlp_w_s, gmlp_b_s, ffn2_norm, final_norm):
    w_s = gmlp_w_s[0]
    b_st = gmlp_b_s[0].T

    xs = x.reshape(N_TOK, D_MODEL)
    mem2d = mem.reshape(B_LOC * MEM_LEN, D_MODEL)
    mem_g = mem_norm.reshape(1, D_MODEL)
    saved, full = [], {}
    for i in range(2):
        xs, s_ffn1 = _ffn_fwd(xs, ffn1_norm[i:i + 1], "ffn1", i, full, get_weights)
        if i == 0:
            memn = _rms_fwd(mem2d, mem_g, deps=(xs,), name="mem_norm_fwd")
        full.update(get_weights((i, "mix_in"), (xs,)))
        mixer = "hgrn" if i == 0 else "gmlp"
        hm, zm = _norm_mm(xs, mix_norm[i:i + 1], full[(f"{mixer}_w_in", 0)], swiglu=False, tm=1024, tn=1280, deps=full.pop("deps", ()),
                          name=f"l{i}_mix_in")
        kv = _mm(memn, full[("mem_w_kv", i)], tb=True, tm=512, tn=512, tk=D_MODEL, out_dtype=F32, name=f"l{i}_mem_kv")
        o_mem = _attn_fwd(zm, kv, name=f"l{i}_attn")
        if i == 0:
            cat, o_pre, s_all = _hgrn_fwd(zm, o_mem, lb_logits, hgrn_gnorm)
            mix_saved = (o_pre, s_all)
        else:
            cat = _gmlp_fwd(zm, o_mem, full["ln_g"], full["ln_b"], w_s, b_st)
            mix_saved = ()
        x_mix = xs
        full.update(get_weights((i, "mix_out"), (cat,)))
        xs = _mm(cat, full[(f"{mixer}_w_out", 0)], tm=1024, tn=D_MODEL, tk=cat.shape[1], out_dtype=F32, res=xs,
                 deps=full.pop("deps", ()), name=f"l{i}_mix_out")
        xs, s_ffn2 = _ffn_fwd(xs, ffn2_norm[i:i + 1], "ffn2", i, full, get_weights)
        saved.append((s_ffn1, (x_mix, hm, kv, zm, cat, mix_saved), s_ffn2))

    dx, dx16, d_final, loss_part = _loss_head(xs, final_norm.reshape(1, D_MODEL), loss_target.reshape(N_TOK, D_MODEL))

    small = {"final_norm": [d_final], "loss": [loss_part]}
    d_ffn1, d_ffn2, d_mix = [None, None], [None, None], [None, None]
    dmemn = jnp.zeros((B_LOC * MEM_LEN, D_MODEL), F32)
    deps = ()
    for i in (1, 0):
        s_ffn1, (x_mix, hm, kv, zm, cat, mix_saved), s_ffn2 = saved[i]
        dx, dx16, d_ffn2[i], dw_in_t, dw_out = _ffn_bwd(
            dx, dx16, s_ffn2, ffn2_norm[i:i + 1], full[("ffn2_w_in", i)], full[("ffn2_w_out", i)], f"l{i}_ffn2", deps)
        deps = put_grads((i, "ffn2"), {("ffn2_w_in", i): dw_in_t, ("ffn2_w_out", i): dw_out})
        mixer = "hgrn" if i == 0 else "gmlp"
        w_in_t, w_out = full[(f"{mixer}_w_in", 0)], full[(f"{mixer}_w_out", 0)]
        width = cat.shape[1]
        g_mix = {}
        g_mix[(f"{mixer}_w_out", 0)] = _mm(cat, dx16, ta=True, tm=1024, tn=D_MODEL, tk=N_TOK, out_dtype=BF16,
                                           deps=deps, name=f"l{i}_mix_out_wgrad")
        dcat = _mm(dx16, w_out, tb=True, tm=1024, tn=width // 2, tk=D_MODEL, out_dtype=F32, name=f"l{i}_mix_out_dgrad")
        dq, dk, dv = _attn_bwd(zm, kv, dcat, do_off=width - XA_HEADS * XA_DIM, name=f"l{i}_attn_bwd")
        if i == 0:
            dzm, dlbl, dgn = _hgrn_bwd(zm, mix_saved[0], dcat, dq, mix_saved[1], lb_logits, hgrn_gnorm)
            small["lb_logits"], small["hgrn_gnorm"] = [dlbl], [dgn]
            deps = ()
        else:
            dzm, dws, dbt, dlng, dlnb = _gmlp_bwd(zm, dcat, dq, full["ln_g"], full["ln_b"], w_s, b_st)
            small["gmlp_b_s"], small["gmlp_ln_g"], small["gmlp_ln_b"] = [dbt.T], [dlng], [dlnb]
            deps = put_grads("w_s", dws)
        g_mix[(f"{mixer}_w_in", 0)] = _mm(dzm, hm, ta=True, tm=1024, tn=D_MODEL, tk=N_TOK, out_dtype=BF16, deps=deps,
                                          name=f"l{i}_mix_in_wgrad")
        dkv = jnp.concatenate([dk, dv], axis=1)
        g_mix[("mem_w_kv", i)] = _mm(dkv, memn, ta=True, tm=512, tn=D_MODEL, tk=B_LOC * MEM_LEN, out_dtype=BF16,
                                     name=f"l{i}_mem_kv_wgrad")
        deps = put_grads((i, "mix"), g_mix)
        dx, dx16, d_mix[i] = _dgrad_norm_bwd(dzm, w_in_t, x_mix, mix_norm[i:i + 1], dx, deps=deps,
                                             name=f"l{i}_mix_in_dgrad")
        dmemn = _mm(dkv, full[("mem_w_kv", i)], tm=B_LOC * MEM_LEN, tn=D_MODEL, tk=512, out_dtype=F32, res=dmemn,
                    name=f"l{i}_mem_kv_dgrad")
        def send_small(dg, i=i, dmemn=dmemn):
            d_ffn1[i] = dg
            _, _, dmem_g = _rms_bwd(mem2d, mem_g, dmemn, dmemn, name="mem_norm_bwd")
            small.update(mem_norm=[dmem_g], ffn1_norm=d_ffn1, ffn2_norm=d_ffn2, mix_norm=d_mix)
            return put_grads("small", small)

        if i == 0:
            send_out = lambda dw_out: put_grads((0, "ffn1_out"), {("ffn1_w_out", 0): dw_out})
            dx, dx16, d_ffn1[i], dw_in_t, _ = _ffn_bwd(
                dx, dx16, s_ffn1, ffn1_norm[i:i + 1], full[("ffn1_w_in", i)], full[("ffn1_w_out", i)], f"l{i}_ffn1",
                after_out_wgrad=send_out, before_in_wgrad=send_small)
            deps = put_grads((0, "ffn1_in"), {("ffn1_w_in", 0): dw_in_t})
        else:
            dx, dx16, d_ffn1[i], dw_in_t, dw_out = _ffn_bwd(
                dx, dx16, s_ffn1, ffn1_norm[i:i + 1], full[("ffn1_w_in", i)], full[("ffn1_w_out", i)], f"l{i}_ffn1")
            deps = put_grads((i, "ffn1"), {("ffn1_w_in", i): dw_in_t, ("ffn1_w_out", i): dw_out})
    return dx, deps
```

```python
import functools
import math

import jax
import jax.numpy as jnp
from jax import lax
from jax.experimental import pallas as pl
from jax.experimental.pallas import tpu as pltpu

F32 = jnp.float32
BF16 = jnp.bfloat16

D_MODEL = 1024
SEQ = 2048
B_LOC = 2
N_TOK = B_LOC * SEQ
MEM_LEN = 256
N_DEV = 8
EPS = 1e-6
D_FF = 2816
HG_HEADS = 8
HG_DIM = 128
HG_CHUNK = 64
HG_NCHUNK = SEQ // HG_CHUNK
GM_CHUNK = 128
GM_GROUPS = 8
GM_WIDTH = 2048
GM_GDIM = GM_WIDTH // GM_GROUPS
XA_HEADS = 4
XA_DIM = 256
XA_OFF = 4096

ADAM_LR = 0.001
ADAM_B1 = 0.9
ADAM_B2 = 0.999
ADAM_EPS = 1e-08
ADAM_WD = 0.01
ADAM_STEP = 10

VMEM_LIMIT_BYTES = 56 * 1024 * 1024
MESH_AXES = ("x", "y", "c")

GROUPS = (
    ("ffn1_w_in", True, 2, 704),
    ("ffn1_w_out", False, 2, 352),
    ("mem_w_kv", True, 2, 256),
    ("hgrn_w_in", True, 1, 640),
    ("hgrn_w_out", False, 1, 256),
    ("gmlp_w_in", True, 1, 640),
    ("gmlp_w_out", False, 1, 384),
    ("ffn2_w_in", True, 2, 704),
    ("ffn2_w_out", False, 2, 352),
)
GROUP_LAYERS = {name: layers for name, _, layers, _ in GROUPS}


def _stage_pieces(layer, block):
    if block == "mix":
        mixer = "hgrn" if layer == 0 else "gmlp"
        return (("mem_w_kv", layer), (f"{mixer}_w_in", 0), (f"{mixer}_w_out", 0))
    return ((f"{block}_w_in", layer), (f"{block}_w_out", layer))


ANY_SPEC = pl.BlockSpec(memory_space=pl.ANY)
HBM_SPEC = pl.BlockSpec(memory_space=pltpu.HBM)
SEM_SPEC = pl.BlockSpec(memory_space=pltpu.SEMAPHORE)


def _cp(*sem):
    return pltpu.CompilerParams(dimension_semantics=sem, vmem_limit_bytes=VMEM_LIMIT_BYTES)


def _sigmoid(x):
    return 0.5 * jnp.tanh(0.5 * x) + 0.5


def _gelu_parts(x):
    cdf = 0.5 * (1.0 + lax.erf(x * (1.0 / math.sqrt(2.0))))
    pdf = jnp.exp(-0.5 * x * x) * (1.0 / math.sqrt(2.0 * math.pi))
    return x * cdf, cdf + x * pdf


def _mm(a, b, *, ta=False, tb=False, tm, tn, tk, out_dtype, res=None, scale=1.0, deps=(), name):
    m, k = (a.shape[1], a.shape[0]) if ta else a.shape
    n, kb = b.shape if tb else (b.shape[1], b.shape[0])
    assert k == kb and m % tm == 0 and n % tn == 0 and k % tk == 0, (name, a.shape, b.shape)
    nk = k // tk
    dn = (((0 if ta else 1,), (1 if tb else 0,)), ((), ()))
    n_in = 2 + (res is not None) + len(deps)

    def body(*refs):
        a_ref, b_ref = refs[:2]
        r_ref = refs[2] if res is not None else None
        o_ref, scr = refs[n_in], refs[n_in + 1:]
        p = lax.dot_general(a_ref[...].astype(BF16), b_ref[...].astype(BF16), dn, preferred_element_type=F32)

        def finish(acc):
            if scale != 1.0:
                acc = scale * acc
            if r_ref is not None:
                acc = r_ref[...] + acc
            o_ref[...] = acc.astype(out_dtype)

        if nk == 1:
            finish(p)
        else:
            acc_ref = scr[0]
            kk = pl.program_id(2)

            @pl.when(kk == 0)
            def _():
                acc_ref[...] = p

            @pl.when(kk > 0)
            def _():
                acc_ref[...] += p

            @pl.when(kk == nk - 1)
            def _():
                finish(acc_ref[...])

    a_spec = pl.BlockSpec((tk, tm), lambda i, j, kk: (kk, i)) if ta else pl.BlockSpec((tm, tk), lambda i, j, kk: (i, kk))
    b_mode = dict(pipeline_mode=pl.Buffered(1)) if n == tn and nk == 1 else {}
    if tb:
        b_spec = pl.BlockSpec((tn, tk), lambda i, j, kk: (j, kk), **b_mode)
    else:
        b_spec = pl.BlockSpec((tk, tn), lambda i, j, kk: (kk, j), **b_mode)
    o_spec = pl.BlockSpec((tm, tn), lambda i, j, kk: (i, j))
    in_specs = [a_spec, b_spec] + ([o_spec] if res is not None else []) + [ANY_SPEC] * len(deps)
    args = (a, b) + ((res,) if res is not None else ()) + tuple(deps)
    return pl.pallas_call(
        body,
        name=name,
        grid=(m // tm, n // tn, nk),
        in_specs=in_specs,
        out_specs=o_spec,
        out_shape=jax.ShapeDtypeStruct((m, n), out_dtype),
        scratch_shapes=[pltpu.VMEM((tm, tn), F32)] if nk > 1 else [],
        compiler_params=_cp("parallel", "parallel", "arbitrary"),
    )(*args)


def _rms_fwd(x, g, *, name, deps=(), tm=512):
    rows = x.shape[0]

    def body(x_ref, g_ref, *rest):
        o_ref = rest[len(deps)]
        xv = x_ref[...]
        r = lax.rsqrt(jnp.mean(xv * xv, axis=-1, keepdims=True) + EPS)
        o_ref[...] = (xv * r * g_ref[...]).astype(BF16)

    row = pl.BlockSpec((tm, D_MODEL), lambda i: (i, 0))
    return pl.pallas_call(
        body,
        name=name,
        grid=(rows // tm,),
        in_specs=[row, pl.BlockSpec((1, D_MODEL), lambda i: (0, 0))] + [ANY_SPEC] * len(deps),
        out_specs=row,
        out_shape=jax.ShapeDtypeStruct((rows, D_MODEL), BF16),
        compiler_params=_cp("parallel"),
    )(x, g, *deps)


def _rms_bwd(x, g, dh, dres, *, name, deps=(), tm=512):
    rows = x.shape[0]

    def body(x_ref, g_ref, dh_ref, dres_ref, *rest):
        dx_ref, dx16_ref, dg_ref = rest[len(deps):]
        xv = x_ref[...]
        r = lax.rsqrt(jnp.mean(xv * xv, axis=-1, keepdims=True) + EPS)
        xhat = xv * r
        dhv = dh_ref[...]
        part = jnp.sum(dhv * xhat, axis=0, keepdims=True)

        @pl.when(pl.program_id(0) == 0)
        def _():
            dg_ref[...] = part

        @pl.when(pl.program_id(0) > 0)
        def _():
            dg_ref[...] += part

        dxh = dhv * g_ref[...]
        dx = dres_ref[...] + r * (dxh - xhat * jnp.mean(dxh * xhat, axis=-1, keepdims=True))
        dx_ref[...] = dx
        dx16_ref[...] = dx.astype(BF16)

    row = pl.BlockSpec((tm, D_MODEL), lambda i: (i, 0))
    vec = pl.BlockSpec((1, D_MODEL), lambda i: (0, 0))
    return pl.pallas_call(
        body,
        name=name,
        grid=(rows // tm,),
        in_specs=[row, vec, row, row] + [ANY_SPEC] * len(deps),
        out_specs=[row, row, vec],
        out_shape=[jax.ShapeDtypeStruct((rows, D_MODEL), F32), jax.ShapeDtypeStruct((rows, D_MODEL), BF16),
                   jax.ShapeDtypeStruct((1, D_MODEL), F32)],
        compiler_params=_cp("arbitrary"),
    )(x, g, dh, dres, *deps)


_NT = (((1,), (1,)), ((), ()))
_TN = (((0,), (0,)), ((), ()))


def _norm_mm(x, g, w_t, *, swiglu, name, tm, tn, deps=()):
    rows = w_t.shape[0]
    half = rows // 2
    nj = (half if swiglu else rows) // tn
    nd = len(deps)

    def body(x_ref, g_ref, w_ref, *rest):
        outs = rest[nd:]
        h_ref, z_ref = outs[:2]

        def norm():
            xv = x_ref[...]
            r = lax.rsqrt(jnp.mean(xv * xv, axis=-1, keepdims=True) + EPS)
            h_ref[...] = (xv * r * g_ref[...]).astype(BF16)

        if swiglu:
            norm()
            h = h_ref[...]
            for j in range(nj):
                cols = slice(j * tn, (j + 1) * tn)
                gate = lax.dot_general(h, w_ref[j * tn:(j + 1) * tn, :], _NT, preferred_element_type=F32)
                up = lax.dot_general(h, w_ref[half + j * tn:half + (j + 1) * tn, :], _NT, preferred_element_type=F32)
                s = _sigmoid(gate)
                silu = gate * s
                z_ref[0, :, cols] = (up * (s + silu * (1.0 - s))).astype(BF16)
                z_ref[1, :, cols] = silu.astype(BF16)
                outs[2][:, cols] = (silu * up).astype(BF16)
        else:
            j = pl.program_id(1)
            pl.when(j == 0)(norm)
            w = w_ref[pl.ds(pl.multiple_of(j * tn, tn), tn), :]
            z_ref[...] = lax.dot_general(h_ref[...], w, _NT, preferred_element_type=F32)

    grid = (N_TOK // tm,) if swiglu else (N_TOK // tm, nj)
    row = pl.BlockSpec((tm, D_MODEL), lambda i, *_: (i, 0))
    out_specs = [row]
    out_shape = [jax.ShapeDtypeStruct((N_TOK, D_MODEL), BF16)]
    if swiglu:
        out_specs += [pl.BlockSpec((2, tm, half), lambda i: (0, i, 0)), pl.BlockSpec((tm, half), lambda i: (i, 0))]
        out_shape += [jax.ShapeDtypeStruct((2, N_TOK, half), BF16), jax.ShapeDtypeStruct((N_TOK, half), BF16)]
    else:
        out_specs.append(pl.BlockSpec((tm, tn), lambda i, j: (i, j)))
        out_shape.append(jax.ShapeDtypeStruct((N_TOK, rows), F32))
    return pl.pallas_call(
        body,
        name=name,
        grid=grid,
        in_specs=[row, pl.BlockSpec((1, D_MODEL), lambda *_: (0, 0)),
                  pl.BlockSpec((rows, D_MODEL), lambda *_: (0, 0), pipeline_mode=pl.Buffered(1))] + [ANY_SPEC] * nd,
        out_specs=out_specs,
        out_shape=out_shape,
        compiler_params=_cp(*(("parallel",) if swiglu else ("parallel", "arbitrary"))),
    )(x, g, w_t, *deps)


def _swiglu_dgrad(dy16, w_out, z, *, scale, name, deps=(), tm=512, tn=1408):
    def body(dy_ref, w_ref, z_ref, *rest):
        dz_ref = rest[len(deps)]
        dy = dy_ref[...]
        for j in range(D_FF // tn):
            cols = slice(j * tn, (j + 1) * tn)
            da = lax.dot_general(dy, w_ref[cols, :], _NT, preferred_element_type=F32) * scale
            dz_ref[0, :, cols] = (da * z_ref[0, :, cols].astype(F32)).astype(BF16)
            dz_ref[1, :, cols] = (da * z_ref[1, :, cols].astype(F32)).astype(BF16)

    planes = pl.BlockSpec((2, tm, D_FF), lambda i: (0, i, 0))
    return pl.pallas_call(
        body,
        name=name,
        grid=(N_TOK // tm,),
        in_specs=[pl.BlockSpec((tm, D_MODEL), lambda i: (i, 0)),
                  pl.BlockSpec((D_FF, D_MODEL), lambda i: (0, 0), pipeline_mode=pl.Buffered(1)), planes]
        + [ANY_SPEC] * len(deps),
        out_specs=planes,
        out_shape=jax.ShapeDtypeStruct((2, N_TOK, D_FF), BF16),
        compiler_params=_cp("parallel"),
    )(dy16, w_out, z, *deps)


def _planes_wgrad(dz, h, *, name, deps=(), tm=1408):
    per_plane = D_FF // tm

    def body(a_ref, b_ref, *rest):
        o_ref = rest[len(deps)]
        o_ref[...] = lax.dot_general(a_ref[...], b_ref[...], _TN, preferred_element_type=F32).astype(BF16)

    return pl.pallas_call(
        body,
        name=name,
        grid=(2 * per_plane,),
        in_specs=[pl.BlockSpec((None, N_TOK, tm),
                               lambda i: (jnp.where(i < per_plane, 0, 1), 0, jnp.where(i < per_plane, i, i - per_plane))),
                  pl.BlockSpec((N_TOK, D_MODEL), lambda i: (0, 0), pipeline_mode=pl.Buffered(1))] + [ANY_SPEC] * len(deps),
        out_specs=pl.BlockSpec((tm, D_MODEL), lambda i: (i, 0)),
        out_shape=jax.ShapeDtypeStruct((2 * D_FF, D_MODEL), BF16),
        compiler_params=_cp("parallel"),
    )(dz, h, *deps)


def _dgrad_norm_bwd(dz, w_t, x, g, dres, *, name, deps=(), tm=512):
    planes = dz.ndim == 3
    rows = w_t.shape[0]
    half = rows // 2
    nd = len(deps)

    def body(a_ref, b_ref, x_ref, g_ref, dres_ref, *rest):
        dx_ref, dx16_ref, dg_ref = rest[nd:]
        if planes:
            dh = jnp.dot(a_ref[0], b_ref[:half, :], preferred_element_type=F32) + jnp.dot(
                a_ref[1], b_ref[half:, :], preferred_element_type=F32)
        else:
            dh = jnp.dot(a_ref[...], b_ref[...], preferred_element_type=F32)
        xv = x_ref[...]
        r = lax.rsqrt(jnp.mean(xv * xv, axis=-1, keepdims=True) + EPS)
        xhat = xv * r
        part = jnp.sum(dh * xhat, axis=0, keepdims=True)

        @pl.when(pl.program_id(0) == 0)
        def _():
            dg_ref[...] = part

        @pl.when(pl.program_id(0) > 0)
        def _():
            dg_ref[...] += part

        dxh = dh * g_ref[...]
        dx = dres_ref[...] + r * (dxh - xhat * jnp.mean(dxh * xhat, axis=-1, keepdims=True))
        dx_ref[...] = dx
        dx16_ref[...] = dx.astype(BF16)

    a_spec = pl.BlockSpec((2, tm, half), lambda i: (0, i, 0)) if planes else pl.BlockSpec((tm, rows), lambda i: (i, 0))
    row = pl.BlockSpec((tm, D_MODEL), lambda i: (i, 0))
    vec = pl.BlockSpec((1, D_MODEL), lambda i: (0, 0))
    return pl.pallas_call(
        body,
        name=name,
        grid=(N_TOK // tm,),
        in_specs=[a_spec, pl.BlockSpec((rows, D_MODEL), lambda i: (0, 0), pipeline_mode=pl.Buffered(1)), row, vec, row]
        + [ANY_SPEC] * nd,
        out_specs=[row, row, vec],
        out_shape=[jax.ShapeDtypeStruct((N_TOK, D_MODEL), F32), jax.ShapeDtypeStruct((N_TOK, D_MODEL), BF16),
                   jax.ShapeDtypeStruct((1, D_MODEL), F32)],
        compiler_params=_cp("arbitrary"),
    )(dz, w_t, x, g, dres, *deps)


def _loss_head(x, g, target, *, tm=512):
    def body(x_ref, g_ref, t_ref, dx_ref, dx16_ref, dg_ref, loss_ref):
        xv = x_ref[...]
        gv = g_ref[...]
        r = lax.rsqrt(jnp.mean(xv * xv, axis=-1, keepdims=True) + EPS)
        xhat = xv * r
        err = xhat * gv - t_ref[...]
        loss_part = jnp.zeros((1, 128), F32) + 0.5 * jnp.sum(jnp.mean(err * err, axis=-1, keepdims=True))
        dy = err * (1.0 / D_MODEL)
        dg_part = jnp.sum(dy * xhat, axis=0, keepdims=True)

        @pl.when(pl.program_id(0) == 0)
        def _():
            dg_ref[...] = dg_part
            loss_ref[...] = loss_part

        @pl.when(pl.program_id(0) > 0)
        def _():
            dg_ref[...] += dg_part
            loss_ref[...] += loss_part

        dxh = dy * gv
        dx = r * (dxh - xhat * jnp.mean(dxh * xhat, axis=-1, keepdims=True))
        dx_ref[...] = dx
        dx16_ref[...] = dx.astype(BF16)

    row = pl.BlockSpec((tm, D_MODEL), lambda i: (i, 0))
    vec = pl.BlockSpec((1, D_MODEL), lambda i: (0, 0))
    return pl.pallas_call(
        body,
        name="loss_head",
        grid=(N_TOK // tm,),
        in_specs=[row, vec, row],
        out_specs=[row, row, vec, pl.BlockSpec((1, 128), lambda i: (0, 0))],
        out_shape=[
            jax.ShapeDtypeStruct((N_TOK, D_MODEL), F32),
            jax.ShapeDtypeStruct((N_TOK, D_MODEL), BF16),
            jax.ShapeDtypeStruct((1, D_MODEL), F32),
            jax.ShapeDtypeStruct((1, 128), F32),
        ],
        compiler_params=_cp("arbitrary"),
    )(x, g, target)


XA_TQ = 2048
XA_SCALE = XA_DIM ** -0.5


def _attn_probs(q16, k16):
    s = lax.dot_general(q16, k16, _NT, preferred_element_type=F32) * XA_SCALE
    e = jnp.exp(s - jnp.max(s, axis=-1, keepdims=True))
    return e / jnp.sum(e, axis=-1, keepdims=True)


def _attn_fwd(z, kv, *, name):
    nt = SEQ // XA_TQ

    def body(q_ref, k_ref, v_ref, o_ref):
        p = _attn_probs(q_ref[...].astype(BF16), k_ref[...].astype(BF16))
        o_ref[...] = jnp.dot(p.astype(BF16), v_ref[...].astype(BF16), preferred_element_type=F32).astype(BF16)

    return pl.pallas_call(
        body,
        name=name,
        grid=(B_LOC, XA_HEADS, nt),
        in_specs=[
            pl.BlockSpec((XA_TQ, XA_DIM), lambda b, h, t: (b * nt + t, XA_OFF // XA_DIM + h)),
            pl.BlockSpec((MEM_LEN, XA_DIM), lambda b, h, t: (b, h)),
            pl.BlockSpec((MEM_LEN, XA_DIM), lambda b, h, t: (b, XA_HEADS + h)),
        ],
        out_specs=pl.BlockSpec((XA_TQ, XA_DIM), lambda b, h, t: (b * nt + t, h)),
        out_shape=jax.ShapeDtypeStruct((N_TOK, XA_HEADS * XA_DIM), BF16),
        compiler_params=_cp("parallel", "parallel", "arbitrary"),
    )(z, kv, kv)


def _attn_bwd(z, kv, dcat, *, do_off, name):
    nt = SEQ // XA_TQ

    def body(q_ref, k_ref, v_ref, do_ref, dq_ref, dk_ref, dv_ref):
        q16 = q_ref[...].astype(BF16)
        k16 = k_ref[...].astype(BF16)
        v16 = v_ref[...].astype(BF16)
        do16 = do_ref[...].astype(BF16)
        p = _attn_probs(q16, k16)
        dv_part = lax.dot_general(p.astype(BF16), do16, _TN, preferred_element_type=F32)
        dp = lax.dot_general(do16, v16, _NT, preferred_element_type=F32)
        ds16 = (p * (dp - jnp.sum(dp * p, axis=-1, keepdims=True)) * XA_SCALE).astype(BF16)
        dq_ref[...] = jnp.dot(ds16, k16, preferred_element_type=F32).astype(BF16)
        dk_part = lax.dot_general(ds16, q16, _TN, preferred_element_type=F32)

        @pl.when(pl.program_id(2) == 0)
        def _():
            dk_ref[...] = dk_part
            dv_ref[...] = dv_part

        @pl.when(pl.program_id(2) > 0)
        def _():
            dk_ref[...] += dk_part
            dv_ref[...] += dv_part

    qspec = pl.BlockSpec((XA_TQ, XA_DIM), lambda b, h, t: (b * nt + t, XA_OFF // XA_DIM + h))
    kspec = lambda off: pl.BlockSpec((MEM_LEN, XA_DIM), lambda b, h, t: (b, off + h))
    return pl.pallas_call(
        body,
        name=name,
        grid=(B_LOC, XA_HEADS, nt),
        in_specs=[qspec, kspec(0), kspec(XA_HEADS),
                  pl.BlockSpec((XA_TQ, XA_DIM), lambda b, h, t: (b * nt + t, do_off // XA_DIM + h))],
        out_specs=[pl.BlockSpec((XA_TQ, XA_DIM), lambda b, h, t: (b * nt + t, h)), kspec(0), kspec(0)],
        out_shape=[
            jax.ShapeDtypeStruct((N_TOK, XA_HEADS * XA_DIM), BF16),
            jax.ShapeDtypeStruct((B_LOC * MEM_LEN, XA_HEADS * XA_DIM), F32),
            jax.ShapeDtypeStruct((B_LOC * MEM_LEN, XA_HEADS * XA_DIM), F32),
        ],
        compiler_params=_cp("parallel", "parallel", "arbitrary"),
    )(z, kv, kv, dcat)


def _tril(n):
    return lax.broadcasted_iota(jnp.int32, (n, n), 0) >= lax.broadcasted_iota(jnp.int32, (n, n), 1)


def _lower_bound(lbl):
    e = jnp.exp(lbl - jnp.max(lbl, axis=0, keepdims=True))
    p = e / jnp.sum(e, axis=0, keepdims=True)
    return p[0:1, :], p


def _hgrn_gates(zq, zf, lb, tril_f):
    sig = _sigmoid(zf)
    f = lb + (1.0 - lb) * sig
    kk = 1.0 - f
    sq = _sigmoid(zq)
    q = zq * sq
    b = jnp.dot(tril_f, jnp.log(f), preferred_element_type=F32, precision=lax.Precision.HIGHEST)
    bl = b[HG_CHUNK - 1:HG_CHUNK, :]
    return q, sq, sig, f, kk, b, bl


HG_TB = 512
HG_CPB = HG_TB // HG_CHUNK
HG_NT = SEQ // HG_TB
HG_WIDTH = HG_HEADS * HG_DIM


def _head(h, section=0):
    return slice(section * HG_WIDTH + h * HG_DIM, section * HG_WIDTH + (h + 1) * HG_DIM)


def _hgrn_fwd(z, o_mem, lb_logits, gnorm):
    def body(zq_ref, zf_ref, zi_ref, zg_ref, omem_ref, lbl_ref, gn_ref, o_ref, opre_ref, sall_ref, st_ref):
        lb, _ = _lower_bound(lbl_ref[...])
        gn = gn_ref[...]
        mask = _tril(HG_CHUNK)
        tril_f = mask.astype(F32)
        o_ref[:, HG_WIDTH:] = omem_ref[...]

        @pl.when(pl.program_id(1) == 0)
        def _():
            st_ref[...] = jnp.zeros_like(st_ref)

        def chunk(c, carry):
            rows = pl.ds(pl.multiple_of(c * HG_CHUNK, HG_CHUNK), HG_CHUNK)
            q, _, _, _, kk, b, bl = _hgrn_gates(zq_ref[rows, :], zf_ref[rows, :], lb, tril_f)
            v16 = zi_ref[rows, :].astype(BF16)
            qd16 = (q * jnp.exp(b)).astype(BF16)
            ki16 = (kk * jnp.exp(-b)).astype(BF16)
            kd16 = (kk * jnp.exp(bl - b)).astype(BF16)
            ebl = jnp.exp(bl)
            zg = zg_ref[rows, :]
            gate = zg * _sigmoid(zg)
            for h in range(HG_HEADS):
                sl = _head(h)
                a = jnp.where(mask, lax.dot_general(qd16[:, sl], ki16[:, sl], _NT, preferred_element_type=F32), 0.0)
                st = st_ref[h]
                sall_ref[0, h, c] = st
                o = jnp.dot(a.astype(BF16), v16[:, sl], preferred_element_type=F32) + lax.dot_general(
                    qd16[:, sl], st.astype(BF16), _NT, preferred_element_type=F32)
                st_ref[h] = st * ebl[:, sl] + lax.dot_general(v16[:, sl], kd16[:, sl], _TN, preferred_element_type=F32)
                opre_ref[rows, sl] = o
                r = lax.rsqrt(jnp.mean(o * o, axis=-1, keepdims=True) + EPS)
                o_ref[rows, sl] = ((o * r * gn) * gate[:, sl]).astype(BF16)
            return carry

        lax.fori_loop(0, HG_CPB, chunk, 0, unroll=4)

    zspec = lambda s: pl.BlockSpec((HG_TB, HG_WIDTH), lambda b, t: (b * HG_NT + t, s))
    return pl.pallas_call(
        body,
        name="hgrn_fwd",
        grid=(B_LOC, HG_NT),
        in_specs=[zspec(0), zspec(1), zspec(2), zspec(3), zspec(0),
                  pl.BlockSpec((3, HG_WIDTH), lambda b, t: (0, 0)), pl.BlockSpec((1, HG_DIM), lambda b, t: (0, 0))],
        out_specs=[pl.BlockSpec((HG_TB, 2 * HG_WIDTH), lambda b, t: (b * HG_NT + t, 0)), zspec(0),
                   pl.BlockSpec((1, HG_HEADS, HG_CPB, HG_DIM, HG_DIM), lambda b, t: (b, 0, t, 0, 0))],
        out_shape=[
            jax.ShapeDtypeStruct((N_TOK, 2 * HG_WIDTH), BF16),
            jax.ShapeDtypeStruct((N_TOK, HG_WIDTH), F32),
            jax.ShapeDtypeStruct((B_LOC, HG_HEADS, HG_NCHUNK, HG_DIM, HG_DIM), F32),
        ],
        scratch_shapes=[pltpu.VMEM((HG_HEADS, HG_DIM, HG_DIM), F32)],
        compiler_params=_cp("parallel", "arbitrary"),
    )(z, z, z, z, o_mem, lb_logits, gnorm)


def _hgrn_bwd(z, opre, dcat, dq_mem, sall, lb_logits, gnorm):
    def body(zq_ref, zf_ref, zi_ref, zg_ref, opre_ref, dout_ref, dqm_ref, sall_ref, lbl_ref, gn_ref,
             dz_ref, dlbl_ref, dgn_ref, dst_ref, dlb_ref, dgn_acc, db_ref, dkk_ref, dbl_ref):
        b_id, t_id = pl.program_id(0), pl.program_id(1)
        lb, p = _lower_bound(lbl_ref[...])
        gn = gn_ref[...]
        mask = _tril(HG_CHUNK)
        tril_f = mask.astype(F32)
        dz_ref[:, 4 * HG_WIDTH:] = dqm_ref[...]

        @pl.when(t_id == 0)
        def _():
            dst_ref[...] = jnp.zeros_like(dst_ref)
            dlb_ref[...] = jnp.zeros_like(dlb_ref)

        @pl.when((b_id == 0) & (t_id == 0))
        def _():
            dgn_acc[...] = jnp.zeros_like(dgn_acc)

        def chunk(i, carry):
            c = HG_CPB - 1 - i
            rows = pl.ds(pl.multiple_of(c * HG_CHUNK, HG_CHUNK), HG_CHUNK)
            zq, zg = zq_ref[rows, :], zg_ref[rows, :]
            q, sq, sig, f, kk, b, bl = _hgrn_gates(zq, zf_ref[rows, :], lb, tril_f)
            v16 = zi_ref[rows, :].astype(BF16)
            eb, enb, ebl_b, ebl = jnp.exp(b), jnp.exp(-b), jnp.exp(bl - b), jnp.exp(bl)
            qd, ki, kd = q * eb, kk * enb, kk * ebl_b
            qd16, ki16, kd16 = qd.astype(BF16), ki.astype(BF16), kd.astype(BF16)
            o_all = opre_ref[rows, :]
            dout = dout_ref[rows, :]
            sg = _sigmoid(zg)
            d_on_all = dout * (zg * sg)
            dgate = dout * (sg * (1.0 + zg * (1.0 - sg)))
            dq_scale = eb * (sq * (1.0 + zq * (1.0 - sq)))
            for h in range(HG_HEADS):
                sl = _head(h)
                o = o_all[:, sl]
                r = lax.rsqrt(jnp.mean(o * o, axis=-1, keepdims=True) + EPS)
                ohat = o * r
                d_on = d_on_all[:, sl]
                dz_ref[rows, _head(h, 3)] = (dgate[:, sl] * (ohat * gn)).astype(BF16)
                dgn_acc[...] += jnp.sum(d_on * ohat, axis=0, keepdims=True)
                dohat = d_on * gn
                do16 = (r * (dohat - ohat * jnp.mean(dohat * ohat, axis=-1, keepdims=True))).astype(BF16)
                st = sall_ref[0, h, c]
                dst = dst_ref[h]
                st16, dst16 = st.astype(BF16), dst.astype(BF16)
                qd_h, ki_h, kd_h, v_h = qd16[:, sl], ki16[:, sl], kd16[:, sl], v16[:, sl]
                a16 = jnp.where(mask, lax.dot_general(qd_h, ki_h, _NT, preferred_element_type=F32), 0.0).astype(BF16)
                da16 = jnp.where(mask, lax.dot_general(do16, v_h, _NT, preferred_element_type=F32), 0.0).astype(BF16)
                dv = lax.dot_general(a16, do16, _TN, preferred_element_type=F32) + lax.dot_general(
                    kd_h, dst16, _NT, preferred_element_type=F32)
                dqd = jnp.dot(da16, ki_h, preferred_element_type=F32) + jnp.dot(do16, st16, preferred_element_type=F32)
                dki = lax.dot_general(da16, qd_h, _TN, preferred_element_type=F32)
                dkd = jnp.dot(v_h, dst16, preferred_element_type=F32)
                dbl_ref[:, sl] = jnp.sum(dkd * kd[:, sl], axis=0, keepdims=True) + ebl[:, sl] * jnp.sum(
                    st * dst, axis=0, keepdims=True)
                dst_ref[h] = dst * ebl[:, sl] + lax.dot_general(do16, qd_h, _TN, preferred_element_type=F32)
                dz_ref[rows, _head(h, 2)] = dv.astype(BF16)
                dz_ref[rows, sl] = (dqd * dq_scale[:, sl]).astype(BF16)
                dkk_ref[:, sl] = dki * enb[:, sl] + dkd * ebl_b[:, sl]
                db_ref[:, sl] = dqd * qd[:, sl] - dki * ki[:, sl] - dkd * kd[:, sl]
            dlogf = lax.dot_general(tril_f, db_ref[...], _TN, preferred_element_type=F32,
                                    precision=lax.Precision.HIGHEST) + dbl_ref[...]
            df = dlogf / f - dkk_ref[...]
            dz_ref[rows, HG_WIDTH:2 * HG_WIDTH] = (df * (1.0 - lb) * sig * (1.0 - sig)).astype(BF16)
            dlb_ref[...] += jnp.sum(df * (1.0 - sig), axis=0, keepdims=True)
            return carry

        lax.fori_loop(0, HG_CPB, chunk, 0, unroll=4)

        @pl.when(t_id == HG_NT - 1)
        def _():
            row0 = (lax.broadcasted_iota(jnp.int32, (3, HG_WIDTH), 0) == 0).astype(F32)
            dlbl_part = dlb_ref[...] * lb * (row0 - p)

            @pl.when(b_id == 0)
            def _():
                dlbl_ref[...] = dlbl_part

            @pl.when(b_id > 0)
            def _():
                dlbl_ref[...] += dlbl_part

            dgn_ref[...] = dgn_acc[...]

    rev = lambda b, t: b * HG_NT + HG_NT - 1 - t
    zspec = lambda s: pl.BlockSpec((HG_TB, HG_WIDTH), lambda b, t: (rev(b, t), s))
    return pl.pallas_call(
        body,
        name="hgrn_bwd",
        grid=(B_LOC, HG_NT),
        in_specs=[zspec(0), zspec(1), zspec(2), zspec(3), zspec(0), zspec(0), zspec(0),
                  pl.BlockSpec((1, HG_HEADS, HG_CPB, HG_DIM, HG_DIM), lambda b, t: (b, 0, HG_NT - 1 - t, 0, 0)),
                  pl.BlockSpec((3, HG_WIDTH), lambda b, t: (0, 0)), pl.BlockSpec((1, HG_DIM), lambda b, t: (0, 0))],
        out_specs=[pl.BlockSpec((HG_TB, 5 * HG_WIDTH), lambda b, t: (rev(b, t), 0)),
                   pl.BlockSpec((3, HG_WIDTH), lambda b, t: (0, 0)), pl.BlockSpec((1, HG_DIM), lambda b, t: (0, 0))],
        out_shape=[jax.ShapeDtypeStruct((N_TOK, 5 * HG_WIDTH), BF16),
                   jax.ShapeDtypeStruct((3, HG_WIDTH), F32), jax.ShapeDtypeStruct((1, HG_DIM), F32)],
        scratch_shapes=[pltpu.VMEM((HG_HEADS, HG_DIM, HG_DIM), F32), pltpu.VMEM((1, HG_WIDTH), F32),
                        pltpu.VMEM((1, HG_DIM), F32), pltpu.VMEM((HG_CHUNK, HG_WIDTH), F32),
                        pltpu.VMEM((HG_CHUNK, HG_WIDTH), F32), pltpu.VMEM((1, HG_WIDTH), F32)],
        compiler_params=_cp("arbitrary", "arbitrary"),
    )(z, z, z, z, opre, dcat, dq_mem, sall, lb_logits, gnorm)


GM_TM = 256


def _gmlp_norm(zv, ln_g, ln_b):
    gv, dgelu = _gelu_parts(zv)
    xc = gv - jnp.mean(gv, axis=-1, keepdims=True)
    rstd = lax.rsqrt(jnp.mean(xc * xc, axis=-1, keepdims=True) + EPS)
    vhat = xc * rstd
    return vhat * ln_g + ln_b, vhat, rstd, dgelu


def _gmlp_specs():
    half = lambda j: pl.BlockSpec((GM_TM, GM_WIDTH), lambda i: (i, j))
    vec = pl.BlockSpec((1, GM_WIDTH), lambda i: (0, 0))
    w = pl.BlockSpec((GM_GROUPS, GM_CHUNK, GM_CHUNK), lambda i: (0, 0, 0))
    bt = pl.BlockSpec((GM_CHUNK, GM_GROUPS), lambda i: (0, 0))
    return half, vec, w, bt


def _gmlp_fwd(z, o_mem, ln_g, ln_b, w_s, b_st):
    def body(zu_ref, zv_ref, omem_ref, g_ref, b_ref, w_ref, bt_ref, o_ref):
        o_ref[:, GM_WIDTH:] = omem_ref[...]
        u, _ = _gelu_parts(zu_ref[...])
        v, _, _, _ = _gmlp_norm(zv_ref[...], g_ref[...], b_ref[...])
        v16 = v.astype(BF16)
        mask = _tril(GM_CHUNK)
        bt = bt_ref[...]
        for g in range(GM_GROUPS):
            wm16 = jnp.where(mask, w_ref[g], 0.0).astype(BF16)
            cols = slice(g * GM_GDIM, (g + 1) * GM_GDIM)
            for c in range(GM_TM // GM_CHUNK):
                rows = slice(c * GM_CHUNK, (c + 1) * GM_CHUNK)
                mixed = jnp.dot(wm16, v16[rows, cols], preferred_element_type=F32) + bt[:, g:g + 1]
                o_ref[rows, cols] = (u[rows, cols] * mixed).astype(BF16)

    half, vec, w, bt = _gmlp_specs()
    return pl.pallas_call(
        body,
        name="gmlp_fwd",
        grid=(N_TOK // GM_TM,),
        in_specs=[half(0), half(1), pl.BlockSpec((GM_TM, XA_HEADS * XA_DIM), lambda i: (i, 0)), vec, vec, w, bt],
        out_specs=pl.BlockSpec((GM_TM, GM_WIDTH + XA_HEADS * XA_DIM), lambda i: (i, 0)),
        out_shape=jax.ShapeDtypeStruct((N_TOK, GM_WIDTH + XA_HEADS * XA_DIM), BF16),
        compiler_params=_cp("parallel"),
    )(z, z, o_mem, ln_g, ln_b, w_s, b_st)


def _gmlp_bwd(z, dcat, dq_mem, ln_g, ln_b, w_s, b_st):
    def body(zu_ref, zv_ref, dout_ref, dqm_ref, g_ref, b_ref, w_ref, bt_ref,
             dz_ref, dw_ref, dbt_ref, dg_ref, db_ref, dv_ref):
        dz_ref[:, 2 * GM_WIDTH:] = dqm_ref[...]
        @pl.when(pl.program_id(0) == 0)
        def _():
            dw_ref[...] = jnp.zeros_like(dw_ref)
            dbt_ref[...] = jnp.zeros_like(dbt_ref)
            dg_ref[...] = jnp.zeros_like(dg_ref)
            db_ref[...] = jnp.zeros_like(db_ref)

        zu = zu_ref[...]
        u, du_dz = _gelu_parts(zu)
        ln_g = g_ref[...]
        v, vhat, rstd, dgv_dz = _gmlp_norm(zv_ref[...], ln_g, b_ref[...])
        v16 = v.astype(BF16)
        dout = dout_ref[...]
        dmixed = dout * u
        dm16 = dmixed.astype(BF16)
        mask = _tril(GM_CHUNK)
        bt = bt_ref[...]
        group_id = lax.broadcasted_iota(jnp.int32, (1, GM_GROUPS), 1)
        dbt = jnp.zeros((GM_CHUNK, GM_GROUPS), F32)
        for g in range(GM_GROUPS):
            wm16 = jnp.where(mask, w_ref[g], 0.0).astype(BF16)
            cols = slice(g * GM_GDIM, (g + 1) * GM_GDIM)
            dw = jnp.zeros((GM_CHUNK, GM_CHUNK), F32)
            dbt_g = jnp.zeros((GM_CHUNK, 1), F32)
            for c in range(GM_TM // GM_CHUNK):
                rows = slice(c * GM_CHUNK, (c + 1) * GM_CHUNK)
                mixed = jnp.dot(wm16, v16[rows, cols], preferred_element_type=F32) + bt[:, g:g + 1]
                dz_ref[rows, cols] = (dout[rows, cols] * mixed * du_dz[rows, cols]).astype(BF16)
                dw += lax.dot_general(dm16[rows, cols], v16[rows, cols], _NT, preferred_element_type=F32)
                dbt_g += jnp.sum(dmixed[rows, cols], axis=-1, keepdims=True)
                dv_ref[rows, cols] = lax.dot_general(wm16, dm16[rows, cols], _TN, preferred_element_type=F32)
            dw_ref[g] += jnp.where(mask, dw, 0.0)
            dbt = dbt + dbt_g * (group_id == g).astype(F32)
        dbt_ref[...] += dbt
        dv = dv_ref[...]
        dg_ref[...] += jnp.sum(dv * vhat, axis=0, keepdims=True)
        db_ref[...] += jnp.sum(dv, axis=0, keepdims=True)
        dvh = dv * ln_g
        dgv = rstd * (dvh - jnp.mean(dvh, axis=-1, keepdims=True) - vhat * jnp.mean(dvh * vhat, axis=-1, keepdims=True))
        dz_ref[:, GM_WIDTH:2 * GM_WIDTH] = (dgv * dgv_dz).astype(BF16)

    half, vec, w, bt = _gmlp_specs()
    dz_width = 2 * GM_WIDTH + XA_HEADS * XA_DIM
    return pl.pallas_call(
        body,
        name="gmlp_bwd",
        grid=(N_TOK // GM_TM,),
        in_specs=[half(0), half(1), half(0), pl.BlockSpec((GM_TM, XA_HEADS * XA_DIM), lambda i: (i, 0)), vec, vec, w, bt],
        out_specs=[pl.BlockSpec((GM_TM, dz_width), lambda i: (i, 0)), w, bt, vec, vec],
        out_shape=[jax.ShapeDtypeStruct((N_TOK, dz_width), BF16),
                   jax.ShapeDtypeStruct((GM_GROUPS, GM_CHUNK, GM_CHUNK), F32),
                   jax.ShapeDtypeStruct((GM_CHUNK, GM_GROUPS), F32),
                   jax.ShapeDtypeStruct((1, GM_WIDTH), F32), jax.ShapeDtypeStruct((1, GM_WIDTH), F32)],
        scratch_shapes=[pltpu.VMEM((GM_TM, GM_WIDTH), F32)],
        compiler_params=_cp("arbitrary"),
    )(z, z, dcat, dq_mem, ln_g, ln_b, w_s, b_st)


def _own_slot(shape):
    return pl.BlockSpec((None,) + tuple(shape), lambda i, me_ref: (me_ref[0],) + (0,) * len(shape))


def _place_rows(w, layer, cuts_columns, me, *, name, deps=()):
    _, r, c = w.shape
    n = c if cuts_columns else r

    def body(me_ref, w_ref, *rest):
        o_ref = rest[len(deps)]
        wv = w_ref[...]
        o_ref[...] = (wv.T if cuts_columns else wv).astype(BF16)

    return pl.pallas_call(
        body,
        name=name,
        grid_spec=pltpu.PrefetchScalarGridSpec(
            num_scalar_prefetch=1, grid=(1,),
            in_specs=[pl.BlockSpec((None, r, c), lambda i, me_ref: (layer, 0, 0))] + [ANY_SPEC] * len(deps),
            out_specs=_own_slot((n, D_MODEL))),
        out_shape=jax.ShapeDtypeStruct((N_DEV, n, D_MODEL), BF16),
        compiler_params=_cp("arbitrary"),
    )(me, w, *deps)


def _place_ln(ln_g, ln_b, me):
    blk = ln_g.shape[1]

    def body(me_ref, g_ref, b_ref, o_ref):
        o_ref[...] = jnp.zeros_like(o_ref)
        o_ref[0:1, :] = g_ref[...]
        o_ref[1:2, :] = b_ref[...]

    vec = pl.BlockSpec((1, blk), lambda i, me_ref: (0, 0))
    return pl.pallas_call(
        body,
        name="place_ln",
        grid_spec=pltpu.PrefetchScalarGridSpec(
            num_scalar_prefetch=1, grid=(1,), in_specs=[vec, vec], out_specs=_own_slot((8, blk))),
        out_shape=jax.ShapeDtypeStruct((N_DEV, 8, blk), F32),
        compiler_params=_cp("arbitrary"),
    )(me, ln_g, ln_b)


def _place_slab(a, me, *, name):
    def body(me_ref, a_ref, o_ref):
        o_ref[...] = a_ref[...]

    return pl.pallas_call(
        body,
        name=name,
        grid_spec=pltpu.PrefetchScalarGridSpec(
            num_scalar_prefetch=1, grid=(1,),
            in_specs=[pl.BlockSpec(a.shape, lambda i, me_ref: (0, 0))], out_specs=_own_slot(a.shape)),
        out_shape=jax.ShapeDtypeStruct((N_DEV,) + a.shape, a.dtype),
        compiler_params=_cp("arbitrary"),
    )(me, a)


def _place_own(grads, me, *, name):
    k = len(grads)

    def body(me_ref, *refs):
        for src, dst in zip(refs[:k], refs[k:]):
            dst[...] = src[...]

    specs = [_own_slot(g.shape[1:]) for g in grads]
    return pl.pallas_call(
        body,
        name=name,
        grid_spec=pltpu.PrefetchScalarGridSpec(num_scalar_prefetch=1, grid=(1,), in_specs=specs, out_specs=specs),
        out_shape=[jax.ShapeDtypeStruct(g.shape, g.dtype) for g in grads],
        compiler_params=_cp("arbitrary"),
    )(me, *grads)


def _mesh_pos():
    x, y, c = (lax.axis_index(a) for a in MESH_AXES)
    return x, y, c, 4 * x + 2 * y + c


def _peer(x, y, c, r):
    px = 1 - x if r & 4 else x
    py = 1 - y if r & 2 else y
    pc = 1 - c if r & 1 else c
    return (px, py, pc), 4 * px + 2 * py + pc


RELATIONS = {"scatter": (1, 2, 3, 4, 5, 6, 7), "gather_all": (1, 2, 3, 4, 5, 6, 7), "gather_chips": (1, 2, 4, 6),
             "gather_sibling": (2, 4, 6)}


def _peer_copies(srcs, lands, send_sems, recv_sems, mode, waits):
    x, y, c, me = _mesh_pos()
    rel = RELATIONS[mode]
    pairs = []
    for ri, r in enumerate(rel):
        if mode == "gather_sibling":
            peer, _ = _peer(x, y, c, 1)
            _, sent_blk = _peer(x, y, c, r)
            _, got_blk = _peer(x, y, c, r ^ 1)
        else:
            peer, peer_blk = _peer(x, y, c, r)
            sent_blk, got_blk = (peer_blk if mode == "scatter" else me), peer_blk
        for k, (src, land) in enumerate(zip(srcs, lands)):
            idx = k * len(rel) + ri
            sems = dict(send_sem=send_sems.at[idx], recv_sem=recv_sems.at[idx], device_id=peer,
                        device_id_type=pl.DeviceIdType.MESH)
            dst_blk = sent_blk if mode == "gather_sibling" else me
            mine = pltpu.make_async_remote_copy(src_ref=src.at[sent_blk], dst_ref=land.at[dst_blk], **sems)
            theirs = pltpu.make_async_remote_copy(src_ref=src.at[sent_blk], dst_ref=land.at[got_blk], **sems) if waits else None
            pairs.append((mine, theirs))
    return pairs


DATAFLOW = pltpu.SideEffectType.DATAFLOW_SIDE_EFFECTING


def _in_hbm(a):
    return pltpu.with_memory_space_constraint(a, pltpu.HBM)


def _copies_start(srcs, lands, *, mode, name, deps=()):
    gather = mode != "scatter"
    arrs = list(lands) if gather else list(srcs) + list(lands)
    n, k, nd = len(arrs), len(lands), len(deps)

    def body(*refs):
        ins, send_sems, recv_sems, token = refs[:n], refs[n + nd], refs[n + nd + 1], refs[2 * n + nd + 2]
        src_refs, land_refs = (ins, ins) if gather else (ins[:k], ins[k:])
        for mine, _ in _peer_copies(src_refs, land_refs, send_sems, recv_sems, mode, waits=False):
            mine.start()
        token[...] = jnp.zeros_like(token)

    n_cp = k * len(RELATIONS[mode])
    return pl.pallas_call(
        body,
        name=name,
        in_specs=[HBM_SPEC] * n + [ANY_SPEC] * nd,
        out_specs=(SEM_SPEC, SEM_SPEC, *[HBM_SPEC] * n, pl.BlockSpec(memory_space=pltpu.VMEM)),
        out_shape=(pltpu.SemaphoreType.DMA((n_cp,)), pltpu.SemaphoreType.DMA((n_cp,)),
                   *[pltpu.HBM(a.shape, a.dtype) for a in arrs], jax.ShapeDtypeStruct((8, 128), F32)),
        input_output_aliases={i: 2 + i for i in range(n)},
        compiler_params=pltpu.CompilerParams(has_side_effects=DATAFLOW),
    )(*[_in_hbm(a) for a in arrs], *deps)


def _copies_wait(arrs, send_sems, recv_sems, after, *, n_lands, mode, name):
    n, k = len(arrs), n_lands
    gather = mode != "scatter"

    def body(*refs):
        ins, send_sems, recv_sems = refs[:n], refs[n], refs[n + 1]
        src_refs, land_refs = (ins, ins) if gather else (ins[:k], ins[k:])
        for mine, theirs in _peer_copies(src_refs, land_refs, send_sems, recv_sems, mode, waits=True):
            mine.wait_send()
            theirs.wait_recv()

    outs = pl.pallas_call(
        body,
        name=name,
        in_specs=[HBM_SPEC] * n + [SEM_SPEC, SEM_SPEC] + [ANY_SPEC] * len(after),
        out_specs=[HBM_SPEC] * n,
        out_shape=[pltpu.HBM(a.shape, a.dtype) for a in arrs],
        input_output_aliases={i: i for i in range(n)},
        compiler_params=pltpu.CompilerParams(has_side_effects=DATAFLOW),
    )(*arrs, send_sems, recv_sems, *after)
    return outs[n - k:]


def _adamw(w, g, m, v):
    m = ADAM_B1 * m + (1.0 - ADAM_B1) * g
    v = ADAM_B2 * v + (1.0 - ADAM_B2) * (g * g)
    m_hat = m / (1.0 - ADAM_B1 ** ADAM_STEP)
    v_hat = v / (1.0 - ADAM_B2 ** ADAM_STEP)
    return -ADAM_LR * (m_hat / (jnp.sqrt(v_hat) + ADAM_EPS) + ADAM_WD * w), m, v


ADAM_TC = 512


def _adam_big(slots, w, m, v, cuts_columns, *, name):
    layers, n, nj = len(slots), slots[0].shape[1], D_MODEL // ADAM_TC

    def body(*refs):
        s_refs = refs[:layers]
        w_ref, m_ref, v_ref, g_ref, d_ref, nm_ref, nv_ref, acc_ref = refs[layers:]
        for ll in range(layers):
            @pl.when(pl.program_id(0) == ll)
            def _(s_ref=s_refs[ll]):
                g = s_ref[0].astype(F32)
                for s in range(1, N_DEV):
                    g = g + s_ref[s].astype(F32)
                acc_ref[...] = g

        g = acc_ref[...].T if cuts_columns else acc_ref[...]
        g_ref[...] = g
        d_ref[...], nm_ref[...], nv_ref[...] = _adamw(w_ref[...], g, m_ref[...], v_ref[...])

    def slot_spec(ll):
        return pl.BlockSpec((N_DEV, n, ADAM_TC),
                            lambda l, j: (0, 0, jnp.where(l < ll, 0, jnp.where(l > ll, nj - 1, j))))

    if cuts_columns:
        w_spec = pl.BlockSpec((None, ADAM_TC, n), lambda l, j: (l, j, 0))
    else:
        w_spec = pl.BlockSpec((None, n, ADAM_TC), lambda l, j: (l, 0, j))
    return pl.pallas_call(
        body,
        name=name,
        grid=(layers, nj),
        in_specs=[slot_spec(ll) for ll in range(layers)] + [w_spec] * 3,
        out_specs=[w_spec] * 4,
        out_shape=[jax.ShapeDtypeStruct(w.shape, F32)] * 4,
        scratch_shapes=[pltpu.VMEM((n, ADAM_TC), F32)],
        compiler_params=_cp("arbitrary", "arbitrary"),
    )(*slots, w, m, v)


def _adam_slabs(slots, ws, ms, vs):
    n = len(slots)

    def body(*refs):
        ins, outs = refs[:4 * n], refs[4 * n:]
        for k in range(n):
            s_ref, w_ref, m_ref, v_ref = ins[k], ins[n + k], ins[2 * n + k], ins[3 * n + k]
            g = s_ref[0]
            for s in range(1, N_DEV):
                g = g + s_ref[s]
            outs[4 * k][...] = g
            outs[4 * k + 1][...], outs[4 * k + 2][...], outs[4 * k + 3][...] = _adamw(w_ref[...], g, m_ref[...], v_ref[...])

    res = pl.pallas_call(
        body,
        name="small_adamw",
        out_shape=[jax.ShapeDtypeStruct(w.shape, F32) for w in ws for _ in range(4)],
        compiler_params=pltpu.CompilerParams(vmem_limit_bytes=VMEM_LIMIT_BYTES),
    )(*slots, *ws, *ms, *vs)
    return [res[4 * k:4 * k + 4] for k in range(n)]


def _adam_vecs(gs, ws, ms, vs):
    n = len(gs)

    def body(*refs):
        ins, outs = refs[:4 * n], refs[4 * n:]
        for k in range(n):
            outs[3 * k][...], outs[3 * k + 1][...], outs[3 * k + 2][...] = _adamw(
                ins[n + k][...], ins[k][...], ins[2 * n + k][...], ins[3 * n + k][...])

    res = pl.pallas_call(
        body,
        name="ln_adamw",
        out_shape=[jax.ShapeDtypeStruct(w.shape, F32) for w in ws for _ in range(3)],
        compiler_params=pltpu.CompilerParams(vmem_limit_bytes=VMEM_LIMIT_BYTES),
    )(*gs, *ws, *ms, *vs)
    return [res[3 * k:3 * k + 3] for k in range(n)]


SLAB_AT = dict(mem_norm=0, lb_logits=1, ffn1_norm=4, mix_norm=6, hgrn_gnorm=8, gmlp_ln_g=9, gmlp_ln_b=11,
               gmlp_b_s=13, ffn2_norm=14, final_norm=16)
SLAB_ROWS = 24
LOSS_ROW = 17
SMALL_SHARDED = ("gmlp_ln_g", "gmlp_ln_b")


def _pack_slab(parts, *, name, deps=()):
    flat, plan = [], []
    for pname, at in SLAB_AT.items():
        for a in parts.get(pname, ()):
            flat.append(a)
            plan.append((at, a.shape))
            at += max(1, a.shape[0] * a.shape[1] // D_MODEL)
    for a in parts.get("loss", ()):
        flat.append(a)
        plan.append((LOSS_ROW, a.shape))

    def body(*refs):
        o_ref = refs[-1]
        o_ref[...] = jnp.zeros_like(o_ref)
        for ref, (at, (r, w)) in zip(refs, plan):
            if w == D_MODEL or r == 1 and w < D_MODEL:
                o_ref[at:at + r, 0:w] = ref[...]
            elif w < D_MODEL:
                for j in range(r):
                    o_ref[at:at + 1, j * w:(j + 1) * w] = ref[j:j + 1, :]
            else:
                for j in range(w // D_MODEL):
                    o_ref[at + j:at + j + 1, :] = ref[:, j * D_MODEL:(j + 1) * D_MODEL]

    return pl.pallas_call(
        body,
        name=name,
        in_specs=[pl.BlockSpec(memory_space=pltpu.VMEM)] * len(flat) + [ANY_SPEC] * len(deps),
        out_shape=jax.ShapeDtypeStruct((SLAB_ROWS, D_MODEL), F32),
        compiler_params=pltpu.CompilerParams(vmem_limit_bytes=VMEM_LIMIT_BYTES),
    )(*flat, *deps)


def _unpack_slab(slab, shapes):
    out = {}
    for pname, at in SLAB_AT.items():
        if pname in SMALL_SHARDED:
            continue
        size = math.prod(shapes[pname])
        rows = max(1, size // D_MODEL)
        out[pname] = slab[at:at + rows].reshape(-1)[:size].reshape(shapes[pname])
    return out


def _ffn_fwd(x, norm_g, block, layer, full, get_weights):
    tag = f"l{layer}_{block}"
    full.update(get_weights((layer, f"{block}_in"), (x,)))
    h, z, act = _norm_mm(x, norm_g, full[(f"{block}_w_in", layer)], swiglu=True, tm=512, tn=1408, deps=full.pop("deps", ()),
                         name=f"{tag}_in")
    full.update(get_weights((layer, f"{block}_out"), (act,)))
    y = _mm(act, full[(f"{block}_w_out", layer)], tm=512, tn=D_MODEL, tk=D_FF, out_dtype=F32, res=x, scale=0.5,
            deps=full.pop("deps", ()), name=f"{tag}_out")
    return y, (x, h, z, act)


def _ffn_bwd(dy, dy16, saved, norm_g, w_in_t, w_out, tag, deps=(), after_out_wgrad=None, before_in_wgrad=None):
    x, h, z, act = saved
    dw_out = _mm(act, dy16, ta=True, tm=1408, tn=D_MODEL, tk=N_TOK, out_dtype=BF16, scale=0.5, deps=deps,
                 name=f"{tag}_out_wgrad")
    sent = after_out_wgrad(dw_out) if after_out_wgrad is not None else ()
    dz = _swiglu_dgrad(dy16, w_out, z, scale=0.5, deps=sent, name=f"{tag}_out_dgrad")
    if before_in_wgrad is None:
        dw_in_t = _planes_wgrad(dz, h, name=f"{tag}_in_wgrad")
        dx, dx16, dg = _dgrad_norm_bwd(dz, w_in_t, x, norm_g, dy, name=f"{tag}_in_dgrad")
    else:
        dx, dx16, dg = _dgrad_norm_bwd(dz, w_in_t, x, norm_g, dy, name=f"{tag}_in_dgrad")
        dw_in_t = _planes_wgrad(dz, h, deps=before_in_wgrad(dg), name=f"{tag}_in_wgrad")
    return dx, dx16, dg, dw_in_t, dw_out


def kernel(x, mem, mem_norm, lb_logits, ffn1_norm, ffn1_w_in, ffn1_w_out, mix_norm, mem_w_kv, hgrn_w_in, hgrn_gnorm, hgrn_w_out, gmlp_w_in, gmlp_ln_g, gmlp_ln_b, gmlp_w_s, gmlp_b_s, gmlp_w_out, ffn2_norm, ffn2_w_in, ffn2_w_out, final_norm, loss_target, m_mem_norm, m_lb_logits, m_ffn1_norm, m_ffn1_w_in, m_ffn1_w_out, m_mix_norm, m_mem_w_kv, m_hgrn_w_in, m_hgrn_gnorm, m_hgrn_w_out, m_gmlp_w_in, m_gmlp_ln_g, m_gmlp_ln_b, m_gmlp_w_s, m_gmlp_b_s, m_gmlp_w_out, m_ffn2_norm, m_ffn2_w_in, m_ffn2_w_out, m_final_norm, v_mem_norm, v_lb_logits, v_ffn1_norm, v_ffn1_w_in, v_ffn1_w_out, v_mix_norm, v_mem_w_kv, v_hgrn_w_in, v_hgrn_gnorm, v_hgrn_w_out, v_gmlp_w_in, v_gmlp_ln_g, v_gmlp_ln_b, v_gmlp_w_s, v_gmlp_b_s, v_gmlp_w_out, v_ffn2_norm, v_ffn2_w_in, v_ffn2_w_out, v_final_norm):
    weights = dict(mem_norm=mem_norm, lb_logits=lb_logits, ffn1_norm=ffn1_norm, ffn1_w_in=ffn1_w_in, ffn1_w_out=ffn1_w_out, mix_norm=mix_norm, mem_w_kv=mem_w_kv, hgrn_w_in=hgrn_w_in, hgrn_gnorm=hgrn_gnorm, hgrn_w_out=hgrn_w_out, gmlp_w_in=gmlp_w_in, gmlp_ln_g=gmlp_ln_g, gmlp_ln_b=gmlp_ln_b, gmlp_w_s=gmlp_w_s, gmlp_b_s=gmlp_b_s, gmlp_w_out=gmlp_w_out, ffn2_norm=ffn2_norm, ffn2_w_in=ffn2_w_in, ffn2_w_out=ffn2_w_out, final_norm=final_norm)
    mom_m = dict(mem_norm=m_mem_norm, lb_logits=m_lb_logits, ffn1_norm=m_ffn1_norm, ffn1_w_in=m_ffn1_w_in, ffn1_w_out=m_ffn1_w_out, mix_norm=m_mix_norm, mem_w_kv=m_mem_w_kv, hgrn_w_in=m_hgrn_w_in, hgrn_gnorm=m_hgrn_gnorm, hgrn_w_out=m_hgrn_w_out, gmlp_w_in=m_gmlp_w_in, gmlp_ln_g=m_gmlp_ln_g, gmlp_ln_b=m_gmlp_ln_b, gmlp_w_s=m_gmlp_w_s, gmlp_b_s=m_gmlp_b_s, gmlp_w_out=m_gmlp_w_out, ffn2_norm=m_ffn2_norm, ffn2_w_in=m_ffn2_w_in, ffn2_w_out=m_ffn2_w_out, final_norm=m_final_norm)
    mom_v = dict(mem_norm=v_mem_norm, lb_logits=v_lb_logits, ffn1_norm=v_ffn1_norm, ffn1_w_in=v_ffn1_w_in, ffn1_w_out=v_ffn1_w_out, mix_norm=v_mix_norm, mem_w_kv=v_mem_w_kv, hgrn_w_in=v_hgrn_w_in, hgrn_gnorm=v_hgrn_gnorm, hgrn_w_out=v_hgrn_w_out, gmlp_w_in=v_gmlp_w_in, gmlp_ln_g=v_gmlp_ln_g, gmlp_ln_b=v_gmlp_ln_b, gmlp_w_s=v_gmlp_w_s, gmlp_b_s=v_gmlp_b_s, gmlp_w_out=v_gmlp_w_out, ffn2_norm=v_ffn2_norm, ffn2_w_in=v_ffn2_w_in, ffn2_w_out=v_ffn2_w_out, final_norm=v_final_norm)
    order = list(weights)
    _, _, _, me = _mesh_pos()
    me_arr = jnp.reshape(me, (1,)).astype(jnp.int32)
    cuts = {name: c for name, c, _, _ in GROUPS}
    rows_already = tuple(name for name, c, _, n in GROUPS if c and n % 128)
    as_rows = lambda a: jnp.transpose(a, (0, 2, 1))
    for name in rows_already:
        weights[name], mom_m[name], mom_v[name] = as_rows(weights[name]), as_rows(mom_m[name]), as_rows(mom_v[name])
        cuts[name] = False

    mix1 =(("mem_w_kv", 1), ("gmlp_w_in", 0), ("gmlp_w_out", 0))
    gather_plan = (
        ((0, "ffn1_in"), _stage_pieces(0, "ffn1")),
        ((0, "mix_in"), _stage_pieces(0, "mix")),
        ((0, "ffn2_in"), _stage_pieces(0, "ffn2")),
        ((1, "ffn1_in"), _stage_pieces(1, "ffn1")),
        ((1, "mix_in"), mix1),
        ((1, "ffn2_in"), _stage_pieces(1, "ffn2")),
    )
    stage_of = {use: k for k, (use, _) in enumerate(gather_plan)}
    in_flight = {}

    def place(k, deps=()):
        pieces = gather_plan[k][1]
        lands = [_place_rows(weights[name], l, cuts[name], me_arr, deps=deps, name=f"place_{name}_{l}")
                 for name, l in pieces]
        if pieces is mix1:
            lands.append(_place_ln(gmlp_ln_g, gmlp_ln_b, me_arr))
        return lands

    placed = {0: place(0)}

    def start_chips(k, deps):
        lands = placed[k]
        send_sems, recv_sems, *thru, token = _copies_start(lands, lands, mode="gather_chips", deps=deps,
                                                           name=f"gather{k}_chips_start")
        in_flight[k] = (thru, send_sems, recv_sems)
        return token

    def pass_to_sibling(k, after):
        thru, send_sems, recv_sems = in_flight[k]
        outs = _copies_wait(thru, send_sems, recv_sems, after, n_lands=len(thru), mode="gather_chips",
                            name=f"gather{k}_chips_wait")
        send_sems, recv_sems, *thru, token = _copies_start(outs, outs, mode="gather_sibling",
                                                           name=f"gather{k}_sibling_start")
        in_flight[k] = (thru, send_sems, recv_sems)
        return token, token

    first_sent = start_chips(0, ())
    placed.update({k: place(k, (first_sent,)) for k in range(1, len(gather_plan))})
    placed_later = tuple(a for k in range(1, len(gather_plan)) for a in placed[k])
    points = [(i, p) for i in (0, 1) for p in ("ffn1_in", "ffn1_out", "mix_in", "mix_out", "ffn2_in", "ffn2_out")]
    pass_at = {j: points[points.index(use) - 1] for j, (use, _) in enumerate(gather_plan) if j}
    pass_at[1] = gather_plan[1][0]

    started = {0}

    def get_weights(use, after):
        tokens, w = [], {}
        k = stage_of.get(use)

        def pass_on(j, after):
            token, landed = pass_to_sibling(j, after)
            tokens.append(token)
            if j + 1 < len(gather_plan) and j + 1 not in started:
                started.add(j + 1)
                tokens.append(start_chips(j + 1, (landed,)))

        if k == 0:
            pass_on(0, tuple(after) + placed_later)
        elif k is not None and pass_at[k] == use:
            pass_on(k, after)
        if k is not None:
            thru, send_sems, recv_sems = in_flight[k]
            outs = _copies_wait(thru, send_sems, recv_sems, after, n_lands=len(thru), mode="gather_sibling",
                                name=f"gather{k}_sibling_wait")
            after = (outs[0],)
            pieces = gather_plan[k][1]
            w = {p: o.reshape(N_DEV * o.shape[1], D_MODEL) for p, o in zip(pieces, outs)}
            if pieces is mix1:
                w["ln_g"] = outs[-1][:, 0, :].reshape(1, GM_WIDTH)
                w["ln_b"] = outs[-1][:, 1, :].reshape(1, GM_WIDTH)
        for j, at in pass_at.items():
            if at == use and j != k:
                pass_on(j, after)
        w["deps"] = tuple(tokens)
        return w

    scatter = {}

    def put_grads(st, grads):
        if st in ("w_s", "small"):
            slab = grads.reshape(GM_GROUPS * GM_CHUNK, GM_CHUNK) if st == "w_s" else _pack_slab(grads, name="pack_small_grads")
            land = _place_slab(slab, me_arr, name=f"{st}_place")
            send_sems, recv_sems, *thru, token = _copies_start([land], [land], mode="gather_all", name=f"{st}_start")
            scatter[st] = (thru, send_sems, recv_sems)
            return (token,)
        views = [g.reshape(N_DEV, -1, D_MODEL) for g in grads.values()]
        recv = _place_own(views, me_arr, name=f"scatter_place_l{st[0]}_{st[1]}")
        send_sems, recv_sems, *thru, token = _copies_start(views, recv, mode="scatter",
                                                           name=f"scatter_start_l{st[0]}_{st[1]}")
        scatter[st] = (tuple(grads), thru, send_sems, recv_sems)
        return (token,)

    dx, last_sent = _step_local(
        x, mem, loss_target, get_weights, put_grads, mem_norm, lb_logits, ffn1_norm, mix_norm, hgrn_gnorm,
        gmlp_w_s, gmlp_b_s, ffn2_norm, final_norm)

    slots = {}

    def wait_grads(blk, after, last=False):
        for st, entry in scatter.items():
            if isinstance(st, tuple) and st[1].startswith(blk) and (st == (0, "ffn1_in")) == last:
                pieces, thru, send_sems, recv_sems = entry
                outs = _copies_wait(thru, send_sems, recv_sems, after, n_lands=len(thru) // 2, mode="scatter",
                                    name=f"scatter_wait_l{st[0]}_{st[1]}")
                slots.update(zip(pieces, outs))

    grad, delta, new_m, new_v = {}, {}, {}, {}

    def adam_groups(names):
        for name in names:
            layers = GROUP_LAYERS[name]
            grad[name], delta[name], new_m[name], new_v[name] = _adam_big(
                [slots[(name, l)] for l in range(layers)], weights[name], mom_m[name], mom_v[name], cuts[name],
                name=f"{name}_adamw")

    wait_grads("ffn2", (dx, *last_sent))
    adam_groups(("ffn2_w_in", "ffn2_w_out"))
    wait_grads("mix", (delta["ffn2_w_out"],))
    adam_groups(("mem_w_kv", "gmlp_w_in", "gmlp_w_out", "hgrn_w_in", "hgrn_w_out"))
    wait_grads("ffn1", (delta["hgrn_w_out"],))
    adam_groups(("ffn1_w_out",))

    def small_parts(src):
        parts = {n: [src[n].reshape(-1, src[n].shape[-1])] for n in SLAB_AT if n not in SMALL_SHARDED}
        return parts

    w_s_rows = lambda a: a.reshape(GM_GROUPS * GM_CHUNK, GM_CHUNK)
    small_done = (delta["hgrn_w_out"],)
    (slab_slots,) = _copies_wait(*scatter["small"], small_done, n_lands=1, mode="gather_all", name="small_wait")
    (ws_slots,) = _copies_wait(*scatter["w_s"], small_done, n_lands=1, mode="gather_all", name="w_s_wait")
    (g_slab, d_slab, nm_slab, nv_slab), (g_ws, d_ws, nm_ws, nv_ws) = _adam_slabs(
        [slab_slots, ws_slots],
        [_pack_slab(small_parts(weights), deps=(dx,), name="pack_small_w"), w_s_rows(gmlp_w_s)],
        [_pack_slab(small_parts(mom_m), deps=(dx,), name="pack_small_m"), w_s_rows(m_gmlp_w_s)],
        [_pack_slab(small_parts(mom_v), deps=(dx,), name="pack_small_v"), w_s_rows(v_gmlp_w_s)])
    shapes = {n: weights[n].shape for n in SLAB_AT}
    for out, slab, ws in ((grad, g_slab, g_ws), (delta, d_slab, d_ws), (new_m, nm_slab, nm_ws), (new_v, nv_slab, nv_ws)):
        out.update(_unpack_slab(slab, shapes))
        out["gmlp_w_s"] = ws.reshape(gmlp_w_s.shape)
    blk = GM_WIDTH // N_DEV
    g_ln = [lax.dynamic_slice(g_slab[SLAB_AT[n]:SLAB_AT[n] + 2].reshape(1, GM_WIDTH), (0, me * blk), (1, blk))
            for n in SMALL_SHARDED]
    ln_out = _adam_vecs(g_ln, [weights[n] for n in SMALL_SHARDED], [mom_m[n] for n in SMALL_SHARDED],
                        [mom_v[n] for n in SMALL_SHARDED])
    for n, g, (d, nm, nv) in zip(SMALL_SHARDED, g_ln, ln_out):
        grad[n], delta[n], new_m[n], new_v[n] = g, d, nm, nv

    wait_grads("ffn1", tuple(delta[n] for n in delta if n in GROUP_LAYERS) + (d_slab,), last=True)
    adam_groups(("ffn1_w_in",))

    for name in rows_already:
        for out in (grad, delta, new_m, new_v):
            out[name] = as_rows(out[name])
    loss = g_slab[LOSS_ROW, 0]
    grad_x = dx.reshape(B_LOC, SEQ, D_MODEL)
    return (loss, grad_x, *[grad[n] for n in order], *[delta[n] for n in order],
            *[new_m[n] for n in order], *[new_v[n] for n in order])


def _step_local(x, mem, loss_target, get_weights, put_grads, mem_norm, lb_logits, ffn1_norm, mix_norm, hgrn_gnorm,
                gmlp_w_s, gmlp_b_s, ffn2_norm, final_norm):
    w_s = gmlp_w_s[0]
    b_st = gmlp_b_s[0].T

    xs = x.reshape(N_TOK, D_MODEL)
    mem2d = mem.reshape(B_LOC * MEM_LEN, D_MODEL)
    mem_g = mem_norm.reshape(1, D_MODEL)
    saved, full = [], {}
    for i in range(2):
        xs, s_ffn1 = _ffn_fwd(xs, ffn1_norm[i:i + 1], "ffn1", i, full, get_weights)
        if i == 0:
            memn = _rms_fwd(mem2d, mem_g, deps=(xs,), name="mem_norm_fwd")
        full.update(get_weights((i, "mix_in"), (xs,)))
        mixer = "hgrn" if i == 0 else "gmlp"
        hm, zm = _norm_mm(xs, mix_norm[i:i + 1], full[(f"{mixer}_w_in", 0)], swiglu=False, tm=1024, tn=1280, deps=full.pop("deps", ()),
                          name=f"l{i}_mix_in")
        kv = _mm(memn, full[("mem_w_kv", i)], tb=True, tm=512, tn=512, tk=D_MODEL, out_dtype=F32, name=f"l{i}_mem_kv")
        o_mem = _attn_fwd(zm, kv, name=f"l{i}_attn")
        if i == 0:
            cat, o_pre, s_all = _hgrn_fwd(zm, o_mem, lb_logits, hgrn_gnorm)
            mix_saved = (o_pre, s_all)
        else:
            cat = _gmlp_fwd(zm, o_mem, full["ln_g"], full["ln_b"], w_s, b_st)
            mix_saved = ()
        x_mix = xs
        full.update(get_weights((i, "mix_out"), (cat,)))
        xs = _mm(cat, full[(f"{mixer}_w_out", 0)], tm=512, tn=D_MODEL, tk=cat.shape[1], out_dtype=F32, res=xs,
                 deps=full.pop("deps", ()), name=f"l{i}_mix_out")
        xs, s_ffn2 = _ffn_fwd(xs, ffn2_norm[i:i + 1], "ffn2", i, full, get_weights)
        saved.append((s_ffn1, (x_mix, hm, kv, zm, cat, mix_saved), s_ffn2))

    dx, dx16, d_final, loss_part = _loss_head(xs, final_norm.reshape(1, D_MODEL), loss_target.reshape(N_TOK, D_MODEL))

    small = {"final_norm": [d_final], "loss": [loss_part]}
    d_ffn1, d_ffn2, d_mix = [None, None], [None, None], [None, None]
    dmemn = jnp.zeros((B_LOC * MEM_LEN, D_MODEL), F32)
    deps = ()
    for i in (1, 0):
        s_ffn1, (x_mix, hm, kv, zm, cat, mix_saved), s_ffn2 = saved[i]
        dx, dx16, d_ffn2[i], dw_in_t, dw_out = _ffn_bwd(
            dx, dx16, s_ffn2, ffn2_norm[i:i + 1], full[("ffn2_w_in", i)], full[("ffn2_w_out", i)], f"l{i}_ffn2", deps)
        deps = put_grads((i, "ffn2"), {("ffn2_w_in", i): dw_in_t, ("ffn2_w_out", i): dw_out})
        mixer = "hgrn" if i == 0 else "gmlp"
        w_in_t, w_out = full[(f"{mixer}_w_in", 0)], full[(f"{mixer}_w_out", 0)]
        width = cat.shape[1]
        g_mix = {}
        g_mix[(f"{mixer}_w_out", 0)] = _mm(cat, dx16, ta=True, tm=1024, tn=D_MODEL, tk=N_TOK, out_dtype=BF16,
                                           deps=deps, name=f"l{i}_mix_out_wgrad")
        dcat = _mm(dx16, w_out, tb=True, tm=1024, tn=width // 2, tk=D_MODEL, out_dtype=F32, name=f"l{i}_mix_out_dgrad")
        dq, dk, dv = _attn_bwd(zm, kv, dcat, do_off=width - XA_HEADS * XA_DIM, name=f"l{i}_attn_bwd")
        if i == 0:
            dzm, dlbl, dgn = _hgrn_bwd(zm, mix_saved[0], dcat, dq, mix_saved[1], lb_logits, hgrn_gnorm)
            small["lb_logits"], small["hgrn_gnorm"] = [dlbl], [dgn]
            deps = ()
        else:
            dzm, dws, dbt, dlng, dlnb = _gmlp_bwd(zm, dcat, dq, full["ln_g"], full["ln_b"], w_s, b_st)
            small["gmlp_b_s"], small["gmlp_ln_g"], small["gmlp_ln_b"] = [dbt.T], [dlng], [dlnb]
            deps = put_grads("w_s", dws)
        g_mix[(f"{mixer}_w_in", 0)] = _mm(dzm, hm, ta=True, tm=1024, tn=D_MODEL, tk=N_TOK, out_dtype=BF16, deps=deps,
                                          name=f"l{i}_mix_in_wgrad")
        dkv = jnp.concatenate([dk, dv], axis=1)
        g_mix[("mem_w_kv", i)] = _mm(dkv, memn, ta=True, tm=512, tn=D_MODEL, tk=B_LOC * MEM_LEN, out_dtype=BF16,
                                     name=f"l{i}_mem_kv_wgrad")
        deps = put_grads((i, "mix"), g_mix)
        dx, dx16, d_mix[i] = _dgrad_norm_bwd(dzm, w_in_t, x_mix, mix_norm[i:i + 1], dx, deps=deps,
                                             name=f"l{i}_mix_in_dgrad")
        dmemn = _mm(dkv, full[("mem_w_kv", i)], tm=B_LOC * MEM_LEN, tn=D_MODEL, tk=512, out_dtype=F32, res=dmemn,
                    name=f"l{i}_mem_kv_dgrad")
        def send_small(dg, i=i, dmemn=dmemn):
            d_ffn1[i] = dg
            _, _, dmem_g = _rms_bwd(mem2d, mem_g, dmemn, dmemn, name="mem_norm_bwd")
            small.update(mem_norm=[dmem_g], ffn1_norm=d_ffn1, ffn2_norm=d_ffn2, mix_norm=d_mix)
            return put_grads("small", small)

        if i == 0:
            send_out = lambda dw_out: put_grads((0, "ffn1_out"), {("ffn1_w_out", 0): dw_out})
            dx, dx16, d_ffn1[i], dw_in_t, _ = _ffn_bwd(
                dx, dx16, s_ffn1, ffn1_norm[i:i + 1], full[("ffn1_w_in", i)], full[("ffn1_w_out", i)], f"l{i}_ffn1",
                after_out_wgrad=send_out, before_in_wgrad=send_small)
            deps = put_grads((0, "ffn1_in"), {("ffn1_w_in", 0): dw_in_t})
        else:
            dx, dx16, d_ffn1[i], dw_in_t, dw_out = _ffn_bwd(
                dx, dx16, s_ffn1, ffn1_norm[i:i + 1], full[("ffn1_w_in", i)], full[("ffn1_w_out", i)], f"l{i}_ffn1")
            deps = put_grads((i, "ffn1"), {("ffn1_w_in", i): dw_in_t, ("ffn1_w_out", i): dw_out})
    return dx, deps
```

```python
import functools
import math

import jax
import jax.numpy as jnp
from jax import lax
from jax.experimental import pallas as pl
from jax.experimental.pallas import tpu as pltpu

F32 = jnp.float32
BF16 = jnp.bfloat16

D_MODEL = 1024
SEQ = 2048
B_LOC = 2
N_TOK = B_LOC * SEQ
MEM_LEN = 256
N_DEV = 8
EPS = 1e-6
D_FF = 2816
HG_HEADS = 8
HG_DIM = 128
HG_CHUNK = 64
HG_NCHUNK = SEQ // HG_CHUNK
GM_CHUNK = 128
GM_GROUPS = 8
GM_WIDTH = 2048
GM_GDIM = GM_WIDTH // GM_GROUPS
XA_HEADS = 4
XA_DIM = 256
XA_OFF = 4096

ADAM_LR = 0.001
ADAM_B1 = 0.9
ADAM_B2 = 0.999
ADAM_EPS = 1e-08
ADAM_WD = 0.01
ADAM_STEP = 10

VMEM_LIMIT_BYTES = 56 * 1024 * 1024
MESH_AXES = ("x", "y", "c")

GROUPS = (
    ("ffn1_w_in", True, 2, 704),
    ("ffn1_w_out", False, 2, 352),
    ("mem_w_kv", True, 2, 256),
    ("hgrn_w_in", True, 1, 640),
    ("hgrn_w_out", False, 1, 256),
    ("gmlp_w_in", True, 1, 640),
    ("gmlp_w_out", False, 1, 384),
    ("ffn2_w_in", True, 2, 704),
    ("ffn2_w_out", False, 2, 352),
)
GROUP_LAYERS = {name: layers for name, _, layers, _ in GROUPS}


def _stage_pieces(layer, block):
    if block == "mix":
        mixer = "hgrn" if layer == 0 else "gmlp"
        return (("mem_w_kv", layer), (f"{mixer}_w_in", 0), (f"{mixer}_w_out", 0))
    return ((f"{block}_w_in", layer), (f"{block}_w_out", layer))


ANY_SPEC = pl.BlockSpec(memory_space=pl.ANY)
HBM_SPEC = pl.BlockSpec(memory_space=pltpu.HBM)
SEM_SPEC = pl.BlockSpec(memory_space=pltpu.SEMAPHORE)


def _cp(*sem):
    return pltpu.CompilerParams(dimension_semantics=sem, vmem_limit_bytes=VMEM_LIMIT_BYTES)


def _sigmoid(x):
    return 0.5 * jnp.tanh(0.5 * x) + 0.5


def _gelu_parts(x):
    cdf = 0.5 * (1.0 + lax.erf(x * (1.0 / math.sqrt(2.0))))
    pdf = jnp.exp(-0.5 * x * x) * (1.0 / math.sqrt(2.0 * math.pi))
    return x * cdf, cdf + x * pdf


def _mm(a, b, *, ta=False, tb=False, tm, tn, tk, out_dtype, res=None, scale=1.0, deps=(), name):
    m, k = (a.shape[1], a.shape[0]) if ta else a.shape
    n, kb = b.shape if tb else (b.shape[1], b.shape[0])
    assert k == kb and m % tm == 0 and n % tn == 0 and k % tk == 0, (name, a.shape, b.shape)
    nk = k // tk
    dn = (((0 if ta else 1,), (1 if tb else 0,)), ((), ()))
    n_in = 2 + (res is not None) + len(deps)

    def body(*refs):
        a_ref, b_ref = refs[:2]
        r_ref = refs[2] if res is not None else None
        o_ref, scr = refs[n_in], refs[n_in + 1:]
        p = lax.dot_general(a_ref[...].astype(BF16), b_ref[...].astype(BF16), dn, preferred_element_type=F32)

        def finish(acc):
            if scale != 1.0:
                acc = scale * acc
            if r_ref is not None:
                acc = r_ref[...] + acc
            o_ref[...] = acc.astype(out_dtype)

        if nk == 1:
            finish(p)
        else:
            acc_ref = scr[0]
            kk = pl.program_id(2)

            @pl.when(kk == 0)
            def _():
                acc_ref[...] = p

            @pl.when(kk > 0)
            def _():
                acc_ref[...] += p

            @pl.when(kk == nk - 1)
            def _():
                finish(acc_ref[...])

    a_spec = pl.BlockSpec((tk, tm), lambda i, j, kk: (kk, i)) if ta else pl.BlockSpec((tm, tk), lambda i, j, kk: (i, kk))
    b_mode = dict(pipeline_mode=pl.Buffered(1)) if n == tn and nk == 1 else {}
    if tb:
        b_spec = pl.BlockSpec((tn, tk), lambda i, j, kk: (j, kk), **b_mode)
    else:
        b_spec = pl.BlockSpec((tk, tn), lambda i, j, kk: (kk, j), **b_mode)
    o_spec = pl.BlockSpec((tm, tn), lambda i, j, kk: (i, j))
    in_specs = [a_spec, b_spec] + ([o_spec] if res is not None else []) + [ANY_SPEC] * len(deps)
    args = (a, b) + ((res,) if res is not None else ()) + tuple(deps)
    return pl.pallas_call(
        body,
        name=name,
        grid=(m // tm, n // tn, nk),
        in_specs=in_specs,
        out_specs=o_spec,
        out_shape=jax.ShapeDtypeStruct((m, n), out_dtype),
        scratch_shapes=[pltpu.VMEM((tm, tn), F32)] if nk > 1 else [],
        compiler_params=_cp("parallel", "parallel", "arbitrary"),
    )(*args)


def _rms_fwd(x, g, *, name, deps=(), tm=512):
    rows = x.shape[0]

    def body(x_ref, g_ref, *rest):
        o_ref = rest[len(deps)]
        xv = x_ref[...]
        r = lax.rsqrt(jnp.mean(xv * xv, axis=-1, keepdims=True) + EPS)
        o_ref[...] = (xv * r * g_ref[...]).astype(BF16)

    row = pl.BlockSpec((tm, D_MODEL), lambda i: (i, 0))
    return pl.pallas_call(
        body,
        name=name,
        grid=(rows // tm,),
        in_specs=[row, pl.BlockSpec((1, D_MODEL), lambda i: (0, 0))] + [ANY_SPEC] * len(deps),
        out_specs=row,
        out_shape=jax.ShapeDtypeStruct((rows, D_MODEL), BF16),
        compiler_params=_cp("parallel"),
    )(x, g, *deps)


def _rms_bwd(x, g, dh, dres, *, name, deps=(), tm=512):
    rows = x.shape[0]

    def body(x_ref, g_ref, dh_ref, dres_ref, *rest):
        dx_ref, dx16_ref, dg_ref = rest[len(deps):]
        xv = x_ref[...]
        r = lax.rsqrt(jnp.mean(xv * xv, axis=-1, keepdims=True) + EPS)
        xhat = xv * r
        dhv = dh_ref[...]
        part = jnp.sum(dhv * xhat, axis=0, keepdims=True)

        @pl.when(pl.program_id(0) == 0)
        def _():
            dg_ref[...] = part

        @pl.when(pl.program_id(0) > 0)
        def _():
            dg_ref[...] += part

        dxh = dhv * g_ref[...]
        dx = dres_ref[...] + r * (dxh - xhat * jnp.mean(dxh * xhat, axis=-1, keepdims=True))
        dx_ref[...] = dx
        dx16_ref[...] = dx.astype(BF16)

    row = pl.BlockSpec((tm, D_MODEL), lambda i: (i, 0))
    vec = pl.BlockSpec((1, D_MODEL), lambda i: (0, 0))
    return pl.pallas_call(
        body,
        name=name,
        grid=(rows // tm,),
        in_specs=[row, vec, row, row] + [ANY_SPEC] * len(deps),
        out_specs=[row, row, vec],
        out_shape=[jax.ShapeDtypeStruct((rows, D_MODEL), F32), jax.ShapeDtypeStruct((rows, D_MODEL), BF16),
                   jax.ShapeDtypeStruct((1, D_MODEL), F32)],
        compiler_params=_cp("arbitrary"),
    )(x, g, dh, dres, *deps)


_NT = (((1,), (1,)), ((), ()))
_TN = (((0,), (0,)), ((), ()))


def _norm_mm(x, g, w_t, *, swiglu, name, tm, tn, deps=()):
    rows = w_t.shape[0]
    half = rows // 2
    nj = (half if swiglu else rows) // tn
    nd = len(deps)

    def body(x_ref, g_ref, w_ref, *rest):
        outs = rest[nd:]
        h_ref, z_ref = outs[:2]

        def norm():
            xv = x_ref[...]
            r = lax.rsqrt(jnp.mean(xv * xv, axis=-1, keepdims=True) + EPS)
            h_ref[...] = (xv * r * g_ref[...]).astype(BF16)

        if swiglu:
            norm()
            h = h_ref[...]
            for j in range(nj):
                cols = slice(j * tn, (j + 1) * tn)
                gate = lax.dot_general(h, w_ref[j * tn:(j + 1) * tn, :], _NT, preferred_element_type=F32)
                up = lax.dot_general(h, w_ref[half + j * tn:half + (j + 1) * tn, :], _NT, preferred_element_type=F32)
                s = _sigmoid(gate)
                silu = gate * s
                z_ref[0, :, cols] = (up * (s + silu * (1.0 - s))).astype(BF16)
                z_ref[1, :, cols] = silu.astype(BF16)
                outs[2][:, cols] = (silu * up).astype(BF16)
        else:
            j = pl.program_id(1)
            pl.when(j == 0)(norm)
            w = w_ref[pl.ds(pl.multiple_of(j * tn, tn), tn), :]
            z_ref[...] = lax.dot_general(h_ref[...], w, _NT, preferred_element_type=F32)

    grid = (N_TOK // tm,) if swiglu else (N_TOK // tm, nj)
    row = pl.BlockSpec((tm, D_MODEL), lambda i, *_: (i, 0))
    out_specs = [row]
    out_shape = [jax.ShapeDtypeStruct((N_TOK, D_MODEL), BF16)]
    if swiglu:
        out_specs += [pl.BlockSpec((2, tm, half), lambda i: (0, i, 0)), pl.BlockSpec((tm, half), lambda i: (i, 0))]
        out_shape += [jax.ShapeDtypeStruct((2, N_TOK, half), BF16), jax.ShapeDtypeStruct((N_TOK, half), BF16)]
    else:
        out_specs.append(pl.BlockSpec((tm, tn), lambda i, j: (i, j)))
        out_shape.append(jax.ShapeDtypeStruct((N_TOK, rows), F32))
    return pl.pallas_call(
        body,
        name=name,
        grid=grid,
        in_specs=[row, pl.BlockSpec((1, D_MODEL), lambda *_: (0, 0)),
                  pl.BlockSpec((rows, D_MODEL), lambda *_: (0, 0), pipeline_mode=pl.Buffered(1))] + [ANY_SPEC] * nd,
        out_specs=out_specs,
        out_shape=out_shape,
        compiler_params=_cp(*(("parallel",) if swiglu else ("parallel", "arbitrary"))),
    )(x, g, w_t, *deps)


def _swiglu_dgrad(dy16, w_out, z, *, scale, name, deps=(), tm=512, tn=1408):
    def body(dy_ref, w_ref, z_ref, *rest):
        dz_ref = rest[len(deps)]
        dy = dy_ref[...]
        for j in range(D_FF // tn):
            cols = slice(j * tn, (j + 1) * tn)
            da = lax.dot_general(dy, w_ref[cols, :], _NT, preferred_element_type=F32) * scale
            dz_ref[0, :, cols] = (da * z_ref[0, :, cols].astype(F32)).astype(BF16)
            dz_ref[1, :, cols] = (da * z_ref[1, :, cols].astype(F32)).astype(BF16)

    planes = pl.BlockSpec((2, tm, D_FF), lambda i: (0, i, 0))
    return pl.pallas_call(
        body,
        name=name,
        grid=(N_TOK // tm,),
        in_specs=[pl.BlockSpec((tm, D_MODEL), lambda i: (i, 0)),
                  pl.BlockSpec((D_FF, D_MODEL), lambda i: (0, 0), pipeline_mode=pl.Buffered(1)), planes]
        + [ANY_SPEC] * len(deps),
        out_specs=planes,
        out_shape=jax.ShapeDtypeStruct((2, N_TOK, D_FF), BF16),
        compiler_params=_cp("parallel"),
    )(dy16, w_out, z, *deps)


def _ffn_dgrad(dy16, dres, w_out, z, w_in_t, x, g, *, scale, name, deps=(), tm=256, tn=1408):
    nd = len(deps)

    def body(dy_ref, dres_ref, wo_ref, z_ref, wi_ref, x_ref, g_ref, *rest):
        dz_ref, dx_ref, dx16_ref, dg_ref = rest[nd:]
        dy = dy_ref[...]
        for j in range(D_FF // tn):
            cols = slice(j * tn, (j + 1) * tn)
            da = lax.dot_general(dy, wo_ref[cols, :], _NT, preferred_element_type=F32) * scale
            dz_ref[0, :, cols] = (da * z_ref[0, :, cols].astype(F32)).astype(BF16)
            dz_ref[1, :, cols] = (da * z_ref[1, :, cols].astype(F32)).astype(BF16)
        dh = jnp.dot(dz_ref[0], wi_ref[:D_FF, :], preferred_element_type=F32) + jnp.dot(
            dz_ref[1], wi_ref[D_FF:, :], preferred_element_type=F32)
        xv = x_ref[...]
        r = lax.rsqrt(jnp.mean(xv * xv, axis=-1, keepdims=True) + EPS)
        xhat = xv * r
        part = jnp.sum(dh * xhat, axis=0, keepdims=True)

        @pl.when(pl.program_id(0) == 0)
        def _():
            dg_ref[...] = part

        @pl.when(pl.program_id(0) > 0)
        def _():
            dg_ref[...] += part

        dxh = dh * g_ref[...]
        dx = dres_ref[...] + r * (dxh - xhat * jnp.mean(dxh * xhat, axis=-1, keepdims=True))
        dx_ref[...] = dx
        dx16_ref[...] = dx.astype(BF16)

    row = pl.BlockSpec((tm, D_MODEL), lambda i: (i, 0))
    vec = pl.BlockSpec((1, D_MODEL), lambda i: (0, 0))
    planes = pl.BlockSpec((2, tm, D_FF), lambda i: (0, i, 0))
    whole = lambda rows: pl.BlockSpec((rows, D_MODEL), lambda i: (0, 0), pipeline_mode=pl.Buffered(1))
    return pl.pallas_call(
        body,
        name=name,
        grid=(N_TOK // tm,),
        in_specs=[row, row, whole(D_FF), planes, whole(2 * D_FF), row, vec] + [ANY_SPEC] * nd,
        out_specs=[planes, row, row, vec],
        out_shape=[jax.ShapeDtypeStruct((2, N_TOK, D_FF), BF16), jax.ShapeDtypeStruct((N_TOK, D_MODEL), F32),
                   jax.ShapeDtypeStruct((N_TOK, D_MODEL), BF16), jax.ShapeDtypeStruct((1, D_MODEL), F32)],
        compiler_params=_cp("arbitrary"),
    )(dy16, dres, w_out, z, w_in_t, x, g, *deps)


def _planes_wgrad(dz, h, *, name, deps=(), tm=1408):
    per_plane = D_FF // tm

    def body(a_ref, b_ref, *rest):
        o_ref = rest[len(deps)]
        o_ref[...] = lax.dot_general(a_ref[...], b_ref[...], _TN, preferred_element_type=F32).astype(BF16)

    return pl.pallas_call(
        body,
        name=name,
        grid=(2 * per_plane,),
        in_specs=[pl.BlockSpec((None, N_TOK, tm),
                               lambda i: (jnp.where(i < per_plane, 0, 1), 0, jnp.where(i < per_plane, i, i - per_plane))),
                  pl.BlockSpec((N_TOK, D_MODEL), lambda i: (0, 0), pipeline_mode=pl.Buffered(1))] + [ANY_SPEC] * len(deps),
        out_specs=pl.BlockSpec((tm, D_MODEL), lambda i: (i, 0)),
        out_shape=jax.ShapeDtypeStruct((2 * D_FF, D_MODEL), BF16),
        compiler_params=_cp("parallel"),
    )(dz, h, *deps)


def _dgrad_norm_bwd(dz, w_t, x, g, dres, *, name, deps=(), tm=512):
    planes = dz.ndim == 3
    rows = w_t.shape[0]
    half = rows // 2
    nd = len(deps)

    def body(a_ref, b_ref, x_ref, g_ref, dres_ref, *rest):
        dx_ref, dx16_ref, dg_ref = rest[nd:]
        if planes:
            dh = jnp.dot(a_ref[0], b_ref[:half, :], preferred_element_type=F32) + jnp.dot(
                a_ref[1], b_ref[half:, :], preferred_element_type=F32)
        else:
            dh = jnp.dot(a_ref[...], b_ref[...], preferred_element_type=F32)
        xv = x_ref[...]
        r = lax.rsqrt(jnp.mean(xv * xv, axis=-1, keepdims=True) + EPS)
        xhat = xv * r
        part = jnp.sum(dh * xhat, axis=0, keepdims=True)

        @pl.when(pl.program_id(0) == 0)
        def _():
            dg_ref[...] = part

        @pl.when(pl.program_id(0) > 0)
        def _():
            dg_ref[...] += part

        dxh = dh * g_ref[...]
        dx = dres_ref[...] + r * (dxh - xhat * jnp.mean(dxh * xhat, axis=-1, keepdims=True))
        dx_ref[...] = dx
        dx16_ref[...] = dx.astype(BF16)

    a_spec = pl.BlockSpec((2, tm, half), lambda i: (0, i, 0)) if planes else pl.BlockSpec((tm, rows), lambda i: (i, 0))
    row = pl.BlockSpec((tm, D_MODEL), lambda i: (i, 0))
    vec = pl.BlockSpec((1, D_MODEL), lambda i: (0, 0))
    return pl.pallas_call(
        body,
        name=name,
        grid=(N_TOK // tm,),
        in_specs=[a_spec, pl.BlockSpec((rows, D_MODEL), lambda i: (0, 0), pipeline_mode=pl.Buffered(1)), row, vec, row]
        + [ANY_SPEC] * nd,
        out_specs=[row, row, vec],
        out_shape=[jax.ShapeDtypeStruct((N_TOK, D_MODEL), F32), jax.ShapeDtypeStruct((N_TOK, D_MODEL), BF16),
                   jax.ShapeDtypeStruct((1, D_MODEL), F32)],
        compiler_params=_cp("arbitrary"),
    )(dz, w_t, x, g, dres, *deps)


def _loss_head(x, g, target, *, tm=512):
    def body(x_ref, g_ref, t_ref, dx_ref, dx16_ref, dg_ref, loss_ref):
        xv = x_ref[...]
        gv = g_ref[...]
        r = lax.rsqrt(jnp.mean(xv * xv, axis=-1, keepdims=True) + EPS)
        xhat = xv * r
        err = xhat * gv - t_ref[...]
        loss_part = jnp.zeros((1, 128), F32) + 0.5 * jnp.sum(jnp.mean(err * err, axis=-1, keepdims=True))
        dy = err * (1.0 / D_MODEL)
        dg_part = jnp.sum(dy * xhat, axis=0, keepdims=True)

        @pl.when(pl.program_id(0) == 0)
        def _():
            dg_ref[...] = dg_part
            loss_ref[...] = loss_part

        @pl.when(pl.program_id(0) > 0)
        def _():
            dg_ref[...] += dg_part
            loss_ref[...] += loss_part

        dxh = dy * gv
        dx = r * (dxh - xhat * jnp.mean(dxh * xhat, axis=-1, keepdims=True))
        dx_ref[...] = dx
        dx16_ref[...] = dx.astype(BF16)

    row = pl.BlockSpec((tm, D_MODEL), lambda i: (i, 0))
    vec = pl.BlockSpec((1, D_MODEL), lambda i: (0, 0))
    return pl.pallas_call(
        body,
        name="loss_head",
        grid=(N_TOK // tm,),
        in_specs=[row, vec, row],
        out_specs=[row, row, vec, pl.BlockSpec((1, 128), lambda i: (0, 0))],
        out_shape=[
            jax.ShapeDtypeStruct((N_TOK, D_MODEL), F32),
            jax.ShapeDtypeStruct((N_TOK, D_MODEL), BF16),
            jax.ShapeDtypeStruct((1, D_MODEL), F32),
            jax.ShapeDtypeStruct((1, 128), F32),
        ],
        compiler_params=_cp("arbitrary"),
    )(x, g, target)


XA_TQ = 2048
XA_SCALE = XA_DIM ** -0.5


def _attn_probs(q16, k16):
    s = lax.dot_general(q16, k16, _NT, preferred_element_type=F32) * XA_SCALE
    e = jnp.exp(s - jnp.max(s, axis=-1, keepdims=True))
    return e / jnp.sum(e, axis=-1, keepdims=True)


def _attn_fwd(z, kv, *, name):
    nt = SEQ // XA_TQ

    def body(q_ref, k_ref, v_ref, o_ref):
        p = _attn_probs(q_ref[...].astype(BF16), k_ref[...].astype(BF16))
        o_ref[...] = jnp.dot(p.astype(BF16), v_ref[...].astype(BF16), preferred_element_type=F32).astype(BF16)

    return pl.pallas_call(
        body,
        name=name,
        grid=(B_LOC, XA_HEADS, nt),
        in_specs=[
            pl.BlockSpec((XA_TQ, XA_DIM), lambda b, h, t: (b * nt + t, XA_OFF // XA_DIM + h)),
            pl.BlockSpec((MEM_LEN, XA_DIM), lambda b, h, t: (b, h)),
            pl.BlockSpec((MEM_LEN, XA_DIM), lambda b, h, t: (b, XA_HEADS + h)),
        ],
        out_specs=pl.BlockSpec((XA_TQ, XA_DIM), lambda b, h, t: (b * nt + t, h)),
        out_shape=jax.ShapeDtypeStruct((N_TOK, XA_HEADS * XA_DIM), BF16),
        compiler_params=_cp("parallel", "parallel", "arbitrary"),
    )(z, kv, kv)


def _attn_bwd(z, kv, dcat, *, do_off, name):
    nt = SEQ // XA_TQ

    def body(q_ref, k_ref, v_ref, do_ref, dq_ref, dk_ref, dv_ref):
        q16 = q_ref[...].astype(BF16)
        k16 = k_ref[...].astype(BF16)
        v16 = v_ref[...].astype(BF16)
        do16 = do_ref[...].astype(BF16)
        p = _attn_probs(q16, k16)
        dv_part = lax.dot_general(p.astype(BF16), do16, _TN, preferred_element_type=F32)
        dp = lax.dot_general(do16, v16, _NT, preferred_element_type=F32)
        ds16 = (p * (dp - jnp.sum(dp * p, axis=-1, keepdims=True)) * XA_SCALE).astype(BF16)
        dq_ref[...] = jnp.dot(ds16, k16, preferred_element_type=F32).astype(BF16)
        dk_part = lax.dot_general(ds16, q16, _TN, preferred_element_type=F32)

        @pl.when(pl.program_id(2) == 0)
        def _():
            dk_ref[...] = dk_part
            dv_ref[...] = dv_part

        @pl.when(pl.program_id(2) > 0)
        def _():
            dk_ref[...] += dk_part
            dv_ref[...] += dv_part

    qspec = pl.BlockSpec((XA_TQ, XA_DIM), lambda b, h, t: (b * nt + t, XA_OFF // XA_DIM + h))
    kspec = lambda off: pl.BlockSpec((MEM_LEN, XA_DIM), lambda b, h, t: (b, off + h))
    return pl.pallas_call(
        body,
        name=name,
        grid=(B_LOC, XA_HEADS, nt),
        in_specs=[qspec, kspec(0), kspec(XA_HEADS),
                  pl.BlockSpec((XA_TQ, XA_DIM), lambda b, h, t: (b * nt + t, do_off // XA_DIM + h))],
        out_specs=[pl.BlockSpec((XA_TQ, XA_DIM), lambda b, h, t: (b * nt + t, h)), kspec(0), kspec(0)],
        out_shape=[
            jax.ShapeDtypeStruct((N_TOK, XA_HEADS * XA_DIM), BF16),
            jax.ShapeDtypeStruct((B_LOC * MEM_LEN, XA_HEADS * XA_DIM), F32),
            jax.ShapeDtypeStruct((B_LOC * MEM_LEN, XA_HEADS * XA_DIM), F32),
        ],
        compiler_params=_cp("parallel", "parallel", "arbitrary"),
    )(z, kv, kv, dcat)


def _tril(n):
    return lax.broadcasted_iota(jnp.int32, (n, n), 0) >= lax.broadcasted_iota(jnp.int32, (n, n), 1)


def _lower_bound(lbl):
    e = jnp.exp(lbl - jnp.max(lbl, axis=0, keepdims=True))
    p = e / jnp.sum(e, axis=0, keepdims=True)
    return p[0:1, :], p


def _hgrn_gates(zq, zf, lb, tril_f):
    sig = _sigmoid(zf)
    f = lb + (1.0 - lb) * sig
    kk = 1.0 - f
    sq = _sigmoid(zq)
    q = zq * sq
    b = jnp.dot(tril_f, jnp.log(f), preferred_element_type=F32, precision=lax.Precision.HIGHEST)
    bl = b[HG_CHUNK - 1:HG_CHUNK, :]
    return q, sq, sig, f, kk, b, bl


HG_TB = 512
HG_CPB = HG_TB // HG_CHUNK
HG_NT = SEQ // HG_TB
HG_WIDTH = HG_HEADS * HG_DIM


def _head(h, section=0):
    return slice(section * HG_WIDTH + h * HG_DIM, section * HG_WIDTH + (h + 1) * HG_DIM)


def _hgrn_fwd(z, o_mem, lb_logits, gnorm):
    def body(zq_ref, zf_ref, zi_ref, zg_ref, omem_ref, lbl_ref, gn_ref, o_ref, opre_ref, sall_ref, st_ref):
        lb, _ = _lower_bound(lbl_ref[...])
        gn = gn_ref[...]
        mask = _tril(HG_CHUNK)
        tril_f = mask.astype(F32)
        o_ref[:, HG_WIDTH:] = omem_ref[...]

        @pl.when(pl.program_id(1) == 0)
        def _():
            st_ref[...] = jnp.zeros_like(st_ref)

        def chunk(c, carry):
            rows = pl.ds(pl.multiple_of(c * HG_CHUNK, HG_CHUNK), HG_CHUNK)
            q, _, _, _, kk, b, bl = _hgrn_gates(zq_ref[rows, :], zf_ref[rows, :], lb, tril_f)
            v16 = zi_ref[rows, :].astype(BF16)
            qd16 = (q * jnp.exp(b)).astype(BF16)
            ki16 = (kk * jnp.exp(-b)).astype(BF16)
            kd16 = (kk * jnp.exp(bl - b)).astype(BF16)
            ebl = jnp.exp(bl)
            zg = zg_ref[rows, :]
            gate = zg * _sigmoid(zg)
            for h in range(HG_HEADS):
                sl = _head(h)
                a = jnp.where(mask, lax.dot_general(qd16[:, sl], ki16[:, sl], _NT, preferred_element_type=F32), 0.0)
                st = st_ref[h]
                sall_ref[0, h, c] = st
                o = jnp.dot(a.astype(BF16), v16[:, sl], preferred_element_type=F32) + lax.dot_general(
                    qd16[:, sl], st.astype(BF16), _NT, preferred_element_type=F32)
                st_ref[h] = st * ebl[:, sl] + lax.dot_general(v16[:, sl], kd16[:, sl], _TN, preferred_element_type=F32)
                opre_ref[rows, sl] = o
                r = lax.rsqrt(jnp.mean(o * o, axis=-1, keepdims=True) + EPS)
                o_ref[rows, sl] = ((o * r * gn) * gate[:, sl]).astype(BF16)
            return carry

        lax.fori_loop(0, HG_CPB, chunk, 0, unroll=4)

    zspec = lambda s: pl.BlockSpec((HG_TB, HG_WIDTH), lambda b, t: (b * HG_NT + t, s))
    return pl.pallas_call(
        body,
        name="hgrn_fwd",
        grid=(B_LOC, HG_NT),
        in_specs=[zspec(0), zspec(1), zspec(2), zspec(3), zspec(0),
                  pl.BlockSpec((3, HG_WIDTH), lambda b, t: (0, 0)), pl.BlockSpec((1, HG_DIM), lambda b, t: (0, 0))],
        out_specs=[pl.BlockSpec((HG_TB, 2 * HG_WIDTH), lambda b, t: (b * HG_NT + t, 0)), zspec(0),
                   pl.BlockSpec((1, HG_HEADS, HG_CPB, HG_DIM, HG_DIM), lambda b, t: (b, 0, t, 0, 0))],
        out_shape=[
            jax.ShapeDtypeStruct((N_TOK, 2 * HG_WIDTH), BF16),
            jax.ShapeDtypeStruct((N_TOK, HG_WIDTH), F32),
            jax.ShapeDtypeStruct((B_LOC, HG_HEADS, HG_NCHUNK, HG_DIM, HG_DIM), F32),
        ],
        scratch_shapes=[pltpu.VMEM((HG_HEADS, HG_DIM, HG_DIM), F32)],
        compiler_params=_cp("parallel", "arbitrary"),
    )(z, z, z, z, o_mem, lb_logits, gnorm)


def _hgrn_bwd(z, opre, dcat, dq_mem, sall, lb_logits, gnorm):
    def body(zq_ref, zf_ref, zi_ref, zg_ref, opre_ref, dout_ref, dqm_ref, sall_ref, lbl_ref, gn_ref,
             dz_ref, dlbl_ref, dgn_ref, dst_ref, dlb_ref, dgn_acc, db_ref, dkk_ref, dbl_ref):
        b_id, t_id = pl.program_id(0), pl.program_id(1)
        lb, p = _lower_bound(lbl_ref[...])
        gn = gn_ref[...]
        mask = _tril(HG_CHUNK)
        tril_f = mask.astype(F32)
        dz_ref[:, 4 * HG_WIDTH:] = dqm_ref[...]

        @pl.when(t_id == 0)
        def _():
            dst_ref[...] = jnp.zeros_like(dst_ref)
            dlb_ref[...] = jnp.zeros_like(dlb_ref)

        @pl.when((b_id == 0) & (t_id == 0))
        def _():
            dgn_acc[...] = jnp.zeros_like(dgn_acc)

        def chunk(i, carry):
            c = HG_CPB - 1 - i
            rows = pl.ds(pl.multiple_of(c * HG_CHUNK, HG_CHUNK), HG_CHUNK)
            zq, zg = zq_ref[rows, :], zg_ref[rows, :]
            q, sq, sig, f, kk, b, bl = _hgrn_gates(zq, zf_ref[rows, :], lb, tril_f)
            v16 = zi_ref[rows, :].astype(BF16)
            eb, enb, ebl_b, ebl = jnp.exp(b), jnp.exp(-b), jnp.exp(bl - b), jnp.exp(bl)
            qd, ki, kd = q * eb, kk * enb, kk * ebl_b
            qd16, ki16, kd16 = qd.astype(BF16), ki.astype(BF16), kd.astype(BF16)
            o_all = opre_ref[rows, :]
            dout = dout_ref[rows, :]
            sg = _sigmoid(zg)
            d_on_all = dout * (zg * sg)
            dgate = dout * (sg * (1.0 + zg * (1.0 - sg)))
            dq_scale = eb * (sq * (1.0 + zq * (1.0 - sq)))
            for h in range(HG_HEADS):
                sl = _head(h)
                o = o_all[:, sl]
                r = lax.rsqrt(jnp.mean(o * o, axis=-1, keepdims=True) + EPS)
                ohat = o * r
                d_on = d_on_all[:, sl]
                dz_ref[rows, _head(h, 3)] = (dgate[:, sl] * (ohat * gn)).astype(BF16)
                dgn_acc[...] += jnp.sum(d_on * ohat, axis=0, keepdims=True)
                dohat = d_on * gn
                do16 = (r * (dohat - ohat * jnp.mean(dohat * ohat, axis=-1, keepdims=True))).astype(BF16)
                st = sall_ref[0, h, c]
                dst = dst_ref[h]
                st16, dst16 = st.astype(BF16), dst.astype(BF16)
                qd_h, ki_h, kd_h, v_h = qd16[:, sl], ki16[:, sl], kd16[:, sl], v16[:, sl]
                a16 = jnp.where(mask, lax.dot_general(qd_h, ki_h, _NT, preferred_element_type=F32), 0.0).astype(BF16)
                da16 = jnp.where(mask, lax.dot_general(do16, v_h, _NT, preferred_element_type=F32), 0.0).astype(BF16)
                dv = lax.dot_general(a16, do16, _TN, preferred_element_type=F32) + lax.dot_general(
                    kd_h, dst16, _NT, preferred_element_type=F32)
                dqd = jnp.dot(da16, ki_h, preferred_element_type=F32) + jnp.dot(do16, st16, preferred_element_type=F32)
                dki = lax.dot_general(da16, qd_h, _TN, preferred_element_type=F32)
                dkd = jnp.dot(v_h, dst16, preferred_element_type=F32)
                dbl_ref[:, sl] = jnp.sum(dkd * kd[:, sl], axis=0, keepdims=True) + ebl[:, sl] * jnp.sum(
                    st * dst, axis=0, keepdims=True)
                dst_ref[h] = dst * ebl[:, sl] + lax.dot_general(do16, qd_h, _TN, preferred_element_type=F32)
                dz_ref[rows, _head(h, 2)] = dv.astype(BF16)
                dz_ref[rows, sl] = (dqd * dq_scale[:, sl]).astype(BF16)
                dkk_ref[:, sl] = dki * enb[:, sl] + dkd * ebl_b[:, sl]
                db_ref[:, sl] = dqd * qd[:, sl] - dki * ki[:, sl] - dkd * kd[:, sl]
            dlogf = lax.dot_general(tril_f, db_ref[...], _TN, preferred_element_type=F32,
                                    precision=lax.Precision.HIGHEST) + dbl_ref[...]
            df = dlogf / f - dkk_ref[...]
            dz_ref[rows, HG_WIDTH:2 * HG_WIDTH] = (df * (1.0 - lb) * sig * (1.0 - sig)).astype(BF16)
            dlb_ref[...] += jnp.sum(df * (1.0 - sig), axis=0, keepdims=True)
            return carry

        lax.fori_loop(0, HG_CPB, chunk, 0, unroll=4)

        @pl.when(t_id == HG_NT - 1)
        def _():
            row0 = (lax.broadcasted_iota(jnp.int32, (3, HG_WIDTH), 0) == 0).astype(F32)
            dlbl_part = dlb_ref[...] * lb * (row0 - p)

            @pl.when(b_id == 0)
            def _():
                dlbl_ref[...] = dlbl_part

            @pl.when(b_id > 0)
            def _():
                dlbl_ref[...] += dlbl_part

            dgn_ref[...] = dgn_acc[...]

    rev = lambda b, t: b * HG_NT + HG_NT - 1 - t
    zspec = lambda s: pl.BlockSpec((HG_TB, HG_WIDTH), lambda b, t: (rev(b, t), s))
    return pl.pallas_call(
        body,
        name="hgrn_bwd",
        grid=(B_LOC, HG_NT),
        in_specs=[zspec(0), zspec(1), zspec(2), zspec(3), zspec(0), zspec(0), zspec(0),
                  pl.BlockSpec((1, HG_HEADS, HG_CPB, HG_DIM, HG_DIM), lambda b, t: (b, 0, HG_NT - 1 - t, 0, 0)),
                  pl.BlockSpec((3, HG_WIDTH), lambda b, t: (0, 0)), pl.BlockSpec((1, HG_DIM), lambda b, t: (0, 0))],
        out_specs=[pl.BlockSpec((HG_TB, 5 * HG_WIDTH), lambda b, t: (rev(b, t), 0)),
                   pl.BlockSpec((3, HG_WIDTH), lambda b, t: (0, 0)), pl.BlockSpec((1, HG_DIM), lambda b, t: (0, 0))],
        out_shape=[jax.ShapeDtypeStruct((N_TOK, 5 * HG_WIDTH), BF16),
                   jax.ShapeDtypeStruct((3, HG_WIDTH), F32), jax.ShapeDtypeStruct((1, HG_DIM), F32)],
        scratch_shapes=[pltpu.VMEM((HG_HEADS, HG_DIM, HG_DIM), F32), pltpu.VMEM((1, HG_WIDTH), F32),
                        pltpu.VMEM((1, HG_DIM), F32), pltpu.VMEM((HG_CHUNK, HG_WIDTH), F32),
                        pltpu.VMEM((HG_CHUNK, HG_WIDTH), F32), pltpu.VMEM((1, HG_WIDTH), F32)],
        compiler_params=_cp("arbitrary", "arbitrary"),
    )(z, z, z, z, opre, dcat, dq_mem, sall, lb_logits, gnorm)


GM_TM = 256


def _gmlp_norm(zv, ln_g, ln_b):
    gv, dgelu = _gelu_parts(zv)
    xc = gv - jnp.mean(gv, axis=-1, keepdims=True)
    rstd = lax.rsqrt(jnp.mean(xc * xc, axis=-1, keepdims=True) + EPS)
    vhat = xc * rstd
    return vhat * ln_g + ln_b, vhat, rstd, dgelu


def _gmlp_specs():
    half = lambda j: pl.BlockSpec((GM_TM, GM_WIDTH), lambda i: (i, j))
    vec = pl.BlockSpec((1, GM_WIDTH), lambda i: (0, 0))
    w = pl.BlockSpec((GM_GROUPS, GM_CHUNK, GM_CHUNK), lambda i: (0, 0, 0))
    bt = pl.BlockSpec((GM_CHUNK, GM_GROUPS), lambda i: (0, 0))
    return half, vec, w, bt


def _gmlp_fwd(z, o_mem, ln_g, ln_b, w_s, b_st):
    def body(zu_ref, zv_ref, omem_ref, g_ref, b_ref, w_ref, bt_ref, o_ref):
        o_ref[:, GM_WIDTH:] = omem_ref[...]
        u, _ = _gelu_parts(zu_ref[...])
        v, _, _, _ = _gmlp_norm(zv_ref[...], g_ref[...], b_ref[...])
        v16 = v.astype(BF16)
        mask = _tril(GM_CHUNK)
        bt = bt_ref[...]
        for g in range(GM_GROUPS):
            wm16 = jnp.where(mask, w_ref[g], 0.0).astype(BF16)
            cols = slice(g * GM_GDIM, (g + 1) * GM_GDIM)
            for c in range(GM_TM // GM_CHUNK):
                rows = slice(c * GM_CHUNK, (c + 1) * GM_CHUNK)
                mixed = jnp.dot(wm16, v16[rows, cols], preferred_element_type=F32) + bt[:, g:g + 1]
                o_ref[rows, cols] = (u[rows, cols] * mixed).astype(BF16)

    half, vec, w, bt = _gmlp_specs()
    return pl.pallas_call(
        body,
        name="gmlp_fwd",
        grid=(N_TOK // GM_TM,),
        in_specs=[half(0), half(1), pl.BlockSpec((GM_TM, XA_HEADS * XA_DIM), lambda i: (i, 0)), vec, vec, w, bt],
        out_specs=pl.BlockSpec((GM_TM, GM_WIDTH + XA_HEADS * XA_DIM), lambda i: (i, 0)),
        out_shape=jax.ShapeDtypeStruct((N_TOK, GM_WIDTH + XA_HEADS * XA_DIM), BF16),
        compiler_params=_cp("parallel"),
    )(z, z, o_mem, ln_g, ln_b, w_s, b_st)


def _gmlp_bwd(z, dcat, dq_mem, ln_g, ln_b, w_s, b_st):
    def body(zu_ref, zv_ref, dout_ref, dqm_ref, g_ref, b_ref, w_ref, bt_ref,
             dz_ref, dw_ref, dbt_ref, dg_ref, db_ref, dv_ref):
        dz_ref[:, 2 * GM_WIDTH:] = dqm_ref[...]
        @pl.when(pl.program_id(0) == 0)
        def _():
            dw_ref[...] = jnp.zeros_like(dw_ref)
            dbt_ref[...] = jnp.zeros_like(dbt_ref)
            dg_ref[...] = jnp.zeros_like(dg_ref)
            db_ref[...] = jnp.zeros_like(db_ref)

        zu = zu_ref[...]
        u, du_dz = _gelu_parts(zu)
        ln_g = g_ref[...]
        v, vhat, rstd, dgv_dz = _gmlp_norm(zv_ref[...], ln_g, b_ref[...])
        v16 = v.astype(BF16)
        dout = dout_ref[...]
        dmixed = dout * u
        dm16 = dmixed.astype(BF16)
        mask = _tril(GM_CHUNK)
        bt = bt_ref[...]
        group_id = lax.broadcasted_iota(jnp.int32, (1, GM_GROUPS), 1)
        dbt = jnp.zeros((GM_CHUNK, GM_GROUPS), F32)
        for g in range(GM_GROUPS):
            wm16 = jnp.where(mask, w_ref[g], 0.0).astype(BF16)
            cols = slice(g * GM_GDIM, (g + 1) * GM_GDIM)
            dw = jnp.zeros((GM_CHUNK, GM_CHUNK), F32)
            dbt_g = jnp.zeros((GM_CHUNK, 1), F32)
            for c in range(GM_TM // GM_CHUNK):
                rows = slice(c * GM_CHUNK, (c + 1) * GM_CHUNK)
                mixed = jnp.dot(wm16, v16[rows, cols], preferred_element_type=F32) + bt[:, g:g + 1]
                dz_ref[rows, cols] = (dout[rows, cols] * mixed * du_dz[rows, cols]).astype(BF16)
                dw += lax.dot_general(dm16[rows, cols], v16[rows, cols], _NT, preferred_element_type=F32)
                dbt_g += jnp.sum(dmixed[rows, cols], axis=-1, keepdims=True)
                dv_ref[rows, cols] = lax.dot_general(wm16, dm16[rows, cols], _TN, preferred_element_type=F32)
            dw_ref[g] += jnp.where(mask, dw, 0.0)
            dbt = dbt + dbt_g * (group_id == g).astype(F32)
        dbt_ref[...] += dbt
        dv = dv_ref[...]
        dg_ref[...] += jnp.sum(dv * vhat, axis=0, keepdims=True)
        db_ref[...] += jnp.sum(dv, axis=0, keepdims=True)
        dvh = dv * ln_g
        dgv = rstd * (dvh - jnp.mean(dvh, axis=-1, keepdims=True) - vhat * jnp.mean(dvh * vhat, axis=-1, keepdims=True))
        dz_ref[:, GM_WIDTH:2 * GM_WIDTH] = (dgv * dgv_dz).astype(BF16)

    half, vec, w, bt = _gmlp_specs()
    dz_width = 2 * GM_WIDTH + XA_HEADS * XA_DIM
    return pl.pallas_call(
        body,
        name="gmlp_bwd",
        grid=(N_TOK // GM_TM,),
        in_specs=[half(0), half(1), half(0), pl.BlockSpec((GM_TM, XA_HEADS * XA_DIM), lambda i: (i, 0)), vec, vec, w, bt],
        out_specs=[pl.BlockSpec((GM_TM, dz_width), lambda i: (i, 0)), w, bt, vec, vec],
        out_shape=[jax.ShapeDtypeStruct((N_TOK, dz_width), BF16),
                   jax.ShapeDtypeStruct((GM_GROUPS, GM_CHUNK, GM_CHUNK), F32),
                   jax.ShapeDtypeStruct((GM_CHUNK, GM_GROUPS), F32),
                   jax.ShapeDtypeStruct((1, GM_WIDTH), F32), jax.ShapeDtypeStruct((1, GM_WIDTH), F32)],
        scratch_shapes=[pltpu.VMEM((GM_TM, GM_WIDTH), F32)],
        compiler_params=_cp("arbitrary"),
    )(z, z, dcat, dq_mem, ln_g, ln_b, w_s, b_st)


def _own_slot(shape):
    return pl.BlockSpec((None,) + tuple(shape), lambda i, me_ref: (me_ref[0],) + (0,) * len(shape))


def _place_rows(w, layer, cuts_columns, me, *, name, deps=()):
    _, r, c = w.shape
    n = c if cuts_columns else r

    def body(me_ref, w_ref, *rest):
        o_ref = rest[len(deps)]
        wv = w_ref[...]
        o_ref[...] = (wv.T if cuts_columns else wv).astype(BF16)

    return pl.pallas_call(
        body,
        name=name,
        grid_spec=pltpu.PrefetchScalarGridSpec(
            num_scalar_prefetch=1, grid=(1,),
            in_specs=[pl.BlockSpec((None, r, c), lambda i, me_ref: (layer, 0, 0))] + [ANY_SPEC] * len(deps),
            out_specs=_own_slot((n, D_MODEL))),
        out_shape=jax.ShapeDtypeStruct((N_DEV, n, D_MODEL), BF16),
        compiler_params=_cp("arbitrary"),
    )(me, w, *deps)


def _place_ln(ln_g, ln_b, me):
    blk = ln_g.shape[1]

    def body(me_ref, g_ref, b_ref, o_ref):
        o_ref[...] = jnp.zeros_like(o_ref)
        o_ref[0:1, :] = g_ref[...]
        o_ref[1:2, :] = b_ref[...]

    vec = pl.BlockSpec((1, blk), lambda i, me_ref: (0, 0))
    return pl.pallas_call(
        body,
        name="place_ln",
        grid_spec=pltpu.PrefetchScalarGridSpec(
            num_scalar_prefetch=1, grid=(1,), in_specs=[vec, vec], out_specs=_own_slot((8, blk))),
        out_shape=jax.ShapeDtypeStruct((N_DEV, 8, blk), F32),
        compiler_params=_cp("arbitrary"),
    )(me, ln_g, ln_b)


def _place_slab(a, me, *, name):
    def body(me_ref, a_ref, o_ref):
        o_ref[...] = a_ref[...]

    return pl.pallas_call(
        body,
        name=name,
        grid_spec=pltpu.PrefetchScalarGridSpec(
            num_scalar_prefetch=1, grid=(1,),
            in_specs=[pl.BlockSpec(a.shape, lambda i, me_ref: (0, 0))], out_specs=_own_slot(a.shape)),
        out_shape=jax.ShapeDtypeStruct((N_DEV,) + a.shape, a.dtype),
        compiler_params=_cp("arbitrary"),
    )(me, a)


def _place_own(grads, me, *, name):
    k = len(grads)

    def body(me_ref, *refs):
        for src, dst in zip(refs[:k], refs[k:]):
            dst[...] = src[...]

    specs = [_own_slot(g.shape[1:]) for g in grads]
    return pl.pallas_call(
        body,
        name=name,
        grid_spec=pltpu.PrefetchScalarGridSpec(num_scalar_prefetch=1, grid=(1,), in_specs=specs, out_specs=specs),
        out_shape=[jax.ShapeDtypeStruct(g.shape, g.dtype) for g in grads],
        compiler_params=_cp("arbitrary"),
    )(me, *grads)


def _mesh_pos():
    x, y, c = (lax.axis_index(a) for a in MESH_AXES)
    return x, y, c, 4 * x + 2 * y + c


def _peer(x, y, c, r):
    px = 1 - x if r & 4 else x
    py = 1 - y if r & 2 else y
    pc = 1 - c if r & 1 else c
    return (px, py, pc), 4 * px + 2 * py + pc


RELATIONS = {"scatter": (1, 2, 3, 4, 5, 6, 7), "gather_all": (1, 2, 3, 4, 5, 6, 7), "gather_chips": (1, 2, 4, 6),
             "gather_sibling": (2, 4, 6)}


def _peer_copies(srcs, lands, send_sems, recv_sems, mode, waits):
    x, y, c, me = _mesh_pos()
    rel = RELATIONS[mode]
    pairs = []
    for ri, r in enumerate(rel):
        if mode == "gather_sibling":
            peer, _ = _peer(x, y, c, 1)
            _, sent_blk = _peer(x, y, c, r)
            _, got_blk = _peer(x, y, c, r ^ 1)
        else:
            peer, peer_blk = _peer(x, y, c, r)
            sent_blk, got_blk = (peer_blk if mode == "scatter" else me), peer_blk
        for k, (src, land) in enumerate(zip(srcs, lands)):
            idx = k * len(rel) + ri
            sems = dict(send_sem=send_sems.at[idx], recv_sem=recv_sems.at[idx], device_id=peer,
                        device_id_type=pl.DeviceIdType.MESH)
            dst_blk = sent_blk if mode == "gather_sibling" else me
            mine = pltpu.make_async_remote_copy(src_ref=src.at[sent_blk], dst_ref=land.at[dst_blk], **sems)
            theirs = pltpu.make_async_remote_copy(src_ref=src.at[sent_blk], dst_ref=land.at[got_blk], **sems) if waits else None
            pairs.append((mine, theirs))
    return pairs


DATAFLOW = pltpu.SideEffectType.DATAFLOW_SIDE_EFFECTING


def _in_hbm(a):
    return pltpu.with_memory_space_constraint(a, pltpu.HBM)


def _copies_start(srcs, lands, *, mode, name, deps=()):
    gather = mode != "scatter"
    arrs = list(lands) if gather else list(srcs) + list(lands)
    n, k, nd = len(arrs), len(lands), len(deps)

    def body(*refs):
        ins, send_sems, recv_sems, token = refs[:n], refs[n + nd], refs[n + nd + 1], refs[2 * n + nd + 2]
        src_refs, land_refs = (ins, ins) if gather else (ins[:k], ins[k:])
        for mine, _ in _peer_copies(src_refs, land_refs, send_sems, recv_sems, mode, waits=False):
            mine.start()
        token[...] = jnp.zeros_like(token)

    n_cp = k * len(RELATIONS[mode])
    return pl.pallas_call(
        body,
        name=name,
        in_specs=[HBM_SPEC] * n + [ANY_SPEC] * nd,
        out_specs=(SEM_SPEC, SEM_SPEC, *[HBM_SPEC] * n, pl.BlockSpec(memory_space=pltpu.VMEM)),
        out_shape=(pltpu.SemaphoreType.DMA((n_cp,)), pltpu.SemaphoreType.DMA((n_cp,)),
                   *[pltpu.HBM(a.shape, a.dtype) for a in arrs], jax.ShapeDtypeStruct((8, 128), F32)),
        input_output_aliases={i: 2 + i for i in range(n)},
        compiler_params=pltpu.CompilerParams(has_side_effects=DATAFLOW),
    )(*[_in_hbm(a) for a in arrs], *deps)


def _copies_wait(arrs, send_sems, recv_sems, after, *, n_lands, mode, name):
    n, k = len(arrs), n_lands
    gather = mode != "scatter"

    def body(*refs):
        ins, send_sems, recv_sems = refs[:n], refs[n], refs[n + 1]
        src_refs, land_refs = (ins, ins) if gather else (ins[:k], ins[k:])
        for mine, theirs in _peer_copies(src_refs, land_refs, send_sems, recv_sems, mode, waits=True):
            mine.wait_send()
            theirs.wait_recv()

    outs = pl.pallas_call(
        body,
        name=name,
        in_specs=[HBM_SPEC] * n + [SEM_SPEC, SEM_SPEC] + [ANY_SPEC] * len(after),
        out_specs=[HBM_SPEC] * n,
        out_shape=[pltpu.HBM(a.shape, a.dtype) for a in arrs],
        input_output_aliases={i: i for i in range(n)},
        compiler_params=pltpu.CompilerParams(has_side_effects=DATAFLOW),
    )(*arrs, send_sems, recv_sems, *after)
    return outs[n - k:]


def _adamw(w, g, m, v):
    m = ADAM_B1 * m + (1.0 - ADAM_B1) * g
    v = ADAM_B2 * v + (1.0 - ADAM_B2) * (g * g)
    m_hat = m / (1.0 - ADAM_B1 ** ADAM_STEP)
    v_hat = v / (1.0 - ADAM_B2 ** ADAM_STEP)
    return -ADAM_LR * (m_hat / (jnp.sqrt(v_hat) + ADAM_EPS) + ADAM_WD * w), m, v


ADAM_TC = 512


def _adam_big(slots, w, m, v, cuts_columns, *, name):
    layers, n, nj = len(slots), slots[0].shape[1], D_MODEL // ADAM_TC

    def body(*refs):
        s_refs = refs[:layers]
        w_ref, m_ref, v_ref, g_ref, d_ref, nm_ref, nv_ref, acc_ref = refs[layers:]
        for ll in range(layers):
            @pl.when(pl.program_id(0) == ll)
            def _(s_ref=s_refs[ll]):
                g = s_ref[0].astype(F32)
                for s in range(1, N_DEV):
                    g = g + s_ref[s].astype(F32)
                acc_ref[...] = g

        g = acc_ref[...].T if cuts_columns else acc_ref[...]
        g_ref[...] = g
        d_ref[...], nm_ref[...], nv_ref[...] = _adamw(w_ref[...], g, m_ref[...], v_ref[...])

    def slot_spec(ll):
        return pl.BlockSpec((N_DEV, n, ADAM_TC),
                            lambda l, j: (0, 0, jnp.where(l < ll, 0, jnp.where(l > ll, nj - 1, j))))

    if cuts_columns:
        w_spec = pl.BlockSpec((None, ADAM_TC, n), lambda l, j: (l, j, 0))
    else:
        w_spec = pl.BlockSpec((None, n, ADAM_TC), lambda l, j: (l, 0, j))
    return pl.pallas_call(
        body,
        name=name,
        grid=(layers, nj),
        in_specs=[slot_spec(ll) for ll in range(layers)] + [w_spec] * 3,
        out_specs=[w_spec] * 4,
        out_shape=[jax.ShapeDtypeStruct(w.shape, F32)] * 4,
        scratch_shapes=[pltpu.VMEM((n, ADAM_TC), F32)],
        compiler_params=_cp("arbitrary", "arbitrary"),
    )(*slots, w, m, v)


def _adam_slabs(slots, ws, ms, vs):
    n = len(slots)

    def body(*refs):
        ins, outs = refs[:4 * n], refs[4 * n:]
        for k in range(n):
            s_ref, w_ref, m_ref, v_ref = ins[k], ins[n + k], ins[2 * n + k], ins[3 * n + k]
            g = s_ref[0]
            for s in range(1, N_DEV):
                g = g + s_ref[s]
            outs[4 * k][...] = g
            outs[4 * k + 1][...], outs[4 * k + 2][...], outs[4 * k + 3][...] = _adamw(w_ref[...], g, m_ref[...], v_ref[...])

    res = pl.pallas_call(
        body,
        name="small_adamw",
        out_shape=[jax.ShapeDtypeStruct(w.shape, F32) for w in ws for _ in range(4)],
        compiler_params=pltpu.CompilerParams(vmem_limit_bytes=VMEM_LIMIT_BYTES),
    )(*slots, *ws, *ms, *vs)
    return [res[4 * k:4 * k + 4] for k in range(n)]


def _adam_vecs(gs, ws, ms, vs):
    n = len(gs)

    def body(*refs):
        ins, outs = refs[:4 * n], refs[4 * n:]
        for k in range(n):
            outs[3 * k][...], outs[3 * k + 1][...], outs[3 * k + 2][...] = _adamw(
                ins[n + k][...], ins[k][...], ins[2 * n + k][...], ins[3 * n + k][...])

    res = pl.pallas_call(
        body,
        name="ln_adamw",
        out_shape=[jax.ShapeDtypeStruct(w.shape, F32) for w in ws for _ in range(3)],
        compiler_params=pltpu.CompilerParams(vmem_limit_bytes=VMEM_LIMIT_BYTES),
    )(*gs, *ws, *ms, *vs)
    return [res[3 * k:3 * k + 3] for k in range(n)]


SLAB_AT = dict(mem_norm=0, lb_logits=1, ffn1_norm=4, mix_norm=6, hgrn_gnorm=8, gmlp_ln_g=9, gmlp_ln_b=11,
               gmlp_b_s=13, ffn2_norm=14, final_norm=16)
SLAB_ROWS = 24
LOSS_ROW = 17
SMALL_SHARDED = ("gmlp_ln_g", "gmlp_ln_b")


def _pack_slab(parts, *, name, deps=()):
    flat, plan = [], []
    for pname, at in SLAB_AT.items():
        for a in parts.get(pname, ()):
            flat.append(a)
            plan.append((at, a.shape))
            at += max(1, a.shape[0] * a.shape[1] // D_MODEL)
    for a in parts.get("loss", ()):
        flat.append(a)
        plan.append((LOSS_ROW, a.shape))

    def body(*refs):
        o_ref = refs[-1]
        o_ref[...] = jnp.zeros_like(o_ref)
        for ref, (at, (r, w)) in zip(refs, plan):
            if w == D_MODEL or r == 1 and w < D_MODEL:
                o_ref[at:at + r, 0:w] = ref[...]
            elif w < D_MODEL:
                for j in range(r):
                    o_ref[at:at + 1, j * w:(j + 1) * w] = ref[j:j + 1, :]
            else:
                for j in range(w // D_MODEL):
                    o_ref[at + j:at + j + 1, :] = ref[:, j * D_MODEL:(j + 1) * D_MODEL]

    return pl.pallas_call(
        body,
        name=name,
        in_specs=[pl.BlockSpec(memory_space=pltpu.VMEM)] * len(flat) + [ANY_SPEC] * len(deps),
        out_shape=jax.ShapeDtypeStruct((SLAB_ROWS, D_MODEL), F32),
        compiler_params=pltpu.CompilerParams(vmem_limit_bytes=VMEM_LIMIT_BYTES),
    )(*flat, *deps)


def _unpack_slab(slab, shapes):
    out = {}
    for pname, at in SLAB_AT.items():
        if pname in SMALL_SHARDED:
            continue
        size = math.prod(shapes[pname])
        rows = max(1, size // D_MODEL)
        out[pname] = slab[at:at + rows].reshape(-1)[:size].reshape(shapes[pname])
    return out


def _ffn_fwd(x, norm_g, block, layer, full, get_weights):
    tag = f"l{layer}_{block}"
    full.update(get_weights((layer, f"{block}_in"), (x,)))
    h, z, act = _norm_mm(x, norm_g, full[(f"{block}_w_in", layer)], swiglu=True, tm=512, tn=1408, deps=full.pop("deps", ()),
                         name=f"{tag}_in")
    full.update(get_weights((layer, f"{block}_out"), (act,)))
    y = _mm(act, full[(f"{block}_w_out", layer)], tm=512, tn=D_MODEL, tk=D_FF, out_dtype=F32, res=x, scale=0.5,
            deps=full.pop("deps", ()), name=f"{tag}_out")
    return y, (x, h, z, act)


def _ffn_bwd(dy, dy16, saved, norm_g, w_in_t, w_out, tag, deps=(), after_out_wgrad=None, before_in_wgrad=None):
    x, h, z, act = saved
    dw_out = _mm(act, dy16, ta=True, tm=1408, tn=D_MODEL, tk=N_TOK, out_dtype=BF16, scale=0.5, deps=deps,
                 name=f"{tag}_out_wgrad")
    sent = after_out_wgrad(dw_out) if after_out_wgrad is not None else ()
    dz, dx, dx16, dg = _ffn_dgrad(dy16, dy, w_out, z, w_in_t, x, norm_g, scale=0.5, deps=sent, name=f"{tag}_dgrad")
    wdeps = before_in_wgrad(dg) if before_in_wgrad is not None else ()
    dw_in_t = _planes_wgrad(dz, h, deps=wdeps, name=f"{tag}_in_wgrad")
    return dx, dx16, dg, dw_in_t, dw_out


def kernel(x, mem, mem_norm, lb_logits, ffn1_norm, ffn1_w_in, ffn1_w_out, mix_norm, mem_w_kv, hgrn_w_in, hgrn_gnorm, hgrn_w_out, gmlp_w_in, gmlp_ln_g, gmlp_ln_b, gmlp_w_s, gmlp_b_s, gmlp_w_out, ffn2_norm, ffn2_w_in, ffn2_w_out, final_norm, loss_target, m_mem_norm, m_lb_logits, m_ffn1_norm, m_ffn1_w_in, m_ffn1_w_out, m_mix_norm, m_mem_w_kv, m_hgrn_w_in, m_hgrn_gnorm, m_hgrn_w_out, m_gmlp_w_in, m_gmlp_ln_g, m_gmlp_ln_b, m_gmlp_w_s, m_gmlp_b_s, m_gmlp_w_out, m_ffn2_norm, m_ffn2_w_in, m_ffn2_w_out, m_final_norm, v_mem_norm, v_lb_logits, v_ffn1_norm, v_ffn1_w_in, v_ffn1_w_out, v_mix_norm, v_mem_w_kv, v_hgrn_w_in, v_hgrn_gnorm, v_hgrn_w_out, v_gmlp_w_in, v_gmlp_ln_g, v_gmlp_ln_b, v_gmlp_w_s, v_gmlp_b_s, v_gmlp_w_out, v_ffn2_norm, v_ffn2_w_in, v_ffn2_w_out, v_final_norm):
    weights = dict(mem_norm=mem_norm, lb_logits=lb_logits, ffn1_norm=ffn1_norm, ffn1_w_in=ffn1_w_in, ffn1_w_out=ffn1_w_out, mix_norm=mix_norm, mem_w_kv=mem_w_kv, hgrn_w_in=hgrn_w_in, hgrn_gnorm=hgrn_gnorm, hgrn_w_out=hgrn_w_out, gmlp_w_in=gmlp_w_in, gmlp_ln_g=gmlp_ln_g, gmlp_ln_b=gmlp_ln_b, gmlp_w_s=gmlp_w_s, gmlp_b_s=gmlp_b_s, gmlp_w_out=gmlp_w_out, ffn2_norm=ffn2_norm, ffn2_w_in=ffn2_w_in, ffn2_w_out=ffn2_w_out, final_norm=final_norm)
    mom_m = dict(mem_norm=m_mem_norm, lb_logits=m_lb_logits, ffn1_norm=m_ffn1_norm, ffn1_w_in=m_ffn1_w_in, ffn1_w_out=m_ffn1_w_out, mix_norm=m_mix_norm, mem_w_kv=m_mem_w_kv, hgrn_w_in=m_hgrn_w_in, hgrn_gnorm=m_hgrn_gnorm, hgrn_w_out=m_hgrn_w_out, gmlp_w_in=m_gmlp_w_in, gmlp_ln_g=m_gmlp_ln_g, gmlp_ln_b=m_gmlp_ln_b, gmlp_w_s=m_gmlp_w_s, gmlp_b_s=m_gmlp_b_s, gmlp_w_out=m_gmlp_w_out, ffn2_norm=m_ffn2_norm, ffn2_w_in=m_ffn2_w_in, ffn2_w_out=m_ffn2_w_out, final_norm=m_final_norm)
    mom_v = dict(mem_norm=v_mem_norm, lb_logits=v_lb_logits, ffn1_norm=v_ffn1_norm, ffn1_w_in=v_ffn1_w_in, ffn1_w_out=v_ffn1_w_out, mix_norm=v_mix_norm, mem_w_kv=v_mem_w_kv, hgrn_w_in=v_hgrn_w_in, hgrn_gnorm=v_hgrn_gnorm, hgrn_w_out=v_hgrn_w_out, gmlp_w_in=v_gmlp_w_in, gmlp_ln_g=v_gmlp_ln_g, gmlp_ln_b=v_gmlp_ln_b, gmlp_w_s=v_gmlp_w_s, gmlp_b_s=v_gmlp_b_s, gmlp_w_out=v_gmlp_w_out, ffn2_norm=v_ffn2_norm, ffn2_w_in=v_ffn2_w_in, ffn2_w_out=v_ffn2_w_out, final_norm=v_final_norm)
    order = list(weights)
    _, _, _, me = _mesh_pos()
    me_arr = jnp.reshape(me, (1,)).astype(jnp.int32)
    cuts = {name: c for name, c, _, _ in GROUPS}
    rows_already = tuple(name for name, c, _, n in GROUPS if c and n % 128)
    as_rows = lambda a: jnp.transpose(a, (0, 2, 1))
    for name in rows_already:
        weights[name], mom_m[name], mom_v[name] = as_rows(weights[name]), as_rows(mom_m[name]), as_rows(mom_v[name])
        cuts[name] = False

    mix1 =(("mem_w_kv", 1), ("gmlp_w_in", 0), ("gmlp_w_out", 0))
    gather_plan = (
        ((0, "ffn1_in"), _stage_pieces(0, "ffn1")),
        ((0, "mix_in"), _stage_pieces(0, "mix")),
        ((0, "ffn2_in"), _stage_pieces(0, "ffn2")),
        ((1, "ffn1_in"), _stage_pieces(1, "ffn1")),
        ((1, "mix_in"), mix1),
        ((1, "ffn2_in"), _stage_pieces(1, "ffn2")),
    )
    stage_of = {use: k for k, (use, _) in enumerate(gather_plan)}
    in_flight = {}

    def place(k, deps=()):
        pieces = gather_plan[k][1]
        lands = [_place_rows(weights[name], l, cuts[name], me_arr, deps=deps, name=f"place_{name}_{l}")
                 for name, l in pieces]
        if pieces is mix1:
            lands.append(_place_ln(gmlp_ln_g, gmlp_ln_b, me_arr))
        return lands

    placed = {0: place(0)}

    def start_chips(k, deps):
        lands = placed[k]
        send_sems, recv_sems, *thru, token = _copies_start(lands, lands, mode="gather_chips", deps=deps,
                                                           name=f"gather{k}_chips_start")
        in_flight[k] = (thru, send_sems, recv_sems)
        return token

    def pass_to_sibling(k, after):
        thru, send_sems, recv_sems = in_flight[k]
        outs = _copies_wait(thru, send_sems, recv_sems, after, n_lands=len(thru), mode="gather_chips",
                            name=f"gather{k}_chips_wait")
        send_sems, recv_sems, *thru, token = _copies_start(outs, outs, mode="gather_sibling",
                                                           name=f"gather{k}_sibling_start")
        in_flight[k] = (thru, send_sems, recv_sems)
        return token, token

    first_sent = start_chips(0, ())
    placed.update({k: place(k, (first_sent,)) for k in range(1, len(gather_plan))})
    placed_later = tuple(a for k in range(1, len(gather_plan)) for a in placed[k])
    points = [(i, p) for i in (0, 1) for p in ("ffn1_in", "ffn1_out", "mix_in", "mix_out", "ffn2_in", "ffn2_out")]
    pass_at = {j: points[points.index(use) - 1] for j, (use, _) in enumerate(gather_plan) if j}
    pass_at[1] = gather_plan[1][0]

    started = {0}

    def get_weights(use, after):
        tokens, w = [], {}
        k = stage_of.get(use)

        def pass_on(j, after):
            token, landed = pass_to_sibling(j, after)
            tokens.append(token)
            if j + 1 < len(gather_plan) and j + 1 not in started:
                started.add(j + 1)
                tokens.append(start_chips(j + 1, (landed,)))

        if k == 0:
            pass_on(0, tuple(after) + placed_later)
        elif k is not None and pass_at[k] == use:
            pass_on(k, after)
        if k is not None:
            thru, send_sems, recv_sems = in_flight[k]
            outs = _copies_wait(thru, send_sems, recv_sems, after, n_lands=len(thru), mode="gather_sibling",
                                name=f"gather{k}_sibling_wait")
            after = (outs[0],)
            pieces = gather_plan[k][1]
            w = {p: o.reshape(N_DEV * o.shape[1], D_MODEL) for p, o in zip(pieces, outs)}
            if pieces is mix1:
                w["ln_g"] = outs[-1][:, 0, :].reshape(1, GM_WIDTH)
                w["ln_b"] = outs[-1][:, 1, :].reshape(1, GM_WIDTH)
        for j, at in pass_at.items():
            if at == use and j != k:
                pass_on(j, after)
        w["deps"] = tuple(tokens)
        return w

    scatter = {}

    def put_grads(st, grads):
        if st in ("w_s", "small"):
            slab = grads.reshape(GM_GROUPS * GM_CHUNK, GM_CHUNK) if st == "w_s" else _pack_slab(grads, name="pack_small_grads")
            land = _place_slab(slab, me_arr, name=f"{st}_place")
            send_sems, recv_sems, *thru, token = _copies_start([land], [land], mode="gather_all", name=f"{st}_start")
            scatter[st] = (thru, send_sems, recv_sems)
            return (token,)
        views = [g.reshape(N_DEV, -1, D_MODEL) for g in grads.values()]
        recv = _place_own(views, me_arr, name=f"scatter_place_l{st[0]}_{st[1]}")
        send_sems, recv_sems, *thru, token = _copies_start(views, recv, mode="scatter",
                                                           name=f"scatter_start_l{st[0]}_{st[1]}")
        scatter[st] = (tuple(grads), thru, send_sems, recv_sems)
        return (token,)

    dx, last_sent = _step_local(
        x, mem, loss_target, get_weights, put_grads, mem_norm, lb_logits, ffn1_norm, mix_norm, hgrn_gnorm,
        gmlp_w_s, gmlp_b_s, ffn2_norm, final_norm)

    slots = {}

    def wait_grads(blk, after, last=False):
        for st, entry in scatter.items():
            if isinstance(st, tuple) and st[1].startswith(blk) and (st == (0, "ffn1_in")) == last:
                pieces, thru, send_sems, recv_sems = entry
                outs = _copies_wait(thru, send_sems, recv_sems, after, n_lands=len(thru) // 2, mode="scatter",
                                    name=f"scatter_wait_l{st[0]}_{st[1]}")
                slots.update(zip(pieces, outs))

    grad, delta, new_m, new_v = {}, {}, {}, {}

    def adam_groups(names):
        for name in names:
            layers = GROUP_LAYERS[name]
            grad[name], delta[name], new_m[name], new_v[name] = _adam_big(
                [slots[(name, l)] for l in range(layers)], weights[name], mom_m[name], mom_v[name], cuts[name],
                name=f"{name}_adamw")

    wait_grads("ffn2", (dx, *last_sent))
    adam_groups(("ffn2_w_in", "ffn2_w_out"))
    wait_grads("mix", (delta["ffn2_w_out"],))
    adam_groups(("mem_w_kv", "gmlp_w_in", "gmlp_w_out", "hgrn_w_in", "hgrn_w_out"))
    wait_grads("ffn1", (delta["hgrn_w_out"],))
    adam_groups(("ffn1_w_out",))

    def small_parts(src):
        parts = {n: [src[n].reshape(-1, src[n].shape[-1])] for n in SLAB_AT if n not in SMALL_SHARDED}
        return parts

    w_s_rows = lambda a: a.reshape(GM_GROUPS * GM_CHUNK, GM_CHUNK)
    small_done = (delta["hgrn_w_out"],)
    (slab_slots,) = _copies_wait(*scatter["small"], small_done, n_lands=1, mode="gather_all", name="small_wait")
    (ws_slots,) = _copies_wait(*scatter["w_s"], small_done, n_lands=1, mode="gather_all", name="w_s_wait")
    (g_slab, d_slab, nm_slab, nv_slab), (g_ws, d_ws, nm_ws, nv_ws) = _adam_slabs(
        [slab_slots, ws_slots],
        [_pack_slab(small_parts(weights), deps=(dx,), name="pack_small_w"), w_s_rows(gmlp_w_s)],
        [_pack_slab(small_parts(mom_m), deps=(dx,), name="pack_small_m"), w_s_rows(m_gmlp_w_s)],
        [_pack_slab(small_parts(mom_v), deps=(dx,), name="pack_small_v"), w_s_rows(v_gmlp_w_s)])
    shapes = {n: weights[n].shape for n in SLAB_AT}
    for out, slab, ws in ((grad, g_slab, g_ws), (delta, d_slab, d_ws), (new_m, nm_slab, nm_ws), (new_v, nv_slab, nv_ws)):
        out.update(_unpack_slab(slab, shapes))
        out["gmlp_w_s"] = ws.reshape(gmlp_w_s.shape)
    blk = GM_WIDTH // N_DEV
    g_ln = [lax.dynamic_slice(g_slab[SLAB_AT[n]:SLAB_AT[n] + 2].reshape(1, GM_WIDTH), (0, me * blk), (1, blk))
            for n in SMALL_SHARDED]
    ln_out = _adam_vecs(g_ln, [weights[n] for n in SMALL_SHARDED], [mom_m[n] for n in SMALL_SHARDED],
                        [mom_v[n] for n in SMALL_SHARDED])
    for n, g, (d, nm, nv) in zip(SMALL_SHARDED, g_ln, ln_out):
        grad[n], delta[n], new_m[n], new_v[n] = g, d, nm, nv

    wait_grads("ffn1", tuple(delta[n] for n in delta if n in GROUP_LAYERS) + (d_slab,), last=True)
    adam_groups(("ffn1_w_in",))

    for name in rows_already:
        for out in (grad, delta, new_m, new_v):
            out[name] = as_rows(out[name])
    loss = g_slab[LOSS_ROW, 0]
    grad_x = dx.reshape(B_LOC, SEQ, D_MODEL)
    return (loss, grad_x, *[grad[n] for n in order], *[delta[n] for n in order],
            *[new_m[n] for n in order], *[new_v[n] for n in order])


def _step_local(x, mem, loss_target, get_weights, put_grads, mem_norm, lb_logits, ffn1_norm, mix_norm, hgrn_gnorm,
                gmlp_w_s, gmlp_b_s, ffn2_norm, final_norm):
    w_s = gmlp_w_s[0]
    b_st = gmlp_b_s[0].T

    xs = x.reshape(N_TOK, D_MODEL)
    mem2d = mem.reshape(B_LOC * MEM_LEN, D_MODEL)
    mem_g = mem_norm.reshape(1, D_MODEL)
    saved, full = [], {}
    for i in range(2):
        xs, s_ffn1 = _ffn_fwd(xs, ffn1_norm[i:i + 1], "ffn1", i, full, get_weights)
        if i == 0:
            memn = _rms_fwd(mem2d, mem_g, deps=(xs,), name="mem_norm_fwd")
        full.update(get_weights((i, "mix_in"), (xs,)))
        mixer = "hgrn" if i == 0 else "gmlp"
        hm, zm = _norm_mm(xs, mix_norm[i:i + 1], full[(f"{mixer}_w_in", 0)], swiglu=False, tm=1024, tn=1280, deps=full.pop("deps", ()),
                          name=f"l{i}_mix_in")
        kv = _mm(memn, full[("mem_w_kv", i)], tb=True, tm=512, tn=512, tk=D_MODEL, out_dtype=F32, name=f"l{i}_mem_kv")
        o_mem = _attn_fwd(zm, kv, name=f"l{i}_attn")
        if i == 0:
            cat, o_pre, s_all = _hgrn_fwd(zm, o_mem, lb_logits, hgrn_gnorm)
            mix_saved = (o_pre, s_all)
        else:
            cat = _gmlp_fwd(zm, o_mem, full["ln_g"], full["ln_b"], w_s, b_st)
            mix_saved = ()
        x_mix = xs
        full.update(get_weights((i, "mix_out"), (cat,)))
        xs = _mm(cat, full[(f"{mixer}_w_out", 0)], tm=512, tn=D_MODEL, tk=cat.shape[1], out_dtype=F32, res=xs,
                 deps=full.pop("deps", ()), name=f"l{i}_mix_out")
        xs, s_ffn2 = _ffn_fwd(xs, ffn2_norm[i:i + 1], "ffn2", i, full, get_weights)
        saved.append((s_ffn1, (x_mix, hm, kv, zm, cat, mix_saved), s_ffn2))

    dx, dx16, d_final, loss_part = _loss_head(xs, final_norm.reshape(1, D_MODEL), loss_target.reshape(N_TOK, D_MODEL))

    small = {"final_norm": [d_final], "loss": [loss_part]}
    d_ffn1, d_ffn2, d_mix = [None, None], [None, None], [None, None]
    dmemn = jnp.zeros((B_LOC * MEM_LEN, D_MODEL), F32)
    deps = ()
    for i in (1, 0):
        s_ffn1, (x_mix, hm, kv, zm, cat, mix_saved), s_ffn2 = saved[i]
        dx, dx16, d_ffn2[i], dw_in_t, dw_out = _ffn_bwd(
            dx, dx16, s_ffn2, ffn2_norm[i:i + 1], full[("ffn2_w_in", i)], full[("ffn2_w_out", i)], f"l{i}_ffn2", deps)
        deps = put_grads((i, "ffn2"), {("ffn2_w_in", i): dw_in_t, ("ffn2_w_out", i): dw_out})
        mixer = "hgrn" if i == 0 else "gmlp"
        w_in_t, w_out = full[(f"{mixer}_w_in", 0)], full[(f"{mixer}_w_out", 0)]
        width = cat.shape[1]
        g_mix = {}
        g_mix[(f"{mixer}_w_out", 0)] = _mm(cat, dx16, ta=True, tm=1024, tn=D_MODEL, tk=N_TOK, out_dtype=BF16,
                                           deps=deps, name=f"l{i}_mix_out_wgrad")
        dcat = _mm(dx16, w_out, tb=True, tm=1024, tn=width // 2, tk=D_MODEL, out_dtype=F32, name=f"l{i}_mix_out_dgrad")
        dq, dk, dv = _attn_bwd(zm, kv, dcat, do_off=width - XA_HEADS * XA_DIM, name=f"l{i}_attn_bwd")
        if i == 0:
            dzm, dlbl, dgn = _hgrn_bwd(zm, mix_saved[0], dcat, dq, mix_saved[1], lb_logits, hgrn_gnorm)
            small["lb_logits"], small["hgrn_gnorm"] = [dlbl], [dgn]
            deps = ()
        else:
            dzm, dws, dbt, dlng, dlnb = _gmlp_bwd(zm, dcat, dq, full["ln_g"], full["ln_b"], w_s, b_st)
            small["gmlp_b_s"], small["gmlp_ln_g"], small["gmlp_ln_b"] = [dbt.T], [dlng], [dlnb]
            deps = put_grads("w_s", dws)
        g_mix[(f"{mixer}_w_in", 0)] = _mm(dzm, hm, ta=True, tm=1024, tn=D_MODEL, tk=N_TOK, out_dtype=BF16, deps=deps,
                                          name=f"l{i}_mix_in_wgrad")
        dkv = jnp.concatenate([dk, dv], axis=1)
        g_mix[("mem_w_kv", i)] = _mm(dkv, memn, ta=True, tm=512, tn=D_MODEL, tk=B_LOC * MEM_LEN, out_dtype=BF16,
                                     name=f"l{i}_mem_kv_wgrad")
        deps = put_grads((i, "mix"), g_mix)
        dx, dx16, d_mix[i] = _dgrad_norm_bwd(dzm, w_in_t, x_mix, mix_norm[i:i + 1], dx, deps=deps,
                                             name=f"l{i}_mix_in_dgrad")
        dmemn = _mm(dkv, full[("mem_w_kv", i)], tm=B_LOC * MEM_LEN, tn=D_MODEL, tk=512, out_dtype=F32, res=dmemn,
                    name=f"l{i}_mem_kv_dgrad")
        def send_small(dg, i=i, dmemn=dmemn):
            d_ffn1[i] = dg
            _, _, dmem_g = _rms_bwd(mem2d, mem_g, dmemn, dmemn, name="mem_norm_bwd")
            small.update(mem_norm=[dmem_g], ffn1_norm=d_ffn1, ffn2_norm=d_ffn2, mix_norm=d_mix)
            return put_grads("small", small)

        if i == 0:
            send_out = lambda dw_out: put_grads((0, "ffn1_out"), {("ffn1_w_out", 0): dw_out})
            dx, dx16, d_ffn1[i], dw_in_t, _ = _ffn_bwd(
                dx, dx16, s_ffn1, ffn1_norm[i:i + 1], full[("ffn1_w_in", i)], full[("ffn1_w_out", i)], f"l{i}_ffn1",
                after_out_wgrad=send_out, before_in_wgrad=send_small)
            deps = put_grads((0, "ffn1_in"), {("ffn1_w_in", 0): dw_in_t})
        else:
            dx, dx16, d_ffn1[i], dw_in_t, dw_out = _ffn_bwd(
                dx, dx16, s_ffn1, ffn1_norm[i:i + 1], full[("ffn1_w_in", i)], full[("ffn1_w_out", i)], f"l{i}_ffn1")
            deps = put_grads((i, "ffn1"), {("ffn1_w_in", i): dw_in_t, ("ffn1_w_out", i): dw_out})
    return dx, deps
```

```python
import functools
import math

import jax
import jax.numpy as jnp
from jax import lax
from jax.experimental import pallas as pl
from jax.experimental.pallas import tpu as pltpu

F32 = jnp.float32
BF16 = jnp.bfloat16

D_MODEL = 1024
SEQ = 2048
B_LOC = 2
N_TOK = B_LOC * SEQ
MEM_LEN = 256
N_DEV = 8
EPS = 1e-6
D_FF = 2816
HG_HEADS = 8
HG_DIM = 128
HG_CHUNK = 64
HG_NCHUNK = SEQ // HG_CHUNK
GM_CHUNK = 128
GM_GROUPS = 8
GM_WIDTH = 2048
GM_GDIM = GM_WIDTH // GM_GROUPS
XA_HEADS = 4
XA_DIM = 256
XA_OFF = 4096

ADAM_LR = 0.001
ADAM_B1 = 0.9
ADAM_B2 = 0.999
ADAM_EPS = 1e-08
ADAM_WD = 0.01
ADAM_STEP = 10

VMEM_LIMIT_BYTES = 56 * 1024 * 1024
MESH_AXES = ("x", "y", "c")

GROUPS = (
    ("ffn1_w_in", True, 2, 704),
    ("ffn1_w_out", False, 2, 352),
    ("mem_w_kv", True, 2, 256),
    ("hgrn_w_in", True, 1, 640),
    ("hgrn_w_out", False, 1, 256),
    ("gmlp_w_in", True, 1, 640),
    ("gmlp_w_out", False, 1, 384),
    ("ffn2_w_in", True, 2, 704),
    ("ffn2_w_out", False, 2, 352),
)
GROUP_LAYERS = {name: layers for name, _, layers, _ in GROUPS}


def _stage_pieces(layer, block):
    if block == "mix":
        mixer = "hgrn" if layer == 0 else "gmlp"
        return (("mem_w_kv", layer), (f"{mixer}_w_in", 0), (f"{mixer}_w_out", 0))
    return ((f"{block}_w_in", layer), (f"{block}_w_out", layer))


ANY_SPEC = pl.BlockSpec(memory_space=pl.ANY)
HBM_SPEC = pl.BlockSpec(memory_space=pltpu.HBM)
SEM_SPEC = pl.BlockSpec(memory_space=pltpu.SEMAPHORE)


def _cp(*sem):
    return pltpu.CompilerParams(dimension_semantics=sem, vmem_limit_bytes=VMEM_LIMIT_BYTES)


def _sigmoid(x):
    return 0.5 * jnp.tanh(0.5 * x) + 0.5


def _gelu_parts(x):
    cdf = 0.5 * (1.0 + lax.erf(x * (1.0 / math.sqrt(2.0))))
    pdf = jnp.exp(-0.5 * x * x) * (1.0 / math.sqrt(2.0 * math.pi))
    return x * cdf, cdf + x * pdf


def _mm(a, b, *, ta=False, tb=False, tm, tn, tk, out_dtype, res=None, scale=1.0, deps=(), name):
    m, k = (a.shape[1], a.shape[0]) if ta else a.shape
    n, kb = b.shape if tb else (b.shape[1], b.shape[0])
    assert k == kb and m % tm == 0 and n % tn == 0 and k % tk == 0, (name, a.shape, b.shape)
    nk = k // tk
    dn = (((0 if ta else 1,), (1 if tb else 0,)), ((), ()))
    n_in = 2 + (res is not None) + len(deps)

    def body(*refs):
        a_ref, b_ref = refs[:2]
        r_ref = refs[2] if res is not None else None
        o_ref, scr = refs[n_in], refs[n_in + 1:]
        p = lax.dot_general(a_ref[...].astype(BF16), b_ref[...].astype(BF16), dn, preferred_element_type=F32)

        def finish(acc):
            if scale != 1.0:
                acc = scale * acc
            if r_ref is not None:
                acc = r_ref[...] + acc
            o_ref[...] = acc.astype(out_dtype)

        if nk == 1:
            finish(p)
        else:
            acc_ref = scr[0]
            kk = pl.program_id(2)

            @pl.when(kk == 0)
            def _():
                acc_ref[...] = p

            @pl.when(kk > 0)
            def _():
                acc_ref[...] += p

            @pl.when(kk == nk - 1)
            def _():
                finish(acc_ref[...])

    a_spec = pl.BlockSpec((tk, tm), lambda i, j, kk: (kk, i)) if ta else pl.BlockSpec((tm, tk), lambda i, j, kk: (i, kk))
    b_mode = dict(pipeline_mode=pl.Buffered(1)) if n == tn and nk == 1 else {}
    if tb:
        b_spec = pl.BlockSpec((tn, tk), lambda i, j, kk: (j, kk), **b_mode)
    else:
        b_spec = pl.BlockSpec((tk, tn), lambda i, j, kk: (kk, j), **b_mode)
    o_spec = pl.BlockSpec((tm, tn), lambda i, j, kk: (i, j))
    in_specs = [a_spec, b_spec] + ([o_spec] if res is not None else []) + [ANY_SPEC] * len(deps)
    args = (a, b) + ((res,) if res is not None else ()) + tuple(deps)
    return pl.pallas_call(
        body,
        name=name,
        grid=(m // tm, n // tn, nk),
        in_specs=in_specs,
        out_specs=o_spec,
        out_shape=jax.ShapeDtypeStruct((m, n), out_dtype),
        scratch_shapes=[pltpu.VMEM((tm, tn), F32)] if nk > 1 else [],
        compiler_params=_cp("parallel", "parallel", "arbitrary"),
    )(*args)


def _rms_fwd(x, g, *, name, deps=(), tm=512):
    rows = x.shape[0]

    def body(x_ref, g_ref, *rest):
        o_ref = rest[len(deps)]
        xv = x_ref[...]
        r = lax.rsqrt(jnp.mean(xv * xv, axis=-1, keepdims=True) + EPS)
        o_ref[...] = (xv * r * g_ref[...]).astype(BF16)

    row = pl.BlockSpec((tm, D_MODEL), lambda i: (i, 0))
    return pl.pallas_call(
        body,
        name=name,
        grid=(rows // tm,),
        in_specs=[row, pl.BlockSpec((1, D_MODEL), lambda i: (0, 0))] + [ANY_SPEC] * len(deps),
        out_specs=row,
        out_shape=jax.ShapeDtypeStruct((rows, D_MODEL), BF16),
        compiler_params=_cp("parallel"),
    )(x, g, *deps)


def _rms_bwd(x, g, dh, dres, *, name, deps=(), tm=512):
    rows = x.shape[0]

    def body(x_ref, g_ref, dh_ref, dres_ref, *rest):
        dx_ref, dx16_ref, dg_ref = rest[len(deps):]
        xv = x_ref[...]
        r = lax.rsqrt(jnp.mean(xv * xv, axis=-1, keepdims=True) + EPS)
        xhat = xv * r
        dhv = dh_ref[...]
        part = jnp.sum(dhv * xhat, axis=0, keepdims=True)

        @pl.when(pl.program_id(0) == 0)
        def _():
            dg_ref[...] = part

        @pl.when(pl.program_id(0) > 0)
        def _():
            dg_ref[...] += part

        dxh = dhv * g_ref[...]
        dx = dres_ref[...] + r * (dxh - xhat * jnp.mean(dxh * xhat, axis=-1, keepdims=True))
        dx_ref[...] = dx
        dx16_ref[...] = dx.astype(BF16)

    row = pl.BlockSpec((tm, D_MODEL), lambda i: (i, 0))
    vec = pl.BlockSpec((1, D_MODEL), lambda i: (0, 0))
    return pl.pallas_call(
        body,
        name=name,
        grid=(rows // tm,),
        in_specs=[row, vec, row, row] + [ANY_SPEC] * len(deps),
        out_specs=[row, row, vec],
        out_shape=[jax.ShapeDtypeStruct((rows, D_MODEL), F32), jax.ShapeDtypeStruct((rows, D_MODEL), BF16),
                   jax.ShapeDtypeStruct((1, D_MODEL), F32)],
        compiler_params=_cp("arbitrary"),
    )(x, g, dh, dres, *deps)


_NT = (((1,), (1,)), ((), ()))
_TN = (((0,), (0,)), ((), ()))


def _norm_mm(x, g, w_t, *, swiglu, name, tm, tn, deps=()):
    rows = w_t.shape[0]
    half = rows // 2
    nj = (half if swiglu else rows) // tn
    nd = len(deps)

    def body(x_ref, g_ref, w_ref, *rest):
        outs = rest[nd:]
        h_ref, z_ref = outs[:2]

        def norm():
            xv = x_ref[...]
            r = lax.rsqrt(jnp.mean(xv * xv, axis=-1, keepdims=True) + EPS)
            h_ref[...] = (xv * r * g_ref[...]).astype(BF16)

        if swiglu:
            norm()
            h = h_ref[...]
            for j in range(nj):
                cols = slice(j * tn, (j + 1) * tn)
                gate = lax.dot_general(h, w_ref[j * tn:(j + 1) * tn, :], _NT, preferred_element_type=F32)
                up = lax.dot_general(h, w_ref[half + j * tn:half + (j + 1) * tn, :], _NT, preferred_element_type=F32)
                s = _sigmoid(gate)
                silu = gate * s
                z_ref[0, :, cols] = (up * (s + silu * (1.0 - s))).astype(BF16)
                z_ref[1, :, cols] = silu.astype(BF16)
                outs[2][:, cols] = (silu * up).astype(BF16)
        else:
            j = pl.program_id(1)
            pl.when(j == 0)(norm)
            w = w_ref[pl.ds(pl.multiple_of(j * tn, tn), tn), :]
            z_ref[...] = lax.dot_general(h_ref[...], w, _NT, preferred_element_type=F32)

    grid = (N_TOK // tm,) if swiglu else (N_TOK // tm, nj)
    row = pl.BlockSpec((tm, D_MODEL), lambda i, *_: (i, 0))
    out_specs = [row]
    out_shape = [jax.ShapeDtypeStruct((N_TOK, D_MODEL), BF16)]
    if swiglu:
        out_specs += [pl.BlockSpec((2, tm, half), lambda i: (0, i, 0)), pl.BlockSpec((tm, half), lambda i: (i, 0))]
        out_shape += [jax.ShapeDtypeStruct((2, N_TOK, half), BF16), jax.ShapeDtypeStruct((N_TOK, half), BF16)]
    else:
        out_specs.append(pl.BlockSpec((tm, tn), lambda i, j: (i, j)))
        out_shape.append(jax.ShapeDtypeStruct((N_TOK, rows), F32))
    return pl.pallas_call(
        body,
        name=name,
        grid=grid,
        in_specs=[row, pl.BlockSpec((1, D_MODEL), lambda *_: (0, 0)),
                  pl.BlockSpec((rows, D_MODEL), lambda *_: (0, 0), pipeline_mode=pl.Buffered(1))] + [ANY_SPEC] * nd,
        out_specs=out_specs,
        out_shape=out_shape,
        compiler_params=_cp(*(("parallel",) if swiglu else ("parallel", "arbitrary"))),
    )(x, g, w_t, *deps)


def _ffn_forward(x, g, w_in_t, w_out, *, scale, name, deps=(), tm=256, tn=1408):
    nd = len(deps)

    def body(x_ref, g_ref, wi_ref, wo_ref, *rest):
        y_ref, h_ref, z_ref, act_ref = rest[nd:]
        xv = x_ref[...]
        r = lax.rsqrt(jnp.mean(xv * xv, axis=-1, keepdims=True) + EPS)
        h = (xv * r * g_ref[...]).astype(BF16)
        h_ref[...] = h
        for j in range(D_FF // tn):
            cols = slice(j * tn, (j + 1) * tn)
            gate = lax.dot_general(h, wi_ref[j * tn:(j + 1) * tn, :], _NT, preferred_element_type=F32)
            up = lax.dot_general(h, wi_ref[D_FF + j * tn:D_FF + (j + 1) * tn, :], _NT, preferred_element_type=F32)
            s = _sigmoid(gate)
            silu = gate * s
            z_ref[0, :, cols] = (up * (s + silu * (1.0 - s))).astype(BF16)
            z_ref[1, :, cols] = silu.astype(BF16)
            act_ref[:, cols] = (silu * up).astype(BF16)
        y_ref[...] = xv + scale * jnp.dot(act_ref[...], wo_ref[...], preferred_element_type=F32)

    row = pl.BlockSpec((tm, D_MODEL), lambda i: (i, 0))
    whole = lambda rows: pl.BlockSpec((rows, D_MODEL), lambda i: (0, 0), pipeline_mode=pl.Buffered(1))
    return pl.pallas_call(
        body,
        name=name,
        grid=(N_TOK // tm,),
        in_specs=[row, pl.BlockSpec((1, D_MODEL), lambda i: (0, 0)), whole(2 * D_FF), whole(D_FF)] + [ANY_SPEC] * nd,
        out_specs=[row, row, pl.BlockSpec((2, tm, D_FF), lambda i: (0, i, 0)), pl.BlockSpec((tm, D_FF), lambda i: (i, 0))],
        out_shape=[jax.ShapeDtypeStruct((N_TOK, D_MODEL), F32), jax.ShapeDtypeStruct((N_TOK, D_MODEL), BF16),
                   jax.ShapeDtypeStruct((2, N_TOK, D_FF), BF16), jax.ShapeDtypeStruct((N_TOK, D_FF), BF16)],
        compiler_params=_cp("parallel"),
    )(x, g, w_in_t, w_out, *deps)


def _swiglu_dgrad(dy16, w_out, z, *, scale, name, deps=(), tm=512, tn=1408):
    def body(dy_ref, w_ref, z_ref, *rest):
        dz_ref = rest[len(deps)]
        dy = dy_ref[...]
        for j in range(D_FF // tn):
            cols = slice(j * tn, (j + 1) * tn)
            da = lax.dot_general(dy, w_ref[cols, :], _NT, preferred_element_type=F32) * scale
            dz_ref[0, :, cols] = (da * z_ref[0, :, cols].astype(F32)).astype(BF16)
            dz_ref[1, :, cols] = (da * z_ref[1, :, cols].astype(F32)).astype(BF16)

    planes = pl.BlockSpec((2, tm, D_FF), lambda i: (0, i, 0))
    return pl.pallas_call(
        body,
        name=name,
        grid=(N_TOK // tm,),
        in_specs=[pl.BlockSpec((tm, D_MODEL), lambda i: (i, 0)),
                  pl.BlockSpec((D_FF, D_MODEL), lambda i: (0, 0), pipeline_mode=pl.Buffered(1)), planes]
        + [ANY_SPEC] * len(deps),
        out_specs=planes,
        out_shape=jax.ShapeDtypeStruct((2, N_TOK, D_FF), BF16),
        compiler_params=_cp("parallel"),
    )(dy16, w_out, z, *deps)


def _ffn_dgrad(dy16, dres, w_out, z, w_in_t, x, g, *, scale, name, deps=(), tm=256, tn=1408):
    nd = len(deps)

    def body(dy_ref, dres_ref, wo_ref, z_ref, wi_ref, x_ref, g_ref, *rest):
        dz_ref, dx_ref, dx16_ref, dg_ref = rest[nd:]
        dy = dy_ref[...]
        for j in range(D_FF // tn):
            cols = slice(j * tn, (j + 1) * tn)
            da = lax.dot_general(dy, wo_ref[cols, :], _NT, preferred_element_type=F32) * scale
            dz_ref[0, :, cols] = (da * z_ref[0, :, cols].astype(F32)).astype(BF16)
            dz_ref[1, :, cols] = (da * z_ref[1, :, cols].astype(F32)).astype(BF16)
        dh = jnp.dot(dz_ref[0], wi_ref[:D_FF, :], preferred_element_type=F32) + jnp.dot(
            dz_ref[1], wi_ref[D_FF:, :], preferred_element_type=F32)
        xv = x_ref[...]
        r = lax.rsqrt(jnp.mean(xv * xv, axis=-1, keepdims=True) + EPS)
        xhat = xv * r
        part = jnp.sum(dh * xhat, axis=0, keepdims=True)

        @pl.when(pl.program_id(0) == 0)
        def _():
            dg_ref[...] = part

        @pl.when(pl.program_id(0) > 0)
        def _():
            dg_ref[...] += part

        dxh = dh * g_ref[...]
        dx = dres_ref[...] + r * (dxh - xhat * jnp.mean(dxh * xhat, axis=-1, keepdims=True))
        dx_ref[...] = dx
        dx16_ref[...] = dx.astype(BF16)

    row = pl.BlockSpec((tm, D_MODEL), lambda i: (i, 0))
    vec = pl.BlockSpec((1, D_MODEL), lambda i: (0, 0))
    planes = pl.BlockSpec((2, tm, D_FF), lambda i: (0, i, 0))
    whole = lambda rows: pl.BlockSpec((rows, D_MODEL), lambda i: (0, 0), pipeline_mode=pl.Buffered(1))
    return pl.pallas_call(
        body,
        name=name,
        grid=(N_TOK // tm,),
        in_specs=[row, row, whole(D_FF), planes, whole(2 * D_FF), row, vec] + [ANY_SPEC] * nd,
        out_specs=[planes, row, row, vec],
        out_shape=[jax.ShapeDtypeStruct((2, N_TOK, D_FF), BF16), jax.ShapeDtypeStruct((N_TOK, D_MODEL), F32),
                   jax.ShapeDtypeStruct((N_TOK, D_MODEL), BF16), jax.ShapeDtypeStruct((1, D_MODEL), F32)],
        compiler_params=_cp("arbitrary"),
    )(dy16, dres, w_out, z, w_in_t, x, g, *deps)


def _planes_wgrad(dz, h, *, name, deps=(), tm=1408):
    per_plane = D_FF // tm

    def body(a_ref, b_ref, *rest):
        o_ref = rest[len(deps)]
        o_ref[...] = lax.dot_general(a_ref[...], b_ref[...], _TN, preferred_element_type=F32).astype(BF16)

    return pl.pallas_call(
        body,
        name=name,
        grid=(2 * per_plane,),
        in_specs=[pl.BlockSpec((None, N_TOK, tm),
                               lambda i: (jnp.where(i < per_plane, 0, 1), 0, jnp.where(i < per_plane, i, i - per_plane))),
                  pl.BlockSpec((N_TOK, D_MODEL), lambda i: (0, 0), pipeline_mode=pl.Buffered(1))] + [ANY_SPEC] * len(deps),
        out_specs=pl.BlockSpec((tm, D_MODEL), lambda i: (i, 0)),
        out_shape=jax.ShapeDtypeStruct((2 * D_FF, D_MODEL), BF16),
        compiler_params=_cp("parallel"),
    )(dz, h, *deps)


def _dgrad_norm_bwd(dz, w_t, x, g, dres, *, name, deps=(), tm=512):
    planes = dz.ndim == 3
    rows = w_t.shape[0]
    half = rows // 2
    nd = len(deps)

    def body(a_ref, b_ref, x_ref, g_ref, dres_ref, *rest):
        dx_ref, dx16_ref, dg_ref = rest[nd:]
        if planes:
            dh = jnp.dot(a_ref[0], b_ref[:half, :], preferred_element_type=F32) + jnp.dot(
                a_ref[1], b_ref[half:, :], preferred_element_type=F32)
        else:
            dh = jnp.dot(a_ref[...], b_ref[...], preferred_element_type=F32)
        xv = x_ref[...]
        r = lax.rsqrt(jnp.mean(xv * xv, axis=-1, keepdims=True) + EPS)
        xhat = xv * r
        part = jnp.sum(dh * xhat, axis=0, keepdims=True)

        @pl.when(pl.program_id(0) == 0)
        def _():
            dg_ref[...] = part

        @pl.when(pl.program_id(0) > 0)
        def _():
            dg_ref[...] += part

        dxh = dh * g_ref[...]
        dx = dres_ref[...] + r * (dxh - xhat * jnp.mean(dxh * xhat, axis=-1, keepdims=True))
        dx_ref[...] = dx
        dx16_ref[...] = dx.astype(BF16)

    a_spec = pl.BlockSpec((2, tm, half), lambda i: (0, i, 0)) if planes else pl.BlockSpec((tm, rows), lambda i: (i, 0))
    row = pl.BlockSpec((tm, D_MODEL), lambda i: (i, 0))
    vec = pl.BlockSpec((1, D_MODEL), lambda i: (0, 0))
    return pl.pallas_call(
        body,
        name=name,
        grid=(N_TOK // tm,),
        in_specs=[a_spec, pl.BlockSpec((rows, D_MODEL), lambda i: (0, 0), pipeline_mode=pl.Buffered(1)), row, vec, row]
        + [ANY_SPEC] * nd,
        out_specs=[row, row, vec],
        out_shape=[jax.ShapeDtypeStruct((N_TOK, D_MODEL), F32), jax.ShapeDtypeStruct((N_TOK, D_MODEL), BF16),
                   jax.ShapeDtypeStruct((1, D_MODEL), F32)],
        compiler_params=_cp("arbitrary"),
    )(dz, w_t, x, g, dres, *deps)


def _loss_head(x, g, target, *, tm=512):
    def body(x_ref, g_ref, t_ref, dx_ref, dx16_ref, dg_ref, loss_ref):
        xv = x_ref[...]
        gv = g_ref[...]
        r = lax.rsqrt(jnp.mean(xv * xv, axis=-1, keepdims=True) + EPS)
        xhat = xv * r
        err = xhat * gv - t_ref[...]
        loss_part = jnp.zeros((1, 128), F32) + 0.5 * jnp.sum(jnp.mean(err * err, axis=-1, keepdims=True))
        dy = err * (1.0 / D_MODEL)
        dg_part = jnp.sum(dy * xhat, axis=0, keepdims=True)

        @pl.when(pl.program_id(0) == 0)
        def _():
            dg_ref[...] = dg_part
            loss_ref[...] = loss_part

        @pl.when(pl.program_id(0) > 0)
        def _():
            dg_ref[...] += dg_part
            loss_ref[...] += loss_part

        dxh = dy * gv
        dx = r * (dxh - xhat * jnp.mean(dxh * xhat, axis=-1, keepdims=True))
        dx_ref[...] = dx
        dx16_ref[...] = dx.astype(BF16)

    row = pl.BlockSpec((tm, D_MODEL), lambda i: (i, 0))
    vec = pl.BlockSpec((1, D_MODEL), lambda i: (0, 0))
    return pl.pallas_call(
        body,
        name="loss_head",
        grid=(N_TOK // tm,),
        in_specs=[row, vec, row],
        out_specs=[row, row, vec, pl.BlockSpec((1, 128), lambda i: (0, 0))],
        out_shape=[
            jax.ShapeDtypeStruct((N_TOK, D_MODEL), F32),
            jax.ShapeDtypeStruct((N_TOK, D_MODEL), BF16),
            jax.ShapeDtypeStruct((1, D_MODEL), F32),
            jax.ShapeDtypeStruct((1, 128), F32),
        ],
        compiler_params=_cp("arbitrary"),
    )(x, g, target)


XA_TQ = 2048
XA_SCALE = XA_DIM ** -0.5


def _attn_probs(q16, k16):
    s = lax.dot_general(q16, k16, _NT, preferred_element_type=F32) * XA_SCALE
    e = jnp.exp(s - jnp.max(s, axis=-1, keepdims=True))
    return e / jnp.sum(e, axis=-1, keepdims=True)


def _attn_fwd(z, kv, *, name):
    nt = SEQ // XA_TQ

    def body(q_ref, k_ref, v_ref, o_ref):
        p = _attn_probs(q_ref[...].astype(BF16), k_ref[...].astype(BF16))
        o_ref[...] = jnp.dot(p.astype(BF16), v_ref[...].astype(BF16), preferred_element_type=F32).astype(BF16)

    return pl.pallas_call(
        body,
        name=name,
        grid=(B_LOC, XA_HEADS, nt),
        in_specs=[
            pl.BlockSpec((XA_TQ, XA_DIM), lambda b, h, t: (b * nt + t, XA_OFF // XA_DIM + h)),
            pl.BlockSpec((MEM_LEN, XA_DIM), lambda b, h, t: (b, h)),
            pl.BlockSpec((MEM_LEN, XA_DIM), lambda b, h, t: (b, XA_HEADS + h)),
        ],
        out_specs=pl.BlockSpec((XA_TQ, XA_DIM), lambda b, h, t: (b * nt + t, h)),
        out_shape=jax.ShapeDtypeStruct((N_TOK, XA_HEADS * XA_DIM), BF16),
        compiler_params=_cp("parallel", "parallel", "arbitrary"),
    )(z, kv, kv)


def _attn_bwd(z, kv, dcat, *, do_off, name):
    nt = SEQ // XA_TQ

    def body(q_ref, k_ref, v_ref, do_ref, dq_ref, dk_ref, dv_ref):
        q16 = q_ref[...].astype(BF16)
        k16 = k_ref[...].astype(BF16)
        v16 = v_ref[...].astype(BF16)
        do16 = do_ref[...].astype(BF16)
        p = _attn_probs(q16, k16)
        dv_part = lax.dot_general(p.astype(BF16), do16, _TN, preferred_element_type=F32)
        dp = lax.dot_general(do16, v16, _NT, preferred_element_type=F32)
        ds16 = (p * (dp - jnp.sum(dp * p, axis=-1, keepdims=True)) * XA_SCALE).astype(BF16)
        dq_ref[...] = jnp.dot(ds16, k16, preferred_element_type=F32).astype(BF16)
        dk_part = lax.dot_general(ds16, q16, _TN, preferred_element_type=F32)

        @pl.when(pl.program_id(2) == 0)
        def _():
            dk_ref[...] = dk_part
            dv_ref[...] = dv_part

        @pl.when(pl.program_id(2) > 0)
        def _():
            dk_ref[...] += dk_part
            dv_ref[...] += dv_part

    qspec = pl.BlockSpec((XA_TQ, XA_DIM), lambda b, h, t: (b * nt + t, XA_OFF // XA_DIM + h))
    kspec = lambda off: pl.BlockSpec((MEM_LEN, XA_DIM), lambda b, h, t: (b, off + h))
    return pl.pallas_call(
        body,
        name=name,
        grid=(B_LOC, XA_HEADS, nt),
        in_specs=[qspec, kspec(0), kspec(XA_HEADS),
                  pl.BlockSpec((XA_TQ, XA_DIM), lambda b, h, t: (b * nt + t, do_off // XA_DIM + h))],
        out_specs=[pl.BlockSpec((XA_TQ, XA_DIM), lambda b, h, t: (b * nt + t, h)), kspec(0), kspec(0)],
        out_shape=[
            jax.ShapeDtypeStruct((N_TOK, XA_HEADS * XA_DIM), BF16),
            jax.ShapeDtypeStruct((B_LOC * MEM_LEN, XA_HEADS * XA_DIM), F32),
            jax.ShapeDtypeStruct((B_LOC * MEM_LEN, XA_HEADS * XA_DIM), F32),
        ],
        compiler_params=_cp("parallel", "parallel", "arbitrary"),
    )(z, kv, kv, dcat)


def _tril(n):
    return lax.broadcasted_iota(jnp.int32, (n, n), 0) >= lax.broadcasted_iota(jnp.int32, (n, n), 1)


def _lower_bound(lbl):
    e = jnp.exp(lbl - jnp.max(lbl, axis=0, keepdims=True))
    p = e / jnp.sum(e, axis=0, keepdims=True)
    return p[0:1, :], p


def _hgrn_gates(zq, zf, lb, tril_f):
    sig = _sigmoid(zf)
    f = lb + (1.0 - lb) * sig
    kk = 1.0 - f
    sq = _sigmoid(zq)
    q = zq * sq
    b = jnp.dot(tril_f, jnp.log(f), preferred_element_type=F32, precision=lax.Precision.HIGHEST)
    bl = b[HG_CHUNK - 1:HG_CHUNK, :]
    return q, sq, sig, f, kk, b, bl


HG_TB = 512
HG_CPB = HG_TB // HG_CHUNK
HG_NT = SEQ // HG_TB
HG_WIDTH = HG_HEADS * HG_DIM


def _head(h, section=0):
    return slice(section * HG_WIDTH + h * HG_DIM, section * HG_WIDTH + (h + 1) * HG_DIM)


def _hgrn_fwd(z, o_mem, lb_logits, gnorm):
    def body(zq_ref, zf_ref, zi_ref, zg_ref, omem_ref, lbl_ref, gn_ref, o_ref, opre_ref, sall_ref, st_ref):
        lb, _ = _lower_bound(lbl_ref[...])
        gn = gn_ref[...]
        mask = _tril(HG_CHUNK)
        tril_f = mask.astype(F32)
        o_ref[:, HG_WIDTH:] = omem_ref[...]

        @pl.when(pl.program_id(1) == 0)
        def _():
            st_ref[...] = jnp.zeros_like(st_ref)

        def chunk(c, carry):
            rows = pl.ds(pl.multiple_of(c * HG_CHUNK, HG_CHUNK), HG_CHUNK)
            q, _, _, _, kk, b, bl = _hgrn_gates(zq_ref[rows, :], zf_ref[rows, :], lb, tril_f)
            v16 = zi_ref[rows, :].astype(BF16)
            qd16 = (q * jnp.exp(b)).astype(BF16)
            ki16 = (kk * jnp.exp(-b)).astype(BF16)
            kd16 = (kk * jnp.exp(bl - b)).astype(BF16)
            ebl = jnp.exp(bl)
            zg = zg_ref[rows, :]
            gate = zg * _sigmoid(zg)
            for h in range(HG_HEADS):
                sl = _head(h)
                a = jnp.where(mask, lax.dot_general(qd16[:, sl], ki16[:, sl], _NT, preferred_element_type=F32), 0.0)
                st = st_ref[h]
                sall_ref[0, h, c] = st
                o = jnp.dot(a.astype(BF16), v16[:, sl], preferred_element_type=F32) + lax.dot_general(
                    qd16[:, sl], st.astype(BF16), _NT, preferred_element_type=F32)
                st_ref[h] = st * ebl[:, sl] + lax.dot_general(v16[:, sl], kd16[:, sl], _TN, preferred_element_type=F32)
                opre_ref[rows, sl] = o
                r = lax.rsqrt(jnp.mean(o * o, axis=-1, keepdims=True) + EPS)
                o_ref[rows, sl] = ((o * r * gn) * gate[:, sl]).astype(BF16)
            return carry

        lax.fori_loop(0, HG_CPB, chunk, 0, unroll=4)

    zspec = lambda s: pl.BlockSpec((HG_TB, HG_WIDTH), lambda b, t: (b * HG_NT + t, s))
    return pl.pallas_call(
        body,
        name="hgrn_fwd",
        grid=(B_LOC, HG_NT),
        in_specs=[zspec(0), zspec(1), zspec(2), zspec(3), zspec(0),
                  pl.BlockSpec((3, HG_WIDTH), lambda b, t: (0, 0)), pl.BlockSpec((1, HG_DIM), lambda b, t: (0, 0))],
        out_specs=[pl.BlockSpec((HG_TB, 2 * HG_WIDTH), lambda b, t: (b * HG_NT + t, 0)), zspec(0),
                   pl.BlockSpec((1, HG_HEADS, HG_CPB, HG_DIM, HG_DIM), lambda b, t: (b, 0, t, 0, 0))],
        out_shape=[
            jax.ShapeDtypeStruct((N_TOK, 2 * HG_WIDTH), BF16),
            jax.ShapeDtypeStruct((N_TOK, HG_WIDTH), F32),
            jax.ShapeDtypeStruct((B_LOC, HG_HEADS, HG_NCHUNK, HG_DIM, HG_DIM), F32),
        ],
        scratch_shapes=[pltpu.VMEM((HG_HEADS, HG_DIM, HG_DIM), F32)],
        compiler_params=_cp("parallel", "arbitrary"),
    )(z, z, z, z, o_mem, lb_logits, gnorm)


def _hgrn_bwd(z, opre, dcat, dq_mem, sall, lb_logits, gnorm):
    def body(zq_ref, zf_ref, zi_ref, zg_ref, opre_ref, dout_ref, dqm_ref, sall_ref, lbl_ref, gn_ref,
             dz_ref, dlbl_ref, dgn_ref, dst_ref, dlb_ref, dgn_acc, db_ref, dkk_ref, dbl_ref):
        b_id, t_id = pl.program_id(0), pl.program_id(1)
        lb, p = _lower_bound(lbl_ref[...])
        gn = gn_ref[...]
        mask = _tril(HG_CHUNK)
        tril_f = mask.astype(F32)
        dz_ref[:, 4 * HG_WIDTH:] = dqm_ref[...]

        @pl.when(t_id == 0)
        def _():
            dst_ref[...] = jnp.zeros_like(dst_ref)
            dlb_ref[...] = jnp.zeros_like(dlb_ref)

        @pl.when((b_id == 0) & (t_id == 0))
        def _():
            dgn_acc[...] = jnp.zeros_like(dgn_acc)

        def chunk(i, carry):
            c = HG_CPB - 1 - i
            rows = pl.ds(pl.multiple_of(c * HG_CHUNK, HG_CHUNK), HG_CHUNK)
            zq, zg = zq_ref[rows, :], zg_ref[rows, :]
            q, sq, sig, f, kk, b, bl = _hgrn_gates(zq, zf_ref[rows, :], lb, tril_f)
            v16 = zi_ref[rows, :].astype(BF16)
            eb, enb, ebl_b, ebl = jnp.exp(b), jnp.exp(-b), jnp.exp(bl - b), jnp.exp(bl)
            qd, ki, kd = q * eb, kk * enb, kk * ebl_b
            qd16, ki16, kd16 = qd.astype(BF16), ki.astype(BF16), kd.astype(BF16)
            o_all = opre_ref[rows, :]
            dout = dout_ref[rows, :]
            sg = _sigmoid(zg)
            d_on_all = dout * (zg * sg)
            dgate = dout * (sg * (1.0 + zg * (1.0 - sg)))
            dq_scale = eb * (sq * (1.0 + zq * (1.0 - sq)))
            for h in range(HG_HEADS):
                sl = _head(h)
                o = o_all[:, sl]
                r = lax.rsqrt(jnp.mean(o * o, axis=-1, keepdims=True) + EPS)
                ohat = o * r
                d_on = d_on_all[:, sl]
                dz_ref[rows, _head(h, 3)] = (dgate[:, sl] * (ohat * gn)).astype(BF16)
                dgn_acc[...] += jnp.sum(d_on * ohat, axis=0, keepdims=True)
                dohat = d_on * gn
                do16 = (r * (dohat - ohat * jnp.mean(dohat * ohat, axis=-1, keepdims=True))).astype(BF16)
                st = sall_ref[0, h, c]
                dst = dst_ref[h]
                st16, dst16 = st.astype(BF16), dst.astype(BF16)
                qd_h, ki_h, kd_h, v_h = qd16[:, sl], ki16[:, sl], kd16[:, sl], v16[:, sl]
                a16 = jnp.where(mask, lax.dot_general(qd_h, ki_h, _NT, preferred_element_type=F32), 0.0).astype(BF16)
                da16 = jnp.where(mask, lax.dot_general(do16, v_h, _NT, preferred_element_type=F32), 0.0).astype(BF16)
                dv = lax.dot_general(a16, do16, _TN, preferred_element_type=F32) + lax.dot_general(
                    kd_h, dst16, _NT, preferred_element_type=F32)
                dqd = jnp.dot(da16, ki_h, preferred_element_type=F32) + jnp.dot(do16, st16, preferred_element_type=F32)
                dki = lax.dot_general(da16, qd_h, _TN, preferred_element_type=F32)
                dkd = jnp.dot(v_h, dst16, preferred_element_type=F32)
                dbl_ref[:, sl] = jnp.sum(dkd * kd[:, sl], axis=0, keepdims=True) + ebl[:, sl] * jnp.sum(
                    st * dst, axis=0, keepdims=True)
                dst_ref[h] = dst * ebl[:, sl] + lax.dot_general(do16, qd_h, _TN, preferred_element_type=F32)
                dz_ref[rows, _head(h, 2)] = dv.astype(BF16)
                dz_ref[rows, sl] = (dqd * dq_scale[:, sl]).astype(BF16)
                dkk_ref[:, sl] = dki * enb[:, sl] + dkd * ebl_b[:, sl]
                db_ref[:, sl] = dqd * qd[:, sl] - dki * ki[:, sl] - dkd * kd[:, sl]
            dlogf = lax.dot_general(tril_f, db_ref[...], _TN, preferred_element_type=F32,
                                    precision=lax.Precision.HIGHEST) + dbl_ref[...]
            df = dlogf / f - dkk_ref[...]
            dz_ref[rows, HG_WIDTH:2 * HG_WIDTH] = (df * (1.0 - lb) * sig * (1.0 - sig)).astype(BF16)
            dlb_ref[...] += jnp.sum(df * (1.0 - sig), axis=0, keepdims=True)
            return carry

        lax.fori_loop(0, HG_CPB, chunk, 0, unroll=4)

        @pl.when(t_id == HG_NT - 1)
        def _():
            row0 = (lax.broadcasted_iota(jnp.int32, (3, HG_WIDTH), 0) == 0).astype(F32)
            dlbl_part = dlb_ref[...] * lb * (row0 - p)

            @pl.when(b_id == 0)
            def _():
                dlbl_ref[...] = dlbl_part

            @pl.when(b_id > 0)
            def _():
                dlbl_ref[...] += dlbl_part

            dgn_ref[...] = dgn_acc[...]

    rev = lambda b, t: b * HG_NT + HG_NT - 1 - t
    zspec = lambda s: pl.BlockSpec((HG_TB, HG_WIDTH), lambda b, t: (rev(b, t), s))
    return pl.pallas_call(
        body,
        name="hgrn_bwd",
        grid=(B_LOC, HG_NT),
        in_specs=[zspec(0), zspec(1), zspec(2), zspec(3), zspec(0), zspec(0), zspec(0),
                  pl.BlockSpec((1, HG_HEADS, HG_CPB, HG_DIM, HG_DIM), lambda b, t: (b, 0, HG_NT - 1 - t, 0, 0)),
                  pl.BlockSpec((3, HG_WIDTH), lambda b, t: (0, 0)), pl.BlockSpec((1, HG_DIM), lambda b, t: (0, 0))],
        out_specs=[pl.BlockSpec((HG_TB, 5 * HG_WIDTH), lambda b, t: (rev(b, t), 0)),
                   pl.BlockSpec((3, HG_WIDTH), lambda b, t: (0, 0)), pl.BlockSpec((1, HG_DIM), lambda b, t: (0, 0))],
        out_shape=[jax.ShapeDtypeStruct((N_TOK, 5 * HG_WIDTH), BF16),
                   jax.ShapeDtypeStruct((3, HG_WIDTH), F32), jax.ShapeDtypeStruct((1, HG_DIM), F32)],
        scratch_shapes=[pltpu.VMEM((HG_HEADS, HG_DIM, HG_DIM), F32), pltpu.VMEM((1, HG_WIDTH), F32),
                        pltpu.VMEM((1, HG_DIM), F32), pltpu.VMEM((HG_CHUNK, HG_WIDTH), F32),
                        pltpu.VMEM((HG_CHUNK, HG_WIDTH), F32), pltpu.VMEM((1, HG_WIDTH), F32)],
        compiler_params=_cp("arbitrary", "arbitrary"),
    )(z, z, z, z, opre, dcat, dq_mem, sall, lb_logits, gnorm)


GM_TM = 256


def _gmlp_norm(zv, ln_g, ln_b):
    gv, dgelu = _gelu_parts(zv)
    xc = gv - jnp.mean(gv, axis=-1, keepdims=True)
    rstd = lax.rsqrt(jnp.mean(xc * xc, axis=-1, keepdims=True) + EPS)
    vhat = xc * rstd
    return vhat * ln_g + ln_b, vhat, rstd, dgelu


def _gmlp_specs():
    half = lambda j: pl.BlockSpec((GM_TM, GM_WIDTH), lambda i: (i, j))
    vec = pl.BlockSpec((1, GM_WIDTH), lambda i: (0, 0))
    w = pl.BlockSpec((GM_GROUPS, GM_CHUNK, GM_CHUNK), lambda i: (0, 0, 0))
    bt = pl.BlockSpec((GM_CHUNK, GM_GROUPS), lambda i: (0, 0))
    return half, vec, w, bt


def _gmlp_fwd(z, o_mem, ln_g, ln_b, w_s, b_st):
    def body(zu_ref, zv_ref, omem_ref, g_ref, b_ref, w_ref, bt_ref, o_ref):
        o_ref[:, GM_WIDTH:] = omem_ref[...]
        u, _ = _gelu_parts(zu_ref[...])
        v, _, _, _ = _gmlp_norm(zv_ref[...], g_ref[...], b_ref[...])
        v16 = v.astype(BF16)
        mask = _tril(GM_CHUNK)
        bt = bt_ref[...]
        for g in range(GM_GROUPS):
            wm16 = jnp.where(mask, w_ref[g], 0.0).astype(BF16)
            cols = slice(g * GM_GDIM, (g + 1) * GM_GDIM)
            for c in range(GM_TM // GM_CHUNK):
                rows = slice(c * GM_CHUNK, (c + 1) * GM_CHUNK)
                mixed = jnp.dot(wm16, v16[rows, cols], preferred_element_type=F32) + bt[:, g:g + 1]
                o_ref[rows, cols] = (u[rows, cols] * mixed).astype(BF16)

    half, vec, w, bt = _gmlp_specs()
    return pl.pallas_call(
        body,
        name="gmlp_fwd",
        grid=(N_TOK // GM_TM,),
        in_specs=[half(0), half(1), pl.BlockSpec((GM_TM, XA_HEADS * XA_DIM), lambda i: (i, 0)), vec, vec, w, bt],
        out_specs=pl.BlockSpec((GM_TM, GM_WIDTH + XA_HEADS * XA_DIM), lambda i: (i, 0)),
        out_shape=jax.ShapeDtypeStruct((N_TOK, GM_WIDTH + XA_HEADS * XA_DIM), BF16),
        compiler_params=_cp("parallel"),
    )(z, z, o_mem, ln_g, ln_b, w_s, b_st)


def _gmlp_bwd(z, dcat, dq_mem, ln_g, ln_b, w_s, b_st):
    def body(zu_ref, zv_ref, dout_ref, dqm_ref, g_ref, b_ref, w_ref, bt_ref,
             dz_ref, dw_ref, dbt_ref, dg_ref, db_ref, dv_ref):
        dz_ref[:, 2 * GM_WIDTH:] = dqm_ref[...]
        @pl.when(pl.program_id(0) == 0)
        def _():
            dw_ref[...] = jnp.zeros_like(dw_ref)
            dbt_ref[...] = jnp.zeros_like(dbt_ref)
            dg_ref[...] = jnp.zeros_like(dg_ref)
            db_ref[...] = jnp.zeros_like(db_ref)

        zu = zu_ref[...]
        u, du_dz = _gelu_parts(zu)
        ln_g = g_ref[...]
        v, vhat, rstd, dgv_dz = _gmlp_norm(zv_ref[...], ln_g, b_ref[...])
        v16 = v.astype(BF16)
        dout = dout_ref[...]
        dmixed = dout * u
        dm16 = dmixed.astype(BF16)
        mask = _tril(GM_CHUNK)
        bt = bt_ref[...]
        group_id = lax.broadcasted_iota(jnp.int32, (1, GM_GROUPS), 1)
        dbt = jnp.zeros((GM_CHUNK, GM_GROUPS), F32)
        for g in range(GM_GROUPS):
            wm16 = jnp.where(mask, w_ref[g], 0.0).astype(BF16)
            cols = slice(g * GM_GDIM, (g + 1) * GM_GDIM)
            dw = jnp.zeros((GM_CHUNK, GM_CHUNK), F32)
            dbt_g = jnp.zeros((GM_CHUNK, 1), F32)
            for c in range(GM_TM // GM_CHUNK):
                rows = slice(c * GM_CHUNK, (c + 1) * GM_CHUNK)
                mixed = jnp.dot(wm16, v16[rows, cols], preferred_element_type=F32) + bt[:, g:g + 1]
                dz_ref[rows, cols] = (dout[rows, cols] * mixed * du_dz[rows, cols]).astype(BF16)
                dw += lax.dot_general(dm16[rows, cols], v16[rows, cols], _NT, preferred_element_type=F32)
                dbt_g += jnp.sum(dmixed[rows, cols], axis=-1, keepdims=True)
                dv_ref[rows, cols] = lax.dot_general(wm16, dm16[rows, cols], _TN, preferred_element_type=F32)
            dw_ref[g] += jnp.where(mask, dw, 0.0)
            dbt = dbt + dbt_g * (group_id == g).astype(F32)
        dbt_ref[...] += dbt
        dv = dv_ref[...]
        dg_ref[...] += jnp.sum(dv * vhat, axis=0, keepdims=True)
        db_ref[...] += jnp.sum(dv, axis=0, keepdims=True)
        dvh = dv * ln_g
        dgv = rstd * (dvh - jnp.mean(dvh, axis=-1, keepdims=True) - vhat * jnp.mean(dvh * vhat, axis=-1, keepdims=True))
        dz_ref[:, GM_WIDTH:2 * GM_WIDTH] = (dgv * dgv_dz).astype(BF16)

    half, vec, w, bt = _gmlp_specs()
    dz_width = 2 * GM_WIDTH + XA_HEADS * XA_DIM
    return pl.pallas_call(
        body,
        name="gmlp_bwd",
        grid=(N_TOK // GM_TM,),
        in_specs=[half(0), half(1), half(0), pl.BlockSpec((GM_TM, XA_HEADS * XA_DIM), lambda i: (i, 0)), vec, vec, w, bt],
        out_specs=[pl.BlockSpec((GM_TM, dz_width), lambda i: (i, 0)), w, bt, vec, vec],
        out_shape=[jax.ShapeDtypeStruct((N_TOK, dz_width), BF16),
                   jax.ShapeDtypeStruct((GM_GROUPS, GM_CHUNK, GM_CHUNK), F32),
                   jax.ShapeDtypeStruct((GM_CHUNK, GM_GROUPS), F32),
                   jax.ShapeDtypeStruct((1, GM_WIDTH), F32), jax.ShapeDtypeStruct((1, GM_WIDTH), F32)],
        scratch_shapes=[pltpu.VMEM((GM_TM, GM_WIDTH), F32)],
        compiler_params=_cp("arbitrary"),
    )(z, z, dcat, dq_mem, ln_g, ln_b, w_s, b_st)


def _own_slot(shape):
    return pl.BlockSpec((None,) + tuple(shape), lambda i, me_ref: (me_ref[0],) + (0,) * len(shape))


def _place_rows(w, layer, cuts_columns, me, *, name, deps=()):
    _, r, c = w.shape
    n = c if cuts_columns else r

    def body(me_ref, w_ref, *rest):
        o_ref = rest[len(deps)]
        wv = w_ref[...]
        o_ref[...] = (wv.T if cuts_columns else wv).astype(BF16)

    return pl.pallas_call(
        body,
        name=name,
        grid_spec=pltpu.PrefetchScalarGridSpec(
            num_scalar_prefetch=1, grid=(1,),
            in_specs=[pl.BlockSpec((None, r, c), lambda i, me_ref: (layer, 0, 0))] + [ANY_SPEC] * len(deps),
            out_specs=_own_slot((n, D_MODEL))),
        out_shape=jax.ShapeDtypeStruct((N_DEV, n, D_MODEL), BF16),
        compiler_params=_cp("arbitrary"),
    )(me, w, *deps)


def _place_ln(ln_g, ln_b, me):
    blk = ln_g.shape[1]

    def body(me_ref, g_ref, b_ref, o_ref):
        o_ref[...] = jnp.zeros_like(o_ref)
        o_ref[0:1, :] = g_ref[...]
        o_ref[1:2, :] = b_ref[...]

    vec = pl.BlockSpec((1, blk), lambda i, me_ref: (0, 0))
    return pl.pallas_call(
        body,
        name="place_ln",
        grid_spec=pltpu.PrefetchScalarGridSpec(
            num_scalar_prefetch=1, grid=(1,), in_specs=[vec, vec], out_specs=_own_slot((8, blk))),
        out_shape=jax.ShapeDtypeStruct((N_DEV, 8, blk), F32),
        compiler_params=_cp("arbitrary"),
    )(me, ln_g, ln_b)


def _place_slab(a, me, *, name):
    def body(me_ref, a_ref, o_ref):
        o_ref[...] = a_ref[...]

    return pl.pallas_call(
        body,
        name=name,
        grid_spec=pltpu.PrefetchScalarGridSpec(
            num_scalar_prefetch=1, grid=(1,),
            in_specs=[pl.BlockSpec(a.shape, lambda i, me_ref: (0, 0))], out_specs=_own_slot(a.shape)),
        out_shape=jax.ShapeDtypeStruct((N_DEV,) + a.shape, a.dtype),
        compiler_params=_cp("arbitrary"),
    )(me, a)


def _place_own(grads, me, *, name):
    k = len(grads)

    def body(me_ref, *refs):
        for src, dst in zip(refs[:k], refs[k:]):
            dst[...] = src[...]

    specs = [_own_slot(g.shape[1:]) for g in grads]
    return pl.pallas_call(
        body,
        name=name,
        grid_spec=pltpu.PrefetchScalarGridSpec(num_scalar_prefetch=1, grid=(1,), in_specs=specs, out_specs=specs),
        out_shape=[jax.ShapeDtypeStruct(g.shape, g.dtype) for g in grads],
        compiler_params=_cp("arbitrary"),
    )(me, *grads)


def _mesh_pos():
    x, y, c = (lax.axis_index(a) for a in MESH_AXES)
    return x, y, c, 4 * x + 2 * y + c


def _peer(x, y, c, r):
    px = 1 - x if r & 4 else x
    py = 1 - y if r & 2 else y
    pc = 1 - c if r & 1 else c
    return (px, py, pc), 4 * px + 2 * py + pc


RELATIONS = {"scatter": (1, 2, 3, 4, 5, 6, 7), "gather_all": (1, 2, 3, 4, 5, 6, 7), "gather_chips": (1, 2, 4, 6),
             "gather_sibling": (2, 4, 6)}


def _peer_copies(srcs, lands, send_sems, recv_sems, mode, waits):
    x, y, c, me = _mesh_pos()
    rel = RELATIONS[mode]
    pairs = []
    for ri, r in enumerate(rel):
        if mode == "gather_sibling":
            peer, _ = _peer(x, y, c, 1)
            _, sent_blk = _peer(x, y, c, r)
            _, got_blk = _peer(x, y, c, r ^ 1)
        else:
            peer, peer_blk = _peer(x, y, c, r)
            sent_blk, got_blk = (peer_blk if mode == "scatter" else me), peer_blk
        for k, (src, land) in enumerate(zip(srcs, lands)):
            idx = k * len(rel) + ri
            sems = dict(send_sem=send_sems.at[idx], recv_sem=recv_sems.at[idx], device_id=peer,
                        device_id_type=pl.DeviceIdType.MESH)
            dst_blk = sent_blk if mode == "gather_sibling" else me
            mine = pltpu.make_async_remote_copy(src_ref=src.at[sent_blk], dst_ref=land.at[dst_blk], **sems)
            theirs = pltpu.make_async_remote_copy(src_ref=src.at[sent_blk], dst_ref=land.at[got_blk], **sems) if waits else None
            pairs.append((mine, theirs))
    return pairs


DATAFLOW = pltpu.SideEffectType.DATAFLOW_SIDE_EFFECTING


def _in_hbm(a):
    return pltpu.with_memory_space_constraint(a, pltpu.HBM)


def _copies_start(srcs, lands, *, mode, name, deps=()):
    gather = mode != "scatter"
    arrs = list(lands) if gather else list(srcs) + list(lands)
    n, k, nd = len(arrs), len(lands), len(deps)

    def body(*refs):
        ins, send_sems, recv_sems, token = refs[:n], refs[n + nd], refs[n + nd + 1], refs[2 * n + nd + 2]
        src_refs, land_refs = (ins, ins) if gather else (ins[:k], ins[k:])
        for mine, _ in _peer_copies(src_refs, land_refs, send_sems, recv_sems, mode, waits=False):
            mine.start()
        token[...] = jnp.zeros_like(token)

    n_cp = k * len(RELATIONS[mode])
    return pl.pallas_call(
        body,
        name=name,
        in_specs=[HBM_SPEC] * n + [ANY_SPEC] * nd,
        out_specs=(SEM_SPEC, SEM_SPEC, *[HBM_SPEC] * n, pl.BlockSpec(memory_space=pltpu.VMEM)),
        out_shape=(pltpu.SemaphoreType.DMA((n_cp,)), pltpu.SemaphoreType.DMA((n_cp,)),
                   *[pltpu.HBM(a.shape, a.dtype) for a in arrs], jax.ShapeDtypeStruct((8, 128), F32)),
        input_output_aliases={i: 2 + i for i in range(n)},
        compiler_params=pltpu.CompilerParams(has_side_effects=DATAFLOW),
    )(*[_in_hbm(a) for a in arrs], *deps)


def _copies_wait(arrs, send_sems, recv_sems, after, *, n_lands, mode, name):
    n, k = len(arrs), n_lands
    gather = mode != "scatter"

    def body(*refs):
        ins, send_sems, recv_sems = refs[:n], refs[n], refs[n + 1]
        src_refs, land_refs = (ins, ins) if gather else (ins[:k], ins[k:])
        for mine, theirs in _peer_copies(src_refs, land_refs, send_sems, recv_sems, mode, waits=True):
            mine.wait_send()
            theirs.wait_recv()

    outs = pl.pallas_call(
        body,
        name=name,
        in_specs=[HBM_SPEC] * n + [SEM_SPEC, SEM_SPEC] + [ANY_SPEC] * len(after),
        out_specs=[HBM_SPEC] * n,
        out_shape=[pltpu.HBM(a.shape, a.dtype) for a in arrs],
        input_output_aliases={i: i for i in range(n)},
        compiler_params=pltpu.CompilerParams(has_side_effects=DATAFLOW),
    )(*arrs, send_sems, recv_sems, *after)
    return outs[n - k:]


def _adamw(w, g, m, v):
    m = ADAM_B1 * m + (1.0 - ADAM_B1) * g
    v = ADAM_B2 * v + (1.0 - ADAM_B2) * (g * g)
    m_hat = m / (1.0 - ADAM_B1 ** ADAM_STEP)
    v_hat = v / (1.0 - ADAM_B2 ** ADAM_STEP)
    return -ADAM_LR * (m_hat / (jnp.sqrt(v_hat) + ADAM_EPS) + ADAM_WD * w), m, v


ADAM_TC = 512


def _adam_big(slots, w, m, v, cuts_columns, *, name):
    layers, n, nj = len(slots), slots[0].shape[1], D_MODEL // ADAM_TC

    def body(*refs):
        s_refs = refs[:layers]
        w_ref, m_ref, v_ref, g_ref, d_ref, nm_ref, nv_ref, acc_ref = refs[layers:]
        for ll in range(layers):
            @pl.when(pl.program_id(0) == ll)
            def _(s_ref=s_refs[ll]):
                g = s_ref[0].astype(F32)
                for s in range(1, N_DEV):
                    g = g + s_ref[s].astype(F32)
                acc_ref[...] = g

        g = acc_ref[...].T if cuts_columns else acc_ref[...]
        g_ref[...] = g
        d_ref[...], nm_ref[...], nv_ref[...] = _adamw(w_ref[...], g, m_ref[...], v_ref[...])

    def slot_spec(ll):
        return pl.BlockSpec((N_DEV, n, ADAM_TC),
                            lambda l, j: (0, 0, jnp.where(l < ll, 0, jnp.where(l > ll, nj - 1, j))))

    if cuts_columns:
        w_spec = pl.BlockSpec((None, ADAM_TC, n), lambda l, j: (l, j, 0))
    else:
        w_spec = pl.BlockSpec((None, n, ADAM_TC), lambda l, j: (l, 0, j))
    return pl.pallas_call(
        body,
        name=name,
        grid=(layers, nj),
        in_specs=[slot_spec(ll) for ll in range(layers)] + [w_spec] * 3,
        out_specs=[w_spec] * 4,
        out_shape=[jax.ShapeDtypeStruct(w.shape, F32)] * 4,
        scratch_shapes=[pltpu.VMEM((n, ADAM_TC), F32)],
        compiler_params=_cp("arbitrary", "arbitrary"),
    )(*slots, w, m, v)


def _adam_slabs(slots, ws, ms, vs):
    n = len(slots)

    def body(*refs):
        ins, outs = refs[:4 * n], refs[4 * n:]
        for k in range(n):
            s_ref, w_ref, m_ref, v_ref = ins[k], ins[n + k], ins[2 * n + k], ins[3 * n + k]
            g = s_ref[0]
            for s in range(1, N_DEV):
                g = g + s_ref[s]
            outs[4 * k][...] = g
            outs[4 * k + 1][...], outs[4 * k + 2][...], outs[4 * k + 3][...] = _adamw(w_ref[...], g, m_ref[...], v_ref[...])

    res = pl.pallas_call(
        body,
        name="small_adamw",
        out_shape=[jax.ShapeDtypeStruct(w.shape, F32) for w in ws for _ in range(4)],
        compiler_params=pltpu.CompilerParams(vmem_limit_bytes=VMEM_LIMIT_BYTES),
    )(*slots, *ws, *ms, *vs)
    return [res[4 * k:4 * k + 4] for k in range(n)]


def _adam_vecs(gs, ws, ms, vs):
    n = len(gs)

    def body(*refs):
        ins, outs = refs[:4 * n], refs[4 * n:]
        for k in range(n):
            outs[3 * k][...], outs[3 * k + 1][...], outs[3 * k + 2][...] = _adamw(
                ins[n + k][...], ins[k][...], ins[2 * n + k][...], ins[3 * n + k][...])

    res = pl.pallas_call(
        body,
        name="ln_adamw",
        out_shape=[jax.ShapeDtypeStruct(w.shape, F32) for w in ws for _ in range(3)],
        compiler_params=pltpu.CompilerParams(vmem_limit_bytes=VMEM_LIMIT_BYTES),
    )(*gs, *ws, *ms, *vs)
    return [res[3 * k:3 * k + 3] for k in range(n)]


SLAB_AT = dict(mem_norm=0, lb_logits=1, ffn1_norm=4, mix_norm=6, hgrn_gnorm=8, gmlp_ln_g=9, gmlp_ln_b=11,
               gmlp_b_s=13, ffn2_norm=14, final_norm=16)
SLAB_ROWS = 24
LOSS_ROW = 17
SMALL_SHARDED = ("gmlp_ln_g", "gmlp_ln_b")


def _pack_slab(parts, *, name, deps=()):
    flat, plan = [], []
    for pname, at in SLAB_AT.items():
        for a in parts.get(pname, ()):
            flat.append(a)
            plan.append((at, a.shape))
            at += max(1, a.shape[0] * a.shape[1] // D_MODEL)
    for a in parts.get("loss", ()):
        flat.append(a)
        plan.append((LOSS_ROW, a.shape))

    def body(*refs):
        o_ref = refs[-1]
        o_ref[...] = jnp.zeros_like(o_ref)
        for ref, (at, (r, w)) in zip(refs, plan):
            if w == D_MODEL or r == 1 and w < D_MODEL:
                o_ref[at:at + r, 0:w] = ref[...]
            elif w < D_MODEL:
                for j in range(r):
                    o_ref[at:at + 1, j * w:(j + 1) * w] = ref[j:j + 1, :]
            else:
                for j in range(w // D_MODEL):
                    o_ref[at + j:at + j + 1, :] = ref[:, j * D_MODEL:(j + 1) * D_MODEL]

    return pl.pallas_call(
        body,
        name=name,
        in_specs=[pl.BlockSpec(memory_space=pltpu.VMEM)] * len(flat) + [ANY_SPEC] * len(deps),
        out_shape=jax.ShapeDtypeStruct((SLAB_ROWS, D_MODEL), F32),
        compiler_params=pltpu.CompilerParams(vmem_limit_bytes=VMEM_LIMIT_BYTES),
    )(*flat, *deps)


def _unpack_slab(slab, shapes):
    out = {}
    for pname, at in SLAB_AT.items():
        if pname in SMALL_SHARDED:
            continue
        size = math.prod(shapes[pname])
        rows = max(1, size // D_MODEL)
        out[pname] = slab[at:at + rows].reshape(-1)[:size].reshape(shapes[pname])
    return out


def _take_weights(full, new):
    deps = full.pop("deps", ()) + new.pop("deps", ())
    full.update(new, deps=deps)


def _ffn_fwd(x, norm_g, block, layer, full, get_weights):
    _take_weights(full, get_weights((layer, f"{block}_in"), (x,)))
    y, h, z, act = _ffn_forward(x, norm_g, full[(f"{block}_w_in", layer)], full[(f"{block}_w_out", layer)], scale=0.5,
                                deps=full.pop("deps", ()), name=f"l{layer}_{block}")
    _take_weights(full, get_weights((layer, f"{block}_out"), (y,)))
    return y, (x, h, z, act)


def _ffn_bwd(dy, dy16, saved, norm_g, w_in_t, w_out, tag, deps=(), after_out_wgrad=None, before_in_wgrad=None):
    x, h, z, act = saved
    dw_out = _mm(act, dy16, ta=True, tm=1408, tn=D_MODEL, tk=N_TOK, out_dtype=BF16, scale=0.5, deps=deps,
                 name=f"{tag}_out_wgrad")
    sent = after_out_wgrad(dw_out) if after_out_wgrad is not None else ()
    dz, dx, dx16, dg = _ffn_dgrad(dy16, dy, w_out, z, w_in_t, x, norm_g, scale=0.5, deps=sent, name=f"{tag}_dgrad")
    wdeps = before_in_wgrad(dg) if before_in_wgrad is not None else ()
    dw_in_t = _planes_wgrad(dz, h, deps=wdeps, name=f"{tag}_in_wgrad")
    return dx, dx16, dg, dw_in_t, dw_out


def kernel(x, mem, mem_norm, lb_logits, ffn1_norm, ffn1_w_in, ffn1_w_out, mix_norm, mem_w_kv, hgrn_w_in, hgrn_gnorm, hgrn_w_out, gmlp_w_in, gmlp_ln_g, gmlp_ln_b, gmlp_w_s, gmlp_b_s, gmlp_w_out, ffn2_norm, ffn2_w_in, ffn2_w_out, final_norm, loss_target, m_mem_norm, m_lb_logits, m_ffn1_norm, m_ffn1_w_in, m_ffn1_w_out, m_mix_norm, m_mem_w_kv, m_hgrn_w_in, m_hgrn_gnorm, m_hgrn_w_out, m_gmlp_w_in, m_gmlp_ln_g, m_gmlp_ln_b, m_gmlp_w_s, m_gmlp_b_s, m_gmlp_w_out, m_ffn2_norm, m_ffn2_w_in, m_ffn2_w_out, m_final_norm, v_mem_norm, v_lb_logits, v_ffn1_norm, v_ffn1_w_in, v_ffn1_w_out, v_mix_norm, v_mem_w_kv, v_hgrn_w_in, v_hgrn_gnorm, v_hgrn_w_out, v_gmlp_w_in, v_gmlp_ln_g, v_gmlp_ln_b, v_gmlp_w_s, v_gmlp_b_s, v_gmlp_w_out, v_ffn2_norm, v_ffn2_w_in, v_ffn2_w_out, v_final_norm):
    weights = dict(mem_norm=mem_norm, lb_logits=lb_logits, ffn1_norm=ffn1_norm, ffn1_w_in=ffn1_w_in, ffn1_w_out=ffn1_w_out, mix_norm=mix_norm, mem_w_kv=mem_w_kv, hgrn_w_in=hgrn_w_in, hgrn_gnorm=hgrn_gnorm, hgrn_w_out=hgrn_w_out, gmlp_w_in=gmlp_w_in, gmlp_ln_g=gmlp_ln_g, gmlp_ln_b=gmlp_ln_b, gmlp_w_s=gmlp_w_s, gmlp_b_s=gmlp_b_s, gmlp_w_out=gmlp_w_out, ffn2_norm=ffn2_norm, ffn2_w_in=ffn2_w_in, ffn2_w_out=ffn2_w_out, final_norm=final_norm)
    mom_m = dict(mem_norm=m_mem_norm, lb_logits=m_lb_logits, ffn1_norm=m_ffn1_norm, ffn1_w_in=m_ffn1_w_in, ffn1_w_out=m_ffn1_w_out, mix_norm=m_mix_norm, mem_w_kv=m_mem_w_kv, hgrn_w_in=m_hgrn_w_in, hgrn_gnorm=m_hgrn_gnorm, hgrn_w_out=m_hgrn_w_out, gmlp_w_in=m_gmlp_w_in, gmlp_ln_g=m_gmlp_ln_g, gmlp_ln_b=m_gmlp_ln_b, gmlp_w_s=m_gmlp_w_s, gmlp_b_s=m_gmlp_b_s, gmlp_w_out=m_gmlp_w_out, ffn2_norm=m_ffn2_norm, ffn2_w_in=m_ffn2_w_in, ffn2_w_out=m_ffn2_w_out, final_norm=m_final_norm)
    mom_v = dict(mem_norm=v_mem_norm, lb_logits=v_lb_logits, ffn1_norm=v_ffn1_norm, ffn1_w_in=v_ffn1_w_in, ffn1_w_out=v_ffn1_w_out, mix_norm=v_mix_norm, mem_w_kv=v_mem_w_kv, hgrn_w_in=v_hgrn_w_in, hgrn_gnorm=v_hgrn_gnorm, hgrn_w_out=v_hgrn_w_out, gmlp_w_in=v_gmlp_w_in, gmlp_ln_g=v_gmlp_ln_g, gmlp_ln_b=v_gmlp_ln_b, gmlp_w_s=v_gmlp_w_s, gmlp_b_s=v_gmlp_b_s, gmlp_w_out=v_gmlp_w_out, ffn2_norm=v_ffn2_norm, ffn2_w_in=v_ffn2_w_in, ffn2_w_out=v_ffn2_w_out, final_norm=v_final_norm)
    order = list(weights)
    _, _, _, me = _mesh_pos()
    me_arr = jnp.reshape(me, (1,)).astype(jnp.int32)
    cuts = {name: c for name, c, _, _ in GROUPS}
    rows_already = tuple(name for name, c, _, n in GROUPS if c and n % 128)
    as_rows = lambda a: jnp.transpose(a, (0, 2, 1))
    for name in rows_already:
        weights[name], mom_m[name], mom_v[name] = as_rows(weights[name]), as_rows(mom_m[name]), as_rows(mom_v[name])
        cuts[name] = False

    mix1 =(("mem_w_kv", 1), ("gmlp_w_in", 0), ("gmlp_w_out", 0))
    gather_plan = (
        ((0, "ffn1_in"), _stage_pieces(0, "ffn1")),
        ((0, "mix_in"), _stage_pieces(0, "mix")),
        ((0, "ffn2_in"), _stage_pieces(0, "ffn2")),
        ((1, "ffn1_in"), _stage_pieces(1, "ffn1")),
        ((1, "mix_in"), mix1),
        ((1, "ffn2_in"), _stage_pieces(1, "ffn2")),
    )
    stage_of = {use: k for k, (use, _) in enumerate(gather_plan)}
    in_flight = {}

    def place(k, deps=()):
        pieces = gather_plan[k][1]
        lands = [_place_rows(weights[name], l, cuts[name], me_arr, deps=deps, name=f"place_{name}_{l}")
                 for name, l in pieces]
        if pieces is mix1:
            lands.append(_place_ln(gmlp_ln_g, gmlp_ln_b, me_arr))
        return lands

    placed = {0: place(0)}

    def start_chips(k, deps):
        lands = placed[k]
        send_sems, recv_sems, *thru, token = _copies_start(lands, lands, mode="gather_chips", deps=deps,
                                                           name=f"gather{k}_chips_start")
        in_flight[k] = (thru, send_sems, recv_sems)
        return token

    def pass_to_sibling(k, after):
        thru, send_sems, recv_sems = in_flight[k]
        outs = _copies_wait(thru, send_sems, recv_sems, after, n_lands=len(thru), mode="gather_chips",
                            name=f"gather{k}_chips_wait")
        send_sems, recv_sems, *thru, token = _copies_start(outs, outs, mode="gather_sibling",
                                                           name=f"gather{k}_sibling_start")
        in_flight[k] = (thru, send_sems, recv_sems)
        return token, token

    first_sent = start_chips(0, ())
    placed.update({k: place(k, (first_sent,)) for k in range(1, len(gather_plan))})
    placed_later = tuple(a for k in range(1, len(gather_plan)) for a in placed[k])
    points = [(i, p) for i in (0, 1) for p in ("ffn1_in", "ffn1_out", "mix_in", "mix_out", "ffn2_in", "ffn2_out")]
    pass_at = {j: points[points.index(use) - 1] for j, (use, _) in enumerate(gather_plan) if j}
    pass_at[1] = gather_plan[1][0]

    started = {0}

    def get_weights(use, after):
        tokens, w = [], {}
        k = stage_of.get(use)

        def pass_on(j, after):
            token, landed = pass_to_sibling(j, after)
            tokens.append(token)
            if j + 1 < len(gather_plan) and j + 1 not in started:
                started.add(j + 1)
                tokens.append(start_chips(j + 1, (landed,)))

        if k == 0:
            pass_on(0, tuple(after) + placed_later)
        elif k is not None and pass_at[k] == use:
            pass_on(k, after)
        if k is not None:
            thru, send_sems, recv_sems = in_flight[k]
            outs = _copies_wait(thru, send_sems, recv_sems, after, n_lands=len(thru), mode="gather_sibling",
                                name=f"gather{k}_sibling_wait")
            after = (outs[0],)
            pieces = gather_plan[k][1]
            w = {p: o.reshape(N_DEV * o.shape[1], D_MODEL) for p, o in zip(pieces, outs)}
            if pieces is mix1:
                w["ln_g"] = outs[-1][:, 0, :].reshape(1, GM_WIDTH)
                w["ln_b"] = outs[-1][:, 1, :].reshape(1, GM_WIDTH)
        for j, at in pass_at.items():
            if at == use and j != k:
                pass_on(j, after)
        w["deps"] = tuple(tokens)
        return w

    scatter = {}

    def put_grads(st, grads):
        if st in ("w_s", "small"):
            slab = grads.reshape(GM_GROUPS * GM_CHUNK, GM_CHUNK) if st == "w_s" else _pack_slab(grads, name="pack_small_grads")
            land = _place_slab(slab, me_arr, name=f"{st}_place")
            send_sems, recv_sems, *thru, token = _copies_start([land], [land], mode="gather_all", name=f"{st}_start")
            scatter[st] = (thru, send_sems, recv_sems)
            return (token,)
        views = [g.reshape(N_DEV, -1, D_MODEL) for g in grads.values()]
        recv = _place_own(views, me_arr, name=f"scatter_place_l{st[0]}_{st[1]}")
        send_sems, recv_sems, *thru, token = _copies_start(views, recv, mode="scatter",
                                                           name=f"scatter_start_l{st[0]}_{st[1]}")
        scatter[st] = (tuple(grads), thru, send_sems, recv_sems)
        return (token,)

    dx, last_sent = _step_local(
        x, mem, loss_target, get_weights, put_grads, mem_norm, lb_logits, ffn1_norm, mix_norm, hgrn_gnorm,
        gmlp_w_s, gmlp_b_s, ffn2_norm, final_norm)

    slots = {}

    def wait_grads(blk, after, last=False):
        for st, entry in scatter.items():
            if isinstance(st, tuple) and st[1].startswith(blk) and (st == (0, "ffn1_in")) == last:
                pieces, thru, send_sems, recv_sems = entry
                outs = _copies_wait(thru, send_sems, recv_sems, after, n_lands=len(thru) // 2, mode="scatter",
                                    name=f"scatter_wait_l{st[0]}_{st[1]}")
                slots.update(zip(pieces, outs))

    grad, delta, new_m, new_v = {}, {}, {}, {}

    def adam_groups(names):
        for name in names:
            layers = GROUP_LAYERS[name]
            grad[name], delta[name], new_m[name], new_v[name] = _adam_big(
                [slots[(name, l)] for l in range(layers)], weights[name], mom_m[name], mom_v[name], cuts[name],
                name=f"{name}_adamw")

    wait_grads("ffn2", (dx, *last_sent))
    adam_groups(("ffn2_w_in", "ffn2_w_out"))
    wait_grads("mix", (delta["ffn2_w_out"],))
    adam_groups(("mem_w_kv", "gmlp_w_in", "gmlp_w_out", "hgrn_w_in", "hgrn_w_out"))
    wait_grads("ffn1", (delta["hgrn_w_out"],))
    adam_groups(("ffn1_w_out",))

    def small_parts(src):
        parts = {n: [src[n].reshape(-1, src[n].shape[-1])] for n in SLAB_AT if n not in SMALL_SHARDED}
        return parts

    w_s_rows = lambda a: a.reshape(GM_GROUPS * GM_CHUNK, GM_CHUNK)
    small_done = (delta["hgrn_w_out"],)
    (slab_slots,) = _copies_wait(*scatter["small"], small_done, n_lands=1, mode="gather_all", name="small_wait")
    (ws_slots,) = _copies_wait(*scatter["w_s"], small_done, n_lands=1, mode="gather_all", name="w_s_wait")
    (g_slab, d_slab, nm_slab, nv_slab), (g_ws, d_ws, nm_ws, nv_ws) = _adam_slabs(
        [slab_slots, ws_slots],
        [_pack_slab(small_parts(weights), deps=(dx,), name="pack_small_w"), w_s_rows(gmlp_w_s)],
        [_pack_slab(small_parts(mom_m), deps=(dx,), name="pack_small_m"), w_s_rows(m_gmlp_w_s)],
        [_pack_slab(small_parts(mom_v), deps=(dx,), name="pack_small_v"), w_s_rows(v_gmlp_w_s)])
    shapes = {n: weights[n].shape for n in SLAB_AT}
    for out, slab, ws in ((grad, g_slab, g_ws), (delta, d_slab, d_ws), (new_m, nm_slab, nm_ws), (new_v, nv_slab, nv_ws)):
        out.update(_unpack_slab(slab, shapes))
        out["gmlp_w_s"] = ws.reshape(gmlp_w_s.shape)
    blk = GM_WIDTH // N_DEV
    g_ln = [lax.dynamic_slice(g_slab[SLAB_AT[n]:SLAB_AT[n] + 2].reshape(1, GM_WIDTH), (0, me * blk), (1, blk))
            for n in SMALL_SHARDED]
    ln_out = _adam_vecs(g_ln, [weights[n] for n in SMALL_SHARDED], [mom_m[n] for n in SMALL_SHARDED],
                        [mom_v[n] for n in SMALL_SHARDED])
    for n, g, (d, nm, nv) in zip(SMALL_SHARDED, g_ln, ln_out):
        grad[n], delta[n], new_m[n], new_v[n] = g, d, nm, nv

    wait_grads("ffn1", tuple(delta[n] for n in delta if n in GROUP_LAYERS) + (d_slab,), last=True)
    adam_groups(("ffn1_w_in",))

    for name in rows_already:
        for out in (grad, delta, new_m, new_v):
            out[name] = as_rows(out[name])
    loss = g_slab[LOSS_ROW, 0]
    grad_x = dx.reshape(B_LOC, SEQ, D_MODEL)
    return (loss, grad_x, *[grad[n] for n in order], *[delta[n] for n in order],
            *[new_m[n] for n in order], *[new_v[n] for n in order])


def _step_local(x, mem, loss_target, get_weights, put_grads, mem_norm, lb_logits, ffn1_norm, mix_norm, hgrn_gnorm,
                gmlp_w_s, gmlp_b_s, ffn2_norm, final_norm):
    w_s = gmlp_w_s[0]
    b_st = gmlp_b_s[0].T

    xs = x.reshape(N_TOK, D_MODEL)
    mem2d = mem.reshape(B_LOC * MEM_LEN, D_MODEL)
    mem_g = mem_norm.reshape(1, D_MODEL)
    saved, full = [], {}
    for i in range(2):
        xs, s_ffn1 = _ffn_fwd(xs, ffn1_norm[i:i + 1], "ffn1", i, full, get_weights)
        if i == 0:
            memn = _rms_fwd(mem2d, mem_g, deps=(xs,), name="mem_norm_fwd")
        _take_weights(full, get_weights((i, "mix_in"), (xs,)))
        mixer = "hgrn" if i == 0 else "gmlp"
        hm, zm = _norm_mm(xs, mix_norm[i:i + 1], full[(f"{mixer}_w_in", 0)], swiglu=False, tm=1024, tn=1280, deps=full.pop("deps", ()),
                          name=f"l{i}_mix_in")
        kv = _mm(memn, full[("mem_w_kv", i)], tb=True, tm=512, tn=512, tk=D_MODEL, out_dtype=F32, name=f"l{i}_mem_kv")
        o_mem = _attn_fwd(zm, kv, name=f"l{i}_attn")
        if i == 0:
            cat, o_pre, s_all = _hgrn_fwd(zm, o_mem, lb_logits, hgrn_gnorm)
            mix_saved = (o_pre, s_all)
        else:
            cat = _gmlp_fwd(zm, o_mem, full["ln_g"], full["ln_b"], w_s, b_st)
            mix_saved = ()
        x_mix = xs
        _take_weights(full, get_weights((i, "mix_out"), (cat,)))
        xs = _mm(cat, full[(f"{mixer}_w_out", 0)], tm=512, tn=D_MODEL, tk=cat.shape[1], out_dtype=F32, res=xs,
                 deps=full.pop("deps", ()), name=f"l{i}_mix_out")
        xs, s_ffn2 = _ffn_fwd(xs, ffn2_norm[i:i + 1], "ffn2", i, full, get_weights)
        saved.append((s_ffn1, (x_mix, hm, kv, zm, cat, mix_saved), s_ffn2))

    dx, dx16, d_final, loss_part = _loss_head(xs, final_norm.reshape(1, D_MODEL), loss_target.reshape(N_TOK, D_MODEL))

    small = {"final_norm": [d_final], "loss": [loss_part]}
    d_ffn1, d_ffn2, d_mix = [None, None], [None, None], [None, None]
    dmemn = jnp.zeros((B_LOC * MEM_LEN, D_MODEL), F32)
    deps = ()
    for i in (1, 0):
        s_ffn1, (x_mix, hm, kv, zm, cat, mix_saved), s_ffn2 = saved[i]
        dx, dx16, d_ffn2[i], dw_in_t, dw_out = _ffn_bwd(
            dx, dx16, s_ffn2, ffn2_norm[i:i + 1], full[("ffn2_w_in", i)], full[("ffn2_w_out", i)], f"l{i}_ffn2", deps)
        deps = put_grads((i, "ffn2"), {("ffn2_w_in", i): dw_in_t, ("ffn2_w_out", i): dw_out})
        mixer = "hgrn" if i == 0 else "gmlp"
        w_in_t, w_out = full[(f"{mixer}_w_in", 0)], full[(f"{mixer}_w_out", 0)]
        width = cat.shape[1]
        g_mix = {}
        g_mix[(f"{mixer}_w_out", 0)] = _mm(cat, dx16, ta=True, tm=1024, tn=D_MODEL, tk=N_TOK, out_dtype=BF16,
                                           deps=deps, name=f"l{i}_mix_out_wgrad")
        dcat = _mm(dx16, w_out, tb=True, tm=1024, tn=width // 2, tk=D_MODEL, out_dtype=F32, name=f"l{i}_mix_out_dgrad")
        dq, dk, dv = _attn_bwd(zm, kv, dcat, do_off=width - XA_HEADS * XA_DIM, name=f"l{i}_attn_bwd")
        if i == 0:
            dzm, dlbl, dgn = _hgrn_bwd(zm, mix_saved[0], dcat, dq, mix_saved[1], lb_logits, hgrn_gnorm)
            small["lb_logits"], small["hgrn_gnorm"] = [dlbl], [dgn]
            deps = ()
        else:
            dzm, dws, dbt, dlng, dlnb = _gmlp_bwd(zm, dcat, dq, full["ln_g"], full["ln_b"], w_s, b_st)
            small["gmlp_b_s"], small["gmlp_ln_g"], small["gmlp_ln_b"] = [dbt.T], [dlng], [dlnb]
            deps = put_grads("w_s", dws)
        g_mix[(f"{mixer}_w_in", 0)] = _mm(dzm, hm, ta=True, tm=1024, tn=D_MODEL, tk=N_TOK, out_dtype=BF16, deps=deps,
                                          name=f"l{i}_mix_in_wgrad")
        dkv = jnp.concatenate([dk, dv], axis=1)
        g_mix[("mem_w_kv", i)] = _mm(dkv, memn, ta=True, tm=512, tn=D_MODEL, tk=B_LOC * MEM_LEN, out_dtype=BF16,
                                     name=f"l{i}_mem_kv_wgrad")
        deps = put_grads((i, "mix"), g_mix)
        dx, dx16, d_mix[i] = _dgrad_norm_bwd(dzm, w_in_t, x_mix, mix_norm[i:i + 1], dx, deps=deps,
                                             name=f"l{i}_mix_in_dgrad")
        dmemn = _mm(dkv, full[("mem_w_kv", i)], tm=B_LOC * MEM_LEN, tn=D_MODEL, tk=512, out_dtype=F32, res=dmemn,
                    name=f"l{i}_mem_kv_dgrad")
        def send_small(dg, i=i, dmemn=dmemn):
            d_ffn1[i] = dg
            _, _, dmem_g = _rms_bwd(mem2d, mem_g, dmemn, dmemn, name="mem_norm_bwd")
            small.update(mem_norm=[dmem_g], ffn1_norm=d_ffn1, ffn2_norm=d_ffn2, mix_norm=d_mix)
            return put_grads("small", small)

        if i == 0:
            send_out = lambda dw_out: put_grads((0, "ffn1_out"), {("ffn1_w_out", 0): dw_out})
            dx, dx16, d_ffn1[i], dw_in_t, _ = _ffn_bwd(
                dx, dx16, s_ffn1, ffn1_norm[i:i + 1], full[("ffn1_w_in", i)], full[("ffn1_w_out", i)], f"l{i}_ffn1",
                after_out_wgrad=send_out, before_in_wgrad=send_small)
            deps = put_grads((0, "ffn1_in"), {("ffn1_w_in", 0): dw_in_t})
        else:
            dx, dx16, d_ffn1[i], dw_in_t, dw_out = _ffn_bwd(
                dx, dx16, s_ffn1, ffn1_norm[i:i + 1], full[("ffn1_w_in", i)], full[("ffn1_w_out", i)], f"l{i}_ffn1")
            deps = put_grads((i, "ffn1"), {("ffn1_w_in", i): dw_in_t, ("ffn1_w_out", i): dw_out})
    return dx, deps
```

```python
import functools
import math

import jax
import jax.numpy as jnp
from jax import lax
from jax.experimental import pallas as pl
from jax.experimental.pallas import tpu as pltpu

F32 = jnp.float32
BF16 = jnp.bfloat16

D_MODEL = 1024
SEQ = 2048
B_LOC = 2
N_TOK = B_LOC * SEQ
MEM_LEN = 256
N_DEV = 8
EPS = 1e-6
D_FF = 2816
HG_HEADS = 8
HG_DIM = 128
HG_CHUNK = 64
HG_NCHUNK = SEQ // HG_CHUNK
GM_CHUNK = 128
GM_GROUPS = 8
GM_WIDTH = 2048
GM_GDIM = GM_WIDTH // GM_GROUPS
XA_HEADS = 4
XA_DIM = 256
XA_OFF = 4096

ADAM_LR = 0.001
ADAM_B1 = 0.9
ADAM_B2 = 0.999
ADAM_EPS = 1e-08
ADAM_WD = 0.01
ADAM_STEP = 10

VMEM_LIMIT_BYTES = 56 * 1024 * 1024
MESH_AXES = ("x", "y", "c")

GROUPS = (
    ("ffn1_w_in", True, 2, 704),
    ("ffn1_w_out", False, 2, 352),
    ("mem_w_kv", True, 2, 256),
    ("hgrn_w_in", True, 1, 640),
    ("hgrn_w_out", False, 1, 256),
    ("gmlp_w_in", True, 1, 640),
    ("gmlp_w_out", False, 1, 384),
    ("ffn2_w_in", True, 2, 704),
    ("ffn2_w_out", False, 2, 352),
)
GROUP_LAYERS = {name: layers for name, _, layers, _ in GROUPS}


def _stage_pieces(layer, block):
    if block == "mix":
        mixer = "hgrn" if layer == 0 else "gmlp"
        return (("mem_w_kv", layer), (f"{mixer}_w_in", 0), (f"{mixer}_w_out", 0))
    return ((f"{block}_w_in", layer), (f"{block}_w_out", layer))


ANY_SPEC = pl.BlockSpec(memory_space=pl.ANY)
HBM_SPEC = pl.BlockSpec(memory_space=pltpu.HBM)
SEM_SPEC = pl.BlockSpec(memory_space=pltpu.SEMAPHORE)


def _cp(*sem):
    return pltpu.CompilerParams(dimension_semantics=sem, vmem_limit_bytes=VMEM_LIMIT_BYTES)


def _sigmoid(x):
    return 0.5 * jnp.tanh(0.5 * x) + 0.5


def _gelu_parts(x):
    cdf = 0.5 * (1.0 + lax.erf(x * (1.0 / math.sqrt(2.0))))
    pdf = jnp.exp(-0.5 * x * x) * (1.0 / math.sqrt(2.0 * math.pi))
    return x * cdf, cdf + x * pdf


def _mm(a, b, *, ta=False, tb=False, tm, tn, tk, out_dtype, res=None, scale=1.0, deps=(), name):
    m, k = (a.shape[1], a.shape[0]) if ta else a.shape
    n, kb = b.shape if tb else (b.shape[1], b.shape[0])
    assert k == kb and m % tm == 0 and n % tn == 0 and k % tk == 0, (name, a.shape, b.shape)
    nk = k // tk
    dn = (((0 if ta else 1,), (1 if tb else 0,)), ((), ()))
    n_in = 2 + (res is not None) + len(deps)

    def body(*refs):
        a_ref, b_ref = refs[:2]
        r_ref = refs[2] if res is not None else None
        o_ref, scr = refs[n_in], refs[n_in + 1:]
        p = lax.dot_general(a_ref[...].astype(BF16), b_ref[...].astype(BF16), dn, preferred_element_type=F32)

        def finish(acc):
            if scale != 1.0:
                acc = scale * acc
            if r_ref is not None:
                acc = r_ref[...] + acc
            o_ref[...] = acc.astype(out_dtype)

        if nk == 1:
            finish(p)
        else:
            acc_ref = scr[0]
            kk = pl.program_id(2)

            @pl.when(kk == 0)
            def _():
                acc_ref[...] = p

            @pl.when(kk > 0)
            def _():
                acc_ref[...] += p

            @pl.when(kk == nk - 1)
            def _():
                finish(acc_ref[...])

    a_spec = pl.BlockSpec((tk, tm), lambda i, j, kk: (kk, i)) if ta else pl.BlockSpec((tm, tk), lambda i, j, kk: (i, kk))
    b_mode = dict(pipeline_mode=pl.Buffered(1)) if n == tn and nk == 1 else {}
    if tb:
        b_spec = pl.BlockSpec((tn, tk), lambda i, j, kk: (j, kk), **b_mode)
    else:
        b_spec = pl.BlockSpec((tk, tn), lambda i, j, kk: (kk, j), **b_mode)
    o_spec = pl.BlockSpec((tm, tn), lambda i, j, kk: (i, j))
    in_specs = [a_spec, b_spec] + ([o_spec] if res is not None else []) + [ANY_SPEC] * len(deps)
    args = (a, b) + ((res,) if res is not None else ()) + tuple(deps)
    return pl.pallas_call(
        body,
        name=name,
        grid=(m // tm, n // tn, nk),
        in_specs=in_specs,
        out_specs=o_spec,
        out_shape=jax.ShapeDtypeStruct((m, n), out_dtype),
        scratch_shapes=[pltpu.VMEM((tm, tn), F32)] if nk > 1 else [],
        compiler_params=_cp("parallel", "parallel", "arbitrary"),
    )(*args)


def _rms_fwd(x, g, *, name, deps=(), tm=512):
    rows = x.shape[0]

    def body(x_ref, g_ref, *rest):
        o_ref = rest[len(deps)]
        xv = x_ref[...]
        r = lax.rsqrt(jnp.mean(xv * xv, axis=-1, keepdims=True) + EPS)
        o_ref[...] = (xv * r * g_ref[...]).astype(BF16)

    row = pl.BlockSpec((tm, D_MODEL), lambda i: (i, 0))
    return pl.pallas_call(
        body,
        name=name,
        grid=(rows // tm,),
        in_specs=[row, pl.BlockSpec((1, D_MODEL), lambda i: (0, 0))] + [ANY_SPEC] * len(deps),
        out_specs=row,
        out_shape=jax.ShapeDtypeStruct((rows, D_MODEL), BF16),
        compiler_params=_cp("parallel"),
    )(x, g, *deps)


def _rms_bwd(x, g, dh, dres, *, name, deps=(), tm=512):
    rows = x.shape[0]

    def body(x_ref, g_ref, dh_ref, dres_ref, *rest):
        dx_ref, dx16_ref, dg_ref = rest[len(deps):]
        xv = x_ref[...]
        r = lax.rsqrt(jnp.mean(xv * xv, axis=-1, keepdims=True) + EPS)
        xhat = xv * r
        dhv = dh_ref[...]
        part = jnp.sum(dhv * xhat, axis=0, keepdims=True)

        @pl.when(pl.program_id(0) == 0)
        def _():
            dg_ref[...] = part

        @pl.when(pl.program_id(0) > 0)
        def _():
            dg_ref[...] += part

        dxh = dhv * g_ref[...]
        dx = dres_ref[...] + r * (dxh - xhat * jnp.mean(dxh * xhat, axis=-1, keepdims=True))
        dx_ref[...] = dx
        dx16_ref[...] = dx.astype(BF16)

    row = pl.BlockSpec((tm, D_MODEL), lambda i: (i, 0))
    vec = pl.BlockSpec((1, D_MODEL), lambda i: (0, 0))
    return pl.pallas_call(
        body,
        name=name,
        grid=(rows // tm,),
        in_specs=[row, vec, row, row] + [ANY_SPEC] * len(deps),
        out_specs=[row, row, vec],
        out_shape=[jax.ShapeDtypeStruct((rows, D_MODEL), F32), jax.ShapeDtypeStruct((rows, D_MODEL), BF16),
                   jax.ShapeDtypeStruct((1, D_MODEL), F32)],
        compiler_params=_cp("arbitrary"),
    )(x, g, dh, dres, *deps)


_NT = (((1,), (1,)), ((), ()))
_TN = (((0,), (0,)), ((), ()))


def _norm_mm(x, g, w_t, *, swiglu, name, tm, tn, deps=()):
    rows = w_t.shape[0]
    half = rows // 2
    nj = (half if swiglu else rows) // tn
    nd = len(deps)

    def body(x_ref, g_ref, w_ref, *rest):
        outs = rest[nd:]
        h_ref, z_ref = outs[:2]

        def norm():
            xv = x_ref[...]
            r = lax.rsqrt(jnp.mean(xv * xv, axis=-1, keepdims=True) + EPS)
            h_ref[...] = (xv * r * g_ref[...]).astype(BF16)

        if swiglu:
            norm()
            h = h_ref[...]
            for j in range(nj):
                cols = slice(j * tn, (j + 1) * tn)
                gate = lax.dot_general(h, w_ref[j * tn:(j + 1) * tn, :], _NT, preferred_element_type=F32)
                up = lax.dot_general(h, w_ref[half + j * tn:half + (j + 1) * tn, :], _NT, preferred_element_type=F32)
                s = _sigmoid(gate)
                silu = gate * s
                z_ref[0, :, cols] = (up * (s + silu * (1.0 - s))).astype(BF16)
                z_ref[1, :, cols] = silu.astype(BF16)
                outs[2][:, cols] = (silu * up).astype(BF16)
        else:
            j = pl.program_id(1)
            pl.when(j == 0)(norm)
            w = w_ref[pl.ds(pl.multiple_of(j * tn, tn), tn), :]
            z_ref[...] = lax.dot_general(h_ref[...], w, _NT, preferred_element_type=F32)

    grid = (N_TOK // tm,) if swiglu else (N_TOK // tm, nj)
    row = pl.BlockSpec((tm, D_MODEL), lambda i, *_: (i, 0))
    out_specs = [row]
    out_shape = [jax.ShapeDtypeStruct((N_TOK, D_MODEL), BF16)]
    if swiglu:
        out_specs += [pl.BlockSpec((2, tm, half), lambda i: (0, i, 0)), pl.BlockSpec((tm, half), lambda i: (i, 0))]
        out_shape += [jax.ShapeDtypeStruct((2, N_TOK, half), BF16), jax.ShapeDtypeStruct((N_TOK, half), BF16)]
    else:
        out_specs.append(pl.BlockSpec((tm, tn), lambda i, j: (i, j)))
        out_shape.append(jax.ShapeDtypeStruct((N_TOK, rows), F32))
    return pl.pallas_call(
        body,
        name=name,
        grid=grid,
        in_specs=[row, pl.BlockSpec((1, D_MODEL), lambda *_: (0, 0)),
                  pl.BlockSpec((rows, D_MODEL), lambda *_: (0, 0), pipeline_mode=pl.Buffered(1))] + [ANY_SPEC] * nd,
        out_specs=out_specs,
        out_shape=out_shape,
        compiler_params=_cp(*(("parallel",) if swiglu else ("parallel", "arbitrary"))),
    )(x, g, w_t, *deps)


def _ffn_forward(x, g, w_in_t, w_out, *, scale, name, deps=(), tm=256, tn=1408):
    nd = len(deps)

    def body(x_ref, g_ref, wi_ref, wo_ref, *rest):
        y_ref, h_ref, z_ref, act_ref = rest[nd:]
        xv = x_ref[...]
        r = lax.rsqrt(jnp.mean(xv * xv, axis=-1, keepdims=True) + EPS)
        h = (xv * r * g_ref[...]).astype(BF16)
        h_ref[...] = h
        for j in range(D_FF // tn):
            cols = slice(j * tn, (j + 1) * tn)
            gate = lax.dot_general(h, wi_ref[j * tn:(j + 1) * tn, :], _NT, preferred_element_type=F32)
            up = lax.dot_general(h, wi_ref[D_FF + j * tn:D_FF + (j + 1) * tn, :], _NT, preferred_element_type=F32)
            s = _sigmoid(gate)
            silu = gate * s
            z_ref[0, :, cols] = (up * (s + silu * (1.0 - s))).astype(BF16)
            z_ref[1, :, cols] = silu.astype(BF16)
            act_ref[:, cols] = (silu * up).astype(BF16)
        y_ref[...] = xv + scale * jnp.dot(act_ref[...], wo_ref[...], preferred_element_type=F32)

    row = pl.BlockSpec((tm, D_MODEL), lambda i: (i, 0))
    whole = lambda rows: pl.BlockSpec((rows, D_MODEL), lambda i: (0, 0), pipeline_mode=pl.Buffered(1))
    return pl.pallas_call(
        body,
        name=name,
        grid=(N_TOK // tm,),
        in_specs=[row, pl.BlockSpec((1, D_MODEL), lambda i: (0, 0)), whole(2 * D_FF), whole(D_FF)] + [ANY_SPEC] * nd,
        out_specs=[row, row, pl.BlockSpec((2, tm, D_FF), lambda i: (0, i, 0)), pl.BlockSpec((tm, D_FF), lambda i: (i, 0))],
        out_shape=[jax.ShapeDtypeStruct((N_TOK, D_MODEL), F32), jax.ShapeDtypeStruct((N_TOK, D_MODEL), BF16),
                   jax.ShapeDtypeStruct((2, N_TOK, D_FF), BF16), jax.ShapeDtypeStruct((N_TOK, D_FF), BF16)],
        compiler_params=_cp("parallel"),
    )(x, g, w_in_t, w_out, *deps)


def _swiglu_dgrad(dy16, w_out, z, *, scale, name, deps=(), tm=512, tn=1408):
    def body(dy_ref, w_ref, z_ref, *rest):
        dz_ref = rest[len(deps)]
        dy = dy_ref[...]
        for j in range(D_FF // tn):
            cols = slice(j * tn, (j + 1) * tn)
            da = lax.dot_general(dy, w_ref[cols, :], _NT, preferred_element_type=F32) * scale
            dz_ref[0, :, cols] = (da * z_ref[0, :, cols].astype(F32)).astype(BF16)
            dz_ref[1, :, cols] = (da * z_ref[1, :, cols].astype(F32)).astype(BF16)

    planes = pl.BlockSpec((2, tm, D_FF), lambda i: (0, i, 0))
    return pl.pallas_call(
        body,
        name=name,
        grid=(N_TOK // tm,),
        in_specs=[pl.BlockSpec((tm, D_MODEL), lambda i: (i, 0)),
                  pl.BlockSpec((D_FF, D_MODEL), lambda i: (0, 0), pipeline_mode=pl.Buffered(1)), planes]
        + [ANY_SPEC] * len(deps),
        out_specs=planes,
        out_shape=jax.ShapeDtypeStruct((2, N_TOK, D_FF), BF16),
        compiler_params=_cp("parallel"),
    )(dy16, w_out, z, *deps)


def _ffn_dgrad(dy16, dres, w_out, z, w_in_t, x, g, *, scale, name, deps=(), tm=256, tn=1408):
    nd = len(deps)

    def body(dy_ref, dres_ref, wo_ref, z_ref, wi_ref, x_ref, g_ref, *rest):
        dz_ref, dx_ref, dx16_ref, dg_ref = rest[nd:]
        dy = dy_ref[...]
        for j in range(D_FF // tn):
            cols = slice(j * tn, (j + 1) * tn)
            da = lax.dot_general(dy, wo_ref[cols, :], _NT, preferred_element_type=F32) * scale
            dz_ref[0, :, cols] = (da * z_ref[0, :, cols].astype(F32)).astype(BF16)
            dz_ref[1, :, cols] = (da * z_ref[1, :, cols].astype(F32)).astype(BF16)
        dh = jnp.dot(dz_ref[0], wi_ref[:D_FF, :], preferred_element_type=F32) + jnp.dot(
            dz_ref[1], wi_ref[D_FF:, :], preferred_element_type=F32)
        xv = x_ref[...]
        r = lax.rsqrt(jnp.mean(xv * xv, axis=-1, keepdims=True) + EPS)
        xhat = xv * r
        part = jnp.sum(dh * xhat, axis=0, keepdims=True)

        @pl.when(pl.program_id(0) == 0)
        def _():
            dg_ref[...] = part

        @pl.when(pl.program_id(0) > 0)
        def _():
            dg_ref[...] += part

        dxh = dh * g_ref[...]
        dx = dres_ref[...] + r * (dxh - xhat * jnp.mean(dxh * xhat, axis=-1, keepdims=True))
        dx_ref[...] = dx
        dx16_ref[...] = dx.astype(BF16)

    row = pl.BlockSpec((tm, D_MODEL), lambda i: (i, 0))
    vec = pl.BlockSpec((1, D_MODEL), lambda i: (0, 0))
    planes = pl.BlockSpec((2, tm, D_FF), lambda i: (0, i, 0))
    whole = lambda rows: pl.BlockSpec((rows, D_MODEL), lambda i: (0, 0), pipeline_mode=pl.Buffered(1))
    return pl.pallas_call(
        body,
        name=name,
        grid=(N_TOK // tm,),
        in_specs=[row, row, whole(D_FF), planes, whole(2 * D_FF), row, vec] + [ANY_SPEC] * nd,
        out_specs=[planes, row, row, vec],
        out_shape=[jax.ShapeDtypeStruct((2, N_TOK, D_FF), BF16), jax.ShapeDtypeStruct((N_TOK, D_MODEL), F32),
                   jax.ShapeDtypeStruct((N_TOK, D_MODEL), BF16), jax.ShapeDtypeStruct((1, D_MODEL), F32)],
        compiler_params=_cp("arbitrary"),
    )(dy16, dres, w_out, z, w_in_t, x, g, *deps)


def _planes_wgrad(dz, h, *, name, deps=(), tm=1408):
    per_plane = D_FF // tm

    def body(a_ref, b_ref, *rest):
        o_ref = rest[len(deps)]
        o_ref[...] = lax.dot_general(a_ref[...], b_ref[...], _TN, preferred_element_type=F32).astype(BF16)

    return pl.pallas_call(
        body,
        name=name,
        grid=(2 * per_plane,),
        in_specs=[pl.BlockSpec((None, N_TOK, tm),
                               lambda i: (jnp.where(i < per_plane, 0, 1), 0, jnp.where(i < per_plane, i, i - per_plane))),
                  pl.BlockSpec((N_TOK, D_MODEL), lambda i: (0, 0), pipeline_mode=pl.Buffered(1))] + [ANY_SPEC] * len(deps),
        out_specs=pl.BlockSpec((tm, D_MODEL), lambda i: (i, 0)),
        out_shape=jax.ShapeDtypeStruct((2 * D_FF, D_MODEL), BF16),
        compiler_params=_cp("parallel"),
    )(dz, h, *deps)


def _dgrad_norm_bwd(dz, w_t, x, g, dres, *, name, deps=(), tm=512):
    planes = dz.ndim == 3
    rows = w_t.shape[0]
    half = rows // 2
    nd = len(deps)

    def body(a_ref, b_ref, x_ref, g_ref, dres_ref, *rest):
        dx_ref, dx16_ref, dg_ref = rest[nd:]
        if planes:
            dh = jnp.dot(a_ref[0], b_ref[:half, :], preferred_element_type=F32) + jnp.dot(
                a_ref[1], b_ref[half:, :], preferred_element_type=F32)
        else:
            dh = jnp.dot(a_ref[...], b_ref[...], preferred_element_type=F32)
        xv = x_ref[...]
        r = lax.rsqrt(jnp.mean(xv * xv, axis=-1, keepdims=True) + EPS)
        xhat = xv * r
        part = jnp.sum(dh * xhat, axis=0, keepdims=True)

        @pl.when(pl.program_id(0) == 0)
        def _():
            dg_ref[...] = part

        @pl.when(pl.program_id(0) > 0)
        def _():
            dg_ref[...] += part

        dxh = dh * g_ref[...]
        dx = dres_ref[...] + r * (dxh - xhat * jnp.mean(dxh * xhat, axis=-1, keepdims=True))
        dx_ref[...] = dx
        dx16_ref[...] = dx.astype(BF16)

    a_spec = pl.BlockSpec((2, tm, half), lambda i: (0, i, 0)) if planes else pl.BlockSpec((tm, rows), lambda i: (i, 0))
    row = pl.BlockSpec((tm, D_MODEL), lambda i: (i, 0))
    vec = pl.BlockSpec((1, D_MODEL), lambda i: (0, 0))
    return pl.pallas_call(
        body,
        name=name,
        grid=(N_TOK // tm,),
        in_specs=[a_spec, pl.BlockSpec((rows, D_MODEL), lambda i: (0, 0), pipeline_mode=pl.Buffered(1)), row, vec, row]
        + [ANY_SPEC] * nd,
        out_specs=[row, row, vec],
        out_shape=[jax.ShapeDtypeStruct((N_TOK, D_MODEL), F32), jax.ShapeDtypeStruct((N_TOK, D_MODEL), BF16),
                   jax.ShapeDtypeStruct((1, D_MODEL), F32)],
        compiler_params=_cp("arbitrary"),
    )(dz, w_t, x, g, dres, *deps)


def _loss_head(x, g, target, *, tm=512):
    def body(x_ref, g_ref, t_ref, dx_ref, dx16_ref, dg_ref, loss_ref):
        xv = x_ref[...]
        gv = g_ref[...]
        r = lax.rsqrt(jnp.mean(xv * xv, axis=-1, keepdims=True) + EPS)
        xhat = xv * r
        err = xhat * gv - t_ref[...]
        loss_part = jnp.zeros((1, 128), F32) + 0.5 * jnp.sum(jnp.mean(err * err, axis=-1, keepdims=True))
        dy = err * (1.0 / D_MODEL)
        dg_part = jnp.sum(dy * xhat, axis=0, keepdims=True)

        @pl.when(pl.program_id(0) == 0)
        def _():
            dg_ref[...] = dg_part
            loss_ref[...] = loss_part

        @pl.when(pl.program_id(0) > 0)
        def _():
            dg_ref[...] += dg_part
            loss_ref[...] += loss_part

        dxh = dy * gv
        dx = r * (dxh - xhat * jnp.mean(dxh * xhat, axis=-1, keepdims=True))
        dx_ref[...] = dx
        dx16_ref[...] = dx.astype(BF16)

    row = pl.BlockSpec((tm, D_MODEL), lambda i: (i, 0))
    vec = pl.BlockSpec((1, D_MODEL), lambda i: (0, 0))
    return pl.pallas_call(
        body,
        name="loss_head",
        grid=(N_TOK // tm,),
        in_specs=[row, vec, row],
        out_specs=[row, row, vec, pl.BlockSpec((1, 128), lambda i: (0, 0))],
        out_shape=[
            jax.ShapeDtypeStruct((N_TOK, D_MODEL), F32),
            jax.ShapeDtypeStruct((N_TOK, D_MODEL), BF16),
            jax.ShapeDtypeStruct((1, D_MODEL), F32),
            jax.ShapeDtypeStruct((1, 128), F32),
        ],
        compiler_params=_cp("arbitrary"),
    )(x, g, target)


XA_TQ = 2048
XA_SCALE = XA_DIM ** -0.5


def _attn_probs(q16, k16):
    s = lax.dot_general(q16, k16, _NT, preferred_element_type=F32) * XA_SCALE
    e = jnp.exp(s - jnp.max(s, axis=-1, keepdims=True))
    return e / jnp.sum(e, axis=-1, keepdims=True)


def _attn_fwd(z, kv, *, name):
    nt = SEQ // XA_TQ

    def body(q_ref, k_ref, v_ref, o_ref):
        p = _attn_probs(q_ref[...].astype(BF16), k_ref[...].astype(BF16))
        o_ref[...] = jnp.dot(p.astype(BF16), v_ref[...].astype(BF16), preferred_element_type=F32).astype(BF16)

    return pl.pallas_call(
        body,
        name=name,
        grid=(B_LOC, XA_HEADS, nt),
        in_specs=[
            pl.BlockSpec((XA_TQ, XA_DIM), lambda b, h, t: (b * nt + t, XA_OFF // XA_DIM + h)),
            pl.BlockSpec((MEM_LEN, XA_DIM), lambda b, h, t: (b, h)),
            pl.BlockSpec((MEM_LEN, XA_DIM), lambda b, h, t: (b, XA_HEADS + h)),
        ],
        out_specs=pl.BlockSpec((XA_TQ, XA_DIM), lambda b, h, t: (b * nt + t, h)),
        out_shape=jax.ShapeDtypeStruct((N_TOK, XA_HEADS * XA_DIM), BF16),
        compiler_params=_cp("parallel", "parallel", "arbitrary"),
    )(z, kv, kv)


def _attn_bwd(z, kv, dcat, *, do_off, name):
    nt = SEQ // XA_TQ

    def body(q_ref, k_ref, v_ref, do_ref, dq_ref, dk_ref, dv_ref):
        q16 = q_ref[...].astype(BF16)
        k16 = k_ref[...].astype(BF16)
        v16 = v_ref[...].astype(BF16)
        do16 = do_ref[...].astype(BF16)
        p = _attn_probs(q16, k16)
        dv_part = lax.dot_general(p.astype(BF16), do16, _TN, preferred_element_type=F32)
        dp = lax.dot_general(do16, v16, _NT, preferred_element_type=F32)
        ds16 = (p * (dp - jnp.sum(dp * p, axis=-1, keepdims=True)) * XA_SCALE).astype(BF16)
        dq_ref[...] = jnp.dot(ds16, k16, preferred_element_type=F32).astype(BF16)
        dk_part = lax.dot_general(ds16, q16, _TN, preferred_element_type=F32)

        @pl.when(pl.program_id(2) == 0)
        def _():
            dk_ref[...] = dk_part
            dv_ref[...] = dv_part

        @pl.when(pl.program_id(2) > 0)
        def _():
            dk_ref[...] += dk_part
            dv_ref[...] += dv_part

    qspec = pl.BlockSpec((XA_TQ, XA_DIM), lambda b, h, t: (b * nt + t, XA_OFF // XA_DIM + h))
    kspec = lambda off: pl.BlockSpec((MEM_LEN, XA_DIM), lambda b, h, t: (b, off + h))
    return pl.pallas_call(
        body,
        name=name,
        grid=(B_LOC, XA_HEADS, nt),
        in_specs=[qspec, kspec(0), kspec(XA_HEADS),
                  pl.BlockSpec((XA_TQ, XA_DIM), lambda b, h, t: (b * nt + t, do_off // XA_DIM + h))],
        out_specs=[pl.BlockSpec((XA_TQ, XA_DIM), lambda b, h, t: (b * nt + t, h)), kspec(0), kspec(0)],
        out_shape=[
            jax.ShapeDtypeStruct((N_TOK, XA_HEADS * XA_DIM), BF16),
            jax.ShapeDtypeStruct((B_LOC * MEM_LEN, XA_HEADS * XA_DIM), F32),
            jax.ShapeDtypeStruct((B_LOC * MEM_LEN, XA_HEADS * XA_DIM), F32),
        ],
        compiler_params=_cp("parallel", "parallel", "arbitrary"),
    )(z, kv, kv, dcat)


def _tril(n):
    return lax.broadcasted_iota(jnp.int32, (n, n), 0) >= lax.broadcasted_iota(jnp.int32, (n, n), 1)


def _lower_bound(lbl):
    e = jnp.exp(lbl - jnp.max(lbl, axis=0, keepdims=True))
    p = e / jnp.sum(e, axis=0, keepdims=True)
    return p[0:1, :], p


def _hgrn_gates(zq, zf, lb, tril_f):
    sig = _sigmoid(zf)
    f = lb + (1.0 - lb) * sig
    kk = 1.0 - f
    sq = _sigmoid(zq)
    q = zq * sq
    b = jnp.dot(tril_f, jnp.log(f), preferred_element_type=F32, precision=lax.Precision.HIGHEST)
    bl = b[HG_CHUNK - 1:HG_CHUNK, :]
    return q, sq, sig, f, kk, b, bl


HG_TB = 512
HG_CPB = HG_TB // HG_CHUNK
HG_NT = SEQ // HG_TB
HG_WIDTH = HG_HEADS * HG_DIM


def _head(h, section=0):
    return slice(section * HG_WIDTH + h * HG_DIM, section * HG_WIDTH + (h + 1) * HG_DIM)


def _hgrn_fwd(z, o_mem, lb_logits, gnorm):
    def body(zq_ref, zf_ref, zi_ref, zg_ref, omem_ref, lbl_ref, gn_ref, o_ref, opre_ref, sall_ref, st_ref):
        lb, _ = _lower_bound(lbl_ref[...])
        gn = gn_ref[...]
        mask = _tril(HG_CHUNK)
        tril_f = mask.astype(F32)
        o_ref[:, HG_WIDTH:] = omem_ref[...]

        @pl.when(pl.program_id(1) == 0)
        def _():
            st_ref[...] = jnp.zeros_like(st_ref)

        def chunk(c, carry):
            rows = pl.ds(pl.multiple_of(c * HG_CHUNK, HG_CHUNK), HG_CHUNK)
            q, _, _, _, kk, b, bl = _hgrn_gates(zq_ref[rows, :], zf_ref[rows, :], lb, tril_f)
            v16 = zi_ref[rows, :].astype(BF16)
            qd16 = (q * jnp.exp(b)).astype(BF16)
            ki16 = (kk * jnp.exp(-b)).astype(BF16)
            kd16 = (kk * jnp.exp(bl - b)).astype(BF16)
            ebl = jnp.exp(bl)
            zg = zg_ref[rows, :]
            gate = zg * _sigmoid(zg)
            for h in range(HG_HEADS):
                sl = _head(h)
                a = jnp.where(mask, lax.dot_general(qd16[:, sl], ki16[:, sl], _NT, preferred_element_type=F32), 0.0)
                st = st_ref[h]
                sall_ref[0, h, c] = st
                o = jnp.dot(a.astype(BF16), v16[:, sl], preferred_element_type=F32) + lax.dot_general(
                    qd16[:, sl], st.astype(BF16), _NT, preferred_element_type=F32)
                st_ref[h] = st * ebl[:, sl] + lax.dot_general(v16[:, sl], kd16[:, sl], _TN, preferred_element_type=F32)
                opre_ref[rows, sl] = o
                r = lax.rsqrt(jnp.mean(o * o, axis=-1, keepdims=True) + EPS)
                o_ref[rows, sl] = ((o * r * gn) * gate[:, sl]).astype(BF16)
            return carry

        lax.fori_loop(0, HG_CPB, chunk, 0, unroll=4)

    zspec = lambda s: pl.BlockSpec((HG_TB, HG_WIDTH), lambda b, t: (b * HG_NT + t, s))
    return pl.pallas_call(
        body,
        name="hgrn_fwd",
        grid=(B_LOC, HG_NT),
        in_specs=[zspec(0), zspec(1), zspec(2), zspec(3), zspec(0),
                  pl.BlockSpec((3, HG_WIDTH), lambda b, t: (0, 0)), pl.BlockSpec((1, HG_DIM), lambda b, t: (0, 0))],
        out_specs=[pl.BlockSpec((HG_TB, 2 * HG_WIDTH), lambda b, t: (b * HG_NT + t, 0)), zspec(0),
                   pl.BlockSpec((1, HG_HEADS, HG_CPB, HG_DIM, HG_DIM), lambda b, t: (b, 0, t, 0, 0))],
        out_shape=[
            jax.ShapeDtypeStruct((N_TOK, 2 * HG_WIDTH), BF16),
            jax.ShapeDtypeStruct((N_TOK, HG_WIDTH), F32),
            jax.ShapeDtypeStruct((B_LOC, HG_HEADS, HG_NCHUNK, HG_DIM, HG_DIM), F32),
        ],
        scratch_shapes=[pltpu.VMEM((HG_HEADS, HG_DIM, HG_DIM), F32)],
        compiler_params=_cp("parallel", "arbitrary"),
    )(z, z, z, z, o_mem, lb_logits, gnorm)


def _hgrn_bwd(z, opre, dcat, dq_mem, sall, lb_logits, gnorm):
    def body(zq_ref, zf_ref, zi_ref, zg_ref, opre_ref, dout_ref, dqm_ref, sall_ref, lbl_ref, gn_ref,
             dz_ref, dlbl_ref, dgn_ref, dst_ref, dlb_ref, dgn_acc, db_ref, dkk_ref, dbl_ref):
        b_id, t_id = pl.program_id(0), pl.program_id(1)
        lb, p = _lower_bound(lbl_ref[...])
        gn = gn_ref[...]
        mask = _tril(HG_CHUNK)
        tril_f = mask.astype(F32)
        dz_ref[:, 4 * HG_WIDTH:] = dqm_ref[...]

        @pl.when(t_id == 0)
        def _():
            dst_ref[...] = jnp.zeros_like(dst_ref)
            dlb_ref[...] = jnp.zeros_like(dlb_ref)

        @pl.when((b_id == 0) & (t_id == 0))
        def _():
            dgn_acc[...] = jnp.zeros_like(dgn_acc)

        def chunk(i, carry):
            c = HG_CPB - 1 - i
            rows = pl.ds(pl.multiple_of(c * HG_CHUNK, HG_CHUNK), HG_CHUNK)
            zq, zg = zq_ref[rows, :], zg_ref[rows, :]
            q, sq, sig, f, kk, b, bl = _hgrn_gates(zq, zf_ref[rows, :], lb, tril_f)
            v16 = zi_ref[rows, :].astype(BF16)
            eb, enb, ebl_b, ebl = jnp.exp(b), jnp.exp(-b), jnp.exp(bl - b), jnp.exp(bl)
            qd, ki, kd = q * eb, kk * enb, kk * ebl_b
            qd16, ki16, kd16 = qd.astype(BF16), ki.astype(BF16), kd.astype(BF16)
            o_all = opre_ref[rows, :]
            dout = dout_ref[rows, :]
            sg = _sigmoid(zg)
            d_on_all = dout * (zg * sg)
            dgate = dout * (sg * (1.0 + zg * (1.0 - sg)))
            dq_scale = eb * (sq * (1.0 + zq * (1.0 - sq)))
            for h in range(HG_HEADS):
                sl = _head(h)
                o = o_all[:, sl]
                r = lax.rsqrt(jnp.mean(o * o, axis=-1, keepdims=True) + EPS)
                ohat = o * r
                d_on = d_on_all[:, sl]
                dz_ref[rows, _head(h, 3)] = (dgate[:, sl] * (ohat * gn)).astype(BF16)
                dgn_acc[...] += jnp.sum(d_on * ohat, axis=0, keepdims=True)
                dohat = d_on * gn
                do16 = (r * (dohat - ohat * jnp.mean(dohat * ohat, axis=-1, keepdims=True))).astype(BF16)
                st = sall_ref[0, h, c]
                dst = dst_ref[h]
                st16, dst16 = st.astype(BF16), dst.astype(BF16)
                qd_h, ki_h, kd_h, v_h = qd16[:, sl], ki16[:, sl], kd16[:, sl], v16[:, sl]
                a16 = jnp.where(mask, lax.dot_general(qd_h, ki_h, _NT, preferred_element_type=F32), 0.0).astype(BF16)
                da16 = jnp.where(mask, lax.dot_general(do16, v_h, _NT, preferred_element_type=F32), 0.0).astype(BF16)
                dv = lax.dot_general(a16, do16, _TN, preferred_element_type=F32) + lax.dot_general(
                    kd_h, dst16, _NT, preferred_element_type=F32)
                dqd = jnp.dot(da16, ki_h, preferred_element_type=F32) + jnp.dot(do16, st16, preferred_element_type=F32)
                dki = lax.dot_general(da16, qd_h, _TN, preferred_element_type=F32)
                dkd = jnp.dot(v_h, dst16, preferred_element_type=F32)
                dbl_ref[:, sl] = jnp.sum(dkd * kd[:, sl], axis=0, keepdims=True) + ebl[:, sl] * jnp.sum(
                    st * dst, axis=0, keepdims=True)
                dst_ref[h] = dst * ebl[:, sl] + lax.dot_general(do16, qd_h, _TN, preferred_element_type=F32)
                dz_ref[rows, _head(h, 2)] = dv.astype(BF16)
                dz_ref[rows, sl] = (dqd * dq_scale[:, sl]).astype(BF16)
                dkk_ref[:, sl] = dki * enb[:, sl] + dkd * ebl_b[:, sl]
                db_ref[:, sl] = dqd * qd[:, sl] - dki * ki[:, sl] - dkd * kd[:, sl]
            dlogf = lax.dot_general(tril_f, db_ref[...], _TN, preferred_element_type=F32,
                                    precision=lax.Precision.HIGHEST) + dbl_ref[...]
            df = dlogf / f - dkk_ref[...]
            dz_ref[rows, HG_WIDTH:2 * HG_WIDTH] = (df * (1.0 - lb) * sig * (1.0 - sig)).astype(BF16)
            dlb_ref[...] += jnp.sum(df * (1.0 - sig), axis=0, keepdims=True)
            return carry

        lax.fori_loop(0, HG_CPB, chunk, 0, unroll=4)

        @pl.when(t_id == HG_NT - 1)
        def _():
            row0 = (lax.broadcasted_iota(jnp.int32, (3, HG_WIDTH), 0) == 0).astype(F32)
            dlbl_part = dlb_ref[...] * lb * (row0 - p)

            @pl.when(b_id == 0)
            def _():
                dlbl_ref[...] = dlbl_part

            @pl.when(b_id > 0)
            def _():
                dlbl_ref[...] += dlbl_part

            dgn_ref[...] = dgn_acc[...]

    rev = lambda b, t: b * HG_NT + HG_NT - 1 - t
    zspec = lambda s: pl.BlockSpec((HG_TB, HG_WIDTH), lambda b, t: (rev(b, t), s))
    return pl.pallas_call(
        body,
        name="hgrn_bwd",
        grid=(B_LOC, HG_NT),
        in_specs=[zspec(0), zspec(1), zspec(2), zspec(3), zspec(0), zspec(0), zspec(0),
                  pl.BlockSpec((1, HG_HEADS, HG_CPB, HG_DIM, HG_DIM), lambda b, t: (b, 0, HG_NT - 1 - t, 0, 0)),
                  pl.BlockSpec((3, HG_WIDTH), lambda b, t: (0, 0)), pl.BlockSpec((1, HG_DIM), lambda b, t: (0, 0))],
        out_specs=[pl.BlockSpec((HG_TB, 5 * HG_WIDTH), lambda b, t: (rev(b, t), 0)),
                   pl.BlockSpec((3, HG_WIDTH), lambda b, t: (0, 0)), pl.BlockSpec((1, HG_DIM), lambda b, t: (0, 0))],
        out_shape=[jax.ShapeDtypeStruct((N_TOK, 5 * HG_WIDTH), BF16),
                   jax.ShapeDtypeStruct((3, HG_WIDTH), F32), jax.ShapeDtypeStruct((1, HG_DIM), F32)],
        scratch_shapes=[pltpu.VMEM((HG_HEADS, HG_DIM, HG_DIM), F32), pltpu.VMEM((1, HG_WIDTH), F32),
                        pltpu.VMEM((1, HG_DIM), F32), pltpu.VMEM((HG_CHUNK, HG_WIDTH), F32),
                        pltpu.VMEM((HG_CHUNK, HG_WIDTH), F32), pltpu.VMEM((1, HG_WIDTH), F32)],
        compiler_params=_cp("arbitrary", "arbitrary"),
    )(z, z, z, z, opre, dcat, dq_mem, sall, lb_logits, gnorm)


GM_TM = 256


def _gmlp_norm(zv, ln_g, ln_b):
    gv, dgelu = _gelu_parts(zv)
    xc = gv - jnp.mean(gv, axis=-1, keepdims=True)
    rstd = lax.rsqrt(jnp.mean(xc * xc, axis=-1, keepdims=True) + EPS)
    vhat = xc * rstd
    return vhat * ln_g + ln_b, vhat, rstd, dgelu


def _gmlp_specs():
    half = lambda j: pl.BlockSpec((GM_TM, GM_WIDTH), lambda i: (i, j))
    vec = pl.BlockSpec((1, GM_WIDTH), lambda i: (0, 0))
    w = pl.BlockSpec((GM_GROUPS, GM_CHUNK, GM_CHUNK), lambda i: (0, 0, 0))
    bt = pl.BlockSpec((GM_CHUNK, GM_GROUPS), lambda i: (0, 0))
    return half, vec, w, bt


def _gmlp_fwd(z, o_mem, ln_g, ln_b, w_s, b_st):
    def body(zu_ref, zv_ref, omem_ref, g_ref, b_ref, w_ref, bt_ref, o_ref):
        o_ref[:, GM_WIDTH:] = omem_ref[...]
        u, _ = _gelu_parts(zu_ref[...])
        v, _, _, _ = _gmlp_norm(zv_ref[...], g_ref[...], b_ref[...])
        v16 = v.astype(BF16)
        mask = _tril(GM_CHUNK)
        bt = bt_ref[...]
        for g in range(GM_GROUPS):
            wm16 = jnp.where(mask, w_ref[g], 0.0).astype(BF16)
            cols = slice(g * GM_GDIM, (g + 1) * GM_GDIM)
            for c in range(GM_TM // GM_CHUNK):
                rows = slice(c * GM_CHUNK, (c + 1) * GM_CHUNK)
                mixed = jnp.dot(wm16, v16[rows, cols], preferred_element_type=F32) + bt[:, g:g + 1]
                o_ref[rows, cols] = (u[rows, cols] * mixed).astype(BF16)

    half, vec, w, bt = _gmlp_specs()
    return pl.pallas_call(
        body,
        name="gmlp_fwd",
        grid=(N_TOK // GM_TM,),
        in_specs=[half(0), half(1), pl.BlockSpec((GM_TM, XA_HEADS * XA_DIM), lambda i: (i, 0)), vec, vec, w, bt],
        out_specs=pl.BlockSpec((GM_TM, GM_WIDTH + XA_HEADS * XA_DIM), lambda i: (i, 0)),
        out_shape=jax.ShapeDtypeStruct((N_TOK, GM_WIDTH + XA_HEADS * XA_DIM), BF16),
        compiler_params=_cp("parallel"),
    )(z, z, o_mem, ln_g, ln_b, w_s, b_st)


def _gmlp_bwd(z, dcat, dq_mem, ln_g, ln_b, w_s, b_st):
    def body(zu_ref, zv_ref, dout_ref, dqm_ref, g_ref, b_ref, w_ref, bt_ref,
             dz_ref, dw_ref, dbt_ref, dg_ref, db_ref, dv_ref):
        dz_ref[:, 2 * GM_WIDTH:] = dqm_ref[...]
        @pl.when(pl.program_id(0) == 0)
        def _():
            dw_ref[...] = jnp.zeros_like(dw_ref)
            dbt_ref[...] = jnp.zeros_like(dbt_ref)
            dg_ref[...] = jnp.zeros_like(dg_ref)
            db_ref[...] = jnp.zeros_like(db_ref)

        zu = zu_ref[...]
        u, du_dz = _gelu_parts(zu)
        ln_g = g_ref[...]
        v, vhat, rstd, dgv_dz = _gmlp_norm(zv_ref[...], ln_g, b_ref[...])
        v16 = v.astype(BF16)
        dout = dout_ref[...]
        dmixed = dout * u
        dm16 = dmixed.astype(BF16)
        mask = _tril(GM_CHUNK)
        bt = bt_ref[...]
        group_id = lax.broadcasted_iota(jnp.int32, (1, GM_GROUPS), 1)
        dbt = jnp.zeros((GM_CHUNK, GM_GROUPS), F32)
        for g in range(GM_GROUPS):
            wm16 = jnp.where(mask, w_ref[g], 0.0).astype(BF16)
            cols = slice(g * GM_GDIM, (g + 1) * GM_GDIM)
            dw = jnp.zeros((GM_CHUNK, GM_CHUNK), F32)
            dbt_g = jnp.zeros((GM_CHUNK, 1), F32)
            for c in range(GM_TM // GM_CHUNK):
                rows = slice(c * GM_CHUNK, (c + 1) * GM_CHUNK)
                mixed = jnp.dot(wm16, v16[rows, cols], preferred_element_type=F32) + bt[:, g:g + 1]
                dz_ref[rows, cols] = (dout[rows, cols] * mixed * du_dz[rows, cols]).astype(BF16)
                dw += lax.dot_general(dm16[rows, cols], v16[rows, cols], _NT, preferred_element_type=F32)
                dbt_g += jnp.sum(dmixed[rows, cols], axis=-1, keepdims=True)
                dv_ref[rows, cols] = lax.dot_general(wm16, dm16[rows, cols], _TN, preferred_element_type=F32)
            dw_ref[g] += jnp.where(mask, dw, 0.0)
            dbt = dbt + dbt_g * (group_id == g).astype(F32)
        dbt_ref[...] += dbt
        dv = dv_ref[...]
        dg_ref[...] += jnp.sum(dv * vhat, axis=0, keepdims=True)
        db_ref[...] += jnp.sum(dv, axis=0, keepdims=True)
        dvh = dv * ln_g
        dgv = rstd * (dvh - jnp.mean(dvh, axis=-1, keepdims=True) - vhat * jnp.mean(dvh * vhat, axis=-1, keepdims=True))
        dz_ref[:, GM_WIDTH:2 * GM_WIDTH] = (dgv * dgv_dz).astype(BF16)

    half, vec, w, bt = _gmlp_specs()
    dz_width = 2 * GM_WIDTH + XA_HEADS * XA_DIM
    return pl.pallas_call(
        body,
        name="gmlp_bwd",
        grid=(N_TOK // GM_TM,),
        in_specs=[half(0), half(1), half(0), pl.BlockSpec((GM_TM, XA_HEADS * XA_DIM), lambda i: (i, 0)), vec, vec, w, bt],
        out_specs=[pl.BlockSpec((GM_TM, dz_width), lambda i: (i, 0)), w, bt, vec, vec],
        out_shape=[jax.ShapeDtypeStruct((N_TOK, dz_width), BF16),
                   jax.ShapeDtypeStruct((GM_GROUPS, GM_CHUNK, GM_CHUNK), F32),
                   jax.ShapeDtypeStruct((GM_CHUNK, GM_GROUPS), F32),
                   jax.ShapeDtypeStruct((1, GM_WIDTH), F32), jax.ShapeDtypeStruct((1, GM_WIDTH), F32)],
        scratch_shapes=[pltpu.VMEM((GM_TM, GM_WIDTH), F32)],
        compiler_params=_cp("arbitrary"),
    )(z, z, dcat, dq_mem, ln_g, ln_b, w_s, b_st)


def _own_slot(shape):
    return pl.BlockSpec((None,) + tuple(shape), lambda i, me_ref: (me_ref[0],) + (0,) * len(shape))


def _place_rows(w, layer, cuts_columns, me, *, name, deps=()):
    _, r, c = w.shape
    n = c if cuts_columns else r

    def body(me_ref, w_ref, *rest):
        o_ref = rest[len(deps)]
        wv = w_ref[...]
        o_ref[...] = (wv.T if cuts_columns else wv).astype(BF16)

    return pl.pallas_call(
        body,
        name=name,
        grid_spec=pltpu.PrefetchScalarGridSpec(
            num_scalar_prefetch=1, grid=(1,),
            in_specs=[pl.BlockSpec((None, r, c), lambda i, me_ref: (layer, 0, 0))] + [ANY_SPEC] * len(deps),
            out_specs=_own_slot((n, D_MODEL))),
        out_shape=jax.ShapeDtypeStruct((N_DEV, n, D_MODEL), BF16),
        compiler_params=_cp("arbitrary"),
    )(me, w, *deps)


def _place_ln(ln_g, ln_b, me):
    blk = ln_g.shape[1]

    def body(me_ref, g_ref, b_ref, o_ref):
        o_ref[...] = jnp.zeros_like(o_ref)
        o_ref[0:1, :] = g_ref[...]
        o_ref[1:2, :] = b_ref[...]

    vec = pl.BlockSpec((1, blk), lambda i, me_ref: (0, 0))
    return pl.pallas_call(
        body,
        name="place_ln",
        grid_spec=pltpu.PrefetchScalarGridSpec(
            num_scalar_prefetch=1, grid=(1,), in_specs=[vec, vec], out_specs=_own_slot((8, blk))),
        out_shape=jax.ShapeDtypeStruct((N_DEV, 8, blk), F32),
        compiler_params=_cp("arbitrary"),
    )(me, ln_g, ln_b)


def _place_slab(a, me, *, name):
    def body(me_ref, a_ref, o_ref):
        o_ref[...] = a_ref[...]

    return pl.pallas_call(
        body,
        name=name,
        grid_spec=pltpu.PrefetchScalarGridSpec(
            num_scalar_prefetch=1, grid=(1,),
            in_specs=[pl.BlockSpec(a.shape, lambda i, me_ref: (0, 0))], out_specs=_own_slot(a.shape)),
        out_shape=jax.ShapeDtypeStruct((N_DEV,) + a.shape, a.dtype),
        compiler_params=_cp("arbitrary"),
    )(me, a)


def _place_own(grads, me, *, name):
    k = len(grads)

    def body(me_ref, *refs):
        for src, dst in zip(refs[:k], refs[k:]):
            dst[...] = src[...]

    specs = [_own_slot(g.shape[1:]) for g in grads]
    return pl.pallas_call(
        body,
        name=name,
        grid_spec=pltpu.PrefetchScalarGridSpec(num_scalar_prefetch=1, grid=(1,), in_specs=specs, out_specs=specs),
        out_shape=[jax.ShapeDtypeStruct(g.shape, g.dtype) for g in grads],
        compiler_params=_cp("arbitrary"),
    )(me, *grads)


def _mesh_pos():
    x, y, c = (lax.axis_index(a) for a in MESH_AXES)
    return x, y, c, 4 * x + 2 * y + c


def _peer(x, y, c, r):
    px = 1 - x if r & 4 else x
    py = 1 - y if r & 2 else y
    pc = 1 - c if r & 1 else c
    return (px, py, pc), 4 * px + 2 * py + pc


RELATIONS = {"scatter": (1, 2, 3, 4, 5, 6, 7), "gather_all": (1, 2, 3, 4, 5, 6, 7), "gather_chips": (1, 2, 4, 6),
             "gather_sibling": (2, 4, 6)}


def _peer_copies(srcs, lands, send_sems, recv_sems, mode, waits):
    x, y, c, me = _mesh_pos()
    rel = RELATIONS[mode]
    pairs = []
    for ri, r in enumerate(rel):
        if mode == "gather_sibling":
            peer, _ = _peer(x, y, c, 1)
            _, sent_blk = _peer(x, y, c, r)
            _, got_blk = _peer(x, y, c, r ^ 1)
        else:
            peer, peer_blk = _peer(x, y, c, r)
            sent_blk, got_blk = (peer_blk if mode == "scatter" else me), peer_blk
        for k, (src, land) in enumerate(zip(srcs, lands)):
            idx = k * len(rel) + ri
            sems = dict(send_sem=send_sems.at[idx], recv_sem=recv_sems.at[idx], device_id=peer,
                        device_id_type=pl.DeviceIdType.MESH)
            dst_blk = sent_blk if mode == "gather_sibling" else me
            mine = pltpu.make_async_remote_copy(src_ref=src.at[sent_blk], dst_ref=land.at[dst_blk], **sems)
            theirs = pltpu.make_async_remote_copy(src_ref=src.at[sent_blk], dst_ref=land.at[got_blk], **sems) if waits else None
            pairs.append((mine, theirs))
    return pairs


DATAFLOW = pltpu.SideEffectType.DATAFLOW_SIDE_EFFECTING


def _in_hbm(a):
    return pltpu.with_memory_space_constraint(a, pltpu.HBM)


def _copies_start(srcs, lands, *, mode, name, deps=()):
    gather = mode != "scatter"
    arrs = list(lands) if gather else list(srcs) + list(lands)
    n, k, nd = len(arrs), len(lands), len(deps)

    def body(*refs):
        ins, send_sems, recv_sems, token = refs[:n], refs[n + nd], refs[n + nd + 1], refs[2 * n + nd + 2]
        src_refs, land_refs = (ins, ins) if gather else (ins[:k], ins[k:])
        for mine, _ in _peer_copies(src_refs, land_refs, send_sems, recv_sems, mode, waits=False):
            mine.start()
        token[...] = jnp.zeros_like(token)

    n_cp = k * len(RELATIONS[mode])
    return pl.pallas_call(
        body,
        name=name,
        in_specs=[HBM_SPEC] * n + [ANY_SPEC] * nd,
        out_specs=(SEM_SPEC, SEM_SPEC, *[HBM_SPEC] * n, pl.BlockSpec(memory_space=pltpu.VMEM)),
        out_shape=(pltpu.SemaphoreType.DMA((n_cp,)), pltpu.SemaphoreType.DMA((n_cp,)),
                   *[pltpu.HBM(a.shape, a.dtype) for a in arrs], jax.ShapeDtypeStruct((8, 128), F32)),
        input_output_aliases={i: 2 + i for i in range(n)},
        compiler_params=pltpu.CompilerParams(has_side_effects=DATAFLOW),
    )(*[_in_hbm(a) for a in arrs], *deps)


def _copies_wait(arrs, send_sems, recv_sems, after, *, n_lands, mode, name):
    n, k = len(arrs), n_lands
    gather = mode != "scatter"

    def body(*refs):
        ins, send_sems, recv_sems = refs[:n], refs[n], refs[n + 1]
        src_refs, land_refs = (ins, ins) if gather else (ins[:k], ins[k:])
        for mine, theirs in _peer_copies(src_refs, land_refs, send_sems, recv_sems, mode, waits=True):
            mine.wait_send()
            theirs.wait_recv()

    outs = pl.pallas_call(
        body,
        name=name,
        in_specs=[HBM_SPEC] * n + [SEM_SPEC, SEM_SPEC] + [ANY_SPEC] * len(after),
        out_specs=[HBM_SPEC] * n,
        out_shape=[pltpu.HBM(a.shape, a.dtype) for a in arrs],
        input_output_aliases={i: i for i in range(n)},
        compiler_params=pltpu.CompilerParams(has_side_effects=DATAFLOW),
    )(*arrs, send_sems, recv_sems, *after)
    return outs[n - k:]


def _adamw(w, g, m, v):
    m = ADAM_B1 * m + (1.0 - ADAM_B1) * g
    v = ADAM_B2 * v + (1.0 - ADAM_B2) * (g * g)
    m_hat = m / (1.0 - ADAM_B1 ** ADAM_STEP)
    v_hat = v / (1.0 - ADAM_B2 ** ADAM_STEP)
    return -ADAM_LR * (m_hat / (jnp.sqrt(v_hat) + ADAM_EPS) + ADAM_WD * w), m, v


ADAM_TC = 512


def _adam_big(slots, w, m, v, cuts_columns, *, name):
    layers, n, nj = len(slots), slots[0].shape[1], D_MODEL // ADAM_TC

    def body(*refs):
        s_refs = refs[:layers]
        w_ref, m_ref, v_ref, g_ref, d_ref, nm_ref, nv_ref, acc_ref = refs[layers:]
        for ll in range(layers):
            @pl.when(pl.program_id(0) == ll)
            def _(s_ref=s_refs[ll]):
                g = s_ref[0].astype(F32)
                for s in range(1, N_DEV):
                    g = g + s_ref[s].astype(F32)
                acc_ref[...] = g

        g = acc_ref[...].T if cuts_columns else acc_ref[...]
        g_ref[...] = g
        d_ref[...], nm_ref[...], nv_ref[...] = _adamw(w_ref[...], g, m_ref[...], v_ref[...])

    def slot_spec(ll):
        return pl.BlockSpec((N_DEV, n, ADAM_TC),
                            lambda l, j: (0, 0, jnp.where(l < ll, 0, jnp.where(l > ll, nj - 1, j))))

    if cuts_columns:
        w_spec = pl.BlockSpec((None, ADAM_TC, n), lambda l, j: (l, j, 0))
    else:
        w_spec = pl.BlockSpec((None, n, ADAM_TC), lambda l, j: (l, 0, j))
    return pl.pallas_call(
        body,
        name=name,
        grid=(layers, nj),
        in_specs=[slot_spec(ll) for ll in range(layers)] + [w_spec] * 3,
        out_specs=[w_spec] * 4,
        out_shape=[jax.ShapeDtypeStruct(w.shape, F32)] * 4,
        scratch_shapes=[pltpu.VMEM((n, ADAM_TC), F32)],
        compiler_params=_cp("arbitrary", "arbitrary"),
    )(*slots, w, m, v)


def _adam_slabs(slots, ws, ms, vs):
    n = len(slots)

    def body(*refs):
        ins, outs = refs[:4 * n], refs[4 * n:]
        for k in range(n):
            s_ref, w_ref, m_ref, v_ref = ins[k], ins[n + k], ins[2 * n + k], ins[3 * n + k]
            g = s_ref[0]
            for s in range(1, N_DEV):
                g = g + s_ref[s]
            outs[4 * k][...] = g
            outs[4 * k + 1][...], outs[4 * k + 2][...], outs[4 * k + 3][...] = _adamw(w_ref[...], g, m_ref[...], v_ref[...])

    res = pl.pallas_call(
        body,
        name="small_adamw",
        out_shape=[jax.ShapeDtypeStruct(w.shape, F32) for w in ws for _ in range(4)],
        compiler_params=pltpu.CompilerParams(vmem_limit_bytes=VMEM_LIMIT_BYTES),
    )(*slots, *ws, *ms, *vs)
    return [res[4 * k:4 * k + 4] for k in range(n)]


def _adam_vecs(gs, ws, ms, vs):
    n = len(gs)

    def body(*refs):
        ins, outs = refs[:4 * n], refs[4 * n:]
        for k in range(n):
            outs[3 * k][...], outs[3 * k + 1][...], outs[3 * k + 2][...] = _adamw(
                ins[n + k][...], ins[k][...], ins[2 * n + k][...], ins[3 * n + k][...])

    res = pl.pallas_call(
        body,
        name="ln_adamw",
        out_shape=[jax.ShapeDtypeStruct(w.shape, F32) for w in ws for _ in range(3)],
        compiler_params=pltpu.CompilerParams(vmem_limit_bytes=VMEM_LIMIT_BYTES),
    )(*gs, *ws, *ms, *vs)
    return [res[3 * k:3 * k + 3] for k in range(n)]


SLAB_AT = dict(mem_norm=0, lb_logits=1, ffn1_norm=4, mix_norm=6, hgrn_gnorm=8, gmlp_ln_g=9, gmlp_ln_b=11,
               gmlp_b_s=13, ffn2_norm=14, final_norm=16)
SLAB_ROWS = 24
LOSS_ROW = 17
SMALL_SHARDED = ("gmlp_ln_g", "gmlp_ln_b")


def _pack_slab(parts, *, name, deps=()):
    flat, plan = [], []
    for pname, at in SLAB_AT.items():
        for a in parts.get(pname, ()):
            flat.append(a)
            plan.append((at, a.shape))
            at += max(1, a.shape[0] * a.shape[1] // D_MODEL)
    for a in parts.get("loss", ()):
        flat.append(a)
        plan.append((LOSS_ROW, a.shape))

    def body(*refs):
        o_ref = refs[-1]
        o_ref[...] = jnp.zeros_like(o_ref)
        for ref, (at, (r, w)) in zip(refs, plan):
            if w == D_MODEL or r == 1 and w < D_MODEL:
                o_ref[at:at + r, 0:w] = ref[...]
            elif w < D_MODEL:
                for j in range(r):
                    o_ref[at:at + 1, j * w:(j + 1) * w] = ref[j:j + 1, :]
            else:
                for j in range(w // D_MODEL):
                    o_ref[at + j:at + j + 1, :] = ref[:, j * D_MODEL:(j + 1) * D_MODEL]

    return pl.pallas_call(
        body,
        name=name,
        in_specs=[pl.BlockSpec(memory_space=pltpu.VMEM)] * len(flat) + [ANY_SPEC] * len(deps),
        out_shape=jax.ShapeDtypeStruct((SLAB_ROWS, D_MODEL), F32),
        compiler_params=pltpu.CompilerParams(vmem_limit_bytes=VMEM_LIMIT_BYTES),
    )(*flat, *deps)


def _unpack_slab(slab, shapes):
    out = {}
    for pname, at in SLAB_AT.items():
        if pname in SMALL_SHARDED:
            continue
        size = math.prod(shapes[pname])
        rows = max(1, size // D_MODEL)
        out[pname] = slab[at:at + rows].reshape(-1)[:size].reshape(shapes[pname])
    return out


def _take_weights(full, new):
    deps = full.pop("deps", ()) + new.pop("deps", ())
    full.update(new, deps=deps)


def _ffn_fwd(x, norm_g, block, layer, full, get_weights):
    _take_weights(full, get_weights((layer, f"{block}_in"), (x,)))
    y, h, z, act = _ffn_forward(x, norm_g, full[(f"{block}_w_in", layer)], full[(f"{block}_w_out", layer)], scale=0.5,
                                deps=full.pop("deps", ()), name=f"l{layer}_{block}")
    _take_weights(full, get_weights((layer, f"{block}_out"), (y,)))
    return y, (x, h, z, act)


def _ffn_bwd(dy, dy16, saved, norm_g, w_in_t, w_out, tag, deps=(), after_out_wgrad=None, before_in_wgrad=None):
    x, h, z, act = saved
    dw_out = _mm(act, dy16, ta=True, tm=1408, tn=D_MODEL, tk=N_TOK, out_dtype=BF16, scale=0.5, deps=deps,
                 name=f"{tag}_out_wgrad")
    sent = after_out_wgrad(dw_out) if after_out_wgrad is not None else ()
    dz, dx, dx16, dg = _ffn_dgrad(dy16, dy, w_out, z, w_in_t, x, norm_g, scale=0.5, deps=sent, name=f"{tag}_dgrad")
    wdeps = before_in_wgrad(dg) if before_in_wgrad is not None else ()
    dw_in_t = _planes_wgrad(dz, h, deps=wdeps, name=f"{tag}_in_wgrad")
    return dx, dx16, dg, dw_in_t, dw_out


def kernel(x, mem, mem_norm, lb_logits, ffn1_norm, ffn1_w_in, ffn1_w_out, mix_norm, mem_w_kv, hgrn_w_in, hgrn_gnorm, hgrn_w_out, gmlp_w_in, gmlp_ln_g, gmlp_ln_b, gmlp_w_s, gmlp_b_s, gmlp_w_out, ffn2_norm, ffn2_w_in, ffn2_w_out, final_norm, loss_target, m_mem_norm, m_lb_logits, m_ffn1_norm, m_ffn1_w_in, m_ffn1_w_out, m_mix_norm, m_mem_w_kv, m_hgrn_w_in, m_hgrn_gnorm, m_hgrn_w_out, m_gmlp_w_in, m_gmlp_ln_g, m_gmlp_ln_b, m_gmlp_w_s, m_gmlp_b_s, m_gmlp_w_out, m_ffn2_norm, m_ffn2_w_in, m_ffn2_w_out, m_final_norm, v_mem_norm, v_lb_logits, v_ffn1_norm, v_ffn1_w_in, v_ffn1_w_out, v_mix_norm, v_mem_w_kv, v_hgrn_w_in, v_hgrn_gnorm, v_hgrn_w_out, v_gmlp_w_in, v_gmlp_ln_g, v_gmlp_ln_b, v_gmlp_w_s, v_gmlp_b_s, v_gmlp_w_out, v_ffn2_norm, v_ffn2_w_in, v_ffn2_w_out, v_final_norm):
    weights = dict(mem_norm=mem_norm, lb_logits=lb_logits, ffn1_norm=ffn1_norm, ffn1_w_in=ffn1_w_in, ffn1_w_out=ffn1_w_out, mix_norm=mix_norm, mem_w_kv=mem_w_kv, hgrn_w_in=hgrn_w_in, hgrn_gnorm=hgrn_gnorm, hgrn_w_out=hgrn_w_out, gmlp_w_in=gmlp_w_in, gmlp_ln_g=gmlp_ln_g, gmlp_ln_b=gmlp_ln_b, gmlp_w_s=gmlp_w_s, gmlp_b_s=gmlp_b_s, gmlp_w_out=gmlp_w_out, ffn2_norm=ffn2_norm, ffn2_w_in=ffn2_w_in, ffn2_w_out=ffn2_w_out, final_norm=final_norm)
    mom_m = dict(mem_norm=m_mem_norm, lb_logits=m_lb_logits, ffn1_norm=m_ffn1_norm, ffn1_w_in=m_ffn1_w_in, ffn1_w_out=m_ffn1_w_out, mix_norm=m_mix_norm, mem_w_kv=m_mem_w_kv, hgrn_w_in=m_hgrn_w_in, hgrn_gnorm=m_hgrn_gnorm, hgrn_w_out=m_hgrn_w_out, gmlp_w_in=m_gmlp_w_in, gmlp_ln_g=m_gmlp_ln_g, gmlp_ln_b=m_gmlp_ln_b, gmlp_w_s=m_gmlp_w_s, gmlp_b_s=m_gmlp_b_s, gmlp_w_out=m_gmlp_w_out, ffn2_norm=m_ffn2_norm, ffn2_w_in=m_ffn2_w_in, ffn2_w_out=m_ffn2_w_out, final_norm=m_final_norm)
    mom_v = dict(mem_norm=v_mem_norm, lb_logits=v_lb_logits, ffn1_norm=v_ffn1_norm, ffn1_w_in=v_ffn1_w_in, ffn1_w_out=v_ffn1_w_out, mix_norm=v_mix_norm, mem_w_kv=v_mem_w_kv, hgrn_w_in=v_hgrn_w_in, hgrn_gnorm=v_hgrn_gnorm, hgrn_w_out=v_hgrn_w_out, gmlp_w_in=v_gmlp_w_in, gmlp_ln_g=v_gmlp_ln_g, gmlp_ln_b=v_gmlp_ln_b, gmlp_w_s=v_gmlp_w_s, gmlp_b_s=v_gmlp_b_s, gmlp_w_out=v_gmlp_w_out, ffn2_norm=v_ffn2_norm, ffn2_w_in=v_ffn2_w_in, ffn2_w_out=v_ffn2_w_out, final_norm=v_final_norm)
    order = list(weights)
    _, _, _, me = _mesh_pos()
    me_arr = jnp.reshape(me, (1,)).astype(jnp.int32)
    cuts = {name: c for name, c, _, _ in GROUPS}
    rows_already = tuple(name for name, c, _, n in GROUPS if c and n % 128)
    as_rows = lambda a: jnp.transpose(a, (0, 2, 1))
    for name in rows_already:
        weights[name], mom_m[name], mom_v[name] = as_rows(weights[name]), as_rows(mom_m[name]), as_rows(mom_v[name])
        cuts[name] = False

    mix1 =(("mem_w_kv", 1), ("gmlp_w_in", 0), ("gmlp_w_out", 0))
    gather_plan = (
        ((0, "ffn1_in"), _stage_pieces(0, "ffn1")),
        ((0, "mix_in"), _stage_pieces(0, "mix")),
        ((0, "ffn2_in"), _stage_pieces(0, "ffn2")),
        ((1, "ffn1_in"), _stage_pieces(1, "ffn1")),
        ((1, "mix_in"), mix1),
        ((1, "ffn2_in"), _stage_pieces(1, "ffn2")),
    )
    stage_of = {use: k for k, (use, _) in enumerate(gather_plan)}
    in_flight = {}

    def place(k, deps=()):
        pieces = gather_plan[k][1]
        lands = [_place_rows(weights[name], l, cuts[name], me_arr, deps=deps, name=f"place_{name}_{l}")
                 for name, l in pieces]
        if pieces is mix1:
            lands.append(_place_ln(gmlp_ln_g, gmlp_ln_b, me_arr))
        return lands

    placed = {0: place(0)}

    def start_chips(k, deps):
        lands = placed[k]
        send_sems, recv_sems, *thru, token = _copies_start(lands, lands, mode="gather_chips", deps=deps,
                                                           name=f"gather{k}_chips_start")
        in_flight[k] = (thru, send_sems, recv_sems)
        return token

    def pass_to_sibling(k, after):
        thru, send_sems, recv_sems = in_flight[k]
        outs = _copies_wait(thru, send_sems, recv_sems, after, n_lands=len(thru), mode="gather_chips",
                            name=f"gather{k}_chips_wait")
        send_sems, recv_sems, *thru, token = _copies_start(outs, outs, mode="gather_sibling",
                                                           name=f"gather{k}_sibling_start")
        in_flight[k] = (thru, send_sems, recv_sems)
        return token, token

    first_sent = start_chips(0, ())
    placed.update({k: place(k, (first_sent,)) for k in range(1, len(gather_plan))})
    placed_later = tuple(a for k in range(1, len(gather_plan)) for a in placed[k])
    points = [(i, p) for i in (0, 1) for p in ("ffn1_in", "ffn1_out", "mix_in", "mix_out", "ffn2_in", "ffn2_out")]
    pass_at = {j: points[points.index(use) - 1] for j, (use, _) in enumerate(gather_plan) if j}
    pass_at[1] = gather_plan[1][0]

    started = {0}
    early_start = (4, (0, "ffn2_in"))

    def get_weights(use, after):
        tokens, w = [], {}
        k = stage_of.get(use)

        def pass_on(j, after):
            token, landed = pass_to_sibling(j, after)
            tokens.append(token)
            if j + 1 < len(gather_plan) and j + 1 not in started:
                started.add(j + 1)
                tokens.append(start_chips(j + 1, (landed,)))

        if k == 0:
            pass_on(0, tuple(after) + placed_later)
        elif k is not None and pass_at[k] == use:
            pass_on(k, after)
        if k is not None:
            thru, send_sems, recv_sems = in_flight[k]
            outs = _copies_wait(thru, send_sems, recv_sems, after, n_lands=len(thru), mode="gather_sibling",
                                name=f"gather{k}_sibling_wait")
            after = (outs[0],)
            pieces = gather_plan[k][1]
            w = {p: o.reshape(N_DEV * o.shape[1], D_MODEL) for p, o in zip(pieces, outs)}
            if pieces is mix1:
                w["ln_g"] = outs[-1][:, 0, :].reshape(1, GM_WIDTH)
                w["ln_b"] = outs[-1][:, 1, :].reshape(1, GM_WIDTH)
        for j, at in pass_at.items():
            if at == use and j != k:
                pass_on(j, after)
        if use == early_start[1] and early_start[0] not in started:
            started.add(early_start[0])
            tokens.append(start_chips(early_start[0], after))
        w["deps"] = tuple(tokens)
        return w

    scatter = {}

    def put_grads(st, grads):
        if st in ("w_s", "small"):
            slab = grads.reshape(GM_GROUPS * GM_CHUNK, GM_CHUNK) if st == "w_s" else _pack_slab(grads, name="pack_small_grads")
            land = _place_slab(slab, me_arr, name=f"{st}_place")
            send_sems, recv_sems, *thru, token = _copies_start([land], [land], mode="gather_all", name=f"{st}_start")
            scatter[st] = (thru, send_sems, recv_sems)
            return (token,)
        views = [g.reshape(N_DEV, -1, D_MODEL) for g in grads.values()]
        recv = _place_own(views, me_arr, name=f"scatter_place_l{st[0]}_{st[1]}")
        send_sems, recv_sems, *thru, token = _copies_start(views, recv, mode="scatter",
                                                           name=f"scatter_start_l{st[0]}_{st[1]}")
        scatter[st] = (tuple(grads), thru, send_sems, recv_sems)
        return (token,)

    dx, last_sent = _step_local(
        x, mem, loss_target, get_weights, put_grads, mem_norm, lb_logits, ffn1_norm, mix_norm, hgrn_gnorm,
        gmlp_w_s, gmlp_b_s, ffn2_norm, final_norm)

    slots = {}

    def wait_grads(blk, after, last=False):
        for st, entry in scatter.items():
            if isinstance(st, tuple) and st[1].startswith(blk) and (st == (0, "ffn1_in")) == last:
                pieces, thru, send_sems, recv_sems = entry
                outs = _copies_wait(thru, send_sems, recv_sems, after, n_lands=len(thru) // 2, mode="scatter",
                                    name=f"scatter_wait_l{st[0]}_{st[1]}")
                slots.update(zip(pieces, outs))

    grad, delta, new_m, new_v = {}, {}, {}, {}

    def adam_groups(names):
        for name in names:
            layers = GROUP_LAYERS[name]
            grad[name], delta[name], new_m[name], new_v[name] = _adam_big(
                [slots[(name, l)] for l in range(layers)], weights[name], mom_m[name], mom_v[name], cuts[name],
                name=f"{name}_adamw")

    wait_grads("ffn2", (dx, *last_sent))
    adam_groups(("ffn2_w_in", "ffn2_w_out"))
    wait_grads("mix", (delta["ffn2_w_out"],))
    adam_groups(("mem_w_kv", "gmlp_w_in", "gmlp_w_out", "hgrn_w_in", "hgrn_w_out"))
    wait_grads("ffn1", (delta["hgrn_w_out"],))
    adam_groups(("ffn1_w_out",))

    def small_parts(src):
        parts = {n: [src[n].reshape(-1, src[n].shape[-1])] for n in SLAB_AT if n not in SMALL_SHARDED}
        return parts

    w_s_rows = lambda a: a.reshape(GM_GROUPS * GM_CHUNK, GM_CHUNK)
    small_done = (delta["hgrn_w_out"],)
    (slab_slots,) = _copies_wait(*scatter["small"], small_done, n_lands=1, mode="gather_all", name="small_wait")
    (ws_slots,) = _copies_wait(*scatter["w_s"], small_done, n_lands=1, mode="gather_all", name="w_s_wait")
    (g_slab, d_slab, nm_slab, nv_slab), (g_ws, d_ws, nm_ws, nv_ws) = _adam_slabs(
        [slab_slots, ws_slots],
        [_pack_slab(small_parts(weights), deps=(dx,), name="pack_small_w"), w_s_rows(gmlp_w_s)],
        [_pack_slab(small_parts(mom_m), deps=(dx,), name="pack_small_m"), w_s_rows(m_gmlp_w_s)],
        [_pack_slab(small_parts(mom_v), deps=(dx,), name="pack_small_v"), w_s_rows(v_gmlp_w_s)])
    shapes = {n: weights[n].shape for n in SLAB_AT}
    for out, slab, ws in ((grad, g_slab, g_ws), (delta, d_slab, d_ws), (new_m, nm_slab, nm_ws), (new_v, nv_slab, nv_ws)):
        out.update(_unpack_slab(slab, shapes))
        out["gmlp_w_s"] = ws.reshape(gmlp_w_s.shape)
    blk = GM_WIDTH // N_DEV
    g_ln = [lax.dynamic_slice(g_slab[SLAB_AT[n]:SLAB_AT[n] + 2].reshape(1, GM_WIDTH), (0, me * blk), (1, blk))
            for n in SMALL_SHARDED]
    ln_out = _adam_vecs(g_ln, [weights[n] for n in SMALL_SHARDED], [mom_m[n] for n in SMALL_SHARDED],
                        [mom_v[n] for n in SMALL_SHARDED])
    for n, g, (d, nm, nv) in zip(SMALL_SHARDED, g_ln, ln_out):
        grad[n], delta[n], new_m[n], new_v[n] = g, d, nm, nv

    wait_grads("ffn1", tuple(delta[n] for n in delta if n in GROUP_LAYERS) + (d_slab,), last=True)
    adam_groups(("ffn1_w_in",))

    for name in rows_already:
        for out in (grad, delta, new_m, new_v):
            out[name] = as_rows(out[name])
    loss = g_slab[LOSS_ROW, 0]
    grad_x = dx.reshape(B_LOC, SEQ, D_MODEL)
    return (loss, grad_x, *[grad[n] for n in order], *[delta[n] for n in order],
            *[new_m[n] for n in order], *[new_v[n] for n in order])


def _step_local(x, mem, loss_target, get_weights, put_grads, mem_norm, lb_logits, ffn1_norm, mix_norm, hgrn_gnorm,
                gmlp_w_s, gmlp_b_s, ffn2_norm, final_norm):
    w_s = gmlp_w_s[0]
    b_st = gmlp_b_s[0].T

    xs = x.reshape(N_TOK, D_MODEL)
    mem2d = mem.reshape(B_LOC * MEM_LEN, D_MODEL)
    mem_g = mem_norm.reshape(1, D_MODEL)
    saved, full = [], {}
    for i in range(2):
        xs, s_ffn1 = _ffn_fwd(xs, ffn1_norm[i:i + 1], "ffn1", i, full, get_weights)
        if i == 0:
            memn = _rms_fwd(mem2d, mem_g, deps=(xs,), name="mem_norm_fwd")
        _take_weights(full, get_weights((i, "mix_in"), (xs,)))
        mixer = "hgrn" if i == 0 else "gmlp"
        hm, zm = _norm_mm(xs, mix_norm[i:i + 1], full[(f"{mixer}_w_in", 0)], swiglu=False, tm=1024, tn=1280, deps=full.pop("deps", ()),
                          name=f"l{i}_mix_in")
        kv = _mm(memn, full[("mem_w_kv", i)], tb=True, tm=512, tn=512, tk=D_MODEL, out_dtype=F32, name=f"l{i}_mem_kv")
        o_mem = _attn_fwd(zm, kv, name=f"l{i}_attn")
        if i == 0:
            cat, o_pre, s_all = _hgrn_fwd(zm, o_mem, lb_logits, hgrn_gnorm)
            mix_saved = (o_pre, s_all)
        else:
            cat = _gmlp_fwd(zm, o_mem, full["ln_g"], full["ln_b"], w_s, b_st)
            mix_saved = ()
        x_mix = xs
        _take_weights(full, get_weights((i, "mix_out"), (cat,)))
        xs = _mm(cat, full[(f"{mixer}_w_out", 0)], tm=512, tn=D_MODEL, tk=cat.shape[1], out_dtype=F32, res=xs,
                 deps=full.pop("deps", ()), name=f"l{i}_mix_out")
        xs, s_ffn2 = _ffn_fwd(xs, ffn2_norm[i:i + 1], "ffn2", i, full, get_weights)
        saved.append((s_ffn1, (x_mix, hm, kv, zm, cat, mix_saved), s_ffn2))

    dx, dx16, d_final, loss_part = _loss_head(xs, final_norm.reshape(1, D_MODEL), loss_target.reshape(N_TOK, D_MODEL))

    small = {"final_norm": [d_final], "loss": [loss_part]}
    d_ffn1, d_ffn2, d_mix = [None, None], [None, None], [None, None]
    dmemn = jnp.zeros((B_LOC * MEM_LEN, D_MODEL), F32)
    deps = ()
    for i in (1, 0):
        s_ffn1, (x_mix, hm, kv, zm, cat, mix_saved), s_ffn2 = saved[i]
        dx, dx16, d_ffn2[i], dw_in_t, dw_out = _ffn_bwd(
            dx, dx16, s_ffn2, ffn2_norm[i:i + 1], full[("ffn2_w_in", i)], full[("ffn2_w_out", i)], f"l{i}_ffn2", deps)
        deps = put_grads((i, "ffn2"), {("ffn2_w_in", i): dw_in_t, ("ffn2_w_out", i): dw_out})
        mixer = "hgrn" if i == 0 else "gmlp"
        w_in_t, w_out = full[(f"{mixer}_w_in", 0)], full[(f"{mixer}_w_out", 0)]
        width = cat.shape[1]
        g_mix = {}
        g_mix[(f"{mixer}_w_out", 0)] = _mm(cat, dx16, ta=True, tm=1024, tn=D_MODEL, tk=N_TOK, out_dtype=BF16,
                                           deps=deps, name=f"l{i}_mix_out_wgrad")
        dcat = _mm(dx16, w_out, tb=True, tm=1024, tn=width // 2, tk=D_MODEL, out_dtype=F32, name=f"l{i}_mix_out_dgrad")
        dq, dk, dv = _attn_bwd(zm, kv, dcat, do_off=width - XA_HEADS * XA_DIM, name=f"l{i}_attn_bwd")
        if i == 0:
            dzm, dlbl, dgn = _hgrn_bwd(zm, mix_saved[0], dcat, dq, mix_saved[1], lb_logits, hgrn_gnorm)
            small["lb_logits"], small["hgrn_gnorm"] = [dlbl], [dgn]
            deps = ()
        else:
            dzm, dws, dbt, dlng, dlnb = _gmlp_bwd(zm, dcat, dq, full["ln_g"], full["ln_b"], w_s, b_st)
            small["gmlp_b_s"], small["gmlp_ln_g"], small["gmlp_ln_b"] = [dbt.T], [dlng], [dlnb]
            deps = put_grads("w_s", dws)
        g_mix[(f"{mixer}_w_in", 0)] = _mm(dzm, hm, ta=True, tm=1024, tn=D_MODEL, tk=N_TOK, out_dtype=BF16, deps=deps,
                                          name=f"l{i}_mix_in_wgrad")
        dkv = jnp.concatenate([dk, dv], axis=1)
        g_mix[("mem_w_kv", i)] = _mm(dkv, memn, ta=True, tm=512, tn=D_MODEL, tk=B_LOC * MEM_LEN, out_dtype=BF16,
                                     name=f"l{i}_mem_kv_wgrad")
        deps = put_grads((i, "mix"), g_mix)
        dx, dx16, d_mix[i] = _dgrad_norm_bwd(dzm, w_in_t, x_mix, mix_norm[i:i + 1], dx, deps=deps,
                                             name=f"l{i}_mix_in_dgrad")
        dmemn = _mm(dkv, full[("mem_w_kv", i)], tm=B_LOC * MEM_LEN, tn=D_MODEL, tk=512, out_dtype=F32, res=dmemn,
                    name=f"l{i}_mem_kv_dgrad")
        def send_small(dg, i=i, dmemn=dmemn):
            d_ffn1[i] = dg
            _, _, dmem_g = _rms_bwd(mem2d, mem_g, dmemn, dmemn, name="mem_norm_bwd")
            small.update(mem_norm=[dmem_g], ffn1_norm=d_ffn1, ffn2_norm=d_ffn2, mix_norm=d_mix)
            return put_grads("small", small)

        if i == 0:
            send_out = lambda dw_out: put_grads((0, "ffn1_out"), {("ffn1_w_out", 0): dw_out})
            dx, dx16, d_ffn1[i], dw_in_t, _ = _ffn_bwd(
                dx, dx16, s_ffn1, ffn1_norm[i:i + 1], full[("ffn1_w_in", i)], full[("ffn1_w_out", i)], f"l{i}_ffn1",
                after_out_wgrad=send_out, before_in_wgrad=send_small)
            deps = put_grads((0, "ffn1_in"), {("ffn1_w_in", 0): dw_in_t})
        else:
            dx, dx16, d_ffn1[i], dw_in_t, dw_out = _ffn_bwd(
                dx, dx16, s_ffn1, ffn1_norm[i:i + 1], full[("ffn1_w_in", i)], full[("ffn1_w_out", i)], f"l{i}_ffn1")
            deps = put_grads((i, "ffn1"), {("ffn1_w_in", i): dw_in_t, ("ffn1_w_out", i): dw_out})
    return dx, deps
```

```python
import functools
import math

import jax
import jax.numpy as jnp
from jax import lax
from jax.experimental import pallas as pl
from jax.experimental.pallas import tpu as pltpu

F32 = jnp.float32
BF16 = jnp.bfloat16

D_MODEL = 1024
SEQ = 2048
B_LOC = 2
N_TOK = B_LOC * SEQ
MEM_LEN = 256
N_DEV = 8
EPS = 1e-6
D_FF = 2816
HG_HEADS = 8
HG_DIM = 128
HG_CHUNK = 64
HG_NCHUNK = SEQ // HG_CHUNK
GM_CHUNK = 128
GM_GROUPS = 8
GM_WIDTH = 2048
GM_GDIM = GM_WIDTH // GM_GROUPS
XA_HEADS = 4
XA_DIM = 256
XA_OFF = 4096

ADAM_LR = 0.001
ADAM_B1 = 0.9
ADAM_B2 = 0.999
ADAM_EPS = 1e-08
ADAM_WD = 0.01
ADAM_STEP = 10

VMEM_LIMIT_BYTES = 56 * 1024 * 1024
MESH_AXES = ("x", "y", "c")

GROUPS = (
    ("ffn1_w_in", True, 2, 704),
    ("ffn1_w_out", False, 2, 352),
    ("mem_w_kv", True, 2, 256),
    ("hgrn_w_in", True, 1, 640),
    ("hgrn_w_out", False, 1, 256),
    ("gmlp_w_in", True, 1, 640),
    ("gmlp_w_out", False, 1, 384),
    ("ffn2_w_in", True, 2, 704),
    ("ffn2_w_out", False, 2, 352),
)
GROUP_LAYERS = {name: layers for name, _, layers, _ in GROUPS}


def _stage_pieces(layer, block):
    if block == "mix":
        mixer = "hgrn" if layer == 0 else "gmlp"
        return (("mem_w_kv", layer), (f"{mixer}_w_in", 0), (f"{mixer}_w_out", 0))
    return ((f"{block}_w_in", layer), (f"{block}_w_out", layer))


ANY_SPEC = pl.BlockSpec(memory_space=pl.ANY)
HBM_SPEC = pl.BlockSpec(memory_space=pltpu.HBM)
SEM_SPEC = pl.BlockSpec(memory_space=pltpu.SEMAPHORE)


def _cp(*sem):
    return pltpu.CompilerParams(dimension_semantics=sem, vmem_limit_bytes=VMEM_LIMIT_BYTES)


def _sigmoid(x):
    return 0.5 * jnp.tanh(0.5 * x) + 0.5


def _gelu_parts(x):
    cdf = 0.5 * (1.0 + lax.erf(x * (1.0 / math.sqrt(2.0))))
    pdf = jnp.exp(-0.5 * x * x) * (1.0 / math.sqrt(2.0 * math.pi))
    return x * cdf, cdf + x * pdf


def _mm(a, b, *, ta=False, tb=False, tm, tn, tk, out_dtype, res=None, scale=1.0, deps=(), name):
    m, k = (a.shape[1], a.shape[0]) if ta else a.shape
    n, kb = b.shape if tb else (b.shape[1], b.shape[0])
    assert k == kb and m % tm == 0 and n % tn == 0 and k % tk == 0, (name, a.shape, b.shape)
    nk = k // tk
    dn = (((0 if ta else 1,), (1 if tb else 0,)), ((), ()))
    n_in = 2 + (res is not None) + len(deps)

    def body(*refs):
        a_ref, b_ref = refs[:2]
        r_ref = refs[2] if res is not None else None
        o_ref, scr = refs[n_in], refs[n_in + 1:]
        p = lax.dot_general(a_ref[...].astype(BF16), b_ref[...].astype(BF16), dn, preferred_element_type=F32)

        def finish(acc):
            if scale != 1.0:
                acc = scale * acc
            if r_ref is not None:
                acc = r_ref[...] + acc
            o_ref[...] = acc.astype(out_dtype)

        if nk == 1:
            finish(p)
        else:
            acc_ref = scr[0]
            kk = pl.program_id(2)

            @pl.when(kk == 0)
            def _():
                acc_ref[...] = p

            @pl.when(kk > 0)
            def _():
                acc_ref[...] += p

            @pl.when(kk == nk - 1)
            def _():
                finish(acc_ref[...])

    a_spec = pl.BlockSpec((tk, tm), lambda i, j, kk: (kk, i)) if ta else pl.BlockSpec((tm, tk), lambda i, j, kk: (i, kk))
    b_mode = dict(pipeline_mode=pl.Buffered(1)) if n == tn and nk == 1 else {}
    if tb:
        b_spec = pl.BlockSpec((tn, tk), lambda i, j, kk: (j, kk), **b_mode)
    else:
        b_spec = pl.BlockSpec((tk, tn), lambda i, j, kk: (kk, j), **b_mode)
    o_spec = pl.BlockSpec((tm, tn), lambda i, j, kk: (i, j))
    in_specs = [a_spec, b_spec] + ([o_spec] if res is not None else []) + [ANY_SPEC] * len(deps)
    args = (a, b) + ((res,) if res is not None else ()) + tuple(deps)
    return pl.pallas_call(
        body,
        name=name,
        grid=(m // tm, n // tn, nk),
        in_specs=in_specs,
        out_specs=o_spec,
        out_shape=jax.ShapeDtypeStruct((m, n), out_dtype),
        scratch_shapes=[pltpu.VMEM((tm, tn), F32)] if nk > 1 else [],
        compiler_params=_cp("parallel", "parallel", "arbitrary"),
    )(*args)


def _rms_fwd(x, g, *, name, deps=(), tm=512):
    rows = x.shape[0]

    def body(x_ref, g_ref, *rest):
        o_ref = rest[len(deps)]
        xv = x_ref[...]
        r = lax.rsqrt(jnp.mean(xv * xv, axis=-1, keepdims=True) + EPS)
        o_ref[...] = (xv * r * g_ref[...]).astype(BF16)

    row = pl.BlockSpec((tm, D_MODEL), lambda i: (i, 0))
    return pl.pallas_call(
        body,
        name=name,
        grid=(rows // tm,),
        in_specs=[row, pl.BlockSpec((1, D_MODEL), lambda i: (0, 0))] + [ANY_SPEC] * len(deps),
        out_specs=row,
        out_shape=jax.ShapeDtypeStruct((rows, D_MODEL), BF16),
        compiler_params=_cp("parallel"),
    )(x, g, *deps)


def _rms_bwd(x, g, dh, dres, *, name, deps=(), tm=512):
    rows = x.shape[0]

    def body(x_ref, g_ref, dh_ref, dres_ref, *rest):
        dx_ref, dx16_ref, dg_ref = rest[len(deps):]
        xv = x_ref[...]
        r = lax.rsqrt(jnp.mean(xv * xv, axis=-1, keepdims=True) + EPS)
        xhat = xv * r
        dhv = dh_ref[...]
        part = jnp.sum(dhv * xhat, axis=0, keepdims=True)

        @pl.when(pl.program_id(0) == 0)
        def _():
            dg_ref[...] = part

        @pl.when(pl.program_id(0) > 0)
        def _():
            dg_ref[...] += part

        dxh = dhv * g_ref[...]
        dx = dres_ref[...] + r * (dxh - xhat * jnp.mean(dxh * xhat, axis=-1, keepdims=True))
        dx_ref[...] = dx
        dx16_ref[...] = dx.astype(BF16)

    row = pl.BlockSpec((tm, D_MODEL), lambda i: (i, 0))
    vec = pl.BlockSpec((1, D_MODEL), lambda i: (0, 0))
    return pl.pallas_call(
        body,
        name=name,
        grid=(rows // tm,),
        in_specs=[row, vec, row, row] + [ANY_SPEC] * len(deps),
        out_specs=[row, row, vec],
        out_shape=[jax.ShapeDtypeStruct((rows, D_MODEL), F32), jax.ShapeDtypeStruct((rows, D_MODEL), BF16),
                   jax.ShapeDtypeStruct((1, D_MODEL), F32)],
        compiler_params=_cp("arbitrary"),
    )(x, g, dh, dres, *deps)


_NT = (((1,), (1,)), ((), ()))
_TN = (((0,), (0,)), ((), ()))


def _norm_mm(x, g, w_t, *, name, tm, tn, deps=()):
    rows = w_t.shape[0]
    nd = len(deps)

    def body(x_ref, g_ref, w_ref, *rest):
        h_ref, z_ref = rest[nd:]
        j = pl.program_id(1)

        @pl.when(j == 0)
        def _():
            xv = x_ref[...]
            r = lax.rsqrt(jnp.mean(xv * xv, axis=-1, keepdims=True) + EPS)
            h_ref[...] = (xv * r * g_ref[...]).astype(BF16)

        w = w_ref[pl.ds(pl.multiple_of(j * tn, tn), tn), :]
        z_ref[...] = lax.dot_general(h_ref[...], w, _NT, preferred_element_type=F32)

    row = pl.BlockSpec((tm, D_MODEL), lambda i, j: (i, 0))
    return pl.pallas_call(
        body,
        name=name,
        grid=(N_TOK // tm, rows // tn),
        in_specs=[row, pl.BlockSpec((1, D_MODEL), lambda i, j: (0, 0)),
                  pl.BlockSpec((rows, D_MODEL), lambda i, j: (0, 0), pipeline_mode=pl.Buffered(1))] + [ANY_SPEC] * nd,
        out_specs=[row, pl.BlockSpec((tm, tn), lambda i, j: (i, j))],
        out_shape=[jax.ShapeDtypeStruct((N_TOK, D_MODEL), BF16), jax.ShapeDtypeStruct((N_TOK, rows), F32)],
        compiler_params=_cp("parallel", "arbitrary"),
    )(x, g, w_t, *deps)


def _ffn_forward(x, g, w_in_t, w_out, *, scale, name, deps=(), tm=256, tn=1408):
    nd = len(deps)

    def body(x_ref, g_ref, wi_ref, wo_ref, *rest):
        y_ref, h_ref, z_ref, act_ref = rest[nd:]
        xv = x_ref[...]
        r = lax.rsqrt(jnp.mean(xv * xv, axis=-1, keepdims=True) + EPS)
        h = (xv * r * g_ref[...]).astype(BF16)
        h_ref[...] = h
        for j in range(D_FF // tn):
            cols = slice(j * tn, (j + 1) * tn)
            gate = lax.dot_general(h, wi_ref[j * tn:(j + 1) * tn, :], _NT, preferred_element_type=F32)
            up = lax.dot_general(h, wi_ref[D_FF + j * tn:D_FF + (j + 1) * tn, :], _NT, preferred_element_type=F32)
            s = _sigmoid(gate)
            silu = gate * s
            z_ref[0, :, cols] = (up * (s + silu * (1.0 - s))).astype(BF16)
            z_ref[1, :, cols] = silu.astype(BF16)
            act_ref[:, cols] = (silu * up).astype(BF16)
        y_ref[...] = xv + scale * jnp.dot(act_ref[...], wo_ref[...], preferred_element_type=F32)

    row = pl.BlockSpec((tm, D_MODEL), lambda i: (i, 0))
    whole = lambda rows: pl.BlockSpec((rows, D_MODEL), lambda i: (0, 0), pipeline_mode=pl.Buffered(1))
    return pl.pallas_call(
        body,
        name=name,
        grid=(N_TOK // tm,),
        in_specs=[row, pl.BlockSpec((1, D_MODEL), lambda i: (0, 0)), whole(2 * D_FF), whole(D_FF)] + [ANY_SPEC] * nd,
        out_specs=[row, row, pl.BlockSpec((2, tm, D_FF), lambda i: (0, i, 0)), pl.BlockSpec((tm, D_FF), lambda i: (i, 0))],
        out_shape=[jax.ShapeDtypeStruct((N_TOK, D_MODEL), F32), jax.ShapeDtypeStruct((N_TOK, D_MODEL), BF16),
                   jax.ShapeDtypeStruct((2, N_TOK, D_FF), BF16), jax.ShapeDtypeStruct((N_TOK, D_FF), BF16)],
        compiler_params=_cp("parallel"),
    )(x, g, w_in_t, w_out, *deps)


def _ffn_dgrad(dy16, dres, w_out, z, w_in_t, x, g, *, scale, name, deps=(), tm=256, tn=1408):
    nd = len(deps)

    def body(dy_ref, dres_ref, wo_ref, z_ref, wi_ref, x_ref, g_ref, *rest):
        dz_ref, dx_ref, dx16_ref, dg_ref = rest[nd:]
        dy = dy_ref[...]
        for j in range(D_FF // tn):
            cols = slice(j * tn, (j + 1) * tn)
            da = lax.dot_general(dy, wo_ref[cols, :], _NT, preferred_element_type=F32) * scale
            dz_ref[0, :, cols] = (da * z_ref[0, :, cols].astype(F32)).astype(BF16)
            dz_ref[1, :, cols] = (da * z_ref[1, :, cols].astype(F32)).astype(BF16)
        dh = jnp.dot(dz_ref[0], wi_ref[:D_FF, :], preferred_element_type=F32) + jnp.dot(
            dz_ref[1], wi_ref[D_FF:, :], preferred_element_type=F32)
        xv = x_ref[...]
        r = lax.rsqrt(jnp.mean(xv * xv, axis=-1, keepdims=True) + EPS)
        xhat = xv * r
        part = jnp.sum(dh * xhat, axis=0, keepdims=True)

        @pl.when(pl.program_id(0) == 0)
        def _():
            dg_ref[...] = part

        @pl.when(pl.program_id(0) > 0)
        def _():
            dg_ref[...] += part

        dxh = dh * g_ref[...]
        dx = dres_ref[...] + r * (dxh - xhat * jnp.mean(dxh * xhat, axis=-1, keepdims=True))
        dx_ref[...] = dx
        dx16_ref[...] = dx.astype(BF16)

    row = pl.BlockSpec((tm, D_MODEL), lambda i: (i, 0))
    vec = pl.BlockSpec((1, D_MODEL), lambda i: (0, 0))
    planes = pl.BlockSpec((2, tm, D_FF), lambda i: (0, i, 0))
    whole = lambda rows: pl.BlockSpec((rows, D_MODEL), lambda i: (0, 0), pipeline_mode=pl.Buffered(1))
    return pl.pallas_call(
        body,
        name=name,
        grid=(N_TOK // tm,),
        in_specs=[row, row, whole(D_FF), planes, whole(2 * D_FF), row, vec] + [ANY_SPEC] * nd,
        out_specs=[planes, row, row, vec],
        out_shape=[jax.ShapeDtypeStruct((2, N_TOK, D_FF), BF16), jax.ShapeDtypeStruct((N_TOK, D_MODEL), F32),
                   jax.ShapeDtypeStruct((N_TOK, D_MODEL), BF16), jax.ShapeDtypeStruct((1, D_MODEL), F32)],
        compiler_params=_cp("arbitrary"),
    )(dy16, dres, w_out, z, w_in_t, x, g, *deps)


def _planes_wgrad(dz, h, *, name, deps=(), tm=1408):
    per_plane = D_FF // tm

    def body(a_ref, b_ref, *rest):
        o_ref = rest[len(deps)]
        o_ref[...] = lax.dot_general(a_ref[...], b_ref[...], _TN, preferred_element_type=F32).astype(BF16)

    return pl.pallas_call(
        body,
        name=name,
        grid=(2 * per_plane,),
        in_specs=[pl.BlockSpec((None, N_TOK, tm),
                               lambda i: (jnp.where(i < per_plane, 0, 1), 0, jnp.where(i < per_plane, i, i - per_plane))),
                  pl.BlockSpec((N_TOK, D_MODEL), lambda i: (0, 0), pipeline_mode=pl.Buffered(1))] + [ANY_SPEC] * len(deps),
        out_specs=pl.BlockSpec((tm, D_MODEL), lambda i: (i, 0)),
        out_shape=jax.ShapeDtypeStruct((2 * D_FF, D_MODEL), BF16),
        compiler_params=_cp("parallel"),
    )(dz, h, *deps)


def _dgrad_norm_bwd(dz, w_t, x, g, dres, *, name, deps=(), tm=512):
    rows = w_t.shape[0]
    nd = len(deps)

    def body(a_ref, b_ref, x_ref, g_ref, dres_ref, *rest):
        dx_ref, dx16_ref, dg_ref = rest[nd:]
        dh = jnp.dot(a_ref[...], b_ref[...], preferred_element_type=F32)
        xv = x_ref[...]
        r = lax.rsqrt(jnp.mean(xv * xv, axis=-1, keepdims=True) + EPS)
        xhat = xv * r
        part = jnp.sum(dh * xhat, axis=0, keepdims=True)

        @pl.when(pl.program_id(0) == 0)
        def _():
            dg_ref[...] = part

        @pl.when(pl.program_id(0) > 0)
        def _():
            dg_ref[...] += part

        dxh = dh * g_ref[...]
        dx = dres_ref[...] + r * (dxh - xhat * jnp.mean(dxh * xhat, axis=-1, keepdims=True))
        dx_ref[...] = dx
        dx16_ref[...] = dx.astype(BF16)

    a_spec = pl.BlockSpec((tm, rows), lambda i: (i, 0))
    row = pl.BlockSpec((tm, D_MODEL), lambda i: (i, 0))
    vec = pl.BlockSpec((1, D_MODEL), lambda i: (0, 0))
    return pl.pallas_call(
        body,
        name=name,
        grid=(N_TOK // tm,),
        in_specs=[a_spec, pl.BlockSpec((rows, D_MODEL), lambda i: (0, 0), pipeline_mode=pl.Buffered(1)), row, vec, row]
        + [ANY_SPEC] * nd,
        out_specs=[row, row, vec],
        out_shape=[jax.ShapeDtypeStruct((N_TOK, D_MODEL), F32), jax.ShapeDtypeStruct((N_TOK, D_MODEL), BF16),
                   jax.ShapeDtypeStruct((1, D_MODEL), F32)],
        compiler_params=_cp("arbitrary"),
    )(dz, w_t, x, g, dres, *deps)


def _loss_head(x, g, target, *, tm=512):
    def body(x_ref, g_ref, t_ref, dx_ref, dx16_ref, dg_ref, loss_ref):
        xv = x_ref[...]
        gv = g_ref[...]
        r = lax.rsqrt(jnp.mean(xv * xv, axis=-1, keepdims=True) + EPS)
        xhat = xv * r
        err = xhat * gv - t_ref[...]
        loss_part = jnp.zeros((1, 128), F32) + 0.5 * jnp.sum(jnp.mean(err * err, axis=-1, keepdims=True))
        dy = err * (1.0 / D_MODEL)
        dg_part = jnp.sum(dy * xhat, axis=0, keepdims=True)

        @pl.when(pl.program_id(0) == 0)
        def _():
            dg_ref[...] = dg_part
            loss_ref[...] = loss_part

        @pl.when(pl.program_id(0) > 0)
        def _():
            dg_ref[...] += dg_part
            loss_ref[...] += loss_part

        dxh = dy * gv
        dx = r * (dxh - xhat * jnp.mean(dxh * xhat, axis=-1, keepdims=True))
        dx_ref[...] = dx
        dx16_ref[...] = dx.astype(BF16)

    row = pl.BlockSpec((tm, D_MODEL), lambda i: (i, 0))
    vec = pl.BlockSpec((1, D_MODEL), lambda i: (0, 0))
    return pl.pallas_call(
        body,
        name="loss_head",
        grid=(N_TOK // tm,),
        in_specs=[row, vec, row],
        out_specs=[row, row, vec, pl.BlockSpec((1, 128), lambda i: (0, 0))],
        out_shape=[
            jax.ShapeDtypeStruct((N_TOK, D_MODEL), F32),
            jax.ShapeDtypeStruct((N_TOK, D_MODEL), BF16),
            jax.ShapeDtypeStruct((1, D_MODEL), F32),
            jax.ShapeDtypeStruct((1, 128), F32),
        ],
        compiler_params=_cp("arbitrary"),
    )(x, g, target)


XA_TQ = 2048
XA_SCALE = XA_DIM ** -0.5


def _attn_probs(q16, k16):
    s = lax.dot_general(q16, k16, _NT, preferred_element_type=F32) * XA_SCALE
    e = jnp.exp(s - jnp.max(s, axis=-1, keepdims=True))
    return e / jnp.sum(e, axis=-1, keepdims=True)


def _attn_fwd(z, kv, *, name):
    nt = SEQ // XA_TQ

    def body(q_ref, k_ref, v_ref, o_ref):
        p = _attn_probs(q_ref[...].astype(BF16), k_ref[...].astype(BF16))
        o_ref[...] = jnp.dot(p.astype(BF16), v_ref[...].astype(BF16), preferred_element_type=F32).astype(BF16)

    return pl.pallas_call(
        body,
        name=name,
        grid=(B_LOC, XA_HEADS, nt),
        in_specs=[
            pl.BlockSpec((XA_TQ, XA_DIM), lambda b, h, t: (b * nt + t, XA_OFF // XA_DIM + h)),
            pl.BlockSpec((MEM_LEN, XA_DIM), lambda b, h, t: (b, h)),
            pl.BlockSpec((MEM_LEN, XA_DIM), lambda b, h, t: (b, XA_HEADS + h)),
        ],
        out_specs=pl.BlockSpec((XA_TQ, XA_DIM), lambda b, h, t: (b * nt + t, h)),
        out_shape=jax.ShapeDtypeStruct((N_TOK, XA_HEADS * XA_DIM), BF16),
        compiler_params=_cp("parallel", "parallel", "arbitrary"),
    )(z, kv, kv)


def _attn_bwd(z, kv, dcat, *, do_off, name):
    nt = SEQ // XA_TQ

    def body(q_ref, k_ref, v_ref, do_ref, dq_ref, dk_ref, dv_ref):
        q16 = q_ref[...].astype(BF16)
        k16 = k_ref[...].astype(BF16)
        v16 = v_ref[...].astype(BF16)
        do16 = do_ref[...].astype(BF16)
        p = _attn_probs(q16, k16)
        dv_part = lax.dot_general(p.astype(BF16), do16, _TN, preferred_element_type=F32)
        dp = lax.dot_general(do16, v16, _NT, preferred_element_type=F32)
        ds16 = (p * (dp - jnp.sum(dp * p, axis=-1, keepdims=True)) * XA_SCALE).astype(BF16)
        dq_ref[...] = jnp.dot(ds16, k16, preferred_element_type=F32).astype(BF16)
        dk_part = lax.dot_general(ds16, q16, _TN, preferred_element_type=F32)

        @pl.when(pl.program_id(2) == 0)
        def _():
            dk_ref[...] = dk_part
            dv_ref[...] = dv_part

        @pl.when(pl.program_id(2) > 0)
        def _():
            dk_ref[...] += dk_part
            dv_ref[...] += dv_part

    qspec = pl.BlockSpec((XA_TQ, XA_DIM), lambda b, h, t: (b * nt + t, XA_OFF // XA_DIM + h))
    kspec = lambda off: pl.BlockSpec((MEM_LEN, XA_DIM), lambda b, h, t: (b, off + h))
    return pl.pallas_call(
        body,
        name=name,
        grid=(B_LOC, XA_HEADS, nt),
        in_specs=[qspec, kspec(0), kspec(XA_HEADS),
                  pl.BlockSpec((XA_TQ, XA_DIM), lambda b, h, t: (b * nt + t, do_off // XA_DIM + h))],
        out_specs=[pl.BlockSpec((XA_TQ, XA_DIM), lambda b, h, t: (b * nt + t, h)), kspec(0), kspec(0)],
        out_shape=[
            jax.ShapeDtypeStruct((N_TOK, XA_HEADS * XA_DIM), BF16),
            jax.ShapeDtypeStruct((B_LOC * MEM_LEN, XA_HEADS * XA_DIM), F32),
            jax.ShapeDtypeStruct((B_LOC * MEM_LEN, XA_HEADS * XA_DIM), F32),
        ],
        compiler_params=_cp("parallel", "parallel", "arbitrary"),
    )(z, kv, kv, dcat)


def _tril(n):
    return lax.broadcasted_iota(jnp.int32, (n, n), 0) >= lax.broadcasted_iota(jnp.int32, (n, n), 1)


def _lower_bound(lbl):
    e = jnp.exp(lbl - jnp.max(lbl, axis=0, keepdims=True))
    p = e / jnp.sum(e, axis=0, keepdims=True)
    return p[0:1, :], p


def _hgrn_gates(zq, zf, lb, tril_f):
    sig = _sigmoid(zf)
    f = lb + (1.0 - lb) * sig
    kk = 1.0 - f
    sq = _sigmoid(zq)
    q = zq * sq
    b = jnp.dot(tril_f, jnp.log(f), preferred_element_type=F32, precision=lax.Precision.HIGHEST)
    bl = b[HG_CHUNK - 1:HG_CHUNK, :]
    return q, sq, sig, f, kk, b, bl


HG_TB = 512
HG_CPB = HG_TB // HG_CHUNK
HG_NT = SEQ // HG_TB
HG_WIDTH = HG_HEADS * HG_DIM


def _head(h, section=0):
    return slice(section * HG_WIDTH + h * HG_DIM, section * HG_WIDTH + (h + 1) * HG_DIM)


def _hgrn_fwd(z, o_mem, lb_logits, gnorm):
    def body(zq_ref, zf_ref, zi_ref, zg_ref, omem_ref, lbl_ref, gn_ref, o_ref, opre_ref, sall_ref, st_ref):
        lb, _ = _lower_bound(lbl_ref[...])
        gn = gn_ref[...]
        mask = _tril(HG_CHUNK)
        tril_f = mask.astype(F32)
        o_ref[:, HG_WIDTH:] = omem_ref[...]

        @pl.when(pl.program_id(1) == 0)
        def _():
            st_ref[...] = jnp.zeros_like(st_ref)

        def chunk(c, carry):
            rows = pl.ds(pl.multiple_of(c * HG_CHUNK, HG_CHUNK), HG_CHUNK)
            q, _, _, _, kk, b, bl = _hgrn_gates(zq_ref[rows, :], zf_ref[rows, :], lb, tril_f)
            v16 = zi_ref[rows, :].astype(BF16)
            qd16 = (q * jnp.exp(b)).astype(BF16)
            ki16 = (kk * jnp.exp(-b)).astype(BF16)
            kd16 = (kk * jnp.exp(bl - b)).astype(BF16)
            ebl = jnp.exp(bl)
            zg = zg_ref[rows, :]
            gate = zg * _sigmoid(zg)
            for h in range(HG_HEADS):
                sl = _head(h)
                a = jnp.where(mask, lax.dot_general(qd16[:, sl], ki16[:, sl], _NT, preferred_element_type=F32), 0.0)
                st = st_ref[h]
                sall_ref[0, h, c] = st
                o = jnp.dot(a.astype(BF16), v16[:, sl], preferred_element_type=F32) + lax.dot_general(
                    qd16[:, sl], st.astype(BF16), _NT, preferred_element_type=F32)
                st_ref[h] = st * ebl[:, sl] + lax.dot_general(v16[:, sl], kd16[:, sl], _TN, preferred_element_type=F32)
                opre_ref[rows, sl] = o
                r = lax.rsqrt(jnp.mean(o * o, axis=-1, keepdims=True) + EPS)
                o_ref[rows, sl] = ((o * r * gn) * gate[:, sl]).astype(BF16)
            return carry

        lax.fori_loop(0, HG_CPB, chunk, 0, unroll=True)

    zspec = lambda s: pl.BlockSpec((HG_TB, HG_WIDTH), lambda b, t: (b * HG_NT + t, s))
    return pl.pallas_call(
        body,
        name="hgrn_fwd",
        grid=(B_LOC, HG_NT),
        in_specs=[zspec(0), zspec(1), zspec(2), zspec(3), zspec(0),
                  pl.BlockSpec((3, HG_WIDTH), lambda b, t: (0, 0)), pl.BlockSpec((1, HG_DIM), lambda b, t: (0, 0))],
        out_specs=[pl.BlockSpec((HG_TB, 2 * HG_WIDTH), lambda b, t: (b * HG_NT + t, 0)), zspec(0),
                   pl.BlockSpec((1, HG_HEADS, HG_CPB, HG_DIM, HG_DIM), lambda b, t: (b, 0, t, 0, 0))],
        out_shape=[
            jax.ShapeDtypeStruct((N_TOK, 2 * HG_WIDTH), BF16),
            jax.ShapeDtypeStruct((N_TOK, HG_WIDTH), F32),
            jax.ShapeDtypeStruct((B_LOC, HG_HEADS, HG_NCHUNK, HG_DIM, HG_DIM), F32),
        ],
        scratch_shapes=[pltpu.VMEM((HG_HEADS, HG_DIM, HG_DIM), F32)],
        compiler_params=_cp("parallel", "arbitrary"),
    )(z, z, z, z, o_mem, lb_logits, gnorm)


def _hgrn_bwd(z, opre, dcat, dq_mem, sall, lb_logits, gnorm):
    def body(zq_ref, zf_ref, zi_ref, zg_ref, opre_ref, dout_ref, dqm_ref, sall_ref, lbl_ref, gn_ref,
             dz_ref, dlbl_ref, dgn_ref, dst_ref, dlb_ref, dgn_acc, db_ref, dkk_ref, dbl_ref):
        b_id, t_id = pl.program_id(0), pl.program_id(1)
        lb, p = _lower_bound(lbl_ref[...])
        gn = gn_ref[...]
        mask = _tril(HG_CHUNK)
        tril_f = mask.astype(F32)
        dz_ref[:, 4 * HG_WIDTH:] = dqm_ref[...]

        @pl.when(t_id == 0)
        def _():
            dst_ref[...] = jnp.zeros_like(dst_ref)
            dlb_ref[...] = jnp.zeros_like(dlb_ref)

        @pl.when((b_id == 0) & (t_id == 0))
        def _():
            dgn_acc[...] = jnp.zeros_like(dgn_acc)

        def chunk(i, carry):
            c = HG_CPB - 1 - i
            rows = pl.ds(pl.multiple_of(c * HG_CHUNK, HG_CHUNK), HG_CHUNK)
            zq, zg = zq_ref[rows, :], zg_ref[rows, :]
            q, sq, sig, f, kk, b, bl = _hgrn_gates(zq, zf_ref[rows, :], lb, tril_f)
            v16 = zi_ref[rows, :].astype(BF16)
            eb, enb, ebl_b, ebl = jnp.exp(b), jnp.exp(-b), jnp.exp(bl - b), jnp.exp(bl)
            qd, ki, kd = q * eb, kk * enb, kk * ebl_b
            qd16, ki16, kd16 = qd.astype(BF16), ki.astype(BF16), kd.astype(BF16)
            o_all = opre_ref[rows, :]
            dout = dout_ref[rows, :]
            sg = _sigmoid(zg)
            d_on_all = dout * (zg * sg)
            dgate = dout * (sg * (1.0 + zg * (1.0 - sg)))
            dq_scale = eb * (sq * (1.0 + zq * (1.0 - sq)))
            for h in range(HG_HEADS):
                sl = _head(h)
                o = o_all[:, sl]
                r = lax.rsqrt(jnp.mean(o * o, axis=-1, keepdims=True) + EPS)
                ohat = o * r
                d_on = d_on_all[:, sl]
                dz_ref[rows, _head(h, 3)] = (dgate[:, sl] * (ohat * gn)).astype(BF16)
                dgn_acc[...] += jnp.sum(d_on * ohat, axis=0, keepdims=True)
                dohat = d_on * gn
                do16 = (r * (dohat - ohat * jnp.mean(dohat * ohat, axis=-1, keepdims=True))).astype(BF16)
                st = sall_ref[0, h, c]
                dst = dst_ref[h]
                st16, dst16 = st.astype(BF16), dst.astype(BF16)
                qd_h, ki_h, kd_h, v_h = qd16[:, sl], ki16[:, sl], kd16[:, sl], v16[:, sl]
                a16 = jnp.where(mask, lax.dot_general(qd_h, ki_h, _NT, preferred_element_type=F32), 0.0).astype(BF16)
                da16 = jnp.where(mask, lax.dot_general(do16, v_h, _NT, preferred_element_type=F32), 0.0).astype(BF16)
                dv = lax.dot_general(a16, do16, _TN, preferred_element_type=F32) + lax.dot_general(
                    kd_h, dst16, _NT, preferred_element_type=F32)
                dqd = jnp.dot(da16, ki_h, preferred_element_type=F32) + jnp.dot(do16, st16, preferred_element_type=F32)
                dki = lax.dot_general(da16, qd_h, _TN, preferred_element_type=F32)
                dkd = jnp.dot(v_h, dst16, preferred_element_type=F32)
                dbl_ref[:, sl] = jnp.sum(dkd * kd[:, sl], axis=0, keepdims=True) + ebl[:, sl] * jnp.sum(
                    st * dst, axis=0, keepdims=True)
                dst_ref[h] = dst * ebl[:, sl] + lax.dot_general(do16, qd_h, _TN, preferred_element_type=F32)
                dz_ref[rows, _head(h, 2)] = dv.astype(BF16)
                dz_ref[rows, sl] = (dqd * dq_scale[:, sl]).astype(BF16)
                dkk_ref[:, sl] = dki * enb[:, sl] + dkd * ebl_b[:, sl]
                db_ref[:, sl] = dqd * qd[:, sl] - dki * ki[:, sl] - dkd * kd[:, sl]
            dlogf = lax.dot_general(tril_f, db_ref[...], _TN, preferred_element_type=F32,
                                    precision=lax.Precision.HIGHEST) + dbl_ref[...]
            df = dlogf / f - dkk_ref[...]
            dz_ref[rows, HG_WIDTH:2 * HG_WIDTH] = (df * (1.0 - lb) * sig * (1.0 - sig)).astype(BF16)
            dlb_ref[...] += jnp.sum(df * (1.0 - sig), axis=0, keepdims=True)
            return carry

        lax.fori_loop(0, HG_CPB, chunk, 0, unroll=True)

        @pl.when(t_id == HG_NT - 1)
        def _():
            row0 = (lax.broadcasted_iota(jnp.int32, (3, HG_WIDTH), 0) == 0).astype(F32)
            dlbl_part = dlb_ref[...] * lb * (row0 - p)

            @pl.when(b_id == 0)
            def _():
                dlbl_ref[...] = dlbl_part

            @pl.when(b_id > 0)
            def _():
                dlbl_ref[...] += dlbl_part

            dgn_ref[...] = dgn_acc[...]

    rev = lambda b, t: b * HG_NT + HG_NT - 1 - t
    zspec = lambda s: pl.BlockSpec((HG_TB, HG_WIDTH), lambda b, t: (rev(b, t), s))
    return pl.pallas_call(
        body,
        name="hgrn_bwd",
        grid=(B_LOC, HG_NT),
        in_specs=[zspec(0), zspec(1), zspec(2), zspec(3), zspec(0), zspec(0), zspec(0),
                  pl.BlockSpec((1, HG_HEADS, HG_CPB, HG_DIM, HG_DIM), lambda b, t: (b, 0, HG_NT - 1 - t, 0, 0)),
                  pl.BlockSpec((3, HG_WIDTH), lambda b, t: (0, 0)), pl.BlockSpec((1, HG_DIM), lambda b, t: (0, 0))],
        out_specs=[pl.BlockSpec((HG_TB, 5 * HG_WIDTH), lambda b, t: (rev(b, t), 0)),
                   pl.BlockSpec((3, HG_WIDTH), lambda b, t: (0, 0)), pl.BlockSpec((1, HG_DIM), lambda b, t: (0, 0))],
        out_shape=[jax.ShapeDtypeStruct((N_TOK, 5 * HG_WIDTH), BF16),
                   jax.ShapeDtypeStruct((3, HG_WIDTH), F32), jax.ShapeDtypeStruct((1, HG_DIM), F32)],
        scratch_shapes=[pltpu.VMEM((HG_HEADS, HG_DIM, HG_DIM), F32), pltpu.VMEM((1, HG_WIDTH), F32),
                        pltpu.VMEM((1, HG_DIM), F32), pltpu.VMEM((HG_CHUNK, HG_WIDTH), F32),
                        pltpu.VMEM((HG_CHUNK, HG_WIDTH), F32), pltpu.VMEM((1, HG_WIDTH), F32)],
        compiler_params=_cp("arbitrary", "arbitrary"),
    )(z, z, z, z, opre, dcat, dq_mem, sall, lb_logits, gnorm)


GM_TM = 256


def _gmlp_norm(zv, ln_g, ln_b):
    gv, dgelu = _gelu_parts(zv)
    xc = gv - jnp.mean(gv, axis=-1, keepdims=True)
    rstd = lax.rsqrt(jnp.mean(xc * xc, axis=-1, keepdims=True) + EPS)
    vhat = xc * rstd
    return vhat * ln_g + ln_b, vhat, rstd, dgelu


def _gmlp_specs():
    half = lambda j: pl.BlockSpec((GM_TM, GM_WIDTH), lambda i: (i, j))
    vec = pl.BlockSpec((1, GM_WIDTH), lambda i: (0, 0))
    w = pl.BlockSpec((GM_GROUPS, GM_CHUNK, GM_CHUNK), lambda i: (0, 0, 0))
    bt = pl.BlockSpec((GM_CHUNK, GM_GROUPS), lambda i: (0, 0))
    return half, vec, w, bt


def _gmlp_fwd(z, o_mem, ln_g, ln_b, w_s, b_st):
    def body(zu_ref, zv_ref, omem_ref, g_ref, b_ref, w_ref, bt_ref, o_ref):
        o_ref[:, GM_WIDTH:] = omem_ref[...]
        u, _ = _gelu_parts(zu_ref[...])
        v, _, _, _ = _gmlp_norm(zv_ref[...], g_ref[...], b_ref[...])
        v16 = v.astype(BF16)
        mask = _tril(GM_CHUNK)
        bt = bt_ref[...]
        for g in range(GM_GROUPS):
            wm16 = jnp.where(mask, w_ref[g], 0.0).astype(BF16)
            cols = slice(g * GM_GDIM, (g + 1) * GM_GDIM)
            for c in range(GM_TM // GM_CHUNK):
                rows = slice(c * GM_CHUNK, (c + 1) * GM_CHUNK)
                mixed = jnp.dot(wm16, v16[rows, cols], preferred_element_type=F32) + bt[:, g:g + 1]
                o_ref[rows, cols] = (u[rows, cols] * mixed).astype(BF16)

    half, vec, w, bt = _gmlp_specs()
    return pl.pallas_call(
        body,
        name="gmlp_fwd",
        grid=(N_TOK // GM_TM,),
        in_specs=[half(0), half(1), pl.BlockSpec((GM_TM, XA_HEADS * XA_DIM), lambda i: (i, 0)), vec, vec, w, bt],
        out_specs=pl.BlockSpec((GM_TM, GM_WIDTH + XA_HEADS * XA_DIM), lambda i: (i, 0)),
        out_shape=jax.ShapeDtypeStruct((N_TOK, GM_WIDTH + XA_HEADS * XA_DIM), BF16),
        compiler_params=_cp("parallel"),
    )(z, z, o_mem, ln_g, ln_b, w_s, b_st)


def _gmlp_bwd(z, dcat, dq_mem, ln_g, ln_b, w_s, b_st):
    def body(zu_ref, zv_ref, dout_ref, dqm_ref, g_ref, b_ref, w_ref, bt_ref,
             dz_ref, dw_ref, dbt_ref, dg_ref, db_ref, dv_ref):
        dz_ref[:, 2 * GM_WIDTH:] = dqm_ref[...]
        @pl.when(pl.program_id(0) == 0)
        def _():
            dw_ref[...] = jnp.zeros_like(dw_ref)
            dbt_ref[...] = jnp.zeros_like(dbt_ref)
            dg_ref[...] = jnp.zeros_like(dg_ref)
            db_ref[...] = jnp.zeros_like(db_ref)

        zu = zu_ref[...]
        u, du_dz = _gelu_parts(zu)
        ln_g = g_ref[...]
        v, vhat, rstd, dgv_dz = _gmlp_norm(zv_ref[...], ln_g, b_ref[...])
        v16 = v.astype(BF16)
        dout = dout_ref[...]
        dmixed = dout * u
        dm16 = dmixed.astype(BF16)
        mask = _tril(GM_CHUNK)
        bt = bt_ref[...]
        group_id = lax.broadcasted_iota(jnp.int32, (1, GM_GROUPS), 1)
        dbt = jnp.zeros((GM_CHUNK, GM_GROUPS), F32)
        for g in range(GM_GROUPS):
            wm16 = jnp.where(mask, w_ref[g], 0.0).astype(BF16)
            cols = slice(g * GM_GDIM, (g + 1) * GM_GDIM)
            dw = jnp.zeros((GM_CHUNK, GM_CHUNK), F32)
            dbt_g = jnp.zeros((GM_CHUNK, 1), F32)
            for c in range(GM_TM // GM_CHUNK):
                rows = slice(c * GM_CHUNK, (c + 1) * GM_CHUNK)
                mixed = jnp.dot(wm16, v16[rows, cols], preferred_element_type=F32) + bt[:, g:g + 1]
                dz_ref[rows, cols] = (dout[rows, cols] * mixed * du_dz[rows, cols]).astype(BF16)
                dw += lax.dot_general(dm16[rows, cols], v16[rows, cols], _NT, preferred_element_type=F32)
                dbt_g += jnp.sum(dmixed[rows, cols], axis=-1, keepdims=True)
                dv_ref[rows, cols] = lax.dot_general(wm16, dm16[rows, cols], _TN, preferred_element_type=F32)
            dw_ref[g] += jnp.where(mask, dw, 0.0)
            dbt = dbt + dbt_g * (group_id == g).astype(F32)
        dbt_ref[...] += dbt
        dv = dv_ref[...]
        dg_ref[...] += jnp.sum(dv * vhat, axis=0, keepdims=True)
        db_ref[...] += jnp.sum(dv, axis=0, keepdims=True)
        dvh = dv * ln_g
        dgv = rstd * (dvh - jnp.mean(dvh, axis=-1, keepdims=True) - vhat * jnp.mean(dvh * vhat, axis=-1, keepdims=True))
        dz_ref[:, GM_WIDTH:2 * GM_WIDTH] = (dgv * dgv_dz).astype(BF16)

    half, vec, w, bt = _gmlp_specs()
    dz_width = 2 * GM_WIDTH + XA_HEADS * XA_DIM
    return pl.pallas_call(
        body,
        name="gmlp_bwd",
        grid=(N_TOK // GM_TM,),
        in_specs=[half(0), half(1), half(0), pl.BlockSpec((GM_TM, XA_HEADS * XA_DIM), lambda i: (i, 0)), vec, vec, w, bt],
        out_specs=[pl.BlockSpec((GM_TM, dz_width), lambda i: (i, 0)), w, bt, vec, vec],
        out_shape=[jax.ShapeDtypeStruct((N_TOK, dz_width), BF16),
                   jax.ShapeDtypeStruct((GM_GROUPS, GM_CHUNK, GM_CHUNK), F32),
                   jax.ShapeDtypeStruct((GM_CHUNK, GM_GROUPS), F32),
                   jax.ShapeDtypeStruct((1, GM_WIDTH), F32), jax.ShapeDtypeStruct((1, GM_WIDTH), F32)],
        scratch_shapes=[pltpu.VMEM((GM_TM, GM_WIDTH), F32)],
        compiler_params=_cp("arbitrary"),
    )(z, z, dcat, dq_mem, ln_g, ln_b, w_s, b_st)


def _own_slot(shape):
    return pl.BlockSpec((None,) + tuple(shape), lambda i, me_ref: (me_ref[0],) + (0,) * len(shape))


def _place_rows(w, layer, cuts_columns, me, *, name, deps=()):
    _, r, c = w.shape
    n = c if cuts_columns else r

    def body(me_ref, w_ref, *rest):
        o_ref = rest[len(deps)]
        wv = w_ref[...]
        o_ref[...] = (wv.T if cuts_columns else wv).astype(BF16)

    return pl.pallas_call(
        body,
        name=name,
        grid_spec=pltpu.PrefetchScalarGridSpec(
            num_scalar_prefetch=1, grid=(1,),
            in_specs=[pl.BlockSpec((None, r, c), lambda i, me_ref: (layer, 0, 0))] + [ANY_SPEC] * len(deps),
            out_specs=_own_slot((n, D_MODEL))),
        out_shape=jax.ShapeDtypeStruct((N_DEV, n, D_MODEL), BF16),
        compiler_params=_cp("arbitrary"),
    )(me, w, *deps)


def _place_ln(ln_g, ln_b, me):
    blk = ln_g.shape[1]

    def body(me_ref, g_ref, b_ref, o_ref):
        o_ref[...] = jnp.zeros_like(o_ref)
        o_ref[0:1, :] = g_ref[...]
        o_ref[1:2, :] = b_ref[...]

    vec = pl.BlockSpec((1, blk), lambda i, me_ref: (0, 0))
    return pl.pallas_call(
        body,
        name="place_ln",
        grid_spec=pltpu.PrefetchScalarGridSpec(
            num_scalar_prefetch=1, grid=(1,), in_specs=[vec, vec], out_specs=_own_slot((8, blk))),
        out_shape=jax.ShapeDtypeStruct((N_DEV, 8, blk), F32),
        compiler_params=_cp("arbitrary"),
    )(me, ln_g, ln_b)


def _place_slab(a, me, *, name):
    def body(me_ref, a_ref, o_ref):
        o_ref[...] = a_ref[...]

    return pl.pallas_call(
        body,
        name=name,
        grid_spec=pltpu.PrefetchScalarGridSpec(
            num_scalar_prefetch=1, grid=(1,),
            in_specs=[pl.BlockSpec(a.shape, lambda i, me_ref: (0, 0))], out_specs=_own_slot(a.shape)),
        out_shape=jax.ShapeDtypeStruct((N_DEV,) + a.shape, a.dtype),
        compiler_params=_cp("arbitrary"),
    )(me, a)


def _place_own(grads, me, *, name):
    k = len(grads)

    def body(me_ref, *refs):
        for src, dst in zip(refs[:k], refs[k:]):
            dst[...] = src[...]

    specs = [_own_slot(g.shape[1:]) for g in grads]
    return pl.pallas_call(
        body,
        name=name,
        grid_spec=pltpu.PrefetchScalarGridSpec(num_scalar_prefetch=1, grid=(1,), in_specs=specs, out_specs=specs),
        out_shape=[jax.ShapeDtypeStruct(g.shape, g.dtype) for g in grads],
        compiler_params=_cp("arbitrary"),
    )(me, *grads)


def _mesh_pos():
    x, y, c = (lax.axis_index(a) for a in MESH_AXES)
    return x, y, c, 4 * x + 2 * y + c


def _peer(x, y, c, r):
    px = 1 - x if r & 4 else x
    py = 1 - y if r & 2 else y
    pc = 1 - c if r & 1 else c
    return (px, py, pc), 4 * px + 2 * py + pc


RELATIONS = {"scatter": (1, 2, 3, 4, 5, 6, 7), "gather_all": (1, 2, 3, 4, 5, 6, 7), "gather_chips": (1, 2, 4, 6),
             "gather_sibling": (2, 4, 6)}


def _peer_copies(srcs, lands, send_sems, recv_sems, mode, waits):
    x, y, c, me = _mesh_pos()
    rel = RELATIONS[mode]
    pairs = []
    for ri, r in enumerate(rel):
        if mode == "gather_sibling":
            peer, _ = _peer(x, y, c, 1)
            _, sent_blk = _peer(x, y, c, r)
            _, got_blk = _peer(x, y, c, r ^ 1)
        else:
            peer, peer_blk = _peer(x, y, c, r)
            sent_blk, got_blk = (peer_blk if mode == "scatter" else me), peer_blk
        for k, (src, land) in enumerate(zip(srcs, lands)):
            idx = k * len(rel) + ri
            sems = dict(send_sem=send_sems.at[idx], recv_sem=recv_sems.at[idx], device_id=peer,
                        device_id_type=pl.DeviceIdType.MESH)
            dst_blk = sent_blk if mode == "gather_sibling" else me
            mine = pltpu.make_async_remote_copy(src_ref=src.at[sent_blk], dst_ref=land.at[dst_blk], **sems)
            theirs = pltpu.make_async_remote_copy(src_ref=src.at[sent_blk], dst_ref=land.at[got_blk], **sems) if waits else None
            pairs.append((mine, theirs))
    return pairs


DATAFLOW = pltpu.SideEffectType.DATAFLOW_SIDE_EFFECTING


def _in_hbm(a):
    return pltpu.with_memory_space_constraint(a, pltpu.HBM)


def _copies_start(srcs, lands, *, mode, name, deps=()):
    gather = mode != "scatter"
    arrs = list(lands) if gather else list(srcs) + list(lands)
    n, k, nd = len(arrs), len(lands), len(deps)

    def body(*refs):
        ins, send_sems, recv_sems, token = refs[:n], refs[n + nd], refs[n + nd + 1], refs[2 * n + nd + 2]
        src_refs, land_refs = (ins, ins) if gather else (ins[:k], ins[k:])
        for mine, _ in _peer_copies(src_refs, land_refs, send_sems, recv_sems, mode, waits=False):
            mine.start()
        token[...] = jnp.zeros_like(token)

    n_cp = k * len(RELATIONS[mode])
    return pl.pallas_call(
        body,
        name=name,
        in_specs=[HBM_SPEC] * n + [ANY_SPEC] * nd,
        out_specs=(SEM_SPEC, SEM_SPEC, *[HBM_SPEC] * n, pl.BlockSpec(memory_space=pltpu.VMEM)),
        out_shape=(pltpu.SemaphoreType.DMA((n_cp,)), pltpu.SemaphoreType.DMA((n_cp,)),
                   *[pltpu.HBM(a.shape, a.dtype) for a in arrs], jax.ShapeDtypeStruct((8, 128), F32)),
        input_output_aliases={i: 2 + i for i in range(n)},
        compiler_params=pltpu.CompilerParams(has_side_effects=DATAFLOW),
    )(*[_in_hbm(a) for a in arrs], *deps)


def _copies_wait(arrs, send_sems, recv_sems, after, *, n_lands, mode, name):
    n, k = len(arrs), n_lands
    gather = mode != "scatter"

    def body(*refs):
        ins, send_sems, recv_sems = refs[:n], refs[n], refs[n + 1]
        src_refs, land_refs = (ins, ins) if gather else (ins[:k], ins[k:])
        for mine, theirs in _peer_copies(src_refs, land_refs, send_sems, recv_sems, mode, waits=True):
            mine.wait_send()
            theirs.wait_recv()

    outs = pl.pallas_call(
        body,
        name=name,
        in_specs=[HBM_SPEC] * n + [SEM_SPEC, SEM_SPEC] + [ANY_SPEC] * len(after),
        out_specs=[HBM_SPEC] * n,
        out_shape=[pltpu.HBM(a.shape, a.dtype) for a in arrs],
        input_output_aliases={i: i for i in range(n)},
        compiler_params=pltpu.CompilerParams(has_side_effects=DATAFLOW),
    )(*arrs, send_sems, recv_sems, *after)
    return outs[n - k:]


def _adamw(w, g, m, v):
    m = ADAM_B1 * m + (1.0 - ADAM_B1) * g
    v = ADAM_B2 * v + (1.0 - ADAM_B2) * (g * g)
    m_hat = m / (1.0 - ADAM_B1 ** ADAM_STEP)
    v_hat = v / (1.0 - ADAM_B2 ** ADAM_STEP)
    return -ADAM_LR * (m_hat / (jnp.sqrt(v_hat) + ADAM_EPS) + ADAM_WD * w), m, v


ADAM_TC = 512


def _adam_big(slots, w, m, v, cuts_columns, *, name):
    layers, n, nj = len(slots), slots[0].shape[1], D_MODEL // ADAM_TC

    def body(*refs):
        s_refs = refs[:layers]
        w_ref, m_ref, v_ref, g_ref, d_ref, nm_ref, nv_ref, acc_ref = refs[layers:]
        for ll in range(layers):
            @pl.when(pl.program_id(0) == ll)
            def _(s_ref=s_refs[ll]):
                g = s_ref[0].astype(F32)
                for s in range(1, N_DEV):
                    g = g + s_ref[s].astype(F32)
                acc_ref[...] = g

        g = acc_ref[...].T if cuts_columns else acc_ref[...]
        g_ref[...] = g
        d_ref[...], nm_ref[...], nv_ref[...] = _adamw(w_ref[...], g, m_ref[...], v_ref[...])

    def slot_spec(ll):
        return pl.BlockSpec((N_DEV, n, ADAM_TC),
                            lambda l, j: (0, 0, jnp.where(l < ll, 0, jnp.where(l > ll, nj - 1, j))))

    if cuts_columns:
        w_spec = pl.BlockSpec((None, ADAM_TC, n), lambda l, j: (l, j, 0))
    else:
        w_spec = pl.BlockSpec((None, n, ADAM_TC), lambda l, j: (l, 0, j))
    return pl.pallas_call(
        body,
        name=name,
        grid=(layers, nj),
        in_specs=[slot_spec(ll) for ll in range(layers)] + [w_spec] * 3,
        out_specs=[w_spec] * 4,
        out_shape=[jax.ShapeDtypeStruct(w.shape, F32)] * 4,
        scratch_shapes=[pltpu.VMEM((n, ADAM_TC), F32)],
        compiler_params=_cp("arbitrary", "arbitrary"),
    )(*slots, w, m, v)


def _adam_slabs(slots, ws, ms, vs):
    n = len(slots)

    def body(*refs):
        ins, outs = refs[:4 * n], refs[4 * n:]
        for k in range(n):
            s_ref, w_ref, m_ref, v_ref = ins[k], ins[n + k], ins[2 * n + k], ins[3 * n + k]
            g = s_ref[0]
            for s in range(1, N_DEV):
                g = g + s_ref[s]
            outs[4 * k][...] = g
            outs[4 * k + 1][...], outs[4 * k + 2][...], outs[4 * k + 3][...] = _adamw(w_ref[...], g, m_ref[...], v_ref[...])

    res = pl.pallas_call(
        body,
        name="small_adamw",
        out_shape=[jax.ShapeDtypeStruct(w.shape, F32) for w in ws for _ in range(4)],
        compiler_params=pltpu.CompilerParams(vmem_limit_bytes=VMEM_LIMIT_BYTES),
    )(*slots, *ws, *ms, *vs)
    return [res[4 * k:4 * k + 4] for k in range(n)]


def _adam_vecs(gs, ws, ms, vs):
    n = len(gs)

    def body(*refs):
        ins, outs = refs[:4 * n], refs[4 * n:]
        for k in range(n):
            outs[3 * k][...], outs[3 * k + 1][...], outs[3 * k + 2][...] = _adamw(
                ins[n + k][...], ins[k][...], ins[2 * n + k][...], ins[3 * n + k][...])

    res = pl.pallas_call(
        body,
        name="ln_adamw",
        out_shape=[jax.ShapeDtypeStruct(w.shape, F32) for w in ws for _ in range(3)],
        compiler_params=pltpu.CompilerParams(vmem_limit_bytes=VMEM_LIMIT_BYTES),
    )(*gs, *ws, *ms, *vs)
    return [res[3 * k:3 * k + 3] for k in range(n)]


SLAB_AT = dict(mem_norm=0, lb_logits=1, ffn1_norm=4, mix_norm=6, hgrn_gnorm=8, gmlp_ln_g=9, gmlp_ln_b=11,
               gmlp_b_s=13, ffn2_norm=14, final_norm=16)
SLAB_ROWS = 24
LOSS_ROW = 17
SMALL_SHARDED = ("gmlp_ln_g", "gmlp_ln_b")


def _pack_slab(parts, *, name, deps=()):
    flat, plan = [], []
    for pname, at in SLAB_AT.items():
        for a in parts.get(pname, ()):
            flat.append(a)
            plan.append((at, a.shape))
            at += max(1, a.shape[0] * a.shape[1] // D_MODEL)
    for a in parts.get("loss", ()):
        flat.append(a)
        plan.append((LOSS_ROW, a.shape))

    def body(*refs):
        o_ref = refs[-1]
        o_ref[...] = jnp.zeros_like(o_ref)
        for ref, (at, (r, w)) in zip(refs, plan):
            if w == D_MODEL or r == 1 and w < D_MODEL:
                o_ref[at:at + r, 0:w] = ref[...]
            elif w < D_MODEL:
                for j in range(r):
                    o_ref[at:at + 1, j * w:(j + 1) * w] = ref[j:j + 1, :]
            else:
                for j in range(w // D_MODEL):
                    o_ref[at + j:at + j + 1, :] = ref[:, j * D_MODEL:(j + 1) * D_MODEL]

    return pl.pallas_call(
        body,
        name=name,
        in_specs=[pl.BlockSpec(memory_space=pltpu.VMEM)] * len(flat) + [ANY_SPEC] * len(deps),
        out_shape=jax.ShapeDtypeStruct((SLAB_ROWS, D_MODEL), F32),
        compiler_params=pltpu.CompilerParams(vmem_limit_bytes=VMEM_LIMIT_BYTES),
    )(*flat, *deps)


def _unpack_slab(slab, shapes):
    out = {}
    for pname, at in SLAB_AT.items():
        if pname in SMALL_SHARDED:
            continue
        size = math.prod(shapes[pname])
        rows = max(1, size // D_MODEL)
        out[pname] = slab[at:at + rows].reshape(-1)[:size].reshape(shapes[pname])
    return out


def _take_weights(full, new):
    deps = full.pop("deps", ()) + new.pop("deps", ())
    full.update(new, deps=deps)


def _ffn_fwd(x, norm_g, block, layer, full, get_weights):
    _take_weights(full, get_weights((layer, f"{block}_in"), (x,)))
    y, h, z, act = _ffn_forward(x, norm_g, full[(f"{block}_w_in", layer)], full[(f"{block}_w_out", layer)], scale=0.5,
                                deps=full.pop("deps", ()), name=f"l{layer}_{block}")
    _take_weights(full, get_weights((layer, f"{block}_out"), (y,)))
    return y, (x, h, z, act)


def _ffn_bwd(dy, dy16, saved, norm_g, w_in_t, w_out, tag, deps=(), after_out_wgrad=None, before_in_wgrad=None):
    x, h, z, act = saved
    dw_out = _mm(act, dy16, ta=True, tm=1408, tn=D_MODEL, tk=N_TOK, out_dtype=BF16, scale=0.5, deps=deps,
                 name=f"{tag}_out_wgrad")
    sent = after_out_wgrad(dw_out) if after_out_wgrad is not None else ()
    dz, dx, dx16, dg = _ffn_dgrad(dy16, dy, w_out, z, w_in_t, x, norm_g, scale=0.5, deps=sent, name=f"{tag}_dgrad")
    wdeps = before_in_wgrad(dg) if before_in_wgrad is not None else ()
    dw_in_t = _planes_wgrad(dz, h, deps=wdeps, name=f"{tag}_in_wgrad")
    return dx, dx16, dg, dw_in_t, dw_out


def kernel(x, mem, mem_norm, lb_logits, ffn1_norm, ffn1_w_in, ffn1_w_out, mix_norm, mem_w_kv, hgrn_w_in, hgrn_gnorm, hgrn_w_out, gmlp_w_in, gmlp_ln_g, gmlp_ln_b, gmlp_w_s, gmlp_b_s, gmlp_w_out, ffn2_norm, ffn2_w_in, ffn2_w_out, final_norm, loss_target, m_mem_norm, m_lb_logits, m_ffn1_norm, m_ffn1_w_in, m_ffn1_w_out, m_mix_norm, m_mem_w_kv, m_hgrn_w_in, m_hgrn_gnorm, m_hgrn_w_out, m_gmlp_w_in, m_gmlp_ln_g, m_gmlp_ln_b, m_gmlp_w_s, m_gmlp_b_s, m_gmlp_w_out, m_ffn2_norm, m_ffn2_w_in, m_ffn2_w_out, m_final_norm, v_mem_norm, v_lb_logits, v_ffn1_norm, v_ffn1_w_in, v_ffn1_w_out, v_mix_norm, v_mem_w_kv, v_hgrn_w_in, v_hgrn_gnorm, v_hgrn_w_out, v_gmlp_w_in, v_gmlp_ln_g, v_gmlp_ln_b, v_gmlp_w_s, v_gmlp_b_s, v_gmlp_w_out, v_ffn2_norm, v_ffn2_w_in, v_ffn2_w_out, v_final_norm):
    weights = dict(mem_norm=mem_norm, lb_logits=lb_logits, ffn1_norm=ffn1_norm, ffn1_w_in=ffn1_w_in, ffn1_w_out=ffn1_w_out, mix_norm=mix_norm, mem_w_kv=mem_w_kv, hgrn_w_in=hgrn_w_in, hgrn_gnorm=hgrn_gnorm, hgrn_w_out=hgrn_w_out, gmlp_w_in=gmlp_w_in, gmlp_ln_g=gmlp_ln_g, gmlp_ln_b=gmlp_ln_b, gmlp_w_s=gmlp_w_s, gmlp_b_s=gmlp_b_s, gmlp_w_out=gmlp_w_out, ffn2_norm=ffn2_norm, ffn2_w_in=ffn2_w_in, ffn2_w_out=ffn2_w_out, final_norm=final_norm)
    mom_m = dict(mem_norm=m_mem_norm, lb_logits=m_lb_logits, ffn1_norm=m_ffn1_norm, ffn1_w_in=m_ffn1_w_in, ffn1_w_out=m_ffn1_w_out, mix_norm=m_mix_norm, mem_w_kv=m_mem_w_kv, hgrn_w_in=m_hgrn_w_in, hgrn_gnorm=m_hgrn_gnorm, hgrn_w_out=m_hgrn_w_out, gmlp_w_in=m_gmlp_w_in, gmlp_ln_g=m_gmlp_ln_g, gmlp_ln_b=m_gmlp_ln_b, gmlp_w_s=m_gmlp_w_s, gmlp_b_s=m_gmlp_b_s, gmlp_w_out=m_gmlp_w_out, ffn2_norm=m_ffn2_norm, ffn2_w_in=m_ffn2_w_in, ffn2_w_out=m_ffn2_w_out, final_norm=m_final_norm)
    mom_v = dict(mem_norm=v_mem_norm, lb_logits=v_lb_logits, ffn1_norm=v_ffn1_norm, ffn1_w_in=v_ffn1_w_in, ffn1_w_out=v_ffn1_w_out, mix_norm=v_mix_norm, mem_w_kv=v_mem_w_kv, hgrn_w_in=v_hgrn_w_in, hgrn_gnorm=v_hgrn_gnorm, hgrn_w_out=v_hgrn_w_out, gmlp_w_in=v_gmlp_w_in, gmlp_ln_g=v_gmlp_ln_g, gmlp_ln_b=v_gmlp_ln_b, gmlp_w_s=v_gmlp_w_s, gmlp_b_s=v_gmlp_b_s, gmlp_w_out=v_gmlp_w_out, ffn2_norm=v_ffn2_norm, ffn2_w_in=v_ffn2_w_in, ffn2_w_out=v_ffn2_w_out, final_norm=v_final_norm)
    order = list(weights)
    _, _, _, me = _mesh_pos()
    me_arr = jnp.reshape(me, (1,)).astype(jnp.int32)
    cuts = {name: c for name, c, _, _ in GROUPS}
    rows_already = tuple(name for name, c, _, n in GROUPS if c and n % 128)
    as_rows = lambda a: jnp.transpose(a, (0, 2, 1))
    for name in rows_already:
        weights[name], mom_m[name], mom_v[name] = as_rows(weights[name]), as_rows(mom_m[name]), as_rows(mom_v[name])
        cuts[name] = False

    mix1 =(("mem_w_kv", 1), ("gmlp_w_in", 0), ("gmlp_w_out", 0))
    gather_plan = (
        ((0, "ffn1_in"), _stage_pieces(0, "ffn1")),
        ((0, "mix_in"), _stage_pieces(0, "mix")),
        ((0, "ffn2_in"), _stage_pieces(0, "ffn2")),
        ((1, "ffn1_in"), _stage_pieces(1, "ffn1")),
        ((1, "mix_in"), mix1),
        ((1, "ffn2_in"), _stage_pieces(1, "ffn2")),
    )
    stage_of = {use: k for k, (use, _) in enumerate(gather_plan)}
    in_flight = {}

    def place(k, deps=()):
        pieces = gather_plan[k][1]
        lands = [_place_rows(weights[name], l, cuts[name], me_arr, deps=deps, name=f"place_{name}_{l}")
                 for name, l in pieces]
        if pieces is mix1:
            lands.append(_place_ln(gmlp_ln_g, gmlp_ln_b, me_arr))
        return lands

    placed = {0: place(0)}

    def start_chips(k, deps):
        lands = placed[k]
        send_sems, recv_sems, *thru, token = _copies_start(lands, lands, mode="gather_chips", deps=deps,
                                                           name=f"gather{k}_chips_start")
        in_flight[k] = (thru, send_sems, recv_sems)
        return token

    def pass_to_sibling(k, after):
        thru, send_sems, recv_sems = in_flight[k]
        outs = _copies_wait(thru, send_sems, recv_sems, after, n_lands=len(thru), mode="gather_chips",
                            name=f"gather{k}_chips_wait")
        send_sems, recv_sems, *thru, token = _copies_start(outs, outs, mode="gather_sibling",
                                                           name=f"gather{k}_sibling_start")
        in_flight[k] = (thru, send_sems, recv_sems)
        return token, token

    first_sent = start_chips(0, ())
    placed.update({k: place(k, (first_sent,)) for k in range(1, len(gather_plan))})
    placed_later = tuple(a for k in range(1, len(gather_plan)) for a in placed[k])
    points = [(i, p) for i in (0, 1) for p in ("ffn1_in", "ffn1_out", "mix_in", "mix_out", "ffn2_in", "ffn2_out")]
    pass_at = {j: points[points.index(use) - 1] for j, (use, _) in enumerate(gather_plan) if j}
    pass_at[1] = gather_plan[1][0]

    started = {0}
    early_start = (4, (0, "ffn2_in"))

    def get_weights(use, after):
        tokens, w = [], {}
        k = stage_of.get(use)

        def pass_on(j, after):
            token, landed = pass_to_sibling(j, after)
            tokens.append(token)
            if j + 1 < len(gather_plan) and j + 1 not in started:
                started.add(j + 1)
                tokens.append(start_chips(j + 1, (landed,)))

        if k == 0:
            pass_on(0, tuple(after) + placed_later)
        elif k is not None and pass_at[k] == use:
            pass_on(k, after)
        if k is not None:
            thru, send_sems, recv_sems = in_flight[k]
            outs = _copies_wait(thru, send_sems, recv_sems, after, n_lands=len(thru), mode="gather_sibling",
                                name=f"gather{k}_sibling_wait")
            after = (outs[0],)
            pieces = gather_plan[k][1]
            w = {p: o.reshape(N_DEV * o.shape[1], D_MODEL) for p, o in zip(pieces, outs)}
            if pieces is mix1:
                w["ln_g"] = outs[-1][:, 0, :].reshape(1, GM_WIDTH)
                w["ln_b"] = outs[-1][:, 1, :].reshape(1, GM_WIDTH)
        for j, at in pass_at.items():
            if at == use and j != k:
                pass_on(j, after)
        if use == early_start[1] and early_start[0] not in started:
            started.add(early_start[0])
            tokens.append(start_chips(early_start[0], after))
        w["deps"] = tuple(tokens)
        return w

    scatter = {}

    def put_grads(st, grads):
        if st in ("w_s", "small"):
            slab = grads.reshape(GM_GROUPS * GM_CHUNK, GM_CHUNK) if st == "w_s" else _pack_slab(grads, name="pack_small_grads")
            land = _place_slab(slab, me_arr, name=f"{st}_place")
            send_sems, recv_sems, *thru, token = _copies_start([land], [land], mode="gather_all", name=f"{st}_start")
            scatter[st] = (thru, send_sems, recv_sems)
            return (token,)
        views = [g.reshape(N_DEV, -1, D_MODEL) for g in grads.values()]
        recv = _place_own(views, me_arr, name=f"scatter_place_l{st[0]}_{st[1]}")
        send_sems, recv_sems, *thru, token = _copies_start(views, recv, mode="scatter",
                                                           name=f"scatter_start_l{st[0]}_{st[1]}")
        scatter[st] = (tuple(grads), thru, send_sems, recv_sems)
        return (token,)

    dx, last_sent = _step_local(
        x, mem, loss_target, get_weights, put_grads, mem_norm, lb_logits, ffn1_norm, mix_norm, hgrn_gnorm,
        gmlp_w_s, gmlp_b_s, ffn2_norm, final_norm)

    slots = {}

    def wait_grads(blk, after, last=False):
        for st, entry in scatter.items():
            if isinstance(st, tuple) and st[1].startswith(blk) and (st == (0, "ffn1_in")) == last:
                pieces, thru, send_sems, recv_sems = entry
                outs = _copies_wait(thru, send_sems, recv_sems, after, n_lands=len(thru) // 2, mode="scatter",
                                    name=f"scatter_wait_l{st[0]}_{st[1]}")
                slots.update(zip(pieces, outs))

    grad, delta, new_m, new_v = {}, {}, {}, {}

    def adam_groups(names):
        for name in names:
            layers = GROUP_LAYERS[name]
            grad[name], delta[name], new_m[name], new_v[name] = _adam_big(
                [slots[(name, l)] for l in range(layers)], weights[name], mom_m[name], mom_v[name], cuts[name],
                name=f"{name}_adamw")

    wait_grads("ffn2", (dx, *last_sent))
    adam_groups(("ffn2_w_in", "ffn2_w_out"))
    wait_grads("mix", (delta["ffn2_w_out"],))
    adam_groups(("mem_w_kv", "gmlp_w_in", "gmlp_w_out", "hgrn_w_in", "hgrn_w_out"))
    wait_grads("ffn1", (delta["hgrn_w_out"],))
    adam_groups(("ffn1_w_out",))

    def small_parts(src):
        parts = {n: [src[n].reshape(-1, src[n].shape[-1])] for n in SLAB_AT if n not in SMALL_SHARDED}
        return parts

    w_s_rows = lambda a: a.reshape(GM_GROUPS * GM_CHUNK, GM_CHUNK)
    small_done = (delta["hgrn_w_out"],)
    (slab_slots,) = _copies_wait(*scatter["small"], small_done, n_lands=1, mode="gather_all", name="small_wait")
    (ws_slots,) = _copies_wait(*scatter["w_s"], small_done, n_lands=1, mode="gather_all", name="w_s_wait")
    (g_slab, d_slab, nm_slab, nv_slab), (g_ws, d_ws, nm_ws, nv_ws) = _adam_slabs(
        [slab_slots, ws_slots],
        [_pack_slab(small_parts(weights), deps=(dx,), name="pack_small_w"), w_s_rows(gmlp_w_s)],
        [_pack_slab(small_parts(mom_m), deps=(dx,), name="pack_small_m"), w_s_rows(m_gmlp_w_s)],
        [_pack_slab(small_parts(mom_v), deps=(dx,), name="pack_small_v"), w_s_rows(v_gmlp_w_s)])
    shapes = {n: weights[n].shape for n in SLAB_AT}
    for out, slab, ws in ((grad, g_slab, g_ws), (delta, d_slab, d_ws), (new_m, nm_slab, nm_ws), (new_v, nv_slab, nv_ws)):
        out.update(_unpack_slab(slab, shapes))
        out["gmlp_w_s"] = ws.reshape(gmlp_w_s.shape)
    blk = GM_WIDTH // N_DEV
    g_ln = [lax.dynamic_slice(g_slab[SLAB_AT[n]:SLAB_AT[n] + 2].reshape(1, GM_WIDTH), (0, me * blk), (1, blk))
            for n in SMALL_SHARDED]
    ln_out = _adam_vecs(g_ln, [weights[n] for n in SMALL_SHARDED], [mom_m[n] for n in SMALL_SHARDED],
                        [mom_v[n] for n in SMALL_SHARDED])
    for n, g, (d, nm, nv) in zip(SMALL_SHARDED, g_ln, ln_out):
        grad[n], delta[n], new_m[n], new_v[n] = g, d, nm, nv

    wait_grads("ffn1", tuple(delta[n] for n in delta if n in GROUP_LAYERS) + (d_slab,), last=True)
    adam_groups(("ffn1_w_in",))

    for name in rows_already:
        for out in (grad, delta, new_m, new_v):
            out[name] = as_rows(out[name])
    loss = g_slab[LOSS_ROW, 0]
    grad_x = dx.reshape(B_LOC, SEQ, D_MODEL)
    return (loss, grad_x, *[grad[n] for n in order], *[delta[n] for n in order],
            *[new_m[n] for n in order], *[new_v[n] for n in order])


def _step_local(x, mem, loss_target, get_weights, put_grads, mem_norm, lb_logits, ffn1_norm, mix_norm, hgrn_gnorm,
                gmlp_w_s, gmlp_b_s, ffn2_norm, final_norm):
    w_s = gmlp_w_s[0]
    b_st = gmlp_b_s[0].T

    xs = x.reshape(N_TOK, D_MODEL)
    mem2d = mem.reshape(B_LOC * MEM_LEN, D_MODEL)
    mem_g = mem_norm.reshape(1, D_MODEL)
    saved, full = [], {}
    for i in range(2):
        xs, s_ffn1 = _ffn_fwd(xs, ffn1_norm[i:i + 1], "ffn1", i, full, get_weights)
        if i == 0:
            memn = _rms_fwd(mem2d, mem_g, deps=(xs,), name="mem_norm_fwd")
        _take_weights(full, get_weights((i, "mix_in"), (xs,)))
        mixer = "hgrn" if i == 0 else "gmlp"
        hm, zm = _norm_mm(xs, mix_norm[i:i + 1], full[(f"{mixer}_w_in", 0)], tm=1024, tn=1280, deps=full.pop("deps", ()),
                          name=f"l{i}_mix_in")
        kv = _mm(memn, full[("mem_w_kv", i)], tb=True, tm=512, tn=512, tk=D_MODEL, out_dtype=F32, name=f"l{i}_mem_kv")
        o_mem = _attn_fwd(zm, kv, name=f"l{i}_attn")
        if i == 0:
            cat, o_pre, s_all = _hgrn_fwd(zm, o_mem, lb_logits, hgrn_gnorm)
            mix_saved = (o_pre, s_all)
        else:
            cat = _gmlp_fwd(zm, o_mem, full["ln_g"], full["ln_b"], w_s, b_st)
            mix_saved = ()
        x_mix = xs
        _take_weights(full, get_weights((i, "mix_out"), (cat,)))
        xs = _mm(cat, full[(f"{mixer}_w_out", 0)], tm=512, tn=D_MODEL, tk=cat.shape[1], out_dtype=F32, res=xs,
                 deps=full.pop("deps", ()), name=f"l{i}_mix_out")
        xs, s_ffn2 = _ffn_fwd(xs, ffn2_norm[i:i + 1], "ffn2", i, full, get_weights)
        saved.append((s_ffn1, (x_mix, hm, kv, zm, cat, mix_saved), s_ffn2))

    dx, dx16, d_final, loss_part = _loss_head(xs, final_norm.reshape(1, D_MODEL), loss_target.reshape(N_TOK, D_MODEL))

    small = {"final_norm": [d_final], "loss": [loss_part]}
    d_ffn1, d_ffn2, d_mix = [None, None], [None, None], [None, None]
    dmemn = jnp.zeros((B_LOC * MEM_LEN, D_MODEL), F32)
    deps = ()
    for i in (1, 0):
        s_ffn1, (x_mix, hm, kv, zm, cat, mix_saved), s_ffn2 = saved[i]
        dx, dx16, d_ffn2[i], dw_in_t, dw_out = _ffn_bwd(
            dx, dx16, s_ffn2, ffn2_norm[i:i + 1], full[("ffn2_w_in", i)], full[("ffn2_w_out", i)], f"l{i}_ffn2", deps)
        deps = put_grads((i, "ffn2"), {("ffn2_w_in", i): dw_in_t, ("ffn2_w_out", i): dw_out})
        mixer = "hgrn" if i == 0 else "gmlp"
        w_in_t, w_out = full[(f"{mixer}_w_in", 0)], full[(f"{mixer}_w_out", 0)]
        width = cat.shape[1]
        g_mix = {}
        g_mix[(f"{mixer}_w_out", 0)] = _mm(cat, dx16, ta=True, tm=1024, tn=D_MODEL, tk=N_TOK, out_dtype=BF16,
                                           deps=deps, name=f"l{i}_mix_out_wgrad")
        dcat = _mm(dx16, w_out, tb=True, tm=1024, tn=width // 2, tk=D_MODEL, out_dtype=F32, name=f"l{i}_mix_out_dgrad")
        dq, dk, dv = _attn_bwd(zm, kv, dcat, do_off=width - XA_HEADS * XA_DIM, name=f"l{i}_attn_bwd")
        if i == 0:
            dzm, dlbl, dgn = _hgrn_bwd(zm, mix_saved[0], dcat, dq, mix_saved[1], lb_logits, hgrn_gnorm)
            small["lb_logits"], small["hgrn_gnorm"] = [dlbl], [dgn]
            deps = ()
        else:
            dzm, dws, dbt, dlng, dlnb = _gmlp_bwd(zm, dcat, dq, full["ln_g"], full["ln_b"], w_s, b_st)
            small["gmlp_b_s"], small["gmlp_ln_g"], small["gmlp_ln_b"] = [dbt.T], [dlng], [dlnb]
            deps = put_grads("w_s", dws)
        g_mix[(f"{mixer}_w_in", 0)] = _mm(dzm, hm, ta=True, tm=1024, tn=D_MODEL, tk=N_TOK, out_dtype=BF16, deps=deps,
                                          name=f"l{i}_mix_in_wgrad")
        dkv = jnp.concatenate([dk, dv], axis=1)
        g_mix[("mem_w_kv", i)] = _mm(dkv, memn, ta=True, tm=512, tn=D_MODEL, tk=B_LOC * MEM_LEN, out_dtype=BF16,
                                     name=f"l{i}_mem_kv_wgrad")
        deps = put_grads((i, "mix"), g_mix)
        dx, dx16, d_mix[i] = _dgrad_norm_bwd(dzm, w_in_t, x_mix, mix_norm[i:i + 1], dx, deps=deps,
                                             name=f"l{i}_mix_in_dgrad")
        dmemn = _mm(dkv, full[("mem_w_kv", i)], tm=B_LOC * MEM_LEN, tn=D_MODEL, tk=512, out_dtype=F32, res=dmemn,
                    name=f"l{i}_mem_kv_dgrad")
        def send_small(dg, i=i, dmemn=dmemn):
            d_ffn1[i] = dg
            _, _, dmem_g = _rms_bwd(mem2d, mem_g, dmemn, dmemn, name="mem_norm_bwd")
            small.update(mem_norm=[dmem_g], ffn1_norm=d_ffn1, ffn2_norm=d_ffn2, mix_norm=d_mix)
            return put_grads("small", small)

        if i == 0:
            send_out = lambda dw_out: put_grads((0, "ffn1_out"), {("ffn1_w_out", 0): dw_out})
            dx, dx16, d_ffn1[i], dw_in_t, _ = _ffn_bwd(
                dx, dx16, s_ffn1, ffn1_norm[i:i + 1], full[("ffn1_w_in", i)], full[("ffn1_w_out", i)], f"l{i}_ffn1",
                after_out_wgrad=send_out, before_in_wgrad=send_small)
            deps = put_grads((0, "ffn1_in"), {("ffn1_w_in", 0): dw_in_t})
        else:
            dx, dx16, d_ffn1[i], dw_in_t, dw_out = _ffn_bwd(
                dx, dx16, s_ffn1, ffn1_norm[i:i + 1], full[("ffn1_w_in", i)], full[("ffn1_w_out", i)], f"l{i}_ffn1")
            deps = put_grads((i, "ffn1"), {("ffn1_w_in", i): dw_in_t, ("ffn1_w_out", i): dw_out})
    return dx, deps
```

```python
import functools
import math

import jax
import jax.numpy as jnp
from jax import lax
from jax.experimental import pallas as pl
from jax.experimental.pallas import tpu as pltpu

F32 = jnp.float32
BF16 = jnp.bfloat16

D_MODEL = 1024
SEQ = 2048
B_LOC = 2
N_TOK = B_LOC * SEQ
MEM_LEN = 256
N_DEV = 8
EPS = 1e-6
D_FF = 2816
HG_HEADS = 8
HG_DIM = 128
HG_CHUNK = 64
HG_NCHUNK = SEQ // HG_CHUNK
GM_CHUNK = 128
GM_GROUPS = 8
GM_WIDTH = 2048
GM_GDIM = GM_WIDTH // GM_GROUPS
XA_HEADS = 4
XA_DIM = 256
XA_OFF = 4096

ADAM_LR = 0.001
ADAM_B1 = 0.9
ADAM_B2 = 0.999
ADAM_EPS = 1e-08
ADAM_WD = 0.01
ADAM_STEP = 10

VMEM_LIMIT_BYTES = 56 * 1024 * 1024
MESH_AXES = ("x", "y", "c")

GROUPS = (
    ("ffn1_w_in", True, 2, 704),
    ("ffn1_w_out", False, 2, 352),
    ("mem_w_kv", True, 2, 256),
    ("hgrn_w_in", True, 1, 640),
    ("hgrn_w_out", False, 1, 256),
    ("gmlp_w_in", True, 1, 640),
    ("gmlp_w_out", False, 1, 384),
    ("ffn2_w_in", True, 2, 704),
    ("ffn2_w_out", False, 2, 352),
)
GROUP_LAYERS = {name: layers for name, _, layers, _ in GROUPS}


def _stage_pieces(layer, block):
    if block == "mix":
        mixer = "hgrn" if layer == 0 else "gmlp"
        return (("mem_w_kv", layer), (f"{mixer}_w_in", 0), (f"{mixer}_w_out", 0))
    return ((f"{block}_w_in", layer), (f"{block}_w_out", layer))


ANY_SPEC = pl.BlockSpec(memory_space=pl.ANY)
HBM_SPEC = pl.BlockSpec(memory_space=pltpu.HBM)
SEM_SPEC = pl.BlockSpec(memory_space=pltpu.SEMAPHORE)


def _cp(*sem):
    return pltpu.CompilerParams(dimension_semantics=sem, vmem_limit_bytes=VMEM_LIMIT_BYTES)


def _sigmoid(x):
    return 0.5 * jnp.tanh(0.5 * x) + 0.5


def _gelu_parts(x):
    cdf = 0.5 * (1.0 + lax.erf(x * (1.0 / math.sqrt(2.0))))
    pdf = jnp.exp(-0.5 * x * x) * (1.0 / math.sqrt(2.0 * math.pi))
    return x * cdf, cdf + x * pdf


def _mm(a, b, *, ta=False, tb=False, tm, tn, tk, out_dtype, res=None, scale=1.0, deps=(), name):
    m, k = (a.shape[1], a.shape[0]) if ta else a.shape
    n, kb = b.shape if tb else (b.shape[1], b.shape[0])
    assert k == kb and m % tm == 0 and n % tn == 0 and k % tk == 0, (name, a.shape, b.shape)
    nk = k // tk
    dn = (((0 if ta else 1,), (1 if tb else 0,)), ((), ()))
    n_in = 2 + (res is not None) + len(deps)

    def body(*refs):
        a_ref, b_ref = refs[:2]
        r_ref = refs[2] if res is not None else None
        o_ref, scr = refs[n_in], refs[n_in + 1:]
        p = lax.dot_general(a_ref[...].astype(BF16), b_ref[...].astype(BF16), dn, preferred_element_type=F32)

        def finish(acc):
            if scale != 1.0:
                acc = scale * acc
            if r_ref is not None:
                acc = r_ref[...] + acc
            o_ref[...] = acc.astype(out_dtype)

        if nk == 1:
            finish(p)
        else:
            acc_ref = scr[0]
            kk = pl.program_id(2)

            @pl.when(kk == 0)
            def _():
                acc_ref[...] = p

            @pl.when(kk > 0)
            def _():
                acc_ref[...] += p

            @pl.when(kk == nk - 1)
            def _():
                finish(acc_ref[...])

    a_spec = pl.BlockSpec((tk, tm), lambda i, j, kk: (kk, i)) if ta else pl.BlockSpec((tm, tk), lambda i, j, kk: (i, kk))
    b_mode = dict(pipeline_mode=pl.Buffered(1)) if n == tn and nk == 1 else {}
    if tb:
        b_spec = pl.BlockSpec((tn, tk), lambda i, j, kk: (j, kk), **b_mode)
    else:
        b_spec = pl.BlockSpec((tk, tn), lambda i, j, kk: (kk, j), **b_mode)
    o_spec = pl.BlockSpec((tm, tn), lambda i, j, kk: (i, j))
    in_specs = [a_spec, b_spec] + ([o_spec] if res is not None else []) + [ANY_SPEC] * len(deps)
    args = (a, b) + ((res,) if res is not None else ()) + tuple(deps)
    return pl.pallas_call(
        body,
        name=name,
        grid=(m // tm, n // tn, nk),
        in_specs=in_specs,
        out_specs=o_spec,
        out_shape=jax.ShapeDtypeStruct((m, n), out_dtype),
        scratch_shapes=[pltpu.VMEM((tm, tn), F32)] if nk > 1 else [],
        compiler_params=_cp("parallel", "parallel", "arbitrary"),
    )(*args)


def _rms_fwd(x, g, *, name, deps=(), tm=512):
    rows = x.shape[0]

    def body(x_ref, g_ref, *rest):
        o_ref = rest[len(deps)]
        xv = x_ref[...]
        r = lax.rsqrt(jnp.mean(xv * xv, axis=-1, keepdims=True) + EPS)
        o_ref[...] = (xv * r * g_ref[...]).astype(BF16)

    row = pl.BlockSpec((tm, D_MODEL), lambda i: (i, 0))
    return pl.pallas_call(
        body,
        name=name,
        grid=(rows // tm,),
        in_specs=[row, pl.BlockSpec((1, D_MODEL), lambda i: (0, 0))] + [ANY_SPEC] * len(deps),
        out_specs=row,
        out_shape=jax.ShapeDtypeStruct((rows, D_MODEL), BF16),
        compiler_params=_cp("parallel"),
    )(x, g, *deps)


def _rms_bwd(x, g, dh, dres, *, name, deps=(), tm=512):
    rows = x.shape[0]

    def body(x_ref, g_ref, dh_ref, dres_ref, *rest):
        dx_ref, dx16_ref, dg_ref = rest[len(deps):]
        xv = x_ref[...]
        r = lax.rsqrt(jnp.mean(xv * xv, axis=-1, keepdims=True) + EPS)
        xhat = xv * r
        dhv = dh_ref[...]
        part = jnp.sum(dhv * xhat, axis=0, keepdims=True)

        @pl.when(pl.program_id(0) == 0)
        def _():
            dg_ref[...] = part

        @pl.when(pl.program_id(0) > 0)
        def _():
            dg_ref[...] += part

        dxh = dhv * g_ref[...]
        dx = dres_ref[...] + r * (dxh - xhat * jnp.mean(dxh * xhat, axis=-1, keepdims=True))
        dx_ref[...] = dx
        dx16_ref[...] = dx.astype(BF16)

    row = pl.BlockSpec((tm, D_MODEL), lambda i: (i, 0))
    vec = pl.BlockSpec((1, D_MODEL), lambda i: (0, 0))
    return pl.pallas_call(
        body,
        name=name,
        grid=(rows // tm,),
        in_specs=[row, vec, row, row] + [ANY_SPEC] * len(deps),
        out_specs=[row, row, vec],
        out_shape=[jax.ShapeDtypeStruct((rows, D_MODEL), F32), jax.ShapeDtypeStruct((rows, D_MODEL), BF16),
                   jax.ShapeDtypeStruct((1, D_MODEL), F32)],
        compiler_params=_cp("arbitrary"),
    )(x, g, dh, dres, *deps)


_NT = (((1,), (1,)), ((), ()))
_TN = (((0,), (0,)), ((), ()))


def _norm_mm(x, g, w_t, *, name, tm, tn, deps=()):
    rows = w_t.shape[0]
    nd = len(deps)

    def body(x_ref, g_ref, w_ref, *rest):
        h_ref, z_ref = rest[nd:]
        j = pl.program_id(1)

        @pl.when(j == 0)
        def _():
            xv = x_ref[...]
            r = lax.rsqrt(jnp.mean(xv * xv, axis=-1, keepdims=True) + EPS)
            h_ref[...] = (xv * r * g_ref[...]).astype(BF16)

        w = w_ref[pl.ds(pl.multiple_of(j * tn, tn), tn), :]
        z_ref[...] = lax.dot_general(h_ref[...], w, _NT, preferred_element_type=F32)

    row = pl.BlockSpec((tm, D_MODEL), lambda i, j: (i, 0))
    return pl.pallas_call(
        body,
        name=name,
        grid=(N_TOK // tm, rows // tn),
        in_specs=[row, pl.BlockSpec((1, D_MODEL), lambda i, j: (0, 0)),
                  pl.BlockSpec((rows, D_MODEL), lambda i, j: (0, 0), pipeline_mode=pl.Buffered(1))] + [ANY_SPEC] * nd,
        out_specs=[row, pl.BlockSpec((tm, tn), lambda i, j: (i, j))],
        out_shape=[jax.ShapeDtypeStruct((N_TOK, D_MODEL), BF16), jax.ShapeDtypeStruct((N_TOK, rows), F32)],
        compiler_params=_cp("parallel", "arbitrary"),
    )(x, g, w_t, *deps)


def _ffn_forward(x, g, w_in_t, w_out, *, scale, name, deps=(), tm=256, tn=1408):
    nd = len(deps)

    def body(x_ref, g_ref, wi_ref, wo_ref, *rest):
        y_ref, h_ref, z_ref, act_ref = rest[nd:]
        xv = x_ref[...]
        r = lax.rsqrt(jnp.mean(xv * xv, axis=-1, keepdims=True) + EPS)
        h = (xv * r * g_ref[...]).astype(BF16)
        h_ref[...] = h
        for j in range(D_FF // tn):
            cols = slice(j * tn, (j + 1) * tn)
            gate = lax.dot_general(h, wi_ref[j * tn:(j + 1) * tn, :], _NT, preferred_element_type=F32)
            up = lax.dot_general(h, wi_ref[D_FF + j * tn:D_FF + (j + 1) * tn, :], _NT, preferred_element_type=F32)
            s = _sigmoid(gate)
            silu = gate * s
            z_ref[0, :, cols] = (up * (s + silu * (1.0 - s))).astype(BF16)
            z_ref[1, :, cols] = silu.astype(BF16)
            act_ref[:, cols] = (silu * up).astype(BF16)
        y_ref[...] = xv + scale * jnp.dot(act_ref[...], wo_ref[...], preferred_element_type=F32)

    row = pl.BlockSpec((tm, D_MODEL), lambda i: (i, 0))
    whole = lambda rows: pl.BlockSpec((rows, D_MODEL), lambda i: (0, 0), pipeline_mode=pl.Buffered(1))
    return pl.pallas_call(
        body,
        name=name,
        grid=(N_TOK // tm,),
        in_specs=[row, pl.BlockSpec((1, D_MODEL), lambda i: (0, 0)), whole(2 * D_FF), whole(D_FF)] + [ANY_SPEC] * nd,
        out_specs=[row, row, pl.BlockSpec((2, tm, D_FF), lambda i: (0, i, 0)), pl.BlockSpec((tm, D_FF), lambda i: (i, 0))],
        out_shape=[jax.ShapeDtypeStruct((N_TOK, D_MODEL), F32), jax.ShapeDtypeStruct((N_TOK, D_MODEL), BF16),
                   jax.ShapeDtypeStruct((2, N_TOK, D_FF), BF16), jax.ShapeDtypeStruct((N_TOK, D_FF), BF16)],
        compiler_params=_cp("parallel"),
    )(x, g, w_in_t, w_out, *deps)


def _ffn_dgrad(dy16, dres, w_out, z, w_in_t, x, g, *, scale, name, deps=(), tm=256, tn=1408):
    nd = len(deps)

    def body(dy_ref, dres_ref, wo_ref, z_ref, wi_ref, x_ref, g_ref, *rest):
        dz_ref, dx_ref, dx16_ref, dg_ref = rest[nd:]
        dy = dy_ref[...]
        for j in range(D_FF // tn):
            cols = slice(j * tn, (j + 1) * tn)
            da = lax.dot_general(dy, wo_ref[cols, :], _NT, preferred_element_type=F32) * scale
            dz_ref[0, :, cols] = (da * z_ref[0, :, cols].astype(F32)).astype(BF16)
            dz_ref[1, :, cols] = (da * z_ref[1, :, cols].astype(F32)).astype(BF16)
        dh = jnp.dot(dz_ref[0], wi_ref[:D_FF, :], preferred_element_type=F32) + jnp.dot(
            dz_ref[1], wi_ref[D_FF:, :], preferred_element_type=F32)
        xv = x_ref[...]
        r = lax.rsqrt(jnp.mean(xv * xv, axis=-1, keepdims=True) + EPS)
        xhat = xv * r
        part = jnp.sum(dh * xhat, axis=0, keepdims=True)

        @pl.when(pl.program_id(0) == 0)
        def _():
            dg_ref[...] = part

        @pl.when(pl.program_id(0) > 0)
        def _():
            dg_ref[...] += part

        dxh = dh * g_ref[...]
        dx = dres_ref[...] + r * (dxh - xhat * jnp.mean(dxh * xhat, axis=-1, keepdims=True))
        dx_ref[...] = dx
        dx16_ref[...] = dx.astype(BF16)

    row = pl.BlockSpec((tm, D_MODEL), lambda i: (i, 0))
    vec = pl.BlockSpec((1, D_MODEL), lambda i: (0, 0))
    planes = pl.BlockSpec((2, tm, D_FF), lambda i: (0, i, 0))
    whole = lambda rows: pl.BlockSpec((rows, D_MODEL), lambda i: (0, 0), pipeline_mode=pl.Buffered(1))
    return pl.pallas_call(
        body,
        name=name,
        grid=(N_TOK // tm,),
        in_specs=[row, row, whole(D_FF), planes, whole(2 * D_FF), row, vec] + [ANY_SPEC] * nd,
        out_specs=[planes, row, row, vec],
        out_shape=[jax.ShapeDtypeStruct((2, N_TOK, D_FF), BF16), jax.ShapeDtypeStruct((N_TOK, D_MODEL), F32),
                   jax.ShapeDtypeStruct((N_TOK, D_MODEL), BF16), jax.ShapeDtypeStruct((1, D_MODEL), F32)],
        compiler_params=_cp("arbitrary"),
    )(dy16, dres, w_out, z, w_in_t, x, g, *deps)


def _planes_wgrad(dz, h, *, name, deps=(), tm=1408):
    per_plane = D_FF // tm

    def body(a_ref, b_ref, *rest):
        o_ref = rest[len(deps)]
        o_ref[...] = lax.dot_general(a_ref[...], b_ref[...], _TN, preferred_element_type=F32).astype(BF16)

    return pl.pallas_call(
        body,
        name=name,
        grid=(2 * per_plane,),
        in_specs=[pl.BlockSpec((None, N_TOK, tm),
                               lambda i: (jnp.where(i < per_plane, 0, 1), 0, jnp.where(i < per_plane, i, i - per_plane))),
                  pl.BlockSpec((N_TOK, D_MODEL), lambda i: (0, 0), pipeline_mode=pl.Buffered(1))] + [ANY_SPEC] * len(deps),
        out_specs=pl.BlockSpec((tm, D_MODEL), lambda i: (i, 0)),
        out_shape=jax.ShapeDtypeStruct((2 * D_FF, D_MODEL), BF16),
        compiler_params=_cp("parallel"),
    )(dz, h, *deps)


def _dgrad_norm_bwd(dz, w_t, x, g, dres, *, name, deps=(), tm=512):
    rows = w_t.shape[0]
    nd = len(deps)

    def body(a_ref, b_ref, x_ref, g_ref, dres_ref, *rest):
        dx_ref, dx16_ref, dg_ref = rest[nd:]
        dh = jnp.dot(a_ref[...], b_ref[...], preferred_element_type=F32)
        xv = x_ref[...]
        r = lax.rsqrt(jnp.mean(xv * xv, axis=-1, keepdims=True) + EPS)
        xhat = xv * r
        part = jnp.sum(dh * xhat, axis=0, keepdims=True)

        @pl.when(pl.program_id(0) == 0)
        def _():
            dg_ref[...] = part

        @pl.when(pl.program_id(0) > 0)
        def _():
            dg_ref[...] += part

        dxh = dh * g_ref[...]
        dx = dres_ref[...] + r * (dxh - xhat * jnp.mean(dxh * xhat, axis=-1, keepdims=True))
        dx_ref[...] = dx
        dx16_ref[...] = dx.astype(BF16)

    a_spec = pl.BlockSpec((tm, rows), lambda i: (i, 0))
    row = pl.BlockSpec((tm, D_MODEL), lambda i: (i, 0))
    vec = pl.BlockSpec((1, D_MODEL), lambda i: (0, 0))
    return pl.pallas_call(
        body,
        name=name,
        grid=(N_TOK // tm,),
        in_specs=[a_spec, pl.BlockSpec((rows, D_MODEL), lambda i: (0, 0), pipeline_mode=pl.Buffered(1)), row, vec, row]
        + [ANY_SPEC] * nd,
        out_specs=[row, row, vec],
        out_shape=[jax.ShapeDtypeStruct((N_TOK, D_MODEL), F32), jax.ShapeDtypeStruct((N_TOK, D_MODEL), BF16),
                   jax.ShapeDtypeStruct((1, D_MODEL), F32)],
        compiler_params=_cp("arbitrary"),
    )(dz, w_t, x, g, dres, *deps)


def _loss_head(x, g, target, *, tm=512):
    def body(x_ref, g_ref, t_ref, dx_ref, dx16_ref, dg_ref, loss_ref):
        xv = x_ref[...]
        gv = g_ref[...]
        r = lax.rsqrt(jnp.mean(xv * xv, axis=-1, keepdims=True) + EPS)
        xhat = xv * r
        err = xhat * gv - t_ref[...]
        loss_part = jnp.zeros((1, 128), F32) + 0.5 * jnp.sum(jnp.mean(err * err, axis=-1, keepdims=True))
        dy = err * (1.0 / D_MODEL)
        dg_part = jnp.sum(dy * xhat, axis=0, keepdims=True)

        @pl.when(pl.program_id(0) == 0)
        def _():
            dg_ref[...] = dg_part
            loss_ref[...] = loss_part

        @pl.when(pl.program_id(0) > 0)
        def _():
            dg_ref[...] += dg_part
            loss_ref[...] += loss_part

        dxh = dy * gv
        dx = r * (dxh - xhat * jnp.mean(dxh * xhat, axis=-1, keepdims=True))
        dx_ref[...] = dx
        dx16_ref[...] = dx.astype(BF16)

    row = pl.BlockSpec((tm, D_MODEL), lambda i: (i, 0))
    vec = pl.BlockSpec((1, D_MODEL), lambda i: (0, 0))
    return pl.pallas_call(
        body,
        name="loss_head",
        grid=(N_TOK // tm,),
        in_specs=[row, vec, row],
        out_specs=[row, row, vec, pl.BlockSpec((1, 128), lambda i: (0, 0))],
        out_shape=[
            jax.ShapeDtypeStruct((N_TOK, D_MODEL), F32),
            jax.ShapeDtypeStruct((N_TOK, D_MODEL), BF16),
            jax.ShapeDtypeStruct((1, D_MODEL), F32),
            jax.ShapeDtypeStruct((1, 128), F32),
        ],
        compiler_params=_cp("arbitrary"),
    )(x, g, target)


XA_TQ = 2048
XA_SCALE = XA_DIM ** -0.5


def _attn_probs(q16, k16):
    s = lax.dot_general(q16, k16, _NT, preferred_element_type=F32) * XA_SCALE
    e = jnp.exp(s - jnp.max(s, axis=-1, keepdims=True))
    return e / jnp.sum(e, axis=-1, keepdims=True)


def _attn_fwd(z, kv, *, name):
    nt = SEQ // XA_TQ

    def body(q_ref, k_ref, v_ref, o_ref):
        p = _attn_probs(q_ref[...].astype(BF16), k_ref[...].astype(BF16))
        o_ref[...] = jnp.dot(p.astype(BF16), v_ref[...].astype(BF16), preferred_element_type=F32).astype(BF16)

    return pl.pallas_call(
        body,
        name=name,
        grid=(B_LOC, XA_HEADS, nt),
        in_specs=[
            pl.BlockSpec((XA_TQ, XA_DIM), lambda b, h, t: (b * nt + t, XA_OFF // XA_DIM + h)),
            pl.BlockSpec((MEM_LEN, XA_DIM), lambda b, h, t: (b, h)),
            pl.BlockSpec((MEM_LEN, XA_DIM), lambda b, h, t: (b, XA_HEADS + h)),
        ],
        out_specs=pl.BlockSpec((XA_TQ, XA_DIM), lambda b, h, t: (b * nt + t, h)),
        out_shape=jax.ShapeDtypeStruct((N_TOK, XA_HEADS * XA_DIM), BF16),
        compiler_params=_cp("parallel", "parallel", "arbitrary"),
    )(z, kv, kv)


def _attn_bwd(z, kv, dcat, *, do_off, name):
    nt = SEQ // XA_TQ

    def body(q_ref, k_ref, v_ref, do_ref, dq_ref, dk_ref, dv_ref):
        q16 = q_ref[...].astype(BF16)
        k16 = k_ref[...].astype(BF16)
        v16 = v_ref[...].astype(BF16)
        do16 = do_ref[...].astype(BF16)
        p = _attn_probs(q16, k16)
        dv_part = lax.dot_general(p.astype(BF16), do16, _TN, preferred_element_type=F32)
        dp = lax.dot_general(do16, v16, _NT, preferred_element_type=F32)
        ds16 = (p * (dp - jnp.sum(dp * p, axis=-1, keepdims=True)) * XA_SCALE).astype(BF16)
        dq_ref[...] = jnp.dot(ds16, k16, preferred_element_type=F32).astype(BF16)
        dk_part = lax.dot_general(ds16, q16, _TN, preferred_element_type=F32)

        @pl.when(pl.program_id(2) == 0)
        def _():
            dk_ref[...] = dk_part
            dv_ref[...] = dv_part

        @pl.when(pl.program_id(2) > 0)
        def _():
            dk_ref[...] += dk_part
            dv_ref[...] += dv_part

    qspec = pl.BlockSpec((XA_TQ, XA_DIM), lambda b, h, t: (b * nt + t, XA_OFF // XA_DIM + h))
    kspec = lambda off: pl.BlockSpec((MEM_LEN, XA_DIM), lambda b, h, t: (b, off + h))
    return pl.pallas_call(
        body,
        name=name,
        grid=(B_LOC, XA_HEADS, nt),
        in_specs=[qspec, kspec(0), kspec(XA_HEADS),
                  pl.BlockSpec((XA_TQ, XA_DIM), lambda b, h, t: (b * nt + t, do_off // XA_DIM + h))],
        out_specs=[pl.BlockSpec((XA_TQ, XA_DIM), lambda b, h, t: (b * nt + t, h)), kspec(0), kspec(0)],
        out_shape=[
            jax.ShapeDtypeStruct((N_TOK, XA_HEADS * XA_DIM), BF16),
            jax.ShapeDtypeStruct((B_LOC * MEM_LEN, XA_HEADS * XA_DIM), F32),
            jax.ShapeDtypeStruct((B_LOC * MEM_LEN, XA_HEADS * XA_DIM), F32),
        ],
        compiler_params=_cp("parallel", "parallel", "arbitrary"),
    )(z, kv, kv, dcat)


def _tril(n):
    return lax.broadcasted_iota(jnp.int32, (n, n), 0) >= lax.broadcasted_iota(jnp.int32, (n, n), 1)


def _lower_bound(lbl):
    e = jnp.exp(lbl - jnp.max(lbl, axis=0, keepdims=True))
    p = e / jnp.sum(e, axis=0, keepdims=True)
    return p[0:1, :], p


def _hgrn_gates(zq, zf, lb, tril_f):
    sig = _sigmoid(zf)
    f = lb + (1.0 - lb) * sig
    kk = 1.0 - f
    sq = _sigmoid(zq)
    q = zq * sq
    b = jnp.dot(tril_f, jnp.log(f), preferred_element_type=F32, precision=lax.Precision.HIGHEST)
    bl = b[HG_CHUNK - 1:HG_CHUNK, :]
    return q, sq, sig, f, kk, b, bl


HG_TB = 512
HG_CPB = HG_TB // HG_CHUNK
HG_NT = SEQ // HG_TB
HG_WIDTH = HG_HEADS * HG_DIM


def _head(h, section=0):
    return slice(section * HG_WIDTH + h * HG_DIM, section * HG_WIDTH + (h + 1) * HG_DIM)


def _hgrn_fwd(z, o_mem, lb_logits, gnorm):
    def body(zq_ref, zf_ref, zi_ref, zg_ref, omem_ref, lbl_ref, gn_ref, o_ref, opre_ref, sall_ref, st_ref):
        lb, _ = _lower_bound(lbl_ref[...])
        gn = gn_ref[...]
        mask = _tril(HG_CHUNK)
        tril_f = mask.astype(F32)
        o_ref[:, HG_WIDTH:] = omem_ref[...]

        @pl.when(pl.program_id(1) == 0)
        def _():
            st_ref[...] = jnp.zeros_like(st_ref)

        def chunk(c, carry):
            rows = pl.ds(pl.multiple_of(c * HG_CHUNK, HG_CHUNK), HG_CHUNK)
            q, _, _, _, kk, b, bl = _hgrn_gates(zq_ref[rows, :], zf_ref[rows, :], lb, tril_f)
            v16 = zi_ref[rows, :].astype(BF16)
            qd16 = (q * jnp.exp(b)).astype(BF16)
            ki16 = (kk * jnp.exp(-b)).astype(BF16)
            kd16 = (kk * jnp.exp(bl - b)).astype(BF16)
            ebl = jnp.exp(bl)
            zg = zg_ref[rows, :]
            gate = zg * _sigmoid(zg)
            for h in range(HG_HEADS):
                sl = _head(h)
                a = jnp.where(mask, lax.dot_general(qd16[:, sl], ki16[:, sl], _NT, preferred_element_type=F32), 0.0)
                st = st_ref[h]
                sall_ref[0, h, c] = st
                o = jnp.dot(a.astype(BF16), v16[:, sl], preferred_element_type=F32) + lax.dot_general(
                    qd16[:, sl], st.astype(BF16), _NT, preferred_element_type=F32)
                st_ref[h] = st * ebl[:, sl] + lax.dot_general(v16[:, sl], kd16[:, sl], _TN, preferred_element_type=F32)
                opre_ref[rows, sl] = o
                r = lax.rsqrt(jnp.mean(o * o, axis=-1, keepdims=True) + EPS)
                o_ref[rows, sl] = ((o * r * gn) * gate[:, sl]).astype(BF16)
            return carry

        lax.fori_loop(0, HG_CPB, chunk, 0, unroll=True)

    zspec = lambda s: pl.BlockSpec((HG_TB, HG_WIDTH), lambda b, t: (b * HG_NT + t, s))
    return pl.pallas_call(
        body,
        name="hgrn_fwd",
        grid=(B_LOC, HG_NT),
        in_specs=[zspec(0), zspec(1), zspec(2), zspec(3), zspec(0),
                  pl.BlockSpec((3, HG_WIDTH), lambda b, t: (0, 0)), pl.BlockSpec((1, HG_DIM), lambda b, t: (0, 0))],
        out_specs=[pl.BlockSpec((HG_TB, 2 * HG_WIDTH), lambda b, t: (b * HG_NT + t, 0)), zspec(0),
                   pl.BlockSpec((1, HG_HEADS, HG_CPB, HG_DIM, HG_DIM), lambda b, t: (b, 0, t, 0, 0))],
        out_shape=[
            jax.ShapeDtypeStruct((N_TOK, 2 * HG_WIDTH), BF16),
            jax.ShapeDtypeStruct((N_TOK, HG_WIDTH), F32),
            jax.ShapeDtypeStruct((B_LOC, HG_HEADS, HG_NCHUNK, HG_DIM, HG_DIM), F32),
        ],
        scratch_shapes=[pltpu.VMEM((HG_HEADS, HG_DIM, HG_DIM), F32)],
        compiler_params=_cp("parallel", "arbitrary"),
    )(z, z, z, z, o_mem, lb_logits, gnorm)


def _hgrn_bwd(z, opre, dcat, dq_mem, sall, lb_logits, gnorm):
    def body(zq_ref, zf_ref, zi_ref, zg_ref, opre_ref, dout_ref, dqm_ref, sall_ref, lbl_ref, gn_ref,
             dz_ref, dlbl_ref, dgn_ref, dst_ref, dlb_ref, dgn_acc, db_ref, dkk_ref, dbl_ref):
        b_id, t_id = pl.program_id(0), pl.program_id(1)
        lb, p = _lower_bound(lbl_ref[...])
        gn = gn_ref[...]
        mask = _tril(HG_CHUNK)
        tril_f = mask.astype(F32)
        dz_ref[:, 4 * HG_WIDTH:] = dqm_ref[...]

        @pl.when(t_id == 0)
        def _():
            dst_ref[...] = jnp.zeros_like(dst_ref)
            dlb_ref[...] = jnp.zeros_like(dlb_ref)

        @pl.when((b_id == 0) & (t_id == 0))
        def _():
            dgn_acc[...] = jnp.zeros_like(dgn_acc)

        def chunk(i, carry):
            c = HG_CPB - 1 - i
            rows = pl.ds(pl.multiple_of(c * HG_CHUNK, HG_CHUNK), HG_CHUNK)
            zq, zg = zq_ref[rows, :], zg_ref[rows, :]
            q, sq, sig, f, kk, b, bl = _hgrn_gates(zq, zf_ref[rows, :], lb, tril_f)
            v16 = zi_ref[rows, :].astype(BF16)
            eb, enb, ebl_b, ebl = jnp.exp(b), jnp.exp(-b), jnp.exp(bl - b), jnp.exp(bl)
            qd, ki, kd = q * eb, kk * enb, kk * ebl_b
            qd16, ki16, kd16 = qd.astype(BF16), ki.astype(BF16), kd.astype(BF16)
            o_all = opre_ref[rows, :]
            dout = dout_ref[rows, :]
            sg = _sigmoid(zg)
            d_on_all = dout * (zg * sg)
            dgate = dout * (sg * (1.0 + zg * (1.0 - sg)))
            dq_scale = eb * (sq * (1.0 + zq * (1.0 - sq)))
            for h in range(HG_HEADS):
                sl = _head(h)
                o = o_all[:, sl]
                r = lax.rsqrt(jnp.mean(o * o, axis=-1, keepdims=True) + EPS)
                ohat = o * r
                d_on = d_on_all[:, sl]
                dz_ref[rows, _head(h, 3)] = (dgate[:, sl] * (ohat * gn)).astype(BF16)
                dgn_acc[...] += jnp.sum(d_on * ohat, axis=0, keepdims=True)
                dohat = d_on * gn
                do16 = (r * (dohat - ohat * jnp.mean(dohat * ohat, axis=-1, keepdims=True))).astype(BF16)
                st = sall_ref[0, h, c]
                dst = dst_ref[h]
                st16, dst16 = st.astype(BF16), dst.astype(BF16)
                qd_h, ki_h, kd_h, v_h = qd16[:, sl], ki16[:, sl], kd16[:, sl], v16[:, sl]
                a16 = jnp.where(mask, lax.dot_general(qd_h, ki_h, _NT, preferred_element_type=F32), 0.0).astype(BF16)
                da16 = jnp.where(mask, lax.dot_general(do16, v_h, _NT, preferred_element_type=F32), 0.0).astype(BF16)
                dv = lax.dot_general(a16, do16, _TN, preferred_element_type=F32) + lax.dot_general(
                    kd_h, dst16, _NT, preferred_element_type=F32)
                dqd = jnp.dot(da16, ki_h, preferred_element_type=F32) + jnp.dot(do16, st16, preferred_element_type=F32)
                dki = lax.dot_general(da16, qd_h, _TN, preferred_element_type=F32)
                dkd = jnp.dot(v_h, dst16, preferred_element_type=F32)
                dbl_ref[:, sl] = jnp.sum(dkd * kd[:, sl], axis=0, keepdims=True) + ebl[:, sl] * jnp.sum(
                    st * dst, axis=0, keepdims=True)
                dst_ref[h] = dst * ebl[:, sl] + lax.dot_general(do16, qd_h, _TN, preferred_element_type=F32)
                dz_ref[rows, _head(h, 2)] = dv.astype(BF16)
                dz_ref[rows, sl] = (dqd * dq_scale[:, sl]).astype(BF16)
                dkk_ref[:, sl] = dki * enb[:, sl] + dkd * ebl_b[:, sl]
                db_ref[:, sl] = dqd * qd[:, sl] - dki * ki[:, sl] - dkd * kd[:, sl]
            dlogf = lax.dot_general(tril_f, db_ref[...], _TN, preferred_element_type=F32,
                                    precision=lax.Precision.HIGHEST) + dbl_ref[...]
            df = dlogf / f - dkk_ref[...]
            dz_ref[rows, HG_WIDTH:2 * HG_WIDTH] = (df * (1.0 - lb) * sig * (1.0 - sig)).astype(BF16)
            dlb_ref[...] += jnp.sum(df * (1.0 - sig), axis=0, keepdims=True)
            return carry

        lax.fori_loop(0, HG_CPB, chunk, 0, unroll=True)

        @pl.when(t_id == HG_NT - 1)
        def _():
            row0 = (lax.broadcasted_iota(jnp.int32, (3, HG_WIDTH), 0) == 0).astype(F32)
            dlbl_part = dlb_ref[...] * lb * (row0 - p)

            @pl.when(b_id == 0)
            def _():
                dlbl_ref[...] = dlbl_part

            @pl.when(b_id > 0)
            def _():
                dlbl_ref[...] += dlbl_part

            dgn_ref[...] = dgn_acc[...]

    rev = lambda b, t: b * HG_NT + HG_NT - 1 - t
    zspec = lambda s: pl.BlockSpec((HG_TB, HG_WIDTH), lambda b, t: (rev(b, t), s))
    return pl.pallas_call(
        body,
        name="hgrn_bwd",
        grid=(B_LOC, HG_NT),
        in_specs=[zspec(0), zspec(1), zspec(2), zspec(3), zspec(0), zspec(0), zspec(0),
                  pl.BlockSpec((1, HG_HEADS, HG_CPB, HG_DIM, HG_DIM), lambda b, t: (b, 0, HG_NT - 1 - t, 0, 0)),
                  pl.BlockSpec((3, HG_WIDTH), lambda b, t: (0, 0)), pl.BlockSpec((1, HG_DIM), lambda b, t: (0, 0))],
        out_specs=[pl.BlockSpec((HG_TB, 5 * HG_WIDTH), lambda b, t: (rev(b, t), 0)),
                   pl.BlockSpec((3, HG_WIDTH), lambda b, t: (0, 0)), pl.BlockSpec((1, HG_DIM), lambda b, t: (0, 0))],
        out_shape=[jax.ShapeDtypeStruct((N_TOK, 5 * HG_WIDTH), BF16),
                   jax.ShapeDtypeStruct((3, HG_WIDTH), F32), jax.ShapeDtypeStruct((1, HG_DIM), F32)],
        scratch_shapes=[pltpu.VMEM((HG_HEADS, HG_DIM, HG_DIM), F32), pltpu.VMEM((1, HG_WIDTH), F32),
                        pltpu.VMEM((1, HG_DIM), F32), pltpu.VMEM((HG_CHUNK, HG_WIDTH), F32),
                        pltpu.VMEM((HG_CHUNK, HG_WIDTH), F32), pltpu.VMEM((1, HG_WIDTH), F32)],
        compiler_params=_cp("arbitrary", "arbitrary"),
    )(z, z, z, z, opre, dcat, dq_mem, sall, lb_logits, gnorm)


GM_TM = 256


def _gmlp_norm(zv, ln_g, ln_b):
    gv, dgelu = _gelu_parts(zv)
    xc = gv - jnp.mean(gv, axis=-1, keepdims=True)
    rstd = lax.rsqrt(jnp.mean(xc * xc, axis=-1, keepdims=True) + EPS)
    vhat = xc * rstd
    return vhat * ln_g + ln_b, vhat, rstd, dgelu


def _gmlp_specs():
    half = lambda j: pl.BlockSpec((GM_TM, GM_WIDTH), lambda i: (i, j))
    vec = pl.BlockSpec((1, GM_WIDTH), lambda i: (0, 0))
    w = pl.BlockSpec((GM_GROUPS, GM_CHUNK, GM_CHUNK), lambda i: (0, 0, 0))
    bt = pl.BlockSpec((GM_CHUNK, GM_GROUPS), lambda i: (0, 0))
    return half, vec, w, bt


def _gmlp_fwd(z, o_mem, ln_g, ln_b, w_s, b_st):
    def body(zu_ref, zv_ref, omem_ref, g_ref, b_ref, w_ref, bt_ref, o_ref):
        o_ref[:, GM_WIDTH:] = omem_ref[...]
        u, _ = _gelu_parts(zu_ref[...])
        v, _, _, _ = _gmlp_norm(zv_ref[...], g_ref[...], b_ref[...])
        v16 = v.astype(BF16)
        mask = _tril(GM_CHUNK)
        bt = bt_ref[...]
        for g in range(GM_GROUPS):
            wm16 = jnp.where(mask, w_ref[g], 0.0).astype(BF16)
            cols = slice(g * GM_GDIM, (g + 1) * GM_GDIM)
            for c in range(GM_TM // GM_CHUNK):
                rows = slice(c * GM_CHUNK, (c + 1) * GM_CHUNK)
                mixed = jnp.dot(wm16, v16[rows, cols], preferred_element_type=F32) + bt[:, g:g + 1]
                o_ref[rows, cols] = (u[rows, cols] * mixed).astype(BF16)

    half, vec, w, bt = _gmlp_specs()
    return pl.pallas_call(
        body,
        name="gmlp_fwd",
        grid=(N_TOK // GM_TM,),
        in_specs=[half(0), half(1), pl.BlockSpec((GM_TM, XA_HEADS * XA_DIM), lambda i: (i, 0)), vec, vec, w, bt],
        out_specs=pl.BlockSpec((GM_TM, GM_WIDTH + XA_HEADS * XA_DIM), lambda i: (i, 0)),
        out_shape=jax.ShapeDtypeStruct((N_TOK, GM_WIDTH + XA_HEADS * XA_DIM), BF16),
        compiler_params=_cp("parallel"),
    )(z, z, o_mem, ln_g, ln_b, w_s, b_st)


def _gmlp_bwd(z, dcat, dq_mem, ln_g, ln_b, w_s, b_st):
    def body(zu_ref, zv_ref, dout_ref, dqm_ref, g_ref, b_ref, w_ref, bt_ref,
             dz_ref, dw_ref, dbt_ref, dg_ref, db_ref, dv_ref):
        dz_ref[:, 2 * GM_WIDTH:] = dqm_ref[...]
        @pl.when(pl.program_id(0) == 0)
        def _():
            dw_ref[...] = jnp.zeros_like(dw_ref)
            dbt_ref[...] = jnp.zeros_like(dbt_ref)
            dg_ref[...] = jnp.zeros_like(dg_ref)
            db_ref[...] = jnp.zeros_like(db_ref)

        zu = zu_ref[...]
        u, du_dz = _gelu_parts(zu)
        ln_g = g_ref[...]
        v, vhat, rstd, dgv_dz = _gmlp_norm(zv_ref[...], ln_g, b_ref[...])
        v16 = v.astype(BF16)
        dout = dout_ref[...]
        dmixed = dout * u
        dm16 = dmixed.astype(BF16)
        mask = _tril(GM_CHUNK)
        bt = bt_ref[...]
        group_id = lax.broadcasted_iota(jnp.int32, (1, GM_GROUPS), 1)
        dbt = jnp.zeros((GM_CHUNK, GM_GROUPS), F32)
        for g in range(GM_GROUPS):
            wm16 = jnp.where(mask, w_ref[g], 0.0).astype(BF16)
            cols = slice(g * GM_GDIM, (g + 1) * GM_GDIM)
            dw = jnp.zeros((GM_CHUNK, GM_CHUNK), F32)
            dbt_g = jnp.zeros((GM_CHUNK, 1), F32)
            for c in range(GM_TM // GM_CHUNK):
                rows = slice(c * GM_CHUNK, (c + 1) * GM_CHUNK)
                mixed = jnp.dot(wm16, v16[rows, cols], preferred_element_type=F32) + bt[:, g:g + 1]
                dz_ref[rows, cols] = (dout[rows, cols] * mixed * du_dz[rows, cols]).astype(BF16)
                dw += lax.dot_general(dm16[rows, cols], v16[rows, cols], _NT, preferred_element_type=F32)
                dbt_g += jnp.sum(dmixed[rows, cols], axis=-1, keepdims=True)
                dv_ref[rows, cols] = lax.dot_general(wm16, dm16[rows, cols], _TN, preferred_element_type=F32)
            dw_ref[g] += jnp.where(mask, dw, 0.0)
            dbt = dbt + dbt_g * (group_id == g).astype(F32)
        dbt_ref[...] += dbt
        dv = dv_ref[...]
        dg_ref[...] += jnp.sum(dv * vhat, axis=0, keepdims=True)
        db_ref[...] += jnp.sum(dv, axis=0, keepdims=True)
        dvh = dv * ln_g
        dgv = rstd * (dvh - jnp.mean(dvh, axis=-1, keepdims=True) - vhat * jnp.mean(dvh * vhat, axis=-1, keepdims=True))
        dz_ref[:, GM_WIDTH:2 * GM_WIDTH] = (dgv * dgv_dz).astype(BF16)

    half, vec, w, bt = _gmlp_specs()
    dz_width = 2 * GM_WIDTH + XA_HEADS * XA_DIM
    return pl.pallas_call(
        body,
        name="gmlp_bwd",
        grid=(N_TOK // GM_TM,),
        in_specs=[half(0), half(1), half(0), pl.BlockSpec((GM_TM, XA_HEADS * XA_DIM), lambda i: (i, 0)), vec, vec, w, bt],
        out_specs=[pl.BlockSpec((GM_TM, dz_width), lambda i: (i, 0)), w, bt, vec, vec],
        out_shape=[jax.ShapeDtypeStruct((N_TOK, dz_width), BF16),
                   jax.ShapeDtypeStruct((GM_GROUPS, GM_CHUNK, GM_CHUNK), F32),
                   jax.ShapeDtypeStruct((GM_CHUNK, GM_GROUPS), F32),
                   jax.ShapeDtypeStruct((1, GM_WIDTH), F32), jax.ShapeDtypeStruct((1, GM_WIDTH), F32)],
        scratch_shapes=[pltpu.VMEM((GM_TM, GM_WIDTH), F32)],
        compiler_params=_cp("arbitrary"),
    )(z, z, dcat, dq_mem, ln_g, ln_b, w_s, b_st)


def _own_slot(shape):
    return pl.BlockSpec((None,) + tuple(shape), lambda i, me_ref: (me_ref[0],) + (0,) * len(shape))


def _place_rows(w, layer, cuts_columns, me, *, name, deps=()):
    _, r, c = w.shape
    n = c if cuts_columns else r

    def body(me_ref, w_ref, *rest):
        o_ref = rest[len(deps)]
        wv = w_ref[...]
        o_ref[...] = (wv.T if cuts_columns else wv).astype(BF16)

    return pl.pallas_call(
        body,
        name=name,
        grid_spec=pltpu.PrefetchScalarGridSpec(
            num_scalar_prefetch=1, grid=(1,),
            in_specs=[pl.BlockSpec((None, r, c), lambda i, me_ref: (layer, 0, 0))] + [ANY_SPEC] * len(deps),
            out_specs=_own_slot((n, D_MODEL))),
        out_shape=jax.ShapeDtypeStruct((N_DEV, n, D_MODEL), BF16),
        compiler_params=_cp("arbitrary"),
    )(me, w, *deps)


def _place_ln(ln_g, ln_b, me):
    blk = ln_g.shape[1]

    def body(me_ref, g_ref, b_ref, o_ref):
        o_ref[...] = jnp.zeros_like(o_ref)
        o_ref[0:1, :] = g_ref[...]
        o_ref[1:2, :] = b_ref[...]

    vec = pl.BlockSpec((1, blk), lambda i, me_ref: (0, 0))
    return pl.pallas_call(
        body,
        name="place_ln",
        grid_spec=pltpu.PrefetchScalarGridSpec(
            num_scalar_prefetch=1, grid=(1,), in_specs=[vec, vec], out_specs=_own_slot((8, blk))),
        out_shape=jax.ShapeDtypeStruct((N_DEV, 8, blk), F32),
        compiler_params=_cp("arbitrary"),
    )(me, ln_g, ln_b)


def _place_slab(a, me, *, name):
    def body(me_ref, a_ref, o_ref):
        o_ref[...] = a_ref[...]

    return pl.pallas_call(
        body,
        name=name,
        grid_spec=pltpu.PrefetchScalarGridSpec(
            num_scalar_prefetch=1, grid=(1,),
            in_specs=[pl.BlockSpec(a.shape, lambda i, me_ref: (0, 0))], out_specs=_own_slot(a.shape)),
        out_shape=jax.ShapeDtypeStruct((N_DEV,) + a.shape, a.dtype),
        compiler_params=_cp("arbitrary"),
    )(me, a)


def _place_own(grads, me, *, name):
    k = len(grads)

    def body(me_ref, *refs):
        for src, dst in zip(refs[:k], refs[k:]):
            dst[...] = src[...]

    specs = [_own_slot(g.shape[1:]) for g in grads]
    return pl.pallas_call(
        body,
        name=name,
        grid_spec=pltpu.PrefetchScalarGridSpec(num_scalar_prefetch=1, grid=(1,), in_specs=specs, out_specs=specs),
        out_shape=[jax.ShapeDtypeStruct(g.shape, g.dtype) for g in grads],
        compiler_params=_cp("arbitrary"),
    )(me, *grads)


def _mesh_pos():
    x, y, c = (lax.axis_index(a) for a in MESH_AXES)
    return x, y, c, 4 * x + 2 * y + c


def _peer(x, y, c, r):
    px = 1 - x if r & 4 else x
    py = 1 - y if r & 2 else y
    pc = 1 - c if r & 1 else c
    return (px, py, pc), 4 * px + 2 * py + pc


RELATIONS = {"scatter": (1, 2, 3, 4, 5, 6, 7), "gather_all": (1, 2, 3, 4, 5, 6, 7), "gather_chips": (1, 2, 4, 6),
             "gather_sibling": (2, 4, 6)}


def _peer_copies(srcs, lands, send_sems, recv_sems, mode, waits):
    x, y, c, me = _mesh_pos()
    rel = RELATIONS[mode]
    pairs = []
    for ri, r in enumerate(rel):
        if mode == "gather_sibling":
            peer, _ = _peer(x, y, c, 1)
            _, sent_blk = _peer(x, y, c, r)
            _, got_blk = _peer(x, y, c, r ^ 1)
        else:
            peer, peer_blk = _peer(x, y, c, r)
            sent_blk, got_blk = (peer_blk if mode == "scatter" else me), peer_blk
        for k, (src, land) in enumerate(zip(srcs, lands)):
            idx = k * len(rel) + ri
            sems = dict(send_sem=send_sems.at[idx], recv_sem=recv_sems.at[idx], device_id=peer,
                        device_id_type=pl.DeviceIdType.MESH)
            dst_blk = sent_blk if mode == "gather_sibling" else me
            mine = pltpu.make_async_remote_copy(src_ref=src.at[sent_blk], dst_ref=land.at[dst_blk], **sems)
            theirs = pltpu.make_async_remote_copy(src_ref=src.at[sent_blk], dst_ref=land.at[got_blk], **sems) if waits else None
            pairs.append((mine, theirs))
    return pairs


DATAFLOW = pltpu.SideEffectType.DATAFLOW_SIDE_EFFECTING


def _in_hbm(a):
    return pltpu.with_memory_space_constraint(a, pltpu.HBM)


def _copies_start(srcs, lands, *, mode, name, deps=()):
    gather = mode != "scatter"
    arrs = list(lands) if gather else list(srcs) + list(lands)
    n, k, nd = len(arrs), len(lands), len(deps)

    def body(*refs):
        ins, send_sems, recv_sems, token = refs[:n], refs[n + nd], refs[n + nd + 1], refs[2 * n + nd + 2]
        src_refs, land_refs = (ins, ins) if gather else (ins[:k], ins[k:])
        for mine, _ in _peer_copies(src_refs, land_refs, send_sems, recv_sems, mode, waits=False):
            mine.start()
        token[...] = jnp.zeros_like(token)

    n_cp = k * len(RELATIONS[mode])
    return pl.pallas_call(
        body,
        name=name,
        in_specs=[HBM_SPEC] * n + [ANY_SPEC] * nd,
        out_specs=(SEM_SPEC, SEM_SPEC, *[HBM_SPEC] * n, pl.BlockSpec(memory_space=pltpu.VMEM)),
        out_shape=(pltpu.SemaphoreType.DMA((n_cp,)), pltpu.SemaphoreType.DMA((n_cp,)),
                   *[pltpu.HBM(a.shape, a.dtype) for a in arrs], jax.ShapeDtypeStruct((8, 128), F32)),
        input_output_aliases={i: 2 + i for i in range(n)},
        compiler_params=pltpu.CompilerParams(has_side_effects=DATAFLOW),
    )(*[_in_hbm(a) for a in arrs], *deps)


def _copies_wait(arrs, send_sems, recv_sems, after, *, n_lands, mode, name):
    n, k = len(arrs), n_lands
    gather = mode != "scatter"

    def body(*refs):
        ins, send_sems, recv_sems = refs[:n], refs[n], refs[n + 1]
        src_refs, land_refs = (ins, ins) if gather else (ins[:k], ins[k:])
        for mine, theirs in _peer_copies(src_refs, land_refs, send_sems, recv_sems, mode, waits=True):
            mine.wait_send()
            theirs.wait_recv()

    outs = pl.pallas_call(
        body,
        name=name,
        in_specs=[HBM_SPEC] * n + [SEM_SPEC, SEM_SPEC] + [ANY_SPEC] * len(after),
        out_specs=[HBM_SPEC] * n,
        out_shape=[pltpu.HBM(a.shape, a.dtype) for a in arrs],
        input_output_aliases={i: i for i in range(n)},
        compiler_params=pltpu.CompilerParams(has_side_effects=DATAFLOW),
    )(*arrs, send_sems, recv_sems, *after)
    return outs[n - k:]


def _adamw(w, g, m, v):
    m = ADAM_B1 * m + (1.0 - ADAM_B1) * g
    v = ADAM_B2 * v + (1.0 - ADAM_B2) * (g * g)
    m_hat = m / (1.0 - ADAM_B1 ** ADAM_STEP)
    v_hat = v / (1.0 - ADAM_B2 ** ADAM_STEP)
    return -ADAM_LR * (m_hat / (jnp.sqrt(v_hat) + ADAM_EPS) + ADAM_WD * w), m, v


ADAM_TC = 512


def _adam_big(slots, w, m, v, cuts_columns, *, name):
    layers, n, nj = len(slots), slots[0].shape[1], D_MODEL // ADAM_TC

    def body(*refs):
        s_refs = refs[:layers]
        w_ref, m_ref, v_ref, g_ref, d_ref, nm_ref, nv_ref, acc_ref = refs[layers:]
        for ll in range(layers):
            @pl.when(pl.program_id(0) == ll)
            def _(s_ref=s_refs[ll]):
                g = s_ref[0].astype(F32)
                for s in range(1, N_DEV):
                    g = g + s_ref[s].astype(F32)
                acc_ref[...] = g

        g = acc_ref[...].T if cuts_columns else acc_ref[...]
        g_ref[...] = g
        d_ref[...], nm_ref[...], nv_ref[...] = _adamw(w_ref[...], g, m_ref[...], v_ref[...])

    def slot_spec(ll):
        return pl.BlockSpec((N_DEV, n, ADAM_TC),
                            lambda l, j: (0, 0, jnp.where(l < ll, 0, jnp.where(l > ll, nj - 1, j))))

    if cuts_columns:
        w_spec = pl.BlockSpec((None, ADAM_TC, n), lambda l, j: (l, j, 0))
    else:
        w_spec = pl.BlockSpec((None, n, ADAM_TC), lambda l, j: (l, 0, j))
    return pl.pallas_call(
        body,
        name=name,
        grid=(layers, nj),
        in_specs=[slot_spec(ll) for ll in range(layers)] + [w_spec] * 3,
        out_specs=[w_spec] * 4,
        out_shape=[jax.ShapeDtypeStruct(w.shape, F32)] * 4,
        scratch_shapes=[pltpu.VMEM((n, ADAM_TC), F32)],
        compiler_params=_cp("arbitrary", "arbitrary"),
    )(*slots, w, m, v)


def _adam_slabs(slots, ws, ms, vs):
    n = len(slots)

    def body(*refs):
        ins, outs = refs[:4 * n], refs[4 * n:]
        for k in range(n):
            s_ref, w_ref, m_ref, v_ref = ins[k], ins[n + k], ins[2 * n + k], ins[3 * n + k]
            g = s_ref[0]
            for s in range(1, N_DEV):
                g = g + s_ref[s]
            outs[4 * k][...] = g
            outs[4 * k + 1][...], outs[4 * k + 2][...], outs[4 * k + 3][...] = _adamw(w_ref[...], g, m_ref[...], v_ref[...])

    res = pl.pallas_call(
        body,
        name="small_adamw",
        out_shape=[jax.ShapeDtypeStruct(w.shape, F32) for w in ws for _ in range(4)],
        compiler_params=pltpu.CompilerParams(vmem_limit_bytes=VMEM_LIMIT_BYTES),
    )(*slots, *ws, *ms, *vs)
    return [res[4 * k:4 * k + 4] for k in range(n)]


def _adam_vecs(gs, ws, ms, vs):
    n = len(gs)

    def body(*refs):
        ins, outs = refs[:4 * n], refs[4 * n:]
        for k in range(n):
            outs[3 * k][...], outs[3 * k + 1][...], outs[3 * k + 2][...] = _adamw(
                ins[n + k][...], ins[k][...], ins[2 * n + k][...], ins[3 * n + k][...])

    res = pl.pallas_call(
        body,
        name="ln_adamw",
        out_shape=[jax.ShapeDtypeStruct(w.shape, F32) for w in ws for _ in range(3)],
        compiler_params=pltpu.CompilerParams(vmem_limit_bytes=VMEM_LIMIT_BYTES),
    )(*gs, *ws, *ms, *vs)
    return [res[3 * k:3 * k + 3] for k in range(n)]


SLAB_AT = dict(mem_norm=0, lb_logits=1, ffn1_norm=4, mix_norm=6, hgrn_gnorm=8, gmlp_ln_g=9, gmlp_ln_b=11,
               gmlp_b_s=13, ffn2_norm=14, final_norm=16)
SLAB_ROWS = 24
LOSS_ROW = 17
SMALL_SHARDED = ("gmlp_ln_g", "gmlp_ln_b")


def _pack_slab(parts, *, name, deps=()):
    flat, plan = [], []
    for pname, at in SLAB_AT.items():
        for a in parts.get(pname, ()):
            flat.append(a)
            plan.append((at, a.shape))
            at += max(1, a.shape[0] * a.shape[1] // D_MODEL)
    for a in parts.get("loss", ()):
        flat.append(a)
        plan.append((LOSS_ROW, a.shape))

    def body(*refs):
        o_ref = refs[-1]
        o_ref[...] = jnp.zeros_like(o_ref)
        for ref, (at, (r, w)) in zip(refs, plan):
            if w == D_MODEL or r == 1 and w < D_MODEL:
                o_ref[at:at + r, 0:w] = ref[...]
            elif w < D_MODEL:
                for j in range(r):
                    o_ref[at:at + 1, j * w:(j + 1) * w] = ref[j:j + 1, :]
            else:
                for j in range(w // D_MODEL):
                    o_ref[at + j:at + j + 1, :] = ref[:, j * D_MODEL:(j + 1) * D_MODEL]

    return pl.pallas_call(
        body,
        name=name,
        in_specs=[pl.BlockSpec(memory_space=pltpu.VMEM)] * len(flat) + [ANY_SPEC] * len(deps),
        out_shape=jax.ShapeDtypeStruct((SLAB_ROWS, D_MODEL), F32),
        compiler_params=pltpu.CompilerParams(vmem_limit_bytes=VMEM_LIMIT_BYTES),
    )(*flat, *deps)


def _unpack_slab(slab, shapes):
    out = {}
    for pname, at in SLAB_AT.items():
        if pname in SMALL_SHARDED:
            continue
        size = math.prod(shapes[pname])
        rows = max(1, size // D_MODEL)
        out[pname] = slab[at:at + rows].reshape(-1)[:size].reshape(shapes[pname])
    return out


def _take_weights(full, new):
    deps = full.pop("deps", ()) + new.pop("deps", ())
    full.update(new, deps=deps)


def _ffn_fwd(x, norm_g, block, layer, full, get_weights):
    _take_weights(full, get_weights((layer, f"{block}_in"), (x,)))
    y, h, z, act = _ffn_forward(x, norm_g, full[(f"{block}_w_in", layer)], full[(f"{block}_w_out", layer)], scale=0.5,
                                deps=full.pop("deps", ()), name=f"l{layer}_{block}")
    _take_weights(full, get_weights((layer, f"{block}_out"), (y,)))
    return y, (x, h, z, act)


def _ffn_bwd(dy, dy16, saved, norm_g, w_in_t, w_out, tag, deps=(), after_out_wgrad=None, before_in_wgrad=None):
    x, h, z, act = saved
    dw_out = _mm(act, dy16, ta=True, tm=1408, tn=D_MODEL, tk=N_TOK, out_dtype=BF16, scale=0.5, deps=deps,
                 name=f"{tag}_out_wgrad")
    sent = after_out_wgrad(dw_out) if after_out_wgrad is not None else ()
    dz, dx, dx16, dg = _ffn_dgrad(dy16, dy, w_out, z, w_in_t, x, norm_g, scale=0.5, deps=sent, name=f"{tag}_dgrad")
    wdeps = before_in_wgrad(dg) if before_in_wgrad is not None else ()
    dw_in_t = _planes_wgrad(dz, h, deps=wdeps, name=f"{tag}_in_wgrad")
    return dx, dx16, dg, dw_in_t, dw_out


def kernel(x, mem, mem_norm, lb_logits, ffn1_norm, ffn1_w_in, ffn1_w_out, mix_norm, mem_w_kv, hgrn_w_in, hgrn_gnorm, hgrn_w_out, gmlp_w_in, gmlp_ln_g, gmlp_ln_b, gmlp_w_s, gmlp_b_s, gmlp_w_out, ffn2_norm, ffn2_w_in, ffn2_w_out, final_norm, loss_target, m_mem_norm, m_lb_logits, m_ffn1_norm, m_ffn1_w_in, m_ffn1_w_out, m_mix_norm, m_mem_w_kv, m_hgrn_w_in, m_hgrn_gnorm, m_hgrn_w_out, m_gmlp_w_in, m_gmlp_ln_g, m_gmlp_ln_b, m_gmlp_w_s, m_gmlp_b_s, m_gmlp_w_out, m_ffn2_norm, m_ffn2_w_in, m_ffn2_w_out, m_final_norm, v_mem_norm, v_lb_logits, v_ffn1_norm, v_ffn1_w_in, v_ffn1_w_out, v_mix_norm, v_mem_w_kv, v_hgrn_w_in, v_hgrn_gnorm, v_hgrn_w_out, v_gmlp_w_in, v_gmlp_ln_g, v_gmlp_ln_b, v_gmlp_w_s, v_gmlp_b_s, v_gmlp_w_out, v_ffn2_norm, v_ffn2_w_in, v_ffn2_w_out, v_final_norm):
    weights = dict(mem_norm=mem_norm, lb_logits=lb_logits, ffn1_norm=ffn1_norm, ffn1_w_in=ffn1_w_in, ffn1_w_out=ffn1_w_out, mix_norm=mix_norm, mem_w_kv=mem_w_kv, hgrn_w_in=hgrn_w_in, hgrn_gnorm=hgrn_gnorm, hgrn_w_out=hgrn_w_out, gmlp_w_in=gmlp_w_in, gmlp_ln_g=gmlp_ln_g, gmlp_ln_b=gmlp_ln_b, gmlp_w_s=gmlp_w_s, gmlp_b_s=gmlp_b_s, gmlp_w_out=gmlp_w_out, ffn2_norm=ffn2_norm, ffn2_w_in=ffn2_w_in, ffn2_w_out=ffn2_w_out, final_norm=final_norm)
    mom_m = dict(mem_norm=m_mem_norm, lb_logits=m_lb_logits, ffn1_norm=m_ffn1_norm, ffn1_w_in=m_ffn1_w_in, ffn1_w_out=m_ffn1_w_out, mix_norm=m_mix_norm, mem_w_kv=m_mem_w_kv, hgrn_w_in=m_hgrn_w_in, hgrn_gnorm=m_hgrn_gnorm, hgrn_w_out=m_hgrn_w_out, gmlp_w_in=m_gmlp_w_in, gmlp_ln_g=m_gmlp_ln_g, gmlp_ln_b=m_gmlp_ln_b, gmlp_w_s=m_gmlp_w_s, gmlp_b_s=m_gmlp_b_s, gmlp_w_out=m_gmlp_w_out, ffn2_norm=m_ffn2_norm, ffn2_w_in=m_ffn2_w_in, ffn2_w_out=m_ffn2_w_out, final_norm=m_final_norm)
    mom_v = dict(mem_norm=v_mem_norm, lb_logits=v_lb_logits, ffn1_norm=v_ffn1_norm, ffn1_w_in=v_ffn1_w_in, ffn1_w_out=v_ffn1_w_out, mix_norm=v_mix_norm, mem_w_kv=v_mem_w_kv, hgrn_w_in=v_hgrn_w_in, hgrn_gnorm=v_hgrn_gnorm, hgrn_w_out=v_hgrn_w_out, gmlp_w_in=v_gmlp_w_in, gmlp_ln_g=v_gmlp_ln_g, gmlp_ln_b=v_gmlp_ln_b, gmlp_w_s=v_gmlp_w_s, gmlp_b_s=v_gmlp_b_s, gmlp_w_out=v_gmlp_w_out, ffn2_norm=v_ffn2_norm, ffn2_w_in=v_ffn2_w_in, ffn2_w_out=v_ffn2_w_out, final_norm=v_final_norm)
    order = list(weights)
    _, _, _, me = _mesh_pos()
    me_arr = jnp.reshape(me, (1,)).astype(jnp.int32)
    cuts = {name: c for name, c, _, _ in GROUPS}
    rows_already = tuple(name for name, c, _, n in GROUPS if c and n % 128)
    as_rows = lambda a: jnp.transpose(a, (0, 2, 1))
    for name in rows_already:
        weights[name], mom_m[name], mom_v[name] = as_rows(weights[name]), as_rows(mom_m[name]), as_rows(mom_v[name])
        cuts[name] = False

    mix1 =(("mem_w_kv", 1), ("gmlp_w_in", 0), ("gmlp_w_out", 0))
    gather_plan = (
        ((0, "ffn1_in"), _stage_pieces(0, "ffn1")),
        ((0, "mix_in"), _stage_pieces(0, "mix")),
        ((0, "ffn2_in"), _stage_pieces(0, "ffn2")),
        ((1, "ffn1_in"), _stage_pieces(1, "ffn1")),
        ((1, "mix_in"), mix1),
        ((1, "ffn2_in"), _stage_pieces(1, "ffn2")),
    )
    stage_of = {use: k for k, (use, _) in enumerate(gather_plan)}
    in_flight = {}

    def place(k, deps=()):
        pieces = gather_plan[k][1]
        lands = [_place_rows(weights[name], l, cuts[name], me_arr, deps=deps, name=f"place_{name}_{l}")
                 for name, l in pieces]
        if pieces is mix1:
            lands.append(_place_ln(gmlp_ln_g, gmlp_ln_b, me_arr))
        return lands

    placed = {0: place(0)}

    def start_chips(k, deps):
        lands = placed[k]
        send_sems, recv_sems, *thru, token = _copies_start(lands, lands, mode="gather_chips", deps=deps,
                                                           name=f"gather{k}_chips_start")
        in_flight[k] = (thru, send_sems, recv_sems)
        return token

    def pass_to_sibling(k, after):
        thru, send_sems, recv_sems = in_flight[k]
        outs = _copies_wait(thru, send_sems, recv_sems, after, n_lands=len(thru), mode="gather_chips",
                            name=f"gather{k}_chips_wait")
        send_sems, recv_sems, *thru, token = _copies_start(outs, outs, mode="gather_sibling",
                                                           name=f"gather{k}_sibling_start")
        in_flight[k] = (thru, send_sems, recv_sems)
        return token, token

    first_sent = start_chips(0, ())
    placed.update({k: place(k, (first_sent,)) for k in range(1, len(gather_plan))})
    placed_later = tuple(a for k in range(1, len(gather_plan)) for a in placed[k])
    points = [(i, p) for i in (0, 1) for p in ("ffn1_in", "ffn1_out", "mix_in", "mix_out", "ffn2_in", "ffn2_out")]
    pass_at = {j: points[points.index(use) - 1] for j, (use, _) in enumerate(gather_plan) if j}
    pass_at[1] = gather_plan[1][0]

    started = {0}
    early_start = (4, (0, "ffn2_in"))

    def get_weights(use, after):
        tokens, w = [], {}
        k = stage_of.get(use)

        def pass_on(j, after):
            token, landed = pass_to_sibling(j, after)
            tokens.append(token)
            if j + 1 < len(gather_plan) and j + 1 not in started:
                started.add(j + 1)
                tokens.append(start_chips(j + 1, (landed,)))

        if k == 0:
            pass_on(0, tuple(after) + placed_later)
        elif k is not None and pass_at[k] == use:
            pass_on(k, after)
        if k is not None:
            thru, send_sems, recv_sems = in_flight[k]
            outs = _copies_wait(thru, send_sems, recv_sems, after, n_lands=len(thru), mode="gather_sibling",
                                name=f"gather{k}_sibling_wait")
            after = (outs[0],)
            pieces = gather_plan[k][1]
            w = {p: o.reshape(N_DEV * o.shape[1], D_MODEL) for p, o in zip(pieces, outs)}
            if pieces is mix1:
                w["ln_g"] = outs[-1][:, 0, :].reshape(1, GM_WIDTH)
                w["ln_b"] = outs[-1][:, 1, :].reshape(1, GM_WIDTH)
        for j, at in pass_at.items():
            if at == use and j != k:
                pass_on(j, after)
        if use == early_start[1] and early_start[0] not in started:
            started.add(early_start[0])
            tokens.append(start_chips(early_start[0], after))
        w["deps"] = tuple(tokens)
        return w

    scatter = {}

    def put_grads(st, grads):
        if st in ("w_s", "small"):
            slab = grads.reshape(GM_GROUPS * GM_CHUNK, GM_CHUNK) if st == "w_s" else _pack_slab(grads, name="pack_small_grads")
            land = _place_slab(slab, me_arr, name=f"{st}_place")
            send_sems, recv_sems, *thru, token = _copies_start([land], [land], mode="gather_all", name=f"{st}_start")
            scatter[st] = (thru, send_sems, recv_sems)
            return (token,)
        views = [g.reshape(N_DEV, -1, D_MODEL) for g in grads.values()]
        recv = _place_own(views, me_arr, name=f"scatter_place_l{st[0]}_{st[1]}")
        send_sems, recv_sems, *thru, token = _copies_start(views, recv, mode="scatter",
                                                           name=f"scatter_start_l{st[0]}_{st[1]}")
        scatter[st] = (tuple(grads), thru, send_sems, recv_sems)
        return (token,)

    dx, last_sent = _step_local(
        x, mem, loss_target, get_weights, put_grads, mem_norm, lb_logits, ffn1_norm, mix_norm, hgrn_gnorm,
        gmlp_w_s, gmlp_b_s, ffn2_norm, final_norm)

    slots = {}

    def wait_grads(blk, after, last=False):
        for st, entry in scatter.items():
            if isinstance(st, tuple) and st[1].startswith(blk) and (st == (0, "ffn1_in")) == last:
                pieces, thru, send_sems, recv_sems = entry
                outs = _copies_wait(thru, send_sems, recv_sems, after, n_lands=len(thru) // 2, mode="scatter",
                                    name=f"scatter_wait_l{st[0]}_{st[1]}")
                slots.update(zip(pieces, outs))

    grad, delta, new_m, new_v = {}, {}, {}, {}

    def adam_groups(names):
        for name in names:
            layers = GROUP_LAYERS[name]
            grad[name], delta[name], new_m[name], new_v[name] = _adam_big(
                [slots[(name, l)] for l in range(layers)], weights[name], mom_m[name], mom_v[name], cuts[name],
                name=f"{name}_adamw")

    wait_grads("ffn2", (dx, *last_sent))
    adam_groups(("ffn2_w_in", "ffn2_w_out"))
    wait_grads("mix", (delta["ffn2_w_out"],))
    adam_groups(("mem_w_kv", "gmlp_w_in", "gmlp_w_out", "hgrn_w_in", "hgrn_w_out"))
    wait_grads("ffn1", (delta["hgrn_w_out"],))
    adam_groups(("ffn1_w_out",))

    def small_parts(src):
        parts = {n: [src[n].reshape(-1, src[n].shape[-1])] for n in SLAB_AT if n not in SMALL_SHARDED}
        return parts

    w_s_rows = lambda a: a.reshape(GM_GROUPS * GM_CHUNK, GM_CHUNK)
    small_done = (delta["hgrn_w_out"],)
    (slab_slots,) = _copies_wait(*scatter["small"], small_done, n_lands=1, mode="gather_all", name="small_wait")
    (ws_slots,) = _copies_wait(*scatter["w_s"], small_done, n_lands=1, mode="gather_all", name="w_s_wait")
    (g_slab, d_slab, nm_slab, nv_slab), (g_ws, d_ws, nm_ws, nv_ws) = _adam_slabs(
        [slab_slots, ws_slots],
        [_pack_slab(small_parts(weights), deps=(dx,), name="pack_small_w"), w_s_rows(gmlp_w_s)],
        [_pack_slab(small_parts(mom_m), deps=(dx,), name="pack_small_m"), w_s_rows(m_gmlp_w_s)],
        [_pack_slab(small_parts(mom_v), deps=(dx,), name="pack_small_v"), w_s_rows(v_gmlp_w_s)])
    shapes = {n: weights[n].shape for n in SLAB_AT}
    for out, slab, ws in ((grad, g_slab, g_ws), (delta, d_slab, d_ws), (new_m, nm_slab, nm_ws), (new_v, nv_slab, nv_ws)):
        out.update(_unpack_slab(slab, shapes))
        out["gmlp_w_s"] = ws.reshape(gmlp_w_s.shape)
    blk = GM_WIDTH // N_DEV
    g_ln = [lax.dynamic_slice(g_slab[SLAB_AT[n]:SLAB_AT[n] + 2].reshape(1, GM_WIDTH), (0, me * blk), (1, blk))
            for n in SMALL_SHARDED]
    ln_out = _adam_vecs(g_ln, [weights[n] for n in SMALL_SHARDED], [mom_m[n] for n in SMALL_SHARDED],
                        [mom_v[n] for n in SMALL_SHARDED])
    for n, g, (d, nm, nv) in zip(SMALL_SHARDED, g_ln, ln_out):
        grad[n], delta[n], new_m[n], new_v[n] = g, d, nm, nv

    wait_grads("ffn1", tuple(delta[n] for n in delta if n in GROUP_LAYERS) + (d_slab,), last=True)
    adam_groups(("ffn1_w_in",))

    for name in rows_already:
        for out in (grad, delta, new_m, new_v):
            out[name] = as_rows(out[name])
    loss = g_slab[LOSS_ROW, 0]
    grad_x = dx.reshape(B_LOC, SEQ, D_MODEL)
    return (loss, grad_x, *[grad[n] for n in order], *[delta[n] for n in order],
            *[new_m[n] for n in order], *[new_v[n] for n in order])


def _step_local(x, mem, loss_target, get_weights, put_grads, mem_norm, lb_logits, ffn1_norm, mix_norm, hgrn_gnorm,
                gmlp_w_s, gmlp_b_s, ffn2_norm, final_norm):
    w_s = gmlp_w_s[0]
    b_st = gmlp_b_s[0].T

    xs = x.reshape(N_TOK, D_MODEL)
    mem2d = mem.reshape(B_LOC * MEM_LEN, D_MODEL)
    mem_g = mem_norm.reshape(1, D_MODEL)
    saved, full = [], {}
    for i in range(2):
        xs, s_ffn1 = _ffn_fwd(xs, ffn1_norm[i:i + 1], "ffn1", i, full, get_weights)
        if i == 0:
            memn = _rms_fwd(mem2d, mem_g, deps=(xs,), name="mem_norm_fwd")
        _take_weights(full, get_weights((i, "mix_in"), (xs,)))
        mixer = "hgrn" if i == 0 else "gmlp"
        hm, zm = _norm_mm(xs, mix_norm[i:i + 1], full[(f"{mixer}_w_in", 0)], tm=512, tn=2560, deps=full.pop("deps", ()),
                          name=f"l{i}_mix_in")
        kv = _mm(memn, full[("mem_w_kv", i)], tb=True, tm=512, tn=512, tk=D_MODEL, out_dtype=F32, name=f"l{i}_mem_kv")
        o_mem = _attn_fwd(zm, kv, name=f"l{i}_attn")
        if i == 0:
            cat, o_pre, s_all = _hgrn_fwd(zm, o_mem, lb_logits, hgrn_gnorm)
            mix_saved = (o_pre, s_all)
        else:
            cat = _gmlp_fwd(zm, o_mem, full["ln_g"], full["ln_b"], w_s, b_st)
            mix_saved = ()
        x_mix = xs
        _take_weights(full, get_weights((i, "mix_out"), (cat,)))
        xs = _mm(cat, full[(f"{mixer}_w_out", 0)], tm=512, tn=D_MODEL, tk=cat.shape[1], out_dtype=F32, res=xs,
                 deps=full.pop("deps", ()), name=f"l{i}_mix_out")
        xs, s_ffn2 = _ffn_fwd(xs, ffn2_norm[i:i + 1], "ffn2", i, full, get_weights)
        saved.append((s_ffn1, (x_mix, hm, kv, zm, cat, mix_saved), s_ffn2))

    dx, dx16, d_final, loss_part = _loss_head(xs, final_norm.reshape(1, D_MODEL), loss_target.reshape(N_TOK, D_MODEL))

    small = {"final_norm": [d_final], "loss": [loss_part]}
    d_ffn1, d_ffn2, d_mix = [None, None], [None, None], [None, None]
    dmemn = jnp.zeros((B_LOC * MEM_LEN, D_MODEL), F32)
    deps = ()
    for i in (1, 0):
        s_ffn1, (x_mix, hm, kv, zm, cat, mix_saved), s_ffn2 = saved[i]
        dx, dx16, d_ffn2[i], dw_in_t, dw_out = _ffn_bwd(
            dx, dx16, s_ffn2, ffn2_norm[i:i + 1], full[("ffn2_w_in", i)], full[("ffn2_w_out", i)], f"l{i}_ffn2", deps)
        deps = put_grads((i, "ffn2"), {("ffn2_w_in", i): dw_in_t, ("ffn2_w_out", i): dw_out})
        mixer = "hgrn" if i == 0 else "gmlp"
        w_in_t, w_out = full[(f"{mixer}_w_in", 0)], full[(f"{mixer}_w_out", 0)]
        width = cat.shape[1]
        g_mix = {}
        g_mix[(f"{mixer}_w_out", 0)] = _mm(cat, dx16, ta=True, tm=1024, tn=D_MODEL, tk=N_TOK, out_dtype=BF16,
                                           deps=deps, name=f"l{i}_mix_out_wgrad")
        dcat = _mm(dx16, w_out, tb=True, tm=1024, tn=width // 2, tk=D_MODEL, out_dtype=F32, name=f"l{i}_mix_out_dgrad")
        dq, dk, dv = _attn_bwd(zm, kv, dcat, do_off=width - XA_HEADS * XA_DIM, name=f"l{i}_attn_bwd")
        if i == 0:
            dzm, dlbl, dgn = _hgrn_bwd(zm, mix_saved[0], dcat, dq, mix_saved[1], lb_logits, hgrn_gnorm)
            small["lb_logits"], small["hgrn_gnorm"] = [dlbl], [dgn]
            deps = ()
        else:
            dzm, dws, dbt, dlng, dlnb = _gmlp_bwd(zm, dcat, dq, full["ln_g"], full["ln_b"], w_s, b_st)
            small["gmlp_b_s"], small["gmlp_ln_g"], small["gmlp_ln_b"] = [dbt.T], [dlng], [dlnb]
            deps = put_grads("w_s", dws)
        g_mix[(f"{mixer}_w_in", 0)] = _mm(dzm, hm, ta=True, tm=1024, tn=D_MODEL, tk=N_TOK, out_dtype=BF16, deps=deps,
                                          name=f"l{i}_mix_in_wgrad")
        dkv = jnp.concatenate([dk, dv], axis=1)
        g_mix[("mem_w_kv", i)] = _mm(dkv, memn, ta=True, tm=512, tn=D_MODEL, tk=B_LOC * MEM_LEN, out_dtype=BF16,
                                     name=f"l{i}_mem_kv_wgrad")
        deps = put_grads((i, "mix"), g_mix)
        dx, dx16, d_mix[i] = _dgrad_norm_bwd(dzm, w_in_t, x_mix, mix_norm[i:i + 1], dx, deps=deps,
                                             name=f"l{i}_mix_in_dgrad")
        dmemn = _mm(dkv, full[("mem_w_kv", i)], tm=B_LOC * MEM_LEN, tn=D_MODEL, tk=512, out_dtype=F32, res=dmemn,
                    name=f"l{i}_mem_kv_dgrad")
        def send_small(dg, i=i, dmemn=dmemn):
            d_ffn1[i] = dg
            _, _, dmem_g = _rms_bwd(mem2d, mem_g, dmemn, dmemn, name="mem_norm_bwd")
            small.update(mem_norm=[dmem_g], ffn1_norm=d_ffn1, ffn2_norm=d_ffn2, mix_norm=d_mix)
            return put_grads("small", small)

        if i == 0:
            send_out = lambda dw_out: put_grads((0, "ffn1_out"), {("ffn1_w_out", 0): dw_out})
            dx, dx16, d_ffn1[i], dw_in_t, _ = _ffn_bwd(
                dx, dx16, s_ffn1, ffn1_norm[i:i + 1], full[("ffn1_w_in", i)], full[("ffn1_w_out", i)], f"l{i}_ffn1",
                after_out_wgrad=send_out, before_in_wgrad=send_small)
            deps = put_grads((0, "ffn1_in"), {("ffn1_w_in", 0): dw_in_t})
        else:
            dx, dx16, d_ffn1[i], dw_in_t, dw_out = _ffn_bwd(
                dx, dx16, s_ffn1, ffn1_norm[i:i + 1], full[("ffn1_w_in", i)], full[("ffn1_w_out", i)], f"l{i}_ffn1")
            deps = put_grads((i, "ffn1"), {("ffn1_w_in", i): dw_in_t, ("ffn1_w_out", i): dw_out})
    return dx, deps
```

```python
import functools
import math

import jax
import jax.numpy as jnp
from jax import lax
from jax.experimental import pallas as pl
from jax.experimental.pallas import tpu as pltpu

F32 = jnp.float32
BF16 = jnp.bfloat16

D_MODEL = 1024
SEQ = 2048
B_LOC = 2
N_TOK = B_LOC * SEQ
MEM_LEN = 256
N_DEV = 8
EPS = 1e-6
D_FF = 2816
HG_HEADS = 8
HG_DIM = 128
HG_CHUNK = 64
HG_NCHUNK = SEQ // HG_CHUNK
GM_CHUNK = 128
GM_GROUPS = 8
GM_WIDTH = 2048
GM_GDIM = GM_WIDTH // GM_GROUPS
XA_HEADS = 4
XA_DIM = 256
XA_OFF = 4096

ADAM_LR = 0.001
ADAM_B1 = 0.9
ADAM_B2 = 0.999
ADAM_EPS = 1e-08
ADAM_WD = 0.01
ADAM_STEP = 10

VMEM_LIMIT_BYTES = 56 * 1024 * 1024
MESH_AXES = ("x", "y", "c")

GROUPS = (
    ("ffn1_w_in", True, 2, 704),
    ("ffn1_w_out", False, 2, 352),
    ("mem_w_kv", True, 2, 256),
    ("hgrn_w_in", True, 1, 640),
    ("hgrn_w_out", False, 1, 256),
    ("gmlp_w_in", True, 1, 640),
    ("gmlp_w_out", False, 1, 384),
    ("ffn2_w_in", True, 2, 704),
    ("ffn2_w_out", False, 2, 352),
)
GROUP_LAYERS = {name: layers for name, _, layers, _ in GROUPS}


def _stage_pieces(layer, block):
    if block == "mix":
        mixer = "hgrn" if layer == 0 else "gmlp"
        return (("mem_w_kv", layer), (f"{mixer}_w_in", 0), (f"{mixer}_w_out", 0))
    return ((f"{block}_w_in", layer), (f"{block}_w_out", layer))


ANY_SPEC = pl.BlockSpec(memory_space=pl.ANY)
HBM_SPEC = pl.BlockSpec(memory_space=pltpu.HBM)
SEM_SPEC = pl.BlockSpec(memory_space=pltpu.SEMAPHORE)


def _cp(*sem):
    return pltpu.CompilerParams(dimension_semantics=sem, vmem_limit_bytes=VMEM_LIMIT_BYTES)


def _sigmoid(x):
    return 0.5 * jnp.tanh(0.5 * x) + 0.5


def _gelu_parts(x):
    cdf = 0.5 * (1.0 + lax.erf(x * (1.0 / math.sqrt(2.0))))
    pdf = jnp.exp(-0.5 * x * x) * (1.0 / math.sqrt(2.0 * math.pi))
    return x * cdf, cdf + x * pdf


def _mm(a, b, *, ta=False, tb=False, tm, tn, tk, out_dtype, res=None, scale=1.0, deps=(), name):
    m, k = (a.shape[1], a.shape[0]) if ta else a.shape
    n, kb = b.shape if tb else (b.shape[1], b.shape[0])
    assert k == kb and m % tm == 0 and n % tn == 0 and k % tk == 0, (name, a.shape, b.shape)
    nk = k // tk
    dn = (((0 if ta else 1,), (1 if tb else 0,)), ((), ()))
    n_in = 2 + (res is not None) + len(deps)

    def body(*refs):
        a_ref, b_ref = refs[:2]
        r_ref = refs[2] if res is not None else None
        o_ref, scr = refs[n_in], refs[n_in + 1:]
        p = lax.dot_general(a_ref[...].astype(BF16), b_ref[...].astype(BF16), dn, preferred_element_type=F32)

        def finish(acc):
            if scale != 1.0:
                acc = scale * acc
            if r_ref is not None:
                acc = r_ref[...] + acc
            o_ref[...] = acc.astype(out_dtype)

        if nk == 1:
            finish(p)
        else:
            acc_ref = scr[0]
            kk = pl.program_id(2)

            @pl.when(kk == 0)
            def _():
                acc_ref[...] = p

            @pl.when(kk > 0)
            def _():
                acc_ref[...] += p

            @pl.when(kk == nk - 1)
            def _():
                finish(acc_ref[...])

    a_spec = pl.BlockSpec((tk, tm), lambda i, j, kk: (kk, i)) if ta else pl.BlockSpec((tm, tk), lambda i, j, kk: (i, kk))
    b_mode = dict(pipeline_mode=pl.Buffered(1)) if n == tn and nk == 1 else {}
    if tb:
        b_spec = pl.BlockSpec((tn, tk), lambda i, j, kk: (j, kk), **b_mode)
    else:
        b_spec = pl.BlockSpec((tk, tn), lambda i, j, kk: (kk, j), **b_mode)
    o_spec = pl.BlockSpec((tm, tn), lambda i, j, kk: (i, j))
    in_specs = [a_spec, b_spec] + ([o_spec] if res is not None else []) + [ANY_SPEC] * len(deps)
    args = (a, b) + ((res,) if res is not None else ()) + tuple(deps)
    return pl.pallas_call(
        body,
        name=name,
        grid=(m // tm, n // tn, nk),
        in_specs=in_specs,
        out_specs=o_spec,
        out_shape=jax.ShapeDtypeStruct((m, n), out_dtype),
        scratch_shapes=[pltpu.VMEM((tm, tn), F32)] if nk > 1 else [],
        compiler_params=_cp("parallel", "parallel", "arbitrary"),
    )(*args)


def _rms_fwd(x, g, *, name, deps=(), tm=512):
    rows = x.shape[0]

    def body(x_ref, g_ref, *rest):
        o_ref = rest[len(deps)]
        xv = x_ref[...]
        r = lax.rsqrt(jnp.mean(xv * xv, axis=-1, keepdims=True) + EPS)
        o_ref[...] = (xv * r * g_ref[...]).astype(BF16)

    row = pl.BlockSpec((tm, D_MODEL), lambda i: (i, 0))
    return pl.pallas_call(
        body,
        name=name,
        grid=(rows // tm,),
        in_specs=[row, pl.BlockSpec((1, D_MODEL), lambda i: (0, 0))] + [ANY_SPEC] * len(deps),
        out_specs=row,
        out_shape=jax.ShapeDtypeStruct((rows, D_MODEL), BF16),
        compiler_params=_cp("parallel"),
    )(x, g, *deps)


def _rms_bwd(x, g, dh, dres, *, name, deps=(), tm=512):
    rows = x.shape[0]

    def body(x_ref, g_ref, dh_ref, dres_ref, *rest):
        dx_ref, dx16_ref, dg_ref = rest[len(deps):]
        xv = x_ref[...]
        r = lax.rsqrt(jnp.mean(xv * xv, axis=-1, keepdims=True) + EPS)
        xhat = xv * r
        dhv = dh_ref[...]
        part = jnp.sum(dhv * xhat, axis=0, keepdims=True)

        @pl.when(pl.program_id(0) == 0)
        def _():
            dg_ref[...] = part

        @pl.when(pl.program_id(0) > 0)
        def _():
            dg_ref[...] += part

        dxh = dhv * g_ref[...]
        dx = dres_ref[...] + r * (dxh - xhat * jnp.mean(dxh * xhat, axis=-1, keepdims=True))
        dx_ref[...] = dx
        dx16_ref[...] = dx.astype(BF16)

    row = pl.BlockSpec((tm, D_MODEL), lambda i: (i, 0))
    vec = pl.BlockSpec((1, D_MODEL), lambda i: (0, 0))
    return pl.pallas_call(
        body,
        name=name,
        grid=(rows // tm,),
        in_specs=[row, vec, row, row] + [ANY_SPEC] * len(deps),
        out_specs=[row, row, vec],
        out_shape=[jax.ShapeDtypeStruct((rows, D_MODEL), F32), jax.ShapeDtypeStruct((rows, D_MODEL), BF16),
                   jax.ShapeDtypeStruct((1, D_MODEL), F32)],
        compiler_params=_cp("arbitrary"),
    )(x, g, dh, dres, *deps)


_NT = (((1,), (1,)), ((), ()))
_TN = (((0,), (0,)), ((), ()))


def _norm_mm(x, g, w_t, *, name, tm, tn, deps=()):
    rows = w_t.shape[0]
    nd = len(deps)

    def body(x_ref, g_ref, w_ref, *rest):
        h_ref, z_ref = rest[nd:]
        j = pl.program_id(1)

        @pl.when(j == 0)
        def _():
            xv = x_ref[...]
            r = lax.rsqrt(jnp.mean(xv * xv, axis=-1, keepdims=True) + EPS)
            h_ref[...] = (xv * r * g_ref[...]).astype(BF16)

        w = w_ref[pl.ds(pl.multiple_of(j * tn, tn), tn), :]
        z_ref[...] = lax.dot_general(h_ref[...], w, _NT, preferred_element_type=F32)

    row = pl.BlockSpec((tm, D_MODEL), lambda i, j: (i, 0))
    return pl.pallas_call(
        body,
        name=name,
        grid=(N_TOK // tm, rows // tn),
        in_specs=[row, pl.BlockSpec((1, D_MODEL), lambda i, j: (0, 0)),
                  pl.BlockSpec((rows, D_MODEL), lambda i, j: (0, 0), pipeline_mode=pl.Buffered(1))] + [ANY_SPEC] * nd,
        out_specs=[row, pl.BlockSpec((tm, tn), lambda i, j: (i, j))],
        out_shape=[jax.ShapeDtypeStruct((N_TOK, D_MODEL), BF16), jax.ShapeDtypeStruct((N_TOK, rows), F32)],
        compiler_params=_cp("parallel", "arbitrary"),
    )(x, g, w_t, *deps)


def _ffn_forward(x, g, w_in_t, w_out, *, scale, name, deps=(), tm=256, tn=1408):
    nd = len(deps)

    def body(x_ref, g_ref, wi_ref, wo_ref, *rest):
        y_ref, h_ref, z_ref, act_ref = rest[nd:]
        xv = x_ref[...]
        r = lax.rsqrt(jnp.mean(xv * xv, axis=-1, keepdims=True) + EPS)
        h = (xv * r * g_ref[...]).astype(BF16)
        h_ref[...] = h
        for j in range(D_FF // tn):
            cols = slice(j * tn, (j + 1) * tn)
            gate = lax.dot_general(h, wi_ref[j * tn:(j + 1) * tn, :], _NT, preferred_element_type=F32)
            up = lax.dot_general(h, wi_ref[D_FF + j * tn:D_FF + (j + 1) * tn, :], _NT, preferred_element_type=F32)
            s = _sigmoid(gate)
            silu = gate * s
            z_ref[0, :, cols] = (up * (s + silu * (1.0 - s))).astype(BF16)
            z_ref[1, :, cols] = silu.astype(BF16)
            act_ref[:, cols] = (silu * up).astype(BF16)
        y_ref[...] = xv + scale * jnp.dot(act_ref[...], wo_ref[...], preferred_element_type=F32)

    row = pl.BlockSpec((tm, D_MODEL), lambda i: (i, 0))
    whole = lambda rows: pl.BlockSpec((rows, D_MODEL), lambda i: (0, 0), pipeline_mode=pl.Buffered(1))
    return pl.pallas_call(
        body,
        name=name,
        grid=(N_TOK // tm,),
        in_specs=[row, pl.BlockSpec((1, D_MODEL), lambda i: (0, 0)), whole(2 * D_FF), whole(D_FF)] + [ANY_SPEC] * nd,
        out_specs=[row, row, pl.BlockSpec((2, tm, D_FF), lambda i: (0, i, 0)), pl.BlockSpec((tm, D_FF), lambda i: (i, 0))],
        out_shape=[jax.ShapeDtypeStruct((N_TOK, D_MODEL), F32), jax.ShapeDtypeStruct((N_TOK, D_MODEL), BF16),
                   jax.ShapeDtypeStruct((2, N_TOK, D_FF), BF16), jax.ShapeDtypeStruct((N_TOK, D_FF), BF16)],
        compiler_params=_cp("parallel"),
    )(x, g, w_in_t, w_out, *deps)


def _ffn_dgrad(dy16, dres, w_out, z, w_in_t, x, g, *, scale, name, deps=(), tm=256, tn=1408):
    nd = len(deps)

    def body(dy_ref, dres_ref, wo_ref, z_ref, wi_hbm, x_ref, g_ref, *rest):
        dz_ref, dx_ref, dx16_ref, dg_ref, wi_ref, wi_sem = rest[nd:]
        fetch = pltpu.make_async_copy(wi_hbm, wi_ref, wi_sem)
        pl.when(pl.program_id(0) == 0)(fetch.start)
        dy = dy_ref[...]
        for j in range(D_FF // tn):
            cols = slice(j * tn, (j + 1) * tn)
            da = lax.dot_general(dy, wo_ref[cols, :], _NT, preferred_element_type=F32) * scale
            dz_ref[0, :, cols] = (da * z_ref[0, :, cols].astype(F32)).astype(BF16)
            dz_ref[1, :, cols] = (da * z_ref[1, :, cols].astype(F32)).astype(BF16)
        pl.when(pl.program_id(0) == 0)(fetch.wait)
        dh = jnp.dot(dz_ref[0], wi_ref[:D_FF, :], preferred_element_type=F32) + jnp.dot(
            dz_ref[1], wi_ref[D_FF:, :], preferred_element_type=F32)
        xv = x_ref[...]
        r = lax.rsqrt(jnp.mean(xv * xv, axis=-1, keepdims=True) + EPS)
        xhat = xv * r
        part = jnp.sum(dh * xhat, axis=0, keepdims=True)

        @pl.when(pl.program_id(0) == 0)
        def _():
            dg_ref[...] = part

        @pl.when(pl.program_id(0) > 0)
        def _():
            dg_ref[...] += part

        dxh = dh * g_ref[...]
        dx = dres_ref[...] + r * (dxh - xhat * jnp.mean(dxh * xhat, axis=-1, keepdims=True))
        dx_ref[...] = dx
        dx16_ref[...] = dx.astype(BF16)

    row = pl.BlockSpec((tm, D_MODEL), lambda i: (i, 0))
    vec = pl.BlockSpec((1, D_MODEL), lambda i: (0, 0))
    planes = pl.BlockSpec((2, tm, D_FF), lambda i: (0, i, 0))
    whole = lambda rows: pl.BlockSpec((rows, D_MODEL), lambda i: (0, 0), pipeline_mode=pl.Buffered(1))
    return pl.pallas_call(
        body,
        name=name,
        grid=(N_TOK // tm,),
        in_specs=[row, row, whole(D_FF), planes, ANY_SPEC, row, vec] + [ANY_SPEC] * nd,
        out_specs=[planes, row, row, vec],
        out_shape=[jax.ShapeDtypeStruct((2, N_TOK, D_FF), BF16), jax.ShapeDtypeStruct((N_TOK, D_MODEL), F32),
                   jax.ShapeDtypeStruct((N_TOK, D_MODEL), BF16), jax.ShapeDtypeStruct((1, D_MODEL), F32)],
        scratch_shapes=[pltpu.VMEM((2 * D_FF, D_MODEL), BF16), pltpu.SemaphoreType.DMA],
        compiler_params=_cp("arbitrary"),
    )(dy16, dres, w_out, z, w_in_t, x, g, *deps)


def _planes_wgrad(dz, h, *, name, deps=(), tm=1408):
    per_plane = D_FF // tm

    def body(a_ref, b_ref, *rest):
        o_ref = rest[len(deps)]
        o_ref[...] = lax.dot_general(a_ref[...], b_ref[...], _TN, preferred_element_type=F32).astype(BF16)

    return pl.pallas_call(
        body,
        name=name,
        grid=(2 * per_plane,),
        in_specs=[pl.BlockSpec((None, N_TOK, tm),
                               lambda i: (jnp.where(i < per_plane, 0, 1), 0, jnp.where(i < per_plane, i, i - per_plane))),
                  pl.BlockSpec((N_TOK, D_MODEL), lambda i: (0, 0), pipeline_mode=pl.Buffered(1))] + [ANY_SPEC] * len(deps),
        out_specs=pl.BlockSpec((tm, D_MODEL), lambda i: (i, 0)),
        out_shape=jax.ShapeDtypeStruct((2 * D_FF, D_MODEL), BF16),
        compiler_params=_cp("parallel"),
    )(dz, h, *deps)


def _dgrad_norm_bwd(dz, w_t, x, g, dres, *, name, deps=(), tm=512):
    rows = w_t.shape[0]
    nd = len(deps)

    def body(a_ref, b_ref, x_ref, g_ref, dres_ref, *rest):
        dx_ref, dx16_ref, dg_ref = rest[nd:]
        dh = jnp.dot(a_ref[...], b_ref[...], preferred_element_type=F32)
        xv = x_ref[...]
        r = lax.rsqrt(jnp.mean(xv * xv, axis=-1, keepdims=True) + EPS)
        xhat = xv * r
        part = jnp.sum(dh * xhat, axis=0, keepdims=True)

        @pl.when(pl.program_id(0) == 0)
        def _():
            dg_ref[...] = part

        @pl.when(pl.program_id(0) > 0)
        def _():
            dg_ref[...] += part

        dxh = dh * g_ref[...]
        dx = dres_ref[...] + r * (dxh - xhat * jnp.mean(dxh * xhat, axis=-1, keepdims=True))
        dx_ref[...] = dx
        dx16_ref[...] = dx.astype(BF16)

    a_spec = pl.BlockSpec((tm, rows), lambda i: (i, 0))
    row = pl.BlockSpec((tm, D_MODEL), lambda i: (i, 0))
    vec = pl.BlockSpec((1, D_MODEL), lambda i: (0, 0))
    return pl.pallas_call(
        body,
        name=name,
        grid=(N_TOK // tm,),
        in_specs=[a_spec, pl.BlockSpec((rows, D_MODEL), lambda i: (0, 0), pipeline_mode=pl.Buffered(1)), row, vec, row]
        + [ANY_SPEC] * nd,
        out_specs=[row, row, vec],
        out_shape=[jax.ShapeDtypeStruct((N_TOK, D_MODEL), F32), jax.ShapeDtypeStruct((N_TOK, D_MODEL), BF16),
                   jax.ShapeDtypeStruct((1, D_MODEL), F32)],
        compiler_params=_cp("arbitrary"),
    )(dz, w_t, x, g, dres, *deps)


def _loss_head(x, g, target, *, tm=512):
    def body(x_ref, g_ref, t_ref, dx_ref, dx16_ref, dg_ref, loss_ref):
        xv = x_ref[...]
        gv = g_ref[...]
        r = lax.rsqrt(jnp.mean(xv * xv, axis=-1, keepdims=True) + EPS)
        xhat = xv * r
        err = xhat * gv - t_ref[...]
        loss_part = jnp.zeros((1, 128), F32) + 0.5 * jnp.sum(jnp.mean(err * err, axis=-1, keepdims=True))
        dy = err * (1.0 / D_MODEL)
        dg_part = jnp.sum(dy * xhat, axis=0, keepdims=True)

        @pl.when(pl.program_id(0) == 0)
        def _():
            dg_ref[...] = dg_part
            loss_ref[...] = loss_part

        @pl.when(pl.program_id(0) > 0)
        def _():
            dg_ref[...] += dg_part
            loss_ref[...] += loss_part

        dxh = dy * gv
        dx = r * (dxh - xhat * jnp.mean(dxh * xhat, axis=-1, keepdims=True))
        dx_ref[...] = dx
        dx16_ref[...] = dx.astype(BF16)

    row = pl.BlockSpec((tm, D_MODEL), lambda i: (i, 0))
    vec = pl.BlockSpec((1, D_MODEL), lambda i: (0, 0))
    return pl.pallas_call(
        body,
        name="loss_head",
        grid=(N_TOK // tm,),
        in_specs=[row, vec, row],
        out_specs=[row, row, vec, pl.BlockSpec((1, 128), lambda i: (0, 0))],
        out_shape=[
            jax.ShapeDtypeStruct((N_TOK, D_MODEL), F32),
            jax.ShapeDtypeStruct((N_TOK, D_MODEL), BF16),
            jax.ShapeDtypeStruct((1, D_MODEL), F32),
            jax.ShapeDtypeStruct((1, 128), F32),
        ],
        compiler_params=_cp("arbitrary"),
    )(x, g, target)


XA_TQ = 2048
XA_SCALE = XA_DIM ** -0.5


def _attn_probs(q16, k16):
    s = lax.dot_general(q16, k16, _NT, preferred_element_type=F32) * XA_SCALE
    e = jnp.exp(s - jnp.max(s, axis=-1, keepdims=True))
    return e / jnp.sum(e, axis=-1, keepdims=True)


def _attn_fwd(z, kv, *, name):
    nt = SEQ // XA_TQ

    def body(q_ref, k_ref, v_ref, o_ref):
        p = _attn_probs(q_ref[...].astype(BF16), k_ref[...].astype(BF16))
        o_ref[...] = jnp.dot(p.astype(BF16), v_ref[...].astype(BF16), preferred_element_type=F32).astype(BF16)

    return pl.pallas_call(
        body,
        name=name,
        grid=(B_LOC, XA_HEADS, nt),
        in_specs=[
            pl.BlockSpec((XA_TQ, XA_DIM), lambda b, h, t: (b * nt + t, XA_OFF // XA_DIM + h)),
            pl.BlockSpec((MEM_LEN, XA_DIM), lambda b, h, t: (b, h)),
            pl.BlockSpec((MEM_LEN, XA_DIM), lambda b, h, t: (b, XA_HEADS + h)),
        ],
        out_specs=pl.BlockSpec((XA_TQ, XA_DIM), lambda b, h, t: (b * nt + t, h)),
        out_shape=jax.ShapeDtypeStruct((N_TOK, XA_HEADS * XA_DIM), BF16),
        compiler_params=_cp("parallel", "parallel", "arbitrary"),
    )(z, kv, kv)


def _attn_bwd(z, kv, dcat, *, do_off, name):
    nt = SEQ // XA_TQ

    def body(q_ref, k_ref, v_ref, do_ref, dq_ref, dk_ref, dv_ref):
        q16 = q_ref[...].astype(BF16)
        k16 = k_ref[...].astype(BF16)
        v16 = v_ref[...].astype(BF16)
        do16 = do_ref[...].astype(BF16)
        p = _attn_probs(q16, k16)
        dv_part = lax.dot_general(p.astype(BF16), do16, _TN, preferred_element_type=F32)
        dp = lax.dot_general(do16, v16, _NT, preferred_element_type=F32)
        ds16 = (p * (dp - jnp.sum(dp * p, axis=-1, keepdims=True)) * XA_SCALE).astype(BF16)
        dq_ref[...] = jnp.dot(ds16, k16, preferred_element_type=F32).astype(BF16)
        dk_part = lax.dot_general(ds16, q16, _TN, preferred_element_type=F32)

        @pl.when(pl.program_id(2) == 0)
        def _():
            dk_ref[...] = dk_part
            dv_ref[...] = dv_part

        @pl.when(pl.program_id(2) > 0)
        def _():
            dk_ref[...] += dk_part
            dv_ref[...] += dv_part

    qspec = pl.BlockSpec((XA_TQ, XA_DIM), lambda b, h, t: (b * nt + t, XA_OFF // XA_DIM + h))
    kspec = lambda off: pl.BlockSpec((MEM_LEN, XA_DIM), lambda b, h, t: (b, off + h))
    return pl.pallas_call(
        body,
        name=name,
        grid=(B_LOC, XA_HEADS, nt),
        in_specs=[qspec, kspec(0), kspec(XA_HEADS),
                  pl.BlockSpec((XA_TQ, XA_DIM), lambda b, h, t: (b * nt + t, do_off // XA_DIM + h))],
        out_specs=[pl.BlockSpec((XA_TQ, XA_DIM), lambda b, h, t: (b * nt + t, h)), kspec(0), kspec(0)],
        out_shape=[
            jax.ShapeDtypeStruct((N_TOK, XA_HEADS * XA_DIM), BF16),
            jax.ShapeDtypeStruct((B_LOC * MEM_LEN, XA_HEADS * XA_DIM), F32),
            jax.ShapeDtypeStruct((B_LOC * MEM_LEN, XA_HEADS * XA_DIM), F32),
        ],
        compiler_params=_cp("parallel", "parallel", "arbitrary"),
    )(z, kv, kv, dcat)


def _tril(n):
    return lax.broadcasted_iota(jnp.int32, (n, n), 0) >= lax.broadcasted_iota(jnp.int32, (n, n), 1)


def _lower_bound(lbl):
    e = jnp.exp(lbl - jnp.max(lbl, axis=0, keepdims=True))
    p = e / jnp.sum(e, axis=0, keepdims=True)
    return p[0:1, :], p


def _hgrn_gates(zq, zf, lb, tril_f):
    sig = _sigmoid(zf)
    f = lb + (1.0 - lb) * sig
    kk = 1.0 - f
    sq = _sigmoid(zq)
    q = zq * sq
    b = jnp.dot(tril_f, jnp.log(f), preferred_element_type=F32, precision=lax.Precision.HIGHEST)
    bl = b[HG_CHUNK - 1:HG_CHUNK, :]
    return q, sq, sig, f, kk, b, bl


HG_TB = 512
HG_CPB = HG_TB // HG_CHUNK
HG_NT = SEQ // HG_TB
HG_WIDTH = HG_HEADS * HG_DIM


def _head(h, section=0):
    return slice(section * HG_WIDTH + h * HG_DIM, section * HG_WIDTH + (h + 1) * HG_DIM)


def _hgrn_fwd(z, o_mem, lb_logits, gnorm):
    def body(zq_ref, zf_ref, zi_ref, zg_ref, omem_ref, lbl_ref, gn_ref, o_ref, opre_ref, sall_ref, st_ref):
        lb, _ = _lower_bound(lbl_ref[...])
        gn = gn_ref[...]
        mask = _tril(HG_CHUNK)
        tril_f = mask.astype(F32)
        o_ref[:, HG_WIDTH:] = omem_ref[...]

        @pl.when(pl.program_id(1) == 0)
        def _():
            st_ref[...] = jnp.zeros_like(st_ref)

        def chunk(c, carry):
            rows = pl.ds(pl.multiple_of(c * HG_CHUNK, HG_CHUNK), HG_CHUNK)
            q, _, _, _, kk, b, bl = _hgrn_gates(zq_ref[rows, :], zf_ref[rows, :], lb, tril_f)
            v16 = zi_ref[rows, :].astype(BF16)
            qd16 = (q * jnp.exp(b)).astype(BF16)
            ki16 = (kk * jnp.exp(-b)).astype(BF16)
            kd16 = (kk * jnp.exp(bl - b)).astype(BF16)
            ebl = jnp.exp(bl)
            zg = zg_ref[rows, :]
            gate = zg * _sigmoid(zg)
            for h in range(HG_HEADS):
                sl = _head(h)
                a = jnp.where(mask, lax.dot_general(qd16[:, sl], ki16[:, sl], _NT, preferred_element_type=F32), 0.0)
                st = st_ref[h]
                sall_ref[0, h, c] = st
                o = jnp.dot(a.astype(BF16), v16[:, sl], preferred_element_type=F32) + lax.dot_general(
                    qd16[:, sl], st.astype(BF16), _NT, preferred_element_type=F32)
                st_ref[h] = st * ebl[:, sl] + lax.dot_general(v16[:, sl], kd16[:, sl], _TN, preferred_element_type=F32)
                opre_ref[rows, sl] = o
                r = lax.rsqrt(jnp.mean(o * o, axis=-1, keepdims=True) + EPS)
                o_ref[rows, sl] = ((o * r * gn) * gate[:, sl]).astype(BF16)
            return carry

        lax.fori_loop(0, HG_CPB, chunk, 0, unroll=True)

    zspec = lambda s: pl.BlockSpec((HG_TB, HG_WIDTH), lambda b, t: (b * HG_NT + t, s))
    return pl.pallas_call(
        body,
        name="hgrn_fwd",
        grid=(B_LOC, HG_NT),
        in_specs=[zspec(0), zspec(1), zspec(2), zspec(3), zspec(0),
                  pl.BlockSpec((3, HG_WIDTH), lambda b, t: (0, 0)), pl.BlockSpec((1, HG_DIM), lambda b, t: (0, 0))],
        out_specs=[pl.BlockSpec((HG_TB, 2 * HG_WIDTH), lambda b, t: (b * HG_NT + t, 0)), zspec(0),
                   pl.BlockSpec((1, HG_HEADS, HG_CPB, HG_DIM, HG_DIM), lambda b, t: (b, 0, t, 0, 0))],
        out_shape=[
            jax.ShapeDtypeStruct((N_TOK, 2 * HG_WIDTH), BF16),
            jax.ShapeDtypeStruct((N_TOK, HG_WIDTH), F32),
            jax.ShapeDtypeStruct((B_LOC, HG_HEADS, HG_NCHUNK, HG_DIM, HG_DIM), F32),
        ],
        scratch_shapes=[pltpu.VMEM((HG_HEADS, HG_DIM, HG_DIM), F32)],
        compiler_params=_cp("parallel", "arbitrary"),
    )(z, z, z, z, o_mem, lb_logits, gnorm)


def _hgrn_bwd(z, opre, dcat, dq_mem, sall, lb_logits, gnorm):
    def body(zq_ref, zf_ref, zi_ref, zg_ref, opre_ref, dout_ref, dqm_ref, sall_ref, lbl_ref, gn_ref,
             dz_ref, dlbl_ref, dgn_ref, dst_ref, dlb_ref, dgn_acc, db_ref, dkk_ref, dbl_ref):
        b_id, t_id = pl.program_id(0), pl.program_id(1)
        lb, p = _lower_bound(lbl_ref[...])
        gn = gn_ref[...]
        mask = _tril(HG_CHUNK)
        tril_f = mask.astype(F32)
        dz_ref[:, 4 * HG_WIDTH:] = dqm_ref[...]

        @pl.when(t_id == 0)
        def _():
            dst_ref[...] = jnp.zeros_like(dst_ref)
            dlb_ref[...] = jnp.zeros_like(dlb_ref)

        @pl.when((b_id == 0) & (t_id == 0))
        def _():
            dgn_acc[...] = jnp.zeros_like(dgn_acc)

        def chunk(i, carry):
            c = HG_CPB - 1 - i
            rows = pl.ds(pl.multiple_of(c * HG_CHUNK, HG_CHUNK), HG_CHUNK)
            zq, zg = zq_ref[rows, :], zg_ref[rows, :]
            q, sq, sig, f, kk, b, bl = _hgrn_gates(zq, zf_ref[rows, :], lb, tril_f)
            v16 = zi_ref[rows, :].astype(BF16)
            eb, enb, ebl_b, ebl = jnp.exp(b), jnp.exp(-b), jnp.exp(bl - b), jnp.exp(bl)
            qd, ki, kd = q * eb, kk * enb, kk * ebl_b
            qd16, ki16, kd16 = qd.astype(BF16), ki.astype(BF16), kd.astype(BF16)
            o_all = opre_ref[rows, :]
            dout = dout_ref[rows, :]
            sg = _sigmoid(zg)
            d_on_all = dout * (zg * sg)
            dgate = dout * (sg * (1.0 + zg * (1.0 - sg)))
            dq_scale = eb * (sq * (1.0 + zq * (1.0 - sq)))
            for h in range(HG_HEADS):
                sl = _head(h)
                o = o_all[:, sl]
                r = lax.rsqrt(jnp.mean(o * o, axis=-1, keepdims=True) + EPS)
                ohat = o * r
                d_on = d_on_all[:, sl]
                dz_ref[rows, _head(h, 3)] = (dgate[:, sl] * (ohat * gn)).astype(BF16)
                dgn_acc[...] += jnp.sum(d_on * ohat, axis=0, keepdims=True)
                dohat = d_on * gn
                do16 = (r * (dohat - ohat * jnp.mean(dohat * ohat, axis=-1, keepdims=True))).astype(BF16)
                st = sall_ref[0, h, c]
                dst = dst_ref[h]
                st16, dst16 = st.astype(BF16), dst.astype(BF16)
                qd_h, ki_h, kd_h, v_h = qd16[:, sl], ki16[:, sl], kd16[:, sl], v16[:, sl]
                a16 = jnp.where(mask, lax.dot_general(qd_h, ki_h, _NT, preferred_element_type=F32), 0.0).astype(BF16)
                da16 = jnp.where(mask, lax.dot_general(do16, v_h, _NT, preferred_element_type=F32), 0.0).astype(BF16)
                dv = lax.dot_general(a16, do16, _TN, preferred_element_type=F32) + lax.dot_general(
                    kd_h, dst16, _NT, preferred_element_type=F32)
                dqd = jnp.dot(da16, ki_h, preferred_element_type=F32) + jnp.dot(do16, st16, preferred_element_type=F32)
                dki = lax.dot_general(da16, qd_h, _TN, preferred_element_type=F32)
                dkd = jnp.dot(v_h, dst16, preferred_element_type=F32)
                dbl_ref[:, sl] = jnp.sum(dkd * kd[:, sl], axis=0, keepdims=True) + ebl[:, sl] * jnp.sum(
                    st * dst, axis=0, keepdims=True)
                dst_ref[h] = dst * ebl[:, sl] + lax.dot_general(do16, qd_h, _TN, preferred_element_type=F32)
                dz_ref[rows, _head(h, 2)] = dv.astype(BF16)
                dz_ref[rows, sl] = (dqd * dq_scale[:, sl]).astype(BF16)
                dkk_ref[:, sl] = dki * enb[:, sl] + dkd * ebl_b[:, sl]
                db_ref[:, sl] = dqd * qd[:, sl] - dki * ki[:, sl] - dkd * kd[:, sl]
            dlogf = lax.dot_general(tril_f, db_ref[...], _TN, preferred_element_type=F32,
                                    precision=lax.Precision.HIGHEST) + dbl_ref[...]
            df = dlogf / f - dkk_ref[...]
            dz_ref[rows, HG_WIDTH:2 * HG_WIDTH] = (df * (1.0 - lb) * sig * (1.0 - sig)).astype(BF16)
            dlb_ref[...] += jnp.sum(df * (1.0 - sig), axis=0, keepdims=True)
            return carry

        lax.fori_loop(0, HG_CPB, chunk, 0, unroll=True)

        @pl.when(t_id == HG_NT - 1)
        def _():
            row0 = (lax.broadcasted_iota(jnp.int32, (3, HG_WIDTH), 0) == 0).astype(F32)
            dlbl_part = dlb_ref[...] * lb * (row0 - p)

            @pl.when(b_id == 0)
            def _():
                dlbl_ref[...] = dlbl_part

            @pl.when(b_id > 0)
            def _():
                dlbl_ref[...] += dlbl_part

            dgn_ref[...] = dgn_acc[...]

    rev = lambda b, t: b * HG_NT + HG_NT - 1 - t
    zspec = lambda s: pl.BlockSpec((HG_TB, HG_WIDTH), lambda b, t: (rev(b, t), s))
    return pl.pallas_call(
        body,
        name="hgrn_bwd",
        grid=(B_LOC, HG_NT),
        in_specs=[zspec(0), zspec(1), zspec(2), zspec(3), zspec(0), zspec(0), zspec(0),
                  pl.BlockSpec((1, HG_HEADS, HG_CPB, HG_DIM, HG_DIM), lambda b, t: (b, 0, HG_NT - 1 - t, 0, 0)),
                  pl.BlockSpec((3, HG_WIDTH), lambda b, t: (0, 0)), pl.BlockSpec((1, HG_DIM), lambda b, t: (0, 0))],
        out_specs=[pl.BlockSpec((HG_TB, 5 * HG_WIDTH), lambda b, t: (rev(b, t), 0)),
                   pl.BlockSpec((3, HG_WIDTH), lambda b, t: (0, 0)), pl.BlockSpec((1, HG_DIM), lambda b, t: (0, 0))],
        out_shape=[jax.ShapeDtypeStruct((N_TOK, 5 * HG_WIDTH), BF16),
                   jax.ShapeDtypeStruct((3, HG_WIDTH), F32), jax.ShapeDtypeStruct((1, HG_DIM), F32)],
        scratch_shapes=[pltpu.VMEM((HG_HEADS, HG_DIM, HG_DIM), F32), pltpu.VMEM((1, HG_WIDTH), F32),
                        pltpu.VMEM((1, HG_DIM), F32), pltpu.VMEM((HG_CHUNK, HG_WIDTH), F32),
                        pltpu.VMEM((HG_CHUNK, HG_WIDTH), F32), pltpu.VMEM((1, HG_WIDTH), F32)],
        compiler_params=_cp("arbitrary", "arbitrary"),
    )(z, z, z, z, opre, dcat, dq_mem, sall, lb_logits, gnorm)


GM_TM = 256


def _gmlp_norm(zv, ln_g, ln_b):
    gv, dgelu = _gelu_parts(zv)
    xc = gv - jnp.mean(gv, axis=-1, keepdims=True)
    rstd = lax.rsqrt(jnp.mean(xc * xc, axis=-1, keepdims=True) + EPS)
    vhat = xc * rstd
    return vhat * ln_g + ln_b, vhat, rstd, dgelu


def _gmlp_specs():
    half = lambda j: pl.BlockSpec((GM_TM, GM_WIDTH), lambda i: (i, j))
    vec = pl.BlockSpec((1, GM_WIDTH), lambda i: (0, 0))
    w = pl.BlockSpec((GM_GROUPS, GM_CHUNK, GM_CHUNK), lambda i: (0, 0, 0))
    bt = pl.BlockSpec((GM_CHUNK, GM_GROUPS), lambda i: (0, 0))
    return half, vec, w, bt


def _gmlp_fwd(z, o_mem, ln_g, ln_b, w_s, b_st):
    def body(zu_ref, zv_ref, omem_ref, g_ref, b_ref, w_ref, bt_ref, o_ref):
        o_ref[:, GM_WIDTH:] = omem_ref[...]
        u, _ = _gelu_parts(zu_ref[...])
        v, _, _, _ = _gmlp_norm(zv_ref[...], g_ref[...], b_ref[...])
        v16 = v.astype(BF16)
        mask = _tril(GM_CHUNK)
        bt = bt_ref[...]
        for g in range(GM_GROUPS):
            wm16 = jnp.where(mask, w_ref[g], 0.0).astype(BF16)
            cols = slice(g * GM_GDIM, (g + 1) * GM_GDIM)
            for c in range(GM_TM // GM_CHUNK):
                rows = slice(c * GM_CHUNK, (c + 1) * GM_CHUNK)
                mixed = jnp.dot(wm16, v16[rows, cols], preferred_element_type=F32) + bt[:, g:g + 1]
                o_ref[rows, cols] = (u[rows, cols] * mixed).astype(BF16)

    half, vec, w, bt = _gmlp_specs()
    return pl.pallas_call(
        body,
        name="gmlp_fwd",
        grid=(N_TOK // GM_TM,),
        in_specs=[half(0), half(1), pl.BlockSpec((GM_TM, XA_HEADS * XA_DIM), lambda i: (i, 0)), vec, vec, w, bt],
        out_specs=pl.BlockSpec((GM_TM, GM_WIDTH + XA_HEADS * XA_DIM), lambda i: (i, 0)),
        out_shape=jax.ShapeDtypeStruct((N_TOK, GM_WIDTH + XA_HEADS * XA_DIM), BF16),
        compiler_params=_cp("parallel"),
    )(z, z, o_mem, ln_g, ln_b, w_s, b_st)


def _gmlp_bwd(z, dcat, dq_mem, ln_g, ln_b, w_s, b_st):
    def body(zu_ref, zv_ref, dout_ref, dqm_ref, g_ref, b_ref, w_ref, bt_ref,
             dz_ref, dw_ref, dbt_ref, dg_ref, db_ref, dv_ref):
        dz_ref[:, 2 * GM_WIDTH:] = dqm_ref[...]
        @pl.when(pl.program_id(0) == 0)
        def _():
            dw_ref[...] = jnp.zeros_like(dw_ref)
            dbt_ref[...] = jnp.zeros_like(dbt_ref)
            dg_ref[...] = jnp.zeros_like(dg_ref)
            db_ref[...] = jnp.zeros_like(db_ref)

        zu = zu_ref[...]
        u, du_dz = _gelu_parts(zu)
        ln_g = g_ref[...]
        v, vhat, rstd, dgv_dz = _gmlp_norm(zv_ref[...], ln_g, b_ref[...])
        v16 = v.astype(BF16)
        dout = dout_ref[...]
        dmixed = dout * u
        dm16 = dmixed.astype(BF16)
        mask = _tril(GM_CHUNK)
        bt = bt_ref[...]
        group_id = lax.broadcasted_iota(jnp.int32, (1, GM_GROUPS), 1)
        dbt = jnp.zeros((GM_CHUNK, GM_GROUPS), F32)
        for g in range(GM_GROUPS):
            wm16 = jnp.where(mask, w_ref[g], 0.0).astype(BF16)
            cols = slice(g * GM_GDIM, (g + 1) * GM_GDIM)
            dw = jnp.zeros((GM_CHUNK, GM_CHUNK), F32)
            dbt_g = jnp.zeros((GM_CHUNK, 1), F32)
            for c in range(GM_TM // GM_CHUNK):
                rows = slice(c * GM_CHUNK, (c + 1) * GM_CHUNK)
                mixed = jnp.dot(wm16, v16[rows, cols], preferred_element_type=F32) + bt[:, g:g + 1]
                dz_ref[rows, cols] = (dout[rows, cols] * mixed * du_dz[rows, cols]).astype(BF16)
                dw += lax.dot_general(dm16[rows, cols], v16[rows, cols], _NT, preferred_element_type=F32)
                dbt_g += jnp.sum(dmixed[rows, cols], axis=-1, keepdims=True)
                dv_ref[rows, cols] = lax.dot_general(wm16, dm16[rows, cols], _TN, preferred_element_type=F32)
            dw_ref[g] += jnp.where(mask, dw, 0.0)
            dbt = dbt + dbt_g * (group_id == g).astype(F32)
        dbt_ref[...] += dbt
        dv = dv_ref[...]
        dg_ref[...] += jnp.sum(dv * vhat, axis=0, keepdims=True)
        db_ref[...] += jnp.sum(dv, axis=0, keepdims=True)
        dvh = dv * ln_g
        dgv = rstd * (dvh - jnp.mean(dvh, axis=-1, keepdims=True) - vhat * jnp.mean(dvh * vhat, axis=-1, keepdims=True))
        dz_ref[:, GM_WIDTH:2 * GM_WIDTH] = (dgv * dgv_dz).astype(BF16)

    half, vec, w, bt = _gmlp_specs()
    dz_width = 2 * GM_WIDTH + XA_HEADS * XA_DIM
    return pl.pallas_call(
        body,
        name="gmlp_bwd",
        grid=(N_TOK // GM_TM,),
        in_specs=[half(0), half(1), half(0), pl.BlockSpec((GM_TM, XA_HEADS * XA_DIM), lambda i: (i, 0)), vec, vec, w, bt],
        out_specs=[pl.BlockSpec((GM_TM, dz_width), lambda i: (i, 0)), w, bt, vec, vec],
        out_shape=[jax.ShapeDtypeStruct((N_TOK, dz_width), BF16),
                   jax.ShapeDtypeStruct((GM_GROUPS, GM_CHUNK, GM_CHUNK), F32),
                   jax.ShapeDtypeStruct((GM_CHUNK, GM_GROUPS), F32),
                   jax.ShapeDtypeStruct((1, GM_WIDTH), F32), jax.ShapeDtypeStruct((1, GM_WIDTH), F32)],
        scratch_shapes=[pltpu.VMEM((GM_TM, GM_WIDTH), F32)],
        compiler_params=_cp("arbitrary"),
    )(z, z, dcat, dq_mem, ln_g, ln_b, w_s, b_st)


def _own_slot(shape):
    return pl.BlockSpec((None,) + tuple(shape), lambda i, me_ref: (me_ref[0],) + (0,) * len(shape))


def _place_rows(w, layer, cuts_columns, me, *, name, deps=()):
    _, r, c = w.shape
    n = c if cuts_columns else r

    def body(me_ref, w_ref, *rest):
        o_ref = rest[len(deps)]
        wv = w_ref[...]
        o_ref[...] = (wv.T if cuts_columns else wv).astype(BF16)

    return pl.pallas_call(
        body,
        name=name,
        grid_spec=pltpu.PrefetchScalarGridSpec(
            num_scalar_prefetch=1, grid=(1,),
            in_specs=[pl.BlockSpec((None, r, c), lambda i, me_ref: (layer, 0, 0))] + [ANY_SPEC] * len(deps),
            out_specs=_own_slot((n, D_MODEL))),
        out_shape=jax.ShapeDtypeStruct((N_DEV, n, D_MODEL), BF16),
        compiler_params=_cp("arbitrary"),
    )(me, w, *deps)


def _place_ln(ln_g, ln_b, me):
    blk = ln_g.shape[1]

    def body(me_ref, g_ref, b_ref, o_ref):
        o_ref[...] = jnp.zeros_like(o_ref)
        o_ref[0:1, :] = g_ref[...]
        o_ref[1:2, :] = b_ref[...]

    vec = pl.BlockSpec((1, blk), lambda i, me_ref: (0, 0))
    return pl.pallas_call(
        body,
        name="place_ln",
        grid_spec=pltpu.PrefetchScalarGridSpec(
            num_scalar_prefetch=1, grid=(1,), in_specs=[vec, vec], out_specs=_own_slot((8, blk))),
        out_shape=jax.ShapeDtypeStruct((N_DEV, 8, blk), F32),
        compiler_params=_cp("arbitrary"),
    )(me, ln_g, ln_b)


def _place_slab(a, me, *, name):
    def body(me_ref, a_ref, o_ref):
        o_ref[...] = a_ref[...]

    return pl.pallas_call(
        body,
        name=name,
        grid_spec=pltpu.PrefetchScalarGridSpec(
            num_scalar_prefetch=1, grid=(1,),
            in_specs=[pl.BlockSpec(a.shape, lambda i, me_ref: (0, 0))], out_specs=_own_slot(a.shape)),
        out_shape=jax.ShapeDtypeStruct((N_DEV,) + a.shape, a.dtype),
        compiler_params=_cp("arbitrary"),
    )(me, a)


def _place_own(grads, me, *, name):
    k = len(grads)

    def body(me_ref, *refs):
        for src, dst in zip(refs[:k], refs[k:]):
            dst[...] = src[...]

    specs = [_own_slot(g.shape[1:]) for g in grads]
    return pl.pallas_call(
        body,
        name=name,
        grid_spec=pltpu.PrefetchScalarGridSpec(num_scalar_prefetch=1, grid=(1,), in_specs=specs, out_specs=specs),
        out_shape=[jax.ShapeDtypeStruct(g.shape, g.dtype) for g in grads],
        compiler_params=_cp("arbitrary"),
    )(me, *grads)


def _mesh_pos():
    x, y, c = (lax.axis_index(a) for a in MESH_AXES)
    return x, y, c, 4 * x + 2 * y + c


def _peer(x, y, c, r):
    px = 1 - x if r & 4 else x
    py = 1 - y if r & 2 else y
    pc = 1 - c if r & 1 else c
    return (px, py, pc), 4 * px + 2 * py + pc


RELATIONS = {"scatter": (1, 2, 3, 4, 5, 6, 7), "gather_all": (1, 2, 3, 4, 5, 6, 7), "gather_chips": (1, 2, 4, 6),
             "gather_sibling": (2, 4, 6)}


def _peer_copies(srcs, lands, send_sems, recv_sems, mode, waits):
    x, y, c, me = _mesh_pos()
    rel = RELATIONS[mode]
    pairs = []
    for ri, r in enumerate(rel):
        if mode == "gather_sibling":
            peer, _ = _peer(x, y, c, 1)
            _, sent_blk = _peer(x, y, c, r)
            _, got_blk = _peer(x, y, c, r ^ 1)
        else:
            peer, peer_blk = _peer(x, y, c, r)
            sent_blk, got_blk = (peer_blk if mode == "scatter" else me), peer_blk
        for k, (src, land) in enumerate(zip(srcs, lands)):
            idx = k * len(rel) + ri
            sems = dict(send_sem=send_sems.at[idx], recv_sem=recv_sems.at[idx], device_id=peer,
                        device_id_type=pl.DeviceIdType.MESH)
            dst_blk = sent_blk if mode == "gather_sibling" else me
            mine = pltpu.make_async_remote_copy(src_ref=src.at[sent_blk], dst_ref=land.at[dst_blk], **sems)
            theirs = pltpu.make_async_remote_copy(src_ref=src.at[sent_blk], dst_ref=land.at[got_blk], **sems) if waits else None
            pairs.append((mine, theirs))
    return pairs


DATAFLOW = pltpu.SideEffectType.DATAFLOW_SIDE_EFFECTING


def _in_hbm(a):
    return pltpu.with_memory_space_constraint(a, pltpu.HBM)


def _copies_start(srcs, lands, *, mode, name, deps=()):
    gather = mode != "scatter"
    arrs = list(lands) if gather else list(srcs) + list(lands)
    n, k, nd = len(arrs), len(lands), len(deps)

    def body(*refs):
        ins, send_sems, recv_sems, token = refs[:n], refs[n + nd], refs[n + nd + 1], refs[2 * n + nd + 2]
        src_refs, land_refs = (ins, ins) if gather else (ins[:k], ins[k:])
        for mine, _ in _peer_copies(src_refs, land_refs, send_sems, recv_sems, mode, waits=False):
            mine.start()
        token[...] = jnp.zeros_like(token)

    n_cp = k * len(RELATIONS[mode])
    return pl.pallas_call(
        body,
        name=name,
        in_specs=[HBM_SPEC] * n + [ANY_SPEC] * nd,
        out_specs=(SEM_SPEC, SEM_SPEC, *[HBM_SPEC] * n, pl.BlockSpec(memory_space=pltpu.VMEM)),
        out_shape=(pltpu.SemaphoreType.DMA((n_cp,)), pltpu.SemaphoreType.DMA((n_cp,)),
                   *[pltpu.HBM(a.shape, a.dtype) for a in arrs], jax.ShapeDtypeStruct((8, 128), F32)),
        input_output_aliases={i: 2 + i for i in range(n)},
        compiler_params=pltpu.CompilerParams(has_side_effects=DATAFLOW),
    )(*[_in_hbm(a) for a in arrs], *deps)


def _copies_wait(arrs, send_sems, recv_sems, after, *, n_lands, mode, name):
    n, k = len(arrs), n_lands
    gather = mode != "scatter"

    def body(*refs):
        ins, send_sems, recv_sems = refs[:n], refs[n], refs[n + 1]
        src_refs, land_refs = (ins, ins) if gather else (ins[:k], ins[k:])
        for mine, theirs in _peer_copies(src_refs, land_refs, send_sems, recv_sems, mode, waits=True):
            mine.wait_send()
            theirs.wait_recv()

    outs = pl.pallas_call(
        body,
        name=name,
        in_specs=[HBM_SPEC] * n + [SEM_SPEC, SEM_SPEC] + [ANY_SPEC] * len(after),
        out_specs=[HBM_SPEC] * n,
        out_shape=[pltpu.HBM(a.shape, a.dtype) for a in arrs],
        input_output_aliases={i: i for i in range(n)},
        compiler_params=pltpu.CompilerParams(has_side_effects=DATAFLOW),
    )(*arrs, send_sems, recv_sems, *after)
    return outs[n - k:]


def _adamw(w, g, m, v):
    m = ADAM_B1 * m + (1.0 - ADAM_B1) * g
    v = ADAM_B2 * v + (1.0 - ADAM_B2) * (g * g)
    m_hat = m / (1.0 - ADAM_B1 ** ADAM_STEP)
    v_hat = v / (1.0 - ADAM_B2 ** ADAM_STEP)
    return -ADAM_LR * (m_hat / (jnp.sqrt(v_hat) + ADAM_EPS) + ADAM_WD * w), m, v


ADAM_TC = 512


def _adam_big(slots, w, m, v, cuts_columns, *, name):
    layers, n, nj = len(slots), slots[0].shape[1], D_MODEL // ADAM_TC

    def body(*refs):
        s_refs = refs[:layers]
        w_ref, m_ref, v_ref, g_ref, d_ref, nm_ref, nv_ref, acc_ref = refs[layers:]
        for ll in range(layers):
            @pl.when(pl.program_id(0) == ll)
            def _(s_ref=s_refs[ll]):
                g = s_ref[0].astype(F32)
                for s in range(1, N_DEV):
                    g = g + s_ref[s].astype(F32)
                acc_ref[...] = g

        g = acc_ref[...].T if cuts_columns else acc_ref[...]
        g_ref[...] = g
        d_ref[...], nm_ref[...], nv_ref[...] = _adamw(w_ref[...], g, m_ref[...], v_ref[...])

    def slot_spec(ll):
        return pl.BlockSpec((N_DEV, n, ADAM_TC),
                            lambda l, j: (0, 0, jnp.where(l < ll, 0, jnp.where(l > ll, nj - 1, j))))

    if cuts_columns:
        w_spec = pl.BlockSpec((None, ADAM_TC, n), lambda l, j: (l, j, 0))
    else:
        w_spec = pl.BlockSpec((None, n, ADAM_TC), lambda l, j: (l, 0, j))
    return pl.pallas_call(
        body,
        name=name,
        grid=(layers, nj),
        in_specs=[slot_spec(ll) for ll in range(layers)] + [w_spec] * 3,
        out_specs=[w_spec] * 4,
        out_shape=[jax.ShapeDtypeStruct(w.shape, F32)] * 4,
        scratch_shapes=[pltpu.VMEM((n, ADAM_TC), F32)],
        compiler_params=_cp("arbitrary", "arbitrary"),
    )(*slots, w, m, v)


def _adam_slabs(slots, ws, ms, vs):
    n = len(slots)

    def body(*refs):
        ins, outs = refs[:4 * n], refs[4 * n:]
        for k in range(n):
            s_ref, w_ref, m_ref, v_ref = ins[k], ins[n + k], ins[2 * n + k], ins[3 * n + k]
            g = s_ref[0]
            for s in range(1, N_DEV):
                g = g + s_ref[s]
            outs[4 * k][...] = g
            outs[4 * k + 1][...], outs[4 * k + 2][...], outs[4 * k + 3][...] = _adamw(w_ref[...], g, m_ref[...], v_ref[...])

    res = pl.pallas_call(
        body,
        name="small_adamw",
        out_shape=[jax.ShapeDtypeStruct(w.shape, F32) for w in ws for _ in range(4)],
        compiler_params=pltpu.CompilerParams(vmem_limit_bytes=VMEM_LIMIT_BYTES),
    )(*slots, *ws, *ms, *vs)
    return [res[4 * k:4 * k + 4] for k in range(n)]


def _adam_vecs(gs, ws, ms, vs):
    n = len(gs)

    def body(*refs):
        ins, outs = refs[:4 * n], refs[4 * n:]
        for k in range(n):
            outs[3 * k][...], outs[3 * k + 1][...], outs[3 * k + 2][...] = _adamw(
                ins[n + k][...], ins[k][...], ins[2 * n + k][...], ins[3 * n + k][...])

    res = pl.pallas_call(
        body,
        name="ln_adamw",
        out_shape=[jax.ShapeDtypeStruct(w.shape, F32) for w in ws for _ in range(3)],
        compiler_params=pltpu.CompilerParams(vmem_limit_bytes=VMEM_LIMIT_BYTES),
    )(*gs, *ws, *ms, *vs)
    return [res[3 * k:3 * k + 3] for k in range(n)]


SLAB_AT = dict(mem_norm=0, lb_logits=1, ffn1_norm=4, mix_norm=6, hgrn_gnorm=8, gmlp_ln_g=9, gmlp_ln_b=11,
               gmlp_b_s=13, ffn2_norm=14, final_norm=16)
SLAB_ROWS = 24
LOSS_ROW = 17
SMALL_SHARDED = ("gmlp_ln_g", "gmlp_ln_b")


def _pack_slab(parts, *, name, deps=()):
    flat, plan = [], []
    for pname, at in SLAB_AT.items():
        for a in parts.get(pname, ()):
            flat.append(a)
            plan.append((at, a.shape))
            at += max(1, a.shape[0] * a.shape[1] // D_MODEL)
    for a in parts.get("loss", ()):
        flat.append(a)
        plan.append((LOSS_ROW, a.shape))

    def body(*refs):
        o_ref = refs[-1]
        o_ref[...] = jnp.zeros_like(o_ref)
        for ref, (at, (r, w)) in zip(refs, plan):
            if w == D_MODEL or r == 1 and w < D_MODEL:
                o_ref[at:at + r, 0:w] = ref[...]
            elif w < D_MODEL:
                for j in range(r):
                    o_ref[at:at + 1, j * w:(j + 1) * w] = ref[j:j + 1, :]
            else:
                for j in range(w // D_MODEL):
                    o_ref[at + j:at + j + 1, :] = ref[:, j * D_MODEL:(j + 1) * D_MODEL]

    return pl.pallas_call(
        body,
        name=name,
        in_specs=[pl.BlockSpec(memory_space=pltpu.VMEM)] * len(flat) + [ANY_SPEC] * len(deps),
        out_shape=jax.ShapeDtypeStruct((SLAB_ROWS, D_MODEL), F32),
        compiler_params=pltpu.CompilerParams(vmem_limit_bytes=VMEM_LIMIT_BYTES),
    )(*flat, *deps)


def _unpack_slab(slab, shapes):
    out = {}
    for pname, at in SLAB_AT.items():
        if pname in SMALL_SHARDED:
            continue
        size = math.prod(shapes[pname])
        rows = max(1, size // D_MODEL)
        out[pname] = slab[at:at + rows].reshape(-1)[:size].reshape(shapes[pname])
    return out


def _take_weights(full, new):
    deps = full.pop("deps", ()) + new.pop("deps", ())
    full.update(new, deps=deps)


def _ffn_fwd(x, norm_g, block, layer, full, get_weights):
    _take_weights(full, get_weights((layer, f"{block}_in"), (x,)))
    y, h, z, act = _ffn_forward(x, norm_g, full[(f"{block}_w_in", layer)], full[(f"{block}_w_out", layer)], scale=0.5,
                                deps=full.pop("deps", ()), name=f"l{layer}_{block}")
    _take_weights(full, get_weights((layer, f"{block}_out"), (y,)))
    return y, (x, h, z, act)


def _ffn_bwd(dy, dy16, saved, norm_g, w_in_t, w_out, tag, deps=(), after_out_wgrad=None, before_in_wgrad=None):
    x, h, z, act = saved
    dw_out = _mm(act, dy16, ta=True, tm=1408, tn=D_MODEL, tk=N_TOK, out_dtype=BF16, scale=0.5, deps=deps,
                 name=f"{tag}_out_wgrad")
    sent = after_out_wgrad(dw_out) if after_out_wgrad is not None else ()
    dz, dx, dx16, dg = _ffn_dgrad(dy16, dy, w_out, z, w_in_t, x, norm_g, scale=0.5, deps=sent, name=f"{tag}_dgrad")
    wdeps = before_in_wgrad(dg) if before_in_wgrad is not None else ()
    dw_in_t = _planes_wgrad(dz, h, deps=wdeps, name=f"{tag}_in_wgrad")
    return dx, dx16, dg, dw_in_t, dw_out


def kernel(x, mem, mem_norm, lb_logits, ffn1_norm, ffn1_w_in, ffn1_w_out, mix_norm, mem_w_kv, hgrn_w_in, hgrn_gnorm, hgrn_w_out, gmlp_w_in, gmlp_ln_g, gmlp_ln_b, gmlp_w_s, gmlp_b_s, gmlp_w_out, ffn2_norm, ffn2_w_in, ffn2_w_out, final_norm, loss_target, m_mem_norm, m_lb_logits, m_ffn1_norm, m_ffn1_w_in, m_ffn1_w_out, m_mix_norm, m_mem_w_kv, m_hgrn_w_in, m_hgrn_gnorm, m_hgrn_w_out, m_gmlp_w_in, m_gmlp_ln_g, m_gmlp_ln_b, m_gmlp_w_s, m_gmlp_b_s, m_gmlp_w_out, m_ffn2_norm, m_ffn2_w_in, m_ffn2_w_out, m_final_norm, v_mem_norm, v_lb_logits, v_ffn1_norm, v_ffn1_w_in, v_ffn1_w_out, v_mix_norm, v_mem_w_kv, v_hgrn_w_in, v_hgrn_gnorm, v_hgrn_w_out, v_gmlp_w_in, v_gmlp_ln_g, v_gmlp_ln_b, v_gmlp_w_s, v_gmlp_b_s, v_gmlp_w_out, v_ffn2_norm, v_ffn2_w_in, v_ffn2_w_out, v_final_norm):
    weights = dict(mem_norm=mem_norm, lb_logits=lb_logits, ffn1_norm=ffn1_norm, ffn1_w_in=ffn1_w_in, ffn1_w_out=ffn1_w_out, mix_norm=mix_norm, mem_w_kv=mem_w_kv, hgrn_w_in=hgrn_w_in, hgrn_gnorm=hgrn_gnorm, hgrn_w_out=hgrn_w_out, gmlp_w_in=gmlp_w_in, gmlp_ln_g=gmlp_ln_g, gmlp_ln_b=gmlp_ln_b, gmlp_w_s=gmlp_w_s, gmlp_b_s=gmlp_b_s, gmlp_w_out=gmlp_w_out, ffn2_norm=ffn2_norm, ffn2_w_in=ffn2_w_in, ffn2_w_out=ffn2_w_out, final_norm=final_norm)
    mom_m = dict(mem_norm=m_mem_norm, lb_logits=m_lb_logits, ffn1_norm=m_ffn1_norm, ffn1_w_in=m_ffn1_w_in, ffn1_w_out=m_ffn1_w_out, mix_norm=m_mix_norm, mem_w_kv=m_mem_w_kv, hgrn_w_in=m_hgrn_w_in, hgrn_gnorm=m_hgrn_gnorm, hgrn_w_out=m_hgrn_w_out, gmlp_w_in=m_gmlp_w_in, gmlp_ln_g=m_gmlp_ln_g, gmlp_ln_b=m_gmlp_ln_b, gmlp_w_s=m_gmlp_w_s, gmlp_b_s=m_gmlp_b_s, gmlp_w_out=m_gmlp_w_out, ffn2_norm=m_ffn2_norm, ffn2_w_in=m_ffn2_w_in, ffn2_w_out=m_ffn2_w_out, final_norm=m_final_norm)
    mom_v = dict(mem_norm=v_mem_norm, lb_logits=v_lb_logits, ffn1_norm=v_ffn1_norm, ffn1_w_in=v_ffn1_w_in, ffn1_w_out=v_ffn1_w_out, mix_norm=v_mix_norm, mem_w_kv=v_mem_w_kv, hgrn_w_in=v_hgrn_w_in, hgrn_gnorm=v_hgrn_gnorm, hgrn_w_out=v_hgrn_w_out, gmlp_w_in=v_gmlp_w_in, gmlp_ln_g=v_gmlp_ln_g, gmlp_ln_b=v_gmlp_ln_b, gmlp_w_s=v_gmlp_w_s, gmlp_b_s=v_gmlp_b_s, gmlp_w_out=v_gmlp_w_out, ffn2_norm=v_ffn2_norm, ffn2_w_in=v_ffn2_w_in, ffn2_w_out=v_ffn2_w_out, final_norm=v_final_norm)
    order = list(weights)
    _, _, _, me = _mesh_pos()
    me_arr = jnp.reshape(me, (1,)).astype(jnp.int32)
    cuts = {name: c for name, c, _, _ in GROUPS}
    rows_already = tuple(name for name, c, _, n in GROUPS if c and n % 128)
    as_rows = lambda a: jnp.transpose(a, (0, 2, 1))
    for name in rows_already:
        weights[name], mom_m[name], mom_v[name] = as_rows(weights[name]), as_rows(mom_m[name]), as_rows(mom_v[name])
        cuts[name] = False

    mix1 =(("mem_w_kv", 1), ("gmlp_w_in", 0), ("gmlp_w_out", 0))
    gather_plan = (
        ((0, "ffn1_in"), _stage_pieces(0, "ffn1")),
        ((0, "mix_in"), _stage_pieces(0, "mix")),
        ((0, "ffn2_in"), _stage_pieces(0, "ffn2")),
        ((1, "ffn1_in"), _stage_pieces(1, "ffn1")),
        ((1, "mix_in"), mix1),
        ((1, "ffn2_in"), _stage_pieces(1, "ffn2")),
    )
    stage_of = {use: k for k, (use, _) in enumerate(gather_plan)}
    in_flight = {}

    def place(k, deps=()):
        pieces = gather_plan[k][1]
        lands = [_place_rows(weights[name], l, cuts[name], me_arr, deps=deps, name=f"place_{name}_{l}")
                 for name, l in pieces]
        if pieces is mix1:
            lands.append(_place_ln(gmlp_ln_g, gmlp_ln_b, me_arr))
        return lands

    placed = {0: place(0)}

    def start_chips(k, deps):
        lands = placed[k]
        send_sems, recv_sems, *thru, token = _copies_start(lands, lands, mode="gather_chips", deps=deps,
                                                           name=f"gather{k}_chips_start")
        in_flight[k] = (thru, send_sems, recv_sems)
        return token

    def pass_to_sibling(k, after):
        thru, send_sems, recv_sems = in_flight[k]
        outs = _copies_wait(thru, send_sems, recv_sems, after, n_lands=len(thru), mode="gather_chips",
                            name=f"gather{k}_chips_wait")
        send_sems, recv_sems, *thru, token = _copies_start(outs, outs, mode="gather_sibling",
                                                           name=f"gather{k}_sibling_start")
        in_flight[k] = (thru, send_sems, recv_sems)
        return token, token

    first_sent = start_chips(0, ())
    placed.update({k: place(k, (first_sent,)) for k in range(1, len(gather_plan))})
    placed_later = tuple(a for k in range(1, len(gather_plan)) for a in placed[k])
    points = [(i, p) for i in (0, 1) for p in ("ffn1_in", "ffn1_out", "mix_in", "mix_out", "ffn2_in", "ffn2_out")]
    pass_at = {j: points[points.index(use) - 1] for j, (use, _) in enumerate(gather_plan) if j}
    pass_at[1] = gather_plan[1][0]

    started = {0}
    early_start = (4, (0, "ffn2_in"))

    def get_weights(use, after):
        tokens, w = [], {}
        k = stage_of.get(use)

        def pass_on(j, after):
            token, landed = pass_to_sibling(j, after)
            tokens.append(token)
            if j + 1 < len(gather_plan) and j + 1 not in started:
                started.add(j + 1)
                tokens.append(start_chips(j + 1, (landed,)))

        if k == 0:
            pass_on(0, tuple(after) + placed_later)
        elif k is not None and pass_at[k] == use:
            pass_on(k, after)
        if k is not None:
            thru, send_sems, recv_sems = in_flight[k]
            outs = _copies_wait(thru, send_sems, recv_sems, after, n_lands=len(thru), mode="gather_sibling",
                                name=f"gather{k}_sibling_wait")
            after = (outs[0],)
            pieces = gather_plan[k][1]
            w = {p: o.reshape(N_DEV * o.shape[1], D_MODEL) for p, o in zip(pieces, outs)}
            if pieces is mix1:
                w["ln_g"] = outs[-1][:, 0, :].reshape(1, GM_WIDTH)
                w["ln_b"] = outs[-1][:, 1, :].reshape(1, GM_WIDTH)
        for j, at in pass_at.items():
            if at == use and j != k:
                pass_on(j, after)
        if use == early_start[1] and early_start[0] not in started:
            started.add(early_start[0])
            tokens.append(start_chips(early_start[0], after))
        w["deps"] = tuple(tokens)
        return w

    scatter = {}

    def put_grads(st, grads):
        if st in ("w_s", "small"):
            slab = grads.reshape(GM_GROUPS * GM_CHUNK, GM_CHUNK) if st == "w_s" else _pack_slab(grads, name="pack_small_grads")
            land = _place_slab(slab, me_arr, name=f"{st}_place")
            send_sems, recv_sems, *thru, token = _copies_start([land], [land], mode="gather_all", name=f"{st}_start")
            scatter[st] = (thru, send_sems, recv_sems)
            return (token,)
        views = [g.reshape(N_DEV, -1, D_MODEL) for g in grads.values()]
        recv = _place_own(views, me_arr, name=f"scatter_place_l{st[0]}_{st[1]}")
        send_sems, recv_sems, *thru, token = _copies_start(views, recv, mode="scatter",
                                                           name=f"scatter_start_l{st[0]}_{st[1]}")
        scatter[st] = (tuple(grads), thru, send_sems, recv_sems)
        return (token,)

    dx, last_sent = _step_local(
        x, mem, loss_target, get_weights, put_grads, mem_norm, lb_logits, ffn1_norm, mix_norm, hgrn_gnorm,
        gmlp_w_s, gmlp_b_s, ffn2_norm, final_norm)

    slots = {}

    def wait_grads(blk, after, last=False):
        for st, entry in scatter.items():
            if isinstance(st, tuple) and st[1].startswith(blk) and (st == (0, "ffn1_in")) == last:
                pieces, thru, send_sems, recv_sems = entry
                outs = _copies_wait(thru, send_sems, recv_sems, after, n_lands=len(thru) // 2, mode="scatter",
                                    name=f"scatter_wait_l{st[0]}_{st[1]}")
                slots.update(zip(pieces, outs))

    grad, delta, new_m, new_v = {}, {}, {}, {}

    def adam_groups(names):
        for name in names:
            layers = GROUP_LAYERS[name]
            grad[name], delta[name], new_m[name], new_v[name] = _adam_big(
                [slots[(name, l)] for l in range(layers)], weights[name], mom_m[name], mom_v[name], cuts[name],
                name=f"{name}_adamw")

    wait_grads("ffn2", (dx, *last_sent))
    adam_groups(("ffn2_w_in", "ffn2_w_out"))
    wait_grads("mix", (delta["ffn2_w_out"],))
    adam_groups(("mem_w_kv", "gmlp_w_in", "gmlp_w_out", "hgrn_w_in", "hgrn_w_out"))
    wait_grads("ffn1", (delta["hgrn_w_out"],))
    adam_groups(("ffn1_w_out",))

    def small_parts(src):
        parts = {n: [src[n].reshape(-1, src[n].shape[-1])] for n in SLAB_AT if n not in SMALL_SHARDED}
        return parts

    w_s_rows = lambda a: a.reshape(GM_GROUPS * GM_CHUNK, GM_CHUNK)
    small_done = (delta["hgrn_w_out"],)
    (slab_slots,) = _copies_wait(*scatter["small"], small_done, n_lands=1, mode="gather_all", name="small_wait")
    (ws_slots,) = _copies_wait(*scatter["w_s"], small_done, n_lands=1, mode="gather_all", name="w_s_wait")
    (g_slab, d_slab, nm_slab, nv_slab), (g_ws, d_ws, nm_ws, nv_ws) = _adam_slabs(
        [slab_slots, ws_slots],
        [_pack_slab(small_parts(weights), deps=(dx,), name="pack_small_w"), w_s_rows(gmlp_w_s)],
        [_pack_slab(small_parts(mom_m), deps=(dx,), name="pack_small_m"), w_s_rows(m_gmlp_w_s)],
        [_pack_slab(small_parts(mom_v), deps=(dx,), name="pack_small_v"), w_s_rows(v_gmlp_w_s)])
    shapes = {n: weights[n].shape for n in SLAB_AT}
    for out, slab, ws in ((grad, g_slab, g_ws), (delta, d_slab, d_ws), (new_m, nm_slab, nm_ws), (new_v, nv_slab, nv_ws)):
        out.update(_unpack_slab(slab, shapes))
        out["gmlp_w_s"] = ws.reshape(gmlp_w_s.shape)
    blk = GM_WIDTH // N_DEV
    g_ln = [lax.dynamic_slice(g_slab[SLAB_AT[n]:SLAB_AT[n] + 2].reshape(1, GM_WIDTH), (0, me * blk), (1, blk))
            for n in SMALL_SHARDED]
    ln_out = _adam_vecs(g_ln, [weights[n] for n in SMALL_SHARDED], [mom_m[n] for n in SMALL_SHARDED],
                        [mom_v[n] for n in SMALL_SHARDED])
    for n, g, (d, nm, nv) in zip(SMALL_SHARDED, g_ln, ln_out):
        grad[n], delta[n], new_m[n], new_v[n] = g, d, nm, nv

    wait_grads("ffn1", tuple(delta[n] for n in delta if n in GROUP_LAYERS) + (d_slab,), last=True)
    adam_groups(("ffn1_w_in",))

    for name in rows_already:
        for out in (grad, delta, new_m, new_v):
            out[name] = as_rows(out[name])
    loss = g_slab[LOSS_ROW, 0]
    grad_x = dx.reshape(B_LOC, SEQ, D_MODEL)
    return (loss, grad_x, *[grad[n] for n in order], *[delta[n] for n in order],
            *[new_m[n] for n in order], *[new_v[n] for n in order])


def _step_local(x, mem, loss_target, get_weights, put_grads, mem_norm, lb_logits, ffn1_norm, mix_norm, hgrn_gnorm,
                gmlp_w_s, gmlp_b_s, ffn2_norm, final_norm):
    w_s = gmlp_w_s[0]
    b_st = gmlp_b_s[0].T

    xs = x.reshape(N_TOK, D_MODEL)
    mem2d = mem.reshape(B_LOC * MEM_LEN, D_MODEL)
    mem_g = mem_norm.reshape(1, D_MODEL)
    saved, full = [], {}
    for i in range(2):
        xs, s_ffn1 = _ffn_fwd(xs, ffn1_norm[i:i + 1], "ffn1", i, full, get_weights)
        if i == 0:
            memn = _rms_fwd(mem2d, mem_g, deps=(xs,), name="mem_norm_fwd")
        _take_weights(full, get_weights((i, "mix_in"), (xs,)))
        mixer = "hgrn" if i == 0 else "gmlp"
        hm, zm = _norm_mm(xs, mix_norm[i:i + 1], full[(f"{mixer}_w_in", 0)], tm=512, tn=2560, deps=full.pop("deps", ()),
                          name=f"l{i}_mix_in")
        kv = _mm(memn, full[("mem_w_kv", i)], tb=True, tm=512, tn=512, tk=D_MODEL, out_dtype=F32, name=f"l{i}_mem_kv")
        o_mem = _attn_fwd(zm, kv, name=f"l{i}_attn")
        if i == 0:
            cat, o_pre, s_all = _hgrn_fwd(zm, o_mem, lb_logits, hgrn_gnorm)
            mix_saved = (o_pre, s_all)
        else:
            cat = _gmlp_fwd(zm, o_mem, full["ln_g"], full["ln_b"], w_s, b_st)
            mix_saved = ()
        x_mix = xs
        _take_weights(full, get_weights((i, "mix_out"), (cat,)))
        xs = _mm(cat, full[(f"{mixer}_w_out", 0)], tm=512, tn=D_MODEL, tk=cat.shape[1], out_dtype=F32, res=xs,
                 deps=full.pop("deps", ()), name=f"l{i}_mix_out")
        xs, s_ffn2 = _ffn_fwd(xs, ffn2_norm[i:i + 1], "ffn2", i, full, get_weights)
        saved.append((s_ffn1, (x_mix, hm, kv, zm, cat, mix_saved), s_ffn2))

    dx, dx16, d_final, loss_part = _loss_head(xs, final_norm.reshape(1, D_MODEL), loss_target.reshape(N_TOK, D_MODEL))

    small = {"final_norm": [d_final], "loss": [loss_part]}
    d_ffn1, d_ffn2, d_mix = [None, None], [None, None], [None, None]
    dmemn = jnp.zeros((B_LOC * MEM_LEN, D_MODEL), F32)
    deps = ()
    for i in (1, 0):
        s_ffn1, (x_mix, hm, kv, zm, cat, mix_saved), s_ffn2 = saved[i]
        dx, dx16, d_ffn2[i], dw_in_t, dw_out = _ffn_bwd(
            dx, dx16, s_ffn2, ffn2_norm[i:i + 1], full[("ffn2_w_in", i)], full[("ffn2_w_out", i)], f"l{i}_ffn2", deps)
        deps = put_grads((i, "ffn2"), {("ffn2_w_in", i): dw_in_t, ("ffn2_w_out", i): dw_out})
        mixer = "hgrn" if i == 0 else "gmlp"
        w_in_t, w_out = full[(f"{mixer}_w_in", 0)], full[(f"{mixer}_w_out", 0)]
        width = cat.shape[1]
        g_mix = {}
        g_mix[(f"{mixer}_w_out", 0)] = _mm(cat, dx16, ta=True, tm=1024, tn=D_MODEL, tk=N_TOK, out_dtype=BF16,
                                           deps=deps, name=f"l{i}_mix_out_wgrad")
        dcat = _mm(dx16, w_out, tb=True, tm=1024, tn=width // 2, tk=D_MODEL, out_dtype=F32, name=f"l{i}_mix_out_dgrad")
        dq, dk, dv = _attn_bwd(zm, kv, dcat, do_off=width - XA_HEADS * XA_DIM, name=f"l{i}_attn_bwd")
        if i == 0:
            dzm, dlbl, dgn = _hgrn_bwd(zm, mix_saved[0], dcat, dq, mix_saved[1], lb_logits, hgrn_gnorm)
            small["lb_logits"], small["hgrn_gnorm"] = [dlbl], [dgn]
            deps = ()
        else:
            dzm, dws, dbt, dlng, dlnb = _gmlp_bwd(zm, dcat, dq, full["ln_g"], full["ln_b"], w_s, b_st)
            small["gmlp_b_s"], small["gmlp_ln_g"], small["gmlp_ln_b"] = [dbt.T], [dlng], [dlnb]
            deps = put_grads("w_s", dws)
        g_mix[(f"{mixer}_w_in", 0)] = _mm(dzm, hm, ta=True, tm=1024, tn=D_MODEL, tk=N_TOK, out_dtype=BF16, deps=deps,
                                          name=f"l{i}_mix_in_wgrad")
        dkv = jnp.concatenate([dk, dv], axis=1)
        g_mix[("mem_w_kv", i)] = _mm(dkv, memn, ta=True, tm=512, tn=D_MODEL, tk=B_LOC * MEM_LEN, out_dtype=BF16,
                                     name=f"l{i}_mem_kv_wgrad")
        deps = put_grads((i, "mix"), g_mix)
        dx, dx16, d_mix[i] = _dgrad_norm_bwd(dzm, w_in_t, x_mix, mix_norm[i:i + 1], dx, deps=deps,
                                             name=f"l{i}_mix_in_dgrad")
        dmemn = _mm(dkv, full[("mem_w_kv", i)], tm=B_LOC * MEM_LEN, tn=D_MODEL, tk=512, out_dtype=F32, res=dmemn,
                    name=f"l{i}_mem_kv_dgrad")
        def send_small(dg, i=i, dmemn=dmemn):
            d_ffn1[i] = dg
            _, _, dmem_g = _rms_bwd(mem2d, mem_g, dmemn, dmemn, name="mem_norm_bwd")
            small.update(mem_norm=[dmem_g], ffn1_norm=d_ffn1, ffn2_norm=d_ffn2, mix_norm=d_mix)
            return put_grads("small", small)

        if i == 0:
            send_out = lambda dw_out: put_grads((0, "ffn1_out"), {("ffn1_w_out", 0): dw_out})
            dx, dx16, d_ffn1[i], dw_in_t, _ = _ffn_bwd(
                dx, dx16, s_ffn1, ffn1_norm[i:i + 1], full[("ffn1_w_in", i)], full[("ffn1_w_out", i)], f"l{i}_ffn1",
                after_out_wgrad=send_out, before_in_wgrad=send_small)
            deps = put_grads((0, "ffn1_in"), {("ffn1_w_in", 0): dw_in_t})
        else:
            dx, dx16, d_ffn1[i], dw_in_t, dw_out = _ffn_bwd(
                dx, dx16, s_ffn1, ffn1_norm[i:i + 1], full[("ffn1_w_in", i)], full[("ffn1_w_out", i)], f"l{i}_ffn1")
            deps = put_grads((i, "ffn1"), {("ffn1_w_in", i): dw_in_t, ("ffn1_w_out", i): dw_out})
    return dx, deps
```
